```python
import math
import jax, jax.numpy as jnp
from jax import lax
import numpy as np

D_MODEL = 1024
BATCH = 8
SEQ = 2048
DEPTH = 1

HEAD_DIM = 64
ATTN_HEADS = 8
DILATION_GROUPS = ((128, 1), (512, 4), (2048, 16))
N_GROUPS = len(DILATION_GROUPS)
ATTN_WIDTH = ATTN_HEADS * HEAD_DIM
QKV_WIDTH = N_GROUPS * ATTN_WIDTH
BLOCK = 128
ROPE_THETA = 10000.0
NEG_INF = -1e30
SSM_GROUP = 16
SSM_GROUPS = 32
SSM_WIDTH = SSM_GROUP * SSM_GROUPS
SSM_STATE = 64
DT_MIN = 1e-3
DT_MAX = 1e-1
N_BRANCH = 2
IN_WIDTH = 3 * QKV_WIDTH + SSM_WIDTH + N_BRANCH * D_MODEL
D_FF = -(-(8 * D_MODEL) // (3 * 256)) * 256
DN_ALPHA = (2.0 * DEPTH) ** 0.25
DN_BETA = (8.0 * DEPTH) ** -0.25
LN_EPS = 1e-5

kernel_name = "dilated_attn_s5_gated_hybrid_deepnorm"


def layer_norm(x, g, b):
    xf = x.astype(jnp.float32)
    mu = jnp.mean(xf, axis=-1, keepdims=True)
    var = jnp.mean(jnp.square(xf - mu), axis=-1, keepdims=True)
    y = (xf - mu) * lax.rsqrt(var + LN_EPS)
    return (y * g.astype(jnp.float32) + b.astype(jnp.float32)).astype(x.dtype)


def apply_rope(t, pos):
    half = HEAD_DIM // 2
    inv_freq = ROPE_THETA ** (-jnp.arange(half, dtype=jnp.float32) / half)
    ang = pos[:, None] * inv_freq[None, :]
    cos = jnp.cos(ang)[None, :, None, None, :]
    sin = jnp.sin(ang)[None, :, None, None, :]
    t1 = t[..., :half].astype(jnp.float32)
    t2 = t[..., half:].astype(jnp.float32)
    return jnp.concatenate([t1 * cos - t2 * sin, t1 * sin + t2 * cos], axis=-1)


def dilated_group_attention(q, k, v, window, dilation):
    b, s, h, dh = q.shape
    n = s // dilation
    nb = -(-n // BLOCK)
    n_pad = nb * BLOCK
    back = window // dilation

    def to_phase_blocks(t):
        t = t.reshape(b, n, dilation, h, dh)
        t = jnp.pad(t, ((0, 0), (0, n_pad - n), (0, 0), (0, 0), (0, 0)))
        return t.reshape(b, nb, BLOCK, dilation, h, dh)

    def with_prev_block(t):
        prev = jnp.pad(t, ((0, 0), (1, 0), (0, 0), (0, 0), (0, 0), (0, 0)))[:, :-1]
        return jnp.concatenate([prev, t], axis=2)

    qb = to_phase_blocks(q)
    kw = with_prev_block(to_phase_blocks(k))
    vw = with_prev_block(to_phase_blocks(v))

    scores = jnp.einsum("bnqrhd,bnkrhd->bnrhqk", qb, kw,
                        preferred_element_type=jnp.float32) / math.sqrt(dh)
    a_idx = jnp.arange(BLOCK)[None, :, None]
    c_idx = jnp.arange(2 * BLOCK)[None, None, :]
    blk = jnp.arange(nb)[:, None, None]
    dist = BLOCK + a_idx - c_idx
    key_sub = (blk - 1) * BLOCK + c_idx
    valid = (dist >= 0) & (dist <= back) & (key_sub >= 0)
    scores = jnp.where(valid[None, :, None, None], scores, NEG_INF)
    lse = jax.nn.logsumexp(scores, axis=-1)
    probs = jnp.exp(scores - lse[..., None])
    out = jnp.einsum("bnrhqk,bnkrhd->bnqrhd", probs, vw.astype(jnp.float32))
    out = out.reshape(b, n_pad, dilation, h, dh)[:, :n].reshape(b, s, h, dh)
    lse = jnp.transpose(lse, (0, 1, 4, 2, 3)).reshape(b, n_pad, dilation, h)[:, :n]
    return out, lse.reshape(b, s, h)


def s5_ssm(u, a_re, a_im, log_dt, b_re, b_im, c_re, c_im, d_skip):
    f32 = jnp.float32
    bsz, s, _ = u.shape
    ug = u.astype(f32).reshape(bsz, s, SSM_GROUPS, SSM_GROUP)
    lam = lax.complex(a_re.astype(f32), a_im.astype(f32))
    dt = jnp.exp(log_dt.astype(f32))[:, None]
    a_bar = jnp.exp(lam * dt)
    b_c = lax.complex(b_re.astype(f32), b_im.astype(f32))
    b_bar = ((a_bar - 1.0) / lam)[..., None] * b_c
    bu = jnp.einsum("gph,bsgh->bsgp", b_bar, ug.astype(jnp.complex64))
    a_seq = jnp.broadcast_to(a_bar, bu.shape)

    def combine(left, right):
        a_l, x_l = left
        a_r, x_r = right
        return a_r * a_l, a_r * x_l + x_r

    _, states = lax.associative_scan(combine, (a_seq, bu), axis=1)
    c_c = lax.complex(c_re.astype(f32), c_im.astype(f32))
    y = jnp.einsum("ghp,bsgp->bsgh", c_c, states).real
    y = y + d_skip.astype(f32).reshape(SSM_GROUPS, SSM_GROUP) * ug
    return y.reshape(bsz, s, SSM_WIDTH)


def _fwd_setup_inputs(seed: int = 0) -> dict:
    key = jax.random.key(seed)
    ks = jax.random.split(key, 24)
    f32 = jnp.float32
    L = DEPTH

    def nrm(k, shape, scale):
        return jax.random.normal(k, shape, f32) * scale

    x = jax.random.normal(ks[0], (BATCH, SEQ, D_MODEL), f32)
    w_in = nrm(ks[1], (L, D_MODEL, IN_WIDTH), D_MODEL ** -0.5)
    b_gate = nrm(ks[2], (L, N_BRANCH, D_MODEL), 0.02)
    w_attn_br = nrm(ks[3], (L, ATTN_WIDTH, D_MODEL), ATTN_WIDTH ** -0.5)
    w_ssm_br = nrm(ks[4], (L, SSM_WIDTH, D_MODEL), SSM_WIDTH ** -0.5)
    w_out = nrm(ks[5], (L, D_MODEL, D_MODEL), DN_BETA * D_MODEL ** -0.5)
    ssm_a_re = -0.5 + nrm(ks[6], (L, SSM_GROUPS, SSM_STATE), 0.01)
    ssm_a_im = (math.pi * jnp.arange(SSM_STATE, dtype=f32))[None, None, :] + nrm(ks[7], (L, SSM_GROUPS, SSM_STATE), 0.01)
    ssm_log_dt = jax.random.uniform(ks[8], (L, SSM_GROUPS), f32, math.log(DT_MIN), math.log(DT_MAX))
    ssm_b_re = nrm(ks[9], (L, SSM_GROUPS, SSM_STATE, SSM_GROUP), (2 * SSM_GROUP) ** -0.5)
    ssm_b_im = nrm(ks[10], (L, SSM_GROUPS, SSM_STATE, SSM_GROUP), (2 * SSM_GROUP) ** -0.5)
    ssm_c_re = nrm(ks[11], (L, SSM_GROUPS, SSM_GROUP, SSM_STATE), SSM_STATE ** -0.5)
    ssm_c_im = nrm(ks[12], (L, SSM_GROUPS, SSM_GROUP, SSM_STATE), SSM_STATE ** -0.5)
    ssm_d = nrm(ks[13], (L, SSM_WIDTH), 1.0)
    w_glu = nrm(ks[14], (L, SSM_WIDTH, 2 * SSM_WIDTH), SSM_WIDTH ** -0.5)
    ln1_g = 1.0 + nrm(ks[15], (L, D_MODEL), 0.02)
    ln1_b = nrm(ks[16], (L, D_MODEL), 0.02)
    w_ff_gate = nrm(ks[17], (L, D_MODEL, D_FF), D_MODEL ** -0.5)
    w_ff_up = nrm(ks[18], (L, D_MODEL, D_FF), D_MODEL ** -0.5)
    w_ff_down = nrm(ks[19], (L, D_FF, D_MODEL), DN_BETA * D_FF ** -0.5)
    ln2_g = 1.0 + nrm(ks[20], (L, D_MODEL), 0.02)
    ln2_b = nrm(ks[21], (L, D_MODEL), 0.02)
    return {"x": x, "w_in": w_in, "b_gate": b_gate, "w_attn_br": w_attn_br,
            "w_ssm_br": w_ssm_br, "w_out": w_out, "ssm_a_re": ssm_a_re,
            "ssm_a_im": ssm_a_im, "ssm_log_dt": ssm_log_dt, "ssm_b_re": ssm_b_re,
            "ssm_b_im": ssm_b_im, "ssm_c_re": ssm_c_re, "ssm_c_im": ssm_c_im,
            "ssm_d": ssm_d, "w_glu": w_glu, "ln1_g": ln1_g, "ln1_b": ln1_b,
            "w_ff_gate": w_ff_gate, "w_ff_up": w_ff_up, "w_ff_down": w_ff_down,
            "ln2_g": ln2_g, "ln2_b": ln2_b}


def _fwd_reference(x, w_in, b_gate, w_attn_br, w_ssm_br, w_out, ssm_a_re, ssm_a_im,
              ssm_log_dt, ssm_b_re, ssm_b_im, ssm_c_re, ssm_c_im, ssm_d, w_glu,
              ln1_g, ln1_b, w_ff_gate, w_ff_up, w_ff_down, ln2_g, ln2_b):
    bsz, s, _ = x.shape
    pos = jnp.arange(s, dtype=jnp.float32)
    for layer in range(DEPTH):
        proj = x @ w_in[layer]
        q = proj[..., :QKV_WIDTH].reshape(bsz, s, N_GROUPS, ATTN_HEADS, HEAD_DIM)
        k = proj[..., QKV_WIDTH:2 * QKV_WIDTH].reshape(bsz, s, N_GROUPS, ATTN_HEADS, HEAD_DIM)
        v = proj[..., 2 * QKV_WIDTH:3 * QKV_WIDTH].reshape(bsz, s, N_GROUPS, ATTN_HEADS, HEAD_DIM)
        u = proj[..., 3 * QKV_WIDTH:3 * QKV_WIDTH + SSM_WIDTH]
        gate_logits = proj[..., 3 * QKV_WIDTH + SSM_WIDTH:].reshape(bsz, s, N_BRANCH, D_MODEL)
        q = apply_rope(q, pos)
        k = apply_rope(k, pos)

        outs, lses = [], []
        for g, (window, dilation) in enumerate(DILATION_GROUPS):
            o_g, lse_g = dilated_group_attention(q[:, :, g], k[:, :, g], v[:, :, g], window, dilation)
            outs.append(o_g)
            lses.append(lse_g)
        wts = jax.nn.softmax(jnp.stack(lses, axis=0), axis=0)
        attn = jnp.sum(wts[..., None] * jnp.stack(outs, axis=0), axis=0)
        y_attn = attn.reshape(bsz, s, ATTN_WIDTH).astype(x.dtype) @ w_attn_br[layer]

        y_s = s5_ssm(u, ssm_a_re[layer], ssm_a_im[layer], ssm_log_dt[layer], ssm_b_re[layer],
                     ssm_b_im[layer], ssm_c_re[layer], ssm_c_im[layer], ssm_d[layer])
        glu = jax.nn.gelu(y_s).astype(x.dtype) @ w_glu[layer]
        y_s = glu[..., :SSM_WIDTH] * jax.nn.sigmoid(glu[..., SSM_WIDTH:])
        y_ssm = y_s @ w_ssm_br[layer]

        gates = jax.nn.sigmoid((gate_logits + b_gate[layer]).astype(jnp.float32))
        mixed = gates[..., 0, :] * y_attn.astype(jnp.float32) + gates[..., 1, :] * y_ssm.astype(jnp.float32)
        mix_out = mixed.astype(x.dtype) @ w_out[layer]
        h = layer_norm(DN_ALPHA * x + mix_out.astype(x.dtype), ln1_g[layer], ln1_b[layer])

        ff = (jax.nn.silu(h @ w_ff_gate[layer]) * (h @ w_ff_up[layer])) @ w_ff_down[layer]
        x = layer_norm(DN_ALPHA * h + ff.astype(h.dtype), ln2_g[layer], ln2_b[layer])
    return x


import jax as _jax
import jax.numpy as _jnp

TWIN_FORMAT = 'train_step'
FWD_PARAMS = ['x', 'w_in', 'b_gate', 'w_attn_br', 'w_ssm_br', 'w_out', 'ssm_a_re', 'ssm_a_im', 'ssm_log_dt', 'ssm_b_re', 'ssm_b_im', 'ssm_c_re', 'ssm_c_im', 'ssm_d', 'w_glu', 'ln1_g', 'ln1_b', 'w_ff_gate', 'w_ff_up', 'w_ff_down', 'ln2_g', 'ln2_b']
TWIN_WEIGHTS = ['w_in', 'b_gate', 'w_attn_br', 'w_ssm_br', 'w_out', 'ssm_a_re', 'ssm_a_im', 'ssm_log_dt', 'ssm_b_re', 'ssm_b_im', 'ssm_c_re', 'ssm_c_im', 'ssm_d', 'w_glu', 'ln1_g', 'ln1_b', 'w_ff_gate', 'w_ff_up', 'w_ff_down', 'ln2_g', 'ln2_b']
TWIN_DIFF_INPUT = 'x'
TWIN_INPUTS = ['x', 'w_in', 'b_gate', 'w_attn_br', 'w_ssm_br', 'w_out', 'ssm_a_re', 'ssm_a_im', 'ssm_log_dt', 'ssm_b_re', 'ssm_b_im', 'ssm_c_re', 'ssm_c_im', 'ssm_d', 'w_glu', 'ln1_g', 'ln1_b', 'w_ff_gate', 'w_ff_up', 'w_ff_down', 'ln2_g', 'ln2_b', 'loss_target', 'm_w_in', 'm_b_gate', 'm_w_attn_br', 'm_w_ssm_br', 'm_w_out', 'm_ssm_a_re', 'm_ssm_a_im', 'm_ssm_log_dt', 'm_ssm_b_re', 'm_ssm_b_im', 'm_ssm_c_re', 'm_ssm_c_im', 'm_ssm_d', 'm_w_glu', 'm_ln1_g', 'm_ln1_b', 'm_w_ff_gate', 'm_w_ff_up', 'm_w_ff_down', 'm_ln2_g', 'm_ln2_b', 'v_w_in', 'v_b_gate', 'v_w_attn_br', 'v_w_ssm_br', 'v_w_out', 'v_ssm_a_re', 'v_ssm_a_im', 'v_ssm_log_dt', 'v_ssm_b_re', 'v_ssm_b_im', 'v_ssm_c_re', 'v_ssm_c_im', 'v_ssm_d', 'v_w_glu', 'v_ln1_g', 'v_ln1_b', 'v_w_ff_gate', 'v_w_ff_up', 'v_w_ff_down', 'v_ln2_g', 'v_ln2_b']
TWIN_OUTPUTS = ['loss', 'grad_x', 'grad_w_in', 'grad_b_gate', 'grad_w_attn_br', 'grad_w_ssm_br', 'grad_w_out', 'grad_ssm_a_re', 'grad_ssm_a_im', 'grad_ssm_log_dt', 'grad_ssm_b_re', 'grad_ssm_b_im', 'grad_ssm_c_re', 'grad_ssm_c_im', 'grad_ssm_d', 'grad_w_glu', 'grad_ln1_g', 'grad_ln1_b', 'grad_w_ff_gate', 'grad_w_ff_up', 'grad_w_ff_down', 'grad_ln2_g', 'grad_ln2_b', 'delta_w_in', 'delta_b_gate', 'delta_w_attn_br', 'delta_w_ssm_br', 'delta_w_out', 'delta_ssm_a_re', 'delta_ssm_a_im', 'delta_ssm_log_dt', 'delta_ssm_b_re', 'delta_ssm_b_im', 'delta_ssm_c_re', 'delta_ssm_c_im', 'delta_ssm_d', 'delta_w_glu', 'delta_ln1_g', 'delta_ln1_b', 'delta_w_ff_gate', 'delta_w_ff_up', 'delta_w_ff_down', 'delta_ln2_g', 'delta_ln2_b', 'new_m_w_in', 'new_m_b_gate', 'new_m_w_attn_br', 'new_m_w_ssm_br', 'new_m_w_out', 'new_m_ssm_a_re', 'new_m_ssm_a_im', 'new_m_ssm_log_dt', 'new_m_ssm_b_re', 'new_m_ssm_b_im', 'new_m_ssm_c_re', 'new_m_ssm_c_im', 'new_m_ssm_d', 'new_m_w_glu', 'new_m_ln1_g', 'new_m_ln1_b', 'new_m_w_ff_gate', 'new_m_w_ff_up', 'new_m_w_ff_down', 'new_m_ln2_g', 'new_m_ln2_b', 'new_v_w_in', 'new_v_b_gate', 'new_v_w_attn_br', 'new_v_w_ssm_br', 'new_v_w_out', 'new_v_ssm_a_re', 'new_v_ssm_a_im', 'new_v_ssm_log_dt', 'new_v_ssm_b_re', 'new_v_ssm_b_im', 'new_v_ssm_c_re', 'new_v_ssm_c_im', 'new_v_ssm_d', 'new_v_w_glu', 'new_v_ln1_g', 'new_v_ln1_b', 'new_v_w_ff_gate', 'new_v_w_ff_up', 'new_v_w_ff_down', 'new_v_ln2_g', 'new_v_ln2_b']
TWIN_LEAF_KINDS = {'loss': 'loss', 'grad_x': 'grad_x', 'grad_w_in': 'grad_w', 'grad_b_gate': 'grad_w', 'grad_w_attn_br': 'grad_w', 'grad_w_ssm_br': 'grad_w', 'grad_w_out': 'grad_w', 'grad_ssm_a_re': 'grad_w', 'grad_ssm_a_im': 'grad_w', 'grad_ssm_log_dt': 'grad_w', 'grad_ssm_b_re': 'grad_w', 'grad_ssm_b_im': 'grad_w', 'grad_ssm_c_re': 'grad_w', 'grad_ssm_c_im': 'grad_w', 'grad_ssm_d': 'grad_w', 'grad_w_glu': 'grad_w', 'grad_ln1_g': 'grad_w', 'grad_ln1_b': 'grad_w', 'grad_w_ff_gate': 'grad_w', 'grad_w_ff_up': 'grad_w', 'grad_w_ff_down': 'grad_w', 'grad_ln2_g': 'grad_w', 'grad_ln2_b': 'grad_w', 'delta_w_in': 'delta_w', 'delta_b_gate': 'delta_w', 'delta_w_attn_br': 'delta_w', 'delta_w_ssm_br': 'delta_w', 'delta_w_out': 'delta_w', 'delta_ssm_a_re': 'delta_w', 'delta_ssm_a_im': 'delta_w', 'delta_ssm_log_dt': 'delta_w', 'delta_ssm_b_re': 'delta_w', 'delta_ssm_b_im': 'delta_w', 'delta_ssm_c_re': 'delta_w', 'delta_ssm_c_im': 'delta_w', 'delta_ssm_d': 'delta_w', 'delta_w_glu': 'delta_w', 'delta_ln1_g': 'delta_w', 'delta_ln1_b': 'delta_w', 'delta_w_ff_gate': 'delta_w', 'delta_w_ff_up': 'delta_w', 'delta_w_ff_down': 'delta_w', 'delta_ln2_g': 'delta_w', 'delta_ln2_b': 'delta_w', 'new_m_w_in': 'new_m', 'new_m_b_gate': 'new_m', 'new_m_w_attn_br': 'new_m', 'new_m_w_ssm_br': 'new_m', 'new_m_w_out': 'new_m', 'new_m_ssm_a_re': 'new_m', 'new_m_ssm_a_im': 'new_m', 'new_m_ssm_log_dt': 'new_m', 'new_m_ssm_b_re': 'new_m', 'new_m_ssm_b_im': 'new_m', 'new_m_ssm_c_re': 'new_m', 'new_m_ssm_c_im': 'new_m', 'new_m_ssm_d': 'new_m', 'new_m_w_glu': 'new_m', 'new_m_ln1_g': 'new_m', 'new_m_ln1_b': 'new_m', 'new_m_w_ff_gate': 'new_m', 'new_m_w_ff_up': 'new_m', 'new_m_w_ff_down': 'new_m', 'new_m_ln2_g': 'new_m', 'new_m_ln2_b': 'new_m', 'new_v_w_in': 'new_v', 'new_v_b_gate': 'new_v', 'new_v_w_attn_br': 'new_v', 'new_v_w_ssm_br': 'new_v', 'new_v_w_out': 'new_v', 'new_v_ssm_a_re': 'new_v', 'new_v_ssm_a_im': 'new_v', 'new_v_ssm_log_dt': 'new_v', 'new_v_ssm_b_re': 'new_v', 'new_v_ssm_b_im': 'new_v', 'new_v_ssm_c_re': 'new_v', 'new_v_ssm_c_im': 'new_v', 'new_v_ssm_d': 'new_v', 'new_v_w_glu': 'new_v', 'new_v_ln1_g': 'new_v', 'new_v_ln1_b': 'new_v', 'new_v_w_ff_gate': 'new_v', 'new_v_w_ff_up': 'new_v', 'new_v_w_ff_down': 'new_v', 'new_v_ln2_g': 'new_v', 'new_v_ln2_b': 'new_v'}


def _forward(args):
    return _fwd_reference(*[args[k] for k in FWD_PARAMS])


def _output_shape():
    out = _jax.eval_shape(lambda: _forward(_fwd_setup_inputs(0)))
    return out.shape, out.dtype

N_MICROBATCH = 1
ADAM_LR = 0.001
ADAM_B1 = 0.9
ADAM_B2 = 0.999
ADAM_EPS = 1e-08
ADAM_WD = 0.01
ADAM_STEP = 10
PER_EXAMPLE_BATCH_AXIS = {'x': 0, 'loss_target': 0}
SHARED_INPUTS = []
_WEIGHT_DTYPES = {'w_in': _jnp.float32, 'b_gate': _jnp.float32, 'w_attn_br': _jnp.float32, 'w_ssm_br': _jnp.float32, 'w_out': _jnp.float32, 'ssm_a_re': _jnp.float32, 'ssm_a_im': _jnp.float32, 'ssm_log_dt': _jnp.float32, 'ssm_b_re': _jnp.float32, 'ssm_b_im': _jnp.float32, 'ssm_c_re': _jnp.float32, 'ssm_c_im': _jnp.float32, 'ssm_d': _jnp.float32, 'w_glu': _jnp.float32, 'ln1_g': _jnp.float32, 'ln1_b': _jnp.float32, 'w_ff_gate': _jnp.float32, 'w_ff_up': _jnp.float32, 'w_ff_down': _jnp.float32, 'ln2_g': _jnp.float32, 'ln2_b': _jnp.float32}
MOMENT_SCALE = {'w_in': 6.849080e-03, 'b_gate': 4.310829e-03, 'w_attn_br': 6.638284e-03, 'w_ssm_br': 1.543384e-02, 'w_out': 2.601749e-02, 'ssm_a_re': 1.307853e-03, 'ssm_a_im': 1.365326e-03, 'ssm_log_dt': 1.332289e+00, 'ssm_b_re': 7.951751e-04, 'ssm_b_im': 8.276479e-04, 'ssm_c_re': 1.141920e-03, 'ssm_c_im': 1.134980e-03, 'ssm_d': 2.182769e-02, 'w_glu': 1.635842e-02, 'ln1_g': 5.060918e-01, 'ln1_b': 2.800184e-01, 'w_ff_gate': 2.232353e-02, 'w_ff_up': 2.160981e-02, 'w_ff_down': 6.018110e-02, 'ln2_g': 1.602578e+01, 'ln2_b': 7.027837e-01}


def _to_microbatches(a, axis):
    t = _jnp.moveaxis(a, axis, 0)
    t = t.reshape((N_MICROBATCH, t.shape[0] // N_MICROBATCH) + t.shape[1:])
    return _jnp.moveaxis(t, 1, axis + 1)


def setup_inputs(seed: int = 0) -> dict:
    inp = _fwd_setup_inputs(seed)
    key = _jax.random.fold_in(_jax.random.key(seed), 7919)
    shape, _ = _output_shape()
    out = dict(inp)
    out["loss_target"] = _jax.random.normal(_jax.random.fold_in(key, 0), shape, _jnp.float32)
    for i, name in enumerate(TWIN_WEIGHTS):
        w = inp[name].astype(_jnp.float32)
        if MOMENT_SCALE is None:
            s = _jnp.sqrt(_jnp.mean(_jnp.square(w)) + 1e-30)
        else:
            s = MOMENT_SCALE[name]
        km, kv = _jax.random.split(_jax.random.fold_in(key, i + 1))
        out[name] = w
        out["m_" + name] = s * _jax.random.normal(km, w.shape, _jnp.float32)
        out["v_" + name] = (s * s) * _jax.random.uniform(kv, w.shape, _jnp.float32, 0.5, 1.5)
    if N_MICROBATCH > 1:
        for name, axis in PER_EXAMPLE_BATCH_AXIS.items():
            out[name] = _to_microbatches(out[name], axis)
    return {'x': out['x'], 'w_in': out['w_in'], 'b_gate': out['b_gate'], 'w_attn_br': out['w_attn_br'], 'w_ssm_br': out['w_ssm_br'], 'w_out': out['w_out'], 'ssm_a_re': out['ssm_a_re'], 'ssm_a_im': out['ssm_a_im'], 'ssm_log_dt': out['ssm_log_dt'], 'ssm_b_re': out['ssm_b_re'], 'ssm_b_im': out['ssm_b_im'], 'ssm_c_re': out['ssm_c_re'], 'ssm_c_im': out['ssm_c_im'], 'ssm_d': out['ssm_d'], 'w_glu': out['w_glu'], 'ln1_g': out['ln1_g'], 'ln1_b': out['ln1_b'], 'w_ff_gate': out['w_ff_gate'], 'w_ff_up': out['w_ff_up'], 'w_ff_down': out['w_ff_down'], 'ln2_g': out['ln2_g'], 'ln2_b': out['ln2_b'], 'loss_target': out['loss_target'], 'm_w_in': out['m_w_in'], 'm_b_gate': out['m_b_gate'], 'm_w_attn_br': out['m_w_attn_br'], 'm_w_ssm_br': out['m_w_ssm_br'], 'm_w_out': out['m_w_out'], 'm_ssm_a_re': out['m_ssm_a_re'], 'm_ssm_a_im': out['m_ssm_a_im'], 'm_ssm_log_dt': out['m_ssm_log_dt'], 'm_ssm_b_re': out['m_ssm_b_re'], 'm_ssm_b_im': out['m_ssm_b_im'], 'm_ssm_c_re': out['m_ssm_c_re'], 'm_ssm_c_im': out['m_ssm_c_im'], 'm_ssm_d': out['m_ssm_d'], 'm_w_glu': out['m_w_glu'], 'm_ln1_g': out['m_ln1_g'], 'm_ln1_b': out['m_ln1_b'], 'm_w_ff_gate': out['m_w_ff_gate'], 'm_w_ff_up': out['m_w_ff_up'], 'm_w_ff_down': out['m_w_ff_down'], 'm_ln2_g': out['m_ln2_g'], 'm_ln2_b': out['m_ln2_b'], 'v_w_in': out['v_w_in'], 'v_b_gate': out['v_b_gate'], 'v_w_attn_br': out['v_w_attn_br'], 'v_w_ssm_br': out['v_w_ssm_br'], 'v_w_out': out['v_w_out'], 'v_ssm_a_re': out['v_ssm_a_re'], 'v_ssm_a_im': out['v_ssm_a_im'], 'v_ssm_log_dt': out['v_ssm_log_dt'], 'v_ssm_b_re': out['v_ssm_b_re'], 'v_ssm_b_im': out['v_ssm_b_im'], 'v_ssm_c_re': out['v_ssm_c_re'], 'v_ssm_c_im': out['v_ssm_c_im'], 'v_ssm_d': out['v_ssm_d'], 'v_w_glu': out['v_w_glu'], 'v_ln1_g': out['v_ln1_g'], 'v_ln1_b': out['v_ln1_b'], 'v_w_ff_gate': out['v_w_ff_gate'], 'v_w_ff_up': out['v_w_ff_up'], 'v_w_ff_down': out['v_w_ff_down'], 'v_ln2_g': out['v_ln2_g'], 'v_ln2_b': out['v_ln2_b']}


def _loss(weights, diff, rest, loss_target):
    with _jax.named_scope("forward"):
        args = {**rest, TWIN_DIFF_INPUT: diff, **{k: w.astype(_WEIGHT_DTYPES[k]) for k, w in weights.items()}}
        y = _forward(args)
    with _jax.named_scope("loss_head"):
        err = _jnp.square(y.astype(_jnp.float32) - loss_target)
        return 0.5 * _jnp.sum(_jnp.mean(err, axis=-1)) if err.ndim else 0.5 * err


def _adamw(w, g, m, v):
    m = ADAM_B1 * m + (1.0 - ADAM_B1) * g
    v = ADAM_B2 * v + (1.0 - ADAM_B2) * _jnp.square(g)
    m_hat = m / (1.0 - ADAM_B1 ** ADAM_STEP)
    v_hat = v / (1.0 - ADAM_B2 ** ADAM_STEP)
    delta = -ADAM_LR * (m_hat / (_jnp.sqrt(v_hat) + ADAM_EPS) + ADAM_WD * w)
    return delta, m, v


def reference(x, w_in, b_gate, w_attn_br, w_ssm_br, w_out, ssm_a_re, ssm_a_im, ssm_log_dt, ssm_b_re, ssm_b_im, ssm_c_re, ssm_c_im, ssm_d, w_glu, ln1_g, ln1_b, w_ff_gate, w_ff_up, w_ff_down, ln2_g, ln2_b, loss_target, m_w_in, m_b_gate, m_w_attn_br, m_w_ssm_br, m_w_out, m_ssm_a_re, m_ssm_a_im, m_ssm_log_dt, m_ssm_b_re, m_ssm_b_im, m_ssm_c_re, m_ssm_c_im, m_ssm_d, m_w_glu, m_ln1_g, m_ln1_b, m_w_ff_gate, m_w_ff_up, m_w_ff_down, m_ln2_g, m_ln2_b, v_w_in, v_b_gate, v_w_attn_br, v_w_ssm_br, v_w_out, v_ssm_a_re, v_ssm_a_im, v_ssm_log_dt, v_ssm_b_re, v_ssm_b_im, v_ssm_c_re, v_ssm_c_im, v_ssm_d, v_w_glu, v_ln1_g, v_ln1_b, v_w_ff_gate, v_w_ff_up, v_w_ff_down, v_ln2_g, v_ln2_b):
    given = dict(x=x, w_in=w_in, b_gate=b_gate, w_attn_br=w_attn_br, w_ssm_br=w_ssm_br, w_out=w_out, ssm_a_re=ssm_a_re, ssm_a_im=ssm_a_im, ssm_log_dt=ssm_log_dt, ssm_b_re=ssm_b_re, ssm_b_im=ssm_b_im, ssm_c_re=ssm_c_re, ssm_c_im=ssm_c_im, ssm_d=ssm_d, w_glu=w_glu, ln1_g=ln1_g, ln1_b=ln1_b, w_ff_gate=w_ff_gate, w_ff_up=w_ff_up, w_ff_down=w_ff_down, ln2_g=ln2_g, ln2_b=ln2_b, loss_target=loss_target, m_w_in=m_w_in, m_b_gate=m_b_gate, m_w_attn_br=m_w_attn_br, m_w_ssm_br=m_w_ssm_br, m_w_out=m_w_out, m_ssm_a_re=m_ssm_a_re, m_ssm_a_im=m_ssm_a_im, m_ssm_log_dt=m_ssm_log_dt, m_ssm_b_re=m_ssm_b_re, m_ssm_b_im=m_ssm_b_im, m_ssm_c_re=m_ssm_c_re, m_ssm_c_im=m_ssm_c_im, m_ssm_d=m_ssm_d, m_w_glu=m_w_glu, m_ln1_g=m_ln1_g, m_ln1_b=m_ln1_b, m_w_ff_gate=m_w_ff_gate, m_w_ff_up=m_w_ff_up, m_w_ff_down=m_w_ff_down, m_ln2_g=m_ln2_g, m_ln2_b=m_ln2_b, v_w_in=v_w_in, v_b_gate=v_b_gate, v_w_attn_br=v_w_attn_br, v_w_ssm_br=v_w_ssm_br, v_w_out=v_w_out, v_ssm_a_re=v_ssm_a_re, v_ssm_a_im=v_ssm_a_im, v_ssm_log_dt=v_ssm_log_dt, v_ssm_b_re=v_ssm_b_re, v_ssm_b_im=v_ssm_b_im, v_ssm_c_re=v_ssm_c_re, v_ssm_c_im=v_ssm_c_im, v_ssm_d=v_ssm_d, v_w_glu=v_w_glu, v_ln1_g=v_ln1_g, v_ln1_b=v_ln1_b, v_w_ff_gate=v_w_ff_gate, v_w_ff_up=v_w_ff_up, v_w_ff_down=v_w_ff_down, v_ln2_g=v_ln2_g, v_ln2_b=v_ln2_b)
    weights = {n: given[n] for n in TWIN_WEIGHTS}
    shared = {n: given[n] for n in SHARED_INPUTS}
    per_example = {n: given[n] for n in ['x']}
    grad_fn = _jax.value_and_grad(_loss, argnums=(0, 1))

    def one_microbatch(ex, loss_target):
        ex = dict(ex)
        diff = ex.pop(TWIN_DIFF_INPUT)
        return grad_fn(weights, diff, {**shared, **ex}, loss_target)

    if N_MICROBATCH == 1:
        loss, (grad_w, grad_x) = one_microbatch(per_example, given["loss_target"])
    else:
        def body(carry, xs):
            loss_sum, grad_sum = carry
            l_k, (gw_k, gx_k) = one_microbatch(xs[0], xs[1])
            with _jax.named_scope("update"):
                return (loss_sum + l_k, _jax.tree.map(_jnp.add, grad_sum, gw_k)), gx_k

        init = (_jnp.zeros((), _jnp.float32), _jax.tree.map(_jnp.zeros_like, weights))
        (loss, grad_w), grad_x = _jax.lax.scan(body, init, (per_example, given["loss_target"]))
    with _jax.named_scope("update"):
        delta_w, new_m, new_v = {}, {}, {}
        for n in TWIN_WEIGHTS:
            delta_w[n], new_m[n], new_v[n] = _adamw(weights[n], grad_w[n], given["m_" + n], given["v_" + n])
    return (loss, grad_x, *[grad_w[n] for n in TWIN_WEIGHTS], *[delta_w[n] for n in TWIN_WEIGHTS],
            *[new_m[n] for n in TWIN_WEIGHTS], *[new_v[n] for n in TWIN_WEIGHTS])
```

```python
import functools
import math

import jax
import jax.numpy as jnp
from jax import lax
from jax.experimental import pallas as pl
from jax.experimental.pallas import tpu as pltpu

F32 = jnp.float32
BF16 = jnp.bfloat16
MESH = pl.DeviceIdType.MESH

D_MODEL = 1024
SEQ = 2048
HEAD_DIM = 64
ATTN_HEADS = 8
DILATIONS = (1, 4, 16)
ATTN_WIDTH = ATTN_HEADS * HEAD_DIM
QKV_WIDTH = 3 * ATTN_WIDTH
BLOCK = 128
ROPE_THETA = 10000.0
NEG_INF = -1e30
SSM_GROUP = 16
SSM_GROUPS = 32
SSM_WIDTH = 512
SSM_STATE = 64
SSM_LANES = SSM_GROUPS * SSM_STATE
SCAN_CHUNKS = 8
SCAN_STEPS = SEQ // SCAN_CHUNKS
SCAN_LANES = 256
IN_WIDTH = 3 * QKV_WIDTH + SSM_WIDTH + 2 * D_MODEL
D_FF = 2816
N_CHIPS = 4
N_DEV = 8
DN_ALPHA = 2.0 ** 0.25
LN_EPS = 1e-5
ADAM_LR = 0.001
ADAM_B1 = 0.9
ADAM_B2 = 0.999
ADAM_EPS = 1e-08
ADAM_WD = 0.01
ADAM_STEP = 10
GELU_C = math.sqrt(2.0 / math.pi)
GELU_K = 0.044715

VMEM_LIMIT_BYTES = 56 * 1024 * 1024


def _sds(shape, dtype=F32):
    return jax.ShapeDtypeStruct(tuple(shape), dtype)


def _cp(semantics=None):
    return pltpu.CompilerParams(dimension_semantics=semantics, vmem_limit_bytes=VMEM_LIMIT_BYTES)


def _matmul(a, b, *, grid, a_spec, b_spec, o_spec, out_shape, dims, k_axis=None, name):
    nk = grid[k_axis] if k_axis is not None else 1
    o_block = tuple(d for d in o_spec.block_shape if d is not None)

    def body(a_ref, b_ref, o_ref, *acc):
        part = lax.dot_general(a_ref[...].astype(BF16), b_ref[...].astype(BF16),
                               (((dims[0],), (dims[1],)), ((), ())), preferred_element_type=F32)
        if k_axis is None:
            o_ref[...] = part.astype(o_ref.dtype)
        else:
            k = pl.program_id(k_axis)

            @pl.when(k == 0)
            def _():
                acc[0][...] = part

            @pl.when(k > 0)
            def _():
                acc[0][...] += part

            @pl.when(k == nk - 1)
            def _():
                o_ref[...] = acc[0][...].astype(o_ref.dtype)

    sem = tuple("arbitrary" if ax == k_axis else "parallel" for ax in range(len(grid)))
    return pl.pallas_call(
        body, grid=grid, in_specs=[a_spec, b_spec], out_specs=o_spec, out_shape=out_shape,
        scratch_shapes=[pltpu.VMEM(o_block, F32)] if k_axis is not None else [],
        compiler_params=_cp(sem), name=name)(a, b)


def _mm_cols(a, wg, *, tm, name, out_dtype=F32, out3d=False):
    m, k = a.shape
    ns = wg.shape[2]
    if out3d:
        o_spec = pl.BlockSpec((None, tm, ns), lambda i, s: (s, i, 0))
        out_shape = _sds((N_CHIPS, m, ns), out_dtype)
    else:
        o_spec = pl.BlockSpec((tm, ns), lambda i, s: (i, s))
        out_shape = _sds((m, N_CHIPS * ns), out_dtype)
    return _matmul(a, wg, grid=(m // tm, N_CHIPS),
                   a_spec=pl.BlockSpec((tm, k), lambda i, s: (i, 0)),
                   b_spec=pl.BlockSpec((None, k, ns), lambda i, s: (s, 0, 0)),
                   o_spec=o_spec, out_shape=out_shape, dims=(1, 0), name=name)


def _mm_cols_nt(dy, wg, *, tm, name, dy3d=False, out_dtype=F32):
    k, ns = wg.shape[1], wg.shape[2]
    if dy3d:
        m = dy.shape[1]
        a_spec = pl.BlockSpec((None, tm, ns), lambda i, s: (s, i, 0))
    else:
        m = dy.shape[0]
        a_spec = pl.BlockSpec((tm, ns), lambda i, s: (i, s))
    return _matmul(dy, wg, grid=(m // tm, N_CHIPS), a_spec=a_spec,
                   b_spec=pl.BlockSpec((None, k, ns), lambda i, s: (s, 0, 0)),
                   o_spec=pl.BlockSpec((tm, k), lambda i, s: (i, 0)),
                   out_shape=_sds((m, k), out_dtype), dims=(1, 1), k_axis=1, name=name)


def _mm_cols_tn(a, dy, *, ns, name, dy3d=False):
    m, k = a.shape
    if dy3d:
        b_spec = pl.BlockSpec((None, m, ns), lambda s: (s, 0, 0))
    else:
        b_spec = pl.BlockSpec((m, ns), lambda s: (0, s))
    return _matmul(a, dy, grid=(N_CHIPS,), a_spec=pl.BlockSpec((m, k), lambda s: (0, 0)), b_spec=b_spec,
                   o_spec=pl.BlockSpec((None, k, ns), lambda s: (s, 0, 0)),
                   out_shape=_sds((N_CHIPS, k, ns), BF16), dims=(0, 0), name=name)


def _mm_plain(a, b, *, tm, tn, name, out_dtype=F32, dims=(1, 0), tk=None):
    m = a.shape[1 - dims[0]]
    kk = a.shape[dims[0]]
    n = b.shape[1 - dims[1]]
    tk = kk if tk is None else tk
    nk = kk // tk

    def a_idx(i, j, k):
        return (i, k) if dims[0] == 1 else (k, i)

    def b_idx(i, j, k):
        return (k, j) if dims[1] == 0 else (j, k)

    a_blk = (tm, tk) if dims[0] == 1 else (tk, tm)
    b_blk = (tk, tn) if dims[1] == 0 else (tn, tk)
    return _matmul(a, b, grid=(m // tm, n // tn, nk),
                   a_spec=pl.BlockSpec(a_blk, a_idx), b_spec=pl.BlockSpec(b_blk, b_idx),
                   o_spec=pl.BlockSpec((tm, tn), lambda i, j, k: (i, j)),
                   out_shape=_sds((m, n), out_dtype), dims=dims, k_axis=2 if nk > 1 else None, name=name)


def _rowwise(fn, tiled, full, outs, accs=(), *, tm, name):
    args, in_specs = [], []
    for t in tiled:
        if isinstance(t, tuple):
            arr, w, cb = t
            in_specs.append(pl.BlockSpec((tm, w), lambda i, cb=cb: (i, cb)))
        else:
            arr = t
            in_specs.append(pl.BlockSpec((tm, arr.shape[1]), lambda i: (i, 0)))
        args.append(arr)
    rows = args[0].shape[0]
    for f in full:
        in_specs.append(pl.BlockSpec(f.shape, lambda i, nd=f.ndim: (0,) * nd))
        args.append(f)
    out_specs = [pl.BlockSpec((tm, o.shape[1]), lambda i: (i, 0)) for o in outs]
    out_specs += [pl.BlockSpec(a.shape, lambda i, nd=len(a.shape): (0,) * nd) for a in accs]
    n_in, n_out = len(args), len(outs)

    def body(*refs):
        res = fn(*[r[...] for r in refs[:n_in]])
        res = res if isinstance(res, (tuple, list)) else (res,)
        for r, v in zip(refs[n_in:n_in + n_out], res[:n_out]):
            r[...] = v.astype(r.dtype)
        i = pl.program_id(0)
        for r, v in zip(refs[n_in + n_out:], res[n_out:]):
            @pl.when(i == 0)
            def _(r=r, v=v):
                r[...] = v

            @pl.when(i > 0)
            def _(r=r, v=v):
                r[...] += v

    res = pl.pallas_call(
        body, grid=(rows // tm,), in_specs=in_specs, out_specs=out_specs, out_shape=list(outs) + list(accs),
        compiler_params=_cp(("arbitrary",) if accs else ("parallel",)), name=name)(*args)
    return res


def _colsum(v):
    return jnp.sum(v, axis=0, keepdims=True)


def _ln_stats(z):
    mu = jnp.mean(z, axis=-1, keepdims=True)
    zc = z - mu
    var = jnp.mean(zc * zc, axis=-1, keepdims=True)
    rstd = lax.rsqrt(var + LN_EPS)
    return zc * rstd, rstd


def _ln_bwd(dy, xhat, rstd, g):
    dxh = dy * g
    m1 = jnp.mean(dxh, axis=-1, keepdims=True)
    m2 = jnp.mean(dxh * xhat, axis=-1, keepdims=True)
    return rstd * (dxh - m1 - xhat * m2)


def _swap_halves(t):
    w = t.shape[-1]
    lane = lax.broadcasted_iota(jnp.int32, t.shape, t.ndim - 1)
    return jnp.where((lane % HEAD_DIM) < HEAD_DIM // 2, pltpu.roll(t, w - HEAD_DIM // 2, t.ndim - 1),
                     pltpu.roll(t, HEAD_DIM // 2, t.ndim - 1))


def _band_mask(i):
    row = lax.broadcasted_iota(jnp.int32, (BLOCK, 2 * BLOCK), 0)
    col = lax.broadcasted_iota(jnp.int32, (BLOCK, 2 * BLOCK), 1)
    dist = BLOCK + row - col
    return (dist >= 0) & (dist <= BLOCK) & ((col >= BLOCK) | (i > 0))


def _attn_fwd(q, k, v, *, name):
    n, c = q.shape
    cw = ATTN_WIDTH

    def body(q_ref, kp_ref, kc_ref, vp_ref, vc_ref, o_ref, lse_ref):
        valid = _band_mask(pl.program_id(1))
        lane = lax.broadcasted_iota(jnp.int32, (BLOCK, 2 * HEAD_DIM), 1)
        first = lane < HEAD_DIM
        for hp in range(cw // (2 * HEAD_DIM)):
            sl = slice(hp * 2 * HEAD_DIM, (hp + 1) * 2 * HEAD_DIM)
            qq = q_ref[:, sl]
            kk = jnp.concatenate([kp_ref[:, sl], kc_ref[:, sl]], axis=0)
            vv = jnp.concatenate([vp_ref[:, sl], vc_ref[:, sl]], axis=0)
            zero = jnp.zeros_like(qq)
            outs, lses = [], []
            for qh in (jnp.where(first, qq, zero), jnp.where(first, zero, qq)):
                s = lax.dot_general(qh, kk, (((1,), (1,)), ((), ())), preferred_element_type=F32)
                s = jnp.where(valid, s, NEG_INF)
                m = jnp.max(s, axis=1, keepdims=True)
                p = jnp.exp(s - m)
                l = jnp.sum(p, axis=1, keepdims=True)
                o = jnp.dot(p.astype(BF16), vv, preferred_element_type=F32)
                outs.append(o / l)
                lses.append(m + jnp.log(l))
            o_ref[:, sl] = jnp.where(first, outs[0], outs[1])
            lse_ref[:, sl] = jnp.where(first, lses[0], lses[1])

    cur = pl.BlockSpec((BLOCK, cw), lambda cb, i: (i, cb))
    prev = pl.BlockSpec((BLOCK, cw), lambda cb, i: (jnp.maximum(i - 1, 0), cb))
    return pl.pallas_call(
        body, grid=(c // cw, n // BLOCK), in_specs=[cur, prev, cur, prev, cur], out_specs=[cur, cur],
        out_shape=[_sds((n, c)), _sds((n, c))], compiler_params=_cp(("parallel", "parallel")), name=name)(q, k, k, v, v)


def _attn_bwd(q, k, v, do, o, lse, *, name):
    n, c = q.shape
    cw = ATTN_WIDTH

    def body(q_ref, kp_ref, kc_ref, vp_ref, vc_ref, do_ref, o_ref, lse_ref, dq_ref, dk_ref, dv_ref):
        i = pl.program_id(1)
        valid = _band_mask(i)
        lane = lax.broadcasted_iota(jnp.int32, (BLOCK, 2 * HEAD_DIM), 1)
        first = lane < HEAD_DIM
        first2 = lax.broadcasted_iota(jnp.int32, (2 * BLOCK, 2 * HEAD_DIM), 1) < HEAD_DIM

        @pl.when(i == 0)
        def _():
            dk_ref[...] = jnp.zeros_like(dk_ref)
            dv_ref[...] = jnp.zeros_like(dv_ref)

        for hp in range(cw // (2 * HEAD_DIM)):
            sl = slice(hp * 2 * HEAD_DIM, (hp + 1) * 2 * HEAD_DIM)
            qq = q_ref[:, sl]
            kk = jnp.concatenate([kp_ref[:, sl], kc_ref[:, sl]], axis=0)
            vv = jnp.concatenate([vp_ref[:, sl], vc_ref[:, sl]], axis=0)
            dof = do_ref[:, sl]
            dd = dof * o_ref[:, sl]
            lse2 = lse_ref[:, sl]
            dob = dof.astype(BF16)
            zq = jnp.zeros_like(qq)
            zd = jnp.zeros_like(dob)
            zf = jnp.zeros_like(dd)
            dqs, dks, dvs = [], [], []
            for h in range(2):
                sel = first if h == 0 else jnp.logical_not(first)
                qh = jnp.where(sel, qq, zq)
                doh = jnp.where(sel, dob, zd)
                delta = jnp.sum(jnp.where(sel, dd, zf), axis=1, keepdims=True)
                lse_h = lse2[:, h * HEAD_DIM:h * HEAD_DIM + 1]
                s = lax.dot_general(qh, kk, (((1,), (1,)), ((), ())), preferred_element_type=F32)
                p = jnp.where(valid, jnp.exp(s - lse_h), 0.0)
                dp = lax.dot_general(doh, vv, (((1,), (1,)), ((), ())), preferred_element_type=F32)
                ds = (p * (dp - delta)).astype(BF16)
                pb = p.astype(BF16)
                dqs.append(jnp.dot(ds, kk, preferred_element_type=F32))
                dks.append(lax.dot_general(ds, qq, (((0,), (0,)), ((), ())), preferred_element_type=F32))
                dvs.append(lax.dot_general(pb, dob, (((0,), (0,)), ((), ())), preferred_element_type=F32))
            dq_ref[:, sl] = jnp.where(first, dqs[0], dqs[1])
            dk2 = jnp.where(first2, dks[0], dks[1])
            dv2 = jnp.where(first2, dvs[0], dvs[1])
            r_cur = pl.ds(pl.multiple_of(i * BLOCK, BLOCK), BLOCK)
            dk_ref[r_cur, sl] += dk2[BLOCK:]
            dv_ref[r_cur, sl] += dv2[BLOCK:]

            @pl.when(i > 0)
            def _(dk2=dk2, dv2=dv2, sl=sl):
                r_prev = pl.ds(pl.multiple_of((i - 1) * BLOCK, BLOCK), BLOCK)
                dk_ref[r_prev, sl] += dk2[:BLOCK]
                dv_ref[r_prev, sl] += dv2[:BLOCK]

    cur = pl.BlockSpec((BLOCK, cw), lambda cb, i: (i, cb))
    prev = pl.BlockSpec((BLOCK, cw), lambda cb, i: (jnp.maximum(i - 1, 0), cb))
    whole = pl.BlockSpec((n, cw), lambda cb, i: (0, cb))
    return pl.pallas_call(
        body, grid=(c // cw, n // BLOCK), in_specs=[cur, prev, cur, prev, cur, cur, cur, cur],
        out_specs=[cur, whole, whole], out_shape=[_sds((n, c)), _sds((n, c)), _sds((n, c))],
        compiler_params=_cp(("parallel", "arbitrary")), name=name)(q, k, k, v, v, do, o, lse)


def _cmul(ar, ai, br, bi):
    return ar * br - ai * bi, ar * bi + ai * br


def _pow256(ar, ai):
    for _ in range(8):
        ar, ai = _cmul(ar, ai, ar, ai)
    return ar, ai


def _chunk_carries(first_r, first_i, pr, pi, reverse):
    rows = lax.broadcasted_iota(jnp.int32, first_r.shape, 0)
    out_r = jnp.zeros_like(first_r)
    out_i = jnp.zeros_like(first_i)
    hr = jnp.zeros_like(first_r[0:1])
    hi = jnp.zeros_like(hr)
    order = range(SCAN_CHUNKS - 1, -1, -1) if reverse else range(SCAN_CHUNKS)
    for c in order:
        out_r = jnp.where(rows == c, hr, out_r)
        out_i = jnp.where(rows == c, hi, out_i)
        tr, ti = _cmul(pr[0:1], pi[0:1], hr, hi)
        hr = first_r[c:c + 1] + tr
        hi = first_i[c:c + 1] + ti
    return out_r, out_i


def _tile(j):
    return pl.ds(pl.multiple_of(j * SCAN_CHUNKS, SCAN_CHUNKS), SCAN_CHUNKS)


def _scan_fwd(bur, bui, ar, ai):
    lb = SCAN_LANES

    def body(bur_ref, bui_ref, ar_ref, ai_ref, hr_ref, hi_ref, er_ref, ei_ref):
        a_r = jnp.broadcast_to(ar_ref[...], (SCAN_CHUNKS, lb))
        a_i = jnp.broadcast_to(ai_ref[...], (SCAN_CHUNKS, lb))

        def local(j, carry):
            tr, ti = _cmul(a_r, a_i, carry[0], carry[1])
            nr = tr + bur_ref[_tile(j), :]
            ni = ti + bui_ref[_tile(j), :]
            hr_ref[_tile(j), :] = nr
            hi_ref[_tile(j), :] = ni
            return nr, ni

        zero = jnp.zeros((SCAN_CHUNKS, lb), F32)
        last_r, last_i = lax.fori_loop(0, SCAN_STEPS, local, (zero, zero), unroll=4)
        pr, pi = _pow256(a_r, a_i)
        er, ei = _chunk_carries(last_r, last_i, pr, pi, reverse=False)
        er_ref[...] = er
        ei_ref[...] = ei

        def fix(j, carry):
            tr, ti = _cmul(carry[0], carry[1], er, ei)
            hr_ref[_tile(j), :] += tr
            hi_ref[_tile(j), :] += ti
            return _cmul(carry[0], carry[1], a_r, a_i)

        lax.fori_loop(0, SCAN_STEPS, fix, (a_r, a_i), unroll=4)

    big = pl.BlockSpec((SEQ, lb), lambda j: (0, j))
    vec = pl.BlockSpec((1, lb), lambda j: (0, j))
    ent = pl.BlockSpec((SCAN_CHUNKS, lb), lambda j: (0, j))
    return pl.pallas_call(
        body, grid=(SSM_LANES // lb,), in_specs=[big, big, vec, vec], out_specs=[big, big, ent, ent],
        out_shape=[_sds((SEQ, SSM_LANES)), _sds((SEQ, SSM_LANES)), _sds((SCAN_CHUNKS, SSM_LANES)),
                   _sds((SCAN_CHUNKS, SSM_LANES))],
        compiler_params=_cp(("parallel",)), name="ssm_scan_fwd")(bur, bui, ar, ai)


def _scan_bwd(gr, gi, hr, hi, er, ei, ar, ai):
    lb = SCAN_LANES

    def body(gr_ref, gi_ref, hr_ref, hi_ref, er_ref, ei_ref, ar_ref, ai_ref, lr_ref, li_ref, dar_ref, dai_ref):
        a_r = jnp.broadcast_to(ar_ref[...], (SCAN_CHUNKS, lb))
        a_i = -jnp.broadcast_to(ai_ref[...], (SCAN_CHUNKS, lb))

        def local(t, carry):
            j = SCAN_STEPS - 1 - t
            tr, ti = _cmul(a_r, a_i, carry[0], carry[1])
            nr = tr + gr_ref[_tile(j), :]
            ni = ti + gi_ref[_tile(j), :]
            lr_ref[_tile(j), :] = nr
            li_ref[_tile(j), :] = ni
            return nr, ni

        zero = jnp.zeros((SCAN_CHUNKS, lb), F32)
        first_r, first_i = lax.fori_loop(0, SCAN_STEPS, local, (zero, zero), unroll=4)
        pr, pi = _pow256(a_r, a_i)
        nxt_r, nxt_i = _chunk_carries(first_r, first_i, pr, pi, reverse=True)

        def accumulate(lam_r, lam_i, hp_r, hp_i, acc):
            return (acc[0] + lam_r * hp_r + lam_i * hp_i, acc[1] + lam_i * hp_r - lam_r * hp_i)

        def fix(t, carry):
            qr, qi, acc_r, acc_i = carry
            j = SCAN_STEPS - 1 - t
            tr, ti = _cmul(qr, qi, nxt_r, nxt_i)
            lam_r = lr_ref[_tile(j), :] + tr
            lam_i = li_ref[_tile(j), :] + ti
            lr_ref[_tile(j), :] = lam_r
            li_ref[_tile(j), :] = lam_i
            acc_r, acc_i = accumulate(lam_r, lam_i, hr_ref[_tile(j - 1), :], hi_ref[_tile(j - 1), :], (acc_r, acc_i))
            qr, qi = _cmul(qr, qi, a_r, a_i)
            return qr, qi, acc_r, acc_i

        qr, qi, acc_r, acc_i = lax.fori_loop(0, SCAN_STEPS - 1, fix, (a_r, a_i, zero, zero), unroll=4)
        tr, ti = _cmul(qr, qi, nxt_r, nxt_i)
        lam_r = lr_ref[_tile(0), :] + tr
        lam_i = li_ref[_tile(0), :] + ti
        lr_ref[_tile(0), :] = lam_r
        li_ref[_tile(0), :] = lam_i
        acc_r, acc_i = accumulate(lam_r, lam_i, er_ref[...], ei_ref[...], (acc_r, acc_i))
        dar_ref[...] = jnp.sum(acc_r, axis=0, keepdims=True)
        dai_ref[...] = jnp.sum(acc_i, axis=0, keepdims=True)

    big = pl.BlockSpec((SEQ, lb), lambda j: (0, j))
    vec = pl.BlockSpec((1, lb), lambda j: (0, j))
    ent = pl.BlockSpec((SCAN_CHUNKS, lb), lambda j: (0, j))
    return pl.pallas_call(
        body, grid=(SSM_LANES // lb,), in_specs=[big, big, big, big, ent, ent, vec, vec],
        out_specs=[big, big, vec, vec],
        out_shape=[_sds((SEQ, SSM_LANES)), _sds((SEQ, SSM_LANES)), _sds((1, SSM_LANES)), _sds((1, SSM_LANES))],
        compiler_params=_cp(("parallel",)), name="ssm_scan_bwd")(gr, gi, hr, hi, er, ei, ar, ai)


def _rope_tables():
    half = HEAD_DIM // 2
    inv_freq = ROPE_THETA ** (-jnp.arange(half, dtype=F32) / half)
    ang = jnp.arange(SEQ, dtype=F32)[:, None] * inv_freq[None, :]
    cos, sin = jnp.cos(ang), jnp.sin(ang)
    cos_f = jnp.concatenate([cos, cos, cos, cos], axis=1)
    sin_s = jnp.concatenate([-sin, sin, -sin, sin], axis=1)
    return cos_f, sin_s


def _tile_lanes(t, width):
    return jnp.concatenate([t] * (width // t.shape[1]), axis=1)


def _to_scan_rows(t):
    return t.reshape(SCAN_CHUNKS, SCAN_STEPS, t.shape[1]).transpose(1, 0, 2).reshape(t.shape)


def _from_scan_rows(t):
    return t.reshape(SCAN_STEPS, SCAN_CHUNKS, t.shape[1]).transpose(1, 0, 2).reshape(t.shape)


def _ssm_discretise(a_re, a_im, log_dt, b_re, b_im):
    lam = lax.complex(a_re, a_im)
    dt = jnp.exp(log_dt)[:, None]
    a_bar = jnp.exp(lam * dt)
    b_bar = ((a_bar - 1.0) / lam)[..., None] * lax.complex(b_re, b_im)
    return a_bar.real, a_bar.imag, b_bar.real, b_bar.imag


def _block_diag_in(b):
    eye = jnp.eye(SSM_GROUPS, dtype=b.dtype)
    return (eye[:, None, :, None] * b.transpose(0, 2, 1)[:, :, None, :]).reshape(SSM_WIDTH, SSM_LANES)


def _block_diag_out(c):
    eye = jnp.eye(SSM_GROUPS, dtype=c.dtype)
    return (eye[:, None, :, None] * c.transpose(0, 2, 1)[:, :, None, :]).reshape(SSM_LANES, SSM_WIDTH)


def _diag_blocks(m, rows, cols):
    m4 = m.reshape(SSM_GROUPS, rows, SSM_GROUPS, cols)
    return jnp.diagonal(m4, axis1=0, axis2=2).transpose(2, 0, 1)


def _local_step(x, tgt, wts, small):
    s = SEQ
    cos_f, sin_s = _rope_tables()

    proj = _mm_cols(x, wts["w_in"], tm=1024, name="proj")

    def rope_split(qr, kr, vr, cf, ss):
        cf4, ss4 = _tile_lanes(cf, QKV_WIDTH), _tile_lanes(ss, QKV_WIDTH)
        q = (qr * cf4 + _swap_halves(qr) * ss4) * (1.0 / math.sqrt(HEAD_DIM))
        k = kr * cf4 + _swap_halves(kr) * ss4
        res = []
        for t in (q, k, vr):
            res += [t[:, g * ATTN_WIDTH:(g + 1) * ATTN_WIDTH] for g in range(3)]
        return res

    qkv = _rowwise(rope_split, [(proj, QKV_WIDTH, 0), (proj, QKV_WIDTH, 1), (proj, QKV_WIDTH, 2), cos_f, sin_s], [],
                   [_sds((s, ATTN_WIDTH), BF16)] * 9, tm=256, name="rope_split")
    qg = [qkv[g].reshape(s // d, d * ATTN_WIDTH) for g, d in enumerate(DILATIONS)]
    kg = [qkv[3 + g].reshape(s // d, d * ATTN_WIDTH) for g, d in enumerate(DILATIONS)]
    vg = [qkv[6 + g].reshape(s // d, d * ATTN_WIDTH) for g, d in enumerate(DILATIONS)]

    o_l = [_attn_fwd(qg[g], kg[g], vg[g], name=f"attn_fwd_{g}") for g in range(3)]

    def merge(o0, o1, o2, l0, l1, l2):
        m = jnp.maximum(jnp.maximum(l0, l1), l2)
        lse = m + jnp.log(jnp.exp(l0 - m) + jnp.exp(l1 - m) + jnp.exp(l2 - m))
        return jnp.exp(l0 - lse) * o0 + jnp.exp(l1 - lse) * o1 + jnp.exp(l2 - lse) * o2, lse

    attn, lse = _rowwise(merge, [o_l[g][0].reshape(s, ATTN_WIDTH) for g in range(3)]
                         + [o_l[g][1].reshape(s, ATTN_WIDTH) for g in range(3)], [],
                         [_sds((s, ATTN_WIDTH)), _sds((s, ATTN_WIDTH))], tm=512, name="attn_merge")
    y_attn = _mm_cols(attn, wts["w_attn_br"], tm=s, name="y_attn")

    (abar_r, abar_i, bbar_r, bbar_i), ssm_vjp = jax.vjp(
        _ssm_discretise, small["ssm_a_re"], small["ssm_a_im"], small["ssm_log_dt"], small["ssm_b_re"], small["ssm_b_im"])
    b_in_r, b_in_i = _block_diag_in(bbar_r).astype(BF16), _block_diag_in(bbar_i).astype(BF16)
    c_out_r = _block_diag_out(small["ssm_c_re"]).astype(BF16)
    c_out_ni = _block_diag_out(-small["ssm_c_im"]).astype(BF16)
    a_r, a_i = abar_r.reshape(1, SSM_LANES), abar_i.reshape(1, SSM_LANES)
    d_skip = small["ssm_d"].reshape(1, SSM_WIDTH)

    u_f = _to_scan_rows(proj[:, 3 * QKV_WIDTH:3 * QKV_WIDTH + SSM_WIDTH])
    u_p = u_f.astype(BF16)
    bu_r = _mm_plain(u_p, b_in_r, tm=s, tn=512, name="ssm_bu_re")
    bu_i = _mm_plain(u_p, b_in_i, tm=s, tn=512, name="ssm_bu_im")
    h_r, h_i, e_r, e_i = _scan_fwd(bu_r, bu_i, a_r, a_i)
    y_1 = _mm_plain(h_r, c_out_r, tm=s, tn=512, tk=512, name="ssm_y_re")
    y_2 = _mm_plain(h_i, c_out_ni, tm=s, tn=512, tk=512, name="ssm_y_im")

    def gelu_fwd(y1, y2, u, dsk):
        y = y1 + y2 + dsk * u
        return y, 0.5 * y * (1.0 + jnp.tanh(GELU_C * (y + GELU_K * y * y * y)))

    y_s5, gel = _rowwise(gelu_fwd, [y_1, y_2, u_f], [d_skip], [_sds((s, SSM_WIDTH)), _sds((s, SSM_WIDTH), BF16)],
                         tm=512, name="ssm_gelu")
    glu = _mm_cols(gel, wts["w_glu"], tm=s, name="glu")

    def glu_fwd(ga, gb):
        return ga * jax.nn.sigmoid(gb)

    (y_glu,) = _rowwise(glu_fwd, [(glu, SSM_WIDTH, 0), (glu, SSM_WIDTH, 1)], [], [_sds((s, SSM_WIDTH), BF16)],
                        tm=512, name="glu_gate")
    y_ssm = _from_scan_rows(_mm_cols(y_glu, wts["w_ssm_br"], tm=s, name="y_ssm"))

    gl0 = (proj, D_MODEL, (3 * QKV_WIDTH + SSM_WIDTH) // D_MODEL)
    gl1 = (proj, D_MODEL, (3 * QKV_WIDTH + SSM_WIDTH) // D_MODEL + 1)
    b_gate = small["b_gate"]

    def gate_mix(l0, l1, ya, ys, bg):
        return jax.nn.sigmoid(l0 + bg[0:1]) * ya + jax.nn.sigmoid(l1 + bg[1:2]) * ys

    (mixed,) = _rowwise(gate_mix, [gl0, gl1, y_attn, y_ssm], [b_gate], [_sds((s, D_MODEL), BF16)], tm=256,
                        name="gate_mix")
    w_out = wts["w_out"].reshape(D_MODEL, D_MODEL)
    mix_out = _mm_plain(mixed, w_out, tm=1024, tn=512, name="mix_out")

    def ln1_fwd(xv, mo, g, b):
        z = DN_ALPHA * xv + mo
        xhat, _ = _ln_stats(z)
        return z, xhat * g + b

    z1, h = _rowwise(ln1_fwd, [x, mix_out], [small["ln1_g"], small["ln1_b"]],
                     [_sds((s, D_MODEL)), _sds((s, D_MODEL))], tm=256, name="ln1")

    ff_a = _mm_cols(h, wts["w_ff_gate"], tm=1024, name="ff_gate", out3d=True)
    ff_b = _mm_cols(h, wts["w_ff_up"], tm=1024, name="ff_up", out3d=True)
    nf = D_FF // N_CHIPS

    def swiglu_fwd(a, b):
        return a * jax.nn.sigmoid(a) * b

    (act,) = _rowwise(swiglu_fwd, [ff_a.reshape(N_CHIPS * s, nf), ff_b.reshape(N_CHIPS * s, nf)], [],
                      [_sds((N_CHIPS * s, nf), BF16)], tm=1024, name="swiglu")
    act = act.reshape(N_CHIPS, s, nf)
    w_down = wts["w_ff_down"]
    ff = _matmul(act, w_down, grid=(2, N_CHIPS),
                 a_spec=pl.BlockSpec((None, 1024, nf), lambda i, k: (k, i, 0)),
                 b_spec=pl.BlockSpec((None, nf, D_MODEL), lambda i, k: (k, 0, 0)),
                 o_spec=pl.BlockSpec((1024, D_MODEL), lambda i, k: (i, 0)),
                 out_shape=_sds((s, D_MODEL)), dims=(1, 0), k_axis=1, name="ff_down")

    def ln2_loss(hv, ffv, tg, g, b):
        z = DN_ALPHA * hv + ffv
        xhat, rstd = _ln_stats(z)
        err = xhat * g + b - tg
        d_out = err * (1.0 / D_MODEL)
        loss_rows = jnp.sum(err * err, axis=-1, keepdims=True) * (0.5 / D_MODEL)
        loss = jnp.broadcast_to(jnp.sum(loss_rows, axis=0, keepdims=True), (1, 128))
        return _ln_bwd(d_out, xhat, rstd, g), loss, _colsum(d_out * xhat), _colsum(d_out)

    dz2, loss_v, d_ln2_g, d_ln2_b = _rowwise(
        ln2_loss, [h, ff, tgt], [small["ln2_g"], small["ln2_b"]], [_sds((s, D_MODEL))],
        [_sds((1, 128)), _sds((1, D_MODEL)), _sds((1, D_MODEL))], tm=256, name="ln2_loss")

    d_act = _matmul(dz2, w_down, grid=(2, N_CHIPS),
                    a_spec=pl.BlockSpec((1024, D_MODEL), lambda i, k: (i, 0)),
                    b_spec=pl.BlockSpec((None, nf, D_MODEL), lambda i, k: (k, 0, 0)),
                    o_spec=pl.BlockSpec((None, 1024, nf), lambda i, k: (k, i, 0)),
                    out_shape=_sds((N_CHIPS, s, nf)), dims=(1, 1), name="d_act")
    g_w_ff_down = _matmul(act, dz2, grid=(N_CHIPS,),
                          a_spec=pl.BlockSpec((None, s, nf), lambda k: (k, 0, 0)),
                          b_spec=pl.BlockSpec((s, D_MODEL), lambda k: (0, 0)),
                          o_spec=pl.BlockSpec((None, nf, D_MODEL), lambda k: (k, 0, 0)),
                          out_shape=_sds((N_CHIPS, nf, D_MODEL), BF16), dims=(0, 0), name="g_w_ff_down")

    def swiglu_bwd(da, a, b):
        sg = jax.nn.sigmoid(a)
        return da * b * sg * (1.0 + a * (1.0 - sg)), da * a * sg

    d_a, d_b = _rowwise(swiglu_bwd, [d_act.reshape(N_CHIPS * s, nf), ff_a.reshape(N_CHIPS * s, nf),
                                     ff_b.reshape(N_CHIPS * s, nf)], [],
                        [_sds((N_CHIPS * s, nf), BF16)] * 2, tm=1024, name="swiglu_bwd")
    d_a, d_b = d_a.reshape(N_CHIPS, s, nf), d_b.reshape(N_CHIPS, s, nf)
    g_w_ff_gate = _mm_cols_tn(h, d_a, ns=nf, name="g_w_ff_gate", dy3d=True)
    g_w_ff_up = _mm_cols_tn(h, d_b, ns=nf, name="g_w_ff_up", dy3d=True)
    dh_a = _mm_cols_nt(d_a, wts["w_ff_gate"], tm=1024, name="dh_gate", dy3d=True)
    dh_b = _mm_cols_nt(d_b, wts["w_ff_up"], tm=1024, name="dh_up", dy3d=True)

    def ln1_bwd(dz, da, db, z, g):
        xhat, rstd = _ln_stats(z)
        dh = DN_ALPHA * dz + da + db
        return _ln_bwd(dh, xhat, rstd, g), _colsum(dh * xhat), _colsum(dh)

    dz1, d_ln1_g, d_ln1_b = _rowwise(ln1_bwd, [dz2, dh_a, dh_b, z1], [small["ln1_g"]], [_sds((s, D_MODEL))],
                                     [_sds((1, D_MODEL)), _sds((1, D_MODEL))], tm=256, name="ln1_bwd")
    d_mixed = _mm_plain(dz1, w_out, tm=1024, tn=512, dims=(1, 1), name="d_mixed")
    g_w_out = _mm_plain(mixed, dz1, tm=D_MODEL, tn=512, dims=(0, 0), out_dtype=BF16, name="g_w_out")
    g_w_out = g_w_out.reshape(N_CHIPS, D_MODEL // N_CHIPS, D_MODEL)

    def gate_bwd(dm, l0, l1, ya, ys, bg):
        g0 = jax.nn.sigmoid(l0 + bg[0:1])
        g1 = jax.nn.sigmoid(l1 + bg[1:2])
        dl0 = dm * ya * g0 * (1.0 - g0)
        dl1 = dm * ys * g1 * (1.0 - g1)
        return dm * g0, dm * g1, jnp.concatenate([dl0, dl1], axis=1), _colsum(dl0), _colsum(dl1)

    d_y_attn, d_y_ssm, d_gl, d_bg0, d_bg1 = _rowwise(
        gate_bwd, [d_mixed, gl0, gl1, y_attn, y_ssm], [b_gate],
        [_sds((s, D_MODEL), BF16), _sds((s, D_MODEL), BF16), _sds((s, 2 * D_MODEL), BF16)],
        [_sds((1, D_MODEL)), _sds((1, D_MODEL))], tm=256, name="gate_bwd")

    d_y_ssm_p = _to_scan_rows(d_y_ssm)
    g_w_ssm_br = _mm_cols_tn(y_glu, d_y_ssm_p, ns=D_MODEL // N_CHIPS, name="g_w_ssm_br")
    d_y_glu = _mm_cols_nt(d_y_ssm_p, wts["w_ssm_br"], tm=s, name="d_y_glu")

    def glu_bwd(dy, ga, gb):
        sg = jax.nn.sigmoid(gb)
        return jnp.concatenate([dy * sg, dy * ga * sg * (1.0 - sg)], axis=1)

    (d_glu,) = _rowwise(glu_bwd, [d_y_glu, (glu, SSM_WIDTH, 0), (glu, SSM_WIDTH, 1)], [],
                        [_sds((s, 2 * SSM_WIDTH), BF16)], tm=512, name="glu_bwd")
    g_w_glu = _mm_cols_tn(gel, d_glu, ns=2 * SSM_WIDTH // N_CHIPS, name="g_w_glu")
    d_gel = _mm_cols_nt(d_glu, wts["w_glu"], tm=s, name="d_gel")

    def gelu_bwd(dg, y, u, dsk):
        th = jnp.tanh(GELU_C * (y + GELU_K * y * y * y))
        dy = dg * (0.5 * (1.0 + th) + 0.5 * y * (1.0 - th * th) * GELU_C * (1.0 + 3.0 * GELU_K * y * y))
        return dy, dy * dsk, _colsum(dy * u)

    d_y, d_u_skip, d_ssm_d = _rowwise(gelu_bwd, [d_gel, y_s5, u_f], [d_skip],
                                      [_sds((s, SSM_WIDTH), BF16), _sds((s, SSM_WIDTH))], [_sds((1, SSM_WIDTH))],
                                      tm=512, name="gelu_bwd")
    g_h_r = _mm_plain(d_y, c_out_r, tm=s, tn=512, dims=(1, 1), name="ssm_gh_re")
    g_h_i = _mm_plain(d_y, c_out_ni, tm=s, tn=512, dims=(1, 1), name="ssm_gh_im")
    d_c_r = _mm_plain(h_r, d_y, tm=512, tn=SSM_WIDTH, dims=(0, 0), name="ssm_dc_re")
    d_c_ni = _mm_plain(h_i, d_y, tm=512, tn=SSM_WIDTH, dims=(0, 0), name="ssm_dc_im")
    lam_r, lam_i, d_abar_r, d_abar_i = _scan_bwd(g_h_r, g_h_i, h_r, h_i, e_r, e_i, a_r, a_i)
    d_bin_r = _mm_plain(u_p, lam_r, tm=SSM_WIDTH, tn=512, dims=(0, 0), name="ssm_db_re")
    d_bin_i = _mm_plain(u_p, lam_i, tm=SSM_WIDTH, tn=512, dims=(0, 0), name="ssm_db_im")
    d_u_r = _mm_plain(lam_r, b_in_r, tm=s, tn=SSM_WIDTH, tk=512, dims=(1, 1), name="ssm_du_re")
    d_u_i = _mm_plain(lam_i, b_in_i, tm=s, tn=SSM_WIDTH, tk=512, dims=(1, 1), name="ssm_du_im")

    def add3(a, b, c):
        return a + b + c

    (d_u_p,) = _rowwise(add3, [d_u_skip, d_u_r, d_u_i], [], [_sds((s, SSM_WIDTH), BF16)], tm=512, name="ssm_du")
    d_u = _from_scan_rows(d_u_p)
    d_bbar_r = _diag_blocks(d_bin_r, SSM_GROUP, SSM_STATE).transpose(0, 2, 1)
    d_bbar_i = _diag_blocks(d_bin_i, SSM_GROUP, SSM_STATE).transpose(0, 2, 1)
    d_a_re, d_a_im, d_log_dt, d_b_re, d_b_im = ssm_vjp(
        (d_abar_r.reshape(SSM_GROUPS, SSM_STATE), d_abar_i.reshape(SSM_GROUPS, SSM_STATE), d_bbar_r, d_bbar_i))
    d_c_re = _diag_blocks(d_c_r, SSM_STATE, SSM_GROUP).transpose(0, 2, 1)
    d_c_im = -_diag_blocks(d_c_ni, SSM_STATE, SSM_GROUP).transpose(0, 2, 1)

    g_w_attn_br = _mm_cols_tn(attn, d_y_attn, ns=D_MODEL // N_CHIPS, name="g_w_attn_br")
    d_attn = _mm_cols_nt(d_y_attn, wts["w_attn_br"], tm=s, name="d_attn")
    dqkv = []
    for g, d in enumerate(DILATIONS):
        n, c = s // d, d * ATTN_WIDTH
        dqkv.append(_attn_bwd(qg[g], kg[g], vg[g], d_attn.reshape(n, c), attn.reshape(n, c), lse.reshape(n, c),
                              name=f"attn_bwd_{g}"))

    def rope_bwd(dq0, dq1, dq2, dk0, dk1, dk2, dv0, dv1, dv2, cf, ss):
        cf4, ss4 = _tile_lanes(cf, QKV_WIDTH), _tile_lanes(ss, QKV_WIDTH)
        dq = jnp.concatenate([dq0, dq1, dq2], axis=1) * (1.0 / math.sqrt(HEAD_DIM))
        dk = jnp.concatenate([dk0, dk1, dk2], axis=1)
        dq = dq * cf4 + _swap_halves(dq * ss4)
        dk = dk * cf4 + _swap_halves(dk * ss4)
        return jnp.concatenate([dq, dk, dv0, dv1, dv2], axis=1)

    (d_qkv,) = _rowwise(rope_bwd, [dqkv[g][j].reshape(s, ATTN_WIDTH) for j in range(3) for g in range(3)]
                        + [cos_f, sin_s], [], [_sds((s, 3 * QKV_WIDTH), BF16)], tm=256, name="rope_bwd")

    d_proj = jnp.concatenate([d_qkv, d_u, d_gl], axis=1)
    g_w_in = _mm_cols_tn(x, d_proj, ns=IN_WIDTH // N_CHIPS, name="g_w_in")
    dx_proj = _mm_cols_nt(d_proj, wts["w_in"], tm=1024, name="dx_proj")

    def dx_sum(dz, dxp):
        return DN_ALPHA * dz + dxp

    (grad_x,) = _rowwise(dx_sum, [dz1, dx_proj], [], [_sds((s, D_MODEL))], tm=512, name="grad_x")

    big = {"w_in": g_w_in, "w_attn_br": g_w_attn_br, "w_ssm_br": g_w_ssm_br, "w_out": g_w_out, "w_glu": g_w_glu,
           "w_ff_gate": g_w_ff_gate, "w_ff_up": g_w_ff_up, "w_ff_down": g_w_ff_down}
    small_g = {"b_gate": jnp.concatenate([d_bg0, d_bg1], axis=0), "ssm_a_re": d_a_re, "ssm_a_im": d_a_im,
               "ssm_log_dt": d_log_dt, "ssm_b_re": d_b_re, "ssm_b_im": d_b_im, "ssm_c_re": d_c_re, "ssm_c_im": d_c_im,
               "ssm_d": d_ssm_d.reshape(SSM_WIDTH), "ln1_g": d_ln1_g, "ln1_b": d_ln1_b, "ln2_g": d_ln2_g,
               "ln2_b": d_ln2_b}
    return loss_v[0, 0], grad_x, big, small_g


ANY = pl.BlockSpec(memory_space=pl.ANY)


def _place():
    return lax.axis_index("x"), lax.axis_index("y"), lax.axis_index("c")


def _other_chips(x, y):
    return [(1 - x, y), (x, 1 - y), (1 - x, 1 - y)]


def _gather_weights(shards):
    nw = len(shards)

    def body(*refs):
        ins, outs = refs[:nw], refs[nw:2 * nw]
        send_sems, recv_sems, pass_send, pass_recv, local_sems = refs[2 * nw:]
        x, y, c = _place()
        chip = 2 * x + y
        chips = _other_chips(x, y)
        started = []
        for w in range(nw):
            hw = shards[w].shape[0] // 2
            mine = pl.ds(c * hw, hw)
            own = pltpu.make_async_copy(ins[w], outs[w].at[chip], local_sems.at[w])
            own.start()
            started.append(own)
            for j, (cx, cy) in enumerate(chips):
                cp = pltpu.make_async_remote_copy(
                    src_ref=ins[w].at[mine], dst_ref=outs[w].at[chip, mine], send_sem=send_sems.at[w, j],
                    recv_sem=recv_sems.at[w, j], device_id=(cx, cy, c), device_id_type=MESH)
                cp.start()
                started.append(cp)
        passed = []
        for w in range(nw):
            hw = shards[w].shape[0] // 2
            mine = pl.ds(c * hw, hw)
            for j, (cx, cy) in enumerate(chips):
                landed = outs[w].at[2 * cx + cy, mine]
                pltpu.make_async_remote_copy(
                    src_ref=ins[w].at[mine], dst_ref=landed, send_sem=send_sems.at[w, j],
                    recv_sem=recv_sems.at[w, j], device_id=(cx, cy, c), device_id_type=MESH).wait_recv()
                cp = pltpu.make_async_remote_copy(
                    src_ref=landed, dst_ref=landed, send_sem=pass_send.at[w, j], recv_sem=pass_recv.at[w, j],
                    device_id=(x, y, 1 - c), device_id_type=MESH)
                cp.start()
                passed.append(cp)
        for w in range(nw):
            hw = shards[w].shape[0] // 2
            theirs = pl.ds((1 - c) * hw, hw)
            for j, (cx, cy) in enumerate(chips):
                landed = outs[w].at[2 * cx + cy, theirs]
                pltpu.make_async_remote_copy(
                    src_ref=landed, dst_ref=landed, send_sem=pass_send.at[w, j], recv_sem=pass_recv.at[w, j],
                    device_id=(x, y, 1 - c), device_id_type=MESH).wait_recv()
        for cp in started[0::4]:
            cp.wait()
        for cp in [s for i, s in enumerate(started) if i % 4] + passed:
            cp.wait_send()

    sem = pltpu.SemaphoreType.DMA
    return pl.pallas_call(
        body, in_specs=[ANY] * nw, out_specs=[ANY] * nw,
        out_shape=[_sds((N_CHIPS,) + a.shape, a.dtype) for a in shards],
        scratch_shapes=[sem((nw, 3)), sem((nw, 3)), sem((nw, 3)), sem((nw, 3)), sem((nw,))],
        name="gather_weights")(*shards)


def _swap_other_halves(grads):
    nw = len(grads)

    def body(*refs):
        ins, outs = refs[:nw], refs[nw:2 * nw]
        send_sems, recv_sems = refs[2 * nw:]
        x, y, c = _place()
        cps = []
        for w in range(nw):
            hw = grads[w].shape[1] // 2
            cp = pltpu.make_async_remote_copy(
                src_ref=ins[w].at[:, pl.ds((1 - c) * hw, hw)], dst_ref=outs[w], send_sem=send_sems.at[w],
                recv_sem=recv_sems.at[w], device_id=(x, y, 1 - c), device_id_type=MESH)
            cp.start()
            cps.append(cp)
        for cp in cps:
            cp.wait()

    sem = pltpu.SemaphoreType.DMA
    return pl.pallas_call(
        body, in_specs=[ANY] * nw, out_specs=[ANY] * nw,
        out_shape=[_sds((N_CHIPS, g.shape[1] // 2, g.shape[2]), g.dtype) for g in grads],
        scratch_shapes=[sem((nw,)), sem((nw,))], name="swap_other_halves")(*grads)


def _add_my_half(core, g, other):
    n, r, cols = g.shape
    hw = r // 2

    def body(core_ref, g_ref, o_ref, out_ref):
        out_ref[...] = (g_ref[...].astype(F32) + o_ref[...].astype(F32)).astype(out_ref.dtype)

    return pl.pallas_call(
        body,
        grid_spec=pltpu.PrefetchScalarGridSpec(
            num_scalar_prefetch=1, grid=(n,),
            in_specs=[pl.BlockSpec((None, None, hw, cols), lambda s, core_ref: (s, core_ref[0], 0, 0)),
                      pl.BlockSpec((None, hw, cols), lambda s, core_ref: (s, 0, 0))],
            out_specs=pl.BlockSpec((None, hw, cols), lambda s, core_ref: (s, 0, 0))),
        out_shape=_sds((n, hw, cols), BF16), compiler_params=_cp(("parallel",)),
        name="add_my_half")(core, g.reshape(n, 2, hw, cols), other)


def _scatter_partials(parts):
    nw = len(parts)

    def body(*refs):
        ins, outs = refs[:nw], refs[nw:2 * nw]
        send_sems, recv_sems = refs[2 * nw:]
        x, y, c = _place()
        cps = []
        for w in range(nw):
            for j, (cx, cy) in enumerate(_other_chips(x, y)):
                cp = pltpu.make_async_remote_copy(
                    src_ref=ins[w].at[2 * cx + cy], dst_ref=outs[w].at[j], send_sem=send_sems.at[w, j],
                    recv_sem=recv_sems.at[w, j], device_id=(cx, cy, c), device_id_type=MESH)
                cp.start()
                cps.append(cp)
        for cp in cps:
            cp.wait()

    sem = pltpu.SemaphoreType.DMA
    return pl.pallas_call(
        body, in_specs=[ANY] * nw, out_specs=[ANY] * nw,
        out_shape=[_sds((3,) + p.shape[1:], p.dtype) for p in parts],
        scratch_shapes=[sem((nw, 3)), sem((nw, 3))], name="scatter_partials")(*parts)


def _sum_partials(chip, part, recv):
    _, hw, cols = part.shape
    th = hw // 2 if hw % 32 == 0 else hw

    def body(chip_ref, p_ref, r_ref, out_ref):
        acc = p_ref[...].astype(F32)
        for j in range(3):
            acc = acc + r_ref[j].astype(F32)
        out_ref[...] = acc

    return pl.pallas_call(
        body,
        grid_spec=pltpu.PrefetchScalarGridSpec(
            num_scalar_prefetch=1, grid=(hw // th,),
            in_specs=[pl.BlockSpec((None, th, cols), lambda i, chip_ref: (chip_ref[0], i, 0)),
                      pl.BlockSpec((3, th, cols), lambda i, chip_ref: (0, i, 0))],
            out_specs=pl.BlockSpec((th, cols), lambda i, chip_ref: (i, 0))),
        out_shape=_sds((hw, cols)), compiler_params=_cp(("parallel",)), name="sum_partials")(chip, part, recv)


def _join_halves(halves):
    nw = len(halves)

    def body(*refs):
        ins, outs = refs[:nw], refs[nw:2 * nw]
        send_sems, recv_sems, local_sems = refs[2 * nw:]
        x, y, c = _place()
        cps, own = [], []
        for w in range(nw):
            hw = halves[w].shape[0]
            mine = outs[w].at[pl.ds(c * hw, hw)]
            lc = pltpu.make_async_copy(ins[w], mine, local_sems.at[w])
            lc.start()
            own.append(lc)
            cp = pltpu.make_async_remote_copy(
                src_ref=ins[w], dst_ref=mine, send_sem=send_sems.at[w], recv_sem=recv_sems.at[w],
                device_id=(x, y, 1 - c), device_id_type=MESH)
            cp.start()
            cps.append(cp)
        for w in range(nw):
            hw = halves[w].shape[0]
            theirs = outs[w].at[pl.ds((1 - c) * hw, hw)]
            pltpu.make_async_remote_copy(
                src_ref=ins[w], dst_ref=theirs, send_sem=send_sems.at[w], recv_sem=recv_sems.at[w],
                device_id=(x, y, 1 - c), device_id_type=MESH).wait_recv()
        for cp in cps:
            cp.wait_send()
        for lc in own:
            lc.wait()

    sem = pltpu.SemaphoreType.DMA
    return pl.pallas_call(
        body, in_specs=[ANY] * nw, out_specs=[ANY] * nw,
        out_shape=[_sds((2 * h.shape[0], h.shape[1]), h.dtype) for h in halves],
        scratch_shapes=[sem((nw,)), sem((nw,)), sem((nw,))], name="join_halves")(*halves)


def _allreduce_rows(vec, *, name):
    rows = vec.shape[0]

    def body(v_ref, out_ref, slots, send_sems, recv_sems):
        x, y, c = _place()
        me = 4 * x + 2 * y + c
        slots[me] = v_ref[...]
        peers = []
        for mask in range(1, N_DEV):
            px = 1 - x if mask & 4 else x
            py = 1 - y if mask & 2 else y
            pc = 1 - c if mask & 1 else c
            peers.append((px, py, pc))
        cps = []
        for k, peer in enumerate(peers):
            cp = pltpu.make_async_remote_copy(
                src_ref=v_ref, dst_ref=slots.at[me], send_sem=send_sems.at[k], recv_sem=recv_sems.at[k],
                device_id=peer, device_id_type=MESH)
            cp.start()
            cps.append(cp)
        for k, (px, py, pc) in enumerate(peers):
            pltpu.make_async_remote_copy(
                src_ref=v_ref, dst_ref=slots.at[4 * px + 2 * py + pc], send_sem=send_sems.at[k],
                recv_sem=recv_sems.at[k], device_id=(px, py, pc), device_id_type=MESH).wait_recv()
        for cp in cps:
            cp.wait_send()
        acc = slots[0]
        for d in range(1, N_DEV):
            acc = acc + slots[d]
        out_ref[...] = acc

    vmem = pl.BlockSpec(memory_space=pltpu.VMEM)
    return pl.pallas_call(
        body, in_specs=[vmem], out_specs=vmem, out_shape=_sds((rows, 128)),
        scratch_shapes=[pltpu.VMEM((N_DEV, rows, 128), F32), pltpu.SemaphoreType.DMA((N_DEV - 1,)),
                        pltpu.SemaphoreType.DMA((N_DEV - 1,))],
        compiler_params=pltpu.CompilerParams(vmem_limit_bytes=VMEM_LIMIT_BYTES), name=name)(vec)


def _reduce_scatter(grads, core, chip):
    others = _swap_other_halves(grads)
    parts = [_add_my_half(core, g, o) for g, o in zip(grads, others)]
    recvd = _scatter_partials(parts)
    halves = [_sum_partials(chip, p, r) for p, r in zip(parts, recvd)]
    return _join_halves(halves)


def _adamw(w, g, m, v, *, name):
    rows, cols = w.shape
    tm = rows
    while tm * cols > 128 * 1024 * 2 and tm % 16 == 0:
        tm //= 2

    def step(wv, gv, mv, vv):
        m2 = ADAM_B1 * mv + (1.0 - ADAM_B1) * gv
        v2 = ADAM_B2 * vv + (1.0 - ADAM_B2) * (gv * gv)
        m_hat = m2 / (1.0 - ADAM_B1 ** ADAM_STEP)
        v_hat = v2 / (1.0 - ADAM_B2 ** ADAM_STEP)
        return -ADAM_LR * (m_hat / (jnp.sqrt(v_hat) + ADAM_EPS) + ADAM_WD * wv), m2, v2

    return _rowwise(step, [w, g, m, v], [], [_sds((rows, cols))] * 3, tm=tm, name=name)


def _pack_rows(arrs):
    flat = jnp.concatenate([a.reshape(-1).astype(F32) for a in arrs])
    rows = -(-flat.shape[0] // 1024) * 8
    return jnp.pad(flat, (0, rows * 128 - flat.shape[0])).reshape(rows, 128)


def _unpack_rows(vec, shapes):
    flat = vec.reshape(-1)
    out, off = [], 0
    for shp in shapes:
        size = math.prod(shp)
        out.append(flat[off:off + size].reshape(shp))
        off += size
    return out


BIG = ("w_in", "w_attn_br", "w_ssm_br", "w_out", "w_glu", "w_ff_gate", "w_ff_up", "w_ff_down")
SMALL = ("b_gate", "ssm_a_re", "ssm_a_im", "ssm_log_dt", "ssm_b_re", "ssm_b_im", "ssm_c_re", "ssm_c_im", "ssm_d",
         "ln1_g", "ln1_b", "ln2_g", "ln2_b")
WEIGHTS = ("w_in", "b_gate", "w_attn_br", "w_ssm_br", "w_out", "ssm_a_re", "ssm_a_im", "ssm_log_dt", "ssm_b_re",
           "ssm_b_im", "ssm_c_re", "ssm_c_im", "ssm_d", "w_glu", "ln1_g", "ln1_b", "w_ff_gate", "w_ff_up", "w_ff_down",
           "ln2_g", "ln2_b")


def kernel(x, w_in, b_gate, w_attn_br, w_ssm_br, w_out, ssm_a_re, ssm_a_im, ssm_log_dt, ssm_b_re, ssm_b_im, ssm_c_re, ssm_c_im, ssm_d, w_glu, ln1_g, ln1_b, w_ff_gate, w_ff_up, w_ff_down, ln2_g, ln2_b, loss_target, m_w_in, m_b_gate, m_w_attn_br, m_w_ssm_br, m_w_out, m_ssm_a_re, m_ssm_a_im, m_ssm_log_dt, m_ssm_b_re, m_ssm_b_im, m_ssm_c_re, m_ssm_c_im, m_ssm_d, m_w_glu, m_ln1_g, m_ln1_b, m_w_ff_gate, m_w_ff_up, m_w_ff_down, m_ln2_g, m_ln2_b, v_w_in, v_b_gate, v_w_attn_br, v_w_ssm_br, v_w_out, v_ssm_a_re, v_ssm_a_im, v_ssm_log_dt, v_ssm_b_re, v_ssm_b_im, v_ssm_c_re, v_ssm_c_im, v_ssm_d, v_w_glu, v_ln1_g, v_ln1_b, v_w_ff_gate, v_w_ff_up, v_w_ff_down, v_ln2_g, v_ln2_b):
    given = dict(locals())
    px, py, pc = _place()
    chip = 2 * px + py
    core_s = jnp.reshape(pc, (1,)).astype(jnp.int32)
    chip_s = jnp.reshape(chip, (1,)).astype(jnp.int32)

    gathered = _gather_weights([given[n][0].astype(BF16) for n in BIG])
    wts = dict(zip(BIG, gathered))
    ncol = D_MODEL // N_CHIPS
    bg_mine = jnp.where(pc == 0, b_gate[0], jnp.zeros_like(b_gate[0]))
    bg_full = lax.dynamic_update_slice(jnp.zeros((2, D_MODEL), F32), bg_mine, (0, chip * ncol))
    bg_full = _allreduce_rows(bg_full.reshape(16, 128), name="gather_gate_bias").reshape(2, D_MODEL)
    small = {n: given[n][0] for n in SMALL if n.startswith("ssm")}
    small.update({n: given[n] for n in ("ln1_g", "ln1_b", "ln2_g", "ln2_b")})
    small["b_gate"] = bg_full

    loss_mine, grad_x, big_g, small_g = _local_step(x[0], loss_target[0], wts, small)
    loss = lax.psum(loss_mine, ("x", "y", "c"))

    reduced = dict(zip(BIG, _reduce_scatter([big_g[n] for n in BIG], core_s, chip_s)))
    shapes = [small_g[n].shape for n in SMALL]
    summed = _unpack_rows(_allreduce_rows(_pack_rows([small_g[n] for n in SMALL]), name="allreduce_small"), shapes)
    grads = {n: reduced[n].reshape(given[n].shape) for n in BIG}
    for n, g in zip(SMALL, summed):
        if n == "b_gate":
            g = lax.dynamic_slice(g, (0, chip * ncol), (2, ncol))
        grads[n] = g.reshape(given[n].shape)

    delta, new_m, new_v = {}, {}, {}
    for n in BIG:
        shp = given[n].shape
        two = (shp[1], shp[2])
        d, m2, v2 = _adamw(given[n].reshape(two), grads[n].reshape(two), given["m_" + n].reshape(two),
                           given["v_" + n].reshape(two), name="adamw_" + n)
        delta[n], new_m[n], new_v[n] = d.reshape(shp), m2.reshape(shp), v2.reshape(shp)
    shapes = [given[n].shape for n in SMALL]
    packed = [_pack_rows([src[n] for n in SMALL]) for src in
              (given, grads, {n: given["m_" + n] for n in SMALL}, {n: given["v_" + n] for n in SMALL})]
    for out, vec in zip((delta, new_m, new_v), _adamw(*packed, name="adamw_small")):
        out.update(zip(SMALL, _unpack_rows(vec, shapes)))

    return (loss, grad_x.reshape(x.shape), *[grads[n] for n in WEIGHTS], *[delta[n] for n in WEIGHTS],
            *[new_m[n] for n in WEIGHTS], *[new_v[n] for n in WEIGHTS])
```

```python
import math

import jax
import jax.numpy as jnp
from jax import lax
from jax.experimental import pallas as pl
from jax.experimental.pallas import tpu as pltpu
from jax.experimental.pallas import tpu_sc as plsc

F32 = jnp.float32
BF16 = jnp.bfloat16
MESH = pl.DeviceIdType.MESH

D_MODEL = 1024
SEQ = 2048
HEAD_DIM = 64
ATTN_HEADS = 8
DILATIONS = (1, 4, 16)
ATTN_WIDTH = ATTN_HEADS * HEAD_DIM
QKV_WIDTH = 3 * ATTN_WIDTH
BLOCK = 128
ROPE_THETA = 10000.0
NEG_INF = -1e30
SSM_GROUP = 16
SSM_GROUPS = 32
SSM_WIDTH = 512
SSM_STATE = 64
SSM_LANES = SSM_GROUPS * SSM_STATE
SCAN_CHUNKS = 8
SCAN_STEPS = SEQ // SCAN_CHUNKS
SCAN_LANES = 256
IN_WIDTH = 3 * QKV_WIDTH + SSM_WIDTH + 2 * D_MODEL
D_FF = 2816
N_CHIPS = 4
N_DEV = 8
DN_ALPHA = 2.0 ** 0.25
LN_EPS = 1e-5
ADAM_LR = 0.001
ADAM_B1 = 0.9
ADAM_B2 = 0.999
ADAM_EPS = 1e-08
ADAM_WD = 0.01
ADAM_STEP = 10
GELU_C = math.sqrt(2.0 / math.pi)
GELU_K = 0.044715

VMEM_LIMIT_BYTES = 56 * 1024 * 1024


def _sds(shape, dtype=F32):
    return jax.ShapeDtypeStruct(tuple(shape), dtype)


def _cp(semantics=None):
    return pltpu.CompilerParams(dimension_semantics=semantics, vmem_limit_bytes=VMEM_LIMIT_BYTES)


def _matmul(a, b, *, grid, a_spec, b_spec, o_spec, out_shape, dims, k_axis=None, name):
    nk = grid[k_axis] if k_axis is not None else 1
    o_block = tuple(d for d in o_spec.block_shape if d is not None)

    def body(a_ref, b_ref, o_ref, *acc):
        part = lax.dot_general(a_ref[...].astype(BF16), b_ref[...].astype(BF16),
                               (((dims[0],), (dims[1],)), ((), ())), preferred_element_type=F32)
        if k_axis is None:
            o_ref[...] = part.astype(o_ref.dtype)
        else:
            k = pl.program_id(k_axis)

            @pl.when(k == 0)
            def _():
                acc[0][...] = part

            @pl.when(k > 0)
            def _():
                acc[0][...] += part

            @pl.when(k == nk - 1)
            def _():
                o_ref[...] = acc[0][...].astype(o_ref.dtype)

    sem = tuple("arbitrary" if ax == k_axis else "parallel" for ax in range(len(grid)))
    return pl.pallas_call(
        body, grid=grid, in_specs=[a_spec, b_spec], out_specs=o_spec, out_shape=out_shape,
        scratch_shapes=[pltpu.VMEM(o_block, F32)] if k_axis is not None else [],
        compiler_params=_cp(sem), name=name)(a, b)


def _mm_cols(a, wg, *, tm, name, out_dtype=F32, out3d=False):
    m, k = a.shape
    ns = wg.shape[2]
    if out3d:
        o_spec = pl.BlockSpec((None, tm, ns), lambda i, s: (s, i, 0))
        out_shape = _sds((N_CHIPS, m, ns), out_dtype)
    else:
        o_spec = pl.BlockSpec((tm, ns), lambda i, s: (i, s))
        out_shape = _sds((m, N_CHIPS * ns), out_dtype)
    return _matmul(a, wg, grid=(m // tm, N_CHIPS),
                   a_spec=pl.BlockSpec((tm, k), lambda i, s: (i, 0)),
                   b_spec=pl.BlockSpec((None, k, ns), lambda i, s: (s, 0, 0)),
                   o_spec=o_spec, out_shape=out_shape, dims=(1, 0), name=name)


def _mm_cols_nt(dy, wg, *, tm, name, dy3d=False, out_dtype=F32):
    k, ns = wg.shape[1], wg.shape[2]
    if dy3d:
        m = dy.shape[1]
        a_spec = pl.BlockSpec((None, tm, ns), lambda i, s: (s, i, 0))
    else:
        m = dy.shape[0]
        a_spec = pl.BlockSpec((tm, ns), lambda i, s: (i, s))
    return _matmul(dy, wg, grid=(m // tm, N_CHIPS), a_spec=a_spec,
                   b_spec=pl.BlockSpec((None, k, ns), lambda i, s: (s, 0, 0)),
                   o_spec=pl.BlockSpec((tm, k), lambda i, s: (i, 0)),
                   out_shape=_sds((m, k), out_dtype), dims=(1, 1), k_axis=1, name=name)


def _mm_cols_tn(a, dy, *, ns, name, dy3d=False):
    m, k = a.shape
    if dy3d:
        b_spec = pl.BlockSpec((None, m, ns), lambda s: (s, 0, 0))
    else:
        b_spec = pl.BlockSpec((m, ns), lambda s: (0, s))
    return _matmul(a, dy, grid=(N_CHIPS,), a_spec=pl.BlockSpec((m, k), lambda s: (0, 0)), b_spec=b_spec,
                   o_spec=pl.BlockSpec((None, k, ns), lambda s: (s, 0, 0)),
                   out_shape=_sds((N_CHIPS, k, ns), BF16), dims=(0, 0), name=name)


def _mm_plain(a, b, *, tm, tn, name, out_dtype=F32, dims=(1, 0), tk=None):
    m = a.shape[1 - dims[0]]
    kk = a.shape[dims[0]]
    n = b.shape[1 - dims[1]]
    tk = kk if tk is None else tk
    nk = kk // tk

    def a_idx(i, j, k):
        return (i, k) if dims[0] == 1 else (k, i)

    def b_idx(i, j, k):
        return (k, j) if dims[1] == 0 else (j, k)

    a_blk = (tm, tk) if dims[0] == 1 else (tk, tm)
    b_blk = (tk, tn) if dims[1] == 0 else (tn, tk)
    return _matmul(a, b, grid=(m // tm, n // tn, nk),
                   a_spec=pl.BlockSpec(a_blk, a_idx), b_spec=pl.BlockSpec(b_blk, b_idx),
                   o_spec=pl.BlockSpec((tm, tn), lambda i, j, k: (i, j)),
                   out_shape=_sds((m, n), out_dtype), dims=dims, k_axis=2 if nk > 1 else None, name=name)


def _rowwise(fn, tiled, full, outs, accs=(), *, tm, name):
    args, in_specs = [], []
    for t in tiled:
        if isinstance(t, tuple):
            arr, w, cb = t
            in_specs.append(pl.BlockSpec((tm, w), lambda i, cb=cb: (i, cb)))
        else:
            arr = t
            in_specs.append(pl.BlockSpec((tm, arr.shape[1]), lambda i: (i, 0)))
        args.append(arr)
    rows = args[0].shape[0]
    for f in full:
        in_specs.append(pl.BlockSpec(f.shape, lambda i, nd=f.ndim: (0,) * nd))
        args.append(f)
    out_specs = [pl.BlockSpec((tm, o.shape[1]), lambda i: (i, 0)) for o in outs]
    out_specs += [pl.BlockSpec(a.shape, lambda i, nd=len(a.shape): (0,) * nd) for a in accs]
    n_in, n_out = len(args), len(outs)

    def body(*refs):
        res = fn(*[r[...] for r in refs[:n_in]])
        res = res if isinstance(res, (tuple, list)) else (res,)
        for r, v in zip(refs[n_in:n_in + n_out], res[:n_out]):
            r[...] = v.astype(r.dtype)
        i = pl.program_id(0)
        for r, v in zip(refs[n_in + n_out:], res[n_out:]):
            @pl.when(i == 0)
            def _(r=r, v=v):
                r[...] = v

            @pl.when(i > 0)
            def _(r=r, v=v):
                r[...] += v

    res = pl.pallas_call(
        body, grid=(rows // tm,), in_specs=in_specs, out_specs=out_specs, out_shape=list(outs) + list(accs),
        compiler_params=_cp(("arbitrary",) if accs else ("parallel",)), name=name)(*args)
    return res


def _colsum(v):
    return jnp.sum(v, axis=0, keepdims=True)


def _ln_stats(z):
    mu = jnp.mean(z, axis=-1, keepdims=True)
    zc = z - mu
    var = jnp.mean(zc * zc, axis=-1, keepdims=True)
    rstd = lax.rsqrt(var + LN_EPS)
    return zc * rstd, rstd


def _ln_bwd(dy, xhat, rstd, g):
    dxh = dy * g
    m1 = jnp.mean(dxh, axis=-1, keepdims=True)
    m2 = jnp.mean(dxh * xhat, axis=-1, keepdims=True)
    return rstd * (dxh - m1 - xhat * m2)


def _swap_halves(t):
    w = t.shape[-1]
    lane = lax.broadcasted_iota(jnp.int32, t.shape, t.ndim - 1)
    return jnp.where((lane % HEAD_DIM) < HEAD_DIM // 2, pltpu.roll(t, w - HEAD_DIM // 2, t.ndim - 1),
                     pltpu.roll(t, HEAD_DIM // 2, t.ndim - 1))


def _band_mask(i):
    row = lax.broadcasted_iota(jnp.int32, (BLOCK, 2 * BLOCK), 0)
    col = lax.broadcasted_iota(jnp.int32, (BLOCK, 2 * BLOCK), 1)
    dist = BLOCK + row - col
    return (dist >= 0) & (dist <= BLOCK) & ((col >= BLOCK) | (i > 0))


def _attn_fwd(q, k, v, *, name):
    n, c = q.shape
    cw = ATTN_WIDTH

    def body(q_ref, kp_ref, kc_ref, vp_ref, vc_ref, o_ref, lse_ref):
        valid = _band_mask(pl.program_id(1))
        lane = lax.broadcasted_iota(jnp.int32, (BLOCK, 2 * HEAD_DIM), 1)
        first = lane < HEAD_DIM
        for hp in range(cw // (2 * HEAD_DIM)):
            sl = slice(hp * 2 * HEAD_DIM, (hp + 1) * 2 * HEAD_DIM)
            qq = q_ref[:, sl]
            kk = jnp.concatenate([kp_ref[:, sl], kc_ref[:, sl]], axis=0)
            vv = jnp.concatenate([vp_ref[:, sl], vc_ref[:, sl]], axis=0)
            zero = jnp.zeros_like(qq)
            outs, lses = [], []
            for qh in (jnp.where(first, qq, zero), jnp.where(first, zero, qq)):
                s = lax.dot_general(qh, kk, (((1,), (1,)), ((), ())), preferred_element_type=F32)
                s = jnp.where(valid, s, NEG_INF)
                m = jnp.max(s, axis=1, keepdims=True)
                p = jnp.exp(s - m)
                l = jnp.sum(p, axis=1, keepdims=True)
                o = jnp.dot(p.astype(BF16), vv, preferred_element_type=F32)
                outs.append(o / l)
                lses.append(m + jnp.log(l))
            o_ref[:, sl] = jnp.where(first, outs[0], outs[1])
            lse_ref[:, sl] = jnp.where(first, lses[0], lses[1])

    cur = pl.BlockSpec((BLOCK, cw), lambda cb, i: (i, cb))
    prev = pl.BlockSpec((BLOCK, cw), lambda cb, i: (jnp.maximum(i - 1, 0), cb))
    return pl.pallas_call(
        body, grid=(c // cw, n // BLOCK), in_specs=[cur, prev, cur, prev, cur], out_specs=[cur, cur],
        out_shape=[_sds((n, c)), _sds((n, c))], compiler_params=_cp(("parallel", "parallel")), name=name)(q, k, k, v, v)


def _attn_bwd(q, k, v, do, o, lse, *, name):
    n, c = q.shape
    cw = ATTN_WIDTH

    def body(q_ref, kp_ref, kc_ref, vp_ref, vc_ref, do_ref, o_ref, lse_ref, dq_ref, dk_ref, dv_ref):
        i = pl.program_id(1)
        valid = _band_mask(i)
        lane = lax.broadcasted_iota(jnp.int32, (BLOCK, 2 * HEAD_DIM), 1)
        first = lane < HEAD_DIM
        first2 = lax.broadcasted_iota(jnp.int32, (2 * BLOCK, 2 * HEAD_DIM), 1) < HEAD_DIM

        @pl.when(i == 0)
        def _():
            dk_ref[...] = jnp.zeros_like(dk_ref)
            dv_ref[...] = jnp.zeros_like(dv_ref)

        for hp in range(cw // (2 * HEAD_DIM)):
            sl = slice(hp * 2 * HEAD_DIM, (hp + 1) * 2 * HEAD_DIM)
            qq = q_ref[:, sl]
            kk = jnp.concatenate([kp_ref[:, sl], kc_ref[:, sl]], axis=0)
            vv = jnp.concatenate([vp_ref[:, sl], vc_ref[:, sl]], axis=0)
            dof = do_ref[:, sl]
            dd = dof * o_ref[:, sl]
            lse2 = lse_ref[:, sl]
            dob = dof.astype(BF16)
            zq = jnp.zeros_like(qq)
            zd = jnp.zeros_like(dob)
            zf = jnp.zeros_like(dd)
            dqs, dks, dvs = [], [], []
            for h in range(2):
                sel = first if h == 0 else jnp.logical_not(first)
                qh = jnp.where(sel, qq, zq)
                doh = jnp.where(sel, dob, zd)
                delta = jnp.sum(jnp.where(sel, dd, zf), axis=1, keepdims=True)
                lse_h = lse2[:, h * HEAD_DIM:h * HEAD_DIM + 1]
                s = lax.dot_general(qh, kk, (((1,), (1,)), ((), ())), preferred_element_type=F32)
                p = jnp.where(valid, jnp.exp(s - lse_h), 0.0)
                dp = lax.dot_general(doh, vv, (((1,), (1,)), ((), ())), preferred_element_type=F32)
                ds = (p * (dp - delta)).astype(BF16)
                pb = p.astype(BF16)
                dqs.append(jnp.dot(ds, kk, preferred_element_type=F32))
                dks.append(lax.dot_general(ds, qq, (((0,), (0,)), ((), ())), preferred_element_type=F32))
                dvs.append(lax.dot_general(pb, dob, (((0,), (0,)), ((), ())), preferred_element_type=F32))
            dq_ref[:, sl] = jnp.where(first, dqs[0], dqs[1])
            dk2 = jnp.where(first2, dks[0], dks[1])
            dv2 = jnp.where(first2, dvs[0], dvs[1])
            r_cur = pl.ds(pl.multiple_of(i * BLOCK, BLOCK), BLOCK)
            dk_ref[r_cur, sl] += dk2[BLOCK:]
            dv_ref[r_cur, sl] += dv2[BLOCK:]

            @pl.when(i > 0)
            def _(dk2=dk2, dv2=dv2, sl=sl):
                r_prev = pl.ds(pl.multiple_of((i - 1) * BLOCK, BLOCK), BLOCK)
                dk_ref[r_prev, sl] += dk2[:BLOCK]
                dv_ref[r_prev, sl] += dv2[:BLOCK]

    cur = pl.BlockSpec((BLOCK, cw), lambda cb, i: (i, cb))
    prev = pl.BlockSpec((BLOCK, cw), lambda cb, i: (jnp.maximum(i - 1, 0), cb))
    whole = pl.BlockSpec((n, cw), lambda cb, i: (0, cb))
    return pl.pallas_call(
        body, grid=(c // cw, n // BLOCK), in_specs=[cur, prev, cur, prev, cur, cur, cur, cur],
        out_specs=[cur, whole, whole], out_shape=[_sds((n, c)), _sds((n, c)), _sds((n, c))],
        compiler_params=_cp(("parallel", "arbitrary")), name=name)(q, k, k, v, v, do, o, lse)


def _cmul(ar, ai, br, bi):
    return ar * br - ai * bi, ar * bi + ai * br


def _pow256(ar, ai):
    for _ in range(8):
        ar, ai = _cmul(ar, ai, ar, ai)
    return ar, ai


def _chunk_carries(first_r, first_i, pr, pi, reverse):
    rows = lax.broadcasted_iota(jnp.int32, first_r.shape, 0)
    out_r = jnp.zeros_like(first_r)
    out_i = jnp.zeros_like(first_i)
    hr = jnp.zeros_like(first_r[0:1])
    hi = jnp.zeros_like(hr)
    order = range(SCAN_CHUNKS - 1, -1, -1) if reverse else range(SCAN_CHUNKS)
    for c in order:
        out_r = jnp.where(rows == c, hr, out_r)
        out_i = jnp.where(rows == c, hi, out_i)
        tr, ti = _cmul(pr[0:1], pi[0:1], hr, hi)
        hr = first_r[c:c + 1] + tr
        hi = first_i[c:c + 1] + ti
    return out_r, out_i


def _tile(j):
    return pl.ds(pl.multiple_of(j * SCAN_CHUNKS, SCAN_CHUNKS), SCAN_CHUNKS)


def _scan_fwd(bur, bui, ar, ai):
    lb = SCAN_LANES

    def body(bur_ref, bui_ref, ar_ref, ai_ref, hr_ref, hi_ref, er_ref, ei_ref):
        a_r = jnp.broadcast_to(ar_ref[...], (SCAN_CHUNKS, lb))
        a_i = jnp.broadcast_to(ai_ref[...], (SCAN_CHUNKS, lb))

        def local(j, carry):
            tr, ti = _cmul(a_r, a_i, carry[0], carry[1])
            nr = tr + bur_ref[_tile(j), :]
            ni = ti + bui_ref[_tile(j), :]
            hr_ref[_tile(j), :] = nr
            hi_ref[_tile(j), :] = ni
            return nr, ni

        zero = jnp.zeros((SCAN_CHUNKS, lb), F32)
        last_r, last_i = lax.fori_loop(0, SCAN_STEPS, local, (zero, zero), unroll=4)
        pr, pi = _pow256(a_r, a_i)
        er, ei = _chunk_carries(last_r, last_i, pr, pi, reverse=False)
        er_ref[...] = er
        ei_ref[...] = ei

        def fix(j, carry):
            tr, ti = _cmul(carry[0], carry[1], er, ei)
            hr_ref[_tile(j), :] += tr
            hi_ref[_tile(j), :] += ti
            return _cmul(carry[0], carry[1], a_r, a_i)

        lax.fori_loop(0, SCAN_STEPS, fix, (a_r, a_i), unroll=4)

    big = pl.BlockSpec((SEQ, lb), lambda j: (0, j))
    vec = pl.BlockSpec((1, lb), lambda j: (0, j))
    ent = pl.BlockSpec((SCAN_CHUNKS, lb), lambda j: (0, j))
    return pl.pallas_call(
        body, grid=(SSM_LANES // lb,), in_specs=[big, big, vec, vec], out_specs=[big, big, ent, ent],
        out_shape=[_sds((SEQ, SSM_LANES)), _sds((SEQ, SSM_LANES)), _sds((SCAN_CHUNKS, SSM_LANES)),
                   _sds((SCAN_CHUNKS, SSM_LANES))],
        compiler_params=_cp(("parallel",)), name="ssm_scan_fwd")(bur, bui, ar, ai)


def _scan_bwd(gr, gi, hr, hi, er, ei, ar, ai):
    lb = SCAN_LANES

    def body(gr_ref, gi_ref, hr_ref, hi_ref, er_ref, ei_ref, ar_ref, ai_ref, lr_ref, li_ref, dar_ref, dai_ref):
        a_r = jnp.broadcast_to(ar_ref[...], (SCAN_CHUNKS, lb))
        a_i = -jnp.broadcast_to(ai_ref[...], (SCAN_CHUNKS, lb))

        def local(t, carry):
            j = SCAN_STEPS - 1 - t
            tr, ti = _cmul(a_r, a_i, carry[0], carry[1])
            nr = tr + gr_ref[_tile(j), :]
            ni = ti + gi_ref[_tile(j), :]
            lr_ref[_tile(j), :] = nr
            li_ref[_tile(j), :] = ni
            return nr, ni

        zero = jnp.zeros((SCAN_CHUNKS, lb), F32)
        first_r, first_i = lax.fori_loop(0, SCAN_STEPS, local, (zero, zero), unroll=4)
        pr, pi = _pow256(a_r, a_i)
        nxt_r, nxt_i = _chunk_carries(first_r, first_i, pr, pi, reverse=True)

        def accumulate(lam_r, lam_i, hp_r, hp_i, acc):
            return (acc[0] + lam_r * hp_r + lam_i * hp_i, acc[1] + lam_i * hp_r - lam_r * hp_i)

        def fix(t, carry):
            qr, qi, acc_r, acc_i = carry
            j = SCAN_STEPS - 1 - t
            tr, ti = _cmul(qr, qi, nxt_r, nxt_i)
            lam_r = lr_ref[_tile(j), :] + tr
            lam_i = li_ref[_tile(j), :] + ti
            lr_ref[_tile(j), :] = lam_r
            li_ref[_tile(j), :] = lam_i
            acc_r, acc_i = accumulate(lam_r, lam_i, hr_ref[_tile(j - 1), :], hi_ref[_tile(j - 1), :], (acc_r, acc_i))
            qr, qi = _cmul(qr, qi, a_r, a_i)
            return qr, qi, acc_r, acc_i

        qr, qi, acc_r, acc_i = lax.fori_loop(0, SCAN_STEPS - 1, fix, (a_r, a_i, zero, zero), unroll=4)
        tr, ti = _cmul(qr, qi, nxt_r, nxt_i)
        lam_r = lr_ref[_tile(0), :] + tr
        lam_i = li_ref[_tile(0), :] + ti
        lr_ref[_tile(0), :] = lam_r
        li_ref[_tile(0), :] = lam_i
        acc_r, acc_i = accumulate(lam_r, lam_i, er_ref[...], ei_ref[...], (acc_r, acc_i))
        dar_ref[...] = jnp.sum(acc_r, axis=0, keepdims=True)
        dai_ref[...] = jnp.sum(acc_i, axis=0, keepdims=True)

    big = pl.BlockSpec((SEQ, lb), lambda j: (0, j))
    vec = pl.BlockSpec((1, lb), lambda j: (0, j))
    ent = pl.BlockSpec((SCAN_CHUNKS, lb), lambda j: (0, j))
    return pl.pallas_call(
        body, grid=(SSM_LANES // lb,), in_specs=[big, big, big, big, ent, ent, vec, vec],
        out_specs=[big, big, vec, vec],
        out_shape=[_sds((SEQ, SSM_LANES)), _sds((SEQ, SSM_LANES)), _sds((1, SSM_LANES)), _sds((1, SSM_LANES))],
        compiler_params=_cp(("parallel",)), name="ssm_scan_bwd")(gr, gi, hr, hi, er, ei, ar, ai)


def _rope_tables():
    half = HEAD_DIM // 2
    inv_freq = ROPE_THETA ** (-jnp.arange(half, dtype=F32) / half)
    ang = jnp.arange(SEQ, dtype=F32)[:, None] * inv_freq[None, :]
    cos, sin = jnp.cos(ang), jnp.sin(ang)
    cos_f = jnp.concatenate([cos, cos, cos, cos], axis=1)
    sin_s = jnp.concatenate([-sin, sin, -sin, sin], axis=1)
    return cos_f, sin_s


def _tile_lanes(t, width):
    return jnp.concatenate([t] * (width // t.shape[1]), axis=1)


def _to_scan_rows(t):
    return t.reshape(SCAN_CHUNKS, SCAN_STEPS, t.shape[1]).transpose(1, 0, 2).reshape(t.shape)


def _from_scan_rows(t):
    return t.reshape(SCAN_STEPS, SCAN_CHUNKS, t.shape[1]).transpose(1, 0, 2).reshape(t.shape)


def _ssm_discretise(a_re, a_im, log_dt, b_re, b_im):
    lam = lax.complex(a_re, a_im)
    dt = jnp.exp(log_dt)[:, None]
    a_bar = jnp.exp(lam * dt)
    b_bar = ((a_bar - 1.0) / lam)[..., None] * lax.complex(b_re, b_im)
    return a_bar.real, a_bar.imag, b_bar.real, b_bar.imag


def _block_diag_in(b):
    eye = jnp.eye(SSM_GROUPS, dtype=b.dtype)
    return (eye[:, None, :, None] * b.transpose(0, 2, 1)[:, :, None, :]).reshape(SSM_WIDTH, SSM_LANES)


def _block_diag_out(c):
    eye = jnp.eye(SSM_GROUPS, dtype=c.dtype)
    return (eye[:, None, :, None] * c.transpose(0, 2, 1)[:, :, None, :]).reshape(SSM_LANES, SSM_WIDTH)


def _diag_blocks(m, rows, cols):
    m4 = m.reshape(SSM_GROUPS, rows, SSM_GROUPS, cols)
    return jnp.diagonal(m4, axis1=0, axis2=2).transpose(2, 0, 1)


def _local_step(x, tgt, wts, small):
    s = SEQ
    cos_f, sin_s = _rope_tables()

    proj = _mm_cols(x, wts["w_in"], tm=1024, name="proj")

    def rope_split(qr, kr, vr, cf, ss):
        cf4, ss4 = _tile_lanes(cf, QKV_WIDTH), _tile_lanes(ss, QKV_WIDTH)
        q = (qr * cf4 + _swap_halves(qr) * ss4) * (1.0 / math.sqrt(HEAD_DIM))
        k = kr * cf4 + _swap_halves(kr) * ss4
        res = []
        for t in (q, k, vr):
            res += [t[:, g * ATTN_WIDTH:(g + 1) * ATTN_WIDTH] for g in range(3)]
        return res

    qkv = _rowwise(rope_split, [(proj, QKV_WIDTH, 0), (proj, QKV_WIDTH, 1), (proj, QKV_WIDTH, 2), cos_f, sin_s], [],
                   [_sds((s, ATTN_WIDTH), BF16)] * 9, tm=256, name="rope_split")
    qg = [qkv[g].reshape(s // d, d * ATTN_WIDTH) for g, d in enumerate(DILATIONS)]
    kg = [qkv[3 + g].reshape(s // d, d * ATTN_WIDTH) for g, d in enumerate(DILATIONS)]
    vg = [qkv[6 + g].reshape(s // d, d * ATTN_WIDTH) for g, d in enumerate(DILATIONS)]

    o_l = [_attn_fwd(qg[g], kg[g], vg[g], name=f"attn_fwd_{g}") for g in range(3)]

    def merge(o0, o1, o2, l0, l1, l2):
        m = jnp.maximum(jnp.maximum(l0, l1), l2)
        lse = m + jnp.log(jnp.exp(l0 - m) + jnp.exp(l1 - m) + jnp.exp(l2 - m))
        return jnp.exp(l0 - lse) * o0 + jnp.exp(l1 - lse) * o1 + jnp.exp(l2 - lse) * o2, lse

    attn, lse = _rowwise(merge, [o_l[g][0].reshape(s, ATTN_WIDTH) for g in range(3)]
                         + [o_l[g][1].reshape(s, ATTN_WIDTH) for g in range(3)], [],
                         [_sds((s, ATTN_WIDTH)), _sds((s, ATTN_WIDTH))], tm=512, name="attn_merge")
    y_attn = _mm_cols(attn, wts["w_attn_br"], tm=s, name="y_attn")

    (abar_r, abar_i, bbar_r, bbar_i), ssm_vjp = jax.vjp(
        _ssm_discretise, small["ssm_a_re"], small["ssm_a_im"], small["ssm_log_dt"], small["ssm_b_re"], small["ssm_b_im"])
    b_in_r, b_in_i = _block_diag_in(bbar_r).astype(BF16), _block_diag_in(bbar_i).astype(BF16)
    c_out_r = _block_diag_out(small["ssm_c_re"]).astype(BF16)
    c_out_ni = _block_diag_out(-small["ssm_c_im"]).astype(BF16)
    a_r, a_i = abar_r.reshape(1, SSM_LANES), abar_i.reshape(1, SSM_LANES)
    d_skip = small["ssm_d"].reshape(1, SSM_WIDTH)

    u_f = _to_scan_rows(proj[:, 3 * QKV_WIDTH:3 * QKV_WIDTH + SSM_WIDTH])
    u_p = u_f.astype(BF16)
    bu_r = _mm_plain(u_p, b_in_r, tm=s, tn=512, name="ssm_bu_re")
    bu_i = _mm_plain(u_p, b_in_i, tm=s, tn=512, name="ssm_bu_im")
    h_r, h_i, e_r, e_i = _scan_fwd(bu_r, bu_i, a_r, a_i)
    y_1 = _mm_plain(h_r, c_out_r, tm=s, tn=512, tk=512, name="ssm_y_re")
    y_2 = _mm_plain(h_i, c_out_ni, tm=s, tn=512, tk=512, name="ssm_y_im")

    def gelu_fwd(y1, y2, u, dsk):
        y = y1 + y2 + dsk * u
        return y, 0.5 * y * (1.0 + jnp.tanh(GELU_C * (y + GELU_K * y * y * y)))

    y_s5, gel = _rowwise(gelu_fwd, [y_1, y_2, u_f], [d_skip], [_sds((s, SSM_WIDTH)), _sds((s, SSM_WIDTH), BF16)],
                         tm=512, name="ssm_gelu")
    glu = _mm_cols(gel, wts["w_glu"], tm=s, name="glu")

    def glu_fwd(ga, gb):
        return ga * jax.nn.sigmoid(gb)

    (y_glu,) = _rowwise(glu_fwd, [(glu, SSM_WIDTH, 0), (glu, SSM_WIDTH, 1)], [], [_sds((s, SSM_WIDTH), BF16)],
                        tm=512, name="glu_gate")
    y_ssm = _from_scan_rows(_mm_cols(y_glu, wts["w_ssm_br"], tm=s, name="y_ssm"))

    gl0 = (proj, D_MODEL, (3 * QKV_WIDTH + SSM_WIDTH) // D_MODEL)
    gl1 = (proj, D_MODEL, (3 * QKV_WIDTH + SSM_WIDTH) // D_MODEL + 1)
    b_gate = small["b_gate"]

    def gate_mix(l0, l1, ya, ys, bg):
        return jax.nn.sigmoid(l0 + bg[0:1]) * ya + jax.nn.sigmoid(l1 + bg[1:2]) * ys

    (mixed,) = _rowwise(gate_mix, [gl0, gl1, y_attn, y_ssm], [b_gate], [_sds((s, D_MODEL), BF16)], tm=256,
                        name="gate_mix")
    w_out = wts["w_out"].reshape(D_MODEL, D_MODEL)
    mix_out = _mm_plain(mixed, w_out, tm=1024, tn=512, name="mix_out")

    def ln1_fwd(xv, mo, g, b):
        z = DN_ALPHA * xv + mo
        xhat, _ = _ln_stats(z)
        return z, xhat * g + b

    z1, h = _rowwise(ln1_fwd, [x, mix_out], [small["ln1_g"], small["ln1_b"]],
                     [_sds((s, D_MODEL)), _sds((s, D_MODEL))], tm=256, name="ln1")

    ff_a = _mm_cols(h, wts["w_ff_gate"], tm=1024, name="ff_gate", out3d=True)
    ff_b = _mm_cols(h, wts["w_ff_up"], tm=1024, name="ff_up", out3d=True)
    nf = D_FF // N_CHIPS

    def swiglu_fwd(a, b):
        return a * jax.nn.sigmoid(a) * b

    (act,) = _rowwise(swiglu_fwd, [ff_a.reshape(N_CHIPS * s, nf), ff_b.reshape(N_CHIPS * s, nf)], [],
                      [_sds((N_CHIPS * s, nf), BF16)], tm=1024, name="swiglu")
    act = act.reshape(N_CHIPS, s, nf)
    w_down = wts["w_ff_down"]
    ff = _matmul(act, w_down, grid=(2, N_CHIPS),
                 a_spec=pl.BlockSpec((None, 1024, nf), lambda i, k: (k, i, 0)),
                 b_spec=pl.BlockSpec((None, nf, D_MODEL), lambda i, k: (k, 0, 0)),
                 o_spec=pl.BlockSpec((1024, D_MODEL), lambda i, k: (i, 0)),
                 out_shape=_sds((s, D_MODEL)), dims=(1, 0), k_axis=1, name="ff_down")

    def ln2_loss(hv, ffv, tg, g, b):
        z = DN_ALPHA * hv + ffv
        xhat, rstd = _ln_stats(z)
        err = xhat * g + b - tg
        d_out = err * (1.0 / D_MODEL)
        loss_rows = jnp.sum(err * err, axis=-1, keepdims=True) * (0.5 / D_MODEL)
        loss = jnp.broadcast_to(jnp.sum(loss_rows, axis=0, keepdims=True), (1, 128))
        return _ln_bwd(d_out, xhat, rstd, g), loss, _colsum(d_out * xhat), _colsum(d_out)

    dz2, loss_v, d_ln2_g, d_ln2_b = _rowwise(
        ln2_loss, [h, ff, tgt], [small["ln2_g"], small["ln2_b"]], [_sds((s, D_MODEL))],
        [_sds((1, 128)), _sds((1, D_MODEL)), _sds((1, D_MODEL))], tm=256, name="ln2_loss")

    d_act = _matmul(dz2, w_down, grid=(2, N_CHIPS),
                    a_spec=pl.BlockSpec((1024, D_MODEL), lambda i, k: (i, 0)),
                    b_spec=pl.BlockSpec((None, nf, D_MODEL), lambda i, k: (k, 0, 0)),
                    o_spec=pl.BlockSpec((None, 1024, nf), lambda i, k: (k, i, 0)),
                    out_shape=_sds((N_CHIPS, s, nf)), dims=(1, 1), name="d_act")
    g_w_ff_down = _matmul(act, dz2, grid=(N_CHIPS,),
                          a_spec=pl.BlockSpec((None, s, nf), lambda k: (k, 0, 0)),
                          b_spec=pl.BlockSpec((s, D_MODEL), lambda k: (0, 0)),
                          o_spec=pl.BlockSpec((None, nf, D_MODEL), lambda k: (k, 0, 0)),
                          out_shape=_sds((N_CHIPS, nf, D_MODEL), BF16), dims=(0, 0), name="g_w_ff_down")

    def swiglu_bwd(da, a, b):
        sg = jax.nn.sigmoid(a)
        return da * b * sg * (1.0 + a * (1.0 - sg)), da * a * sg

    d_a, d_b = _rowwise(swiglu_bwd, [d_act.reshape(N_CHIPS * s, nf), ff_a.reshape(N_CHIPS * s, nf),
                                     ff_b.reshape(N_CHIPS * s, nf)], [],
                        [_sds((N_CHIPS * s, nf), BF16)] * 2, tm=1024, name="swiglu_bwd")
    d_a, d_b = d_a.reshape(N_CHIPS, s, nf), d_b.reshape(N_CHIPS, s, nf)
    g_w_ff_gate = _mm_cols_tn(h, d_a, ns=nf, name="g_w_ff_gate", dy3d=True)
    g_w_ff_up = _mm_cols_tn(h, d_b, ns=nf, name="g_w_ff_up", dy3d=True)
    dh_a = _mm_cols_nt(d_a, wts["w_ff_gate"], tm=1024, name="dh_gate", dy3d=True)
    dh_b = _mm_cols_nt(d_b, wts["w_ff_up"], tm=1024, name="dh_up", dy3d=True)

    def ln1_bwd(dz, da, db, z, g):
        xhat, rstd = _ln_stats(z)
        dh = DN_ALPHA * dz + da + db
        return _ln_bwd(dh, xhat, rstd, g), _colsum(dh * xhat), _colsum(dh)

    dz1, d_ln1_g, d_ln1_b = _rowwise(ln1_bwd, [dz2, dh_a, dh_b, z1], [small["ln1_g"]], [_sds((s, D_MODEL))],
                                     [_sds((1, D_MODEL)), _sds((1, D_MODEL))], tm=256, name="ln1_bwd")
    d_mixed = _mm_plain(dz1, w_out, tm=1024, tn=512, dims=(1, 1), name="d_mixed")
    g_w_out = _mm_plain(mixed, dz1, tm=D_MODEL, tn=512, dims=(0, 0), out_dtype=BF16, name="g_w_out")
    g_w_out = g_w_out.reshape(N_CHIPS, D_MODEL // N_CHIPS, D_MODEL)

    def gate_bwd(dm, l0, l1, ya, ys, bg):
        g0 = jax.nn.sigmoid(l0 + bg[0:1])
        g1 = jax.nn.sigmoid(l1 + bg[1:2])
        dl0 = dm * ya * g0 * (1.0 - g0)
        dl1 = dm * ys * g1 * (1.0 - g1)
        return dm * g0, dm * g1, jnp.concatenate([dl0, dl1], axis=1), _colsum(dl0), _colsum(dl1)

    d_y_attn, d_y_ssm, d_gl, d_bg0, d_bg1 = _rowwise(
        gate_bwd, [d_mixed, gl0, gl1, y_attn, y_ssm], [b_gate],
        [_sds((s, D_MODEL), BF16), _sds((s, D_MODEL), BF16), _sds((s, 2 * D_MODEL), BF16)],
        [_sds((1, D_MODEL)), _sds((1, D_MODEL))], tm=256, name="gate_bwd")

    d_y_ssm_p = _to_scan_rows(d_y_ssm)
    g_w_ssm_br = _mm_cols_tn(y_glu, d_y_ssm_p, ns=D_MODEL // N_CHIPS, name="g_w_ssm_br")
    d_y_glu = _mm_cols_nt(d_y_ssm_p, wts["w_ssm_br"], tm=s, name="d_y_glu")

    def glu_bwd(dy, ga, gb):
        sg = jax.nn.sigmoid(gb)
        return jnp.concatenate([dy * sg, dy * ga * sg * (1.0 - sg)], axis=1)

    (d_glu,) = _rowwise(glu_bwd, [d_y_glu, (glu, SSM_WIDTH, 0), (glu, SSM_WIDTH, 1)], [],
                        [_sds((s, 2 * SSM_WIDTH), BF16)], tm=512, name="glu_bwd")
    g_w_glu = _mm_cols_tn(gel, d_glu, ns=2 * SSM_WIDTH // N_CHIPS, name="g_w_glu")
    d_gel = _mm_cols_nt(d_glu, wts["w_glu"], tm=s, name="d_gel")

    def gelu_bwd(dg, y, u, dsk):
        th = jnp.tanh(GELU_C * (y + GELU_K * y * y * y))
        dy = dg * (0.5 * (1.0 + th) + 0.5 * y * (1.0 - th * th) * GELU_C * (1.0 + 3.0 * GELU_K * y * y))
        return dy, dy * dsk, _colsum(dy * u)

    d_y, d_u_skip, d_ssm_d = _rowwise(gelu_bwd, [d_gel, y_s5, u_f], [d_skip],
                                      [_sds((s, SSM_WIDTH), BF16), _sds((s, SSM_WIDTH))], [_sds((1, SSM_WIDTH))],
                                      tm=512, name="gelu_bwd")
    g_h_r = _mm_plain(d_y, c_out_r, tm=s, tn=512, dims=(1, 1), name="ssm_gh_re")
    g_h_i = _mm_plain(d_y, c_out_ni, tm=s, tn=512, dims=(1, 1), name="ssm_gh_im")
    d_c_r = _mm_plain(h_r, d_y, tm=512, tn=SSM_WIDTH, dims=(0, 0), name="ssm_dc_re")
    d_c_ni = _mm_plain(h_i, d_y, tm=512, tn=SSM_WIDTH, dims=(0, 0), name="ssm_dc_im")
    lam_r, lam_i, d_abar_r, d_abar_i = _scan_bwd(g_h_r, g_h_i, h_r, h_i, e_r, e_i, a_r, a_i)
    d_bin_r = _mm_plain(u_p, lam_r, tm=SSM_WIDTH, tn=512, dims=(0, 0), name="ssm_db_re")
    d_bin_i = _mm_plain(u_p, lam_i, tm=SSM_WIDTH, tn=512, dims=(0, 0), name="ssm_db_im")
    d_u_r = _mm_plain(lam_r, b_in_r, tm=s, tn=SSM_WIDTH, tk=512, dims=(1, 1), name="ssm_du_re")
    d_u_i = _mm_plain(lam_i, b_in_i, tm=s, tn=SSM_WIDTH, tk=512, dims=(1, 1), name="ssm_du_im")

    def add3(a, b, c):
        return a + b + c

    (d_u_p,) = _rowwise(add3, [d_u_skip, d_u_r, d_u_i], [], [_sds((s, SSM_WIDTH), BF16)], tm=512, name="ssm_du")
    d_u = _from_scan_rows(d_u_p)
    d_bbar_r = _diag_blocks(d_bin_r, SSM_GROUP, SSM_STATE).transpose(0, 2, 1)
    d_bbar_i = _diag_blocks(d_bin_i, SSM_GROUP, SSM_STATE).transpose(0, 2, 1)
    d_a_re, d_a_im, d_log_dt, d_b_re, d_b_im = ssm_vjp(
        (d_abar_r.reshape(SSM_GROUPS, SSM_STATE), d_abar_i.reshape(SSM_GROUPS, SSM_STATE), d_bbar_r, d_bbar_i))
    d_c_re = _diag_blocks(d_c_r, SSM_STATE, SSM_GROUP).transpose(0, 2, 1)
    d_c_im = -_diag_blocks(d_c_ni, SSM_STATE, SSM_GROUP).transpose(0, 2, 1)

    g_w_attn_br = _mm_cols_tn(attn, d_y_attn, ns=D_MODEL // N_CHIPS, name="g_w_attn_br")
    d_attn = _mm_cols_nt(d_y_attn, wts["w_attn_br"], tm=s, name="d_attn")
    dqkv = []
    for g, d in enumerate(DILATIONS):
        n, c = s // d, d * ATTN_WIDTH
        dqkv.append(_attn_bwd(qg[g], kg[g], vg[g], d_attn.reshape(n, c), attn.reshape(n, c), lse.reshape(n, c),
                              name=f"attn_bwd_{g}"))

    def rope_bwd(dq0, dq1, dq2, dk0, dk1, dk2, dv0, dv1, dv2, cf, ss):
        cf4, ss4 = _tile_lanes(cf, QKV_WIDTH), _tile_lanes(ss, QKV_WIDTH)
        dq = jnp.concatenate([dq0, dq1, dq2], axis=1) * (1.0 / math.sqrt(HEAD_DIM))
        dk = jnp.concatenate([dk0, dk1, dk2], axis=1)
        dq = dq * cf4 + _swap_halves(dq * ss4)
        dk = dk * cf4 + _swap_halves(dk * ss4)
        return jnp.concatenate([dq, dk, dv0, dv1, dv2], axis=1)

    (d_qkv,) = _rowwise(rope_bwd, [dqkv[g][j].reshape(s, ATTN_WIDTH) for j in range(3) for g in range(3)]
                        + [cos_f, sin_s], [], [_sds((s, 3 * QKV_WIDTH), BF16)], tm=256, name="rope_bwd")

    d_proj = jnp.concatenate([d_qkv, d_u, d_gl], axis=1)
    g_w_in = _mm_cols_tn(x, d_proj, ns=IN_WIDTH // N_CHIPS, name="g_w_in")
    dx_proj = _mm_cols_nt(d_proj, wts["w_in"], tm=1024, name="dx_proj")

    def dx_sum(dz, dxp):
        return DN_ALPHA * dz + dxp

    (grad_x,) = _rowwise(dx_sum, [dz1, dx_proj], [], [_sds((s, D_MODEL))], tm=512, name="grad_x")

    big = {"w_in": g_w_in, "w_attn_br": g_w_attn_br, "w_ssm_br": g_w_ssm_br, "w_out": g_w_out, "w_glu": g_w_glu,
           "w_ff_gate": g_w_ff_gate, "w_ff_up": g_w_ff_up, "w_ff_down": g_w_ff_down}
    small_g = {"b_gate": jnp.concatenate([d_bg0, d_bg1], axis=0), "ssm_a_re": d_a_re, "ssm_a_im": d_a_im,
               "ssm_log_dt": d_log_dt, "ssm_b_re": d_b_re, "ssm_b_im": d_b_im, "ssm_c_re": d_c_re, "ssm_c_im": d_c_im,
               "ssm_d": d_ssm_d.reshape(SSM_WIDTH), "ln1_g": d_ln1_g, "ln1_b": d_ln1_b, "ln2_g": d_ln2_g,
               "ln2_b": d_ln2_b}
    return loss_v[0, 0], grad_x, big, small_g


GATHER_ID, SWAP_ID, SCATTER_ID, JOIN_ID = 1, 2, 3, 4


def _place():
    return lax.axis_index("x"), lax.axis_index("y"), lax.axis_index("c")


def _other_chips(x, y):
    return [(1 - x, y), (x, 1 - y), (1 - x, 1 - y)]


def _handshake(peers):
    barrier = pltpu.get_barrier_semaphore()
    for peer in peers:
        pl.semaphore_signal(barrier, inc=1, device_id=peer, device_id_type=MESH)
    pl.semaphore_wait(barrier, len(peers))


def _sequencer(body, arrays, out_type, sems, collective_id, name):
    return pl.kernel(body, name=name, out_type=out_type,
                     mesh=plsc.ScalarSubcoreMesh(axis_name="sequencer", num_cores=1), scratch_types=sems,
                     compiler_params=pltpu.CompilerParams(collective_id=collective_id))(*arrays)


def _gather_weights(shards, *, name):
    nw = len(shards)

    def body(*refs):
        ins, outs = refs[:nw], refs[nw:2 * nw]
        send_sems, recv_sems, pass_send, pass_recv, local_sems = refs[2 * nw:]
        x, y, c = _place()
        chip = 2 * x + y
        chips = _other_chips(x, y)
        _handshake([(x, y, 1 - c)] + [(cx, cy, c) for cx, cy in chips])
        started = []
        for w in range(nw):
            hw = shards[w].shape[0] // 2
            mine = pl.ds(c * hw, hw)
            own = pltpu.make_async_copy(ins[w], outs[w].at[chip], local_sems.at[w])
            own.start()
            started.append(own)
            for j, (cx, cy) in enumerate(chips):
                cp = pltpu.make_async_remote_copy(
                    src_ref=ins[w].at[mine], dst_ref=outs[w].at[chip, mine], send_sem=send_sems.at[w, j],
                    recv_sem=recv_sems.at[w, j], device_id=(cx, cy, c), device_id_type=MESH)
                cp.start()
                started.append(cp)
        passed = []
        for w in range(nw):
            hw = shards[w].shape[0] // 2
            mine = pl.ds(c * hw, hw)
            for j, (cx, cy) in enumerate(chips):
                landed = outs[w].at[2 * cx + cy, mine]
                pltpu.make_async_remote_copy(
                    src_ref=ins[w].at[mine], dst_ref=landed, send_sem=send_sems.at[w, j],
                    recv_sem=recv_sems.at[w, j], device_id=(cx, cy, c), device_id_type=MESH).wait_recv()
                cp = pltpu.make_async_remote_copy(
                    src_ref=landed, dst_ref=landed, send_sem=pass_send.at[w, j], recv_sem=pass_recv.at[w, j],
                    device_id=(x, y, 1 - c), device_id_type=MESH)
                cp.start()
                passed.append(cp)
        for w in range(nw):
            hw = shards[w].shape[0] // 2
            theirs = pl.ds((1 - c) * hw, hw)
            for j, (cx, cy) in enumerate(chips):
                landed = outs[w].at[2 * cx + cy, theirs]
                pltpu.make_async_remote_copy(
                    src_ref=landed, dst_ref=landed, send_sem=pass_send.at[w, j], recv_sem=pass_recv.at[w, j],
                    device_id=(x, y, 1 - c), device_id_type=MESH).wait_recv()
        for cp in started[0::4]:
            cp.wait()
        for cp in [s for i, s in enumerate(started) if i % 4] + passed:
            cp.wait_send()

    sem = pltpu.SemaphoreType.DMA
    return _sequencer(body, shards, [_sds((N_CHIPS,) + a.shape, a.dtype) for a in shards],
                      [sem((nw, 3)), sem((nw, 3)), sem((nw, 3)), sem((nw, 3)), sem((nw,))], GATHER_ID, name)


def _swap_other_halves(grads, *, name):
    nw = len(grads)

    def body(*refs):
        ins, outs = refs[:nw], refs[nw:2 * nw]
        send_sems, recv_sems = refs[2 * nw:]
        x, y, c = _place()
        _handshake([(x, y, 1 - c)])
        cps = []
        for w in range(nw):
            hw = grads[w].shape[1] // 2
            cp = pltpu.make_async_remote_copy(
                src_ref=ins[w].at[:, pl.ds((1 - c) * hw, hw)], dst_ref=outs[w], send_sem=send_sems.at[w],
                recv_sem=recv_sems.at[w], device_id=(x, y, 1 - c), device_id_type=MESH)
            cp.start()
            cps.append(cp)
        for cp in cps:
            cp.wait()

    sem = pltpu.SemaphoreType.DMA
    return _sequencer(body, grads, [_sds((N_CHIPS, g.shape[1] // 2, g.shape[2]), g.dtype) for g in grads],
                      [sem((nw,)), sem((nw,))], SWAP_ID, name)


def _add_my_half(core, g, other):
    n, r, cols = g.shape
    hw = r // 2

    def body(core_ref, g_ref, o_ref, out_ref):
        out_ref[...] = (g_ref[...].astype(F32) + o_ref[...].astype(F32)).astype(out_ref.dtype)

    return pl.pallas_call(
        body,
        grid_spec=pltpu.PrefetchScalarGridSpec(
            num_scalar_prefetch=1, grid=(n,),
            in_specs=[pl.BlockSpec((None, None, hw, cols), lambda s, core_ref: (s, core_ref[0], 0, 0)),
                      pl.BlockSpec((None, hw, cols), lambda s, core_ref: (s, 0, 0))],
            out_specs=pl.BlockSpec((None, hw, cols), lambda s, core_ref: (s, 0, 0))),
        out_shape=_sds((n, hw, cols), BF16), compiler_params=_cp(("parallel",)),
        name="add_my_half")(core, g.reshape(n, 2, hw, cols), other)


def _scatter_partials(parts, *, name):
    nw = len(parts)

    def body(*refs):
        ins, outs = refs[:nw], refs[nw:2 * nw]
        send_sems, recv_sems = refs[2 * nw:]
        x, y, c = _place()
        _handshake([(cx, cy, c) for cx, cy in _other_chips(x, y)])
        cps = []
        for w in range(nw):
            for j, (cx, cy) in enumerate(_other_chips(x, y)):
                cp = pltpu.make_async_remote_copy(
                    src_ref=ins[w].at[2 * cx + cy], dst_ref=outs[w].at[j], send_sem=send_sems.at[w, j],
                    recv_sem=recv_sems.at[w, j], device_id=(cx, cy, c), device_id_type=MESH)
                cp.start()
                cps.append(cp)
        for cp in cps:
            cp.wait()

    sem = pltpu.SemaphoreType.DMA
    return _sequencer(body, parts, [_sds((3,) + p.shape[1:], p.dtype) for p in parts],
                      [sem((nw, 3)), sem((nw, 3))], SCATTER_ID, name)


def _sum_partials(chip, part, recv):
    _, hw, cols = part.shape
    th = hw // 2 if hw % 32 == 0 else hw

    def body(chip_ref, p_ref, r_ref, out_ref):
        acc = p_ref[...].astype(F32)
        for j in range(3):
            acc = acc + r_ref[j].astype(F32)
        out_ref[...] = acc

    return pl.pallas_call(
        body,
        grid_spec=pltpu.PrefetchScalarGridSpec(
            num_scalar_prefetch=1, grid=(hw // th,),
            in_specs=[pl.BlockSpec((None, th, cols), lambda i, chip_ref: (chip_ref[0], i, 0)),
                      pl.BlockSpec((3, th, cols), lambda i, chip_ref: (0, i, 0))],
            out_specs=pl.BlockSpec((th, cols), lambda i, chip_ref: (i, 0))),
        out_shape=_sds((hw, cols)), compiler_params=_cp(("parallel",)), name="sum_partials")(chip, part, recv)


def _swap_reduced_halves(halves, *, name):
    nw = len(halves)

    def body(*refs):
        ins, outs = refs[:nw], refs[nw:2 * nw]
        send_sems, recv_sems = refs[2 * nw:]
        x, y, c = _place()
        _handshake([(x, y, 1 - c)])
        cps = []
        for w in range(nw):
            cp = pltpu.make_async_remote_copy(
                src_ref=ins[w], dst_ref=outs[w], send_sem=send_sems.at[w], recv_sem=recv_sems.at[w],
                device_id=(x, y, 1 - c), device_id_type=MESH)
            cp.start()
            cps.append(cp)
        for cp in cps:
            cp.wait()

    sem = pltpu.SemaphoreType.DMA
    return _sequencer(body, halves, [_sds(h.shape, h.dtype) for h in halves], [sem((nw,)), sem((nw,))], JOIN_ID, name)


def _allreduce_rows(vec, *, name):
    rows = vec.shape[0]

    def body(v_ref, out_ref, slots, send_sems, recv_sems):
        x, y, c = _place()
        me = 4 * x + 2 * y + c
        slots[me] = v_ref[...]
        peers = []
        for mask in range(1, N_DEV):
            px = 1 - x if mask & 4 else x
            py = 1 - y if mask & 2 else y
            pc = 1 - c if mask & 1 else c
            peers.append((px, py, pc))
        cps = []
        for k, peer in enumerate(peers):
            cp = pltpu.make_async_remote_copy(
                src_ref=v_ref, dst_ref=slots.at[me], send_sem=send_sems.at[k], recv_sem=recv_sems.at[k],
                device_id=peer, device_id_type=MESH)
            cp.start()
            cps.append(cp)
        for k, (px, py, pc) in enumerate(peers):
            pltpu.make_async_remote_copy(
                src_ref=v_ref, dst_ref=slots.at[4 * px + 2 * py + pc], send_sem=send_sems.at[k],
                recv_sem=recv_sems.at[k], device_id=(px, py, pc), device_id_type=MESH).wait_recv()
        for cp in cps:
            cp.wait_send()
        acc = slots[0]
        for d in range(1, N_DEV):
            acc = acc + slots[d]
        out_ref[...] = acc

    vmem = pl.BlockSpec(memory_space=pltpu.VMEM)
    return pl.pallas_call(
        body, in_specs=[vmem], out_specs=vmem, out_shape=_sds((rows, 128)),
        scratch_shapes=[pltpu.VMEM((N_DEV, rows, 128), F32), pltpu.SemaphoreType.DMA((N_DEV - 1,)),
                        pltpu.SemaphoreType.DMA((N_DEV - 1,))],
        compiler_params=pltpu.CompilerParams(vmem_limit_bytes=VMEM_LIMIT_BYTES), name=name)(vec)


def _reduce_scatter(grads, core, chip, *, tag):
    others = _swap_other_halves(grads, name="swap_other_halves_" + tag)
    parts = [_add_my_half(core, g, o) for g, o in zip(grads, others)]
    recvd = _scatter_partials(parts, name="scatter_partials_" + tag)
    mine = [_sum_partials(chip, p, r) for p, r in zip(parts, recvd)]
    return mine, _swap_reduced_halves(mine, name="swap_reduced_halves_" + tag)


ADAM_BLOCK_ELEMS = 256 * 1024


def _adam_rows(rows, cols):
    tm = rows
    while tm * cols > ADAM_BLOCK_ELEMS and tm % 16 == 0:
        tm //= 2
    return tm


def _adam_step(wv, gv, mv, vv):
    m2 = ADAM_B1 * mv + (1.0 - ADAM_B1) * gv
    v2 = ADAM_B2 * vv + (1.0 - ADAM_B2) * (gv * gv)
    m_hat = m2 / (1.0 - ADAM_B1 ** ADAM_STEP)
    v_hat = v2 / (1.0 - ADAM_B2 ** ADAM_STEP)
    return -ADAM_LR * (m_hat / (jnp.sqrt(v_hat) + ADAM_EPS) + ADAM_WD * wv), m2, v2


def _adamw(w, g, m, v, *, name):
    rows, cols = w.shape
    return _rowwise(_adam_step, [w, g, m, v], [], [_sds((rows, cols))] * 3, tm=_adam_rows(rows, cols), name=name)


def _adamw_halves(core, w, g_mine, g_theirs, m, v, *, name):
    rows, cols = w.shape
    hw = rows // 2
    tm = _adam_rows(hw, cols)
    per_half = hw // tm

    def body(core_ref, w_ref, gm_ref, gt_ref, m_ref, v_ref, g_out, d_out, m_out, v_out):
        mine = (pl.program_id(0) // per_half) == core_ref[0]
        g = jnp.where(mine, gm_ref[...], gt_ref[...])
        d, m2, v2 = _adam_step(w_ref[...], g, m_ref[...], v_ref[...])
        g_out[...] = g
        d_out[...] = d
        m_out[...] = m2
        v_out[...] = v2

    full = pl.BlockSpec((tm, cols), lambda i, core_ref: (i, 0))
    half = pl.BlockSpec((tm, cols), lambda i, core_ref: (i % per_half, 0))
    return pl.pallas_call(
        body,
        grid_spec=pltpu.PrefetchScalarGridSpec(
            num_scalar_prefetch=1, grid=(rows // tm,), in_specs=[full, half, half, full, full],
            out_specs=[full, full, full, full]),
        out_shape=[_sds((rows, cols))] * 4, compiler_params=_cp(("parallel",)), name=name)(
            core, w, g_mine, g_theirs, m, v)


def _pack_rows(arrs):
    flat = jnp.concatenate([a.reshape(-1).astype(F32) for a in arrs])
    rows = -(-flat.shape[0] // 1024) * 8
    return jnp.pad(flat, (0, rows * 128 - flat.shape[0])).reshape(rows, 128)


def _unpack_rows(vec, shapes):
    flat = vec.reshape(-1)
    out, off = [], 0
    for shp in shapes:
        size = math.prod(shp)
        out.append(flat[off:off + size].reshape(shp))
        off += size
    return out


SMALL = ("b_gate", "ssm_a_re", "ssm_a_im", "ssm_log_dt", "ssm_b_re", "ssm_b_im", "ssm_c_re", "ssm_c_im", "ssm_d",
         "ln1_g", "ln1_b", "ln2_g", "ln2_b")
GATHER_GROUPS = (("w_in", ("w_in",)), ("mixer", ("w_attn_br", "w_ssm_br", "w_glu", "w_out")),
                 ("ffn", ("w_ff_gate", "w_ff_up", "w_ff_down")))
REDUCE_GROUPS = (("ffn", ("w_ff_down", "w_ff_gate", "w_ff_up")),
                 ("mixer", ("w_out", "w_ssm_br", "w_glu", "w_attn_br")), ("w_in", ("w_in",)))
WEIGHTS = ("w_in", "b_gate", "w_attn_br", "w_ssm_br", "w_out", "ssm_a_re", "ssm_a_im", "ssm_log_dt", "ssm_b_re",
           "ssm_b_im", "ssm_c_re", "ssm_c_im", "ssm_d", "w_glu", "ln1_g", "ln1_b", "w_ff_gate", "w_ff_up", "w_ff_down",
           "ln2_g", "ln2_b")


def kernel(x, w_in, b_gate, w_attn_br, w_ssm_br, w_out, ssm_a_re, ssm_a_im, ssm_log_dt, ssm_b_re, ssm_b_im, ssm_c_re, ssm_c_im, ssm_d, w_glu, ln1_g, ln1_b, w_ff_gate, w_ff_up, w_ff_down, ln2_g, ln2_b, loss_target, m_w_in, m_b_gate, m_w_attn_br, m_w_ssm_br, m_w_out, m_ssm_a_re, m_ssm_a_im, m_ssm_log_dt, m_ssm_b_re, m_ssm_b_im, m_ssm_c_re, m_ssm_c_im, m_ssm_d, m_w_glu, m_ln1_g, m_ln1_b, m_w_ff_gate, m_w_ff_up, m_w_ff_down, m_ln2_g, m_ln2_b, v_w_in, v_b_gate, v_w_attn_br, v_w_ssm_br, v_w_out, v_ssm_a_re, v_ssm_a_im, v_ssm_log_dt, v_ssm_b_re, v_ssm_b_im, v_ssm_c_re, v_ssm_c_im, v_ssm_d, v_w_glu, v_ln1_g, v_ln1_b, v_w_ff_gate, v_w_ff_up, v_w_ff_down, v_ln2_g, v_ln2_b):
    given = dict(locals())
    px, py, pc = _place()
    chip = 2 * px + py
    core_s = jnp.reshape(pc, (1,)).astype(jnp.int32)
    chip_s = jnp.reshape(chip, (1,)).astype(jnp.int32)

    wts = {}
    for tag, names in GATHER_GROUPS:
        wts.update(zip(names, _gather_weights([given[n][0].astype(BF16) for n in names], name="gather_" + tag)))
    ncol = D_MODEL // N_CHIPS
    bg_mine = jnp.where(pc == 0, b_gate[0], jnp.zeros_like(b_gate[0]))
    bg_full = lax.dynamic_update_slice(jnp.zeros((2, D_MODEL), F32), bg_mine, (0, chip * ncol))
    bg_full = _allreduce_rows(bg_full.reshape(16, 128), name="gather_gate_bias").reshape(2, D_MODEL)
    small = {n: given[n][0] for n in SMALL if n.startswith("ssm")}
    small.update({n: given[n] for n in ("ln1_g", "ln1_b", "ln2_g", "ln2_b")})
    small["b_gate"] = bg_full

    loss_mine, grad_x, big_g, small_g = _local_step(x[0], loss_target[0], wts, small)
    loss = lax.psum(loss_mine, ("x", "y", "c"))

    grads, delta, new_m, new_v = {}, {}, {}, {}
    for tag, names in REDUCE_GROUPS:
        mine, theirs = _reduce_scatter([big_g[n] for n in names], core_s, chip_s, tag=tag)
        for n, g_mine, g_theirs in zip(names, mine, theirs):
            shp = given[n].shape
            two = (shp[1], shp[2])
            res = _adamw_halves(core_s, given[n].reshape(two), g_mine, g_theirs, given["m_" + n].reshape(two),
                                given["v_" + n].reshape(two), name="adamw_" + n)
            grads[n], delta[n], new_m[n], new_v[n] = [r.reshape(shp) for r in res]
    shapes = [small_g[n].shape for n in SMALL]
    summed = _unpack_rows(_allreduce_rows(_pack_rows([small_g[n] for n in SMALL]), name="allreduce_small"), shapes)
    for n, g in zip(SMALL, summed):
        if n == "b_gate":
            g = lax.dynamic_slice(g, (0, chip * ncol), (2, ncol))
        grads[n] = g.reshape(given[n].shape)

    shapes = [given[n].shape for n in SMALL]
    packed = [_pack_rows([src[n] for n in SMALL]) for src in
              (given, grads, {n: given["m_" + n] for n in SMALL}, {n: given["v_" + n] for n in SMALL})]
    for out, vec in zip((delta, new_m, new_v), _adamw(*packed, name="adamw_small")):
        out.update(zip(SMALL, _unpack_rows(vec, shapes)))

    return (loss, grad_x.reshape(x.shape), *[grads[n] for n in WEIGHTS], *[delta[n] for n in WEIGHTS],
            *[new_m[n] for n in WEIGHTS], *[new_v[n] for n in WEIGHTS])
```

```python
import math

import jax
import jax.numpy as jnp
from jax import lax
from jax.experimental import pallas as pl
from jax.experimental.pallas import tpu as pltpu
from jax.experimental.pallas import tpu_sc as plsc

F32 = jnp.float32
BF16 = jnp.bfloat16
MESH = pl.DeviceIdType.MESH

D_MODEL = 1024
SEQ = 2048
HEAD_DIM = 64
ATTN_HEADS = 8
DILATIONS = (1, 4, 16)
ATTN_WIDTH = ATTN_HEADS * HEAD_DIM
QKV_WIDTH = 3 * ATTN_WIDTH
BLOCK = 128
ROPE_THETA = 10000.0
NEG_INF = -1e30
SSM_GROUP = 16
SSM_GROUPS = 32
SSM_WIDTH = 512
SSM_STATE = 64
SSM_LANES = SSM_GROUPS * SSM_STATE
SCAN_CHUNKS = 8
SCAN_STEPS = SEQ // SCAN_CHUNKS
SCAN_LANES = 128
IN_WIDTH = 3 * QKV_WIDTH + SSM_WIDTH + 2 * D_MODEL
D_FF = 2816
N_CHIPS = 4
N_DEV = 8
DN_ALPHA = 2.0 ** 0.25
LN_EPS = 1e-5
ADAM_LR = 0.001
ADAM_B1 = 0.9
ADAM_B2 = 0.999
ADAM_EPS = 1e-08
ADAM_WD = 0.01
ADAM_STEP = 10
GELU_C = math.sqrt(2.0 / math.pi)
GELU_K = 0.044715

VMEM_LIMIT_BYTES = 56 * 1024 * 1024


def _sds(shape, dtype=F32):
    return jax.ShapeDtypeStruct(tuple(shape), dtype)


def _cp(semantics=None):
    return pltpu.CompilerParams(dimension_semantics=semantics, vmem_limit_bytes=VMEM_LIMIT_BYTES)


HBM_OPERAND = pl.BlockSpec(memory_space=pl.ANY)


def _matmul(a, b, *, grid, a_spec, b_spec, o_spec, out_shape, dims, k_axis=None, name, after=()):
    nk = grid[k_axis] if k_axis is not None else 1
    o_block = tuple(d for d in o_spec.block_shape if d is not None)
    n_after = len(after)

    def body(a_ref, b_ref, *rest):
        o_ref, acc = rest[n_after], rest[n_after + 1:]
        part = lax.dot_general(a_ref[...].astype(BF16), b_ref[...].astype(BF16),
                               (((dims[0],), (dims[1],)), ((), ())), preferred_element_type=F32)
        if k_axis is None:
            o_ref[...] = part.astype(o_ref.dtype)
        else:
            k = pl.program_id(k_axis)

            @pl.when(k == 0)
            def _():
                acc[0][...] = part

            @pl.when(k > 0)
            def _():
                acc[0][...] += part

            @pl.when(k == nk - 1)
            def _():
                o_ref[...] = acc[0][...].astype(o_ref.dtype)

    sem = tuple("arbitrary" if ax == k_axis else "parallel" for ax in range(len(grid)))
    return pl.pallas_call(
        body, grid=grid, in_specs=[a_spec, b_spec] + [HBM_OPERAND] * n_after, out_specs=o_spec, out_shape=out_shape,
        scratch_shapes=[pltpu.VMEM(o_block, F32)] if k_axis is not None else [],
        compiler_params=_cp(sem), name=name)(a, b, *after)


def _mm_cols(a, wg, *, tm, name, out_dtype=F32, out3d=False):
    m, k = a.shape
    ns = wg.shape[2]
    if out3d:
        o_spec = pl.BlockSpec((None, tm, ns), lambda i, s: (s, i, 0))
        out_shape = _sds((N_CHIPS, m, ns), out_dtype)
    else:
        o_spec = pl.BlockSpec((tm, ns), lambda i, s: (i, s))
        out_shape = _sds((m, N_CHIPS * ns), out_dtype)
    return _matmul(a, wg, grid=(m // tm, N_CHIPS),
                   a_spec=pl.BlockSpec((tm, k), lambda i, s: (i, 0)),
                   b_spec=pl.BlockSpec((None, k, ns), lambda i, s: (s, 0, 0)),
                   o_spec=o_spec, out_shape=out_shape, dims=(1, 0), name=name)


def _mm_cols_nt(dy, wg, *, tm, name, dy3d=False, out_dtype=F32, after=()):
    k, ns = wg.shape[1], wg.shape[2]
    if dy3d:
        m = dy.shape[1]
        a_spec = pl.BlockSpec((None, tm, ns), lambda i, s: (s, i, 0))
    else:
        m = dy.shape[0]
        a_spec = pl.BlockSpec((tm, ns), lambda i, s: (i, s))
    return _matmul(dy, wg, grid=(m // tm, N_CHIPS), a_spec=a_spec,
                   b_spec=pl.BlockSpec((None, k, ns), lambda i, s: (s, 0, 0)),
                   o_spec=pl.BlockSpec((tm, k), lambda i, s: (i, 0)),
                   out_shape=_sds((m, k), out_dtype), dims=(1, 1), k_axis=1, name=name, after=after)


def _mm_cols_tn(a, dy, *, ns, name, dy3d=False):
    m, k = a.shape
    if dy3d:
        b_spec = pl.BlockSpec((None, m, ns), lambda s: (s, 0, 0))
    else:
        b_spec = pl.BlockSpec((m, ns), lambda s: (0, s))
    return _matmul(a, dy, grid=(N_CHIPS,), a_spec=pl.BlockSpec((m, k), lambda s: (0, 0)), b_spec=b_spec,
                   o_spec=pl.BlockSpec((None, k, ns), lambda s: (s, 0, 0)),
                   out_shape=_sds((N_CHIPS, k, ns), BF16), dims=(0, 0), name=name)


def _mm_plain(a, b, *, tm, tn, name, out_dtype=F32, dims=(1, 0), tk=None):
    m = a.shape[1 - dims[0]]
    kk = a.shape[dims[0]]
    n = b.shape[1 - dims[1]]
    tk = kk if tk is None else tk
    nk = kk // tk

    def a_idx(i, j, k):
        return (i, k) if dims[0] == 1 else (k, i)

    def b_idx(i, j, k):
        return (k, j) if dims[1] == 0 else (j, k)

    a_blk = (tm, tk) if dims[0] == 1 else (tk, tm)
    b_blk = (tk, tn) if dims[1] == 0 else (tn, tk)
    return _matmul(a, b, grid=(m // tm, n // tn, nk),
                   a_spec=pl.BlockSpec(a_blk, a_idx), b_spec=pl.BlockSpec(b_blk, b_idx),
                   o_spec=pl.BlockSpec((tm, tn), lambda i, j, k: (i, j)),
                   out_shape=_sds((m, n), out_dtype), dims=dims, k_axis=2 if nk > 1 else None, name=name)


def _rowwise(fn, tiled, full, outs, accs=(), *, tm, name, after=()):
    args, in_specs = [], []
    for t in tiled:
        if isinstance(t, tuple):
            arr, w, cb = t
            in_specs.append(pl.BlockSpec((tm, w), lambda i, cb=cb: (i, cb)))
        else:
            arr = t
            in_specs.append(pl.BlockSpec((tm, arr.shape[1]), lambda i: (i, 0)))
        args.append(arr)
    rows = args[0].shape[0]
    for f in full:
        in_specs.append(pl.BlockSpec(f.shape, lambda i, nd=f.ndim: (0,) * nd))
        args.append(f)
    out_specs = [pl.BlockSpec((tm, o.shape[1]), lambda i: (i, 0)) for o in outs]
    out_specs += [pl.BlockSpec(a.shape, lambda i, nd=len(a.shape): (0,) * nd) for a in accs]
    n_in, n_out = len(args), len(outs)
    in_specs += [HBM_OPERAND] * len(after)
    first_out = n_in + len(after)

    def body(*refs):
        res = fn(*[r[...] for r in refs[:n_in]])
        res = res if isinstance(res, (tuple, list)) else (res,)
        for r, v in zip(refs[first_out:first_out + n_out], res[:n_out]):
            r[...] = v.astype(r.dtype)
        i = pl.program_id(0)
        for r, v in zip(refs[first_out + n_out:], res[n_out:]):
            @pl.when(i == 0)
            def _(r=r, v=v):
                r[...] = v

            @pl.when(i > 0)
            def _(r=r, v=v):
                r[...] += v

    res = pl.pallas_call(
        body, grid=(rows // tm,), in_specs=in_specs, out_specs=out_specs, out_shape=list(outs) + list(accs),
        compiler_params=_cp(("arbitrary",) if accs else ("parallel",)), name=name)(*args, *after)
    return res


def _colsum(v):
    return jnp.sum(v, axis=0, keepdims=True)


def _ln_stats(z):
    mu = jnp.mean(z, axis=-1, keepdims=True)
    zc = z - mu
    var = jnp.mean(zc * zc, axis=-1, keepdims=True)
    rstd = lax.rsqrt(var + LN_EPS)
    return zc * rstd, rstd


def _ln_bwd(dy, xhat, rstd, g):
    dxh = dy * g
    m1 = jnp.mean(dxh, axis=-1, keepdims=True)
    m2 = jnp.mean(dxh * xhat, axis=-1, keepdims=True)
    return rstd * (dxh - m1 - xhat * m2)


def _swap_halves(t):
    w = t.shape[-1]
    lane = lax.broadcasted_iota(jnp.int32, t.shape, t.ndim - 1)
    return jnp.where((lane % HEAD_DIM) < HEAD_DIM // 2, pltpu.roll(t, w - HEAD_DIM // 2, t.ndim - 1),
                     pltpu.roll(t, HEAD_DIM // 2, t.ndim - 1))


def _band_mask(i):
    row = lax.broadcasted_iota(jnp.int32, (BLOCK, 2 * BLOCK), 0)
    col = lax.broadcasted_iota(jnp.int32, (BLOCK, 2 * BLOCK), 1)
    dist = BLOCK + row - col
    return (dist >= 0) & (dist <= BLOCK) & ((col >= BLOCK) | (i > 0))


PAIR = 2 * HEAD_DIM
UNITS = SEQ // BLOCK
ROPE_ROWS = 256


def _rope(t, cf, ss):
    return t * cf + _swap_halves(t) * ss


def _rope_transposed(d, cf, ss):
    return d * cf + _swap_halves(d * ss)


def _unit_rows(u, dil):
    if dil == 1:
        i = u
        start = pl.multiple_of(u * BLOCK, BLOCK)
        prev = pl.multiple_of(jnp.maximum(u - 1, 0) * BLOCK, BLOCK)
        return i, pl.ds(start, BLOCK), pl.ds(prev, BLOCK)
    rho = jnp.bitwise_and(u, dil - 1)
    i = jnp.right_shift(u, dil.bit_length() - 1)
    start = rho + dil * BLOCK * i
    prev = rho + dil * BLOCK * jnp.maximum(i - 1, 0)
    return i, pl.ds(start, BLOCK, stride=dil), pl.ds(prev, BLOCK, stride=dil)


def _causal_mask():
    row = lax.broadcasted_iota(jnp.int32, (BLOCK, BLOCK), 0)
    col = lax.broadcasted_iota(jnp.int32, (BLOCK, BLOCK), 1)
    return row >= col


def _pair_views(col0):
    return [pl.BlockSpec((SEQ, PAIR), lambda hp, g=g: (0, col0 // PAIR + g * (ATTN_WIDTH // PAIR) + hp))
            for g in range(len(DILATIONS))]


def _rotate_keys(k_refs, kr_refs, cf_ref, ss_ref):
    def step(t, carry):
        rows = pl.ds(pl.multiple_of(t * ROPE_ROWS, ROPE_ROWS), ROPE_ROWS)
        cf, ss = cf_ref[rows, :], ss_ref[rows, :]
        for k_ref, kr_ref in zip(k_refs, kr_refs):
            kr_ref[rows, :] = _rope(k_ref[rows, :], cf, ss)
        return carry

    lax.fori_loop(0, SEQ // ROPE_ROWS, step, 0)


def _attention_fwd(proj, cos_f, sin_s):
    ng = len(DILATIONS)

    def body(*refs):
        q_refs, k_refs, v_refs = refs[:ng], refs[ng:2 * ng], refs[2 * ng:3 * ng]
        cf_ref, ss_ref, attn_ref, lse_ref = refs[3 * ng:3 * ng + 4]
        kr_refs = refs[3 * ng + 4:]
        _rotate_keys(k_refs, kr_refs, cf_ref, ss_ref)
        first = lax.broadcasted_iota(jnp.int32, (BLOCK, PAIR), 1) < HEAD_DIM
        for g, dil in enumerate(DILATIONS):
            two_blocks = SEQ // dil > BLOCK

            def unit(u, carry, g=g, dil=dil, two_blocks=two_blocks):
                i, rows, prev = _unit_rows(u, dil)
                qq = (_rope(q_refs[g][rows, :], cf_ref[rows, :], ss_ref[rows, :]) * (1.0 / math.sqrt(HEAD_DIM))).astype(BF16)
                if two_blocks:
                    kk = jnp.concatenate([kr_refs[g][prev, :], kr_refs[g][rows, :]], axis=0).astype(BF16)
                    vv = jnp.concatenate([v_refs[g][prev, :], v_refs[g][rows, :]], axis=0).astype(BF16)
                    valid = _band_mask(i)
                else:
                    kk = kr_refs[g][rows, :].astype(BF16)
                    vv = v_refs[g][rows, :].astype(BF16)
                    valid = _causal_mask()
                zero = jnp.zeros_like(qq)
                outs, lses = [], []
                for qh in (jnp.where(first, qq, zero), jnp.where(first, zero, qq)):
                    s = lax.dot_general(qh, kk, (((1,), (1,)), ((), ())), preferred_element_type=F32)
                    s = jnp.where(valid, s, NEG_INF)
                    m = jnp.max(s, axis=1, keepdims=True)
                    p = jnp.exp(s - m)
                    l = jnp.sum(p, axis=1, keepdims=True)
                    outs.append(jnp.dot(p.astype(BF16), vv, preferred_element_type=F32) / l)
                    lses.append(m + jnp.log(l))
                o = jnp.where(first, outs[0], outs[1])
                lse = jnp.where(first, lses[0], lses[1])
                if g == 0:
                    attn_ref[rows, :] = o
                    lse_ref[rows, :] = lse
                else:
                    lse_old = lse_ref[rows, :]
                    m = jnp.maximum(lse_old, lse)
                    lse_new = m + jnp.log(jnp.exp(lse_old - m) + jnp.exp(lse - m))
                    attn_ref[rows, :] = attn_ref[rows, :] * jnp.exp(lse_old - lse_new) + o * jnp.exp(lse - lse_new)
                    lse_ref[rows, :] = lse_new
                return carry

            lax.fori_loop(0, UNITS, unit, 0)

    whole = pl.BlockSpec((SEQ, PAIR), lambda hp: (0, 0))
    out = pl.BlockSpec((SEQ, PAIR), lambda hp: (0, hp))
    return pl.pallas_call(
        body, grid=(ATTN_WIDTH // PAIR,),
        in_specs=_pair_views(0) + _pair_views(QKV_WIDTH) + _pair_views(2 * QKV_WIDTH) + [whole, whole],
        out_specs=[out, out], out_shape=[_sds((SEQ, ATTN_WIDTH)), _sds((SEQ, ATTN_WIDTH))],
        scratch_shapes=[pltpu.VMEM((SEQ, PAIR), F32)] * ng,
        compiler_params=_cp(("parallel",)), name="attention_fwd")(*([proj] * (3 * ng)), cos_f, sin_s)


def _attention_bwd(g, proj, cos_f, sin_s, d_attn, attn, lse):
    dil = DILATIONS[g]
    two_blocks = SEQ // dil > BLOCK

    def body(q_ref, k_ref, v_ref, cf_ref, ss_ref, do_ref, o_ref, lse_ref, dq_out, dk_out, dv_out,
             kr_ref, dq_acc, dk_acc, dv_acc):
        _rotate_keys([k_ref], [kr_ref], cf_ref, ss_ref)
        dk_acc[...] = jnp.zeros_like(dk_acc)
        dv_acc[...] = jnp.zeros_like(dv_acc)
        first = lax.broadcasted_iota(jnp.int32, (BLOCK, PAIR), 1) < HEAD_DIM
        nk = 2 * BLOCK if two_blocks else BLOCK
        first_k = lax.broadcasted_iota(jnp.int32, (nk, PAIR), 1) < HEAD_DIM

        def unit(u, carry):
            i, rows, prev = _unit_rows(u, dil)
            qq = (_rope(q_ref[rows, :], cf_ref[rows, :], ss_ref[rows, :]) * (1.0 / math.sqrt(HEAD_DIM))).astype(BF16)
            if two_blocks:
                kk = jnp.concatenate([kr_ref[prev, :], kr_ref[rows, :]], axis=0).astype(BF16)
                vv = jnp.concatenate([v_ref[prev, :], v_ref[rows, :]], axis=0).astype(BF16)
                valid = _band_mask(i)
            else:
                kk = kr_ref[rows, :].astype(BF16)
                vv = v_ref[rows, :].astype(BF16)
                valid = _causal_mask()
            dof = do_ref[rows, :]
            dd = dof * o_ref[rows, :]
            lse2 = lse_ref[rows, :]
            dob = dof.astype(BF16)
            zq, zd, zf = jnp.zeros_like(qq), jnp.zeros_like(dob), jnp.zeros_like(dd)
            dqs, dks, dvs = [], [], []
            for h in range(2):
                sel = first if h == 0 else jnp.logical_not(first)
                qh = jnp.where(sel, qq, zq)
                doh = jnp.where(sel, dob, zd)
                delta = jnp.sum(jnp.where(sel, dd, zf), axis=1, keepdims=True)
                lse_h = lse2[:, h * HEAD_DIM:h * HEAD_DIM + 1]
                s = lax.dot_general(qh, kk, (((1,), (1,)), ((), ())), preferred_element_type=F32)
                p = jnp.where(valid, jnp.exp(s - lse_h), 0.0)
                dp = lax.dot_general(doh, vv, (((1,), (1,)), ((), ())), preferred_element_type=F32)
                ds = (p * (dp - delta)).astype(BF16)
                dqs.append(jnp.dot(ds, kk, preferred_element_type=F32))
                dks.append(lax.dot_general(ds, qq, (((0,), (0,)), ((), ())), preferred_element_type=F32))
                dvs.append(lax.dot_general(p.astype(BF16), dob, (((0,), (0,)), ((), ())), preferred_element_type=F32))
            dq_acc[rows, :] = jnp.where(first, dqs[0], dqs[1])
            dk2 = jnp.where(first_k, dks[0], dks[1])
            dv2 = jnp.where(first_k, dvs[0], dvs[1])
            dk_acc[rows, :] += dk2[nk - BLOCK:]
            dv_acc[rows, :] += dv2[nk - BLOCK:]
            if two_blocks:
                @pl.when(i > 0)
                def _():
                    dk_acc[prev, :] += dk2[:BLOCK]
                    dv_acc[prev, :] += dv2[:BLOCK]
            return carry

        lax.fori_loop(0, UNITS, unit, 0)

        def finish(t, carry):
            rows = pl.ds(pl.multiple_of(t * ROPE_ROWS, ROPE_ROWS), ROPE_ROWS)
            cf, ss = cf_ref[rows, :], ss_ref[rows, :]
            dq = dq_acc[rows, :] * (1.0 / math.sqrt(HEAD_DIM))
            dq_out[rows, :] = _rope_transposed(dq, cf, ss).astype(BF16)
            dk_out[rows, :] = _rope_transposed(dk_acc[rows, :], cf, ss).astype(BF16)
            dv_out[rows, :] = dv_acc[rows, :].astype(BF16)
            return carry

        lax.fori_loop(0, SEQ // ROPE_ROWS, finish, 0)

    whole = pl.BlockSpec((SEQ, PAIR), lambda hp: (0, 0))
    pair = pl.BlockSpec((SEQ, PAIR), lambda hp: (0, hp))
    views = [_pair_views(col0)[g] for col0 in (0, QKV_WIDTH, 2 * QKV_WIDTH)]
    return pl.pallas_call(
        body, grid=(ATTN_WIDTH // PAIR,), in_specs=views + [whole, whole, pair, pair, pair],
        out_specs=[pair, pair, pair], out_shape=[_sds((SEQ, ATTN_WIDTH), BF16)] * 3,
        scratch_shapes=[pltpu.VMEM((SEQ, PAIR), F32)] * 4,
        compiler_params=_cp(("parallel",)), name=f"attention_bwd_{g}")(proj, proj, proj, cos_f, sin_s, d_attn, attn, lse)


def _cmul(ar, ai, br, bi):
    return ar * br - ai * bi, ar * bi + ai * br


def _pow256(ar, ai):
    for _ in range(8):
        ar, ai = _cmul(ar, ai, ar, ai)
    return ar, ai


def _chunk_carries(first_r, first_i, pr, pi, reverse):
    rows = lax.broadcasted_iota(jnp.int32, first_r.shape, 0)
    out_r = jnp.zeros_like(first_r)
    out_i = jnp.zeros_like(first_i)
    hr = jnp.zeros_like(first_r[0:1])
    hi = jnp.zeros_like(hr)
    order = range(SCAN_CHUNKS - 1, -1, -1) if reverse else range(SCAN_CHUNKS)
    for c in order:
        out_r = jnp.where(rows == c, hr, out_r)
        out_i = jnp.where(rows == c, hi, out_i)
        tr, ti = _cmul(pr[0:1], pi[0:1], hr, hi)
        hr = first_r[c:c + 1] + tr
        hi = first_i[c:c + 1] + ti
    return out_r, out_i


def _tile(j):
    return pl.ds(j, SCAN_CHUNKS, stride=SCAN_STEPS)


def _scan_fwd(bur, bui, ar, ai):
    lb = SCAN_LANES

    def body(bur_ref, bui_ref, ar_ref, ai_ref, hr_ref, hi_ref, er_ref, ei_ref):
        a_r = jnp.broadcast_to(ar_ref[...], (SCAN_CHUNKS, lb))
        a_i = jnp.broadcast_to(ai_ref[...], (SCAN_CHUNKS, lb))

        def local(j, carry):
            tr, ti = _cmul(a_r, a_i, carry[0], carry[1])
            nr = tr + bur_ref[_tile(j), :]
            ni = ti + bui_ref[_tile(j), :]
            hr_ref[_tile(j), :] = nr
            hi_ref[_tile(j), :] = ni
            return nr, ni

        zero = jnp.zeros((SCAN_CHUNKS, lb), F32)
        last_r, last_i = lax.fori_loop(0, SCAN_STEPS, local, (zero, zero), unroll=4)
        pr, pi = _pow256(a_r, a_i)
        er, ei = _chunk_carries(last_r, last_i, pr, pi, reverse=False)
        er_ref[...] = er
        ei_ref[...] = ei

        def fix(j, carry):
            tr, ti = _cmul(carry[0], carry[1], er, ei)
            hr_ref[_tile(j), :] += tr
            hi_ref[_tile(j), :] += ti
            return _cmul(carry[0], carry[1], a_r, a_i)

        lax.fori_loop(0, SCAN_STEPS, fix, (a_r, a_i), unroll=4)

    big = pl.BlockSpec((SEQ, lb), lambda j: (0, j))
    vec = pl.BlockSpec((1, lb), lambda j: (0, j))
    ent = pl.BlockSpec((SCAN_CHUNKS, lb), lambda j: (0, j))
    return pl.pallas_call(
        body, grid=(SSM_LANES // lb,), in_specs=[big, big, vec, vec], out_specs=[big, big, ent, ent],
        out_shape=[_sds((SEQ, SSM_LANES)), _sds((SEQ, SSM_LANES)), _sds((SCAN_CHUNKS, SSM_LANES)),
                   _sds((SCAN_CHUNKS, SSM_LANES))],
        compiler_params=_cp(("parallel",)), name="ssm_scan_fwd")(bur, bui, ar, ai)


def _scan_bwd(gr, gi, hr, hi, er, ei, ar, ai):
    lb = SCAN_LANES

    def body(gr_ref, gi_ref, hr_ref, hi_ref, er_ref, ei_ref, ar_ref, ai_ref, lr_ref, li_ref, dar_ref, dai_ref):
        a_r = jnp.broadcast_to(ar_ref[...], (SCAN_CHUNKS, lb))
        a_i = -jnp.broadcast_to(ai_ref[...], (SCAN_CHUNKS, lb))

        def local(t, carry):
            j = SCAN_STEPS - 1 - t
            tr, ti = _cmul(a_r, a_i, carry[0], carry[1])
            nr = tr + gr_ref[_tile(j), :]
            ni = ti + gi_ref[_tile(j), :]
            lr_ref[_tile(j), :] = nr
            li_ref[_tile(j), :] = ni
            return nr, ni

        zero = jnp.zeros((SCAN_CHUNKS, lb), F32)
        first_r, first_i = lax.fori_loop(0, SCAN_STEPS, local, (zero, zero), unroll=4)
        pr, pi = _pow256(a_r, a_i)
        nxt_r, nxt_i = _chunk_carries(first_r, first_i, pr, pi, reverse=True)

        def accumulate(lam_r, lam_i, hp_r, hp_i, acc):
            return (acc[0] + lam_r * hp_r + lam_i * hp_i, acc[1] + lam_i * hp_r - lam_r * hp_i)

        def fix(t, carry):
            qr, qi, acc_r, acc_i = carry
            j = SCAN_STEPS - 1 - t
            tr, ti = _cmul(qr, qi, nxt_r, nxt_i)
            lam_r = lr_ref[_tile(j), :] + tr
            lam_i = li_ref[_tile(j), :] + ti
            lr_ref[_tile(j), :] = lam_r
            li_ref[_tile(j), :] = lam_i
            acc_r, acc_i = accumulate(lam_r, lam_i, hr_ref[_tile(j - 1), :], hi_ref[_tile(j - 1), :], (acc_r, acc_i))
            qr, qi = _cmul(qr, qi, a_r, a_i)
            return qr, qi, acc_r, acc_i

        qr, qi, acc_r, acc_i = lax.fori_loop(0, SCAN_STEPS - 1, fix, (a_r, a_i, zero, zero), unroll=4)
        tr, ti = _cmul(qr, qi, nxt_r, nxt_i)
        lam_r = lr_ref[_tile(0), :] + tr
        lam_i = li_ref[_tile(0), :] + ti
        lr_ref[_tile(0), :] = lam_r
        li_ref[_tile(0), :] = lam_i
        acc_r, acc_i = accumulate(lam_r, lam_i, er_ref[...], ei_ref[...], (acc_r, acc_i))
        dar_ref[...] = jnp.sum(acc_r, axis=0, keepdims=True)
        dai_ref[...] = jnp.sum(acc_i, axis=0, keepdims=True)

    big = pl.BlockSpec((SEQ, lb), lambda j: (0, j))
    vec = pl.BlockSpec((1, lb), lambda j: (0, j))
    ent = pl.BlockSpec((SCAN_CHUNKS, lb), lambda j: (0, j))
    return pl.pallas_call(
        body, grid=(SSM_LANES // lb,), in_specs=[big, big, big, big, ent, ent, vec, vec],
        out_specs=[big, big, vec, vec],
        out_shape=[_sds((SEQ, SSM_LANES)), _sds((SEQ, SSM_LANES)), _sds((1, SSM_LANES)), _sds((1, SSM_LANES))],
        compiler_params=_cp(("parallel",)), name="ssm_scan_bwd")(gr, gi, hr, hi, er, ei, ar, ai)


def _rope_tables():
    half = HEAD_DIM // 2
    inv_freq = ROPE_THETA ** (-jnp.arange(half, dtype=F32) / half)
    ang = jnp.arange(SEQ, dtype=F32)[:, None] * inv_freq[None, :]
    cos, sin = jnp.cos(ang), jnp.sin(ang)
    cos_f = jnp.concatenate([cos, cos, cos, cos], axis=1)
    sin_s = jnp.concatenate([-sin, sin, -sin, sin], axis=1)
    return cos_f, sin_s


def _ssm_discretise(a_re, a_im, log_dt, b_re, b_im):
    lam = lax.complex(a_re, a_im)
    dt = jnp.exp(log_dt)[:, None]
    a_bar = jnp.exp(lam * dt)
    b_bar = ((a_bar - 1.0) / lam)[..., None] * lax.complex(b_re, b_im)
    return a_bar.real, a_bar.imag, b_bar.real, b_bar.imag


def _block_diag_in(b):
    eye = jnp.eye(SSM_GROUPS, dtype=b.dtype)
    return (eye[:, None, :, None] * b.transpose(0, 2, 1)[:, :, None, :]).reshape(SSM_WIDTH, SSM_LANES)


def _block_diag_out(c):
    eye = jnp.eye(SSM_GROUPS, dtype=c.dtype)
    return (eye[:, None, :, None] * c.transpose(0, 2, 1)[:, :, None, :]).reshape(SSM_LANES, SSM_WIDTH)


def _diag_blocks(m, rows, cols):
    m4 = m.reshape(SSM_GROUPS, rows, SSM_GROUPS, cols)
    return jnp.diagonal(m4, axis1=0, axis2=2).transpose(2, 0, 1)


def _local_step(x, tgt, wts, small):
    s = SEQ
    cos_f, sin_s = _rope_tables()

    proj = _mm_cols(x, wts["w_in"], tm=1024, name="proj")

    attn, lse = _attention_fwd(proj, cos_f, sin_s)
    y_attn = _mm_cols(attn, wts["w_attn_br"], tm=s, name="y_attn")

    (abar_r, abar_i, bbar_r, bbar_i), ssm_vjp = jax.vjp(
        _ssm_discretise, small["ssm_a_re"], small["ssm_a_im"], small["ssm_log_dt"], small["ssm_b_re"], small["ssm_b_im"])
    b_in_r, b_in_i = _block_diag_in(bbar_r).astype(BF16), _block_diag_in(bbar_i).astype(BF16)
    c_out_r = _block_diag_out(small["ssm_c_re"]).astype(BF16)
    c_out_ni = _block_diag_out(-small["ssm_c_im"]).astype(BF16)
    a_r, a_i = abar_r.reshape(1, SSM_LANES), abar_i.reshape(1, SSM_LANES)
    d_skip = small["ssm_d"].reshape(1, SSM_WIDTH)

    u_f = proj[:, 3 * QKV_WIDTH:3 * QKV_WIDTH + SSM_WIDTH]
    u_p = u_f.astype(BF16)
    bu_r = _mm_plain(u_p, b_in_r, tm=s, tn=512, name="ssm_bu_re")
    bu_i = _mm_plain(u_p, b_in_i, tm=s, tn=512, name="ssm_bu_im")
    h_r, h_i, e_r, e_i = _scan_fwd(bu_r, bu_i, a_r, a_i)
    y_1 = _mm_plain(h_r, c_out_r, tm=s, tn=512, tk=512, name="ssm_y_re")
    y_2 = _mm_plain(h_i, c_out_ni, tm=s, tn=512, tk=512, name="ssm_y_im")

    def gelu_fwd(y1, y2, u, dsk):
        y = y1 + y2 + dsk * u
        return y, 0.5 * y * (1.0 + jnp.tanh(GELU_C * (y + GELU_K * y * y * y)))

    y_s5, gel = _rowwise(gelu_fwd, [y_1, y_2, u_f], [d_skip], [_sds((s, SSM_WIDTH)), _sds((s, SSM_WIDTH), BF16)],
                         tm=512, name="ssm_gelu")
    glu = _mm_cols(gel, wts["w_glu"], tm=s, name="glu")

    def glu_fwd(ga, gb):
        return ga * jax.nn.sigmoid(gb)

    (y_glu,) = _rowwise(glu_fwd, [(glu, SSM_WIDTH, 0), (glu, SSM_WIDTH, 1)], [], [_sds((s, SSM_WIDTH), BF16)],
                        tm=512, name="glu_gate")
    y_ssm = _mm_cols(y_glu, wts["w_ssm_br"], tm=s, name="y_ssm")

    gl0 = (proj, D_MODEL, (3 * QKV_WIDTH + SSM_WIDTH) // D_MODEL)
    gl1 = (proj, D_MODEL, (3 * QKV_WIDTH + SSM_WIDTH) // D_MODEL + 1)
    b_gate = small["b_gate"]

    def gate_mix(l0, l1, ya, ys, bg):
        return jax.nn.sigmoid(l0 + bg[0:1]) * ya + jax.nn.sigmoid(l1 + bg[1:2]) * ys

    (mixed,) = _rowwise(gate_mix, [gl0, gl1, y_attn, y_ssm], [b_gate], [_sds((s, D_MODEL), BF16)], tm=256,
                        name="gate_mix")
    w_out = wts["w_out"].reshape(D_MODEL, D_MODEL)
    mix_out = _mm_plain(mixed, w_out, tm=1024, tn=512, name="mix_out")

    def ln1_fwd(xv, mo, g, b):
        z = DN_ALPHA * xv + mo
        xhat, _ = _ln_stats(z)
        return z, xhat * g + b

    z1, h = _rowwise(ln1_fwd, [x, mix_out], [small["ln1_g"], small["ln1_b"]],
                     [_sds((s, D_MODEL)), _sds((s, D_MODEL))], tm=256, name="ln1")

    ff_a = _mm_cols(h, wts["w_ff_gate"], tm=1024, name="ff_gate", out3d=True)
    ff_b = _mm_cols(h, wts["w_ff_up"], tm=1024, name="ff_up", out3d=True)
    nf = D_FF // N_CHIPS

    def swiglu_fwd(a, b):
        return a * jax.nn.sigmoid(a) * b

    (act,) = _rowwise(swiglu_fwd, [ff_a.reshape(N_CHIPS * s, nf), ff_b.reshape(N_CHIPS * s, nf)], [],
                      [_sds((N_CHIPS * s, nf), BF16)], tm=1024, name="swiglu")
    act = act.reshape(N_CHIPS, s, nf)
    w_down = wts["w_ff_down"]
    ff = _matmul(act, w_down, grid=(2, N_CHIPS),
                 a_spec=pl.BlockSpec((None, 1024, nf), lambda i, k: (k, i, 0)),
                 b_spec=pl.BlockSpec((None, nf, D_MODEL), lambda i, k: (k, 0, 0)),
                 o_spec=pl.BlockSpec((1024, D_MODEL), lambda i, k: (i, 0)),
                 out_shape=_sds((s, D_MODEL)), dims=(1, 0), k_axis=1, name="ff_down")

    def ln2_loss(hv, ffv, tg, g, b):
        z = DN_ALPHA * hv + ffv
        xhat, rstd = _ln_stats(z)
        err = xhat * g + b - tg
        d_out = err * (1.0 / D_MODEL)
        loss_rows = jnp.sum(err * err, axis=-1, keepdims=True) * (0.5 / D_MODEL)
        loss = jnp.broadcast_to(jnp.sum(loss_rows, axis=0, keepdims=True), (1, 128))
        return _ln_bwd(d_out, xhat, rstd, g), loss, _colsum(d_out * xhat), _colsum(d_out)

    dz2, loss_v, d_ln2_g, d_ln2_b = _rowwise(
        ln2_loss, [h, ff, tgt], [small["ln2_g"], small["ln2_b"]], [_sds((s, D_MODEL))],
        [_sds((1, 128)), _sds((1, D_MODEL)), _sds((1, D_MODEL))], tm=256, name="ln2_loss")

    d_act = _matmul(dz2, w_down, grid=(2, N_CHIPS),
                    a_spec=pl.BlockSpec((1024, D_MODEL), lambda i, k: (i, 0)),
                    b_spec=pl.BlockSpec((None, nf, D_MODEL), lambda i, k: (k, 0, 0)),
                    o_spec=pl.BlockSpec((None, 1024, nf), lambda i, k: (k, i, 0)),
                    out_shape=_sds((N_CHIPS, s, nf)), dims=(1, 1), name="d_act")
    g_w_ff_down = _matmul(act, dz2, grid=(N_CHIPS,),
                          a_spec=pl.BlockSpec((None, s, nf), lambda k: (k, 0, 0)),
                          b_spec=pl.BlockSpec((s, D_MODEL), lambda k: (0, 0)),
                          o_spec=pl.BlockSpec((None, nf, D_MODEL), lambda k: (k, 0, 0)),
                          out_shape=_sds((N_CHIPS, nf, D_MODEL), BF16), dims=(0, 0), name="g_w_ff_down")

    def swiglu_bwd(da, a, b):
        sg = jax.nn.sigmoid(a)
        return da * b * sg * (1.0 + a * (1.0 - sg)), da * a * sg

    d_a, d_b = _rowwise(swiglu_bwd, [d_act.reshape(N_CHIPS * s, nf), ff_a.reshape(N_CHIPS * s, nf),
                                     ff_b.reshape(N_CHIPS * s, nf)], [],
                        [_sds((N_CHIPS * s, nf), BF16)] * 2, tm=1024, name="swiglu_bwd")
    d_a, d_b = d_a.reshape(N_CHIPS, s, nf), d_b.reshape(N_CHIPS, s, nf)
    g_w_ff_gate = _mm_cols_tn(h, d_a, ns=nf, name="g_w_ff_gate", dy3d=True)
    g_w_ff_up = _mm_cols_tn(h, d_b, ns=nf, name="g_w_ff_up", dy3d=True)
    dh_a = _mm_cols_nt(d_a, wts["w_ff_gate"], tm=1024, name="dh_gate", dy3d=True)
    dh_b = _mm_cols_nt(d_b, wts["w_ff_up"], tm=1024, name="dh_up", dy3d=True)

    def ln1_bwd(dz, da, db, z, g):
        xhat, rstd = _ln_stats(z)
        dh = DN_ALPHA * dz + da + db
        return _ln_bwd(dh, xhat, rstd, g), _colsum(dh * xhat), _colsum(dh)

    dz1, d_ln1_g, d_ln1_b = _rowwise(ln1_bwd, [dz2, dh_a, dh_b, z1], [small["ln1_g"]], [_sds((s, D_MODEL))],
                                     [_sds((1, D_MODEL)), _sds((1, D_MODEL))], tm=256, name="ln1_bwd")
    d_mixed = _mm_plain(dz1, w_out, tm=1024, tn=512, dims=(1, 1), name="d_mixed")
    g_w_out = _mm_plain(mixed, dz1, tm=D_MODEL, tn=512, dims=(0, 0), out_dtype=BF16, name="g_w_out")
    g_w_out = g_w_out.reshape(N_CHIPS, D_MODEL // N_CHIPS, D_MODEL)

    def gate_bwd(dm, l0, l1, ya, ys, bg):
        g0 = jax.nn.sigmoid(l0 + bg[0:1])
        g1 = jax.nn.sigmoid(l1 + bg[1:2])
        dl0 = dm * ya * g0 * (1.0 - g0)
        dl1 = dm * ys * g1 * (1.0 - g1)
        return dm * g0, dm * g1, jnp.concatenate([dl0, dl1], axis=1), _colsum(dl0), _colsum(dl1)

    d_y_attn, d_y_ssm, d_gl, d_bg0, d_bg1 = _rowwise(
        gate_bwd, [d_mixed, gl0, gl1, y_attn, y_ssm], [b_gate],
        [_sds((s, D_MODEL), BF16), _sds((s, D_MODEL), BF16), _sds((s, 2 * D_MODEL), BF16)],
        [_sds((1, D_MODEL)), _sds((1, D_MODEL))], tm=256, name="gate_bwd")

    g_w_ssm_br = _mm_cols_tn(y_glu, d_y_ssm, ns=D_MODEL // N_CHIPS, name="g_w_ssm_br")
    d_y_glu = _mm_cols_nt(d_y_ssm, wts["w_ssm_br"], tm=s, name="d_y_glu")

    def glu_bwd(dy, ga, gb):
        sg = jax.nn.sigmoid(gb)
        return jnp.concatenate([dy * sg, dy * ga * sg * (1.0 - sg)], axis=1)

    (d_glu,) = _rowwise(glu_bwd, [d_y_glu, (glu, SSM_WIDTH, 0), (glu, SSM_WIDTH, 1)], [],
                        [_sds((s, 2 * SSM_WIDTH), BF16)], tm=512, name="glu_bwd")
    g_w_glu = _mm_cols_tn(gel, d_glu, ns=2 * SSM_WIDTH // N_CHIPS, name="g_w_glu")
    d_gel = _mm_cols_nt(d_glu, wts["w_glu"], tm=s, name="d_gel")

    def gelu_bwd(dg, y, u, dsk):
        th = jnp.tanh(GELU_C * (y + GELU_K * y * y * y))
        dy = dg * (0.5 * (1.0 + th) + 0.5 * y * (1.0 - th * th) * GELU_C * (1.0 + 3.0 * GELU_K * y * y))
        return dy, dy * dsk, _colsum(dy * u)

    d_y, d_u_skip, d_ssm_d = _rowwise(gelu_bwd, [d_gel, y_s5, u_f], [d_skip],
                                      [_sds((s, SSM_WIDTH), BF16), _sds((s, SSM_WIDTH))], [_sds((1, SSM_WIDTH))],
                                      tm=512, name="gelu_bwd")
    g_h_r = _mm_plain(d_y, c_out_r, tm=s, tn=512, dims=(1, 1), name="ssm_gh_re")
    g_h_i = _mm_plain(d_y, c_out_ni, tm=s, tn=512, dims=(1, 1), name="ssm_gh_im")
    d_c_r = _mm_plain(h_r, d_y, tm=512, tn=SSM_WIDTH, dims=(0, 0), name="ssm_dc_re")
    d_c_ni = _mm_plain(h_i, d_y, tm=512, tn=SSM_WIDTH, dims=(0, 0), name="ssm_dc_im")
    lam_r, lam_i, d_abar_r, d_abar_i = _scan_bwd(g_h_r, g_h_i, h_r, h_i, e_r, e_i, a_r, a_i)
    d_bin_r = _mm_plain(u_p, lam_r, tm=SSM_WIDTH, tn=512, dims=(0, 0), name="ssm_db_re")
    d_bin_i = _mm_plain(u_p, lam_i, tm=SSM_WIDTH, tn=512, dims=(0, 0), name="ssm_db_im")
    d_u_r = _mm_plain(lam_r, b_in_r, tm=s, tn=SSM_WIDTH, tk=512, dims=(1, 1), name="ssm_du_re")
    d_u_i = _mm_plain(lam_i, b_in_i, tm=s, tn=SSM_WIDTH, tk=512, dims=(1, 1), name="ssm_du_im")

    def add3(a, b, c):
        return a + b + c

    (d_u,) = _rowwise(add3, [d_u_skip, d_u_r, d_u_i], [], [_sds((s, SSM_WIDTH), BF16)], tm=512, name="ssm_du")
    d_bbar_r = _diag_blocks(d_bin_r, SSM_GROUP, SSM_STATE).transpose(0, 2, 1)
    d_bbar_i = _diag_blocks(d_bin_i, SSM_GROUP, SSM_STATE).transpose(0, 2, 1)
    d_a_re, d_a_im, d_log_dt, d_b_re, d_b_im = ssm_vjp(
        (d_abar_r.reshape(SSM_GROUPS, SSM_STATE), d_abar_i.reshape(SSM_GROUPS, SSM_STATE), d_bbar_r, d_bbar_i))
    d_c_re = _diag_blocks(d_c_r, SSM_STATE, SSM_GROUP).transpose(0, 2, 1)
    d_c_im = -_diag_blocks(d_c_ni, SSM_STATE, SSM_GROUP).transpose(0, 2, 1)

    g_w_attn_br = _mm_cols_tn(attn, d_y_attn, ns=D_MODEL // N_CHIPS, name="g_w_attn_br")
    d_attn = _mm_cols_nt(d_y_attn, wts["w_attn_br"], tm=s, name="d_attn")
    dqkv = [_attention_bwd(g, proj, cos_f, sin_s, d_attn, attn, lse) for g in range(len(DILATIONS))]

    d_proj = jnp.concatenate([dqkv[g][j] for j in range(3) for g in range(len(DILATIONS))] + [d_u, d_gl],
                             axis=1)
    g_w_in = _mm_cols_tn(x, d_proj, ns=IN_WIDTH // N_CHIPS, name="g_w_in")
    dx_proj = _mm_cols_nt(d_proj, wts["w_in"], tm=1024, name="dx_proj", after=(g_w_in,))

    def dx_sum(dz, dxp):
        return DN_ALPHA * dz + dxp

    (grad_x,) = _rowwise(dx_sum, [dz1, dx_proj], [], [_sds((s, D_MODEL))], tm=512, name="grad_x")

    big = {"w_in": g_w_in, "w_attn_br": g_w_attn_br, "w_ssm_br": g_w_ssm_br, "w_out": g_w_out, "w_glu": g_w_glu,
           "w_ff_gate": g_w_ff_gate, "w_ff_up": g_w_ff_up, "w_ff_down": g_w_ff_down}
    small_g = {"b_gate": jnp.concatenate([d_bg0, d_bg1], axis=0), "ssm_a_re": d_a_re, "ssm_a_im": d_a_im,
               "ssm_log_dt": d_log_dt, "ssm_b_re": d_b_re, "ssm_b_im": d_b_im, "ssm_c_re": d_c_re, "ssm_c_im": d_c_im,
               "ssm_d": d_ssm_d.reshape(SSM_WIDTH), "ln1_g": d_ln1_g, "ln1_b": d_ln1_b, "ln2_g": d_ln2_g,
               "ln2_b": d_ln2_b}
    marks = {"ln1_bwd": dz1, "scan_bwd": lam_r, "attention_bwd_0": dqkv[0][0], "dx_proj": dx_proj}
    return loss_v[0, 0], grad_x, big, small_g, marks


GATHER_ID, SWAP_ID, SCATTER_ID, JOIN_ID = 1, 2, 3, 4


def _place():
    return lax.axis_index("x"), lax.axis_index("y"), lax.axis_index("c")


def _other_chips(x, y):
    return [(1 - x, y), (x, 1 - y), (1 - x, 1 - y)]


def _handshake(peers):
    barrier = pltpu.get_barrier_semaphore()
    for peer in peers:
        pl.semaphore_signal(barrier, inc=1, device_id=peer, device_id_type=MESH)
    pl.semaphore_wait(barrier, len(peers))


def _sequencer(body, arrays, out_type, sems, collective_id, name):
    return pl.kernel(body, name=name, out_type=out_type,
                     mesh=plsc.ScalarSubcoreMesh(axis_name="sequencer", num_cores=1), scratch_types=sems,
                     compiler_params=pltpu.CompilerParams(collective_id=collective_id))(*arrays)


def _gather_weights(shards, *, name):
    nw = len(shards)

    def body(*refs):
        ins, outs = refs[:nw], refs[nw:2 * nw]
        send_sems, recv_sems, pass_send, pass_recv, local_sems = refs[2 * nw:]
        x, y, c = _place()
        chip = 2 * x + y
        chips = _other_chips(x, y)
        _handshake([(x, y, 1 - c)] + [(cx, cy, c) for cx, cy in chips])
        started = []
        for w in range(nw):
            hw = shards[w].shape[0] // 2
            mine = pl.ds(c * hw, hw)
            own = pltpu.make_async_copy(ins[w], outs[w].at[chip], local_sems.at[w])
            own.start()
            started.append(own)
            for j, (cx, cy) in enumerate(chips):
                cp = pltpu.make_async_remote_copy(
                    src_ref=ins[w].at[mine], dst_ref=outs[w].at[chip, mine], send_sem=send_sems.at[w, j],
                    recv_sem=recv_sems.at[w, j], device_id=(cx, cy, c), device_id_type=MESH)
                cp.start()
                started.append(cp)
        passed = []
        for w in range(nw):
            hw = shards[w].shape[0] // 2
            mine = pl.ds(c * hw, hw)
            for j, (cx, cy) in enumerate(chips):
                landed = outs[w].at[2 * cx + cy, mine]
                pltpu.make_async_remote_copy(
                    src_ref=ins[w].at[mine], dst_ref=landed, send_sem=send_sems.at[w, j],
                    recv_sem=recv_sems.at[w, j], device_id=(cx, cy, c), device_id_type=MESH).wait_recv()
                cp = pltpu.make_async_remote_copy(
                    src_ref=landed, dst_ref=landed, send_sem=pass_send.at[w, j], recv_sem=pass_recv.at[w, j],
                    device_id=(x, y, 1 - c), device_id_type=MESH)
                cp.start()
                passed.append(cp)
        for w in range(nw):
            hw = shards[w].shape[0] // 2
            theirs = pl.ds((1 - c) * hw, hw)
            for j, (cx, cy) in enumerate(chips):
                landed = outs[w].at[2 * cx + cy, theirs]
                pltpu.make_async_remote_copy(
                    src_ref=landed, dst_ref=landed, send_sem=pass_send.at[w, j], recv_sem=pass_recv.at[w, j],
                    device_id=(x, y, 1 - c), device_id_type=MESH).wait_recv()
        for cp in started[0::4]:
            cp.wait()
        for cp in [s for i, s in enumerate(started) if i % 4] + passed:
            cp.wait_send()

    sem = pltpu.SemaphoreType.DMA
    return _sequencer(body, shards, [_sds((N_CHIPS,) + a.shape, a.dtype) for a in shards],
                      [sem((nw, 3)), sem((nw, 3)), sem((nw, 3)), sem((nw, 3)), sem((nw,))], GATHER_ID, name)


def _swap_other_halves(grads, *, name):
    nw = len(grads)

    def body(*refs):
        ins, outs = refs[:nw], refs[nw:2 * nw]
        send_sems, recv_sems = refs[2 * nw:]
        x, y, c = _place()
        _handshake([(x, y, 1 - c)])
        cps = []
        for w in range(nw):
            hw = grads[w].shape[1] // 2
            cp = pltpu.make_async_remote_copy(
                src_ref=ins[w].at[:, pl.ds((1 - c) * hw, hw)], dst_ref=outs[w], send_sem=send_sems.at[w],
                recv_sem=recv_sems.at[w], device_id=(x, y, 1 - c), device_id_type=MESH)
            cp.start()
            cps.append(cp)
        for cp in cps:
            cp.wait()

    sem = pltpu.SemaphoreType.DMA
    return _sequencer(body, grads, [_sds((N_CHIPS, g.shape[1] // 2, g.shape[2]), g.dtype) for g in grads],
                      [sem((nw,)), sem((nw,))], SWAP_ID, name)


def _add_my_half(core, g, other, after=()):
    n, r, cols = g.shape
    hw = r // 2

    def body(core_ref, g_ref, o_ref, *rest):
        out_ref = rest[len(after)]
        out_ref[...] = (g_ref[...].astype(F32) + o_ref[...].astype(F32)).astype(out_ref.dtype)

    return pl.pallas_call(
        body,
        grid_spec=pltpu.PrefetchScalarGridSpec(
            num_scalar_prefetch=1, grid=(n,),
            in_specs=[pl.BlockSpec((None, None, hw, cols), lambda s, core_ref: (s, core_ref[0], 0, 0)),
                      pl.BlockSpec((None, hw, cols), lambda s, core_ref: (s, 0, 0))] + [HBM_OPERAND] * len(after),
            out_specs=pl.BlockSpec((None, hw, cols), lambda s, core_ref: (s, 0, 0))),
        out_shape=_sds((n, hw, cols), BF16), compiler_params=_cp(("parallel",)),
        name="add_my_half")(core, g.reshape(n, 2, hw, cols), other, *after)


def _scatter_partials(parts, *, name):
    nw = len(parts)

    def body(*refs):
        ins, outs = refs[:nw], refs[nw:2 * nw]
        send_sems, recv_sems = refs[2 * nw:]
        x, y, c = _place()
        _handshake([(cx, cy, c) for cx, cy in _other_chips(x, y)])
        cps = []
        for w in range(nw):
            for j, (cx, cy) in enumerate(_other_chips(x, y)):
                cp = pltpu.make_async_remote_copy(
                    src_ref=ins[w].at[2 * cx + cy], dst_ref=outs[w].at[j], send_sem=send_sems.at[w, j],
                    recv_sem=recv_sems.at[w, j], device_id=(cx, cy, c), device_id_type=MESH)
                cp.start()
                cps.append(cp)
        for cp in cps:
            cp.wait()

    sem = pltpu.SemaphoreType.DMA
    return _sequencer(body, parts, [_sds((3,) + p.shape[1:], p.dtype) for p in parts],
                      [sem((nw, 3)), sem((nw, 3))], SCATTER_ID, name)


def _sum_partials(chip, part, recv, after=()):
    _, hw, cols = part.shape
    th = hw // 2 if hw % 32 == 0 else hw

    def body(chip_ref, p_ref, r_ref, *rest):
        out_ref = rest[len(after)]
        acc = p_ref[...].astype(F32)
        for j in range(3):
            acc = acc + r_ref[j].astype(F32)
        out_ref[...] = acc

    return pl.pallas_call(
        body,
        grid_spec=pltpu.PrefetchScalarGridSpec(
            num_scalar_prefetch=1, grid=(hw // th,),
            in_specs=[pl.BlockSpec((None, th, cols), lambda i, chip_ref: (chip_ref[0], i, 0)),
                      pl.BlockSpec((3, th, cols), lambda i, chip_ref: (0, i, 0))] + [HBM_OPERAND] * len(after),
            out_specs=pl.BlockSpec((th, cols), lambda i, chip_ref: (i, 0))),
        out_shape=_sds((hw, cols)), compiler_params=_cp(("parallel",)), name="sum_partials")(
            chip, part, recv, *after)


def _swap_reduced_halves(halves, *, name):
    nw = len(halves)

    def body(*refs):
        ins, outs = refs[:nw], refs[nw:2 * nw]
        send_sems, recv_sems = refs[2 * nw:]
        x, y, c = _place()
        _handshake([(x, y, 1 - c)])
        cps = []
        for w in range(nw):
            cp = pltpu.make_async_remote_copy(
                src_ref=ins[w], dst_ref=outs[w], send_sem=send_sems.at[w], recv_sem=recv_sems.at[w],
                device_id=(x, y, 1 - c), device_id_type=MESH)
            cp.start()
            cps.append(cp)
        for cp in cps:
            cp.wait()

    sem = pltpu.SemaphoreType.DMA
    return _sequencer(body, halves, [_sds(h.shape, h.dtype) for h in halves], [sem((nw,)), sem((nw,))], JOIN_ID, name)


def _allreduce_rows(vec, *, name, after=()):
    rows = vec.shape[0]

    def body(v_ref, *rest):
        out_ref, slots, send_sems, recv_sems = rest[len(after):]
        x, y, c = _place()
        me = 4 * x + 2 * y + c
        slots[me] = v_ref[...]
        peers = []
        for mask in range(1, N_DEV):
            px = 1 - x if mask & 4 else x
            py = 1 - y if mask & 2 else y
            pc = 1 - c if mask & 1 else c
            peers.append((px, py, pc))
        cps = []
        for k, peer in enumerate(peers):
            cp = pltpu.make_async_remote_copy(
                src_ref=v_ref, dst_ref=slots.at[me], send_sem=send_sems.at[k], recv_sem=recv_sems.at[k],
                device_id=peer, device_id_type=MESH)
            cp.start()
            cps.append(cp)
        for k, (px, py, pc) in enumerate(peers):
            pltpu.make_async_remote_copy(
                src_ref=v_ref, dst_ref=slots.at[4 * px + 2 * py + pc], send_sem=send_sems.at[k],
                recv_sem=recv_sems.at[k], device_id=(px, py, pc), device_id_type=MESH).wait_recv()
        for cp in cps:
            cp.wait_send()
        acc = slots[0]
        for d in range(1, N_DEV):
            acc = acc + slots[d]
        out_ref[...] = acc

    vmem = pl.BlockSpec(memory_space=pltpu.VMEM)
    return pl.pallas_call(
        body, in_specs=[vmem] + [HBM_OPERAND] * len(after), out_specs=vmem, out_shape=_sds((rows, 128)),
        scratch_shapes=[pltpu.VMEM((N_DEV, rows, 128), F32), pltpu.SemaphoreType.DMA((N_DEV - 1,)),
                        pltpu.SemaphoreType.DMA((N_DEV - 1,))],
        compiler_params=pltpu.CompilerParams(vmem_limit_bytes=VMEM_LIMIT_BYTES), name=name)(vec, *after)


def _reduce_scatter_start(grads, core, *, tag, add_after=()):
    others = _swap_other_halves(grads, name="swap_other_halves_" + tag)
    parts = [_add_my_half(core, g, o, add_after) for g, o in zip(grads, others)]
    return parts, _scatter_partials(parts, name="scatter_partials_" + tag)


def _reduce_scatter_finish(parts, recvd, chip, *, tag, sum_after=()):
    mine = [_sum_partials(chip, p, r, sum_after) for p, r in zip(parts, recvd)]
    return mine, _swap_reduced_halves(mine, name="swap_reduced_halves_" + tag)


ADAM_BLOCK_ELEMS = 256 * 1024


def _adam_rows(rows, cols):
    tm = rows
    while tm * cols > ADAM_BLOCK_ELEMS and tm % 16 == 0:
        tm //= 2
    return tm


def _adam_step(wv, gv, mv, vv):
    m2 = ADAM_B1 * mv + (1.0 - ADAM_B1) * gv
    v2 = ADAM_B2 * vv + (1.0 - ADAM_B2) * (gv * gv)
    m_hat = m2 / (1.0 - ADAM_B1 ** ADAM_STEP)
    v_hat = v2 / (1.0 - ADAM_B2 ** ADAM_STEP)
    return -ADAM_LR * (m_hat / (jnp.sqrt(v_hat) + ADAM_EPS) + ADAM_WD * wv), m2, v2


def _adamw(w, g, m, v, *, name):
    rows, cols = w.shape
    return _rowwise(_adam_step, [w, g, m, v], [], [_sds((rows, cols))] * 3, tm=_adam_rows(rows, cols), name=name)


def _adamw_halves(core, w, g_mine, g_theirs, m, v, *, name, after=()):
    rows, cols = w.shape
    hw = rows // 2
    tm = _adam_rows(hw, cols)
    per_half = hw // tm

    def body(core_ref, w_ref, gm_ref, gt_ref, m_ref, v_ref, *rest):
        g_out, d_out, m_out, v_out = rest[len(after):]
        mine = (pl.program_id(0) // per_half) == core_ref[0]
        g = jnp.where(mine, gm_ref[...], gt_ref[...])
        d, m2, v2 = _adam_step(w_ref[...], g, m_ref[...], v_ref[...])
        g_out[...] = g
        d_out[...] = d
        m_out[...] = m2
        v_out[...] = v2

    full = pl.BlockSpec((tm, cols), lambda i, core_ref: (i, 0))
    half = pl.BlockSpec((tm, cols), lambda i, core_ref: (i % per_half, 0))
    return pl.pallas_call(
        body,
        grid_spec=pltpu.PrefetchScalarGridSpec(
            num_scalar_prefetch=1, grid=(rows // tm,),
            in_specs=[full, half, half, full, full] + [HBM_OPERAND] * len(after), out_specs=[full, full, full, full]),
        out_shape=[_sds((rows, cols))] * 4, compiler_params=_cp(("parallel",)), name=name)(
            core, w, g_mine, g_theirs, m, v, *after)


def _pack_rows(arrs):
    flat = jnp.concatenate([a.reshape(-1).astype(F32) for a in arrs])
    rows = -(-flat.shape[0] // 1024) * 8
    return jnp.pad(flat, (0, rows * 128 - flat.shape[0])).reshape(rows, 128)


def _unpack_rows(vec, shapes):
    flat = vec.reshape(-1)
    out, off = [], 0
    for shp in shapes:
        size = math.prod(shp)
        out.append(flat[off:off + size].reshape(shp))
        off += size
    return out


SMALL = ("b_gate", "ssm_a_re", "ssm_a_im", "ssm_log_dt", "ssm_b_re", "ssm_b_im", "ssm_c_re", "ssm_c_im", "ssm_d",
         "ln1_g", "ln1_b", "ln2_g", "ln2_b")
GATHER_GROUPS = (("w_in", ("w_in",)), ("mixer", ("w_attn_br", "w_ssm_br", "w_glu", "w_out")),
                 ("ffn", ("w_ff_gate", "w_ff_up", "w_ff_down")))
REDUCE_GROUPS = (("ffn", ("w_ff_down", "w_ff_gate", "w_ff_up")),
                 ("mixer", ("w_out", "w_ssm_br", "w_glu", "w_attn_br")), ("w_in", ("w_in",)))
WEIGHTS = ("w_in", "b_gate", "w_attn_br", "w_ssm_br", "w_out", "ssm_a_re", "ssm_a_im", "ssm_log_dt", "ssm_b_re",
           "ssm_b_im", "ssm_c_re", "ssm_c_im", "ssm_d", "w_glu", "ln1_g", "ln1_b", "w_ff_gate", "w_ff_up", "w_ff_down",
           "ln2_g", "ln2_b")


def kernel(x, w_in, b_gate, w_attn_br, w_ssm_br, w_out, ssm_a_re, ssm_a_im, ssm_log_dt, ssm_b_re, ssm_b_im, ssm_c_re, ssm_c_im, ssm_d, w_glu, ln1_g, ln1_b, w_ff_gate, w_ff_up, w_ff_down, ln2_g, ln2_b, loss_target, m_w_in, m_b_gate, m_w_attn_br, m_w_ssm_br, m_w_out, m_ssm_a_re, m_ssm_a_im, m_ssm_log_dt, m_ssm_b_re, m_ssm_b_im, m_ssm_c_re, m_ssm_c_im, m_ssm_d, m_w_glu, m_ln1_g, m_ln1_b, m_w_ff_gate, m_w_ff_up, m_w_ff_down, m_ln2_g, m_ln2_b, v_w_in, v_b_gate, v_w_attn_br, v_w_ssm_br, v_w_out, v_ssm_a_re, v_ssm_a_im, v_ssm_log_dt, v_ssm_b_re, v_ssm_b_im, v_ssm_c_re, v_ssm_c_im, v_ssm_d, v_w_glu, v_ln1_g, v_ln1_b, v_w_ff_gate, v_w_ff_up, v_w_ff_down, v_ln2_g, v_ln2_b):
    given = dict(locals())
    px, py, pc = _place()
    chip = 2 * px + py
    core_s = jnp.reshape(pc, (1,)).astype(jnp.int32)
    chip_s = jnp.reshape(chip, (1,)).astype(jnp.int32)

    wts = {}
    for tag, names in GATHER_GROUPS:
        wts.update(zip(names, _gather_weights([given[n][0].astype(BF16) for n in names], name="gather_" + tag)))
    ncol = D_MODEL // N_CHIPS
    bg_mine = jnp.where(pc == 0, b_gate[0], jnp.zeros_like(b_gate[0]))
    bg_full = lax.dynamic_update_slice(jnp.zeros((2, D_MODEL), F32), bg_mine, (0, chip * ncol))
    bg_full = _allreduce_rows(bg_full.reshape(16, 128), name="gather_gate_bias").reshape(2, D_MODEL)
    small = {n: given[n][0] for n in SMALL if n.startswith("ssm")}
    small.update({n: given[n] for n in ("ln1_g", "ln1_b", "ln2_g", "ln2_b")})
    small["b_gate"] = bg_full

    loss_mine, grad_x, big_g, small_g, marks = _local_step(x[0], loss_target[0], wts, small)
    loss = lax.psum(loss_mine, ("x", "y", "c"))

    groups = dict(REDUCE_GROUPS)
    add_after = {"ffn": (marks["ln1_bwd"],), "mixer": (marks["scan_bwd"],), "w_in": (marks["dx_proj"],)}
    parts, recvd = {}, {}
    for tag, names in REDUCE_GROUPS:
        parts[tag], recvd[tag] = _reduce_scatter_start([big_g[n] for n in names], core_s, tag=tag,
                                                       add_after=add_after[tag])
    grads, delta, new_m, new_v = {}, {}, {}, {}

    def finish(tag, sum_after, adam_after):
        mine, theirs = _reduce_scatter_finish(parts[tag], recvd[tag], chip_s, tag=tag, sum_after=sum_after)
        for n, g_mine, g_theirs in zip(groups[tag], mine, theirs):
            shp = given[n].shape
            two = (shp[1], shp[2])
            res = _adamw_halves(core_s, given[n].reshape(two), g_mine, g_theirs, given["m_" + n].reshape(two),
                                given["v_" + n].reshape(two), name="adamw_" + n, after=adam_after)
            grads[n], delta[n], new_m[n], new_v[n] = [r.reshape(shp) for r in res]

    in_flight = (parts["w_in"][0],)
    finish("ffn", (marks["scan_bwd"],), in_flight)
    finish("mixer", (marks["attention_bwd_0"],), in_flight)
    shapes = [small_g[n].shape for n in SMALL]
    summed = _unpack_rows(_allreduce_rows(_pack_rows([small_g[n] for n in SMALL]), name="allreduce_small",
                                          after=in_flight), shapes)
    for n, g in zip(SMALL, summed):
        if n == "b_gate":
            g = lax.dynamic_slice(g, (0, chip * ncol), (2, ncol))
        grads[n] = g.reshape(given[n].shape)
    shapes = [given[n].shape for n in SMALL]
    packed = [_pack_rows([src[n] for n in SMALL]) for src in
              (given, grads, {n: given["m_" + n] for n in SMALL}, {n: given["v_" + n] for n in SMALL})]
    small_out = _adamw(*packed, name="adamw_small")
    for out, vec in zip((delta, new_m, new_v), small_out):
        out.update(zip(SMALL, _unpack_rows(vec, shapes)))
    behind = [delta[n] for tag in ("ffn", "mixer") for n in groups[tag]] + [small_out[0], grad_x]
    finish("w_in", tuple(behind), ())

    return (loss, grad_x.reshape(x.shape), *[grads[n] for n in WEIGHTS], *[delta[n] for n in WEIGHTS],
            *[new_m[n] for n in WEIGHTS], *[new_v[n] for n in WEIGHTS])
```

```python
import math

import jax
import jax.numpy as jnp
from jax import lax
from jax.experimental import pallas as pl
from jax.experimental.pallas import tpu as pltpu
from jax.experimental.pallas import tpu_sc as plsc

F32 = jnp.float32
BF16 = jnp.bfloat16
MESH = pl.DeviceIdType.MESH

D_MODEL = 1024
SEQ = 2048
HEAD_DIM = 64
ATTN_HEADS = 8
DILATIONS = (1, 4, 16)
ATTN_WIDTH = ATTN_HEADS * HEAD_DIM
QKV_WIDTH = 3 * ATTN_WIDTH
BLOCK = 128
ROPE_THETA = 10000.0
NEG_INF = -1e30
SSM_GROUP = 16
SSM_GROUPS = 32
SSM_WIDTH = 512
SSM_STATE = 64
SSM_LANES = SSM_GROUPS * SSM_STATE
SCAN_CHUNKS = 8
SCAN_STEPS = SEQ // SCAN_CHUNKS
SCAN_LANES = 256
IN_WIDTH = 3 * QKV_WIDTH + SSM_WIDTH + 2 * D_MODEL
D_FF = 2816
N_CHIPS = 4
N_DEV = 8
DN_ALPHA = 2.0 ** 0.25
LN_EPS = 1e-5
ADAM_LR = 0.001
ADAM_B1 = 0.9
ADAM_B2 = 0.999
ADAM_EPS = 1e-08
ADAM_WD = 0.01
ADAM_STEP = 10
GELU_C = math.sqrt(2.0 / math.pi)
GELU_K = 0.044715

VMEM_LIMIT_BYTES = 56 * 1024 * 1024


def _sds(shape, dtype=F32):
    return jax.ShapeDtypeStruct(tuple(shape), dtype)


def _cp(semantics=None):
    return pltpu.CompilerParams(dimension_semantics=semantics, vmem_limit_bytes=VMEM_LIMIT_BYTES)


HBM_OPERAND = pl.BlockSpec(memory_space=pl.ANY)


def _matmul(a, b, *, grid, a_spec, b_spec, o_spec, out_shape, dims, k_axis=None, name, after=()):
    nk = grid[k_axis] if k_axis is not None else 1
    o_block = tuple(d for d in o_spec.block_shape if d is not None)
    n_after = len(after)

    def body(a_ref, b_ref, *rest):
        o_ref, acc = rest[n_after], rest[n_after + 1:]
        part = lax.dot_general(a_ref[...].astype(BF16), b_ref[...].astype(BF16),
                               (((dims[0],), (dims[1],)), ((), ())), preferred_element_type=F32)
        if k_axis is None:
            o_ref[...] = part.astype(o_ref.dtype)
        else:
            k = pl.program_id(k_axis)

            @pl.when(k == 0)
            def _():
                acc[0][...] = part

            @pl.when(k > 0)
            def _():
                acc[0][...] += part

            @pl.when(k == nk - 1)
            def _():
                o_ref[...] = acc[0][...].astype(o_ref.dtype)

    sem = tuple("arbitrary" if ax == k_axis else "parallel" for ax in range(len(grid)))
    return pl.pallas_call(
        body, grid=grid, in_specs=[a_spec, b_spec] + [HBM_OPERAND] * n_after, out_specs=o_spec, out_shape=out_shape,
        scratch_shapes=[pltpu.VMEM(o_block, F32)] if k_axis is not None else [],
        compiler_params=_cp(sem), name=name)(a, b, *after)


def _mm_cols(a, wg, *, tm, name, out_dtype=F32, out3d=False):
    m, k = a.shape
    ns = wg.shape[2]
    if out3d:
        o_spec = pl.BlockSpec((None, tm, ns), lambda i, s: (s, i, 0))
        out_shape = _sds((N_CHIPS, m, ns), out_dtype)
    else:
        o_spec = pl.BlockSpec((tm, ns), lambda i, s: (i, s))
        out_shape = _sds((m, N_CHIPS * ns), out_dtype)
    return _matmul(a, wg, grid=(m // tm, N_CHIPS),
                   a_spec=pl.BlockSpec((tm, k), lambda i, s: (i, 0)),
                   b_spec=pl.BlockSpec((None, k, ns), lambda i, s: (s, 0, 0)),
                   o_spec=o_spec, out_shape=out_shape, dims=(1, 0), name=name)


def _mm_cols_nt(dy, wg, *, tm, name, dy3d=False, out_dtype=F32, after=()):
    k, ns = wg.shape[1], wg.shape[2]
    if dy3d:
        m = dy.shape[1]
        a_spec = pl.BlockSpec((None, tm, ns), lambda i, s: (s, i, 0))
    else:
        m = dy.shape[0]
        a_spec = pl.BlockSpec((tm, ns), lambda i, s: (i, s))
    return _matmul(dy, wg, grid=(m // tm, N_CHIPS), a_spec=a_spec,
                   b_spec=pl.BlockSpec((None, k, ns), lambda i, s: (s, 0, 0)),
                   o_spec=pl.BlockSpec((tm, k), lambda i, s: (i, 0)),
                   out_shape=_sds((m, k), out_dtype), dims=(1, 1), k_axis=1, name=name, after=after)


def _mm_cols_tn(a, dy, *, ns, name, dy3d=False):
    m, k = a.shape
    if dy3d:
        b_spec = pl.BlockSpec((None, m, ns), lambda s: (s, 0, 0))
    else:
        b_spec = pl.BlockSpec((m, ns), lambda s: (0, s))
    return _matmul(a, dy, grid=(N_CHIPS,), a_spec=pl.BlockSpec((m, k), lambda s: (0, 0)), b_spec=b_spec,
                   o_spec=pl.BlockSpec((None, k, ns), lambda s: (s, 0, 0)),
                   out_shape=_sds((N_CHIPS, k, ns), BF16), dims=(0, 0), name=name)


def _mm_plain(a, b, *, tm, tn, name, out_dtype=F32, dims=(1, 0), tk=None):
    m = a.shape[1 - dims[0]]
    kk = a.shape[dims[0]]
    n = b.shape[1 - dims[1]]
    tk = kk if tk is None else tk
    nk = kk // tk

    def a_idx(i, j, k):
        return (i, k) if dims[0] == 1 else (k, i)

    def b_idx(i, j, k):
        return (k, j) if dims[1] == 0 else (j, k)

    a_blk = (tm, tk) if dims[0] == 1 else (tk, tm)
    b_blk = (tk, tn) if dims[1] == 0 else (tn, tk)
    return _matmul(a, b, grid=(m // tm, n // tn, nk),
                   a_spec=pl.BlockSpec(a_blk, a_idx), b_spec=pl.BlockSpec(b_blk, b_idx),
                   o_spec=pl.BlockSpec((tm, tn), lambda i, j, k: (i, j)),
                   out_shape=_sds((m, n), out_dtype), dims=dims, k_axis=2 if nk > 1 else None, name=name)


def _rowwise(fn, tiled, full, outs, accs=(), *, tm, name, after=()):
    args, in_specs = [], []
    for t in tiled:
        if isinstance(t, tuple):
            arr, w, cb = t
            in_specs.append(pl.BlockSpec((tm, w), lambda i, cb=cb: (i, cb)))
        else:
            arr = t
            in_specs.append(pl.BlockSpec((tm, arr.shape[1]), lambda i: (i, 0)))
        args.append(arr)
    rows = args[0].shape[0]
    for f in full:
        in_specs.append(pl.BlockSpec(f.shape, lambda i, nd=f.ndim: (0,) * nd))
        args.append(f)
    out_specs = [pl.BlockSpec((tm, o.shape[1]), lambda i: (i, 0)) for o in outs]
    out_specs += [pl.BlockSpec(a.shape, lambda i, nd=len(a.shape): (0,) * nd) for a in accs]
    n_in, n_out = len(args), len(outs)
    in_specs += [HBM_OPERAND] * len(after)
    first_out = n_in + len(after)

    def body(*refs):
        res = fn(*[r[...] for r in refs[:n_in]])
        res = res if isinstance(res, (tuple, list)) else (res,)
        for r, v in zip(refs[first_out:first_out + n_out], res[:n_out]):
            r[...] = v.astype(r.dtype)
        i = pl.program_id(0)
        for r, v in zip(refs[first_out + n_out:], res[n_out:]):
            @pl.when(i == 0)
            def _(r=r, v=v):
                r[...] = v

            @pl.when(i > 0)
            def _(r=r, v=v):
                r[...] += v

    res = pl.pallas_call(
        body, grid=(rows // tm,), in_specs=in_specs, out_specs=out_specs, out_shape=list(outs) + list(accs),
        compiler_params=_cp(("arbitrary",) if accs else ("parallel",)), name=name)(*args, *after)
    return res


def _colsum(v):
    return jnp.sum(v, axis=0, keepdims=True)


def _ln_stats(z):
    mu = jnp.mean(z, axis=-1, keepdims=True)
    zc = z - mu
    var = jnp.mean(zc * zc, axis=-1, keepdims=True)
    rstd = lax.rsqrt(var + LN_EPS)
    return zc * rstd, rstd


def _ln_bwd(dy, xhat, rstd, g):
    dxh = dy * g
    m1 = jnp.mean(dxh, axis=-1, keepdims=True)
    m2 = jnp.mean(dxh * xhat, axis=-1, keepdims=True)
    return rstd * (dxh - m1 - xhat * m2)


def _swap_halves(t):
    w = t.shape[-1]
    lane = lax.broadcasted_iota(jnp.int32, t.shape, t.ndim - 1)
    return jnp.where((lane % HEAD_DIM) < HEAD_DIM // 2, pltpu.roll(t, w - HEAD_DIM // 2, t.ndim - 1),
                     pltpu.roll(t, HEAD_DIM // 2, t.ndim - 1))


def _band_mask(i):
    row = lax.broadcasted_iota(jnp.int32, (BLOCK, 2 * BLOCK), 0)
    col = lax.broadcasted_iota(jnp.int32, (BLOCK, 2 * BLOCK), 1)
    dist = BLOCK + row - col
    return (dist >= 0) & (dist <= BLOCK) & ((col >= BLOCK) | (i > 0))


PAIR = 2 * HEAD_DIM
UNITS = SEQ // BLOCK
ROPE_ROWS = 256


def _rope(t, cf, ss):
    return t * cf + _swap_halves(t) * ss


def _rope_transposed(d, cf, ss):
    return d * cf + _swap_halves(d * ss)


def _unit_rows(u, dil):
    if dil == 1:
        i = u
        start = pl.multiple_of(u * BLOCK, BLOCK)
        prev = pl.multiple_of(jnp.maximum(u - 1, 0) * BLOCK, BLOCK)
        return i, pl.ds(start, BLOCK), pl.ds(prev, BLOCK)
    rho = jnp.bitwise_and(u, dil - 1)
    i = jnp.right_shift(u, dil.bit_length() - 1)
    start = rho + dil * BLOCK * i
    prev = rho + dil * BLOCK * jnp.maximum(i - 1, 0)
    return i, pl.ds(start, BLOCK, stride=dil), pl.ds(prev, BLOCK, stride=dil)


def _causal_mask():
    row = lax.broadcasted_iota(jnp.int32, (BLOCK, BLOCK), 0)
    col = lax.broadcasted_iota(jnp.int32, (BLOCK, BLOCK), 1)
    return row >= col


def _pair_views(col0):
    return [pl.BlockSpec((SEQ, PAIR), lambda hp, g=g: (0, col0 // PAIR + g * (ATTN_WIDTH // PAIR) + hp))
            for g in range(len(DILATIONS))]


def _rotate_keys(k_refs, kr_refs, cf_ref, ss_ref):
    def step(t, carry):
        rows = pl.ds(pl.multiple_of(t * ROPE_ROWS, ROPE_ROWS), ROPE_ROWS)
        cf, ss = cf_ref[rows, :], ss_ref[rows, :]
        for k_ref, kr_ref in zip(k_refs, kr_refs):
            kr_ref[rows, :] = _rope(k_ref[rows, :], cf, ss)
        return carry

    lax.fori_loop(0, SEQ // ROPE_ROWS, step, 0)


def _attention_fwd(proj, cos_f, sin_s):
    ng = len(DILATIONS)

    def body(*refs):
        q_refs, k_refs, v_refs = refs[:ng], refs[ng:2 * ng], refs[2 * ng:3 * ng]
        cf_ref, ss_ref, attn_ref, lse_ref = refs[3 * ng:3 * ng + 4]
        kr_refs = refs[3 * ng + 4:]
        _rotate_keys(k_refs, kr_refs, cf_ref, ss_ref)
        first = lax.broadcasted_iota(jnp.int32, (BLOCK, PAIR), 1) < HEAD_DIM
        for g, dil in enumerate(DILATIONS):
            two_blocks = SEQ // dil > BLOCK

            def unit(u, carry, g=g, dil=dil, two_blocks=two_blocks):
                i, rows, prev = _unit_rows(u, dil)
                qq = (_rope(q_refs[g][rows, :], cf_ref[rows, :], ss_ref[rows, :]) * (1.0 / math.sqrt(HEAD_DIM))).astype(BF16)
                if two_blocks:
                    kk = jnp.concatenate([kr_refs[g][prev, :], kr_refs[g][rows, :]], axis=0).astype(BF16)
                    vv = jnp.concatenate([v_refs[g][prev, :], v_refs[g][rows, :]], axis=0).astype(BF16)
                    valid = _band_mask(i)
                else:
                    kk = kr_refs[g][rows, :].astype(BF16)
                    vv = v_refs[g][rows, :].astype(BF16)
                    valid = _causal_mask()
                zero = jnp.zeros_like(qq)
                outs, lses = [], []
                for qh in (jnp.where(first, qq, zero), jnp.where(first, zero, qq)):
                    s = lax.dot_general(qh, kk, (((1,), (1,)), ((), ())), preferred_element_type=F32)
                    s = jnp.where(valid, s, NEG_INF)
                    m = jnp.max(s, axis=1, keepdims=True)
                    p = jnp.exp(s - m)
                    l = jnp.sum(p, axis=1, keepdims=True)
                    outs.append(jnp.dot(p.astype(BF16), vv, preferred_element_type=F32) / l)
                    lses.append(m + jnp.log(l))
                o = jnp.where(first, outs[0], outs[1])
                lse = jnp.where(first, lses[0], lses[1])
                if g == 0:
                    attn_ref[rows, :] = o
                    lse_ref[rows, :] = lse
                else:
                    lse_old = lse_ref[rows, :]
                    m = jnp.maximum(lse_old, lse)
                    lse_new = m + jnp.log(jnp.exp(lse_old - m) + jnp.exp(lse - m))
                    attn_ref[rows, :] = attn_ref[rows, :] * jnp.exp(lse_old - lse_new) + o * jnp.exp(lse - lse_new)
                    lse_ref[rows, :] = lse_new
                return carry

            lax.fori_loop(0, UNITS, unit, 0)

    whole = pl.BlockSpec((SEQ, PAIR), lambda hp: (0, 0))
    out = pl.BlockSpec((SEQ, PAIR), lambda hp: (0, hp))
    return pl.pallas_call(
        body, grid=(ATTN_WIDTH // PAIR,),
        in_specs=_pair_views(0) + _pair_views(QKV_WIDTH) + _pair_views(2 * QKV_WIDTH) + [whole, whole],
        out_specs=[out, out], out_shape=[_sds((SEQ, ATTN_WIDTH)), _sds((SEQ, ATTN_WIDTH))],
        scratch_shapes=[pltpu.VMEM((SEQ, PAIR), F32)] * ng,
        compiler_params=_cp(("parallel",)), name="attention_fwd")(*([proj] * (3 * ng)), cos_f, sin_s)


def _attention_bwd(g, proj, cos_f, sin_s, d_attn, attn, lse):
    dil = DILATIONS[g]
    two_blocks = SEQ // dil > BLOCK

    def body(q_ref, k_ref, v_ref, cf_ref, ss_ref, do_ref, o_ref, lse_ref, dq_out, dk_out, dv_out,
             kr_ref, dq_acc, dk_acc, dv_acc):
        _rotate_keys([k_ref], [kr_ref], cf_ref, ss_ref)
        dk_acc[...] = jnp.zeros_like(dk_acc)
        dv_acc[...] = jnp.zeros_like(dv_acc)
        first = lax.broadcasted_iota(jnp.int32, (BLOCK, PAIR), 1) < HEAD_DIM
        nk = 2 * BLOCK if two_blocks else BLOCK
        first_k = lax.broadcasted_iota(jnp.int32, (nk, PAIR), 1) < HEAD_DIM

        def unit(u, carry):
            i, rows, prev = _unit_rows(u, dil)
            qq = (_rope(q_ref[rows, :], cf_ref[rows, :], ss_ref[rows, :]) * (1.0 / math.sqrt(HEAD_DIM))).astype(BF16)
            if two_blocks:
                kk = jnp.concatenate([kr_ref[prev, :], kr_ref[rows, :]], axis=0).astype(BF16)
                vv = jnp.concatenate([v_ref[prev, :], v_ref[rows, :]], axis=0).astype(BF16)
                valid = _band_mask(i)
            else:
                kk = kr_ref[rows, :].astype(BF16)
                vv = v_ref[rows, :].astype(BF16)
                valid = _causal_mask()
            dof = do_ref[rows, :]
            dd = dof * o_ref[rows, :]
            lse2 = lse_ref[rows, :]
            dob = dof.astype(BF16)
            zq, zd, zf = jnp.zeros_like(qq), jnp.zeros_like(dob), jnp.zeros_like(dd)
            dqs, dks, dvs = [], [], []
            for h in range(2):
                sel = first if h == 0 else jnp.logical_not(first)
                qh = jnp.where(sel, qq, zq)
                doh = jnp.where(sel, dob, zd)
                delta = jnp.sum(jnp.where(sel, dd, zf), axis=1, keepdims=True)
                lse_h = lse2[:, h * HEAD_DIM:h * HEAD_DIM + 1]
                s = lax.dot_general(qh, kk, (((1,), (1,)), ((), ())), preferred_element_type=F32)
                p = jnp.where(valid, jnp.exp(s - lse_h), 0.0)
                dp = lax.dot_general(doh, vv, (((1,), (1,)), ((), ())), preferred_element_type=F32)
                ds = (p * (dp - delta)).astype(BF16)
                dqs.append(jnp.dot(ds, kk, preferred_element_type=F32))
                dks.append(lax.dot_general(ds, qq, (((0,), (0,)), ((), ())), preferred_element_type=F32))
                dvs.append(lax.dot_general(p.astype(BF16), dob, (((0,), (0,)), ((), ())), preferred_element_type=F32))
            dq_acc[rows, :] = jnp.where(first, dqs[0], dqs[1])
            dk2 = jnp.where(first_k, dks[0], dks[1])
            dv2 = jnp.where(first_k, dvs[0], dvs[1])
            dk_acc[rows, :] += dk2[nk - BLOCK:]
            dv_acc[rows, :] += dv2[nk - BLOCK:]
            if two_blocks:
                @pl.when(i > 0)
                def _():
                    dk_acc[prev, :] += dk2[:BLOCK]
                    dv_acc[prev, :] += dv2[:BLOCK]
            return carry

        lax.fori_loop(0, UNITS, unit, 0)

        def finish(t, carry):
            rows = pl.ds(pl.multiple_of(t * ROPE_ROWS, ROPE_ROWS), ROPE_ROWS)
            cf, ss = cf_ref[rows, :], ss_ref[rows, :]
            dq = dq_acc[rows, :] * (1.0 / math.sqrt(HEAD_DIM))
            dq_out[rows, :] = _rope_transposed(dq, cf, ss).astype(BF16)
            dk_out[rows, :] = _rope_transposed(dk_acc[rows, :], cf, ss).astype(BF16)
            dv_out[rows, :] = dv_acc[rows, :].astype(BF16)
            return carry

        lax.fori_loop(0, SEQ // ROPE_ROWS, finish, 0)

    whole = pl.BlockSpec((SEQ, PAIR), lambda hp: (0, 0))
    pair = pl.BlockSpec((SEQ, PAIR), lambda hp: (0, hp))
    views = [_pair_views(col0)[g] for col0 in (0, QKV_WIDTH, 2 * QKV_WIDTH)]
    return pl.pallas_call(
        body, grid=(ATTN_WIDTH // PAIR,), in_specs=views + [whole, whole, pair, pair, pair],
        out_specs=[pair, pair, pair], out_shape=[_sds((SEQ, ATTN_WIDTH), BF16)] * 3,
        scratch_shapes=[pltpu.VMEM((SEQ, PAIR), F32)] * 4,
        compiler_params=_cp(("parallel",)), name=f"attention_bwd_{g}")(proj, proj, proj, cos_f, sin_s, d_attn, attn, lse)


def _cmul(ar, ai, br, bi):
    return ar * br - ai * bi, ar * bi + ai * br


def _pow256(ar, ai):
    for _ in range(8):
        ar, ai = _cmul(ar, ai, ar, ai)
    return ar, ai


def _chunk_carries(first_r, first_i, pr, pi, reverse):
    rows = lax.broadcasted_iota(jnp.int32, first_r.shape, 0)
    out_r = jnp.zeros_like(first_r)
    out_i = jnp.zeros_like(first_i)
    hr = jnp.zeros_like(first_r[0:1])
    hi = jnp.zeros_like(hr)
    order = range(SCAN_CHUNKS - 1, -1, -1) if reverse else range(SCAN_CHUNKS)
    for c in order:
        out_r = jnp.where(rows == c, hr, out_r)
        out_i = jnp.where(rows == c, hi, out_i)
        tr, ti = _cmul(pr[0:1], pi[0:1], hr, hi)
        hr = first_r[c:c + 1] + tr
        hi = first_i[c:c + 1] + ti
    return out_r, out_i


def _tile(j):
    return pl.ds(pl.multiple_of(j * SCAN_CHUNKS, SCAN_CHUNKS), SCAN_CHUNKS)


def _to_scan_rows(t):
    return t.reshape(SCAN_CHUNKS, SCAN_STEPS, t.shape[1]).transpose(1, 0, 2).reshape(t.shape)


def _from_scan_rows(t):
    return t.reshape(SCAN_STEPS, SCAN_CHUNKS, t.shape[1]).transpose(1, 0, 2).reshape(t.shape)


def _scan_fwd(bur, bui, ar, ai):
    lb = SCAN_LANES

    def body(bur_ref, bui_ref, ar_ref, ai_ref, hr_ref, hi_ref, er_ref, ei_ref):
        a_r = jnp.broadcast_to(ar_ref[...], (SCAN_CHUNKS, lb))
        a_i = jnp.broadcast_to(ai_ref[...], (SCAN_CHUNKS, lb))

        def local(j, carry):
            tr, ti = _cmul(a_r, a_i, carry[0], carry[1])
            nr = tr + bur_ref[_tile(j), :]
            ni = ti + bui_ref[_tile(j), :]
            hr_ref[_tile(j), :] = nr
            hi_ref[_tile(j), :] = ni
            return nr, ni

        zero = jnp.zeros((SCAN_CHUNKS, lb), F32)
        last_r, last_i = lax.fori_loop(0, SCAN_STEPS, local, (zero, zero), unroll=4)
        pr, pi = _pow256(a_r, a_i)
        er, ei = _chunk_carries(last_r, last_i, pr, pi, reverse=False)
        er_ref[...] = er
        ei_ref[...] = ei

        def fix(j, carry):
            tr, ti = _cmul(carry[0], carry[1], er, ei)
            hr_ref[_tile(j), :] += tr
            hi_ref[_tile(j), :] += ti
            return _cmul(carry[0], carry[1], a_r, a_i)

        lax.fori_loop(0, SCAN_STEPS, fix, (a_r, a_i), unroll=4)

    big = pl.BlockSpec((SEQ, lb), lambda j: (0, j))
    vec = pl.BlockSpec((1, lb), lambda j: (0, j))
    ent = pl.BlockSpec((SCAN_CHUNKS, lb), lambda j: (0, j))
    return pl.pallas_call(
        body, grid=(SSM_LANES // lb,), in_specs=[big, big, vec, vec], out_specs=[big, big, ent, ent],
        out_shape=[_sds((SEQ, SSM_LANES)), _sds((SEQ, SSM_LANES)), _sds((SCAN_CHUNKS, SSM_LANES)),
                   _sds((SCAN_CHUNKS, SSM_LANES))],
        compiler_params=_cp(("parallel",)), name="ssm_scan_fwd")(bur, bui, ar, ai)


def _scan_bwd(gr, gi, hr, hi, er, ei, ar, ai):
    lb = SCAN_LANES

    def body(gr_ref, gi_ref, hr_ref, hi_ref, er_ref, ei_ref, ar_ref, ai_ref, lr_ref, li_ref, dar_ref, dai_ref):
        a_r = jnp.broadcast_to(ar_ref[...], (SCAN_CHUNKS, lb))
        a_i = -jnp.broadcast_to(ai_ref[...], (SCAN_CHUNKS, lb))

        def local(t, carry):
            j = SCAN_STEPS - 1 - t
            tr, ti = _cmul(a_r, a_i, carry[0], carry[1])
            nr = tr + gr_ref[_tile(j), :]
            ni = ti + gi_ref[_tile(j), :]
            lr_ref[_tile(j), :] = nr
            li_ref[_tile(j), :] = ni
            return nr, ni

        zero = jnp.zeros((SCAN_CHUNKS, lb), F32)
        first_r, first_i = lax.fori_loop(0, SCAN_STEPS, local, (zero, zero), unroll=4)
        pr, pi = _pow256(a_r, a_i)
        nxt_r, nxt_i = _chunk_carries(first_r, first_i, pr, pi, reverse=True)

        def accumulate(lam_r, lam_i, hp_r, hp_i, acc):
            return (acc[0] + lam_r * hp_r + lam_i * hp_i, acc[1] + lam_i * hp_r - lam_r * hp_i)

        def fix(t, carry):
            qr, qi, acc_r, acc_i = carry
            j = SCAN_STEPS - 1 - t
            tr, ti = _cmul(qr, qi, nxt_r, nxt_i)
            lam_r = lr_ref[_tile(j), :] + tr
            lam_i = li_ref[_tile(j), :] + ti
            lr_ref[_tile(j), :] = lam_r
            li_ref[_tile(j), :] = lam_i
            acc_r, acc_i = accumulate(lam_r, lam_i, hr_ref[_tile(j - 1), :], hi_ref[_tile(j - 1), :], (acc_r, acc_i))
            qr, qi = _cmul(qr, qi, a_r, a_i)
            return qr, qi, acc_r, acc_i

        qr, qi, acc_r, acc_i = lax.fori_loop(0, SCAN_STEPS - 1, fix, (a_r, a_i, zero, zero), unroll=4)
        tr, ti = _cmul(qr, qi, nxt_r, nxt_i)
        lam_r = lr_ref[_tile(0), :] + tr
        lam_i = li_ref[_tile(0), :] + ti
        lr_ref[_tile(0), :] = lam_r
        li_ref[_tile(0), :] = lam_i
        acc_r, acc_i = accumulate(lam_r, lam_i, er_ref[...], ei_ref[...], (acc_r, acc_i))
        dar_ref[...] = jnp.sum(acc_r, axis=0, keepdims=True)
        dai_ref[...] = jnp.sum(acc_i, axis=0, keepdims=True)

    big = pl.BlockSpec((SEQ, lb), lambda j: (0, j))
    vec = pl.BlockSpec((1, lb), lambda j: (0, j))
    ent = pl.BlockSpec((SCAN_CHUNKS, lb), lambda j: (0, j))
    return pl.pallas_call(
        body, grid=(SSM_LANES // lb,), in_specs=[big, big, big, big, ent, ent, vec, vec],
        out_specs=[big, big, vec, vec],
        out_shape=[_sds((SEQ, SSM_LANES)), _sds((SEQ, SSM_LANES)), _sds((1, SSM_LANES)), _sds((1, SSM_LANES))],
        compiler_params=_cp(("parallel",)), name="ssm_scan_bwd")(gr, gi, hr, hi, er, ei, ar, ai)


def _rope_tables():
    half = HEAD_DIM // 2
    inv_freq = ROPE_THETA ** (-jnp.arange(half, dtype=F32) / half)
    ang = jnp.arange(SEQ, dtype=F32)[:, None] * inv_freq[None, :]
    cos, sin = jnp.cos(ang), jnp.sin(ang)
    cos_f = jnp.concatenate([cos, cos, cos, cos], axis=1)
    sin_s = jnp.concatenate([-sin, sin, -sin, sin], axis=1)
    return cos_f, sin_s


def _ssm_discretise(a_re, a_im, log_dt, b_re, b_im):
    lam = lax.complex(a_re, a_im)
    dt = jnp.exp(log_dt)[:, None]
    a_bar = jnp.exp(lam * dt)
    b_bar = ((a_bar - 1.0) / lam)[..., None] * lax.complex(b_re, b_im)
    return a_bar.real, a_bar.imag, b_bar.real, b_bar.imag


def _block_diag_in(b):
    eye = jnp.eye(SSM_GROUPS, dtype=b.dtype)
    return (eye[:, None, :, None] * b.transpose(0, 2, 1)[:, :, None, :]).reshape(SSM_WIDTH, SSM_LANES)


def _block_diag_out(c):
    eye = jnp.eye(SSM_GROUPS, dtype=c.dtype)
    return (eye[:, None, :, None] * c.transpose(0, 2, 1)[:, :, None, :]).reshape(SSM_LANES, SSM_WIDTH)


def _diag_blocks(m, rows, cols):
    m4 = m.reshape(SSM_GROUPS, rows, SSM_GROUPS, cols)
    return jnp.diagonal(m4, axis1=0, axis2=2).transpose(2, 0, 1)


def _local_step(x, tgt, wts, small):
    s = SEQ
    cos_f, sin_s = _rope_tables()

    proj = _mm_cols(x, wts["w_in"], tm=1024, name="proj")

    attn, lse = _attention_fwd(proj, cos_f, sin_s)
    y_attn = _mm_cols(attn, wts["w_attn_br"], tm=s, name="y_attn")

    (abar_r, abar_i, bbar_r, bbar_i), ssm_vjp = jax.vjp(
        _ssm_discretise, small["ssm_a_re"], small["ssm_a_im"], small["ssm_log_dt"], small["ssm_b_re"], small["ssm_b_im"])
    b_in_r, b_in_i = _block_diag_in(bbar_r).astype(BF16), _block_diag_in(bbar_i).astype(BF16)
    c_out_r = _block_diag_out(small["ssm_c_re"]).astype(BF16)
    c_out_ni = _block_diag_out(-small["ssm_c_im"]).astype(BF16)
    a_r, a_i = abar_r.reshape(1, SSM_LANES), abar_i.reshape(1, SSM_LANES)
    d_skip = small["ssm_d"].reshape(1, SSM_WIDTH)

    u_f = _to_scan_rows(proj[:, 3 * QKV_WIDTH:3 * QKV_WIDTH + SSM_WIDTH])
    u_p = u_f.astype(BF16)
    bu_r = _mm_plain(u_p, b_in_r, tm=s, tn=512, name="ssm_bu_re")
    bu_i = _mm_plain(u_p, b_in_i, tm=s, tn=512, name="ssm_bu_im")
    h_r, h_i, e_r, e_i = _scan_fwd(bu_r, bu_i, a_r, a_i)
    y_1 = _mm_plain(h_r, c_out_r, tm=s, tn=512, tk=512, name="ssm_y_re")
    y_2 = _mm_plain(h_i, c_out_ni, tm=s, tn=512, tk=512, name="ssm_y_im")

    def gelu_fwd(y1, y2, u, dsk):
        y = y1 + y2 + dsk * u
        return y, 0.5 * y * (1.0 + jnp.tanh(GELU_C * (y + GELU_K * y * y * y)))

    y_s5, gel = _rowwise(gelu_fwd, [y_1, y_2, u_f], [d_skip], [_sds((s, SSM_WIDTH)), _sds((s, SSM_WIDTH), BF16)],
                         tm=512, name="ssm_gelu")
    glu = _mm_cols(gel, wts["w_glu"], tm=s, name="glu")

    def glu_fwd(ga, gb):
        return ga * jax.nn.sigmoid(gb)

    (y_glu,) = _rowwise(glu_fwd, [(glu, SSM_WIDTH, 0), (glu, SSM_WIDTH, 1)], [], [_sds((s, SSM_WIDTH), BF16)],
                        tm=512, name="glu_gate")
    y_glu = _from_scan_rows(y_glu)
    y_ssm = _mm_cols(y_glu, wts["w_ssm_br"], tm=s, name="y_ssm")

    gl0 = (proj, D_MODEL, (3 * QKV_WIDTH + SSM_WIDTH) // D_MODEL)
    gl1 = (proj, D_MODEL, (3 * QKV_WIDTH + SSM_WIDTH) // D_MODEL + 1)
    b_gate = small["b_gate"]

    def gate_mix(l0, l1, ya, ys, bg):
        return jax.nn.sigmoid(l0 + bg[0:1]) * ya + jax.nn.sigmoid(l1 + bg[1:2]) * ys

    (mixed,) = _rowwise(gate_mix, [gl0, gl1, y_attn, y_ssm], [b_gate], [_sds((s, D_MODEL), BF16)], tm=256,
                        name="gate_mix")
    w_out = wts["w_out"].reshape(D_MODEL, D_MODEL)
    mix_out = _mm_plain(mixed, w_out, tm=1024, tn=512, name="mix_out")

    def ln1_fwd(xv, mo, g, b):
        z = DN_ALPHA * xv + mo
        xhat, _ = _ln_stats(z)
        return z, xhat * g + b

    z1, h = _rowwise(ln1_fwd, [x, mix_out], [small["ln1_g"], small["ln1_b"]],
                     [_sds((s, D_MODEL)), _sds((s, D_MODEL))], tm=256, name="ln1")

    ff_a = _mm_cols(h, wts["w_ff_gate"], tm=1024, name="ff_gate", out3d=True)
    ff_b = _mm_cols(h, wts["w_ff_up"], tm=1024, name="ff_up", out3d=True)
    nf = D_FF // N_CHIPS

    def swiglu_fwd(a, b):
        return a * jax.nn.sigmoid(a) * b

    (act,) = _rowwise(swiglu_fwd, [ff_a.reshape(N_CHIPS * s, nf), ff_b.reshape(N_CHIPS * s, nf)], [],
                      [_sds((N_CHIPS * s, nf), BF16)], tm=1024, name="swiglu")
    act = act.reshape(N_CHIPS, s, nf)
    w_down = wts["w_ff_down"]
    ff = _matmul(act, w_down, grid=(2, N_CHIPS),
                 a_spec=pl.BlockSpec((None, 1024, nf), lambda i, k: (k, i, 0)),
                 b_spec=pl.BlockSpec((None, nf, D_MODEL), lambda i, k: (k, 0, 0)),
                 o_spec=pl.BlockSpec((1024, D_MODEL), lambda i, k: (i, 0)),
                 out_shape=_sds((s, D_MODEL)), dims=(1, 0), k_axis=1, name="ff_down")

    def ln2_loss(hv, ffv, tg, g, b):
        z = DN_ALPHA * hv + ffv
        xhat, rstd = _ln_stats(z)
        err = xhat * g + b - tg
        d_out = err * (1.0 / D_MODEL)
        loss_rows = jnp.sum(err * err, axis=-1, keepdims=True) * (0.5 / D_MODEL)
        loss = jnp.broadcast_to(jnp.sum(loss_rows, axis=0, keepdims=True), (1, 128))
        return _ln_bwd(d_out, xhat, rstd, g), loss, _colsum(d_out * xhat), _colsum(d_out)

    dz2, loss_v, d_ln2_g, d_ln2_b = _rowwise(
        ln2_loss, [h, ff, tgt], [small["ln2_g"], small["ln2_b"]], [_sds((s, D_MODEL))],
        [_sds((1, 128)), _sds((1, D_MODEL)), _sds((1, D_MODEL))], tm=256, name="ln2_loss")

    d_act = _matmul(dz2, w_down, grid=(2, N_CHIPS),
                    a_spec=pl.BlockSpec((1024, D_MODEL), lambda i, k: (i, 0)),
                    b_spec=pl.BlockSpec((None, nf, D_MODEL), lambda i, k: (k, 0, 0)),
                    o_spec=pl.BlockSpec((None, 1024, nf), lambda i, k: (k, i, 0)),
                    out_shape=_sds((N_CHIPS, s, nf)), dims=(1, 1), name="d_act")
    g_w_ff_down = _matmul(act, dz2, grid=(N_CHIPS,),
                          a_spec=pl.BlockSpec((None, s, nf), lambda k: (k, 0, 0)),
                          b_spec=pl.BlockSpec((s, D_MODEL), lambda k: (0, 0)),
                          o_spec=pl.BlockSpec((None, nf, D_MODEL), lambda k: (k, 0, 0)),
                          out_shape=_sds((N_CHIPS, nf, D_MODEL), BF16), dims=(0, 0), name="g_w_ff_down")

    def swiglu_bwd(da, a, b):
        sg = jax.nn.sigmoid(a)
        return da * b * sg * (1.0 + a * (1.0 - sg)), da * a * sg

    d_a, d_b = _rowwise(swiglu_bwd, [d_act.reshape(N_CHIPS * s, nf), ff_a.reshape(N_CHIPS * s, nf),
                                     ff_b.reshape(N_CHIPS * s, nf)], [],
                        [_sds((N_CHIPS * s, nf), BF16)] * 2, tm=1024, name="swiglu_bwd")
    d_a, d_b = d_a.reshape(N_CHIPS, s, nf), d_b.reshape(N_CHIPS, s, nf)
    g_w_ff_gate = _mm_cols_tn(h, d_a, ns=nf, name="g_w_ff_gate", dy3d=True)
    g_w_ff_up = _mm_cols_tn(h, d_b, ns=nf, name="g_w_ff_up", dy3d=True)
    dh_a = _mm_cols_nt(d_a, wts["w_ff_gate"], tm=1024, name="dh_gate", dy3d=True)
    dh_b = _mm_cols_nt(d_b, wts["w_ff_up"], tm=1024, name="dh_up", dy3d=True)

    def ln1_bwd(dz, da, db, z, g):
        xhat, rstd = _ln_stats(z)
        dh = DN_ALPHA * dz + da + db
        return _ln_bwd(dh, xhat, rstd, g), _colsum(dh * xhat), _colsum(dh)

    dz1, d_ln1_g, d_ln1_b = _rowwise(ln1_bwd, [dz2, dh_a, dh_b, z1], [small["ln1_g"]], [_sds((s, D_MODEL))],
                                     [_sds((1, D_MODEL)), _sds((1, D_MODEL))], tm=256, name="ln1_bwd")
    d_mixed = _mm_plain(dz1, w_out, tm=1024, tn=512, dims=(1, 1), name="d_mixed")
    g_w_out = _mm_plain(mixed, dz1, tm=D_MODEL, tn=512, dims=(0, 0), out_dtype=BF16, name="g_w_out")
    g_w_out = g_w_out.reshape(N_CHIPS, D_MODEL // N_CHIPS, D_MODEL)

    def gate_bwd(dm, l0, l1, ya, ys, bg):
        g0 = jax.nn.sigmoid(l0 + bg[0:1])
        g1 = jax.nn.sigmoid(l1 + bg[1:2])
        dl0 = dm * ya * g0 * (1.0 - g0)
        dl1 = dm * ys * g1 * (1.0 - g1)
        return dm * g0, dm * g1, jnp.concatenate([dl0, dl1], axis=1), _colsum(dl0), _colsum(dl1)

    d_y_attn, d_y_ssm, d_gl, d_bg0, d_bg1 = _rowwise(
        gate_bwd, [d_mixed, gl0, gl1, y_attn, y_ssm], [b_gate],
        [_sds((s, D_MODEL), BF16), _sds((s, D_MODEL), BF16), _sds((s, 2 * D_MODEL), BF16)],
        [_sds((1, D_MODEL)), _sds((1, D_MODEL))], tm=256, name="gate_bwd")

    g_w_ssm_br = _mm_cols_tn(y_glu, d_y_ssm, ns=D_MODEL // N_CHIPS, name="g_w_ssm_br")
    d_y_glu = _to_scan_rows(_mm_cols_nt(d_y_ssm, wts["w_ssm_br"], tm=s, name="d_y_glu"))

    def glu_bwd(dy, ga, gb):
        sg = jax.nn.sigmoid(gb)
        return jnp.concatenate([dy * sg, dy * ga * sg * (1.0 - sg)], axis=1)

    (d_glu,) = _rowwise(glu_bwd, [d_y_glu, (glu, SSM_WIDTH, 0), (glu, SSM_WIDTH, 1)], [],
                        [_sds((s, 2 * SSM_WIDTH), BF16)], tm=512, name="glu_bwd")
    g_w_glu = _mm_cols_tn(gel, d_glu, ns=2 * SSM_WIDTH // N_CHIPS, name="g_w_glu")
    d_gel = _mm_cols_nt(d_glu, wts["w_glu"], tm=s, name="d_gel")

    def gelu_bwd(dg, y, u, dsk):
        th = jnp.tanh(GELU_C * (y + GELU_K * y * y * y))
        dy = dg * (0.5 * (1.0 + th) + 0.5 * y * (1.0 - th * th) * GELU_C * (1.0 + 3.0 * GELU_K * y * y))
        return dy, dy * dsk, _colsum(dy * u)

    d_y, d_u_skip, d_ssm_d = _rowwise(gelu_bwd, [d_gel, y_s5, u_f], [d_skip],
                                      [_sds((s, SSM_WIDTH), BF16), _sds((s, SSM_WIDTH))], [_sds((1, SSM_WIDTH))],
                                      tm=512, name="gelu_bwd")
    g_h_r = _mm_plain(d_y, c_out_r, tm=s, tn=512, dims=(1, 1), name="ssm_gh_re")
    g_h_i = _mm_plain(d_y, c_out_ni, tm=s, tn=512, dims=(1, 1), name="ssm_gh_im")
    d_c_r = _mm_plain(h_r, d_y, tm=512, tn=SSM_WIDTH, dims=(0, 0), name="ssm_dc_re")
    d_c_ni = _mm_plain(h_i, d_y, tm=512, tn=SSM_WIDTH, dims=(0, 0), name="ssm_dc_im")
    lam_r, lam_i, d_abar_r, d_abar_i = _scan_bwd(g_h_r, g_h_i, h_r, h_i, e_r, e_i, a_r, a_i)
    d_bin_r = _mm_plain(u_p, lam_r, tm=SSM_WIDTH, tn=512, dims=(0, 0), name="ssm_db_re")
    d_bin_i = _mm_plain(u_p, lam_i, tm=SSM_WIDTH, tn=512, dims=(0, 0), name="ssm_db_im")
    d_u_r = _mm_plain(lam_r, b_in_r, tm=s, tn=SSM_WIDTH, tk=512, dims=(1, 1), name="ssm_du_re")
    d_u_i = _mm_plain(lam_i, b_in_i, tm=s, tn=SSM_WIDTH, tk=512, dims=(1, 1), name="ssm_du_im")

    def add3(a, b, c):
        return a + b + c

    (d_u,) = _rowwise(add3, [d_u_skip, d_u_r, d_u_i], [], [_sds((s, SSM_WIDTH), BF16)], tm=512, name="ssm_du")
    d_u = _from_scan_rows(d_u)
    d_bbar_r = _diag_blocks(d_bin_r, SSM_GROUP, SSM_STATE).transpose(0, 2, 1)
    d_bbar_i = _diag_blocks(d_bin_i, SSM_GROUP, SSM_STATE).transpose(0, 2, 1)
    d_a_re, d_a_im, d_log_dt, d_b_re, d_b_im = ssm_vjp(
        (d_abar_r.reshape(SSM_GROUPS, SSM_STATE), d_abar_i.reshape(SSM_GROUPS, SSM_STATE), d_bbar_r, d_bbar_i))
    d_c_re = _diag_blocks(d_c_r, SSM_STATE, SSM_GROUP).transpose(0, 2, 1)
    d_c_im = -_diag_blocks(d_c_ni, SSM_STATE, SSM_GROUP).transpose(0, 2, 1)

    g_w_attn_br = _mm_cols_tn(attn, d_y_attn, ns=D_MODEL // N_CHIPS, name="g_w_attn_br")
    d_attn = _mm_cols_nt(d_y_attn, wts["w_attn_br"], tm=s, name="d_attn")
    dqkv = [_attention_bwd(g, proj, cos_f, sin_s, d_attn, attn, lse) for g in range(len(DILATIONS))]

    d_proj = jnp.concatenate([dqkv[g][j] for j in range(3) for g in range(len(DILATIONS))] + [d_u, d_gl],
                             axis=1)
    g_w_in = _mm_cols_tn(x, d_proj, ns=IN_WIDTH // N_CHIPS, name="g_w_in")
    dx_proj = _mm_cols_nt(d_proj, wts["w_in"], tm=1024, name="dx_proj", after=(g_w_in,))

    def dx_sum(dz, dxp):
        return DN_ALPHA * dz + dxp

    (grad_x,) = _rowwise(dx_sum, [dz1, dx_proj], [], [_sds((s, D_MODEL))], tm=512, name="grad_x")

    big = {"w_in": g_w_in, "w_attn_br": g_w_attn_br, "w_ssm_br": g_w_ssm_br, "w_out": g_w_out, "w_glu": g_w_glu,
           "w_ff_gate": g_w_ff_gate, "w_ff_up": g_w_ff_up, "w_ff_down": g_w_ff_down}
    small_g = {"b_gate": jnp.concatenate([d_bg0, d_bg1], axis=0), "ssm_a_re": d_a_re, "ssm_a_im": d_a_im,
               "ssm_log_dt": d_log_dt, "ssm_b_re": d_b_re, "ssm_b_im": d_b_im, "ssm_c_re": d_c_re, "ssm_c_im": d_c_im,
               "ssm_d": d_ssm_d.reshape(SSM_WIDTH), "ln1_g": d_ln1_g, "ln1_b": d_ln1_b, "ln2_g": d_ln2_g,
               "ln2_b": d_ln2_b}
    marks = {"ln1_bwd": dz1, "scan_bwd": lam_r, "attention_bwd_0": dqkv[0][0], "dx_proj": dx_proj}
    return loss_v[0, 0], grad_x, big, small_g, marks


GATHER_ID, SWAP_ID, SCATTER_ID, JOIN_ID = 1, 2, 3, 4


def _place():
    return lax.axis_index("x"), lax.axis_index("y"), lax.axis_index("c")


def _other_chips(x, y):
    return [(1 - x, y), (x, 1 - y), (1 - x, 1 - y)]


def _handshake(peers):
    barrier = pltpu.get_barrier_semaphore()
    for peer in peers:
        pl.semaphore_signal(barrier, inc=1, device_id=peer, device_id_type=MESH)
    pl.semaphore_wait(barrier, len(peers))


def _sequencer(body, arrays, out_type, sems, collective_id, name):
    return pl.kernel(body, name=name, out_type=out_type,
                     mesh=plsc.ScalarSubcoreMesh(axis_name="sequencer", num_cores=1), scratch_types=sems,
                     compiler_params=pltpu.CompilerParams(collective_id=collective_id))(*arrays)


def _gather_weights(shards, *, name):
    nw = len(shards)

    def body(*refs):
        ins, outs = refs[:nw], refs[nw:2 * nw]
        send_sems, recv_sems, pass_send, pass_recv, local_sems = refs[2 * nw:]
        x, y, c = _place()
        chip = 2 * x + y
        chips = _other_chips(x, y)
        _handshake([(x, y, 1 - c)] + [(cx, cy, c) for cx, cy in chips])
        started = []
        for w in range(nw):
            hw = shards[w].shape[0] // 2
            mine = pl.ds(c * hw, hw)
            own = pltpu.make_async_copy(ins[w], outs[w].at[chip], local_sems.at[w])
            own.start()
            started.append(own)
            for j, (cx, cy) in enumerate(chips):
                cp = pltpu.make_async_remote_copy(
                    src_ref=ins[w].at[mine], dst_ref=outs[w].at[chip, mine], send_sem=send_sems.at[w, j],
                    recv_sem=recv_sems.at[w, j], device_id=(cx, cy, c), device_id_type=MESH)
                cp.start()
                started.append(cp)
        passed = []
        for w in range(nw):
            hw = shards[w].shape[0] // 2
            mine = pl.ds(c * hw, hw)
            for j, (cx, cy) in enumerate(chips):
                landed = outs[w].at[2 * cx + cy, mine]
                pltpu.make_async_remote_copy(
                    src_ref=ins[w].at[mine], dst_ref=landed, send_sem=send_sems.at[w, j],
                    recv_sem=recv_sems.at[w, j], device_id=(cx, cy, c), device_id_type=MESH).wait_recv()
                cp = pltpu.make_async_remote_copy(
                    src_ref=landed, dst_ref=landed, send_sem=pass_send.at[w, j], recv_sem=pass_recv.at[w, j],
                    device_id=(x, y, 1 - c), device_id_type=MESH)
                cp.start()
                passed.append(cp)
        for w in range(nw):
            hw = shards[w].shape[0] // 2
            theirs = pl.ds((1 - c) * hw, hw)
            for j, (cx, cy) in enumerate(chips):
                landed = outs[w].at[2 * cx + cy, theirs]
                pltpu.make_async_remote_copy(
                    src_ref=landed, dst_ref=landed, send_sem=pass_send.at[w, j], recv_sem=pass_recv.at[w, j],
                    device_id=(x, y, 1 - c), device_id_type=MESH).wait_recv()
        for cp in started[0::4]:
            cp.wait()
        for cp in [s for i, s in enumerate(started) if i % 4] + passed:
            cp.wait_send()

    sem = pltpu.SemaphoreType.DMA
    return _sequencer(body, shards, [_sds((N_CHIPS,) + a.shape, a.dtype) for a in shards],
                      [sem((nw, 3)), sem((nw, 3)), sem((nw, 3)), sem((nw, 3)), sem((nw,))], GATHER_ID, name)


def _swap_other_halves(grads, *, name):
    nw = len(grads)

    def body(*refs):
        ins, outs = refs[:nw], refs[nw:2 * nw]
        send_sems, recv_sems = refs[2 * nw:]
        x, y, c = _place()
        _handshake([(x, y, 1 - c)])
        cps = []
        for w in range(nw):
            hw = grads[w].shape[1] // 2
            cp = pltpu.make_async_remote_copy(
                src_ref=ins[w].at[:, pl.ds((1 - c) * hw, hw)], dst_ref=outs[w], send_sem=send_sems.at[w],
                recv_sem=recv_sems.at[w], device_id=(x, y, 1 - c), device_id_type=MESH)
            cp.start()
            cps.append(cp)
        for cp in cps:
            cp.wait()

    sem = pltpu.SemaphoreType.DMA
    return _sequencer(body, grads, [_sds((N_CHIPS, g.shape[1] // 2, g.shape[2]), g.dtype) for g in grads],
                      [sem((nw,)), sem((nw,))], SWAP_ID, name)


def _add_my_half(core, g, other, after=()):
    n, r, cols = g.shape
    hw = r // 2

    def body(core_ref, g_ref, o_ref, *rest):
        out_ref = rest[len(after)]
        out_ref[...] = (g_ref[...].astype(F32) + o_ref[...].astype(F32)).astype(out_ref.dtype)

    return pl.pallas_call(
        body,
        grid_spec=pltpu.PrefetchScalarGridSpec(
            num_scalar_prefetch=1, grid=(n,),
            in_specs=[pl.BlockSpec((None, None, hw, cols), lambda s, core_ref: (s, core_ref[0], 0, 0)),
                      pl.BlockSpec((None, hw, cols), lambda s, core_ref: (s, 0, 0))] + [HBM_OPERAND] * len(after),
            out_specs=pl.BlockSpec((None, hw, cols), lambda s, core_ref: (s, 0, 0))),
        out_shape=_sds((n, hw, cols), BF16), compiler_params=_cp(("parallel",)),
        name="add_my_half")(core, g.reshape(n, 2, hw, cols), other, *after)


def _scatter_partials(parts, *, name):
    nw = len(parts)

    def body(*refs):
        ins, outs = refs[:nw], refs[nw:2 * nw]
        send_sems, recv_sems = refs[2 * nw:]
        x, y, c = _place()
        _handshake([(cx, cy, c) for cx, cy in _other_chips(x, y)])
        cps = []
        for w in range(nw):
            for j, (cx, cy) in enumerate(_other_chips(x, y)):
                cp = pltpu.make_async_remote_copy(
                    src_ref=ins[w].at[2 * cx + cy], dst_ref=outs[w].at[j], send_sem=send_sems.at[w, j],
                    recv_sem=recv_sems.at[w, j], device_id=(cx, cy, c), device_id_type=MESH)
                cp.start()
                cps.append(cp)
        for cp in cps:
            cp.wait()

    sem = pltpu.SemaphoreType.DMA
    return _sequencer(body, parts, [_sds((3,) + p.shape[1:], p.dtype) for p in parts],
                      [sem((nw, 3)), sem((nw, 3))], SCATTER_ID, name)


def _sum_partials(chip, part, recv, after=()):
    _, hw, cols = part.shape
    th = hw // 2 if hw % 32 == 0 else hw

    def body(chip_ref, p_ref, r_ref, *rest):
        out_ref = rest[len(after)]
        acc = p_ref[...].astype(F32)
        for j in range(3):
            acc = acc + r_ref[j].astype(F32)
        out_ref[...] = acc

    return pl.pallas_call(
        body,
        grid_spec=pltpu.PrefetchScalarGridSpec(
            num_scalar_prefetch=1, grid=(hw // th,),
            in_specs=[pl.BlockSpec((None, th, cols), lambda i, chip_ref: (chip_ref[0], i, 0)),
                      pl.BlockSpec((3, th, cols), lambda i, chip_ref: (0, i, 0))] + [HBM_OPERAND] * len(after),
            out_specs=pl.BlockSpec((th, cols), lambda i, chip_ref: (i, 0))),
        out_shape=_sds((hw, cols)), compiler_params=_cp(("parallel",)), name="sum_partials")(
            chip, part, recv, *after)


def _swap_reduced_halves(halves, *, name):
    nw = len(halves)

    def body(*refs):
        ins, outs = refs[:nw], refs[nw:2 * nw]
        send_sems, recv_sems = refs[2 * nw:]
        x, y, c = _place()
        _handshake([(x, y, 1 - c)])
        cps = []
        for w in range(nw):
            cp = pltpu.make_async_remote_copy(
                src_ref=ins[w], dst_ref=outs[w], send_sem=send_sems.at[w], recv_sem=recv_sems.at[w],
                device_id=(x, y, 1 - c), device_id_type=MESH)
            cp.start()
            cps.append(cp)
        for cp in cps:
            cp.wait()

    sem = pltpu.SemaphoreType.DMA
    return _sequencer(body, halves, [_sds(h.shape, h.dtype) for h in halves], [sem((nw,)), sem((nw,))], JOIN_ID, name)


def _allreduce_rows(vec, *, name, after=()):
    rows = vec.shape[0]

    def body(v_ref, *rest):
        out_ref, slots, send_sems, recv_sems = rest[len(after):]
        x, y, c = _place()
        me = 4 * x + 2 * y + c
        slots[me] = v_ref[...]
        peers = []
        for mask in range(1, N_DEV):
            px = 1 - x if mask & 4 else x
            py = 1 - y if mask & 2 else y
            pc = 1 - c if mask & 1 else c
            peers.append((px, py, pc))
        cps = []
        for k, peer in enumerate(peers):
            cp = pltpu.make_async_remote_copy(
                src_ref=v_ref, dst_ref=slots.at[me], send_sem=send_sems.at[k], recv_sem=recv_sems.at[k],
                device_id=peer, device_id_type=MESH)
            cp.start()
            cps.append(cp)
        for k, (px, py, pc) in enumerate(peers):
            pltpu.make_async_remote_copy(
                src_ref=v_ref, dst_ref=slots.at[4 * px + 2 * py + pc], send_sem=send_sems.at[k],
                recv_sem=recv_sems.at[k], device_id=(px, py, pc), device_id_type=MESH).wait_recv()
        for cp in cps:
            cp.wait_send()
        acc = slots[0]
        for d in range(1, N_DEV):
            acc = acc + slots[d]
        out_ref[...] = acc

    vmem = pl.BlockSpec(memory_space=pltpu.VMEM)
    return pl.pallas_call(
        body, in_specs=[vmem] + [HBM_OPERAND] * len(after), out_specs=vmem, out_shape=_sds((rows, 128)),
        scratch_shapes=[pltpu.VMEM((N_DEV, rows, 128), F32), pltpu.SemaphoreType.DMA((N_DEV - 1,)),
                        pltpu.SemaphoreType.DMA((N_DEV - 1,))],
        compiler_params=pltpu.CompilerParams(vmem_limit_bytes=VMEM_LIMIT_BYTES), name=name)(vec, *after)


def _reduce_scatter_start(grads, core, *, tag, add_after=()):
    others = _swap_other_halves(grads, name="swap_other_halves_" + tag)
    parts = [_add_my_half(core, g, o, add_after) for g, o in zip(grads, others)]
    return parts, _scatter_partials(parts, name="scatter_partials_" + tag)


def _reduce_scatter_finish(parts, recvd, chip, *, tag, sum_after=()):
    mine = [_sum_partials(chip, p, r, sum_after) for p, r in zip(parts, recvd)]
    return mine, _swap_reduced_halves(mine, name="swap_reduced_halves_" + tag)


ADAM_BLOCK_ELEMS = 256 * 1024


def _adam_rows(rows, cols):
    tm = rows
    while tm * cols > ADAM_BLOCK_ELEMS and tm % 16 == 0:
        tm //= 2
    return tm


def _adam_step(wv, gv, mv, vv):
    m2 = ADAM_B1 * mv + (1.0 - ADAM_B1) * gv
    v2 = ADAM_B2 * vv + (1.0 - ADAM_B2) * (gv * gv)
    m_hat = m2 / (1.0 - ADAM_B1 ** ADAM_STEP)
    v_hat = v2 / (1.0 - ADAM_B2 ** ADAM_STEP)
    return -ADAM_LR * (m_hat / (jnp.sqrt(v_hat) + ADAM_EPS) + ADAM_WD * wv), m2, v2


def _adamw(w, g, m, v, *, name):
    rows, cols = w.shape
    return _rowwise(_adam_step, [w, g, m, v], [], [_sds((rows, cols))] * 3, tm=_adam_rows(rows, cols), name=name)


def _adamw_halves(core, w, g_mine, g_theirs, m, v, *, name, after=()):
    rows, cols = w.shape
    hw = rows // 2
    tm = _adam_rows(hw, cols)
    per_half = hw // tm

    def body(core_ref, w_ref, gm_ref, gt_ref, m_ref, v_ref, *rest):
        g_out, d_out, m_out, v_out = rest[len(after):]
        mine = (pl.program_id(0) // per_half) == core_ref[0]
        g = jnp.where(mine, gm_ref[...], gt_ref[...])
        d, m2, v2 = _adam_step(w_ref[...], g, m_ref[...], v_ref[...])
        g_out[...] = g
        d_out[...] = d
        m_out[...] = m2
        v_out[...] = v2

    full = pl.BlockSpec((tm, cols), lambda i, core_ref: (i, 0))
    half = pl.BlockSpec((tm, cols), lambda i, core_ref: (i % per_half, 0))
    return pl.pallas_call(
        body,
        grid_spec=pltpu.PrefetchScalarGridSpec(
            num_scalar_prefetch=1, grid=(rows // tm,),
            in_specs=[full, half, half, full, full] + [HBM_OPERAND] * len(after), out_specs=[full, full, full, full]),
        out_shape=[_sds((rows, cols))] * 4, compiler_params=_cp(("parallel",)), name=name)(
            core, w, g_mine, g_theirs, m, v, *after)


def _pack_rows(arrs):
    flat = jnp.concatenate([a.reshape(-1).astype(F32) for a in arrs])
    rows = -(-flat.shape[0] // 1024) * 8
    return jnp.pad(flat, (0, rows * 128 - flat.shape[0])).reshape(rows, 128)


def _unpack_rows(vec, shapes):
    flat = vec.reshape(-1)
    out, off = [], 0
    for shp in shapes:
        size = math.prod(shp)
        out.append(flat[off:off + size].reshape(shp))
        off += size
    return out


SMALL = ("b_gate", "ssm_a_re", "ssm_a_im", "ssm_log_dt", "ssm_b_re", "ssm_b_im", "ssm_c_re", "ssm_c_im", "ssm_d",
         "ln1_g", "ln1_b", "ln2_g", "ln2_b")
GATHER_GROUPS = (("w_in", ("w_in",)), ("mixer", ("w_attn_br", "w_ssm_br", "w_glu", "w_out")),
                 ("ffn", ("w_ff_gate", "w_ff_up", "w_ff_down")))
REDUCE_GROUPS = (("ffn", ("w_ff_down", "w_ff_gate", "w_ff_up")),
                 ("mixer", ("w_out", "w_ssm_br", "w_glu", "w_attn_br")), ("w_in", ("w_in",)))
WEIGHTS = ("w_in", "b_gate", "w_attn_br", "w_ssm_br", "w_out", "ssm_a_re", "ssm_a_im", "ssm_log_dt", "ssm_b_re",
           "ssm_b_im", "ssm_c_re", "ssm_c_im", "ssm_d", "w_glu", "ln1_g", "ln1_b", "w_ff_gate", "w_ff_up", "w_ff_down",
           "ln2_g", "ln2_b")


def kernel(x, w_in, b_gate, w_attn_br, w_ssm_br, w_out, ssm_a_re, ssm_a_im, ssm_log_dt, ssm_b_re, ssm_b_im, ssm_c_re, ssm_c_im, ssm_d, w_glu, ln1_g, ln1_b, w_ff_gate, w_ff_up, w_ff_down, ln2_g, ln2_b, loss_target, m_w_in, m_b_gate, m_w_attn_br, m_w_ssm_br, m_w_out, m_ssm_a_re, m_ssm_a_im, m_ssm_log_dt, m_ssm_b_re, m_ssm_b_im, m_ssm_c_re, m_ssm_c_im, m_ssm_d, m_w_glu, m_ln1_g, m_ln1_b, m_w_ff_gate, m_w_ff_up, m_w_ff_down, m_ln2_g, m_ln2_b, v_w_in, v_b_gate, v_w_attn_br, v_w_ssm_br, v_w_out, v_ssm_a_re, v_ssm_a_im, v_ssm_log_dt, v_ssm_b_re, v_ssm_b_im, v_ssm_c_re, v_ssm_c_im, v_ssm_d, v_w_glu, v_ln1_g, v_ln1_b, v_w_ff_gate, v_w_ff_up, v_w_ff_down, v_ln2_g, v_ln2_b):
    given = dict(locals())
    px, py, pc = _place()
    chip = 2 * px + py
    core_s = jnp.reshape(pc, (1,)).astype(jnp.int32)
    chip_s = jnp.reshape(chip, (1,)).astype(jnp.int32)

    wts = {}
    for tag, names in GATHER_GROUPS:
        wts.update(zip(names, _gather_weights([given[n][0].astype(BF16) for n in names], name="gather_" + tag)))
    ncol = D_MODEL // N_CHIPS
    bg_mine = jnp.where(pc == 0, b_gate[0], jnp.zeros_like(b_gate[0]))
    bg_full = lax.dynamic_update_slice(jnp.zeros((2, D_MODEL), F32), bg_mine, (0, chip * ncol))
    bg_full = _allreduce_rows(bg_full.reshape(16, 128), name="gather_gate_bias").reshape(2, D_MODEL)
    small = {n: given[n][0] for n in SMALL if n.startswith("ssm")}
    small.update({n: given[n] for n in ("ln1_g", "ln1_b", "ln2_g", "ln2_b")})
    small["b_gate"] = bg_full

    loss_mine, grad_x, big_g, small_g, marks = _local_step(x[0], loss_target[0], wts, small)
    loss = lax.psum(loss_mine, ("x", "y", "c"))

    groups = dict(REDUCE_GROUPS)
    add_after = {"ffn": (marks["ln1_bwd"],), "mixer": (marks["scan_bwd"],), "w_in": (marks["dx_proj"],)}
    parts, recvd = {}, {}
    for tag, names in REDUCE_GROUPS:
        parts[tag], recvd[tag] = _reduce_scatter_start([big_g[n] for n in names], core_s, tag=tag,
                                                       add_after=add_after[tag])
    grads, delta, new_m, new_v = {}, {}, {}, {}

    def finish(tag, sum_after, adam_after):
        mine, theirs = _reduce_scatter_finish(parts[tag], recvd[tag], chip_s, tag=tag, sum_after=sum_after)
        for n, g_mine, g_theirs in zip(groups[tag], mine, theirs):
            shp = given[n].shape
            two = (shp[1], shp[2])
            res = _adamw_halves(core_s, given[n].reshape(two), g_mine, g_theirs, given["m_" + n].reshape(two),
                                given["v_" + n].reshape(two), name="adamw_" + n, after=adam_after)
            grads[n], delta[n], new_m[n], new_v[n] = [r.reshape(shp) for r in res]

    in_flight = (parts["w_in"][0],)
    finish("ffn", (marks["scan_bwd"],), in_flight)
    finish("mixer", (marks["attention_bwd_0"],), in_flight)
    shapes = [small_g[n].shape for n in SMALL]
    summed = _unpack_rows(_allreduce_rows(_pack_rows([small_g[n] for n in SMALL]), name="allreduce_small",
                                          after=in_flight), shapes)
    for n, g in zip(SMALL, summed):
        if n == "b_gate":
            g = lax.dynamic_slice(g, (0, chip * ncol), (2, ncol))
        grads[n] = g.reshape(given[n].shape)
    shapes = [given[n].shape for n in SMALL]
    packed = [_pack_rows([src[n] for n in SMALL]) for src in
              (given, grads, {n: given["m_" + n] for n in SMALL}, {n: given["v_" + n] for n in SMALL})]
    small_out = _adamw(*packed, name="adamw_small")
    for out, vec in zip((delta, new_m, new_v), small_out):
        out.update(zip(SMALL, _unpack_rows(vec, shapes)))
    behind = [delta[n] for tag in ("ffn", "mixer") for n in groups[tag]] + [small_out[0], grad_x]
    finish("w_in", tuple(behind), ())

    return (loss, grad_x.reshape(x.shape), *[grads[n] for n in WEIGHTS], *[delta[n] for n in WEIGHTS],
            *[new_m[n] for n in WEIGHTS], *[new_v[n] for n in WEIGHTS])
```

```python
import math

import jax
import jax.numpy as jnp
from jax import lax
from jax.experimental import pallas as pl
from jax.experimental.pallas import tpu as pltpu
from jax.experimental.pallas import tpu_sc as plsc

F32 = jnp.float32
BF16 = jnp.bfloat16
MESH = pl.DeviceIdType.MESH

D_MODEL = 1024
SEQ = 2048
HEAD_DIM = 64
ATTN_HEADS = 8
DILATIONS = (1, 4, 16)
ATTN_WIDTH = ATTN_HEADS * HEAD_DIM
QKV_WIDTH = 3 * ATTN_WIDTH
BLOCK = 128
ROPE_THETA = 10000.0
NEG_INF = -1e30
SSM_GROUP = 16
SSM_GROUPS = 32
SSM_WIDTH = 512
SSM_STATE = 64
SSM_LANES = SSM_GROUPS * SSM_STATE
SCAN_CHUNKS = 8
SCAN_STEPS = SEQ // SCAN_CHUNKS
SCAN_LANES = 256
IN_WIDTH = 3 * QKV_WIDTH + SSM_WIDTH + 2 * D_MODEL
D_FF = 2816
N_CHIPS = 4
N_DEV = 8
DN_ALPHA = 2.0 ** 0.25
LN_EPS = 1e-5
ADAM_LR = 0.001
ADAM_B1 = 0.9
ADAM_B2 = 0.999
ADAM_EPS = 1e-08
ADAM_WD = 0.01
ADAM_STEP = 10
GELU_C = math.sqrt(2.0 / math.pi)
GELU_K = 0.044715

VMEM_LIMIT_BYTES = 56 * 1024 * 1024


def _sds(shape, dtype=F32):
    return jax.ShapeDtypeStruct(tuple(shape), dtype)


def _cp(semantics=None):
    return pltpu.CompilerParams(dimension_semantics=semantics, vmem_limit_bytes=VMEM_LIMIT_BYTES)


HBM_OPERAND = pl.BlockSpec(memory_space=pl.ANY)


def _matmul(a, b, *, grid, a_spec, b_spec, o_spec, out_shape, dims, k_axis=None, name, after=()):
    nk = grid[k_axis] if k_axis is not None else 1
    o_block = tuple(d for d in o_spec.block_shape if d is not None)
    n_after = len(after)

    def body(a_ref, b_ref, *rest):
        o_ref, acc = rest[n_after], rest[n_after + 1:]
        part = lax.dot_general(a_ref[...].astype(BF16), b_ref[...].astype(BF16),
                               (((dims[0],), (dims[1],)), ((), ())), preferred_element_type=F32)
        if k_axis is None:
            o_ref[...] = part.astype(o_ref.dtype)
        else:
            k = pl.program_id(k_axis)

            @pl.when(k == 0)
            def _():
                acc[0][...] = part

            @pl.when(k > 0)
            def _():
                acc[0][...] += part

            @pl.when(k == nk - 1)
            def _():
                o_ref[...] = acc[0][...].astype(o_ref.dtype)

    sem = tuple("arbitrary" if ax == k_axis else "parallel" for ax in range(len(grid)))
    return pl.pallas_call(
        body, grid=grid, in_specs=[a_spec, b_spec] + [HBM_OPERAND] * n_after, out_specs=o_spec, out_shape=out_shape,
        scratch_shapes=[pltpu.VMEM(o_block, F32)] if k_axis is not None else [],
        compiler_params=_cp(sem), name=name)(a, b, *after)


def _mm_cols(a, wg, *, tm, name, out_dtype=F32, out3d=False):
    m, k = a.shape
    ns = wg.shape[2]
    if out3d:
        o_spec = pl.BlockSpec((None, tm, ns), lambda i, s: (s, i, 0))
        out_shape = _sds((N_CHIPS, m, ns), out_dtype)
    else:
        o_spec = pl.BlockSpec((tm, ns), lambda i, s: (i, s))
        out_shape = _sds((m, N_CHIPS * ns), out_dtype)
    return _matmul(a, wg, grid=(m // tm, N_CHIPS),
                   a_spec=pl.BlockSpec((tm, k), lambda i, s: (i, 0)),
                   b_spec=pl.BlockSpec((None, k, ns), lambda i, s: (s, 0, 0)),
                   o_spec=o_spec, out_shape=out_shape, dims=(1, 0), name=name)


def _mm_cols_nt(dy, wg, *, tm, name, dy3d=False, out_dtype=F32, after=()):
    k, ns = wg.shape[1], wg.shape[2]
    if dy3d:
        m = dy.shape[1]
        a_spec = pl.BlockSpec((None, tm, ns), lambda i, s: (s, i, 0))
    else:
        m = dy.shape[0]
        a_spec = pl.BlockSpec((tm, ns), lambda i, s: (i, s))
    return _matmul(dy, wg, grid=(m // tm, N_CHIPS), a_spec=a_spec,
                   b_spec=pl.BlockSpec((None, k, ns), lambda i, s: (s, 0, 0)),
                   o_spec=pl.BlockSpec((tm, k), lambda i, s: (i, 0)),
                   out_shape=_sds((m, k), out_dtype), dims=(1, 1), k_axis=1, name=name, after=after)


def _mm_cols_tn(a, dy, *, ns, name, dy3d=False):
    m, k = a.shape
    if dy3d:
        b_spec = pl.BlockSpec((None, m, ns), lambda s: (s, 0, 0))
    else:
        b_spec = pl.BlockSpec((m, ns), lambda s: (0, s))
    return _matmul(a, dy, grid=(N_CHIPS,), a_spec=pl.BlockSpec((m, k), lambda s: (0, 0)), b_spec=b_spec,
                   o_spec=pl.BlockSpec((None, k, ns), lambda s: (s, 0, 0)),
                   out_shape=_sds((N_CHIPS, k, ns), BF16), dims=(0, 0), name=name)


def _mm_plain(a, b, *, tm, tn, name, out_dtype=F32, dims=(1, 0), tk=None):
    m = a.shape[1 - dims[0]]
    kk = a.shape[dims[0]]
    n = b.shape[1 - dims[1]]
    tk = kk if tk is None else tk
    nk = kk // tk

    def a_idx(i, j, k):
        return (i, k) if dims[0] == 1 else (k, i)

    def b_idx(i, j, k):
        return (k, j) if dims[1] == 0 else (j, k)

    a_blk = (tm, tk) if dims[0] == 1 else (tk, tm)
    b_blk = (tk, tn) if dims[1] == 0 else (tn, tk)
    return _matmul(a, b, grid=(m // tm, n // tn, nk),
                   a_spec=pl.BlockSpec(a_blk, a_idx), b_spec=pl.BlockSpec(b_blk, b_idx),
                   o_spec=pl.BlockSpec((tm, tn), lambda i, j, k: (i, j)),
                   out_shape=_sds((m, n), out_dtype), dims=dims, k_axis=2 if nk > 1 else None, name=name)


def _rowwise(fn, tiled, full, outs, accs=(), *, tm, name, after=()):
    args, in_specs = [], []
    for t in tiled:
        if isinstance(t, tuple):
            arr, w, cb = t
            in_specs.append(pl.BlockSpec((tm, w), lambda i, cb=cb: (i, cb)))
        else:
            arr = t
            in_specs.append(pl.BlockSpec((tm, arr.shape[1]), lambda i: (i, 0)))
        args.append(arr)
    rows = args[0].shape[0]
    for f in full:
        in_specs.append(pl.BlockSpec(f.shape, lambda i, nd=f.ndim: (0,) * nd))
        args.append(f)
    out_specs = [pl.BlockSpec((tm, o.shape[1]), lambda i: (i, 0)) for o in outs]
    out_specs += [pl.BlockSpec(a.shape, lambda i, nd=len(a.shape): (0,) * nd) for a in accs]
    n_in, n_out = len(args), len(outs)
    in_specs += [HBM_OPERAND] * len(after)
    first_out = n_in + len(after)

    def body(*refs):
        res = fn(*[r[...] for r in refs[:n_in]])
        res = res if isinstance(res, (tuple, list)) else (res,)
        for r, v in zip(refs[first_out:first_out + n_out], res[:n_out]):
            r[...] = v.astype(r.dtype)
        i = pl.program_id(0)
        for r, v in zip(refs[first_out + n_out:], res[n_out:]):
            @pl.when(i == 0)
            def _(r=r, v=v):
                r[...] = v

            @pl.when(i > 0)
            def _(r=r, v=v):
                r[...] += v

    res = pl.pallas_call(
        body, grid=(rows // tm,), in_specs=in_specs, out_specs=out_specs, out_shape=list(outs) + list(accs),
        compiler_params=_cp(("arbitrary",) if accs else ("parallel",)), name=name)(*args, *after)
    return res


def _colsum(v):
    return jnp.sum(v, axis=0, keepdims=True)


def _ln_stats(z):
    mu = jnp.mean(z, axis=-1, keepdims=True)
    zc = z - mu
    var = jnp.mean(zc * zc, axis=-1, keepdims=True)
    rstd = lax.rsqrt(var + LN_EPS)
    return zc * rstd, rstd


def _ln_bwd(dy, xhat, rstd, g):
    dxh = dy * g
    m1 = jnp.mean(dxh, axis=-1, keepdims=True)
    m2 = jnp.mean(dxh * xhat, axis=-1, keepdims=True)
    return rstd * (dxh - m1 - xhat * m2)


def _swap_halves(t):
    w = t.shape[-1]
    lane = lax.broadcasted_iota(jnp.int32, t.shape, t.ndim - 1)
    return jnp.where((lane % HEAD_DIM) < HEAD_DIM // 2, pltpu.roll(t, w - HEAD_DIM // 2, t.ndim - 1),
                     pltpu.roll(t, HEAD_DIM // 2, t.ndim - 1))


def _band_mask(i):
    row = lax.broadcasted_iota(jnp.int32, (BLOCK, 2 * BLOCK), 0)
    col = lax.broadcasted_iota(jnp.int32, (BLOCK, 2 * BLOCK), 1)
    dist = BLOCK + row - col
    return (dist >= 0) & (dist <= BLOCK) & ((col >= BLOCK) | (i > 0))


PAIR = 2 * HEAD_DIM
UNITS = SEQ // BLOCK
ROPE_ROWS = 256


def _rope(t, cf, ss):
    return t * cf + _swap_halves(t) * ss


def _rope_transposed(d, cf, ss):
    return d * cf + _swap_halves(d * ss)


def _unit_rows(u, dil):
    if dil == 1:
        i = u
        start = pl.multiple_of(u * BLOCK, BLOCK)
        prev = pl.multiple_of(jnp.maximum(u - 1, 0) * BLOCK, BLOCK)
        return i, pl.ds(start, BLOCK), pl.ds(prev, BLOCK)
    rho = jnp.bitwise_and(u, dil - 1)
    i = jnp.right_shift(u, dil.bit_length() - 1)
    start = rho + dil * BLOCK * i
    prev = rho + dil * BLOCK * jnp.maximum(i - 1, 0)
    return i, pl.ds(start, BLOCK, stride=dil), pl.ds(prev, BLOCK, stride=dil)


def _causal_mask():
    row = lax.broadcasted_iota(jnp.int32, (BLOCK, BLOCK), 0)
    col = lax.broadcasted_iota(jnp.int32, (BLOCK, BLOCK), 1)
    return row >= col


def _pair_views(col0):
    return [pl.BlockSpec((SEQ, PAIR), lambda hp, g=g: (0, col0 // PAIR + g * (ATTN_WIDTH // PAIR) + hp))
            for g in range(len(DILATIONS))]


def _rotate_keys(k_refs, kr_refs, cf_ref, ss_ref):
    def step(t, carry):
        rows = pl.ds(pl.multiple_of(t * ROPE_ROWS, ROPE_ROWS), ROPE_ROWS)
        cf, ss = cf_ref[rows, :], ss_ref[rows, :]
        for k_ref, kr_ref in zip(k_refs, kr_refs):
            kr_ref[rows, :] = _rope(k_ref[rows, :], cf, ss)
        return carry

    lax.fori_loop(0, SEQ // ROPE_ROWS, step, 0)


def _attention_fwd(proj, cos_f, sin_s):
    ng = len(DILATIONS)

    def body(*refs):
        q_refs, k_refs, v_refs = refs[:ng], refs[ng:2 * ng], refs[2 * ng:3 * ng]
        cf_ref, ss_ref, attn_ref, lse_ref = refs[3 * ng:3 * ng + 4]
        kr_refs = refs[3 * ng + 4:]
        _rotate_keys(k_refs, kr_refs, cf_ref, ss_ref)
        first = lax.broadcasted_iota(jnp.int32, (BLOCK, PAIR), 1) < HEAD_DIM
        for g, dil in enumerate(DILATIONS):
            two_blocks = SEQ // dil > BLOCK

            def unit(u, carry, g=g, dil=dil, two_blocks=two_blocks):
                i, rows, prev = _unit_rows(u, dil)
                qq = (_rope(q_refs[g][rows, :], cf_ref[rows, :], ss_ref[rows, :]) * (1.0 / math.sqrt(HEAD_DIM))).astype(BF16)
                if two_blocks:
                    kk = jnp.concatenate([kr_refs[g][prev, :], kr_refs[g][rows, :]], axis=0).astype(BF16)
                    vv = jnp.concatenate([v_refs[g][prev, :], v_refs[g][rows, :]], axis=0).astype(BF16)
                    valid = _band_mask(i)
                else:
                    kk = kr_refs[g][rows, :].astype(BF16)
                    vv = v_refs[g][rows, :].astype(BF16)
                    valid = _causal_mask()
                zero = jnp.zeros_like(qq)
                outs, lses = [], []
                for qh in (jnp.where(first, qq, zero), jnp.where(first, zero, qq)):
                    s = lax.dot_general(qh, kk, (((1,), (1,)), ((), ())), preferred_element_type=F32)
                    s = jnp.where(valid, s, NEG_INF)
                    m = jnp.max(s, axis=1, keepdims=True)
                    p = jnp.exp(s - m)
                    l = jnp.sum(p, axis=1, keepdims=True)
                    outs.append(jnp.dot(p.astype(BF16), vv, preferred_element_type=F32) / l)
                    lses.append(m + jnp.log(l))
                o = jnp.where(first, outs[0], outs[1])
                lse = jnp.where(first, lses[0], lses[1])
                if g == 0:
                    attn_ref[rows, :] = o
                    lse_ref[rows, :] = lse
                else:
                    lse_old = lse_ref[rows, :]
                    m = jnp.maximum(lse_old, lse)
                    lse_new = m + jnp.log(jnp.exp(lse_old - m) + jnp.exp(lse - m))
                    attn_ref[rows, :] = attn_ref[rows, :] * jnp.exp(lse_old - lse_new) + o * jnp.exp(lse - lse_new)
                    lse_ref[rows, :] = lse_new
                return carry

            lax.fori_loop(0, UNITS, unit, 0)

    whole = pl.BlockSpec((SEQ, PAIR), lambda hp: (0, 0))
    out = pl.BlockSpec((SEQ, PAIR), lambda hp: (0, hp))
    return pl.pallas_call(
        body, grid=(ATTN_WIDTH // PAIR,),
        in_specs=_pair_views(0) + _pair_views(QKV_WIDTH) + _pair_views(2 * QKV_WIDTH) + [whole, whole],
        out_specs=[out, out], out_shape=[_sds((SEQ, ATTN_WIDTH)), _sds((SEQ, ATTN_WIDTH))],
        scratch_shapes=[pltpu.VMEM((SEQ, PAIR), F32)] * ng,
        compiler_params=_cp(("parallel",)), name="attention_fwd")(*([proj] * (3 * ng)), cos_f, sin_s)


def _attention_bwd(g, proj, cos_f, sin_s, d_attn, attn, lse):
    dil = DILATIONS[g]
    two_blocks = SEQ // dil > BLOCK

    def body(q_ref, k_ref, v_ref, cf_ref, ss_ref, do_ref, o_ref, lse_ref, dq_out, dk_out, dv_out,
             kr_ref, dq_acc, dk_acc, dv_acc):
        _rotate_keys([k_ref], [kr_ref], cf_ref, ss_ref)
        dk_acc[...] = jnp.zeros_like(dk_acc)
        dv_acc[...] = jnp.zeros_like(dv_acc)
        first = lax.broadcasted_iota(jnp.int32, (BLOCK, PAIR), 1) < HEAD_DIM
        nk = 2 * BLOCK if two_blocks else BLOCK
        first_k = lax.broadcasted_iota(jnp.int32, (nk, PAIR), 1) < HEAD_DIM

        def unit(u, carry):
            i, rows, prev = _unit_rows(u, dil)
            qq = (_rope(q_ref[rows, :], cf_ref[rows, :], ss_ref[rows, :]) * (1.0 / math.sqrt(HEAD_DIM))).astype(BF16)
            if two_blocks:
                kk = jnp.concatenate([kr_ref[prev, :], kr_ref[rows, :]], axis=0).astype(BF16)
                vv = jnp.concatenate([v_ref[prev, :], v_ref[rows, :]], axis=0).astype(BF16)
                valid = _band_mask(i)
            else:
                kk = kr_ref[rows, :].astype(BF16)
                vv = v_ref[rows, :].astype(BF16)
                valid = _causal_mask()
            dof = do_ref[rows, :]
            dd = dof * o_ref[rows, :]
            lse2 = lse_ref[rows, :]
            dob = dof.astype(BF16)
            zq, zd, zf = jnp.zeros_like(qq), jnp.zeros_like(dob), jnp.zeros_like(dd)
            dqs, dks, dvs = [], [], []
            for h in range(2):
                sel = first if h == 0 else jnp.logical_not(first)
                qh = jnp.where(sel, qq, zq)
                doh = jnp.where(sel, dob, zd)
                delta = jnp.sum(jnp.where(sel, dd, zf), axis=1, keepdims=True)
                lse_h = lse2[:, h * HEAD_DIM:h * HEAD_DIM + 1]
                s = lax.dot_general(qh, kk, (((1,), (1,)), ((), ())), preferred_element_type=F32)
                p = jnp.where(valid, jnp.exp(s - lse_h), 0.0)
                dp = lax.dot_general(doh, vv, (((1,), (1,)), ((), ())), preferred_element_type=F32)
                ds = (p * (dp - delta)).astype(BF16)
                dqs.append(jnp.dot(ds, kk, preferred_element_type=F32))
                dks.append(lax.dot_general(ds, qq, (((0,), (0,)), ((), ())), preferred_element_type=F32))
                dvs.append(lax.dot_general(p.astype(BF16), dob, (((0,), (0,)), ((), ())), preferred_element_type=F32))
            dq_acc[rows, :] = jnp.where(first, dqs[0], dqs[1])
            dk2 = jnp.where(first_k, dks[0], dks[1])
            dv2 = jnp.where(first_k, dvs[0], dvs[1])
            dk_acc[rows, :] += dk2[nk - BLOCK:]
            dv_acc[rows, :] += dv2[nk - BLOCK:]
            if two_blocks:
                @pl.when(i > 0)
                def _():
                    dk_acc[prev, :] += dk2[:BLOCK]
                    dv_acc[prev, :] += dv2[:BLOCK]
            return carry

        lax.fori_loop(0, UNITS, unit, 0)

        def finish(t, carry):
            rows = pl.ds(pl.multiple_of(t * ROPE_ROWS, ROPE_ROWS), ROPE_ROWS)
            cf, ss = cf_ref[rows, :], ss_ref[rows, :]
            dq = dq_acc[rows, :] * (1.0 / math.sqrt(HEAD_DIM))
            dq_out[rows, :] = _rope_transposed(dq, cf, ss).astype(BF16)
            dk_out[rows, :] = _rope_transposed(dk_acc[rows, :], cf, ss).astype(BF16)
            dv_out[rows, :] = dv_acc[rows, :].astype(BF16)
            return carry

        lax.fori_loop(0, SEQ // ROPE_ROWS, finish, 0)

    whole = pl.BlockSpec((SEQ, PAIR), lambda hp: (0, 0))
    pair = pl.BlockSpec((SEQ, PAIR), lambda hp: (0, hp))
    views = [_pair_views(col0)[g] for col0 in (0, QKV_WIDTH, 2 * QKV_WIDTH)]
    return pl.pallas_call(
        body, grid=(ATTN_WIDTH // PAIR,), in_specs=views + [whole, whole, pair, pair, pair],
        out_specs=[pair, pair, pair], out_shape=[_sds((SEQ, ATTN_WIDTH), BF16)] * 3,
        scratch_shapes=[pltpu.VMEM((SEQ, PAIR), F32)] * 4,
        compiler_params=_cp(("parallel",)), name=f"attention_bwd_{g}")(proj, proj, proj, cos_f, sin_s, d_attn, attn, lse)


def _cmul(ar, ai, br, bi):
    return ar * br - ai * bi, ar * bi + ai * br


def _pow256(ar, ai):
    for _ in range(8):
        ar, ai = _cmul(ar, ai, ar, ai)
    return ar, ai


def _chunk_carries(first_r, first_i, pr, pi, reverse):
    rows = lax.broadcasted_iota(jnp.int32, first_r.shape, 0)
    out_r = jnp.zeros_like(first_r)
    out_i = jnp.zeros_like(first_i)
    hr = jnp.zeros_like(first_r[0:1])
    hi = jnp.zeros_like(hr)
    order = range(SCAN_CHUNKS - 1, -1, -1) if reverse else range(SCAN_CHUNKS)
    for c in order:
        out_r = jnp.where(rows == c, hr, out_r)
        out_i = jnp.where(rows == c, hi, out_i)
        tr, ti = _cmul(pr[0:1], pi[0:1], hr, hi)
        hr = first_r[c:c + 1] + tr
        hi = first_i[c:c + 1] + ti
    return out_r, out_i


def _tile(j):
    return pl.ds(pl.multiple_of(j * SCAN_CHUNKS, SCAN_CHUNKS), SCAN_CHUNKS)


def _to_scan_rows(t):
    return t.reshape(SCAN_CHUNKS, SCAN_STEPS, t.shape[1]).transpose(1, 0, 2).reshape(t.shape)


def _from_scan_rows(t):
    return t.reshape(SCAN_STEPS, SCAN_CHUNKS, t.shape[1]).transpose(1, 0, 2).reshape(t.shape)


def _scan_fwd(bur, bui, ar, ai):
    lb = SCAN_LANES

    def body(bur_ref, bui_ref, ar_ref, ai_ref, hr_ref, hi_ref, er_ref, ei_ref):
        a_r = jnp.broadcast_to(ar_ref[...], (SCAN_CHUNKS, lb))
        a_i = jnp.broadcast_to(ai_ref[...], (SCAN_CHUNKS, lb))

        def local(j, carry):
            tr, ti = _cmul(a_r, a_i, carry[0], carry[1])
            nr = tr + bur_ref[_tile(j), :]
            ni = ti + bui_ref[_tile(j), :]
            hr_ref[_tile(j), :] = nr
            hi_ref[_tile(j), :] = ni
            return nr, ni

        zero = jnp.zeros((SCAN_CHUNKS, lb), F32)
        last_r, last_i = lax.fori_loop(0, SCAN_STEPS, local, (zero, zero), unroll=4)
        pr, pi = _pow256(a_r, a_i)
        er, ei = _chunk_carries(last_r, last_i, pr, pi, reverse=False)
        er_ref[...] = er
        ei_ref[...] = ei

        def fix(j, carry):
            tr, ti = _cmul(carry[0], carry[1], er, ei)
            hr_ref[_tile(j), :] += tr
            hi_ref[_tile(j), :] += ti
            return _cmul(carry[0], carry[1], a_r, a_i)

        lax.fori_loop(0, SCAN_STEPS, fix, (a_r, a_i), unroll=4)

    big = pl.BlockSpec((SEQ, lb), lambda j: (0, j))
    vec = pl.BlockSpec((1, lb), lambda j: (0, j))
    ent = pl.BlockSpec((SCAN_CHUNKS, lb), lambda j: (0, j))
    return pl.pallas_call(
        body, grid=(SSM_LANES // lb,), in_specs=[big, big, vec, vec], out_specs=[big, big, ent, ent],
        out_shape=[_sds((SEQ, SSM_LANES)), _sds((SEQ, SSM_LANES)), _sds((SCAN_CHUNKS, SSM_LANES)),
                   _sds((SCAN_CHUNKS, SSM_LANES))],
        compiler_params=_cp(("parallel",)), name="ssm_scan_fwd")(bur, bui, ar, ai)


def _scan_bwd(gr, gi, hr, hi, er, ei, ar, ai):
    lb = SCAN_LANES

    def body(gr_ref, gi_ref, hr_ref, hi_ref, er_ref, ei_ref, ar_ref, ai_ref, lr_ref, li_ref, dar_ref, dai_ref):
        a_r = jnp.broadcast_to(ar_ref[...], (SCAN_CHUNKS, lb))
        a_i = -jnp.broadcast_to(ai_ref[...], (SCAN_CHUNKS, lb))

        def local(t, carry):
            j = SCAN_STEPS - 1 - t
            tr, ti = _cmul(a_r, a_i, carry[0], carry[1])
            nr = tr + gr_ref[_tile(j), :]
            ni = ti + gi_ref[_tile(j), :]
            lr_ref[_tile(j), :] = nr
            li_ref[_tile(j), :] = ni
            return nr, ni

        zero = jnp.zeros((SCAN_CHUNKS, lb), F32)
        first_r, first_i = lax.fori_loop(0, SCAN_STEPS, local, (zero, zero), unroll=4)
        pr, pi = _pow256(a_r, a_i)
        nxt_r, nxt_i = _chunk_carries(first_r, first_i, pr, pi, reverse=True)

        def accumulate(lam_r, lam_i, hp_r, hp_i, acc):
            return (acc[0] + lam_r * hp_r + lam_i * hp_i, acc[1] + lam_i * hp_r - lam_r * hp_i)

        def fix(t, carry):
            qr, qi, acc_r, acc_i = carry
            j = SCAN_STEPS - 1 - t
            tr, ti = _cmul(qr, qi, nxt_r, nxt_i)
            lam_r = lr_ref[_tile(j), :] + tr
            lam_i = li_ref[_tile(j), :] + ti
            lr_ref[_tile(j), :] = lam_r
            li_ref[_tile(j), :] = lam_i
            acc_r, acc_i = accumulate(lam_r, lam_i, hr_ref[_tile(j - 1), :], hi_ref[_tile(j - 1), :], (acc_r, acc_i))
            qr, qi = _cmul(qr, qi, a_r, a_i)
            return qr, qi, acc_r, acc_i

        qr, qi, acc_r, acc_i = lax.fori_loop(0, SCAN_STEPS - 1, fix, (a_r, a_i, zero, zero), unroll=4)
        tr, ti = _cmul(qr, qi, nxt_r, nxt_i)
        lam_r = lr_ref[_tile(0), :] + tr
        lam_i = li_ref[_tile(0), :] + ti
        lr_ref[_tile(0), :] = lam_r
        li_ref[_tile(0), :] = lam_i
        acc_r, acc_i = accumulate(lam_r, lam_i, er_ref[...], ei_ref[...], (acc_r, acc_i))
        dar_ref[...] = jnp.sum(acc_r, axis=0, keepdims=True)
        dai_ref[...] = jnp.sum(acc_i, axis=0, keepdims=True)

    big = pl.BlockSpec((SEQ, lb), lambda j: (0, j))
    vec = pl.BlockSpec((1, lb), lambda j: (0, j))
    ent = pl.BlockSpec((SCAN_CHUNKS, lb), lambda j: (0, j))
    return pl.pallas_call(
        body, grid=(SSM_LANES // lb,), in_specs=[big, big, big, big, ent, ent, vec, vec],
        out_specs=[big, big, vec, vec],
        out_shape=[_sds((SEQ, SSM_LANES)), _sds((SEQ, SSM_LANES)), _sds((1, SSM_LANES)), _sds((1, SSM_LANES))],
        compiler_params=_cp(("parallel",)), name="ssm_scan_bwd")(gr, gi, hr, hi, er, ei, ar, ai)


def _rope_tables():
    half = HEAD_DIM // 2
    inv_freq = ROPE_THETA ** (-jnp.arange(half, dtype=F32) / half)
    ang = jnp.arange(SEQ, dtype=F32)[:, None] * inv_freq[None, :]
    cos, sin = jnp.cos(ang), jnp.sin(ang)
    cos_f = jnp.concatenate([cos, cos, cos, cos], axis=1)
    sin_s = jnp.concatenate([-sin, sin, -sin, sin], axis=1)
    return cos_f, sin_s


def _ssm_discretise(a_re, a_im, log_dt, b_re, b_im):
    lam = lax.complex(a_re, a_im)
    dt = jnp.exp(log_dt)[:, None]
    a_bar = jnp.exp(lam * dt)
    b_bar = ((a_bar - 1.0) / lam)[..., None] * lax.complex(b_re, b_im)
    return a_bar.real, a_bar.imag, b_bar.real, b_bar.imag


def _block_diag_in(b):
    eye = jnp.eye(SSM_GROUPS, dtype=b.dtype)
    return (eye[:, None, :, None] * b.transpose(0, 2, 1)[:, :, None, :]).reshape(SSM_WIDTH, SSM_LANES)


def _block_diag_out(c):
    eye = jnp.eye(SSM_GROUPS, dtype=c.dtype)
    return (eye[:, None, :, None] * c.transpose(0, 2, 1)[:, :, None, :]).reshape(SSM_LANES, SSM_WIDTH)


SSM_SLABS = 4
SLAB_GROUPS = SSM_GROUPS // SSM_SLABS
SLAB_IN = SSM_WIDTH // SSM_SLABS
SLAB_STATE = SSM_LANES // SSM_SLABS


def _ssm_in(u, b_in, *, name, transpose=False):
    if transpose:
        return _matmul(u, b_in, grid=(SSM_SLABS,), a_spec=pl.BlockSpec((SEQ, SLAB_STATE), lambda j: (0, j)),
                       b_spec=pl.BlockSpec((SLAB_IN, SLAB_STATE), lambda j: (j, j)),
                       o_spec=pl.BlockSpec((SEQ, SLAB_IN), lambda j: (0, j)), out_shape=_sds((SEQ, SSM_WIDTH)),
                       dims=(1, 1), name=name)
    return _matmul(u, b_in, grid=(SSM_SLABS,), a_spec=pl.BlockSpec((SEQ, SLAB_IN), lambda j: (0, j)),
                   b_spec=pl.BlockSpec((SLAB_IN, SLAB_STATE), lambda j: (j, j)),
                   o_spec=pl.BlockSpec((SEQ, SLAB_STATE), lambda j: (0, j)), out_shape=_sds((SEQ, SSM_LANES)),
                   dims=(1, 0), name=name)


def _ssm_out(h, c_out, *, name, transpose=False):
    if transpose:
        return _matmul(h, c_out, grid=(SSM_SLABS,), a_spec=pl.BlockSpec((SEQ, SLAB_IN), lambda j: (0, j)),
                       b_spec=pl.BlockSpec((SLAB_STATE, SLAB_IN), lambda j: (j, j)),
                       o_spec=pl.BlockSpec((SEQ, SLAB_STATE), lambda j: (0, j)), out_shape=_sds((SEQ, SSM_LANES)),
                       dims=(1, 1), name=name)
    return _matmul(h, c_out, grid=(SSM_SLABS,), a_spec=pl.BlockSpec((SEQ, SLAB_STATE), lambda j: (0, j)),
                   b_spec=pl.BlockSpec((SLAB_STATE, SLAB_IN), lambda j: (j, j)),
                   o_spec=pl.BlockSpec((SEQ, SLAB_IN), lambda j: (0, j)), out_shape=_sds((SEQ, SSM_WIDTH)),
                   dims=(1, 0), name=name)


def _diag_block_grad(a, b, *, name):
    ra = a.shape[1] // SSM_SLABS
    cb = b.shape[1] // SSM_SLABS
    wa, wb = ra // SLAB_GROUPS, cb // SLAB_GROUPS

    def body(a_ref, b_ref, o_ref):
        d = lax.dot_general(a_ref[...].astype(BF16), b_ref[...].astype(BF16), (((0,), (0,)), ((), ())),
                            preferred_element_type=F32)
        row_g = jnp.right_shift(lax.broadcasted_iota(jnp.int32, (ra, cb), 0), wa.bit_length() - 1)
        col_g = jnp.right_shift(lax.broadcasted_iota(jnp.int32, (ra, cb), 1), wb.bit_length() - 1)
        d = jnp.where(row_g == col_g, d, 0.0)
        fold = (jnp.bitwise_and(lax.broadcasted_iota(jnp.int32, (cb, wb), 0), wb - 1)
                == lax.broadcasted_iota(jnp.int32, (cb, wb), 1)).astype(F32)
        o_ref[...] = jnp.dot(d, fold, preferred_element_type=F32, precision=lax.Precision.HIGHEST)

    return pl.pallas_call(
        body, grid=(SSM_SLABS,), in_specs=[pl.BlockSpec((SEQ, ra), lambda j: (0, j)),
                                           pl.BlockSpec((SEQ, cb), lambda j: (0, j))],
        out_specs=pl.BlockSpec((ra, wb), lambda j: (j, 0)), out_shape=_sds((a.shape[1], wb)),
        compiler_params=_cp(("parallel",)), name=name)(a, b)


def _local_step(x, tgt, wts, small):
    s = SEQ
    cos_f, sin_s = _rope_tables()

    proj = _mm_cols(x, wts["w_in"], tm=1024, name="proj")

    attn, lse = _attention_fwd(proj, cos_f, sin_s)
    y_attn = _mm_cols(attn, wts["w_attn_br"], tm=s, name="y_attn")

    (abar_r, abar_i, bbar_r, bbar_i), ssm_vjp = jax.vjp(
        _ssm_discretise, small["ssm_a_re"], small["ssm_a_im"], small["ssm_log_dt"], small["ssm_b_re"], small["ssm_b_im"])
    b_in_r, b_in_i = _block_diag_in(bbar_r).astype(BF16), _block_diag_in(bbar_i).astype(BF16)
    c_out_r = _block_diag_out(small["ssm_c_re"]).astype(BF16)
    c_out_ni = _block_diag_out(-small["ssm_c_im"]).astype(BF16)
    a_r, a_i = abar_r.reshape(1, SSM_LANES), abar_i.reshape(1, SSM_LANES)
    d_skip = small["ssm_d"].reshape(1, SSM_WIDTH)

    u_f = _to_scan_rows(proj[:, 3 * QKV_WIDTH:3 * QKV_WIDTH + SSM_WIDTH])
    u_p = u_f.astype(BF16)
    bu_r = _ssm_in(u_p, b_in_r, name="ssm_bu_re")
    bu_i = _ssm_in(u_p, b_in_i, name="ssm_bu_im")
    h_r, h_i, e_r, e_i = _scan_fwd(bu_r, bu_i, a_r, a_i)
    y_1 = _ssm_out(h_r, c_out_r, name="ssm_y_re")
    y_2 = _ssm_out(h_i, c_out_ni, name="ssm_y_im")

    def gelu_fwd(y1, y2, u, dsk):
        y = y1 + y2 + dsk * u
        return y, 0.5 * y * (1.0 + jnp.tanh(GELU_C * (y + GELU_K * y * y * y)))

    y_s5, gel = _rowwise(gelu_fwd, [y_1, y_2, u_f], [d_skip], [_sds((s, SSM_WIDTH)), _sds((s, SSM_WIDTH), BF16)],
                         tm=512, name="ssm_gelu")
    glu = _mm_cols(gel, wts["w_glu"], tm=s, name="glu")

    def glu_fwd(ga, gb):
        return ga * jax.nn.sigmoid(gb)

    (y_glu,) = _rowwise(glu_fwd, [(glu, SSM_WIDTH, 0), (glu, SSM_WIDTH, 1)], [], [_sds((s, SSM_WIDTH), BF16)],
                        tm=512, name="glu_gate")
    y_glu = _from_scan_rows(y_glu)
    y_ssm = _mm_cols(y_glu, wts["w_ssm_br"], tm=s, name="y_ssm")

    gl0 = (proj, D_MODEL, (3 * QKV_WIDTH + SSM_WIDTH) // D_MODEL)
    gl1 = (proj, D_MODEL, (3 * QKV_WIDTH + SSM_WIDTH) // D_MODEL + 1)
    b_gate = small["b_gate"]

    def gate_mix(l0, l1, ya, ys, bg):
        return jax.nn.sigmoid(l0 + bg[0:1]) * ya + jax.nn.sigmoid(l1 + bg[1:2]) * ys

    (mixed,) = _rowwise(gate_mix, [gl0, gl1, y_attn, y_ssm], [b_gate], [_sds((s, D_MODEL), BF16)], tm=256,
                        name="gate_mix")
    w_out = wts["w_out"].reshape(D_MODEL, D_MODEL)
    mix_out = _mm_plain(mixed, w_out, tm=1024, tn=512, name="mix_out")

    def ln1_fwd(xv, mo, g, b):
        z = DN_ALPHA * xv + mo
        xhat, _ = _ln_stats(z)
        return z, xhat * g + b

    z1, h = _rowwise(ln1_fwd, [x, mix_out], [small["ln1_g"], small["ln1_b"]],
                     [_sds((s, D_MODEL)), _sds((s, D_MODEL))], tm=256, name="ln1")

    ff_a = _mm_cols(h, wts["w_ff_gate"], tm=1024, name="ff_gate", out3d=True)
    ff_b = _mm_cols(h, wts["w_ff_up"], tm=1024, name="ff_up", out3d=True)
    nf = D_FF // N_CHIPS

    def swiglu_fwd(a, b):
        return a * jax.nn.sigmoid(a) * b

    (act,) = _rowwise(swiglu_fwd, [ff_a.reshape(N_CHIPS * s, nf), ff_b.reshape(N_CHIPS * s, nf)], [],
                      [_sds((N_CHIPS * s, nf), BF16)], tm=1024, name="swiglu")
    act = act.reshape(N_CHIPS, s, nf)
    w_down = wts["w_ff_down"]
    ff = _matmul(act, w_down, grid=(2, N_CHIPS),
                 a_spec=pl.BlockSpec((None, 1024, nf), lambda i, k: (k, i, 0)),
                 b_spec=pl.BlockSpec((None, nf, D_MODEL), lambda i, k: (k, 0, 0)),
                 o_spec=pl.BlockSpec((1024, D_MODEL), lambda i, k: (i, 0)),
                 out_shape=_sds((s, D_MODEL)), dims=(1, 0), k_axis=1, name="ff_down")

    def ln2_loss(hv, ffv, tg, g, b):
        z = DN_ALPHA * hv + ffv
        xhat, rstd = _ln_stats(z)
        err = xhat * g + b - tg
        d_out = err * (1.0 / D_MODEL)
        loss_rows = jnp.sum(err * err, axis=-1, keepdims=True) * (0.5 / D_MODEL)
        loss = jnp.broadcast_to(jnp.sum(loss_rows, axis=0, keepdims=True), (1, 128))
        return _ln_bwd(d_out, xhat, rstd, g), loss, _colsum(d_out * xhat), _colsum(d_out)

    dz2, loss_v, d_ln2_g, d_ln2_b = _rowwise(
        ln2_loss, [h, ff, tgt], [small["ln2_g"], small["ln2_b"]], [_sds((s, D_MODEL))],
        [_sds((1, 128)), _sds((1, D_MODEL)), _sds((1, D_MODEL))], tm=256, name="ln2_loss")

    d_act = _matmul(dz2, w_down, grid=(2, N_CHIPS),
                    a_spec=pl.BlockSpec((1024, D_MODEL), lambda i, k: (i, 0)),
                    b_spec=pl.BlockSpec((None, nf, D_MODEL), lambda i, k: (k, 0, 0)),
                    o_spec=pl.BlockSpec((None, 1024, nf), lambda i, k: (k, i, 0)),
                    out_shape=_sds((N_CHIPS, s, nf)), dims=(1, 1), name="d_act")
    g_w_ff_down = _matmul(act, dz2, grid=(N_CHIPS,),
                          a_spec=pl.BlockSpec((None, s, nf), lambda k: (k, 0, 0)),
                          b_spec=pl.BlockSpec((s, D_MODEL), lambda k: (0, 0)),
                          o_spec=pl.BlockSpec((None, nf, D_MODEL), lambda k: (k, 0, 0)),
                          out_shape=_sds((N_CHIPS, nf, D_MODEL), BF16), dims=(0, 0), name="g_w_ff_down")

    def swiglu_bwd(da, a, b):
        sg = jax.nn.sigmoid(a)
        return da * b * sg * (1.0 + a * (1.0 - sg)), da * a * sg

    d_a, d_b = _rowwise(swiglu_bwd, [d_act.reshape(N_CHIPS * s, nf), ff_a.reshape(N_CHIPS * s, nf),
                                     ff_b.reshape(N_CHIPS * s, nf)], [],
                        [_sds((N_CHIPS * s, nf), BF16)] * 2, tm=1024, name="swiglu_bwd")
    d_a, d_b = d_a.reshape(N_CHIPS, s, nf), d_b.reshape(N_CHIPS, s, nf)
    g_w_ff_gate = _mm_cols_tn(h, d_a, ns=nf, name="g_w_ff_gate", dy3d=True)
    g_w_ff_up = _mm_cols_tn(h, d_b, ns=nf, name="g_w_ff_up", dy3d=True)
    dh_a = _mm_cols_nt(d_a, wts["w_ff_gate"], tm=1024, name="dh_gate", dy3d=True)
    dh_b = _mm_cols_nt(d_b, wts["w_ff_up"], tm=1024, name="dh_up", dy3d=True)

    def ln1_bwd(dz, da, db, z, g):
        xhat, rstd = _ln_stats(z)
        dh = DN_ALPHA * dz + da + db
        return _ln_bwd(dh, xhat, rstd, g), _colsum(dh * xhat), _colsum(dh)

    dz1, d_ln1_g, d_ln1_b = _rowwise(ln1_bwd, [dz2, dh_a, dh_b, z1], [small["ln1_g"]], [_sds((s, D_MODEL))],
                                     [_sds((1, D_MODEL)), _sds((1, D_MODEL))], tm=256, name="ln1_bwd")
    d_mixed = _mm_plain(dz1, w_out, tm=1024, tn=512, dims=(1, 1), name="d_mixed")
    g_w_out = _mm_plain(mixed, dz1, tm=D_MODEL, tn=512, dims=(0, 0), out_dtype=BF16, name="g_w_out")
    g_w_out = g_w_out.reshape(N_CHIPS, D_MODEL // N_CHIPS, D_MODEL)

    def gate_bwd(dm, l0, l1, ya, ys, bg):
        g0 = jax.nn.sigmoid(l0 + bg[0:1])
        g1 = jax.nn.sigmoid(l1 + bg[1:2])
        dl0 = dm * ya * g0 * (1.0 - g0)
        dl1 = dm * ys * g1 * (1.0 - g1)
        return dm * g0, dm * g1, jnp.concatenate([dl0, dl1], axis=1), _colsum(dl0), _colsum(dl1)

    d_y_attn, d_y_ssm, d_gl, d_bg0, d_bg1 = _rowwise(
        gate_bwd, [d_mixed, gl0, gl1, y_attn, y_ssm], [b_gate],
        [_sds((s, D_MODEL), BF16), _sds((s, D_MODEL), BF16), _sds((s, 2 * D_MODEL), BF16)],
        [_sds((1, D_MODEL)), _sds((1, D_MODEL))], tm=256, name="gate_bwd")

    g_w_ssm_br = _mm_cols_tn(y_glu, d_y_ssm, ns=D_MODEL // N_CHIPS, name="g_w_ssm_br")
    d_y_glu = _to_scan_rows(_mm_cols_nt(d_y_ssm, wts["w_ssm_br"], tm=s, name="d_y_glu"))

    def glu_bwd(dy, ga, gb):
        sg = jax.nn.sigmoid(gb)
        return jnp.concatenate([dy * sg, dy * ga * sg * (1.0 - sg)], axis=1)

    (d_glu,) = _rowwise(glu_bwd, [d_y_glu, (glu, SSM_WIDTH, 0), (glu, SSM_WIDTH, 1)], [],
                        [_sds((s, 2 * SSM_WIDTH), BF16)], tm=512, name="glu_bwd")
    g_w_glu = _mm_cols_tn(gel, d_glu, ns=2 * SSM_WIDTH // N_CHIPS, name="g_w_glu")
    d_gel = _mm_cols_nt(d_glu, wts["w_glu"], tm=s, name="d_gel")

    def gelu_bwd(dg, y, u, dsk):
        th = jnp.tanh(GELU_C * (y + GELU_K * y * y * y))
        dy = dg * (0.5 * (1.0 + th) + 0.5 * y * (1.0 - th * th) * GELU_C * (1.0 + 3.0 * GELU_K * y * y))
        return dy, dy * dsk, _colsum(dy * u)

    d_y, d_u_skip, d_ssm_d = _rowwise(gelu_bwd, [d_gel, y_s5, u_f], [d_skip],
                                      [_sds((s, SSM_WIDTH), BF16), _sds((s, SSM_WIDTH))], [_sds((1, SSM_WIDTH))],
                                      tm=512, name="gelu_bwd")
    g_h_r = _ssm_out(d_y, c_out_r, name="ssm_gh_re", transpose=True)
    g_h_i = _ssm_out(d_y, c_out_ni, name="ssm_gh_im", transpose=True)
    d_c_r = _diag_block_grad(h_r, d_y, name="ssm_dc_re")
    d_c_ni = _diag_block_grad(h_i, d_y, name="ssm_dc_im")
    lam_r, lam_i, d_abar_r, d_abar_i = _scan_bwd(g_h_r, g_h_i, h_r, h_i, e_r, e_i, a_r, a_i)
    d_bin_r = _diag_block_grad(u_p, lam_r, name="ssm_db_re")
    d_bin_i = _diag_block_grad(u_p, lam_i, name="ssm_db_im")
    d_u_r = _ssm_in(lam_r, b_in_r, name="ssm_du_re", transpose=True)
    d_u_i = _ssm_in(lam_i, b_in_i, name="ssm_du_im", transpose=True)

    def add3(a, b, c):
        return a + b + c

    (d_u,) = _rowwise(add3, [d_u_skip, d_u_r, d_u_i], [], [_sds((s, SSM_WIDTH), BF16)], tm=512, name="ssm_du")
    d_u = _from_scan_rows(d_u)
    d_bbar_r = d_bin_r.reshape(SSM_GROUPS, SSM_GROUP, SSM_STATE).transpose(0, 2, 1)
    d_bbar_i = d_bin_i.reshape(SSM_GROUPS, SSM_GROUP, SSM_STATE).transpose(0, 2, 1)
    d_a_re, d_a_im, d_log_dt, d_b_re, d_b_im = ssm_vjp(
        (d_abar_r.reshape(SSM_GROUPS, SSM_STATE), d_abar_i.reshape(SSM_GROUPS, SSM_STATE), d_bbar_r, d_bbar_i))
    d_c_re = d_c_r.reshape(SSM_GROUPS, SSM_STATE, SSM_GROUP).transpose(0, 2, 1)
    d_c_im = -d_c_ni.reshape(SSM_GROUPS, SSM_STATE, SSM_GROUP).transpose(0, 2, 1)

    g_w_attn_br = _mm_cols_tn(attn, d_y_attn, ns=D_MODEL // N_CHIPS, name="g_w_attn_br")
    d_attn = _mm_cols_nt(d_y_attn, wts["w_attn_br"], tm=s, name="d_attn")
    dqkv = [_attention_bwd(g, proj, cos_f, sin_s, d_attn, attn, lse) for g in range(len(DILATIONS))]

    d_proj = jnp.concatenate([dqkv[g][j] for j in range(3) for g in range(len(DILATIONS))] + [d_u, d_gl],
                             axis=1)
    g_w_in = _mm_cols_tn(x, d_proj, ns=IN_WIDTH // N_CHIPS, name="g_w_in")
    dx_proj = _mm_cols_nt(d_proj, wts["w_in"], tm=1024, name="dx_proj", after=(g_w_in,))

    def dx_sum(dz, dxp):
        return DN_ALPHA * dz + dxp

    (grad_x,) = _rowwise(dx_sum, [dz1, dx_proj], [], [_sds((s, D_MODEL))], tm=512, name="grad_x")

    big = {"w_in": g_w_in, "w_attn_br": g_w_attn_br, "w_ssm_br": g_w_ssm_br, "w_out": g_w_out, "w_glu": g_w_glu,
           "w_ff_gate": g_w_ff_gate, "w_ff_up": g_w_ff_up, "w_ff_down": g_w_ff_down}
    small_g = {"b_gate": jnp.concatenate([d_bg0, d_bg1], axis=0), "ssm_a_re": d_a_re, "ssm_a_im": d_a_im,
               "ssm_log_dt": d_log_dt, "ssm_b_re": d_b_re, "ssm_b_im": d_b_im, "ssm_c_re": d_c_re, "ssm_c_im": d_c_im,
               "ssm_d": d_ssm_d.reshape(SSM_WIDTH), "ln1_g": d_ln1_g, "ln1_b": d_ln1_b, "ln2_g": d_ln2_g,
               "ln2_b": d_ln2_b}
    marks = {"ln1_bwd": dz1, "scan_bwd": lam_r, "attention_bwd_0": dqkv[0][0], "dx_proj": dx_proj}
    return loss_v[0, 0], grad_x, big, small_g, marks


GATHER_ID, SWAP_ID, SCATTER_ID, JOIN_ID = 1, 2, 3, 4


def _place():
    return lax.axis_index("x"), lax.axis_index("y"), lax.axis_index("c")


def _other_chips(x, y):
    return [(1 - x, y), (x, 1 - y), (1 - x, 1 - y)]


def _handshake(peers):
    barrier = pltpu.get_barrier_semaphore()
    for peer in peers:
        pl.semaphore_signal(barrier, inc=1, device_id=peer, device_id_type=MESH)
    pl.semaphore_wait(barrier, len(peers))


def _sequencer(body, arrays, out_type, sems, collective_id, name):
    return pl.kernel(body, name=name, out_type=out_type,
                     mesh=plsc.ScalarSubcoreMesh(axis_name="sequencer", num_cores=1), scratch_types=sems,
                     compiler_params=pltpu.CompilerParams(collective_id=collective_id))(*arrays)


def _gather_weights(shards, *, name):
    nw = len(shards)

    def body(*refs):
        ins, outs = refs[:nw], refs[nw:2 * nw]
        send_sems, recv_sems, pass_send, pass_recv, local_sems = refs[2 * nw:]
        x, y, c = _place()
        chip = 2 * x + y
        chips = _other_chips(x, y)
        _handshake([(x, y, 1 - c)] + [(cx, cy, c) for cx, cy in chips])
        started = []
        for w in range(nw):
            hw = shards[w].shape[0] // 2
            mine = pl.ds(c * hw, hw)
            own = pltpu.make_async_copy(ins[w], outs[w].at[chip], local_sems.at[w])
            own.start()
            started.append(own)
            for j, (cx, cy) in enumerate(chips):
                cp = pltpu.make_async_remote_copy(
                    src_ref=ins[w].at[mine], dst_ref=outs[w].at[chip, mine], send_sem=send_sems.at[w, j],
                    recv_sem=recv_sems.at[w, j], device_id=(cx, cy, c), device_id_type=MESH)
                cp.start()
                started.append(cp)
        passed = []
        for w in range(nw):
            hw = shards[w].shape[0] // 2
            mine = pl.ds(c * hw, hw)
            for j, (cx, cy) in enumerate(chips):
                landed = outs[w].at[2 * cx + cy, mine]
                pltpu.make_async_remote_copy(
                    src_ref=ins[w].at[mine], dst_ref=landed, send_sem=send_sems.at[w, j],
                    recv_sem=recv_sems.at[w, j], device_id=(cx, cy, c), device_id_type=MESH).wait_recv()
                cp = pltpu.make_async_remote_copy(
                    src_ref=landed, dst_ref=landed, send_sem=pass_send.at[w, j], recv_sem=pass_recv.at[w, j],
                    device_id=(x, y, 1 - c), device_id_type=MESH)
                cp.start()
                passed.append(cp)
        for w in range(nw):
            hw = shards[w].shape[0] // 2
            theirs = pl.ds((1 - c) * hw, hw)
            for j, (cx, cy) in enumerate(chips):
                landed = outs[w].at[2 * cx + cy, theirs]
                pltpu.make_async_remote_copy(
                    src_ref=landed, dst_ref=landed, send_sem=pass_send.at[w, j], recv_sem=pass_recv.at[w, j],
                    device_id=(x, y, 1 - c), device_id_type=MESH).wait_recv()
        for cp in started[0::4]:
            cp.wait()
        for cp in [s for i, s in enumerate(started) if i % 4] + passed:
            cp.wait_send()

    sem = pltpu.SemaphoreType.DMA
    return _sequencer(body, shards, [_sds((N_CHIPS,) + a.shape, a.dtype) for a in shards],
                      [sem((nw, 3)), sem((nw, 3)), sem((nw, 3)), sem((nw, 3)), sem((nw,))], GATHER_ID, name)


def _swap_other_halves(grads, *, name):
    nw = len(grads)

    def body(*refs):
        ins, outs = refs[:nw], refs[nw:2 * nw]
        send_sems, recv_sems = refs[2 * nw:]
        x, y, c = _place()
        _handshake([(x, y, 1 - c)])
        cps = []
        for w in range(nw):
            hw = grads[w].shape[1] // 2
            cp = pltpu.make_async_remote_copy(
                src_ref=ins[w].at[:, pl.ds((1 - c) * hw, hw)], dst_ref=outs[w], send_sem=send_sems.at[w],
                recv_sem=recv_sems.at[w], device_id=(x, y, 1 - c), device_id_type=MESH)
            cp.start()
            cps.append(cp)
        for cp in cps:
            cp.wait()

    sem = pltpu.SemaphoreType.DMA
    return _sequencer(body, grads, [_sds((N_CHIPS, g.shape[1] // 2, g.shape[2]), g.dtype) for g in grads],
                      [sem((nw,)), sem((nw,))], SWAP_ID, name)


def _add_my_half(core, g, other, after=()):
    n, r, cols = g.shape
    hw = r // 2

    def body(core_ref, g_ref, o_ref, *rest):
        out_ref = rest[len(after)]
        out_ref[...] = (g_ref[...].astype(F32) + o_ref[...].astype(F32)).astype(out_ref.dtype)

    return pl.pallas_call(
        body,
        grid_spec=pltpu.PrefetchScalarGridSpec(
            num_scalar_prefetch=1, grid=(n,),
            in_specs=[pl.BlockSpec((None, None, hw, cols), lambda s, core_ref: (s, core_ref[0], 0, 0)),
                      pl.BlockSpec((None, hw, cols), lambda s, core_ref: (s, 0, 0))] + [HBM_OPERAND] * len(after),
            out_specs=pl.BlockSpec((None, hw, cols), lambda s, core_ref: (s, 0, 0))),
        out_shape=_sds((n, hw, cols), BF16), compiler_params=_cp(("parallel",)),
        name="add_my_half")(core, g.reshape(n, 2, hw, cols), other, *after)


def _scatter_partials(parts, *, name):
    nw = len(parts)

    def body(*refs):
        ins, outs = refs[:nw], refs[nw:2 * nw]
        send_sems, recv_sems = refs[2 * nw:]
        x, y, c = _place()
        _handshake([(cx, cy, c) for cx, cy in _other_chips(x, y)])
        cps = []
        for w in range(nw):
            for j, (cx, cy) in enumerate(_other_chips(x, y)):
                cp = pltpu.make_async_remote_copy(
                    src_ref=ins[w].at[2 * cx + cy], dst_ref=outs[w].at[j], send_sem=send_sems.at[w, j],
                    recv_sem=recv_sems.at[w, j], device_id=(cx, cy, c), device_id_type=MESH)
                cp.start()
                cps.append(cp)
        for cp in cps:
            cp.wait()

    sem = pltpu.SemaphoreType.DMA
    return _sequencer(body, parts, [_sds((3,) + p.shape[1:], p.dtype) for p in parts],
                      [sem((nw, 3)), sem((nw, 3))], SCATTER_ID, name)


def _sum_partials(chip, part, recv, after=()):
    _, hw, cols = part.shape
    th = hw // 2 if hw % 32 == 0 else hw

    def body(chip_ref, p_ref, r_ref, *rest):
        out_ref = rest[len(after)]
        acc = p_ref[...].astype(F32)
        for j in range(3):
            acc = acc + r_ref[j].astype(F32)
        out_ref[...] = acc

    return pl.pallas_call(
        body,
        grid_spec=pltpu.PrefetchScalarGridSpec(
            num_scalar_prefetch=1, grid=(hw // th,),
            in_specs=[pl.BlockSpec((None, th, cols), lambda i, chip_ref: (chip_ref[0], i, 0)),
                      pl.BlockSpec((3, th, cols), lambda i, chip_ref: (0, i, 0))] + [HBM_OPERAND] * len(after),
            out_specs=pl.BlockSpec((th, cols), lambda i, chip_ref: (i, 0))),
        out_shape=_sds((hw, cols)), compiler_params=_cp(("parallel",)), name="sum_partials")(
            chip, part, recv, *after)


def _swap_reduced_halves(halves, *, name):
    nw = len(halves)

    def body(*refs):
        ins, outs = refs[:nw], refs[nw:2 * nw]
        send_sems, recv_sems = refs[2 * nw:]
        x, y, c = _place()
        _handshake([(x, y, 1 - c)])
        cps = []
        for w in range(nw):
            cp = pltpu.make_async_remote_copy(
                src_ref=ins[w], dst_ref=outs[w], send_sem=send_sems.at[w], recv_sem=recv_sems.at[w],
                device_id=(x, y, 1 - c), device_id_type=MESH)
            cp.start()
            cps.append(cp)
        for cp in cps:
            cp.wait()

    sem = pltpu.SemaphoreType.DMA
    return _sequencer(body, halves, [_sds(h.shape, h.dtype) for h in halves], [sem((nw,)), sem((nw,))], JOIN_ID, name)


def _allreduce_rows(vec, *, name, after=()):
    rows = vec.shape[0]

    def body(v_ref, *rest):
        out_ref, slots, send_sems, recv_sems = rest[len(after):]
        x, y, c = _place()
        me = 4 * x + 2 * y + c
        slots[me] = v_ref[...]
        peers = []
        for mask in range(1, N_DEV):
            px = 1 - x if mask & 4 else x
            py = 1 - y if mask & 2 else y
            pc = 1 - c if mask & 1 else c
            peers.append((px, py, pc))
        cps = []
        for k, peer in enumerate(peers):
            cp = pltpu.make_async_remote_copy(
                src_ref=v_ref, dst_ref=slots.at[me], send_sem=send_sems.at[k], recv_sem=recv_sems.at[k],
                device_id=peer, device_id_type=MESH)
            cp.start()
            cps.append(cp)
        for k, (px, py, pc) in enumerate(peers):
            pltpu.make_async_remote_copy(
                src_ref=v_ref, dst_ref=slots.at[4 * px + 2 * py + pc], send_sem=send_sems.at[k],
                recv_sem=recv_sems.at[k], device_id=(px, py, pc), device_id_type=MESH).wait_recv()
        for cp in cps:
            cp.wait_send()
        acc = slots[0]
        for d in range(1, N_DEV):
            acc = acc + slots[d]
        out_ref[...] = acc

    vmem = pl.BlockSpec(memory_space=pltpu.VMEM)
    return pl.pallas_call(
        body, in_specs=[vmem] + [HBM_OPERAND] * len(after), out_specs=vmem, out_shape=_sds((rows, 128)),
        scratch_shapes=[pltpu.VMEM((N_DEV, rows, 128), F32), pltpu.SemaphoreType.DMA((N_DEV - 1,)),
                        pltpu.SemaphoreType.DMA((N_DEV - 1,))],
        compiler_params=pltpu.CompilerParams(vmem_limit_bytes=VMEM_LIMIT_BYTES), name=name)(vec, *after)


def _reduce_scatter_start(grads, core, *, tag, add_after=()):
    others = _swap_other_halves(grads, name="swap_other_halves_" + tag)
    parts = [_add_my_half(core, g, o, add_after) for g, o in zip(grads, others)]
    return parts, _scatter_partials(parts, name="scatter_partials_" + tag)


def _reduce_scatter_finish(parts, recvd, chip, *, tag, sum_after=()):
    mine = [_sum_partials(chip, p, r, sum_after) for p, r in zip(parts, recvd)]
    return mine, _swap_reduced_halves(mine, name="swap_reduced_halves_" + tag)


ADAM_BLOCK_ELEMS = 256 * 1024


def _adam_rows(rows, cols):
    tm = rows
    while tm * cols > ADAM_BLOCK_ELEMS and tm % 16 == 0:
        tm //= 2
    return tm


def _adam_step(wv, gv, mv, vv):
    m2 = ADAM_B1 * mv + (1.0 - ADAM_B1) * gv
    v2 = ADAM_B2 * vv + (1.0 - ADAM_B2) * (gv * gv)
    m_hat = m2 / (1.0 - ADAM_B1 ** ADAM_STEP)
    v_hat = v2 / (1.0 - ADAM_B2 ** ADAM_STEP)
    return -ADAM_LR * (m_hat / (jnp.sqrt(v_hat) + ADAM_EPS) + ADAM_WD * wv), m2, v2


def _adamw(w, g, m, v, *, name):
    rows, cols = w.shape
    return _rowwise(_adam_step, [w, g, m, v], [], [_sds((rows, cols))] * 3, tm=_adam_rows(rows, cols), name=name)


def _adamw_halves(core, w, g_mine, g_theirs, m, v, *, name, after=()):
    rows, cols = w.shape
    hw = rows // 2
    tm = _adam_rows(hw, cols)
    per_half = hw // tm

    def body(core_ref, w_ref, gm_ref, gt_ref, m_ref, v_ref, *rest):
        g_out, d_out, m_out, v_out = rest[len(after):]
        mine = (pl.program_id(0) // per_half) == core_ref[0]
        g = jnp.where(mine, gm_ref[...], gt_ref[...])
        d, m2, v2 = _adam_step(w_ref[...], g, m_ref[...], v_ref[...])
        g_out[...] = g
        d_out[...] = d
        m_out[...] = m2
        v_out[...] = v2

    full = pl.BlockSpec((tm, cols), lambda i, core_ref: (i, 0))
    half = pl.BlockSpec((tm, cols), lambda i, core_ref: (i % per_half, 0))
    return pl.pallas_call(
        body,
        grid_spec=pltpu.PrefetchScalarGridSpec(
            num_scalar_prefetch=1, grid=(rows // tm,),
            in_specs=[full, half, half, full, full] + [HBM_OPERAND] * len(after), out_specs=[full, full, full, full]),
        out_shape=[_sds((rows, cols))] * 4, compiler_params=_cp(("parallel",)), name=name)(
            core, w, g_mine, g_theirs, m, v, *after)


def _pack_rows(arrs):
    flat = jnp.concatenate([a.reshape(-1).astype(F32) for a in arrs])
    rows = -(-flat.shape[0] // 1024) * 8
    return jnp.pad(flat, (0, rows * 128 - flat.shape[0])).reshape(rows, 128)


def _unpack_rows(vec, shapes):
    flat = vec.reshape(-1)
    out, off = [], 0
    for shp in shapes:
        size = math.prod(shp)
        out.append(flat[off:off + size].reshape(shp))
        off += size
    return out


SMALL = ("b_gate", "ssm_a_re", "ssm_a_im", "ssm_log_dt", "ssm_b_re", "ssm_b_im", "ssm_c_re", "ssm_c_im", "ssm_d",
         "ln1_g", "ln1_b", "ln2_g", "ln2_b")
GATHER_GROUPS = (("w_in", ("w_in",)), ("mixer", ("w_attn_br", "w_ssm_br", "w_glu", "w_out")),
                 ("ffn", ("w_ff_gate", "w_ff_up", "w_ff_down")))
REDUCE_GROUPS = (("ffn", ("w_ff_down", "w_ff_gate", "w_ff_up")),
                 ("mixer", ("w_out", "w_ssm_br", "w_glu", "w_attn_br")), ("w_in", ("w_in",)))
WEIGHTS = ("w_in", "b_gate", "w_attn_br", "w_ssm_br", "w_out", "ssm_a_re", "ssm_a_im", "ssm_log_dt", "ssm_b_re",
           "ssm_b_im", "ssm_c_re", "ssm_c_im", "ssm_d", "w_glu", "ln1_g", "ln1_b", "w_ff_gate", "w_ff_up", "w_ff_down",
           "ln2_g", "ln2_b")


def kernel(x, w_in, b_gate, w_attn_br, w_ssm_br, w_out, ssm_a_re, ssm_a_im, ssm_log_dt, ssm_b_re, ssm_b_im, ssm_c_re, ssm_c_im, ssm_d, w_glu, ln1_g, ln1_b, w_ff_gate, w_ff_up, w_ff_down, ln2_g, ln2_b, loss_target, m_w_in, m_b_gate, m_w_attn_br, m_w_ssm_br, m_w_out, m_ssm_a_re, m_ssm_a_im, m_ssm_log_dt, m_ssm_b_re, m_ssm_b_im, m_ssm_c_re, m_ssm_c_im, m_ssm_d, m_w_glu, m_ln1_g, m_ln1_b, m_w_ff_gate, m_w_ff_up, m_w_ff_down, m_ln2_g, m_ln2_b, v_w_in, v_b_gate, v_w_attn_br, v_w_ssm_br, v_w_out, v_ssm_a_re, v_ssm_a_im, v_ssm_log_dt, v_ssm_b_re, v_ssm_b_im, v_ssm_c_re, v_ssm_c_im, v_ssm_d, v_w_glu, v_ln1_g, v_ln1_b, v_w_ff_gate, v_w_ff_up, v_w_ff_down, v_ln2_g, v_ln2_b):
    given = dict(locals())
    px, py, pc = _place()
    chip = 2 * px + py
    core_s = jnp.reshape(pc, (1,)).astype(jnp.int32)
    chip_s = jnp.reshape(chip, (1,)).astype(jnp.int32)

    wts = {}
    for tag, names in GATHER_GROUPS:
        wts.update(zip(names, _gather_weights([given[n][0].astype(BF16) for n in names], name="gather_" + tag)))
    ncol = D_MODEL // N_CHIPS
    bg_mine = jnp.where(pc == 0, b_gate[0], jnp.zeros_like(b_gate[0]))
    bg_full = lax.dynamic_update_slice(jnp.zeros((2, D_MODEL), F32), bg_mine, (0, chip * ncol))
    bg_full = _allreduce_rows(bg_full.reshape(16, 128), name="gather_gate_bias").reshape(2, D_MODEL)
    small = {n: given[n][0] for n in SMALL if n.startswith("ssm")}
    small.update({n: given[n] for n in ("ln1_g", "ln1_b", "ln2_g", "ln2_b")})
    small["b_gate"] = bg_full

    loss_mine, grad_x, big_g, small_g, marks = _local_step(x[0], loss_target[0], wts, small)
    loss = lax.psum(loss_mine, ("x", "y", "c"))

    groups = dict(REDUCE_GROUPS)
    add_after = {"ffn": (marks["ln1_bwd"],), "mixer": (marks["scan_bwd"],), "w_in": (marks["dx_proj"],)}
    parts, recvd = {}, {}
    for tag, names in REDUCE_GROUPS:
        parts[tag], recvd[tag] = _reduce_scatter_start([big_g[n] for n in names], core_s, tag=tag,
                                                       add_after=add_after[tag])
    grads, delta, new_m, new_v = {}, {}, {}, {}

    def finish(tag, sum_after, adam_after):
        mine, theirs = _reduce_scatter_finish(parts[tag], recvd[tag], chip_s, tag=tag, sum_after=sum_after)
        for n, g_mine, g_theirs in zip(groups[tag], mine, theirs):
            shp = given[n].shape
            two = (shp[1], shp[2])
            res = _adamw_halves(core_s, given[n].reshape(two), g_mine, g_theirs, given["m_" + n].reshape(two),
                                given["v_" + n].reshape(two), name="adamw_" + n, after=adam_after)
            grads[n], delta[n], new_m[n], new_v[n] = [r.reshape(shp) for r in res]

    in_flight = (parts["w_in"][0],)
    finish("ffn", (marks["scan_bwd"],), (marks["attention_bwd_0"],))
    finish("mixer", (marks["attention_bwd_0"],), in_flight)
    shapes = [small_g[n].shape for n in SMALL]
    summed = _unpack_rows(_allreduce_rows(_pack_rows([small_g[n] for n in SMALL]), name="allreduce_small",
                                          after=in_flight), shapes)
    for n, g in zip(SMALL, summed):
        if n == "b_gate":
            g = lax.dynamic_slice(g, (0, chip * ncol), (2, ncol))
        grads[n] = g.reshape(given[n].shape)
    shapes = [given[n].shape for n in SMALL]
    packed = [_pack_rows([src[n] for n in SMALL]) for src in
              (given, grads, {n: given["m_" + n] for n in SMALL}, {n: given["v_" + n] for n in SMALL})]
    small_out = _adamw(*packed, name="adamw_small")
    for out, vec in zip((delta, new_m, new_v), small_out):
        out.update(zip(SMALL, _unpack_rows(vec, shapes)))
    behind = [delta[n] for tag in ("ffn", "mixer") for n in groups[tag]] + [small_out[0], grad_x]
    finish("w_in", tuple(behind), ())

    return (loss, grad_x.reshape(x.shape), *[grads[n] for n in WEIGHTS], *[delta[n] for n in WEIGHTS],
            *[new_m[n] for n in WEIGHTS], *[new_v[n] for n in WEIGHTS])
```

```python
import math

import jax
import jax.numpy as jnp
from jax import lax
from jax.experimental import pallas as pl
from jax.experimental.pallas import tpu as pltpu
from jax.experimental.pallas import tpu_sc as plsc

F32 = jnp.float32
BF16 = jnp.bfloat16
MESH = pl.DeviceIdType.MESH

D_MODEL = 1024
SEQ = 2048
HEAD_DIM = 64
ATTN_HEADS = 8
DILATIONS = (1, 4, 16)
ATTN_WIDTH = ATTN_HEADS * HEAD_DIM
QKV_WIDTH = 3 * ATTN_WIDTH
BLOCK = 128
ROPE_THETA = 10000.0
NEG_INF = -1e30
SSM_GROUP = 16
SSM_GROUPS = 32
SSM_WIDTH = 512
SSM_STATE = 64
SSM_LANES = SSM_GROUPS * SSM_STATE
SCAN_CHUNKS = 8
SCAN_STEPS = SEQ // SCAN_CHUNKS
SCAN_LANES = 256
IN_WIDTH = 3 * QKV_WIDTH + SSM_WIDTH + 2 * D_MODEL
D_FF = 2816
N_CHIPS = 4
N_DEV = 8
DN_ALPHA = 2.0 ** 0.25
LN_EPS = 1e-5
ADAM_LR = 0.001
ADAM_B1 = 0.9
ADAM_B2 = 0.999
ADAM_EPS = 1e-08
ADAM_WD = 0.01
ADAM_STEP = 10
GELU_C = math.sqrt(2.0 / math.pi)
GELU_K = 0.044715

VMEM_LIMIT_BYTES = 56 * 1024 * 1024


def _sds(shape, dtype=F32):
    return jax.ShapeDtypeStruct(tuple(shape), dtype)


def _cp(semantics=None):
    return pltpu.CompilerParams(dimension_semantics=semantics, vmem_limit_bytes=VMEM_LIMIT_BYTES)


HBM_OPERAND = pl.BlockSpec(memory_space=pl.ANY)


def _matmul(a, b, *, grid, a_spec, b_spec, o_spec, out_shape, dims, k_axis=None, name, after=()):
    nk = grid[k_axis] if k_axis is not None else 1
    o_block = tuple(d for d in o_spec.block_shape if d is not None)
    n_after = len(after)

    def body(a_ref, b_ref, *rest):
        o_ref, acc = rest[n_after], rest[n_after + 1:]
        part = lax.dot_general(a_ref[...].astype(BF16), b_ref[...].astype(BF16),
                               (((dims[0],), (dims[1],)), ((), ())), preferred_element_type=F32)
        if k_axis is None:
            o_ref[...] = part.astype(o_ref.dtype)
        else:
            k = pl.program_id(k_axis)

            @pl.when(k == 0)
            def _():
                acc[0][...] = part

            @pl.when(k > 0)
            def _():
                acc[0][...] += part

            @pl.when(k == nk - 1)
            def _():
                o_ref[...] = acc[0][...].astype(o_ref.dtype)

    sem = tuple("arbitrary" if ax == k_axis else "parallel" for ax in range(len(grid)))
    return pl.pallas_call(
        body, grid=grid, in_specs=[a_spec, b_spec] + [HBM_OPERAND] * n_after, out_specs=o_spec, out_shape=out_shape,
        scratch_shapes=[pltpu.VMEM(o_block, F32)] if k_axis is not None else [],
        compiler_params=_cp(sem), name=name)(a, b, *after)


def _mm_cols(a, wg, *, tm, name, out_dtype=F32, out3d=False):
    m, k = a.shape
    ns = wg.shape[2]
    if out3d:
        o_spec = pl.BlockSpec((None, tm, ns), lambda i, s: (s, i, 0))
        out_shape = _sds((N_CHIPS, m, ns), out_dtype)
    else:
        o_spec = pl.BlockSpec((tm, ns), lambda i, s: (i, s))
        out_shape = _sds((m, N_CHIPS * ns), out_dtype)
    return _matmul(a, wg, grid=(m // tm, N_CHIPS),
                   a_spec=pl.BlockSpec((tm, k), lambda i, s: (i, 0)),
                   b_spec=pl.BlockSpec((None, k, ns), lambda i, s: (s, 0, 0)),
                   o_spec=o_spec, out_shape=out_shape, dims=(1, 0), name=name)


def _mm_cols_nt(dy, wg, *, tm, name, dy3d=False, out_dtype=F32, after=()):
    k, ns = wg.shape[1], wg.shape[2]
    if dy3d:
        m = dy.shape[1]
        a_spec = pl.BlockSpec((None, tm, ns), lambda i, s: (s, i, 0))
    else:
        m = dy.shape[0]
        a_spec = pl.BlockSpec((tm, ns), lambda i, s: (i, s))
    return _matmul(dy, wg, grid=(m // tm, N_CHIPS), a_spec=a_spec,
                   b_spec=pl.BlockSpec((None, k, ns), lambda i, s: (s, 0, 0)),
                   o_spec=pl.BlockSpec((tm, k), lambda i, s: (i, 0)),
                   out_shape=_sds((m, k), out_dtype), dims=(1, 1), k_axis=1, name=name, after=after)


def _mm_cols_tn(a, dy, *, ns, name, dy3d=False):
    m, k = a.shape
    if dy3d:
        b_spec = pl.BlockSpec((None, m, ns), lambda s: (s, 0, 0))
    else:
        b_spec = pl.BlockSpec((m, ns), lambda s: (0, s))
    return _matmul(a, dy, grid=(N_CHIPS,), a_spec=pl.BlockSpec((m, k), lambda s: (0, 0)), b_spec=b_spec,
                   o_spec=pl.BlockSpec((None, k, ns), lambda s: (s, 0, 0)),
                   out_shape=_sds((N_CHIPS, k, ns), BF16), dims=(0, 0), name=name)


def _mm_plain(a, b, *, tm, tn, name, out_dtype=F32, dims=(1, 0), tk=None):
    m = a.shape[1 - dims[0]]
    kk = a.shape[dims[0]]
    n = b.shape[1 - dims[1]]
    tk = kk if tk is None else tk
    nk = kk // tk

    def a_idx(i, j, k):
        return (i, k) if dims[0] == 1 else (k, i)

    def b_idx(i, j, k):
        return (k, j) if dims[1] == 0 else (j, k)

    a_blk = (tm, tk) if dims[0] == 1 else (tk, tm)
    b_blk = (tk, tn) if dims[1] == 0 else (tn, tk)
    return _matmul(a, b, grid=(m // tm, n // tn, nk),
                   a_spec=pl.BlockSpec(a_blk, a_idx), b_spec=pl.BlockSpec(b_blk, b_idx),
                   o_spec=pl.BlockSpec((tm, tn), lambda i, j, k: (i, j)),
                   out_shape=_sds((m, n), out_dtype), dims=dims, k_axis=2 if nk > 1 else None, name=name)


def _rowwise(fn, tiled, full, outs, accs=(), *, tm, name, after=()):
    args, in_specs = [], []
    for t in tiled:
        if isinstance(t, tuple):
            arr, w, cb = t
            in_specs.append(pl.BlockSpec((tm, w), lambda i, cb=cb: (i, cb)))
        else:
            arr = t
            in_specs.append(pl.BlockSpec((tm, arr.shape[1]), lambda i: (i, 0)))
        args.append(arr)
    rows = args[0].shape[0]
    for f in full:
        in_specs.append(pl.BlockSpec(f.shape, lambda i, nd=f.ndim: (0,) * nd))
        args.append(f)
    out_specs = [pl.BlockSpec((tm, o.shape[1]), lambda i: (i, 0)) for o in outs]
    out_specs += [pl.BlockSpec(a.shape, lambda i, nd=len(a.shape): (0,) * nd) for a in accs]
    n_in, n_out = len(args), len(outs)
    in_specs += [HBM_OPERAND] * len(after)
    first_out = n_in + len(after)

    def body(*refs):
        res = fn(*[r[...] for r in refs[:n_in]])
        res = res if isinstance(res, (tuple, list)) else (res,)
        for r, v in zip(refs[first_out:first_out + n_out], res[:n_out]):
            r[...] = v.astype(r.dtype)
        i = pl.program_id(0)
        for r, v in zip(refs[first_out + n_out:], res[n_out:]):
            @pl.when(i == 0)
            def _(r=r, v=v):
                r[...] = v

            @pl.when(i > 0)
            def _(r=r, v=v):
                r[...] += v

    res = pl.pallas_call(
        body, grid=(rows // tm,), in_specs=in_specs, out_specs=out_specs, out_shape=list(outs) + list(accs),
        compiler_params=_cp(("arbitrary",) if accs else ("parallel",)), name=name)(*args, *after)
    return res


def _colsum(v):
    return jnp.sum(v, axis=0, keepdims=True)


def _ln_stats(z):
    mu = jnp.mean(z, axis=-1, keepdims=True)
    zc = z - mu
    var = jnp.mean(zc * zc, axis=-1, keepdims=True)
    rstd = lax.rsqrt(var + LN_EPS)
    return zc * rstd, rstd


def _ln_bwd(dy, xhat, rstd, g):
    dxh = dy * g
    m1 = jnp.mean(dxh, axis=-1, keepdims=True)
    m2 = jnp.mean(dxh * xhat, axis=-1, keepdims=True)
    return rstd * (dxh - m1 - xhat * m2)


def _swap_halves(t):
    w = t.shape[-1]
    lane = lax.broadcasted_iota(jnp.int32, t.shape, t.ndim - 1)
    return jnp.where((lane % HEAD_DIM) < HEAD_DIM // 2, pltpu.roll(t, w - HEAD_DIM // 2, t.ndim - 1),
                     pltpu.roll(t, HEAD_DIM // 2, t.ndim - 1))


def _band_mask(i):
    row = lax.broadcasted_iota(jnp.int32, (BLOCK, 2 * BLOCK), 0)
    col = lax.broadcasted_iota(jnp.int32, (BLOCK, 2 * BLOCK), 1)
    dist = BLOCK + row - col
    return (dist >= 0) & (dist <= BLOCK) & ((col >= BLOCK) | (i > 0))


PAIR = 2 * HEAD_DIM
UNITS = SEQ // BLOCK
ROPE_ROWS = 256


def _rope(t, cf, ss):
    return t * cf + _swap_halves(t) * ss


def _rope_transposed(d, cf, ss):
    return d * cf + _swap_halves(d * ss)


def _unit_rows(u, dil):
    if dil == 1:
        i = u
        start = pl.multiple_of(u * BLOCK, BLOCK)
        prev = pl.multiple_of(jnp.maximum(u - 1, 0) * BLOCK, BLOCK)
        return i, pl.ds(start, BLOCK), pl.ds(prev, BLOCK)
    rho = jnp.bitwise_and(u, dil - 1)
    i = jnp.right_shift(u, dil.bit_length() - 1)
    start = rho + dil * BLOCK * i
    prev = rho + dil * BLOCK * jnp.maximum(i - 1, 0)
    return i, pl.ds(start, BLOCK, stride=dil), pl.ds(prev, BLOCK, stride=dil)


def _causal_mask():
    row = lax.broadcasted_iota(jnp.int32, (BLOCK, BLOCK), 0)
    col = lax.broadcasted_iota(jnp.int32, (BLOCK, BLOCK), 1)
    return row >= col


def _pair_views(col0):
    return [pl.BlockSpec((SEQ, PAIR), lambda hp, g=g: (0, col0 // PAIR + g * (ATTN_WIDTH // PAIR) + hp))
            for g in range(len(DILATIONS))]


def _rotate_keys(k_refs, kr_refs, cf_ref, ss_ref):
    def step(t, carry):
        rows = pl.ds(pl.multiple_of(t * ROPE_ROWS, ROPE_ROWS), ROPE_ROWS)
        cf, ss = cf_ref[rows, :], ss_ref[rows, :]
        for k_ref, kr_ref in zip(k_refs, kr_refs):
            kr_ref[rows, :] = _rope(k_ref[rows, :], cf, ss)
        return carry

    lax.fori_loop(0, SEQ // ROPE_ROWS, step, 0)


def _attention_fwd(proj, cos_f, sin_s):
    ng = len(DILATIONS)

    def body(*refs):
        q_refs, k_refs, v_refs = refs[:ng], refs[ng:2 * ng], refs[2 * ng:3 * ng]
        cf_ref, ss_ref, attn_ref, lse_ref = refs[3 * ng:3 * ng + 4]
        kr_refs = refs[3 * ng + 4:]
        _rotate_keys(k_refs, kr_refs, cf_ref, ss_ref)
        first = lax.broadcasted_iota(jnp.int32, (BLOCK, PAIR), 1) < HEAD_DIM
        for g, dil in enumerate(DILATIONS):
            two_blocks = SEQ // dil > BLOCK

            def unit(u, carry, g=g, dil=dil, two_blocks=two_blocks):
                i, rows, prev = _unit_rows(u, dil)
                qq = (_rope(q_refs[g][rows, :], cf_ref[rows, :], ss_ref[rows, :]) * (1.0 / math.sqrt(HEAD_DIM))).astype(BF16)
                if two_blocks:
                    kk = jnp.concatenate([kr_refs[g][prev, :], kr_refs[g][rows, :]], axis=0).astype(BF16)
                    vv = jnp.concatenate([v_refs[g][prev, :], v_refs[g][rows, :]], axis=0).astype(BF16)
                    valid = _band_mask(i)
                else:
                    kk = kr_refs[g][rows, :].astype(BF16)
                    vv = v_refs[g][rows, :].astype(BF16)
                    valid = _causal_mask()
                zero = jnp.zeros_like(qq)
                outs, lses = [], []
                for qh in (jnp.where(first, qq, zero), jnp.where(first, zero, qq)):
                    s = lax.dot_general(qh, kk, (((1,), (1,)), ((), ())), preferred_element_type=F32)
                    s = jnp.where(valid, s, NEG_INF)
                    m = jnp.max(s, axis=1, keepdims=True)
                    p = jnp.exp(s - m)
                    l = jnp.sum(p, axis=1, keepdims=True)
                    outs.append(jnp.dot(p.astype(BF16), vv, preferred_element_type=F32) / l)
                    lses.append(m + jnp.log(l))
                o = jnp.where(first, outs[0], outs[1])
                lse = jnp.where(first, lses[0], lses[1])
                if g == 0:
                    attn_ref[rows, :] = o
                    lse_ref[rows, :] = lse
                else:
                    lse_old = lse_ref[rows, :]
                    m = jnp.maximum(lse_old, lse)
                    lse_new = m + jnp.log(jnp.exp(lse_old - m) + jnp.exp(lse - m))
                    attn_ref[rows, :] = attn_ref[rows, :] * jnp.exp(lse_old - lse_new) + o * jnp.exp(lse - lse_new)
                    lse_ref[rows, :] = lse_new
                return carry

            lax.fori_loop(0, UNITS, unit, 0)

    whole = pl.BlockSpec((SEQ, PAIR), lambda hp: (0, 0))
    out = pl.BlockSpec((SEQ, PAIR), lambda hp: (0, hp))
    return pl.pallas_call(
        body, grid=(ATTN_WIDTH // PAIR,),
        in_specs=_pair_views(0) + _pair_views(QKV_WIDTH) + _pair_views(2 * QKV_WIDTH) + [whole, whole],
        out_specs=[out, out], out_shape=[_sds((SEQ, ATTN_WIDTH)), _sds((SEQ, ATTN_WIDTH))],
        scratch_shapes=[pltpu.VMEM((SEQ, PAIR), F32)] * ng,
        compiler_params=_cp(("parallel",)), name="attention_fwd")(*([proj] * (3 * ng)), cos_f, sin_s)


def _attention_bwd(g, proj, cos_f, sin_s, d_attn, attn, lse):
    dil = DILATIONS[g]
    two_blocks = SEQ // dil > BLOCK

    def body(q_ref, k_ref, v_ref, cf_ref, ss_ref, do_ref, o_ref, lse_ref, dq_out, dk_out, dv_out,
             kr_ref, dq_acc, dk_acc, dv_acc):
        _rotate_keys([k_ref], [kr_ref], cf_ref, ss_ref)
        dk_acc[...] = jnp.zeros_like(dk_acc)
        dv_acc[...] = jnp.zeros_like(dv_acc)
        first = lax.broadcasted_iota(jnp.int32, (BLOCK, PAIR), 1) < HEAD_DIM
        nk = 2 * BLOCK if two_blocks else BLOCK
        first_k = lax.broadcasted_iota(jnp.int32, (nk, PAIR), 1) < HEAD_DIM

        def unit(u, carry):
            i, rows, prev = _unit_rows(u, dil)
            qq = (_rope(q_ref[rows, :], cf_ref[rows, :], ss_ref[rows, :]) * (1.0 / math.sqrt(HEAD_DIM))).astype(BF16)
            if two_blocks:
                kk = jnp.concatenate([kr_ref[prev, :], kr_ref[rows, :]], axis=0).astype(BF16)
                vv = jnp.concatenate([v_ref[prev, :], v_ref[rows, :]], axis=0).astype(BF16)
                valid = _band_mask(i)
            else:
                kk = kr_ref[rows, :].astype(BF16)
                vv = v_ref[rows, :].astype(BF16)
                valid = _causal_mask()
            dof = do_ref[rows, :]
            dd = dof * o_ref[rows, :]
            lse2 = lse_ref[rows, :]
            dob = dof.astype(BF16)
            zq, zd, zf = jnp.zeros_like(qq), jnp.zeros_like(dob), jnp.zeros_like(dd)
            dqs, dks, dvs = [], [], []
            for h in range(2):
                sel = first if h == 0 else jnp.logical_not(first)
                qh = jnp.where(sel, qq, zq)
                doh = jnp.where(sel, dob, zd)
                delta = jnp.sum(jnp.where(sel, dd, zf), axis=1, keepdims=True)
                lse_h = lse2[:, h * HEAD_DIM:h * HEAD_DIM + 1]
                s = lax.dot_general(qh, kk, (((1,), (1,)), ((), ())), preferred_element_type=F32)
                p = jnp.where(valid, jnp.exp(s - lse_h), 0.0)
                dp = lax.dot_general(doh, vv, (((1,), (1,)), ((), ())), preferred_element_type=F32)
                ds = (p * (dp - delta)).astype(BF16)
                dqs.append(jnp.dot(ds, kk, preferred_element_type=F32))
                dks.append(lax.dot_general(ds, qq, (((0,), (0,)), ((), ())), preferred_element_type=F32))
                dvs.append(lax.dot_general(p.astype(BF16), dob, (((0,), (0,)), ((), ())), preferred_element_type=F32))
            dq_acc[rows, :] = jnp.where(first, dqs[0], dqs[1])
            dk2 = jnp.where(first_k, dks[0], dks[1])
            dv2 = jnp.where(first_k, dvs[0], dvs[1])
            dk_acc[rows, :] += dk2[nk - BLOCK:]
            dv_acc[rows, :] += dv2[nk - BLOCK:]
            if two_blocks:
                @pl.when(i > 0)
                def _():
                    dk_acc[prev, :] += dk2[:BLOCK]
                    dv_acc[prev, :] += dv2[:BLOCK]
            return carry

        lax.fori_loop(0, UNITS, unit, 0)

        def finish(t, carry):
            rows = pl.ds(pl.multiple_of(t * ROPE_ROWS, ROPE_ROWS), ROPE_ROWS)
            cf, ss = cf_ref[rows, :], ss_ref[rows, :]
            dq = dq_acc[rows, :] * (1.0 / math.sqrt(HEAD_DIM))
            dq_out[rows, :] = _rope_transposed(dq, cf, ss).astype(BF16)
            dk_out[rows, :] = _rope_transposed(dk_acc[rows, :], cf, ss).astype(BF16)
            dv_out[rows, :] = dv_acc[rows, :].astype(BF16)
            return carry

        lax.fori_loop(0, SEQ // ROPE_ROWS, finish, 0)

    whole = pl.BlockSpec((SEQ, PAIR), lambda hp: (0, 0))
    pair = pl.BlockSpec((SEQ, PAIR), lambda hp: (0, hp))
    views = [_pair_views(col0)[g] for col0 in (0, QKV_WIDTH, 2 * QKV_WIDTH)]
    return pl.pallas_call(
        body, grid=(ATTN_WIDTH // PAIR,), in_specs=views + [whole, whole, pair, pair, pair],
        out_specs=[pair, pair, pair], out_shape=[_sds((SEQ, ATTN_WIDTH), BF16)] * 3,
        scratch_shapes=[pltpu.VMEM((SEQ, PAIR), F32)] * 4,
        compiler_params=_cp(("parallel",)), name=f"attention_bwd_{g}")(proj, proj, proj, cos_f, sin_s, d_attn, attn, lse)


def _cmul(ar, ai, br, bi):
    return ar * br - ai * bi, ar * bi + ai * br


def _pow256(ar, ai):
    for _ in range(8):
        ar, ai = _cmul(ar, ai, ar, ai)
    return ar, ai


def _chunk_carries(first_r, first_i, pr, pi, reverse):
    rows = lax.broadcasted_iota(jnp.int32, first_r.shape, 0)
    out_r = jnp.zeros_like(first_r)
    out_i = jnp.zeros_like(first_i)
    hr = jnp.zeros_like(first_r[0:1])
    hi = jnp.zeros_like(hr)
    order = range(SCAN_CHUNKS - 1, -1, -1) if reverse else range(SCAN_CHUNKS)
    for c in order:
        out_r = jnp.where(rows == c, hr, out_r)
        out_i = jnp.where(rows == c, hi, out_i)
        tr, ti = _cmul(pr[0:1], pi[0:1], hr, hi)
        hr = first_r[c:c + 1] + tr
        hi = first_i[c:c + 1] + ti
    return out_r, out_i


def _tile(j):
    return pl.ds(pl.multiple_of(j * SCAN_CHUNKS, SCAN_CHUNKS), SCAN_CHUNKS)


def _to_scan_rows(t):
    return t.reshape(SCAN_CHUNKS, SCAN_STEPS, t.shape[1]).transpose(1, 0, 2).reshape(t.shape)


def _from_scan_rows(t):
    return t.reshape(SCAN_STEPS, SCAN_CHUNKS, t.shape[1]).transpose(1, 0, 2).reshape(t.shape)


def _scan_fwd(bur, bui, ar, ai):
    lb = SCAN_LANES

    def body(bur_ref, bui_ref, ar_ref, ai_ref, hr_ref, hi_ref, er_ref, ei_ref):
        a_r = jnp.broadcast_to(ar_ref[...], (SCAN_CHUNKS, lb))
        a_i = jnp.broadcast_to(ai_ref[...], (SCAN_CHUNKS, lb))

        def local(j, carry):
            tr, ti = _cmul(a_r, a_i, carry[0], carry[1])
            nr = tr + bur_ref[_tile(j), :]
            ni = ti + bui_ref[_tile(j), :]
            hr_ref[_tile(j), :] = nr
            hi_ref[_tile(j), :] = ni
            return nr, ni

        zero = jnp.zeros((SCAN_CHUNKS, lb), F32)
        last_r, last_i = lax.fori_loop(0, SCAN_STEPS, local, (zero, zero), unroll=4)
        pr, pi = _pow256(a_r, a_i)
        er, ei = _chunk_carries(last_r, last_i, pr, pi, reverse=False)
        er_ref[...] = er
        ei_ref[...] = ei

        def fix(j, carry):
            tr, ti = _cmul(carry[0], carry[1], er, ei)
            hr_ref[_tile(j), :] += tr
            hi_ref[_tile(j), :] += ti
            return _cmul(carry[0], carry[1], a_r, a_i)

        lax.fori_loop(0, SCAN_STEPS, fix, (a_r, a_i), unroll=4)

    big = pl.BlockSpec((SEQ, lb), lambda j: (0, j))
    vec = pl.BlockSpec((1, lb), lambda j: (0, j))
    ent = pl.BlockSpec((SCAN_CHUNKS, lb), lambda j: (0, j))
    return pl.pallas_call(
        body, grid=(SSM_LANES // lb,), in_specs=[big, big, vec, vec], out_specs=[big, big, ent, ent],
        out_shape=[_sds((SEQ, SSM_LANES)), _sds((SEQ, SSM_LANES)), _sds((SCAN_CHUNKS, SSM_LANES)),
                   _sds((SCAN_CHUNKS, SSM_LANES))],
        compiler_params=_cp(("parallel",)), name="ssm_scan_fwd")(bur, bui, ar, ai)


def _scan_bwd(gr, gi, hr, hi, er, ei, ar, ai):
    lb = SCAN_LANES

    def body(gr_ref, gi_ref, hr_ref, hi_ref, er_ref, ei_ref, ar_ref, ai_ref, lr_ref, li_ref, dar_ref, dai_ref):
        a_r = jnp.broadcast_to(ar_ref[...], (SCAN_CHUNKS, lb))
        a_i = -jnp.broadcast_to(ai_ref[...], (SCAN_CHUNKS, lb))

        def local(t, carry):
            j = SCAN_STEPS - 1 - t
            tr, ti = _cmul(a_r, a_i, carry[0], carry[1])
            nr = tr + gr_ref[_tile(j), :]
            ni = ti + gi_ref[_tile(j), :]
            lr_ref[_tile(j), :] = nr
            li_ref[_tile(j), :] = ni
            return nr, ni

        zero = jnp.zeros((SCAN_CHUNKS, lb), F32)
        first_r, first_i = lax.fori_loop(0, SCAN_STEPS, local, (zero, zero), unroll=4)
        pr, pi = _pow256(a_r, a_i)
        nxt_r, nxt_i = _chunk_carries(first_r, first_i, pr, pi, reverse=True)

        def accumulate(lam_r, lam_i, hp_r, hp_i, acc):
            return (acc[0] + lam_r * hp_r + lam_i * hp_i, acc[1] + lam_i * hp_r - lam_r * hp_i)

        def fix(t, carry):
            qr, qi, acc_r, acc_i = carry
            j = SCAN_STEPS - 1 - t
            tr, ti = _cmul(qr, qi, nxt_r, nxt_i)
            lam_r = lr_ref[_tile(j), :] + tr
            lam_i = li_ref[_tile(j), :] + ti
            lr_ref[_tile(j), :] = lam_r
            li_ref[_tile(j), :] = lam_i
            acc_r, acc_i = accumulate(lam_r, lam_i, hr_ref[_tile(j - 1), :], hi_ref[_tile(j - 1), :], (acc_r, acc_i))
            qr, qi = _cmul(qr, qi, a_r, a_i)
            return qr, qi, acc_r, acc_i

        qr, qi, acc_r, acc_i = lax.fori_loop(0, SCAN_STEPS - 1, fix, (a_r, a_i, zero, zero), unroll=4)
        tr, ti = _cmul(qr, qi, nxt_r, nxt_i)
        lam_r = lr_ref[_tile(0), :] + tr
        lam_i = li_ref[_tile(0), :] + ti
        lr_ref[_tile(0), :] = lam_r
        li_ref[_tile(0), :] = lam_i
        acc_r, acc_i = accumulate(lam_r, lam_i, er_ref[...], ei_ref[...], (acc_r, acc_i))
        dar_ref[...] = jnp.sum(acc_r, axis=0, keepdims=True)
        dai_ref[...] = jnp.sum(acc_i, axis=0, keepdims=True)

    big = pl.BlockSpec((SEQ, lb), lambda j: (0, j))
    vec = pl.BlockSpec((1, lb), lambda j: (0, j))
    ent = pl.BlockSpec((SCAN_CHUNKS, lb), lambda j: (0, j))
    return pl.pallas_call(
        body, grid=(SSM_LANES // lb,), in_specs=[big, big, big, big, ent, ent, vec, vec],
        out_specs=[big, big, vec, vec],
        out_shape=[_sds((SEQ, SSM_LANES)), _sds((SEQ, SSM_LANES)), _sds((1, SSM_LANES)), _sds((1, SSM_LANES))],
        compiler_params=_cp(("parallel",)), name="ssm_scan_bwd")(gr, gi, hr, hi, er, ei, ar, ai)


def _rope_tables():
    half = HEAD_DIM // 2
    inv_freq = ROPE_THETA ** (-jnp.arange(half, dtype=F32) / half)
    ang = jnp.arange(SEQ, dtype=F32)[:, None] * inv_freq[None, :]
    cos, sin = jnp.cos(ang), jnp.sin(ang)
    cos_f = jnp.concatenate([cos, cos, cos, cos], axis=1)
    sin_s = jnp.concatenate([-sin, sin, -sin, sin], axis=1)
    return cos_f, sin_s


def _ssm_discretise(a_re, a_im, log_dt, b_re, b_im):
    lam = lax.complex(a_re, a_im)
    dt = jnp.exp(log_dt)[:, None]
    a_bar = jnp.exp(lam * dt)
    b_bar = ((a_bar - 1.0) / lam)[..., None] * lax.complex(b_re, b_im)
    return a_bar.real, a_bar.imag, b_bar.real, b_bar.imag


def _block_diag_in(b):
    eye = jnp.eye(SSM_GROUPS, dtype=b.dtype)
    return (eye[:, None, :, None] * b.transpose(0, 2, 1)[:, :, None, :]).reshape(SSM_WIDTH, SSM_LANES)


def _block_diag_out(c):
    eye = jnp.eye(SSM_GROUPS, dtype=c.dtype)
    return (eye[:, None, :, None] * c.transpose(0, 2, 1)[:, :, None, :]).reshape(SSM_LANES, SSM_WIDTH)


SSM_SLABS = 4
SLAB_GROUPS = SSM_GROUPS // SSM_SLABS
SLAB_IN = SSM_WIDTH // SSM_SLABS
SLAB_STATE = SSM_LANES // SSM_SLABS


def _ssm_in(u, b_in, *, name, transpose=False):
    if transpose:
        return _matmul(u, b_in, grid=(SSM_SLABS,), a_spec=pl.BlockSpec((SEQ, SLAB_STATE), lambda j: (0, j)),
                       b_spec=pl.BlockSpec((SLAB_IN, SLAB_STATE), lambda j: (j, j)),
                       o_spec=pl.BlockSpec((SEQ, SLAB_IN), lambda j: (0, j)), out_shape=_sds((SEQ, SSM_WIDTH)),
                       dims=(1, 1), name=name)
    return _matmul(u, b_in, grid=(SSM_SLABS,), a_spec=pl.BlockSpec((SEQ, SLAB_IN), lambda j: (0, j)),
                   b_spec=pl.BlockSpec((SLAB_IN, SLAB_STATE), lambda j: (j, j)),
                   o_spec=pl.BlockSpec((SEQ, SLAB_STATE), lambda j: (0, j)), out_shape=_sds((SEQ, SSM_LANES)),
                   dims=(1, 0), name=name)


def _ssm_out(h, c_out, *, name, transpose=False):
    if transpose:
        return _matmul(h, c_out, grid=(SSM_SLABS,), a_spec=pl.BlockSpec((SEQ, SLAB_IN), lambda j: (0, j)),
                       b_spec=pl.BlockSpec((SLAB_STATE, SLAB_IN), lambda j: (j, j)),
                       o_spec=pl.BlockSpec((SEQ, SLAB_STATE), lambda j: (0, j)), out_shape=_sds((SEQ, SSM_LANES)),
                       dims=(1, 1), name=name)
    return _matmul(h, c_out, grid=(SSM_SLABS,), a_spec=pl.BlockSpec((SEQ, SLAB_STATE), lambda j: (0, j)),
                   b_spec=pl.BlockSpec((SLAB_STATE, SLAB_IN), lambda j: (j, j)),
                   o_spec=pl.BlockSpec((SEQ, SLAB_IN), lambda j: (0, j)), out_shape=_sds((SEQ, SSM_WIDTH)),
                   dims=(1, 0), name=name)


def _diag_block_grad(a, b, *, name):
    ra = a.shape[1] // SSM_SLABS
    cb = b.shape[1] // SSM_SLABS
    wa, wb = ra // SLAB_GROUPS, cb // SLAB_GROUPS

    def body(a_ref, b_ref, o_ref):
        d = lax.dot_general(a_ref[...].astype(BF16), b_ref[...].astype(BF16), (((0,), (0,)), ((), ())),
                            preferred_element_type=F32)
        row_g = jnp.right_shift(lax.broadcasted_iota(jnp.int32, (ra, cb), 0), wa.bit_length() - 1)
        col_g = jnp.right_shift(lax.broadcasted_iota(jnp.int32, (ra, cb), 1), wb.bit_length() - 1)
        d = jnp.where(row_g == col_g, d, 0.0)
        fold = (jnp.bitwise_and(lax.broadcasted_iota(jnp.int32, (cb, wb), 0), wb - 1)
                == lax.broadcasted_iota(jnp.int32, (cb, wb), 1)).astype(F32)
        o_ref[...] = jnp.dot(d, fold, preferred_element_type=F32, precision=lax.Precision.HIGHEST)

    return pl.pallas_call(
        body, grid=(SSM_SLABS,), in_specs=[pl.BlockSpec((SEQ, ra), lambda j: (0, j)),
                                           pl.BlockSpec((SEQ, cb), lambda j: (0, j))],
        out_specs=pl.BlockSpec((ra, wb), lambda j: (j, 0)), out_shape=_sds((a.shape[1], wb)),
        compiler_params=_cp(("parallel",)), name=name)(a, b)


FF_ROWS = 1024
FF_SHARD = D_FF // N_CHIPS


def _dot_nt(a, b):
    return lax.dot_general(a, b, (((1,), (1,)), ((), ())), preferred_element_type=F32)


def _ffn_up(h, w_gate_t, w_up_t):
    def body(h_ref, wg_ref, wu_ref, a_ref, b_ref, act_ref):
        hb = h_ref[...].astype(BF16)
        a = _dot_nt(hb, wg_ref[...])
        b = _dot_nt(hb, wu_ref[...])
        a_ref[...] = a
        b_ref[...] = b
        act_ref[...] = (a * jax.nn.sigmoid(a) * b).astype(BF16)

    w_spec = pl.BlockSpec((None, FF_SHARD, D_MODEL), lambda i, k: (k, 0, 0))
    o_spec = pl.BlockSpec((None, FF_ROWS, FF_SHARD), lambda i, k: (k, i, 0))
    shape = (N_CHIPS, SEQ, FF_SHARD)
    return pl.pallas_call(
        body, grid=(SEQ // FF_ROWS, N_CHIPS),
        in_specs=[pl.BlockSpec((FF_ROWS, D_MODEL), lambda i, k: (i, 0)), w_spec, w_spec],
        out_specs=[o_spec, o_spec, o_spec], out_shape=[_sds(shape), _sds(shape), _sds(shape, BF16)],
        compiler_params=_cp(("parallel", "parallel")), name="ffn_up")(h, w_gate_t, w_up_t)


def _ffn_down_bwd(dz, w_down, a, b):
    def body(dz_ref, wd_ref, a_ref, b_ref, da_ref, db_ref):
        d_act = _dot_nt(dz_ref[...].astype(BF16), wd_ref[...])
        av = a_ref[...]
        sg = jax.nn.sigmoid(av)
        da_ref[...] = (d_act * b_ref[...] * sg * (1.0 + av * (1.0 - sg))).astype(BF16)
        db_ref[...] = (d_act * av * sg).astype(BF16)

    t_spec = pl.BlockSpec((None, FF_ROWS, FF_SHARD), lambda i, k: (k, i, 0))
    shape = (N_CHIPS, SEQ, FF_SHARD)
    return pl.pallas_call(
        body, grid=(SEQ // FF_ROWS, N_CHIPS),
        in_specs=[pl.BlockSpec((FF_ROWS, D_MODEL), lambda i, k: (i, 0)),
                  pl.BlockSpec((None, FF_SHARD, D_MODEL), lambda i, k: (k, 0, 0)), t_spec, t_spec],
        out_specs=[t_spec, t_spec], out_shape=[_sds(shape, BF16), _sds(shape, BF16)],
        compiler_params=_cp(("parallel", "parallel")), name="ffn_down_bwd")(dz, w_down, a, b)


def _ffn_dh(d_a, d_b, w_gate_t, w_up_t):
    def body(da_ref, db_ref, wg_ref, wu_ref, o_ref, acc):
        k = pl.program_id(1)
        part = (jnp.dot(da_ref[...], wg_ref[...], preferred_element_type=F32)
                + jnp.dot(db_ref[...], wu_ref[...], preferred_element_type=F32))

        @pl.when(k == 0)
        def _():
            acc[...] = part

        @pl.when(k > 0)
        def _():
            acc[...] += part

        @pl.when(k == N_CHIPS - 1)
        def _():
            o_ref[...] = acc[...]

    t_spec = pl.BlockSpec((None, FF_ROWS, FF_SHARD), lambda i, k: (k, i, 0))
    w_spec = pl.BlockSpec((None, FF_SHARD, D_MODEL), lambda i, k: (k, 0, 0))
    return pl.pallas_call(
        body, grid=(SEQ // FF_ROWS, N_CHIPS), in_specs=[t_spec, t_spec, w_spec, w_spec],
        out_specs=pl.BlockSpec((FF_ROWS, D_MODEL), lambda i, k: (i, 0)), out_shape=_sds((SEQ, D_MODEL)),
        scratch_shapes=[pltpu.VMEM((FF_ROWS, D_MODEL), F32)],
        compiler_params=_cp(("parallel", "arbitrary")), name="ffn_dh")(d_a, d_b, w_gate_t, w_up_t)


def _local_step(x, tgt, wts, small):
    s = SEQ
    cos_f, sin_s = _rope_tables()

    proj = _mm_cols(x, wts["w_in"], tm=1024, name="proj")

    attn, lse = _attention_fwd(proj, cos_f, sin_s)
    y_attn = _mm_cols(attn, wts["w_attn_br"], tm=s, name="y_attn")

    (abar_r, abar_i, bbar_r, bbar_i), ssm_vjp = jax.vjp(
        _ssm_discretise, small["ssm_a_re"], small["ssm_a_im"], small["ssm_log_dt"], small["ssm_b_re"], small["ssm_b_im"])
    b_in_r, b_in_i = _block_diag_in(bbar_r).astype(BF16), _block_diag_in(bbar_i).astype(BF16)
    c_out_r = _block_diag_out(small["ssm_c_re"]).astype(BF16)
    c_out_ni = _block_diag_out(-small["ssm_c_im"]).astype(BF16)
    a_r, a_i = abar_r.reshape(1, SSM_LANES), abar_i.reshape(1, SSM_LANES)
    d_skip = small["ssm_d"].reshape(1, SSM_WIDTH)

    u_f = _to_scan_rows(proj[:, 3 * QKV_WIDTH:3 * QKV_WIDTH + SSM_WIDTH])
    u_p = u_f.astype(BF16)
    bu_r = _ssm_in(u_p, b_in_r, name="ssm_bu_re")
    bu_i = _ssm_in(u_p, b_in_i, name="ssm_bu_im")
    h_r, h_i, e_r, e_i = _scan_fwd(bu_r, bu_i, a_r, a_i)
    y_1 = _ssm_out(h_r, c_out_r, name="ssm_y_re")
    y_2 = _ssm_out(h_i, c_out_ni, name="ssm_y_im")

    def gelu_fwd(y1, y2, u, dsk):
        y = y1 + y2 + dsk * u
        return y, 0.5 * y * (1.0 + jnp.tanh(GELU_C * (y + GELU_K * y * y * y)))

    y_s5, gel = _rowwise(gelu_fwd, [y_1, y_2, u_f], [d_skip], [_sds((s, SSM_WIDTH)), _sds((s, SSM_WIDTH), BF16)],
                         tm=512, name="ssm_gelu")
    glu = _mm_cols(gel, wts["w_glu"], tm=s, name="glu")

    def glu_fwd(ga, gb):
        return ga * jax.nn.sigmoid(gb)

    (y_glu,) = _rowwise(glu_fwd, [(glu, SSM_WIDTH, 0), (glu, SSM_WIDTH, 1)], [], [_sds((s, SSM_WIDTH), BF16)],
                        tm=512, name="glu_gate")
    y_glu = _from_scan_rows(y_glu)
    y_ssm = _mm_cols(y_glu, wts["w_ssm_br"], tm=s, name="y_ssm")

    gl0 = (proj, D_MODEL, (3 * QKV_WIDTH + SSM_WIDTH) // D_MODEL)
    gl1 = (proj, D_MODEL, (3 * QKV_WIDTH + SSM_WIDTH) // D_MODEL + 1)
    b_gate = small["b_gate"]

    def gate_mix(l0, l1, ya, ys, bg):
        return jax.nn.sigmoid(l0 + bg[0:1]) * ya + jax.nn.sigmoid(l1 + bg[1:2]) * ys

    (mixed,) = _rowwise(gate_mix, [gl0, gl1, y_attn, y_ssm], [b_gate], [_sds((s, D_MODEL), BF16)], tm=256,
                        name="gate_mix")
    w_out = wts["w_out"].reshape(D_MODEL, D_MODEL)
    mix_out = _mm_plain(mixed, w_out, tm=1024, tn=512, name="mix_out")

    def ln1_fwd(xv, mo, g, b):
        z = DN_ALPHA * xv + mo
        xhat, _ = _ln_stats(z)
        return z, xhat * g + b

    z1, h = _rowwise(ln1_fwd, [x, mix_out], [small["ln1_g"], small["ln1_b"]],
                     [_sds((s, D_MODEL)), _sds((s, D_MODEL))], tm=256, name="ln1")

    nf = D_FF // N_CHIPS
    w_gate_t, w_up_t, w_down = wts["w_ff_gate"], wts["w_ff_up"], wts["w_ff_down"]
    ff_a, ff_b, act = _ffn_up(h, w_gate_t, w_up_t)
    ff = _matmul(act, w_down, grid=(2, N_CHIPS),
                 a_spec=pl.BlockSpec((None, 1024, nf), lambda i, k: (k, i, 0)),
                 b_spec=pl.BlockSpec((None, nf, D_MODEL), lambda i, k: (k, 0, 0)),
                 o_spec=pl.BlockSpec((1024, D_MODEL), lambda i, k: (i, 0)),
                 out_shape=_sds((s, D_MODEL)), dims=(1, 0), k_axis=1, name="ff_down")

    def ln2_loss(hv, ffv, tg, g, b):
        z = DN_ALPHA * hv + ffv
        xhat, rstd = _ln_stats(z)
        err = xhat * g + b - tg
        d_out = err * (1.0 / D_MODEL)
        loss_rows = jnp.sum(err * err, axis=-1, keepdims=True) * (0.5 / D_MODEL)
        loss = jnp.broadcast_to(jnp.sum(loss_rows, axis=0, keepdims=True), (1, 128))
        return _ln_bwd(d_out, xhat, rstd, g), loss, _colsum(d_out * xhat), _colsum(d_out)

    dz2, loss_v, d_ln2_g, d_ln2_b = _rowwise(
        ln2_loss, [h, ff, tgt], [small["ln2_g"], small["ln2_b"]], [_sds((s, D_MODEL))],
        [_sds((1, 128)), _sds((1, D_MODEL)), _sds((1, D_MODEL))], tm=256, name="ln2_loss")

    d_a, d_b = _ffn_down_bwd(dz2, w_down, ff_a, ff_b)

    def grad_rows(lhs, rhs, name):
        return _matmul(lhs, rhs, grid=(N_CHIPS,), a_spec=pl.BlockSpec((None, s, nf), lambda k: (k, 0, 0)),
                       b_spec=pl.BlockSpec((s, D_MODEL), lambda k: (0, 0)),
                       o_spec=pl.BlockSpec((None, nf, D_MODEL), lambda k: (k, 0, 0)),
                       out_shape=_sds((N_CHIPS, nf, D_MODEL), BF16), dims=(0, 0), name=name)

    g_w_ff_down = grad_rows(act, dz2, "g_w_ff_down")
    g_w_ff_gate = grad_rows(d_a, h, "g_w_ff_gate")
    g_w_ff_up = grad_rows(d_b, h, "g_w_ff_up")
    dh_ff = _ffn_dh(d_a, d_b, w_gate_t, w_up_t)

    def ln1_bwd(dz, dff, z, g):
        xhat, rstd = _ln_stats(z)
        dh = DN_ALPHA * dz + dff
        return _ln_bwd(dh, xhat, rstd, g), _colsum(dh * xhat), _colsum(dh)

    dz1, d_ln1_g, d_ln1_b = _rowwise(ln1_bwd, [dz2, dh_ff, z1], [small["ln1_g"]], [_sds((s, D_MODEL))],
                                     [_sds((1, D_MODEL)), _sds((1, D_MODEL))], tm=256, name="ln1_bwd")
    d_mixed = _mm_plain(dz1, w_out, tm=1024, tn=512, dims=(1, 1), name="d_mixed")
    g_w_out = _mm_plain(mixed, dz1, tm=D_MODEL, tn=512, dims=(0, 0), out_dtype=BF16, name="g_w_out")
    g_w_out = g_w_out.reshape(N_CHIPS, D_MODEL // N_CHIPS, D_MODEL)

    def gate_bwd(dm, l0, l1, ya, ys, bg):
        g0 = jax.nn.sigmoid(l0 + bg[0:1])
        g1 = jax.nn.sigmoid(l1 + bg[1:2])
        dl0 = dm * ya * g0 * (1.0 - g0)
        dl1 = dm * ys * g1 * (1.0 - g1)
        return dm * g0, dm * g1, jnp.concatenate([dl0, dl1], axis=1), _colsum(dl0), _colsum(dl1)

    d_y_attn, d_y_ssm, d_gl, d_bg0, d_bg1 = _rowwise(
        gate_bwd, [d_mixed, gl0, gl1, y_attn, y_ssm], [b_gate],
        [_sds((s, D_MODEL), BF16), _sds((s, D_MODEL), BF16), _sds((s, 2 * D_MODEL), BF16)],
        [_sds((1, D_MODEL)), _sds((1, D_MODEL))], tm=256, name="gate_bwd")

    g_w_ssm_br = _mm_cols_tn(y_glu, d_y_ssm, ns=D_MODEL // N_CHIPS, name="g_w_ssm_br")
    d_y_glu = _to_scan_rows(_mm_cols_nt(d_y_ssm, wts["w_ssm_br"], tm=s, name="d_y_glu"))

    def glu_bwd(dy, ga, gb):
        sg = jax.nn.sigmoid(gb)
        return jnp.concatenate([dy * sg, dy * ga * sg * (1.0 - sg)], axis=1)

    (d_glu,) = _rowwise(glu_bwd, [d_y_glu, (glu, SSM_WIDTH, 0), (glu, SSM_WIDTH, 1)], [],
                        [_sds((s, 2 * SSM_WIDTH), BF16)], tm=512, name="glu_bwd")
    g_w_glu = _mm_cols_tn(gel, d_glu, ns=2 * SSM_WIDTH // N_CHIPS, name="g_w_glu")
    d_gel = _mm_cols_nt(d_glu, wts["w_glu"], tm=s, name="d_gel")

    def gelu_bwd(dg, y, u, dsk):
        th = jnp.tanh(GELU_C * (y + GELU_K * y * y * y))
        dy = dg * (0.5 * (1.0 + th) + 0.5 * y * (1.0 - th * th) * GELU_C * (1.0 + 3.0 * GELU_K * y * y))
        return dy, dy * dsk, _colsum(dy * u)

    d_y, d_u_skip, d_ssm_d = _rowwise(gelu_bwd, [d_gel, y_s5, u_f], [d_skip],
                                      [_sds((s, SSM_WIDTH), BF16), _sds((s, SSM_WIDTH))], [_sds((1, SSM_WIDTH))],
                                      tm=512, name="gelu_bwd")
    g_h_r = _ssm_out(d_y, c_out_r, name="ssm_gh_re", transpose=True)
    g_h_i = _ssm_out(d_y, c_out_ni, name="ssm_gh_im", transpose=True)
    d_c_r = _diag_block_grad(h_r, d_y, name="ssm_dc_re")
    d_c_ni = _diag_block_grad(h_i, d_y, name="ssm_dc_im")
    lam_r, lam_i, d_abar_r, d_abar_i = _scan_bwd(g_h_r, g_h_i, h_r, h_i, e_r, e_i, a_r, a_i)
    d_bin_r = _diag_block_grad(u_p, lam_r, name="ssm_db_re")
    d_bin_i = _diag_block_grad(u_p, lam_i, name="ssm_db_im")
    d_u_r = _ssm_in(lam_r, b_in_r, name="ssm_du_re", transpose=True)
    d_u_i = _ssm_in(lam_i, b_in_i, name="ssm_du_im", transpose=True)

    def add3(a, b, c):
        return a + b + c

    (d_u,) = _rowwise(add3, [d_u_skip, d_u_r, d_u_i], [], [_sds((s, SSM_WIDTH), BF16)], tm=512, name="ssm_du")
    d_u = _from_scan_rows(d_u)
    d_bbar_r = d_bin_r.reshape(SSM_GROUPS, SSM_GROUP, SSM_STATE).transpose(0, 2, 1)
    d_bbar_i = d_bin_i.reshape(SSM_GROUPS, SSM_GROUP, SSM_STATE).transpose(0, 2, 1)
    d_a_re, d_a_im, d_log_dt, d_b_re, d_b_im = ssm_vjp(
        (d_abar_r.reshape(SSM_GROUPS, SSM_STATE), d_abar_i.reshape(SSM_GROUPS, SSM_STATE), d_bbar_r, d_bbar_i))
    d_c_re = d_c_r.reshape(SSM_GROUPS, SSM_STATE, SSM_GROUP).transpose(0, 2, 1)
    d_c_im = -d_c_ni.reshape(SSM_GROUPS, SSM_STATE, SSM_GROUP).transpose(0, 2, 1)

    g_w_attn_br = _mm_cols_tn(attn, d_y_attn, ns=D_MODEL // N_CHIPS, name="g_w_attn_br")
    d_attn = _mm_cols_nt(d_y_attn, wts["w_attn_br"], tm=s, name="d_attn")
    dqkv = [_attention_bwd(g, proj, cos_f, sin_s, d_attn, attn, lse) for g in range(len(DILATIONS))]

    d_proj = jnp.concatenate([dqkv[g][j] for j in range(3) for g in range(len(DILATIONS))] + [d_u, d_gl],
                             axis=1)
    g_w_in = _mm_cols_tn(x, d_proj, ns=IN_WIDTH // N_CHIPS, name="g_w_in")
    dx_proj = _mm_cols_nt(d_proj, wts["w_in"], tm=1024, name="dx_proj", after=(g_w_in,))

    def dx_sum(dz, dxp):
        return DN_ALPHA * dz + dxp

    (grad_x,) = _rowwise(dx_sum, [dz1, dx_proj], [], [_sds((s, D_MODEL))], tm=512, name="grad_x")

    big = {"w_in": g_w_in, "w_attn_br": g_w_attn_br, "w_ssm_br": g_w_ssm_br, "w_out": g_w_out, "w_glu": g_w_glu,
           "w_ff_gate": g_w_ff_gate, "w_ff_up": g_w_ff_up, "w_ff_down": g_w_ff_down}
    small_g = {"b_gate": jnp.concatenate([d_bg0, d_bg1], axis=0), "ssm_a_re": d_a_re, "ssm_a_im": d_a_im,
               "ssm_log_dt": d_log_dt, "ssm_b_re": d_b_re, "ssm_b_im": d_b_im, "ssm_c_re": d_c_re, "ssm_c_im": d_c_im,
               "ssm_d": d_ssm_d.reshape(SSM_WIDTH), "ln1_g": d_ln1_g, "ln1_b": d_ln1_b, "ln2_g": d_ln2_g,
               "ln2_b": d_ln2_b}
    marks = {"ln1_bwd": dz1, "scan_bwd": lam_r, "attention_bwd_0": dqkv[0][0], "dx_proj": dx_proj}
    return loss_v[0, 0], grad_x, big, small_g, marks


GATHER_ID, SWAP_ID, SCATTER_ID, JOIN_ID = 1, 2, 3, 4


def _place():
    return lax.axis_index("x"), lax.axis_index("y"), lax.axis_index("c")


def _other_chips(x, y):
    return [(1 - x, y), (x, 1 - y), (1 - x, 1 - y)]


def _handshake(peers):
    barrier = pltpu.get_barrier_semaphore()
    for peer in peers:
        pl.semaphore_signal(barrier, inc=1, device_id=peer, device_id_type=MESH)
    pl.semaphore_wait(barrier, len(peers))


def _sequencer(body, arrays, out_type, sems, collective_id, name):
    return pl.kernel(body, name=name, out_type=out_type,
                     mesh=plsc.ScalarSubcoreMesh(axis_name="sequencer", num_cores=1), scratch_types=sems,
                     compiler_params=pltpu.CompilerParams(collective_id=collective_id))(*arrays)


def _gather_weights(shards, *, name):
    nw = len(shards)

    def body(*refs):
        ins, outs = refs[:nw], refs[nw:2 * nw]
        send_sems, recv_sems, pass_send, pass_recv, local_sems = refs[2 * nw:]
        x, y, c = _place()
        chip = 2 * x + y
        chips = _other_chips(x, y)
        _handshake([(x, y, 1 - c)] + [(cx, cy, c) for cx, cy in chips])
        started = []
        for w in range(nw):
            hw = shards[w].shape[0] // 2
            mine = pl.ds(c * hw, hw)
            own = pltpu.make_async_copy(ins[w], outs[w].at[chip], local_sems.at[w])
            own.start()
            started.append(own)
            for j, (cx, cy) in enumerate(chips):
                cp = pltpu.make_async_remote_copy(
                    src_ref=ins[w].at[mine], dst_ref=outs[w].at[chip, mine], send_sem=send_sems.at[w, j],
                    recv_sem=recv_sems.at[w, j], device_id=(cx, cy, c), device_id_type=MESH)
                cp.start()
                started.append(cp)
        passed = []
        for w in range(nw):
            hw = shards[w].shape[0] // 2
            mine = pl.ds(c * hw, hw)
            for j, (cx, cy) in enumerate(chips):
                landed = outs[w].at[2 * cx + cy, mine]
                pltpu.make_async_remote_copy(
                    src_ref=ins[w].at[mine], dst_ref=landed, send_sem=send_sems.at[w, j],
                    recv_sem=recv_sems.at[w, j], device_id=(cx, cy, c), device_id_type=MESH).wait_recv()
                cp = pltpu.make_async_remote_copy(
                    src_ref=landed, dst_ref=landed, send_sem=pass_send.at[w, j], recv_sem=pass_recv.at[w, j],
                    device_id=(x, y, 1 - c), device_id_type=MESH)
                cp.start()
                passed.append(cp)
        for w in range(nw):
            hw = shards[w].shape[0] // 2
            theirs = pl.ds((1 - c) * hw, hw)
            for j, (cx, cy) in enumerate(chips):
                landed = outs[w].at[2 * cx + cy, theirs]
                pltpu.make_async_remote_copy(
                    src_ref=landed, dst_ref=landed, send_sem=pass_send.at[w, j], recv_sem=pass_recv.at[w, j],
                    device_id=(x, y, 1 - c), device_id_type=MESH).wait_recv()
        for cp in started[0::4]:
            cp.wait()
        for cp in [s for i, s in enumerate(started) if i % 4] + passed:
            cp.wait_send()

    sem = pltpu.SemaphoreType.DMA
    return _sequencer(body, shards, [_sds((N_CHIPS,) + a.shape, a.dtype) for a in shards],
                      [sem((nw, 3)), sem((nw, 3)), sem((nw, 3)), sem((nw, 3)), sem((nw,))], GATHER_ID, name)


def _swap_other_halves(grads, *, name):
    nw = len(grads)

    def body(*refs):
        ins, outs = refs[:nw], refs[nw:2 * nw]
        send_sems, recv_sems = refs[2 * nw:]
        x, y, c = _place()
        _handshake([(x, y, 1 - c)])
        cps = []
        for w in range(nw):
            hw = grads[w].shape[1] // 2
            cp = pltpu.make_async_remote_copy(
                src_ref=ins[w].at[:, pl.ds((1 - c) * hw, hw)], dst_ref=outs[w], send_sem=send_sems.at[w],
                recv_sem=recv_sems.at[w], device_id=(x, y, 1 - c), device_id_type=MESH)
            cp.start()
            cps.append(cp)
        for cp in cps:
            cp.wait()

    sem = pltpu.SemaphoreType.DMA
    return _sequencer(body, grads, [_sds((N_CHIPS, g.shape[1] // 2, g.shape[2]), g.dtype) for g in grads],
                      [sem((nw,)), sem((nw,))], SWAP_ID, name)


def _add_my_half(core, g, other, after=()):
    n, r, cols = g.shape
    hw = r // 2

    def body(core_ref, g_ref, o_ref, *rest):
        out_ref = rest[len(after)]
        out_ref[...] = (g_ref[...].astype(F32) + o_ref[...].astype(F32)).astype(out_ref.dtype)

    return pl.pallas_call(
        body,
        grid_spec=pltpu.PrefetchScalarGridSpec(
            num_scalar_prefetch=1, grid=(n,),
            in_specs=[pl.BlockSpec((None, None, hw, cols), lambda s, core_ref: (s, core_ref[0], 0, 0)),
                      pl.BlockSpec((None, hw, cols), lambda s, core_ref: (s, 0, 0))] + [HBM_OPERAND] * len(after),
            out_specs=pl.BlockSpec((None, hw, cols), lambda s, core_ref: (s, 0, 0))),
        out_shape=_sds((n, hw, cols), BF16), compiler_params=_cp(("parallel",)),
        name="add_my_half")(core, g.reshape(n, 2, hw, cols), other, *after)


def _scatter_partials(parts, *, name):
    nw = len(parts)

    def body(*refs):
        ins, outs = refs[:nw], refs[nw:2 * nw]
        send_sems, recv_sems = refs[2 * nw:]
        x, y, c = _place()
        _handshake([(cx, cy, c) for cx, cy in _other_chips(x, y)])
        cps = []
        for w in range(nw):
            for j, (cx, cy) in enumerate(_other_chips(x, y)):
                cp = pltpu.make_async_remote_copy(
                    src_ref=ins[w].at[2 * cx + cy], dst_ref=outs[w].at[j], send_sem=send_sems.at[w, j],
                    recv_sem=recv_sems.at[w, j], device_id=(cx, cy, c), device_id_type=MESH)
                cp.start()
                cps.append(cp)
        for cp in cps:
            cp.wait()

    sem = pltpu.SemaphoreType.DMA
    return _sequencer(body, parts, [_sds((3,) + p.shape[1:], p.dtype) for p in parts],
                      [sem((nw, 3)), sem((nw, 3))], SCATTER_ID, name)


def _sum_partials(chip, part, recv, after=()):
    _, hw, cols = part.shape
    th = hw // 2 if hw % 32 == 0 else hw

    def body(chip_ref, p_ref, r_ref, *rest):
        out_ref = rest[len(after)]
        acc = p_ref[...].astype(F32)
        for j in range(3):
            acc = acc + r_ref[j].astype(F32)
        out_ref[...] = acc

    return pl.pallas_call(
        body,
        grid_spec=pltpu.PrefetchScalarGridSpec(
            num_scalar_prefetch=1, grid=(hw // th,),
            in_specs=[pl.BlockSpec((None, th, cols), lambda i, chip_ref: (chip_ref[0], i, 0)),
                      pl.BlockSpec((3, th, cols), lambda i, chip_ref: (0, i, 0))] + [HBM_OPERAND] * len(after),
            out_specs=pl.BlockSpec((th, cols), lambda i, chip_ref: (i, 0))),
        out_shape=_sds((hw, cols)), compiler_params=_cp(("parallel",)), name="sum_partials")(
            chip, part, recv, *after)


def _swap_reduced_halves(halves, *, name):
    nw = len(halves)

    def body(*refs):
        ins, outs = refs[:nw], refs[nw:2 * nw]
        send_sems, recv_sems = refs[2 * nw:]
        x, y, c = _place()
        _handshake([(x, y, 1 - c)])
        cps = []
        for w in range(nw):
            cp = pltpu.make_async_remote_copy(
                src_ref=ins[w], dst_ref=outs[w], send_sem=send_sems.at[w], recv_sem=recv_sems.at[w],
                device_id=(x, y, 1 - c), device_id_type=MESH)
            cp.start()
            cps.append(cp)
        for cp in cps:
            cp.wait()

    sem = pltpu.SemaphoreType.DMA
    return _sequencer(body, halves, [_sds(h.shape, h.dtype) for h in halves], [sem((nw,)), sem((nw,))], JOIN_ID, name)


def _allreduce_rows(vec, *, name, after=()):
    rows = vec.shape[0]

    def body(v_ref, *rest):
        out_ref, slots, send_sems, recv_sems = rest[len(after):]
        x, y, c = _place()
        me = 4 * x + 2 * y + c
        slots[me] = v_ref[...]
        peers = []
        for mask in range(1, N_DEV):
            px = 1 - x if mask & 4 else x
            py = 1 - y if mask & 2 else y
            pc = 1 - c if mask & 1 else c
            peers.append((px, py, pc))
        cps = []
        for k, peer in enumerate(peers):
            cp = pltpu.make_async_remote_copy(
                src_ref=v_ref, dst_ref=slots.at[me], send_sem=send_sems.at[k], recv_sem=recv_sems.at[k],
                device_id=peer, device_id_type=MESH)
            cp.start()
            cps.append(cp)
        for k, (px, py, pc) in enumerate(peers):
            pltpu.make_async_remote_copy(
                src_ref=v_ref, dst_ref=slots.at[4 * px + 2 * py + pc], send_sem=send_sems.at[k],
                recv_sem=recv_sems.at[k], device_id=(px, py, pc), device_id_type=MESH).wait_recv()
        for cp in cps:
            cp.wait_send()
        acc = slots[0]
        for d in range(1, N_DEV):
            acc = acc + slots[d]
        out_ref[...] = acc

    vmem = pl.BlockSpec(memory_space=pltpu.VMEM)
    return pl.pallas_call(
        body, in_specs=[vmem] + [HBM_OPERAND] * len(after), out_specs=vmem, out_shape=_sds((rows, 128)),
        scratch_shapes=[pltpu.VMEM((N_DEV, rows, 128), F32), pltpu.SemaphoreType.DMA((N_DEV - 1,)),
                        pltpu.SemaphoreType.DMA((N_DEV - 1,))],
        compiler_params=pltpu.CompilerParams(vmem_limit_bytes=VMEM_LIMIT_BYTES), name=name)(vec, *after)


def _reduce_scatter_start(grads, core, *, tag, add_after=()):
    others = _swap_other_halves(grads, name="swap_other_halves_" + tag)
    parts = [_add_my_half(core, g, o, add_after) for g, o in zip(grads, others)]
    return parts, _scatter_partials(parts, name="scatter_partials_" + tag)


def _reduce_scatter_finish(parts, recvd, chip, *, tag, sum_after=()):
    mine = [_sum_partials(chip, p, r, sum_after) for p, r in zip(parts, recvd)]
    return mine, _swap_reduced_halves(mine, name="swap_reduced_halves_" + tag)


ADAM_BLOCK_ELEMS = 256 * 1024


def _adam_rows(rows, cols):
    tm = rows
    while tm * cols > ADAM_BLOCK_ELEMS and tm % 16 == 0:
        tm //= 2
    return tm


def _adam_step(wv, gv, mv, vv):
    m2 = ADAM_B1 * mv + (1.0 - ADAM_B1) * gv
    v2 = ADAM_B2 * vv + (1.0 - ADAM_B2) * (gv * gv)
    m_hat = m2 / (1.0 - ADAM_B1 ** ADAM_STEP)
    v_hat = v2 / (1.0 - ADAM_B2 ** ADAM_STEP)
    return -ADAM_LR * (m_hat / (jnp.sqrt(v_hat) + ADAM_EPS) + ADAM_WD * wv), m2, v2


def _adamw(w, g, m, v, *, name):
    rows, cols = w.shape
    return _rowwise(_adam_step, [w, g, m, v], [], [_sds((rows, cols))] * 3, tm=_adam_rows(rows, cols), name=name)


def _adamw_halves(core, w, g_mine, g_theirs, m, v, *, name, after=()):
    rows, cols = w.shape
    hw = rows // 2
    tm = _adam_rows(hw, cols)
    per_half = hw // tm

    def body(core_ref, w_ref, gm_ref, gt_ref, m_ref, v_ref, *rest):
        g_out, d_out, m_out, v_out = rest[len(after):]
        mine = (pl.program_id(0) // per_half) == core_ref[0]
        g = jnp.where(mine, gm_ref[...], gt_ref[...])
        d, m2, v2 = _adam_step(w_ref[...], g, m_ref[...], v_ref[...])
        g_out[...] = g
        d_out[...] = d
        m_out[...] = m2
        v_out[...] = v2

    full = pl.BlockSpec((tm, cols), lambda i, core_ref: (i, 0))
    half = pl.BlockSpec((tm, cols), lambda i, core_ref: (i % per_half, 0))
    return pl.pallas_call(
        body,
        grid_spec=pltpu.PrefetchScalarGridSpec(
            num_scalar_prefetch=1, grid=(rows // tm,),
            in_specs=[full, half, half, full, full] + [HBM_OPERAND] * len(after), out_specs=[full, full, full, full]),
        out_shape=[_sds((rows, cols))] * 4, compiler_params=_cp(("parallel",)), name=name)(
            core, w, g_mine, g_theirs, m, v, *after)


HELD_TRANSPOSED = ("w_ff_gate", "w_ff_up")


def _as_rows(name, arr):
    return arr[0].T if name in HELD_TRANSPOSED else arr[0]


def _from_rows(name, arr2d):
    return (arr2d.T if name in HELD_TRANSPOSED else arr2d)[None]


STORED_SWAPPED = ("ssm_b_re", "ssm_b_im")


def _as_stored(name, arr):
    return jnp.swapaxes(arr, -1, -2) if name in STORED_SWAPPED else arr


def _pack_rows(arrs):
    flat = jnp.concatenate([a.reshape(-1).astype(F32) for a in arrs])
    rows = -(-flat.shape[0] // 1024) * 8
    return jnp.pad(flat, (0, rows * 128 - flat.shape[0])).reshape(rows, 128)


def _unpack_rows(vec, shapes):
    flat = vec.reshape(-1)
    out, off = [], 0
    for shp in shapes:
        size = math.prod(shp)
        out.append(flat[off:off + size].reshape(shp))
        off += size
    return out


SMALL = ("b_gate", "ssm_a_re", "ssm_a_im", "ssm_log_dt", "ssm_b_re", "ssm_b_im", "ssm_c_re", "ssm_c_im", "ssm_d",
         "ln1_g", "ln1_b", "ln2_g", "ln2_b")
GATHER_GROUPS = (("w_in", ("w_in",)), ("mixer", ("w_attn_br", "w_ssm_br", "w_glu", "w_out")),
                 ("ffn", ("w_ff_gate", "w_ff_up", "w_ff_down")))
REDUCE_GROUPS = (("ffn", ("w_ff_down", "w_ff_gate", "w_ff_up")),
                 ("mixer", ("w_out", "w_ssm_br", "w_glu", "w_attn_br")), ("w_in", ("w_in",)))
WEIGHTS = ("w_in", "b_gate", "w_attn_br", "w_ssm_br", "w_out", "ssm_a_re", "ssm_a_im", "ssm_log_dt", "ssm_b_re",
           "ssm_b_im", "ssm_c_re", "ssm_c_im", "ssm_d", "w_glu", "ln1_g", "ln1_b", "w_ff_gate", "w_ff_up", "w_ff_down",
           "ln2_g", "ln2_b")


def kernel(x, w_in, b_gate, w_attn_br, w_ssm_br, w_out, ssm_a_re, ssm_a_im, ssm_log_dt, ssm_b_re, ssm_b_im, ssm_c_re, ssm_c_im, ssm_d, w_glu, ln1_g, ln1_b, w_ff_gate, w_ff_up, w_ff_down, ln2_g, ln2_b, loss_target, m_w_in, m_b_gate, m_w_attn_br, m_w_ssm_br, m_w_out, m_ssm_a_re, m_ssm_a_im, m_ssm_log_dt, m_ssm_b_re, m_ssm_b_im, m_ssm_c_re, m_ssm_c_im, m_ssm_d, m_w_glu, m_ln1_g, m_ln1_b, m_w_ff_gate, m_w_ff_up, m_w_ff_down, m_ln2_g, m_ln2_b, v_w_in, v_b_gate, v_w_attn_br, v_w_ssm_br, v_w_out, v_ssm_a_re, v_ssm_a_im, v_ssm_log_dt, v_ssm_b_re, v_ssm_b_im, v_ssm_c_re, v_ssm_c_im, v_ssm_d, v_w_glu, v_ln1_g, v_ln1_b, v_w_ff_gate, v_w_ff_up, v_w_ff_down, v_ln2_g, v_ln2_b):
    given = dict(locals())
    px, py, pc = _place()
    chip = 2 * px + py
    core_s = jnp.reshape(pc, (1,)).astype(jnp.int32)
    chip_s = jnp.reshape(chip, (1,)).astype(jnp.int32)

    wts = {}
    for tag, names in GATHER_GROUPS:
        wts.update(zip(names, _gather_weights([_as_rows(n, given[n]).astype(BF16) for n in names],
                                              name="gather_" + tag)))
    ncol = D_MODEL // N_CHIPS
    bg_mine = jnp.where(pc == 0, b_gate[0], jnp.zeros_like(b_gate[0]))
    bg_full = lax.dynamic_update_slice(jnp.zeros((2, D_MODEL), F32), bg_mine, (0, chip * ncol))
    bg_full = _allreduce_rows(bg_full.reshape(16, 128), name="gather_gate_bias").reshape(2, D_MODEL)
    small = {n: given[n][0] for n in SMALL if n.startswith("ssm")}
    small.update({n: given[n] for n in ("ln1_g", "ln1_b", "ln2_g", "ln2_b")})
    small["b_gate"] = bg_full

    loss_mine, grad_x, big_g, small_g, marks = _local_step(x[0], loss_target[0], wts, small)
    loss = lax.psum(loss_mine, ("x", "y", "c"))

    groups = dict(REDUCE_GROUPS)
    add_after = {"ffn": (marks["ln1_bwd"],), "mixer": (marks["scan_bwd"],), "w_in": (marks["dx_proj"],)}
    parts, recvd = {}, {}
    for tag, names in REDUCE_GROUPS:
        parts[tag], recvd[tag] = _reduce_scatter_start([big_g[n] for n in names], core_s, tag=tag,
                                                       add_after=add_after[tag])
    grads, delta, new_m, new_v = {}, {}, {}, {}

    def finish(tag, sum_after, adam_after):
        mine, theirs = _reduce_scatter_finish(parts[tag], recvd[tag], chip_s, tag=tag, sum_after=sum_after)
        for n, g_mine, g_theirs in zip(groups[tag], mine, theirs):
            res = _adamw_halves(core_s, _as_rows(n, given[n]), g_mine, g_theirs, _as_rows(n, given["m_" + n]),
                                _as_rows(n, given["v_" + n]), name="adamw_" + n, after=adam_after)
            grads[n], delta[n], new_m[n], new_v[n] = [_from_rows(n, r) for r in res]

    in_flight = (parts["w_in"][0],)
    finish("ffn", (marks["scan_bwd"],), (marks["attention_bwd_0"],))
    finish("mixer", (marks["attention_bwd_0"],), in_flight)
    stored = [_as_stored(n, small_g[n]) for n in SMALL]
    summed = _unpack_rows(_allreduce_rows(_pack_rows(stored), name="allreduce_small", after=in_flight),
                          [a.shape for a in stored])
    for n, g in zip(SMALL, summed):
        g = _as_stored(n, g)
        if n == "b_gate":
            g = lax.dynamic_slice(g, (0, chip * ncol), (2, ncol))
        grads[n] = g.reshape(given[n].shape)
    packed = [_pack_rows([_as_stored(n, src[n]) for n in SMALL]) for src in
              (given, grads, {n: given["m_" + n] for n in SMALL}, {n: given["v_" + n] for n in SMALL})]
    shapes = [_as_stored(n, given[n]).shape for n in SMALL]
    small_out = _adamw(*packed, name="adamw_small")
    for out, vec in zip((delta, new_m, new_v), small_out):
        out.update((n, _as_stored(n, a)) for n, a in zip(SMALL, _unpack_rows(vec, shapes)))
    behind = [delta[n] for tag in ("ffn", "mixer") for n in groups[tag]] + [small_out[0], grad_x]
    finish("w_in", tuple(behind), ())

    return (loss, grad_x.reshape(x.shape), *[grads[n] for n in WEIGHTS], *[delta[n] for n in WEIGHTS],
            *[new_m[n] for n in WEIGHTS], *[new_v[n] for n in WEIGHTS])
```

```python
import math

import jax
import jax.numpy as jnp
from jax import lax
from jax.experimental import pallas as pl
from jax.experimental.pallas import tpu as pltpu
from jax.experimental.pallas import tpu_sc as plsc

F32 = jnp.float32
BF16 = jnp.bfloat16
MESH = pl.DeviceIdType.MESH

D_MODEL = 1024
SEQ = 2048
HEAD_DIM = 64
ATTN_HEADS = 8
DILATIONS = (1, 4, 16)
ATTN_WIDTH = ATTN_HEADS * HEAD_DIM
QKV_WIDTH = 3 * ATTN_WIDTH
BLOCK = 128
ROPE_THETA = 10000.0
NEG_INF = -1e30
SSM_GROUP = 16
SSM_GROUPS = 32
SSM_WIDTH = 512
SSM_STATE = 64
SSM_LANES = SSM_GROUPS * SSM_STATE
SCAN_CHUNKS = 8
SCAN_STEPS = SEQ // SCAN_CHUNKS
SCAN_LANES = 256
IN_WIDTH = 3 * QKV_WIDTH + SSM_WIDTH + 2 * D_MODEL
D_FF = 2816
N_CHIPS = 4
N_DEV = 8
DN_ALPHA = 2.0 ** 0.25
LN_EPS = 1e-5
ADAM_LR = 0.001
ADAM_B1 = 0.9
ADAM_B2 = 0.999
ADAM_EPS = 1e-08
ADAM_WD = 0.01
ADAM_STEP = 10
GELU_C = math.sqrt(2.0 / math.pi)
GELU_K = 0.044715

VMEM_LIMIT_BYTES = 56 * 1024 * 1024


def _sds(shape, dtype=F32):
    return jax.ShapeDtypeStruct(tuple(shape), dtype)


def _cp(semantics=None):
    return pltpu.CompilerParams(dimension_semantics=semantics, vmem_limit_bytes=VMEM_LIMIT_BYTES)


HBM_OPERAND = pl.BlockSpec(memory_space=pl.ANY)


def _matmul(a, b, *, grid, a_spec, b_spec, o_spec, out_shape, dims, k_axis=None, name, after=()):
    nk = grid[k_axis] if k_axis is not None else 1
    o_block = tuple(d for d in o_spec.block_shape if d is not None)
    n_after = len(after)

    def body(a_ref, b_ref, *rest):
        o_ref, acc = rest[n_after], rest[n_after + 1:]
        part = lax.dot_general(a_ref[...].astype(BF16), b_ref[...].astype(BF16),
                               (((dims[0],), (dims[1],)), ((), ())), preferred_element_type=F32)
        if k_axis is None:
            o_ref[...] = part.astype(o_ref.dtype)
        else:
            k = pl.program_id(k_axis)

            @pl.when(k == 0)
            def _():
                acc[0][...] = part

            @pl.when(k > 0)
            def _():
                acc[0][...] += part

            @pl.when(k == nk - 1)
            def _():
                o_ref[...] = acc[0][...].astype(o_ref.dtype)

    sem = tuple("arbitrary" if ax == k_axis else "parallel" for ax in range(len(grid)))
    return pl.pallas_call(
        body, grid=grid, in_specs=[a_spec, b_spec] + [HBM_OPERAND] * n_after, out_specs=o_spec, out_shape=out_shape,
        scratch_shapes=[pltpu.VMEM(o_block, F32)] if k_axis is not None else [],
        compiler_params=_cp(sem), name=name)(a, b, *after)


def _mm_cols(a, wg, *, tm, name, out_dtype=F32, out3d=False):
    m, k = a.shape
    ns = wg.shape[2]
    if out3d:
        o_spec = pl.BlockSpec((None, tm, ns), lambda i, s: (s, i, 0))
        out_shape = _sds((N_CHIPS, m, ns), out_dtype)
    else:
        o_spec = pl.BlockSpec((tm, ns), lambda i, s: (i, s))
        out_shape = _sds((m, N_CHIPS * ns), out_dtype)
    return _matmul(a, wg, grid=(m // tm, N_CHIPS),
                   a_spec=pl.BlockSpec((tm, k), lambda i, s: (i, 0)),
                   b_spec=pl.BlockSpec((None, k, ns), lambda i, s: (s, 0, 0)),
                   o_spec=o_spec, out_shape=out_shape, dims=(1, 0), name=name)


def _mm_cols_nt(dy, wg, *, tm, name, dy3d=False, out_dtype=F32, after=()):
    k, ns = wg.shape[1], wg.shape[2]
    if dy3d:
        m = dy.shape[1]
        a_spec = pl.BlockSpec((None, tm, ns), lambda i, s: (s, i, 0))
    else:
        m = dy.shape[0]
        a_spec = pl.BlockSpec((tm, ns), lambda i, s: (i, s))
    return _matmul(dy, wg, grid=(m // tm, N_CHIPS), a_spec=a_spec,
                   b_spec=pl.BlockSpec((None, k, ns), lambda i, s: (s, 0, 0)),
                   o_spec=pl.BlockSpec((tm, k), lambda i, s: (i, 0)),
                   out_shape=_sds((m, k), out_dtype), dims=(1, 1), k_axis=1, name=name, after=after)


def _mm_cols_tn(a, dy, *, ns, name, dy3d=False):
    m, k = a.shape
    if dy3d:
        b_spec = pl.BlockSpec((None, m, ns), lambda s: (s, 0, 0))
    else:
        b_spec = pl.BlockSpec((m, ns), lambda s: (0, s))
    return _matmul(a, dy, grid=(N_CHIPS,), a_spec=pl.BlockSpec((m, k), lambda s: (0, 0)), b_spec=b_spec,
                   o_spec=pl.BlockSpec((None, k, ns), lambda s: (s, 0, 0)),
                   out_shape=_sds((N_CHIPS, k, ns), BF16), dims=(0, 0), name=name)


def _mm_plain(a, b, *, tm, tn, name, out_dtype=F32, dims=(1, 0), tk=None):
    m = a.shape[1 - dims[0]]
    kk = a.shape[dims[0]]
    n = b.shape[1 - dims[1]]
    tk = kk if tk is None else tk
    nk = kk // tk

    def a_idx(i, j, k):
        return (i, k) if dims[0] == 1 else (k, i)

    def b_idx(i, j, k):
        return (k, j) if dims[1] == 0 else (j, k)

    a_blk = (tm, tk) if dims[0] == 1 else (tk, tm)
    b_blk = (tk, tn) if dims[1] == 0 else (tn, tk)
    return _matmul(a, b, grid=(m // tm, n // tn, nk),
                   a_spec=pl.BlockSpec(a_blk, a_idx), b_spec=pl.BlockSpec(b_blk, b_idx),
                   o_spec=pl.BlockSpec((tm, tn), lambda i, j, k: (i, j)),
                   out_shape=_sds((m, n), out_dtype), dims=dims, k_axis=2 if nk > 1 else None, name=name)


def _rowwise(fn, tiled, full, outs, accs=(), *, tm, name, after=()):
    args, in_specs = [], []
    for t in tiled:
        if isinstance(t, tuple):
            arr, w, cb = t
            in_specs.append(pl.BlockSpec((tm, w), lambda i, cb=cb: (i, cb)))
        else:
            arr = t
            in_specs.append(pl.BlockSpec((tm, arr.shape[1]), lambda i: (i, 0)))
        args.append(arr)
    rows = args[0].shape[0]
    for f in full:
        in_specs.append(pl.BlockSpec(f.shape, lambda i, nd=f.ndim: (0,) * nd))
        args.append(f)
    out_specs = [pl.BlockSpec((tm, o.shape[1]), lambda i: (i, 0)) for o in outs]
    out_specs += [pl.BlockSpec(a.shape, lambda i, nd=len(a.shape): (0,) * nd) for a in accs]
    n_in, n_out = len(args), len(outs)
    in_specs += [HBM_OPERAND] * len(after)
    first_out = n_in + len(after)

    def body(*refs):
        res = fn(*[r[...] for r in refs[:n_in]])
        res = res if isinstance(res, (tuple, list)) else (res,)
        for r, v in zip(refs[first_out:first_out + n_out], res[:n_out]):
            r[...] = v.astype(r.dtype)
        i = pl.program_id(0)
        for r, v in zip(refs[first_out + n_out:], res[n_out:]):
            @pl.when(i == 0)
            def _(r=r, v=v):
                r[...] = v

            @pl.when(i > 0)
            def _(r=r, v=v):
                r[...] += v

    res = pl.pallas_call(
        body, grid=(rows // tm,), in_specs=in_specs, out_specs=out_specs, out_shape=list(outs) + list(accs),
        compiler_params=_cp(("arbitrary",) if accs else ("parallel",)), name=name)(*args, *after)
    return res


def _colsum(v):
    return jnp.sum(v, axis=0, keepdims=True)


def _ln_stats(z):
    mu = jnp.mean(z, axis=-1, keepdims=True)
    zc = z - mu
    var = jnp.mean(zc * zc, axis=-1, keepdims=True)
    rstd = lax.rsqrt(var + LN_EPS)
    return zc * rstd, rstd


def _ln_bwd(dy, xhat, rstd, g):
    dxh = dy * g
    m1 = jnp.mean(dxh, axis=-1, keepdims=True)
    m2 = jnp.mean(dxh * xhat, axis=-1, keepdims=True)
    return rstd * (dxh - m1 - xhat * m2)


def _swap_halves(t):
    w = t.shape[-1]
    lane = lax.broadcasted_iota(jnp.int32, t.shape, t.ndim - 1)
    return jnp.where((lane % HEAD_DIM) < HEAD_DIM // 2, pltpu.roll(t, w - HEAD_DIM // 2, t.ndim - 1),
                     pltpu.roll(t, HEAD_DIM // 2, t.ndim - 1))


PHASES = max(DILATIONS)
PAIR = 2 * HEAD_DIM
UNITS = SEQ // BLOCK
ROPE_ROWS = 256


def _to_phase_rows(t):
    return t.reshape(SEQ // PHASES, PHASES, t.shape[1]).transpose(1, 0, 2).reshape(t.shape)


def _reorder_rows(arr, *, to_phase, name):
    def body(i_ref, o_ref):
        for rho in range(PHASES):
            phase = pl.ds(rho * BLOCK, BLOCK)
            strided = pl.ds(rho, BLOCK, stride=PHASES)
            if to_phase:
                o_ref[phase, :] = i_ref[strided, :]
            else:
                o_ref[strided, :] = i_ref[phase, :]

    spec = pl.BlockSpec((SEQ, BLOCK), lambda j: (0, j))
    return pl.pallas_call(body, grid=(arr.shape[1] // BLOCK,), in_specs=[spec], out_specs=spec,
                          out_shape=_sds(arr.shape), compiler_params=_cp(("parallel",)), name=name)(arr)


def _rope(t, cf, ss):
    return t * cf + _swap_halves(t) * ss


def _rope_transposed(d, cf, ss):
    return d * cf + _swap_halves(d * ss)


def _unit_pieces(u, dil):
    pieces, length = PHASES // dil, 8 * dil
    if dil == 1:
        rho, i = 0, u
    elif dil == PHASES:
        rho, i = u, 0
    else:
        rho, i = jnp.bitwise_and(u, dil - 1), jnp.right_shift(u, dil.bit_length() - 1)
    before = jnp.maximum(i - 1, 0)
    cur = [pl.multiple_of((rho + dil * k) * BLOCK + length * i, 8) for k in range(pieces)]
    prev = [pl.multiple_of((rho + dil * k) * BLOCK + length * before, 8) for k in range(pieces)]
    return i, cur, prev


def _load_tile(ref, starts, dil):
    return jnp.concatenate([ref[pl.ds(st, 8 * dil), :] for st in starts], axis=0)


def _store_tile(ref, starts, dil, val, accumulate=False):
    length = 8 * dil
    for k, st in enumerate(starts):
        piece = val[k * length:(k + 1) * length]
        if accumulate:
            ref[pl.ds(st, length), :] += piece
        else:
            ref[pl.ds(st, length), :] = piece


def _tile_position(idx, dil):
    pieces, length = PHASES // dil, 8 * dil
    return pieces * jnp.bitwise_and(idx, length - 1) + jnp.right_shift(idx, length.bit_length() - 1)


def _band_mask(i, dil):
    row = lax.broadcasted_iota(jnp.int32, (BLOCK, 2 * BLOCK), 0)
    col = lax.broadcasted_iota(jnp.int32, (BLOCK, 2 * BLOCK), 1)
    key_pos = _tile_position(jnp.bitwise_and(col, BLOCK - 1), dil) + jnp.where(col >= BLOCK, 0, -BLOCK)
    dist = _tile_position(row, dil) - key_pos
    return (dist >= 0) & (dist <= BLOCK) & ((col >= BLOCK) | (i > 0))


def _causal_mask():
    row = lax.broadcasted_iota(jnp.int32, (BLOCK, BLOCK), 0)
    col = lax.broadcasted_iota(jnp.int32, (BLOCK, BLOCK), 1)
    return row >= col


def _pair_views(col0):
    return [pl.BlockSpec((SEQ, PAIR), lambda hp, g=g: (0, col0 // PAIR + g * (ATTN_WIDTH // PAIR) + hp))
            for g in range(len(DILATIONS))]


def _rotate_keys(k_refs, kr_refs, cf_ref, ss_ref):
    def step(t, carry):
        rows = pl.ds(pl.multiple_of(t * ROPE_ROWS, ROPE_ROWS), ROPE_ROWS)
        cf, ss = cf_ref[rows, :], ss_ref[rows, :]
        for k_ref, kr_ref in zip(k_refs, kr_refs):
            kr_ref[rows, :] = _rope(k_ref[rows, :], cf, ss)
        return carry

    lax.fori_loop(0, SEQ // ROPE_ROWS, step, 0)


def _attention_fwd(proj, cos_f, sin_s):
    ng = len(DILATIONS)

    def body(*refs):
        q_refs, k_refs, v_refs = refs[:ng], refs[ng:2 * ng], refs[2 * ng:3 * ng]
        cf_ref, ss_ref, attn_ref, lse_ref = refs[3 * ng:3 * ng + 4]
        kr_refs = refs[3 * ng + 4:]
        _rotate_keys(k_refs, kr_refs, cf_ref, ss_ref)
        first = lax.broadcasted_iota(jnp.int32, (BLOCK, PAIR), 1) < HEAD_DIM
        for g, dil in enumerate(DILATIONS):
            two_blocks = SEQ // dil > BLOCK

            def unit(u, carry, g=g, dil=dil, two_blocks=two_blocks):
                i, rows, prev = _unit_pieces(u, dil)
                qq = _rope(_load_tile(q_refs[g], rows, dil), _load_tile(cf_ref, rows, dil), _load_tile(ss_ref, rows, dil))
                qq = (qq * (1.0 / math.sqrt(HEAD_DIM))).astype(BF16)
                kk = _load_tile(kr_refs[g], rows, dil)
                vv = _load_tile(v_refs[g], rows, dil)
                if two_blocks:
                    kk = jnp.concatenate([_load_tile(kr_refs[g], prev, dil), kk], axis=0)
                    vv = jnp.concatenate([_load_tile(v_refs[g], prev, dil), vv], axis=0)
                    valid = _band_mask(i, dil)
                else:
                    valid = _causal_mask()
                kk, vv = kk.astype(BF16), vv.astype(BF16)
                zero = jnp.zeros_like(qq)
                outs, lses = [], []
                for qh in (jnp.where(first, qq, zero), jnp.where(first, zero, qq)):
                    s = lax.dot_general(qh, kk, (((1,), (1,)), ((), ())), preferred_element_type=F32)
                    s = jnp.where(valid, s, NEG_INF)
                    m = jnp.max(s, axis=1, keepdims=True)
                    p = jnp.exp(s - m)
                    l = jnp.sum(p, axis=1, keepdims=True)
                    outs.append(jnp.dot(p.astype(BF16), vv, preferred_element_type=F32) / l)
                    lses.append(m + jnp.log(l))
                o = jnp.where(first, outs[0], outs[1])
                lse = jnp.where(first, lses[0], lses[1])
                if g > 0:
                    lse_old = _load_tile(lse_ref, rows, dil)
                    m = jnp.maximum(lse_old, lse)
                    lse_new = m + jnp.log(jnp.exp(lse_old - m) + jnp.exp(lse - m))
                    o = _load_tile(attn_ref, rows, dil) * jnp.exp(lse_old - lse_new) + o * jnp.exp(lse - lse_new)
                    lse = lse_new
                _store_tile(attn_ref, rows, dil, o)
                _store_tile(lse_ref, rows, dil, lse)
                return carry

            lax.fori_loop(0, UNITS, unit, 0)

    whole = pl.BlockSpec((SEQ, PAIR), lambda hp: (0, 0))
    out = pl.BlockSpec((SEQ, PAIR), lambda hp: (0, hp))
    return pl.pallas_call(
        body, grid=(ATTN_WIDTH // PAIR,),
        in_specs=_pair_views(0) + _pair_views(QKV_WIDTH) + _pair_views(2 * QKV_WIDTH) + [whole, whole],
        out_specs=[out, out], out_shape=[_sds((SEQ, ATTN_WIDTH)), _sds((SEQ, ATTN_WIDTH))],
        scratch_shapes=[pltpu.VMEM((SEQ, PAIR), F32)] * ng,
        compiler_params=_cp(("parallel",)), name="attention_fwd")(*([proj] * (3 * ng)), cos_f, sin_s)


def _attention_bwd(g, proj, cos_f, sin_s, d_attn, attn, lse):
    dil = DILATIONS[g]
    two_blocks = SEQ // dil > BLOCK

    def body(q_ref, k_ref, v_ref, cf_ref, ss_ref, do_ref, o_ref, lse_ref, dq_out, dk_out, dv_out,
             kr_ref, dq_acc, dk_acc, dv_acc):
        _rotate_keys([k_ref], [kr_ref], cf_ref, ss_ref)
        dk_acc[...] = jnp.zeros_like(dk_acc)
        dv_acc[...] = jnp.zeros_like(dv_acc)
        first = lax.broadcasted_iota(jnp.int32, (BLOCK, PAIR), 1) < HEAD_DIM
        nk = 2 * BLOCK if two_blocks else BLOCK
        first_k = lax.broadcasted_iota(jnp.int32, (nk, PAIR), 1) < HEAD_DIM

        def unit(u, carry):
            i, rows, prev = _unit_pieces(u, dil)
            qq = _rope(_load_tile(q_ref, rows, dil), _load_tile(cf_ref, rows, dil), _load_tile(ss_ref, rows, dil))
            qq = (qq * (1.0 / math.sqrt(HEAD_DIM))).astype(BF16)
            kk = _load_tile(kr_ref, rows, dil)
            vv = _load_tile(v_ref, rows, dil)
            if two_blocks:
                kk = jnp.concatenate([_load_tile(kr_ref, prev, dil), kk], axis=0)
                vv = jnp.concatenate([_load_tile(v_ref, prev, dil), vv], axis=0)
                valid = _band_mask(i, dil)
            else:
                valid = _causal_mask()
            kk, vv = kk.astype(BF16), vv.astype(BF16)
            dof = _load_tile(do_ref, rows, dil)
            dd = dof * _load_tile(o_ref, rows, dil)
            lse2 = _load_tile(lse_ref, rows, dil)
            dob = dof.astype(BF16)
            zq, zd, zf = jnp.zeros_like(qq), jnp.zeros_like(dob), jnp.zeros_like(dd)
            dqs, dks, dvs = [], [], []
            for h in range(2):
                sel = first if h == 0 else jnp.logical_not(first)
                qh = jnp.where(sel, qq, zq)
                doh = jnp.where(sel, dob, zd)
                delta = jnp.sum(jnp.where(sel, dd, zf), axis=1, keepdims=True)
                lse_h = lse2[:, h * HEAD_DIM:h * HEAD_DIM + 1]
                s = lax.dot_general(qh, kk, (((1,), (1,)), ((), ())), preferred_element_type=F32)
                p = jnp.where(valid, jnp.exp(s - lse_h), 0.0)
                dp = lax.dot_general(doh, vv, (((1,), (1,)), ((), ())), preferred_element_type=F32)
                ds = (p * (dp - delta)).astype(BF16)
                dqs.append(jnp.dot(ds, kk, preferred_element_type=F32))
                dks.append(lax.dot_general(ds, qq, (((0,), (0,)), ((), ())), preferred_element_type=F32))
                dvs.append(lax.dot_general(p.astype(BF16), dob, (((0,), (0,)), ((), ())), preferred_element_type=F32))
            _store_tile(dq_acc, rows, dil, jnp.where(first, dqs[0], dqs[1]))
            dk2 = jnp.where(first_k, dks[0], dks[1])
            dv2 = jnp.where(first_k, dvs[0], dvs[1])
            _store_tile(dk_acc, rows, dil, dk2[nk - BLOCK:], accumulate=True)
            _store_tile(dv_acc, rows, dil, dv2[nk - BLOCK:], accumulate=True)
            if two_blocks:
                @pl.when(i > 0)
                def _():
                    _store_tile(dk_acc, prev, dil, dk2[:BLOCK], accumulate=True)
                    _store_tile(dv_acc, prev, dil, dv2[:BLOCK], accumulate=True)
            return carry

        lax.fori_loop(0, UNITS, unit, 0)

        def finish(t, carry):
            rows = pl.ds(pl.multiple_of(t * ROPE_ROWS, ROPE_ROWS), ROPE_ROWS)
            cf, ss = cf_ref[rows, :], ss_ref[rows, :]
            dq = dq_acc[rows, :] * (1.0 / math.sqrt(HEAD_DIM))
            dq_out[rows, :] = _rope_transposed(dq, cf, ss).astype(BF16)
            dk_out[rows, :] = _rope_transposed(dk_acc[rows, :], cf, ss).astype(BF16)
            dv_out[rows, :] = dv_acc[rows, :].astype(BF16)
            return carry

        lax.fori_loop(0, SEQ // ROPE_ROWS, finish, 0)

    whole = pl.BlockSpec((SEQ, PAIR), lambda hp: (0, 0))
    pair = pl.BlockSpec((SEQ, PAIR), lambda hp: (0, hp))
    views = [_pair_views(col0)[g] for col0 in (0, QKV_WIDTH, 2 * QKV_WIDTH)]
    return pl.pallas_call(
        body, grid=(ATTN_WIDTH // PAIR,), in_specs=views + [whole, whole, pair, pair, pair],
        out_specs=[pair, pair, pair], out_shape=[_sds((SEQ, ATTN_WIDTH), BF16)] * 3,
        scratch_shapes=[pltpu.VMEM((SEQ, PAIR), F32)] * 4,
        compiler_params=_cp(("parallel",)), name=f"attention_bwd_{g}")(proj, proj, proj, cos_f, sin_s, d_attn, attn, lse)


def _cmul(ar, ai, br, bi):
    return ar * br - ai * bi, ar * bi + ai * br


def _pow256(ar, ai):
    for _ in range(8):
        ar, ai = _cmul(ar, ai, ar, ai)
    return ar, ai


def _chunk_carries(first_r, first_i, pr, pi, reverse):
    rows = lax.broadcasted_iota(jnp.int32, first_r.shape, 0)
    out_r = jnp.zeros_like(first_r)
    out_i = jnp.zeros_like(first_i)
    hr = jnp.zeros_like(first_r[0:1])
    hi = jnp.zeros_like(hr)
    order = range(SCAN_CHUNKS - 1, -1, -1) if reverse else range(SCAN_CHUNKS)
    for c in order:
        out_r = jnp.where(rows == c, hr, out_r)
        out_i = jnp.where(rows == c, hi, out_i)
        tr, ti = _cmul(pr[0:1], pi[0:1], hr, hi)
        hr = first_r[c:c + 1] + tr
        hi = first_i[c:c + 1] + ti
    return out_r, out_i


def _tile(j):
    return pl.ds(pl.multiple_of(j * SCAN_CHUNKS, SCAN_CHUNKS), SCAN_CHUNKS)


def _to_scan_rows(t):
    per = SCAN_STEPS // PHASES
    return t.reshape(PHASES, SCAN_CHUNKS, per, t.shape[1]).transpose(2, 0, 1, 3).reshape(t.shape)


def _from_scan_rows(t):
    per = SCAN_STEPS // PHASES
    return t.reshape(per, PHASES, SCAN_CHUNKS, t.shape[1]).transpose(1, 2, 0, 3).reshape(t.shape)


def _scan_fwd(bur, bui, ar, ai):
    lb = SCAN_LANES

    def body(bur_ref, bui_ref, ar_ref, ai_ref, hr_ref, hi_ref, er_ref, ei_ref):
        a_r = jnp.broadcast_to(ar_ref[...], (SCAN_CHUNKS, lb))
        a_i = jnp.broadcast_to(ai_ref[...], (SCAN_CHUNKS, lb))

        def local(j, carry):
            tr, ti = _cmul(a_r, a_i, carry[0], carry[1])
            nr = tr + bur_ref[_tile(j), :]
            ni = ti + bui_ref[_tile(j), :]
            hr_ref[_tile(j), :] = nr
            hi_ref[_tile(j), :] = ni
            return nr, ni

        zero = jnp.zeros((SCAN_CHUNKS, lb), F32)
        last_r, last_i = lax.fori_loop(0, SCAN_STEPS, local, (zero, zero), unroll=4)
        pr, pi = _pow256(a_r, a_i)
        er, ei = _chunk_carries(last_r, last_i, pr, pi, reverse=False)
        er_ref[...] = er
        ei_ref[...] = ei

        def fix(j, carry):
            tr, ti = _cmul(carry[0], carry[1], er, ei)
            hr_ref[_tile(j), :] += tr
            hi_ref[_tile(j), :] += ti
            return _cmul(carry[0], carry[1], a_r, a_i)

        lax.fori_loop(0, SCAN_STEPS, fix, (a_r, a_i), unroll=4)

    big = pl.BlockSpec((SEQ, lb), lambda j: (0, j))
    vec = pl.BlockSpec((1, lb), lambda j: (0, j))
    ent = pl.BlockSpec((SCAN_CHUNKS, lb), lambda j: (0, j))
    return pl.pallas_call(
        body, grid=(SSM_LANES // lb,), in_specs=[big, big, vec, vec], out_specs=[big, big, ent, ent],
        out_shape=[_sds((SEQ, SSM_LANES)), _sds((SEQ, SSM_LANES)), _sds((SCAN_CHUNKS, SSM_LANES)),
                   _sds((SCAN_CHUNKS, SSM_LANES))],
        compiler_params=_cp(("parallel",)), name="ssm_scan_fwd")(bur, bui, ar, ai)


def _scan_bwd(gr, gi, hr, hi, er, ei, ar, ai):
    lb = SCAN_LANES

    def body(gr_ref, gi_ref, hr_ref, hi_ref, er_ref, ei_ref, ar_ref, ai_ref, lr_ref, li_ref, dar_ref, dai_ref):
        a_r = jnp.broadcast_to(ar_ref[...], (SCAN_CHUNKS, lb))
        a_i = -jnp.broadcast_to(ai_ref[...], (SCAN_CHUNKS, lb))

        def local(t, carry):
            j = SCAN_STEPS - 1 - t
            tr, ti = _cmul(a_r, a_i, carry[0], carry[1])
            nr = tr + gr_ref[_tile(j), :]
            ni = ti + gi_ref[_tile(j), :]
            lr_ref[_tile(j), :] = nr
            li_ref[_tile(j), :] = ni
            return nr, ni

        zero = jnp.zeros((SCAN_CHUNKS, lb), F32)
        first_r, first_i = lax.fori_loop(0, SCAN_STEPS, local, (zero, zero), unroll=4)
        pr, pi = _pow256(a_r, a_i)
        nxt_r, nxt_i = _chunk_carries(first_r, first_i, pr, pi, reverse=True)

        def accumulate(lam_r, lam_i, hp_r, hp_i, acc):
            return (acc[0] + lam_r * hp_r + lam_i * hp_i, acc[1] + lam_i * hp_r - lam_r * hp_i)

        def fix(t, carry):
            qr, qi, acc_r, acc_i = carry
            j = SCAN_STEPS - 1 - t
            tr, ti = _cmul(qr, qi, nxt_r, nxt_i)
            lam_r = lr_ref[_tile(j), :] + tr
            lam_i = li_ref[_tile(j), :] + ti
            lr_ref[_tile(j), :] = lam_r
            li_ref[_tile(j), :] = lam_i
            acc_r, acc_i = accumulate(lam_r, lam_i, hr_ref[_tile(j - 1), :], hi_ref[_tile(j - 1), :], (acc_r, acc_i))
            qr, qi = _cmul(qr, qi, a_r, a_i)
            return qr, qi, acc_r, acc_i

        qr, qi, acc_r, acc_i = lax.fori_loop(0, SCAN_STEPS - 1, fix, (a_r, a_i, zero, zero), unroll=4)
        tr, ti = _cmul(qr, qi, nxt_r, nxt_i)
        lam_r = lr_ref[_tile(0), :] + tr
        lam_i = li_ref[_tile(0), :] + ti
        lr_ref[_tile(0), :] = lam_r
        li_ref[_tile(0), :] = lam_i
        acc_r, acc_i = accumulate(lam_r, lam_i, er_ref[...], ei_ref[...], (acc_r, acc_i))
        dar_ref[...] = jnp.sum(acc_r, axis=0, keepdims=True)
        dai_ref[...] = jnp.sum(acc_i, axis=0, keepdims=True)

    big = pl.BlockSpec((SEQ, lb), lambda j: (0, j))
    vec = pl.BlockSpec((1, lb), lambda j: (0, j))
    ent = pl.BlockSpec((SCAN_CHUNKS, lb), lambda j: (0, j))
    return pl.pallas_call(
        body, grid=(SSM_LANES // lb,), in_specs=[big, big, big, big, ent, ent, vec, vec],
        out_specs=[big, big, vec, vec],
        out_shape=[_sds((SEQ, SSM_LANES)), _sds((SEQ, SSM_LANES)), _sds((1, SSM_LANES)), _sds((1, SSM_LANES))],
        compiler_params=_cp(("parallel",)), name="ssm_scan_bwd")(gr, gi, hr, hi, er, ei, ar, ai)


def _rope_tables():
    half = HEAD_DIM // 2
    inv_freq = ROPE_THETA ** (-jnp.arange(half, dtype=F32) / half)
    ang = jnp.arange(SEQ, dtype=F32)[:, None] * inv_freq[None, :]
    cos, sin = jnp.cos(ang), jnp.sin(ang)
    cos_f = jnp.concatenate([cos, cos, cos, cos], axis=1)
    sin_s = jnp.concatenate([-sin, sin, -sin, sin], axis=1)
    return cos_f, sin_s


def _ssm_discretise(a_re, a_im, log_dt, b_re, b_im):
    lam = lax.complex(a_re, a_im)
    dt = jnp.exp(log_dt)[:, None]
    a_bar = jnp.exp(lam * dt)
    b_bar = ((a_bar - 1.0) / lam)[..., None] * lax.complex(b_re, b_im)
    return a_bar.real, a_bar.imag, b_bar.real, b_bar.imag


def _block_diag_in(b):
    eye = jnp.eye(SSM_GROUPS, dtype=b.dtype)
    return (eye[:, None, :, None] * b.transpose(0, 2, 1)[:, :, None, :]).reshape(SSM_WIDTH, SSM_LANES)


def _block_diag_out(c):
    eye = jnp.eye(SSM_GROUPS, dtype=c.dtype)
    return (eye[:, None, :, None] * c.transpose(0, 2, 1)[:, :, None, :]).reshape(SSM_LANES, SSM_WIDTH)


SSM_SLABS = 4
SLAB_GROUPS = SSM_GROUPS // SSM_SLABS
SLAB_IN = SSM_WIDTH // SSM_SLABS
SLAB_STATE = SSM_LANES // SSM_SLABS


def _ssm_in(u, b_in, *, name, transpose=False):
    if transpose:
        return _matmul(u, b_in, grid=(SSM_SLABS,), a_spec=pl.BlockSpec((SEQ, SLAB_STATE), lambda j: (0, j)),
                       b_spec=pl.BlockSpec((SLAB_IN, SLAB_STATE), lambda j: (j, j)),
                       o_spec=pl.BlockSpec((SEQ, SLAB_IN), lambda j: (0, j)), out_shape=_sds((SEQ, SSM_WIDTH)),
                       dims=(1, 1), name=name)
    return _matmul(u, b_in, grid=(SSM_SLABS,), a_spec=pl.BlockSpec((SEQ, SLAB_IN), lambda j: (0, j)),
                   b_spec=pl.BlockSpec((SLAB_IN, SLAB_STATE), lambda j: (j, j)),
                   o_spec=pl.BlockSpec((SEQ, SLAB_STATE), lambda j: (0, j)), out_shape=_sds((SEQ, SSM_LANES)),
                   dims=(1, 0), name=name)


def _ssm_out(h, c_out, *, name, transpose=False):
    if transpose:
        return _matmul(h, c_out, grid=(SSM_SLABS,), a_spec=pl.BlockSpec((SEQ, SLAB_IN), lambda j: (0, j)),
                       b_spec=pl.BlockSpec((SLAB_STATE, SLAB_IN), lambda j: (j, j)),
                       o_spec=pl.BlockSpec((SEQ, SLAB_STATE), lambda j: (0, j)), out_shape=_sds((SEQ, SSM_LANES)),
                       dims=(1, 1), name=name)
    return _matmul(h, c_out, grid=(SSM_SLABS,), a_spec=pl.BlockSpec((SEQ, SLAB_STATE), lambda j: (0, j)),
                   b_spec=pl.BlockSpec((SLAB_STATE, SLAB_IN), lambda j: (j, j)),
                   o_spec=pl.BlockSpec((SEQ, SLAB_IN), lambda j: (0, j)), out_shape=_sds((SEQ, SSM_WIDTH)),
                   dims=(1, 0), name=name)


def _diag_block_grad(a, b, *, name):
    ra = a.shape[1] // SSM_SLABS
    cb = b.shape[1] // SSM_SLABS
    wa, wb = ra // SLAB_GROUPS, cb // SLAB_GROUPS

    def body(a_ref, b_ref, o_ref):
        d = lax.dot_general(a_ref[...].astype(BF16), b_ref[...].astype(BF16), (((0,), (0,)), ((), ())),
                            preferred_element_type=F32)
        row_g = jnp.right_shift(lax.broadcasted_iota(jnp.int32, (ra, cb), 0), wa.bit_length() - 1)
        col_g = jnp.right_shift(lax.broadcasted_iota(jnp.int32, (ra, cb), 1), wb.bit_length() - 1)
        d = jnp.where(row_g == col_g, d, 0.0)
        fold = (jnp.bitwise_and(lax.broadcasted_iota(jnp.int32, (cb, wb), 0), wb - 1)
                == lax.broadcasted_iota(jnp.int32, (cb, wb), 1)).astype(F32)
        o_ref[...] = jnp.dot(d, fold, preferred_element_type=F32, precision=lax.Precision.HIGHEST)

    return pl.pallas_call(
        body, grid=(SSM_SLABS,), in_specs=[pl.BlockSpec((SEQ, ra), lambda j: (0, j)),
                                           pl.BlockSpec((SEQ, cb), lambda j: (0, j))],
        out_specs=pl.BlockSpec((ra, wb), lambda j: (j, 0)), out_shape=_sds((a.shape[1], wb)),
        compiler_params=_cp(("parallel",)), name=name)(a, b)


FF_ROWS = 1024
FF_SHARD = D_FF // N_CHIPS


def _dot_nt(a, b):
    return lax.dot_general(a, b, (((1,), (1,)), ((), ())), preferred_element_type=F32)


def _ffn_up(h, w_gate_t, w_up_t):
    def body(h_ref, wg_ref, wu_ref, a_ref, b_ref, act_ref):
        hb = h_ref[...].astype(BF16)
        a = _dot_nt(hb, wg_ref[...])
        b = _dot_nt(hb, wu_ref[...])
        a_ref[...] = a
        b_ref[...] = b
        act_ref[...] = (a * jax.nn.sigmoid(a) * b).astype(BF16)

    w_spec = pl.BlockSpec((None, FF_SHARD, D_MODEL), lambda i, k: (k, 0, 0))
    o_spec = pl.BlockSpec((None, FF_ROWS, FF_SHARD), lambda i, k: (k, i, 0))
    shape = (N_CHIPS, SEQ, FF_SHARD)
    return pl.pallas_call(
        body, grid=(SEQ // FF_ROWS, N_CHIPS),
        in_specs=[pl.BlockSpec((FF_ROWS, D_MODEL), lambda i, k: (i, 0)), w_spec, w_spec],
        out_specs=[o_spec, o_spec, o_spec], out_shape=[_sds(shape), _sds(shape), _sds(shape, BF16)],
        compiler_params=_cp(("parallel", "parallel")), name="ffn_up")(h, w_gate_t, w_up_t)


def _ffn_down_bwd(dz, w_down, a, b):
    def body(dz_ref, wd_ref, a_ref, b_ref, da_ref, db_ref):
        d_act = _dot_nt(dz_ref[...].astype(BF16), wd_ref[...])
        av = a_ref[...]
        sg = jax.nn.sigmoid(av)
        da_ref[...] = (d_act * b_ref[...] * sg * (1.0 + av * (1.0 - sg))).astype(BF16)
        db_ref[...] = (d_act * av * sg).astype(BF16)

    t_spec = pl.BlockSpec((None, FF_ROWS, FF_SHARD), lambda i, k: (k, i, 0))
    shape = (N_CHIPS, SEQ, FF_SHARD)
    return pl.pallas_call(
        body, grid=(SEQ // FF_ROWS, N_CHIPS),
        in_specs=[pl.BlockSpec((FF_ROWS, D_MODEL), lambda i, k: (i, 0)),
                  pl.BlockSpec((None, FF_SHARD, D_MODEL), lambda i, k: (k, 0, 0)), t_spec, t_spec],
        out_specs=[t_spec, t_spec], out_shape=[_sds(shape, BF16), _sds(shape, BF16)],
        compiler_params=_cp(("parallel", "parallel")), name="ffn_down_bwd")(dz, w_down, a, b)


def _ffn_dh(d_a, d_b, w_gate_t, w_up_t):
    def body(da_ref, db_ref, wg_ref, wu_ref, o_ref, acc):
        k = pl.program_id(1)
        part = (jnp.dot(da_ref[...], wg_ref[...], preferred_element_type=F32)
                + jnp.dot(db_ref[...], wu_ref[...], preferred_element_type=F32))

        @pl.when(k == 0)
        def _():
            acc[...] = part

        @pl.when(k > 0)
        def _():
            acc[...] += part

        @pl.when(k == N_CHIPS - 1)
        def _():
            o_ref[...] = acc[...]

    t_spec = pl.BlockSpec((None, FF_ROWS, FF_SHARD), lambda i, k: (k, i, 0))
    w_spec = pl.BlockSpec((None, FF_SHARD, D_MODEL), lambda i, k: (k, 0, 0))
    return pl.pallas_call(
        body, grid=(SEQ // FF_ROWS, N_CHIPS), in_specs=[t_spec, t_spec, w_spec, w_spec],
        out_specs=pl.BlockSpec((FF_ROWS, D_MODEL), lambda i, k: (i, 0)), out_shape=_sds((SEQ, D_MODEL)),
        scratch_shapes=[pltpu.VMEM((FF_ROWS, D_MODEL), F32)],
        compiler_params=_cp(("parallel", "arbitrary")), name="ffn_dh")(d_a, d_b, w_gate_t, w_up_t)


def _local_step(x, tgt, wts, small):
    s = SEQ
    cos_f, sin_s = [_to_phase_rows(t) for t in _rope_tables()]
    x = _reorder_rows(x, to_phase=True, name="phase_rows_x")
    tgt = _reorder_rows(tgt, to_phase=True, name="phase_rows_target")

    proj = _mm_cols(x, wts["w_in"], tm=1024, name="proj")

    attn, lse = _attention_fwd(proj, cos_f, sin_s)
    y_attn = _mm_cols(attn, wts["w_attn_br"], tm=s, name="y_attn")

    (abar_r, abar_i, bbar_r, bbar_i), ssm_vjp = jax.vjp(
        _ssm_discretise, small["ssm_a_re"], small["ssm_a_im"], small["ssm_log_dt"], small["ssm_b_re"], small["ssm_b_im"])
    b_in_r, b_in_i = _block_diag_in(bbar_r).astype(BF16), _block_diag_in(bbar_i).astype(BF16)
    c_out_r = _block_diag_out(small["ssm_c_re"]).astype(BF16)
    c_out_ni = _block_diag_out(-small["ssm_c_im"]).astype(BF16)
    a_r, a_i = abar_r.reshape(1, SSM_LANES), abar_i.reshape(1, SSM_LANES)
    d_skip = small["ssm_d"].reshape(1, SSM_WIDTH)

    u_f = _to_scan_rows(proj[:, 3 * QKV_WIDTH:3 * QKV_WIDTH + SSM_WIDTH])
    u_p = u_f.astype(BF16)
    bu_r = _ssm_in(u_p, b_in_r, name="ssm_bu_re")
    bu_i = _ssm_in(u_p, b_in_i, name="ssm_bu_im")
    h_r, h_i, e_r, e_i = _scan_fwd(bu_r, bu_i, a_r, a_i)
    y_1 = _ssm_out(h_r, c_out_r, name="ssm_y_re")
    y_2 = _ssm_out(h_i, c_out_ni, name="ssm_y_im")

    def gelu_fwd(y1, y2, u, dsk):
        y = y1 + y2 + dsk * u
        return y, 0.5 * y * (1.0 + jnp.tanh(GELU_C * (y + GELU_K * y * y * y)))

    y_s5, gel = _rowwise(gelu_fwd, [y_1, y_2, u_f], [d_skip], [_sds((s, SSM_WIDTH)), _sds((s, SSM_WIDTH), BF16)],
                         tm=512, name="ssm_gelu")
    glu = _mm_cols(gel, wts["w_glu"], tm=s, name="glu")

    def glu_fwd(ga, gb):
        return ga * jax.nn.sigmoid(gb)

    (y_glu,) = _rowwise(glu_fwd, [(glu, SSM_WIDTH, 0), (glu, SSM_WIDTH, 1)], [], [_sds((s, SSM_WIDTH), BF16)],
                        tm=512, name="glu_gate")
    y_glu = _from_scan_rows(y_glu)
    y_ssm = _mm_cols(y_glu, wts["w_ssm_br"], tm=s, name="y_ssm")

    gl0 = (proj, D_MODEL, (3 * QKV_WIDTH + SSM_WIDTH) // D_MODEL)
    gl1 = (proj, D_MODEL, (3 * QKV_WIDTH + SSM_WIDTH) // D_MODEL + 1)
    b_gate = small["b_gate"]

    def gate_mix(l0, l1, ya, ys, bg):
        return jax.nn.sigmoid(l0 + bg[0:1]) * ya + jax.nn.sigmoid(l1 + bg[1:2]) * ys

    (mixed,) = _rowwise(gate_mix, [gl0, gl1, y_attn, y_ssm], [b_gate], [_sds((s, D_MODEL), BF16)], tm=256,
                        name="gate_mix")
    w_out = wts["w_out"].reshape(D_MODEL, D_MODEL)
    mix_out = _mm_plain(mixed, w_out, tm=1024, tn=512, name="mix_out")

    def ln1_fwd(xv, mo, g, b):
        z = DN_ALPHA * xv + mo
        xhat, _ = _ln_stats(z)
        return z, xhat * g + b

    z1, h = _rowwise(ln1_fwd, [x, mix_out], [small["ln1_g"], small["ln1_b"]],
                     [_sds((s, D_MODEL)), _sds((s, D_MODEL))], tm=256, name="ln1")

    nf = D_FF // N_CHIPS
    w_gate_t, w_up_t, w_down = wts["w_ff_gate"], wts["w_ff_up"], wts["w_ff_down"]
    ff_a, ff_b, act = _ffn_up(h, w_gate_t, w_up_t)
    ff = _matmul(act, w_down, grid=(2, N_CHIPS),
                 a_spec=pl.BlockSpec((None, 1024, nf), lambda i, k: (k, i, 0)),
                 b_spec=pl.BlockSpec((None, nf, D_MODEL), lambda i, k: (k, 0, 0)),
                 o_spec=pl.BlockSpec((1024, D_MODEL), lambda i, k: (i, 0)),
                 out_shape=_sds((s, D_MODEL)), dims=(1, 0), k_axis=1, name="ff_down")

    def ln2_loss(hv, ffv, tg, g, b):
        z = DN_ALPHA * hv + ffv
        xhat, rstd = _ln_stats(z)
        err = xhat * g + b - tg
        d_out = err * (1.0 / D_MODEL)
        loss_rows = jnp.sum(err * err, axis=-1, keepdims=True) * (0.5 / D_MODEL)
        loss = jnp.broadcast_to(jnp.sum(loss_rows, axis=0, keepdims=True), (1, 128))
        return _ln_bwd(d_out, xhat, rstd, g), loss, _colsum(d_out * xhat), _colsum(d_out)

    dz2, loss_v, d_ln2_g, d_ln2_b = _rowwise(
        ln2_loss, [h, ff, tgt], [small["ln2_g"], small["ln2_b"]], [_sds((s, D_MODEL))],
        [_sds((1, 128)), _sds((1, D_MODEL)), _sds((1, D_MODEL))], tm=256, name="ln2_loss")

    d_a, d_b = _ffn_down_bwd(dz2, w_down, ff_a, ff_b)

    def grad_rows(lhs, rhs, name):
        return _matmul(lhs, rhs, grid=(N_CHIPS,), a_spec=pl.BlockSpec((None, s, nf), lambda k: (k, 0, 0)),
                       b_spec=pl.BlockSpec((s, D_MODEL), lambda k: (0, 0)),
                       o_spec=pl.BlockSpec((None, nf, D_MODEL), lambda k: (k, 0, 0)),
                       out_shape=_sds((N_CHIPS, nf, D_MODEL), BF16), dims=(0, 0), name=name)

    g_w_ff_down = grad_rows(act, dz2, "g_w_ff_down")
    g_w_ff_gate = grad_rows(d_a, h, "g_w_ff_gate")
    g_w_ff_up = grad_rows(d_b, h, "g_w_ff_up")
    dh_ff = _ffn_dh(d_a, d_b, w_gate_t, w_up_t)

    def ln1_bwd(dz, dff, z, g):
        xhat, rstd = _ln_stats(z)
        dh = DN_ALPHA * dz + dff
        return _ln_bwd(dh, xhat, rstd, g), _colsum(dh * xhat), _colsum(dh)

    dz1, d_ln1_g, d_ln1_b = _rowwise(ln1_bwd, [dz2, dh_ff, z1], [small["ln1_g"]], [_sds((s, D_MODEL))],
                                     [_sds((1, D_MODEL)), _sds((1, D_MODEL))], tm=256, name="ln1_bwd")
    d_mixed = _mm_plain(dz1, w_out, tm=1024, tn=512, dims=(1, 1), name="d_mixed")
    g_w_out = _mm_plain(mixed, dz1, tm=D_MODEL, tn=512, dims=(0, 0), out_dtype=BF16, name="g_w_out")
    g_w_out = g_w_out.reshape(N_CHIPS, D_MODEL // N_CHIPS, D_MODEL)

    def gate_bwd(dm, l0, l1, ya, ys, bg):
        g0 = jax.nn.sigmoid(l0 + bg[0:1])
        g1 = jax.nn.sigmoid(l1 + bg[1:2])
        dl0 = dm * ya * g0 * (1.0 - g0)
        dl1 = dm * ys * g1 * (1.0 - g1)
        return dm * g0, dm * g1, jnp.concatenate([dl0, dl1], axis=1), _colsum(dl0), _colsum(dl1)

    d_y_attn, d_y_ssm, d_gl, d_bg0, d_bg1 = _rowwise(
        gate_bwd, [d_mixed, gl0, gl1, y_attn, y_ssm], [b_gate],
        [_sds((s, D_MODEL), BF16), _sds((s, D_MODEL), BF16), _sds((s, 2 * D_MODEL), BF16)],
        [_sds((1, D_MODEL)), _sds((1, D_MODEL))], tm=256, name="gate_bwd")

    g_w_ssm_br = _mm_cols_tn(y_glu, d_y_ssm, ns=D_MODEL // N_CHIPS, name="g_w_ssm_br")
    d_y_glu = _to_scan_rows(_mm_cols_nt(d_y_ssm, wts["w_ssm_br"], tm=s, name="d_y_glu"))

    def glu_bwd(dy, ga, gb):
        sg = jax.nn.sigmoid(gb)
        return jnp.concatenate([dy * sg, dy * ga * sg * (1.0 - sg)], axis=1)

    (d_glu,) = _rowwise(glu_bwd, [d_y_glu, (glu, SSM_WIDTH, 0), (glu, SSM_WIDTH, 1)], [],
                        [_sds((s, 2 * SSM_WIDTH), BF16)], tm=512, name="glu_bwd")
    g_w_glu = _mm_cols_tn(gel, d_glu, ns=2 * SSM_WIDTH // N_CHIPS, name="g_w_glu")
    d_gel = _mm_cols_nt(d_glu, wts["w_glu"], tm=s, name="d_gel")

    def gelu_bwd(dg, y, u, dsk):
        th = jnp.tanh(GELU_C * (y + GELU_K * y * y * y))
        dy = dg * (0.5 * (1.0 + th) + 0.5 * y * (1.0 - th * th) * GELU_C * (1.0 + 3.0 * GELU_K * y * y))
        return dy, dy * dsk, _colsum(dy * u)

    d_y, d_u_skip, d_ssm_d = _rowwise(gelu_bwd, [d_gel, y_s5, u_f], [d_skip],
                                      [_sds((s, SSM_WIDTH), BF16), _sds((s, SSM_WIDTH))], [_sds((1, SSM_WIDTH))],
                                      tm=512, name="gelu_bwd")
    g_h_r = _ssm_out(d_y, c_out_r, name="ssm_gh_re", transpose=True)
    g_h_i = _ssm_out(d_y, c_out_ni, name="ssm_gh_im", transpose=True)
    d_c_r = _diag_block_grad(h_r, d_y, name="ssm_dc_re")
    d_c_ni = _diag_block_grad(h_i, d_y, name="ssm_dc_im")
    lam_r, lam_i, d_abar_r, d_abar_i = _scan_bwd(g_h_r, g_h_i, h_r, h_i, e_r, e_i, a_r, a_i)
    d_bin_r = _diag_block_grad(u_p, lam_r, name="ssm_db_re")
    d_bin_i = _diag_block_grad(u_p, lam_i, name="ssm_db_im")
    d_u_r = _ssm_in(lam_r, b_in_r, name="ssm_du_re", transpose=True)
    d_u_i = _ssm_in(lam_i, b_in_i, name="ssm_du_im", transpose=True)

    def add3(a, b, c):
        return a + b + c

    (d_u,) = _rowwise(add3, [d_u_skip, d_u_r, d_u_i], [], [_sds((s, SSM_WIDTH), BF16)], tm=512, name="ssm_du")
    d_u = _from_scan_rows(d_u)
    d_bbar_r = d_bin_r.reshape(SSM_GROUPS, SSM_GROUP, SSM_STATE).transpose(0, 2, 1)
    d_bbar_i = d_bin_i.reshape(SSM_GROUPS, SSM_GROUP, SSM_STATE).transpose(0, 2, 1)
    d_a_re, d_a_im, d_log_dt, d_b_re, d_b_im = ssm_vjp(
        (d_abar_r.reshape(SSM_GROUPS, SSM_STATE), d_abar_i.reshape(SSM_GROUPS, SSM_STATE), d_bbar_r, d_bbar_i))
    d_c_re = d_c_r.reshape(SSM_GROUPS, SSM_STATE, SSM_GROUP).transpose(0, 2, 1)
    d_c_im = -d_c_ni.reshape(SSM_GROUPS, SSM_STATE, SSM_GROUP).transpose(0, 2, 1)

    g_w_attn_br = _mm_cols_tn(attn, d_y_attn, ns=D_MODEL // N_CHIPS, name="g_w_attn_br")
    d_attn = _mm_cols_nt(d_y_attn, wts["w_attn_br"], tm=s, name="d_attn")
    dqkv = [_attention_bwd(g, proj, cos_f, sin_s, d_attn, attn, lse) for g in range(len(DILATIONS))]

    d_proj = jnp.concatenate([dqkv[g][j] for j in range(3) for g in range(len(DILATIONS))] + [d_u, d_gl],
                             axis=1)
    g_w_in = _mm_cols_tn(x, d_proj, ns=IN_WIDTH // N_CHIPS, name="g_w_in")
    dx_proj = _mm_cols_nt(d_proj, wts["w_in"], tm=1024, name="dx_proj", after=(g_w_in,))

    def dx_sum(dz, dxp):
        return DN_ALPHA * dz + dxp

    (grad_x,) = _rowwise(dx_sum, [dz1, dx_proj], [], [_sds((s, D_MODEL))], tm=512, name="grad_x")
    grad_x = _reorder_rows(grad_x, to_phase=False, name="time_rows_grad_x")

    big = {"w_in": g_w_in, "w_attn_br": g_w_attn_br, "w_ssm_br": g_w_ssm_br, "w_out": g_w_out, "w_glu": g_w_glu,
           "w_ff_gate": g_w_ff_gate, "w_ff_up": g_w_ff_up, "w_ff_down": g_w_ff_down}
    small_g = {"b_gate": jnp.concatenate([d_bg0, d_bg1], axis=0), "ssm_a_re": d_a_re, "ssm_a_im": d_a_im,
               "ssm_log_dt": d_log_dt, "ssm_b_re": d_b_re, "ssm_b_im": d_b_im, "ssm_c_re": d_c_re, "ssm_c_im": d_c_im,
               "ssm_d": d_ssm_d.reshape(SSM_WIDTH), "ln1_g": d_ln1_g, "ln1_b": d_ln1_b, "ln2_g": d_ln2_g,
               "ln2_b": d_ln2_b}
    marks = {"ln1_bwd": dz1, "scan_bwd": lam_r, "attention_bwd_0": dqkv[0][0], "dx_proj": dx_proj}
    return loss_v[0, 0], grad_x, big, small_g, marks


GATHER_ID, SWAP_ID, SCATTER_ID, JOIN_ID = 1, 2, 3, 4


def _place():
    return lax.axis_index("x"), lax.axis_index("y"), lax.axis_index("c")


def _other_chips(x, y):
    return [(1 - x, y), (x, 1 - y), (1 - x, 1 - y)]


def _handshake(peers):
    barrier = pltpu.get_barrier_semaphore()
    for peer in peers:
        pl.semaphore_signal(barrier, inc=1, device_id=peer, device_id_type=MESH)
    pl.semaphore_wait(barrier, len(peers))


def _sequencer(body, arrays, out_type, sems, collective_id, name):
    return pl.kernel(body, name=name, out_type=out_type,
                     mesh=plsc.ScalarSubcoreMesh(axis_name="sequencer", num_cores=1), scratch_types=sems,
                     compiler_params=pltpu.CompilerParams(collective_id=collective_id))(*arrays)


def _gather_weights(shards, *, name):
    nw = len(shards)

    def body(*refs):
        ins, outs = refs[:nw], refs[nw:2 * nw]
        send_sems, recv_sems, pass_send, pass_recv, local_sems = refs[2 * nw:]
        x, y, c = _place()
        chip = 2 * x + y
        chips = _other_chips(x, y)
        _handshake([(x, y, 1 - c)] + [(cx, cy, c) for cx, cy in chips])
        started = []
        for w in range(nw):
            hw = shards[w].shape[0] // 2
            mine = pl.ds(c * hw, hw)
            own = pltpu.make_async_copy(ins[w], outs[w].at[chip], local_sems.at[w])
            own.start()
            started.append(own)
            for j, (cx, cy) in enumerate(chips):
                cp = pltpu.make_async_remote_copy(
                    src_ref=ins[w].at[mine], dst_ref=outs[w].at[chip, mine], send_sem=send_sems.at[w, j],
                    recv_sem=recv_sems.at[w, j], device_id=(cx, cy, c), device_id_type=MESH)
                cp.start()
                started.append(cp)
        passed = []
        for w in range(nw):
            hw = shards[w].shape[0] // 2
            mine = pl.ds(c * hw, hw)
            for j, (cx, cy) in enumerate(chips):
                landed = outs[w].at[2 * cx + cy, mine]
                pltpu.make_async_remote_copy(
                    src_ref=ins[w].at[mine], dst_ref=landed, send_sem=send_sems.at[w, j],
                    recv_sem=recv_sems.at[w, j], device_id=(cx, cy, c), device_id_type=MESH).wait_recv()
                cp = pltpu.make_async_remote_copy(
                    src_ref=landed, dst_ref=landed, send_sem=pass_send.at[w, j], recv_sem=pass_recv.at[w, j],
                    device_id=(x, y, 1 - c), device_id_type=MESH)
                cp.start()
                passed.append(cp)
        for w in range(nw):
            hw = shards[w].shape[0] // 2
            theirs = pl.ds((1 - c) * hw, hw)
            for j, (cx, cy) in enumerate(chips):
                landed = outs[w].at[2 * cx + cy, theirs]
                pltpu.make_async_remote_copy(
                    src_ref=landed, dst_ref=landed, send_sem=pass_send.at[w, j], recv_sem=pass_recv.at[w, j],
                    device_id=(x, y, 1 - c), device_id_type=MESH).wait_recv()
        for cp in started[0::4]:
            cp.wait()
        for cp in [s for i, s in enumerate(started) if i % 4] + passed:
            cp.wait_send()

    sem = pltpu.SemaphoreType.DMA
    return _sequencer(body, shards, [_sds((N_CHIPS,) + a.shape, a.dtype) for a in shards],
                      [sem((nw, 3)), sem((nw, 3)), sem((nw, 3)), sem((nw, 3)), sem((nw,))], GATHER_ID, name)


def _swap_other_halves(grads, *, name):
    nw = len(grads)

    def body(*refs):
        ins, outs = refs[:nw], refs[nw:2 * nw]
        send_sems, recv_sems = refs[2 * nw:]
        x, y, c = _place()
        _handshake([(x, y, 1 - c)])
        cps = []
        for w in range(nw):
            hw = grads[w].shape[1] // 2
            cp = pltpu.make_async_remote_copy(
                src_ref=ins[w].at[:, pl.ds((1 - c) * hw, hw)], dst_ref=outs[w], send_sem=send_sems.at[w],
                recv_sem=recv_sems.at[w], device_id=(x, y, 1 - c), device_id_type=MESH)
            cp.start()
            cps.append(cp)
        for cp in cps:
            cp.wait()

    sem = pltpu.SemaphoreType.DMA
    return _sequencer(body, grads, [_sds((N_CHIPS, g.shape[1] // 2, g.shape[2]), g.dtype) for g in grads],
                      [sem((nw,)), sem((nw,))], SWAP_ID, name)


def _add_my_half(core, g, other, after=()):
    n, r, cols = g.shape
    hw = r // 2

    def body(core_ref, g_ref, o_ref, *rest):
        out_ref = rest[len(after)]
        out_ref[...] = (g_ref[...].astype(F32) + o_ref[...].astype(F32)).astype(out_ref.dtype)

    return pl.pallas_call(
        body,
        grid_spec=pltpu.PrefetchScalarGridSpec(
            num_scalar_prefetch=1, grid=(n,),
            in_specs=[pl.BlockSpec((None, None, hw, cols), lambda s, core_ref: (s, core_ref[0], 0, 0)),
                      pl.BlockSpec((None, hw, cols), lambda s, core_ref: (s, 0, 0))] + [HBM_OPERAND] * len(after),
            out_specs=pl.BlockSpec((None, hw, cols), lambda s, core_ref: (s, 0, 0))),
        out_shape=_sds((n, hw, cols), BF16), compiler_params=_cp(("parallel",)),
        name="add_my_half")(core, g.reshape(n, 2, hw, cols), other, *after)


def _scatter_partials(parts, *, name):
    nw = len(parts)

    def body(*refs):
        ins, outs = refs[:nw], refs[nw:2 * nw]
        send_sems, recv_sems = refs[2 * nw:]
        x, y, c = _place()
        _handshake([(cx, cy, c) for cx, cy in _other_chips(x, y)])
        cps = []
        for w in range(nw):
            for j, (cx, cy) in enumerate(_other_chips(x, y)):
                cp = pltpu.make_async_remote_copy(
                    src_ref=ins[w].at[2 * cx + cy], dst_ref=outs[w].at[j], send_sem=send_sems.at[w, j],
                    recv_sem=recv_sems.at[w, j], device_id=(cx, cy, c), device_id_type=MESH)
                cp.start()
                cps.append(cp)
        for cp in cps:
            cp.wait()

    sem = pltpu.SemaphoreType.DMA
    return _sequencer(body, parts, [_sds((3,) + p.shape[1:], p.dtype) for p in parts],
                      [sem((nw, 3)), sem((nw, 3))], SCATTER_ID, name)


def _sum_partials(chip, part, recv, after=()):
    _, hw, cols = part.shape
    th = hw // 2 if hw % 32 == 0 else hw

    def body(chip_ref, p_ref, r_ref, *rest):
        out_ref = rest[len(after)]
        acc = p_ref[...].astype(F32)
        for j in range(3):
            acc = acc + r_ref[j].astype(F32)
        out_ref[...] = acc

    return pl.pallas_call(
        body,
        grid_spec=pltpu.PrefetchScalarGridSpec(
            num_scalar_prefetch=1, grid=(hw // th,),
            in_specs=[pl.BlockSpec((None, th, cols), lambda i, chip_ref: (chip_ref[0], i, 0)),
                      pl.BlockSpec((3, th, cols), lambda i, chip_ref: (0, i, 0))] + [HBM_OPERAND] * len(after),
            out_specs=pl.BlockSpec((th, cols), lambda i, chip_ref: (i, 0))),
        out_shape=_sds((hw, cols)), compiler_params=_cp(("parallel",)), name="sum_partials")(
            chip, part, recv, *after)


def _swap_reduced_halves(halves, *, name):
    nw = len(halves)

    def body(*refs):
        ins, outs = refs[:nw], refs[nw:2 * nw]
        send_sems, recv_sems = refs[2 * nw:]
        x, y, c = _place()
        _handshake([(x, y, 1 - c)])
        cps = []
        for w in range(nw):
            cp = pltpu.make_async_remote_copy(
                src_ref=ins[w], dst_ref=outs[w], send_sem=send_sems.at[w], recv_sem=recv_sems.at[w],
                device_id=(x, y, 1 - c), device_id_type=MESH)
            cp.start()
            cps.append(cp)
        for cp in cps:
            cp.wait()

    sem = pltpu.SemaphoreType.DMA
    return _sequencer(body, halves, [_sds(h.shape, h.dtype) for h in halves], [sem((nw,)), sem((nw,))], JOIN_ID, name)


def _allreduce_rows(vec, *, name, after=()):
    rows = vec.shape[0]

    def body(v_ref, *rest):
        out_ref, slots, send_sems, recv_sems = rest[len(after):]
        x, y, c = _place()
        me = 4 * x + 2 * y + c
        slots[me] = v_ref[...]
        peers = []
        for mask in range(1, N_DEV):
            px = 1 - x if mask & 4 else x
            py = 1 - y if mask & 2 else y
            pc = 1 - c if mask & 1 else c
            peers.append((px, py, pc))
        cps = []
        for k, peer in enumerate(peers):
            cp = pltpu.make_async_remote_copy(
                src_ref=v_ref, dst_ref=slots.at[me], send_sem=send_sems.at[k], recv_sem=recv_sems.at[k],
                device_id=peer, device_id_type=MESH)
            cp.start()
            cps.append(cp)
        for k, (px, py, pc) in enumerate(peers):
            pltpu.make_async_remote_copy(
                src_ref=v_ref, dst_ref=slots.at[4 * px + 2 * py + pc], send_sem=send_sems.at[k],
                recv_sem=recv_sems.at[k], device_id=(px, py, pc), device_id_type=MESH).wait_recv()
        for cp in cps:
            cp.wait_send()
        acc = slots[0]
        for d in range(1, N_DEV):
            acc = acc + slots[d]
        out_ref[...] = acc

    vmem = pl.BlockSpec(memory_space=pltpu.VMEM)
    return pl.pallas_call(
        body, in_specs=[vmem] + [HBM_OPERAND] * len(after), out_specs=vmem, out_shape=_sds((rows, 128)),
        scratch_shapes=[pltpu.VMEM((N_DEV, rows, 128), F32), pltpu.SemaphoreType.DMA((N_DEV - 1,)),
                        pltpu.SemaphoreType.DMA((N_DEV - 1,))],
        compiler_params=pltpu.CompilerParams(vmem_limit_bytes=VMEM_LIMIT_BYTES), name=name)(vec, *after)


def _reduce_scatter_start(grads, core, *, tag, add_after=()):
    others = _swap_other_halves(grads, name="swap_other_halves_" + tag)
    parts = [_add_my_half(core, g, o, add_after) for g, o in zip(grads, others)]
    return parts, _scatter_partials(parts, name="scatter_partials_" + tag)


def _reduce_scatter_finish(parts, recvd, chip, *, tag, sum_after=()):
    mine = [_sum_partials(chip, p, r, sum_after) for p, r in zip(parts, recvd)]
    return mine, _swap_reduced_halves(mine, name="swap_reduced_halves_" + tag)


ADAM_BLOCK_ELEMS = 256 * 1024


def _adam_rows(rows, cols):
    tm = rows
    while tm * cols > ADAM_BLOCK_ELEMS and tm % 16 == 0:
        tm //= 2
    return tm


def _adam_step(wv, gv, mv, vv):
    m2 = ADAM_B1 * mv + (1.0 - ADAM_B1) * gv
    v2 = ADAM_B2 * vv + (1.0 - ADAM_B2) * (gv * gv)
    m_hat = m2 / (1.0 - ADAM_B1 ** ADAM_STEP)
    v_hat = v2 / (1.0 - ADAM_B2 ** ADAM_STEP)
    return -ADAM_LR * (m_hat / (jnp.sqrt(v_hat) + ADAM_EPS) + ADAM_WD * wv), m2, v2


def _adamw(w, g, m, v, *, name):
    rows, cols = w.shape
    return _rowwise(_adam_step, [w, g, m, v], [], [_sds((rows, cols))] * 3, tm=_adam_rows(rows, cols), name=name)


def _adamw_halves(core, w, g_mine, g_theirs, m, v, *, name, after=()):
    rows, cols = w.shape
    hw = rows // 2
    tm = _adam_rows(hw, cols)
    per_half = hw // tm

    def body(core_ref, w_ref, gm_ref, gt_ref, m_ref, v_ref, *rest):
        g_out, d_out, m_out, v_out = rest[len(after):]
        mine = (pl.program_id(0) // per_half) == core_ref[0]
        g = jnp.where(mine, gm_ref[...], gt_ref[...])
        d, m2, v2 = _adam_step(w_ref[...], g, m_ref[...], v_ref[...])
        g_out[...] = g
        d_out[...] = d
        m_out[...] = m2
        v_out[...] = v2

    full = pl.BlockSpec((tm, cols), lambda i, core_ref: (i, 0))
    half = pl.BlockSpec((tm, cols), lambda i, core_ref: (i % per_half, 0))
    return pl.pallas_call(
        body,
        grid_spec=pltpu.PrefetchScalarGridSpec(
            num_scalar_prefetch=1, grid=(rows // tm,),
            in_specs=[full, half, half, full, full] + [HBM_OPERAND] * len(after), out_specs=[full, full, full, full]),
        out_shape=[_sds((rows, cols))] * 4, compiler_params=_cp(("parallel",)), name=name)(
            core, w, g_mine, g_theirs, m, v, *after)


HELD_TRANSPOSED = ("w_ff_gate", "w_ff_up")


def _as_rows(name, arr):
    return arr[0].T if name in HELD_TRANSPOSED else arr[0]


def _from_rows(name, arr2d):
    return (arr2d.T if name in HELD_TRANSPOSED else arr2d)[None]


STORED_SWAPPED = ("ssm_b_re", "ssm_b_im")


def _as_stored(name, arr):
    return jnp.swapaxes(arr, -1, -2) if name in STORED_SWAPPED else arr


def _pack_rows(arrs):
    flat = jnp.concatenate([a.reshape(-1).astype(F32) for a in arrs])
    rows = -(-flat.shape[0] // 1024) * 8
    return jnp.pad(flat, (0, rows * 128 - flat.shape[0])).reshape(rows, 128)


def _unpack_rows(vec, shapes):
    flat = vec.reshape(-1)
    out, off = [], 0
    for shp in shapes:
        size = math.prod(shp)
        out.append(flat[off:off + size].reshape(shp))
        off += size
    return out


SMALL = ("b_gate", "ssm_a_re", "ssm_a_im", "ssm_log_dt", "ssm_b_re", "ssm_b_im", "ssm_c_re", "ssm_c_im", "ssm_d",
         "ln1_g", "ln1_b", "ln2_g", "ln2_b")
GATHER_GROUPS = (("w_in", ("w_in",)), ("mixer", ("w_attn_br", "w_ssm_br", "w_glu", "w_out")),
                 ("ffn", ("w_ff_gate", "w_ff_up", "w_ff_down")))
REDUCE_GROUPS = (("ffn", ("w_ff_down", "w_ff_gate", "w_ff_up")),
                 ("mixer", ("w_out", "w_ssm_br", "w_glu", "w_attn_br")), ("w_in", ("w_in",)))
WEIGHTS = ("w_in", "b_gate", "w_attn_br", "w_ssm_br", "w_out", "ssm_a_re", "ssm_a_im", "ssm_log_dt", "ssm_b_re",
           "ssm_b_im", "ssm_c_re", "ssm_c_im", "ssm_d", "w_glu", "ln1_g", "ln1_b", "w_ff_gate", "w_ff_up", "w_ff_down",
           "ln2_g", "ln2_b")


def kernel(x, w_in, b_gate, w_attn_br, w_ssm_br, w_out, ssm_a_re, ssm_a_im, ssm_log_dt, ssm_b_re, ssm_b_im, ssm_c_re, ssm_c_im, ssm_d, w_glu, ln1_g, ln1_b, w_ff_gate, w_ff_up, w_ff_down, ln2_g, ln2_b, loss_target, m_w_in, m_b_gate, m_w_attn_br, m_w_ssm_br, m_w_out, m_ssm_a_re, m_ssm_a_im, m_ssm_log_dt, m_ssm_b_re, m_ssm_b_im, m_ssm_c_re, m_ssm_c_im, m_ssm_d, m_w_glu, m_ln1_g, m_ln1_b, m_w_ff_gate, m_w_ff_up, m_w_ff_down, m_ln2_g, m_ln2_b, v_w_in, v_b_gate, v_w_attn_br, v_w_ssm_br, v_w_out, v_ssm_a_re, v_ssm_a_im, v_ssm_log_dt, v_ssm_b_re, v_ssm_b_im, v_ssm_c_re, v_ssm_c_im, v_ssm_d, v_w_glu, v_ln1_g, v_ln1_b, v_w_ff_gate, v_w_ff_up, v_w_ff_down, v_ln2_g, v_ln2_b):
    given = dict(locals())
    px, py, pc = _place()
    chip = 2 * px + py
    core_s = jnp.reshape(pc, (1,)).astype(jnp.int32)
    chip_s = jnp.reshape(chip, (1,)).astype(jnp.int32)

    wts = {}
    for tag, names in GATHER_GROUPS:
        wts.update(zip(names, _gather_weights([_as_rows(n, given[n]).astype(BF16) for n in names],
                                              name="gather_" + tag)))
    ncol = D_MODEL // N_CHIPS
    bg_mine = jnp.where(pc == 0, b_gate[0], jnp.zeros_like(b_gate[0]))
    bg_full = lax.dynamic_update_slice(jnp.zeros((2, D_MODEL), F32), bg_mine, (0, chip * ncol))
    bg_full = _allreduce_rows(bg_full.reshape(16, 128), name="gather_gate_bias").reshape(2, D_MODEL)
    small = {n: given[n][0] for n in SMALL if n.startswith("ssm")}
    small.update({n: given[n] for n in ("ln1_g", "ln1_b", "ln2_g", "ln2_b")})
    small["b_gate"] = bg_full

    loss_mine, grad_x, big_g, small_g, marks = _local_step(x[0], loss_target[0], wts, small)
    loss = lax.psum(loss_mine, ("x", "y", "c"))

    groups = dict(REDUCE_GROUPS)
    add_after = {"ffn": (marks["ln1_bwd"],), "mixer": (marks["scan_bwd"],), "w_in": (marks["dx_proj"],)}
    parts, recvd = {}, {}
    for tag, names in REDUCE_GROUPS:
        parts[tag], recvd[tag] = _reduce_scatter_start([big_g[n] for n in names], core_s, tag=tag,
                                                       add_after=add_after[tag])
    grads, delta, new_m, new_v = {}, {}, {}, {}

    def finish(tag, sum_after, adam_after):
        mine, theirs = _reduce_scatter_finish(parts[tag], recvd[tag], chip_s, tag=tag, sum_after=sum_after)
        for n, g_mine, g_theirs in zip(groups[tag], mine, theirs):
            res = _adamw_halves(core_s, _as_rows(n, given[n]), g_mine, g_theirs, _as_rows(n, given["m_" + n]),
                                _as_rows(n, given["v_" + n]), name="adamw_" + n, after=adam_after)
            grads[n], delta[n], new_m[n], new_v[n] = [_from_rows(n, r) for r in res]

    in_flight = (parts["w_in"][0],)
    finish("ffn", (marks["scan_bwd"],), (marks["attention_bwd_0"],))
    finish("mixer", (marks["attention_bwd_0"],), in_flight)
    stored = [_as_stored(n, small_g[n]) for n in SMALL]
    summed = _unpack_rows(_allreduce_rows(_pack_rows(stored), name="allreduce_small", after=in_flight),
                          [a.shape for a in stored])
    for n, g in zip(SMALL, summed):
        g = _as_stored(n, g)
        if n == "b_gate":
            g = lax.dynamic_slice(g, (0, chip * ncol), (2, ncol))
        grads[n] = g.reshape(given[n].shape)
    packed = [_pack_rows([_as_stored(n, src[n]) for n in SMALL]) for src in
              (given, grads, {n: given["m_" + n] for n in SMALL}, {n: given["v_" + n] for n in SMALL})]
    shapes = [_as_stored(n, given[n]).shape for n in SMALL]
    small_out = _adamw(*packed, name="adamw_small")
    for out, vec in zip((delta, new_m, new_v), small_out):
        out.update((n, _as_stored(n, a)) for n, a in zip(SMALL, _unpack_rows(vec, shapes)))
    behind = [delta[n] for tag in ("ffn", "mixer") for n in groups[tag]] + [small_out[0], grad_x]
    finish("w_in", tuple(behind), ())

    return (loss, grad_x.reshape(x.shape), *[grads[n] for n in WEIGHTS], *[delta[n] for n in WEIGHTS],
            *[new_m[n] for n in WEIGHTS], *[new_v[n] for n in WEIGHTS])
```

```python
import math

import jax
import jax.numpy as jnp
from jax import lax
from jax.experimental import pallas as pl
from jax.experimental.pallas import tpu as pltpu
from jax.experimental.pallas import tpu_sc as plsc

F32 = jnp.float32
BF16 = jnp.bfloat16
MESH = pl.DeviceIdType.MESH

D_MODEL = 1024
SEQ = 2048
HEAD_DIM = 64
ATTN_HEADS = 8
DILATIONS = (1, 4, 16)
ATTN_WIDTH = ATTN_HEADS * HEAD_DIM
QKV_WIDTH = 3 * ATTN_WIDTH
BLOCK = 128
ROPE_THETA = 10000.0
NEG_INF = -1e30
SSM_GROUP = 16
SSM_GROUPS = 32
SSM_WIDTH = 512
SSM_STATE = 64
SSM_LANES = SSM_GROUPS * SSM_STATE
SCAN_CHUNKS = 8
SCAN_STEPS = SEQ // SCAN_CHUNKS
SCAN_LANES = 256
IN_WIDTH = 3 * QKV_WIDTH + SSM_WIDTH + 2 * D_MODEL
D_FF = 2816
N_CHIPS = 4
N_DEV = 8
DN_ALPHA = 2.0 ** 0.25
LN_EPS = 1e-5
ADAM_LR = 0.001
ADAM_B1 = 0.9
ADAM_B2 = 0.999
ADAM_EPS = 1e-08
ADAM_WD = 0.01
ADAM_STEP = 10
GELU_C = math.sqrt(2.0 / math.pi)
GELU_K = 0.044715

VMEM_LIMIT_BYTES = 56 * 1024 * 1024


def _sds(shape, dtype=F32):
    return jax.ShapeDtypeStruct(tuple(shape), dtype)


def _cp(semantics=None):
    return pltpu.CompilerParams(dimension_semantics=semantics, vmem_limit_bytes=VMEM_LIMIT_BYTES)


HBM_OPERAND = pl.BlockSpec(memory_space=pl.ANY)


def _matmul(a, b, *, grid, a_spec, b_spec, o_spec, out_shape, dims, k_axis=None, name, after=()):
    nk = grid[k_axis] if k_axis is not None else 1
    o_block = tuple(d for d in o_spec.block_shape if d is not None)
    n_after = len(after)

    def body(a_ref, b_ref, *rest):
        o_ref, acc = rest[n_after], rest[n_after + 1:]
        part = lax.dot_general(a_ref[...].astype(BF16), b_ref[...].astype(BF16),
                               (((dims[0],), (dims[1],)), ((), ())), preferred_element_type=F32)
        if k_axis is None:
            o_ref[...] = part.astype(o_ref.dtype)
        else:
            k = pl.program_id(k_axis)

            @pl.when(k == 0)
            def _():
                acc[0][...] = part

            @pl.when(k > 0)
            def _():
                acc[0][...] += part

            @pl.when(k == nk - 1)
            def _():
                o_ref[...] = acc[0][...].astype(o_ref.dtype)

    sem = tuple("arbitrary" if ax == k_axis else "parallel" for ax in range(len(grid)))
    return pl.pallas_call(
        body, grid=grid, in_specs=[a_spec, b_spec] + [HBM_OPERAND] * n_after, out_specs=o_spec, out_shape=out_shape,
        scratch_shapes=[pltpu.VMEM(o_block, F32)] if k_axis is not None else [],
        compiler_params=_cp(sem), name=name)(a, b, *after)


def _mm_cols(a, wg, *, tm, name, out_dtype=F32, out3d=False):
    m, k = a.shape
    ns = wg.shape[2]
    if out3d:
        o_spec = pl.BlockSpec((None, tm, ns), lambda i, s: (s, i, 0))
        out_shape = _sds((N_CHIPS, m, ns), out_dtype)
    else:
        o_spec = pl.BlockSpec((tm, ns), lambda i, s: (i, s))
        out_shape = _sds((m, N_CHIPS * ns), out_dtype)
    return _matmul(a, wg, grid=(m // tm, N_CHIPS),
                   a_spec=pl.BlockSpec((tm, k), lambda i, s: (i, 0)),
                   b_spec=pl.BlockSpec((None, k, ns), lambda i, s: (s, 0, 0)),
                   o_spec=o_spec, out_shape=out_shape, dims=(1, 0), name=name)


def _mm_cols_nt(dy, wg, *, tm, name, dy3d=False, out_dtype=F32, after=()):
    k, ns = wg.shape[1], wg.shape[2]
    if dy3d:
        m = dy.shape[1]
        a_spec = pl.BlockSpec((None, tm, ns), lambda i, s: (s, i, 0))
    else:
        m = dy.shape[0]
        a_spec = pl.BlockSpec((tm, ns), lambda i, s: (i, s))
    return _matmul(dy, wg, grid=(m // tm, N_CHIPS), a_spec=a_spec,
                   b_spec=pl.BlockSpec((None, k, ns), lambda i, s: (s, 0, 0)),
                   o_spec=pl.BlockSpec((tm, k), lambda i, s: (i, 0)),
                   out_shape=_sds((m, k), out_dtype), dims=(1, 1), k_axis=1, name=name, after=after)


def _mm_cols_tn(a, dy, *, ns, name, dy3d=False):
    m, k = a.shape
    if dy3d:
        b_spec = pl.BlockSpec((None, m, ns), lambda s: (s, 0, 0))
    else:
        b_spec = pl.BlockSpec((m, ns), lambda s: (0, s))
    return _matmul(a, dy, grid=(N_CHIPS,), a_spec=pl.BlockSpec((m, k), lambda s: (0, 0)), b_spec=b_spec,
                   o_spec=pl.BlockSpec((None, k, ns), lambda s: (s, 0, 0)),
                   out_shape=_sds((N_CHIPS, k, ns), BF16), dims=(0, 0), name=name)


def _mm_plain(a, b, *, tm, tn, name, out_dtype=F32, dims=(1, 0), tk=None):
    m = a.shape[1 - dims[0]]
    kk = a.shape[dims[0]]
    n = b.shape[1 - dims[1]]
    tk = kk if tk is None else tk
    nk = kk // tk

    def a_idx(i, j, k):
        return (i, k) if dims[0] == 1 else (k, i)

    def b_idx(i, j, k):
        return (k, j) if dims[1] == 0 else (j, k)

    a_blk = (tm, tk) if dims[0] == 1 else (tk, tm)
    b_blk = (tk, tn) if dims[1] == 0 else (tn, tk)
    return _matmul(a, b, grid=(m // tm, n // tn, nk),
                   a_spec=pl.BlockSpec(a_blk, a_idx), b_spec=pl.BlockSpec(b_blk, b_idx),
                   o_spec=pl.BlockSpec((tm, tn), lambda i, j, k: (i, j)),
                   out_shape=_sds((m, n), out_dtype), dims=dims, k_axis=2 if nk > 1 else None, name=name)


def _rowwise(fn, tiled, full, outs, accs=(), *, tm, name, after=()):
    args, in_specs = [], []
    for t in tiled:
        if isinstance(t, tuple):
            arr, w, cb = t
            in_specs.append(pl.BlockSpec((tm, w), lambda i, cb=cb: (i, cb)))
        else:
            arr = t
            in_specs.append(pl.BlockSpec((tm, arr.shape[1]), lambda i: (i, 0)))
        args.append(arr)
    rows = args[0].shape[0]
    for f in full:
        in_specs.append(pl.BlockSpec(f.shape, lambda i, nd=f.ndim: (0,) * nd))
        args.append(f)
    out_specs = [pl.BlockSpec((tm, o.shape[1]), lambda i: (i, 0)) for o in outs]
    out_specs += [pl.BlockSpec(a.shape, lambda i, nd=len(a.shape): (0,) * nd) for a in accs]
    n_in, n_out = len(args), len(outs)
    in_specs += [HBM_OPERAND] * len(after)
    first_out = n_in + len(after)

    def body(*refs):
        res = fn(*[r[...] for r in refs[:n_in]])
        res = res if isinstance(res, (tuple, list)) else (res,)
        for r, v in zip(refs[first_out:first_out + n_out], res[:n_out]):
            r[...] = v.astype(r.dtype)
        i = pl.program_id(0)
        for r, v in zip(refs[first_out + n_out:], res[n_out:]):
            @pl.when(i == 0)
            def _(r=r, v=v):
                r[...] = v

            @pl.when(i > 0)
            def _(r=r, v=v):
                r[...] += v

    res = pl.pallas_call(
        body, grid=(rows // tm,), in_specs=in_specs, out_specs=out_specs, out_shape=list(outs) + list(accs),
        compiler_params=_cp(("arbitrary",) if accs else ("parallel",)), name=name)(*args, *after)
    return res


def _colsum(v):
    return jnp.sum(v, axis=0, keepdims=True)


def _ln_stats(z):
    mu = jnp.mean(z, axis=-1, keepdims=True)
    zc = z - mu
    var = jnp.mean(zc * zc, axis=-1, keepdims=True)
    rstd = lax.rsqrt(var + LN_EPS)
    return zc * rstd, rstd


def _ln_bwd(dy, xhat, rstd, g):
    dxh = dy * g
    m1 = jnp.mean(dxh, axis=-1, keepdims=True)
    m2 = jnp.mean(dxh * xhat, axis=-1, keepdims=True)
    return rstd * (dxh - m1 - xhat * m2)


def _swap_halves(t):
    w = t.shape[-1]
    lane = lax.broadcasted_iota(jnp.int32, t.shape, t.ndim - 1)
    return jnp.where((lane % HEAD_DIM) < HEAD_DIM // 2, pltpu.roll(t, w - HEAD_DIM // 2, t.ndim - 1),
                     pltpu.roll(t, HEAD_DIM // 2, t.ndim - 1))


PHASES = max(DILATIONS)
PAIR = 2 * HEAD_DIM
UNITS = SEQ // BLOCK
UNIT_UNROLL = 4
ROPE_ROWS = 256


def _to_phase_rows(t):
    return t.reshape(SEQ // PHASES, PHASES, t.shape[1]).transpose(1, 0, 2).reshape(t.shape)


def _reorder_rows(arr, *, to_phase, name):
    def body(i_ref, o_ref):
        for rho in range(PHASES):
            phase = pl.ds(rho * BLOCK, BLOCK)
            strided = pl.ds(rho, BLOCK, stride=PHASES)
            if to_phase:
                o_ref[phase, :] = i_ref[strided, :]
            else:
                o_ref[strided, :] = i_ref[phase, :]

    spec = pl.BlockSpec((SEQ, BLOCK), lambda j: (0, j))
    return pl.pallas_call(body, grid=(arr.shape[1] // BLOCK,), in_specs=[spec], out_specs=spec,
                          out_shape=_sds(arr.shape), compiler_params=_cp(("parallel",)), name=name)(arr)


def _rope(t, cf, ss):
    return t * cf + _swap_halves(t) * ss


def _rope_transposed(d, cf, ss):
    return d * cf + _swap_halves(d * ss)


def _unit_pieces(u, dil):
    pieces, length = PHASES // dil, 8 * dil
    if dil == 1:
        rho, i = 0, u
    elif dil == PHASES:
        rho, i = u, 0
    else:
        rho, i = jnp.bitwise_and(u, dil - 1), jnp.right_shift(u, dil.bit_length() - 1)
    before = jnp.maximum(i - 1, 0)
    cur = [pl.multiple_of((rho + dil * k) * BLOCK + length * i, 8) for k in range(pieces)]
    prev = [pl.multiple_of((rho + dil * k) * BLOCK + length * before, 8) for k in range(pieces)]
    return i, cur, prev


def _load_tile(ref, starts, dil):
    return jnp.concatenate([ref[pl.ds(st, 8 * dil), :] for st in starts], axis=0)


def _store_tile(ref, starts, dil, val, head=None, accumulate=False):
    length = 8 * dil
    lanes = slice(None) if head is None else pl.ds(head * HEAD_DIM, HEAD_DIM)
    cols = slice(None) if head is None else slice(head * HEAD_DIM, (head + 1) * HEAD_DIM)
    for k, st in enumerate(starts):
        piece = val[k * length:(k + 1) * length, cols]
        if accumulate:
            ref[pl.ds(st, length), lanes] += piece
        else:
            ref[pl.ds(st, length), lanes] = piece


def _tile_position(idx, dil):
    pieces, length = PHASES // dil, 8 * dil
    return pieces * jnp.bitwise_and(idx, length - 1) + jnp.right_shift(idx, length.bit_length() - 1)


def _band_mask(i, dil):
    row = lax.broadcasted_iota(jnp.int32, (BLOCK, 2 * BLOCK), 0)
    col = lax.broadcasted_iota(jnp.int32, (BLOCK, 2 * BLOCK), 1)
    key_pos = _tile_position(jnp.bitwise_and(col, BLOCK - 1), dil) + jnp.where(col >= BLOCK, 0, -BLOCK)
    dist = _tile_position(row, dil) - key_pos
    return (dist >= 0) & (dist <= BLOCK) & ((col >= BLOCK) | (i > 0))


def _causal_mask():
    row = lax.broadcasted_iota(jnp.int32, (BLOCK, BLOCK), 0)
    col = lax.broadcasted_iota(jnp.int32, (BLOCK, BLOCK), 1)
    return row >= col


def _pair_views(col0):
    return [pl.BlockSpec((SEQ, PAIR), lambda hp, g=g: (0, col0 // PAIR + g * (ATTN_WIDTH // PAIR) + hp))
            for g in range(len(DILATIONS))]


def _rotate(in_refs, out_refs, cf_ref, ss_ref, scale):
    def step(t, carry):
        rows = pl.ds(pl.multiple_of(t * ROPE_ROWS, ROPE_ROWS), ROPE_ROWS)
        cf, ss = cf_ref[rows, :] * scale, ss_ref[rows, :] * scale
        for i_ref, o_ref in zip(in_refs, out_refs):
            o_ref[rows, :] = _rope(i_ref[rows, :], cf, ss)
        return carry

    lax.fori_loop(0, SEQ // ROPE_ROWS, step, 0)


def _attention_fwd(proj, cos_f, sin_s):
    ng = len(DILATIONS)

    def body(*refs):
        q_refs, k_refs, v_refs = refs[:ng], refs[ng:2 * ng], refs[2 * ng:3 * ng]
        cf_ref, ss_ref, attn_ref, lse_ref = refs[3 * ng:3 * ng + 4]
        scratch = refs[3 * ng + 4:]
        qr_refs, kr_refs = scratch[:ng], scratch[ng:]
        _rotate(q_refs, qr_refs, cf_ref, ss_ref, 1.0 / math.sqrt(HEAD_DIM))
        _rotate(k_refs, kr_refs, cf_ref, ss_ref, 1.0)
        first = lax.broadcasted_iota(jnp.int32, (BLOCK, PAIR), 1) < HEAD_DIM
        for g, dil in enumerate(DILATIONS):
            two_blocks = SEQ // dil > BLOCK

            def unit(u, carry, g=g, dil=dil, two_blocks=two_blocks):
                i, rows, prev = _unit_pieces(u, dil)
                qq = _load_tile(qr_refs[g], rows, dil).astype(BF16)
                kk = _load_tile(kr_refs[g], rows, dil)
                vv = _load_tile(v_refs[g], rows, dil)
                if two_blocks:
                    kk = jnp.concatenate([_load_tile(kr_refs[g], prev, dil), kk], axis=0)
                    vv = jnp.concatenate([_load_tile(v_refs[g], prev, dil), vv], axis=0)
                    valid = _band_mask(i, dil)
                else:
                    valid = _causal_mask()
                kk, vv = kk.astype(BF16), vv.astype(BF16)
                zero = jnp.zeros_like(qq)
                outs, lses = [], []
                for qh in (jnp.where(first, qq, zero), jnp.where(first, zero, qq)):
                    s = jnp.where(valid, _dot_nt(qh, kk), NEG_INF)
                    m = jnp.max(s, axis=1, keepdims=True)
                    p = jnp.exp(s - m)
                    l = jnp.sum(p, axis=1, keepdims=True)
                    outs.append(jnp.dot(p.astype(BF16), vv, preferred_element_type=F32) * (1.0 / l))
                    lses.append(m + jnp.log(l))
                o = jnp.where(first, outs[0], outs[1])
                lse = jnp.where(first, lses[0], lses[1])
                if g > 0:
                    lse_old = _load_tile(lse_ref, rows, dil)
                    m = jnp.maximum(lse_old, lse)
                    lse_new = m + jnp.log(jnp.exp(lse_old - m) + jnp.exp(lse - m))
                    o = _load_tile(attn_ref, rows, dil) * jnp.exp(lse_old - lse_new) + o * jnp.exp(lse - lse_new)
                    lse = lse_new
                _store_tile(attn_ref, rows, dil, o)
                _store_tile(lse_ref, rows, dil, lse)
                return carry

            lax.fori_loop(0, UNITS, unit, 0, unroll=UNIT_UNROLL)

    whole = pl.BlockSpec((SEQ, PAIR), lambda hp: (0, 0))
    out = pl.BlockSpec((SEQ, PAIR), lambda hp: (0, hp))
    return pl.pallas_call(
        body, grid=(ATTN_WIDTH // PAIR,),
        in_specs=_pair_views(0) + _pair_views(QKV_WIDTH) + _pair_views(2 * QKV_WIDTH) + [whole, whole],
        out_specs=[out, out], out_shape=[_sds((SEQ, ATTN_WIDTH)), _sds((SEQ, ATTN_WIDTH))],
        scratch_shapes=[pltpu.VMEM((SEQ, PAIR), F32)] * (2 * ng),
        compiler_params=_cp(("parallel",)), name="attention_fwd")(*([proj] * (3 * ng)), cos_f, sin_s)


def _attention_bwd(g, proj, cos_f, sin_s, d_attn, attn, lse):
    dil = DILATIONS[g]
    two_blocks = SEQ // dil > BLOCK

    def body(q_ref, k_ref, v_ref, cf_ref, ss_ref, do_ref, o_ref, lse_ref, dq_out, dk_out, dv_out,
             qr_ref, kr_ref, dq_acc, dk_acc, dv_acc):
        _rotate([q_ref], [qr_ref], cf_ref, ss_ref, 1.0 / math.sqrt(HEAD_DIM))
        _rotate([k_ref], [kr_ref], cf_ref, ss_ref, 1.0)
        dk_acc[...] = jnp.zeros_like(dk_acc)
        dv_acc[...] = jnp.zeros_like(dv_acc)
        lane = lax.broadcasted_iota(jnp.int32, (BLOCK, PAIR), 1)
        nk = 2 * BLOCK if two_blocks else BLOCK

        for head in range(2):
            mine = (lane < HEAD_DIM) if head == 0 else (lane >= HEAD_DIM)

            def load(u, head=head, mine=mine):
                i, rows, prev = _unit_pieces(u, dil)
                qq = _load_tile(qr_ref, rows, dil).astype(BF16)
                kk = _load_tile(kr_ref, rows, dil)
                vv = _load_tile(v_ref, rows, dil)
                if two_blocks:
                    kk = jnp.concatenate([_load_tile(kr_ref, prev, dil), kk], axis=0)
                    vv = jnp.concatenate([_load_tile(v_ref, prev, dil), vv], axis=0)
                    valid = _band_mask(i, dil)
                else:
                    valid = _causal_mask()
                dof = _load_tile(do_ref, rows, dil)
                delta = jnp.sum(jnp.where(mine, dof * _load_tile(o_ref, rows, dil), 0.0), axis=1, keepdims=True)
                lse_h = _load_tile(lse_ref, rows, dil)[:, head * HEAD_DIM:head * HEAD_DIM + 1]
                return (rows, prev), (qq, kk.astype(BF16), vv.astype(BF16), valid, dof.astype(BF16), delta, lse_h)

            def compute(qq, kk, vv, valid, dob, delta, lse_h, mine=mine):
                s = _dot_nt(jnp.where(mine, qq, jnp.zeros_like(qq)), kk)
                p = jnp.where(valid, jnp.exp(s - lse_h), 0.0)
                dp = _dot_nt(jnp.where(mine, dob, jnp.zeros_like(dob)), vv)
                ds = (p * (dp - delta)).astype(BF16)
                dq = jnp.dot(ds, kk, preferred_element_type=F32)
                dk = lax.dot_general(ds, qq, (((0,), (0,)), ((), ())), preferred_element_type=F32)
                dv = lax.dot_general(p.astype(BF16), dob, (((0,), (0,)), ((), ())), preferred_element_type=F32)
                return dq, dk, dv

            def units(t, carry, head=head, load=load, compute=compute):
                loaded = [load(t * UNIT_UNROLL + j) for j in range(UNIT_UNROLL)]
                done = [compute(*item[1]) for item in loaded]
                for ((rows, prev), _), (dq, dk, dv) in zip(loaded, done):
                    _store_tile(dq_acc, rows, dil, dq, head)
                    _store_tile(dk_acc, rows, dil, dk[nk - BLOCK:], head, accumulate=True)
                    _store_tile(dv_acc, rows, dil, dv[nk - BLOCK:], head, accumulate=True)
                    if two_blocks:
                        _store_tile(dk_acc, prev, dil, dk[:BLOCK], head, accumulate=True)
                        _store_tile(dv_acc, prev, dil, dv[:BLOCK], head, accumulate=True)
                return carry

            lax.fori_loop(0, UNITS // UNIT_UNROLL, units, 0)

        def finish(t, carry):
            rows = pl.ds(pl.multiple_of(t * ROPE_ROWS, ROPE_ROWS), ROPE_ROWS)
            cf, ss = cf_ref[rows, :], ss_ref[rows, :]
            dq = dq_acc[rows, :] * (1.0 / math.sqrt(HEAD_DIM))
            dq_out[rows, :] = _rope_transposed(dq, cf, ss).astype(BF16)
            dk_out[rows, :] = _rope_transposed(dk_acc[rows, :], cf, ss).astype(BF16)
            dv_out[rows, :] = dv_acc[rows, :].astype(BF16)
            return carry

        lax.fori_loop(0, SEQ // ROPE_ROWS, finish, 0)

    whole = pl.BlockSpec((SEQ, PAIR), lambda hp: (0, 0))
    pair = pl.BlockSpec((SEQ, PAIR), lambda hp: (0, hp))
    views = [_pair_views(col0)[g] for col0 in (0, QKV_WIDTH, 2 * QKV_WIDTH)]
    return pl.pallas_call(
        body, grid=(ATTN_WIDTH // PAIR,), in_specs=views + [whole, whole, pair, pair, pair],
        out_specs=[pair, pair, pair], out_shape=[_sds((SEQ, ATTN_WIDTH), BF16)] * 3,
        scratch_shapes=[pltpu.VMEM((SEQ, PAIR), F32)] * 5,
        compiler_params=_cp(("parallel",)), name=f"attention_bwd_{g}")(proj, proj, proj, cos_f, sin_s, d_attn, attn, lse)


def _cmul(ar, ai, br, bi):
    return ar * br - ai * bi, ar * bi + ai * br


def _pow256(ar, ai):
    for _ in range(8):
        ar, ai = _cmul(ar, ai, ar, ai)
    return ar, ai


def _chunk_carries(first_r, first_i, pr, pi, reverse):
    rows = lax.broadcasted_iota(jnp.int32, first_r.shape, 0)
    out_r = jnp.zeros_like(first_r)
    out_i = jnp.zeros_like(first_i)
    hr = jnp.zeros_like(first_r[0:1])
    hi = jnp.zeros_like(hr)
    order = range(SCAN_CHUNKS - 1, -1, -1) if reverse else range(SCAN_CHUNKS)
    for c in order:
        out_r = jnp.where(rows == c, hr, out_r)
        out_i = jnp.where(rows == c, hi, out_i)
        tr, ti = _cmul(pr[0:1], pi[0:1], hr, hi)
        hr = first_r[c:c + 1] + tr
        hi = first_i[c:c + 1] + ti
    return out_r, out_i


def _tile(j):
    return pl.ds(pl.multiple_of(j * SCAN_CHUNKS, SCAN_CHUNKS), SCAN_CHUNKS)


def _to_scan_rows(t):
    per = SCAN_STEPS // PHASES
    return t.reshape(PHASES, SCAN_CHUNKS, per, t.shape[1]).transpose(2, 0, 1, 3).reshape(t.shape)


def _from_scan_rows(t):
    per = SCAN_STEPS // PHASES
    return t.reshape(per, PHASES, SCAN_CHUNKS, t.shape[1]).transpose(1, 2, 0, 3).reshape(t.shape)


def _scan_fwd(bur, bui, ar, ai):
    lb = SCAN_LANES

    def body(bur_ref, bui_ref, ar_ref, ai_ref, hr_ref, hi_ref, er_ref, ei_ref):
        a_r = jnp.broadcast_to(ar_ref[...], (SCAN_CHUNKS, lb))
        a_i = jnp.broadcast_to(ai_ref[...], (SCAN_CHUNKS, lb))

        def local(j, carry):
            tr, ti = _cmul(a_r, a_i, carry[0], carry[1])
            nr = tr + bur_ref[_tile(j), :]
            ni = ti + bui_ref[_tile(j), :]
            hr_ref[_tile(j), :] = nr
            hi_ref[_tile(j), :] = ni
            return nr, ni

        zero = jnp.zeros((SCAN_CHUNKS, lb), F32)
        last_r, last_i = lax.fori_loop(0, SCAN_STEPS, local, (zero, zero), unroll=4)
        pr, pi = _pow256(a_r, a_i)
        er, ei = _chunk_carries(last_r, last_i, pr, pi, reverse=False)
        er_ref[...] = er
        ei_ref[...] = ei

        def fix(j, carry):
            tr, ti = _cmul(carry[0], carry[1], er, ei)
            hr_ref[_tile(j), :] += tr
            hi_ref[_tile(j), :] += ti
            return _cmul(carry[0], carry[1], a_r, a_i)

        lax.fori_loop(0, SCAN_STEPS, fix, (a_r, a_i), unroll=4)

    big = pl.BlockSpec((SEQ, lb), lambda j: (0, j))
    vec = pl.BlockSpec((1, lb), lambda j: (0, j))
    ent = pl.BlockSpec((SCAN_CHUNKS, lb), lambda j: (0, j))
    return pl.pallas_call(
        body, grid=(SSM_LANES // lb,), in_specs=[big, big, vec, vec], out_specs=[big, big, ent, ent],
        out_shape=[_sds((SEQ, SSM_LANES)), _sds((SEQ, SSM_LANES)), _sds((SCAN_CHUNKS, SSM_LANES)),
                   _sds((SCAN_CHUNKS, SSM_LANES))],
        compiler_params=_cp(("parallel",)), name="ssm_scan_fwd")(bur, bui, ar, ai)


def _scan_bwd(gr, gi, hr, hi, er, ei, ar, ai):
    lb = SCAN_LANES

    def body(gr_ref, gi_ref, hr_ref, hi_ref, er_ref, ei_ref, ar_ref, ai_ref, lr_ref, li_ref, dar_ref, dai_ref):
        a_r = jnp.broadcast_to(ar_ref[...], (SCAN_CHUNKS, lb))
        a_i = -jnp.broadcast_to(ai_ref[...], (SCAN_CHUNKS, lb))

        def local(t, carry):
            j = SCAN_STEPS - 1 - t
            tr, ti = _cmul(a_r, a_i, carry[0], carry[1])
            nr = tr + gr_ref[_tile(j), :]
            ni = ti + gi_ref[_tile(j), :]
            lr_ref[_tile(j), :] = nr
            li_ref[_tile(j), :] = ni
            return nr, ni

        zero = jnp.zeros((SCAN_CHUNKS, lb), F32)
        first_r, first_i = lax.fori_loop(0, SCAN_STEPS, local, (zero, zero), unroll=4)
        pr, pi = _pow256(a_r, a_i)
        nxt_r, nxt_i = _chunk_carries(first_r, first_i, pr, pi, reverse=True)

        def accumulate(lam_r, lam_i, hp_r, hp_i, acc):
            return (acc[0] + lam_r * hp_r + lam_i * hp_i, acc[1] + lam_i * hp_r - lam_r * hp_i)

        def fix(t, carry):
            qr, qi, acc_r, acc_i = carry
            j = SCAN_STEPS - 1 - t
            tr, ti = _cmul(qr, qi, nxt_r, nxt_i)
            lam_r = lr_ref[_tile(j), :] + tr
            lam_i = li_ref[_tile(j), :] + ti
            lr_ref[_tile(j), :] = lam_r
            li_ref[_tile(j), :] = lam_i
            acc_r, acc_i = accumulate(lam_r, lam_i, hr_ref[_tile(j - 1), :], hi_ref[_tile(j - 1), :], (acc_r, acc_i))
            qr, qi = _cmul(qr, qi, a_r, a_i)
            return qr, qi, acc_r, acc_i

        qr, qi, acc_r, acc_i = lax.fori_loop(0, SCAN_STEPS - 1, fix, (a_r, a_i, zero, zero), unroll=4)
        tr, ti = _cmul(qr, qi, nxt_r, nxt_i)
        lam_r = lr_ref[_tile(0), :] + tr
        lam_i = li_ref[_tile(0), :] + ti
        lr_ref[_tile(0), :] = lam_r
        li_ref[_tile(0), :] = lam_i
        acc_r, acc_i = accumulate(lam_r, lam_i, er_ref[...], ei_ref[...], (acc_r, acc_i))
        dar_ref[...] = jnp.sum(acc_r, axis=0, keepdims=True)
        dai_ref[...] = jnp.sum(acc_i, axis=0, keepdims=True)

    big = pl.BlockSpec((SEQ, lb), lambda j: (0, j))
    vec = pl.BlockSpec((1, lb), lambda j: (0, j))
    ent = pl.BlockSpec((SCAN_CHUNKS, lb), lambda j: (0, j))
    return pl.pallas_call(
        body, grid=(SSM_LANES // lb,), in_specs=[big, big, big, big, ent, ent, vec, vec],
        out_specs=[big, big, vec, vec],
        out_shape=[_sds((SEQ, SSM_LANES)), _sds((SEQ, SSM_LANES)), _sds((1, SSM_LANES)), _sds((1, SSM_LANES))],
        compiler_params=_cp(("parallel",)), name="ssm_scan_bwd")(gr, gi, hr, hi, er, ei, ar, ai)


def _rope_tables():
    half = HEAD_DIM // 2
    inv_freq = ROPE_THETA ** (-jnp.arange(half, dtype=F32) / half)
    ang = jnp.arange(SEQ, dtype=F32)[:, None] * inv_freq[None, :]
    cos, sin = jnp.cos(ang), jnp.sin(ang)
    cos_f = jnp.concatenate([cos, cos, cos, cos], axis=1)
    sin_s = jnp.concatenate([-sin, sin, -sin, sin], axis=1)
    return cos_f, sin_s


def _ssm_discretise(a_re, a_im, log_dt, b_re, b_im):
    lam = lax.complex(a_re, a_im)
    dt = jnp.exp(log_dt)[:, None]
    a_bar = jnp.exp(lam * dt)
    b_bar = ((a_bar - 1.0) / lam)[..., None] * lax.complex(b_re, b_im)
    return a_bar.real, a_bar.imag, b_bar.real, b_bar.imag


def _block_diag_in(b):
    eye = jnp.eye(SSM_GROUPS, dtype=b.dtype)
    return (eye[:, None, :, None] * b.transpose(0, 2, 1)[:, :, None, :]).reshape(SSM_WIDTH, SSM_LANES)


def _block_diag_out(c):
    eye = jnp.eye(SSM_GROUPS, dtype=c.dtype)
    return (eye[:, None, :, None] * c.transpose(0, 2, 1)[:, :, None, :]).reshape(SSM_LANES, SSM_WIDTH)


SSM_SLABS = 4
SLAB_GROUPS = SSM_GROUPS // SSM_SLABS
SLAB_IN = SSM_WIDTH // SSM_SLABS
SLAB_STATE = SSM_LANES // SSM_SLABS


def _ssm_in(u, b_in, *, name, transpose=False):
    if transpose:
        return _matmul(u, b_in, grid=(SSM_SLABS,), a_spec=pl.BlockSpec((SEQ, SLAB_STATE), lambda j: (0, j)),
                       b_spec=pl.BlockSpec((SLAB_IN, SLAB_STATE), lambda j: (j, j)),
                       o_spec=pl.BlockSpec((SEQ, SLAB_IN), lambda j: (0, j)), out_shape=_sds((SEQ, SSM_WIDTH)),
                       dims=(1, 1), name=name)
    return _matmul(u, b_in, grid=(SSM_SLABS,), a_spec=pl.BlockSpec((SEQ, SLAB_IN), lambda j: (0, j)),
                   b_spec=pl.BlockSpec((SLAB_IN, SLAB_STATE), lambda j: (j, j)),
                   o_spec=pl.BlockSpec((SEQ, SLAB_STATE), lambda j: (0, j)), out_shape=_sds((SEQ, SSM_LANES)),
                   dims=(1, 0), name=name)


def _ssm_out(h, c_out, *, name, transpose=False):
    if transpose:
        return _matmul(h, c_out, grid=(SSM_SLABS,), a_spec=pl.BlockSpec((SEQ, SLAB_IN), lambda j: (0, j)),
                       b_spec=pl.BlockSpec((SLAB_STATE, SLAB_IN), lambda j: (j, j)),
                       o_spec=pl.BlockSpec((SEQ, SLAB_STATE), lambda j: (0, j)), out_shape=_sds((SEQ, SSM_LANES)),
                       dims=(1, 1), name=name)
    return _matmul(h, c_out, grid=(SSM_SLABS,), a_spec=pl.BlockSpec((SEQ, SLAB_STATE), lambda j: (0, j)),
                   b_spec=pl.BlockSpec((SLAB_STATE, SLAB_IN), lambda j: (j, j)),
                   o_spec=pl.BlockSpec((SEQ, SLAB_IN), lambda j: (0, j)), out_shape=_sds((SEQ, SSM_WIDTH)),
                   dims=(1, 0), name=name)


def _diag_block_grad(a, b, *, name):
    ra = a.shape[1] // SSM_SLABS
    cb = b.shape[1] // SSM_SLABS
    wa, wb = ra // SLAB_GROUPS, cb // SLAB_GROUPS

    def body(a_ref, b_ref, o_ref):
        d = lax.dot_general(a_ref[...].astype(BF16), b_ref[...].astype(BF16), (((0,), (0,)), ((), ())),
                            preferred_element_type=F32)
        row_g = jnp.right_shift(lax.broadcasted_iota(jnp.int32, (ra, cb), 0), wa.bit_length() - 1)
        col_g = jnp.right_shift(lax.broadcasted_iota(jnp.int32, (ra, cb), 1), wb.bit_length() - 1)
        d = jnp.where(row_g == col_g, d, 0.0)
        fold = (jnp.bitwise_and(lax.broadcasted_iota(jnp.int32, (cb, wb), 0), wb - 1)
                == lax.broadcasted_iota(jnp.int32, (cb, wb), 1)).astype(F32)
        o_ref[...] = jnp.dot(d, fold, preferred_element_type=F32, precision=lax.Precision.HIGHEST)

    return pl.pallas_call(
        body, grid=(SSM_SLABS,), in_specs=[pl.BlockSpec((SEQ, ra), lambda j: (0, j)),
                                           pl.BlockSpec((SEQ, cb), lambda j: (0, j))],
        out_specs=pl.BlockSpec((ra, wb), lambda j: (j, 0)), out_shape=_sds((a.shape[1], wb)),
        compiler_params=_cp(("parallel",)), name=name)(a, b)


FF_ROWS = 1024
FF_SHARD = D_FF // N_CHIPS


def _dot_nt(a, b):
    return lax.dot_general(a, b, (((1,), (1,)), ((), ())), preferred_element_type=F32)


def _ffn_up(h, w_gate_t, w_up_t):
    def body(h_ref, wg_ref, wu_ref, a_ref, b_ref, act_ref):
        hb = h_ref[...].astype(BF16)
        a = _dot_nt(hb, wg_ref[...])
        b = _dot_nt(hb, wu_ref[...])
        a_ref[...] = a
        b_ref[...] = b
        act_ref[...] = (a * jax.nn.sigmoid(a) * b).astype(BF16)

    w_spec = pl.BlockSpec((None, FF_SHARD, D_MODEL), lambda i, k: (k, 0, 0))
    o_spec = pl.BlockSpec((None, FF_ROWS, FF_SHARD), lambda i, k: (k, i, 0))
    shape = (N_CHIPS, SEQ, FF_SHARD)
    return pl.pallas_call(
        body, grid=(SEQ // FF_ROWS, N_CHIPS),
        in_specs=[pl.BlockSpec((FF_ROWS, D_MODEL), lambda i, k: (i, 0)), w_spec, w_spec],
        out_specs=[o_spec, o_spec, o_spec], out_shape=[_sds(shape), _sds(shape), _sds(shape, BF16)],
        compiler_params=_cp(("parallel", "parallel")), name="ffn_up")(h, w_gate_t, w_up_t)


def _ffn_down_bwd(dz, w_down, a, b):
    def body(dz_ref, wd_ref, a_ref, b_ref, da_ref, db_ref):
        d_act = _dot_nt(dz_ref[...].astype(BF16), wd_ref[...])
        av = a_ref[...]
        sg = jax.nn.sigmoid(av)
        da_ref[...] = (d_act * b_ref[...] * sg * (1.0 + av * (1.0 - sg))).astype(BF16)
        db_ref[...] = (d_act * av * sg).astype(BF16)

    t_spec = pl.BlockSpec((None, FF_ROWS, FF_SHARD), lambda i, k: (k, i, 0))
    shape = (N_CHIPS, SEQ, FF_SHARD)
    return pl.pallas_call(
        body, grid=(SEQ // FF_ROWS, N_CHIPS),
        in_specs=[pl.BlockSpec((FF_ROWS, D_MODEL), lambda i, k: (i, 0)),
                  pl.BlockSpec((None, FF_SHARD, D_MODEL), lambda i, k: (k, 0, 0)), t_spec, t_spec],
        out_specs=[t_spec, t_spec], out_shape=[_sds(shape, BF16), _sds(shape, BF16)],
        compiler_params=_cp(("parallel", "parallel")), name="ffn_down_bwd")(dz, w_down, a, b)


def _ffn_dh(d_a, d_b, w_gate_t, w_up_t):
    def body(da_ref, db_ref, wg_ref, wu_ref, o_ref, acc):
        k = pl.program_id(1)
        part = (jnp.dot(da_ref[...], wg_ref[...], preferred_element_type=F32)
                + jnp.dot(db_ref[...], wu_ref[...], preferred_element_type=F32))

        @pl.when(k == 0)
        def _():
            acc[...] = part

        @pl.when(k > 0)
        def _():
            acc[...] += part

        @pl.when(k == N_CHIPS - 1)
        def _():
            o_ref[...] = acc[...]

    t_spec = pl.BlockSpec((None, FF_ROWS, FF_SHARD), lambda i, k: (k, i, 0))
    w_spec = pl.BlockSpec((None, FF_SHARD, D_MODEL), lambda i, k: (k, 0, 0))
    return pl.pallas_call(
        body, grid=(SEQ // FF_ROWS, N_CHIPS), in_specs=[t_spec, t_spec, w_spec, w_spec],
        out_specs=pl.BlockSpec((FF_ROWS, D_MODEL), lambda i, k: (i, 0)), out_shape=_sds((SEQ, D_MODEL)),
        scratch_shapes=[pltpu.VMEM((FF_ROWS, D_MODEL), F32)],
        compiler_params=_cp(("parallel", "arbitrary")), name="ffn_dh")(d_a, d_b, w_gate_t, w_up_t)


def _local_step(x, tgt, wts, small):
    s = SEQ
    cos_f, sin_s = [_to_phase_rows(t) for t in _rope_tables()]
    x = _reorder_rows(x, to_phase=True, name="phase_rows_x")
    tgt = _reorder_rows(tgt, to_phase=True, name="phase_rows_target")

    proj = _mm_cols(x, wts["w_in"], tm=1024, name="proj")

    attn, lse = _attention_fwd(proj, cos_f, sin_s)
    y_attn = _mm_cols(attn, wts["w_attn_br"], tm=s, name="y_attn")

    (abar_r, abar_i, bbar_r, bbar_i), ssm_vjp = jax.vjp(
        _ssm_discretise, small["ssm_a_re"], small["ssm_a_im"], small["ssm_log_dt"], small["ssm_b_re"], small["ssm_b_im"])
    b_in_r, b_in_i = _block_diag_in(bbar_r).astype(BF16), _block_diag_in(bbar_i).astype(BF16)
    c_out_r = _block_diag_out(small["ssm_c_re"]).astype(BF16)
    c_out_ni = _block_diag_out(-small["ssm_c_im"]).astype(BF16)
    a_r, a_i = abar_r.reshape(1, SSM_LANES), abar_i.reshape(1, SSM_LANES)
    d_skip = small["ssm_d"].reshape(1, SSM_WIDTH)

    u_f = _to_scan_rows(proj[:, 3 * QKV_WIDTH:3 * QKV_WIDTH + SSM_WIDTH])
    u_p = u_f.astype(BF16)
    bu_r = _ssm_in(u_p, b_in_r, name="ssm_bu_re")
    bu_i = _ssm_in(u_p, b_in_i, name="ssm_bu_im")
    h_r, h_i, e_r, e_i = _scan_fwd(bu_r, bu_i, a_r, a_i)
    y_1 = _ssm_out(h_r, c_out_r, name="ssm_y_re")
    y_2 = _ssm_out(h_i, c_out_ni, name="ssm_y_im")

    def gelu_fwd(y1, y2, u, dsk):
        y = y1 + y2 + dsk * u
        return y, 0.5 * y * (1.0 + jnp.tanh(GELU_C * (y + GELU_K * y * y * y)))

    y_s5, gel = _rowwise(gelu_fwd, [y_1, y_2, u_f], [d_skip], [_sds((s, SSM_WIDTH)), _sds((s, SSM_WIDTH), BF16)],
                         tm=512, name="ssm_gelu")
    glu = _mm_cols(gel, wts["w_glu"], tm=s, name="glu")

    def glu_fwd(ga, gb):
        return ga * jax.nn.sigmoid(gb)

    (y_glu,) = _rowwise(glu_fwd, [(glu, SSM_WIDTH, 0), (glu, SSM_WIDTH, 1)], [], [_sds((s, SSM_WIDTH), BF16)],
                        tm=512, name="glu_gate")
    y_glu = _from_scan_rows(y_glu)
    y_ssm = _mm_cols(y_glu, wts["w_ssm_br"], tm=s, name="y_ssm")

    gl0 = (proj, D_MODEL, (3 * QKV_WIDTH + SSM_WIDTH) // D_MODEL)
    gl1 = (proj, D_MODEL, (3 * QKV_WIDTH + SSM_WIDTH) // D_MODEL + 1)
    b_gate = small["b_gate"]

    def gate_mix(l0, l1, ya, ys, bg):
        return jax.nn.sigmoid(l0 + bg[0:1]) * ya + jax.nn.sigmoid(l1 + bg[1:2]) * ys

    (mixed,) = _rowwise(gate_mix, [gl0, gl1, y_attn, y_ssm], [b_gate], [_sds((s, D_MODEL), BF16)], tm=256,
                        name="gate_mix")
    w_out = wts["w_out"].reshape(D_MODEL, D_MODEL)
    mix_out = _mm_plain(mixed, w_out, tm=1024, tn=512, name="mix_out")

    def ln1_fwd(xv, mo, g, b):
        z = DN_ALPHA * xv + mo
        xhat, _ = _ln_stats(z)
        return z, xhat * g + b

    z1, h = _rowwise(ln1_fwd, [x, mix_out], [small["ln1_g"], small["ln1_b"]],
                     [_sds((s, D_MODEL)), _sds((s, D_MODEL))], tm=256, name="ln1")

    nf = D_FF // N_CHIPS
    w_gate_t, w_up_t, w_down = wts["w_ff_gate"], wts["w_ff_up"], wts["w_ff_down"]
    ff_a, ff_b, act = _ffn_up(h, w_gate_t, w_up_t)
    ff = _matmul(act, w_down, grid=(2, N_CHIPS),
                 a_spec=pl.BlockSpec((None, 1024, nf), lambda i, k: (k, i, 0)),
                 b_spec=pl.BlockSpec((None, nf, D_MODEL), lambda i, k: (k, 0, 0)),
                 o_spec=pl.BlockSpec((1024, D_MODEL), lambda i, k: (i, 0)),
                 out_shape=_sds((s, D_MODEL)), dims=(1, 0), k_axis=1, name="ff_down")

    def ln2_loss(hv, ffv, tg, g, b):
        z = DN_ALPHA * hv + ffv
        xhat, rstd = _ln_stats(z)
        err = xhat * g + b - tg
        d_out = err * (1.0 / D_MODEL)
        loss_rows = jnp.sum(err * err, axis=-1, keepdims=True) * (0.5 / D_MODEL)
        loss = jnp.broadcast_to(jnp.sum(loss_rows, axis=0, keepdims=True), (1, 128))
        return _ln_bwd(d_out, xhat, rstd, g), loss, _colsum(d_out * xhat), _colsum(d_out)

    dz2, loss_v, d_ln2_g, d_ln2_b = _rowwise(
        ln2_loss, [h, ff, tgt], [small["ln2_g"], small["ln2_b"]], [_sds((s, D_MODEL))],
        [_sds((1, 128)), _sds((1, D_MODEL)), _sds((1, D_MODEL))], tm=256, name="ln2_loss")

    d_a, d_b = _ffn_down_bwd(dz2, w_down, ff_a, ff_b)

    def grad_rows(lhs, rhs, name):
        return _matmul(lhs, rhs, grid=(N_CHIPS,), a_spec=pl.BlockSpec((None, s, nf), lambda k: (k, 0, 0)),
                       b_spec=pl.BlockSpec((s, D_MODEL), lambda k: (0, 0)),
                       o_spec=pl.BlockSpec((None, nf, D_MODEL), lambda k: (k, 0, 0)),
                       out_shape=_sds((N_CHIPS, nf, D_MODEL), BF16), dims=(0, 0), name=name)

    g_w_ff_down = grad_rows(act, dz2, "g_w_ff_down")
    g_w_ff_gate = grad_rows(d_a, h, "g_w_ff_gate")
    g_w_ff_up = grad_rows(d_b, h, "g_w_ff_up")
    dh_ff = _ffn_dh(d_a, d_b, w_gate_t, w_up_t)

    def ln1_bwd(dz, dff, z, g):
        xhat, rstd = _ln_stats(z)
        dh = DN_ALPHA * dz + dff
        return _ln_bwd(dh, xhat, rstd, g), _colsum(dh * xhat), _colsum(dh)

    dz1, d_ln1_g, d_ln1_b = _rowwise(ln1_bwd, [dz2, dh_ff, z1], [small["ln1_g"]], [_sds((s, D_MODEL))],
                                     [_sds((1, D_MODEL)), _sds((1, D_MODEL))], tm=256, name="ln1_bwd")
    d_mixed = _mm_plain(dz1, w_out, tm=1024, tn=512, dims=(1, 1), name="d_mixed")
    g_w_out = _mm_plain(mixed, dz1, tm=D_MODEL, tn=512, dims=(0, 0), out_dtype=BF16, name="g_w_out")
    g_w_out = g_w_out.reshape(N_CHIPS, D_MODEL // N_CHIPS, D_MODEL)

    def gate_bwd(dm, l0, l1, ya, ys, bg):
        g0 = jax.nn.sigmoid(l0 + bg[0:1])
        g1 = jax.nn.sigmoid(l1 + bg[1:2])
        dl0 = dm * ya * g0 * (1.0 - g0)
        dl1 = dm * ys * g1 * (1.0 - g1)
        return dm * g0, dm * g1, jnp.concatenate([dl0, dl1], axis=1), _colsum(dl0), _colsum(dl1)

    d_y_attn, d_y_ssm, d_gl, d_bg0, d_bg1 = _rowwise(
        gate_bwd, [d_mixed, gl0, gl1, y_attn, y_ssm], [b_gate],
        [_sds((s, D_MODEL), BF16), _sds((s, D_MODEL), BF16), _sds((s, 2 * D_MODEL), BF16)],
        [_sds((1, D_MODEL)), _sds((1, D_MODEL))], tm=256, name="gate_bwd")

    g_w_ssm_br = _mm_cols_tn(y_glu, d_y_ssm, ns=D_MODEL // N_CHIPS, name="g_w_ssm_br")
    d_y_glu = _to_scan_rows(_mm_cols_nt(d_y_ssm, wts["w_ssm_br"], tm=s, name="d_y_glu"))

    def glu_bwd(dy, ga, gb):
        sg = jax.nn.sigmoid(gb)
        return jnp.concatenate([dy * sg, dy * ga * sg * (1.0 - sg)], axis=1)

    (d_glu,) = _rowwise(glu_bwd, [d_y_glu, (glu, SSM_WIDTH, 0), (glu, SSM_WIDTH, 1)], [],
                        [_sds((s, 2 * SSM_WIDTH), BF16)], tm=512, name="glu_bwd")
    g_w_glu = _mm_cols_tn(gel, d_glu, ns=2 * SSM_WIDTH // N_CHIPS, name="g_w_glu")
    d_gel = _mm_cols_nt(d_glu, wts["w_glu"], tm=s, name="d_gel")

    def gelu_bwd(dg, y, u, dsk):
        th = jnp.tanh(GELU_C * (y + GELU_K * y * y * y))
        dy = dg * (0.5 * (1.0 + th) + 0.5 * y * (1.0 - th * th) * GELU_C * (1.0 + 3.0 * GELU_K * y * y))
        return dy, dy * dsk, _colsum(dy * u)

    d_y, d_u_skip, d_ssm_d = _rowwise(gelu_bwd, [d_gel, y_s5, u_f], [d_skip],
                                      [_sds((s, SSM_WIDTH), BF16), _sds((s, SSM_WIDTH))], [_sds((1, SSM_WIDTH))],
                                      tm=512, name="gelu_bwd")
    g_h_r = _ssm_out(d_y, c_out_r, name="ssm_gh_re", transpose=True)
    g_h_i = _ssm_out(d_y, c_out_ni, name="ssm_gh_im", transpose=True)
    d_c_r = _diag_block_grad(h_r, d_y, name="ssm_dc_re")
    d_c_ni = _diag_block_grad(h_i, d_y, name="ssm_dc_im")
    lam_r, lam_i, d_abar_r, d_abar_i = _scan_bwd(g_h_r, g_h_i, h_r, h_i, e_r, e_i, a_r, a_i)
    d_bin_r = _diag_block_grad(u_p, lam_r, name="ssm_db_re")
    d_bin_i = _diag_block_grad(u_p, lam_i, name="ssm_db_im")
    d_u_r = _ssm_in(lam_r, b_in_r, name="ssm_du_re", transpose=True)
    d_u_i = _ssm_in(lam_i, b_in_i, name="ssm_du_im", transpose=True)

    def add3(a, b, c):
        return a + b + c

    (d_u,) = _rowwise(add3, [d_u_skip, d_u_r, d_u_i], [], [_sds((s, SSM_WIDTH), BF16)], tm=512, name="ssm_du")
    d_u = _from_scan_rows(d_u)
    d_bbar_r = d_bin_r.reshape(SSM_GROUPS, SSM_GROUP, SSM_STATE).transpose(0, 2, 1)
    d_bbar_i = d_bin_i.reshape(SSM_GROUPS, SSM_GROUP, SSM_STATE).transpose(0, 2, 1)
    d_a_re, d_a_im, d_log_dt, d_b_re, d_b_im = ssm_vjp(
        (d_abar_r.reshape(SSM_GROUPS, SSM_STATE), d_abar_i.reshape(SSM_GROUPS, SSM_STATE), d_bbar_r, d_bbar_i))
    d_c_re = d_c_r.reshape(SSM_GROUPS, SSM_STATE, SSM_GROUP).transpose(0, 2, 1)
    d_c_im = -d_c_ni.reshape(SSM_GROUPS, SSM_STATE, SSM_GROUP).transpose(0, 2, 1)

    g_w_attn_br = _mm_cols_tn(attn, d_y_attn, ns=D_MODEL // N_CHIPS, name="g_w_attn_br")
    d_attn = _mm_cols_nt(d_y_attn, wts["w_attn_br"], tm=s, name="d_attn")
    dqkv = [_attention_bwd(g, proj, cos_f, sin_s, d_attn, attn, lse) for g in range(len(DILATIONS))]

    d_proj = jnp.concatenate([dqkv[g][j] for j in range(3) for g in range(len(DILATIONS))] + [d_u, d_gl],
                             axis=1)
    g_w_in = _mm_cols_tn(x, d_proj, ns=IN_WIDTH // N_CHIPS, name="g_w_in")
    dx_proj = _mm_cols_nt(d_proj, wts["w_in"], tm=1024, name="dx_proj", after=(g_w_in,))

    def dx_sum(dz, dxp):
        return DN_ALPHA * dz + dxp

    (grad_x,) = _rowwise(dx_sum, [dz1, dx_proj], [], [_sds((s, D_MODEL))], tm=512, name="grad_x")
    grad_x = _reorder_rows(grad_x, to_phase=False, name="time_rows_grad_x")

    big = {"w_in": g_w_in, "w_attn_br": g_w_attn_br, "w_ssm_br": g_w_ssm_br, "w_out": g_w_out, "w_glu": g_w_glu,
           "w_ff_gate": g_w_ff_gate, "w_ff_up": g_w_ff_up, "w_ff_down": g_w_ff_down}
    small_g = {"b_gate": jnp.concatenate([d_bg0, d_bg1], axis=0), "ssm_a_re": d_a_re, "ssm_a_im": d_a_im,
               "ssm_log_dt": d_log_dt, "ssm_b_re": d_b_re, "ssm_b_im": d_b_im, "ssm_c_re": d_c_re, "ssm_c_im": d_c_im,
               "ssm_d": d_ssm_d.reshape(SSM_WIDTH), "ln1_g": d_ln1_g, "ln1_b": d_ln1_b, "ln2_g": d_ln2_g,
               "ln2_b": d_ln2_b}
    marks = {"ln1_bwd": dz1, "scan_bwd": lam_r, "attention_bwd_0": dqkv[0][0], "dx_proj": dx_proj}
    return loss_v[0, 0], grad_x, big, small_g, marks


GATHER_ID, SWAP_ID, SCATTER_ID, JOIN_ID = 1, 2, 3, 4


def _place():
    return lax.axis_index("x"), lax.axis_index("y"), lax.axis_index("c")


def _other_chips(x, y):
    return [(1 - x, y), (x, 1 - y), (1 - x, 1 - y)]


def _handshake(peers):
    barrier = pltpu.get_barrier_semaphore()
    for peer in peers:
        pl.semaphore_signal(barrier, inc=1, device_id=peer, device_id_type=MESH)
    pl.semaphore_wait(barrier, len(peers))


def _sequencer(body, arrays, out_type, sems, collective_id, name):
    return pl.kernel(body, name=name, out_type=out_type,
                     mesh=plsc.ScalarSubcoreMesh(axis_name="sequencer", num_cores=1), scratch_types=sems,
                     compiler_params=pltpu.CompilerParams(collective_id=collective_id))(*arrays)


def _gather_weights(shards, *, name):
    nw = len(shards)

    def body(*refs):
        ins, outs = refs[:nw], refs[nw:2 * nw]
        send_sems, recv_sems, pass_send, pass_recv, local_sems = refs[2 * nw:]
        x, y, c = _place()
        chip = 2 * x + y
        chips = _other_chips(x, y)
        _handshake([(x, y, 1 - c)] + [(cx, cy, c) for cx, cy in chips])
        started = []
        for w in range(nw):
            hw = shards[w].shape[0] // 2
            mine = pl.ds(c * hw, hw)
            own = pltpu.make_async_copy(ins[w], outs[w].at[chip], local_sems.at[w])
            own.start()
            started.append(own)
            for j, (cx, cy) in enumerate(chips):
                cp = pltpu.make_async_remote_copy(
                    src_ref=ins[w].at[mine], dst_ref=outs[w].at[chip, mine], send_sem=send_sems.at[w, j],
                    recv_sem=recv_sems.at[w, j], device_id=(cx, cy, c), device_id_type=MESH)
                cp.start()
                started.append(cp)
        passed = []
        for w in range(nw):
            hw = shards[w].shape[0] // 2
            mine = pl.ds(c * hw, hw)
            for j, (cx, cy) in enumerate(chips):
                landed = outs[w].at[2 * cx + cy, mine]
                pltpu.make_async_remote_copy(
                    src_ref=ins[w].at[mine], dst_ref=landed, send_sem=send_sems.at[w, j],
                    recv_sem=recv_sems.at[w, j], device_id=(cx, cy, c), device_id_type=MESH).wait_recv()
                cp = pltpu.make_async_remote_copy(
                    src_ref=landed, dst_ref=landed, send_sem=pass_send.at[w, j], recv_sem=pass_recv.at[w, j],
                    device_id=(x, y, 1 - c), device_id_type=MESH)
                cp.start()
                passed.append(cp)
        for w in range(nw):
            hw = shards[w].shape[0] // 2
            theirs = pl.ds((1 - c) * hw, hw)
            for j, (cx, cy) in enumerate(chips):
                landed = outs[w].at[2 * cx + cy, theirs]
                pltpu.make_async_remote_copy(
                    src_ref=landed, dst_ref=landed, send_sem=pass_send.at[w, j], recv_sem=pass_recv.at[w, j],
                    device_id=(x, y, 1 - c), device_id_type=MESH).wait_recv()
        for cp in started[0::4]:
            cp.wait()
        for cp in [s for i, s in enumerate(started) if i % 4] + passed:
            cp.wait_send()

    sem = pltpu.SemaphoreType.DMA
    return _sequencer(body, shards, [_sds((N_CHIPS,) + a.shape, a.dtype) for a in shards],
                      [sem((nw, 3)), sem((nw, 3)), sem((nw, 3)), sem((nw, 3)), sem((nw,))], GATHER_ID, name)


def _swap_other_halves(grads, *, name):
    nw = len(grads)

    def body(*refs):
        ins, outs = refs[:nw], refs[nw:2 * nw]
        send_sems, recv_sems = refs[2 * nw:]
        x, y, c = _place()
        _handshake([(x, y, 1 - c)])
        cps = []
        for w in range(nw):
            hw = grads[w].shape[1] // 2
            cp = pltpu.make_async_remote_copy(
                src_ref=ins[w].at[:, pl.ds((1 - c) * hw, hw)], dst_ref=outs[w], send_sem=send_sems.at[w],
                recv_sem=recv_sems.at[w], device_id=(x, y, 1 - c), device_id_type=MESH)
            cp.start()
            cps.append(cp)
        for cp in cps:
            cp.wait()

    sem = pltpu.SemaphoreType.DMA
    return _sequencer(body, grads, [_sds((N_CHIPS, g.shape[1] // 2, g.shape[2]), g.dtype) for g in grads],
                      [sem((nw,)), sem((nw,))], SWAP_ID, name)


def _add_my_half(core, g, other, after=()):
    n, r, cols = g.shape
    hw = r // 2

    def body(core_ref, g_ref, o_ref, *rest):
        out_ref = rest[len(after)]
        out_ref[...] = (g_ref[...].astype(F32) + o_ref[...].astype(F32)).astype(out_ref.dtype)

    return pl.pallas_call(
        body,
        grid_spec=pltpu.PrefetchScalarGridSpec(
            num_scalar_prefetch=1, grid=(n,),
            in_specs=[pl.BlockSpec((None, None, hw, cols), lambda s, core_ref: (s, core_ref[0], 0, 0)),
                      pl.BlockSpec((None, hw, cols), lambda s, core_ref: (s, 0, 0))] + [HBM_OPERAND] * len(after),
            out_specs=pl.BlockSpec((None, hw, cols), lambda s, core_ref: (s, 0, 0))),
        out_shape=_sds((n, hw, cols), BF16), compiler_params=_cp(("parallel",)),
        name="add_my_half")(core, g.reshape(n, 2, hw, cols), other, *after)


def _scatter_partials(parts, *, name):
    nw = len(parts)

    def body(*refs):
        ins, outs = refs[:nw], refs[nw:2 * nw]
        send_sems, recv_sems = refs[2 * nw:]
        x, y, c = _place()
        _handshake([(cx, cy, c) for cx, cy in _other_chips(x, y)])
        cps = []
        for w in range(nw):
            for j, (cx, cy) in enumerate(_other_chips(x, y)):
                cp = pltpu.make_async_remote_copy(
                    src_ref=ins[w].at[2 * cx + cy], dst_ref=outs[w].at[j], send_sem=send_sems.at[w, j],
                    recv_sem=recv_sems.at[w, j], device_id=(cx, cy, c), device_id_type=MESH)
                cp.start()
                cps.append(cp)
        for cp in cps:
            cp.wait()

    sem = pltpu.SemaphoreType.DMA
    return _sequencer(body, parts, [_sds((3,) + p.shape[1:], p.dtype) for p in parts],
                      [sem((nw, 3)), sem((nw, 3))], SCATTER_ID, name)


def _sum_partials(chip, part, recv, after=()):
    _, hw, cols = part.shape
    th = hw // 2 if hw % 32 == 0 else hw

    def body(chip_ref, p_ref, r_ref, *rest):
        out_ref = rest[len(after)]
        acc = p_ref[...].astype(F32)
        for j in range(3):
            acc = acc + r_ref[j].astype(F32)
        out_ref[...] = acc

    return pl.pallas_call(
        body,
        grid_spec=pltpu.PrefetchScalarGridSpec(
            num_scalar_prefetch=1, grid=(hw // th,),
            in_specs=[pl.BlockSpec((None, th, cols), lambda i, chip_ref: (chip_ref[0], i, 0)),
                      pl.BlockSpec((3, th, cols), lambda i, chip_ref: (0, i, 0))] + [HBM_OPERAND] * len(after),
            out_specs=pl.BlockSpec((th, cols), lambda i, chip_ref: (i, 0))),
        out_shape=_sds((hw, cols)), compiler_params=_cp(("parallel",)), name="sum_partials")(
            chip, part, recv, *after)


def _swap_reduced_halves(halves, *, name):
    nw = len(halves)

    def body(*refs):
        ins, outs = refs[:nw], refs[nw:2 * nw]
        send_sems, recv_sems = refs[2 * nw:]
        x, y, c = _place()
        _handshake([(x, y, 1 - c)])
        cps = []
        for w in range(nw):
            cp = pltpu.make_async_remote_copy(
                src_ref=ins[w], dst_ref=outs[w], send_sem=send_sems.at[w], recv_sem=recv_sems.at[w],
                device_id=(x, y, 1 - c), device_id_type=MESH)
            cp.start()
            cps.append(cp)
        for cp in cps:
            cp.wait()

    sem = pltpu.SemaphoreType.DMA
    return _sequencer(body, halves, [_sds(h.shape, h.dtype) for h in halves], [sem((nw,)), sem((nw,))], JOIN_ID, name)


def _allreduce_rows(vec, *, name, after=()):
    rows = vec.shape[0]

    def body(v_ref, *rest):
        out_ref, slots, send_sems, recv_sems = rest[len(after):]
        x, y, c = _place()
        me = 4 * x + 2 * y + c
        slots[me] = v_ref[...]
        peers = []
        for mask in range(1, N_DEV):
            px = 1 - x if mask & 4 else x
            py = 1 - y if mask & 2 else y
            pc = 1 - c if mask & 1 else c
            peers.append((px, py, pc))
        cps = []
        for k, peer in enumerate(peers):
            cp = pltpu.make_async_remote_copy(
                src_ref=v_ref, dst_ref=slots.at[me], send_sem=send_sems.at[k], recv_sem=recv_sems.at[k],
                device_id=peer, device_id_type=MESH)
            cp.start()
            cps.append(cp)
        for k, (px, py, pc) in enumerate(peers):
            pltpu.make_async_remote_copy(
                src_ref=v_ref, dst_ref=slots.at[4 * px + 2 * py + pc], send_sem=send_sems.at[k],
                recv_sem=recv_sems.at[k], device_id=(px, py, pc), device_id_type=MESH).wait_recv()
        for cp in cps:
            cp.wait_send()
        acc = slots[0]
        for d in range(1, N_DEV):
            acc = acc + slots[d]
        out_ref[...] = acc

    vmem = pl.BlockSpec(memory_space=pltpu.VMEM)
    return pl.pallas_call(
        body, in_specs=[vmem] + [HBM_OPERAND] * len(after), out_specs=vmem, out_shape=_sds((rows, 128)),
        scratch_shapes=[pltpu.VMEM((N_DEV, rows, 128), F32), pltpu.SemaphoreType.DMA((N_DEV - 1,)),
                        pltpu.SemaphoreType.DMA((N_DEV - 1,))],
        compiler_params=pltpu.CompilerParams(vmem_limit_bytes=VMEM_LIMIT_BYTES), name=name)(vec, *after)


def _reduce_scatter_start(grads, core, *, tag, add_after=()):
    others = _swap_other_halves(grads, name="swap_other_halves_" + tag)
    parts = [_add_my_half(core, g, o, add_after) for g, o in zip(grads, others)]
    return parts, _scatter_partials(parts, name="scatter_partials_" + tag)


def _reduce_scatter_finish(parts, recvd, chip, *, tag, sum_after=()):
    mine = [_sum_partials(chip, p, r, sum_after) for p, r in zip(parts, recvd)]
    return mine, _swap_reduced_halves(mine, name="swap_reduced_halves_" + tag)


ADAM_BLOCK_ELEMS = 256 * 1024


def _adam_rows(rows, cols):
    tm = rows
    while tm * cols > ADAM_BLOCK_ELEMS and tm % 16 == 0:
        tm //= 2
    return tm


def _adam_step(wv, gv, mv, vv):
    m2 = ADAM_B1 * mv + (1.0 - ADAM_B1) * gv
    v2 = ADAM_B2 * vv + (1.0 - ADAM_B2) * (gv * gv)
    m_hat = m2 / (1.0 - ADAM_B1 ** ADAM_STEP)
    v_hat = v2 / (1.0 - ADAM_B2 ** ADAM_STEP)
    return -ADAM_LR * (m_hat / (jnp.sqrt(v_hat) + ADAM_EPS) + ADAM_WD * wv), m2, v2


def _adamw(w, g, m, v, *, name):
    rows, cols = w.shape
    return _rowwise(_adam_step, [w, g, m, v], [], [_sds((rows, cols))] * 3, tm=_adam_rows(rows, cols), name=name)


def _adamw_halves(core, w, g_mine, g_theirs, m, v, *, name, after=()):
    rows, cols = w.shape
    hw = rows // 2
    tm = _adam_rows(hw, cols)
    per_half = hw // tm

    def body(core_ref, w_ref, gm_ref, gt_ref, m_ref, v_ref, *rest):
        g_out, d_out, m_out, v_out = rest[len(after):]
        mine = (pl.program_id(0) // per_half) == core_ref[0]
        g = jnp.where(mine, gm_ref[...], gt_ref[...])
        d, m2, v2 = _adam_step(w_ref[...], g, m_ref[...], v_ref[...])
        g_out[...] = g
        d_out[...] = d
        m_out[...] = m2
        v_out[...] = v2

    full = pl.BlockSpec((tm, cols), lambda i, core_ref: (i, 0))
    half = pl.BlockSpec((tm, cols), lambda i, core_ref: (i % per_half, 0))
    return pl.pallas_call(
        body,
        grid_spec=pltpu.PrefetchScalarGridSpec(
            num_scalar_prefetch=1, grid=(rows // tm,),
            in_specs=[full, half, half, full, full] + [HBM_OPERAND] * len(after), out_specs=[full, full, full, full]),
        out_shape=[_sds((rows, cols))] * 4, compiler_params=_cp(("parallel",)), name=name)(
            core, w, g_mine, g_theirs, m, v, *after)


HELD_TRANSPOSED = ("w_ff_gate", "w_ff_up")


def _as_rows(name, arr):
    return arr[0].T if name in HELD_TRANSPOSED else arr[0]


def _from_rows(name, arr2d):
    return (arr2d.T if name in HELD_TRANSPOSED else arr2d)[None]


STORED_SWAPPED = ("ssm_b_re", "ssm_b_im")


def _as_stored(name, arr):
    return jnp.swapaxes(arr, -1, -2) if name in STORED_SWAPPED else arr


def _pack_rows(arrs):
    flat = jnp.concatenate([a.reshape(-1).astype(F32) for a in arrs])
    rows = -(-flat.shape[0] // 1024) * 8
    return jnp.pad(flat, (0, rows * 128 - flat.shape[0])).reshape(rows, 128)


def _unpack_rows(vec, shapes):
    flat = vec.reshape(-1)
    out, off = [], 0
    for shp in shapes:
        size = math.prod(shp)
        out.append(flat[off:off + size].reshape(shp))
        off += size
    return out


SMALL = ("b_gate", "ssm_a_re", "ssm_a_im", "ssm_log_dt", "ssm_b_re", "ssm_b_im", "ssm_c_re", "ssm_c_im", "ssm_d",
         "ln1_g", "ln1_b", "ln2_g", "ln2_b")
GATHER_GROUPS = (("w_in", ("w_in",)), ("mixer", ("w_attn_br", "w_ssm_br", "w_glu", "w_out")),
                 ("ffn", ("w_ff_gate", "w_ff_up", "w_ff_down")))
REDUCE_GROUPS = (("ffn", ("w_ff_down", "w_ff_gate", "w_ff_up")),
                 ("mixer", ("w_out", "w_ssm_br", "w_glu", "w_attn_br")), ("w_in", ("w_in",)))
WEIGHTS = ("w_in", "b_gate", "w_attn_br", "w_ssm_br", "w_out", "ssm_a_re", "ssm_a_im", "ssm_log_dt", "ssm_b_re",
           "ssm_b_im", "ssm_c_re", "ssm_c_im", "ssm_d", "w_glu", "ln1_g", "ln1_b", "w_ff_gate", "w_ff_up", "w_ff_down",
           "ln2_g", "ln2_b")


def kernel(x, w_in, b_gate, w_attn_br, w_ssm_br, w_out, ssm_a_re, ssm_a_im, ssm_log_dt, ssm_b_re, ssm_b_im, ssm_c_re, ssm_c_im, ssm_d, w_glu, ln1_g, ln1_b, w_ff_gate, w_ff_up, w_ff_down, ln2_g, ln2_b, loss_target, m_w_in, m_b_gate, m_w_attn_br, m_w_ssm_br, m_w_out, m_ssm_a_re, m_ssm_a_im, m_ssm_log_dt, m_ssm_b_re, m_ssm_b_im, m_ssm_c_re, m_ssm_c_im, m_ssm_d, m_w_glu, m_ln1_g, m_ln1_b, m_w_ff_gate, m_w_ff_up, m_w_ff_down, m_ln2_g, m_ln2_b, v_w_in, v_b_gate, v_w_attn_br, v_w_ssm_br, v_w_out, v_ssm_a_re, v_ssm_a_im, v_ssm_log_dt, v_ssm_b_re, v_ssm_b_im, v_ssm_c_re, v_ssm_c_im, v_ssm_d, v_w_glu, v_ln1_g, v_ln1_b, v_w_ff_gate, v_w_ff_up, v_w_ff_down, v_ln2_g, v_ln2_b):
    given = dict(locals())
    px, py, pc = _place()
    chip = 2 * px + py
    core_s = jnp.reshape(pc, (1,)).astype(jnp.int32)
    chip_s = jnp.reshape(chip, (1,)).astype(jnp.int32)

    wts = {}
    for tag, names in GATHER_GROUPS:
        wts.update(zip(names, _gather_weights([_as_rows(n, given[n]).astype(BF16) for n in names],
                                              name="gather_" + tag)))
    ncol = D_MODEL // N_CHIPS
    bg_mine = jnp.where(pc == 0, b_gate[0], jnp.zeros_like(b_gate[0]))
    bg_full = lax.dynamic_update_slice(jnp.zeros((2, D_MODEL), F32), bg_mine, (0, chip * ncol))
    bg_full = _allreduce_rows(bg_full.reshape(16, 128), name="gather_gate_bias").reshape(2, D_MODEL)
    small = {n: given[n][0] for n in SMALL if n.startswith("ssm")}
    small.update({n: given[n] for n in ("ln1_g", "ln1_b", "ln2_g", "ln2_b")})
    small["b_gate"] = bg_full

    loss_mine, grad_x, big_g, small_g, marks = _local_step(x[0], loss_target[0], wts, small)
    loss = lax.psum(loss_mine, ("x", "y", "c"))

    groups = dict(REDUCE_GROUPS)
    add_after = {"ffn": (marks["ln1_bwd"],), "mixer": (marks["scan_bwd"],), "w_in": (marks["dx_proj"],)}
    parts, recvd = {}, {}
    for tag, names in REDUCE_GROUPS:
        parts[tag], recvd[tag] = _reduce_scatter_start([big_g[n] for n in names], core_s, tag=tag,
                                                       add_after=add_after[tag])
    grads, delta, new_m, new_v = {}, {}, {}, {}

    def finish(tag, sum_after, adam_after):
        mine, theirs = _reduce_scatter_finish(parts[tag], recvd[tag], chip_s, tag=tag, sum_after=sum_after)
        for n, g_mine, g_theirs in zip(groups[tag], mine, theirs):
            res = _adamw_halves(core_s, _as_rows(n, given[n]), g_mine, g_theirs, _as_rows(n, given["m_" + n]),
                                _as_rows(n, given["v_" + n]), name="adamw_" + n, after=adam_after)
            grads[n], delta[n], new_m[n], new_v[n] = [_from_rows(n, r) for r in res]

    in_flight = (parts["w_in"][0],)
    finish("ffn", (marks["scan_bwd"],), (marks["attention_bwd_0"],))
    finish("mixer", (marks["attention_bwd_0"],), in_flight)
    stored = [_as_stored(n, small_g[n]) for n in SMALL]
    summed = _unpack_rows(_allreduce_rows(_pack_rows(stored), name="allreduce_small", after=in_flight),
                          [a.shape for a in stored])
    for n, g in zip(SMALL, summed):
        g = _as_stored(n, g)
        if n == "b_gate":
            g = lax.dynamic_slice(g, (0, chip * ncol), (2, ncol))
        grads[n] = g.reshape(given[n].shape)
    packed = [_pack_rows([_as_stored(n, src[n]) for n in SMALL]) for src in
              (given, grads, {n: given["m_" + n] for n in SMALL}, {n: given["v_" + n] for n in SMALL})]
    shapes = [_as_stored(n, given[n]).shape for n in SMALL]
    small_out = _adamw(*packed, name="adamw_small")
    for out, vec in zip((delta, new_m, new_v), small_out):
        out.update((n, _as_stored(n, a)) for n, a in zip(SMALL, _unpack_rows(vec, shapes)))
    behind = [delta[n] for tag in ("ffn", "mixer") for n in groups[tag]] + [small_out[0], grad_x]
    finish("w_in", tuple(behind), ())

    return (loss, grad_x.reshape(x.shape), *[grads[n] for n in WEIGHTS], *[delta[n] for n in WEIGHTS],
            *[new_m[n] for n in WEIGHTS], *[new_v[n] for n in WEIGHTS])
```

```python
import math

import jax
import jax.numpy as jnp
from jax import lax
from jax.experimental import pallas as pl
from jax.experimental.pallas import tpu as pltpu
from jax.experimental.pallas import tpu_sc as plsc

F32 = jnp.float32
BF16 = jnp.bfloat16
MESH = pl.DeviceIdType.MESH

D_MODEL = 1024
SEQ = 2048
HEAD_DIM = 64
ATTN_HEADS = 8
DILATIONS = (1, 4, 16)
ATTN_WIDTH = ATTN_HEADS * HEAD_DIM
QKV_WIDTH = 3 * ATTN_WIDTH
BLOCK = 128
ROPE_THETA = 10000.0
NEG_INF = -1e30
SSM_GROUP = 16
SSM_GROUPS = 32
SSM_WIDTH = 512
SSM_STATE = 64
SSM_LANES = SSM_GROUPS * SSM_STATE
SCAN_CHUNKS = 8
SCAN_STEPS = SEQ // SCAN_CHUNKS
IN_WIDTH = 3 * QKV_WIDTH + SSM_WIDTH + 2 * D_MODEL
D_FF = 2816
N_CHIPS = 4
N_DEV = 8
DN_ALPHA = 2.0 ** 0.25
LN_EPS = 1e-5
ADAM_LR = 0.001
ADAM_B1 = 0.9
ADAM_B2 = 0.999
ADAM_EPS = 1e-08
ADAM_WD = 0.01
ADAM_STEP = 10
GELU_C = math.sqrt(2.0 / math.pi)
GELU_K = 0.044715

VMEM_LIMIT_BYTES = 56 * 1024 * 1024


def _sds(shape, dtype=F32):
    return jax.ShapeDtypeStruct(tuple(shape), dtype)


def _cp(semantics=None):
    return pltpu.CompilerParams(dimension_semantics=semantics, vmem_limit_bytes=VMEM_LIMIT_BYTES)


HBM_OPERAND = pl.BlockSpec(memory_space=pl.ANY)


def _matmul(a, b, *, grid, a_spec, b_spec, o_spec, out_shape, dims, k_axis=None, name, after=()):
    nk = grid[k_axis] if k_axis is not None else 1
    o_block = tuple(d for d in o_spec.block_shape if d is not None)
    n_after = len(after)

    def body(a_ref, b_ref, *rest):
        o_ref, acc = rest[n_after], rest[n_after + 1:]
        part = lax.dot_general(a_ref[...].astype(BF16), b_ref[...].astype(BF16),
                               (((dims[0],), (dims[1],)), ((), ())), preferred_element_type=F32)
        if k_axis is None:
            o_ref[...] = part.astype(o_ref.dtype)
        else:
            k = pl.program_id(k_axis)

            @pl.when(k == 0)
            def _():
                acc[0][...] = part

            @pl.when(k > 0)
            def _():
                acc[0][...] += part

            @pl.when(k == nk - 1)
            def _():
                o_ref[...] = acc[0][...].astype(o_ref.dtype)

    sem = tuple("arbitrary" if ax == k_axis else "parallel" for ax in range(len(grid)))
    return pl.pallas_call(
        body, grid=grid, in_specs=[a_spec, b_spec] + [HBM_OPERAND] * n_after, out_specs=o_spec, out_shape=out_shape,
        scratch_shapes=[pltpu.VMEM(o_block, F32)] if k_axis is not None else [],
        compiler_params=_cp(sem), name=name)(a, b, *after)


def _mm_cols(a, wg, *, tm, name, out_dtype=F32, out3d=False):
    m, k = a.shape
    ns = wg.shape[2]
    if out3d:
        o_spec = pl.BlockSpec((None, tm, ns), lambda i, s: (s, i, 0))
        out_shape = _sds((N_CHIPS, m, ns), out_dtype)
    else:
        o_spec = pl.BlockSpec((tm, ns), lambda i, s: (i, s))
        out_shape = _sds((m, N_CHIPS * ns), out_dtype)
    return _matmul(a, wg, grid=(m // tm, N_CHIPS),
                   a_spec=pl.BlockSpec((tm, k), lambda i, s: (i, 0)),
                   b_spec=pl.BlockSpec((None, k, ns), lambda i, s: (s, 0, 0)),
                   o_spec=o_spec, out_shape=out_shape, dims=(1, 0), name=name)


def _mm_cols_nt(dy, wg, *, tm, name, dy3d=False, out_dtype=F32, after=()):
    k, ns = wg.shape[1], wg.shape[2]
    if dy3d:
        m = dy.shape[1]
        a_spec = pl.BlockSpec((None, tm, ns), lambda i, s: (s, i, 0))
    else:
        m = dy.shape[0]
        a_spec = pl.BlockSpec((tm, ns), lambda i, s: (i, s))
    return _matmul(dy, wg, grid=(m // tm, N_CHIPS), a_spec=a_spec,
                   b_spec=pl.BlockSpec((None, k, ns), lambda i, s: (s, 0, 0)),
                   o_spec=pl.BlockSpec((tm, k), lambda i, s: (i, 0)),
                   out_shape=_sds((m, k), out_dtype), dims=(1, 1), k_axis=1, name=name, after=after)


def _mm_cols_tn(a, dy, *, ns, name, dy3d=False):
    m, k = a.shape
    if dy3d:
        b_spec = pl.BlockSpec((None, m, ns), lambda s: (s, 0, 0))
    else:
        b_spec = pl.BlockSpec((m, ns), lambda s: (0, s))
    return _matmul(a, dy, grid=(N_CHIPS,), a_spec=pl.BlockSpec((m, k), lambda s: (0, 0)), b_spec=b_spec,
                   o_spec=pl.BlockSpec((None, k, ns), lambda s: (s, 0, 0)),
                   out_shape=_sds((N_CHIPS, k, ns), BF16), dims=(0, 0), name=name)


def _mm_plain(a, b, *, tm, tn, name, out_dtype=F32, dims=(1, 0), tk=None):
    m = a.shape[1 - dims[0]]
    kk = a.shape[dims[0]]
    n = b.shape[1 - dims[1]]
    tk = kk if tk is None else tk
    nk = kk // tk

    def a_idx(i, j, k):
        return (i, k) if dims[0] == 1 else (k, i)

    def b_idx(i, j, k):
        return (k, j) if dims[1] == 0 else (j, k)

    a_blk = (tm, tk) if dims[0] == 1 else (tk, tm)
    b_blk = (tk, tn) if dims[1] == 0 else (tn, tk)
    return _matmul(a, b, grid=(m // tm, n // tn, nk),
                   a_spec=pl.BlockSpec(a_blk, a_idx), b_spec=pl.BlockSpec(b_blk, b_idx),
                   o_spec=pl.BlockSpec((tm, tn), lambda i, j, k: (i, j)),
                   out_shape=_sds((m, n), out_dtype), dims=dims, k_axis=2 if nk > 1 else None, name=name)


def _rowwise(fn, tiled, full, outs, accs=(), *, tm, name, after=()):
    args, in_specs = [], []
    for t in tiled:
        if isinstance(t, tuple):
            arr, w, cb = t
            in_specs.append(pl.BlockSpec((tm, w), lambda i, cb=cb: (i, cb)))
        else:
            arr = t
            in_specs.append(pl.BlockSpec((tm, arr.shape[1]), lambda i: (i, 0)))
        args.append(arr)
    rows = args[0].shape[0]
    for f in full:
        in_specs.append(pl.BlockSpec(f.shape, lambda i, nd=f.ndim: (0,) * nd))
        args.append(f)
    out_specs = [pl.BlockSpec((tm, o.shape[1]), lambda i: (i, 0)) for o in outs]
    out_specs += [pl.BlockSpec(a.shape, lambda i, nd=len(a.shape): (0,) * nd) for a in accs]
    n_in, n_out = len(args), len(outs)
    in_specs += [HBM_OPERAND] * len(after)
    first_out = n_in + len(after)

    def body(*refs):
        res = fn(*[r[...] for r in refs[:n_in]])
        res = res if isinstance(res, (tuple, list)) else (res,)
        for r, v in zip(refs[first_out:first_out + n_out], res[:n_out]):
            r[...] = v.astype(r.dtype)
        i = pl.program_id(0)
        for r, v in zip(refs[first_out + n_out:], res[n_out:]):
            @pl.when(i == 0)
            def _(r=r, v=v):
                r[...] = v

            @pl.when(i > 0)
            def _(r=r, v=v):
                r[...] += v

    res = pl.pallas_call(
        body, grid=(rows // tm,), in_specs=in_specs, out_specs=out_specs, out_shape=list(outs) + list(accs),
        compiler_params=_cp(("arbitrary",) if accs else ("parallel",)), name=name)(*args, *after)
    return res


def _colsum(v):
    return jnp.sum(v, axis=0, keepdims=True)


def _ln_stats(z):
    mu = jnp.mean(z, axis=-1, keepdims=True)
    zc = z - mu
    var = jnp.mean(zc * zc, axis=-1, keepdims=True)
    rstd = lax.rsqrt(var + LN_EPS)
    return zc * rstd, rstd


def _ln_bwd(dy, xhat, rstd, g):
    dxh = dy * g
    m1 = jnp.mean(dxh, axis=-1, keepdims=True)
    m2 = jnp.mean(dxh * xhat, axis=-1, keepdims=True)
    return rstd * (dxh - m1 - xhat * m2)


def _swap_halves(t):
    w = t.shape[-1]
    lane = lax.broadcasted_iota(jnp.int32, t.shape, t.ndim - 1)
    return jnp.where((lane % HEAD_DIM) < HEAD_DIM // 2, pltpu.roll(t, w - HEAD_DIM // 2, t.ndim - 1),
                     pltpu.roll(t, HEAD_DIM // 2, t.ndim - 1))


PHASES = max(DILATIONS)
PAIR = 2 * HEAD_DIM
UNITS = SEQ // BLOCK
UNIT_UNROLL = 4
ROPE_ROWS = 256


def _to_phase_rows(t):
    return t.reshape(SEQ // PHASES, PHASES, t.shape[1]).transpose(1, 0, 2).reshape(t.shape)


def _reorder_rows(arr, *, to_phase, name):
    def body(i_ref, o_ref):
        for rho in range(PHASES):
            phase = pl.ds(rho * BLOCK, BLOCK)
            strided = pl.ds(rho, BLOCK, stride=PHASES)
            if to_phase:
                o_ref[phase, :] = i_ref[strided, :]
            else:
                o_ref[strided, :] = i_ref[phase, :]

    spec = pl.BlockSpec((SEQ, BLOCK), lambda j: (0, j))
    return pl.pallas_call(body, grid=(arr.shape[1] // BLOCK,), in_specs=[spec], out_specs=spec,
                          out_shape=_sds(arr.shape), compiler_params=_cp(("parallel",)), name=name)(arr)


def _rope(t, cf, ss):
    return t * cf + _swap_halves(t) * ss


def _rope_transposed(d, cf, ss):
    return d * cf + _swap_halves(d * ss)


def _unit_pieces(u, dil):
    pieces, length = PHASES // dil, 8 * dil
    if dil == 1:
        rho, i = 0, u
    elif dil == PHASES:
        rho, i = u, 0
    else:
        rho, i = jnp.bitwise_and(u, dil - 1), jnp.right_shift(u, dil.bit_length() - 1)
    before = jnp.maximum(i - 1, 0)
    cur = [pl.multiple_of((rho + dil * k) * BLOCK + length * i, 8) for k in range(pieces)]
    prev = [pl.multiple_of((rho + dil * k) * BLOCK + length * before, 8) for k in range(pieces)]
    return i, cur, prev


def _load_tile(ref, starts, dil):
    return jnp.concatenate([ref[pl.ds(st, 8 * dil), :] for st in starts], axis=0)


def _store_tile(ref, starts, dil, val, head=None, accumulate=False):
    length = 8 * dil
    lanes = slice(None) if head is None else pl.ds(head * HEAD_DIM, HEAD_DIM)
    cols = slice(None) if head is None else slice(head * HEAD_DIM, (head + 1) * HEAD_DIM)
    for k, st in enumerate(starts):
        piece = val[k * length:(k + 1) * length, cols]
        if accumulate:
            ref[pl.ds(st, length), lanes] += piece
        else:
            ref[pl.ds(st, length), lanes] = piece


def _tile_position(idx, dil):
    pieces, length = PHASES // dil, 8 * dil
    return pieces * jnp.bitwise_and(idx, length - 1) + jnp.right_shift(idx, length.bit_length() - 1)


def _band_mask(i, dil):
    row = lax.broadcasted_iota(jnp.int32, (BLOCK, 2 * BLOCK), 0)
    col = lax.broadcasted_iota(jnp.int32, (BLOCK, 2 * BLOCK), 1)
    key_pos = _tile_position(jnp.bitwise_and(col, BLOCK - 1), dil) + jnp.where(col >= BLOCK, 0, -BLOCK)
    dist = _tile_position(row, dil) - key_pos
    return (dist >= 0) & (dist <= BLOCK) & ((col >= BLOCK) | (i > 0))


def _causal_mask():
    row = lax.broadcasted_iota(jnp.int32, (BLOCK, BLOCK), 0)
    col = lax.broadcasted_iota(jnp.int32, (BLOCK, BLOCK), 1)
    return row >= col


def _pair_views(col0):
    return [pl.BlockSpec((SEQ, PAIR), lambda hp, g=g: (0, col0 // PAIR + g * (ATTN_WIDTH // PAIR) + hp))
            for g in range(len(DILATIONS))]


def _rotate(in_refs, out_refs, cf_ref, ss_ref, scale):
    def step(t, carry):
        rows = pl.ds(pl.multiple_of(t * ROPE_ROWS, ROPE_ROWS), ROPE_ROWS)
        cf, ss = cf_ref[rows, :] * scale, ss_ref[rows, :] * scale
        for i_ref, o_ref in zip(in_refs, out_refs):
            o_ref[rows, :] = _rope(i_ref[rows, :], cf, ss)
        return carry

    lax.fori_loop(0, SEQ // ROPE_ROWS, step, 0)


def _attention_fwd(proj, cos_f, sin_s):
    ng = len(DILATIONS)

    def body(*refs):
        q_refs, k_refs, v_refs = refs[:ng], refs[ng:2 * ng], refs[2 * ng:3 * ng]
        cf_ref, ss_ref, attn_ref, lse_ref = refs[3 * ng:3 * ng + 4]
        scratch = refs[3 * ng + 4:]
        qr_refs, kr_refs = scratch[:ng], scratch[ng:]
        _rotate(q_refs, qr_refs, cf_ref, ss_ref, 1.0 / math.sqrt(HEAD_DIM))
        _rotate(k_refs, kr_refs, cf_ref, ss_ref, 1.0)
        first = lax.broadcasted_iota(jnp.int32, (BLOCK, PAIR), 1) < HEAD_DIM
        for g, dil in enumerate(DILATIONS):
            two_blocks = SEQ // dil > BLOCK

            def unit(u, carry, g=g, dil=dil, two_blocks=two_blocks):
                i, rows, prev = _unit_pieces(u, dil)
                qq = _load_tile(qr_refs[g], rows, dil).astype(BF16)
                kk = _load_tile(kr_refs[g], rows, dil)
                vv = _load_tile(v_refs[g], rows, dil)
                if two_blocks:
                    kk = jnp.concatenate([_load_tile(kr_refs[g], prev, dil), kk], axis=0)
                    vv = jnp.concatenate([_load_tile(v_refs[g], prev, dil), vv], axis=0)
                    valid = _band_mask(i, dil)
                else:
                    valid = _causal_mask()
                kk, vv = kk.astype(BF16), vv.astype(BF16)
                zero = jnp.zeros_like(qq)
                outs, lses = [], []
                for qh in (jnp.where(first, qq, zero), jnp.where(first, zero, qq)):
                    s = jnp.where(valid, _dot_nt(qh, kk), NEG_INF)
                    m = jnp.max(s, axis=1, keepdims=True)
                    p = jnp.exp(s - m)
                    l = jnp.sum(p, axis=1, keepdims=True)
                    outs.append(jnp.dot(p.astype(BF16), vv, preferred_element_type=F32) * (1.0 / l))
                    lses.append(m + jnp.log(l))
                o = jnp.where(first, outs[0], outs[1])
                lse = jnp.where(first, lses[0], lses[1])
                if g > 0:
                    lse_old = _load_tile(lse_ref, rows, dil)
                    m = jnp.maximum(lse_old, lse)
                    lse_new = m + jnp.log(jnp.exp(lse_old - m) + jnp.exp(lse - m))
                    o = _load_tile(attn_ref, rows, dil) * jnp.exp(lse_old - lse_new) + o * jnp.exp(lse - lse_new)
                    lse = lse_new
                _store_tile(attn_ref, rows, dil, o)
                _store_tile(lse_ref, rows, dil, lse)
                return carry

            lax.fori_loop(0, UNITS, unit, 0, unroll=UNIT_UNROLL)

    whole = pl.BlockSpec((SEQ, PAIR), lambda hp: (0, 0))
    out = pl.BlockSpec((SEQ, PAIR), lambda hp: (0, hp))
    return pl.pallas_call(
        body, grid=(ATTN_WIDTH // PAIR,),
        in_specs=_pair_views(0) + _pair_views(QKV_WIDTH) + _pair_views(2 * QKV_WIDTH) + [whole, whole],
        out_specs=[out, out], out_shape=[_sds((SEQ, ATTN_WIDTH)), _sds((SEQ, ATTN_WIDTH))],
        scratch_shapes=[pltpu.VMEM((SEQ, PAIR), F32)] * (2 * ng),
        compiler_params=_cp(("parallel",)), name="attention_fwd")(*([proj] * (3 * ng)), cos_f, sin_s)


def _attention_bwd(g, proj, cos_f, sin_s, d_attn, attn, lse):
    dil = DILATIONS[g]
    two_blocks = SEQ // dil > BLOCK

    def body(q_ref, k_ref, v_ref, cf_ref, ss_ref, do_ref, o_ref, lse_ref, dq_out, dk_out, dv_out,
             qr_ref, kr_ref, dq_acc, dk_acc, dv_acc):
        _rotate([q_ref], [qr_ref], cf_ref, ss_ref, 1.0 / math.sqrt(HEAD_DIM))
        _rotate([k_ref], [kr_ref], cf_ref, ss_ref, 1.0)
        dk_acc[...] = jnp.zeros_like(dk_acc)
        dv_acc[...] = jnp.zeros_like(dv_acc)
        nk = 2 * BLOCK if two_blocks else BLOCK
        first = lax.broadcasted_iota(jnp.int32, (BLOCK, PAIR), 1) < HEAD_DIM
        first_k = lax.broadcasted_iota(jnp.int32, (nk, PAIR), 1) < HEAD_DIM

        def unit(u, carry):
            i, rows, prev = _unit_pieces(u, dil)
            qq = _load_tile(qr_ref, rows, dil).astype(BF16)
            kk = _load_tile(kr_ref, rows, dil)
            vv = _load_tile(v_ref, rows, dil)
            if two_blocks:
                kk = jnp.concatenate([_load_tile(kr_ref, prev, dil), kk], axis=0)
                vv = jnp.concatenate([_load_tile(v_ref, prev, dil), vv], axis=0)
                valid = _band_mask(i, dil)
            else:
                valid = _causal_mask()
            kk, vv = kk.astype(BF16), vv.astype(BF16)
            dof = _load_tile(do_ref, rows, dil)
            dd = dof * _load_tile(o_ref, rows, dil)
            lse2 = _load_tile(lse_ref, rows, dil)
            dob = dof.astype(BF16)
            zq, zd, zf = jnp.zeros_like(qq), jnp.zeros_like(dob), jnp.zeros_like(dd)
            dqs, dks, dvs = [], [], []
            for head in range(2):
                mine = first if head == 0 else jnp.logical_not(first)
                delta = jnp.sum(jnp.where(mine, dd, zf), axis=1, keepdims=True)
                lse_h = lse2[:, head * HEAD_DIM:head * HEAD_DIM + 1]
                s = _dot_nt(jnp.where(mine, qq, zq), kk)
                p = jnp.where(valid, jnp.exp(s - lse_h), 0.0)
                dp = _dot_nt(jnp.where(mine, dob, zd), vv)
                ds = (p * (dp - delta)).astype(BF16)
                dqs.append(jnp.dot(ds, kk, preferred_element_type=F32))
                dks.append(lax.dot_general(ds, qq, (((0,), (0,)), ((), ())), preferred_element_type=F32))
                dvs.append(lax.dot_general(p.astype(BF16), dob, (((0,), (0,)), ((), ())), preferred_element_type=F32))
            dk = jnp.where(first_k, dks[0], dks[1])
            dv = jnp.where(first_k, dvs[0], dvs[1])
            _store_tile(dq_acc, rows, dil, jnp.where(first, dqs[0], dqs[1]))
            _store_tile(dk_acc, rows, dil, dk[nk - BLOCK:], accumulate=True)
            _store_tile(dv_acc, rows, dil, dv[nk - BLOCK:], accumulate=True)
            if two_blocks:
                _store_tile(dk_acc, prev, dil, dk[:BLOCK], accumulate=True)
                _store_tile(dv_acc, prev, dil, dv[:BLOCK], accumulate=True)
            return carry

        lax.fori_loop(0, UNITS, unit, 0, unroll=UNIT_UNROLL)

        def finish(t, carry):
            rows = pl.ds(pl.multiple_of(t * ROPE_ROWS, ROPE_ROWS), ROPE_ROWS)
            cf, ss = cf_ref[rows, :], ss_ref[rows, :]
            dq = dq_acc[rows, :] * (1.0 / math.sqrt(HEAD_DIM))
            dq_out[rows, :] = _rope_transposed(dq, cf, ss).astype(BF16)
            dk_out[rows, :] = _rope_transposed(dk_acc[rows, :], cf, ss).astype(BF16)
            dv_out[rows, :] = dv_acc[rows, :].astype(BF16)
            return carry

        lax.fori_loop(0, SEQ // ROPE_ROWS, finish, 0)

    whole = pl.BlockSpec((SEQ, PAIR), lambda hp: (0, 0))
    pair = pl.BlockSpec((SEQ, PAIR), lambda hp: (0, hp))
    views = [_pair_views(col0)[g] for col0 in (0, QKV_WIDTH, 2 * QKV_WIDTH)]
    return pl.pallas_call(
        body, grid=(ATTN_WIDTH // PAIR,), in_specs=views + [whole, whole, pair, pair, pair],
        out_specs=[pair, pair, pair], out_shape=[_sds((SEQ, ATTN_WIDTH), BF16)] * 3,
        scratch_shapes=[pltpu.VMEM((SEQ, PAIR), F32)] * 5,
        compiler_params=_cp(("parallel",)), name=f"attention_bwd_{g}")(proj, proj, proj, cos_f, sin_s, d_attn, attn, lse)


def _cmul(ar, ai, br, bi):
    return ar * br - ai * bi, ar * bi + ai * br


def _pow256(ar, ai):
    for _ in range(8):
        ar, ai = _cmul(ar, ai, ar, ai)
    return ar, ai


def _chunk_carries(first_r, first_i, pr, pi, reverse):
    rows = lax.broadcasted_iota(jnp.int32, first_r.shape, 0)
    out_r = jnp.zeros_like(first_r)
    out_i = jnp.zeros_like(first_i)
    hr = jnp.zeros_like(first_r[0:1])
    hi = jnp.zeros_like(hr)
    order = range(SCAN_CHUNKS - 1, -1, -1) if reverse else range(SCAN_CHUNKS)
    for c in order:
        out_r = jnp.where(rows == c, hr, out_r)
        out_i = jnp.where(rows == c, hi, out_i)
        tr, ti = _cmul(pr[0:1], pi[0:1], hr, hi)
        hr = first_r[c:c + 1] + tr
        hi = first_i[c:c + 1] + ti
    return out_r, out_i


def _tile(j):
    return pl.ds(pl.multiple_of(j * SCAN_CHUNKS, SCAN_CHUNKS), SCAN_CHUNKS)


def _to_scan_rows(t):
    per = SCAN_STEPS // PHASES
    return t.reshape(PHASES, SCAN_CHUNKS, per, t.shape[1]).transpose(2, 0, 1, 3).reshape(t.shape)


def _from_scan_rows(t):
    per = SCAN_STEPS // PHASES
    return t.reshape(per, PHASES, SCAN_CHUNKS, t.shape[1]).transpose(1, 2, 0, 3).reshape(t.shape)


def _scan_in_place(hr_ref, hi_ref, a_r, a_i):
    def local(j, carry):
        tr, ti = _cmul(a_r, a_i, carry[0], carry[1])
        nr = tr + hr_ref[_tile(j), :]
        ni = ti + hi_ref[_tile(j), :]
        hr_ref[_tile(j), :] = nr
        hi_ref[_tile(j), :] = ni
        return nr, ni

    zero = jnp.zeros_like(a_r)
    last_r, last_i = lax.fori_loop(0, SCAN_STEPS, local, (zero, zero), unroll=4)
    pr, pi = _pow256(a_r, a_i)
    er, ei = _chunk_carries(last_r, last_i, pr, pi, reverse=False)

    def fix(j, carry):
        tr, ti = _cmul(carry[0], carry[1], er, ei)
        hr_ref[_tile(j), :] += tr
        hi_ref[_tile(j), :] += ti
        return _cmul(carry[0], carry[1], a_r, a_i)

    lax.fori_loop(0, SCAN_STEPS, fix, (a_r, a_i), unroll=4)
    return er, ei


def _reverse_scan_in_place(lr_ref, li_ref, hr_ref, hi_ref, er, ei, a_r, a_i):
    def local(t, carry):
        j = SCAN_STEPS - 1 - t
        tr, ti = _cmul(a_r, a_i, carry[0], carry[1])
        nr = tr + lr_ref[_tile(j), :]
        ni = ti + li_ref[_tile(j), :]
        lr_ref[_tile(j), :] = nr
        li_ref[_tile(j), :] = ni
        return nr, ni

    zero = jnp.zeros_like(a_r)
    first_r, first_i = lax.fori_loop(0, SCAN_STEPS, local, (zero, zero), unroll=4)
    pr, pi = _pow256(a_r, a_i)
    nxt_r, nxt_i = _chunk_carries(first_r, first_i, pr, pi, reverse=True)

    def accumulate(lam_r, lam_i, hp_r, hp_i, acc):
        return (acc[0] + lam_r * hp_r + lam_i * hp_i, acc[1] + lam_i * hp_r - lam_r * hp_i)

    def fix(t, carry):
        qr, qi, acc_r, acc_i = carry
        j = SCAN_STEPS - 1 - t
        tr, ti = _cmul(qr, qi, nxt_r, nxt_i)
        lam_r = lr_ref[_tile(j), :] + tr
        lam_i = li_ref[_tile(j), :] + ti
        lr_ref[_tile(j), :] = lam_r
        li_ref[_tile(j), :] = lam_i
        acc_r, acc_i = accumulate(lam_r, lam_i, hr_ref[_tile(j - 1), :], hi_ref[_tile(j - 1), :], (acc_r, acc_i))
        qr, qi = _cmul(qr, qi, a_r, a_i)
        return qr, qi, acc_r, acc_i

    qr, qi, acc_r, acc_i = lax.fori_loop(0, SCAN_STEPS - 1, fix, (a_r, a_i, zero, zero), unroll=4)
    tr, ti = _cmul(qr, qi, nxt_r, nxt_i)
    lam_r = lr_ref[_tile(0), :] + tr
    lam_i = li_ref[_tile(0), :] + ti
    lr_ref[_tile(0), :] = lam_r
    li_ref[_tile(0), :] = lam_i
    acc_r, acc_i = accumulate(lam_r, lam_i, er, ei, (acc_r, acc_i))
    return jnp.sum(acc_r, axis=0, keepdims=True), jnp.sum(acc_i, axis=0, keepdims=True)


def _rope_tables():
    half = HEAD_DIM // 2
    inv_freq = ROPE_THETA ** (-jnp.arange(half, dtype=F32) / half)
    ang = jnp.arange(SEQ, dtype=F32)[:, None] * inv_freq[None, :]
    cos, sin = jnp.cos(ang), jnp.sin(ang)
    cos_f = jnp.concatenate([cos, cos, cos, cos], axis=1)
    sin_s = jnp.concatenate([-sin, sin, -sin, sin], axis=1)
    return cos_f, sin_s


def _ssm_discretise(a_re, a_im, log_dt, b_re, b_im):
    lam = lax.complex(a_re, a_im)
    dt = jnp.exp(log_dt)[:, None]
    a_bar = jnp.exp(lam * dt)
    b_bar = ((a_bar - 1.0) / lam)[..., None] * lax.complex(b_re, b_im)
    return a_bar.real, a_bar.imag, b_bar.real, b_bar.imag


def _block_diag_in(b):
    eye = jnp.eye(SSM_GROUPS, dtype=b.dtype)
    return (eye[:, None, :, None] * b.transpose(0, 2, 1)[:, :, None, :]).reshape(SSM_WIDTH, SSM_LANES)


def _block_diag_out(c):
    eye = jnp.eye(SSM_GROUPS, dtype=c.dtype)
    return (eye[:, None, :, None] * c.transpose(0, 2, 1)[:, :, None, :]).reshape(SSM_LANES, SSM_WIDTH)


SSM_SLABS = 4
SLAB_GROUPS = SSM_GROUPS // SSM_SLABS
SLAB_IN = SSM_WIDTH // SSM_SLABS
SLAB_STATE = SSM_LANES // SSM_SLABS


def _diag_blocks(a, b):
    ra, cb = a.shape[1], b.shape[1]
    wa, wb = ra // SLAB_GROUPS, cb // SLAB_GROUPS
    d = lax.dot_general(a, b, (((0,), (0,)), ((), ())), preferred_element_type=F32)
    row_g = jnp.right_shift(lax.broadcasted_iota(jnp.int32, (ra, cb), 0), wa.bit_length() - 1)
    col_g = jnp.right_shift(lax.broadcasted_iota(jnp.int32, (ra, cb), 1), wb.bit_length() - 1)
    d = jnp.where(row_g == col_g, d, 0.0)
    fold = (jnp.bitwise_and(lax.broadcasted_iota(jnp.int32, (cb, wb), 0), wb - 1)
            == lax.broadcasted_iota(jnp.int32, (cb, wb), 1)).astype(F32)
    return jnp.dot(d, fold, preferred_element_type=F32, precision=lax.Precision.HIGHEST)


def _slab_specs():
    tok = pl.BlockSpec((SEQ, SLAB_IN), lambda j: (0, j))
    state = pl.BlockSpec((SEQ, SLAB_STATE), lambda j: (0, j))
    b_in = pl.BlockSpec((SLAB_IN, SLAB_STATE), lambda j: (j, j))
    c_out = pl.BlockSpec((SLAB_STATE, SLAB_IN), lambda j: (j, j))
    vec = pl.BlockSpec((1, SLAB_STATE), lambda j: (0, j))
    ent = pl.BlockSpec((SCAN_CHUNKS, SLAB_STATE), lambda j: (0, j))
    return tok, state, b_in, c_out, vec, ent


def _ssm_forward(u, b_in_r, b_in_i, c_out_r, c_out_ni, a_r, a_i):
    def body(u_ref, br_ref, bi_ref, cr_ref, ci_ref, ar_ref, ai_ref, y_ref, hr_ref, hi_ref, er_ref, ei_ref):
        uu = u_ref[...]
        hr_ref[...] = jnp.dot(uu, br_ref[...], preferred_element_type=F32)
        hi_ref[...] = jnp.dot(uu, bi_ref[...], preferred_element_type=F32)
        a_re = jnp.broadcast_to(ar_ref[...], (SCAN_CHUNKS, SLAB_STATE))
        a_im = jnp.broadcast_to(ai_ref[...], (SCAN_CHUNKS, SLAB_STATE))
        er_ref[...], ei_ref[...] = _scan_in_place(hr_ref, hi_ref, a_re, a_im)
        y_ref[...] = (jnp.dot(hr_ref[...].astype(BF16), cr_ref[...], preferred_element_type=F32)
                      + jnp.dot(hi_ref[...].astype(BF16), ci_ref[...], preferred_element_type=F32))

    tok, state, b_in, c_out, vec, ent = _slab_specs()
    return pl.pallas_call(
        body, grid=(SSM_SLABS,), in_specs=[tok, b_in, b_in, c_out, c_out, vec, vec],
        out_specs=[tok, state, state, ent, ent],
        out_shape=[_sds((SEQ, SSM_WIDTH)), _sds((SEQ, SSM_LANES)), _sds((SEQ, SSM_LANES)),
                   _sds((SCAN_CHUNKS, SSM_LANES)), _sds((SCAN_CHUNKS, SSM_LANES))],
        compiler_params=_cp(("parallel",)), name="ssm_forward")(u, b_in_r, b_in_i, c_out_r, c_out_ni, a_r, a_i)


def _ssm_backward(d_y, d_u_skip, u, h_r, h_i, e_r, e_i, b_in_r, b_in_i, c_out_r, c_out_ni, a_r, a_i):
    def body(dy_ref, skip_ref, u_ref, hr_ref, hi_ref, er_ref, ei_ref, br_ref, bi_ref, cr_ref, ci_ref, ar_ref, ai_ref,
             du_ref, dar_ref, dai_ref, dcr_ref, dci_ref, dbr_ref, dbi_ref, lr_ref, li_ref):
        dy = dy_ref[...]
        lr_ref[...] = _dot_nt(dy, cr_ref[...])
        li_ref[...] = _dot_nt(dy, ci_ref[...])
        a_re = jnp.broadcast_to(ar_ref[...], (SCAN_CHUNKS, SLAB_STATE))
        a_im = -jnp.broadcast_to(ai_ref[...], (SCAN_CHUNKS, SLAB_STATE))
        dar_ref[...], dai_ref[...] = _reverse_scan_in_place(lr_ref, li_ref, hr_ref, hi_ref, er_ref[...], ei_ref[...],
                                                            a_re, a_im)
        dcr_ref[...] = _diag_blocks(hr_ref[...].astype(BF16), dy)
        dci_ref[...] = _diag_blocks(hi_ref[...].astype(BF16), dy)
        lam_r, lam_i = lr_ref[...].astype(BF16), li_ref[...].astype(BF16)
        uu = u_ref[...]
        dbr_ref[...] = _diag_blocks(uu, lam_r)
        dbi_ref[...] = _diag_blocks(uu, lam_i)
        du = skip_ref[...] + _dot_nt(lam_r, br_ref[...]) + _dot_nt(lam_i, bi_ref[...])
        du_ref[...] = du.astype(BF16)

    tok, state, b_in, c_out, vec, ent = _slab_specs()
    dc = pl.BlockSpec((SLAB_STATE, SSM_GROUP), lambda j: (j, 0))
    db = pl.BlockSpec((SLAB_IN, SSM_STATE), lambda j: (j, 0))
    return pl.pallas_call(
        body, grid=(SSM_SLABS,), in_specs=[tok, tok, tok, state, state, ent, ent, b_in, b_in, c_out, c_out, vec, vec],
        out_specs=[tok, vec, vec, dc, dc, db, db],
        out_shape=[_sds((SEQ, SSM_WIDTH), BF16), _sds((1, SSM_LANES)), _sds((1, SSM_LANES)),
                   _sds((SSM_LANES, SSM_GROUP)), _sds((SSM_LANES, SSM_GROUP)),
                   _sds((SSM_WIDTH, SSM_STATE)), _sds((SSM_WIDTH, SSM_STATE))],
        scratch_shapes=[pltpu.VMEM((SEQ, SLAB_STATE), F32)] * 2,
        compiler_params=_cp(("parallel",)), name="ssm_backward")(
            d_y, d_u_skip, u, h_r, h_i, e_r, e_i, b_in_r, b_in_i, c_out_r, c_out_ni, a_r, a_i)


FF_ROWS = 1024
FF_SHARD = D_FF // N_CHIPS


def _dot_nt(a, b):
    return lax.dot_general(a, b, (((1,), (1,)), ((), ())), preferred_element_type=F32)


def _ffn_up(h, w_gate_t, w_up_t):
    def body(h_ref, wg_ref, wu_ref, a_ref, b_ref, act_ref):
        hb = h_ref[...].astype(BF16)
        a = _dot_nt(hb, wg_ref[...])
        b = _dot_nt(hb, wu_ref[...])
        a_ref[...] = a
        b_ref[...] = b
        act_ref[...] = (a * jax.nn.sigmoid(a) * b).astype(BF16)

    w_spec = pl.BlockSpec((None, FF_SHARD, D_MODEL), lambda i, k: (k, 0, 0))
    o_spec = pl.BlockSpec((None, FF_ROWS, FF_SHARD), lambda i, k: (k, i, 0))
    shape = (N_CHIPS, SEQ, FF_SHARD)
    return pl.pallas_call(
        body, grid=(SEQ // FF_ROWS, N_CHIPS),
        in_specs=[pl.BlockSpec((FF_ROWS, D_MODEL), lambda i, k: (i, 0)), w_spec, w_spec],
        out_specs=[o_spec, o_spec, o_spec], out_shape=[_sds(shape), _sds(shape), _sds(shape, BF16)],
        compiler_params=_cp(("parallel", "parallel")), name="ffn_up")(h, w_gate_t, w_up_t)


def _ffn_down_bwd(dz, w_down, a, b):
    def body(dz_ref, wd_ref, a_ref, b_ref, da_ref, db_ref):
        d_act = _dot_nt(dz_ref[...].astype(BF16), wd_ref[...])
        av = a_ref[...]
        sg = jax.nn.sigmoid(av)
        da_ref[...] = (d_act * b_ref[...] * sg * (1.0 + av * (1.0 - sg))).astype(BF16)
        db_ref[...] = (d_act * av * sg).astype(BF16)

    t_spec = pl.BlockSpec((None, FF_ROWS, FF_SHARD), lambda i, k: (k, i, 0))
    shape = (N_CHIPS, SEQ, FF_SHARD)
    return pl.pallas_call(
        body, grid=(SEQ // FF_ROWS, N_CHIPS),
        in_specs=[pl.BlockSpec((FF_ROWS, D_MODEL), lambda i, k: (i, 0)),
                  pl.BlockSpec((None, FF_SHARD, D_MODEL), lambda i, k: (k, 0, 0)), t_spec, t_spec],
        out_specs=[t_spec, t_spec], out_shape=[_sds(shape, BF16), _sds(shape, BF16)],
        compiler_params=_cp(("parallel", "parallel")), name="ffn_down_bwd")(dz, w_down, a, b)


def _ffn_dh(d_a, d_b, w_gate_t, w_up_t):
    def body(da_ref, db_ref, wg_ref, wu_ref, o_ref, acc):
        k = pl.program_id(1)
        part = (jnp.dot(da_ref[...], wg_ref[...], preferred_element_type=F32)
                + jnp.dot(db_ref[...], wu_ref[...], preferred_element_type=F32))

        @pl.when(k == 0)
        def _():
            acc[...] = part

        @pl.when(k > 0)
        def _():
            acc[...] += part

        @pl.when(k == N_CHIPS - 1)
        def _():
            o_ref[...] = acc[...]

    t_spec = pl.BlockSpec((None, FF_ROWS, FF_SHARD), lambda i, k: (k, i, 0))
    w_spec = pl.BlockSpec((None, FF_SHARD, D_MODEL), lambda i, k: (k, 0, 0))
    return pl.pallas_call(
        body, grid=(SEQ // FF_ROWS, N_CHIPS), in_specs=[t_spec, t_spec, w_spec, w_spec],
        out_specs=pl.BlockSpec((FF_ROWS, D_MODEL), lambda i, k: (i, 0)), out_shape=_sds((SEQ, D_MODEL)),
        scratch_shapes=[pltpu.VMEM((FF_ROWS, D_MODEL), F32)],
        compiler_params=_cp(("parallel", "arbitrary")), name="ffn_dh")(d_a, d_b, w_gate_t, w_up_t)


def _local_step(x, tgt, wts, small):
    s = SEQ
    cos_f, sin_s = [_to_phase_rows(t) for t in _rope_tables()]
    x = _reorder_rows(x, to_phase=True, name="phase_rows_x")
    tgt = _reorder_rows(tgt, to_phase=True, name="phase_rows_target")

    proj = _mm_cols(x, wts["w_in"], tm=1024, name="proj")

    attn, lse = _attention_fwd(proj, cos_f, sin_s)
    y_attn = _mm_cols(attn, wts["w_attn_br"], tm=s, name="y_attn")

    (abar_r, abar_i, bbar_r, bbar_i), ssm_vjp = jax.vjp(
        _ssm_discretise, small["ssm_a_re"], small["ssm_a_im"], small["ssm_log_dt"], small["ssm_b_re"], small["ssm_b_im"])
    b_in_r, b_in_i = _block_diag_in(bbar_r).astype(BF16), _block_diag_in(bbar_i).astype(BF16)
    c_out_r = _block_diag_out(small["ssm_c_re"]).astype(BF16)
    c_out_ni = _block_diag_out(-small["ssm_c_im"]).astype(BF16)
    a_r, a_i = abar_r.reshape(1, SSM_LANES), abar_i.reshape(1, SSM_LANES)
    d_skip = small["ssm_d"].reshape(1, SSM_WIDTH)

    u_f = _to_scan_rows(proj[:, 3 * QKV_WIDTH:3 * QKV_WIDTH + SSM_WIDTH])
    u_p = u_f.astype(BF16)
    y_c, h_r, h_i, e_r, e_i = _ssm_forward(u_p, b_in_r, b_in_i, c_out_r, c_out_ni, a_r, a_i)

    def gelu_fwd(yc, u, dsk):
        y = yc + dsk * u
        return y, 0.5 * y * (1.0 + jnp.tanh(GELU_C * (y + GELU_K * y * y * y)))

    y_s5, gel = _rowwise(gelu_fwd, [y_c, u_f], [d_skip], [_sds((s, SSM_WIDTH)), _sds((s, SSM_WIDTH), BF16)],
                         tm=512, name="ssm_gelu")
    glu = _mm_cols(gel, wts["w_glu"], tm=s, name="glu")

    def glu_fwd(ga, gb):
        return ga * jax.nn.sigmoid(gb)

    (y_glu,) = _rowwise(glu_fwd, [(glu, SSM_WIDTH, 0), (glu, SSM_WIDTH, 1)], [], [_sds((s, SSM_WIDTH), BF16)],
                        tm=512, name="glu_gate")
    y_glu = _from_scan_rows(y_glu)
    y_ssm = _mm_cols(y_glu, wts["w_ssm_br"], tm=s, name="y_ssm")

    gl0 = (proj, D_MODEL, (3 * QKV_WIDTH + SSM_WIDTH) // D_MODEL)
    gl1 = (proj, D_MODEL, (3 * QKV_WIDTH + SSM_WIDTH) // D_MODEL + 1)
    b_gate = small["b_gate"]

    def gate_mix(l0, l1, ya, ys, bg):
        return jax.nn.sigmoid(l0 + bg[0:1]) * ya + jax.nn.sigmoid(l1 + bg[1:2]) * ys

    (mixed,) = _rowwise(gate_mix, [gl0, gl1, y_attn, y_ssm], [b_gate], [_sds((s, D_MODEL), BF16)], tm=256,
                        name="gate_mix")
    w_out = wts["w_out"].reshape(D_MODEL, D_MODEL)
    mix_out = _mm_plain(mixed, w_out, tm=1024, tn=512, name="mix_out")

    def ln1_fwd(xv, mo, g, b):
        z = DN_ALPHA * xv + mo
        xhat, _ = _ln_stats(z)
        return z, xhat * g + b

    z1, h = _rowwise(ln1_fwd, [x, mix_out], [small["ln1_g"], small["ln1_b"]],
                     [_sds((s, D_MODEL)), _sds((s, D_MODEL))], tm=256, name="ln1")

    nf = D_FF // N_CHIPS
    w_gate_t, w_up_t, w_down = wts["w_ff_gate"], wts["w_ff_up"], wts["w_ff_down"]
    ff_a, ff_b, act = _ffn_up(h, w_gate_t, w_up_t)
    ff = _matmul(act, w_down, grid=(2, N_CHIPS),
                 a_spec=pl.BlockSpec((None, 1024, nf), lambda i, k: (k, i, 0)),
                 b_spec=pl.BlockSpec((None, nf, D_MODEL), lambda i, k: (k, 0, 0)),
                 o_spec=pl.BlockSpec((1024, D_MODEL), lambda i, k: (i, 0)),
                 out_shape=_sds((s, D_MODEL)), dims=(1, 0), k_axis=1, name="ff_down")

    def ln2_loss(hv, ffv, tg, g, b):
        z = DN_ALPHA * hv + ffv
        xhat, rstd = _ln_stats(z)
        err = xhat * g + b - tg
        d_out = err * (1.0 / D_MODEL)
        loss_rows = jnp.sum(err * err, axis=-1, keepdims=True) * (0.5 / D_MODEL)
        loss = jnp.broadcast_to(jnp.sum(loss_rows, axis=0, keepdims=True), (1, 128))
        return _ln_bwd(d_out, xhat, rstd, g), loss, _colsum(d_out * xhat), _colsum(d_out)

    dz2, loss_v, d_ln2_g, d_ln2_b = _rowwise(
        ln2_loss, [h, ff, tgt], [small["ln2_g"], small["ln2_b"]], [_sds((s, D_MODEL))],
        [_sds((1, 128)), _sds((1, D_MODEL)), _sds((1, D_MODEL))], tm=256, name="ln2_loss")

    d_a, d_b = _ffn_down_bwd(dz2, w_down, ff_a, ff_b)

    def grad_rows(lhs, rhs, name):
        return _matmul(lhs, rhs, grid=(N_CHIPS,), a_spec=pl.BlockSpec((None, s, nf), lambda k: (k, 0, 0)),
                       b_spec=pl.BlockSpec((s, D_MODEL), lambda k: (0, 0)),
                       o_spec=pl.BlockSpec((None, nf, D_MODEL), lambda k: (k, 0, 0)),
                       out_shape=_sds((N_CHIPS, nf, D_MODEL), BF16), dims=(0, 0), name=name)

    g_w_ff_down = grad_rows(act, dz2, "g_w_ff_down")
    g_w_ff_gate = grad_rows(d_a, h, "g_w_ff_gate")
    g_w_ff_up = grad_rows(d_b, h, "g_w_ff_up")
    dh_ff = _ffn_dh(d_a, d_b, w_gate_t, w_up_t)

    def ln1_bwd(dz, dff, z, g):
        xhat, rstd = _ln_stats(z)
        dh = DN_ALPHA * dz + dff
        return _ln_bwd(dh, xhat, rstd, g), _colsum(dh * xhat), _colsum(dh)

    dz1, d_ln1_g, d_ln1_b = _rowwise(ln1_bwd, [dz2, dh_ff, z1], [small["ln1_g"]], [_sds((s, D_MODEL))],
                                     [_sds((1, D_MODEL)), _sds((1, D_MODEL))], tm=256, name="ln1_bwd")
    d_mixed = _mm_plain(dz1, w_out, tm=1024, tn=512, dims=(1, 1), name="d_mixed")
    g_w_out = _mm_plain(mixed, dz1, tm=D_MODEL, tn=512, dims=(0, 0), out_dtype=BF16, name="g_w_out")
    g_w_out = g_w_out.reshape(N_CHIPS, D_MODEL // N_CHIPS, D_MODEL)

    def gate_bwd(dm, l0, l1, ya, ys, bg):
        g0 = jax.nn.sigmoid(l0 + bg[0:1])
        g1 = jax.nn.sigmoid(l1 + bg[1:2])
        dl0 = dm * ya * g0 * (1.0 - g0)
        dl1 = dm * ys * g1 * (1.0 - g1)
        return dm * g0, dm * g1, jnp.concatenate([dl0, dl1], axis=1), _colsum(dl0), _colsum(dl1)

    d_y_attn, d_y_ssm, d_gl, d_bg0, d_bg1 = _rowwise(
        gate_bwd, [d_mixed, gl0, gl1, y_attn, y_ssm], [b_gate],
        [_sds((s, D_MODEL), BF16), _sds((s, D_MODEL), BF16), _sds((s, 2 * D_MODEL), BF16)],
        [_sds((1, D_MODEL)), _sds((1, D_MODEL))], tm=256, name="gate_bwd")

    g_w_ssm_br = _mm_cols_tn(y_glu, d_y_ssm, ns=D_MODEL // N_CHIPS, name="g_w_ssm_br")
    d_y_glu = _to_scan_rows(_mm_cols_nt(d_y_ssm, wts["w_ssm_br"], tm=s, name="d_y_glu"))

    def glu_bwd(dy, ga, gb):
        sg = jax.nn.sigmoid(gb)
        return jnp.concatenate([dy * sg, dy * ga * sg * (1.0 - sg)], axis=1)

    (d_glu,) = _rowwise(glu_bwd, [d_y_glu, (glu, SSM_WIDTH, 0), (glu, SSM_WIDTH, 1)], [],
                        [_sds((s, 2 * SSM_WIDTH), BF16)], tm=512, name="glu_bwd")
    g_w_glu = _mm_cols_tn(gel, d_glu, ns=2 * SSM_WIDTH // N_CHIPS, name="g_w_glu")
    d_gel = _mm_cols_nt(d_glu, wts["w_glu"], tm=s, name="d_gel")

    def gelu_bwd(dg, y, u, dsk):
        th = jnp.tanh(GELU_C * (y + GELU_K * y * y * y))
        dy = dg * (0.5 * (1.0 + th) + 0.5 * y * (1.0 - th * th) * GELU_C * (1.0 + 3.0 * GELU_K * y * y))
        return dy, dy * dsk, _colsum(dy * u)

    d_y, d_u_skip, d_ssm_d = _rowwise(gelu_bwd, [d_gel, y_s5, u_f], [d_skip],
                                      [_sds((s, SSM_WIDTH), BF16), _sds((s, SSM_WIDTH))], [_sds((1, SSM_WIDTH))],
                                      tm=512, name="gelu_bwd")
    d_u, d_abar_r, d_abar_i, d_c_r, d_c_ni, d_bin_r, d_bin_i = _ssm_backward(
        d_y, d_u_skip, u_p, h_r, h_i, e_r, e_i, b_in_r, b_in_i, c_out_r, c_out_ni, a_r, a_i)
    d_u = _from_scan_rows(d_u)
    d_bbar_r = d_bin_r.reshape(SSM_GROUPS, SSM_GROUP, SSM_STATE).transpose(0, 2, 1)
    d_bbar_i = d_bin_i.reshape(SSM_GROUPS, SSM_GROUP, SSM_STATE).transpose(0, 2, 1)
    d_a_re, d_a_im, d_log_dt, d_b_re, d_b_im = ssm_vjp(
        (d_abar_r.reshape(SSM_GROUPS, SSM_STATE), d_abar_i.reshape(SSM_GROUPS, SSM_STATE), d_bbar_r, d_bbar_i))
    d_c_re = d_c_r.reshape(SSM_GROUPS, SSM_STATE, SSM_GROUP).transpose(0, 2, 1)
    d_c_im = -d_c_ni.reshape(SSM_GROUPS, SSM_STATE, SSM_GROUP).transpose(0, 2, 1)

    g_w_attn_br = _mm_cols_tn(attn, d_y_attn, ns=D_MODEL // N_CHIPS, name="g_w_attn_br")
    d_attn = _mm_cols_nt(d_y_attn, wts["w_attn_br"], tm=s, name="d_attn")
    dqkv = [_attention_bwd(g, proj, cos_f, sin_s, d_attn, attn, lse) for g in range(len(DILATIONS))]

    d_proj = jnp.concatenate([dqkv[g][j] for j in range(3) for g in range(len(DILATIONS))] + [d_u, d_gl],
                             axis=1)
    g_w_in = _mm_cols_tn(x, d_proj, ns=IN_WIDTH // N_CHIPS, name="g_w_in")
    dx_proj = _mm_cols_nt(d_proj, wts["w_in"], tm=1024, name="dx_proj", after=(g_w_in,))

    def dx_sum(dz, dxp):
        return DN_ALPHA * dz + dxp

    (grad_x,) = _rowwise(dx_sum, [dz1, dx_proj], [], [_sds((s, D_MODEL))], tm=512, name="grad_x")
    grad_x = _reorder_rows(grad_x, to_phase=False, name="time_rows_grad_x")

    big = {"w_in": g_w_in, "w_attn_br": g_w_attn_br, "w_ssm_br": g_w_ssm_br, "w_out": g_w_out, "w_glu": g_w_glu,
           "w_ff_gate": g_w_ff_gate, "w_ff_up": g_w_ff_up, "w_ff_down": g_w_ff_down}
    small_g = {"b_gate": jnp.concatenate([d_bg0, d_bg1], axis=0), "ssm_a_re": d_a_re, "ssm_a_im": d_a_im,
               "ssm_log_dt": d_log_dt, "ssm_b_re": d_b_re, "ssm_b_im": d_b_im, "ssm_c_re": d_c_re, "ssm_c_im": d_c_im,
               "ssm_d": d_ssm_d.reshape(SSM_WIDTH), "ln1_g": d_ln1_g, "ln1_b": d_ln1_b, "ln2_g": d_ln2_g,
               "ln2_b": d_ln2_b}
    marks = {"ln1_bwd": dz1, "scan_bwd": d_abar_r,"attention_bwd_0": dqkv[0][0], "dx_proj": dx_proj}
    return loss_v[0, 0], grad_x, big, small_g, marks


GATHER_ID, SWAP_ID, SCATTER_ID, JOIN_ID = 1, 2, 3, 4


def _place():
    return lax.axis_index("x"), lax.axis_index("y"), lax.axis_index("c")


def _other_chips(x, y):
    return [(1 - x, y), (x, 1 - y), (1 - x, 1 - y)]


def _handshake(peers):
    barrier = pltpu.get_barrier_semaphore()
    for peer in peers:
        pl.semaphore_signal(barrier, inc=1, device_id=peer, device_id_type=MESH)
    pl.semaphore_wait(barrier, len(peers))


def _sequencer(body, arrays, out_type, sems, collective_id, name):
    return pl.kernel(body, name=name, out_type=out_type,
                     mesh=plsc.ScalarSubcoreMesh(axis_name="sequencer", num_cores=1), scratch_types=sems,
                     compiler_params=pltpu.CompilerParams(collective_id=collective_id))(*arrays)


def _gather_weights(shards, *, name):
    nw = len(shards)

    def body(*refs):
        ins, outs = refs[:nw], refs[nw:2 * nw]
        send_sems, recv_sems, pass_send, pass_recv, local_sems = refs[2 * nw:]
        x, y, c = _place()
        chip = 2 * x + y
        chips = _other_chips(x, y)
        _handshake([(x, y, 1 - c)] + [(cx, cy, c) for cx, cy in chips])
        started = []
        for w in range(nw):
            hw = shards[w].shape[0] // 2
            mine = pl.ds(c * hw, hw)
            own = pltpu.make_async_copy(ins[w], outs[w].at[chip], local_sems.at[w])
            own.start()
            started.append(own)
            for j, (cx, cy) in enumerate(chips):
                cp = pltpu.make_async_remote_copy(
                    src_ref=ins[w].at[mine], dst_ref=outs[w].at[chip, mine], send_sem=send_sems.at[w, j],
                    recv_sem=recv_sems.at[w, j], device_id=(cx, cy, c), device_id_type=MESH)
                cp.start()
                started.append(cp)
        passed = []
        for w in range(nw):
            hw = shards[w].shape[0] // 2
            mine = pl.ds(c * hw, hw)
            for j, (cx, cy) in enumerate(chips):
                landed = outs[w].at[2 * cx + cy, mine]
                pltpu.make_async_remote_copy(
                    src_ref=ins[w].at[mine], dst_ref=landed, send_sem=send_sems.at[w, j],
                    recv_sem=recv_sems.at[w, j], device_id=(cx, cy, c), device_id_type=MESH).wait_recv()
                cp = pltpu.make_async_remote_copy(
                    src_ref=landed, dst_ref=landed, send_sem=pass_send.at[w, j], recv_sem=pass_recv.at[w, j],
                    device_id=(x, y, 1 - c), device_id_type=MESH)
                cp.start()
                passed.append(cp)
        for w in range(nw):
            hw = shards[w].shape[0] // 2
            theirs = pl.ds((1 - c) * hw, hw)
            for j, (cx, cy) in enumerate(chips):
                landed = outs[w].at[2 * cx + cy, theirs]
                pltpu.make_async_remote_copy(
                    src_ref=landed, dst_ref=landed, send_sem=pass_send.at[w, j], recv_sem=pass_recv.at[w, j],
                    device_id=(x, y, 1 - c), device_id_type=MESH).wait_recv()
        for cp in started[0::4]:
            cp.wait()
        for cp in [s for i, s in enumerate(started) if i % 4] + passed:
            cp.wait_send()

    sem = pltpu.SemaphoreType.DMA
    return _sequencer(body, shards, [_sds((N_CHIPS,) + a.shape, a.dtype) for a in shards],
                      [sem((nw, 3)), sem((nw, 3)), sem((nw, 3)), sem((nw, 3)), sem((nw,))], GATHER_ID, name)


def _swap_other_halves(grads, *, name):
    nw = len(grads)

    def body(*refs):
        ins, outs = refs[:nw], refs[nw:2 * nw]
        send_sems, recv_sems = refs[2 * nw:]
        x, y, c = _place()
        _handshake([(x, y, 1 - c)])
        cps = []
        for w in range(nw):
            hw = grads[w].shape[1] // 2
            cp = pltpu.make_async_remote_copy(
                src_ref=ins[w].at[:, pl.ds((1 - c) * hw, hw)], dst_ref=outs[w], send_sem=send_sems.at[w],
                recv_sem=recv_sems.at[w], device_id=(x, y, 1 - c), device_id_type=MESH)
            cp.start()
            cps.append(cp)
        for cp in cps:
            cp.wait()

    sem = pltpu.SemaphoreType.DMA
    return _sequencer(body, grads, [_sds((N_CHIPS, g.shape[1] // 2, g.shape[2]), g.dtype) for g in grads],
                      [sem((nw,)), sem((nw,))], SWAP_ID, name)


def _add_my_half(core, g, other, after=()):
    n, r, cols = g.shape
    hw = r // 2

    def body(core_ref, g_ref, o_ref, *rest):
        out_ref = rest[len(after)]
        out_ref[...] = (g_ref[...].astype(F32) + o_ref[...].astype(F32)).astype(out_ref.dtype)

    return pl.pallas_call(
        body,
        grid_spec=pltpu.PrefetchScalarGridSpec(
            num_scalar_prefetch=1, grid=(n,),
            in_specs=[pl.BlockSpec((None, None, hw, cols), lambda s, core_ref: (s, core_ref[0], 0, 0)),
                      pl.BlockSpec((None, hw, cols), lambda s, core_ref: (s, 0, 0))] + [HBM_OPERAND] * len(after),
            out_specs=pl.BlockSpec((None, hw, cols), lambda s, core_ref: (s, 0, 0))),
        out_shape=_sds((n, hw, cols), BF16), compiler_params=_cp(("parallel",)),
        name="add_my_half")(core, g.reshape(n, 2, hw, cols), other, *after)


def _scatter_partials(parts, *, name):
    nw = len(parts)

    def body(*refs):
        ins, outs = refs[:nw], refs[nw:2 * nw]
        send_sems, recv_sems = refs[2 * nw:]
        x, y, c = _place()
        _handshake([(cx, cy, c) for cx, cy in _other_chips(x, y)])
        cps = []
        for w in range(nw):
            for j, (cx, cy) in enumerate(_other_chips(x, y)):
                cp = pltpu.make_async_remote_copy(
                    src_ref=ins[w].at[2 * cx + cy], dst_ref=outs[w].at[j], send_sem=send_sems.at[w, j],
                    recv_sem=recv_sems.at[w, j], device_id=(cx, cy, c), device_id_type=MESH)
                cp.start()
                cps.append(cp)
        for cp in cps:
            cp.wait()

    sem = pltpu.SemaphoreType.DMA
    return _sequencer(body, parts, [_sds((3,) + p.shape[1:], p.dtype) for p in parts],
                      [sem((nw, 3)), sem((nw, 3))], SCATTER_ID, name)


def _sum_partials(chip, part, recv, after=()):
    _, hw, cols = part.shape
    th = hw // 2 if hw % 32 == 0 else hw

    def body(chip_ref, p_ref, r_ref, *rest):
        out_ref = rest[len(after)]
        acc = p_ref[...].astype(F32)
        for j in range(3):
            acc = acc + r_ref[j].astype(F32)
        out_ref[...] = acc

    return pl.pallas_call(
        body,
        grid_spec=pltpu.PrefetchScalarGridSpec(
            num_scalar_prefetch=1, grid=(hw // th,),
            in_specs=[pl.BlockSpec((None, th, cols), lambda i, chip_ref: (chip_ref[0], i, 0)),
                      pl.BlockSpec((3, th, cols), lambda i, chip_ref: (0, i, 0))] + [HBM_OPERAND] * len(after),
            out_specs=pl.BlockSpec((th, cols), lambda i, chip_ref: (i, 0))),
        out_shape=_sds((hw, cols)), compiler_params=_cp(("parallel",)), name="sum_partials")(
            chip, part, recv, *after)


def _swap_reduced_halves(halves, *, name):
    nw = len(halves)

    def body(*refs):
        ins, outs = refs[:nw], refs[nw:2 * nw]
        send_sems, recv_sems = refs[2 * nw:]
        x, y, c = _place()
        _handshake([(x, y, 1 - c)])
        cps = []
        for w in range(nw):
            cp = pltpu.make_async_remote_copy(
                src_ref=ins[w], dst_ref=outs[w], send_sem=send_sems.at[w], recv_sem=recv_sems.at[w],
                device_id=(x, y, 1 - c), device_id_type=MESH)
            cp.start()
            cps.append(cp)
        for cp in cps:
            cp.wait()

    sem = pltpu.SemaphoreType.DMA
    return _sequencer(body, halves, [_sds(h.shape, h.dtype) for h in halves], [sem((nw,)), sem((nw,))], JOIN_ID, name)


def _allreduce_rows(vec, *, name, after=()):
    rows = vec.shape[0]

    def body(v_ref, *rest):
        out_ref, slots, send_sems, recv_sems = rest[len(after):]
        x, y, c = _place()
        me = 4 * x + 2 * y + c
        slots[me] = v_ref[...]
        peers = []
        for mask in range(1, N_DEV):
            px = 1 - x if mask & 4 else x
            py = 1 - y if mask & 2 else y
            pc = 1 - c if mask & 1 else c
            peers.append((px, py, pc))
        cps = []
        for k, peer in enumerate(peers):
            cp = pltpu.make_async_remote_copy(
                src_ref=v_ref, dst_ref=slots.at[me], send_sem=send_sems.at[k], recv_sem=recv_sems.at[k],
                device_id=peer, device_id_type=MESH)
            cp.start()
            cps.append(cp)
        for k, (px, py, pc) in enumerate(peers):
            pltpu.make_async_remote_copy(
                src_ref=v_ref, dst_ref=slots.at[4 * px + 2 * py + pc], send_sem=send_sems.at[k],
                recv_sem=recv_sems.at[k], device_id=(px, py, pc), device_id_type=MESH).wait_recv()
        for cp in cps:
            cp.wait_send()
        acc = slots[0]
        for d in range(1, N_DEV):
            acc = acc + slots[d]
        out_ref[...] = acc

    vmem = pl.BlockSpec(memory_space=pltpu.VMEM)
    return pl.pallas_call(
        body, in_specs=[vmem] + [HBM_OPERAND] * len(after), out_specs=vmem, out_shape=_sds((rows, 128)),
        scratch_shapes=[pltpu.VMEM((N_DEV, rows, 128), F32), pltpu.SemaphoreType.DMA((N_DEV - 1,)),
                        pltpu.SemaphoreType.DMA((N_DEV - 1,))],
        compiler_params=pltpu.CompilerParams(vmem_limit_bytes=VMEM_LIMIT_BYTES), name=name)(vec, *after)


def _reduce_scatter_start(grads, core, *, tag, add_after=()):
    others = _swap_other_halves(grads, name="swap_other_halves_" + tag)
    parts = [_add_my_half(core, g, o, add_after) for g, o in zip(grads, others)]
    return parts, _scatter_partials(parts, name="scatter_partials_" + tag)


def _reduce_scatter_finish(parts, recvd, chip, *, tag, sum_after=()):
    mine = [_sum_partials(chip, p, r, sum_after) for p, r in zip(parts, recvd)]
    return mine, _swap_reduced_halves(mine, name="swap_reduced_halves_" + tag)


ADAM_BLOCK_ELEMS = 256 * 1024


def _adam_rows(rows, cols):
    tm = rows
    while tm * cols > ADAM_BLOCK_ELEMS and tm % 16 == 0:
        tm //= 2
    return tm


def _adam_step(wv, gv, mv, vv):
    m2 = ADAM_B1 * mv + (1.0 - ADAM_B1) * gv
    v2 = ADAM_B2 * vv + (1.0 - ADAM_B2) * (gv * gv)
    m_hat = m2 / (1.0 - ADAM_B1 ** ADAM_STEP)
    v_hat = v2 / (1.0 - ADAM_B2 ** ADAM_STEP)
    return -ADAM_LR * (m_hat / (jnp.sqrt(v_hat) + ADAM_EPS) + ADAM_WD * wv), m2, v2


def _adamw(w, g, m, v, *, name):
    rows, cols = w.shape
    return _rowwise(_adam_step, [w, g, m, v], [], [_sds((rows, cols))] * 3, tm=_adam_rows(rows, cols), name=name)


def _adamw_halves(core, w, g_mine, g_theirs, m, v, *, name, after=()):
    rows, cols = w.shape
    hw = rows // 2
    tm = _adam_rows(hw, cols)
    per_half = hw // tm

    def body(core_ref, w_ref, gm_ref, gt_ref, m_ref, v_ref, *rest):
        g_out, d_out, m_out, v_out = rest[len(after):]
        mine = (pl.program_id(0) // per_half) == core_ref[0]
        g = jnp.where(mine, gm_ref[...], gt_ref[...])
        d, m2, v2 = _adam_step(w_ref[...], g, m_ref[...], v_ref[...])
        g_out[...] = g
        d_out[...] = d
        m_out[...] = m2
        v_out[...] = v2

    full = pl.BlockSpec((tm, cols), lambda i, core_ref: (i, 0))
    half = pl.BlockSpec((tm, cols), lambda i, core_ref: (i % per_half, 0))
    return pl.pallas_call(
        body,
        grid_spec=pltpu.PrefetchScalarGridSpec(
            num_scalar_prefetch=1, grid=(rows // tm,),
            in_specs=[full, half, half, full, full] + [HBM_OPERAND] * len(after), out_specs=[full, full, full, full]),
        out_shape=[_sds((rows, cols))] * 4, compiler_params=_cp(("parallel",)), name=name)(
            core, w, g_mine, g_theirs, m, v, *after)


HELD_TRANSPOSED = ("w_ff_gate", "w_ff_up")


def _as_rows(name, arr):
    return arr[0].T if name in HELD_TRANSPOSED else arr[0]


def _from_rows(name, arr2d):
    return (arr2d.T if name in HELD_TRANSPOSED else arr2d)[None]


STORED_SWAPPED = ("ssm_b_re", "ssm_b_im")


def _as_stored(name, arr):
    return jnp.swapaxes(arr, -1, -2) if name in STORED_SWAPPED else arr


def _pack_rows(arrs):
    flat = jnp.concatenate([a.reshape(-1).astype(F32) for a in arrs])
    rows = -(-flat.shape[0] // 1024) * 8
    return jnp.pad(flat, (0, rows * 128 - flat.shape[0])).reshape(rows, 128)


def _unpack_rows(vec, shapes):
    flat = vec.reshape(-1)
    out, off = [], 0
    for shp in shapes:
        size = math.prod(shp)
        out.append(flat[off:off + size].reshape(shp))
        off += size
    return out


SMALL = ("b_gate", "ssm_a_re", "ssm_a_im", "ssm_log_dt", "ssm_b_re", "ssm_b_im", "ssm_c_re", "ssm_c_im", "ssm_d",
         "ln1_g", "ln1_b", "ln2_g", "ln2_b")
GATHER_GROUPS = (("w_in", ("w_in",)), ("mixer", ("w_attn_br", "w_ssm_br", "w_glu", "w_out")),
                 ("ffn", ("w_ff_gate", "w_ff_up", "w_ff_down")))
REDUCE_GROUPS = (("ffn", ("w_ff_down", "w_ff_gate", "w_ff_up")),
                 ("mixer", ("w_out", "w_ssm_br", "w_glu", "w_attn_br")), ("w_in", ("w_in",)))
WEIGHTS = ("w_in", "b_gate", "w_attn_br", "w_ssm_br", "w_out", "ssm_a_re", "ssm_a_im", "ssm_log_dt", "ssm_b_re",
           "ssm_b_im", "ssm_c_re", "ssm_c_im", "ssm_d", "w_glu", "ln1_g", "ln1_b", "w_ff_gate", "w_ff_up", "w_ff_down",
           "ln2_g", "ln2_b")


def kernel(x, w_in, b_gate, w_attn_br, w_ssm_br, w_out, ssm_a_re, ssm_a_im, ssm_log_dt, ssm_b_re, ssm_b_im, ssm_c_re, ssm_c_im, ssm_d, w_glu, ln1_g, ln1_b, w_ff_gate, w_ff_up, w_ff_down, ln2_g, ln2_b, loss_target, m_w_in, m_b_gate, m_w_attn_br, m_w_ssm_br, m_w_out, m_ssm_a_re, m_ssm_a_im, m_ssm_log_dt, m_ssm_b_re, m_ssm_b_im, m_ssm_c_re, m_ssm_c_im, m_ssm_d, m_w_glu, m_ln1_g, m_ln1_b, m_w_ff_gate, m_w_ff_up, m_w_ff_down, m_ln2_g, m_ln2_b, v_w_in, v_b_gate, v_w_attn_br, v_w_ssm_br, v_w_out, v_ssm_a_re, v_ssm_a_im, v_ssm_log_dt, v_ssm_b_re, v_ssm_b_im, v_ssm_c_re, v_ssm_c_im, v_ssm_d, v_w_glu, v_ln1_g, v_ln1_b, v_w_ff_gate, v_w_ff_up, v_w_ff_down, v_ln2_g, v_ln2_b):
    given = dict(locals())
    px, py, pc = _place()
    chip = 2 * px + py
    core_s = jnp.reshape(pc, (1,)).astype(jnp.int32)
    chip_s = jnp.reshape(chip, (1,)).astype(jnp.int32)

    wts = {}
    for tag, names in GATHER_GROUPS:
        wts.update(zip(names, _gather_weights([_as_rows(n, given[n]).astype(BF16) for n in names],
                                              name="gather_" + tag)))
    ncol = D_MODEL // N_CHIPS
    bg_mine = jnp.where(pc == 0, b_gate[0], jnp.zeros_like(b_gate[0]))
    bg_full = lax.dynamic_update_slice(jnp.zeros((2, D_MODEL), F32), bg_mine, (0, chip * ncol))
    bg_full = _allreduce_rows(bg_full.reshape(16, 128), name="gather_gate_bias").reshape(2, D_MODEL)
    small = {n: given[n][0] for n in SMALL if n.startswith("ssm")}
    small.update({n: given[n] for n in ("ln1_g", "ln1_b", "ln2_g", "ln2_b")})
    small["b_gate"] = bg_full

    loss_mine, grad_x, big_g, small_g, marks = _local_step(x[0], loss_target[0], wts, small)
    loss = lax.psum(loss_mine, ("x", "y", "c"))

    groups = dict(REDUCE_GROUPS)
    add_after = {"ffn": (marks["ln1_bwd"],), "mixer": (marks["scan_bwd"],), "w_in": (marks["dx_proj"],)}
    parts, recvd = {}, {}
    for tag, names in REDUCE_GROUPS:
        parts[tag], recvd[tag] = _reduce_scatter_start([big_g[n] for n in names], core_s, tag=tag,
                                                       add_after=add_after[tag])
    grads, delta, new_m, new_v = {}, {}, {}, {}

    def finish(tag, sum_after, adam_after):
        mine, theirs = _reduce_scatter_finish(parts[tag], recvd[tag], chip_s, tag=tag, sum_after=sum_after)
        for n, g_mine, g_theirs in zip(groups[tag], mine, theirs):
            res = _adamw_halves(core_s, _as_rows(n, given[n]), g_mine, g_theirs, _as_rows(n, given["m_" + n]),
                                _as_rows(n, given["v_" + n]), name="adamw_" + n, after=adam_after)
            grads[n], delta[n], new_m[n], new_v[n] = [_from_rows(n, r) for r in res]

    in_flight = (parts["w_in"][0],)
    finish("ffn", (marks["scan_bwd"],), (marks["attention_bwd_0"],))
    finish("mixer", (marks["attention_bwd_0"],), in_flight)
    stored = [_as_stored(n, small_g[n]) for n in SMALL]
    summed = _unpack_rows(_allreduce_rows(_pack_rows(stored), name="allreduce_small", after=in_flight),
                          [a.shape for a in stored])
    for n, g in zip(SMALL, summed):
        g = _as_stored(n, g)
        if n == "b_gate":
            g = lax.dynamic_slice(g, (0, chip * ncol), (2, ncol))
        grads[n] = g.reshape(given[n].shape)
    packed = [_pack_rows([_as_stored(n, src[n]) for n in SMALL]) for src in
              (given, grads, {n: given["m_" + n] for n in SMALL}, {n: given["v_" + n] for n in SMALL})]
    shapes = [_as_stored(n, given[n]).shape for n in SMALL]
    small_out = _adamw(*packed, name="adamw_small")
    for out, vec in zip((delta, new_m, new_v), small_out):
        out.update((n, _as_stored(n, a)) for n, a in zip(SMALL, _unpack_rows(vec, shapes)))
    behind = [delta[n] for tag in ("ffn", "mixer") for n in groups[tag]] + [small_out[0], grad_x]
    finish("w_in", tuple(behind), ())

    return (loss, grad_x.reshape(x.shape), *[grads[n] for n in WEIGHTS], *[delta[n] for n in WEIGHTS],
            *[new_m[n] for n in WEIGHTS], *[new_v[n] for n in WEIGHTS])
```

```python
import math

import jax
import jax.numpy as jnp
from jax import lax
from jax.experimental import pallas as pl
from jax.experimental.pallas import tpu as pltpu
from jax.experimental.pallas import tpu_sc as plsc

F32 = jnp.float32
BF16 = jnp.bfloat16
MESH = pl.DeviceIdType.MESH

D_MODEL = 1024
SEQ = 2048
HEAD_DIM = 64
ATTN_HEADS = 8
DILATIONS = (1, 4, 16)
ATTN_WIDTH = ATTN_HEADS * HEAD_DIM
QKV_WIDTH = 3 * ATTN_WIDTH
BLOCK = 128
ROPE_THETA = 10000.0
NEG_INF = -1e30
SSM_GROUP = 16
SSM_GROUPS = 32
SSM_WIDTH = 512
SSM_STATE = 64
SSM_LANES = SSM_GROUPS * SSM_STATE
SCAN_CHUNKS = 8
SCAN_STEPS = SEQ // SCAN_CHUNKS
IN_WIDTH = 3 * QKV_WIDTH + SSM_WIDTH + 2 * D_MODEL
D_FF = 2816
N_CHIPS = 4
N_DEV = 8
DN_ALPHA = 2.0 ** 0.25
LN_EPS = 1e-5
ADAM_LR = 0.001
ADAM_B1 = 0.9
ADAM_B2 = 0.999
ADAM_EPS = 1e-08
ADAM_WD = 0.01
ADAM_STEP = 10
GELU_C = math.sqrt(2.0 / math.pi)
GELU_K = 0.044715

VMEM_LIMIT_BYTES = 56 * 1024 * 1024


def _sds(shape, dtype=F32):
    return jax.ShapeDtypeStruct(tuple(shape), dtype)


def _cp(semantics=None):
    return pltpu.CompilerParams(dimension_semantics=semantics, vmem_limit_bytes=VMEM_LIMIT_BYTES)


HBM_OPERAND = pl.BlockSpec(memory_space=pl.ANY)


def _matmul(a, b, *, grid, a_spec, b_spec, o_spec, out_shape, dims, k_axis=None, name, after=()):
    nk = grid[k_axis] if k_axis is not None else 1
    o_block = tuple(d for d in o_spec.block_shape if d is not None)
    n_after = len(after)

    def body(a_ref, b_ref, *rest):
        o_ref, acc = rest[n_after], rest[n_after + 1:]
        part = lax.dot_general(a_ref[...].astype(BF16), b_ref[...].astype(BF16),
                               (((dims[0],), (dims[1],)), ((), ())), preferred_element_type=F32)
        if k_axis is None:
            o_ref[...] = part.astype(o_ref.dtype)
        else:
            k = pl.program_id(k_axis)

            @pl.when(k == 0)
            def _():
                acc[0][...] = part

            @pl.when(k > 0)
            def _():
                acc[0][...] += part

            @pl.when(k == nk - 1)
            def _():
                o_ref[...] = acc[0][...].astype(o_ref.dtype)

    sem = tuple("arbitrary" if ax == k_axis else "parallel" for ax in range(len(grid)))
    return pl.pallas_call(
        body, grid=grid, in_specs=[a_spec, b_spec] + [HBM_OPERAND] * n_after, out_specs=o_spec, out_shape=out_shape,
        scratch_shapes=[pltpu.VMEM(o_block, F32)] if k_axis is not None else [],
        compiler_params=_cp(sem), name=name)(a, b, *after)


def _mm_cols(a, wg, *, tm, name, out_dtype=F32, out3d=False):
    m, k = a.shape
    ns = wg.shape[2]
    if out3d:
        o_spec = pl.BlockSpec((None, tm, ns), lambda i, s: (s, i, 0))
        out_shape = _sds((N_CHIPS, m, ns), out_dtype)
    else:
        o_spec = pl.BlockSpec((tm, ns), lambda i, s: (i, s))
        out_shape = _sds((m, N_CHIPS * ns), out_dtype)
    return _matmul(a, wg, grid=(m // tm, N_CHIPS),
                   a_spec=pl.BlockSpec((tm, k), lambda i, s: (i, 0)),
                   b_spec=pl.BlockSpec((None, k, ns), lambda i, s: (s, 0, 0)),
                   o_spec=o_spec, out_shape=out_shape, dims=(1, 0), name=name)


def _mm_cols_nt(dy, wg, *, tm, name, dy3d=False, out_dtype=F32, after=()):
    k, ns = wg.shape[1], wg.shape[2]
    if dy3d:
        m = dy.shape[1]
        a_spec = pl.BlockSpec((None, tm, ns), lambda i, s: (s, i, 0))
    else:
        m = dy.shape[0]
        a_spec = pl.BlockSpec((tm, ns), lambda i, s: (i, s))
    return _matmul(dy, wg, grid=(m // tm, N_CHIPS), a_spec=a_spec,
                   b_spec=pl.BlockSpec((None, k, ns), lambda i, s: (s, 0, 0)),
                   o_spec=pl.BlockSpec((tm, k), lambda i, s: (i, 0)),
                   out_shape=_sds((m, k), out_dtype), dims=(1, 1), k_axis=1, name=name, after=after)


def _mm_cols_tn(a, dy, *, ns, name, after=()):
    m, k = a.shape
    return _matmul(a, dy, grid=(N_CHIPS,), a_spec=pl.BlockSpec((m, k), lambda s: (0, 0)),
                   b_spec=pl.BlockSpec((m, ns), lambda s: (0, s)),
                   o_spec=pl.BlockSpec((None, k, ns), lambda s: (s, 0, 0)),
                   out_shape=_sds((N_CHIPS, k, ns), BF16), dims=(0, 0), name=name, after=after)


def _mm_plain(a, b, *, tm, tn, name, out_dtype=F32, dims=(1, 0), tk=None):
    m = a.shape[1 - dims[0]]
    kk = a.shape[dims[0]]
    n = b.shape[1 - dims[1]]
    tk = kk if tk is None else tk
    nk = kk // tk

    def a_idx(i, j, k):
        return (i, k) if dims[0] == 1 else (k, i)

    def b_idx(i, j, k):
        return (k, j) if dims[1] == 0 else (j, k)

    a_blk = (tm, tk) if dims[0] == 1 else (tk, tm)
    b_blk = (tk, tn) if dims[1] == 0 else (tn, tk)
    return _matmul(a, b, grid=(m // tm, n // tn, nk),
                   a_spec=pl.BlockSpec(a_blk, a_idx), b_spec=pl.BlockSpec(b_blk, b_idx),
                   o_spec=pl.BlockSpec((tm, tn), lambda i, j, k: (i, j)),
                   out_shape=_sds((m, n), out_dtype), dims=dims, k_axis=2 if nk > 1 else None, name=name)


def _rowwise(fn, tiled, full, outs, accs=(), *, tm, name, after=()):
    args, in_specs = [], []
    for t in tiled:
        if isinstance(t, tuple):
            arr, w, cb = t
            in_specs.append(pl.BlockSpec((tm, w), lambda i, cb=cb: (i, cb)))
        else:
            arr = t
            in_specs.append(pl.BlockSpec((tm, arr.shape[1]), lambda i: (i, 0)))
        args.append(arr)
    rows = args[0].shape[0]
    for f in full:
        in_specs.append(pl.BlockSpec(f.shape, lambda i, nd=f.ndim: (0,) * nd))
        args.append(f)
    out_specs = [pl.BlockSpec((tm, o.shape[1]), lambda i: (i, 0)) for o in outs]
    out_specs += [pl.BlockSpec(a.shape, lambda i, nd=len(a.shape): (0,) * nd) for a in accs]
    n_in, n_out = len(args), len(outs)
    in_specs += [HBM_OPERAND] * len(after)
    first_out = n_in + len(after)

    def body(*refs):
        res = fn(*[r[...] for r in refs[:n_in]])
        res = res if isinstance(res, (tuple, list)) else (res,)
        for r, v in zip(refs[first_out:first_out + n_out], res[:n_out]):
            r[...] = v.astype(r.dtype)
        i = pl.program_id(0)
        for r, v in zip(refs[first_out + n_out:], res[n_out:]):
            @pl.when(i == 0)
            def _(r=r, v=v):
                r[...] = v

            @pl.when(i > 0)
            def _(r=r, v=v):
                r[...] += v

    res = pl.pallas_call(
        body, grid=(rows // tm,), in_specs=in_specs, out_specs=out_specs, out_shape=list(outs) + list(accs),
        compiler_params=_cp(("arbitrary",) if accs else ("parallel",)), name=name)(*args, *after)
    return res


def _colsum(v):
    return jnp.sum(v, axis=0, keepdims=True)


def _ln_stats(z):
    mu = jnp.mean(z, axis=-1, keepdims=True)
    zc = z - mu
    var = jnp.mean(zc * zc, axis=-1, keepdims=True)
    rstd = lax.rsqrt(var + LN_EPS)
    return zc * rstd, rstd


def _ln_bwd(dy, xhat, rstd, g):
    dxh = dy * g
    m1 = jnp.mean(dxh, axis=-1, keepdims=True)
    m2 = jnp.mean(dxh * xhat, axis=-1, keepdims=True)
    return rstd * (dxh - m1 - xhat * m2)


def _swap_halves(t):
    w = t.shape[-1]
    lane = lax.broadcasted_iota(jnp.int32, t.shape, t.ndim - 1)
    return jnp.where((lane % HEAD_DIM) < HEAD_DIM // 2, pltpu.roll(t, w - HEAD_DIM // 2, t.ndim - 1),
                     pltpu.roll(t, HEAD_DIM // 2, t.ndim - 1))


PHASES = max(DILATIONS)
PAIR = 2 * HEAD_DIM
UNITS = SEQ // BLOCK
UNIT_UNROLL = 4
ROPE_ROWS = 256


def _to_phase_rows(t):
    return t.reshape(SEQ // PHASES, PHASES, t.shape[1]).transpose(1, 0, 2).reshape(t.shape)


def _reorder_rows(arr, *, to_phase, name):
    def body(i_ref, o_ref):
        for rho in range(PHASES):
            phase = pl.ds(rho * BLOCK, BLOCK)
            strided = pl.ds(rho, BLOCK, stride=PHASES)
            if to_phase:
                o_ref[phase, :] = i_ref[strided, :]
            else:
                o_ref[strided, :] = i_ref[phase, :]

    spec = pl.BlockSpec((SEQ, BLOCK), lambda j: (0, j))
    return pl.pallas_call(body, grid=(arr.shape[1] // BLOCK,), in_specs=[spec], out_specs=spec,
                          out_shape=_sds(arr.shape), compiler_params=_cp(("parallel",)), name=name)(arr)


def _rope(t, cf, ss):
    return t * cf + _swap_halves(t) * ss


def _rope_transposed(d, cf, ss):
    return d * cf + _swap_halves(d * ss)


def _unit_pieces(u, dil):
    pieces, length = PHASES // dil, 8 * dil
    if dil == 1:
        rho, i = 0, u
    elif dil == PHASES:
        rho, i = u, 0
    else:
        rho, i = jnp.bitwise_and(u, dil - 1), jnp.right_shift(u, dil.bit_length() - 1)
    before = jnp.maximum(i - 1, 0)
    cur = [pl.multiple_of((rho + dil * k) * BLOCK + length * i, 8) for k in range(pieces)]
    prev = [pl.multiple_of((rho + dil * k) * BLOCK + length * before, 8) for k in range(pieces)]
    return i, cur, prev


def _load_tile(ref, starts, dil):
    return jnp.concatenate([ref[pl.ds(st, 8 * dil), :] for st in starts], axis=0)


def _store_tile(ref, starts, dil, val, head=None, accumulate=False):
    length = 8 * dil
    lanes = slice(None) if head is None else pl.ds(head * HEAD_DIM, HEAD_DIM)
    cols = slice(None) if head is None else slice(head * HEAD_DIM, (head + 1) * HEAD_DIM)
    for k, st in enumerate(starts):
        piece = val[k * length:(k + 1) * length, cols]
        if accumulate:
            ref[pl.ds(st, length), lanes] += piece
        else:
            ref[pl.ds(st, length), lanes] = piece


def _tile_position(idx, dil):
    pieces, length = PHASES // dil, 8 * dil
    return pieces * jnp.bitwise_and(idx, length - 1) + jnp.right_shift(idx, length.bit_length() - 1)


def _band_mask(i, dil):
    row = lax.broadcasted_iota(jnp.int32, (BLOCK, 2 * BLOCK), 0)
    col = lax.broadcasted_iota(jnp.int32, (BLOCK, 2 * BLOCK), 1)
    key_pos = _tile_position(jnp.bitwise_and(col, BLOCK - 1), dil) + jnp.where(col >= BLOCK, 0, -BLOCK)
    dist = _tile_position(row, dil) - key_pos
    return (dist >= 0) & (dist <= BLOCK) & ((col >= BLOCK) | (i > 0))


def _causal_mask():
    row = lax.broadcasted_iota(jnp.int32, (BLOCK, BLOCK), 0)
    col = lax.broadcasted_iota(jnp.int32, (BLOCK, BLOCK), 1)
    return row >= col


def _pair_views(col0):
    return [pl.BlockSpec((SEQ, PAIR), lambda hp, g=g: (0, col0 // PAIR + g * (ATTN_WIDTH // PAIR) + hp))
            for g in range(len(DILATIONS))]


def _rotate(in_refs, out_refs, cf_ref, ss_ref, scale):
    def step(t, carry):
        rows = pl.ds(pl.multiple_of(t * ROPE_ROWS, ROPE_ROWS), ROPE_ROWS)
        cf, ss = cf_ref[rows, :] * scale, ss_ref[rows, :] * scale
        for i_ref, o_ref in zip(in_refs, out_refs):
            o_ref[rows, :] = _rope(i_ref[rows, :], cf, ss)
        return carry

    lax.fori_loop(0, SEQ // ROPE_ROWS, step, 0)


def _attention_fwd(proj, cos_f, sin_s):
    ng = len(DILATIONS)

    def body(*refs):
        q_refs, k_refs, v_refs = refs[:ng], refs[ng:2 * ng], refs[2 * ng:3 * ng]
        cf_ref, ss_ref, attn_ref, lse_ref = refs[3 * ng:3 * ng + 4]
        scratch = refs[3 * ng + 4:]
        qr_refs, kr_refs = scratch[:ng], scratch[ng:]
        _rotate(q_refs, qr_refs, cf_ref, ss_ref, 1.0 / math.sqrt(HEAD_DIM))
        _rotate(k_refs, kr_refs, cf_ref, ss_ref, 1.0)
        first = lax.broadcasted_iota(jnp.int32, (BLOCK, PAIR), 1) < HEAD_DIM
        for g, dil in enumerate(DILATIONS):
            two_blocks = SEQ // dil > BLOCK

            def unit(u, carry, g=g, dil=dil, two_blocks=two_blocks):
                i, rows, prev = _unit_pieces(u, dil)
                qq = _load_tile(qr_refs[g], rows, dil).astype(BF16)
                kk = _load_tile(kr_refs[g], rows, dil)
                vv = _load_tile(v_refs[g], rows, dil)
                if two_blocks:
                    kk = jnp.concatenate([_load_tile(kr_refs[g], prev, dil), kk], axis=0)
                    vv = jnp.concatenate([_load_tile(v_refs[g], prev, dil), vv], axis=0)
                    valid = _band_mask(i, dil)
                else:
                    valid = _causal_mask()
                kk, vv = kk.astype(BF16), vv.astype(BF16)
                zero = jnp.zeros_like(qq)
                outs, lses = [], []
                for qh in (jnp.where(first, qq, zero), jnp.where(first, zero, qq)):
                    s = jnp.where(valid, _dot_nt(qh, kk), NEG_INF)
                    m = jnp.max(s, axis=1, keepdims=True)
                    p = jnp.exp(s - m)
                    l = jnp.sum(p, axis=1, keepdims=True)
                    outs.append(jnp.dot(p.astype(BF16), vv, preferred_element_type=F32) * (1.0 / l))
                    lses.append(m + jnp.log(l))
                o = jnp.where(first, outs[0], outs[1])
                lse = jnp.where(first, lses[0], lses[1])
                if g > 0:
                    lse_old = _load_tile(lse_ref, rows, dil)
                    m = jnp.maximum(lse_old, lse)
                    lse_new = m + jnp.log(jnp.exp(lse_old - m) + jnp.exp(lse - m))
                    o = _load_tile(attn_ref, rows, dil) * jnp.exp(lse_old - lse_new) + o * jnp.exp(lse - lse_new)
                    lse = lse_new
                _store_tile(attn_ref, rows, dil, o)
                _store_tile(lse_ref, rows, dil, lse)
                return carry

            lax.fori_loop(0, UNITS, unit, 0, unroll=UNIT_UNROLL)

    whole = pl.BlockSpec((SEQ, PAIR), lambda hp: (0, 0))
    out = pl.BlockSpec((SEQ, PAIR), lambda hp: (0, hp))
    return pl.pallas_call(
        body, grid=(ATTN_WIDTH // PAIR,),
        in_specs=_pair_views(0) + _pair_views(QKV_WIDTH) + _pair_views(2 * QKV_WIDTH) + [whole, whole],
        out_specs=[out, out], out_shape=[_sds((SEQ, ATTN_WIDTH)), _sds((SEQ, ATTN_WIDTH))],
        scratch_shapes=[pltpu.VMEM((SEQ, PAIR), F32)] * (2 * ng),
        compiler_params=_cp(("parallel",)), name="attention_fwd")(*([proj] * (3 * ng)), cos_f, sin_s)


def _attention_bwd(g, proj, cos_f, sin_s, d_attn, attn, lse):
    dil = DILATIONS[g]
    two_blocks = SEQ // dil > BLOCK

    def body(q_ref, k_ref, v_ref, cf_ref, ss_ref, do_ref, o_ref, lse_ref, dq_out, dk_out, dv_out,
             qr_ref, kr_ref, dq_acc, dk_acc, dv_acc):
        _rotate([q_ref], [qr_ref], cf_ref, ss_ref, 1.0 / math.sqrt(HEAD_DIM))
        _rotate([k_ref], [kr_ref], cf_ref, ss_ref, 1.0)
        dk_acc[...] = jnp.zeros_like(dk_acc)
        dv_acc[...] = jnp.zeros_like(dv_acc)
        nk = 2 * BLOCK if two_blocks else BLOCK
        first = lax.broadcasted_iota(jnp.int32, (BLOCK, PAIR), 1) < HEAD_DIM
        first_k = lax.broadcasted_iota(jnp.int32, (nk, PAIR), 1) < HEAD_DIM

        def unit(u, carry):
            i, rows, prev = _unit_pieces(u, dil)
            qq = _load_tile(qr_ref, rows, dil).astype(BF16)
            kk = _load_tile(kr_ref, rows, dil)
            vv = _load_tile(v_ref, rows, dil)
            if two_blocks:
                kk = jnp.concatenate([_load_tile(kr_ref, prev, dil), kk], axis=0)
                vv = jnp.concatenate([_load_tile(v_ref, prev, dil), vv], axis=0)
                valid = _band_mask(i, dil)
            else:
                valid = _causal_mask()
            kk, vv = kk.astype(BF16), vv.astype(BF16)
            dof = _load_tile(do_ref, rows, dil)
            dd = dof * _load_tile(o_ref, rows, dil)
            lse2 = _load_tile(lse_ref, rows, dil)
            dob = dof.astype(BF16)
            zq, zd, zf = jnp.zeros_like(qq), jnp.zeros_like(dob), jnp.zeros_like(dd)
            dqs, dks, dvs = [], [], []
            for head in range(2):
                mine = first if head == 0 else jnp.logical_not(first)
                delta = jnp.sum(jnp.where(mine, dd, zf), axis=1, keepdims=True)
                lse_h = lse2[:, head * HEAD_DIM:head * HEAD_DIM + 1]
                s = _dot_nt(jnp.where(mine, qq, zq), kk)
                p = jnp.where(valid, jnp.exp(s - lse_h), 0.0)
                dp = _dot_nt(jnp.where(mine, dob, zd), vv)
                ds = (p * (dp - delta)).astype(BF16)
                dqs.append(jnp.dot(ds, kk, preferred_element_type=F32))
                dks.append(lax.dot_general(ds, qq, (((0,), (0,)), ((), ())), preferred_element_type=F32))
                dvs.append(lax.dot_general(p.astype(BF16), dob, (((0,), (0,)), ((), ())), preferred_element_type=F32))
            dk = jnp.where(first_k, dks[0], dks[1])
            dv = jnp.where(first_k, dvs[0], dvs[1])
            _store_tile(dq_acc, rows, dil, jnp.where(first, dqs[0], dqs[1]))
            _store_tile(dk_acc, rows, dil, dk[nk - BLOCK:], accumulate=True)
            _store_tile(dv_acc, rows, dil, dv[nk - BLOCK:], accumulate=True)
            if two_blocks:
                _store_tile(dk_acc, prev, dil, dk[:BLOCK], accumulate=True)
                _store_tile(dv_acc, prev, dil, dv[:BLOCK], accumulate=True)
            return carry

        lax.fori_loop(0, UNITS, unit, 0, unroll=UNIT_UNROLL)

        def finish(t, carry):
            rows = pl.ds(pl.multiple_of(t * ROPE_ROWS, ROPE_ROWS), ROPE_ROWS)
            cf, ss = cf_ref[rows, :], ss_ref[rows, :]
            dq = dq_acc[rows, :] * (1.0 / math.sqrt(HEAD_DIM))
            dq_out[rows, :] = _rope_transposed(dq, cf, ss).astype(BF16)
            dk_out[rows, :] = _rope_transposed(dk_acc[rows, :], cf, ss).astype(BF16)
            dv_out[rows, :] = dv_acc[rows, :].astype(BF16)
            return carry

        lax.fori_loop(0, SEQ // ROPE_ROWS, finish, 0)

    whole = pl.BlockSpec((SEQ, PAIR), lambda hp: (0, 0))
    pair = pl.BlockSpec((SEQ, PAIR), lambda hp: (0, hp))
    views = [_pair_views(col0)[g] for col0 in (0, QKV_WIDTH, 2 * QKV_WIDTH)]
    return pl.pallas_call(
        body, grid=(ATTN_WIDTH // PAIR,), in_specs=views + [whole, whole, pair, pair, pair],
        out_specs=[pair, pair, pair], out_shape=[_sds((SEQ, ATTN_WIDTH), BF16)] * 3,
        scratch_shapes=[pltpu.VMEM((SEQ, PAIR), F32)] * 5,
        compiler_params=_cp(("parallel",)), name=f"attention_bwd_{g}")(proj, proj, proj, cos_f, sin_s, d_attn, attn, lse)


def _cmul(ar, ai, br, bi):
    return ar * br - ai * bi, ar * bi + ai * br


def _pow256(ar, ai):
    for _ in range(8):
        ar, ai = _cmul(ar, ai, ar, ai)
    return ar, ai


def _chunk_carries(first_r, first_i, pr, pi, reverse):
    rows = lax.broadcasted_iota(jnp.int32, first_r.shape, 0)
    out_r = jnp.zeros_like(first_r)
    out_i = jnp.zeros_like(first_i)
    hr = jnp.zeros_like(first_r[0:1])
    hi = jnp.zeros_like(hr)
    order = range(SCAN_CHUNKS - 1, -1, -1) if reverse else range(SCAN_CHUNKS)
    for c in order:
        out_r = jnp.where(rows == c, hr, out_r)
        out_i = jnp.where(rows == c, hi, out_i)
        tr, ti = _cmul(pr[0:1], pi[0:1], hr, hi)
        hr = first_r[c:c + 1] + tr
        hi = first_i[c:c + 1] + ti
    return out_r, out_i


def _tile(j):
    return pl.ds(pl.multiple_of(j * SCAN_CHUNKS, SCAN_CHUNKS), SCAN_CHUNKS)


def _to_scan_rows(t):
    per = SCAN_STEPS // PHASES
    return t.reshape(PHASES, SCAN_CHUNKS, per, t.shape[1]).transpose(2, 0, 1, 3).reshape(t.shape)


def _from_scan_rows(t):
    per = SCAN_STEPS // PHASES
    return t.reshape(per, PHASES, SCAN_CHUNKS, t.shape[1]).transpose(1, 2, 0, 3).reshape(t.shape)


def _scan_in_place(hr_ref, hi_ref, a_r, a_i):
    def local(j, carry):
        tr, ti = _cmul(a_r, a_i, carry[0], carry[1])
        nr = tr + hr_ref[_tile(j), :]
        ni = ti + hi_ref[_tile(j), :]
        hr_ref[_tile(j), :] = nr
        hi_ref[_tile(j), :] = ni
        return nr, ni

    zero = jnp.zeros_like(a_r)
    last_r, last_i = lax.fori_loop(0, SCAN_STEPS, local, (zero, zero), unroll=4)
    pr, pi = _pow256(a_r, a_i)
    er, ei = _chunk_carries(last_r, last_i, pr, pi, reverse=False)

    def fix(j, carry):
        tr, ti = _cmul(carry[0], carry[1], er, ei)
        hr_ref[_tile(j), :] += tr
        hi_ref[_tile(j), :] += ti
        return _cmul(carry[0], carry[1], a_r, a_i)

    lax.fori_loop(0, SCAN_STEPS, fix, (a_r, a_i), unroll=4)
    return er, ei


def _reverse_scan_in_place(lr_ref, li_ref, hr_ref, hi_ref, er, ei, a_r, a_i):
    def local(t, carry):
        j = SCAN_STEPS - 1 - t
        tr, ti = _cmul(a_r, a_i, carry[0], carry[1])
        nr = tr + lr_ref[_tile(j), :]
        ni = ti + li_ref[_tile(j), :]
        lr_ref[_tile(j), :] = nr
        li_ref[_tile(j), :] = ni
        return nr, ni

    zero = jnp.zeros_like(a_r)
    first_r, first_i = lax.fori_loop(0, SCAN_STEPS, local, (zero, zero), unroll=4)
    pr, pi = _pow256(a_r, a_i)
    nxt_r, nxt_i = _chunk_carries(first_r, first_i, pr, pi, reverse=True)

    def accumulate(lam_r, lam_i, hp_r, hp_i, acc):
        return (acc[0] + lam_r * hp_r + lam_i * hp_i, acc[1] + lam_i * hp_r - lam_r * hp_i)

    def fix(t, carry):
        qr, qi, acc_r, acc_i = carry
        j = SCAN_STEPS - 1 - t
        tr, ti = _cmul(qr, qi, nxt_r, nxt_i)
        lam_r = lr_ref[_tile(j), :] + tr
        lam_i = li_ref[_tile(j), :] + ti
        lr_ref[_tile(j), :] = lam_r
        li_ref[_tile(j), :] = lam_i
        acc_r, acc_i = accumulate(lam_r, lam_i, hr_ref[_tile(j - 1), :], hi_ref[_tile(j - 1), :], (acc_r, acc_i))
        qr, qi = _cmul(qr, qi, a_r, a_i)
        return qr, qi, acc_r, acc_i

    qr, qi, acc_r, acc_i = lax.fori_loop(0, SCAN_STEPS - 1, fix, (a_r, a_i, zero, zero), unroll=4)
    tr, ti = _cmul(qr, qi, nxt_r, nxt_i)
    lam_r = lr_ref[_tile(0), :] + tr
    lam_i = li_ref[_tile(0), :] + ti
    lr_ref[_tile(0), :] = lam_r
    li_ref[_tile(0), :] = lam_i
    acc_r, acc_i = accumulate(lam_r, lam_i, er, ei, (acc_r, acc_i))
    return jnp.sum(acc_r, axis=0, keepdims=True), jnp.sum(acc_i, axis=0, keepdims=True)


def _rope_tables():
    half = HEAD_DIM // 2
    inv_freq = ROPE_THETA ** (-jnp.arange(half, dtype=F32) / half)
    ang = jnp.arange(SEQ, dtype=F32)[:, None] * inv_freq[None, :]
    cos, sin = jnp.cos(ang), jnp.sin(ang)
    cos_f = jnp.concatenate([cos, cos, cos, cos], axis=1)
    sin_s = jnp.concatenate([-sin, sin, -sin, sin], axis=1)
    return cos_f, sin_s


def _ssm_discretise(a_re, a_im, log_dt, b_re, b_im):
    lam = lax.complex(a_re, a_im)
    dt = jnp.exp(log_dt)[:, None]
    a_bar = jnp.exp(lam * dt)
    b_bar = ((a_bar - 1.0) / lam)[..., None] * lax.complex(b_re, b_im)
    return a_bar.real, a_bar.imag, b_bar.real, b_bar.imag


def _block_diag_in(b):
    eye = jnp.eye(SSM_GROUPS, dtype=b.dtype)
    return (eye[:, None, :, None] * b.transpose(0, 2, 1)[:, :, None, :]).reshape(SSM_WIDTH, SSM_LANES)


def _block_diag_out(c):
    eye = jnp.eye(SSM_GROUPS, dtype=c.dtype)
    return (eye[:, None, :, None] * c.transpose(0, 2, 1)[:, :, None, :]).reshape(SSM_LANES, SSM_WIDTH)


SSM_SLABS = 4
SLAB_GROUPS = SSM_GROUPS // SSM_SLABS
SLAB_IN = SSM_WIDTH // SSM_SLABS
SLAB_STATE = SSM_LANES // SSM_SLABS


def _diag_blocks(a, b):
    ra, cb = a.shape[1], b.shape[1]
    wa, wb = ra // SLAB_GROUPS, cb // SLAB_GROUPS
    d = lax.dot_general(a, b, (((0,), (0,)), ((), ())), preferred_element_type=F32)
    row_g = jnp.right_shift(lax.broadcasted_iota(jnp.int32, (ra, cb), 0), wa.bit_length() - 1)
    col_g = jnp.right_shift(lax.broadcasted_iota(jnp.int32, (ra, cb), 1), wb.bit_length() - 1)
    d = jnp.where(row_g == col_g, d, 0.0)
    fold = (jnp.bitwise_and(lax.broadcasted_iota(jnp.int32, (cb, wb), 0), wb - 1)
            == lax.broadcasted_iota(jnp.int32, (cb, wb), 1)).astype(F32)
    return jnp.dot(d, fold, preferred_element_type=F32, precision=lax.Precision.HIGHEST)


def _slab_specs():
    tok = pl.BlockSpec((SEQ, SLAB_IN), lambda j: (0, j))
    state = pl.BlockSpec((SEQ, SLAB_STATE), lambda j: (0, j))
    b_in = pl.BlockSpec((SLAB_IN, SLAB_STATE), lambda j: (j, j))
    c_out = pl.BlockSpec((SLAB_STATE, SLAB_IN), lambda j: (j, j))
    vec = pl.BlockSpec((1, SLAB_STATE), lambda j: (0, j))
    ent = pl.BlockSpec((SCAN_CHUNKS, SLAB_STATE), lambda j: (0, j))
    return tok, state, b_in, c_out, vec, ent


def _ssm_forward(u, b_in_r, b_in_i, c_out_r, c_out_ni, a_r, a_i):
    def body(u_ref, br_ref, bi_ref, cr_ref, ci_ref, ar_ref, ai_ref, y_ref, hr_ref, hi_ref, er_ref, ei_ref):
        uu = u_ref[...]
        hr_ref[...] = jnp.dot(uu, br_ref[...], preferred_element_type=F32)
        hi_ref[...] = jnp.dot(uu, bi_ref[...], preferred_element_type=F32)
        a_re = jnp.broadcast_to(ar_ref[...], (SCAN_CHUNKS, SLAB_STATE))
        a_im = jnp.broadcast_to(ai_ref[...], (SCAN_CHUNKS, SLAB_STATE))
        er_ref[...], ei_ref[...] = _scan_in_place(hr_ref, hi_ref, a_re, a_im)
        y_ref[...] = (jnp.dot(hr_ref[...].astype(BF16), cr_ref[...], preferred_element_type=F32)
                      + jnp.dot(hi_ref[...].astype(BF16), ci_ref[...], preferred_element_type=F32))

    tok, state, b_in, c_out, vec, ent = _slab_specs()
    return pl.pallas_call(
        body, grid=(SSM_SLABS,), in_specs=[tok, b_in, b_in, c_out, c_out, vec, vec],
        out_specs=[tok, state, state, ent, ent],
        out_shape=[_sds((SEQ, SSM_WIDTH)), _sds((SEQ, SSM_LANES)), _sds((SEQ, SSM_LANES)),
                   _sds((SCAN_CHUNKS, SSM_LANES)), _sds((SCAN_CHUNKS, SSM_LANES))],
        compiler_params=_cp(("parallel",)), name="ssm_forward")(u, b_in_r, b_in_i, c_out_r, c_out_ni, a_r, a_i)


def _ssm_backward(d_y, d_u_skip, u, h_r, h_i, e_r, e_i, b_in_r, b_in_i, c_out_r, c_out_ni, a_r, a_i):
    def body(dy_ref, skip_ref, u_ref, hr_ref, hi_ref, er_ref, ei_ref, br_ref, bi_ref, cr_ref, ci_ref, ar_ref, ai_ref,
             du_ref, dar_ref, dai_ref, dcr_ref, dci_ref, dbr_ref, dbi_ref, lr_ref, li_ref):
        dy = dy_ref[...]
        lr_ref[...] = _dot_nt(dy, cr_ref[...])
        li_ref[...] = _dot_nt(dy, ci_ref[...])
        a_re = jnp.broadcast_to(ar_ref[...], (SCAN_CHUNKS, SLAB_STATE))
        a_im = -jnp.broadcast_to(ai_ref[...], (SCAN_CHUNKS, SLAB_STATE))
        dar_ref[...], dai_ref[...] = _reverse_scan_in_place(lr_ref, li_ref, hr_ref, hi_ref, er_ref[...], ei_ref[...],
                                                            a_re, a_im)
        dcr_ref[...] = _diag_blocks(hr_ref[...].astype(BF16), dy)
        dci_ref[...] = _diag_blocks(hi_ref[...].astype(BF16), dy)
        lam_r, lam_i = lr_ref[...].astype(BF16), li_ref[...].astype(BF16)
        uu = u_ref[...]
        dbr_ref[...] = _diag_blocks(uu, lam_r)
        dbi_ref[...] = _diag_blocks(uu, lam_i)
        du = skip_ref[...] + _dot_nt(lam_r, br_ref[...]) + _dot_nt(lam_i, bi_ref[...])
        du_ref[...] = du.astype(BF16)

    tok, state, b_in, c_out, vec, ent = _slab_specs()
    dc = pl.BlockSpec((SLAB_STATE, SSM_GROUP), lambda j: (j, 0))
    db = pl.BlockSpec((SLAB_IN, SSM_STATE), lambda j: (j, 0))
    return pl.pallas_call(
        body, grid=(SSM_SLABS,), in_specs=[tok, tok, tok, state, state, ent, ent, b_in, b_in, c_out, c_out, vec, vec],
        out_specs=[tok, vec, vec, dc, dc, db, db],
        out_shape=[_sds((SEQ, SSM_WIDTH), BF16), _sds((1, SSM_LANES)), _sds((1, SSM_LANES)),
                   _sds((SSM_LANES, SSM_GROUP)), _sds((SSM_LANES, SSM_GROUP)),
                   _sds((SSM_WIDTH, SSM_STATE)), _sds((SSM_WIDTH, SSM_STATE))],
        scratch_shapes=[pltpu.VMEM((SEQ, SLAB_STATE), F32)] * 2,
        compiler_params=_cp(("parallel",)), name="ssm_backward")(
            d_y, d_u_skip, u, h_r, h_i, e_r, e_i, b_in_r, b_in_i, c_out_r, c_out_ni, a_r, a_i)


FF_ROWS = 1024
FF_SHARD = D_FF // N_CHIPS


def _dot_nt(a, b):
    return lax.dot_general(a, b, (((1,), (1,)), ((), ())), preferred_element_type=F32)


def _ffn_up(h, w_gate_t, w_up_t):
    def body(h_ref, wg_ref, wu_ref, a_ref, b_ref, act_ref):
        hb = h_ref[...].astype(BF16)
        a = _dot_nt(hb, wg_ref[...])
        b = _dot_nt(hb, wu_ref[...])
        a_ref[...] = a
        b_ref[...] = b
        act_ref[...] = (a * jax.nn.sigmoid(a) * b).astype(BF16)

    w_spec = pl.BlockSpec((None, FF_SHARD, D_MODEL), lambda i, k: (k, 0, 0))
    o_spec = pl.BlockSpec((None, FF_ROWS, FF_SHARD), lambda i, k: (k, i, 0))
    shape = (N_CHIPS, SEQ, FF_SHARD)
    return pl.pallas_call(
        body, grid=(SEQ // FF_ROWS, N_CHIPS),
        in_specs=[pl.BlockSpec((FF_ROWS, D_MODEL), lambda i, k: (i, 0)), w_spec, w_spec],
        out_specs=[o_spec, o_spec, o_spec], out_shape=[_sds(shape), _sds(shape), _sds(shape, BF16)],
        compiler_params=_cp(("parallel", "parallel")), name="ffn_up")(h, w_gate_t, w_up_t)


def _ffn_down_bwd(dz, w_down, a, b):
    def body(dz_ref, wd_ref, a_ref, b_ref, da_ref, db_ref):
        d_act = _dot_nt(dz_ref[...].astype(BF16), wd_ref[...])
        av = a_ref[...]
        sg = jax.nn.sigmoid(av)
        da_ref[...] = (d_act * b_ref[...] * sg * (1.0 + av * (1.0 - sg))).astype(BF16)
        db_ref[...] = (d_act * av * sg).astype(BF16)

    t_spec = pl.BlockSpec((None, FF_ROWS, FF_SHARD), lambda i, k: (k, i, 0))
    shape = (N_CHIPS, SEQ, FF_SHARD)
    return pl.pallas_call(
        body, grid=(SEQ // FF_ROWS, N_CHIPS),
        in_specs=[pl.BlockSpec((FF_ROWS, D_MODEL), lambda i, k: (i, 0)),
                  pl.BlockSpec((None, FF_SHARD, D_MODEL), lambda i, k: (k, 0, 0)), t_spec, t_spec],
        out_specs=[t_spec, t_spec], out_shape=[_sds(shape, BF16), _sds(shape, BF16)],
        compiler_params=_cp(("parallel", "parallel")), name="ffn_down_bwd")(dz, w_down, a, b)


def _ffn_dh(d_a, d_b, w_gate_t, w_up_t):
    def body(da_ref, db_ref, wg_ref, wu_ref, o_ref, acc):
        k = pl.program_id(1)
        part = (jnp.dot(da_ref[...], wg_ref[...], preferred_element_type=F32)
                + jnp.dot(db_ref[...], wu_ref[...], preferred_element_type=F32))

        @pl.when(k == 0)
        def _():
            acc[...] = part

        @pl.when(k > 0)
        def _():
            acc[...] += part

        @pl.when(k == N_CHIPS - 1)
        def _():
            o_ref[...] = acc[...]

    t_spec = pl.BlockSpec((None, FF_ROWS, FF_SHARD), lambda i, k: (k, i, 0))
    w_spec = pl.BlockSpec((None, FF_SHARD, D_MODEL), lambda i, k: (k, 0, 0))
    return pl.pallas_call(
        body, grid=(SEQ // FF_ROWS, N_CHIPS), in_specs=[t_spec, t_spec, w_spec, w_spec],
        out_specs=pl.BlockSpec((FF_ROWS, D_MODEL), lambda i, k: (i, 0)), out_shape=_sds((SEQ, D_MODEL)),
        scratch_shapes=[pltpu.VMEM((FF_ROWS, D_MODEL), F32)],
        compiler_params=_cp(("parallel", "arbitrary")), name="ffn_dh")(d_a, d_b, w_gate_t, w_up_t)


def _local_step(x, tgt, wts, small):
    s = SEQ
    cos_f, sin_s = [_to_phase_rows(t) for t in _rope_tables()]
    x = _reorder_rows(x, to_phase=True, name="phase_rows_x")
    tgt = _reorder_rows(tgt, to_phase=True, name="phase_rows_target")

    proj = _mm_cols(x, wts["w_in"], tm=1024, name="proj")

    attn, lse = _attention_fwd(proj, cos_f, sin_s)
    y_attn = _mm_cols(attn, wts["w_attn_br"], tm=s, name="y_attn")

    (abar_r, abar_i, bbar_r, bbar_i), ssm_vjp = jax.vjp(
        _ssm_discretise, small["ssm_a_re"], small["ssm_a_im"], small["ssm_log_dt"], small["ssm_b_re"], small["ssm_b_im"])
    b_in_r, b_in_i = _block_diag_in(bbar_r).astype(BF16), _block_diag_in(bbar_i).astype(BF16)
    c_out_r = _block_diag_out(small["ssm_c_re"]).astype(BF16)
    c_out_ni = _block_diag_out(-small["ssm_c_im"]).astype(BF16)
    a_r, a_i = abar_r.reshape(1, SSM_LANES), abar_i.reshape(1, SSM_LANES)
    d_skip = small["ssm_d"].reshape(1, SSM_WIDTH)

    u_f = _to_scan_rows(proj[:, 3 * QKV_WIDTH:3 * QKV_WIDTH + SSM_WIDTH])
    u_p = u_f.astype(BF16)
    y_c, h_r, h_i, e_r, e_i = _ssm_forward(u_p, b_in_r, b_in_i, c_out_r, c_out_ni, a_r, a_i)

    def gelu_fwd(yc, u, dsk):
        y = yc + dsk * u
        return y, 0.5 * y * (1.0 + jnp.tanh(GELU_C * (y + GELU_K * y * y * y)))

    y_s5, gel = _rowwise(gelu_fwd, [y_c, u_f], [d_skip], [_sds((s, SSM_WIDTH)), _sds((s, SSM_WIDTH), BF16)],
                         tm=512, name="ssm_gelu")
    glu = _mm_cols(gel, wts["w_glu"], tm=s, name="glu")

    def glu_fwd(ga, gb):
        return ga * jax.nn.sigmoid(gb)

    (y_glu,) = _rowwise(glu_fwd, [(glu, SSM_WIDTH, 0), (glu, SSM_WIDTH, 1)], [], [_sds((s, SSM_WIDTH), BF16)],
                        tm=512, name="glu_gate")
    y_glu = _from_scan_rows(y_glu)
    y_ssm = _mm_cols(y_glu, wts["w_ssm_br"], tm=s, name="y_ssm")

    gl0 = (proj, D_MODEL, (3 * QKV_WIDTH + SSM_WIDTH) // D_MODEL)
    gl1 = (proj, D_MODEL, (3 * QKV_WIDTH + SSM_WIDTH) // D_MODEL + 1)
    b_gate = small["b_gate"]

    def gate_mix(l0, l1, ya, ys, bg):
        return jax.nn.sigmoid(l0 + bg[0:1]) * ya + jax.nn.sigmoid(l1 + bg[1:2]) * ys

    (mixed,) = _rowwise(gate_mix, [gl0, gl1, y_attn, y_ssm], [b_gate], [_sds((s, D_MODEL), BF16)], tm=256,
                        name="gate_mix")
    w_out = wts["w_out"].reshape(D_MODEL, D_MODEL)
    mix_out = _mm_plain(mixed, w_out, tm=1024, tn=512, name="mix_out")

    def ln1_fwd(xv, mo, g, b):
        z = DN_ALPHA * xv + mo
        xhat, _ = _ln_stats(z)
        return z, xhat * g + b

    z1, h = _rowwise(ln1_fwd, [x, mix_out], [small["ln1_g"], small["ln1_b"]],
                     [_sds((s, D_MODEL)), _sds((s, D_MODEL))], tm=256, name="ln1")

    nf = D_FF // N_CHIPS
    w_gate_t, w_up_t, w_down = wts["w_ff_gate"], wts["w_ff_up"], wts["w_ff_down"]
    ff_a, ff_b, act = _ffn_up(h, w_gate_t, w_up_t)
    ff = _matmul(act, w_down, grid=(2, N_CHIPS),
                 a_spec=pl.BlockSpec((None, 1024, nf), lambda i, k: (k, i, 0)),
                 b_spec=pl.BlockSpec((None, nf, D_MODEL), lambda i, k: (k, 0, 0)),
                 o_spec=pl.BlockSpec((1024, D_MODEL), lambda i, k: (i, 0)),
                 out_shape=_sds((s, D_MODEL)), dims=(1, 0), k_axis=1, name="ff_down")

    def ln2_loss(hv, ffv, tg, g, b):
        z = DN_ALPHA * hv + ffv
        xhat, rstd = _ln_stats(z)
        err = xhat * g + b - tg
        d_out = err * (1.0 / D_MODEL)
        loss_rows = jnp.sum(err * err, axis=-1, keepdims=True) * (0.5 / D_MODEL)
        loss = jnp.broadcast_to(jnp.sum(loss_rows, axis=0, keepdims=True), (1, 128))
        return _ln_bwd(d_out, xhat, rstd, g), loss, _colsum(d_out * xhat), _colsum(d_out)

    dz2, loss_v, d_ln2_g, d_ln2_b = _rowwise(
        ln2_loss, [h, ff, tgt], [small["ln2_g"], small["ln2_b"]], [_sds((s, D_MODEL))],
        [_sds((1, 128)), _sds((1, D_MODEL)), _sds((1, D_MODEL))], tm=256, name="ln2_loss")

    d_a, d_b = _ffn_down_bwd(dz2, w_down, ff_a, ff_b)

    def grad_rows(lhs, rhs, name):
        return _matmul(lhs, rhs, grid=(N_CHIPS,), a_spec=pl.BlockSpec((None, s, nf), lambda k: (k, 0, 0)),
                       b_spec=pl.BlockSpec((s, D_MODEL), lambda k: (0, 0)),
                       o_spec=pl.BlockSpec((None, nf, D_MODEL), lambda k: (k, 0, 0)),
                       out_shape=_sds((N_CHIPS, nf, D_MODEL), BF16), dims=(0, 0), name=name)

    g_w_ff_down = grad_rows(act, dz2, "g_w_ff_down")
    g_w_ff_gate = grad_rows(d_a, h, "g_w_ff_gate")
    g_w_ff_up = grad_rows(d_b, h, "g_w_ff_up")
    dh_ff = _ffn_dh(d_a, d_b, w_gate_t, w_up_t)

    def ln1_bwd(dz, dff, z, g):
        xhat, rstd = _ln_stats(z)
        dh = DN_ALPHA * dz + dff
        return _ln_bwd(dh, xhat, rstd, g), _colsum(dh * xhat), _colsum(dh)

    dz1, d_ln1_g, d_ln1_b = _rowwise(ln1_bwd, [dz2, dh_ff, z1], [small["ln1_g"]], [_sds((s, D_MODEL))],
                                     [_sds((1, D_MODEL)), _sds((1, D_MODEL))], tm=256, name="ln1_bwd")
    d_mixed = _mm_plain(dz1, w_out, tm=1024, tn=512, dims=(1, 1), name="d_mixed")
    g_w_out = _mm_plain(mixed, dz1, tm=D_MODEL, tn=512, dims=(0, 0), out_dtype=BF16, name="g_w_out")
    g_w_out = g_w_out.reshape(N_CHIPS, D_MODEL // N_CHIPS, D_MODEL)

    def gate_bwd(dm, l0, l1, ya, ys, bg):
        g0 = jax.nn.sigmoid(l0 + bg[0:1])
        g1 = jax.nn.sigmoid(l1 + bg[1:2])
        dl0 = dm * ya * g0 * (1.0 - g0)
        dl1 = dm * ys * g1 * (1.0 - g1)
        return dm * g0, dm * g1, jnp.concatenate([dl0, dl1], axis=1), _colsum(dl0), _colsum(dl1)

    d_y_attn, d_y_ssm, d_gl, d_bg0, d_bg1 = _rowwise(
        gate_bwd, [d_mixed, gl0, gl1, y_attn, y_ssm], [b_gate],
        [_sds((s, D_MODEL), BF16), _sds((s, D_MODEL), BF16), _sds((s, 2 * D_MODEL), BF16)],
        [_sds((1, D_MODEL)), _sds((1, D_MODEL))], tm=256, name="gate_bwd")

    g_w_ssm_br = _mm_cols_tn(y_glu, d_y_ssm, ns=D_MODEL // N_CHIPS, name="g_w_ssm_br")
    d_y_glu = _to_scan_rows(_mm_cols_nt(d_y_ssm, wts["w_ssm_br"], tm=s, name="d_y_glu"))

    def glu_bwd(dy, ga, gb):
        sg = jax.nn.sigmoid(gb)
        return jnp.concatenate([dy * sg, dy * ga * sg * (1.0 - sg)], axis=1)

    (d_glu,) = _rowwise(glu_bwd, [d_y_glu, (glu, SSM_WIDTH, 0), (glu, SSM_WIDTH, 1)], [],
                        [_sds((s, 2 * SSM_WIDTH), BF16)], tm=512, name="glu_bwd")
    g_w_glu = _mm_cols_tn(gel, d_glu, ns=2 * SSM_WIDTH // N_CHIPS, name="g_w_glu")
    d_gel = _mm_cols_nt(d_glu, wts["w_glu"], tm=s, name="d_gel")

    def gelu_bwd(dg, y, u, dsk):
        th = jnp.tanh(GELU_C * (y + GELU_K * y * y * y))
        dy = dg * (0.5 * (1.0 + th) + 0.5 * y * (1.0 - th * th) * GELU_C * (1.0 + 3.0 * GELU_K * y * y))
        return dy, dy * dsk, _colsum(dy * u)

    d_y, d_u_skip, d_ssm_d = _rowwise(gelu_bwd, [d_gel, y_s5, u_f], [d_skip],
                                      [_sds((s, SSM_WIDTH), BF16), _sds((s, SSM_WIDTH))], [_sds((1, SSM_WIDTH))],
                                      tm=512, name="gelu_bwd")
    d_u, d_abar_r, d_abar_i, d_c_r, d_c_ni, d_bin_r, d_bin_i = _ssm_backward(
        d_y, d_u_skip, u_p, h_r, h_i, e_r, e_i, b_in_r, b_in_i, c_out_r, c_out_ni, a_r, a_i)
    d_u = _from_scan_rows(d_u)
    d_bbar_r = d_bin_r.reshape(SSM_GROUPS, SSM_GROUP, SSM_STATE).transpose(0, 2, 1)
    d_bbar_i = d_bin_i.reshape(SSM_GROUPS, SSM_GROUP, SSM_STATE).transpose(0, 2, 1)
    d_a_re, d_a_im, d_log_dt, d_b_re, d_b_im = ssm_vjp(
        (d_abar_r.reshape(SSM_GROUPS, SSM_STATE), d_abar_i.reshape(SSM_GROUPS, SSM_STATE), d_bbar_r, d_bbar_i))
    d_c_re = d_c_r.reshape(SSM_GROUPS, SSM_STATE, SSM_GROUP).transpose(0, 2, 1)
    d_c_im = -d_c_ni.reshape(SSM_GROUPS, SSM_STATE, SSM_GROUP).transpose(0, 2, 1)

    g_w_attn_br = _mm_cols_tn(attn, d_y_attn, ns=D_MODEL // N_CHIPS, name="g_w_attn_br")
    d_attn = _mm_cols_nt(d_y_attn, wts["w_attn_br"], tm=s, name="d_attn")
    dqkv = [_attention_bwd(g, proj, cos_f, sin_s, d_attn, attn, lse) for g in range(len(DILATIONS))]

    d_proj = jnp.concatenate([dqkv[g][j] for j in range(3) for g in range(len(DILATIONS))] + [d_u, d_gl],
                             axis=1)
    dx_proj = _mm_cols_nt(d_proj, wts["w_in"], tm=1024, name="dx_proj")
    g_w_in = _mm_cols_tn(x, d_proj, ns=IN_WIDTH // N_CHIPS, name="g_w_in", after=(dx_proj,))

    def dx_sum(dz, dxp):
        return DN_ALPHA * dz + dxp

    (grad_x,) = _rowwise(dx_sum, [dz1, dx_proj], [], [_sds((s, D_MODEL))], tm=512, name="grad_x", after=(g_w_in,))
    grad_x = _reorder_rows(grad_x, to_phase=False, name="time_rows_grad_x")

    big = {"w_in": g_w_in, "w_attn_br": g_w_attn_br, "w_ssm_br": g_w_ssm_br, "w_out": g_w_out, "w_glu": g_w_glu,
           "w_ff_gate": g_w_ff_gate, "w_ff_up": g_w_ff_up, "w_ff_down": g_w_ff_down}
    small_g = {"b_gate": jnp.concatenate([d_bg0, d_bg1], axis=0), "ssm_a_re": d_a_re, "ssm_a_im": d_a_im,
               "ssm_log_dt": d_log_dt, "ssm_b_re": d_b_re, "ssm_b_im": d_b_im, "ssm_c_re": d_c_re, "ssm_c_im": d_c_im,
               "ssm_d": d_ssm_d.reshape(SSM_WIDTH), "ln1_g": d_ln1_g, "ln1_b": d_ln1_b, "ln2_g": d_ln2_g,
               "ln2_b": d_ln2_b}
    marks = {"ln1_bwd": dz1, "scan_bwd": d_abar_r,"attention_bwd_0": dqkv[0][0], "dx_proj": dx_proj}
    return loss_v[0, 0], grad_x, big, small_g, marks


GATHER_ID, SWAP_ID, SCATTER_ID, JOIN_ID, EXCHANGE_ID = 1, 2, 3, 4, 5


def _place():
    return lax.axis_index("x"), lax.axis_index("y"), lax.axis_index("c")


def _other_chips(x, y):
    return [(1 - x, y), (x, 1 - y), (1 - x, 1 - y)]


def _handshake(peers):
    barrier = pltpu.get_barrier_semaphore()
    for peer in peers:
        pl.semaphore_signal(barrier, inc=1, device_id=peer, device_id_type=MESH)
    pl.semaphore_wait(barrier, len(peers))


def _sequencer(body, arrays, out_type, sems, collective_id, name):
    return pl.kernel(body, name=name, out_type=out_type,
                     mesh=plsc.ScalarSubcoreMesh(axis_name="sequencer", num_cores=1), scratch_types=sems,
                     compiler_params=pltpu.CompilerParams(collective_id=collective_id))(*arrays)


def _gather_weights(shards, *, name):
    nw = len(shards)

    def body(*refs):
        ins, outs = refs[:nw], refs[nw:2 * nw]
        send_sems, recv_sems, pass_send, pass_recv, local_sems = refs[2 * nw:]
        x, y, c = _place()
        chip = 2 * x + y
        chips = _other_chips(x, y)
        _handshake([(x, y, 1 - c)] + [(cx, cy, c) for cx, cy in chips])
        started = []
        for w in range(nw):
            hw = shards[w].shape[0] // 2
            mine = pl.ds(c * hw, hw)
            own = pltpu.make_async_copy(ins[w], outs[w].at[chip], local_sems.at[w])
            own.start()
            started.append(own)
            for j, (cx, cy) in enumerate(chips):
                cp = pltpu.make_async_remote_copy(
                    src_ref=ins[w].at[mine], dst_ref=outs[w].at[chip, mine], send_sem=send_sems.at[w, j],
                    recv_sem=recv_sems.at[w, j], device_id=(cx, cy, c), device_id_type=MESH)
                cp.start()
                started.append(cp)
        passed = []
        for w in range(nw):
            hw = shards[w].shape[0] // 2
            mine = pl.ds(c * hw, hw)
            for j, (cx, cy) in enumerate(chips):
                landed = outs[w].at[2 * cx + cy, mine]
                pltpu.make_async_remote_copy(
                    src_ref=ins[w].at[mine], dst_ref=landed, send_sem=send_sems.at[w, j],
                    recv_sem=recv_sems.at[w, j], device_id=(cx, cy, c), device_id_type=MESH).wait_recv()
                cp = pltpu.make_async_remote_copy(
                    src_ref=landed, dst_ref=landed, send_sem=pass_send.at[w, j], recv_sem=pass_recv.at[w, j],
                    device_id=(x, y, 1 - c), device_id_type=MESH)
                cp.start()
                passed.append(cp)
        for w in range(nw):
            hw = shards[w].shape[0] // 2
            theirs = pl.ds((1 - c) * hw, hw)
            for j, (cx, cy) in enumerate(chips):
                landed = outs[w].at[2 * cx + cy, theirs]
                pltpu.make_async_remote_copy(
                    src_ref=landed, dst_ref=landed, send_sem=pass_send.at[w, j], recv_sem=pass_recv.at[w, j],
                    device_id=(x, y, 1 - c), device_id_type=MESH).wait_recv()
        for cp in started[0::4]:
            cp.wait()
        for cp in [s for i, s in enumerate(started) if i % 4] + passed:
            cp.wait_send()

    sem = pltpu.SemaphoreType.DMA
    return _sequencer(body, shards, [_sds((N_CHIPS,) + a.shape, a.dtype) for a in shards],
                      [sem((nw, 3)), sem((nw, 3)), sem((nw, 3)), sem((nw, 3)), sem((nw,))], GATHER_ID, name)


def _swap_other_halves(grads, *, name):
    nw = len(grads)

    def body(*refs):
        ins, outs = refs[:nw], refs[nw:2 * nw]
        send_sems, recv_sems = refs[2 * nw:]
        x, y, c = _place()
        _handshake([(x, y, 1 - c)])
        cps = []
        for w in range(nw):
            hw = grads[w].shape[1] // 2
            cp = pltpu.make_async_remote_copy(
                src_ref=ins[w].at[:, pl.ds((1 - c) * hw, hw)], dst_ref=outs[w], send_sem=send_sems.at[w],
                recv_sem=recv_sems.at[w], device_id=(x, y, 1 - c), device_id_type=MESH)
            cp.start()
            cps.append(cp)
        for cp in cps:
            cp.wait()

    sem = pltpu.SemaphoreType.DMA
    return _sequencer(body, grads, [_sds((N_CHIPS, g.shape[1] // 2, g.shape[2]), g.dtype) for g in grads],
                      [sem((nw,)), sem((nw,))], SWAP_ID, name)


def _add_my_half(core, g, other, after=()):
    n, r, cols = g.shape
    hw = r // 2

    def body(core_ref, g_ref, o_ref, *rest):
        out_ref = rest[len(after)]
        out_ref[...] = (g_ref[...].astype(F32) + o_ref[...].astype(F32)).astype(out_ref.dtype)

    return pl.pallas_call(
        body,
        grid_spec=pltpu.PrefetchScalarGridSpec(
            num_scalar_prefetch=1, grid=(n,),
            in_specs=[pl.BlockSpec((None, None, hw, cols), lambda s, core_ref: (s, core_ref[0], 0, 0)),
                      pl.BlockSpec((None, hw, cols), lambda s, core_ref: (s, 0, 0))] + [HBM_OPERAND] * len(after),
            out_specs=pl.BlockSpec((None, hw, cols), lambda s, core_ref: (s, 0, 0))),
        out_shape=_sds((n, hw, cols), BF16), compiler_params=_cp(("parallel",)),
        name="add_my_half")(core, g.reshape(n, 2, hw, cols), other, *after)


def _scatter_partials(parts, *, name):
    nw = len(parts)

    def body(*refs):
        ins, outs = refs[:nw], refs[nw:2 * nw]
        send_sems, recv_sems = refs[2 * nw:]
        x, y, c = _place()
        _handshake([(cx, cy, c) for cx, cy in _other_chips(x, y)])
        cps = []
        for w in range(nw):
            for j, (cx, cy) in enumerate(_other_chips(x, y)):
                cp = pltpu.make_async_remote_copy(
                    src_ref=ins[w].at[2 * cx + cy], dst_ref=outs[w].at[j], send_sem=send_sems.at[w, j],
                    recv_sem=recv_sems.at[w, j], device_id=(cx, cy, c), device_id_type=MESH)
                cp.start()
                cps.append(cp)
        for cp in cps:
            cp.wait()

    sem = pltpu.SemaphoreType.DMA
    return _sequencer(body, parts, [_sds((3,) + p.shape[1:], p.dtype) for p in parts],
                      [sem((nw, 3)), sem((nw, 3))], SCATTER_ID, name)


def _sum_partials(chip, part, recv, after=()):
    _, hw, cols = part.shape
    th = hw // 2 if hw % 32 == 0 else hw

    def body(chip_ref, p_ref, r_ref, *rest):
        out_ref = rest[len(after)]
        acc = p_ref[...].astype(F32)
        for j in range(3):
            acc = acc + r_ref[j].astype(F32)
        out_ref[...] = acc

    return pl.pallas_call(
        body,
        grid_spec=pltpu.PrefetchScalarGridSpec(
            num_scalar_prefetch=1, grid=(hw // th,),
            in_specs=[pl.BlockSpec((None, th, cols), lambda i, chip_ref: (chip_ref[0], i, 0)),
                      pl.BlockSpec((3, th, cols), lambda i, chip_ref: (0, i, 0))] + [HBM_OPERAND] * len(after),
            out_specs=pl.BlockSpec((th, cols), lambda i, chip_ref: (i, 0))),
        out_shape=_sds((hw, cols)), compiler_params=_cp(("parallel",)), name="sum_partials")(
            chip, part, recv, *after)


def _swap_reduced_halves(halves, *, name):
    nw = len(halves)

    def body(*refs):
        ins, outs = refs[:nw], refs[nw:2 * nw]
        send_sems, recv_sems = refs[2 * nw:]
        x, y, c = _place()
        _handshake([(x, y, 1 - c)])
        cps = []
        for w in range(nw):
            cp = pltpu.make_async_remote_copy(
                src_ref=ins[w], dst_ref=outs[w], send_sem=send_sems.at[w], recv_sem=recv_sems.at[w],
                device_id=(x, y, 1 - c), device_id_type=MESH)
            cp.start()
            cps.append(cp)
        for cp in cps:
            cp.wait()

    sem = pltpu.SemaphoreType.DMA
    return _sequencer(body, halves, [_sds(h.shape, h.dtype) for h in halves], [sem((nw,)), sem((nw,))], JOIN_ID, name)


def _allreduce_rows(vec, *, name, after=()):
    rows = vec.shape[0]

    def body(v_ref, *rest):
        out_ref, slots, send_sems, recv_sems = rest[len(after):]
        x, y, c = _place()
        me = 4 * x + 2 * y + c
        slots[me] = v_ref[...]
        peers = []
        for mask in range(1, N_DEV):
            px = 1 - x if mask & 4 else x
            py = 1 - y if mask & 2 else y
            pc = 1 - c if mask & 1 else c
            peers.append((px, py, pc))
        cps = []
        for k, peer in enumerate(peers):
            cp = pltpu.make_async_remote_copy(
                src_ref=v_ref, dst_ref=slots.at[me], send_sem=send_sems.at[k], recv_sem=recv_sems.at[k],
                device_id=peer, device_id_type=MESH)
            cp.start()
            cps.append(cp)
        for k, (px, py, pc) in enumerate(peers):
            pltpu.make_async_remote_copy(
                src_ref=v_ref, dst_ref=slots.at[4 * px + 2 * py + pc], send_sem=send_sems.at[k],
                recv_sem=recv_sems.at[k], device_id=(px, py, pc), device_id_type=MESH).wait_recv()
        for cp in cps:
            cp.wait_send()
        acc = slots[0]
        for d in range(1, N_DEV):
            acc = acc + slots[d]
        out_ref[...] = acc

    vmem = pl.BlockSpec(memory_space=pltpu.VMEM)
    return pl.pallas_call(
        body, in_specs=[vmem] + [HBM_OPERAND] * len(after), out_specs=vmem, out_shape=_sds((rows, 128)),
        scratch_shapes=[pltpu.VMEM((N_DEV, rows, 128), F32), pltpu.SemaphoreType.DMA((N_DEV - 1,)),
                        pltpu.SemaphoreType.DMA((N_DEV - 1,))],
        compiler_params=pltpu.CompilerParams(vmem_limit_bytes=VMEM_LIMIT_BYTES), name=name)(vec, *after)


def _exchange_rows(vec, *, name):
    def body(v_ref, slots, send_sems, recv_sems, local_sem):
        x, y, c = _place()
        me = 4 * x + 2 * y + c
        peers = []
        for mask in range(1, N_DEV):
            peers.append((1 - x if mask & 4 else x, 1 - y if mask & 2 else y, 1 - c if mask & 1 else c))
        _handshake(peers)
        own = pltpu.make_async_copy(v_ref, slots.at[me], local_sem)
        own.start()
        cps = []
        for k, peer in enumerate(peers):
            cp = pltpu.make_async_remote_copy(
                src_ref=v_ref, dst_ref=slots.at[me], send_sem=send_sems.at[k], recv_sem=recv_sems.at[k],
                device_id=peer, device_id_type=MESH)
            cp.start()
            cps.append(cp)
        for k, (px, py, pc) in enumerate(peers):
            pltpu.make_async_remote_copy(
                src_ref=v_ref, dst_ref=slots.at[4 * px + 2 * py + pc], send_sem=send_sems.at[k],
                recv_sem=recv_sems.at[k], device_id=(px, py, pc), device_id_type=MESH).wait_recv()
        for cp in cps:
            cp.wait_send()
        own.wait()

    sem = pltpu.SemaphoreType.DMA
    return _sequencer(body, [vec], [_sds((N_DEV,) + vec.shape)], [sem((N_DEV - 1,)), sem((N_DEV - 1,)), sem(())],
                      EXCHANGE_ID, name)[0]


def _sum_slots(slots, *, name, after=()):
    def body(s_ref, *rest):
        out_ref = rest[len(after)]
        acc = s_ref[0]
        for d in range(1, N_DEV):
            acc = acc + s_ref[d]
        out_ref[...] = acc

    vmem = pl.BlockSpec(memory_space=pltpu.VMEM)
    return pl.pallas_call(
        body, in_specs=[vmem] + [HBM_OPERAND] * len(after), out_specs=vmem, out_shape=_sds(slots.shape[1:]),
        compiler_params=pltpu.CompilerParams(vmem_limit_bytes=VMEM_LIMIT_BYTES), name=name)(slots, *after)


def _reduce_scatter_start(grads, core, *, tag, add_after=()):
    others = _swap_other_halves(grads, name="swap_other_halves_" + tag)
    parts = [_add_my_half(core, g, o, add_after) for g, o in zip(grads, others)]
    return parts, _scatter_partials(parts, name="scatter_partials_" + tag)


def _reduce_scatter_finish(parts, recvd, chip, *, tag, sum_after=()):
    mine = [_sum_partials(chip, p, r, sum_after) for p, r in zip(parts, recvd)]
    return mine, _swap_reduced_halves(mine, name="swap_reduced_halves_" + tag)


ADAM_BLOCK_ELEMS = 256 * 1024


def _adam_rows(rows, cols):
    tm = rows
    while tm * cols > ADAM_BLOCK_ELEMS and tm % 16 == 0:
        tm //= 2
    return tm


def _adam_step(wv, gv, mv, vv):
    m2 = ADAM_B1 * mv + (1.0 - ADAM_B1) * gv
    v2 = ADAM_B2 * vv + (1.0 - ADAM_B2) * (gv * gv)
    m_hat = m2 / (1.0 - ADAM_B1 ** ADAM_STEP)
    v_hat = v2 / (1.0 - ADAM_B2 ** ADAM_STEP)
    return -ADAM_LR * (m_hat / (jnp.sqrt(v_hat) + ADAM_EPS) + ADAM_WD * wv), m2, v2


def _adamw(w, g, m, v, *, name):
    rows, cols = w.shape
    return _rowwise(_adam_step, [w, g, m, v], [], [_sds((rows, cols))] * 3, tm=_adam_rows(rows, cols), name=name)


def _adamw_halves(core, w, g_mine, g_theirs, m, v, *, name, after=()):
    rows, cols = w.shape
    hw = rows // 2
    tm = _adam_rows(hw, cols)
    per_half = hw // tm

    def body(core_ref, w_ref, gm_ref, gt_ref, m_ref, v_ref, *rest):
        g_out, d_out, m_out, v_out = rest[len(after):]
        mine = (pl.program_id(0) // per_half) == core_ref[0]
        g = jnp.where(mine, gm_ref[...], gt_ref[...])
        d, m2, v2 = _adam_step(w_ref[...], g, m_ref[...], v_ref[...])
        g_out[...] = g
        d_out[...] = d
        m_out[...] = m2
        v_out[...] = v2

    full = pl.BlockSpec((tm, cols), lambda i, core_ref: (i, 0))
    half = pl.BlockSpec((tm, cols), lambda i, core_ref: (i % per_half, 0))
    return pl.pallas_call(
        body,
        grid_spec=pltpu.PrefetchScalarGridSpec(
            num_scalar_prefetch=1, grid=(rows // tm,),
            in_specs=[full, half, half, full, full] + [HBM_OPERAND] * len(after), out_specs=[full, full, full, full]),
        out_shape=[_sds((rows, cols))] * 4, compiler_params=_cp(("parallel",)), name=name)(
            core, w, g_mine, g_theirs, m, v, *after)


HELD_TRANSPOSED = ("w_ff_gate", "w_ff_up")


def _as_rows(name, arr):
    return arr[0].T if name in HELD_TRANSPOSED else arr[0]


def _from_rows(name, arr2d):
    return (arr2d.T if name in HELD_TRANSPOSED else arr2d)[None]


STORED_SWAPPED = ("ssm_b_re", "ssm_b_im")


def _as_stored(name, arr):
    return jnp.swapaxes(arr, -1, -2) if name in STORED_SWAPPED else arr


def _pack_rows(arrs):
    flat = jnp.concatenate([a.reshape(-1).astype(F32) for a in arrs])
    rows = -(-flat.shape[0] // 1024) * 8
    return jnp.pad(flat, (0, rows * 128 - flat.shape[0])).reshape(rows, 128)


def _unpack_rows(vec, shapes):
    flat = vec.reshape(-1)
    out, off = [], 0
    for shp in shapes:
        size = math.prod(shp)
        out.append(flat[off:off + size].reshape(shp))
        off += size
    return out


SMALL = ("b_gate", "ssm_a_re", "ssm_a_im", "ssm_log_dt", "ssm_b_re", "ssm_b_im", "ssm_c_re", "ssm_c_im", "ssm_d",
         "ln1_g", "ln1_b", "ln2_g", "ln2_b")
GATHER_GROUPS = (("w_in", ("w_in",)), ("mixer", ("w_attn_br", "w_ssm_br", "w_glu", "w_out")),
                 ("ffn", ("w_ff_gate", "w_ff_up", "w_ff_down")))
REDUCE_GROUPS = (("ffn", ("w_ff_down", "w_ff_gate", "w_ff_up")),
                 ("mixer", ("w_out", "w_ssm_br", "w_glu", "w_attn_br")), ("w_in", ("w_in",)))
WEIGHTS = ("w_in", "b_gate", "w_attn_br", "w_ssm_br", "w_out", "ssm_a_re", "ssm_a_im", "ssm_log_dt", "ssm_b_re",
           "ssm_b_im", "ssm_c_re", "ssm_c_im", "ssm_d", "w_glu", "ln1_g", "ln1_b", "w_ff_gate", "w_ff_up", "w_ff_down",
           "ln2_g", "ln2_b")


def kernel(x, w_in, b_gate, w_attn_br, w_ssm_br, w_out, ssm_a_re, ssm_a_im, ssm_log_dt, ssm_b_re, ssm_b_im, ssm_c_re, ssm_c_im, ssm_d, w_glu, ln1_g, ln1_b, w_ff_gate, w_ff_up, w_ff_down, ln2_g, ln2_b, loss_target, m_w_in, m_b_gate, m_w_attn_br, m_w_ssm_br, m_w_out, m_ssm_a_re, m_ssm_a_im, m_ssm_log_dt, m_ssm_b_re, m_ssm_b_im, m_ssm_c_re, m_ssm_c_im, m_ssm_d, m_w_glu, m_ln1_g, m_ln1_b, m_w_ff_gate, m_w_ff_up, m_w_ff_down, m_ln2_g, m_ln2_b, v_w_in, v_b_gate, v_w_attn_br, v_w_ssm_br, v_w_out, v_ssm_a_re, v_ssm_a_im, v_ssm_log_dt, v_ssm_b_re, v_ssm_b_im, v_ssm_c_re, v_ssm_c_im, v_ssm_d, v_w_glu, v_ln1_g, v_ln1_b, v_w_ff_gate, v_w_ff_up, v_w_ff_down, v_ln2_g, v_ln2_b):
    given = dict(locals())
    px, py, pc = _place()
    chip = 2 * px + py
    core_s = jnp.reshape(pc, (1,)).astype(jnp.int32)
    chip_s = jnp.reshape(chip, (1,)).astype(jnp.int32)

    wts = {}
    for tag, names in GATHER_GROUPS:
        wts.update(zip(names, _gather_weights([_as_rows(n, given[n]).astype(BF16) for n in names],
                                              name="gather_" + tag)))
    ncol = D_MODEL // N_CHIPS
    bg_mine = jnp.where(pc == 0, b_gate[0], jnp.zeros_like(b_gate[0]))
    bg_full = lax.dynamic_update_slice(jnp.zeros((2, D_MODEL), F32), bg_mine, (0, chip * ncol))
    bg_full = _allreduce_rows(bg_full.reshape(16, 128), name="gather_gate_bias").reshape(2, D_MODEL)
    small = {n: given[n][0] for n in SMALL if n.startswith("ssm")}
    small.update({n: given[n] for n in ("ln1_g", "ln1_b", "ln2_g", "ln2_b")})
    small["b_gate"] = bg_full

    loss_mine, grad_x, big_g, small_g, marks = _local_step(x[0], loss_target[0], wts, small)
    loss = lax.psum(loss_mine, ("x", "y", "c"))

    groups = dict(REDUCE_GROUPS)
    add_after = {"ffn": (marks["ln1_bwd"],), "mixer": (marks["scan_bwd"],), "w_in": (grad_x,)}
    parts, recvd = {}, {}
    for tag, names in REDUCE_GROUPS:
        parts[tag], recvd[tag] = _reduce_scatter_start([big_g[n] for n in names], core_s, tag=tag,
                                                       add_after=add_after[tag])
    grads, delta, new_m, new_v = {}, {}, {}, {}

    def finish(tag, sum_after, adam_after):
        mine, theirs = _reduce_scatter_finish(parts[tag], recvd[tag], chip_s, tag=tag, sum_after=sum_after)
        for n, g_mine, g_theirs in zip(groups[tag], mine, theirs):
            res = _adamw_halves(core_s, _as_rows(n, given[n]), g_mine, g_theirs, _as_rows(n, given["m_" + n]),
                                _as_rows(n, given["v_" + n]), name="adamw_" + n, after=adam_after)
            grads[n], delta[n], new_m[n], new_v[n] = [_from_rows(n, r) for r in res]

    in_flight = (parts["w_in"][0],)
    finish("ffn", (marks["scan_bwd"],), in_flight)
    finish("mixer", (marks["attention_bwd_0"],), in_flight)
    stored = [_as_stored(n, small_g[n]) for n in SMALL]
    slots = _exchange_rows(_pack_rows(stored), name="exchange_small")
    summed = _unpack_rows(_sum_slots(slots, name="sum_small", after=in_flight), [a.shape for a in stored])
    for n, g in zip(SMALL, summed):
        g = _as_stored(n, g)
        if n == "b_gate":
            g = lax.dynamic_slice(g, (0, chip * ncol), (2, ncol))
        grads[n] = g.reshape(given[n].shape)
    packed = [_pack_rows([_as_stored(n, src[n]) for n in SMALL]) for src in
              (given, grads, {n: given["m_" + n] for n in SMALL}, {n: given["v_" + n] for n in SMALL})]
    shapes = [_as_stored(n, given[n]).shape for n in SMALL]
    small_out = _adamw(*packed, name="adamw_small")
    for out, vec in zip((delta, new_m, new_v), small_out):
        out.update((n, _as_stored(n, a)) for n, a in zip(SMALL, _unpack_rows(vec, shapes)))
    behind = [delta[n] for tag in ("ffn", "mixer") for n in groups[tag]] + [small_out[0]]
    finish("w_in", tuple(behind), ())

    return (loss, grad_x.reshape(x.shape), *[grads[n] for n in WEIGHTS], *[delta[n] for n in WEIGHTS],
            *[new_m[n] for n in WEIGHTS], *[new_v[n] for n in WEIGHTS])
```

```python
import math

import jax
import jax.numpy as jnp
from jax import lax
from jax.experimental import pallas as pl
from jax.experimental.pallas import tpu as pltpu
from jax.experimental.pallas import tpu_sc as plsc

F32 = jnp.float32
BF16 = jnp.bfloat16
MESH = pl.DeviceIdType.MESH

D_MODEL = 1024
SEQ = 2048
HEAD_DIM = 64
ATTN_HEADS = 8
DILATIONS = (1, 4, 16)
ATTN_WIDTH = ATTN_HEADS * HEAD_DIM
QKV_WIDTH = 3 * ATTN_WIDTH
BLOCK = 128
ROPE_THETA = 10000.0
NEG_INF = -1e30
SSM_GROUP = 16
SSM_GROUPS = 32
SSM_WIDTH = 512
SSM_STATE = 64
SSM_LANES = SSM_GROUPS * SSM_STATE
SCAN_CHUNKS = 8
SCAN_STEPS = SEQ // SCAN_CHUNKS
IN_WIDTH = 3 * QKV_WIDTH + SSM_WIDTH + 2 * D_MODEL
D_FF = 2816
N_CHIPS = 4
N_DEV = 8
DN_ALPHA = 2.0 ** 0.25
LN_EPS = 1e-5
ADAM_LR = 0.001
ADAM_B1 = 0.9
ADAM_B2 = 0.999
ADAM_EPS = 1e-08
ADAM_WD = 0.01
ADAM_STEP = 10
GELU_C = math.sqrt(2.0 / math.pi)
GELU_K = 0.044715

VMEM_LIMIT_BYTES = 56 * 1024 * 1024


def _sds(shape, dtype=F32):
    return jax.ShapeDtypeStruct(tuple(shape), dtype)


def _cp(semantics=None):
    return pltpu.CompilerParams(dimension_semantics=semantics, vmem_limit_bytes=VMEM_LIMIT_BYTES)


HBM_OPERAND = pl.BlockSpec(memory_space=pl.ANY)


def _matmul(a, b, *, grid, a_spec, b_spec, o_spec, out_shape, dims, k_axis=None, name, after=()):
    nk = grid[k_axis] if k_axis is not None else 1
    o_block = tuple(d for d in o_spec.block_shape if d is not None)
    n_after = len(after)

    def body(a_ref, b_ref, *rest):
        o_ref, acc = rest[n_after], rest[n_after + 1:]
        part = lax.dot_general(a_ref[...].astype(BF16), b_ref[...].astype(BF16),
                               (((dims[0],), (dims[1],)), ((), ())), preferred_element_type=F32)
        if k_axis is None:
            o_ref[...] = part.astype(o_ref.dtype)
        else:
            k = pl.program_id(k_axis)

            @pl.when(k == 0)
            def _():
                acc[0][...] = part

            @pl.when(k > 0)
            def _():
                acc[0][...] += part

            @pl.when(k == nk - 1)
            def _():
                o_ref[...] = acc[0][...].astype(o_ref.dtype)

    sem = tuple("arbitrary" if ax == k_axis else "parallel" for ax in range(len(grid)))
    return pl.pallas_call(
        body, grid=grid, in_specs=[a_spec, b_spec] + [HBM_OPERAND] * n_after, out_specs=o_spec, out_shape=out_shape,
        scratch_shapes=[pltpu.VMEM(o_block, F32)] if k_axis is not None else [],
        compiler_params=_cp(sem), name=name)(a, b, *after)


def _mm_cols(a, wg, *, tm, name, out_dtype=F32, out3d=False):
    m, k = a.shape
    ns = wg.shape[2]
    if out3d:
        o_spec = pl.BlockSpec((None, tm, ns), lambda i, s: (s, i, 0))
        out_shape = _sds((N_CHIPS, m, ns), out_dtype)
    else:
        o_spec = pl.BlockSpec((tm, ns), lambda i, s: (i, s))
        out_shape = _sds((m, N_CHIPS * ns), out_dtype)
    return _matmul(a, wg, grid=(m // tm, N_CHIPS),
                   a_spec=pl.BlockSpec((tm, k), lambda i, s: (i, 0)),
                   b_spec=pl.BlockSpec((None, k, ns), lambda i, s: (s, 0, 0)),
                   o_spec=o_spec, out_shape=out_shape, dims=(1, 0), name=name)


def _mm_cols_nt(dy, wg, *, tm, name, dy3d=False, out_dtype=F32, after=()):
    k, ns = wg.shape[1], wg.shape[2]
    if dy3d:
        m = dy.shape[1]
        a_spec = pl.BlockSpec((None, tm, ns), lambda i, s: (s, i, 0))
    else:
        m = dy.shape[0]
        a_spec = pl.BlockSpec((tm, ns), lambda i, s: (i, s))
    return _matmul(dy, wg, grid=(m // tm, N_CHIPS), a_spec=a_spec,
                   b_spec=pl.BlockSpec((None, k, ns), lambda i, s: (s, 0, 0)),
                   o_spec=pl.BlockSpec((tm, k), lambda i, s: (i, 0)),
                   out_shape=_sds((m, k), out_dtype), dims=(1, 1), k_axis=1, name=name, after=after)


def _mm_cols_tn(a, dy, *, ns, name, after=()):
    m, k = a.shape
    return _matmul(a, dy, grid=(N_CHIPS,), a_spec=pl.BlockSpec((m, k), lambda s: (0, 0)),
                   b_spec=pl.BlockSpec((m, ns), lambda s: (0, s)),
                   o_spec=pl.BlockSpec((None, k, ns), lambda s: (s, 0, 0)),
                   out_shape=_sds((N_CHIPS, k, ns), BF16), dims=(0, 0), name=name, after=after)


def _mm_plain(a, b, *, tm, tn, name, out_dtype=F32, dims=(1, 0), tk=None):
    m = a.shape[1 - dims[0]]
    kk = a.shape[dims[0]]
    n = b.shape[1 - dims[1]]
    tk = kk if tk is None else tk
    nk = kk // tk

    def a_idx(i, j, k):
        return (i, k) if dims[0] == 1 else (k, i)

    def b_idx(i, j, k):
        return (k, j) if dims[1] == 0 else (j, k)

    a_blk = (tm, tk) if dims[0] == 1 else (tk, tm)
    b_blk = (tk, tn) if dims[1] == 0 else (tn, tk)
    return _matmul(a, b, grid=(m // tm, n // tn, nk),
                   a_spec=pl.BlockSpec(a_blk, a_idx), b_spec=pl.BlockSpec(b_blk, b_idx),
                   o_spec=pl.BlockSpec((tm, tn), lambda i, j, k: (i, j)),
                   out_shape=_sds((m, n), out_dtype), dims=dims, k_axis=2 if nk > 1 else None, name=name)


def _rowwise(fn, tiled, full, outs, accs=(), *, tm, name, after=()):
    args, in_specs = [], []
    for t in tiled:
        if isinstance(t, tuple):
            arr, w, cb = t
            in_specs.append(pl.BlockSpec((tm, w), lambda i, cb=cb: (i, cb)))
        else:
            arr = t
            in_specs.append(pl.BlockSpec((tm, arr.shape[1]), lambda i: (i, 0)))
        args.append(arr)
    rows = args[0].shape[0]
    for f in full:
        in_specs.append(pl.BlockSpec(f.shape, lambda i, nd=f.ndim: (0,) * nd))
        args.append(f)
    out_specs = [pl.BlockSpec((tm, o.shape[1]), lambda i: (i, 0)) for o in outs]
    out_specs += [pl.BlockSpec(a.shape, lambda i, nd=len(a.shape): (0,) * nd) for a in accs]
    n_in, n_out = len(args), len(outs)
    in_specs += [HBM_OPERAND] * len(after)
    first_out = n_in + len(after)

    def body(*refs):
        res = fn(*[r[...] for r in refs[:n_in]])
        res = res if isinstance(res, (tuple, list)) else (res,)
        for r, v in zip(refs[first_out:first_out + n_out], res[:n_out]):
            r[...] = v.astype(r.dtype)
        i = pl.program_id(0)
        for r, v in zip(refs[first_out + n_out:], res[n_out:]):
            @pl.when(i == 0)
            def _(r=r, v=v):
                r[...] = v

            @pl.when(i > 0)
            def _(r=r, v=v):
                r[...] += v

    res = pl.pallas_call(
        body, grid=(rows // tm,), in_specs=in_specs, out_specs=out_specs, out_shape=list(outs) + list(accs),
        compiler_params=_cp(("arbitrary",) if accs else ("parallel",)), name=name)(*args, *after)
    return res


def _colsum(v):
    return jnp.sum(v, axis=0, keepdims=True)


def _ln_stats(z):
    mu = jnp.mean(z, axis=-1, keepdims=True)
    zc = z - mu
    var = jnp.mean(zc * zc, axis=-1, keepdims=True)
    rstd = lax.rsqrt(var + LN_EPS)
    return zc * rstd, rstd


def _ln_bwd(dy, xhat, rstd, g):
    dxh = dy * g
    m1 = jnp.mean(dxh, axis=-1, keepdims=True)
    m2 = jnp.mean(dxh * xhat, axis=-1, keepdims=True)
    return rstd * (dxh - m1 - xhat * m2)


def _swap_halves(t):
    w = t.shape[-1]
    lane = lax.broadcasted_iota(jnp.int32, t.shape, t.ndim - 1)
    return jnp.where((lane % HEAD_DIM) < HEAD_DIM // 2, pltpu.roll(t, w - HEAD_DIM // 2, t.ndim - 1),
                     pltpu.roll(t, HEAD_DIM // 2, t.ndim - 1))


PHASES = max(DILATIONS)
PAIR = 2 * HEAD_DIM
UNITS = SEQ // BLOCK
UNIT_UNROLL = 4
ROPE_ROWS = 256


def _to_phase_rows(t):
    return t.reshape(SEQ // PHASES, PHASES, t.shape[1]).transpose(1, 0, 2).reshape(t.shape)


def _reorder_rows(arr, *, to_phase, name):
    def body(i_ref, o_ref):
        for rho in range(PHASES):
            phase = pl.ds(rho * BLOCK, BLOCK)
            strided = pl.ds(rho, BLOCK, stride=PHASES)
            if to_phase:
                o_ref[phase, :] = i_ref[strided, :]
            else:
                o_ref[strided, :] = i_ref[phase, :]

    spec = pl.BlockSpec((SEQ, BLOCK), lambda j: (0, j))
    return pl.pallas_call(body, grid=(arr.shape[1] // BLOCK,), in_specs=[spec], out_specs=spec,
                          out_shape=_sds(arr.shape), compiler_params=_cp(("parallel",)), name=name)(arr)


def _rope(t, cf, ss):
    return t * cf + _swap_halves(t) * ss


def _rope_transposed(d, cf, ss):
    return d * cf + _swap_halves(d * ss)


def _unit_pieces(u, dil):
    pieces, length = PHASES // dil, 8 * dil
    if dil == 1:
        rho, i = 0, u
    elif dil == PHASES:
        rho, i = u, 0
    else:
        rho, i = jnp.bitwise_and(u, dil - 1), jnp.right_shift(u, dil.bit_length() - 1)
    before = jnp.maximum(i - 1, 0)
    cur = [pl.multiple_of((rho + dil * k) * BLOCK + length * i, 8) for k in range(pieces)]
    prev = [pl.multiple_of((rho + dil * k) * BLOCK + length * before, 8) for k in range(pieces)]
    return i, cur, prev


def _load_tile(ref, starts, dil):
    return jnp.concatenate([ref[pl.ds(st, 8 * dil), :] for st in starts], axis=0)


def _store_tile(ref, starts, dil, val, head=None, accumulate=False):
    length = 8 * dil
    lanes = slice(None) if head is None else pl.ds(head * HEAD_DIM, HEAD_DIM)
    cols = slice(None) if head is None else slice(head * HEAD_DIM, (head + 1) * HEAD_DIM)
    for k, st in enumerate(starts):
        piece = val[k * length:(k + 1) * length, cols]
        if accumulate:
            ref[pl.ds(st, length), lanes] += piece
        else:
            ref[pl.ds(st, length), lanes] = piece


def _tile_position(idx, dil):
    pieces, length = PHASES // dil, 8 * dil
    return pieces * jnp.bitwise_and(idx, length - 1) + jnp.right_shift(idx, length.bit_length() - 1)


def _band_mask(i, dil):
    row = lax.broadcasted_iota(jnp.int32, (BLOCK, 2 * BLOCK), 0)
    col = lax.broadcasted_iota(jnp.int32, (BLOCK, 2 * BLOCK), 1)
    key_pos = _tile_position(jnp.bitwise_and(col, BLOCK - 1), dil) + jnp.where(col >= BLOCK, 0, -BLOCK)
    dist = _tile_position(row, dil) - key_pos
    return (dist >= 0) & (dist <= BLOCK) & ((col >= BLOCK) | (i > 0))


def _causal_mask():
    row = lax.broadcasted_iota(jnp.int32, (BLOCK, BLOCK), 0)
    col = lax.broadcasted_iota(jnp.int32, (BLOCK, BLOCK), 1)
    return row >= col


def _pair_views(col0):
    return [pl.BlockSpec((SEQ, PAIR), lambda hp, g=g: (0, col0 // PAIR + g * (ATTN_WIDTH // PAIR) + hp))
            for g in range(len(DILATIONS))]


def _rotate(in_refs, out_refs, cf_ref, ss_ref, scale):
    def step(t, carry):
        rows = pl.ds(pl.multiple_of(t * ROPE_ROWS, ROPE_ROWS), ROPE_ROWS)
        cf, ss = cf_ref[rows, :] * scale, ss_ref[rows, :] * scale
        for i_ref, o_ref in zip(in_refs, out_refs):
            o_ref[rows, :] = _rope(i_ref[rows, :], cf, ss)
        return carry

    lax.fori_loop(0, SEQ // ROPE_ROWS, step, 0)


def _attention_fwd(proj, cos_f, sin_s):
    ng = len(DILATIONS)

    def body(*refs):
        q_refs, k_refs, v_refs = refs[:ng], refs[ng:2 * ng], refs[2 * ng:3 * ng]
        cf_ref, ss_ref, attn_ref, lse_ref = refs[3 * ng:3 * ng + 4]
        scratch = refs[3 * ng + 4:]
        qr_refs, kr_refs = scratch[:ng], scratch[ng:]
        _rotate(q_refs, qr_refs, cf_ref, ss_ref, 1.0 / math.sqrt(HEAD_DIM))
        _rotate(k_refs, kr_refs, cf_ref, ss_ref, 1.0)
        first = lax.broadcasted_iota(jnp.int32, (BLOCK, PAIR), 1) < HEAD_DIM
        for g, dil in enumerate(DILATIONS):
            two_blocks = SEQ // dil > BLOCK

            def unit(u, carry, g=g, dil=dil, two_blocks=two_blocks):
                i, rows, prev = _unit_pieces(u, dil)
                qq = _load_tile(qr_refs[g], rows, dil).astype(BF16)
                kk = _load_tile(kr_refs[g], rows, dil)
                vv = _load_tile(v_refs[g], rows, dil)
                if two_blocks:
                    kk = jnp.concatenate([_load_tile(kr_refs[g], prev, dil), kk], axis=0)
                    vv = jnp.concatenate([_load_tile(v_refs[g], prev, dil), vv], axis=0)
                    valid = _band_mask(i, dil)
                else:
                    valid = _causal_mask()
                kk, vv = kk.astype(BF16), vv.astype(BF16)
                zero = jnp.zeros_like(qq)
                outs, lses = [], []
                for qh in (jnp.where(first, qq, zero), jnp.where(first, zero, qq)):
                    s = jnp.where(valid, _dot_nt(qh, kk), NEG_INF)
                    m = jnp.max(s, axis=1, keepdims=True)
                    p = jnp.exp(s - m)
                    l = jnp.sum(p, axis=1, keepdims=True)
                    outs.append(jnp.dot(p.astype(BF16), vv, preferred_element_type=F32) * (1.0 / l))
                    lses.append(m + jnp.log(l))
                o = jnp.where(first, outs[0], outs[1])
                lse = jnp.where(first, lses[0], lses[1])
                if g > 0:
                    lse_old = _load_tile(lse_ref, rows, dil)
                    m = jnp.maximum(lse_old, lse)
                    lse_new = m + jnp.log(jnp.exp(lse_old - m) + jnp.exp(lse - m))
                    o = _load_tile(attn_ref, rows, dil) * jnp.exp(lse_old - lse_new) + o * jnp.exp(lse - lse_new)
                    lse = lse_new
                _store_tile(attn_ref, rows, dil, o)
                _store_tile(lse_ref, rows, dil, lse)
                return carry

            lax.fori_loop(0, UNITS, unit, 0, unroll=UNIT_UNROLL)

    whole = pl.BlockSpec((SEQ, PAIR), lambda hp: (0, 0))
    out = pl.BlockSpec((SEQ, PAIR), lambda hp: (0, hp))
    return pl.pallas_call(
        body, grid=(ATTN_WIDTH // PAIR,),
        in_specs=_pair_views(0) + _pair_views(QKV_WIDTH) + _pair_views(2 * QKV_WIDTH) + [whole, whole],
        out_specs=[out, out], out_shape=[_sds((SEQ, ATTN_WIDTH)), _sds((SEQ, ATTN_WIDTH))],
        scratch_shapes=[pltpu.VMEM((SEQ, PAIR), F32)] * (2 * ng),
        compiler_params=_cp(("parallel",)), name="attention_fwd")(*([proj] * (3 * ng)), cos_f, sin_s)


def _attention_bwd(g, proj, cos_f, sin_s, d_attn, attn, lse):
    dil = DILATIONS[g]
    two_blocks = SEQ // dil > BLOCK

    def body(q_ref, k_ref, v_ref, cf_ref, ss_ref, do_ref, o_ref, lse_ref, dq_out, dk_out, dv_out,
             qr_ref, kr_ref, dq_acc, dk_acc, dv_acc):
        _rotate([q_ref], [qr_ref], cf_ref, ss_ref, 1.0 / math.sqrt(HEAD_DIM))
        _rotate([k_ref], [kr_ref], cf_ref, ss_ref, 1.0)
        dk_acc[...] = jnp.zeros_like(dk_acc)
        dv_acc[...] = jnp.zeros_like(dv_acc)
        nk = 2 * BLOCK if two_blocks else BLOCK
        first = lax.broadcasted_iota(jnp.int32, (BLOCK, PAIR), 1) < HEAD_DIM
        first_k = lax.broadcasted_iota(jnp.int32, (nk, PAIR), 1) < HEAD_DIM

        def unit(u, carry):
            i, rows, prev = _unit_pieces(u, dil)
            qq = _load_tile(qr_ref, rows, dil).astype(BF16)
            kk = _load_tile(kr_ref, rows, dil)
            vv = _load_tile(v_ref, rows, dil)
            if two_blocks:
                kk = jnp.concatenate([_load_tile(kr_ref, prev, dil), kk], axis=0)
                vv = jnp.concatenate([_load_tile(v_ref, prev, dil), vv], axis=0)
                valid = _band_mask(i, dil)
            else:
                valid = _causal_mask()
            kk, vv = kk.astype(BF16), vv.astype(BF16)
            dof = _load_tile(do_ref, rows, dil)
            dd = dof * _load_tile(o_ref, rows, dil)
            lse2 = _load_tile(lse_ref, rows, dil)
            dob = dof.astype(BF16)
            zq, zd, zf = jnp.zeros_like(qq), jnp.zeros_like(dob), jnp.zeros_like(dd)
            dqs, dks, dvs = [], [], []
            for head in range(2):
                mine = first if head == 0 else jnp.logical_not(first)
                delta = jnp.sum(jnp.where(mine, dd, zf), axis=1, keepdims=True)
                lse_h = lse2[:, head * HEAD_DIM:head * HEAD_DIM + 1]
                s = _dot_nt(jnp.where(mine, qq, zq), kk)
                p = jnp.where(valid, jnp.exp(s - lse_h), 0.0)
                dp = _dot_nt(jnp.where(mine, dob, zd), vv)
                ds = (p * (dp - delta)).astype(BF16)
                dqs.append(jnp.dot(ds, kk, preferred_element_type=F32))
                dks.append(lax.dot_general(ds, qq, (((0,), (0,)), ((), ())), preferred_element_type=F32))
                dvs.append(lax.dot_general(p.astype(BF16), dob, (((0,), (0,)), ((), ())), preferred_element_type=F32))
            dk = jnp.where(first_k, dks[0], dks[1])
            dv = jnp.where(first_k, dvs[0], dvs[1])
            _store_tile(dq_acc, rows, dil, jnp.where(first, dqs[0], dqs[1]))
            _store_tile(dk_acc, rows, dil, dk[nk - BLOCK:], accumulate=True)
            _store_tile(dv_acc, rows, dil, dv[nk - BLOCK:], accumulate=True)
            if two_blocks:
                _store_tile(dk_acc, prev, dil, dk[:BLOCK], accumulate=True)
                _store_tile(dv_acc, prev, dil, dv[:BLOCK], accumulate=True)
            return carry

        lax.fori_loop(0, UNITS, unit, 0, unroll=UNIT_UNROLL)

        def finish(t, carry):
            rows = pl.ds(pl.multiple_of(t * ROPE_ROWS, ROPE_ROWS), ROPE_ROWS)
            cf, ss = cf_ref[rows, :], ss_ref[rows, :]
            dq = dq_acc[rows, :] * (1.0 / math.sqrt(HEAD_DIM))
            dq_out[rows, :] = _rope_transposed(dq, cf, ss).astype(BF16)
            dk_out[rows, :] = _rope_transposed(dk_acc[rows, :], cf, ss).astype(BF16)
            dv_out[rows, :] = dv_acc[rows, :].astype(BF16)
            return carry

        lax.fori_loop(0, SEQ // ROPE_ROWS, finish, 0)

    whole = pl.BlockSpec((SEQ, PAIR), lambda hp: (0, 0))
    pair = pl.BlockSpec((SEQ, PAIR), lambda hp: (0, hp))
    views = [_pair_views(col0)[g] for col0 in (0, QKV_WIDTH, 2 * QKV_WIDTH)]
    return pl.pallas_call(
        body, grid=(ATTN_WIDTH // PAIR,), in_specs=views + [whole, whole, pair, pair, pair],
        out_specs=[pair, pair, pair], out_shape=[_sds((SEQ, ATTN_WIDTH), BF16)] * 3,
        scratch_shapes=[pltpu.VMEM((SEQ, PAIR), F32)] * 5,
        compiler_params=_cp(("parallel",)), name=f"attention_bwd_{g}")(proj, proj, proj, cos_f, sin_s, d_attn, attn, lse)


def _cmul(ar, ai, br, bi):
    return ar * br - ai * bi, ar * bi + ai * br


def _pow256(ar, ai):
    for _ in range(8):
        ar, ai = _cmul(ar, ai, ar, ai)
    return ar, ai


def _chunk_carries(first_r, first_i, pr, pi, reverse):
    rows = lax.broadcasted_iota(jnp.int32, first_r.shape, 0)
    out_r = jnp.zeros_like(first_r)
    out_i = jnp.zeros_like(first_i)
    hr = jnp.zeros_like(first_r[0:1])
    hi = jnp.zeros_like(hr)
    order = range(SCAN_CHUNKS - 1, -1, -1) if reverse else range(SCAN_CHUNKS)
    for c in order:
        out_r = jnp.where(rows == c, hr, out_r)
        out_i = jnp.where(rows == c, hi, out_i)
        tr, ti = _cmul(pr[0:1], pi[0:1], hr, hi)
        hr = first_r[c:c + 1] + tr
        hi = first_i[c:c + 1] + ti
    return out_r, out_i


def _tile(j):
    return pl.ds(pl.multiple_of(j * SCAN_CHUNKS, SCAN_CHUNKS), SCAN_CHUNKS)


def _to_scan_rows(t):
    per = SCAN_STEPS // PHASES
    return t.reshape(PHASES, SCAN_CHUNKS, per, t.shape[1]).transpose(2, 0, 1, 3).reshape(t.shape)


def _from_scan_rows(t):
    per = SCAN_STEPS // PHASES
    return t.reshape(per, PHASES, SCAN_CHUNKS, t.shape[1]).transpose(1, 2, 0, 3).reshape(t.shape)


def _scan_in_place(hr_ref, hi_ref, a_r, a_i):
    def local(j, carry):
        tr, ti = _cmul(a_r, a_i, carry[0], carry[1])
        nr = tr + hr_ref[_tile(j), :]
        ni = ti + hi_ref[_tile(j), :]
        hr_ref[_tile(j), :] = nr
        hi_ref[_tile(j), :] = ni
        return nr, ni

    zero = jnp.zeros_like(a_r)
    last_r, last_i = lax.fori_loop(0, SCAN_STEPS, local, (zero, zero), unroll=4)
    pr, pi = _pow256(a_r, a_i)
    er, ei = _chunk_carries(last_r, last_i, pr, pi, reverse=False)

    def fix(j, carry):
        tr, ti = _cmul(carry[0], carry[1], er, ei)
        hr_ref[_tile(j), :] += tr
        hi_ref[_tile(j), :] += ti
        return _cmul(carry[0], carry[1], a_r, a_i)

    lax.fori_loop(0, SCAN_STEPS, fix, (a_r, a_i), unroll=4)
    return er, ei


def _reverse_scan_in_place(lr_ref, li_ref, hr_ref, hi_ref, er, ei, a_r, a_i):
    def local(t, carry):
        j = SCAN_STEPS - 1 - t
        tr, ti = _cmul(a_r, a_i, carry[0], carry[1])
        nr = tr + lr_ref[_tile(j), :]
        ni = ti + li_ref[_tile(j), :]
        lr_ref[_tile(j), :] = nr
        li_ref[_tile(j), :] = ni
        return nr, ni

    zero = jnp.zeros_like(a_r)
    first_r, first_i = lax.fori_loop(0, SCAN_STEPS, local, (zero, zero), unroll=4)
    pr, pi = _pow256(a_r, a_i)
    nxt_r, nxt_i = _chunk_carries(first_r, first_i, pr, pi, reverse=True)

    def accumulate(lam_r, lam_i, hp_r, hp_i, acc):
        return (acc[0] + lam_r * hp_r + lam_i * hp_i, acc[1] + lam_i * hp_r - lam_r * hp_i)

    def fix(t, carry):
        qr, qi, acc_r, acc_i = carry
        j = SCAN_STEPS - 1 - t
        tr, ti = _cmul(qr, qi, nxt_r, nxt_i)
        lam_r = lr_ref[_tile(j), :] + tr
        lam_i = li_ref[_tile(j), :] + ti
        lr_ref[_tile(j), :] = lam_r
        li_ref[_tile(j), :] = lam_i
        acc_r, acc_i = accumulate(lam_r, lam_i, hr_ref[_tile(j - 1), :], hi_ref[_tile(j - 1), :], (acc_r, acc_i))
        qr, qi = _cmul(qr, qi, a_r, a_i)
        return qr, qi, acc_r, acc_i

    qr, qi, acc_r, acc_i = lax.fori_loop(0, SCAN_STEPS - 1, fix, (a_r, a_i, zero, zero), unroll=4)
    tr, ti = _cmul(qr, qi, nxt_r, nxt_i)
    lam_r = lr_ref[_tile(0), :] + tr
    lam_i = li_ref[_tile(0), :] + ti
    lr_ref[_tile(0), :] = lam_r
    li_ref[_tile(0), :] = lam_i
    acc_r, acc_i = accumulate(lam_r, lam_i, er, ei, (acc_r, acc_i))
    return jnp.sum(acc_r, axis=0, keepdims=True), jnp.sum(acc_i, axis=0, keepdims=True)


def _rope_tables():
    half = HEAD_DIM // 2
    inv_freq = ROPE_THETA ** (-jnp.arange(half, dtype=F32) / half)
    ang = jnp.arange(SEQ, dtype=F32)[:, None] * inv_freq[None, :]
    cos, sin = jnp.cos(ang), jnp.sin(ang)
    cos_f = jnp.concatenate([cos, cos, cos, cos], axis=1)
    sin_s = jnp.concatenate([-sin, sin, -sin, sin], axis=1)
    return cos_f, sin_s


def _ssm_discretise(a_re, a_im, log_dt, b_re, b_im):
    lam = lax.complex(a_re, a_im)
    dt = jnp.exp(log_dt)[:, None]
    a_bar = jnp.exp(lam * dt)
    b_bar = ((a_bar - 1.0) / lam)[..., None] * lax.complex(b_re, b_im)
    return a_bar.real, a_bar.imag, b_bar.real, b_bar.imag


SSM_SLABS = 4
SLAB_GROUPS = SSM_GROUPS // SSM_SLABS
SLAB_IN = SSM_WIDTH // SSM_SLABS
SLAB_STATE = SSM_LANES // SSM_SLABS


def _slab_block_diag(blocks):
    _, r, c = blocks.shape
    eye = jnp.eye(SLAB_GROUPS, dtype=blocks.dtype)
    b5 = blocks.reshape(SSM_SLABS, SLAB_GROUPS, r, 1, c) * eye[None, :, None, :, None]
    return b5.reshape(SSM_SLABS, SLAB_GROUPS * r, SLAB_GROUPS * c)


def _diag_blocks(a, b):
    ra, cb = a.shape[1], b.shape[1]
    wa, wb = ra // SLAB_GROUPS, cb // SLAB_GROUPS
    d = lax.dot_general(a, b, (((0,), (0,)), ((), ())), preferred_element_type=F32)
    row_g = jnp.right_shift(lax.broadcasted_iota(jnp.int32, (ra, cb), 0), wa.bit_length() - 1)
    col_g = jnp.right_shift(lax.broadcasted_iota(jnp.int32, (ra, cb), 1), wb.bit_length() - 1)
    d = jnp.where(row_g == col_g, d, 0.0)
    fold = (jnp.bitwise_and(lax.broadcasted_iota(jnp.int32, (cb, wb), 0), wb - 1)
            == lax.broadcasted_iota(jnp.int32, (cb, wb), 1)).astype(F32)
    return jnp.dot(d, fold, preferred_element_type=F32, precision=lax.Precision.HIGHEST)


def _slab_specs():
    tok = pl.BlockSpec((SEQ, SLAB_IN), lambda j: (0, j))
    state = pl.BlockSpec((SEQ, SLAB_STATE), lambda j: (0, j))
    b_in = pl.BlockSpec((None, SLAB_IN, SLAB_STATE), lambda j: (j, 0, 0))
    c_out = pl.BlockSpec((None, SLAB_STATE, SLAB_IN), lambda j: (j, 0, 0))
    vec = pl.BlockSpec((1, SLAB_STATE), lambda j: (0, j))
    ent = pl.BlockSpec((SCAN_CHUNKS, SLAB_STATE), lambda j: (0, j))
    return tok, state, b_in, c_out, vec, ent


def _ssm_forward(u, b_in_r, b_in_i, c_out_r, c_out_ni, a_r, a_i):
    def body(u_ref, br_ref, bi_ref, cr_ref, ci_ref, ar_ref, ai_ref, y_ref, hr_ref, hi_ref, er_ref, ei_ref):
        uu = u_ref[...]
        hr_ref[...] = jnp.dot(uu, br_ref[...], preferred_element_type=F32)
        hi_ref[...] = jnp.dot(uu, bi_ref[...], preferred_element_type=F32)
        a_re = jnp.broadcast_to(ar_ref[...], (SCAN_CHUNKS, SLAB_STATE))
        a_im = jnp.broadcast_to(ai_ref[...], (SCAN_CHUNKS, SLAB_STATE))
        er_ref[...], ei_ref[...] = _scan_in_place(hr_ref, hi_ref, a_re, a_im)
        y_ref[...] = (jnp.dot(hr_ref[...].astype(BF16), cr_ref[...], preferred_element_type=F32)
                      + jnp.dot(hi_ref[...].astype(BF16), ci_ref[...], preferred_element_type=F32))

    tok, state, b_in, c_out, vec, ent = _slab_specs()
    return pl.pallas_call(
        body, grid=(SSM_SLABS,), in_specs=[tok, b_in, b_in, c_out, c_out, vec, vec],
        out_specs=[tok, state, state, ent, ent],
        out_shape=[_sds((SEQ, SSM_WIDTH)), _sds((SEQ, SSM_LANES)), _sds((SEQ, SSM_LANES)),
                   _sds((SCAN_CHUNKS, SSM_LANES)), _sds((SCAN_CHUNKS, SSM_LANES))],
        compiler_params=_cp(("parallel",)), name="ssm_forward")(u, b_in_r, b_in_i, c_out_r, c_out_ni, a_r, a_i)


def _ssm_backward(d_y, d_u_skip, u, h_r, h_i, e_r, e_i, b_in_r, b_in_i, c_out_r, c_out_ni, a_r, a_i):
    def body(dy_ref, skip_ref, u_ref, hr_ref, hi_ref, er_ref, ei_ref, br_ref, bi_ref, cr_ref, ci_ref, ar_ref, ai_ref,
             du_ref, dar_ref, dai_ref, dcr_ref, dci_ref, dbr_ref, dbi_ref, lr_ref, li_ref):
        dy = dy_ref[...]
        lr_ref[...] = _dot_nt(dy, cr_ref[...])
        li_ref[...] = _dot_nt(dy, ci_ref[...])
        a_re = jnp.broadcast_to(ar_ref[...], (SCAN_CHUNKS, SLAB_STATE))
        a_im = -jnp.broadcast_to(ai_ref[...], (SCAN_CHUNKS, SLAB_STATE))
        dar_ref[...], dai_ref[...] = _reverse_scan_in_place(lr_ref, li_ref, hr_ref, hi_ref, er_ref[...], ei_ref[...],
                                                            a_re, a_im)
        dcr_ref[...] = _diag_blocks(hr_ref[...].astype(BF16), dy)
        dci_ref[...] = _diag_blocks(hi_ref[...].astype(BF16), dy)
        lam_r, lam_i = lr_ref[...].astype(BF16), li_ref[...].astype(BF16)
        uu = u_ref[...]
        dbr_ref[...] = _diag_blocks(uu, lam_r)
        dbi_ref[...] = _diag_blocks(uu, lam_i)
        du = skip_ref[...] + _dot_nt(lam_r, br_ref[...]) + _dot_nt(lam_i, bi_ref[...])
        du_ref[...] = du.astype(BF16)

    tok, state, b_in, c_out, vec, ent = _slab_specs()
    dc = pl.BlockSpec((SLAB_STATE, SSM_GROUP), lambda j: (j, 0))
    db = pl.BlockSpec((SLAB_IN, SSM_STATE), lambda j: (j, 0))
    return pl.pallas_call(
        body, grid=(SSM_SLABS,), in_specs=[tok, tok, tok, state, state, ent, ent, b_in, b_in, c_out, c_out, vec, vec],
        out_specs=[tok, vec, vec, dc, dc, db, db],
        out_shape=[_sds((SEQ, SSM_WIDTH), BF16), _sds((1, SSM_LANES)), _sds((1, SSM_LANES)),
                   _sds((SSM_LANES, SSM_GROUP)), _sds((SSM_LANES, SSM_GROUP)),
                   _sds((SSM_WIDTH, SSM_STATE)), _sds((SSM_WIDTH, SSM_STATE))],
        scratch_shapes=[pltpu.VMEM((SEQ, SLAB_STATE), F32)] * 2,
        compiler_params=_cp(("parallel",)), name="ssm_backward")(
            d_y, d_u_skip, u, h_r, h_i, e_r, e_i, b_in_r, b_in_i, c_out_r, c_out_ni, a_r, a_i)


FF_ROWS = 1024
FF_SHARD = D_FF // N_CHIPS


def _dot_nt(a, b):
    return lax.dot_general(a, b, (((1,), (1,)), ((), ())), preferred_element_type=F32)


def _ffn_up(h, w_gate_t, w_up_t):
    def body(h_ref, wg_ref, wu_ref, a_ref, b_ref, act_ref):
        hb = h_ref[...].astype(BF16)
        a = _dot_nt(hb, wg_ref[...])
        b = _dot_nt(hb, wu_ref[...])
        a_ref[...] = a
        b_ref[...] = b
        act_ref[...] = (a * jax.nn.sigmoid(a) * b).astype(BF16)

    w_spec = pl.BlockSpec((None, FF_SHARD, D_MODEL), lambda i, k: (k, 0, 0))
    o_spec = pl.BlockSpec((None, FF_ROWS, FF_SHARD), lambda i, k: (k, i, 0))
    shape = (N_CHIPS, SEQ, FF_SHARD)
    return pl.pallas_call(
        body, grid=(SEQ // FF_ROWS, N_CHIPS),
        in_specs=[pl.BlockSpec((FF_ROWS, D_MODEL), lambda i, k: (i, 0)), w_spec, w_spec],
        out_specs=[o_spec, o_spec, o_spec], out_shape=[_sds(shape), _sds(shape), _sds(shape, BF16)],
        compiler_params=_cp(("parallel", "parallel")), name="ffn_up")(h, w_gate_t, w_up_t)


def _ffn_down_bwd(dz, w_down, a, b):
    def body(dz_ref, wd_ref, a_ref, b_ref, da_ref, db_ref):
        d_act = _dot_nt(dz_ref[...].astype(BF16), wd_ref[...])
        av = a_ref[...]
        sg = jax.nn.sigmoid(av)
        da_ref[...] = (d_act * b_ref[...] * sg * (1.0 + av * (1.0 - sg))).astype(BF16)
        db_ref[...] = (d_act * av * sg).astype(BF16)

    t_spec = pl.BlockSpec((None, FF_ROWS, FF_SHARD), lambda i, k: (k, i, 0))
    shape = (N_CHIPS, SEQ, FF_SHARD)
    return pl.pallas_call(
        body, grid=(SEQ // FF_ROWS, N_CHIPS),
        in_specs=[pl.BlockSpec((FF_ROWS, D_MODEL), lambda i, k: (i, 0)),
                  pl.BlockSpec((None, FF_SHARD, D_MODEL), lambda i, k: (k, 0, 0)), t_spec, t_spec],
        out_specs=[t_spec, t_spec], out_shape=[_sds(shape, BF16), _sds(shape, BF16)],
        compiler_params=_cp(("parallel", "parallel")), name="ffn_down_bwd")(dz, w_down, a, b)


def _ffn_dh(d_a, d_b, w_gate_t, w_up_t):
    def body(da_ref, db_ref, wg_ref, wu_ref, o_ref, acc):
        k = pl.program_id(1)
        part = (jnp.dot(da_ref[...], wg_ref[...], preferred_element_type=F32)
                + jnp.dot(db_ref[...], wu_ref[...], preferred_element_type=F32))

        @pl.when(k == 0)
        def _():
            acc[...] = part

        @pl.when(k > 0)
        def _():
            acc[...] += part

        @pl.when(k == N_CHIPS - 1)
        def _():
            o_ref[...] = acc[...]

    t_spec = pl.BlockSpec((None, FF_ROWS, FF_SHARD), lambda i, k: (k, i, 0))
    w_spec = pl.BlockSpec((None, FF_SHARD, D_MODEL), lambda i, k: (k, 0, 0))
    return pl.pallas_call(
        body, grid=(SEQ // FF_ROWS, N_CHIPS), in_specs=[t_spec, t_spec, w_spec, w_spec],
        out_specs=pl.BlockSpec((FF_ROWS, D_MODEL), lambda i, k: (i, 0)), out_shape=_sds((SEQ, D_MODEL)),
        scratch_shapes=[pltpu.VMEM((FF_ROWS, D_MODEL), F32)],
        compiler_params=_cp(("parallel", "arbitrary")), name="ffn_dh")(d_a, d_b, w_gate_t, w_up_t)


def _local_step(x, tgt, wts, small):
    s = SEQ
    cos_f, sin_s = [_to_phase_rows(t) for t in _rope_tables()]
    x = _reorder_rows(x, to_phase=True, name="phase_rows_x")
    tgt = _reorder_rows(tgt, to_phase=True, name="phase_rows_target")

    proj = _mm_cols(x, wts["w_in"], tm=1024, name="proj")

    attn, lse = _attention_fwd(proj, cos_f, sin_s)
    y_attn = _mm_cols(attn, wts["w_attn_br"], tm=s, name="y_attn")

    (abar_r, abar_i, bbar_r, bbar_i), ssm_vjp = jax.vjp(
        _ssm_discretise, small["ssm_a_re"], small["ssm_a_im"], small["ssm_log_dt"], small["ssm_b_re"], small["ssm_b_im"])
    b_in_r, b_in_i = [_slab_block_diag(b.transpose(0, 2, 1)).astype(BF16) for b in (bbar_r, bbar_i)]
    c_out_r = _slab_block_diag(small["ssm_c_re"].transpose(0, 2, 1)).astype(BF16)
    c_out_ni = _slab_block_diag(-small["ssm_c_im"].transpose(0, 2, 1)).astype(BF16)
    a_r, a_i = abar_r.reshape(1, SSM_LANES), abar_i.reshape(1, SSM_LANES)
    d_skip = small["ssm_d"].reshape(1, SSM_WIDTH)

    u_f = _to_scan_rows(proj[:, 3 * QKV_WIDTH:3 * QKV_WIDTH + SSM_WIDTH])
    u_p = u_f.astype(BF16)
    y_c, h_r, h_i, e_r, e_i = _ssm_forward(u_p, b_in_r, b_in_i, c_out_r, c_out_ni, a_r, a_i)

    def gelu_fwd(yc, u, dsk):
        y = yc + dsk * u
        return y, 0.5 * y * (1.0 + jnp.tanh(GELU_C * (y + GELU_K * y * y * y)))

    y_s5, gel = _rowwise(gelu_fwd, [y_c, u_f], [d_skip], [_sds((s, SSM_WIDTH)), _sds((s, SSM_WIDTH), BF16)],
                         tm=512, name="ssm_gelu")
    glu = _mm_cols(gel, wts["w_glu"], tm=s, name="glu")

    def glu_fwd(ga, gb):
        return ga * jax.nn.sigmoid(gb)

    (y_glu,) = _rowwise(glu_fwd, [(glu, SSM_WIDTH, 0), (glu, SSM_WIDTH, 1)], [], [_sds((s, SSM_WIDTH), BF16)],
                        tm=512, name="glu_gate")
    y_glu = _from_scan_rows(y_glu)
    y_ssm = _mm_cols(y_glu, wts["w_ssm_br"], tm=s, name="y_ssm")

    gl0 = (proj, D_MODEL, (3 * QKV_WIDTH + SSM_WIDTH) // D_MODEL)
    gl1 = (proj, D_MODEL, (3 * QKV_WIDTH + SSM_WIDTH) // D_MODEL + 1)
    b_gate = small["b_gate"]

    def gate_mix(l0, l1, ya, ys, bg):
        return jax.nn.sigmoid(l0 + bg[0:1]) * ya + jax.nn.sigmoid(l1 + bg[1:2]) * ys

    (mixed,) = _rowwise(gate_mix, [gl0, gl1, y_attn, y_ssm], [b_gate], [_sds((s, D_MODEL), BF16)], tm=256,
                        name="gate_mix")
    w_out = wts["w_out"].reshape(D_MODEL, D_MODEL)
    mix_out = _mm_plain(mixed, w_out, tm=1024, tn=512, name="mix_out")

    def ln1_fwd(xv, mo, g, b):
        z = DN_ALPHA * xv + mo
        xhat, _ = _ln_stats(z)
        return z, xhat * g + b

    z1, h = _rowwise(ln1_fwd, [x, mix_out], [small["ln1_g"], small["ln1_b"]],
                     [_sds((s, D_MODEL)), _sds((s, D_MODEL))], tm=256, name="ln1")

    nf = D_FF // N_CHIPS
    w_gate_t, w_up_t, w_down = wts["w_ff_gate"], wts["w_ff_up"], wts["w_ff_down"]
    ff_a, ff_b, act = _ffn_up(h, w_gate_t, w_up_t)
    ff = _matmul(act, w_down, grid=(2, N_CHIPS),
                 a_spec=pl.BlockSpec((None, 1024, nf), lambda i, k: (k, i, 0)),
                 b_spec=pl.BlockSpec((None, nf, D_MODEL), lambda i, k: (k, 0, 0)),
                 o_spec=pl.BlockSpec((1024, D_MODEL), lambda i, k: (i, 0)),
                 out_shape=_sds((s, D_MODEL)), dims=(1, 0), k_axis=1, name="ff_down")

    def ln2_loss(hv, ffv, tg, g, b):
        z = DN_ALPHA * hv + ffv
        xhat, rstd = _ln_stats(z)
        err = xhat * g + b - tg
        d_out = err * (1.0 / D_MODEL)
        loss_rows = jnp.sum(err * err, axis=-1, keepdims=True) * (0.5 / D_MODEL)
        loss = jnp.broadcast_to(jnp.sum(loss_rows, axis=0, keepdims=True), (1, 128))
        return _ln_bwd(d_out, xhat, rstd, g), loss, _colsum(d_out * xhat), _colsum(d_out)

    dz2, loss_v, d_ln2_g, d_ln2_b = _rowwise(
        ln2_loss, [h, ff, tgt], [small["ln2_g"], small["ln2_b"]], [_sds((s, D_MODEL))],
        [_sds((1, 128)), _sds((1, D_MODEL)), _sds((1, D_MODEL))], tm=256, name="ln2_loss")

    d_a, d_b = _ffn_down_bwd(dz2, w_down, ff_a, ff_b)

    def grad_rows(lhs, rhs, name):
        return _matmul(lhs, rhs, grid=(N_CHIPS,), a_spec=pl.BlockSpec((None, s, nf), lambda k: (k, 0, 0)),
                       b_spec=pl.BlockSpec((s, D_MODEL), lambda k: (0, 0)),
                       o_spec=pl.BlockSpec((None, nf, D_MODEL), lambda k: (k, 0, 0)),
                       out_shape=_sds((N_CHIPS, nf, D_MODEL), BF16), dims=(0, 0), name=name)

    g_w_ff_down = grad_rows(act, dz2, "g_w_ff_down")
    g_w_ff_gate = grad_rows(d_a, h, "g_w_ff_gate")
    g_w_ff_up = grad_rows(d_b, h, "g_w_ff_up")
    dh_ff = _ffn_dh(d_a, d_b, w_gate_t, w_up_t)

    def ln1_bwd(dz, dff, z, g):
        xhat, rstd = _ln_stats(z)
        dh = DN_ALPHA * dz + dff
        return _ln_bwd(dh, xhat, rstd, g), _colsum(dh * xhat), _colsum(dh)

    dz1, d_ln1_g, d_ln1_b = _rowwise(ln1_bwd, [dz2, dh_ff, z1], [small["ln1_g"]], [_sds((s, D_MODEL))],
                                     [_sds((1, D_MODEL)), _sds((1, D_MODEL))], tm=256, name="ln1_bwd")
    d_mixed = _mm_plain(dz1, w_out, tm=1024, tn=512, dims=(1, 1), name="d_mixed")
    g_w_out = _mm_plain(mixed, dz1, tm=D_MODEL, tn=512, dims=(0, 0), out_dtype=BF16, name="g_w_out")
    g_w_out = g_w_out.reshape(N_CHIPS, D_MODEL // N_CHIPS, D_MODEL)

    def gate_bwd(dm, l0, l1, ya, ys, bg):
        g0 = jax.nn.sigmoid(l0 + bg[0:1])
        g1 = jax.nn.sigmoid(l1 + bg[1:2])
        dl0 = dm * ya * g0 * (1.0 - g0)
        dl1 = dm * ys * g1 * (1.0 - g1)
        return dm * g0, dm * g1, jnp.concatenate([dl0, dl1], axis=1), _colsum(dl0), _colsum(dl1)

    d_y_attn, d_y_ssm, d_gl, d_bg0, d_bg1 = _rowwise(
        gate_bwd, [d_mixed, gl0, gl1, y_attn, y_ssm], [b_gate],
        [_sds((s, D_MODEL), BF16), _sds((s, D_MODEL), BF16), _sds((s, 2 * D_MODEL), BF16)],
        [_sds((1, D_MODEL)), _sds((1, D_MODEL))], tm=256, name="gate_bwd")

    g_w_ssm_br = _mm_cols_tn(y_glu, d_y_ssm, ns=D_MODEL // N_CHIPS, name="g_w_ssm_br")
    d_y_glu = _to_scan_rows(_mm_cols_nt(d_y_ssm, wts["w_ssm_br"], tm=s, name="d_y_glu"))

    def glu_bwd(dy, ga, gb):
        sg = jax.nn.sigmoid(gb)
        return jnp.concatenate([dy * sg, dy * ga * sg * (1.0 - sg)], axis=1)

    (d_glu,) = _rowwise(glu_bwd, [d_y_glu, (glu, SSM_WIDTH, 0), (glu, SSM_WIDTH, 1)], [],
                        [_sds((s, 2 * SSM_WIDTH), BF16)], tm=512, name="glu_bwd")
    g_w_glu = _mm_cols_tn(gel, d_glu, ns=2 * SSM_WIDTH // N_CHIPS, name="g_w_glu")
    d_gel = _mm_cols_nt(d_glu, wts["w_glu"], tm=s, name="d_gel")

    def gelu_bwd(dg, y, u, dsk):
        th = jnp.tanh(GELU_C * (y + GELU_K * y * y * y))
        dy = dg * (0.5 * (1.0 + th) + 0.5 * y * (1.0 - th * th) * GELU_C * (1.0 + 3.0 * GELU_K * y * y))
        return dy, dy * dsk, _colsum(dy * u)

    d_y, d_u_skip, d_ssm_d = _rowwise(gelu_bwd, [d_gel, y_s5, u_f], [d_skip],
                                      [_sds((s, SSM_WIDTH), BF16), _sds((s, SSM_WIDTH))], [_sds((1, SSM_WIDTH))],
                                      tm=512, name="gelu_bwd")
    d_u, d_abar_r, d_abar_i, d_c_r, d_c_ni, d_bin_r, d_bin_i = _ssm_backward(
        d_y, d_u_skip, u_p, h_r, h_i, e_r, e_i, b_in_r, b_in_i, c_out_r, c_out_ni, a_r, a_i)
    d_u = _from_scan_rows(d_u)
    d_bbar_r = d_bin_r.reshape(SSM_GROUPS, SSM_GROUP, SSM_STATE).transpose(0, 2, 1)
    d_bbar_i = d_bin_i.reshape(SSM_GROUPS, SSM_GROUP, SSM_STATE).transpose(0, 2, 1)
    d_a_re, d_a_im, d_log_dt, d_b_re, d_b_im = ssm_vjp(
        (d_abar_r.reshape(SSM_GROUPS, SSM_STATE), d_abar_i.reshape(SSM_GROUPS, SSM_STATE), d_bbar_r, d_bbar_i))
    d_c_re = d_c_r.reshape(SSM_GROUPS, SSM_STATE, SSM_GROUP).transpose(0, 2, 1)
    d_c_im = -d_c_ni.reshape(SSM_GROUPS, SSM_STATE, SSM_GROUP).transpose(0, 2, 1)

    g_w_attn_br = _mm_cols_tn(attn, d_y_attn, ns=D_MODEL // N_CHIPS, name="g_w_attn_br")
    d_attn = _mm_cols_nt(d_y_attn, wts["w_attn_br"], tm=s, name="d_attn")
    dqkv = [_attention_bwd(g, proj, cos_f, sin_s, d_attn, attn, lse) for g in range(len(DILATIONS))]

    d_proj = jnp.concatenate([dqkv[g][j] for j in range(3) for g in range(len(DILATIONS))] + [d_u, d_gl],
                             axis=1)
    dx_proj = _mm_cols_nt(d_proj, wts["w_in"], tm=1024, name="dx_proj")
    g_w_in = _mm_cols_tn(x, d_proj, ns=IN_WIDTH // N_CHIPS, name="g_w_in", after=(dx_proj,))

    def dx_sum(dz, dxp):
        return DN_ALPHA * dz + dxp

    (grad_x,) = _rowwise(dx_sum, [dz1, dx_proj], [], [_sds((s, D_MODEL))], tm=512, name="grad_x", after=(g_w_in,))
    grad_x = _reorder_rows(grad_x, to_phase=False, name="time_rows_grad_x")

    big = {"w_in": g_w_in, "w_attn_br": g_w_attn_br, "w_ssm_br": g_w_ssm_br, "w_out": g_w_out, "w_glu": g_w_glu,
           "w_ff_gate": g_w_ff_gate, "w_ff_up": g_w_ff_up, "w_ff_down": g_w_ff_down}
    small_g = {"b_gate": jnp.concatenate([d_bg0, d_bg1], axis=0), "ssm_a_re": d_a_re, "ssm_a_im": d_a_im,
               "ssm_log_dt": d_log_dt, "ssm_b_re": d_b_re, "ssm_b_im": d_b_im, "ssm_c_re": d_c_re, "ssm_c_im": d_c_im,
               "ssm_d": d_ssm_d.reshape(SSM_WIDTH), "ln1_g": d_ln1_g, "ln1_b": d_ln1_b, "ln2_g": d_ln2_g,
               "ln2_b": d_ln2_b}
    marks = {"ln1_bwd": dz1, "scan_bwd": d_abar_r,"attention_bwd_0": dqkv[0][0], "dx_proj": dx_proj}
    return loss_v[0, 0], grad_x, big, small_g, marks


GATHER_ID, SWAP_ID, SCATTER_ID, JOIN_ID, EXCHANGE_ID = 1, 2, 3, 4, 5


def _place():
    return lax.axis_index("x"), lax.axis_index("y"), lax.axis_index("c")


def _other_chips(x, y):
    return [(1 - x, y), (x, 1 - y), (1 - x, 1 - y)]


def _handshake(peers):
    barrier = pltpu.get_barrier_semaphore()
    for peer in peers:
        pl.semaphore_signal(barrier, inc=1, device_id=peer, device_id_type=MESH)
    pl.semaphore_wait(barrier, len(peers))


def _sequencer(body, arrays, out_type, sems, collective_id, name):
    return pl.kernel(body, name=name, out_type=out_type,
                     mesh=plsc.ScalarSubcoreMesh(axis_name="sequencer", num_cores=1), scratch_types=sems,
                     compiler_params=pltpu.CompilerParams(collective_id=collective_id))(*arrays)


def _gather_weights(shards, *, name):
    nw = len(shards)

    def body(*refs):
        ins, outs = refs[:nw], refs[nw:2 * nw]
        send_sems, recv_sems, pass_send, pass_recv, local_sems = refs[2 * nw:]
        x, y, c = _place()
        chip = 2 * x + y
        chips = _other_chips(x, y)
        _handshake([(x, y, 1 - c)] + [(cx, cy, c) for cx, cy in chips])
        started = []
        for w in range(nw):
            hw = shards[w].shape[0] // 2
            mine = pl.ds(c * hw, hw)
            own = pltpu.make_async_copy(ins[w], outs[w].at[chip], local_sems.at[w])
            own.start()
            started.append(own)
            for j, (cx, cy) in enumerate(chips):
                cp = pltpu.make_async_remote_copy(
                    src_ref=ins[w].at[mine], dst_ref=outs[w].at[chip, mine], send_sem=send_sems.at[w, j],
                    recv_sem=recv_sems.at[w, j], device_id=(cx, cy, c), device_id_type=MESH)
                cp.start()
                started.append(cp)
        passed = []
        for w in range(nw):
            hw = shards[w].shape[0] // 2
            mine = pl.ds(c * hw, hw)
            for j, (cx, cy) in enumerate(chips):
                landed = outs[w].at[2 * cx + cy, mine]
                pltpu.make_async_remote_copy(
                    src_ref=ins[w].at[mine], dst_ref=landed, send_sem=send_sems.at[w, j],
                    recv_sem=recv_sems.at[w, j], device_id=(cx, cy, c), device_id_type=MESH).wait_recv()
                cp = pltpu.make_async_remote_copy(
                    src_ref=landed, dst_ref=landed, send_sem=pass_send.at[w, j], recv_sem=pass_recv.at[w, j],
                    device_id=(x, y, 1 - c), device_id_type=MESH)
                cp.start()
                passed.append(cp)
        for w in range(nw):
            hw = shards[w].shape[0] // 2
            theirs = pl.ds((1 - c) * hw, hw)
            for j, (cx, cy) in enumerate(chips):
                landed = outs[w].at[2 * cx + cy, theirs]
                pltpu.make_async_remote_copy(
                    src_ref=landed, dst_ref=landed, send_sem=pass_send.at[w, j], recv_sem=pass_recv.at[w, j],
                    device_id=(x, y, 1 - c), device_id_type=MESH).wait_recv()
        for cp in started[0::4]:
            cp.wait()
        for cp in [s for i, s in enumerate(started) if i % 4] + passed:
            cp.wait_send()

    sem = pltpu.SemaphoreType.DMA
    return _sequencer(body, shards, [_sds((N_CHIPS,) + a.shape, a.dtype) for a in shards],
                      [sem((nw, 3)), sem((nw, 3)), sem((nw, 3)), sem((nw, 3)), sem((nw,))], GATHER_ID, name)


def _swap_other_halves(grads, *, name):
    nw = len(grads)

    def body(*refs):
        ins, outs = refs[:nw], refs[nw:2 * nw]
        send_sems, recv_sems = refs[2 * nw:]
        x, y, c = _place()
        _handshake([(x, y, 1 - c)])
        cps = []
        for w in range(nw):
            hw = grads[w].shape[1] // 2
            cp = pltpu.make_async_remote_copy(
                src_ref=ins[w].at[:, pl.ds((1 - c) * hw, hw)], dst_ref=outs[w], send_sem=send_sems.at[w],
                recv_sem=recv_sems.at[w], device_id=(x, y, 1 - c), device_id_type=MESH)
            cp.start()
            cps.append(cp)
        for cp in cps:
            cp.wait()

    sem = pltpu.SemaphoreType.DMA
    return _sequencer(body, grads, [_sds((N_CHIPS, g.shape[1] // 2, g.shape[2]), g.dtype) for g in grads],
                      [sem((nw,)), sem((nw,))], SWAP_ID, name)


def _add_my_halves(core, grads, others, *, name, after=()):
    nw = len(grads)
    halves = [g.shape[1] // 2 for g in grads]

    def body(core_ref, *refs):
        outs = refs[2 * nw + len(after):]
        for g_ref, o_ref, out_ref in zip(refs[:nw], refs[nw:2 * nw], outs):
            out_ref[...] = (g_ref[...].astype(F32) + o_ref[...].astype(F32)).astype(out_ref.dtype)

    in_specs = [pl.BlockSpec((None, None, hw, g.shape[2]), lambda s, core_ref: (s, core_ref[0], 0, 0))
                for g, hw in zip(grads, halves)]
    in_specs += [pl.BlockSpec((None, hw, g.shape[2]), lambda s, core_ref: (s, 0, 0)) for g, hw in zip(grads, halves)]
    return pl.pallas_call(
        body,
        grid_spec=pltpu.PrefetchScalarGridSpec(
            num_scalar_prefetch=1, grid=(N_CHIPS,), in_specs=in_specs + [HBM_OPERAND] * len(after),
            out_specs=[pl.BlockSpec((None, hw, g.shape[2]), lambda s, core_ref: (s, 0, 0))
                       for g, hw in zip(grads, halves)]),
        out_shape=[_sds((N_CHIPS, hw, g.shape[2]), BF16) for g, hw in zip(grads, halves)],
        compiler_params=_cp(("parallel",)), name=name)(
            core, *[g.reshape(N_CHIPS, 2, hw, g.shape[2]) for g, hw in zip(grads, halves)], *others, *after)


def _scatter_partials(parts, *, name):
    nw = len(parts)

    def body(*refs):
        ins, outs = refs[:nw], refs[nw:2 * nw]
        send_sems, recv_sems = refs[2 * nw:]
        x, y, c = _place()
        _handshake([(cx, cy, c) for cx, cy in _other_chips(x, y)])
        cps = []
        for w in range(nw):
            for j, (cx, cy) in enumerate(_other_chips(x, y)):
                cp = pltpu.make_async_remote_copy(
                    src_ref=ins[w].at[2 * cx + cy], dst_ref=outs[w].at[j], send_sem=send_sems.at[w, j],
                    recv_sem=recv_sems.at[w, j], device_id=(cx, cy, c), device_id_type=MESH)
                cp.start()
                cps.append(cp)
        for cp in cps:
            cp.wait()

    sem = pltpu.SemaphoreType.DMA
    return _sequencer(body, parts, [_sds((3,) + p.shape[1:], p.dtype) for p in parts],
                      [sem((nw, 3)), sem((nw, 3))], SCATTER_ID, name)


SUM_STEPS = 2


def _sum_partials(chip, parts, recvd, *, name, after=()):
    nw = len(parts)
    rows = [p.shape[1] // SUM_STEPS for p in parts]

    def body(chip_ref, *refs):
        outs = refs[2 * nw + len(after):]
        for p_ref, r_ref, out_ref in zip(refs[:nw], refs[nw:2 * nw], outs):
            acc = p_ref[...].astype(F32)
            for j in range(3):
                acc = acc + r_ref[j].astype(F32)
            out_ref[...] = acc

    in_specs = [pl.BlockSpec((None, th, p.shape[2]), lambda i, chip_ref: (chip_ref[0], i, 0))
                for p, th in zip(parts, rows)]
    in_specs += [pl.BlockSpec((3, th, p.shape[2]), lambda i, chip_ref: (0, i, 0)) for p, th in zip(parts, rows)]
    return pl.pallas_call(
        body,
        grid_spec=pltpu.PrefetchScalarGridSpec(
            num_scalar_prefetch=1, grid=(SUM_STEPS,), in_specs=in_specs + [HBM_OPERAND] * len(after),
            out_specs=[pl.BlockSpec((th, p.shape[2]), lambda i, chip_ref: (i, 0)) for p, th in zip(parts, rows)]),
        out_shape=[_sds(p.shape[1:]) for p in parts], compiler_params=_cp(("parallel",)), name=name)(
            chip, *parts, *recvd, *after)


def _swap_reduced_halves(halves, *, name):
    nw = len(halves)

    def body(*refs):
        ins, outs = refs[:nw], refs[nw:2 * nw]
        send_sems, recv_sems = refs[2 * nw:]
        x, y, c = _place()
        _handshake([(x, y, 1 - c)])
        cps = []
        for w in range(nw):
            cp = pltpu.make_async_remote_copy(
                src_ref=ins[w], dst_ref=outs[w], send_sem=send_sems.at[w], recv_sem=recv_sems.at[w],
                device_id=(x, y, 1 - c), device_id_type=MESH)
            cp.start()
            cps.append(cp)
        for cp in cps:
            cp.wait()

    sem = pltpu.SemaphoreType.DMA
    return _sequencer(body, halves, [_sds(h.shape, h.dtype) for h in halves], [sem((nw,)), sem((nw,))], JOIN_ID, name)


def _exchange_rows(vec, *, name):
    def body(v_ref, slots, send_sems, recv_sems, local_sem):
        x, y, c = _place()
        me = 4 * x + 2 * y + c
        peers = []
        for mask in range(1, N_DEV):
            peers.append((1 - x if mask & 4 else x, 1 - y if mask & 2 else y, 1 - c if mask & 1 else c))
        _handshake(peers)
        own = pltpu.make_async_copy(v_ref, slots.at[me], local_sem)
        own.start()
        cps = []
        for k, peer in enumerate(peers):
            cp = pltpu.make_async_remote_copy(
                src_ref=v_ref, dst_ref=slots.at[me], send_sem=send_sems.at[k], recv_sem=recv_sems.at[k],
                device_id=peer, device_id_type=MESH)
            cp.start()
            cps.append(cp)
        for k, (px, py, pc) in enumerate(peers):
            pltpu.make_async_remote_copy(
                src_ref=v_ref, dst_ref=slots.at[4 * px + 2 * py + pc], send_sem=send_sems.at[k],
                recv_sem=recv_sems.at[k], device_id=(px, py, pc), device_id_type=MESH).wait_recv()
        for cp in cps:
            cp.wait_send()
        own.wait()

    sem = pltpu.SemaphoreType.DMA
    return _sequencer(body, [vec], [_sds((N_DEV,) + vec.shape)], [sem((N_DEV - 1,)), sem((N_DEV - 1,)), sem(())],
                      EXCHANGE_ID, name)[0]


def _sum_slots(slots, *, name, after=()):
    def body(s_ref, *rest):
        out_ref = rest[len(after)]
        acc = s_ref[0]
        for d in range(1, N_DEV):
            acc = acc + s_ref[d]
        out_ref[...] = acc

    vmem = pl.BlockSpec(memory_space=pltpu.VMEM)
    return pl.pallas_call(
        body, in_specs=[vmem] + [HBM_OPERAND] * len(after), out_specs=vmem, out_shape=_sds(slots.shape[1:]),
        compiler_params=pltpu.CompilerParams(vmem_limit_bytes=VMEM_LIMIT_BYTES), name=name)(slots, *after)


def _reduce_scatter_start(grads, core, *, tag, add_after=()):
    others = _swap_other_halves(grads, name="swap_other_halves_" + tag)
    parts = _add_my_halves(core, grads, others, name="add_my_halves_" + tag, after=add_after)
    return parts, _scatter_partials(parts, name="scatter_partials_" + tag)


def _reduce_scatter_finish(parts, recvd, chip, *, tag, sum_after=()):
    mine = _sum_partials(chip, parts, recvd, name="sum_partials_" + tag, after=sum_after)
    return mine, _swap_reduced_halves(mine, name="swap_reduced_halves_" + tag)


ADAM_BLOCK_ELEMS = 256 * 1024


def _adam_rows(rows, cols):
    tm = rows
    while tm * cols > ADAM_BLOCK_ELEMS and tm % 16 == 0:
        tm //= 2
    return tm


def _adam_step(wv, gv, mv, vv):
    m2 = ADAM_B1 * mv + (1.0 - ADAM_B1) * gv
    v2 = ADAM_B2 * vv + (1.0 - ADAM_B2) * (gv * gv)
    m_hat = m2 / (1.0 - ADAM_B1 ** ADAM_STEP)
    v_hat = v2 / (1.0 - ADAM_B2 ** ADAM_STEP)
    return -ADAM_LR * (m_hat / (jnp.sqrt(v_hat) + ADAM_EPS) + ADAM_WD * wv), m2, v2


def _adamw(w, g, m, v, *, name):
    rows, cols = w.shape
    return _rowwise(_adam_step, [w, g, m, v], [], [_sds((rows, cols))] * 3, tm=_adam_rows(rows, cols), name=name)


def _adamw_halves(core, w, g_mine, g_theirs, m, v, *, name, after=()):
    rows, cols = w.shape
    hw = rows // 2
    tm = _adam_rows(hw, cols)
    per_half = hw // tm

    def body(core_ref, w_ref, gm_ref, gt_ref, m_ref, v_ref, *rest):
        g_out, d_out, m_out, v_out = rest[len(after):]
        mine = (pl.program_id(0) // per_half) == core_ref[0]
        g = jnp.where(mine, gm_ref[...], gt_ref[...])
        d, m2, v2 = _adam_step(w_ref[...], g, m_ref[...], v_ref[...])
        g_out[...] = g
        d_out[...] = d
        m_out[...] = m2
        v_out[...] = v2

    full = pl.BlockSpec((tm, cols), lambda i, core_ref: (i, 0))
    half = pl.BlockSpec((tm, cols), lambda i, core_ref: (i % per_half, 0))
    return pl.pallas_call(
        body,
        grid_spec=pltpu.PrefetchScalarGridSpec(
            num_scalar_prefetch=1, grid=(rows // tm,),
            in_specs=[full, half, half, full, full] + [HBM_OPERAND] * len(after), out_specs=[full, full, full, full]),
        out_shape=[_sds((rows, cols))] * 4, compiler_params=_cp(("parallel",)), name=name)(
            core, w, g_mine, g_theirs, m, v, *after)


HELD_TRANSPOSED = ("w_ff_gate", "w_ff_up")


def _as_rows(name, arr):
    return arr[0].T if name in HELD_TRANSPOSED else arr[0]


def _from_rows(name, arr2d):
    return (arr2d.T if name in HELD_TRANSPOSED else arr2d)[None]


STORED_SWAPPED = ("ssm_b_re", "ssm_b_im")


def _as_stored(name, arr):
    return jnp.swapaxes(arr, -1, -2) if name in STORED_SWAPPED else arr


def _pack_rows(arrs):
    flat = jnp.concatenate([a.reshape(-1).astype(F32) for a in arrs])
    rows = -(-flat.shape[0] // 1024) * 8
    return jnp.pad(flat, (0, rows * 128 - flat.shape[0])).reshape(rows, 128)


def _unpack_rows(vec, shapes):
    flat = vec.reshape(-1)
    out, off = [], 0
    for shp in shapes:
        size = math.prod(shp)
        out.append(flat[off:off + size].reshape(shp))
        off += size
    return out


SMALL = ("b_gate", "ssm_a_re", "ssm_a_im", "ssm_log_dt", "ssm_b_re", "ssm_b_im", "ssm_c_re", "ssm_c_im", "ssm_d",
         "ln1_g", "ln1_b", "ln2_g", "ln2_b")
GATHER_GROUPS = (("w_in", ("w_in",)), ("mixer", ("w_attn_br", "w_ssm_br", "w_glu", "w_out")),
                 ("ffn", ("w_ff_gate", "w_ff_up", "w_ff_down")))
REDUCE_GROUPS = (("ffn", ("w_ff_down", "w_ff_gate", "w_ff_up")),
                 ("mixer", ("w_out", "w_ssm_br", "w_glu", "w_attn_br")), ("w_in", ("w_in",)))
WEIGHTS = ("w_in", "b_gate", "w_attn_br", "w_ssm_br", "w_out", "ssm_a_re", "ssm_a_im", "ssm_log_dt", "ssm_b_re",
           "ssm_b_im", "ssm_c_re", "ssm_c_im", "ssm_d", "w_glu", "ln1_g", "ln1_b", "w_ff_gate", "w_ff_up", "w_ff_down",
           "ln2_g", "ln2_b")


def kernel(x, w_in, b_gate, w_attn_br, w_ssm_br, w_out, ssm_a_re, ssm_a_im, ssm_log_dt, ssm_b_re, ssm_b_im, ssm_c_re, ssm_c_im, ssm_d, w_glu, ln1_g, ln1_b, w_ff_gate, w_ff_up, w_ff_down, ln2_g, ln2_b, loss_target, m_w_in, m_b_gate, m_w_attn_br, m_w_ssm_br, m_w_out, m_ssm_a_re, m_ssm_a_im, m_ssm_log_dt, m_ssm_b_re, m_ssm_b_im, m_ssm_c_re, m_ssm_c_im, m_ssm_d, m_w_glu, m_ln1_g, m_ln1_b, m_w_ff_gate, m_w_ff_up, m_w_ff_down, m_ln2_g, m_ln2_b, v_w_in, v_b_gate, v_w_attn_br, v_w_ssm_br, v_w_out, v_ssm_a_re, v_ssm_a_im, v_ssm_log_dt, v_ssm_b_re, v_ssm_b_im, v_ssm_c_re, v_ssm_c_im, v_ssm_d, v_w_glu, v_ln1_g, v_ln1_b, v_w_ff_gate, v_w_ff_up, v_w_ff_down, v_ln2_g, v_ln2_b):
    given = dict(locals())
    px, py, pc = _place()
    chip = 2 * px + py
    core_s = jnp.reshape(pc, (1,)).astype(jnp.int32)
    chip_s = jnp.reshape(chip, (1,)).astype(jnp.int32)

    wts = {}
    for tag, names in GATHER_GROUPS:
        wts.update(zip(names, _gather_weights([_as_rows(n, given[n]).astype(BF16) for n in names],
                                              name="gather_" + tag)))
    ncol = D_MODEL // N_CHIPS
    bg_mine = jnp.where(pc == 0, b_gate[0], jnp.zeros_like(b_gate[0]))
    bg_full = lax.dynamic_update_slice(jnp.zeros((2, D_MODEL), F32), bg_mine, (0, chip * ncol))
    bg_slots = _exchange_rows(bg_full.reshape(16, 128), name="exchange_gate_bias")
    bg_full = _sum_slots(bg_slots, name="sum_gate_bias").reshape(2, D_MODEL)
    small = {n: given[n][0] for n in SMALL if n.startswith("ssm")}
    small.update({n: given[n] for n in ("ln1_g", "ln1_b", "ln2_g", "ln2_b")})
    small["b_gate"] = bg_full

    loss_mine, grad_x, big_g, small_g, marks = _local_step(x[0], loss_target[0], wts, small)
    loss = lax.psum(loss_mine, ("x", "y", "c"))

    groups = dict(REDUCE_GROUPS)
    add_after = {"ffn": (marks["ln1_bwd"],), "mixer": (marks["scan_bwd"],), "w_in": (grad_x,)}
    parts, recvd = {}, {}
    for tag, names in REDUCE_GROUPS:
        parts[tag], recvd[tag] = _reduce_scatter_start([big_g[n] for n in names], core_s, tag=tag,
                                                       add_after=add_after[tag])
    grads, delta, new_m, new_v = {}, {}, {}, {}

    def finish(tag, sum_after, adam_after):
        mine, theirs = _reduce_scatter_finish(parts[tag], recvd[tag], chip_s, tag=tag, sum_after=sum_after)
        for n, g_mine, g_theirs in zip(groups[tag], mine, theirs):
            res = _adamw_halves(core_s, _as_rows(n, given[n]), g_mine, g_theirs, _as_rows(n, given["m_" + n]),
                                _as_rows(n, given["v_" + n]), name="adamw_" + n, after=adam_after)
            grads[n], delta[n], new_m[n], new_v[n] = [_from_rows(n, r) for r in res]

    in_flight = (parts["w_in"][0],)
    finish("ffn", (marks["scan_bwd"],), in_flight)
    finish("mixer", (marks["attention_bwd_0"],), in_flight)
    stored = [_as_stored(n, small_g[n]) for n in SMALL]
    slots = _exchange_rows(_pack_rows(stored), name="exchange_small")
    summed = _unpack_rows(_sum_slots(slots, name="sum_small", after=in_flight), [a.shape for a in stored])
    for n, g in zip(SMALL, summed):
        g = _as_stored(n, g)
        if n == "b_gate":
            g = lax.dynamic_slice(g, (0, chip * ncol), (2, ncol))
        grads[n] = g.reshape(given[n].shape)
    packed = [_pack_rows([_as_stored(n, src[n]) for n in SMALL]) for src in
              (given, grads, {n: given["m_" + n] for n in SMALL}, {n: given["v_" + n] for n in SMALL})]
    shapes = [_as_stored(n, given[n]).shape for n in SMALL]
    small_out = _adamw(*packed, name="adamw_small")
    for out, vec in zip((delta, new_m, new_v), small_out):
        out.update((n, _as_stored(n, a)) for n, a in zip(SMALL, _unpack_rows(vec, shapes)))
    behind = [delta[n] for tag in ("ffn", "mixer") for n in groups[tag]] + [small_out[0]]
    finish("w_in", tuple(behind), ())

    return (loss, grad_x.reshape(x.shape), *[grads[n] for n in WEIGHTS], *[delta[n] for n in WEIGHTS],
            *[new_m[n] for n in WEIGHTS], *[new_v[n] for n in WEIGHTS])
```

```python
import math

import jax
import jax.numpy as jnp
from jax import lax
from jax.experimental import pallas as pl
from jax.experimental.pallas import tpu as pltpu
from jax.experimental.pallas import tpu_sc as plsc

F32 = jnp.float32
BF16 = jnp.bfloat16
MESH = pl.DeviceIdType.MESH

D_MODEL = 1024
SEQ = 2048
HEAD_DIM = 64
ATTN_HEADS = 8
DILATIONS = (1, 4, 16)
ATTN_WIDTH = ATTN_HEADS * HEAD_DIM
QKV_WIDTH = 3 * ATTN_WIDTH
BLOCK = 128
ROPE_THETA = 10000.0
NEG_INF = -1e30
SSM_GROUP = 16
SSM_GROUPS = 32
SSM_WIDTH = 512
SSM_STATE = 64
SSM_LANES = SSM_GROUPS * SSM_STATE
SCAN_CHUNKS = 8
SCAN_STEPS = SEQ // SCAN_CHUNKS
IN_WIDTH = 3 * QKV_WIDTH + SSM_WIDTH + 2 * D_MODEL
D_FF = 2816
N_CHIPS = 4
N_DEV = 8
DN_ALPHA = 2.0 ** 0.25
LN_EPS = 1e-5
ADAM_LR = 0.001
ADAM_B1 = 0.9
ADAM_B2 = 0.999
ADAM_EPS = 1e-08
ADAM_WD = 0.01
ADAM_STEP = 10
GELU_C = math.sqrt(2.0 / math.pi)
GELU_K = 0.044715

VMEM_LIMIT_BYTES = 56 * 1024 * 1024


def _sds(shape, dtype=F32):
    return jax.ShapeDtypeStruct(tuple(shape), dtype)


def _cp(semantics=None):
    return pltpu.CompilerParams(dimension_semantics=semantics, vmem_limit_bytes=VMEM_LIMIT_BYTES)


HBM_OPERAND = pl.BlockSpec(memory_space=pl.ANY)


def _matmul(a, b, *, grid, a_spec, b_spec, o_spec, out_shape, dims, k_axis=None, name, after=()):
    nk = grid[k_axis] if k_axis is not None else 1
    o_block = tuple(d for d in o_spec.block_shape if d is not None)
    n_after = len(after)

    def body(a_ref, b_ref, *rest):
        o_ref, acc = rest[n_after], rest[n_after + 1:]
        part = lax.dot_general(a_ref[...].astype(BF16), b_ref[...].astype(BF16),
                               (((dims[0],), (dims[1],)), ((), ())), preferred_element_type=F32)
        if k_axis is None:
            o_ref[...] = part.astype(o_ref.dtype)
        else:
            k = pl.program_id(k_axis)

            @pl.when(k == 0)
            def _():
                acc[0][...] = part

            @pl.when(k > 0)
            def _():
                acc[0][...] += part

            @pl.when(k == nk - 1)
            def _():
                o_ref[...] = acc[0][...].astype(o_ref.dtype)

    sem = tuple("arbitrary" if ax == k_axis else "parallel" for ax in range(len(grid)))
    return pl.pallas_call(
        body, grid=grid, in_specs=[a_spec, b_spec] + [HBM_OPERAND] * n_after, out_specs=o_spec, out_shape=out_shape,
        scratch_shapes=[pltpu.VMEM(o_block, F32)] if k_axis is not None else [],
        compiler_params=_cp(sem), name=name)(a, b, *after)


def _mm_cols(a, wg, *, tm, name, out_dtype=F32, out3d=False):
    m, k = a.shape
    ns = wg.shape[2]
    if out3d:
        o_spec = pl.BlockSpec((None, tm, ns), lambda i, s: (s, i, 0))
        out_shape = _sds((N_CHIPS, m, ns), out_dtype)
    else:
        o_spec = pl.BlockSpec((tm, ns), lambda i, s: (i, s))
        out_shape = _sds((m, N_CHIPS * ns), out_dtype)
    return _matmul(a, wg, grid=(m // tm, N_CHIPS),
                   a_spec=pl.BlockSpec((tm, k), lambda i, s: (i, 0)),
                   b_spec=pl.BlockSpec((None, k, ns), lambda i, s: (s, 0, 0)),
                   o_spec=o_spec, out_shape=out_shape, dims=(1, 0), name=name)


def _mm_cols_nt(dy, wg, *, tm, name, dy3d=False, out_dtype=F32, after=()):
    k, ns = wg.shape[1], wg.shape[2]
    if dy3d:
        m = dy.shape[1]
        a_spec = pl.BlockSpec((None, tm, ns), lambda i, s: (s, i, 0))
    else:
        m = dy.shape[0]
        a_spec = pl.BlockSpec((tm, ns), lambda i, s: (i, s))
    return _matmul(dy, wg, grid=(m // tm, N_CHIPS), a_spec=a_spec,
                   b_spec=pl.BlockSpec((None, k, ns), lambda i, s: (s, 0, 0)),
                   o_spec=pl.BlockSpec((tm, k), lambda i, s: (i, 0)),
                   out_shape=_sds((m, k), out_dtype), dims=(1, 1), k_axis=1, name=name, after=after)


def _mm_cols_tn(a, dy, *, ns, name, after=()):
    m, k = a.shape
    return _matmul(a, dy, grid=(N_CHIPS,), a_spec=pl.BlockSpec((m, k), lambda s: (0, 0)),
                   b_spec=pl.BlockSpec((m, ns), lambda s: (0, s)),
                   o_spec=pl.BlockSpec((None, k, ns), lambda s: (s, 0, 0)),
                   out_shape=_sds((N_CHIPS, k, ns), BF16), dims=(0, 0), name=name, after=after)


def _mm_plain(a, b, *, tm, tn, name, out_dtype=F32, dims=(1, 0), tk=None):
    m = a.shape[1 - dims[0]]
    kk = a.shape[dims[0]]
    n = b.shape[1 - dims[1]]
    tk = kk if tk is None else tk
    nk = kk // tk

    def a_idx(i, j, k):
        return (i, k) if dims[0] == 1 else (k, i)

    def b_idx(i, j, k):
        return (k, j) if dims[1] == 0 else (j, k)

    a_blk = (tm, tk) if dims[0] == 1 else (tk, tm)
    b_blk = (tk, tn) if dims[1] == 0 else (tn, tk)
    return _matmul(a, b, grid=(m // tm, n // tn, nk),
                   a_spec=pl.BlockSpec(a_blk, a_idx), b_spec=pl.BlockSpec(b_blk, b_idx),
                   o_spec=pl.BlockSpec((tm, tn), lambda i, j, k: (i, j)),
                   out_shape=_sds((m, n), out_dtype), dims=dims, k_axis=2 if nk > 1 else None, name=name)


def _rowwise(fn, tiled, full, outs, accs=(), *, tm, name, after=()):
    args, in_specs = [], []
    for t in tiled:
        if isinstance(t, tuple):
            arr, w, cb = t
            in_specs.append(pl.BlockSpec((tm, w), lambda i, cb=cb: (i, cb)))
        else:
            arr = t
            in_specs.append(pl.BlockSpec((tm, arr.shape[1]), lambda i: (i, 0)))
        args.append(arr)
    rows = args[0].shape[0]
    for f in full:
        in_specs.append(pl.BlockSpec(f.shape, lambda i, nd=f.ndim: (0,) * nd))
        args.append(f)
    out_specs = [pl.BlockSpec((tm, o.shape[1]), lambda i: (i, 0)) for o in outs]
    out_specs += [pl.BlockSpec(a.shape, lambda i, nd=len(a.shape): (0,) * nd) for a in accs]
    n_in, n_out = len(args), len(outs)
    in_specs += [HBM_OPERAND] * len(after)
    first_out = n_in + len(after)

    def body(*refs):
        res = fn(*[r[...] for r in refs[:n_in]])
        res = res if isinstance(res, (tuple, list)) else (res,)
        for r, v in zip(refs[first_out:first_out + n_out], res[:n_out]):
            r[...] = v.astype(r.dtype)
        i = pl.program_id(0)
        for r, v in zip(refs[first_out + n_out:], res[n_out:]):
            @pl.when(i == 0)
            def _(r=r, v=v):
                r[...] = v

            @pl.when(i > 0)
            def _(r=r, v=v):
                r[...] += v

    res = pl.pallas_call(
        body, grid=(rows // tm,), in_specs=in_specs, out_specs=out_specs, out_shape=list(outs) + list(accs),
        compiler_params=_cp(("arbitrary",) if accs else ("parallel",)), name=name)(*args, *after)
    return res


def _colsum(v):
    return jnp.sum(v, axis=0, keepdims=True)


def _ln_stats(z):
    mu = jnp.mean(z, axis=-1, keepdims=True)
    zc = z - mu
    var = jnp.mean(zc * zc, axis=-1, keepdims=True)
    rstd = lax.rsqrt(var + LN_EPS)
    return zc * rstd, rstd


def _ln_bwd(dy, xhat, rstd, g):
    dxh = dy * g
    m1 = jnp.mean(dxh, axis=-1, keepdims=True)
    m2 = jnp.mean(dxh * xhat, axis=-1, keepdims=True)
    return rstd * (dxh - m1 - xhat * m2)


def _swap_halves(t):
    w = t.shape[-1]
    lane = lax.broadcasted_iota(jnp.int32, t.shape, t.ndim - 1)
    return jnp.where((lane % HEAD_DIM) < HEAD_DIM // 2, pltpu.roll(t, w - HEAD_DIM // 2, t.ndim - 1),
                     pltpu.roll(t, HEAD_DIM // 2, t.ndim - 1))


PHASES = max(DILATIONS)
PAIR = 2 * HEAD_DIM
UNITS = SEQ // BLOCK
UNIT_BATCH = 4
ROPE_ROWS = 256


def _to_phase_rows(t):
    return t.reshape(SEQ // PHASES, PHASES, t.shape[1]).transpose(1, 0, 2).reshape(t.shape)


def _reorder_rows(arr, *, to_phase, name):
    def body(i_ref, o_ref):
        for rho in range(PHASES):
            phase = pl.ds(rho * BLOCK, BLOCK)
            strided = pl.ds(rho, BLOCK, stride=PHASES)
            if to_phase:
                o_ref[phase, :] = i_ref[strided, :]
            else:
                o_ref[strided, :] = i_ref[phase, :]

    spec = pl.BlockSpec((SEQ, BLOCK), lambda j: (0, j))
    return pl.pallas_call(body, grid=(arr.shape[1] // BLOCK,), in_specs=[spec], out_specs=spec,
                          out_shape=_sds(arr.shape), compiler_params=_cp(("parallel",)), name=name)(arr)


def _rope(t, cf, ss):
    return t * cf + _swap_halves(t) * ss


def _rope_transposed(d, cf, ss):
    return d * cf + _swap_halves(d * ss)


def _unit_pieces(u, dil):
    pieces, length = PHASES // dil, 8 * dil
    if dil == 1:
        rho, i = 0, u
    elif dil == PHASES:
        rho, i = u, 0
    else:
        rho, i = jnp.bitwise_and(u, dil - 1), jnp.right_shift(u, dil.bit_length() - 1)
    before = jnp.maximum(i - 1, 0)
    cur = [pl.multiple_of((rho + dil * k) * BLOCK + length * i, 8) for k in range(pieces)]
    prev = [pl.multiple_of((rho + dil * k) * BLOCK + length * before, 8) for k in range(pieces)]
    return i, cur, prev


def _load_tile(ref, starts, dil):
    return jnp.concatenate([ref[pl.ds(st, 8 * dil), :] for st in starts], axis=0)


def _store_tile(ref, starts, dil, val, head=None, accumulate=False):
    length = 8 * dil
    lanes = slice(None) if head is None else pl.ds(head * HEAD_DIM, HEAD_DIM)
    cols = slice(None) if head is None else slice(head * HEAD_DIM, (head + 1) * HEAD_DIM)
    for k, st in enumerate(starts):
        piece = val[k * length:(k + 1) * length, cols]
        if accumulate:
            ref[pl.ds(st, length), lanes] += piece
        else:
            ref[pl.ds(st, length), lanes] = piece


def _tile_position(idx, dil):
    pieces, length = PHASES // dil, 8 * dil
    return pieces * jnp.bitwise_and(idx, length - 1) + jnp.right_shift(idx, length.bit_length() - 1)


def _band_mask(i, dil):
    row = lax.broadcasted_iota(jnp.int32, (BLOCK, 2 * BLOCK), 0)
    col = lax.broadcasted_iota(jnp.int32, (BLOCK, 2 * BLOCK), 1)
    key_pos = _tile_position(jnp.bitwise_and(col, BLOCK - 1), dil) + jnp.where(col >= BLOCK, 0, -BLOCK)
    dist = _tile_position(row, dil) - key_pos
    return (dist >= 0) & (dist <= BLOCK) & ((col >= BLOCK) | (i > 0))


def _causal_mask():
    row = lax.broadcasted_iota(jnp.int32, (BLOCK, BLOCK), 0)
    col = lax.broadcasted_iota(jnp.int32, (BLOCK, BLOCK), 1)
    return row >= col


def _pair_views(col0):
    return [pl.BlockSpec((SEQ, PAIR), lambda hp, g=g: (0, col0 // PAIR + g * (ATTN_WIDTH // PAIR) + hp))
            for g in range(len(DILATIONS))]


def _rotate(in_refs, out_refs, cf_ref, ss_ref, scale):
    def step(t, carry):
        rows = pl.ds(pl.multiple_of(t * ROPE_ROWS, ROPE_ROWS), ROPE_ROWS)
        cf, ss = cf_ref[rows, :] * scale, ss_ref[rows, :] * scale
        for i_ref, o_ref in zip(in_refs, out_refs):
            o_ref[rows, :] = _rope(i_ref[rows, :], cf, ss)
        return carry

    lax.fori_loop(0, SEQ // ROPE_ROWS, step, 0)


def _attention_fwd(proj, cos_f, sin_s):
    ng = len(DILATIONS)

    def body(*refs):
        q_refs, k_refs, v_refs = refs[:ng], refs[ng:2 * ng], refs[2 * ng:3 * ng]
        cf_ref, ss_ref, attn_ref, lse_ref = refs[3 * ng:3 * ng + 4]
        scratch = refs[3 * ng + 4:]
        qr_refs, kr_refs = scratch[:ng], scratch[ng:]
        _rotate(q_refs, qr_refs, cf_ref, ss_ref, 1.0 / math.sqrt(HEAD_DIM))
        _rotate(k_refs, kr_refs, cf_ref, ss_ref, 1.0)
        first = lax.broadcasted_iota(jnp.int32, (BLOCK, PAIR), 1) < HEAD_DIM
        for g, dil in enumerate(DILATIONS):
            two_blocks = SEQ // dil > BLOCK

            def units(t, carry, g=g, dil=dil, two_blocks=two_blocks):
                picked = [_unit_pieces(t * UNIT_BATCH + j, dil) for j in range(UNIT_BATCH)]

                def tiles(ref, with_prev=False):
                    if with_prev and two_blocks:
                        return jnp.stack([jnp.concatenate([_load_tile(ref, prev, dil), _load_tile(ref, rows, dil)],
                                                          axis=0) for _, rows, prev in picked])
                    return jnp.stack([_load_tile(ref, rows, dil) for _, rows, _ in picked])

                qq = tiles(qr_refs[g]).astype(BF16)
                kk = tiles(kr_refs[g], True).astype(BF16)
                vv = tiles(v_refs[g], True).astype(BF16)
                if two_blocks:
                    valid = jnp.stack([_band_mask(i, dil) for i, _, _ in picked])
                else:
                    valid = _causal_mask()[None]
                mine = first[None]
                zero = jnp.zeros_like(qq)
                outs, lses = [], []
                for qh in (jnp.where(mine, qq, zero), jnp.where(mine, zero, qq)):
                    s = jnp.einsum("pqd,pkd->pqk", qh, kk, preferred_element_type=F32)
                    s = jnp.where(valid, s, NEG_INF)
                    m = jnp.max(s, axis=-1, keepdims=True)
                    p = jnp.exp(s - m)
                    l = jnp.sum(p, axis=-1, keepdims=True)
                    outs.append(jnp.einsum("pqk,pkd->pqd", p.astype(BF16), vv, preferred_element_type=F32) * (1.0 / l))
                    lses.append(m + jnp.log(l))
                o = jnp.where(mine, outs[0], outs[1])
                lse = jnp.where(mine, lses[0], lses[1])
                if g > 0:
                    lse_old = tiles(lse_ref)
                    m = jnp.maximum(lse_old, lse)
                    lse_new = m + jnp.log(jnp.exp(lse_old - m) + jnp.exp(lse - m))
                    o = tiles(attn_ref) * jnp.exp(lse_old - lse_new) + o * jnp.exp(lse - lse_new)
                    lse = lse_new
                for j, (_, rows, _) in enumerate(picked):
                    _store_tile(attn_ref, rows, dil, o[j])
                    _store_tile(lse_ref, rows, dil, lse[j])
                return carry

            lax.fori_loop(0, UNITS // UNIT_BATCH, units, 0)

    whole = pl.BlockSpec((SEQ, PAIR), lambda hp: (0, 0))
    out = pl.BlockSpec((SEQ, PAIR), lambda hp: (0, hp))
    return pl.pallas_call(
        body, grid=(ATTN_WIDTH // PAIR,),
        in_specs=_pair_views(0) + _pair_views(QKV_WIDTH) + _pair_views(2 * QKV_WIDTH) + [whole, whole],
        out_specs=[out, out], out_shape=[_sds((SEQ, ATTN_WIDTH)), _sds((SEQ, ATTN_WIDTH))],
        scratch_shapes=[pltpu.VMEM((SEQ, PAIR), F32)] * (2 * ng),
        compiler_params=_cp(("parallel",)), name="attention_fwd")(*([proj] * (3 * ng)), cos_f, sin_s)


def _attention_bwd(g, proj, cos_f, sin_s, d_attn, attn, lse):
    dil = DILATIONS[g]
    two_blocks = SEQ // dil > BLOCK

    def body(q_ref, k_ref, v_ref, cf_ref, ss_ref, do_ref, o_ref, lse_ref, dq_out, dk_out, dv_out,
             qr_ref, kr_ref, dq_acc, dk_acc, dv_acc):
        _rotate([q_ref], [qr_ref], cf_ref, ss_ref, 1.0 / math.sqrt(HEAD_DIM))
        _rotate([k_ref], [kr_ref], cf_ref, ss_ref, 1.0)
        dk_acc[...] = jnp.zeros_like(dk_acc)
        dv_acc[...] = jnp.zeros_like(dv_acc)
        nk = 2 * BLOCK if two_blocks else BLOCK
        first = lax.broadcasted_iota(jnp.int32, (BLOCK, PAIR), 1) < HEAD_DIM
        first_k = lax.broadcasted_iota(jnp.int32, (nk, PAIR), 1) < HEAD_DIM

        def units(t, carry):
            picked = [_unit_pieces(t * UNIT_BATCH + j, dil) for j in range(UNIT_BATCH)]

            def tiles(ref, with_prev=False):
                if with_prev and two_blocks:
                    return jnp.stack([jnp.concatenate([_load_tile(ref, prev, dil), _load_tile(ref, rows, dil)], axis=0)
                                      for _, rows, prev in picked])
                return jnp.stack([_load_tile(ref, rows, dil) for _, rows, _ in picked])

            qq = tiles(qr_ref).astype(BF16)
            kk = tiles(kr_ref, True).astype(BF16)
            vv = tiles(v_ref, True).astype(BF16)
            dof = tiles(do_ref)
            dd = dof * tiles(o_ref)
            lse3 = tiles(lse_ref)
            dob = dof.astype(BF16)
            if two_blocks:
                valid = jnp.stack([_band_mask(i, dil) for i, _, _ in picked])
            else:
                valid = _causal_mask()[None]
            zq, zf = jnp.zeros_like(qq), jnp.zeros_like(dd)
            dqs, dks, dvs = [], [], []
            for head in range(2):
                mine = first[None] if head == 0 else jnp.logical_not(first)[None]
                delta = jnp.sum(jnp.where(mine, dd, zf), axis=-1, keepdims=True)
                lse_h = lse3[:, :, head * HEAD_DIM:head * HEAD_DIM + 1]
                s = jnp.einsum("pqd,pkd->pqk", jnp.where(mine, qq, zq), kk, preferred_element_type=F32)
                p = jnp.where(valid, jnp.exp(s - lse_h), 0.0)
                dp = jnp.einsum("pqd,pkd->pqk", jnp.where(mine, dob, zq), vv, preferred_element_type=F32)
                ds = (p * (dp - delta)).astype(BF16)
                dqs.append(jnp.einsum("pqk,pkd->pqd", ds, kk, preferred_element_type=F32))
                dks.append(jnp.einsum("pqk,pqd->pkd", ds, qq, preferred_element_type=F32))
                dvs.append(jnp.einsum("pqk,pqd->pkd", p.astype(BF16), dob, preferred_element_type=F32))
            dq = jnp.where(first[None], dqs[0], dqs[1])
            dk = jnp.where(first_k[None], dks[0], dks[1])
            dv = jnp.where(first_k[None], dvs[0], dvs[1])
            for j, (_, rows, prev) in enumerate(picked):
                _store_tile(dq_acc, rows, dil, dq[j])
                _store_tile(dk_acc, rows, dil, dk[j, nk - BLOCK:], accumulate=True)
                _store_tile(dv_acc, rows, dil, dv[j, nk - BLOCK:], accumulate=True)
                if two_blocks:
                    _store_tile(dk_acc, prev, dil, dk[j, :BLOCK], accumulate=True)
                    _store_tile(dv_acc, prev, dil, dv[j, :BLOCK], accumulate=True)
            return carry

        lax.fori_loop(0, UNITS // UNIT_BATCH, units, 0)

        def finish(t, carry):
            rows = pl.ds(pl.multiple_of(t * ROPE_ROWS, ROPE_ROWS), ROPE_ROWS)
            cf, ss = cf_ref[rows, :], ss_ref[rows, :]
            dq = dq_acc[rows, :] * (1.0 / math.sqrt(HEAD_DIM))
            dq_out[rows, :] = _rope_transposed(dq, cf, ss).astype(BF16)
            dk_out[rows, :] = _rope_transposed(dk_acc[rows, :], cf, ss).astype(BF16)
            dv_out[rows, :] = dv_acc[rows, :].astype(BF16)
            return carry

        lax.fori_loop(0, SEQ // ROPE_ROWS, finish, 0)

    whole = pl.BlockSpec((SEQ, PAIR), lambda hp: (0, 0))
    pair = pl.BlockSpec((SEQ, PAIR), lambda hp: (0, hp))
    views = [_pair_views(col0)[g] for col0 in (0, QKV_WIDTH, 2 * QKV_WIDTH)]
    return pl.pallas_call(
        body, grid=(ATTN_WIDTH // PAIR,), in_specs=views + [whole, whole, pair, pair, pair],
        out_specs=[pair, pair, pair], out_shape=[_sds((SEQ, ATTN_WIDTH), BF16)] * 3,
        scratch_shapes=[pltpu.VMEM((SEQ, PAIR), F32)] * 5,
        compiler_params=_cp(("parallel",)), name=f"attention_bwd_{g}")(proj, proj, proj, cos_f, sin_s, d_attn, attn, lse)


def _cmul(ar, ai, br, bi):
    return ar * br - ai * bi, ar * bi + ai * br


def _pow256(ar, ai):
    for _ in range(8):
        ar, ai = _cmul(ar, ai, ar, ai)
    return ar, ai


def _chunk_carries(first_r, first_i, pr, pi, reverse):
    rows = lax.broadcasted_iota(jnp.int32, first_r.shape, 0)
    out_r = jnp.zeros_like(first_r)
    out_i = jnp.zeros_like(first_i)
    hr = jnp.zeros_like(first_r[0:1])
    hi = jnp.zeros_like(hr)
    order = range(SCAN_CHUNKS - 1, -1, -1) if reverse else range(SCAN_CHUNKS)
    for c in order:
        out_r = jnp.where(rows == c, hr, out_r)
        out_i = jnp.where(rows == c, hi, out_i)
        tr, ti = _cmul(pr[0:1], pi[0:1], hr, hi)
        hr = first_r[c:c + 1] + tr
        hi = first_i[c:c + 1] + ti
    return out_r, out_i


def _tile(j):
    return pl.ds(pl.multiple_of(j * SCAN_CHUNKS, SCAN_CHUNKS), SCAN_CHUNKS)


def _to_scan_rows(t):
    per = SCAN_STEPS // PHASES
    return t.reshape(PHASES, SCAN_CHUNKS, per, t.shape[1]).transpose(2, 0, 1, 3).reshape(t.shape)


def _from_scan_rows(t):
    per = SCAN_STEPS // PHASES
    return t.reshape(per, PHASES, SCAN_CHUNKS, t.shape[1]).transpose(1, 2, 0, 3).reshape(t.shape)


def _scan_in_place(hr_ref, hi_ref, a_r, a_i):
    def local(j, carry):
        tr, ti = _cmul(a_r, a_i, carry[0], carry[1])
        nr = tr + hr_ref[_tile(j), :]
        ni = ti + hi_ref[_tile(j), :]
        hr_ref[_tile(j), :] = nr
        hi_ref[_tile(j), :] = ni
        return nr, ni

    zero = jnp.zeros_like(a_r)
    last_r, last_i = lax.fori_loop(0, SCAN_STEPS, local, (zero, zero), unroll=4)
    pr, pi = _pow256(a_r, a_i)
    er, ei = _chunk_carries(last_r, last_i, pr, pi, reverse=False)

    def fix(j, carry):
        tr, ti = _cmul(carry[0], carry[1], er, ei)
        hr_ref[_tile(j), :] += tr
        hi_ref[_tile(j), :] += ti
        return _cmul(carry[0], carry[1], a_r, a_i)

    lax.fori_loop(0, SCAN_STEPS, fix, (a_r, a_i), unroll=4)
    return er, ei


def _reverse_scan_in_place(lr_ref, li_ref, hr_ref, hi_ref, er, ei, a_r, a_i):
    def local(t, carry):
        j = SCAN_STEPS - 1 - t
        tr, ti = _cmul(a_r, a_i, carry[0], carry[1])
        nr = tr + lr_ref[_tile(j), :]
        ni = ti + li_ref[_tile(j), :]
        lr_ref[_tile(j), :] = nr
        li_ref[_tile(j), :] = ni
        return nr, ni

    zero = jnp.zeros_like(a_r)
    first_r, first_i = lax.fori_loop(0, SCAN_STEPS, local, (zero, zero), unroll=4)
    pr, pi = _pow256(a_r, a_i)
    nxt_r, nxt_i = _chunk_carries(first_r, first_i, pr, pi, reverse=True)

    def accumulate(lam_r, lam_i, hp_r, hp_i, acc):
        return (acc[0] + lam_r * hp_r + lam_i * hp_i, acc[1] + lam_i * hp_r - lam_r * hp_i)

    def fix(t, carry):
        qr, qi, acc_r, acc_i = carry
        j = SCAN_STEPS - 1 - t
        tr, ti = _cmul(qr, qi, nxt_r, nxt_i)
        lam_r = lr_ref[_tile(j), :] + tr
        lam_i = li_ref[_tile(j), :] + ti
        lr_ref[_tile(j), :] = lam_r
        li_ref[_tile(j), :] = lam_i
        acc_r, acc_i = accumulate(lam_r, lam_i, hr_ref[_tile(j - 1), :], hi_ref[_tile(j - 1), :], (acc_r, acc_i))
        qr, qi = _cmul(qr, qi, a_r, a_i)
        return qr, qi, acc_r, acc_i

    qr, qi, acc_r, acc_i = lax.fori_loop(0, SCAN_STEPS - 1, fix, (a_r, a_i, zero, zero), unroll=4)
    tr, ti = _cmul(qr, qi, nxt_r, nxt_i)
    lam_r = lr_ref[_tile(0), :] + tr
    lam_i = li_ref[_tile(0), :] + ti
    lr_ref[_tile(0), :] = lam_r
    li_ref[_tile(0), :] = lam_i
    acc_r, acc_i = accumulate(lam_r, lam_i, er, ei, (acc_r, acc_i))
    return jnp.sum(acc_r, axis=0, keepdims=True), jnp.sum(acc_i, axis=0, keepdims=True)


def _rope_tables():
    half = HEAD_DIM // 2
    inv_freq = ROPE_THETA ** (-jnp.arange(half, dtype=F32) / half)
    ang = jnp.arange(SEQ, dtype=F32)[:, None] * inv_freq[None, :]
    cos, sin = jnp.cos(ang), jnp.sin(ang)
    cos_f = jnp.concatenate([cos, cos, cos, cos], axis=1)
    sin_s = jnp.concatenate([-sin, sin, -sin, sin], axis=1)
    return cos_f, sin_s


def _ssm_discretise(a_re, a_im, log_dt, b_re, b_im):
    lam = lax.complex(a_re, a_im)
    dt = jnp.exp(log_dt)[:, None]
    a_bar = jnp.exp(lam * dt)
    b_bar = ((a_bar - 1.0) / lam)[..., None] * lax.complex(b_re, b_im)
    return a_bar.real, a_bar.imag, b_bar.real, b_bar.imag


SSM_SLABS = 4
SLAB_GROUPS = SSM_GROUPS // SSM_SLABS
SLAB_IN = SSM_WIDTH // SSM_SLABS
SLAB_STATE = SSM_LANES // SSM_SLABS


def _slab_block_diag(blocks):
    _, r, c = blocks.shape
    eye = jnp.eye(SLAB_GROUPS, dtype=blocks.dtype)
    b5 = blocks.reshape(SSM_SLABS, SLAB_GROUPS, r, 1, c) * eye[None, :, None, :, None]
    return b5.reshape(SSM_SLABS, SLAB_GROUPS * r, SLAB_GROUPS * c)


def _diag_blocks(a, b):
    ra, cb = a.shape[1], b.shape[1]
    wa, wb = ra // SLAB_GROUPS, cb // SLAB_GROUPS
    d = lax.dot_general(a, b, (((0,), (0,)), ((), ())), preferred_element_type=F32)
    row_g = jnp.right_shift(lax.broadcasted_iota(jnp.int32, (ra, cb), 0), wa.bit_length() - 1)
    col_g = jnp.right_shift(lax.broadcasted_iota(jnp.int32, (ra, cb), 1), wb.bit_length() - 1)
    d = jnp.where(row_g == col_g, d, 0.0)
    fold = (jnp.bitwise_and(lax.broadcasted_iota(jnp.int32, (cb, wb), 0), wb - 1)
            == lax.broadcasted_iota(jnp.int32, (cb, wb), 1)).astype(F32)
    return jnp.dot(d, fold, preferred_element_type=F32, precision=lax.Precision.HIGHEST)


def _slab_specs():
    tok = pl.BlockSpec((SEQ, SLAB_IN), lambda j: (0, j))
    state = pl.BlockSpec((SEQ, SLAB_STATE), lambda j: (0, j))
    b_in = pl.BlockSpec((None, SLAB_IN, SLAB_STATE), lambda j: (j, 0, 0))
    c_out = pl.BlockSpec((None, SLAB_STATE, SLAB_IN), lambda j: (j, 0, 0))
    vec = pl.BlockSpec((1, SLAB_STATE), lambda j: (0, j))
    ent = pl.BlockSpec((SCAN_CHUNKS, SLAB_STATE), lambda j: (0, j))
    return tok, state, b_in, c_out, vec, ent


def _ssm_forward(u, b_in_r, b_in_i, c_out_r, c_out_ni, a_r, a_i):
    def body(u_ref, br_ref, bi_ref, cr_ref, ci_ref, ar_ref, ai_ref, y_ref, hr_ref, hi_ref, er_ref, ei_ref):
        uu = u_ref[...]
        hr_ref[...] = jnp.dot(uu, br_ref[...], preferred_element_type=F32)
        hi_ref[...] = jnp.dot(uu, bi_ref[...], preferred_element_type=F32)
        a_re = jnp.broadcast_to(ar_ref[...], (SCAN_CHUNKS, SLAB_STATE))
        a_im = jnp.broadcast_to(ai_ref[...], (SCAN_CHUNKS, SLAB_STATE))
        er_ref[...], ei_ref[...] = _scan_in_place(hr_ref, hi_ref, a_re, a_im)
        y_ref[...] = (jnp.dot(hr_ref[...].astype(BF16), cr_ref[...], preferred_element_type=F32)
                      + jnp.dot(hi_ref[...].astype(BF16), ci_ref[...], preferred_element_type=F32))

    tok, state, b_in, c_out, vec, ent = _slab_specs()
    return pl.pallas_call(
        body, grid=(SSM_SLABS,), in_specs=[tok, b_in, b_in, c_out, c_out, vec, vec],
        out_specs=[tok, state, state, ent, ent],
        out_shape=[_sds((SEQ, SSM_WIDTH)), _sds((SEQ, SSM_LANES)), _sds((SEQ, SSM_LANES)),
                   _sds((SCAN_CHUNKS, SSM_LANES)), _sds((SCAN_CHUNKS, SSM_LANES))],
        compiler_params=_cp(("parallel",)), name="ssm_forward")(u, b_in_r, b_in_i, c_out_r, c_out_ni, a_r, a_i)


def _ssm_backward(d_y, d_u_skip, u, h_r, h_i, e_r, e_i, b_in_r, b_in_i, c_out_r, c_out_ni, a_r, a_i):
    def body(dy_ref, skip_ref, u_ref, hr_ref, hi_ref, er_ref, ei_ref, br_ref, bi_ref, cr_ref, ci_ref, ar_ref, ai_ref,
             du_ref, dar_ref, dai_ref, dcr_ref, dci_ref, dbr_ref, dbi_ref, lr_ref, li_ref):
        dy = dy_ref[...]
        lr_ref[...] = _dot_nt(dy, cr_ref[...])
        li_ref[...] = _dot_nt(dy, ci_ref[...])
        a_re = jnp.broadcast_to(ar_ref[...], (SCAN_CHUNKS, SLAB_STATE))
        a_im = -jnp.broadcast_to(ai_ref[...], (SCAN_CHUNKS, SLAB_STATE))
        dar_ref[...], dai_ref[...] = _reverse_scan_in_place(lr_ref, li_ref, hr_ref, hi_ref, er_ref[...], ei_ref[...],
                                                            a_re, a_im)
        dcr_ref[...] = _diag_blocks(hr_ref[...].astype(BF16), dy)
        dci_ref[...] = _diag_blocks(hi_ref[...].astype(BF16), dy)
        lam_r, lam_i = lr_ref[...].astype(BF16), li_ref[...].astype(BF16)
        uu = u_ref[...]
        dbr_ref[...] = _diag_blocks(uu, lam_r)
        dbi_ref[...] = _diag_blocks(uu, lam_i)
        du = skip_ref[...] + _dot_nt(lam_r, br_ref[...]) + _dot_nt(lam_i, bi_ref[...])
        du_ref[...] = du.astype(BF16)

    tok, state, b_in, c_out, vec, ent = _slab_specs()
    dc = pl.BlockSpec((SLAB_STATE, SSM_GROUP), lambda j: (j, 0))
    db = pl.BlockSpec((SLAB_IN, SSM_STATE), lambda j: (j, 0))
    return pl.pallas_call(
        body, grid=(SSM_SLABS,), in_specs=[tok, tok, tok, state, state, ent, ent, b_in, b_in, c_out, c_out, vec, vec],
        out_specs=[tok, vec, vec, dc, dc, db, db],
        out_shape=[_sds((SEQ, SSM_WIDTH), BF16), _sds((1, SSM_LANES)), _sds((1, SSM_LANES)),
                   _sds((SSM_LANES, SSM_GROUP)), _sds((SSM_LANES, SSM_GROUP)),
                   _sds((SSM_WIDTH, SSM_STATE)), _sds((SSM_WIDTH, SSM_STATE))],
        scratch_shapes=[pltpu.VMEM((SEQ, SLAB_STATE), F32)] * 2,
        compiler_params=_cp(("parallel",)), name="ssm_backward")(
            d_y, d_u_skip, u, h_r, h_i, e_r, e_i, b_in_r, b_in_i, c_out_r, c_out_ni, a_r, a_i)


FF_ROWS = 1024
FF_SHARD = D_FF // N_CHIPS


def _dot_nt(a, b):
    return lax.dot_general(a, b, (((1,), (1,)), ((), ())), preferred_element_type=F32)


def _ffn_up(h, w_gate_t, w_up_t):
    def body(h_ref, wg_ref, wu_ref, a_ref, b_ref, act_ref):
        hb = h_ref[...].astype(BF16)
        a = _dot_nt(hb, wg_ref[...])
        b = _dot_nt(hb, wu_ref[...])
        a_ref[...] = a
        b_ref[...] = b
        act_ref[...] = (a * jax.nn.sigmoid(a) * b).astype(BF16)

    w_spec = pl.BlockSpec((None, FF_SHARD, D_MODEL), lambda i, k: (k, 0, 0))
    o_spec = pl.BlockSpec((None, FF_ROWS, FF_SHARD), lambda i, k: (k, i, 0))
    shape = (N_CHIPS, SEQ, FF_SHARD)
    return pl.pallas_call(
        body, grid=(SEQ // FF_ROWS, N_CHIPS),
        in_specs=[pl.BlockSpec((FF_ROWS, D_MODEL), lambda i, k: (i, 0)), w_spec, w_spec],
        out_specs=[o_spec, o_spec, o_spec], out_shape=[_sds(shape), _sds(shape), _sds(shape, BF16)],
        compiler_params=_cp(("parallel", "parallel")), name="ffn_up")(h, w_gate_t, w_up_t)


def _ffn_down_bwd(dz, w_down, a, b):
    def body(dz_ref, wd_ref, a_ref, b_ref, da_ref, db_ref):
        d_act = _dot_nt(dz_ref[...].astype(BF16), wd_ref[...])
        av = a_ref[...]
        sg = jax.nn.sigmoid(av)
        da_ref[...] = (d_act * b_ref[...] * sg * (1.0 + av * (1.0 - sg))).astype(BF16)
        db_ref[...] = (d_act * av * sg).astype(BF16)

    t_spec = pl.BlockSpec((None, FF_ROWS, FF_SHARD), lambda i, k: (k, i, 0))
    shape = (N_CHIPS, SEQ, FF_SHARD)
    return pl.pallas_call(
        body, grid=(SEQ // FF_ROWS, N_CHIPS),
        in_specs=[pl.BlockSpec((FF_ROWS, D_MODEL), lambda i, k: (i, 0)),
                  pl.BlockSpec((None, FF_SHARD, D_MODEL), lambda i, k: (k, 0, 0)), t_spec, t_spec],
        out_specs=[t_spec, t_spec], out_shape=[_sds(shape, BF16), _sds(shape, BF16)],
        compiler_params=_cp(("parallel", "parallel")), name="ffn_down_bwd")(dz, w_down, a, b)


def _ffn_dh(d_a, d_b, w_gate_t, w_up_t):
    def body(da_ref, db_ref, wg_ref, wu_ref, o_ref, acc):
        k = pl.program_id(1)
        part = (jnp.dot(da_ref[...], wg_ref[...], preferred_element_type=F32)
                + jnp.dot(db_ref[...], wu_ref[...], preferred_element_type=F32))

        @pl.when(k == 0)
        def _():
            acc[...] = part

        @pl.when(k > 0)
        def _():
            acc[...] += part

        @pl.when(k == N_CHIPS - 1)
        def _():
            o_ref[...] = acc[...]

    t_spec = pl.BlockSpec((None, FF_ROWS, FF_SHARD), lambda i, k: (k, i, 0))
    w_spec = pl.BlockSpec((None, FF_SHARD, D_MODEL), lambda i, k: (k, 0, 0))
    return pl.pallas_call(
        body, grid=(SEQ // FF_ROWS, N_CHIPS), in_specs=[t_spec, t_spec, w_spec, w_spec],
        out_specs=pl.BlockSpec((FF_ROWS, D_MODEL), lambda i, k: (i, 0)), out_shape=_sds((SEQ, D_MODEL)),
        scratch_shapes=[pltpu.VMEM((FF_ROWS, D_MODEL), F32)],
        compiler_params=_cp(("parallel", "arbitrary")), name="ffn_dh")(d_a, d_b, w_gate_t, w_up_t)


def _local_step(x, tgt, wts, small):
    s = SEQ
    cos_f, sin_s = [_to_phase_rows(t) for t in _rope_tables()]
    x = _reorder_rows(x, to_phase=True, name="phase_rows_x")
    tgt = _reorder_rows(tgt, to_phase=True, name="phase_rows_target")

    proj = _mm_cols(x, wts["w_in"], tm=1024, name="proj")

    attn, lse = _attention_fwd(proj, cos_f, sin_s)
    y_attn = _mm_cols(attn, wts["w_attn_br"], tm=s, name="y_attn")

    (abar_r, abar_i, bbar_r, bbar_i), ssm_vjp = jax.vjp(
        _ssm_discretise, small["ssm_a_re"], small["ssm_a_im"], small["ssm_log_dt"], small["ssm_b_re"], small["ssm_b_im"])
    b_in_r, b_in_i = [_slab_block_diag(b.transpose(0, 2, 1)).astype(BF16) for b in (bbar_r, bbar_i)]
    c_out_r = _slab_block_diag(small["ssm_c_re"].transpose(0, 2, 1)).astype(BF16)
    c_out_ni = _slab_block_diag(-small["ssm_c_im"].transpose(0, 2, 1)).astype(BF16)
    a_r, a_i = abar_r.reshape(1, SSM_LANES), abar_i.reshape(1, SSM_LANES)
    d_skip = small["ssm_d"].reshape(1, SSM_WIDTH)

    u_f = _to_scan_rows(proj[:, 3 * QKV_WIDTH:3 * QKV_WIDTH + SSM_WIDTH])
    u_p = u_f.astype(BF16)
    y_c, h_r, h_i, e_r, e_i = _ssm_forward(u_p, b_in_r, b_in_i, c_out_r, c_out_ni, a_r, a_i)

    def gelu_fwd(yc, u, dsk):
        y = yc + dsk * u
        return y, 0.5 * y * (1.0 + jnp.tanh(GELU_C * (y + GELU_K * y * y * y)))

    y_s5, gel = _rowwise(gelu_fwd, [y_c, u_f], [d_skip], [_sds((s, SSM_WIDTH)), _sds((s, SSM_WIDTH), BF16)],
                         tm=512, name="ssm_gelu")
    glu = _mm_cols(gel, wts["w_glu"], tm=s, name="glu")

    def glu_fwd(ga, gb):
        return ga * jax.nn.sigmoid(gb)

    (y_glu,) = _rowwise(glu_fwd, [(glu, SSM_WIDTH, 0), (glu, SSM_WIDTH, 1)], [], [_sds((s, SSM_WIDTH), BF16)],
                        tm=512, name="glu_gate")
    y_glu = _from_scan_rows(y_glu)
    y_ssm = _mm_cols(y_glu, wts["w_ssm_br"], tm=s, name="y_ssm")

    gl0 = (proj, D_MODEL, (3 * QKV_WIDTH + SSM_WIDTH) // D_MODEL)
    gl1 = (proj, D_MODEL, (3 * QKV_WIDTH + SSM_WIDTH) // D_MODEL + 1)
    b_gate = small["b_gate"]

    def gate_mix(l0, l1, ya, ys, bg):
        return jax.nn.sigmoid(l0 + bg[0:1]) * ya + jax.nn.sigmoid(l1 + bg[1:2]) * ys

    (mixed,) = _rowwise(gate_mix, [gl0, gl1, y_attn, y_ssm], [b_gate], [_sds((s, D_MODEL), BF16)], tm=256,
                        name="gate_mix")
    w_out = wts["w_out"].reshape(D_MODEL, D_MODEL)
    mix_out = _mm_plain(mixed, w_out, tm=1024, tn=512, name="mix_out")

    def ln1_fwd(xv, mo, g, b):
        z = DN_ALPHA * xv + mo
        xhat, _ = _ln_stats(z)
        return z, xhat * g + b

    z1, h = _rowwise(ln1_fwd, [x, mix_out], [small["ln1_g"], small["ln1_b"]],
                     [_sds((s, D_MODEL)), _sds((s, D_MODEL))], tm=256, name="ln1")

    nf = D_FF // N_CHIPS
    w_gate_t, w_up_t, w_down = wts["w_ff_gate"], wts["w_ff_up"], wts["w_ff_down"]
    ff_a, ff_b, act = _ffn_up(h, w_gate_t, w_up_t)
    ff = _matmul(act, w_down, grid=(2, N_CHIPS),
                 a_spec=pl.BlockSpec((None, 1024, nf), lambda i, k: (k, i, 0)),
                 b_spec=pl.BlockSpec((None, nf, D_MODEL), lambda i, k: (k, 0, 0)),
                 o_spec=pl.BlockSpec((1024, D_MODEL), lambda i, k: (i, 0)),
                 out_shape=_sds((s, D_MODEL)), dims=(1, 0), k_axis=1, name="ff_down")

    def ln2_loss(hv, ffv, tg, g, b):
        z = DN_ALPHA * hv + ffv
        xhat, rstd = _ln_stats(z)
        err = xhat * g + b - tg
        d_out = err * (1.0 / D_MODEL)
        loss_rows = jnp.sum(err * err, axis=-1, keepdims=True) * (0.5 / D_MODEL)
        loss = jnp.broadcast_to(jnp.sum(loss_rows, axis=0, keepdims=True), (1, 128))
        return _ln_bwd(d_out, xhat, rstd, g), loss, _colsum(d_out * xhat), _colsum(d_out)

    dz2, loss_v, d_ln2_g, d_ln2_b = _rowwise(
        ln2_loss, [h, ff, tgt], [small["ln2_g"], small["ln2_b"]], [_sds((s, D_MODEL))],
        [_sds((1, 128)), _sds((1, D_MODEL)), _sds((1, D_MODEL))], tm=256, name="ln2_loss")

    d_a, d_b = _ffn_down_bwd(dz2, w_down, ff_a, ff_b)

    def grad_rows(lhs, rhs, name):
        return _matmul(lhs, rhs, grid=(N_CHIPS,), a_spec=pl.BlockSpec((None, s, nf), lambda k: (k, 0, 0)),
                       b_spec=pl.BlockSpec((s, D_MODEL), lambda k: (0, 0)),
                       o_spec=pl.BlockSpec((None, nf, D_MODEL), lambda k: (k, 0, 0)),
                       out_shape=_sds((N_CHIPS, nf, D_MODEL), BF16), dims=(0, 0), name=name)

    g_w_ff_down = grad_rows(act, dz2, "g_w_ff_down")
    g_w_ff_gate = grad_rows(d_a, h, "g_w_ff_gate")
    g_w_ff_up = grad_rows(d_b, h, "g_w_ff_up")
    dh_ff = _ffn_dh(d_a, d_b, w_gate_t, w_up_t)

    def ln1_bwd(dz, dff, z, g):
        xhat, rstd = _ln_stats(z)
        dh = DN_ALPHA * dz + dff
        return _ln_bwd(dh, xhat, rstd, g), _colsum(dh * xhat), _colsum(dh)

    dz1, d_ln1_g, d_ln1_b = _rowwise(ln1_bwd, [dz2, dh_ff, z1], [small["ln1_g"]], [_sds((s, D_MODEL))],
                                     [_sds((1, D_MODEL)), _sds((1, D_MODEL))], tm=256, name="ln1_bwd")
    d_mixed = _mm_plain(dz1, w_out, tm=1024, tn=512, dims=(1, 1), name="d_mixed")
    g_w_out = _mm_plain(mixed, dz1, tm=D_MODEL, tn=512, dims=(0, 0), out_dtype=BF16, name="g_w_out")
    g_w_out = g_w_out.reshape(N_CHIPS, D_MODEL // N_CHIPS, D_MODEL)

    def gate_bwd(dm, l0, l1, ya, ys, bg):
        g0 = jax.nn.sigmoid(l0 + bg[0:1])
        g1 = jax.nn.sigmoid(l1 + bg[1:2])
        dl0 = dm * ya * g0 * (1.0 - g0)
        dl1 = dm * ys * g1 * (1.0 - g1)
        return dm * g0, dm * g1, jnp.concatenate([dl0, dl1], axis=1), _colsum(dl0), _colsum(dl1)

    d_y_attn, d_y_ssm, d_gl, d_bg0, d_bg1 = _rowwise(
        gate_bwd, [d_mixed, gl0, gl1, y_attn, y_ssm], [b_gate],
        [_sds((s, D_MODEL), BF16), _sds((s, D_MODEL), BF16), _sds((s, 2 * D_MODEL), BF16)],
        [_sds((1, D_MODEL)), _sds((1, D_MODEL))], tm=256, name="gate_bwd")

    g_w_ssm_br = _mm_cols_tn(y_glu, d_y_ssm, ns=D_MODEL // N_CHIPS, name="g_w_ssm_br")
    d_y_glu = _to_scan_rows(_mm_cols_nt(d_y_ssm, wts["w_ssm_br"], tm=s, name="d_y_glu"))

    def glu_bwd(dy, ga, gb):
        sg = jax.nn.sigmoid(gb)
        return jnp.concatenate([dy * sg, dy * ga * sg * (1.0 - sg)], axis=1)

    (d_glu,) = _rowwise(glu_bwd, [d_y_glu, (glu, SSM_WIDTH, 0), (glu, SSM_WIDTH, 1)], [],
                        [_sds((s, 2 * SSM_WIDTH), BF16)], tm=512, name="glu_bwd")
    g_w_glu = _mm_cols_tn(gel, d_glu, ns=2 * SSM_WIDTH // N_CHIPS, name="g_w_glu")
    d_gel = _mm_cols_nt(d_glu, wts["w_glu"], tm=s, name="d_gel")

    def gelu_bwd(dg, y, u, dsk):
        th = jnp.tanh(GELU_C * (y + GELU_K * y * y * y))
        dy = dg * (0.5 * (1.0 + th) + 0.5 * y * (1.0 - th * th) * GELU_C * (1.0 + 3.0 * GELU_K * y * y))
        return dy, dy * dsk, _colsum(dy * u)

    d_y, d_u_skip, d_ssm_d = _rowwise(gelu_bwd, [d_gel, y_s5, u_f], [d_skip],
                                      [_sds((s, SSM_WIDTH), BF16), _sds((s, SSM_WIDTH))], [_sds((1, SSM_WIDTH))],
                                      tm=512, name="gelu_bwd")
    d_u, d_abar_r, d_abar_i, d_c_r, d_c_ni, d_bin_r, d_bin_i = _ssm_backward(
        d_y, d_u_skip, u_p, h_r, h_i, e_r, e_i, b_in_r, b_in_i, c_out_r, c_out_ni, a_r, a_i)
    d_u = _from_scan_rows(d_u)
    d_bbar_r = d_bin_r.reshape(SSM_GROUPS, SSM_GROUP, SSM_STATE).transpose(0, 2, 1)
    d_bbar_i = d_bin_i.reshape(SSM_GROUPS, SSM_GROUP, SSM_STATE).transpose(0, 2, 1)
    d_a_re, d_a_im, d_log_dt, d_b_re, d_b_im = ssm_vjp(
        (d_abar_r.reshape(SSM_GROUPS, SSM_STATE), d_abar_i.reshape(SSM_GROUPS, SSM_STATE), d_bbar_r, d_bbar_i))
    d_c_re = d_c_r.reshape(SSM_GROUPS, SSM_STATE, SSM_GROUP).transpose(0, 2, 1)
    d_c_im = -d_c_ni.reshape(SSM_GROUPS, SSM_STATE, SSM_GROUP).transpose(0, 2, 1)

    g_w_attn_br = _mm_cols_tn(attn, d_y_attn, ns=D_MODEL // N_CHIPS, name="g_w_attn_br")
    d_attn = _mm_cols_nt(d_y_attn, wts["w_attn_br"], tm=s, name="d_attn")
    dqkv = [_attention_bwd(g, proj, cos_f, sin_s, d_attn, attn, lse) for g in range(len(DILATIONS))]

    d_proj = jnp.concatenate([dqkv[g][j] for j in range(3) for g in range(len(DILATIONS))] + [d_u, d_gl],
                             axis=1)
    dx_proj = _mm_cols_nt(d_proj, wts["w_in"], tm=1024, name="dx_proj")
    g_w_in = _mm_cols_tn(x, d_proj, ns=IN_WIDTH // N_CHIPS, name="g_w_in", after=(dx_proj,))

    def dx_sum(dz, dxp):
        return DN_ALPHA * dz + dxp

    (grad_x,) = _rowwise(dx_sum, [dz1, dx_proj], [], [_sds((s, D_MODEL))], tm=512, name="grad_x", after=(g_w_in,))
    grad_x = _reorder_rows(grad_x, to_phase=False, name="time_rows_grad_x")

    big = {"w_in": g_w_in, "w_attn_br": g_w_attn_br, "w_ssm_br": g_w_ssm_br, "w_out": g_w_out, "w_glu": g_w_glu,
           "w_ff_gate": g_w_ff_gate, "w_ff_up": g_w_ff_up, "w_ff_down": g_w_ff_down}
    small_g = {"b_gate": jnp.concatenate([d_bg0, d_bg1], axis=0), "ssm_a_re": d_a_re, "ssm_a_im": d_a_im,
               "ssm_log_dt": d_log_dt, "ssm_b_re": d_b_re, "ssm_b_im": d_b_im, "ssm_c_re": d_c_re, "ssm_c_im": d_c_im,
               "ssm_d": d_ssm_d.reshape(SSM_WIDTH), "ln1_g": d_ln1_g, "ln1_b": d_ln1_b, "ln2_g": d_ln2_g,
               "ln2_b": d_ln2_b}
    marks = {"ln1_bwd": dz1, "scan_bwd": d_abar_r,"attention_bwd_0": dqkv[0][0], "dx_proj": dx_proj}
    return loss_v[0, 0], grad_x, big, small_g, marks


GATHER_ID, SWAP_ID, SCATTER_ID, JOIN_ID, EXCHANGE_ID = 1, 2, 3, 4, 5


def _place():
    return lax.axis_index("x"), lax.axis_index("y"), lax.axis_index("c")


def _other_chips(x, y):
    return [(1 - x, y), (x, 1 - y), (1 - x, 1 - y)]


def _handshake(peers):
    barrier = pltpu.get_barrier_semaphore()
    for peer in peers:
        pl.semaphore_signal(barrier, inc=1, device_id=peer, device_id_type=MESH)
    pl.semaphore_wait(barrier, len(peers))


def _sequencer(body, arrays, out_type, sems, collective_id, name):
    return pl.kernel(body, name=name, out_type=out_type,
                     mesh=plsc.ScalarSubcoreMesh(axis_name="sequencer", num_cores=1), scratch_types=sems,
                     compiler_params=pltpu.CompilerParams(collective_id=collective_id))(*arrays)


def _gather_weights(shards, *, name):
    nw = len(shards)

    def body(*refs):
        ins, outs = refs[:nw], refs[nw:2 * nw]
        send_sems, recv_sems, pass_send, pass_recv, local_sems = refs[2 * nw:]
        x, y, c = _place()
        chip = 2 * x + y
        chips = _other_chips(x, y)
        _handshake([(x, y, 1 - c)] + [(cx, cy, c) for cx, cy in chips])
        started = []
        for w in range(nw):
            hw = shards[w].shape[0] // 2
            mine = pl.ds(c * hw, hw)
            own = pltpu.make_async_copy(ins[w], outs[w].at[chip], local_sems.at[w])
            own.start()
            started.append(own)
            for j, (cx, cy) in enumerate(chips):
                cp = pltpu.make_async_remote_copy(
                    src_ref=ins[w].at[mine], dst_ref=outs[w].at[chip, mine], send_sem=send_sems.at[w, j],
                    recv_sem=recv_sems.at[w, j], device_id=(cx, cy, c), device_id_type=MESH)
                cp.start()
                started.append(cp)
        passed = []
        for w in range(nw):
            hw = shards[w].shape[0] // 2
            mine = pl.ds(c * hw, hw)
            for j, (cx, cy) in enumerate(chips):
                landed = outs[w].at[2 * cx + cy, mine]
                pltpu.make_async_remote_copy(
                    src_ref=ins[w].at[mine], dst_ref=landed, send_sem=send_sems.at[w, j],
                    recv_sem=recv_sems.at[w, j], device_id=(cx, cy, c), device_id_type=MESH).wait_recv()
                cp = pltpu.make_async_remote_copy(
                    src_ref=landed, dst_ref=landed, send_sem=pass_send.at[w, j], recv_sem=pass_recv.at[w, j],
                    device_id=(x, y, 1 - c), device_id_type=MESH)
                cp.start()
                passed.append(cp)
        for w in range(nw):
            hw = shards[w].shape[0] // 2
            theirs = pl.ds((1 - c) * hw, hw)
            for j, (cx, cy) in enumerate(chips):
                landed = outs[w].at[2 * cx + cy, theirs]
                pltpu.make_async_remote_copy(
                    src_ref=landed, dst_ref=landed, send_sem=pass_send.at[w, j], recv_sem=pass_recv.at[w, j],
                    device_id=(x, y, 1 - c), device_id_type=MESH).wait_recv()
        for cp in started[0::4]:
            cp.wait()
        for cp in [s for i, s in enumerate(started) if i % 4] + passed:
            cp.wait_send()

    sem = pltpu.SemaphoreType.DMA
    return _sequencer(body, shards, [_sds((N_CHIPS,) + a.shape, a.dtype) for a in shards],
                      [sem((nw, 3)), sem((nw, 3)), sem((nw, 3)), sem((nw, 3)), sem((nw,))], GATHER_ID, name)


def _swap_other_halves(grads, *, name):
    nw = len(grads)

    def body(*refs):
        ins, outs = refs[:nw], refs[nw:2 * nw]
        send_sems, recv_sems = refs[2 * nw:]
        x, y, c = _place()
        _handshake([(x, y, 1 - c)])
        cps = []
        for w in range(nw):
            hw = grads[w].shape[1] // 2
            cp = pltpu.make_async_remote_copy(
                src_ref=ins[w].at[:, pl.ds((1 - c) * hw, hw)], dst_ref=outs[w], send_sem=send_sems.at[w],
                recv_sem=recv_sems.at[w], device_id=(x, y, 1 - c), device_id_type=MESH)
            cp.start()
            cps.append(cp)
        for cp in cps:
            cp.wait()

    sem = pltpu.SemaphoreType.DMA
    return _sequencer(body, grads, [_sds((N_CHIPS, g.shape[1] // 2, g.shape[2]), g.dtype) for g in grads],
                      [sem((nw,)), sem((nw,))], SWAP_ID, name)


def _add_my_halves(core, grads, others, *, name, after=()):
    nw = len(grads)
    halves = [g.shape[1] // 2 for g in grads]

    def body(core_ref, *refs):
        outs = refs[2 * nw + len(after):]
        for g_ref, o_ref, out_ref in zip(refs[:nw], refs[nw:2 * nw], outs):
            out_ref[...] = (g_ref[...].astype(F32) + o_ref[...].astype(F32)).astype(out_ref.dtype)

    in_specs = [pl.BlockSpec((None, None, hw, g.shape[2]), lambda s, core_ref: (s, core_ref[0], 0, 0))
                for g, hw in zip(grads, halves)]
    in_specs += [pl.BlockSpec((None, hw, g.shape[2]), lambda s, core_ref: (s, 0, 0)) for g, hw in zip(grads, halves)]
    return pl.pallas_call(
        body,
        grid_spec=pltpu.PrefetchScalarGridSpec(
            num_scalar_prefetch=1, grid=(N_CHIPS,), in_specs=in_specs + [HBM_OPERAND] * len(after),
            out_specs=[pl.BlockSpec((None, hw, g.shape[2]), lambda s, core_ref: (s, 0, 0))
                       for g, hw in zip(grads, halves)]),
        out_shape=[_sds((N_CHIPS, hw, g.shape[2]), BF16) for g, hw in zip(grads, halves)],
        compiler_params=_cp(("parallel",)), name=name)(
            core, *[g.reshape(N_CHIPS, 2, hw, g.shape[2]) for g, hw in zip(grads, halves)], *others, *after)


def _scatter_partials(parts, *, name):
    nw = len(parts)

    def body(*refs):
        ins, outs = refs[:nw], refs[nw:2 * nw]
        send_sems, recv_sems = refs[2 * nw:]
        x, y, c = _place()
        _handshake([(cx, cy, c) for cx, cy in _other_chips(x, y)])
        cps = []
        for w in range(nw):
            for j, (cx, cy) in enumerate(_other_chips(x, y)):
                cp = pltpu.make_async_remote_copy(
                    src_ref=ins[w].at[2 * cx + cy], dst_ref=outs[w].at[j], send_sem=send_sems.at[w, j],
                    recv_sem=recv_sems.at[w, j], device_id=(cx, cy, c), device_id_type=MESH)
                cp.start()
                cps.append(cp)
        for cp in cps:
            cp.wait()

    sem = pltpu.SemaphoreType.DMA
    return _sequencer(body, parts, [_sds((3,) + p.shape[1:], p.dtype) for p in parts],
                      [sem((nw, 3)), sem((nw, 3))], SCATTER_ID, name)


SUM_STEPS = 2


def _sum_partials(chip, parts, recvd, *, name, after=()):
    nw = len(parts)
    rows = [p.shape[1] // SUM_STEPS for p in parts]

    def body(chip_ref, *refs):
        outs = refs[2 * nw + len(after):]
        for p_ref, r_ref, out_ref in zip(refs[:nw], refs[nw:2 * nw], outs):
            acc = p_ref[...].astype(F32)
            for j in range(3):
                acc = acc + r_ref[j].astype(F32)
            out_ref[...] = acc

    in_specs = [pl.BlockSpec((None, th, p.shape[2]), lambda i, chip_ref: (chip_ref[0], i, 0))
                for p, th in zip(parts, rows)]
    in_specs += [pl.BlockSpec((3, th, p.shape[2]), lambda i, chip_ref: (0, i, 0)) for p, th in zip(parts, rows)]
    return pl.pallas_call(
        body,
        grid_spec=pltpu.PrefetchScalarGridSpec(
            num_scalar_prefetch=1, grid=(SUM_STEPS,), in_specs=in_specs + [HBM_OPERAND] * len(after),
            out_specs=[pl.BlockSpec((th, p.shape[2]), lambda i, chip_ref: (i, 0)) for p, th in zip(parts, rows)]),
        out_shape=[_sds(p.shape[1:]) for p in parts], compiler_params=_cp(("parallel",)), name=name)(
            chip, *parts, *recvd, *after)


def _swap_reduced_halves(halves, *, name):
    nw = len(halves)

    def body(*refs):
        ins, outs = refs[:nw], refs[nw:2 * nw]
        send_sems, recv_sems = refs[2 * nw:]
        x, y, c = _place()
        _handshake([(x, y, 1 - c)])
        cps = []
        for w in range(nw):
            cp = pltpu.make_async_remote_copy(
                src_ref=ins[w], dst_ref=outs[w], send_sem=send_sems.at[w], recv_sem=recv_sems.at[w],
                device_id=(x, y, 1 - c), device_id_type=MESH)
            cp.start()
            cps.append(cp)
        for cp in cps:
            cp.wait()

    sem = pltpu.SemaphoreType.DMA
    return _sequencer(body, halves, [_sds(h.shape, h.dtype) for h in halves], [sem((nw,)), sem((nw,))], JOIN_ID, name)


def _exchange_rows(vec, *, name):
    def body(v_ref, slots, send_sems, recv_sems, local_sem):
        x, y, c = _place()
        me = 4 * x + 2 * y + c
        peers = []
        for mask in range(1, N_DEV):
            peers.append((1 - x if mask & 4 else x, 1 - y if mask & 2 else y, 1 - c if mask & 1 else c))
        _handshake(peers)
        own = pltpu.make_async_copy(v_ref, slots.at[me], local_sem)
        own.start()
        cps = []
        for k, peer in enumerate(peers):
            cp = pltpu.make_async_remote_copy(
                src_ref=v_ref, dst_ref=slots.at[me], send_sem=send_sems.at[k], recv_sem=recv_sems.at[k],
                device_id=peer, device_id_type=MESH)
            cp.start()
            cps.append(cp)
        for k, (px, py, pc) in enumerate(peers):
            pltpu.make_async_remote_copy(
                src_ref=v_ref, dst_ref=slots.at[4 * px + 2 * py + pc], send_sem=send_sems.at[k],
                recv_sem=recv_sems.at[k], device_id=(px, py, pc), device_id_type=MESH).wait_recv()
        for cp in cps:
            cp.wait_send()
        own.wait()

    sem = pltpu.SemaphoreType.DMA
    return _sequencer(body, [vec], [_sds((N_DEV,) + vec.shape)], [sem((N_DEV - 1,)), sem((N_DEV - 1,)), sem(())],
                      EXCHANGE_ID, name)[0]


def _sum_slots(slots, *, name, after=()):
    def body(s_ref, *rest):
        out_ref = rest[len(after)]
        acc = s_ref[0]
        for d in range(1, N_DEV):
            acc = acc + s_ref[d]
        out_ref[...] = acc

    vmem = pl.BlockSpec(memory_space=pltpu.VMEM)
    return pl.pallas_call(
        body, in_specs=[vmem] + [HBM_OPERAND] * len(after), out_specs=vmem, out_shape=_sds(slots.shape[1:]),
        compiler_params=pltpu.CompilerParams(vmem_limit_bytes=VMEM_LIMIT_BYTES), name=name)(slots, *after)


def _reduce_scatter_start(grads, core, *, tag, add_after=()):
    others = _swap_other_halves(grads, name="swap_other_halves_" + tag)
    parts = _add_my_halves(core, grads, others, name="add_my_halves_" + tag, after=add_after)
    return parts, _scatter_partials(parts, name="scatter_partials_" + tag)


def _reduce_scatter_finish(parts, recvd, chip, *, tag, sum_after=()):
    mine = _sum_partials(chip, parts, recvd, name="sum_partials_" + tag, after=sum_after)
    return mine, _swap_reduced_halves(mine, name="swap_reduced_halves_" + tag)


ADAM_BLOCK_ELEMS = 256 * 1024


def _adam_rows(rows, cols):
    tm = rows
    while tm * cols > ADAM_BLOCK_ELEMS and tm % 16 == 0:
        tm //= 2
    return tm


def _adam_step(wv, gv, mv, vv):
    m2 = ADAM_B1 * mv + (1.0 - ADAM_B1) * gv
    v2 = ADAM_B2 * vv + (1.0 - ADAM_B2) * (gv * gv)
    m_hat = m2 / (1.0 - ADAM_B1 ** ADAM_STEP)
    v_hat = v2 / (1.0 - ADAM_B2 ** ADAM_STEP)
    return -ADAM_LR * (m_hat / (jnp.sqrt(v_hat) + ADAM_EPS) + ADAM_WD * wv), m2, v2


def _adamw(w, g, m, v, *, name):
    rows, cols = w.shape
    return _rowwise(_adam_step, [w, g, m, v], [], [_sds((rows, cols))] * 3, tm=_adam_rows(rows, cols), name=name)


def _adamw_halves(core, w, g_mine, g_theirs, m, v, *, name, after=()):
    rows, cols = w.shape
    hw = rows // 2
    tm = _adam_rows(hw, cols)
    per_half = hw // tm

    def body(core_ref, w_ref, gm_ref, gt_ref, m_ref, v_ref, *rest):
        g_out, d_out, m_out, v_out = rest[len(after):]
        mine = (pl.program_id(0) // per_half) == core_ref[0]
        g = jnp.where(mine, gm_ref[...], gt_ref[...])
        d, m2, v2 = _adam_step(w_ref[...], g, m_ref[...], v_ref[...])
        g_out[...] = g
        d_out[...] = d
        m_out[...] = m2
        v_out[...] = v2

    full = pl.BlockSpec((tm, cols), lambda i, core_ref: (i, 0))
    half = pl.BlockSpec((tm, cols), lambda i, core_ref: (i % per_half, 0))
    return pl.pallas_call(
        body,
        grid_spec=pltpu.PrefetchScalarGridSpec(
            num_scalar_prefetch=1, grid=(rows // tm,),
            in_specs=[full, half, half, full, full] + [HBM_OPERAND] * len(after), out_specs=[full, full, full, full]),
        out_shape=[_sds((rows, cols))] * 4, compiler_params=_cp(("parallel",)), name=name)(
            core, w, g_mine, g_theirs, m, v, *after)


HELD_TRANSPOSED = ("w_ff_gate", "w_ff_up")


def _as_rows(name, arr):
    return arr[0].T if name in HELD_TRANSPOSED else arr[0]


def _from_rows(name, arr2d):
    return (arr2d.T if name in HELD_TRANSPOSED else arr2d)[None]


STORED_SWAPPED = ("ssm_b_re", "ssm_b_im")


def _as_stored(name, arr):
    return jnp.swapaxes(arr, -1, -2) if name in STORED_SWAPPED else arr


def _pack_rows(arrs):
    flat = jnp.concatenate([a.reshape(-1).astype(F32) for a in arrs])
    rows = -(-flat.shape[0] // 1024) * 8
    return jnp.pad(flat, (0, rows * 128 - flat.shape[0])).reshape(rows, 128)


def _unpack_rows(vec, shapes):
    flat = vec.reshape(-1)
    out, off = [], 0
    for shp in shapes:
        size = math.prod(shp)
        out.append(flat[off:off + size].reshape(shp))
        off += size
    return out


SMALL = ("b_gate", "ssm_a_re", "ssm_a_im", "ssm_log_dt", "ssm_b_re", "ssm_b_im", "ssm_c_re", "ssm_c_im", "ssm_d",
         "ln1_g", "ln1_b", "ln2_g", "ln2_b")
GATHER_GROUPS = (("w_in", ("w_in",)), ("mixer", ("w_attn_br", "w_ssm_br", "w_glu", "w_out")),
                 ("ffn", ("w_ff_gate", "w_ff_up", "w_ff_down")))
REDUCE_GROUPS = (("ffn", ("w_ff_down", "w_ff_gate", "w_ff_up")),
                 ("mixer", ("w_out", "w_ssm_br", "w_glu", "w_attn_br")), ("w_in", ("w_in",)))
WEIGHTS = ("w_in", "b_gate", "w_attn_br", "w_ssm_br", "w_out", "ssm_a_re", "ssm_a_im", "ssm_log_dt", "ssm_b_re",
           "ssm_b_im", "ssm_c_re", "ssm_c_im", "ssm_d", "w_glu", "ln1_g", "ln1_b", "w_ff_gate", "w_ff_up", "w_ff_down",
           "ln2_g", "ln2_b")


def kernel(x, w_in, b_gate, w_attn_br, w_ssm_br, w_out, ssm_a_re, ssm_a_im, ssm_log_dt, ssm_b_re, ssm_b_im, ssm_c_re, ssm_c_im, ssm_d, w_glu, ln1_g, ln1_b, w_ff_gate, w_ff_up, w_ff_down, ln2_g, ln2_b, loss_target, m_w_in, m_b_gate, m_w_attn_br, m_w_ssm_br, m_w_out, m_ssm_a_re, m_ssm_a_im, m_ssm_log_dt, m_ssm_b_re, m_ssm_b_im, m_ssm_c_re, m_ssm_c_im, m_ssm_d, m_w_glu, m_ln1_g, m_ln1_b, m_w_ff_gate, m_w_ff_up, m_w_ff_down, m_ln2_g, m_ln2_b, v_w_in, v_b_gate, v_w_attn_br, v_w_ssm_br, v_w_out, v_ssm_a_re, v_ssm_a_im, v_ssm_log_dt, v_ssm_b_re, v_ssm_b_im, v_ssm_c_re, v_ssm_c_im, v_ssm_d, v_w_glu, v_ln1_g, v_ln1_b, v_w_ff_gate, v_w_ff_up, v_w_ff_down, v_ln2_g, v_ln2_b):
    given = dict(locals())
    px, py, pc = _place()
    chip = 2 * px + py
    core_s = jnp.reshape(pc, (1,)).astype(jnp.int32)
    chip_s = jnp.reshape(chip, (1,)).astype(jnp.int32)

    wts = {}
    for tag, names in GATHER_GROUPS:
        wts.update(zip(names, _gather_weights([_as_rows(n, given[n]).astype(BF16) for n in names],
                                              name="gather_" + tag)))
    ncol = D_MODEL // N_CHIPS
    bg_mine = jnp.where(pc == 0, b_gate[0], jnp.zeros_like(b_gate[0]))
    bg_full = lax.dynamic_update_slice(jnp.zeros((2, D_MODEL), F32), bg_mine, (0, chip * ncol))
    bg_slots = _exchange_rows(bg_full.reshape(16, 128), name="exchange_gate_bias")
    bg_full = _sum_slots(bg_slots, name="sum_gate_bias").reshape(2, D_MODEL)
    small = {n: given[n][0] for n in SMALL if n.startswith("ssm")}
    small.update({n: given[n] for n in ("ln1_g", "ln1_b", "ln2_g", "ln2_b")})
    small["b_gate"] = bg_full

    loss_mine, grad_x, big_g, small_g, marks = _local_step(x[0], loss_target[0], wts, small)
    loss = lax.psum(loss_mine, ("x", "y", "c"))

    groups = dict(REDUCE_GROUPS)
    add_after = {"ffn": (marks["ln1_bwd"],), "mixer": (marks["scan_bwd"],), "w_in": (grad_x,)}
    parts, recvd = {}, {}
    for tag, names in REDUCE_GROUPS:
        parts[tag], recvd[tag] = _reduce_scatter_start([big_g[n] for n in names], core_s, tag=tag,
                                                       add_after=add_after[tag])
    grads, delta, new_m, new_v = {}, {}, {}, {}

    def finish(tag, sum_after, adam_after):
        mine, theirs = _reduce_scatter_finish(parts[tag], recvd[tag], chip_s, tag=tag, sum_after=sum_after)
        for n, g_mine, g_theirs in zip(groups[tag], mine, theirs):
            res = _adamw_halves(core_s, _as_rows(n, given[n]), g_mine, g_theirs, _as_rows(n, given["m_" + n]),
                                _as_rows(n, given["v_" + n]), name="adamw_" + n, after=adam_after)
            grads[n], delta[n], new_m[n], new_v[n] = [_from_rows(n, r) for r in res]

    in_flight = (parts["w_in"][0],)
    finish("ffn", (marks["scan_bwd"],), in_flight)
    finish("mixer", (marks["attention_bwd_0"],), in_flight)
    stored = [_as_stored(n, small_g[n]) for n in SMALL]
    slots = _exchange_rows(_pack_rows(stored), name="exchange_small")
    summed = _unpack_rows(_sum_slots(slots, name="sum_small", after=in_flight), [a.shape for a in stored])
    for n, g in zip(SMALL, summed):
        g = _as_stored(n, g)
        if n == "b_gate":
            g = lax.dynamic_slice(g, (0, chip * ncol), (2, ncol))
        grads[n] = g.reshape(given[n].shape)
    packed = [_pack_rows([_as_stored(n, src[n]) for n in SMALL]) for src in
              (given, grads, {n: given["m_" + n] for n in SMALL}, {n: given["v_" + n] for n in SMALL})]
    shapes = [_as_stored(n, given[n]).shape for n in SMALL]
    small_out = _adamw(*packed, name="adamw_small")
    for out, vec in zip((delta, new_m, new_v), small_out):
        out.update((n, _as_stored(n, a)) for n, a in zip(SMALL, _unpack_rows(vec, shapes)))
    behind = [delta[n] for tag in ("ffn", "mixer") for n in groups[tag]] + [small_out[0]]
    finish("w_in", tuple(behind), ())

    return (loss, grad_x.reshape(x.shape), *[grads[n] for n in WEIGHTS], *[delta[n] for n in WEIGHTS],
            *[new_m[n] for n in WEIGHTS], *[new_v[n] for n in WEIGHTS])
```

```python
import math

import jax
import jax.numpy as jnp
from jax import lax
from jax.experimental import pallas as pl
from jax.experimental.pallas import tpu as pltpu
from jax.experimental.pallas import tpu_sc as plsc

F32 = jnp.float32
BF16 = jnp.bfloat16
MESH = pl.DeviceIdType.MESH

D_MODEL = 1024
SEQ = 2048
HEAD_DIM = 64
ATTN_HEADS = 8
DILATIONS = (1, 4, 16)
ATTN_WIDTH = ATTN_HEADS * HEAD_DIM
QKV_WIDTH = 3 * ATTN_WIDTH
BLOCK = 128
ROPE_THETA = 10000.0
NEG_INF = -1e30
SSM_GROUP = 16
SSM_GROUPS = 32
SSM_WIDTH = 512
SSM_STATE = 64
SSM_LANES = SSM_GROUPS * SSM_STATE
SCAN_CHUNKS = 8
SCAN_STEPS = SEQ // SCAN_CHUNKS
IN_WIDTH = 3 * QKV_WIDTH + SSM_WIDTH + 2 * D_MODEL
D_FF = 2816
N_CHIPS = 4
N_DEV = 8
DN_ALPHA = 2.0 ** 0.25
LN_EPS = 1e-5
ADAM_LR = 0.001
ADAM_B1 = 0.9
ADAM_B2 = 0.999
ADAM_EPS = 1e-08
ADAM_WD = 0.01
ADAM_STEP = 10
GELU_C = math.sqrt(2.0 / math.pi)
GELU_K = 0.044715

VMEM_LIMIT_BYTES = 56 * 1024 * 1024


def _sds(shape, dtype=F32):
    return jax.ShapeDtypeStruct(tuple(shape), dtype)


def _cp(semantics=None):
    return pltpu.CompilerParams(dimension_semantics=semantics, vmem_limit_bytes=VMEM_LIMIT_BYTES)


HBM_OPERAND = pl.BlockSpec(memory_space=pl.ANY)


def _matmul(a, b, *, grid, a_spec, b_spec, o_spec, out_shape, dims, k_axis=None, name, after=(), into=None):
    nk = grid[k_axis] if k_axis is not None else 1
    o_block = tuple(d for d in o_spec.block_shape if d is not None)
    after = tuple(after) + ((into,) if into is not None else ())
    n_after = len(after)
    aliases = {1 + n_after: 0} if into is not None else {}

    def body(a_ref, b_ref, *rest):
        o_ref, acc = rest[n_after], rest[n_after + 1:]
        part = lax.dot_general(a_ref[...].astype(BF16), b_ref[...].astype(BF16),
                               (((dims[0],), (dims[1],)), ((), ())), preferred_element_type=F32)
        if k_axis is None:
            o_ref[...] = part.astype(o_ref.dtype)
        else:
            k = pl.program_id(k_axis)

            @pl.when(k == 0)
            def _():
                acc[0][...] = part

            @pl.when(k > 0)
            def _():
                acc[0][...] += part

            @pl.when(k == nk - 1)
            def _():
                o_ref[...] = acc[0][...].astype(o_ref.dtype)

    sem = tuple("arbitrary" if ax == k_axis else "parallel" for ax in range(len(grid)))
    return pl.pallas_call(
        body, grid=grid, in_specs=[a_spec, b_spec] + [HBM_OPERAND] * n_after, out_specs=o_spec, out_shape=out_shape,
        scratch_shapes=[pltpu.VMEM(o_block, F32)] if k_axis is not None else [], input_output_aliases=aliases,
        compiler_params=_cp(sem), name=name)(a, b, *after)


def _mm_cols(a, wg, *, tm, name, out_dtype=F32, part=(1, 0), into=None):
    m, k = a.shape
    ns = wg.shape[2]
    n, p = part
    return _matmul(a, wg, grid=(m // tm, N_CHIPS),
                   a_spec=pl.BlockSpec((tm, k), lambda i, s: (i, 0)),
                   b_spec=pl.BlockSpec((None, k, ns), lambda i, s: (s, 0, 0)),
                   o_spec=pl.BlockSpec((tm, ns), lambda i, s: (i, s * n + p)),
                   out_shape=_sds((m, N_CHIPS * ns * n), out_dtype), dims=(1, 0), name=name, into=into)


def _mm_cols_nt(dy, wg, *, tm, name, out_dtype=F32, after=(), part=(1, 0)):
    k, ns = wg.shape[1], wg.shape[2]
    n, p = part
    m = dy.shape[0]
    a_spec = pl.BlockSpec((tm, ns), lambda i, s: (i, s * n + p))
    return _matmul(dy, wg, grid=(m // tm, N_CHIPS), a_spec=a_spec,
                   b_spec=pl.BlockSpec((None, k, ns), lambda i, s: (s, 0, 0)),
                   o_spec=pl.BlockSpec((tm, k), lambda i, s: (i, 0)),
                   out_shape=_sds((m, k), out_dtype), dims=(1, 1), k_axis=1, name=name, after=after)


def _mm_cols_tn(a, dy, *, ns, name, after=()):
    m, k = a.shape
    return _matmul(a, dy, grid=(N_CHIPS,), a_spec=pl.BlockSpec((m, k), lambda s: (0, 0)),
                   b_spec=pl.BlockSpec((m, ns), lambda s: (0, s)),
                   o_spec=pl.BlockSpec((None, k, ns), lambda s: (s, 0, 0)),
                   out_shape=_sds((N_CHIPS, k, ns), BF16), dims=(0, 0), name=name, after=after)


def _mm_plain(a, b, *, tm, tn, name, out_dtype=F32, dims=(1, 0), tk=None):
    m = a.shape[1 - dims[0]]
    kk = a.shape[dims[0]]
    n = b.shape[1 - dims[1]]
    tk = kk if tk is None else tk
    nk = kk // tk

    def a_idx(i, j, k):
        return (i, k) if dims[0] == 1 else (k, i)

    def b_idx(i, j, k):
        return (k, j) if dims[1] == 0 else (j, k)

    a_blk = (tm, tk) if dims[0] == 1 else (tk, tm)
    b_blk = (tk, tn) if dims[1] == 0 else (tn, tk)
    return _matmul(a, b, grid=(m // tm, n // tn, nk),
                   a_spec=pl.BlockSpec(a_blk, a_idx), b_spec=pl.BlockSpec(b_blk, b_idx),
                   o_spec=pl.BlockSpec((tm, tn), lambda i, j, k: (i, j)),
                   out_shape=_sds((m, n), out_dtype), dims=dims, k_axis=2 if nk > 1 else None, name=name)


def _rowwise(fn, tiled, full, outs, accs=(), *, tm, name, after=()):
    args, in_specs = [], []
    for t in tiled:
        if isinstance(t, tuple):
            arr, w, cb = t
            in_specs.append(pl.BlockSpec((tm, w), lambda i, cb=cb: (i, cb)))
        else:
            arr = t
            in_specs.append(pl.BlockSpec((tm, arr.shape[1]), lambda i: (i, 0)))
        args.append(arr)
    rows = args[0].shape[0]
    for f in full:
        in_specs.append(pl.BlockSpec(f.shape, lambda i, nd=f.ndim: (0,) * nd))
        args.append(f)
    out_specs = [pl.BlockSpec((tm, o.shape[1]), lambda i: (i, 0)) for o in outs]
    out_specs += [pl.BlockSpec(a.shape, lambda i, nd=len(a.shape): (0,) * nd) for a in accs]
    n_in, n_out = len(args), len(outs)
    in_specs += [HBM_OPERAND] * len(after)
    first_out = n_in + len(after)

    def body(*refs):
        res = fn(*[r[...] for r in refs[:n_in]])
        res = res if isinstance(res, (tuple, list)) else (res,)
        for r, v in zip(refs[first_out:first_out + n_out], res[:n_out]):
            r[...] = v.astype(r.dtype)
        i = pl.program_id(0)
        for r, v in zip(refs[first_out + n_out:], res[n_out:]):
            @pl.when(i == 0)
            def _(r=r, v=v):
                r[...] = v

            @pl.when(i > 0)
            def _(r=r, v=v):
                r[...] += v

    res = pl.pallas_call(
        body, grid=(rows // tm,), in_specs=in_specs, out_specs=out_specs, out_shape=list(outs) + list(accs),
        compiler_params=_cp(("arbitrary",) if accs else ("parallel",)), name=name)(*args, *after)
    return res


def _colsum(v):
    return jnp.sum(v, axis=0, keepdims=True)


def _ln_stats(z):
    mu = jnp.mean(z, axis=-1, keepdims=True)
    zc = z - mu
    var = jnp.mean(zc * zc, axis=-1, keepdims=True)
    rstd = lax.rsqrt(var + LN_EPS)
    return zc * rstd, rstd


def _ln_bwd(dy, xhat, rstd, g):
    dxh = dy * g
    m1 = jnp.mean(dxh, axis=-1, keepdims=True)
    m2 = jnp.mean(dxh * xhat, axis=-1, keepdims=True)
    return rstd * (dxh - m1 - xhat * m2)


def _swap_halves(t):
    w = t.shape[-1]
    lane = lax.broadcasted_iota(jnp.int32, t.shape, t.ndim - 1)
    return jnp.where((lane % HEAD_DIM) < HEAD_DIM // 2, pltpu.roll(t, w - HEAD_DIM // 2, t.ndim - 1),
                     pltpu.roll(t, HEAD_DIM // 2, t.ndim - 1))


PHASES = max(DILATIONS)
PAIR = 2 * HEAD_DIM
UNITS = SEQ // BLOCK
UNIT_BATCH = 4
ROPE_ROWS = 256


def _to_phase_rows(t):
    return t.reshape(SEQ // PHASES, PHASES, t.shape[1]).transpose(1, 0, 2).reshape(t.shape)


def _reorder_rows(arr, *, to_phase, name):
    def body(i_ref, o_ref):
        for rho in range(PHASES):
            phase = pl.ds(rho * BLOCK, BLOCK)
            strided = pl.ds(rho, BLOCK, stride=PHASES)
            if to_phase:
                o_ref[phase, :] = i_ref[strided, :]
            else:
                o_ref[strided, :] = i_ref[phase, :]

    spec = pl.BlockSpec((SEQ, BLOCK), lambda j: (0, j))
    return pl.pallas_call(body, grid=(arr.shape[1] // BLOCK,), in_specs=[spec], out_specs=spec,
                          out_shape=_sds(arr.shape), compiler_params=_cp(("parallel",)), name=name)(arr)


def _rope(t, cf, ss):
    return t * cf + _swap_halves(t) * ss


def _rope_transposed(d, cf, ss):
    return d * cf + _swap_halves(d * ss)


def _unit_pieces(u, dil):
    pieces, length = PHASES // dil, 8 * dil
    if dil == 1:
        rho, i = 0, u
    elif dil == PHASES:
        rho, i = u, 0
    else:
        rho, i = jnp.bitwise_and(u, dil - 1), jnp.right_shift(u, dil.bit_length() - 1)
    before = jnp.maximum(i - 1, 0)
    cur = [pl.multiple_of((rho + dil * k) * BLOCK + length * i, 8) for k in range(pieces)]
    prev = [pl.multiple_of((rho + dil * k) * BLOCK + length * before, 8) for k in range(pieces)]
    return i, cur, prev


def _load_tile(ref, starts, dil):
    return jnp.concatenate([ref[pl.ds(st, 8 * dil), :] for st in starts], axis=0)


def _store_tile(ref, starts, dil, val, head=None, accumulate=False):
    length = 8 * dil
    lanes = slice(None) if head is None else pl.ds(head * HEAD_DIM, HEAD_DIM)
    cols = slice(None) if head is None else slice(head * HEAD_DIM, (head + 1) * HEAD_DIM)
    for k, st in enumerate(starts):
        piece = val[k * length:(k + 1) * length, cols]
        if accumulate:
            ref[pl.ds(st, length), lanes] += piece
        else:
            ref[pl.ds(st, length), lanes] = piece


def _tile_position(idx, dil):
    pieces, length = PHASES // dil, 8 * dil
    return pieces * jnp.bitwise_and(idx, length - 1) + jnp.right_shift(idx, length.bit_length() - 1)


def _band_mask(i, dil):
    row = lax.broadcasted_iota(jnp.int32, (BLOCK, 2 * BLOCK), 0)
    col = lax.broadcasted_iota(jnp.int32, (BLOCK, 2 * BLOCK), 1)
    key_pos = _tile_position(jnp.bitwise_and(col, BLOCK - 1), dil) + jnp.where(col >= BLOCK, 0, -BLOCK)
    dist = _tile_position(row, dil) - key_pos
    return (dist >= 0) & (dist <= BLOCK) & ((col >= BLOCK) | (i > 0))


def _causal_mask():
    row = lax.broadcasted_iota(jnp.int32, (BLOCK, BLOCK), 0)
    col = lax.broadcasted_iota(jnp.int32, (BLOCK, BLOCK), 1)
    return row >= col


def _pair_views(col0):
    return [pl.BlockSpec((SEQ, PAIR), lambda hp, g=g: (0, col0 // PAIR + g * (ATTN_WIDTH // PAIR) + hp))
            for g in range(len(DILATIONS))]


def _rotate(in_refs, out_refs, cf_ref, ss_ref, scale):
    def step(t, carry):
        rows = pl.ds(pl.multiple_of(t * ROPE_ROWS, ROPE_ROWS), ROPE_ROWS)
        cf, ss = cf_ref[rows, :] * scale, ss_ref[rows, :] * scale
        for i_ref, o_ref in zip(in_refs, out_refs):
            o_ref[rows, :] = _rope(i_ref[rows, :], cf, ss)
        return carry

    lax.fori_loop(0, SEQ // ROPE_ROWS, step, 0)


def _attention_fwd(proj, cos_f, sin_s):
    ng = len(DILATIONS)

    def body(*refs):
        q_refs, k_refs, v_refs = refs[:ng], refs[ng:2 * ng], refs[2 * ng:3 * ng]
        cf_ref, ss_ref, attn_ref, lse_ref = refs[3 * ng:3 * ng + 4]
        scratch = refs[3 * ng + 4:]
        qr_refs, kr_refs = scratch[:ng], scratch[ng:]
        _rotate(q_refs, qr_refs, cf_ref, ss_ref, 1.0 / math.sqrt(HEAD_DIM))
        _rotate(k_refs, kr_refs, cf_ref, ss_ref, 1.0)
        first = lax.broadcasted_iota(jnp.int32, (BLOCK, PAIR), 1) < HEAD_DIM
        for g, dil in enumerate(DILATIONS):
            two_blocks = SEQ // dil > BLOCK

            def units(t, carry, g=g, dil=dil, two_blocks=two_blocks):
                picked = [_unit_pieces(t * UNIT_BATCH + j, dil) for j in range(UNIT_BATCH)]

                def tiles(ref, with_prev=False):
                    if with_prev and two_blocks:
                        return jnp.stack([jnp.concatenate([_load_tile(ref, prev, dil), _load_tile(ref, rows, dil)],
                                                          axis=0) for _, rows, prev in picked])
                    return jnp.stack([_load_tile(ref, rows, dil) for _, rows, _ in picked])

                qq = tiles(qr_refs[g]).astype(BF16)
                kk = tiles(kr_refs[g], True).astype(BF16)
                vv = tiles(v_refs[g], True).astype(BF16)
                if two_blocks:
                    valid = jnp.stack([_band_mask(i, dil) for i, _, _ in picked])
                else:
                    valid = _causal_mask()[None]
                mine = first[None]
                zero = jnp.zeros_like(qq)
                outs, lses = [], []
                for qh in (jnp.where(mine, qq, zero), jnp.where(mine, zero, qq)):
                    s = jnp.einsum("pqd,pkd->pqk", qh, kk, preferred_element_type=F32)
                    s = jnp.where(valid, s, NEG_INF)
                    m = jnp.max(s, axis=-1, keepdims=True)
                    p = jnp.exp(s - m)
                    l = jnp.sum(p, axis=-1, keepdims=True)
                    outs.append(jnp.einsum("pqk,pkd->pqd", p.astype(BF16), vv, preferred_element_type=F32) * (1.0 / l))
                    lses.append(m + jnp.log(l))
                o = jnp.where(mine, outs[0], outs[1])
                lse = jnp.where(mine, lses[0], lses[1])
                if g > 0:
                    lse_old = tiles(lse_ref)
                    m = jnp.maximum(lse_old, lse)
                    lse_new = m + jnp.log(jnp.exp(lse_old - m) + jnp.exp(lse - m))
                    o = tiles(attn_ref) * jnp.exp(lse_old - lse_new) + o * jnp.exp(lse - lse_new)
                    lse = lse_new
                for j, (_, rows, _) in enumerate(picked):
                    _store_tile(attn_ref, rows, dil, o[j])
                    _store_tile(lse_ref, rows, dil, lse[j])
                return carry

            lax.fori_loop(0, UNITS // UNIT_BATCH, units, 0)

    whole = pl.BlockSpec((SEQ, PAIR), lambda hp: (0, 0))
    out = pl.BlockSpec((SEQ, PAIR), lambda hp: (0, hp))
    return pl.pallas_call(
        body, grid=(ATTN_WIDTH // PAIR,),
        in_specs=_pair_views(0) + _pair_views(QKV_WIDTH) + _pair_views(2 * QKV_WIDTH) + [whole, whole],
        out_specs=[out, out], out_shape=[_sds((SEQ, ATTN_WIDTH)), _sds((SEQ, ATTN_WIDTH))],
        scratch_shapes=[pltpu.VMEM((SEQ, PAIR), F32)] * (2 * ng),
        compiler_params=_cp(("parallel",)), name="attention_fwd")(*([proj] * (3 * ng)), cos_f, sin_s)


def _attention_bwd(g, proj, cos_f, sin_s, d_attn, attn, lse):
    dil = DILATIONS[g]
    two_blocks = SEQ // dil > BLOCK

    def body(q_ref, k_ref, v_ref, cf_ref, ss_ref, do_ref, o_ref, lse_ref, dq_out, dk_out, dv_out,
             qr_ref, kr_ref, dq_acc, dk_acc, dv_acc):
        _rotate([q_ref], [qr_ref], cf_ref, ss_ref, 1.0 / math.sqrt(HEAD_DIM))
        _rotate([k_ref], [kr_ref], cf_ref, ss_ref, 1.0)
        dk_acc[...] = jnp.zeros_like(dk_acc)
        dv_acc[...] = jnp.zeros_like(dv_acc)
        nk = 2 * BLOCK if two_blocks else BLOCK
        first = lax.broadcasted_iota(jnp.int32, (BLOCK, PAIR), 1) < HEAD_DIM
        first_k = lax.broadcasted_iota(jnp.int32, (nk, PAIR), 1) < HEAD_DIM

        def units(t, carry):
            picked = [_unit_pieces(t * UNIT_BATCH + j, dil) for j in range(UNIT_BATCH)]

            def tiles(ref, with_prev=False):
                if with_prev and two_blocks:
                    return jnp.stack([jnp.concatenate([_load_tile(ref, prev, dil), _load_tile(ref, rows, dil)], axis=0)
                                      for _, rows, prev in picked])
                return jnp.stack([_load_tile(ref, rows, dil) for _, rows, _ in picked])

            qq = tiles(qr_ref).astype(BF16)
            kk = tiles(kr_ref, True).astype(BF16)
            vv = tiles(v_ref, True).astype(BF16)
            dof = tiles(do_ref)
            dd = dof * tiles(o_ref)
            lse3 = tiles(lse_ref)
            dob = dof.astype(BF16)
            if two_blocks:
                valid = jnp.stack([_band_mask(i, dil) for i, _, _ in picked])
            else:
                valid = _causal_mask()[None]
            zq, zf = jnp.zeros_like(qq), jnp.zeros_like(dd)
            dqs, dks, dvs = [], [], []
            for head in range(2):
                mine = first[None] if head == 0 else jnp.logical_not(first)[None]
                delta = jnp.sum(jnp.where(mine, dd, zf), axis=-1, keepdims=True)
                lse_h = lse3[:, :, head * HEAD_DIM:head * HEAD_DIM + 1]
                s = jnp.einsum("pqd,pkd->pqk", jnp.where(mine, qq, zq), kk, preferred_element_type=F32)
                p = jnp.where(valid, jnp.exp(s - lse_h), 0.0)
                dp = jnp.einsum("pqd,pkd->pqk", jnp.where(mine, dob, zq), vv, preferred_element_type=F32)
                ds = (p * (dp - delta)).astype(BF16)
                dqs.append(jnp.einsum("pqk,pkd->pqd", ds, kk, preferred_element_type=F32))
                dks.append(jnp.einsum("pqk,pqd->pkd", ds, qq, preferred_element_type=F32))
                dvs.append(jnp.einsum("pqk,pqd->pkd", p.astype(BF16), dob, preferred_element_type=F32))
            dq = jnp.where(first[None], dqs[0], dqs[1])
            dk = jnp.where(first_k[None], dks[0], dks[1])
            dv = jnp.where(first_k[None], dvs[0], dvs[1])
            for j, (_, rows, prev) in enumerate(picked):
                _store_tile(dq_acc, rows, dil, dq[j])
                _store_tile(dk_acc, rows, dil, dk[j, nk - BLOCK:], accumulate=True)
                _store_tile(dv_acc, rows, dil, dv[j, nk - BLOCK:], accumulate=True)
                if two_blocks:
                    _store_tile(dk_acc, prev, dil, dk[j, :BLOCK], accumulate=True)
                    _store_tile(dv_acc, prev, dil, dv[j, :BLOCK], accumulate=True)
            return carry

        lax.fori_loop(0, UNITS // UNIT_BATCH, units, 0)

        def finish(t, carry):
            rows = pl.ds(pl.multiple_of(t * ROPE_ROWS, ROPE_ROWS), ROPE_ROWS)
            cf, ss = cf_ref[rows, :], ss_ref[rows, :]
            dq = dq_acc[rows, :] * (1.0 / math.sqrt(HEAD_DIM))
            dq_out[rows, :] = _rope_transposed(dq, cf, ss).astype(BF16)
            dk_out[rows, :] = _rope_transposed(dk_acc[rows, :], cf, ss).astype(BF16)
            dv_out[rows, :] = dv_acc[rows, :].astype(BF16)
            return carry

        lax.fori_loop(0, SEQ // ROPE_ROWS, finish, 0)

    whole = pl.BlockSpec((SEQ, PAIR), lambda hp: (0, 0))
    pair = pl.BlockSpec((SEQ, PAIR), lambda hp: (0, hp))
    views = [_pair_views(col0)[g] for col0 in (0, QKV_WIDTH, 2 * QKV_WIDTH)]
    return pl.pallas_call(
        body, grid=(ATTN_WIDTH // PAIR,), in_specs=views + [whole, whole, pair, pair, pair],
        out_specs=[pair, pair, pair], out_shape=[_sds((SEQ, ATTN_WIDTH), BF16)] * 3,
        scratch_shapes=[pltpu.VMEM((SEQ, PAIR), F32)] * 5,
        compiler_params=_cp(("parallel",)), name=f"attention_bwd_{g}")(proj, proj, proj, cos_f, sin_s, d_attn, attn, lse)


def _cmul(ar, ai, br, bi):
    return ar * br - ai * bi, ar * bi + ai * br


def _pow256(ar, ai):
    for _ in range(8):
        ar, ai = _cmul(ar, ai, ar, ai)
    return ar, ai


def _chunk_carries(first_r, first_i, pr, pi, reverse):
    rows = lax.broadcasted_iota(jnp.int32, first_r.shape, 0)
    out_r = jnp.zeros_like(first_r)
    out_i = jnp.zeros_like(first_i)
    hr = jnp.zeros_like(first_r[0:1])
    hi = jnp.zeros_like(hr)
    order = range(SCAN_CHUNKS - 1, -1, -1) if reverse else range(SCAN_CHUNKS)
    for c in order:
        out_r = jnp.where(rows == c, hr, out_r)
        out_i = jnp.where(rows == c, hi, out_i)
        tr, ti = _cmul(pr[0:1], pi[0:1], hr, hi)
        hr = first_r[c:c + 1] + tr
        hi = first_i[c:c + 1] + ti
    return out_r, out_i


def _tile(j):
    return pl.ds(pl.multiple_of(j * SCAN_CHUNKS, SCAN_CHUNKS), SCAN_CHUNKS)


def _to_scan_rows(t):
    per = SCAN_STEPS // PHASES
    return t.reshape(PHASES, SCAN_CHUNKS, per, t.shape[1]).transpose(2, 0, 1, 3).reshape(t.shape)


def _from_scan_rows(t):
    per = SCAN_STEPS // PHASES
    return t.reshape(per, PHASES, SCAN_CHUNKS, t.shape[1]).transpose(1, 2, 0, 3).reshape(t.shape)


def _scan_in_place(hr_ref, hi_ref, a_r, a_i):
    def local(j, carry):
        tr, ti = _cmul(a_r, a_i, carry[0], carry[1])
        nr = tr + hr_ref[_tile(j), :]
        ni = ti + hi_ref[_tile(j), :]
        hr_ref[_tile(j), :] = nr
        hi_ref[_tile(j), :] = ni
        return nr, ni

    zero = jnp.zeros_like(a_r)
    last_r, last_i = lax.fori_loop(0, SCAN_STEPS, local, (zero, zero), unroll=4)
    pr, pi = _pow256(a_r, a_i)
    er, ei = _chunk_carries(last_r, last_i, pr, pi, reverse=False)

    def fix(j, carry):
        tr, ti = _cmul(carry[0], carry[1], er, ei)
        hr_ref[_tile(j), :] += tr
        hi_ref[_tile(j), :] += ti
        return _cmul(carry[0], carry[1], a_r, a_i)

    lax.fori_loop(0, SCAN_STEPS, fix, (a_r, a_i), unroll=4)
    return er, ei


def _reverse_scan_in_place(lr_ref, li_ref, hr_ref, hi_ref, er, ei, a_r, a_i):
    def local(t, carry):
        j = SCAN_STEPS - 1 - t
        tr, ti = _cmul(a_r, a_i, carry[0], carry[1])
        nr = tr + lr_ref[_tile(j), :]
        ni = ti + li_ref[_tile(j), :]
        lr_ref[_tile(j), :] = nr
        li_ref[_tile(j), :] = ni
        return nr, ni

    zero = jnp.zeros_like(a_r)
    first_r, first_i = lax.fori_loop(0, SCAN_STEPS, local, (zero, zero), unroll=4)
    pr, pi = _pow256(a_r, a_i)
    nxt_r, nxt_i = _chunk_carries(first_r, first_i, pr, pi, reverse=True)

    def accumulate(lam_r, lam_i, hp_r, hp_i, acc):
        return (acc[0] + lam_r * hp_r + lam_i * hp_i, acc[1] + lam_i * hp_r - lam_r * hp_i)

    def fix(t, carry):
        qr, qi, acc_r, acc_i = carry
        j = SCAN_STEPS - 1 - t
        tr, ti = _cmul(qr, qi, nxt_r, nxt_i)
        lam_r = lr_ref[_tile(j), :] + tr
        lam_i = li_ref[_tile(j), :] + ti
        lr_ref[_tile(j), :] = lam_r
        li_ref[_tile(j), :] = lam_i
        acc_r, acc_i = accumulate(lam_r, lam_i, hr_ref[_tile(j - 1), :], hi_ref[_tile(j - 1), :], (acc_r, acc_i))
        qr, qi = _cmul(qr, qi, a_r, a_i)
        return qr, qi, acc_r, acc_i

    qr, qi, acc_r, acc_i = lax.fori_loop(0, SCAN_STEPS - 1, fix, (a_r, a_i, zero, zero), unroll=4)
    tr, ti = _cmul(qr, qi, nxt_r, nxt_i)
    lam_r = lr_ref[_tile(0), :] + tr
    lam_i = li_ref[_tile(0), :] + ti
    lr_ref[_tile(0), :] = lam_r
    li_ref[_tile(0), :] = lam_i
    acc_r, acc_i = accumulate(lam_r, lam_i, er, ei, (acc_r, acc_i))
    return jnp.sum(acc_r, axis=0, keepdims=True), jnp.sum(acc_i, axis=0, keepdims=True)


def _rope_tables():
    half = HEAD_DIM // 2
    inv_freq = ROPE_THETA ** (-jnp.arange(half, dtype=F32) / half)
    ang = jnp.arange(SEQ, dtype=F32)[:, None] * inv_freq[None, :]
    cos, sin = jnp.cos(ang), jnp.sin(ang)
    cos_f = jnp.concatenate([cos, cos, cos, cos], axis=1)
    sin_s = jnp.concatenate([-sin, sin, -sin, sin], axis=1)
    return cos_f, sin_s


def _ssm_discretise(a_re, a_im, log_dt, b_re, b_im):
    lam = lax.complex(a_re, a_im)
    dt = jnp.exp(log_dt)[:, None]
    a_bar = jnp.exp(lam * dt)
    b_bar = ((a_bar - 1.0) / lam)[..., None] * lax.complex(b_re, b_im)
    return a_bar.real, a_bar.imag, b_bar.real, b_bar.imag


SSM_SLABS = 4
SLAB_GROUPS = SSM_GROUPS // SSM_SLABS
SLAB_IN = SSM_WIDTH // SSM_SLABS
SLAB_STATE = SSM_LANES // SSM_SLABS


def _slab_block_diag(blocks):
    _, r, c = blocks.shape
    eye = jnp.eye(SLAB_GROUPS, dtype=blocks.dtype)
    b5 = blocks.reshape(SSM_SLABS, SLAB_GROUPS, r, 1, c) * eye[None, :, None, :, None]
    return b5.reshape(SSM_SLABS, SLAB_GROUPS * r, SLAB_GROUPS * c)


def _diag_blocks(a, b):
    ra, cb = a.shape[1], b.shape[1]
    wa, wb = ra // SLAB_GROUPS, cb // SLAB_GROUPS
    d = lax.dot_general(a, b, (((0,), (0,)), ((), ())), preferred_element_type=F32)
    row_g = jnp.right_shift(lax.broadcasted_iota(jnp.int32, (ra, cb), 0), wa.bit_length() - 1)
    col_g = jnp.right_shift(lax.broadcasted_iota(jnp.int32, (ra, cb), 1), wb.bit_length() - 1)
    d = jnp.where(row_g == col_g, d, 0.0)
    fold = (jnp.bitwise_and(lax.broadcasted_iota(jnp.int32, (cb, wb), 0), wb - 1)
            == lax.broadcasted_iota(jnp.int32, (cb, wb), 1)).astype(F32)
    return jnp.dot(d, fold, preferred_element_type=F32, precision=lax.Precision.HIGHEST)


def _slab_specs():
    tok = pl.BlockSpec((SEQ, SLAB_IN), lambda j: (0, j))
    state = pl.BlockSpec((SEQ, SLAB_STATE), lambda j: (0, j))
    b_in = pl.BlockSpec((None, SLAB_IN, SLAB_STATE), lambda j: (j, 0, 0))
    c_out = pl.BlockSpec((None, SLAB_STATE, SLAB_IN), lambda j: (j, 0, 0))
    vec = pl.BlockSpec((1, SLAB_STATE), lambda j: (0, j))
    ent = pl.BlockSpec((SCAN_CHUNKS, SLAB_STATE), lambda j: (0, j))
    return tok, state, b_in, c_out, vec, ent


def _ssm_forward(u, b_in_r, b_in_i, c_out_r, c_out_ni, a_r, a_i):
    def body(u_ref, br_ref, bi_ref, cr_ref, ci_ref, ar_ref, ai_ref, y_ref, hr_ref, hi_ref, er_ref, ei_ref):
        uu = u_ref[...]
        hr_ref[...] = jnp.dot(uu, br_ref[...], preferred_element_type=F32)
        hi_ref[...] = jnp.dot(uu, bi_ref[...], preferred_element_type=F32)
        a_re = jnp.broadcast_to(ar_ref[...], (SCAN_CHUNKS, SLAB_STATE))
        a_im = jnp.broadcast_to(ai_ref[...], (SCAN_CHUNKS, SLAB_STATE))
        er_ref[...], ei_ref[...] = _scan_in_place(hr_ref, hi_ref, a_re, a_im)
        y_ref[...] = (jnp.dot(hr_ref[...].astype(BF16), cr_ref[...], preferred_element_type=F32)
                      + jnp.dot(hi_ref[...].astype(BF16), ci_ref[...], preferred_element_type=F32))

    tok, state, b_in, c_out, vec, ent = _slab_specs()
    return pl.pallas_call(
        body, grid=(SSM_SLABS,), in_specs=[tok, b_in, b_in, c_out, c_out, vec, vec],
        out_specs=[tok, state, state, ent, ent],
        out_shape=[_sds((SEQ, SSM_WIDTH)), _sds((SEQ, SSM_LANES)), _sds((SEQ, SSM_LANES)),
                   _sds((SCAN_CHUNKS, SSM_LANES)), _sds((SCAN_CHUNKS, SSM_LANES))],
        compiler_params=_cp(("parallel",)), name="ssm_forward")(u, b_in_r, b_in_i, c_out_r, c_out_ni, a_r, a_i)


def _ssm_backward(d_y, d_u_skip, u, h_r, h_i, e_r, e_i, b_in_r, b_in_i, c_out_r, c_out_ni, a_r, a_i):
    def body(dy_ref, skip_ref, u_ref, hr_ref, hi_ref, er_ref, ei_ref, br_ref, bi_ref, cr_ref, ci_ref, ar_ref, ai_ref,
             du_ref, dar_ref, dai_ref, dcr_ref, dci_ref, dbr_ref, dbi_ref, lr_ref, li_ref):
        dy = dy_ref[...]
        lr_ref[...] = _dot_nt(dy, cr_ref[...])
        li_ref[...] = _dot_nt(dy, ci_ref[...])
        a_re = jnp.broadcast_to(ar_ref[...], (SCAN_CHUNKS, SLAB_STATE))
        a_im = -jnp.broadcast_to(ai_ref[...], (SCAN_CHUNKS, SLAB_STATE))
        dar_ref[...], dai_ref[...] = _reverse_scan_in_place(lr_ref, li_ref, hr_ref, hi_ref, er_ref[...], ei_ref[...],
                                                            a_re, a_im)
        dcr_ref[...] = _diag_blocks(hr_ref[...].astype(BF16), dy)
        dci_ref[...] = _diag_blocks(hi_ref[...].astype(BF16), dy)
        lam_r, lam_i = lr_ref[...].astype(BF16), li_ref[...].astype(BF16)
        uu = u_ref[...]
        dbr_ref[...] = _diag_blocks(uu, lam_r)
        dbi_ref[...] = _diag_blocks(uu, lam_i)
        du = skip_ref[...] + _dot_nt(lam_r, br_ref[...]) + _dot_nt(lam_i, bi_ref[...])
        du_ref[...] = du.astype(BF16)

    tok, state, b_in, c_out, vec, ent = _slab_specs()
    dc = pl.BlockSpec((SLAB_STATE, SSM_GROUP), lambda j: (j, 0))
    db = pl.BlockSpec((SLAB_IN, SSM_STATE), lambda j: (j, 0))
    return pl.pallas_call(
        body, grid=(SSM_SLABS,), in_specs=[tok, tok, tok, state, state, ent, ent, b_in, b_in, c_out, c_out, vec, vec],
        out_specs=[tok, vec, vec, dc, dc, db, db],
        out_shape=[_sds((SEQ, SSM_WIDTH), BF16), _sds((1, SSM_LANES)), _sds((1, SSM_LANES)),
                   _sds((SSM_LANES, SSM_GROUP)), _sds((SSM_LANES, SSM_GROUP)),
                   _sds((SSM_WIDTH, SSM_STATE)), _sds((SSM_WIDTH, SSM_STATE))],
        scratch_shapes=[pltpu.VMEM((SEQ, SLAB_STATE), F32)] * 2,
        compiler_params=_cp(("parallel",)), name="ssm_backward")(
            d_y, d_u_skip, u, h_r, h_i, e_r, e_i, b_in_r, b_in_i, c_out_r, c_out_ni, a_r, a_i)


FF_ROWS = 1024
FF_SHARD = D_FF // N_CHIPS


def _dot_nt(a, b):
    return lax.dot_general(a, b, (((1,), (1,)), ((), ())), preferred_element_type=F32)


def _ffn_up(h, w_gate_t, w_up_t):
    def body(h_ref, wg_ref, wu_ref, a_ref, b_ref, act_ref):
        hb = h_ref[...].astype(BF16)
        a = _dot_nt(hb, wg_ref[...])
        b = _dot_nt(hb, wu_ref[...])
        a_ref[...] = a
        b_ref[...] = b
        act_ref[...] = (a * jax.nn.sigmoid(a) * b).astype(BF16)

    w_spec = pl.BlockSpec((None, FF_SHARD, D_MODEL), lambda i, k: (k, 0, 0))
    o_spec = pl.BlockSpec((None, FF_ROWS, FF_SHARD), lambda i, k: (k, i, 0))
    shape = (N_CHIPS, SEQ, FF_SHARD)
    return pl.pallas_call(
        body, grid=(SEQ // FF_ROWS, N_CHIPS),
        in_specs=[pl.BlockSpec((FF_ROWS, D_MODEL), lambda i, k: (i, 0)), w_spec, w_spec],
        out_specs=[o_spec, o_spec, o_spec], out_shape=[_sds(shape), _sds(shape), _sds(shape, BF16)],
        compiler_params=_cp(("parallel", "parallel")), name="ffn_up")(h, w_gate_t, w_up_t)


def _ffn_down_bwd(dz, w_down, a, b):
    def body(dz_ref, wd_ref, a_ref, b_ref, da_ref, db_ref):
        d_act = _dot_nt(dz_ref[...].astype(BF16), wd_ref[...])
        av = a_ref[...]
        sg = jax.nn.sigmoid(av)
        da_ref[...] = (d_act * b_ref[...] * sg * (1.0 + av * (1.0 - sg))).astype(BF16)
        db_ref[...] = (d_act * av * sg).astype(BF16)

    t_spec = pl.BlockSpec((None, FF_ROWS, FF_SHARD), lambda i, k: (k, i, 0))
    shape = (N_CHIPS, SEQ, FF_SHARD)
    return pl.pallas_call(
        body, grid=(SEQ // FF_ROWS, N_CHIPS),
        in_specs=[pl.BlockSpec((FF_ROWS, D_MODEL), lambda i, k: (i, 0)),
                  pl.BlockSpec((None, FF_SHARD, D_MODEL), lambda i, k: (k, 0, 0)), t_spec, t_spec],
        out_specs=[t_spec, t_spec], out_shape=[_sds(shape, BF16), _sds(shape, BF16)],
        compiler_params=_cp(("parallel", "parallel")), name="ffn_down_bwd")(dz, w_down, a, b)


def _ffn_dh(d_a, d_b, w_gate_t, w_up_t):
    def body(da_ref, db_ref, wg_ref, wu_ref, o_ref, acc):
        k = pl.program_id(1)
        part = (jnp.dot(da_ref[...], wg_ref[...], preferred_element_type=F32)
                + jnp.dot(db_ref[...], wu_ref[...], preferred_element_type=F32))

        @pl.when(k == 0)
        def _():
            acc[...] = part

        @pl.when(k > 0)
        def _():
            acc[...] += part

        @pl.when(k == N_CHIPS - 1)
        def _():
            o_ref[...] = acc[...]

    t_spec = pl.BlockSpec((None, FF_ROWS, FF_SHARD), lambda i, k: (k, i, 0))
    w_spec = pl.BlockSpec((None, FF_SHARD, D_MODEL), lambda i, k: (k, 0, 0))
    return pl.pallas_call(
        body, grid=(SEQ // FF_ROWS, N_CHIPS), in_specs=[t_spec, t_spec, w_spec, w_spec],
        out_specs=pl.BlockSpec((FF_ROWS, D_MODEL), lambda i, k: (i, 0)), out_shape=_sds((SEQ, D_MODEL)),
        scratch_shapes=[pltpu.VMEM((FF_ROWS, D_MODEL), F32)],
        compiler_params=_cp(("parallel", "arbitrary")), name="ffn_dh")(d_a, d_b, w_gate_t, w_up_t)


def _local_step(x, tgt, wts, small):
    s = SEQ
    cos_f, sin_s = [_to_phase_rows(t) for t in _rope_tables()]
    x = _reorder_rows(x, to_phase=True, name="phase_rows_x")
    tgt = _reorder_rows(tgt, to_phase=True, name="phase_rows_target")

    proj = _mm_cols(x, wts["w_in_0"], tm=1024, name="proj_0", part=(W_IN_PIECES, 0))
    for p in range(1, W_IN_PIECES):
        proj = _mm_cols(x, wts[f"w_in_{p}"], tm=1024, name=f"proj_{p}", part=(W_IN_PIECES, p), into=proj)

    attn, lse = _attention_fwd(proj, cos_f, sin_s)
    y_attn = _mm_cols(attn, wts["w_attn_br"], tm=s, name="y_attn")

    (abar_r, abar_i, bbar_r, bbar_i), ssm_vjp = jax.vjp(
        _ssm_discretise, small["ssm_a_re"], small["ssm_a_im"], small["ssm_log_dt"], small["ssm_b_re"], small["ssm_b_im"])
    b_in_r, b_in_i = [_slab_block_diag(b.transpose(0, 2, 1)).astype(BF16) for b in (bbar_r, bbar_i)]
    c_out_r = _slab_block_diag(small["ssm_c_re"].transpose(0, 2, 1)).astype(BF16)
    c_out_ni = _slab_block_diag(-small["ssm_c_im"].transpose(0, 2, 1)).astype(BF16)
    a_r, a_i = abar_r.reshape(1, SSM_LANES), abar_i.reshape(1, SSM_LANES)
    d_skip = small["ssm_d"].reshape(1, SSM_WIDTH)

    u_f = _to_scan_rows(proj[:, 3 * QKV_WIDTH:3 * QKV_WIDTH + SSM_WIDTH])
    u_p = u_f.astype(BF16)
    y_c, h_r, h_i, e_r, e_i = _ssm_forward(u_p, b_in_r, b_in_i, c_out_r, c_out_ni, a_r, a_i)

    def gelu_fwd(yc, u, dsk):
        y = yc + dsk * u
        return y, 0.5 * y * (1.0 + jnp.tanh(GELU_C * (y + GELU_K * y * y * y)))

    y_s5, gel = _rowwise(gelu_fwd, [y_c, u_f], [d_skip], [_sds((s, SSM_WIDTH)), _sds((s, SSM_WIDTH), BF16)],
                         tm=512, name="ssm_gelu")
    glu = _mm_cols(gel, wts["w_glu"], tm=s, name="glu")

    def glu_fwd(ga, gb):
        return ga * jax.nn.sigmoid(gb)

    (y_glu,) = _rowwise(glu_fwd, [(glu, SSM_WIDTH, 0), (glu, SSM_WIDTH, 1)], [], [_sds((s, SSM_WIDTH), BF16)],
                        tm=512, name="glu_gate")
    y_glu = _from_scan_rows(y_glu)
    y_ssm = _mm_cols(y_glu, wts["w_ssm_br"], tm=s, name="y_ssm")

    gl0 = (proj, D_MODEL, (3 * QKV_WIDTH + SSM_WIDTH) // D_MODEL)
    gl1 = (proj, D_MODEL, (3 * QKV_WIDTH + SSM_WIDTH) // D_MODEL + 1)
    b_gate = small["b_gate"]

    def gate_mix(l0, l1, ya, ys, bg):
        return jax.nn.sigmoid(l0 + bg[0:1]) * ya + jax.nn.sigmoid(l1 + bg[1:2]) * ys

    (mixed,) = _rowwise(gate_mix, [gl0, gl1, y_attn, y_ssm], [b_gate], [_sds((s, D_MODEL), BF16)], tm=256,
                        name="gate_mix")
    w_out = wts["w_out"].reshape(D_MODEL, D_MODEL)
    mix_out = _mm_plain(mixed, w_out, tm=1024, tn=512, name="mix_out")

    def ln1_fwd(xv, mo, g, b):
        z = DN_ALPHA * xv + mo
        xhat, _ = _ln_stats(z)
        return z, xhat * g + b

    z1, h = _rowwise(ln1_fwd, [x, mix_out], [small["ln1_g"], small["ln1_b"]],
                     [_sds((s, D_MODEL)), _sds((s, D_MODEL))], tm=256, name="ln1")

    nf = D_FF // N_CHIPS
    w_gate_t, w_up_t, w_down = wts["w_ff_gate"], wts["w_ff_up"], wts["w_ff_down"]
    ff_a, ff_b, act = _ffn_up(h, w_gate_t, w_up_t)
    ff = _matmul(act, w_down, grid=(2, N_CHIPS),
                 a_spec=pl.BlockSpec((None, 1024, nf), lambda i, k: (k, i, 0)),
                 b_spec=pl.BlockSpec((None, nf, D_MODEL), lambda i, k: (k, 0, 0)),
                 o_spec=pl.BlockSpec((1024, D_MODEL), lambda i, k: (i, 0)),
                 out_shape=_sds((s, D_MODEL)), dims=(1, 0), k_axis=1, name="ff_down")

    def ln2_loss(hv, ffv, tg, g, b):
        z = DN_ALPHA * hv + ffv
        xhat, rstd = _ln_stats(z)
        err = xhat * g + b - tg
        d_out = err * (1.0 / D_MODEL)
        loss_rows = jnp.sum(err * err, axis=-1, keepdims=True) * (0.5 / D_MODEL)
        loss = jnp.broadcast_to(jnp.sum(loss_rows, axis=0, keepdims=True), (1, 128))
        return _ln_bwd(d_out, xhat, rstd, g), loss, _colsum(d_out * xhat), _colsum(d_out)

    dz2, loss_v, d_ln2_g, d_ln2_b = _rowwise(
        ln2_loss, [h, ff, tgt], [small["ln2_g"], small["ln2_b"]], [_sds((s, D_MODEL))],
        [_sds((1, 128)), _sds((1, D_MODEL)), _sds((1, D_MODEL))], tm=256, name="ln2_loss")

    d_a, d_b = _ffn_down_bwd(dz2, w_down, ff_a, ff_b)

    def grad_rows(lhs, rhs, name):
        return _matmul(lhs, rhs, grid=(N_CHIPS,), a_spec=pl.BlockSpec((None, s, nf), lambda k: (k, 0, 0)),
                       b_spec=pl.BlockSpec((s, D_MODEL), lambda k: (0, 0)),
                       o_spec=pl.BlockSpec((None, nf, D_MODEL), lambda k: (k, 0, 0)),
                       out_shape=_sds((N_CHIPS, nf, D_MODEL), BF16), dims=(0, 0), name=name)

    g_w_ff_down = grad_rows(act, dz2, "g_w_ff_down")
    g_w_ff_gate = grad_rows(d_a, h, "g_w_ff_gate")
    g_w_ff_up = grad_rows(d_b, h, "g_w_ff_up")
    dh_ff = _ffn_dh(d_a, d_b, w_gate_t, w_up_t)

    def ln1_bwd(dz, dff, z, g):
        xhat, rstd = _ln_stats(z)
        dh = DN_ALPHA * dz + dff
        return _ln_bwd(dh, xhat, rstd, g), _colsum(dh * xhat), _colsum(dh)

    dz1, d_ln1_g, d_ln1_b = _rowwise(ln1_bwd, [dz2, dh_ff, z1], [small["ln1_g"]], [_sds((s, D_MODEL))],
                                     [_sds((1, D_MODEL)), _sds((1, D_MODEL))], tm=256, name="ln1_bwd")
    d_mixed = _mm_plain(dz1, w_out, tm=1024, tn=512, dims=(1, 1), name="d_mixed")
    g_w_out = _mm_plain(mixed, dz1, tm=D_MODEL, tn=512, dims=(0, 0), out_dtype=BF16, name="g_w_out")
    g_w_out = g_w_out.reshape(N_CHIPS, D_MODEL // N_CHIPS, D_MODEL)

    def gate_bwd(dm, l0, l1, ya, ys, bg):
        g0 = jax.nn.sigmoid(l0 + bg[0:1])
        g1 = jax.nn.sigmoid(l1 + bg[1:2])
        dl0 = dm * ya * g0 * (1.0 - g0)
        dl1 = dm * ys * g1 * (1.0 - g1)
        return dm * g0, dm * g1, jnp.concatenate([dl0, dl1], axis=1), _colsum(dl0), _colsum(dl1)

    d_y_attn, d_y_ssm, d_gl, d_bg0, d_bg1 = _rowwise(
        gate_bwd, [d_mixed, gl0, gl1, y_attn, y_ssm], [b_gate],
        [_sds((s, D_MODEL), BF16), _sds((s, D_MODEL), BF16), _sds((s, 2 * D_MODEL), BF16)],
        [_sds((1, D_MODEL)), _sds((1, D_MODEL))], tm=256, name="gate_bwd")

    g_w_ssm_br = _mm_cols_tn(y_glu, d_y_ssm, ns=D_MODEL // N_CHIPS, name="g_w_ssm_br")
    d_y_glu = _to_scan_rows(_mm_cols_nt(d_y_ssm, wts["w_ssm_br"], tm=s, name="d_y_glu"))

    def glu_bwd(dy, ga, gb):
        sg = jax.nn.sigmoid(gb)
        return jnp.concatenate([dy * sg, dy * ga * sg * (1.0 - sg)], axis=1)

    (d_glu,) = _rowwise(glu_bwd, [d_y_glu, (glu, SSM_WIDTH, 0), (glu, SSM_WIDTH, 1)], [],
                        [_sds((s, 2 * SSM_WIDTH), BF16)], tm=512, name="glu_bwd")
    g_w_glu = _mm_cols_tn(gel, d_glu, ns=2 * SSM_WIDTH // N_CHIPS, name="g_w_glu")
    d_gel = _mm_cols_nt(d_glu, wts["w_glu"], tm=s, name="d_gel")

    def gelu_bwd(dg, y, u, dsk):
        th = jnp.tanh(GELU_C * (y + GELU_K * y * y * y))
        dy = dg * (0.5 * (1.0 + th) + 0.5 * y * (1.0 - th * th) * GELU_C * (1.0 + 3.0 * GELU_K * y * y))
        return dy, dy * dsk, _colsum(dy * u)

    d_y, d_u_skip, d_ssm_d = _rowwise(gelu_bwd, [d_gel, y_s5, u_f], [d_skip],
                                      [_sds((s, SSM_WIDTH), BF16), _sds((s, SSM_WIDTH))], [_sds((1, SSM_WIDTH))],
                                      tm=512, name="gelu_bwd")
    d_u, d_abar_r, d_abar_i, d_c_r, d_c_ni, d_bin_r, d_bin_i = _ssm_backward(
        d_y, d_u_skip, u_p, h_r, h_i, e_r, e_i, b_in_r, b_in_i, c_out_r, c_out_ni, a_r, a_i)
    d_u = _from_scan_rows(d_u)
    d_bbar_r = d_bin_r.reshape(SSM_GROUPS, SSM_GROUP, SSM_STATE).transpose(0, 2, 1)
    d_bbar_i = d_bin_i.reshape(SSM_GROUPS, SSM_GROUP, SSM_STATE).transpose(0, 2, 1)
    d_a_re, d_a_im, d_log_dt, d_b_re, d_b_im = ssm_vjp(
        (d_abar_r.reshape(SSM_GROUPS, SSM_STATE), d_abar_i.reshape(SSM_GROUPS, SSM_STATE), d_bbar_r, d_bbar_i))
    d_c_re = d_c_r.reshape(SSM_GROUPS, SSM_STATE, SSM_GROUP).transpose(0, 2, 1)
    d_c_im = -d_c_ni.reshape(SSM_GROUPS, SSM_STATE, SSM_GROUP).transpose(0, 2, 1)

    g_w_attn_br = _mm_cols_tn(attn, d_y_attn, ns=D_MODEL // N_CHIPS, name="g_w_attn_br")
    d_attn = _mm_cols_nt(d_y_attn, wts["w_attn_br"], tm=s, name="d_attn")
    dqkv = [_attention_bwd(g, proj, cos_f, sin_s, d_attn, attn, lse) for g in range(len(DILATIONS))]

    d_proj = jnp.concatenate([dqkv[g][j] for j in range(3) for g in range(len(DILATIONS))] + [d_u, d_gl],
                             axis=1)
    dx_proj = [_mm_cols_nt(d_proj, wts[f"w_in_{p}"], tm=1024, name=f"dx_proj_{p}", part=(W_IN_PIECES, p))
               for p in range(W_IN_PIECES)]
    g_w_in = _mm_cols_tn(x, d_proj, ns=IN_WIDTH // N_CHIPS, name="g_w_in", after=tuple(dx_proj))

    def dx_sum(dz, *dxp):
        return DN_ALPHA * dz + sum(dxp)

    (grad_x,) = _rowwise(dx_sum, [dz1] + dx_proj, [], [_sds((s, D_MODEL))], tm=512, name="grad_x", after=(g_w_in,))
    grad_x = _reorder_rows(grad_x, to_phase=False, name="time_rows_grad_x")

    big = {"w_in": g_w_in, "w_attn_br": g_w_attn_br, "w_ssm_br": g_w_ssm_br, "w_out": g_w_out, "w_glu": g_w_glu,
           "w_ff_gate": g_w_ff_gate, "w_ff_up": g_w_ff_up, "w_ff_down": g_w_ff_down}
    small_g = {"b_gate": jnp.concatenate([d_bg0, d_bg1], axis=0), "ssm_a_re": d_a_re, "ssm_a_im": d_a_im,
               "ssm_log_dt": d_log_dt, "ssm_b_re": d_b_re, "ssm_b_im": d_b_im, "ssm_c_re": d_c_re, "ssm_c_im": d_c_im,
               "ssm_d": d_ssm_d.reshape(SSM_WIDTH), "ln1_g": d_ln1_g, "ln1_b": d_ln1_b, "ln2_g": d_ln2_g,
               "ln2_b": d_ln2_b}
    marks = {"ln1_bwd": dz1, "scan_bwd": d_abar_r, "attention_bwd_0": dqkv[0][0]}
    return loss_v[0, 0], grad_x, big, small_g, marks


GATHER_ID, SWAP_ID, SCATTER_ID, JOIN_ID, EXCHANGE_ID = 1, 2, 3, 4, 5


def _place():
    return lax.axis_index("x"), lax.axis_index("y"), lax.axis_index("c")


def _other_chips(x, y):
    return [(1 - x, y), (x, 1 - y), (1 - x, 1 - y)]


def _handshake(peers):
    barrier = pltpu.get_barrier_semaphore()
    for peer in peers:
        pl.semaphore_signal(barrier, inc=1, device_id=peer, device_id_type=MESH)
    pl.semaphore_wait(barrier, len(peers))


def _sequencer(body, arrays, out_type, sems, collective_id, name):
    return pl.kernel(body, name=name, out_type=out_type,
                     mesh=plsc.ScalarSubcoreMesh(axis_name="sequencer", num_cores=1), scratch_types=sems,
                     compiler_params=pltpu.CompilerParams(collective_id=collective_id))(*arrays)


def _gather_weights(shards, *, name):
    nw = len(shards)

    def body(*refs):
        ins, outs = refs[:nw], refs[nw:2 * nw]
        send_sems, recv_sems, pass_send, pass_recv, local_sems = refs[2 * nw:]
        x, y, c = _place()
        chip = 2 * x + y
        chips = _other_chips(x, y)
        _handshake([(x, y, 1 - c)] + [(cx, cy, c) for cx, cy in chips])
        started = []
        for w in range(nw):
            hw = shards[w].shape[0] // 2
            mine = pl.ds(c * hw, hw)
            own = pltpu.make_async_copy(ins[w], outs[w].at[chip], local_sems.at[w])
            own.start()
            started.append(own)
            for j, (cx, cy) in enumerate(chips):
                cp = pltpu.make_async_remote_copy(
                    src_ref=ins[w].at[mine], dst_ref=outs[w].at[chip, mine], send_sem=send_sems.at[w, j],
                    recv_sem=recv_sems.at[w, j], device_id=(cx, cy, c), device_id_type=MESH)
                cp.start()
                started.append(cp)
        passed = []
        for w in range(nw):
            hw = shards[w].shape[0] // 2
            mine = pl.ds(c * hw, hw)
            for j, (cx, cy) in enumerate(chips):
                landed = outs[w].at[2 * cx + cy, mine]
                pltpu.make_async_remote_copy(
                    src_ref=ins[w].at[mine], dst_ref=landed, send_sem=send_sems.at[w, j],
                    recv_sem=recv_sems.at[w, j], device_id=(cx, cy, c), device_id_type=MESH).wait_recv()
                cp = pltpu.make_async_remote_copy(
                    src_ref=landed, dst_ref=landed, send_sem=pass_send.at[w, j], recv_sem=pass_recv.at[w, j],
                    device_id=(x, y, 1 - c), device_id_type=MESH)
                cp.start()
                passed.append(cp)
        for w in range(nw):
            hw = shards[w].shape[0] // 2
            theirs = pl.ds((1 - c) * hw, hw)
            for j, (cx, cy) in enumerate(chips):
                landed = outs[w].at[2 * cx + cy, theirs]
                pltpu.make_async_remote_copy(
                    src_ref=landed, dst_ref=landed, send_sem=pass_send.at[w, j], recv_sem=pass_recv.at[w, j],
                    device_id=(x, y, 1 - c), device_id_type=MESH).wait_recv()
        for cp in started[0::4]:
            cp.wait()
        for cp in [s for i, s in enumerate(started) if i % 4] + passed:
            cp.wait_send()

    sem = pltpu.SemaphoreType.DMA
    return _sequencer(body, shards, [_sds((N_CHIPS,) + a.shape, a.dtype) for a in shards],
                      [sem((nw, 3)), sem((nw, 3)), sem((nw, 3)), sem((nw, 3)), sem((nw,))], GATHER_ID, name)


def _swap_other_halves(grads, *, name):
    nw = len(grads)

    def body(*refs):
        ins, outs = refs[:nw], refs[nw:2 * nw]
        send_sems, recv_sems = refs[2 * nw:]
        x, y, c = _place()
        _handshake([(x, y, 1 - c)])
        cps = []
        for w in range(nw):
            hw = grads[w].shape[1] // 2
            cp = pltpu.make_async_remote_copy(
                src_ref=ins[w].at[:, pl.ds((1 - c) * hw, hw)], dst_ref=outs[w], send_sem=send_sems.at[w],
                recv_sem=recv_sems.at[w], device_id=(x, y, 1 - c), device_id_type=MESH)
            cp.start()
            cps.append(cp)
        for cp in cps:
            cp.wait()

    sem = pltpu.SemaphoreType.DMA
    return _sequencer(body, grads, [_sds((N_CHIPS, g.shape[1] // 2, g.shape[2]), g.dtype) for g in grads],
                      [sem((nw,)), sem((nw,))], SWAP_ID, name)


def _add_my_halves(core, grads, others, *, name, after=()):
    nw = len(grads)
    halves = [g.shape[1] // 2 for g in grads]

    def body(core_ref, *refs):
        outs = refs[2 * nw + len(after):]
        for g_ref, o_ref, out_ref in zip(refs[:nw], refs[nw:2 * nw], outs):
            out_ref[...] = (g_ref[...].astype(F32) + o_ref[...].astype(F32)).astype(out_ref.dtype)

    in_specs = [pl.BlockSpec((None, None, hw, g.shape[2]), lambda s, core_ref: (s, core_ref[0], 0, 0))
                for g, hw in zip(grads, halves)]
    in_specs += [pl.BlockSpec((None, hw, g.shape[2]), lambda s, core_ref: (s, 0, 0)) for g, hw in zip(grads, halves)]
    return pl.pallas_call(
        body,
        grid_spec=pltpu.PrefetchScalarGridSpec(
            num_scalar_prefetch=1, grid=(N_CHIPS,), in_specs=in_specs + [HBM_OPERAND] * len(after),
            out_specs=[pl.BlockSpec((None, hw, g.shape[2]), lambda s, core_ref: (s, 0, 0))
                       for g, hw in zip(grads, halves)]),
        out_shape=[_sds((N_CHIPS, hw, g.shape[2]), BF16) for g, hw in zip(grads, halves)],
        compiler_params=_cp(("parallel",)), name=name)(
            core, *[g.reshape(N_CHIPS, 2, hw, g.shape[2]) for g, hw in zip(grads, halves)], *others, *after)


def _scatter_partials(parts, *, name):
    nw = len(parts)

    def body(*refs):
        ins, outs = refs[:nw], refs[nw:2 * nw]
        send_sems, recv_sems = refs[2 * nw:]
        x, y, c = _place()
        _handshake([(cx, cy, c) for cx, cy in _other_chips(x, y)])
        cps = []
        for w in range(nw):
            for j, (cx, cy) in enumerate(_other_chips(x, y)):
                cp = pltpu.make_async_remote_copy(
                    src_ref=ins[w].at[2 * cx + cy], dst_ref=outs[w].at[j], send_sem=send_sems.at[w, j],
                    recv_sem=recv_sems.at[w, j], device_id=(cx, cy, c), device_id_type=MESH)
                cp.start()
                cps.append(cp)
        for cp in cps:
            cp.wait()

    sem = pltpu.SemaphoreType.DMA
    return _sequencer(body, parts, [_sds((3,) + p.shape[1:], p.dtype) for p in parts],
                      [sem((nw, 3)), sem((nw, 3))], SCATTER_ID, name)


SUM_STEPS = 2


def _sum_partials(chip, parts, recvd, *, name, after=()):
    nw = len(parts)
    rows = [p.shape[1] // SUM_STEPS for p in parts]

    def body(chip_ref, *refs):
        outs = refs[2 * nw + len(after):]
        for p_ref, r_ref, out_ref in zip(refs[:nw], refs[nw:2 * nw], outs):
            acc = p_ref[...].astype(F32)
            for j in range(3):
                acc = acc + r_ref[j].astype(F32)
            out_ref[...] = acc

    in_specs = [pl.BlockSpec((None, th, p.shape[2]), lambda i, chip_ref: (chip_ref[0], i, 0))
                for p, th in zip(parts, rows)]
    in_specs += [pl.BlockSpec((3, th, p.shape[2]), lambda i, chip_ref: (0, i, 0)) for p, th in zip(parts, rows)]
    return pl.pallas_call(
        body,
        grid_spec=pltpu.PrefetchScalarGridSpec(
            num_scalar_prefetch=1, grid=(SUM_STEPS,), in_specs=in_specs + [HBM_OPERAND] * len(after),
            out_specs=[pl.BlockSpec((th, p.shape[2]), lambda i, chip_ref: (i, 0)) for p, th in zip(parts, rows)]),
        out_shape=[_sds(p.shape[1:]) for p in parts], compiler_params=_cp(("parallel",)), name=name)(
            chip, *parts, *recvd, *after)


def _swap_reduced_halves(halves, *, name):
    nw = len(halves)

    def body(*refs):
        ins, outs = refs[:nw], refs[nw:2 * nw]
        send_sems, recv_sems = refs[2 * nw:]
        x, y, c = _place()
        _handshake([(x, y, 1 - c)])
        cps = []
        for w in range(nw):
            cp = pltpu.make_async_remote_copy(
                src_ref=ins[w], dst_ref=outs[w], send_sem=send_sems.at[w], recv_sem=recv_sems.at[w],
                device_id=(x, y, 1 - c), device_id_type=MESH)
            cp.start()
            cps.append(cp)
        for cp in cps:
            cp.wait()

    sem = pltpu.SemaphoreType.DMA
    return _sequencer(body, halves, [_sds(h.shape, h.dtype) for h in halves], [sem((nw,)), sem((nw,))], JOIN_ID, name)


def _exchange_rows(vec, *, name):
    def body(v_ref, slots, send_sems, recv_sems, local_sem):
        x, y, c = _place()
        me = 4 * x + 2 * y + c
        peers = []
        for mask in range(1, N_DEV):
            peers.append((1 - x if mask & 4 else x, 1 - y if mask & 2 else y, 1 - c if mask & 1 else c))
        _handshake(peers)
        own = pltpu.make_async_copy(v_ref, slots.at[me], local_sem)
        own.start()
        cps = []
        for k, peer in enumerate(peers):
            cp = pltpu.make_async_remote_copy(
                src_ref=v_ref, dst_ref=slots.at[me], send_sem=send_sems.at[k], recv_sem=recv_sems.at[k],
                device_id=peer, device_id_type=MESH)
            cp.start()
            cps.append(cp)
        for k, (px, py, pc) in enumerate(peers):
            pltpu.make_async_remote_copy(
                src_ref=v_ref, dst_ref=slots.at[4 * px + 2 * py + pc], send_sem=send_sems.at[k],
                recv_sem=recv_sems.at[k], device_id=(px, py, pc), device_id_type=MESH).wait_recv()
        for cp in cps:
            cp.wait_send()
        own.wait()

    sem = pltpu.SemaphoreType.DMA
    return _sequencer(body, [vec], [_sds((N_DEV,) + vec.shape)], [sem((N_DEV - 1,)), sem((N_DEV - 1,)), sem(())],
                      EXCHANGE_ID, name)[0]


def _sum_slots(slots, *, name, after=()):
    def body(s_ref, *rest):
        out_ref = rest[len(after)]
        acc = s_ref[0]
        for d in range(1, N_DEV):
            acc = acc + s_ref[d]
        out_ref[...] = acc

    vmem = pl.BlockSpec(memory_space=pltpu.VMEM)
    return pl.pallas_call(
        body, in_specs=[vmem] + [HBM_OPERAND] * len(after), out_specs=vmem, out_shape=_sds(slots.shape[1:]),
        compiler_params=pltpu.CompilerParams(vmem_limit_bytes=VMEM_LIMIT_BYTES), name=name)(slots, *after)


def _reduce_scatter_start(grads, core, *, tag, add_after=()):
    others = _swap_other_halves(grads, name="swap_other_halves_" + tag)
    parts = _add_my_halves(core, grads, others, name="add_my_halves_" + tag, after=add_after)
    return parts, _scatter_partials(parts, name="scatter_partials_" + tag)


def _reduce_scatter_finish(parts, recvd, chip, *, tag, sum_after=()):
    mine = _sum_partials(chip, parts, recvd, name="sum_partials_" + tag, after=sum_after)
    return mine, _swap_reduced_halves(mine, name="swap_reduced_halves_" + tag)


ADAM_BLOCK_ELEMS = 256 * 1024


def _adam_rows(rows, cols):
    tm = rows
    while tm * cols > ADAM_BLOCK_ELEMS and tm % 16 == 0:
        tm //= 2
    return tm


def _adam_step(wv, gv, mv, vv):
    m2 = ADAM_B1 * mv + (1.0 - ADAM_B1) * gv
    v2 = ADAM_B2 * vv + (1.0 - ADAM_B2) * (gv * gv)
    m_hat = m2 / (1.0 - ADAM_B1 ** ADAM_STEP)
    v_hat = v2 / (1.0 - ADAM_B2 ** ADAM_STEP)
    return -ADAM_LR * (m_hat / (jnp.sqrt(v_hat) + ADAM_EPS) + ADAM_WD * wv), m2, v2


def _adamw(w, g, m, v, *, name):
    rows, cols = w.shape
    return _rowwise(_adam_step, [w, g, m, v], [], [_sds((rows, cols))] * 3, tm=_adam_rows(rows, cols), name=name)


def _adamw_halves(core, w, g_mine, g_theirs, m, v, *, name, after=()):
    rows, cols = w.shape
    hw = rows // 2
    tm = _adam_rows(hw, cols)
    per_half = hw // tm

    def body(core_ref, w_ref, gm_ref, gt_ref, m_ref, v_ref, *rest):
        g_out, d_out, m_out, v_out = rest[len(after):]
        mine = (pl.program_id(0) // per_half) == core_ref[0]
        g = jnp.where(mine, gm_ref[...], gt_ref[...])
        d, m2, v2 = _adam_step(w_ref[...], g, m_ref[...], v_ref[...])
        g_out[...] = g
        d_out[...] = d
        m_out[...] = m2
        v_out[...] = v2

    full = pl.BlockSpec((tm, cols), lambda i, core_ref: (i, 0))
    half = pl.BlockSpec((tm, cols), lambda i, core_ref: (i % per_half, 0))
    return pl.pallas_call(
        body,
        grid_spec=pltpu.PrefetchScalarGridSpec(
            num_scalar_prefetch=1, grid=(rows // tm,),
            in_specs=[full, half, half, full, full] + [HBM_OPERAND] * len(after), out_specs=[full, full, full, full]),
        out_shape=[_sds((rows, cols))] * 4, compiler_params=_cp(("parallel",)), name=name)(
            core, w, g_mine, g_theirs, m, v, *after)


HELD_TRANSPOSED = ("w_ff_gate", "w_ff_up")


def _as_rows(name, arr):
    return arr[0].T if name in HELD_TRANSPOSED else arr[0]


def _from_rows(name, arr2d):
    return (arr2d.T if name in HELD_TRANSPOSED else arr2d)[None]


STORED_SWAPPED = ("ssm_b_re", "ssm_b_im")


def _as_stored(name, arr):
    return jnp.swapaxes(arr, -1, -2) if name in STORED_SWAPPED else arr


def _pack_rows(arrs):
    flat = jnp.concatenate([a.reshape(-1).astype(F32) for a in arrs])
    rows = -(-flat.shape[0] // 1024) * 8
    return jnp.pad(flat, (0, rows * 128 - flat.shape[0])).reshape(rows, 128)


def _unpack_rows(vec, shapes):
    flat = vec.reshape(-1)
    out, off = [], 0
    for shp in shapes:
        size = math.prod(shp)
        out.append(flat[off:off + size].reshape(shp))
        off += size
    return out


SMALL = ("b_gate", "ssm_a_re", "ssm_a_im", "ssm_log_dt", "ssm_b_re", "ssm_b_im", "ssm_c_re", "ssm_c_im", "ssm_d",
         "ln1_g", "ln1_b", "ln2_g", "ln2_b")
W_IN_PIECES = 2
GATHER_GROUPS = (("mixer", ("w_attn_br", "w_ssm_br", "w_glu", "w_out")), ("ffn", ("w_ff_gate", "w_ff_up", "w_ff_down")))
REDUCE_GROUPS = (("ffn", ("w_ff_down", "w_ff_gate", "w_ff_up")),
                 ("mixer", ("w_out", "w_ssm_br", "w_glu", "w_attn_br")), ("w_in", ("w_in",)))
WEIGHTS = ("w_in", "b_gate", "w_attn_br", "w_ssm_br", "w_out", "ssm_a_re", "ssm_a_im", "ssm_log_dt", "ssm_b_re",
           "ssm_b_im", "ssm_c_re", "ssm_c_im", "ssm_d", "w_glu", "ln1_g", "ln1_b", "w_ff_gate", "w_ff_up", "w_ff_down",
           "ln2_g", "ln2_b")


def kernel(x, w_in, b_gate, w_attn_br, w_ssm_br, w_out, ssm_a_re, ssm_a_im, ssm_log_dt, ssm_b_re, ssm_b_im, ssm_c_re, ssm_c_im, ssm_d, w_glu, ln1_g, ln1_b, w_ff_gate, w_ff_up, w_ff_down, ln2_g, ln2_b, loss_target, m_w_in, m_b_gate, m_w_attn_br, m_w_ssm_br, m_w_out, m_ssm_a_re, m_ssm_a_im, m_ssm_log_dt, m_ssm_b_re, m_ssm_b_im, m_ssm_c_re, m_ssm_c_im, m_ssm_d, m_w_glu, m_ln1_g, m_ln1_b, m_w_ff_gate, m_w_ff_up, m_w_ff_down, m_ln2_g, m_ln2_b, v_w_in, v_b_gate, v_w_attn_br, v_w_ssm_br, v_w_out, v_ssm_a_re, v_ssm_a_im, v_ssm_log_dt, v_ssm_b_re, v_ssm_b_im, v_ssm_c_re, v_ssm_c_im, v_ssm_d, v_w_glu, v_ln1_g, v_ln1_b, v_w_ff_gate, v_w_ff_up, v_w_ff_down, v_ln2_g, v_ln2_b):
    given = dict(locals())
    px, py, pc = _place()
    chip = 2 * px + py
    core_s = jnp.reshape(pc, (1,)).astype(jnp.int32)
    chip_s = jnp.reshape(chip, (1,)).astype(jnp.int32)

    wts = {}
    piece = IN_WIDTH // N_CHIPS // W_IN_PIECES
    for p in range(W_IN_PIECES):
        cols = w_in[0][:, p * piece:(p + 1) * piece].astype(BF16)
        (wts[f"w_in_{p}"],) = _gather_weights([cols], name=f"gather_w_in_{p}")
    for tag, names in GATHER_GROUPS:
        wts.update(zip(names, _gather_weights([_as_rows(n, given[n]).astype(BF16) for n in names],
                                              name="gather_" + tag)))
    ncol = D_MODEL // N_CHIPS
    bg_mine = jnp.where(pc == 0, b_gate[0], jnp.zeros_like(b_gate[0]))
    bg_full = lax.dynamic_update_slice(jnp.zeros((2, D_MODEL), F32), bg_mine, (0, chip * ncol))
    bg_slots = _exchange_rows(bg_full.reshape(16, 128), name="exchange_gate_bias")
    bg_full = _sum_slots(bg_slots, name="sum_gate_bias").reshape(2, D_MODEL)
    small = {n: given[n][0] for n in SMALL if n.startswith("ssm")}
    small.update({n: given[n] for n in ("ln1_g", "ln1_b", "ln2_g", "ln2_b")})
    small["b_gate"] = bg_full

    loss_mine, grad_x, big_g, small_g, marks = _local_step(x[0], loss_target[0], wts, small)
    loss = lax.psum(loss_mine, ("x", "y", "c"))

    groups = dict(REDUCE_GROUPS)
    add_after = {"ffn": (marks["ln1_bwd"],), "mixer": (marks["scan_bwd"],), "w_in": (grad_x,)}
    parts, recvd = {}, {}
    for tag, names in REDUCE_GROUPS:
        parts[tag], recvd[tag] = _reduce_scatter_start([big_g[n] for n in names], core_s, tag=tag,
                                                       add_after=add_after[tag])
    grads, delta, new_m, new_v = {}, {}, {}, {}

    def finish(tag, sum_after, adam_after):
        mine, theirs = _reduce_scatter_finish(parts[tag], recvd[tag], chip_s, tag=tag, sum_after=sum_after)
        for n, g_mine, g_theirs in zip(groups[tag], mine, theirs):
            res = _adamw_halves(core_s, _as_rows(n, given[n]), g_mine, g_theirs, _as_rows(n, given["m_" + n]),
                                _as_rows(n, given["v_" + n]), name="adamw_" + n, after=adam_after)
            grads[n], delta[n], new_m[n], new_v[n] = [_from_rows(n, r) for r in res]

    in_flight = (parts["w_in"][0],)
    finish("ffn", (marks["scan_bwd"],), in_flight)
    finish("mixer", (marks["attention_bwd_0"],), in_flight)
    stored = [_as_stored(n, small_g[n]) for n in SMALL]
    slots = _exchange_rows(_pack_rows(stored), name="exchange_small")
    summed = _unpack_rows(_sum_slots(slots, name="sum_small", after=in_flight), [a.shape for a in stored])
    for n, g in zip(SMALL, summed):
        g = _as_stored(n, g)
        if n == "b_gate":
            g = lax.dynamic_slice(g, (0, chip * ncol), (2, ncol))
        grads[n] = g.reshape(given[n].shape)
    packed = [_pack_rows([_as_stored(n, src[n]) for n in SMALL]) for src in
              (given, grads, {n: given["m_" + n] for n in SMALL}, {n: given["v_" + n] for n in SMALL})]
    shapes = [_as_stored(n, given[n]).shape for n in SMALL]
    small_out = _adamw(*packed, name="adamw_small")
    for out, vec in zip((delta, new_m, new_v), small_out):
        out.update((n, _as_stored(n, a)) for n, a in zip(SMALL, _unpack_rows(vec, shapes)))
    behind = [delta[n] for tag in ("ffn", "mixer") for n in groups[tag]] + [small_out[0]]
    finish("w_in", tuple(behind), ())

    return (loss, grad_x.reshape(x.shape), *[grads[n] for n in WEIGHTS], *[delta[n] for n in WEIGHTS],
            *[new_m[n] for n in WEIGHTS], *[new_v[n] for n in WEIGHTS])
```

```python
import math

import jax
import jax.numpy as jnp
from jax import lax
from jax.experimental import pallas as pl
from jax.experimental.pallas import tpu as pltpu
from jax.experimental.pallas import tpu_sc as plsc

F32 = jnp.float32
BF16 = jnp.bfloat16
MESH = pl.DeviceIdType.MESH

D_MODEL = 1024
SEQ = 2048
HEAD_DIM = 64
ATTN_HEADS = 8
DILATIONS = (1, 4, 16)
ATTN_WIDTH = ATTN_HEADS * HEAD_DIM
QKV_WIDTH = 3 * ATTN_WIDTH
BLOCK = 128
ROPE_THETA = 10000.0
NEG_INF = -1e30
SSM_GROUP = 16
SSM_GROUPS = 32
SSM_WIDTH = 512
SSM_STATE = 64
SSM_LANES = SSM_GROUPS * SSM_STATE
SCAN_CHUNKS = 8
SCAN_STEPS = SEQ // SCAN_CHUNKS
IN_WIDTH = 3 * QKV_WIDTH + SSM_WIDTH + 2 * D_MODEL
D_FF = 2816
N_CHIPS = 4
N_DEV = 8
DN_ALPHA = 2.0 ** 0.25
LN_EPS = 1e-5
ADAM_LR = 0.001
ADAM_B1 = 0.9
ADAM_B2 = 0.999
ADAM_EPS = 1e-08
ADAM_WD = 0.01
ADAM_STEP = 10
GELU_C = math.sqrt(2.0 / math.pi)
GELU_K = 0.044715

VMEM_LIMIT_BYTES = 56 * 1024 * 1024


def _sds(shape, dtype=F32):
    return jax.ShapeDtypeStruct(tuple(shape), dtype)


def _cp(semantics=None):
    return pltpu.CompilerParams(dimension_semantics=semantics, vmem_limit_bytes=VMEM_LIMIT_BYTES)


HBM_OPERAND = pl.BlockSpec(memory_space=pl.ANY)


def _matmul(a, b, *, grid, a_spec, b_spec, o_spec, out_shape, dims, k_axis=None, name, after=()):
    nk = grid[k_axis] if k_axis is not None else 1
    o_block = tuple(d for d in o_spec.block_shape if d is not None)
    n_after = len(after)

    def body(a_ref, b_ref, *rest):
        o_ref, acc = rest[n_after], rest[n_after + 1:]
        part = lax.dot_general(a_ref[...].astype(BF16), b_ref[...].astype(BF16),
                               (((dims[0],), (dims[1],)), ((), ())), preferred_element_type=F32)
        if k_axis is None:
            o_ref[...] = part.astype(o_ref.dtype)
        else:
            k = pl.program_id(k_axis)

            @pl.when(k == 0)
            def _():
                acc[0][...] = part

            @pl.when(k > 0)
            def _():
                acc[0][...] += part

            @pl.when(k == nk - 1)
            def _():
                o_ref[...] = acc[0][...].astype(o_ref.dtype)

    sem = tuple("arbitrary" if ax == k_axis else "parallel" for ax in range(len(grid)))
    return pl.pallas_call(
        body, grid=grid, in_specs=[a_spec, b_spec] + [HBM_OPERAND] * n_after, out_specs=o_spec, out_shape=out_shape,
        scratch_shapes=[pltpu.VMEM(o_block, F32)] if k_axis is not None else [],
        compiler_params=_cp(sem), name=name)(a, b, *after)


def _mm_cols(a, wg, *, tm, name, out_dtype=F32):
    m, k = a.shape
    ns = wg.shape[2]
    return _matmul(a, wg, grid=(m // tm, N_CHIPS),
                   a_spec=pl.BlockSpec((tm, k), lambda i, s: (i, 0)),
                   b_spec=pl.BlockSpec((None, k, ns), lambda i, s: (s, 0, 0)),
                   o_spec=pl.BlockSpec((tm, ns), lambda i, s: (i, s)),
                   out_shape=_sds((m, N_CHIPS * ns), out_dtype), dims=(1, 0), name=name)


def _mm_cols_nt(dy, wg, *, tm, name, out_dtype=F32, after=()):
    k, ns = wg.shape[1], wg.shape[2]
    m = dy.shape[0]
    a_spec = pl.BlockSpec((tm, ns), lambda i, s: (i, s))
    return _matmul(dy, wg, grid=(m // tm, N_CHIPS), a_spec=a_spec,
                   b_spec=pl.BlockSpec((None, k, ns), lambda i, s: (s, 0, 0)),
                   o_spec=pl.BlockSpec((tm, k), lambda i, s: (i, 0)),
                   out_shape=_sds((m, k), out_dtype), dims=(1, 1), k_axis=1, name=name, after=after)


def _mm_cols_tn(a, dy, *, ns, name, after=()):
    m, k = a.shape
    return _matmul(a, dy, grid=(N_CHIPS,), a_spec=pl.BlockSpec((m, k), lambda s: (0, 0)),
                   b_spec=pl.BlockSpec((m, ns), lambda s: (0, s)),
                   o_spec=pl.BlockSpec((None, k, ns), lambda s: (s, 0, 0)),
                   out_shape=_sds((N_CHIPS, k, ns), BF16), dims=(0, 0), name=name, after=after)


def _mm_plain(a, b, *, tm, tn, name, out_dtype=F32, dims=(1, 0), tk=None):
    m = a.shape[1 - dims[0]]
    kk = a.shape[dims[0]]
    n = b.shape[1 - dims[1]]
    tk = kk if tk is None else tk
    nk = kk // tk

    def a_idx(i, j, k):
        return (i, k) if dims[0] == 1 else (k, i)

    def b_idx(i, j, k):
        return (k, j) if dims[1] == 0 else (j, k)

    a_blk = (tm, tk) if dims[0] == 1 else (tk, tm)
    b_blk = (tk, tn) if dims[1] == 0 else (tn, tk)
    return _matmul(a, b, grid=(m // tm, n // tn, nk),
                   a_spec=pl.BlockSpec(a_blk, a_idx), b_spec=pl.BlockSpec(b_blk, b_idx),
                   o_spec=pl.BlockSpec((tm, tn), lambda i, j, k: (i, j)),
                   out_shape=_sds((m, n), out_dtype), dims=dims, k_axis=2 if nk > 1 else None, name=name)


def _rowwise(fn, tiled, full, outs, accs=(), *, tm, name, after=()):
    args, in_specs = [], []
    for t in tiled:
        if isinstance(t, tuple):
            arr, w, cb = t
            in_specs.append(pl.BlockSpec((tm, w), lambda i, cb=cb: (i, cb)))
        else:
            arr = t
            in_specs.append(pl.BlockSpec((tm, arr.shape[1]), lambda i: (i, 0)))
        args.append(arr)
    rows = args[0].shape[0]
    for f in full:
        in_specs.append(pl.BlockSpec(f.shape, lambda i, nd=f.ndim: (0,) * nd))
        args.append(f)
    out_specs = [pl.BlockSpec((tm, o.shape[1]), lambda i: (i, 0)) for o in outs]
    out_specs += [pl.BlockSpec(a.shape, lambda i, nd=len(a.shape): (0,) * nd) for a in accs]
    n_in, n_out = len(args), len(outs)
    in_specs += [HBM_OPERAND] * len(after)
    first_out = n_in + len(after)

    def body(*refs):
        res = fn(*[r[...] for r in refs[:n_in]])
        res = res if isinstance(res, (tuple, list)) else (res,)
        for r, v in zip(refs[first_out:first_out + n_out], res[:n_out]):
            r[...] = v.astype(r.dtype)
        i = pl.program_id(0)
        for r, v in zip(refs[first_out + n_out:], res[n_out:]):
            @pl.when(i == 0)
            def _(r=r, v=v):
                r[...] = v

            @pl.when(i > 0)
            def _(r=r, v=v):
                r[...] += v

    res = pl.pallas_call(
        body, grid=(rows // tm,), in_specs=in_specs, out_specs=out_specs, out_shape=list(outs) + list(accs),
        compiler_params=_cp(("arbitrary",) if accs else ("parallel",)), name=name)(*args, *after)
    return res


def _colsum(v):
    return jnp.sum(v, axis=0, keepdims=True)


def _ln_stats(z):
    mu = jnp.mean(z, axis=-1, keepdims=True)
    zc = z - mu
    var = jnp.mean(zc * zc, axis=-1, keepdims=True)
    rstd = lax.rsqrt(var + LN_EPS)
    return zc * rstd, rstd


def _ln_bwd(dy, xhat, rstd, g):
    dxh = dy * g
    m1 = jnp.mean(dxh, axis=-1, keepdims=True)
    m2 = jnp.mean(dxh * xhat, axis=-1, keepdims=True)
    return rstd * (dxh - m1 - xhat * m2)


def _swap_halves(t):
    w = t.shape[-1]
    lane = lax.broadcasted_iota(jnp.int32, t.shape, t.ndim - 1)
    return jnp.where((lane % HEAD_DIM) < HEAD_DIM // 2, pltpu.roll(t, w - HEAD_DIM // 2, t.ndim - 1),
                     pltpu.roll(t, HEAD_DIM // 2, t.ndim - 1))


PHASES = max(DILATIONS)
PAIR = 2 * HEAD_DIM
UNITS = SEQ // BLOCK
UNIT_BATCH = 4
ROPE_ROWS = 256


def _to_phase_rows(t):
    return t.reshape(SEQ // PHASES, PHASES, t.shape[1]).transpose(1, 0, 2).reshape(t.shape)


def _reorder_rows(arr, *, to_phase, name):
    def body(i_ref, o_ref):
        for rho in range(PHASES):
            phase = pl.ds(rho * BLOCK, BLOCK)
            strided = pl.ds(rho, BLOCK, stride=PHASES)
            if to_phase:
                o_ref[phase, :] = i_ref[strided, :]
            else:
                o_ref[strided, :] = i_ref[phase, :]

    spec = pl.BlockSpec((SEQ, BLOCK), lambda j: (0, j))
    return pl.pallas_call(body, grid=(arr.shape[1] // BLOCK,), in_specs=[spec], out_specs=spec,
                          out_shape=_sds(arr.shape), compiler_params=_cp(("parallel",)), name=name)(arr)


def _rope(t, cf, ss):
    return t * cf + _swap_halves(t) * ss


def _rope_transposed(d, cf, ss):
    return d * cf + _swap_halves(d * ss)


def _unit_pieces(u, dil):
    pieces, length = PHASES // dil, 8 * dil
    if dil == 1:
        rho, i = 0, u
    elif dil == PHASES:
        rho, i = u, 0
    else:
        rho, i = jnp.bitwise_and(u, dil - 1), jnp.right_shift(u, dil.bit_length() - 1)
    before = jnp.maximum(i - 1, 0)
    cur = [pl.multiple_of((rho + dil * k) * BLOCK + length * i, 8) for k in range(pieces)]
    prev = [pl.multiple_of((rho + dil * k) * BLOCK + length * before, 8) for k in range(pieces)]
    return i, cur, prev


def _load_tile(ref, starts, dil):
    return jnp.concatenate([ref[pl.ds(st, 8 * dil), :] for st in starts], axis=0)


def _store_tile(ref, starts, dil, val, head=None, accumulate=False):
    length = 8 * dil
    lanes = slice(None) if head is None else pl.ds(head * HEAD_DIM, HEAD_DIM)
    cols = slice(None) if head is None else slice(head * HEAD_DIM, (head + 1) * HEAD_DIM)
    for k, st in enumerate(starts):
        piece = val[k * length:(k + 1) * length, cols]
        if accumulate:
            ref[pl.ds(st, length), lanes] += piece
        else:
            ref[pl.ds(st, length), lanes] = piece


def _tile_position(idx, dil):
    pieces, length = PHASES // dil, 8 * dil
    return pieces * jnp.bitwise_and(idx, length - 1) + jnp.right_shift(idx, length.bit_length() - 1)


def _band_mask(i, dil):
    row = lax.broadcasted_iota(jnp.int32, (BLOCK, 2 * BLOCK), 0)
    col = lax.broadcasted_iota(jnp.int32, (BLOCK, 2 * BLOCK), 1)
    key_pos = _tile_position(jnp.bitwise_and(col, BLOCK - 1), dil) + jnp.where(col >= BLOCK, 0, -BLOCK)
    dist = _tile_position(row, dil) - key_pos
    return (dist >= 0) & (dist <= BLOCK) & ((col >= BLOCK) | (i > 0))


def _causal_mask():
    row = lax.broadcasted_iota(jnp.int32, (BLOCK, BLOCK), 0)
    col = lax.broadcasted_iota(jnp.int32, (BLOCK, BLOCK), 1)
    return row >= col


def _pair_views(col0):
    return [pl.BlockSpec((SEQ, PAIR), lambda hp, g=g: (0, col0 // PAIR + g * (ATTN_WIDTH // PAIR) + hp))
            for g in range(len(DILATIONS))]


def _rotate(in_refs, out_refs, cf_ref, ss_ref, scale):
    def step(t, carry):
        rows = pl.ds(pl.multiple_of(t * ROPE_ROWS, ROPE_ROWS), ROPE_ROWS)
        cf, ss = cf_ref[rows, :] * scale, ss_ref[rows, :] * scale
        for i_ref, o_ref in zip(in_refs, out_refs):
            o_ref[rows, :] = _rope(i_ref[rows, :], cf, ss)
        return carry

    lax.fori_loop(0, SEQ // ROPE_ROWS, step, 0)


def _attention_fwd(proj, cos_f, sin_s):
    ng = len(DILATIONS)

    def body(*refs):
        q_refs, k_refs, v_refs = refs[:ng], refs[ng:2 * ng], refs[2 * ng:3 * ng]
        cf_ref, ss_ref, attn_ref, lse_ref = refs[3 * ng:3 * ng + 4]
        scratch = refs[3 * ng + 4:]
        qr_refs, kr_refs = scratch[:ng], scratch[ng:]
        _rotate(q_refs, qr_refs, cf_ref, ss_ref, 1.0 / math.sqrt(HEAD_DIM))
        _rotate(k_refs, kr_refs, cf_ref, ss_ref, 1.0)
        first = lax.broadcasted_iota(jnp.int32, (BLOCK, PAIR), 1) < HEAD_DIM
        for g, dil in enumerate(DILATIONS):
            two_blocks = SEQ // dil > BLOCK

            def units(t, carry, g=g, dil=dil, two_blocks=two_blocks):
                picked = [_unit_pieces(t * UNIT_BATCH + j, dil) for j in range(UNIT_BATCH)]

                def tiles(ref, with_prev=False):
                    if with_prev and two_blocks:
                        return jnp.stack([jnp.concatenate([_load_tile(ref, prev, dil), _load_tile(ref, rows, dil)],
                                                          axis=0) for _, rows, prev in picked])
                    return jnp.stack([_load_tile(ref, rows, dil) for _, rows, _ in picked])

                qq = tiles(qr_refs[g]).astype(BF16)
                kk = tiles(kr_refs[g], True).astype(BF16)
                vv = tiles(v_refs[g], True).astype(BF16)
                if two_blocks:
                    valid = jnp.stack([_band_mask(i, dil) for i, _, _ in picked])
                else:
                    valid = _causal_mask()[None]
                mine = first[None]
                zero = jnp.zeros_like(qq)
                outs, lses = [], []
                for qh in (jnp.where(mine, qq, zero), jnp.where(mine, zero, qq)):
                    s = jnp.einsum("pqd,pkd->pqk", qh, kk, preferred_element_type=F32)
                    s = jnp.where(valid, s, NEG_INF)
                    m = jnp.max(s, axis=-1, keepdims=True)
                    p = jnp.exp(s - m)
                    l = jnp.sum(p, axis=-1, keepdims=True)
                    outs.append(jnp.einsum("pqk,pkd->pqd", p.astype(BF16), vv, preferred_element_type=F32) * (1.0 / l))
                    lses.append(m + jnp.log(l))
                o = jnp.where(mine, outs[0], outs[1])
                lse = jnp.where(mine, lses[0], lses[1])
                if g > 0:
                    lse_old = tiles(lse_ref)
                    m = jnp.maximum(lse_old, lse)
                    lse_new = m + jnp.log(jnp.exp(lse_old - m) + jnp.exp(lse - m))
                    o = tiles(attn_ref) * jnp.exp(lse_old - lse_new) + o * jnp.exp(lse - lse_new)
                    lse = lse_new
                for j, (_, rows, _) in enumerate(picked):
                    _store_tile(attn_ref, rows, dil, o[j])
                    _store_tile(lse_ref, rows, dil, lse[j])
                return carry

            lax.fori_loop(0, UNITS // UNIT_BATCH, units, 0)

    whole = pl.BlockSpec((SEQ, PAIR), lambda hp: (0, 0))
    out = pl.BlockSpec((SEQ, PAIR), lambda hp: (0, hp))
    return pl.pallas_call(
        body, grid=(ATTN_WIDTH // PAIR,),
        in_specs=_pair_views(0) + _pair_views(QKV_WIDTH) + _pair_views(2 * QKV_WIDTH) + [whole, whole],
        out_specs=[out, out], out_shape=[_sds((SEQ, ATTN_WIDTH)), _sds((SEQ, ATTN_WIDTH))],
        scratch_shapes=[pltpu.VMEM((SEQ, PAIR), F32)] * (2 * ng),
        compiler_params=_cp(("parallel",)), name="attention_fwd")(*([proj] * (3 * ng)), cos_f, sin_s)


def _attention_bwd(g, proj, cos_f, sin_s, d_attn, attn, lse):
    dil = DILATIONS[g]
    two_blocks = SEQ // dil > BLOCK

    def body(q_ref, k_ref, v_ref, cf_ref, ss_ref, do_ref, o_ref, lse_ref, dq_out, dk_out, dv_out,
             qr_ref, kr_ref, dq_acc, dk_acc, dv_acc):
        _rotate([q_ref], [qr_ref], cf_ref, ss_ref, 1.0 / math.sqrt(HEAD_DIM))
        _rotate([k_ref], [kr_ref], cf_ref, ss_ref, 1.0)
        dk_acc[...] = jnp.zeros_like(dk_acc)
        dv_acc[...] = jnp.zeros_like(dv_acc)
        nk = 2 * BLOCK if two_blocks else BLOCK
        first = lax.broadcasted_iota(jnp.int32, (BLOCK, PAIR), 1) < HEAD_DIM
        first_k = lax.broadcasted_iota(jnp.int32, (nk, PAIR), 1) < HEAD_DIM

        def units(t, carry):
            picked = [_unit_pieces(t * UNIT_BATCH + j, dil) for j in range(UNIT_BATCH)]

            def tiles(ref, with_prev=False):
                if with_prev and two_blocks:
                    return jnp.stack([jnp.concatenate([_load_tile(ref, prev, dil), _load_tile(ref, rows, dil)], axis=0)
                                      for _, rows, prev in picked])
                return jnp.stack([_load_tile(ref, rows, dil) for _, rows, _ in picked])

            qq = tiles(qr_ref).astype(BF16)
            kk = tiles(kr_ref, True).astype(BF16)
            vv = tiles(v_ref, True).astype(BF16)
            dof = tiles(do_ref)
            dd = dof * tiles(o_ref)
            lse3 = tiles(lse_ref)
            dob = dof.astype(BF16)
            if two_blocks:
                valid = jnp.stack([_band_mask(i, dil) for i, _, _ in picked])
            else:
                valid = _causal_mask()[None]
            zq, zf = jnp.zeros_like(qq), jnp.zeros_like(dd)
            dqs, dks, dvs = [], [], []
            for head in range(2):
                mine = first[None] if head == 0 else jnp.logical_not(first)[None]
                delta = jnp.sum(jnp.where(mine, dd, zf), axis=-1, keepdims=True)
                lse_h = lse3[:, :, head * HEAD_DIM:head * HEAD_DIM + 1]
                s = jnp.einsum("pqd,pkd->pqk", jnp.where(mine, qq, zq), kk, preferred_element_type=F32)
                p = jnp.where(valid, jnp.exp(s - lse_h), 0.0)
                dp = jnp.einsum("pqd,pkd->pqk", jnp.where(mine, dob, zq), vv, preferred_element_type=F32)
                ds = (p * (dp - delta)).astype(BF16)
                dqs.append(jnp.einsum("pqk,pkd->pqd", ds, kk, preferred_element_type=F32))
                dks.append(jnp.einsum("pqk,pqd->pkd", ds, qq, preferred_element_type=F32))
                dvs.append(jnp.einsum("pqk,pqd->pkd", p.astype(BF16), dob, preferred_element_type=F32))
            dq = jnp.where(first[None], dqs[0], dqs[1])
            dk = jnp.where(first_k[None], dks[0], dks[1])
            dv = jnp.where(first_k[None], dvs[0], dvs[1])
            for j, (_, rows, prev) in enumerate(picked):
                _store_tile(dq_acc, rows, dil, dq[j])
                _store_tile(dk_acc, rows, dil, dk[j, nk - BLOCK:], accumulate=True)
                _store_tile(dv_acc, rows, dil, dv[j, nk - BLOCK:], accumulate=True)
                if two_blocks:
                    _store_tile(dk_acc, prev, dil, dk[j, :BLOCK], accumulate=True)
                    _store_tile(dv_acc, prev, dil, dv[j, :BLOCK], accumulate=True)
            return carry

        lax.fori_loop(0, UNITS // UNIT_BATCH, units, 0)

        def finish(t, carry):
            rows = pl.ds(pl.multiple_of(t * ROPE_ROWS, ROPE_ROWS), ROPE_ROWS)
            cf, ss = cf_ref[rows, :], ss_ref[rows, :]
            dq = dq_acc[rows, :] * (1.0 / math.sqrt(HEAD_DIM))
            dq_out[rows, :] = _rope_transposed(dq, cf, ss).astype(BF16)
            dk_out[rows, :] = _rope_transposed(dk_acc[rows, :], cf, ss).astype(BF16)
            dv_out[rows, :] = dv_acc[rows, :].astype(BF16)
            return carry

        lax.fori_loop(0, SEQ // ROPE_ROWS, finish, 0)

    whole = pl.BlockSpec((SEQ, PAIR), lambda hp: (0, 0))
    pair = pl.BlockSpec((SEQ, PAIR), lambda hp: (0, hp))
    views = [_pair_views(col0)[g] for col0 in (0, QKV_WIDTH, 2 * QKV_WIDTH)]
    return pl.pallas_call(
        body, grid=(ATTN_WIDTH // PAIR,), in_specs=views + [whole, whole, pair, pair, pair],
        out_specs=[pair, pair, pair], out_shape=[_sds((SEQ, ATTN_WIDTH), BF16)] * 3,
        scratch_shapes=[pltpu.VMEM((SEQ, PAIR), F32)] * 5,
        compiler_params=_cp(("parallel",)), name=f"attention_bwd_{g}")(proj, proj, proj, cos_f, sin_s, d_attn, attn, lse)


def _cmul(ar, ai, br, bi):
    return ar * br - ai * bi, ar * bi + ai * br


def _pow256(ar, ai):
    for _ in range(8):
        ar, ai = _cmul(ar, ai, ar, ai)
    return ar, ai


def _chunk_carries(first_r, first_i, pr, pi, reverse):
    rows = lax.broadcasted_iota(jnp.int32, first_r.shape, 0)
    out_r = jnp.zeros_like(first_r)
    out_i = jnp.zeros_like(first_i)
    hr = jnp.zeros_like(first_r[0:1])
    hi = jnp.zeros_like(hr)
    order = range(SCAN_CHUNKS - 1, -1, -1) if reverse else range(SCAN_CHUNKS)
    for c in order:
        out_r = jnp.where(rows == c, hr, out_r)
        out_i = jnp.where(rows == c, hi, out_i)
        tr, ti = _cmul(pr[0:1], pi[0:1], hr, hi)
        hr = first_r[c:c + 1] + tr
        hi = first_i[c:c + 1] + ti
    return out_r, out_i


def _tile(j):
    return pl.ds(pl.multiple_of(j * SCAN_CHUNKS, SCAN_CHUNKS), SCAN_CHUNKS)


def _to_scan_rows(t):
    per = SCAN_STEPS // PHASES
    return t.reshape(PHASES, SCAN_CHUNKS, per, t.shape[1]).transpose(2, 0, 1, 3).reshape(t.shape)


def _from_scan_rows(t):
    per = SCAN_STEPS // PHASES
    return t.reshape(per, PHASES, SCAN_CHUNKS, t.shape[1]).transpose(1, 2, 0, 3).reshape(t.shape)


def _scan_in_place(hr_ref, hi_ref, a_r, a_i):
    def local(j, carry):
        tr, ti = _cmul(a_r, a_i, carry[0], carry[1])
        nr = tr + hr_ref[_tile(j), :]
        ni = ti + hi_ref[_tile(j), :]
        hr_ref[_tile(j), :] = nr
        hi_ref[_tile(j), :] = ni
        return nr, ni

    zero = jnp.zeros_like(a_r)
    last_r, last_i = lax.fori_loop(0, SCAN_STEPS, local, (zero, zero), unroll=4)
    pr, pi = _pow256(a_r, a_i)
    er, ei = _chunk_carries(last_r, last_i, pr, pi, reverse=False)

    def fix(j, carry):
        tr, ti = _cmul(carry[0], carry[1], er, ei)
        hr_ref[_tile(j), :] += tr
        hi_ref[_tile(j), :] += ti
        return _cmul(carry[0], carry[1], a_r, a_i)

    lax.fori_loop(0, SCAN_STEPS, fix, (a_r, a_i), unroll=4)
    return er, ei


def _reverse_scan_in_place(lr_ref, li_ref, hr_ref, hi_ref, er, ei, a_r, a_i):
    def local(t, carry):
        j = SCAN_STEPS - 1 - t
        tr, ti = _cmul(a_r, a_i, carry[0], carry[1])
        nr = tr + lr_ref[_tile(j), :]
        ni = ti + li_ref[_tile(j), :]
        lr_ref[_tile(j), :] = nr
        li_ref[_tile(j), :] = ni
        return nr, ni

    zero = jnp.zeros_like(a_r)
    first_r, first_i = lax.fori_loop(0, SCAN_STEPS, local, (zero, zero), unroll=4)
    pr, pi = _pow256(a_r, a_i)
    nxt_r, nxt_i = _chunk_carries(first_r, first_i, pr, pi, reverse=True)

    def accumulate(lam_r, lam_i, hp_r, hp_i, acc):
        return (acc[0] + lam_r * hp_r + lam_i * hp_i, acc[1] + lam_i * hp_r - lam_r * hp_i)

    def fix(t, carry):
        qr, qi, acc_r, acc_i = carry
        j = SCAN_STEPS - 1 - t
        tr, ti = _cmul(qr, qi, nxt_r, nxt_i)
        lam_r = lr_ref[_tile(j), :] + tr
        lam_i = li_ref[_tile(j), :] + ti
        lr_ref[_tile(j), :] = lam_r
        li_ref[_tile(j), :] = lam_i
        acc_r, acc_i = accumulate(lam_r, lam_i, hr_ref[_tile(j - 1), :], hi_ref[_tile(j - 1), :], (acc_r, acc_i))
        qr, qi = _cmul(qr, qi, a_r, a_i)
        return qr, qi, acc_r, acc_i

    qr, qi, acc_r, acc_i = lax.fori_loop(0, SCAN_STEPS - 1, fix, (a_r, a_i, zero, zero), unroll=4)
    tr, ti = _cmul(qr, qi, nxt_r, nxt_i)
    lam_r = lr_ref[_tile(0), :] + tr
    lam_i = li_ref[_tile(0), :] + ti
    lr_ref[_tile(0), :] = lam_r
    li_ref[_tile(0), :] = lam_i
    acc_r, acc_i = accumulate(lam_r, lam_i, er, ei, (acc_r, acc_i))
    return jnp.sum(acc_r, axis=0, keepdims=True), jnp.sum(acc_i, axis=0, keepdims=True)


def _rope_tables():
    half = HEAD_DIM // 2
    inv_freq = ROPE_THETA ** (-jnp.arange(half, dtype=F32) / half)
    ang = jnp.arange(SEQ, dtype=F32)[:, None] * inv_freq[None, :]
    cos, sin = jnp.cos(ang), jnp.sin(ang)
    cos_f = jnp.concatenate([cos, cos, cos, cos], axis=1)
    sin_s = jnp.concatenate([-sin, sin, -sin, sin], axis=1)
    return cos_f, sin_s


def _ssm_discretise(a_re, a_im, log_dt, b_re, b_im):
    lam = lax.complex(a_re, a_im)
    dt = jnp.exp(log_dt)[:, None]
    a_bar = jnp.exp(lam * dt)
    b_bar = ((a_bar - 1.0) / lam)[..., None] * lax.complex(b_re, b_im)
    return a_bar.real, a_bar.imag, b_bar.real, b_bar.imag


SSM_SLABS = 4
SLAB_GROUPS = SSM_GROUPS // SSM_SLABS
SLAB_IN = SSM_WIDTH // SSM_SLABS
SLAB_STATE = SSM_LANES // SSM_SLABS


def _slab_block_diag(blocks):
    _, r, c = blocks.shape
    eye = jnp.eye(SLAB_GROUPS, dtype=blocks.dtype)
    b5 = blocks.reshape(SSM_SLABS, SLAB_GROUPS, r, 1, c) * eye[None, :, None, :, None]
    return b5.reshape(SSM_SLABS, SLAB_GROUPS * r, SLAB_GROUPS * c)


def _diag_blocks(a, b):
    ra, cb = a.shape[1], b.shape[1]
    wa, wb = ra // SLAB_GROUPS, cb // SLAB_GROUPS
    d = lax.dot_general(a, b, (((0,), (0,)), ((), ())), preferred_element_type=F32)
    row_g = jnp.right_shift(lax.broadcasted_iota(jnp.int32, (ra, cb), 0), wa.bit_length() - 1)
    col_g = jnp.right_shift(lax.broadcasted_iota(jnp.int32, (ra, cb), 1), wb.bit_length() - 1)
    d = jnp.where(row_g == col_g, d, 0.0)
    fold = (jnp.bitwise_and(lax.broadcasted_iota(jnp.int32, (cb, wb), 0), wb - 1)
            == lax.broadcasted_iota(jnp.int32, (cb, wb), 1)).astype(F32)
    return jnp.dot(d, fold, preferred_element_type=F32, precision=lax.Precision.HIGHEST)


def _slab_specs():
    tok = pl.BlockSpec((SEQ, SLAB_IN), lambda j: (0, j))
    state = pl.BlockSpec((SEQ, SLAB_STATE), lambda j: (0, j))
    b_in = pl.BlockSpec((None, SLAB_IN, SLAB_STATE), lambda j: (j, 0, 0))
    c_out = pl.BlockSpec((None, SLAB_STATE, SLAB_IN), lambda j: (j, 0, 0))
    vec = pl.BlockSpec((1, SLAB_STATE), lambda j: (0, j))
    ent = pl.BlockSpec((SCAN_CHUNKS, SLAB_STATE), lambda j: (0, j))
    return tok, state, b_in, c_out, vec, ent


def _ssm_forward(u, b_in_r, b_in_i, c_out_r, c_out_ni, a_r, a_i):
    def body(u_ref, br_ref, bi_ref, cr_ref, ci_ref, ar_ref, ai_ref, y_ref, hr_ref, hi_ref, er_ref, ei_ref):
        uu = u_ref[...]
        hr_ref[...] = jnp.dot(uu, br_ref[...], preferred_element_type=F32)
        hi_ref[...] = jnp.dot(uu, bi_ref[...], preferred_element_type=F32)
        a_re = jnp.broadcast_to(ar_ref[...], (SCAN_CHUNKS, SLAB_STATE))
        a_im = jnp.broadcast_to(ai_ref[...], (SCAN_CHUNKS, SLAB_STATE))
        er_ref[...], ei_ref[...] = _scan_in_place(hr_ref, hi_ref, a_re, a_im)
        y_ref[...] = (jnp.dot(hr_ref[...].astype(BF16), cr_ref[...], preferred_element_type=F32)
                      + jnp.dot(hi_ref[...].astype(BF16), ci_ref[...], preferred_element_type=F32))

    tok, state, b_in, c_out, vec, ent = _slab_specs()
    return pl.pallas_call(
        body, grid=(SSM_SLABS,), in_specs=[tok, b_in, b_in, c_out, c_out, vec, vec],
        out_specs=[tok, state, state, ent, ent],
        out_shape=[_sds((SEQ, SSM_WIDTH)), _sds((SEQ, SSM_LANES)), _sds((SEQ, SSM_LANES)),
                   _sds((SCAN_CHUNKS, SSM_LANES)), _sds((SCAN_CHUNKS, SSM_LANES))],
        compiler_params=_cp(("parallel",)), name="ssm_forward")(u, b_in_r, b_in_i, c_out_r, c_out_ni, a_r, a_i)


def _ssm_backward(d_y, d_u_skip, u, h_r, h_i, e_r, e_i, b_in_r, b_in_i, c_out_r, c_out_ni, a_r, a_i):
    def body(dy_ref, skip_ref, u_ref, hr_ref, hi_ref, er_ref, ei_ref, br_ref, bi_ref, cr_ref, ci_ref, ar_ref, ai_ref,
             du_ref, dar_ref, dai_ref, dcr_ref, dci_ref, dbr_ref, dbi_ref, lr_ref, li_ref):
        dy = dy_ref[...]
        lr_ref[...] = _dot_nt(dy, cr_ref[...])
        li_ref[...] = _dot_nt(dy, ci_ref[...])
        a_re = jnp.broadcast_to(ar_ref[...], (SCAN_CHUNKS, SLAB_STATE))
        a_im = -jnp.broadcast_to(ai_ref[...], (SCAN_CHUNKS, SLAB_STATE))
        dar_ref[...], dai_ref[...] = _reverse_scan_in_place(lr_ref, li_ref, hr_ref, hi_ref, er_ref[...], ei_ref[...],
                                                            a_re, a_im)
        dcr_ref[...] = _diag_blocks(hr_ref[...].astype(BF16), dy)
        dci_ref[...] = _diag_blocks(hi_ref[...].astype(BF16), dy)
        lam_r, lam_i = lr_ref[...].astype(BF16), li_ref[...].astype(BF16)
        uu = u_ref[...]
        dbr_ref[...] = _diag_blocks(uu, lam_r)
        dbi_ref[...] = _diag_blocks(uu, lam_i)
        du = skip_ref[...] + _dot_nt(lam_r, br_ref[...]) + _dot_nt(lam_i, bi_ref[...])
        du_ref[...] = du.astype(BF16)

    tok, state, b_in, c_out, vec, ent = _slab_specs()
    dc = pl.BlockSpec((SLAB_STATE, SSM_GROUP), lambda j: (j, 0))
    db = pl.BlockSpec((SLAB_IN, SSM_STATE), lambda j: (j, 0))
    return pl.pallas_call(
        body, grid=(SSM_SLABS,), in_specs=[tok, tok, tok, state, state, ent, ent, b_in, b_in, c_out, c_out, vec, vec],
        out_specs=[tok, vec, vec, dc, dc, db, db],
        out_shape=[_sds((SEQ, SSM_WIDTH), BF16), _sds((1, SSM_LANES)), _sds((1, SSM_LANES)),
                   _sds((SSM_LANES, SSM_GROUP)), _sds((SSM_LANES, SSM_GROUP)),
                   _sds((SSM_WIDTH, SSM_STATE)), _sds((SSM_WIDTH, SSM_STATE))],
        scratch_shapes=[pltpu.VMEM((SEQ, SLAB_STATE), F32)] * 2,
        compiler_params=_cp(("parallel",)), name="ssm_backward")(
            d_y, d_u_skip, u, h_r, h_i, e_r, e_i, b_in_r, b_in_i, c_out_r, c_out_ni, a_r, a_i)


FF_ROWS = 1024
FF_SHARD = D_FF // N_CHIPS


def _dot_nt(a, b):
    return lax.dot_general(a, b, (((1,), (1,)), ((), ())), preferred_element_type=F32)


def _ffn_up(h, w_gate_t, w_up_t):
    def body(h_ref, wg_ref, wu_ref, a_ref, b_ref, act_ref):
        hb = h_ref[...].astype(BF16)
        a = _dot_nt(hb, wg_ref[...])
        b = _dot_nt(hb, wu_ref[...])
        a_ref[...] = a
        b_ref[...] = b
        act_ref[...] = (a * jax.nn.sigmoid(a) * b).astype(BF16)

    w_spec = pl.BlockSpec((None, FF_SHARD, D_MODEL), lambda i, k: (k, 0, 0))
    o_spec = pl.BlockSpec((None, FF_ROWS, FF_SHARD), lambda i, k: (k, i, 0))
    shape = (N_CHIPS, SEQ, FF_SHARD)
    return pl.pallas_call(
        body, grid=(SEQ // FF_ROWS, N_CHIPS),
        in_specs=[pl.BlockSpec((FF_ROWS, D_MODEL), lambda i, k: (i, 0)), w_spec, w_spec],
        out_specs=[o_spec, o_spec, o_spec], out_shape=[_sds(shape), _sds(shape), _sds(shape, BF16)],
        compiler_params=_cp(("parallel", "parallel")), name="ffn_up")(h, w_gate_t, w_up_t)


def _ffn_down_bwd(dz, w_down, a, b):
    def body(dz_ref, wd_ref, a_ref, b_ref, da_ref, db_ref):
        d_act = _dot_nt(dz_ref[...].astype(BF16), wd_ref[...])
        av = a_ref[...]
        sg = jax.nn.sigmoid(av)
        da_ref[...] = (d_act * b_ref[...] * sg * (1.0 + av * (1.0 - sg))).astype(BF16)
        db_ref[...] = (d_act * av * sg).astype(BF16)

    t_spec = pl.BlockSpec((None, FF_ROWS, FF_SHARD), lambda i, k: (k, i, 0))
    shape = (N_CHIPS, SEQ, FF_SHARD)
    return pl.pallas_call(
        body, grid=(SEQ // FF_ROWS, N_CHIPS),
        in_specs=[pl.BlockSpec((FF_ROWS, D_MODEL), lambda i, k: (i, 0)),
                  pl.BlockSpec((None, FF_SHARD, D_MODEL), lambda i, k: (k, 0, 0)), t_spec, t_spec],
        out_specs=[t_spec, t_spec], out_shape=[_sds(shape, BF16), _sds(shape, BF16)],
        compiler_params=_cp(("parallel", "parallel")), name="ffn_down_bwd")(dz, w_down, a, b)


def _ffn_dh(d_a, d_b, w_gate_t, w_up_t):
    def body(da_ref, db_ref, wg_ref, wu_ref, o_ref, acc):
        k = pl.program_id(1)
        part = (jnp.dot(da_ref[...], wg_ref[...], preferred_element_type=F32)
                + jnp.dot(db_ref[...], wu_ref[...], preferred_element_type=F32))

        @pl.when(k == 0)
        def _():
            acc[...] = part

        @pl.when(k > 0)
        def _():
            acc[...] += part

        @pl.when(k == N_CHIPS - 1)
        def _():
            o_ref[...] = acc[...]

    t_spec = pl.BlockSpec((None, FF_ROWS, FF_SHARD), lambda i, k: (k, i, 0))
    w_spec = pl.BlockSpec((None, FF_SHARD, D_MODEL), lambda i, k: (k, 0, 0))
    return pl.pallas_call(
        body, grid=(SEQ // FF_ROWS, N_CHIPS), in_specs=[t_spec, t_spec, w_spec, w_spec],
        out_specs=pl.BlockSpec((FF_ROWS, D_MODEL), lambda i, k: (i, 0)), out_shape=_sds((SEQ, D_MODEL)),
        scratch_shapes=[pltpu.VMEM((FF_ROWS, D_MODEL), F32)],
        compiler_params=_cp(("parallel", "arbitrary")), name="ffn_dh")(d_a, d_b, w_gate_t, w_up_t)


def _local_step(x, tgt, wts, small):
    s = SEQ
    cos_f, sin_s = [_to_phase_rows(t) for t in _rope_tables()]
    x = _reorder_rows(x, to_phase=True, name="phase_rows_x")
    tgt = _reorder_rows(tgt, to_phase=True, name="phase_rows_target")

    proj = _mm_cols(x, wts["w_in"], tm=1024, name="proj")

    attn, lse = _attention_fwd(proj, cos_f, sin_s)
    y_attn = _mm_cols(attn, wts["w_attn_br"], tm=s, name="y_attn")

    (abar_r, abar_i, bbar_r, bbar_i), ssm_vjp = jax.vjp(
        _ssm_discretise, small["ssm_a_re"], small["ssm_a_im"], small["ssm_log_dt"], small["ssm_b_re"], small["ssm_b_im"])
    b_in_r, b_in_i = [_slab_block_diag(b.transpose(0, 2, 1)).astype(BF16) for b in (bbar_r, bbar_i)]
    c_out_r = _slab_block_diag(small["ssm_c_re"].transpose(0, 2, 1)).astype(BF16)
    c_out_ni = _slab_block_diag(-small["ssm_c_im"].transpose(0, 2, 1)).astype(BF16)
    a_r, a_i = abar_r.reshape(1, SSM_LANES), abar_i.reshape(1, SSM_LANES)
    d_skip = small["ssm_d"].reshape(1, SSM_WIDTH)

    u_f = _to_scan_rows(proj[:, 3 * QKV_WIDTH:3 * QKV_WIDTH + SSM_WIDTH])
    u_p = u_f.astype(BF16)
    y_c, h_r, h_i, e_r, e_i = _ssm_forward(u_p, b_in_r, b_in_i, c_out_r, c_out_ni, a_r, a_i)

    def gelu_fwd(yc, u, dsk):
        y = yc + dsk * u
        return y, 0.5 * y * (1.0 + jnp.tanh(GELU_C * (y + GELU_K * y * y * y)))

    y_s5, gel = _rowwise(gelu_fwd, [y_c, u_f], [d_skip], [_sds((s, SSM_WIDTH)), _sds((s, SSM_WIDTH), BF16)],
                         tm=512, name="ssm_gelu")
    glu = _mm_cols(gel, wts["w_glu"], tm=s, name="glu")

    def glu_fwd(ga, gb):
        return ga * jax.nn.sigmoid(gb)

    (y_glu,) = _rowwise(glu_fwd, [(glu, SSM_WIDTH, 0), (glu, SSM_WIDTH, 1)], [], [_sds((s, SSM_WIDTH), BF16)],
                        tm=512, name="glu_gate")
    y_glu = _from_scan_rows(y_glu)
    y_ssm = _mm_cols(y_glu, wts["w_ssm_br"], tm=s, name="y_ssm")

    gl0 = (proj, D_MODEL, (3 * QKV_WIDTH + SSM_WIDTH) // D_MODEL)
    gl1 = (proj, D_MODEL, (3 * QKV_WIDTH + SSM_WIDTH) // D_MODEL + 1)
    b_gate = small["b_gate"]

    def gate_mix(l0, l1, ya, ys, bg):
        return jax.nn.sigmoid(l0 + bg[0:1]) * ya + jax.nn.sigmoid(l1 + bg[1:2]) * ys

    (mixed,) = _rowwise(gate_mix, [gl0, gl1, y_attn, y_ssm], [b_gate], [_sds((s, D_MODEL), BF16)], tm=256,
                        name="gate_mix")
    w_out = wts["w_out"].reshape(D_MODEL, D_MODEL)
    mix_out = _mm_plain(mixed, w_out, tm=1024, tn=512, name="mix_out")

    def ln1_fwd(xv, mo, g, b):
        z = DN_ALPHA * xv + mo
        xhat, _ = _ln_stats(z)
        return z, xhat * g + b

    z1, h = _rowwise(ln1_fwd, [x, mix_out], [small["ln1_g"], small["ln1_b"]],
                     [_sds((s, D_MODEL)), _sds((s, D_MODEL))], tm=256, name="ln1")

    nf = D_FF // N_CHIPS
    w_gate_t, w_up_t, w_down = wts["w_ff_gate"], wts["w_ff_up"], wts["w_ff_down"]
    ff_a, ff_b, act = _ffn_up(h, w_gate_t, w_up_t)
    ff = _matmul(act, w_down, grid=(2, N_CHIPS),
                 a_spec=pl.BlockSpec((None, 1024, nf), lambda i, k: (k, i, 0)),
                 b_spec=pl.BlockSpec((None, nf, D_MODEL), lambda i, k: (k, 0, 0)),
                 o_spec=pl.BlockSpec((1024, D_MODEL), lambda i, k: (i, 0)),
                 out_shape=_sds((s, D_MODEL)), dims=(1, 0), k_axis=1, name="ff_down")

    def ln2_loss(hv, ffv, tg, g, b):
        z = DN_ALPHA * hv + ffv
        xhat, rstd = _ln_stats(z)
        err = xhat * g + b - tg
        d_out = err * (1.0 / D_MODEL)
        loss_rows = jnp.sum(err * err, axis=-1, keepdims=True) * (0.5 / D_MODEL)
        loss = jnp.broadcast_to(jnp.sum(loss_rows, axis=0, keepdims=True), (1, 128))
        return _ln_bwd(d_out, xhat, rstd, g), loss, _colsum(d_out * xhat), _colsum(d_out)

    dz2, loss_v, d_ln2_g, d_ln2_b = _rowwise(
        ln2_loss, [h, ff, tgt], [small["ln2_g"], small["ln2_b"]], [_sds((s, D_MODEL))],
        [_sds((1, 128)), _sds((1, D_MODEL)), _sds((1, D_MODEL))], tm=256, name="ln2_loss")

    d_a, d_b = _ffn_down_bwd(dz2, w_down, ff_a, ff_b)

    def grad_rows(lhs, rhs, name):
        return _matmul(lhs, rhs, grid=(N_CHIPS,), a_spec=pl.BlockSpec((None, s, nf), lambda k: (k, 0, 0)),
                       b_spec=pl.BlockSpec((s, D_MODEL), lambda k: (0, 0)),
                       o_spec=pl.BlockSpec((None, nf, D_MODEL), lambda k: (k, 0, 0)),
                       out_shape=_sds((N_CHIPS, nf, D_MODEL), BF16), dims=(0, 0), name=name)

    g_w_ff_down = grad_rows(act, dz2, "g_w_ff_down")
    g_w_ff_gate = grad_rows(d_a, h, "g_w_ff_gate")
    g_w_ff_up = grad_rows(d_b, h, "g_w_ff_up")
    dh_ff = _ffn_dh(d_a, d_b, w_gate_t, w_up_t)

    def ln1_bwd(dz, dff, z, g):
        xhat, rstd = _ln_stats(z)
        dh = DN_ALPHA * dz + dff
        return _ln_bwd(dh, xhat, rstd, g), _colsum(dh * xhat), _colsum(dh)

    dz1, d_ln1_g, d_ln1_b = _rowwise(ln1_bwd, [dz2, dh_ff, z1], [small["ln1_g"]], [_sds((s, D_MODEL))],
                                     [_sds((1, D_MODEL)), _sds((1, D_MODEL))], tm=256, name="ln1_bwd")
    d_mixed = _mm_plain(dz1, w_out, tm=1024, tn=512, dims=(1, 1), name="d_mixed")
    g_w_out = _mm_plain(mixed, dz1, tm=D_MODEL, tn=512, dims=(0, 0), out_dtype=BF16, name="g_w_out")
    g_w_out = g_w_out.reshape(N_CHIPS, D_MODEL // N_CHIPS, D_MODEL)

    def gate_bwd(dm, l0, l1, ya, ys, bg):
        g0 = jax.nn.sigmoid(l0 + bg[0:1])
        g1 = jax.nn.sigmoid(l1 + bg[1:2])
        dl0 = dm * ya * g0 * (1.0 - g0)
        dl1 = dm * ys * g1 * (1.0 - g1)
        return dm * g0, dm * g1, jnp.concatenate([dl0, dl1], axis=1), _colsum(dl0), _colsum(dl1)

    d_y_attn, d_y_ssm, d_gl, d_bg0, d_bg1 = _rowwise(
        gate_bwd, [d_mixed, gl0, gl1, y_attn, y_ssm], [b_gate],
        [_sds((s, D_MODEL), BF16), _sds((s, D_MODEL), BF16), _sds((s, 2 * D_MODEL), BF16)],
        [_sds((1, D_MODEL)), _sds((1, D_MODEL))], tm=256, name="gate_bwd")

    g_w_ssm_br = _mm_cols_tn(y_glu, d_y_ssm, ns=D_MODEL // N_CHIPS, name="g_w_ssm_br")
    d_y_glu = _to_scan_rows(_mm_cols_nt(d_y_ssm, wts["w_ssm_br"], tm=s, name="d_y_glu"))

    def glu_bwd(dy, ga, gb):
        sg = jax.nn.sigmoid(gb)
        return jnp.concatenate([dy * sg, dy * ga * sg * (1.0 - sg)], axis=1)

    (d_glu,) = _rowwise(glu_bwd, [d_y_glu, (glu, SSM_WIDTH, 0), (glu, SSM_WIDTH, 1)], [],
                        [_sds((s, 2 * SSM_WIDTH), BF16)], tm=512, name="glu_bwd")
    g_w_glu = _mm_cols_tn(gel, d_glu, ns=2 * SSM_WIDTH // N_CHIPS, name="g_w_glu")
    d_gel = _mm_cols_nt(d_glu, wts["w_glu"], tm=s, name="d_gel")

    def gelu_bwd(dg, y, u, dsk):
        th = jnp.tanh(GELU_C * (y + GELU_K * y * y * y))
        dy = dg * (0.5 * (1.0 + th) + 0.5 * y * (1.0 - th * th) * GELU_C * (1.0 + 3.0 * GELU_K * y * y))
        return dy, dy * dsk, _colsum(dy * u)

    d_y, d_u_skip, d_ssm_d = _rowwise(gelu_bwd, [d_gel, y_s5, u_f], [d_skip],
                                      [_sds((s, SSM_WIDTH), BF16), _sds((s, SSM_WIDTH))], [_sds((1, SSM_WIDTH))],
                                      tm=512, name="gelu_bwd")
    d_u, d_abar_r, d_abar_i, d_c_r, d_c_ni, d_bin_r, d_bin_i = _ssm_backward(
        d_y, d_u_skip, u_p, h_r, h_i, e_r, e_i, b_in_r, b_in_i, c_out_r, c_out_ni, a_r, a_i)
    d_u = _from_scan_rows(d_u)
    d_bbar_r = d_bin_r.reshape(SSM_GROUPS, SSM_GROUP, SSM_STATE).transpose(0, 2, 1)
    d_bbar_i = d_bin_i.reshape(SSM_GROUPS, SSM_GROUP, SSM_STATE).transpose(0, 2, 1)
    d_a_re, d_a_im, d_log_dt, d_b_re, d_b_im = ssm_vjp(
        (d_abar_r.reshape(SSM_GROUPS, SSM_STATE), d_abar_i.reshape(SSM_GROUPS, SSM_STATE), d_bbar_r, d_bbar_i))
    d_c_re = d_c_r.reshape(SSM_GROUPS, SSM_STATE, SSM_GROUP).transpose(0, 2, 1)
    d_c_im = -d_c_ni.reshape(SSM_GROUPS, SSM_STATE, SSM_GROUP).transpose(0, 2, 1)

    g_w_attn_br = _mm_cols_tn(attn, d_y_attn, ns=D_MODEL // N_CHIPS, name="g_w_attn_br")
    d_attn = _mm_cols_nt(d_y_attn, wts["w_attn_br"], tm=s, name="d_attn")
    dqkv = [_attention_bwd(g, proj, cos_f, sin_s, d_attn, attn, lse) for g in range(len(DILATIONS))]

    d_proj = jnp.concatenate([dqkv[g][j] for j in range(3) for g in range(len(DILATIONS))] + [d_u, d_gl],
                             axis=1)
    g_w_in = _mm_cols_tn(x, d_proj, ns=IN_WIDTH // N_CHIPS, name="g_w_in")

    def grad_x_after(after):
        dx_proj = _mm_cols_nt(d_proj, wts["w_in"], tm=1024, name="dx_proj", after=after)

        def dx_sum(dz, dxp):
            return DN_ALPHA * dz + dxp

        (gx,) = _rowwise(dx_sum, [dz1, dx_proj], [], [_sds((s, D_MODEL))], tm=512, name="grad_x")
        return _reorder_rows(gx, to_phase=False, name="time_rows_grad_x")

    big = {"w_in": g_w_in, "w_attn_br": g_w_attn_br, "w_ssm_br": g_w_ssm_br, "w_out": g_w_out, "w_glu": g_w_glu,
           "w_ff_gate": g_w_ff_gate, "w_ff_up": g_w_ff_up, "w_ff_down": g_w_ff_down}
    small_g = {"b_gate": jnp.concatenate([d_bg0, d_bg1], axis=0), "ssm_a_re": d_a_re, "ssm_a_im": d_a_im,
               "ssm_log_dt": d_log_dt, "ssm_b_re": d_b_re, "ssm_b_im": d_b_im, "ssm_c_re": d_c_re, "ssm_c_im": d_c_im,
               "ssm_d": d_ssm_d.reshape(SSM_WIDTH), "ln1_g": d_ln1_g, "ln1_b": d_ln1_b, "ln2_g": d_ln2_g,
               "ln2_b": d_ln2_b}
    marks = {"ln1_bwd": dz1, "scan_bwd": d_abar_r, "attention_bwd_0": dqkv[0][0]}
    return loss_v[0, 0], grad_x_after, big, small_g, marks


GATHER_ID, SWAP_ID, SCATTER_ID, JOIN_ID, EXCHANGE_ID = 1, 2, 3, 4, 5


def _place():
    return lax.axis_index("x"), lax.axis_index("y"), lax.axis_index("c")


def _other_chips(x, y):
    return [(1 - x, y), (x, 1 - y), (1 - x, 1 - y)]


def _handshake(peers):
    barrier = pltpu.get_barrier_semaphore()
    for peer in peers:
        pl.semaphore_signal(barrier, inc=1, device_id=peer, device_id_type=MESH)
    pl.semaphore_wait(barrier, len(peers))


def _sequencer(body, arrays, out_type, sems, collective_id, name):
    return pl.kernel(body, name=name, out_type=out_type,
                     mesh=plsc.ScalarSubcoreMesh(axis_name="sequencer", num_cores=1), scratch_types=sems,
                     compiler_params=pltpu.CompilerParams(collective_id=collective_id))(*arrays)


def _gather_weights(shards, *, name):
    nw = len(shards)

    def body(*refs):
        ins, outs = refs[:nw], refs[nw:2 * nw]
        send_sems, recv_sems, pass_send, pass_recv, local_sems = refs[2 * nw:]
        x, y, c = _place()
        chip = 2 * x + y
        chips = _other_chips(x, y)
        _handshake([(x, y, 1 - c)] + [(cx, cy, c) for cx, cy in chips])
        started = []
        for w in range(nw):
            hw = shards[w].shape[0] // 2
            mine = pl.ds(c * hw, hw)
            own = pltpu.make_async_copy(ins[w], outs[w].at[chip], local_sems.at[w])
            own.start()
            started.append(own)
            for j, (cx, cy) in enumerate(chips):
                cp = pltpu.make_async_remote_copy(
                    src_ref=ins[w].at[mine], dst_ref=outs[w].at[chip, mine], send_sem=send_sems.at[w, j],
                    recv_sem=recv_sems.at[w, j], device_id=(cx, cy, c), device_id_type=MESH)
                cp.start()
                started.append(cp)
        passed = []
        for w in range(nw):
            hw = shards[w].shape[0] // 2
            mine = pl.ds(c * hw, hw)
            for j, (cx, cy) in enumerate(chips):
                landed = outs[w].at[2 * cx + cy, mine]
                pltpu.make_async_remote_copy(
                    src_ref=ins[w].at[mine], dst_ref=landed, send_sem=send_sems.at[w, j],
                    recv_sem=recv_sems.at[w, j], device_id=(cx, cy, c), device_id_type=MESH).wait_recv()
                cp = pltpu.make_async_remote_copy(
                    src_ref=landed, dst_ref=landed, send_sem=pass_send.at[w, j], recv_sem=pass_recv.at[w, j],
                    device_id=(x, y, 1 - c), device_id_type=MESH)
                cp.start()
                passed.append(cp)
        for w in range(nw):
            hw = shards[w].shape[0] // 2
            theirs = pl.ds((1 - c) * hw, hw)
            for j, (cx, cy) in enumerate(chips):
                landed = outs[w].at[2 * cx + cy, theirs]
                pltpu.make_async_remote_copy(
                    src_ref=landed, dst_ref=landed, send_sem=pass_send.at[w, j], recv_sem=pass_recv.at[w, j],
                    device_id=(x, y, 1 - c), device_id_type=MESH).wait_recv()
        for cp in started[0::4]:
            cp.wait()
        for cp in [s for i, s in enumerate(started) if i % 4] + passed:
            cp.wait_send()

    sem = pltpu.SemaphoreType.DMA
    return _sequencer(body, shards, [_sds((N_CHIPS,) + a.shape, a.dtype) for a in shards],
                      [sem((nw, 3)), sem((nw, 3)), sem((nw, 3)), sem((nw, 3)), sem((nw,))], GATHER_ID, name)


def _swap_other_halves(grads, *, name):
    nw = len(grads)

    def body(*refs):
        ins, outs = refs[:nw], refs[nw:2 * nw]
        send_sems, recv_sems = refs[2 * nw:]
        x, y, c = _place()
        _handshake([(x, y, 1 - c)])
        cps = []
        for w in range(nw):
            hw = grads[w].shape[1] // 2
            cp = pltpu.make_async_remote_copy(
                src_ref=ins[w].at[:, pl.ds((1 - c) * hw, hw)], dst_ref=outs[w], send_sem=send_sems.at[w],
                recv_sem=recv_sems.at[w], device_id=(x, y, 1 - c), device_id_type=MESH)
            cp.start()
            cps.append(cp)
        for cp in cps:
            cp.wait()

    sem = pltpu.SemaphoreType.DMA
    return _sequencer(body, grads, [_sds((N_CHIPS, g.shape[1] // 2, g.shape[2]), g.dtype) for g in grads],
                      [sem((nw,)), sem((nw,))], SWAP_ID, name)


def _add_my_halves(core, grads, others, *, name, after=()):
    nw = len(grads)
    halves = [g.shape[1] // 2 for g in grads]

    def body(core_ref, *refs):
        outs = refs[2 * nw + len(after):]
        for g_ref, o_ref, out_ref in zip(refs[:nw], refs[nw:2 * nw], outs):
            out_ref[...] = (g_ref[...].astype(F32) + o_ref[...].astype(F32)).astype(out_ref.dtype)

    in_specs = [pl.BlockSpec((None, None, hw, g.shape[2]), lambda s, core_ref: (s, core_ref[0], 0, 0))
                for g, hw in zip(grads, halves)]
    in_specs += [pl.BlockSpec((None, hw, g.shape[2]), lambda s, core_ref: (s, 0, 0)) for g, hw in zip(grads, halves)]
    return pl.pallas_call(
        body,
        grid_spec=pltpu.PrefetchScalarGridSpec(
            num_scalar_prefetch=1, grid=(N_CHIPS,), in_specs=in_specs + [HBM_OPERAND] * len(after),
            out_specs=[pl.BlockSpec((None, hw, g.shape[2]), lambda s, core_ref: (s, 0, 0))
                       for g, hw in zip(grads, halves)]),
        out_shape=[_sds((N_CHIPS, hw, g.shape[2]), BF16) for g, hw in zip(grads, halves)],
        compiler_params=_cp(("parallel",)), name=name)(
            core, *[g.reshape(N_CHIPS, 2, hw, g.shape[2]) for g, hw in zip(grads, halves)], *others, *after)


def _scatter_partials(parts, *, name):
    nw = len(parts)

    def body(*refs):
        ins, outs = refs[:nw], refs[nw:2 * nw]
        send_sems, recv_sems = refs[2 * nw:]
        x, y, c = _place()
        _handshake([(cx, cy, c) for cx, cy in _other_chips(x, y)])
        cps = []
        for w in range(nw):
            for j, (cx, cy) in enumerate(_other_chips(x, y)):
                cp = pltpu.make_async_remote_copy(
                    src_ref=ins[w].at[2 * cx + cy], dst_ref=outs[w].at[j], send_sem=send_sems.at[w, j],
                    recv_sem=recv_sems.at[w, j], device_id=(cx, cy, c), device_id_type=MESH)
                cp.start()
                cps.append(cp)
        for cp in cps:
            cp.wait()

    sem = pltpu.SemaphoreType.DMA
    return _sequencer(body, parts, [_sds((3,) + p.shape[1:], p.dtype) for p in parts],
                      [sem((nw, 3)), sem((nw, 3))], SCATTER_ID, name)


SUM_STEPS = 2


def _sum_partials(chip, parts, recvd, *, name, after=()):
    nw = len(parts)
    rows = [p.shape[1] // SUM_STEPS for p in parts]

    def body(chip_ref, *refs):
        outs = refs[2 * nw + len(after):]
        for p_ref, r_ref, out_ref in zip(refs[:nw], refs[nw:2 * nw], outs):
            acc = p_ref[...].astype(F32)
            for j in range(3):
                acc = acc + r_ref[j].astype(F32)
            out_ref[...] = acc

    in_specs = [pl.BlockSpec((None, th, p.shape[2]), lambda i, chip_ref: (chip_ref[0], i, 0))
                for p, th in zip(parts, rows)]
    in_specs += [pl.BlockSpec((3, th, p.shape[2]), lambda i, chip_ref: (0, i, 0)) for p, th in zip(parts, rows)]
    return pl.pallas_call(
        body,
        grid_spec=pltpu.PrefetchScalarGridSpec(
            num_scalar_prefetch=1, grid=(SUM_STEPS,), in_specs=in_specs + [HBM_OPERAND] * len(after),
            out_specs=[pl.BlockSpec((th, p.shape[2]), lambda i, chip_ref: (i, 0)) for p, th in zip(parts, rows)]),
        out_shape=[_sds(p.shape[1:]) for p in parts], compiler_params=_cp(("parallel",)), name=name)(
            chip, *parts, *recvd, *after)


def _swap_reduced_halves(halves, *, name):
    nw = len(halves)

    def body(*refs):
        ins, outs = refs[:nw], refs[nw:2 * nw]
        send_sems, recv_sems = refs[2 * nw:]
        x, y, c = _place()
        _handshake([(x, y, 1 - c)])
        cps = []
        for w in range(nw):
            cp = pltpu.make_async_remote_copy(
                src_ref=ins[w], dst_ref=outs[w], send_sem=send_sems.at[w], recv_sem=recv_sems.at[w],
                device_id=(x, y, 1 - c), device_id_type=MESH)
            cp.start()
            cps.append(cp)
        for cp in cps:
            cp.wait()

    sem = pltpu.SemaphoreType.DMA
    return _sequencer(body, halves, [_sds(h.shape, h.dtype) for h in halves], [sem((nw,)), sem((nw,))], JOIN_ID, name)


def _exchange_rows(vec, *, name):
    def body(v_ref, slots, send_sems, recv_sems, local_sem):
        x, y, c = _place()
        me = 4 * x + 2 * y + c
        peers = []
        for mask in range(1, N_DEV):
            peers.append((1 - x if mask & 4 else x, 1 - y if mask & 2 else y, 1 - c if mask & 1 else c))
        _handshake(peers)
        own = pltpu.make_async_copy(v_ref, slots.at[me], local_sem)
        own.start()
        cps = []
        for k, peer in enumerate(peers):
            cp = pltpu.make_async_remote_copy(
                src_ref=v_ref, dst_ref=slots.at[me], send_sem=send_sems.at[k], recv_sem=recv_sems.at[k],
                device_id=peer, device_id_type=MESH)
            cp.start()
            cps.append(cp)
        for k, (px, py, pc) in enumerate(peers):
            pltpu.make_async_remote_copy(
                src_ref=v_ref, dst_ref=slots.at[4 * px + 2 * py + pc], send_sem=send_sems.at[k],
                recv_sem=recv_sems.at[k], device_id=(px, py, pc), device_id_type=MESH).wait_recv()
        for cp in cps:
            cp.wait_send()
        own.wait()

    sem = pltpu.SemaphoreType.DMA
    return _sequencer(body, [vec], [_sds((N_DEV,) + vec.shape)], [sem((N_DEV - 1,)), sem((N_DEV - 1,)), sem(())],
                      EXCHANGE_ID, name)[0]


def _sum_slots(slots, *, name, after=()):
    def body(s_ref, *rest):
        out_ref = rest[len(after)]
        acc = s_ref[0]
        for d in range(1, N_DEV):
            acc = acc + s_ref[d]
        out_ref[...] = acc

    vmem = pl.BlockSpec(memory_space=pltpu.VMEM)
    return pl.pallas_call(
        body, in_specs=[vmem] + [HBM_OPERAND] * len(after), out_specs=vmem, out_shape=_sds(slots.shape[1:]),
        compiler_params=pltpu.CompilerParams(vmem_limit_bytes=VMEM_LIMIT_BYTES), name=name)(slots, *after)


def _reduce_scatter_start(grads, core, *, tag, add_after=()):
    others = _swap_other_halves(grads, name="swap_other_halves_" + tag)
    parts = _add_my_halves(core, grads, others, name="add_my_halves_" + tag, after=add_after)
    return parts, _scatter_partials(parts, name="scatter_partials_" + tag)


def _reduce_scatter_finish(parts, recvd, chip, *, tag, sum_after=()):
    mine = _sum_partials(chip, parts, recvd, name="sum_partials_" + tag, after=sum_after)
    return mine, _swap_reduced_halves(mine, name="swap_reduced_halves_" + tag)


ADAM_BLOCK_ELEMS = 256 * 1024


def _adam_rows(rows, cols):
    tm = rows
    while tm * cols > ADAM_BLOCK_ELEMS and tm % 16 == 0:
        tm //= 2
    return tm


def _adam_step(wv, gv, mv, vv):
    m2 = ADAM_B1 * mv + (1.0 - ADAM_B1) * gv
    v2 = ADAM_B2 * vv + (1.0 - ADAM_B2) * (gv * gv)
    m_hat = m2 / (1.0 - ADAM_B1 ** ADAM_STEP)
    v_hat = v2 / (1.0 - ADAM_B2 ** ADAM_STEP)
    return -ADAM_LR * (m_hat / (jnp.sqrt(v_hat) + ADAM_EPS) + ADAM_WD * wv), m2, v2


def _adamw(w, g, m, v, *, name):
    rows, cols = w.shape
    return _rowwise(_adam_step, [w, g, m, v], [], [_sds((rows, cols))] * 3, tm=_adam_rows(rows, cols), name=name)


def _adamw_halves(core, w, g_mine, g_theirs, m, v, *, name, after=()):
    rows, cols = w.shape
    hw = rows // 2
    tm = _adam_rows(hw, cols)
    per_half = hw // tm

    def body(core_ref, w_ref, gm_ref, gt_ref, m_ref, v_ref, *rest):
        g_out, d_out, m_out, v_out = rest[len(after):]
        mine = (pl.program_id(0) // per_half) == core_ref[0]
        g = jnp.where(mine, gm_ref[...], gt_ref[...])
        d, m2, v2 = _adam_step(w_ref[...], g, m_ref[...], v_ref[...])
        g_out[...] = g
        d_out[...] = d
        m_out[...] = m2
        v_out[...] = v2

    full = pl.BlockSpec((tm, cols), lambda i, core_ref: (i, 0))
    half = pl.BlockSpec((tm, cols), lambda i, core_ref: (i % per_half, 0))
    return pl.pallas_call(
        body,
        grid_spec=pltpu.PrefetchScalarGridSpec(
            num_scalar_prefetch=1, grid=(rows // tm,),
            in_specs=[full, half, half, full, full] + [HBM_OPERAND] * len(after), out_specs=[full, full, full, full]),
        out_shape=[_sds((rows, cols))] * 4, compiler_params=_cp(("parallel",)), name=name)(
            core, w, g_mine, g_theirs, m, v, *after)


HELD_TRANSPOSED = ("w_ff_gate", "w_ff_up")


def _as_rows(name, arr):
    return arr[0].T if name in HELD_TRANSPOSED else arr[0]


def _from_rows(name, arr2d):
    return (arr2d.T if name in HELD_TRANSPOSED else arr2d)[None]


STORED_SWAPPED = ("ssm_b_re", "ssm_b_im")


def _as_stored(name, arr):
    return jnp.swapaxes(arr, -1, -2) if name in STORED_SWAPPED else arr


def _pack_rows(arrs):
    flat = jnp.concatenate([a.reshape(-1).astype(F32) for a in arrs])
    rows = -(-flat.shape[0] // 1024) * 8
    return jnp.pad(flat, (0, rows * 128 - flat.shape[0])).reshape(rows, 128)


def _unpack_rows(vec, shapes):
    flat = vec.reshape(-1)
    out, off = [], 0
    for shp in shapes:
        size = math.prod(shp)
        out.append(flat[off:off + size].reshape(shp))
        off += size
    return out


SMALL = ("b_gate", "ssm_a_re", "ssm_a_im", "ssm_log_dt", "ssm_b_re", "ssm_b_im", "ssm_c_re", "ssm_c_im", "ssm_d",
         "ln1_g", "ln1_b", "ln2_g", "ln2_b")
GATHER_GROUPS = (("w_in", ("w_in",)), ("mixer", ("w_attn_br", "w_ssm_br", "w_glu", "w_out")),
                 ("ffn", ("w_ff_gate", "w_ff_up", "w_ff_down")))
REDUCE_GROUPS = (("ffn", ("w_ff_down", "w_ff_gate", "w_ff_up")),
                 ("mixer", ("w_out", "w_ssm_br", "w_glu", "w_attn_br")), ("w_in", ("w_in",)))
WEIGHTS = ("w_in", "b_gate", "w_attn_br", "w_ssm_br", "w_out", "ssm_a_re", "ssm_a_im", "ssm_log_dt", "ssm_b_re",
           "ssm_b_im", "ssm_c_re", "ssm_c_im", "ssm_d", "w_glu", "ln1_g", "ln1_b", "w_ff_gate", "w_ff_up", "w_ff_down",
           "ln2_g", "ln2_b")


def kernel(x, w_in, b_gate, w_attn_br, w_ssm_br, w_out, ssm_a_re, ssm_a_im, ssm_log_dt, ssm_b_re, ssm_b_im, ssm_c_re, ssm_c_im, ssm_d, w_glu, ln1_g, ln1_b, w_ff_gate, w_ff_up, w_ff_down, ln2_g, ln2_b, loss_target, m_w_in, m_b_gate, m_w_attn_br, m_w_ssm_br, m_w_out, m_ssm_a_re, m_ssm_a_im, m_ssm_log_dt, m_ssm_b_re, m_ssm_b_im, m_ssm_c_re, m_ssm_c_im, m_ssm_d, m_w_glu, m_ln1_g, m_ln1_b, m_w_ff_gate, m_w_ff_up, m_w_ff_down, m_ln2_g, m_ln2_b, v_w_in, v_b_gate, v_w_attn_br, v_w_ssm_br, v_w_out, v_ssm_a_re, v_ssm_a_im, v_ssm_log_dt, v_ssm_b_re, v_ssm_b_im, v_ssm_c_re, v_ssm_c_im, v_ssm_d, v_w_glu, v_ln1_g, v_ln1_b, v_w_ff_gate, v_w_ff_up, v_w_ff_down, v_ln2_g, v_ln2_b):
    given = dict(locals())
    px, py, pc = _place()
    chip = 2 * px + py
    core_s = jnp.reshape(pc, (1,)).astype(jnp.int32)
    chip_s = jnp.reshape(chip, (1,)).astype(jnp.int32)

    wts = {}
    for tag, names in GATHER_GROUPS:
        wts.update(zip(names, _gather_weights([_as_rows(n, given[n]).astype(BF16) for n in names],
                                              name="gather_" + tag)))
    ncol = D_MODEL // N_CHIPS
    bg_mine = jnp.where(pc == 0, b_gate[0], jnp.zeros_like(b_gate[0]))
    bg_full = lax.dynamic_update_slice(jnp.zeros((2, D_MODEL), F32), bg_mine, (0, chip * ncol))
    bg_slots = _exchange_rows(bg_full.reshape(16, 128), name="exchange_gate_bias")
    bg_full = _sum_slots(bg_slots, name="sum_gate_bias").reshape(2, D_MODEL)
    small = {n: given[n][0] for n in SMALL if n.startswith("ssm")}
    small.update({n: given[n] for n in ("ln1_g", "ln1_b", "ln2_g", "ln2_b")})
    small["b_gate"] = bg_full

    loss_mine, grad_x_after, big_g, small_g, marks = _local_step(x[0], loss_target[0], wts, small)
    loss = lax.psum(loss_mine, ("x", "y", "c"))

    groups = dict(REDUCE_GROUPS)
    parts, recvd = {}, {}
    grads, delta, new_m, new_v = {}, {}, {}, {}

    def start(tag, add_after):
        parts[tag], recvd[tag] = _reduce_scatter_start([big_g[n] for n in groups[tag]], core_s, tag=tag,
                                                       add_after=add_after)

    def finish(tag, sum_after, adam_after):
        mine, theirs = _reduce_scatter_finish(parts[tag], recvd[tag], chip_s, tag=tag, sum_after=sum_after)
        for n, g_mine, g_theirs in zip(groups[tag], mine, theirs):
            res = _adamw_halves(core_s, _as_rows(n, given[n]), g_mine, g_theirs, _as_rows(n, given["m_" + n]),
                                _as_rows(n, given["v_" + n]), name="adamw_" + n, after=adam_after)
            grads[n], delta[n], new_m[n], new_v[n] = [_from_rows(n, r) for r in res]

    start("ffn", (marks["ln1_bwd"],))
    start("mixer", (marks["scan_bwd"],))
    finish("mixer", (marks["attention_bwd_0"],), (big_g["w_in"],))
    start("w_in", tuple(delta[n] for n in groups["mixer"]))
    in_flight = (parts["w_in"][0],)
    grad_x = grad_x_after(in_flight)
    finish("ffn", (marks["scan_bwd"],), in_flight)
    stored = [_as_stored(n, small_g[n]) for n in SMALL]
    slots = _exchange_rows(_pack_rows(stored), name="exchange_small")
    summed = _unpack_rows(_sum_slots(slots, name="sum_small", after=in_flight), [a.shape for a in stored])
    for n, g in zip(SMALL, summed):
        g = _as_stored(n, g)
        if n == "b_gate":
            g = lax.dynamic_slice(g, (0, chip * ncol), (2, ncol))
        grads[n] = g.reshape(given[n].shape)
    packed = [_pack_rows([_as_stored(n, src[n]) for n in SMALL]) for src in
              (given, grads, {n: given["m_" + n] for n in SMALL}, {n: given["v_" + n] for n in SMALL})]
    shapes = [_as_stored(n, given[n]).shape for n in SMALL]
    small_out = _adamw(*packed, name="adamw_small")
    for out, vec in zip((delta, new_m, new_v), small_out):
        out.update((n, _as_stored(n, a)) for n, a in zip(SMALL, _unpack_rows(vec, shapes)))
    behind = [delta[n] for n in groups["ffn"]] + [small_out[0], grad_x]
    finish("w_in", tuple(behind), ())

    return (loss, grad_x.reshape(x.shape), *[grads[n] for n in WEIGHTS], *[delta[n] for n in WEIGHTS],
            *[new_m[n] for n in WEIGHTS], *[new_v[n] for n in WEIGHTS])
```

```python
import math

import jax
import jax.numpy as jnp
from jax import lax
from jax.experimental import pallas as pl
from jax.experimental.pallas import tpu as pltpu
from jax.experimental.pallas import tpu_sc as plsc

F32 = jnp.float32
BF16 = jnp.bfloat16
MESH = pl.DeviceIdType.MESH

D_MODEL = 1024
SEQ = 2048
HEAD_DIM = 64
ATTN_HEADS = 8
DILATIONS = (1, 4, 16)
ATTN_WIDTH = ATTN_HEADS * HEAD_DIM
QKV_WIDTH = 3 * ATTN_WIDTH
BLOCK = 128
ROPE_THETA = 10000.0
NEG_INF = -1e30
SSM_GROUP = 16
SSM_GROUPS = 32
SSM_WIDTH = 512
SSM_STATE = 64
SSM_LANES = SSM_GROUPS * SSM_STATE
SCAN_CHUNKS = 8
SCAN_STEPS = SEQ // SCAN_CHUNKS
IN_WIDTH = 3 * QKV_WIDTH + SSM_WIDTH + 2 * D_MODEL
D_FF = 2816
N_CHIPS = 4
N_DEV = 8
DN_ALPHA = 2.0 ** 0.25
LN_EPS = 1e-5
ADAM_LR = 0.001
ADAM_B1 = 0.9
ADAM_B2 = 0.999
ADAM_EPS = 1e-08
ADAM_WD = 0.01
ADAM_STEP = 10
GELU_C = math.sqrt(2.0 / math.pi)
GELU_K = 0.044715

VMEM_LIMIT_BYTES = 56 * 1024 * 1024


def _sds(shape, dtype=F32):
    return jax.ShapeDtypeStruct(tuple(shape), dtype)


def _cp(semantics=None):
    return pltpu.CompilerParams(dimension_semantics=semantics, vmem_limit_bytes=VMEM_LIMIT_BYTES)


HBM_OPERAND = pl.BlockSpec(memory_space=pl.ANY)


def _matmul(a, b, *, grid, a_spec, b_spec, o_spec, out_shape, dims, k_axis=None, name, after=()):
    nk = grid[k_axis] if k_axis is not None else 1
    o_block = tuple(d for d in o_spec.block_shape if d is not None)
    n_after = len(after)

    def body(a_ref, b_ref, *rest):
        o_ref, acc = rest[n_after], rest[n_after + 1:]
        part = lax.dot_general(a_ref[...].astype(BF16), b_ref[...].astype(BF16),
                               (((dims[0],), (dims[1],)), ((), ())), preferred_element_type=F32)
        if k_axis is None:
            o_ref[...] = part.astype(o_ref.dtype)
        else:
            k = pl.program_id(k_axis)

            @pl.when(k == 0)
            def _():
                acc[0][...] = part

            @pl.when(k > 0)
            def _():
                acc[0][...] += part

            @pl.when(k == nk - 1)
            def _():
                o_ref[...] = acc[0][...].astype(o_ref.dtype)

    sem = tuple("arbitrary" if ax == k_axis else "parallel" for ax in range(len(grid)))
    return pl.pallas_call(
        body, grid=grid, in_specs=[a_spec, b_spec] + [HBM_OPERAND] * n_after, out_specs=o_spec, out_shape=out_shape,
        scratch_shapes=[pltpu.VMEM(o_block, F32)] if k_axis is not None else [],
        compiler_params=_cp(sem), name=name)(a, b, *after)


def _mm_cols(a, wg, *, tm, name, out_dtype=F32):
    m, k = a.shape
    ns = wg.shape[2]
    return _matmul(a, wg, grid=(m // tm, N_CHIPS),
                   a_spec=pl.BlockSpec((tm, k), lambda i, s: (i, 0)),
                   b_spec=pl.BlockSpec((None, k, ns), lambda i, s: (s, 0, 0)),
                   o_spec=pl.BlockSpec((tm, ns), lambda i, s: (i, s)),
                   out_shape=_sds((m, N_CHIPS * ns), out_dtype), dims=(1, 0), name=name)


def _mm_cols_nt(dy, wg, *, tm, name, out_dtype=F32, after=()):
    k, ns = wg.shape[1], wg.shape[2]
    m = dy.shape[0]
    a_spec = pl.BlockSpec((tm, ns), lambda i, s: (i, s))
    return _matmul(dy, wg, grid=(m // tm, N_CHIPS), a_spec=a_spec,
                   b_spec=pl.BlockSpec((None, k, ns), lambda i, s: (s, 0, 0)),
                   o_spec=pl.BlockSpec((tm, k), lambda i, s: (i, 0)),
                   out_shape=_sds((m, k), out_dtype), dims=(1, 1), k_axis=1, name=name, after=after)


def _mm_cols_tn(a, dy, *, ns, name, after=()):
    m, k = a.shape
    return _matmul(a, dy, grid=(N_CHIPS,), a_spec=pl.BlockSpec((m, k), lambda s: (0, 0)),
                   b_spec=pl.BlockSpec((m, ns), lambda s: (0, s)),
                   o_spec=pl.BlockSpec((None, k, ns), lambda s: (s, 0, 0)),
                   out_shape=_sds((N_CHIPS, k, ns), BF16), dims=(0, 0), name=name, after=after)


def _mm_plain(a, b, *, tm, tn, name, out_dtype=F32, dims=(1, 0), tk=None):
    m = a.shape[1 - dims[0]]
    kk = a.shape[dims[0]]
    n = b.shape[1 - dims[1]]
    tk = kk if tk is None else tk
    nk = kk // tk

    def a_idx(i, j, k):
        return (i, k) if dims[0] == 1 else (k, i)

    def b_idx(i, j, k):
        return (k, j) if dims[1] == 0 else (j, k)

    a_blk = (tm, tk) if dims[0] == 1 else (tk, tm)
    b_blk = (tk, tn) if dims[1] == 0 else (tn, tk)
    return _matmul(a, b, grid=(m // tm, n // tn, nk),
                   a_spec=pl.BlockSpec(a_blk, a_idx), b_spec=pl.BlockSpec(b_blk, b_idx),
                   o_spec=pl.BlockSpec((tm, tn), lambda i, j, k: (i, j)),
                   out_shape=_sds((m, n), out_dtype), dims=dims, k_axis=2 if nk > 1 else None, name=name)


def _rowwise(fn, tiled, full, outs, accs=(), *, tm, name, after=()):
    args, in_specs = [], []
    for t in tiled:
        if isinstance(t, tuple):
            arr, w, cb = t
            in_specs.append(pl.BlockSpec((tm, w), lambda i, cb=cb: (i, cb)))
        else:
            arr = t
            in_specs.append(pl.BlockSpec((tm, arr.shape[1]), lambda i: (i, 0)))
        args.append(arr)
    rows = args[0].shape[0]
    for f in full:
        in_specs.append(pl.BlockSpec(f.shape, lambda i, nd=f.ndim: (0,) * nd))
        args.append(f)
    out_specs = [pl.BlockSpec((tm, o.shape[1]), lambda i: (i, 0)) for o in outs]
    out_specs += [pl.BlockSpec(a.shape, lambda i, nd=len(a.shape): (0,) * nd) for a in accs]
    n_in, n_out = len(args), len(outs)
    in_specs += [HBM_OPERAND] * len(after)
    first_out = n_in + len(after)

    def body(*refs):
        res = fn(*[r[...] for r in refs[:n_in]])
        res = res if isinstance(res, (tuple, list)) else (res,)
        for r, v in zip(refs[first_out:first_out + n_out], res[:n_out]):
            r[...] = v.astype(r.dtype)
        i = pl.program_id(0)
        for r, v in zip(refs[first_out + n_out:], res[n_out:]):
            @pl.when(i == 0)
            def _(r=r, v=v):
                r[...] = v

            @pl.when(i > 0)
            def _(r=r, v=v):
                r[...] += v

    res = pl.pallas_call(
        body, grid=(rows // tm,), in_specs=in_specs, out_specs=out_specs, out_shape=list(outs) + list(accs),
        compiler_params=_cp(("arbitrary",) if accs else ("parallel",)), name=name)(*args, *after)
    return res


def _colsum(v):
    return jnp.sum(v, axis=0, keepdims=True)


def _ln_stats(z):
    mu = jnp.mean(z, axis=-1, keepdims=True)
    zc = z - mu
    var = jnp.mean(zc * zc, axis=-1, keepdims=True)
    rstd = lax.rsqrt(var + LN_EPS)
    return zc * rstd, rstd


def _ln_bwd(dy, xhat, rstd, g):
    dxh = dy * g
    m1 = jnp.mean(dxh, axis=-1, keepdims=True)
    m2 = jnp.mean(dxh * xhat, axis=-1, keepdims=True)
    return rstd * (dxh - m1 - xhat * m2)


def _swap_halves(t):
    w = t.shape[-1]
    lane = lax.broadcasted_iota(jnp.int32, t.shape, t.ndim - 1)
    return jnp.where((lane % HEAD_DIM) < HEAD_DIM // 2, pltpu.roll(t, w - HEAD_DIM // 2, t.ndim - 1),
                     pltpu.roll(t, HEAD_DIM // 2, t.ndim - 1))


PHASES = max(DILATIONS)
PAIR = 2 * HEAD_DIM
UNITS = SEQ // BLOCK
UNIT_BATCH = 8
ROPE_ROWS = 256


def _to_phase_rows(t):
    return t.reshape(SEQ // PHASES, PHASES, t.shape[1]).transpose(1, 0, 2).reshape(t.shape)


def _reorder_rows(arr, plus=None, *, to_phase, name, scale=1.0):
    def body(*refs):
        o_ref = refs[-1]
        for rho in range(PHASES):
            phase = pl.ds(rho * BLOCK, BLOCK)
            strided = pl.ds(rho, BLOCK, stride=PHASES)
            src, dst = (strided, phase) if to_phase else (phase, strided)
            val = refs[0][src, :]
            if scale != 1.0:
                val = val * scale
            if plus is not None:
                val = val + refs[1][src, :]
            o_ref[dst, :] = val

    spec = pl.BlockSpec((SEQ, BLOCK), lambda j: (0, j))
    ins = [arr] if plus is None else [arr, plus]
    return pl.pallas_call(body, grid=(arr.shape[1] // BLOCK,), in_specs=[spec] * len(ins), out_specs=spec,
                          out_shape=_sds(arr.shape), compiler_params=_cp(("parallel",)), name=name)(*ins)


def _rope(t, cf, ss):
    return t * cf + _swap_halves(t) * ss


def _rope_transposed(d, cf, ss):
    return d * cf + _swap_halves(d * ss)


def _unit_pieces(u, dil):
    pieces, length = PHASES // dil, 8 * dil
    if dil == 1:
        rho, i = 0, u
    elif dil == PHASES:
        rho, i = u, 0
    else:
        rho, i = jnp.bitwise_and(u, dil - 1), jnp.right_shift(u, dil.bit_length() - 1)
    before = jnp.maximum(i - 1, 0)
    cur = [pl.multiple_of((rho + dil * k) * BLOCK + length * i, 8) for k in range(pieces)]
    prev = [pl.multiple_of((rho + dil * k) * BLOCK + length * before, 8) for k in range(pieces)]
    return i, cur, prev


def _load_tile(ref, starts, dil):
    return jnp.concatenate([ref[pl.ds(st, 8 * dil), :] for st in starts], axis=0)


def _store_tile(ref, starts, dil, val, head=None, accumulate=False):
    length = 8 * dil
    lanes = slice(None) if head is None else pl.ds(head * HEAD_DIM, HEAD_DIM)
    cols = slice(None) if head is None else slice(head * HEAD_DIM, (head + 1) * HEAD_DIM)
    for k, st in enumerate(starts):
        piece = val[k * length:(k + 1) * length, cols]
        if accumulate:
            ref[pl.ds(st, length), lanes] += piece
        else:
            ref[pl.ds(st, length), lanes] = piece


def _tile_position(idx, dil):
    pieces, length = PHASES // dil, 8 * dil
    return pieces * jnp.bitwise_and(idx, length - 1) + jnp.right_shift(idx, length.bit_length() - 1)


def _band_mask(i, dil):
    row = lax.broadcasted_iota(jnp.int32, (BLOCK, 2 * BLOCK), 0)
    col = lax.broadcasted_iota(jnp.int32, (BLOCK, 2 * BLOCK), 1)
    key_pos = _tile_position(jnp.bitwise_and(col, BLOCK - 1), dil) + jnp.where(col >= BLOCK, 0, -BLOCK)
    dist = _tile_position(row, dil) - key_pos
    return (dist >= 0) & (dist <= BLOCK) & ((col >= BLOCK) | (i > 0))


def _causal_mask():
    row = lax.broadcasted_iota(jnp.int32, (BLOCK, BLOCK), 0)
    col = lax.broadcasted_iota(jnp.int32, (BLOCK, BLOCK), 1)
    return row >= col


def _pair_views(col0):
    return [pl.BlockSpec((SEQ, PAIR), lambda hp, g=g: (0, col0 // PAIR + g * (ATTN_WIDTH // PAIR) + hp))
            for g in range(len(DILATIONS))]


def _rotate(in_refs, out_refs, cf_ref, ss_ref, scale):
    def step(t, carry):
        rows = pl.ds(pl.multiple_of(t * ROPE_ROWS, ROPE_ROWS), ROPE_ROWS)
        cf, ss = cf_ref[rows, :] * scale, ss_ref[rows, :] * scale
        for i_ref, o_ref in zip(in_refs, out_refs):
            o_ref[rows, :] = _rope(i_ref[rows, :], cf, ss)
        return carry

    lax.fori_loop(0, SEQ // ROPE_ROWS, step, 0)


def _attention_fwd(proj, cos_f, sin_s):
    ng = len(DILATIONS)

    def body(*refs):
        q_refs, k_refs, v_refs = refs[:ng], refs[ng:2 * ng], refs[2 * ng:3 * ng]
        cf_ref, ss_ref, attn_ref, lse_ref = refs[3 * ng:3 * ng + 4]
        scratch = refs[3 * ng + 4:]
        qr_refs, kr_refs = scratch[:ng], scratch[ng:]
        _rotate(q_refs, qr_refs, cf_ref, ss_ref, 1.0 / math.sqrt(HEAD_DIM))
        _rotate(k_refs, kr_refs, cf_ref, ss_ref, 1.0)
        first = lax.broadcasted_iota(jnp.int32, (BLOCK, PAIR), 1) < HEAD_DIM
        for g, dil in enumerate(DILATIONS):
            two_blocks = SEQ // dil > BLOCK

            def units(t, carry, g=g, dil=dil, two_blocks=two_blocks):
                picked = [_unit_pieces(t * UNIT_BATCH + j, dil) for j in range(UNIT_BATCH)]

                def tiles(ref, with_prev=False):
                    if with_prev and two_blocks:
                        return jnp.stack([jnp.concatenate([_load_tile(ref, prev, dil), _load_tile(ref, rows, dil)],
                                                          axis=0) for _, rows, prev in picked])
                    return jnp.stack([_load_tile(ref, rows, dil) for _, rows, _ in picked])

                qq = tiles(qr_refs[g]).astype(BF16)
                kk = tiles(kr_refs[g], True).astype(BF16)
                vv = tiles(v_refs[g], True).astype(BF16)
                if two_blocks:
                    valid = jnp.stack([_band_mask(i, dil) for i, _, _ in picked])
                else:
                    valid = _causal_mask()[None]
                mine = first[None]
                zero = jnp.zeros_like(qq)
                outs, lses = [], []
                for qh in (jnp.where(mine, qq, zero), jnp.where(mine, zero, qq)):
                    s = jnp.einsum("pqd,pkd->pqk", qh, kk, preferred_element_type=F32)
                    s = jnp.where(valid, s, NEG_INF)
                    m = jnp.max(s, axis=-1, keepdims=True)
                    p = jnp.exp(s - m)
                    l = jnp.sum(p, axis=-1, keepdims=True)
                    outs.append(jnp.einsum("pqk,pkd->pqd", p.astype(BF16), vv, preferred_element_type=F32) * (1.0 / l))
                    lses.append(m + jnp.log(l))
                o = jnp.where(mine, outs[0], outs[1])
                lse = jnp.where(mine, lses[0], lses[1])
                if g > 0:
                    lse_old = tiles(lse_ref)
                    m = jnp.maximum(lse_old, lse)
                    lse_new = m + jnp.log(jnp.exp(lse_old - m) + jnp.exp(lse - m))
                    o = tiles(attn_ref) * jnp.exp(lse_old - lse_new) + o * jnp.exp(lse - lse_new)
                    lse = lse_new
                for j, (_, rows, _) in enumerate(picked):
                    _store_tile(attn_ref, rows, dil, o[j])
                    _store_tile(lse_ref, rows, dil, lse[j])
                return carry

            lax.fori_loop(0, UNITS // UNIT_BATCH, units, 0)

    whole = pl.BlockSpec((SEQ, PAIR), lambda hp: (0, 0))
    out = pl.BlockSpec((SEQ, PAIR), lambda hp: (0, hp))
    return pl.pallas_call(
        body, grid=(ATTN_WIDTH // PAIR,),
        in_specs=_pair_views(0) + _pair_views(QKV_WIDTH) + _pair_views(2 * QKV_WIDTH) + [whole, whole],
        out_specs=[out, out], out_shape=[_sds((SEQ, ATTN_WIDTH)), _sds((SEQ, ATTN_WIDTH))],
        scratch_shapes=[pltpu.VMEM((SEQ, PAIR), F32)] * (2 * ng),
        compiler_params=_cp(("parallel",)), name="attention_fwd")(*([proj] * (3 * ng)), cos_f, sin_s)


def _attention_bwd(g, proj, cos_f, sin_s, d_attn, attn, lse):
    dil = DILATIONS[g]
    two_blocks = SEQ // dil > BLOCK

    def body(q_ref, k_ref, v_ref, cf_ref, ss_ref, do_ref, o_ref, lse_ref, dq_out, dk_out, dv_out,
             qr_ref, kr_ref, dq_acc, dk_acc, dv_acc):
        _rotate([q_ref], [qr_ref], cf_ref, ss_ref, 1.0 / math.sqrt(HEAD_DIM))
        _rotate([k_ref], [kr_ref], cf_ref, ss_ref, 1.0)
        dk_acc[...] = jnp.zeros_like(dk_acc)
        dv_acc[...] = jnp.zeros_like(dv_acc)
        nk = 2 * BLOCK if two_blocks else BLOCK
        first = lax.broadcasted_iota(jnp.int32, (BLOCK, PAIR), 1) < HEAD_DIM
        first_k = lax.broadcasted_iota(jnp.int32, (nk, PAIR), 1) < HEAD_DIM

        def units(t, carry):
            picked = [_unit_pieces(t * UNIT_BATCH + j, dil) for j in range(UNIT_BATCH)]

            def tiles(ref, with_prev=False):
                if with_prev and two_blocks:
                    return jnp.stack([jnp.concatenate([_load_tile(ref, prev, dil), _load_tile(ref, rows, dil)], axis=0)
                                      for _, rows, prev in picked])
                return jnp.stack([_load_tile(ref, rows, dil) for _, rows, _ in picked])

            qq = tiles(qr_ref).astype(BF16)
            kk = tiles(kr_ref, True).astype(BF16)
            vv = tiles(v_ref, True).astype(BF16)
            dof = tiles(do_ref)
            dd = dof * tiles(o_ref)
            lse3 = tiles(lse_ref)
            dob = dof.astype(BF16)
            if two_blocks:
                valid = jnp.stack([_band_mask(i, dil) for i, _, _ in picked])
            else:
                valid = _causal_mask()[None]
            zq, zf = jnp.zeros_like(qq), jnp.zeros_like(dd)
            dqs, dks, dvs = [], [], []
            for head in range(2):
                mine = first[None] if head == 0 else jnp.logical_not(first)[None]
                delta = jnp.sum(jnp.where(mine, dd, zf), axis=-1, keepdims=True)
                lse_h = lse3[:, :, head * HEAD_DIM:head * HEAD_DIM + 1]
                s = jnp.einsum("pqd,pkd->pqk", jnp.where(mine, qq, zq), kk, preferred_element_type=F32)
                p = jnp.where(valid, jnp.exp(s - lse_h), 0.0)
                dp = jnp.einsum("pqd,pkd->pqk", jnp.where(mine, dob, zq), vv, preferred_element_type=F32)
                ds = (p * (dp - delta)).astype(BF16)
                dqs.append(jnp.einsum("pqk,pkd->pqd", ds, kk, preferred_element_type=F32))
                dks.append(jnp.einsum("pqk,pqd->pkd", ds, qq, preferred_element_type=F32))
                dvs.append(jnp.einsum("pqk,pqd->pkd", p.astype(BF16), dob, preferred_element_type=F32))
            dq = jnp.where(first[None], dqs[0], dqs[1])
            dk = jnp.where(first_k[None], dks[0], dks[1])
            dv = jnp.where(first_k[None], dvs[0], dvs[1])
            for j, (_, rows, prev) in enumerate(picked):
                _store_tile(dq_acc, rows, dil, dq[j])
                _store_tile(dk_acc, rows, dil, dk[j, nk - BLOCK:], accumulate=True)
                _store_tile(dv_acc, rows, dil, dv[j, nk - BLOCK:], accumulate=True)
                if two_blocks:
                    _store_tile(dk_acc, prev, dil, dk[j, :BLOCK], accumulate=True)
                    _store_tile(dv_acc, prev, dil, dv[j, :BLOCK], accumulate=True)
            return carry

        lax.fori_loop(0, UNITS // UNIT_BATCH, units, 0)

        def finish(t, carry):
            rows = pl.ds(pl.multiple_of(t * ROPE_ROWS, ROPE_ROWS), ROPE_ROWS)
            cf, ss = cf_ref[rows, :], ss_ref[rows, :]
            dq = dq_acc[rows, :] * (1.0 / math.sqrt(HEAD_DIM))
            dq_out[rows, :] = _rope_transposed(dq, cf, ss).astype(BF16)
            dk_out[rows, :] = _rope_transposed(dk_acc[rows, :], cf, ss).astype(BF16)
            dv_out[rows, :] = dv_acc[rows, :].astype(BF16)
            return carry

        lax.fori_loop(0, SEQ // ROPE_ROWS, finish, 0)

    whole = pl.BlockSpec((SEQ, PAIR), lambda hp: (0, 0))
    pair = pl.BlockSpec((SEQ, PAIR), lambda hp: (0, hp))
    views = [_pair_views(col0)[g] for col0 in (0, QKV_WIDTH, 2 * QKV_WIDTH)]
    return pl.pallas_call(
        body, grid=(ATTN_WIDTH // PAIR,), in_specs=views + [whole, whole, pair, pair, pair],
        out_specs=[pair, pair, pair], out_shape=[_sds((SEQ, ATTN_WIDTH), BF16)] * 3,
        scratch_shapes=[pltpu.VMEM((SEQ, PAIR), F32)] * 5,
        compiler_params=_cp(("parallel",)), name=f"attention_bwd_{g}")(proj, proj, proj, cos_f, sin_s, d_attn, attn, lse)


def _cmul(ar, ai, br, bi):
    return ar * br - ai * bi, ar * bi + ai * br


def _pow256(ar, ai):
    for _ in range(8):
        ar, ai = _cmul(ar, ai, ar, ai)
    return ar, ai


def _chunk_carries(first_r, first_i, pr, pi, reverse):
    rows = lax.broadcasted_iota(jnp.int32, first_r.shape, 0)
    out_r = jnp.zeros_like(first_r)
    out_i = jnp.zeros_like(first_i)
    hr = jnp.zeros_like(first_r[0:1])
    hi = jnp.zeros_like(hr)
    order = range(SCAN_CHUNKS - 1, -1, -1) if reverse else range(SCAN_CHUNKS)
    for c in order:
        out_r = jnp.where(rows == c, hr, out_r)
        out_i = jnp.where(rows == c, hi, out_i)
        tr, ti = _cmul(pr[0:1], pi[0:1], hr, hi)
        hr = first_r[c:c + 1] + tr
        hi = first_i[c:c + 1] + ti
    return out_r, out_i


def _tile(j):
    return pl.ds(pl.multiple_of(j * SCAN_CHUNKS, SCAN_CHUNKS), SCAN_CHUNKS)


def _to_scan_rows(t):
    per = SCAN_STEPS // PHASES
    return t.reshape(PHASES, SCAN_CHUNKS, per, t.shape[1]).transpose(2, 0, 1, 3).reshape(t.shape)


def _from_scan_rows(t):
    per = SCAN_STEPS // PHASES
    return t.reshape(per, PHASES, SCAN_CHUNKS, t.shape[1]).transpose(1, 2, 0, 3).reshape(t.shape)


def _scan_in_place(hr_ref, hi_ref, a_r, a_i):
    def local(j, carry):
        tr, ti = _cmul(a_r, a_i, carry[0], carry[1])
        nr = tr + hr_ref[_tile(j), :]
        ni = ti + hi_ref[_tile(j), :]
        hr_ref[_tile(j), :] = nr
        hi_ref[_tile(j), :] = ni
        return nr, ni

    zero = jnp.zeros_like(a_r)
    last_r, last_i = lax.fori_loop(0, SCAN_STEPS, local, (zero, zero), unroll=4)
    pr, pi = _pow256(a_r, a_i)
    er, ei = _chunk_carries(last_r, last_i, pr, pi, reverse=False)

    def fix(j, carry):
        tr, ti = _cmul(carry[0], carry[1], er, ei)
        hr_ref[_tile(j), :] += tr
        hi_ref[_tile(j), :] += ti
        return _cmul(carry[0], carry[1], a_r, a_i)

    lax.fori_loop(0, SCAN_STEPS, fix, (a_r, a_i), unroll=4)
    return er, ei


def _reverse_scan_in_place(lr_ref, li_ref, hr_ref, hi_ref, er, ei, a_r, a_i):
    def local(t, carry):
        j = SCAN_STEPS - 1 - t
        tr, ti = _cmul(a_r, a_i, carry[0], carry[1])
        nr = tr + lr_ref[_tile(j), :]
        ni = ti + li_ref[_tile(j), :]
        lr_ref[_tile(j), :] = nr
        li_ref[_tile(j), :] = ni
        return nr, ni

    zero = jnp.zeros_like(a_r)
    first_r, first_i = lax.fori_loop(0, SCAN_STEPS, local, (zero, zero), unroll=4)
    pr, pi = _pow256(a_r, a_i)
    nxt_r, nxt_i = _chunk_carries(first_r, first_i, pr, pi, reverse=True)

    def accumulate(lam_r, lam_i, hp_r, hp_i, acc):
        return (acc[0] + lam_r * hp_r + lam_i * hp_i, acc[1] + lam_i * hp_r - lam_r * hp_i)

    def fix(t, carry):
        qr, qi, acc_r, acc_i = carry
        j = SCAN_STEPS - 1 - t
        tr, ti = _cmul(qr, qi, nxt_r, nxt_i)
        lam_r = lr_ref[_tile(j), :] + tr
        lam_i = li_ref[_tile(j), :] + ti
        lr_ref[_tile(j), :] = lam_r
        li_ref[_tile(j), :] = lam_i
        acc_r, acc_i = accumulate(lam_r, lam_i, hr_ref[_tile(j - 1), :], hi_ref[_tile(j - 1), :], (acc_r, acc_i))
        qr, qi = _cmul(qr, qi, a_r, a_i)
        return qr, qi, acc_r, acc_i

    qr, qi, acc_r, acc_i = lax.fori_loop(0, SCAN_STEPS - 1, fix, (a_r, a_i, zero, zero), unroll=4)
    tr, ti = _cmul(qr, qi, nxt_r, nxt_i)
    lam_r = lr_ref[_tile(0), :] + tr
    lam_i = li_ref[_tile(0), :] + ti
    lr_ref[_tile(0), :] = lam_r
    li_ref[_tile(0), :] = lam_i
    acc_r, acc_i = accumulate(lam_r, lam_i, er, ei, (acc_r, acc_i))
    return jnp.sum(acc_r, axis=0, keepdims=True), jnp.sum(acc_i, axis=0, keepdims=True)


def _rope_tables():
    half = HEAD_DIM // 2
    inv_freq = ROPE_THETA ** (-jnp.arange(half, dtype=F32) / half)
    ang = jnp.arange(SEQ, dtype=F32)[:, None] * inv_freq[None, :]
    cos, sin = jnp.cos(ang), jnp.sin(ang)
    cos_f = jnp.concatenate([cos, cos, cos, cos], axis=1)
    sin_s = jnp.concatenate([-sin, sin, -sin, sin], axis=1)
    return cos_f, sin_s


def _ssm_discretise(a_re, a_im, log_dt, b_re, b_im):
    lam = lax.complex(a_re, a_im)
    dt = jnp.exp(log_dt)[:, None]
    a_bar = jnp.exp(lam * dt)
    b_bar = ((a_bar - 1.0) / lam)[..., None] * lax.complex(b_re, b_im)
    return a_bar.real, a_bar.imag, b_bar.real, b_bar.imag


SSM_SLABS = 4
SLAB_GROUPS = SSM_GROUPS // SSM_SLABS
SLAB_IN = SSM_WIDTH // SSM_SLABS
SLAB_STATE = SSM_LANES // SSM_SLABS


def _slab_block_diag(blocks):
    _, r, c = blocks.shape
    eye = jnp.eye(SLAB_GROUPS, dtype=blocks.dtype)
    b5 = blocks.reshape(SSM_SLABS, SLAB_GROUPS, r, 1, c) * eye[None, :, None, :, None]
    return b5.reshape(SSM_SLABS, SLAB_GROUPS * r, SLAB_GROUPS * c)


def _diag_blocks(a, b):
    ra, cb = a.shape[1], b.shape[1]
    wa, wb = ra // SLAB_GROUPS, cb // SLAB_GROUPS
    d = lax.dot_general(a, b, (((0,), (0,)), ((), ())), preferred_element_type=F32)
    row_g = jnp.right_shift(lax.broadcasted_iota(jnp.int32, (ra, cb), 0), wa.bit_length() - 1)
    col_g = jnp.right_shift(lax.broadcasted_iota(jnp.int32, (ra, cb), 1), wb.bit_length() - 1)
    d = jnp.where(row_g == col_g, d, 0.0)
    fold = (jnp.bitwise_and(lax.broadcasted_iota(jnp.int32, (cb, wb), 0), wb - 1)
            == lax.broadcasted_iota(jnp.int32, (cb, wb), 1)).astype(F32)
    return jnp.dot(d, fold, preferred_element_type=F32, precision=lax.Precision.HIGHEST)


def _slab_specs():
    tok = pl.BlockSpec((SEQ, SLAB_IN), lambda j: (0, j))
    state = pl.BlockSpec((SEQ, SLAB_STATE), lambda j: (0, j))
    b_in = pl.BlockSpec((None, SLAB_IN, SLAB_STATE), lambda j: (j, 0, 0))
    c_out = pl.BlockSpec((None, SLAB_STATE, SLAB_IN), lambda j: (j, 0, 0))
    vec = pl.BlockSpec((1, SLAB_STATE), lambda j: (0, j))
    ent = pl.BlockSpec((SCAN_CHUNKS, SLAB_STATE), lambda j: (0, j))
    return tok, state, b_in, c_out, vec, ent


def _ssm_forward(u, b_in_r, b_in_i, c_out_r, c_out_ni, a_r, a_i):
    def body(u_ref, br_ref, bi_ref, cr_ref, ci_ref, ar_ref, ai_ref, y_ref, hr_ref, hi_ref, er_ref, ei_ref):
        uu = u_ref[...]
        hr_ref[...] = jnp.dot(uu, br_ref[...], preferred_element_type=F32)
        hi_ref[...] = jnp.dot(uu, bi_ref[...], preferred_element_type=F32)
        a_re = jnp.broadcast_to(ar_ref[...], (SCAN_CHUNKS, SLAB_STATE))
        a_im = jnp.broadcast_to(ai_ref[...], (SCAN_CHUNKS, SLAB_STATE))
        er_ref[...], ei_ref[...] = _scan_in_place(hr_ref, hi_ref, a_re, a_im)
        y_ref[...] = (jnp.dot(hr_ref[...].astype(BF16), cr_ref[...], preferred_element_type=F32)
                      + jnp.dot(hi_ref[...].astype(BF16), ci_ref[...], preferred_element_type=F32))

    tok, state, b_in, c_out, vec, ent = _slab_specs()
    return pl.pallas_call(
        body, grid=(SSM_SLABS,), in_specs=[tok, b_in, b_in, c_out, c_out, vec, vec],
        out_specs=[tok, state, state, ent, ent],
        out_shape=[_sds((SEQ, SSM_WIDTH)), _sds((SEQ, SSM_LANES)), _sds((SEQ, SSM_LANES)),
                   _sds((SCAN_CHUNKS, SSM_LANES)), _sds((SCAN_CHUNKS, SSM_LANES))],
        compiler_params=_cp(("parallel",)), name="ssm_forward")(u, b_in_r, b_in_i, c_out_r, c_out_ni, a_r, a_i)


def _ssm_backward(d_y, d_u_skip, u, h_r, h_i, e_r, e_i, b_in_r, b_in_i, c_out_r, c_out_ni, a_r, a_i):
    def body(dy_ref, skip_ref, u_ref, hr_ref, hi_ref, er_ref, ei_ref, br_ref, bi_ref, cr_ref, ci_ref, ar_ref, ai_ref,
             du_ref, dar_ref, dai_ref, dcr_ref, dci_ref, dbr_ref, dbi_ref, lr_ref, li_ref):
        dy = dy_ref[...]
        lr_ref[...] = _dot_nt(dy, cr_ref[...])
        li_ref[...] = _dot_nt(dy, ci_ref[...])
        a_re = jnp.broadcast_to(ar_ref[...], (SCAN_CHUNKS, SLAB_STATE))
        a_im = -jnp.broadcast_to(ai_ref[...], (SCAN_CHUNKS, SLAB_STATE))
        dar_ref[...], dai_ref[...] = _reverse_scan_in_place(lr_ref, li_ref, hr_ref, hi_ref, er_ref[...], ei_ref[...],
                                                            a_re, a_im)
        dcr_ref[...] = _diag_blocks(hr_ref[...].astype(BF16), dy)
        dci_ref[...] = _diag_blocks(hi_ref[...].astype(BF16), dy)
        lam_r, lam_i = lr_ref[...].astype(BF16), li_ref[...].astype(BF16)
        uu = u_ref[...]
        dbr_ref[...] = _diag_blocks(uu, lam_r)
        dbi_ref[...] = _diag_blocks(uu, lam_i)
        du = skip_ref[...] + _dot_nt(lam_r, br_ref[...]) + _dot_nt(lam_i, bi_ref[...])
        du_ref[...] = du.astype(BF16)

    tok, state, b_in, c_out, vec, ent = _slab_specs()
    dc = pl.BlockSpec((SLAB_STATE, SSM_GROUP), lambda j: (j, 0))
    db = pl.BlockSpec((SLAB_IN, SSM_STATE), lambda j: (j, 0))
    return pl.pallas_call(
        body, grid=(SSM_SLABS,), in_specs=[tok, tok, tok, state, state, ent, ent, b_in, b_in, c_out, c_out, vec, vec],
        out_specs=[tok, vec, vec, dc, dc, db, db],
        out_shape=[_sds((SEQ, SSM_WIDTH), BF16), _sds((1, SSM_LANES)), _sds((1, SSM_LANES)),
                   _sds((SSM_LANES, SSM_GROUP)), _sds((SSM_LANES, SSM_GROUP)),
                   _sds((SSM_WIDTH, SSM_STATE)), _sds((SSM_WIDTH, SSM_STATE))],
        scratch_shapes=[pltpu.VMEM((SEQ, SLAB_STATE), F32)] * 2,
        compiler_params=_cp(("parallel",)), name="ssm_backward")(
            d_y, d_u_skip, u, h_r, h_i, e_r, e_i, b_in_r, b_in_i, c_out_r, c_out_ni, a_r, a_i)


FF_ROWS = 1024
FF_SHARD = D_FF // N_CHIPS


def _dot_nt(a, b):
    return lax.dot_general(a, b, (((1,), (1,)), ((), ())), preferred_element_type=F32)


def _ffn_up(h, w_gate_t, w_up_t):
    def body(h_ref, wg_ref, wu_ref, a_ref, b_ref, act_ref):
        hb = h_ref[...].astype(BF16)
        a = _dot_nt(hb, wg_ref[...])
        b = _dot_nt(hb, wu_ref[...])
        a_ref[...] = a
        b_ref[...] = b
        act_ref[...] = (a * jax.nn.sigmoid(a) * b).astype(BF16)

    w_spec = pl.BlockSpec((None, FF_SHARD, D_MODEL), lambda i, k: (k, 0, 0))
    o_spec = pl.BlockSpec((None, FF_ROWS, FF_SHARD), lambda i, k: (k, i, 0))
    shape = (N_CHIPS, SEQ, FF_SHARD)
    return pl.pallas_call(
        body, grid=(SEQ // FF_ROWS, N_CHIPS),
        in_specs=[pl.BlockSpec((FF_ROWS, D_MODEL), lambda i, k: (i, 0)), w_spec, w_spec],
        out_specs=[o_spec, o_spec, o_spec], out_shape=[_sds(shape), _sds(shape), _sds(shape, BF16)],
        compiler_params=_cp(("parallel", "parallel")), name="ffn_up")(h, w_gate_t, w_up_t)


def _ffn_down_bwd(dz, w_down, a, b):
    def body(dz_ref, wd_ref, a_ref, b_ref, da_ref, db_ref):
        d_act = _dot_nt(dz_ref[...].astype(BF16), wd_ref[...])
        av = a_ref[...]
        sg = jax.nn.sigmoid(av)
        da_ref[...] = (d_act * b_ref[...] * sg * (1.0 + av * (1.0 - sg))).astype(BF16)
        db_ref[...] = (d_act * av * sg).astype(BF16)

    t_spec = pl.BlockSpec((None, FF_ROWS, FF_SHARD), lambda i, k: (k, i, 0))
    shape = (N_CHIPS, SEQ, FF_SHARD)
    return pl.pallas_call(
        body, grid=(SEQ // FF_ROWS, N_CHIPS),
        in_specs=[pl.BlockSpec((FF_ROWS, D_MODEL), lambda i, k: (i, 0)),
                  pl.BlockSpec((None, FF_SHARD, D_MODEL), lambda i, k: (k, 0, 0)), t_spec, t_spec],
        out_specs=[t_spec, t_spec], out_shape=[_sds(shape, BF16), _sds(shape, BF16)],
        compiler_params=_cp(("parallel", "parallel")), name="ffn_down_bwd")(dz, w_down, a, b)


def _ffn_dh(d_a, d_b, w_gate_t, w_up_t):
    def body(da_ref, db_ref, wg_ref, wu_ref, o_ref, acc):
        k = pl.program_id(1)
        part = (jnp.dot(da_ref[...], wg_ref[...], preferred_element_type=F32)
                + jnp.dot(db_ref[...], wu_ref[...], preferred_element_type=F32))

        @pl.when(k == 0)
        def _():
            acc[...] = part

        @pl.when(k > 0)
        def _():
            acc[...] += part

        @pl.when(k == N_CHIPS - 1)
        def _():
            o_ref[...] = acc[...]

    t_spec = pl.BlockSpec((None, FF_ROWS, FF_SHARD), lambda i, k: (k, i, 0))
    w_spec = pl.BlockSpec((None, FF_SHARD, D_MODEL), lambda i, k: (k, 0, 0))
    return pl.pallas_call(
        body, grid=(SEQ // FF_ROWS, N_CHIPS), in_specs=[t_spec, t_spec, w_spec, w_spec],
        out_specs=pl.BlockSpec((FF_ROWS, D_MODEL), lambda i, k: (i, 0)), out_shape=_sds((SEQ, D_MODEL)),
        scratch_shapes=[pltpu.VMEM((FF_ROWS, D_MODEL), F32)],
        compiler_params=_cp(("parallel", "arbitrary")), name="ffn_dh")(d_a, d_b, w_gate_t, w_up_t)


def _local_step(x, tgt, wts, small):
    s = SEQ
    cos_f, sin_s = [_to_phase_rows(t) for t in _rope_tables()]
    x = _reorder_rows(x, to_phase=True, name="phase_rows_x")
    tgt = _reorder_rows(tgt, to_phase=True, name="phase_rows_target")

    proj = _mm_cols(x, wts["w_in"], tm=1024, name="proj")

    attn, lse = _attention_fwd(proj, cos_f, sin_s)
    y_attn = _mm_cols(attn, wts["w_attn_br"], tm=s, name="y_attn")

    (abar_r, abar_i, bbar_r, bbar_i), ssm_vjp = jax.vjp(
        _ssm_discretise, small["ssm_a_re"], small["ssm_a_im"], small["ssm_log_dt"], small["ssm_b_re"], small["ssm_b_im"])
    b_in_r, b_in_i = [_slab_block_diag(b.transpose(0, 2, 1)).astype(BF16) for b in (bbar_r, bbar_i)]
    c_out_r = _slab_block_diag(small["ssm_c_re"].transpose(0, 2, 1)).astype(BF16)
    c_out_ni = _slab_block_diag(-small["ssm_c_im"].transpose(0, 2, 1)).astype(BF16)
    a_r, a_i = abar_r.reshape(1, SSM_LANES), abar_i.reshape(1, SSM_LANES)
    d_skip = small["ssm_d"].reshape(1, SSM_WIDTH)

    u_f = _to_scan_rows(proj[:, 3 * QKV_WIDTH:3 * QKV_WIDTH + SSM_WIDTH])
    u_p = u_f.astype(BF16)
    y_c, h_r, h_i, e_r, e_i = _ssm_forward(u_p, b_in_r, b_in_i, c_out_r, c_out_ni, a_r, a_i)

    def gelu_fwd(yc, u, dsk):
        y = yc + dsk * u
        return y, 0.5 * y * (1.0 + jnp.tanh(GELU_C * (y + GELU_K * y * y * y)))

    y_s5, gel = _rowwise(gelu_fwd, [y_c, u_f], [d_skip], [_sds((s, SSM_WIDTH)), _sds((s, SSM_WIDTH), BF16)],
                         tm=512, name="ssm_gelu")
    glu = _mm_cols(gel, wts["w_glu"], tm=s, name="glu")

    def glu_fwd(ga, gb):
        return ga * jax.nn.sigmoid(gb)

    (y_glu,) = _rowwise(glu_fwd, [(glu, SSM_WIDTH, 0), (glu, SSM_WIDTH, 1)], [], [_sds((s, SSM_WIDTH), BF16)],
                        tm=512, name="glu_gate")
    y_glu = _from_scan_rows(y_glu)
    y_ssm = _mm_cols(y_glu, wts["w_ssm_br"], tm=s, name="y_ssm")

    gl0 = (proj, D_MODEL, (3 * QKV_WIDTH + SSM_WIDTH) // D_MODEL)
    gl1 = (proj, D_MODEL, (3 * QKV_WIDTH + SSM_WIDTH) // D_MODEL + 1)
    b_gate = small["b_gate"]

    def gate_mix(l0, l1, ya, ys, bg):
        return jax.nn.sigmoid(l0 + bg[0:1]) * ya + jax.nn.sigmoid(l1 + bg[1:2]) * ys

    (mixed,) = _rowwise(gate_mix, [gl0, gl1, y_attn, y_ssm], [b_gate], [_sds((s, D_MODEL), BF16)], tm=256,
                        name="gate_mix")
    w_out = wts["w_out"].reshape(D_MODEL, D_MODEL)
    mix_out = _mm_plain(mixed, w_out, tm=1024, tn=512, name="mix_out")

    def ln1_fwd(xv, mo, g, b):
        z = DN_ALPHA * xv + mo
        xhat, _ = _ln_stats(z)
        return z, xhat * g + b

    z1, h = _rowwise(ln1_fwd, [x, mix_out], [small["ln1_g"], small["ln1_b"]],
                     [_sds((s, D_MODEL)), _sds((s, D_MODEL))], tm=256, name="ln1")

    nf = D_FF // N_CHIPS
    w_gate_t, w_up_t, w_down = wts["w_ff_gate"], wts["w_ff_up"], wts["w_ff_down"]
    ff_a, ff_b, act = _ffn_up(h, w_gate_t, w_up_t)
    ff = _matmul(act, w_down, grid=(2, N_CHIPS),
                 a_spec=pl.BlockSpec((None, 1024, nf), lambda i, k: (k, i, 0)),
                 b_spec=pl.BlockSpec((None, nf, D_MODEL), lambda i, k: (k, 0, 0)),
                 o_spec=pl.BlockSpec((1024, D_MODEL), lambda i, k: (i, 0)),
                 out_shape=_sds((s, D_MODEL)), dims=(1, 0), k_axis=1, name="ff_down")

    def ln2_loss(hv, ffv, tg, g, b):
        z = DN_ALPHA * hv + ffv
        xhat, rstd = _ln_stats(z)
        err = xhat * g + b - tg
        d_out = err * (1.0 / D_MODEL)
        loss_rows = jnp.sum(err * err, axis=-1, keepdims=True) * (0.5 / D_MODEL)
        loss = jnp.broadcast_to(jnp.sum(loss_rows, axis=0, keepdims=True), (1, 128))
        return _ln_bwd(d_out, xhat, rstd, g), loss, _colsum(d_out * xhat), _colsum(d_out)

    dz2, loss_v, d_ln2_g, d_ln2_b = _rowwise(
        ln2_loss, [h, ff, tgt], [small["ln2_g"], small["ln2_b"]], [_sds((s, D_MODEL))],
        [_sds((1, 128)), _sds((1, D_MODEL)), _sds((1, D_MODEL))], tm=256, name="ln2_loss")

    d_a, d_b = _ffn_down_bwd(dz2, w_down, ff_a, ff_b)

    def grad_rows(lhs, rhs, name):
        return _matmul(lhs, rhs, grid=(N_CHIPS,), a_spec=pl.BlockSpec((None, s, nf), lambda k: (k, 0, 0)),
                       b_spec=pl.BlockSpec((s, D_MODEL), lambda k: (0, 0)),
                       o_spec=pl.BlockSpec((None, nf, D_MODEL), lambda k: (k, 0, 0)),
                       out_shape=_sds((N_CHIPS, nf, D_MODEL), BF16), dims=(0, 0), name=name)

    g_w_ff_down = grad_rows(act, dz2, "g_w_ff_down")
    g_w_ff_gate = grad_rows(d_a, h, "g_w_ff_gate")
    g_w_ff_up = grad_rows(d_b, h, "g_w_ff_up")
    dh_ff = _ffn_dh(d_a, d_b, w_gate_t, w_up_t)

    def ln1_bwd(dz, dff, z, g):
        xhat, rstd = _ln_stats(z)
        dh = DN_ALPHA * dz + dff
        return _ln_bwd(dh, xhat, rstd, g), _colsum(dh * xhat), _colsum(dh)

    dz1, d_ln1_g, d_ln1_b = _rowwise(ln1_bwd, [dz2, dh_ff, z1], [small["ln1_g"]], [_sds((s, D_MODEL))],
                                     [_sds((1, D_MODEL)), _sds((1, D_MODEL))], tm=256, name="ln1_bwd")
    d_mixed = _mm_plain(dz1, w_out, tm=1024, tn=512, dims=(1, 1), name="d_mixed")
    g_w_out = _mm_plain(mixed, dz1, tm=D_MODEL, tn=512, dims=(0, 0), out_dtype=BF16, name="g_w_out")
    g_w_out = g_w_out.reshape(N_CHIPS, D_MODEL // N_CHIPS, D_MODEL)

    def gate_bwd(dm, l0, l1, ya, ys, bg):
        g0 = jax.nn.sigmoid(l0 + bg[0:1])
        g1 = jax.nn.sigmoid(l1 + bg[1:2])
        dl0 = dm * ya * g0 * (1.0 - g0)
        dl1 = dm * ys * g1 * (1.0 - g1)
        return dm * g0, dm * g1, jnp.concatenate([dl0, dl1], axis=1), _colsum(dl0), _colsum(dl1)

    d_y_attn, d_y_ssm, d_gl, d_bg0, d_bg1 = _rowwise(
        gate_bwd, [d_mixed, gl0, gl1, y_attn, y_ssm], [b_gate],
        [_sds((s, D_MODEL), BF16), _sds((s, D_MODEL), BF16), _sds((s, 2 * D_MODEL), BF16)],
        [_sds((1, D_MODEL)), _sds((1, D_MODEL))], tm=256, name="gate_bwd")

    g_w_ssm_br = _mm_cols_tn(y_glu, d_y_ssm, ns=D_MODEL // N_CHIPS, name="g_w_ssm_br")
    d_y_glu = _to_scan_rows(_mm_cols_nt(d_y_ssm, wts["w_ssm_br"], tm=s, name="d_y_glu"))

    def glu_bwd(dy, ga, gb):
        sg = jax.nn.sigmoid(gb)
        return jnp.concatenate([dy * sg, dy * ga * sg * (1.0 - sg)], axis=1)

    (d_glu,) = _rowwise(glu_bwd, [d_y_glu, (glu, SSM_WIDTH, 0), (glu, SSM_WIDTH, 1)], [],
                        [_sds((s, 2 * SSM_WIDTH), BF16)], tm=512, name="glu_bwd")
    g_w_glu = _mm_cols_tn(gel, d_glu, ns=2 * SSM_WIDTH // N_CHIPS, name="g_w_glu")
    d_gel = _mm_cols_nt(d_glu, wts["w_glu"], tm=s, name="d_gel")

    def gelu_bwd(dg, y, u, dsk):
        th = jnp.tanh(GELU_C * (y + GELU_K * y * y * y))
        dy = dg * (0.5 * (1.0 + th) + 0.5 * y * (1.0 - th * th) * GELU_C * (1.0 + 3.0 * GELU_K * y * y))
        return dy, dy * dsk, _colsum(dy * u)

    d_y, d_u_skip, d_ssm_d = _rowwise(gelu_bwd, [d_gel, y_s5, u_f], [d_skip],
                                      [_sds((s, SSM_WIDTH), BF16), _sds((s, SSM_WIDTH))], [_sds((1, SSM_WIDTH))],
                                      tm=512, name="gelu_bwd")
    d_u, d_abar_r, d_abar_i, d_c_r, d_c_ni, d_bin_r, d_bin_i = _ssm_backward(
        d_y, d_u_skip, u_p, h_r, h_i, e_r, e_i, b_in_r, b_in_i, c_out_r, c_out_ni, a_r, a_i)
    d_u = _from_scan_rows(d_u)
    d_bbar_r = d_bin_r.reshape(SSM_GROUPS, SSM_GROUP, SSM_STATE).transpose(0, 2, 1)
    d_bbar_i = d_bin_i.reshape(SSM_GROUPS, SSM_GROUP, SSM_STATE).transpose(0, 2, 1)
    d_a_re, d_a_im, d_log_dt, d_b_re, d_b_im = ssm_vjp(
        (d_abar_r.reshape(SSM_GROUPS, SSM_STATE), d_abar_i.reshape(SSM_GROUPS, SSM_STATE), d_bbar_r, d_bbar_i))
    d_c_re = d_c_r.reshape(SSM_GROUPS, SSM_STATE, SSM_GROUP).transpose(0, 2, 1)
    d_c_im = -d_c_ni.reshape(SSM_GROUPS, SSM_STATE, SSM_GROUP).transpose(0, 2, 1)

    g_w_attn_br = _mm_cols_tn(attn, d_y_attn, ns=D_MODEL // N_CHIPS, name="g_w_attn_br")
    d_attn = _mm_cols_nt(d_y_attn, wts["w_attn_br"], tm=s, name="d_attn")
    dqkv = [_attention_bwd(g, proj, cos_f, sin_s, d_attn, attn, lse) for g in range(len(DILATIONS))]

    d_proj = jnp.concatenate([dqkv[g][j] for j in range(3) for g in range(len(DILATIONS))] + [d_u, d_gl],
                             axis=1)
    g_w_in = _mm_cols_tn(x, d_proj, ns=IN_WIDTH // N_CHIPS, name="g_w_in")

    def grad_x_after(after):
        dx_proj = _mm_cols_nt(d_proj, wts["w_in"], tm=1024, name="dx_proj", after=after)
        return _reorder_rows(dz1, dx_proj, to_phase=False, name="grad_x", scale=DN_ALPHA)

    big = {"w_in": g_w_in, "w_attn_br": g_w_attn_br, "w_ssm_br": g_w_ssm_br, "w_out": g_w_out, "w_glu": g_w_glu,
           "w_ff_gate": g_w_ff_gate, "w_ff_up": g_w_ff_up, "w_ff_down": g_w_ff_down}
    small_g = {"b_gate": jnp.concatenate([d_bg0, d_bg1], axis=0), "ssm_a_re": d_a_re, "ssm_a_im": d_a_im,
               "ssm_log_dt": d_log_dt, "ssm_b_re": d_b_re, "ssm_b_im": d_b_im, "ssm_c_re": d_c_re, "ssm_c_im": d_c_im,
               "ssm_d": d_ssm_d.reshape(SSM_WIDTH), "ln1_g": d_ln1_g, "ln1_b": d_ln1_b, "ln2_g": d_ln2_g,
               "ln2_b": d_ln2_b}
    marks = {"ln1_bwd": dz1, "scan_bwd": d_abar_r, "attention_bwd_0": dqkv[0][0]}
    return loss_v[0, 0], grad_x_after, big, small_g, marks


GATHER_ID, SWAP_ID, SCATTER_ID, JOIN_ID, EXCHANGE_ID = 1, 2, 3, 4, 5


def _place():
    return lax.axis_index("x"), lax.axis_index("y"), lax.axis_index("c")


def _other_chips(x, y):
    return [(1 - x, y), (x, 1 - y), (1 - x, 1 - y)]


def _handshake(peers):
    barrier = pltpu.get_barrier_semaphore()
    for peer in peers:
        pl.semaphore_signal(barrier, inc=1, device_id=peer, device_id_type=MESH)
    pl.semaphore_wait(barrier, len(peers))


def _sequencer(body, arrays, out_type, sems, collective_id, name):
    return pl.kernel(body, name=name, out_type=out_type,
                     mesh=plsc.ScalarSubcoreMesh(axis_name="sequencer", num_cores=1), scratch_types=sems,
                     compiler_params=pltpu.CompilerParams(collective_id=collective_id))(*arrays)


def _gather_weights(shards, *, name):
    nw = len(shards)

    def body(*refs):
        ins, outs = refs[:nw], refs[nw:2 * nw]
        send_sems, recv_sems, pass_send, pass_recv, local_sems = refs[2 * nw:]
        x, y, c = _place()
        chip = 2 * x + y
        chips = _other_chips(x, y)
        _handshake([(x, y, 1 - c)] + [(cx, cy, c) for cx, cy in chips])
        started = []
        for w in range(nw):
            hw = shards[w].shape[0] // 2
            mine = pl.ds(c * hw, hw)
            own = pltpu.make_async_copy(ins[w], outs[w].at[chip], local_sems.at[w])
            own.start()
            started.append(own)
            for j, (cx, cy) in enumerate(chips):
                cp = pltpu.make_async_remote_copy(
                    src_ref=ins[w].at[mine], dst_ref=outs[w].at[chip, mine], send_sem=send_sems.at[w, j],
                    recv_sem=recv_sems.at[w, j], device_id=(cx, cy, c), device_id_type=MESH)
                cp.start()
                started.append(cp)
        passed = []
        for w in range(nw):
            hw = shards[w].shape[0] // 2
            mine = pl.ds(c * hw, hw)
            for j, (cx, cy) in enumerate(chips):
                landed = outs[w].at[2 * cx + cy, mine]
                pltpu.make_async_remote_copy(
                    src_ref=ins[w].at[mine], dst_ref=landed, send_sem=send_sems.at[w, j],
                    recv_sem=recv_sems.at[w, j], device_id=(cx, cy, c), device_id_type=MESH).wait_recv()
                cp = pltpu.make_async_remote_copy(
                    src_ref=landed, dst_ref=landed, send_sem=pass_send.at[w, j], recv_sem=pass_recv.at[w, j],
                    device_id=(x, y, 1 - c), device_id_type=MESH)
                cp.start()
                passed.append(cp)
        for w in range(nw):
            hw = shards[w].shape[0] // 2
            theirs = pl.ds((1 - c) * hw, hw)
            for j, (cx, cy) in enumerate(chips):
                landed = outs[w].at[2 * cx + cy, theirs]
                pltpu.make_async_remote_copy(
                    src_ref=landed, dst_ref=landed, send_sem=pass_send.at[w, j], recv_sem=pass_recv.at[w, j],
                    device_id=(x, y, 1 - c), device_id_type=MESH).wait_recv()
        for cp in started[0::4]:
            cp.wait()
        for cp in [s for i, s in enumerate(started) if i % 4] + passed:
            cp.wait_send()

    sem = pltpu.SemaphoreType.DMA
    return _sequencer(body, shards, [_sds((N_CHIPS,) + a.shape, a.dtype) for a in shards],
                      [sem((nw, 3)), sem((nw, 3)), sem((nw, 3)), sem((nw, 3)), sem((nw,))], GATHER_ID, name)


def _swap_other_halves(grads, *, name):
    nw = len(grads)

    def body(*refs):
        ins, outs = refs[:nw], refs[nw:2 * nw]
        send_sems, recv_sems = refs[2 * nw:]
        x, y, c = _place()
        _handshake([(x, y, 1 - c)])
        cps = []
        for w in range(nw):
            hw = grads[w].shape[1] // 2
            cp = pltpu.make_async_remote_copy(
                src_ref=ins[w].at[:, pl.ds((1 - c) * hw, hw)], dst_ref=outs[w], send_sem=send_sems.at[w],
                recv_sem=recv_sems.at[w], device_id=(x, y, 1 - c), device_id_type=MESH)
            cp.start()
            cps.append(cp)
        for cp in cps:
            cp.wait()

    sem = pltpu.SemaphoreType.DMA
    return _sequencer(body, grads, [_sds((N_CHIPS, g.shape[1] // 2, g.shape[2]), g.dtype) for g in grads],
                      [sem((nw,)), sem((nw,))], SWAP_ID, name)


def _add_my_halves(core, grads, others, *, name, after=()):
    nw = len(grads)
    halves = [g.shape[1] // 2 for g in grads]

    def body(core_ref, *refs):
        outs = refs[2 * nw + len(after):]
        for g_ref, o_ref, out_ref in zip(refs[:nw], refs[nw:2 * nw], outs):
            out_ref[...] = (g_ref[...].astype(F32) + o_ref[...].astype(F32)).astype(out_ref.dtype)

    in_specs = [pl.BlockSpec((None, None, hw, g.shape[2]), lambda s, core_ref: (s, core_ref[0], 0, 0))
                for g, hw in zip(grads, halves)]
    in_specs += [pl.BlockSpec((None, hw, g.shape[2]), lambda s, core_ref: (s, 0, 0)) for g, hw in zip(grads, halves)]
    return pl.pallas_call(
        body,
        grid_spec=pltpu.PrefetchScalarGridSpec(
            num_scalar_prefetch=1, grid=(N_CHIPS,), in_specs=in_specs + [HBM_OPERAND] * len(after),
            out_specs=[pl.BlockSpec((None, hw, g.shape[2]), lambda s, core_ref: (s, 0, 0))
                       for g, hw in zip(grads, halves)]),
        out_shape=[_sds((N_CHIPS, hw, g.shape[2]), BF16) for g, hw in zip(grads, halves)],
        compiler_params=_cp(("parallel",)), name=name)(
            core, *[g.reshape(N_CHIPS, 2, hw, g.shape[2]) for g, hw in zip(grads, halves)], *others, *after)


def _scatter_partials(parts, *, name):
    nw = len(parts)

    def body(*refs):
        ins, outs = refs[:nw], refs[nw:2 * nw]
        send_sems, recv_sems = refs[2 * nw:]
        x, y, c = _place()
        _handshake([(cx, cy, c) for cx, cy in _other_chips(x, y)])
        cps = []
        for w in range(nw):
            for j, (cx, cy) in enumerate(_other_chips(x, y)):
                cp = pltpu.make_async_remote_copy(
                    src_ref=ins[w].at[2 * cx + cy], dst_ref=outs[w].at[j], send_sem=send_sems.at[w, j],
                    recv_sem=recv_sems.at[w, j], device_id=(cx, cy, c), device_id_type=MESH)
                cp.start()
                cps.append(cp)
        for cp in cps:
            cp.wait()

    sem = pltpu.SemaphoreType.DMA
    return _sequencer(body, parts, [_sds((3,) + p.shape[1:], p.dtype) for p in parts],
                      [sem((nw, 3)), sem((nw, 3))], SCATTER_ID, name)


SUM_STEPS = 2


def _sum_partials(chip, parts, recvd, *, name, after=()):
    nw = len(parts)
    rows = [p.shape[1] // SUM_STEPS for p in parts]

    def body(chip_ref, *refs):
        outs = refs[2 * nw + len(after):]
        for p_ref, r_ref, out_ref in zip(refs[:nw], refs[nw:2 * nw], outs):
            acc = p_ref[...].astype(F32)
            for j in range(3):
                acc = acc + r_ref[j].astype(F32)
            out_ref[...] = acc

    in_specs = [pl.BlockSpec((None, th, p.shape[2]), lambda i, chip_ref: (chip_ref[0], i, 0))
                for p, th in zip(parts, rows)]
    in_specs += [pl.BlockSpec((3, th, p.shape[2]), lambda i, chip_ref: (0, i, 0)) for p, th in zip(parts, rows)]
    return pl.pallas_call(
        body,
        grid_spec=pltpu.PrefetchScalarGridSpec(
            num_scalar_prefetch=1, grid=(SUM_STEPS,), in_specs=in_specs + [HBM_OPERAND] * len(after),
            out_specs=[pl.BlockSpec((th, p.shape[2]), lambda i, chip_ref: (i, 0)) for p, th in zip(parts, rows)]),
        out_shape=[_sds(p.shape[1:]) for p in parts], compiler_params=_cp(("parallel",)), name=name)(
            chip, *parts, *recvd, *after)


def _swap_reduced_halves(halves, *, name):
    nw = len(halves)

    def body(*refs):
        ins, outs = refs[:nw], refs[nw:2 * nw]
        send_sems, recv_sems = refs[2 * nw:]
        x, y, c = _place()
        _handshake([(x, y, 1 - c)])
        cps = []
        for w in range(nw):
            cp = pltpu.make_async_remote_copy(
                src_ref=ins[w], dst_ref=outs[w], send_sem=send_sems.at[w], recv_sem=recv_sems.at[w],
                device_id=(x, y, 1 - c), device_id_type=MESH)
            cp.start()
            cps.append(cp)
        for cp in cps:
            cp.wait()

    sem = pltpu.SemaphoreType.DMA
    return _sequencer(body, halves, [_sds(h.shape, h.dtype) for h in halves], [sem((nw,)), sem((nw,))], JOIN_ID, name)


def _exchange_rows(vec, *, name):
    def body(v_ref, slots, send_sems, recv_sems, local_sem):
        x, y, c = _place()
        me = 4 * x + 2 * y + c
        peers = []
        for mask in range(1, N_DEV):
            peers.append((1 - x if mask & 4 else x, 1 - y if mask & 2 else y, 1 - c if mask & 1 else c))
        _handshake(peers)
        own = pltpu.make_async_copy(v_ref, slots.at[me], local_sem)
        own.start()
        cps = []
        for k, peer in enumerate(peers):
            cp = pltpu.make_async_remote_copy(
                src_ref=v_ref, dst_ref=slots.at[me], send_sem=send_sems.at[k], recv_sem=recv_sems.at[k],
                device_id=peer, device_id_type=MESH)
            cp.start()
            cps.append(cp)
        for k, (px, py, pc) in enumerate(peers):
            pltpu.make_async_remote_copy(
                src_ref=v_ref, dst_ref=slots.at[4 * px + 2 * py + pc], send_sem=send_sems.at[k],
                recv_sem=recv_sems.at[k], device_id=(px, py, pc), device_id_type=MESH).wait_recv()
        for cp in cps:
            cp.wait_send()
        own.wait()

    sem = pltpu.SemaphoreType.DMA
    return _sequencer(body, [vec], [_sds((N_DEV,) + vec.shape)], [sem((N_DEV - 1,)), sem((N_DEV - 1,)), sem(())],
                      EXCHANGE_ID, name)[0]


def _sum_slots(slots, *, name, after=()):
    def body(s_ref, *rest):
        out_ref = rest[len(after)]
        acc = s_ref[0]
        for d in range(1, N_DEV):
            acc = acc + s_ref[d]
        out_ref[...] = acc

    vmem = pl.BlockSpec(memory_space=pltpu.VMEM)
    return pl.pallas_call(
        body, in_specs=[vmem] + [HBM_OPERAND] * len(after), out_specs=vmem, out_shape=_sds(slots.shape[1:]),
        compiler_params=pltpu.CompilerParams(vmem_limit_bytes=VMEM_LIMIT_BYTES), name=name)(slots, *after)


def _reduce_scatter_start(grads, core, *, tag, add_after=()):
    others = _swap_other_halves(grads, name="swap_other_halves_" + tag)
    parts = _add_my_halves(core, grads, others, name="add_my_halves_" + tag, after=add_after)
    return parts, _scatter_partials(parts, name="scatter_partials_" + tag)


def _reduce_scatter_finish(parts, recvd, chip, *, tag, sum_after=()):
    mine = _sum_partials(chip, parts, recvd, name="sum_partials_" + tag, after=sum_after)
    return mine, _swap_reduced_halves(mine, name="swap_reduced_halves_" + tag)


ADAM_BLOCK_ELEMS = 256 * 1024


def _adam_rows(rows, cols):
    tm = rows
    while tm * cols > ADAM_BLOCK_ELEMS and tm % 16 == 0:
        tm //= 2
    return tm


def _adam_step(wv, gv, mv, vv):
    m2 = ADAM_B1 * mv + (1.0 - ADAM_B1) * gv
    v2 = ADAM_B2 * vv + (1.0 - ADAM_B2) * (gv * gv)
    m_hat = m2 / (1.0 - ADAM_B1 ** ADAM_STEP)
    v_hat = v2 / (1.0 - ADAM_B2 ** ADAM_STEP)
    return -ADAM_LR * (m_hat / (jnp.sqrt(v_hat) + ADAM_EPS) + ADAM_WD * wv), m2, v2


def _adamw(w, g, m, v, *, name):
    rows, cols = w.shape
    return _rowwise(_adam_step, [w, g, m, v], [], [_sds((rows, cols))] * 3, tm=_adam_rows(rows, cols), name=name)


def _adamw_halves(core, w, g_mine, g_theirs, m, v, *, name, after=()):
    rows, cols = w.shape
    hw = rows // 2
    tm = _adam_rows(hw, cols)
    per_half = hw // tm

    def body(core_ref, w_ref, gm_ref, gt_ref, m_ref, v_ref, *rest):
        g_out, d_out, m_out, v_out = rest[len(after):]
        mine = (pl.program_id(0) // per_half) == core_ref[0]
        g = jnp.where(mine, gm_ref[...], gt_ref[...])
        d, m2, v2 = _adam_step(w_ref[...], g, m_ref[...], v_ref[...])
        g_out[...] = g
        d_out[...] = d
        m_out[...] = m2
        v_out[...] = v2

    full = pl.BlockSpec((tm, cols), lambda i, core_ref: (i, 0))
    half = pl.BlockSpec((tm, cols), lambda i, core_ref: (i % per_half, 0))
    return pl.pallas_call(
        body,
        grid_spec=pltpu.PrefetchScalarGridSpec(
            num_scalar_prefetch=1, grid=(rows // tm,),
            in_specs=[full, half, half, full, full] + [HBM_OPERAND] * len(after), out_specs=[full, full, full, full]),
        out_shape=[_sds((rows, cols))] * 4, compiler_params=_cp(("parallel",)), name=name)(
            core, w, g_mine, g_theirs, m, v, *after)


HELD_TRANSPOSED = ("w_ff_gate", "w_ff_up")


def _as_rows(name, arr):
    return arr[0].T if name in HELD_TRANSPOSED else arr[0]


def _from_rows(name, arr2d):
    return (arr2d.T if name in HELD_TRANSPOSED else arr2d)[None]


STORED_SWAPPED = ("ssm_b_re", "ssm_b_im")


def _as_stored(name, arr):
    return jnp.swapaxes(arr, -1, -2) if name in STORED_SWAPPED else arr


def _pack_rows(arrs):
    flat = jnp.concatenate([a.reshape(-1).astype(F32) for a in arrs])
    rows = -(-flat.shape[0] // 1024) * 8
    return jnp.pad(flat, (0, rows * 128 - flat.shape[0])).reshape(rows, 128)


def _unpack_rows(vec, shapes):
    flat = vec.reshape(-1)
    out, off = [], 0
    for shp in shapes:
        size = math.prod(shp)
        out.append(flat[off:off + size].reshape(shp))
        off += size
    return out


SMALL = ("b_gate", "ssm_a_re", "ssm_a_im", "ssm_log_dt", "ssm_b_re", "ssm_b_im", "ssm_c_re", "ssm_c_im", "ssm_d",
         "ln1_g", "ln1_b", "ln2_g", "ln2_b")
GATHER_GROUPS = (("w_in", ("w_in",)), ("mixer", ("w_attn_br", "w_ssm_br", "w_glu", "w_out")),
                 ("ffn", ("w_ff_gate", "w_ff_up", "w_ff_down")))
REDUCE_GROUPS = (("ffn", ("w_ff_down", "w_ff_gate", "w_ff_up")),
                 ("mixer", ("w_out", "w_ssm_br", "w_glu", "w_attn_br")), ("w_in", ("w_in",)))
WEIGHTS = ("w_in", "b_gate", "w_attn_br", "w_ssm_br", "w_out", "ssm_a_re", "ssm_a_im", "ssm_log_dt", "ssm_b_re",
           "ssm_b_im", "ssm_c_re", "ssm_c_im", "ssm_d", "w_glu", "ln1_g", "ln1_b", "w_ff_gate", "w_ff_up", "w_ff_down",
           "ln2_g", "ln2_b")


def kernel(x, w_in, b_gate, w_attn_br, w_ssm_br, w_out, ssm_a_re, ssm_a_im, ssm_log_dt, ssm_b_re, ssm_b_im, ssm_c_re, ssm_c_im, ssm_d, w_glu, ln1_g, ln1_b, w_ff_gate, w_ff_up, w_ff_down, ln2_g, ln2_b, loss_target, m_w_in, m_b_gate, m_w_attn_br, m_w_ssm_br, m_w_out, m_ssm_a_re, m_ssm_a_im, m_ssm_log_dt, m_ssm_b_re, m_ssm_b_im, m_ssm_c_re, m_ssm_c_im, m_ssm_d, m_w_glu, m_ln1_g, m_ln1_b, m_w_ff_gate, m_w_ff_up, m_w_ff_down, m_ln2_g, m_ln2_b, v_w_in, v_b_gate, v_w_attn_br, v_w_ssm_br, v_w_out, v_ssm_a_re, v_ssm_a_im, v_ssm_log_dt, v_ssm_b_re, v_ssm_b_im, v_ssm_c_re, v_ssm_c_im, v_ssm_d, v_w_glu, v_ln1_g, v_ln1_b, v_w_ff_gate, v_w_ff_up, v_w_ff_down, v_ln2_g, v_ln2_b):
    given = dict(locals())
    px, py, pc = _place()
    chip = 2 * px + py
    core_s = jnp.reshape(pc, (1,)).astype(jnp.int32)
    chip_s = jnp.reshape(chip, (1,)).astype(jnp.int32)

    wts = {}
    for tag, names in GATHER_GROUPS:
        wts.update(zip(names, _gather_weights([_as_rows(n, given[n]).astype(BF16) for n in names],
                                              name="gather_" + tag)))
    ncol = D_MODEL // N_CHIPS
    bg_mine = jnp.where(pc == 0, b_gate[0], jnp.zeros_like(b_gate[0]))
    bg_full = lax.dynamic_update_slice(jnp.zeros((2, D_MODEL), F32), bg_mine, (0, chip * ncol))
    bg_slots = _exchange_rows(bg_full.reshape(16, 128), name="exchange_gate_bias")
    bg_full = _sum_slots(bg_slots, name="sum_gate_bias").reshape(2, D_MODEL)
    small = {n: given[n][0] for n in SMALL if n.startswith("ssm")}
    small.update({n: given[n] for n in ("ln1_g", "ln1_b", "ln2_g", "ln2_b")})
    small["b_gate"] = bg_full

    loss_mine, grad_x_after, big_g, small_g, marks = _local_step(x[0], loss_target[0], wts, small)

    groups = dict(REDUCE_GROUPS)
    parts, recvd = {}, {}
    grads, delta, new_m, new_v = {}, {}, {}, {}

    def start(tag, add_after):
        parts[tag], recvd[tag] = _reduce_scatter_start([big_g[n] for n in groups[tag]], core_s, tag=tag,
                                                       add_after=add_after)

    def finish(tag, sum_after, adam_after):
        mine, theirs = _reduce_scatter_finish(parts[tag], recvd[tag], chip_s, tag=tag, sum_after=sum_after)
        for n, g_mine, g_theirs in zip(groups[tag], mine, theirs):
            res = _adamw_halves(core_s, _as_rows(n, given[n]), g_mine, g_theirs, _as_rows(n, given["m_" + n]),
                                _as_rows(n, given["v_" + n]), name="adamw_" + n, after=adam_after)
            grads[n], delta[n], new_m[n], new_v[n] = [_from_rows(n, r) for r in res]

    start("ffn", (marks["ln1_bwd"],))
    start("mixer", (marks["scan_bwd"],))
    finish("mixer", (marks["attention_bwd_0"],), (big_g["w_in"],))
    start("w_in", tuple(delta[n] for n in groups["mixer"]))
    in_flight = (parts["w_in"][0],)
    grad_x = grad_x_after(in_flight)
    finish("ffn", (marks["scan_bwd"],), in_flight)
    stored = [_as_stored(n, small_g[n]) for n in SMALL] + [loss_mine.reshape(1)]
    slots = _exchange_rows(_pack_rows(stored), name="exchange_small")
    summed = _unpack_rows(_sum_slots(slots, name="sum_small", after=in_flight), [a.shape for a in stored])
    loss = summed.pop()[0]
    for n, g in zip(SMALL, summed):
        g = _as_stored(n, g)
        if n == "b_gate":
            g = lax.dynamic_slice(g, (0, chip * ncol), (2, ncol))
        grads[n] = g.reshape(given[n].shape)
    packed = [_pack_rows([_as_stored(n, src[n]) for n in SMALL]) for src in
              (given, grads, {n: given["m_" + n] for n in SMALL}, {n: given["v_" + n] for n in SMALL})]
    shapes = [_as_stored(n, given[n]).shape for n in SMALL]
    small_out = _adamw(*packed, name="adamw_small")
    for out, vec in zip((delta, new_m, new_v), small_out):
        out.update((n, _as_stored(n, a)) for n, a in zip(SMALL, _unpack_rows(vec, shapes)))
    behind = [delta[n] for n in groups["ffn"]] + [small_out[0], grad_x]
    finish("w_in", tuple(behind), ())

    return (loss, grad_x.reshape(x.shape), *[grads[n] for n in WEIGHTS], *[delta[n] for n in WEIGHTS],
            *[new_m[n] for n in WEIGHTS], *[new_v[n] for n in WEIGHTS])
```

```python
import math

import jax
import jax.numpy as jnp
from jax import lax
from jax.experimental import pallas as pl
from jax.experimental.pallas import tpu as pltpu
from jax.experimental.pallas import tpu_sc as plsc

F32 = jnp.float32
BF16 = jnp.bfloat16
MESH = pl.DeviceIdType.MESH

D_MODEL = 1024
SEQ = 2048
HEAD_DIM = 64
ATTN_HEADS = 8
DILATIONS = (1, 4, 16)
ATTN_WIDTH = ATTN_HEADS * HEAD_DIM
QKV_WIDTH = 3 * ATTN_WIDTH
BLOCK = 128
ROPE_THETA = 10000.0
NEG_INF = -1e30
SSM_GROUP = 16
SSM_GROUPS = 32
SSM_WIDTH = 512
SSM_STATE = 64
SSM_LANES = SSM_GROUPS * SSM_STATE
SCAN_CHUNKS = 8
SCAN_STEPS = SEQ // SCAN_CHUNKS
IN_WIDTH = 3 * QKV_WIDTH + SSM_WIDTH + 2 * D_MODEL
D_FF = 2816
N_CHIPS = 4
N_DEV = 8
DN_ALPHA = 2.0 ** 0.25
LN_EPS = 1e-5
ADAM_LR = 0.001
ADAM_B1 = 0.9
ADAM_B2 = 0.999
ADAM_EPS = 1e-08
ADAM_WD = 0.01
ADAM_STEP = 10
GELU_C = math.sqrt(2.0 / math.pi)
GELU_K = 0.044715

VMEM_LIMIT_BYTES = 56 * 1024 * 1024


def _sds(shape, dtype=F32):
    return jax.ShapeDtypeStruct(tuple(shape), dtype)


def _cp(semantics=None):
    return pltpu.CompilerParams(dimension_semantics=semantics, vmem_limit_bytes=VMEM_LIMIT_BYTES)


HBM_OPERAND = pl.BlockSpec(memory_space=pl.ANY)


def _matmul(a, b, *, grid, a_spec, b_spec, o_spec, out_shape, dims, k_axis=None, name, after=()):
    nk = grid[k_axis] if k_axis is not None else 1
    o_block = tuple(d for d in o_spec.block_shape if d is not None)
    n_after = len(after)

    def body(a_ref, b_ref, *rest):
        o_ref, acc = rest[n_after], rest[n_after + 1:]
        part = lax.dot_general(a_ref[...].astype(BF16), b_ref[...].astype(BF16),
                               (((dims[0],), (dims[1],)), ((), ())), preferred_element_type=F32)
        if k_axis is None:
            o_ref[...] = part.astype(o_ref.dtype)
        else:
            k = pl.program_id(k_axis)

            @pl.when(k == 0)
            def _():
                acc[0][...] = part

            @pl.when(k > 0)
            def _():
                acc[0][...] += part

            @pl.when(k == nk - 1)
            def _():
                o_ref[...] = acc[0][...].astype(o_ref.dtype)

    sem = tuple("arbitrary" if ax == k_axis else "parallel" for ax in range(len(grid)))
    return pl.pallas_call(
        body, grid=grid, in_specs=[a_spec, b_spec] + [HBM_OPERAND] * n_after, out_specs=o_spec, out_shape=out_shape,
        scratch_shapes=[pltpu.VMEM(o_block, F32)] if k_axis is not None else [],
        compiler_params=_cp(sem), name=name)(a, b, *after)


def _mm_cols(a, wg, *, tm, name, out_dtype=F32):
    m, k = a.shape
    ns = wg.shape[2]
    return _matmul(a, wg, grid=(m // tm, N_CHIPS),
                   a_spec=pl.BlockSpec((tm, k), lambda i, s: (i, 0)),
                   b_spec=pl.BlockSpec((None, k, ns), lambda i, s: (s, 0, 0)),
                   o_spec=pl.BlockSpec((tm, ns), lambda i, s: (i, s)),
                   out_shape=_sds((m, N_CHIPS * ns), out_dtype), dims=(1, 0), name=name)


def _mm_cols_nt(dy, wg, *, tm, name, out_dtype=F32, after=()):
    k, ns = wg.shape[1], wg.shape[2]
    m = dy.shape[0]
    a_spec = pl.BlockSpec((tm, ns), lambda i, s: (i, s))
    return _matmul(dy, wg, grid=(m // tm, N_CHIPS), a_spec=a_spec,
                   b_spec=pl.BlockSpec((None, k, ns), lambda i, s: (s, 0, 0)),
                   o_spec=pl.BlockSpec((tm, k), lambda i, s: (i, 0)),
                   out_shape=_sds((m, k), out_dtype), dims=(1, 1), k_axis=1, name=name, after=after)


def _mm_cols_tn(a, dy, *, ns, name, after=()):
    m, k = a.shape
    return _matmul(a, dy, grid=(N_CHIPS,), a_spec=pl.BlockSpec((m, k), lambda s: (0, 0)),
                   b_spec=pl.BlockSpec((m, ns), lambda s: (0, s)),
                   o_spec=pl.BlockSpec((None, k, ns), lambda s: (s, 0, 0)),
                   out_shape=_sds((N_CHIPS, k, ns), BF16), dims=(0, 0), name=name, after=after)


def _mm_plain(a, b, *, tm, tn, name, out_dtype=F32, dims=(1, 0), tk=None):
    m = a.shape[1 - dims[0]]
    kk = a.shape[dims[0]]
    n = b.shape[1 - dims[1]]
    tk = kk if tk is None else tk
    nk = kk // tk

    def a_idx(i, j, k):
        return (i, k) if dims[0] == 1 else (k, i)

    def b_idx(i, j, k):
        return (k, j) if dims[1] == 0 else (j, k)

    a_blk = (tm, tk) if dims[0] == 1 else (tk, tm)
    b_blk = (tk, tn) if dims[1] == 0 else (tn, tk)
    return _matmul(a, b, grid=(m // tm, n // tn, nk),
                   a_spec=pl.BlockSpec(a_blk, a_idx), b_spec=pl.BlockSpec(b_blk, b_idx),
                   o_spec=pl.BlockSpec((tm, tn), lambda i, j, k: (i, j)),
                   out_shape=_sds((m, n), out_dtype), dims=dims, k_axis=2 if nk > 1 else None, name=name)


def _rowwise(fn, tiled, full, outs, accs=(), *, tm, name, after=()):
    args, in_specs = [], []
    for t in tiled:
        if isinstance(t, tuple):
            arr, w, cb = t
            in_specs.append(pl.BlockSpec((tm, w), lambda i, cb=cb: (i, cb)))
        else:
            arr = t
            in_specs.append(pl.BlockSpec((tm, arr.shape[1]), lambda i: (i, 0)))
        args.append(arr)
    rows = args[0].shape[0]
    for f in full:
        in_specs.append(pl.BlockSpec(f.shape, lambda i, nd=f.ndim: (0,) * nd))
        args.append(f)
    out_specs = [pl.BlockSpec((tm, o.shape[1]), lambda i: (i, 0)) for o in outs]
    out_specs += [pl.BlockSpec(a.shape, lambda i, nd=len(a.shape): (0,) * nd) for a in accs]
    n_in, n_out = len(args), len(outs)
    in_specs += [HBM_OPERAND] * len(after)
    first_out = n_in + len(after)

    def body(*refs):
        res = fn(*[r[...] for r in refs[:n_in]])
        res = res if isinstance(res, (tuple, list)) else (res,)
        for r, v in zip(refs[first_out:first_out + n_out], res[:n_out]):
            r[...] = v.astype(r.dtype)
        i = pl.program_id(0)
        for r, v in zip(refs[first_out + n_out:], res[n_out:]):
            @pl.when(i == 0)
            def _(r=r, v=v):
                r[...] = v

            @pl.when(i > 0)
            def _(r=r, v=v):
                r[...] += v

    res = pl.pallas_call(
        body, grid=(rows // tm,), in_specs=in_specs, out_specs=out_specs, out_shape=list(outs) + list(accs),
        compiler_params=_cp(("arbitrary",) if accs else ("parallel",)), name=name)(*args, *after)
    return res


def _colsum(v):
    return jnp.sum(v, axis=0, keepdims=True)


def _ln_stats(z):
    mu = jnp.mean(z, axis=-1, keepdims=True)
    zc = z - mu
    var = jnp.mean(zc * zc, axis=-1, keepdims=True)
    rstd = lax.rsqrt(var + LN_EPS)
    return zc * rstd, rstd


def _ln_bwd(dy, xhat, rstd, g):
    dxh = dy * g
    m1 = jnp.mean(dxh, axis=-1, keepdims=True)
    m2 = jnp.mean(dxh * xhat, axis=-1, keepdims=True)
    return rstd * (dxh - m1 - xhat * m2)


def _swap_halves(t):
    w = t.shape[-1]
    lane = lax.broadcasted_iota(jnp.int32, t.shape, t.ndim - 1)
    return jnp.where((lane % HEAD_DIM) < HEAD_DIM // 2, pltpu.roll(t, w - HEAD_DIM // 2, t.ndim - 1),
                     pltpu.roll(t, HEAD_DIM // 2, t.ndim - 1))


PHASES = max(DILATIONS)
PAIR = 2 * HEAD_DIM
UNITS = SEQ // BLOCK
UNIT_BATCH = 8
ROPE_ROWS = 256


def _to_phase_rows(t):
    return t.reshape(SEQ // PHASES, PHASES, t.shape[1]).transpose(1, 0, 2).reshape(t.shape)


def _reorder_rows(arr, plus=None, *, to_phase, name, scale=1.0):
    def body(*refs):
        o_ref = refs[-1]
        for rho in range(PHASES):
            phase = pl.ds(rho * BLOCK, BLOCK)
            strided = pl.ds(rho, BLOCK, stride=PHASES)
            src, dst = (strided, phase) if to_phase else (phase, strided)
            val = refs[0][src, :]
            if scale != 1.0:
                val = val * scale
            if plus is not None:
                val = val + refs[1][src, :]
            o_ref[dst, :] = val

    spec = pl.BlockSpec((SEQ, BLOCK), lambda j: (0, j))
    ins = [arr] if plus is None else [arr, plus]
    return pl.pallas_call(body, grid=(arr.shape[1] // BLOCK,), in_specs=[spec] * len(ins), out_specs=spec,
                          out_shape=_sds(arr.shape), compiler_params=_cp(("parallel",)), name=name)(*ins)


def _rope(t, cf, ss):
    return t * cf + _swap_halves(t) * ss


def _rope_transposed(d, cf, ss):
    return d * cf + _swap_halves(d * ss)


def _unit_pieces(u, dil):
    pieces, length = PHASES // dil, 8 * dil
    if dil == 1:
        rho, i = 0, u
    elif dil == PHASES:
        rho, i = u, 0
    else:
        rho, i = jnp.bitwise_and(u, dil - 1), jnp.right_shift(u, dil.bit_length() - 1)
    before = jnp.maximum(i - 1, 0)
    cur = [pl.multiple_of((rho + dil * k) * BLOCK + length * i, 8) for k in range(pieces)]
    prev = [pl.multiple_of((rho + dil * k) * BLOCK + length * before, 8) for k in range(pieces)]
    return i, cur, prev


def _load_tile(ref, starts, dil):
    return jnp.concatenate([ref[pl.ds(st, 8 * dil), :] for st in starts], axis=0)


def _store_tile(ref, starts, dil, val, head=None, accumulate=False):
    length = 8 * dil
    lanes = slice(None) if head is None else pl.ds(head * HEAD_DIM, HEAD_DIM)
    cols = slice(None) if head is None else slice(head * HEAD_DIM, (head + 1) * HEAD_DIM)
    for k, st in enumerate(starts):
        piece = val[k * length:(k + 1) * length, cols]
        if accumulate:
            ref[pl.ds(st, length), lanes] += piece
        else:
            ref[pl.ds(st, length), lanes] = piece


def _tile_position(idx, dil):
    pieces, length = PHASES // dil, 8 * dil
    return pieces * jnp.bitwise_and(idx, length - 1) + jnp.right_shift(idx, length.bit_length() - 1)


def _band_mask(i, dil):
    row = lax.broadcasted_iota(jnp.int32, (BLOCK, 2 * BLOCK), 0)
    col = lax.broadcasted_iota(jnp.int32, (BLOCK, 2 * BLOCK), 1)
    key_pos = _tile_position(jnp.bitwise_and(col, BLOCK - 1), dil) + jnp.where(col >= BLOCK, 0, -BLOCK)
    dist = _tile_position(row, dil) - key_pos
    return (dist >= 0) & (dist <= BLOCK) & ((col >= BLOCK) | (i > 0))


def _causal_mask():
    row = lax.broadcasted_iota(jnp.int32, (BLOCK, BLOCK), 0)
    col = lax.broadcasted_iota(jnp.int32, (BLOCK, BLOCK), 1)
    return row >= col


def _pair_views(col0):
    return [pl.BlockSpec((SEQ, PAIR), lambda hp, g=g: (0, col0 // PAIR + g * (ATTN_WIDTH // PAIR) + hp))
            for g in range(len(DILATIONS))]


def _rotate(in_refs, out_refs, cf_ref, ss_ref, scale):
    def step(t, carry):
        rows = pl.ds(pl.multiple_of(t * ROPE_ROWS, ROPE_ROWS), ROPE_ROWS)
        cf, ss = cf_ref[rows, :] * scale, ss_ref[rows, :] * scale
        for i_ref, o_ref in zip(in_refs, out_refs):
            o_ref[rows, :] = _rope(i_ref[rows, :], cf, ss)
        return carry

    lax.fori_loop(0, SEQ // ROPE_ROWS, step, 0)


def _attention_fwd(proj, cos_f, sin_s):
    ng = len(DILATIONS)

    def body(*refs):
        q_refs, k_refs, v_refs = refs[:ng], refs[ng:2 * ng], refs[2 * ng:3 * ng]
        cf_ref, ss_ref, attn_ref, lse_ref = refs[3 * ng:3 * ng + 4]
        scratch = refs[3 * ng + 4:]
        qr_refs, kr_refs = scratch[:ng], scratch[ng:]
        _rotate(q_refs, qr_refs, cf_ref, ss_ref, 1.0 / math.sqrt(HEAD_DIM))
        _rotate(k_refs, kr_refs, cf_ref, ss_ref, 1.0)
        first = lax.broadcasted_iota(jnp.int32, (BLOCK, PAIR), 1) < HEAD_DIM
        for g, dil in enumerate(DILATIONS):
            two_blocks = SEQ // dil > BLOCK

            def units(t, carry, g=g, dil=dil, two_blocks=two_blocks):
                picked = [_unit_pieces(t * UNIT_BATCH + j, dil) for j in range(UNIT_BATCH)]

                def tiles(ref, with_prev=False):
                    if with_prev and two_blocks:
                        return jnp.stack([jnp.concatenate([_load_tile(ref, prev, dil), _load_tile(ref, rows, dil)],
                                                          axis=0) for _, rows, prev in picked])
                    return jnp.stack([_load_tile(ref, rows, dil) for _, rows, _ in picked])

                qq = tiles(qr_refs[g]).astype(BF16)
                kk = tiles(kr_refs[g], True).astype(BF16)
                vv = tiles(v_refs[g], True).astype(BF16)
                if two_blocks:
                    valid = jnp.stack([_band_mask(i, dil) for i, _, _ in picked])
                else:
                    valid = _causal_mask()[None]
                mine = first[None]
                zero = jnp.zeros_like(qq)
                outs, lses = [], []
                for qh in (jnp.where(mine, qq, zero), jnp.where(mine, zero, qq)):
                    s = jnp.einsum("pqd,pkd->pqk", qh, kk, preferred_element_type=F32)
                    s = jnp.where(valid, s, NEG_INF)
                    m = jnp.max(s, axis=-1, keepdims=True)
                    p = jnp.exp(s - m)
                    l = jnp.sum(p, axis=-1, keepdims=True)
                    outs.append(jnp.einsum("pqk,pkd->pqd", p.astype(BF16), vv, preferred_element_type=F32) * (1.0 / l))
                    lses.append(m + jnp.log(l))
                o = jnp.where(mine, outs[0], outs[1])
                lse = jnp.where(mine, lses[0], lses[1])
                if g > 0:
                    lse_old = tiles(lse_ref)
                    m = jnp.maximum(lse_old, lse)
                    lse_new = m + jnp.log(jnp.exp(lse_old - m) + jnp.exp(lse - m))
                    o = tiles(attn_ref) * jnp.exp(lse_old - lse_new) + o * jnp.exp(lse - lse_new)
                    lse = lse_new
                for j, (_, rows, _) in enumerate(picked):
                    _store_tile(attn_ref, rows, dil, o[j])
                    _store_tile(lse_ref, rows, dil, lse[j])
                return carry

            lax.fori_loop(0, UNITS // UNIT_BATCH, units, 0)

    whole = pl.BlockSpec((SEQ, PAIR), lambda hp: (0, 0))
    out = pl.BlockSpec((SEQ, PAIR), lambda hp: (0, hp))
    return pl.pallas_call(
        body, grid=(ATTN_WIDTH // PAIR,),
        in_specs=_pair_views(0) + _pair_views(QKV_WIDTH) + _pair_views(2 * QKV_WIDTH) + [whole, whole],
        out_specs=[out, out], out_shape=[_sds((SEQ, ATTN_WIDTH)), _sds((SEQ, ATTN_WIDTH))],
        scratch_shapes=[pltpu.VMEM((SEQ, PAIR), F32)] * (2 * ng),
        compiler_params=_cp(("parallel",)), name="attention_fwd")(*([proj] * (3 * ng)), cos_f, sin_s)


def _attention_bwd(g, proj, cos_f, sin_s, d_attn, attn, lse):
    dil = DILATIONS[g]
    two_blocks = SEQ // dil > BLOCK

    def body(q_ref, k_ref, v_ref, cf_ref, ss_ref, do_ref, o_ref, lse_ref, dq_out, dk_out, dv_out,
             qr_ref, kr_ref, dq_acc, dk_acc, dv_acc):
        _rotate([q_ref], [qr_ref], cf_ref, ss_ref, 1.0 / math.sqrt(HEAD_DIM))
        _rotate([k_ref], [kr_ref], cf_ref, ss_ref, 1.0)
        dk_acc[...] = jnp.zeros_like(dk_acc)
        dv_acc[...] = jnp.zeros_like(dv_acc)
        nk = 2 * BLOCK if two_blocks else BLOCK
        first = lax.broadcasted_iota(jnp.int32, (BLOCK, PAIR), 1) < HEAD_DIM
        first_k = lax.broadcasted_iota(jnp.int32, (nk, PAIR), 1) < HEAD_DIM

        def units(t, carry):
            picked = [_unit_pieces(t * UNIT_BATCH + j, dil) for j in range(UNIT_BATCH)]

            def tiles(ref, with_prev=False):
                if with_prev and two_blocks:
                    return jnp.stack([jnp.concatenate([_load_tile(ref, prev, dil), _load_tile(ref, rows, dil)], axis=0)
                                      for _, rows, prev in picked])
                return jnp.stack([_load_tile(ref, rows, dil) for _, rows, _ in picked])

            qq = tiles(qr_ref).astype(BF16)
            kk = tiles(kr_ref, True).astype(BF16)
            vv = tiles(v_ref, True).astype(BF16)
            dof = tiles(do_ref)
            dd = dof * tiles(o_ref)
            lse3 = tiles(lse_ref)
            dob = dof.astype(BF16)
            if two_blocks:
                valid = jnp.stack([_band_mask(i, dil) for i, _, _ in picked])
            else:
                valid = _causal_mask()[None]
            zq, zf = jnp.zeros_like(qq), jnp.zeros_like(dd)
            dqs, dks, dvs = [], [], []
            for head in range(2):
                mine = first[None] if head == 0 else jnp.logical_not(first)[None]
                delta = jnp.sum(jnp.where(mine, dd, zf), axis=-1, keepdims=True)
                lse_h = lse3[:, :, head * HEAD_DIM:head * HEAD_DIM + 1]
                s = jnp.einsum("pqd,pkd->pqk", jnp.where(mine, qq, zq), kk, preferred_element_type=F32)
                p = jnp.where(valid, jnp.exp(s - lse_h), 0.0)
                dp = jnp.einsum("pqd,pkd->pqk", jnp.where(mine, dob, zq), vv, preferred_element_type=F32)
                ds = (p * (dp - delta)).astype(BF16)
                dqs.append(jnp.einsum("pqk,pkd->pqd", ds, kk, preferred_element_type=F32))
                dks.append(jnp.einsum("pqk,pqd->pkd", ds, qq, preferred_element_type=F32))
                dvs.append(jnp.einsum("pqk,pqd->pkd", p.astype(BF16), dob, preferred_element_type=F32))
            dq = jnp.where(first[None], dqs[0], dqs[1])
            dk = jnp.where(first_k[None], dks[0], dks[1])
            dv = jnp.where(first_k[None], dvs[0], dvs[1])
            for j, (_, rows, prev) in enumerate(picked):
                _store_tile(dq_acc, rows, dil, dq[j])
                _store_tile(dk_acc, rows, dil, dk[j, nk - BLOCK:], accumulate=True)
                _store_tile(dv_acc, rows, dil, dv[j, nk - BLOCK:], accumulate=True)
                if two_blocks:
                    _store_tile(dk_acc, prev, dil, dk[j, :BLOCK], accumulate=True)
                    _store_tile(dv_acc, prev, dil, dv[j, :BLOCK], accumulate=True)
            return carry

        lax.fori_loop(0, UNITS // UNIT_BATCH, units, 0)

        def finish(t, carry):
            rows = pl.ds(pl.multiple_of(t * ROPE_ROWS, ROPE_ROWS), ROPE_ROWS)
            cf, ss = cf_ref[rows, :], ss_ref[rows, :]
            dq = dq_acc[rows, :] * (1.0 / math.sqrt(HEAD_DIM))
            dq_out[rows, :] = _rope_transposed(dq, cf, ss).astype(BF16)
            dk_out[rows, :] = _rope_transposed(dk_acc[rows, :], cf, ss).astype(BF16)
            dv_out[rows, :] = dv_acc[rows, :].astype(BF16)
            return carry

        lax.fori_loop(0, SEQ // ROPE_ROWS, finish, 0)

    whole = pl.BlockSpec((SEQ, PAIR), lambda hp: (0, 0))
    pair = pl.BlockSpec((SEQ, PAIR), lambda hp: (0, hp))
    views = [_pair_views(col0)[g] for col0 in (0, QKV_WIDTH, 2 * QKV_WIDTH)]
    return pl.pallas_call(
        body, grid=(ATTN_WIDTH // PAIR,), in_specs=views + [whole, whole, pair, pair, pair],
        out_specs=[pair, pair, pair], out_shape=[_sds((SEQ, ATTN_WIDTH), BF16)] * 3,
        scratch_shapes=[pltpu.VMEM((SEQ, PAIR), F32)] * 5,
        compiler_params=_cp(("parallel",)), name=f"attention_bwd_{g}")(proj, proj, proj, cos_f, sin_s, d_attn, attn, lse)


def _cmul(ar, ai, br, bi):
    return ar * br - ai * bi, ar * bi + ai * br


def _pow256(ar, ai):
    for _ in range(8):
        ar, ai = _cmul(ar, ai, ar, ai)
    return ar, ai


def _chunk_carries(first_r, first_i, pr, pi, reverse):
    rows = lax.broadcasted_iota(jnp.int32, first_r.shape, 0)
    out_r = jnp.zeros_like(first_r)
    out_i = jnp.zeros_like(first_i)
    hr = jnp.zeros_like(first_r[0:1])
    hi = jnp.zeros_like(hr)
    order = range(SCAN_CHUNKS - 1, -1, -1) if reverse else range(SCAN_CHUNKS)
    for c in order:
        out_r = jnp.where(rows == c, hr, out_r)
        out_i = jnp.where(rows == c, hi, out_i)
        tr, ti = _cmul(pr[0:1], pi[0:1], hr, hi)
        hr = first_r[c:c + 1] + tr
        hi = first_i[c:c + 1] + ti
    return out_r, out_i


def _tile(j):
    return pl.ds(pl.multiple_of(j * SCAN_CHUNKS, SCAN_CHUNKS), SCAN_CHUNKS)


def _to_scan_rows(t):
    per = SCAN_STEPS // PHASES
    return t.reshape(PHASES, SCAN_CHUNKS, per, t.shape[1]).transpose(2, 0, 1, 3).reshape(t.shape)


def _from_scan_rows(t):
    per = SCAN_STEPS // PHASES
    return t.reshape(per, PHASES, SCAN_CHUNKS, t.shape[1]).transpose(1, 2, 0, 3).reshape(t.shape)


def _scan_in_place(hr_ref, hi_ref, a_r, a_i):
    def local(j, carry):
        tr, ti = _cmul(a_r, a_i, carry[0], carry[1])
        nr = tr + hr_ref[_tile(j), :]
        ni = ti + hi_ref[_tile(j), :]
        hr_ref[_tile(j), :] = nr
        hi_ref[_tile(j), :] = ni
        return nr, ni

    zero = jnp.zeros_like(a_r)
    last_r, last_i = lax.fori_loop(0, SCAN_STEPS, local, (zero, zero), unroll=4)
    pr, pi = _pow256(a_r, a_i)
    er, ei = _chunk_carries(last_r, last_i, pr, pi, reverse=False)

    def fix(j, carry):
        tr, ti = _cmul(carry[0], carry[1], er, ei)
        hr_ref[_tile(j), :] += tr
        hi_ref[_tile(j), :] += ti
        return _cmul(carry[0], carry[1], a_r, a_i)

    lax.fori_loop(0, SCAN_STEPS, fix, (a_r, a_i), unroll=4)
    return er, ei


def _reverse_scan_in_place(lr_ref, li_ref, hr_ref, hi_ref, er, ei, a_r, a_i):
    def local(t, carry):
        j = SCAN_STEPS - 1 - t
        tr, ti = _cmul(a_r, a_i, carry[0], carry[1])
        nr = tr + lr_ref[_tile(j), :]
        ni = ti + li_ref[_tile(j), :]
        lr_ref[_tile(j), :] = nr
        li_ref[_tile(j), :] = ni
        return nr, ni

    zero = jnp.zeros_like(a_r)
    first_r, first_i = lax.fori_loop(0, SCAN_STEPS, local, (zero, zero), unroll=4)
    pr, pi = _pow256(a_r, a_i)
    nxt_r, nxt_i = _chunk_carries(first_r, first_i, pr, pi, reverse=True)

    def accumulate(lam_r, lam_i, hp_r, hp_i, acc):
        return (acc[0] + lam_r * hp_r + lam_i * hp_i, acc[1] + lam_i * hp_r - lam_r * hp_i)

    def fix(t, carry):
        qr, qi, acc_r, acc_i = carry
        j = SCAN_STEPS - 1 - t
        tr, ti = _cmul(qr, qi, nxt_r, nxt_i)
        lam_r = lr_ref[_tile(j), :] + tr
        lam_i = li_ref[_tile(j), :] + ti
        lr_ref[_tile(j), :] = lam_r
        li_ref[_tile(j), :] = lam_i
        acc_r, acc_i = accumulate(lam_r, lam_i, hr_ref[_tile(j - 1), :], hi_ref[_tile(j - 1), :], (acc_r, acc_i))
        qr, qi = _cmul(qr, qi, a_r, a_i)
        return qr, qi, acc_r, acc_i

    qr, qi, acc_r, acc_i = lax.fori_loop(0, SCAN_STEPS - 1, fix, (a_r, a_i, zero, zero), unroll=4)
    tr, ti = _cmul(qr, qi, nxt_r, nxt_i)
    lam_r = lr_ref[_tile(0), :] + tr
    lam_i = li_ref[_tile(0), :] + ti
    lr_ref[_tile(0), :] = lam_r
    li_ref[_tile(0), :] = lam_i
    acc_r, acc_i = accumulate(lam_r, lam_i, er, ei, (acc_r, acc_i))
    return jnp.sum(acc_r, axis=0, keepdims=True), jnp.sum(acc_i, axis=0, keepdims=True)


def _rope_tables():
    half = HEAD_DIM // 2
    inv_freq = ROPE_THETA ** (-jnp.arange(half, dtype=F32) / half)
    ang = jnp.arange(SEQ, dtype=F32)[:, None] * inv_freq[None, :]
    cos, sin = jnp.cos(ang), jnp.sin(ang)
    cos_f = jnp.concatenate([cos, cos, cos, cos], axis=1)
    sin_s = jnp.concatenate([-sin, sin, -sin, sin], axis=1)
    return cos_f, sin_s


def _ssm_discretise(a_re, a_im, log_dt, b_re, b_im):
    lam = lax.complex(a_re, a_im)
    dt = jnp.exp(log_dt)[:, None]
    a_bar = jnp.exp(lam * dt)
    b_bar = ((a_bar - 1.0) / lam)[..., None] * lax.complex(b_re, b_im)
    return a_bar.real, a_bar.imag, b_bar.real, b_bar.imag


SSM_SLABS = 4
SLAB_GROUPS = SSM_GROUPS // SSM_SLABS
SLAB_IN = SSM_WIDTH // SSM_SLABS
SLAB_STATE = SSM_LANES // SSM_SLABS


def _slab_block_diag(blocks):
    _, r, c = blocks.shape
    eye = jnp.eye(SLAB_GROUPS, dtype=blocks.dtype)
    b5 = blocks.reshape(SSM_SLABS, SLAB_GROUPS, r, 1, c) * eye[None, :, None, :, None]
    return b5.reshape(SSM_SLABS, SLAB_GROUPS * r, SLAB_GROUPS * c)


def _diag_blocks(a, b):
    ra, cb = a.shape[1], b.shape[1]
    wa, wb = ra // SLAB_GROUPS, cb // SLAB_GROUPS
    d = lax.dot_general(a, b, (((0,), (0,)), ((), ())), preferred_element_type=F32)
    row_g = jnp.right_shift(lax.broadcasted_iota(jnp.int32, (ra, cb), 0), wa.bit_length() - 1)
    col_g = jnp.right_shift(lax.broadcasted_iota(jnp.int32, (ra, cb), 1), wb.bit_length() - 1)
    d = jnp.where(row_g == col_g, d, 0.0)
    fold = (jnp.bitwise_and(lax.broadcasted_iota(jnp.int32, (cb, wb), 0), wb - 1)
            == lax.broadcasted_iota(jnp.int32, (cb, wb), 1)).astype(F32)
    return jnp.dot(d, fold, preferred_element_type=F32, precision=lax.Precision.HIGHEST)


def _slab_specs():
    tok = pl.BlockSpec((SEQ, SLAB_IN), lambda j: (0, j))
    state = pl.BlockSpec((SEQ, SLAB_STATE), lambda j: (0, j))
    b_in = pl.BlockSpec((None, SLAB_IN, SLAB_STATE), lambda j: (j, 0, 0))
    c_out = pl.BlockSpec((None, SLAB_STATE, SLAB_IN), lambda j: (j, 0, 0))
    vec = pl.BlockSpec((1, SLAB_STATE), lambda j: (0, j))
    ent = pl.BlockSpec((SCAN_CHUNKS, SLAB_STATE), lambda j: (0, j))
    return tok, state, b_in, c_out, vec, ent


def _ssm_forward(u, b_in_r, b_in_i, c_out_r, c_out_ni, a_r, a_i):
    def body(u_ref, br_ref, bi_ref, cr_ref, ci_ref, ar_ref, ai_ref, y_ref, hr_ref, hi_ref, er_ref, ei_ref):
        uu = u_ref[...]
        hr_ref[...] = jnp.dot(uu, br_ref[...], preferred_element_type=F32)
        hi_ref[...] = jnp.dot(uu, bi_ref[...], preferred_element_type=F32)
        a_re = jnp.broadcast_to(ar_ref[...], (SCAN_CHUNKS, SLAB_STATE))
        a_im = jnp.broadcast_to(ai_ref[...], (SCAN_CHUNKS, SLAB_STATE))
        er_ref[...], ei_ref[...] = _scan_in_place(hr_ref, hi_ref, a_re, a_im)
        y_ref[...] = (jnp.dot(hr_ref[...].astype(BF16), cr_ref[...], preferred_element_type=F32)
                      + jnp.dot(hi_ref[...].astype(BF16), ci_ref[...], preferred_element_type=F32))

    tok, state, b_in, c_out, vec, ent = _slab_specs()
    return pl.pallas_call(
        body, grid=(SSM_SLABS,), in_specs=[tok, b_in, b_in, c_out, c_out, vec, vec],
        out_specs=[tok, state, state, ent, ent],
        out_shape=[_sds((SEQ, SSM_WIDTH)), _sds((SEQ, SSM_LANES)), _sds((SEQ, SSM_LANES)),
                   _sds((SCAN_CHUNKS, SSM_LANES)), _sds((SCAN_CHUNKS, SSM_LANES))],
        compiler_params=_cp(("parallel",)), name="ssm_forward")(u, b_in_r, b_in_i, c_out_r, c_out_ni, a_r, a_i)


def _ssm_backward(d_y, d_u_skip, u, h_r, h_i, e_r, e_i, b_in_r, b_in_i, c_out_r, c_out_ni, a_r, a_i):
    def body(dy_ref, skip_ref, u_ref, hr_ref, hi_ref, er_ref, ei_ref, br_ref, bi_ref, cr_ref, ci_ref, ar_ref, ai_ref,
             du_ref, dar_ref, dai_ref, dcr_ref, dci_ref, dbr_ref, dbi_ref, lr_ref, li_ref):
        dy = dy_ref[...]
        lr_ref[...] = _dot_nt(dy, cr_ref[...])
        li_ref[...] = _dot_nt(dy, ci_ref[...])
        a_re = jnp.broadcast_to(ar_ref[...], (SCAN_CHUNKS, SLAB_STATE))
        a_im = -jnp.broadcast_to(ai_ref[...], (SCAN_CHUNKS, SLAB_STATE))
        dar_ref[...], dai_ref[...] = _reverse_scan_in_place(lr_ref, li_ref, hr_ref, hi_ref, er_ref[...], ei_ref[...],
                                                            a_re, a_im)
        dcr_ref[...] = _diag_blocks(dy, hr_ref[...].astype(BF16))
        dci_ref[...] = _diag_blocks(dy, hi_ref[...].astype(BF16))
        lam_r, lam_i = lr_ref[...].astype(BF16), li_ref[...].astype(BF16)
        uu = u_ref[...]
        dbr_ref[...] = _diag_blocks(uu, lam_r)
        dbi_ref[...] = _diag_blocks(uu, lam_i)
        du = skip_ref[...] + _dot_nt(lam_r, br_ref[...]) + _dot_nt(lam_i, bi_ref[...])
        du_ref[...] = du.astype(BF16)

    tok, state, b_in, c_out, vec, ent = _slab_specs()
    db = pl.BlockSpec((SLAB_IN, SSM_STATE), lambda j: (j, 0))
    return pl.pallas_call(
        body, grid=(SSM_SLABS,), in_specs=[tok, tok, tok, state, state, ent, ent, b_in, b_in, c_out, c_out, vec, vec],
        out_specs=[tok, vec, vec, db, db, db, db],
        out_shape=[_sds((SEQ, SSM_WIDTH), BF16), _sds((1, SSM_LANES)), _sds((1, SSM_LANES))]
        + [_sds((SSM_WIDTH, SSM_STATE))] * 4,
        scratch_shapes=[pltpu.VMEM((SEQ, SLAB_STATE), F32)] * 2,
        compiler_params=_cp(("parallel",)), name="ssm_backward")(
            d_y, d_u_skip, u, h_r, h_i, e_r, e_i, b_in_r, b_in_i, c_out_r, c_out_ni, a_r, a_i)


FF_ROWS = 1024
FF_SHARD = D_FF // N_CHIPS


def _dot_nt(a, b):
    return lax.dot_general(a, b, (((1,), (1,)), ((), ())), preferred_element_type=F32)


def _ffn_up(h, w_gate_t, w_up_t):
    def body(h_ref, wg_ref, wu_ref, a_ref, b_ref, act_ref):
        hb = h_ref[...].astype(BF16)
        a = _dot_nt(hb, wg_ref[...])
        b = _dot_nt(hb, wu_ref[...])
        a_ref[...] = a
        b_ref[...] = b
        act_ref[...] = (a * jax.nn.sigmoid(a) * b).astype(BF16)

    w_spec = pl.BlockSpec((None, FF_SHARD, D_MODEL), lambda i, k: (k, 0, 0))
    o_spec = pl.BlockSpec((None, FF_ROWS, FF_SHARD), lambda i, k: (k, i, 0))
    shape = (N_CHIPS, SEQ, FF_SHARD)
    return pl.pallas_call(
        body, grid=(SEQ // FF_ROWS, N_CHIPS),
        in_specs=[pl.BlockSpec((FF_ROWS, D_MODEL), lambda i, k: (i, 0)), w_spec, w_spec],
        out_specs=[o_spec, o_spec, o_spec], out_shape=[_sds(shape), _sds(shape), _sds(shape, BF16)],
        compiler_params=_cp(("parallel", "parallel")), name="ffn_up")(h, w_gate_t, w_up_t)


def _ffn_down_ln2_loss(act, w_down, h, tgt, ln_g, ln_b):
    def body(act_ref, w_ref, h_ref, tgt_ref, g_ref, b_ref, dz_ref, loss_ref, dg_ref, db_ref, acc):
        i, k = pl.program_id(0), pl.program_id(1)
        part = jnp.dot(act_ref[...], w_ref[...], preferred_element_type=F32)

        @pl.when(k == 0)
        def _():
            acc[...] = part

        @pl.when(k > 0)
        def _():
            acc[...] += part

        @pl.when(k == N_CHIPS - 1)
        def _():
            g = g_ref[...]
            xhat, rstd = _ln_stats(DN_ALPHA * h_ref[...] + acc[...])
            err = xhat * g + b_ref[...] - tgt_ref[...]
            d_out = err * (1.0 / D_MODEL)
            dz_ref[...] = _ln_bwd(d_out, xhat, rstd, g)
            loss_rows = jnp.sum(err * err, axis=-1, keepdims=True) * (0.5 / D_MODEL)
            sums = (jnp.broadcast_to(jnp.sum(loss_rows, axis=0, keepdims=True), loss_ref.shape),
                    _colsum(d_out * xhat), _colsum(d_out))
            for ref, val in zip((loss_ref, dg_ref, db_ref), sums):
                @pl.when(i == 0)
                def _(ref=ref, val=val):
                    ref[...] = val

                @pl.when(i > 0)
                def _(ref=ref, val=val):
                    ref[...] += val

    row = pl.BlockSpec((FF_ROWS, D_MODEL), lambda i, k: (i, 0))
    vec = pl.BlockSpec((1, D_MODEL), lambda i, k: (0, 0))
    return pl.pallas_call(
        body, grid=(SEQ // FF_ROWS, N_CHIPS),
        in_specs=[pl.BlockSpec((None, FF_ROWS, FF_SHARD), lambda i, k: (k, i, 0)),
                  pl.BlockSpec((None, FF_SHARD, D_MODEL), lambda i, k: (k, 0, 0)), row, row, vec, vec],
        out_specs=[row, pl.BlockSpec((1, BLOCK), lambda i, k: (0, 0)), vec, vec],
        out_shape=[_sds((SEQ, D_MODEL)), _sds((1, BLOCK)), _sds((1, D_MODEL)), _sds((1, D_MODEL))],
        scratch_shapes=[pltpu.VMEM((FF_ROWS, D_MODEL), F32)],
        compiler_params=_cp(("arbitrary", "arbitrary")), name="ffn_down_ln2_loss")(act, w_down, h, tgt, ln_g, ln_b)


def _ffn_down_bwd(dz, w_down, a, b):
    def body(dz_ref, wd_ref, a_ref, b_ref, da_ref, db_ref):
        d_act = _dot_nt(dz_ref[...].astype(BF16), wd_ref[...])
        av = a_ref[...]
        sg = jax.nn.sigmoid(av)
        da_ref[...] = (d_act * b_ref[...] * sg * (1.0 + av * (1.0 - sg))).astype(BF16)
        db_ref[...] = (d_act * av * sg).astype(BF16)

    t_spec = pl.BlockSpec((None, FF_ROWS, FF_SHARD), lambda i, k: (k, i, 0))
    shape = (N_CHIPS, SEQ, FF_SHARD)
    return pl.pallas_call(
        body, grid=(SEQ // FF_ROWS, N_CHIPS),
        in_specs=[pl.BlockSpec((FF_ROWS, D_MODEL), lambda i, k: (i, 0)),
                  pl.BlockSpec((None, FF_SHARD, D_MODEL), lambda i, k: (k, 0, 0)), t_spec, t_spec],
        out_specs=[t_spec, t_spec], out_shape=[_sds(shape, BF16), _sds(shape, BF16)],
        compiler_params=_cp(("parallel", "parallel")), name="ffn_down_bwd")(dz, w_down, a, b)


def _ffn_dh(d_a, d_b, w_gate_t, w_up_t):
    def body(da_ref, db_ref, wg_ref, wu_ref, o_ref, acc):
        k = pl.program_id(1)
        part = (jnp.dot(da_ref[...], wg_ref[...], preferred_element_type=F32)
                + jnp.dot(db_ref[...], wu_ref[...], preferred_element_type=F32))

        @pl.when(k == 0)
        def _():
            acc[...] = part

        @pl.when(k > 0)
        def _():
            acc[...] += part

        @pl.when(k == N_CHIPS - 1)
        def _():
            o_ref[...] = acc[...]

    t_spec = pl.BlockSpec((None, FF_ROWS, FF_SHARD), lambda i, k: (k, i, 0))
    w_spec = pl.BlockSpec((None, FF_SHARD, D_MODEL), lambda i, k: (k, 0, 0))
    return pl.pallas_call(
        body, grid=(SEQ // FF_ROWS, N_CHIPS), in_specs=[t_spec, t_spec, w_spec, w_spec],
        out_specs=pl.BlockSpec((FF_ROWS, D_MODEL), lambda i, k: (i, 0)), out_shape=_sds((SEQ, D_MODEL)),
        scratch_shapes=[pltpu.VMEM((FF_ROWS, D_MODEL), F32)],
        compiler_params=_cp(("parallel", "arbitrary")), name="ffn_dh")(d_a, d_b, w_gate_t, w_up_t)


def _local_step(x, tgt, wts, small):
    s = SEQ
    cos_f, sin_s = [_to_phase_rows(t) for t in _rope_tables()]
    x = _reorder_rows(x, to_phase=True, name="phase_rows_x")
    tgt = _reorder_rows(tgt, to_phase=True, name="phase_rows_target")

    proj = _mm_cols(x, wts["w_in"], tm=1024, name="proj")

    attn, lse = _attention_fwd(proj, cos_f, sin_s)

    (abar_r, abar_i, bbar_r, bbar_i), ssm_vjp = jax.vjp(
        _ssm_discretise, small["ssm_a_re"], small["ssm_a_im"], small["ssm_log_dt"], small["ssm_b_re"], small["ssm_b_im"])
    b_in_r, b_in_i = [_slab_block_diag(b.transpose(0, 2, 1)).astype(BF16) for b in (bbar_r, bbar_i)]
    c_out_r = _slab_block_diag(small["ssm_c_re"].transpose(0, 2, 1)).astype(BF16)
    c_out_ni = _slab_block_diag(-small["ssm_c_im"].transpose(0, 2, 1)).astype(BF16)
    a_r, a_i = abar_r.reshape(1, SSM_LANES), abar_i.reshape(1, SSM_LANES)
    d_skip = small["ssm_d"].reshape(1, SSM_WIDTH)

    u_f = _to_scan_rows(proj[:, 3 * QKV_WIDTH:3 * QKV_WIDTH + SSM_WIDTH])
    u_p = u_f.astype(BF16)
    y_c, h_r, h_i, e_r, e_i = _ssm_forward(u_p, b_in_r, b_in_i, c_out_r, c_out_ni, a_r, a_i)

    def gelu_fwd(yc, u, dsk):
        y = yc + dsk * u
        return y, 0.5 * y * (1.0 + jnp.tanh(GELU_C * (y + GELU_K * y * y * y)))

    y_s5, gel = _rowwise(gelu_fwd, [y_c, u_f], [d_skip], [_sds((s, SSM_WIDTH)), _sds((s, SSM_WIDTH), BF16)],
                         tm=512, name="ssm_gelu")
    glu = _mm_cols(gel, wts["w_glu"], tm=s, name="glu")

    def glu_fwd(ga, gb):
        return ga * jax.nn.sigmoid(gb)

    (y_glu,) = _rowwise(glu_fwd, [(glu, SSM_WIDTH, 0), (glu, SSM_WIDTH, 1)], [], [_sds((s, SSM_WIDTH), BF16)],
                        tm=512, name="glu_gate")
    y_glu = _from_scan_rows(y_glu)

    gl0 = (proj, D_MODEL, (3 * QKV_WIDTH + SSM_WIDTH) // D_MODEL)
    gl1 = (proj, D_MODEL, (3 * QKV_WIDTH + SSM_WIDTH) // D_MODEL + 1)
    b_gate = small["b_gate"]
    w_out = wts["w_out"].reshape(D_MODEL, D_MODEL)

    def branch(t, wg):
        return jnp.concatenate([jnp.dot(t, wg[k], preferred_element_type=F32) for k in range(N_CHIPS)], axis=1)

    def mix_ln1(l0, l1, at, yg, xv, bg, wa, ws, wo, g, b):
        ya = branch(at.astype(BF16), wa)
        ys = branch(yg, ws)
        mixed = (jax.nn.sigmoid(l0 + bg[0:1]) * ya + jax.nn.sigmoid(l1 + bg[1:2]) * ys).astype(BF16)
        z = DN_ALPHA * xv + jnp.dot(mixed, wo, preferred_element_type=F32)
        xhat, _ = _ln_stats(z)
        return ya, ys, mixed, z, xhat * g + b

    y_attn, y_ssm, mixed, z1, h = _rowwise(
        mix_ln1, [gl0, gl1, attn, y_glu, x],
        [b_gate, wts["w_attn_br"], wts["w_ssm_br"], w_out, small["ln1_g"], small["ln1_b"]],
        [_sds((s, D_MODEL)), _sds((s, D_MODEL)), _sds((s, D_MODEL), BF16), _sds((s, D_MODEL)), _sds((s, D_MODEL))],
        tm=256, name="mix_ln1")

    nf = D_FF // N_CHIPS
    w_gate_t, w_up_t, w_down = wts["w_ff_gate"], wts["w_ff_up"], wts["w_ff_down"]
    ff_a, ff_b, act = _ffn_up(h, w_gate_t, w_up_t)
    dz2, loss_v, d_ln2_g, d_ln2_b = _ffn_down_ln2_loss(act, w_down, h, tgt, small["ln2_g"], small["ln2_b"])

    d_a, d_b = _ffn_down_bwd(dz2, w_down, ff_a, ff_b)

    def grad_rows(lhs, rhs, name):
        return _matmul(lhs, rhs, grid=(N_CHIPS,), a_spec=pl.BlockSpec((None, s, nf), lambda k: (k, 0, 0)),
                       b_spec=pl.BlockSpec((s, D_MODEL), lambda k: (0, 0)),
                       o_spec=pl.BlockSpec((None, nf, D_MODEL), lambda k: (k, 0, 0)),
                       out_shape=_sds((N_CHIPS, nf, D_MODEL), BF16), dims=(0, 0), name=name)

    g_w_ff_down = grad_rows(act, dz2, "g_w_ff_down")
    g_w_ff_gate = grad_rows(d_a, h, "g_w_ff_gate")
    g_w_ff_up = grad_rows(d_b, h, "g_w_ff_up")
    dh_ff = _ffn_dh(d_a, d_b, w_gate_t, w_up_t)

    def ln1_gate_bwd(dz, dff, z, l0, l1, ya, ys, g, bg, wo):
        xhat, rstd = _ln_stats(z)
        dh = DN_ALPHA * dz + dff
        dz_in = _ln_bwd(dh, xhat, rstd, g)
        dm = _dot_nt(dz_in.astype(BF16), wo)
        g0 = jax.nn.sigmoid(l0 + bg[0:1])
        g1 = jax.nn.sigmoid(l1 + bg[1:2])
        dl0 = dm * ya * g0 * (1.0 - g0)
        dl1 = dm * ys * g1 * (1.0 - g1)
        return (dz_in, dm * g0, dm * g1, jnp.concatenate([dl0, dl1], axis=1),
                _colsum(dh * xhat), _colsum(dh), _colsum(dl0), _colsum(dl1))

    dz1, d_y_attn, d_y_ssm, d_gl, d_ln1_g, d_ln1_b, d_bg0, d_bg1 = _rowwise(
        ln1_gate_bwd, [dz2, dh_ff, z1, gl0, gl1, y_attn, y_ssm], [small["ln1_g"], b_gate, w_out],
        [_sds((s, D_MODEL)), _sds((s, D_MODEL), BF16), _sds((s, D_MODEL), BF16), _sds((s, 2 * D_MODEL), BF16)],
        [_sds((1, D_MODEL))] * 4, tm=256, name="ln1_gate_bwd")
    g_w_out = _mm_plain(mixed, dz1, tm=D_MODEL, tn=512, dims=(0, 0), out_dtype=BF16, name="g_w_out")
    g_w_out = g_w_out.reshape(N_CHIPS, D_MODEL // N_CHIPS, D_MODEL)

    g_w_ssm_br = _mm_cols_tn(y_glu, d_y_ssm, ns=D_MODEL // N_CHIPS, name="g_w_ssm_br")
    d_y_glu = _to_scan_rows(_mm_cols_nt(d_y_ssm, wts["w_ssm_br"], tm=s, name="d_y_glu"))

    def glu_bwd(dy, ga, gb):
        sg = jax.nn.sigmoid(gb)
        return jnp.concatenate([dy * sg, dy * ga * sg * (1.0 - sg)], axis=1)

    (d_glu,) = _rowwise(glu_bwd, [d_y_glu, (glu, SSM_WIDTH, 0), (glu, SSM_WIDTH, 1)], [],
                        [_sds((s, 2 * SSM_WIDTH), BF16)], tm=512, name="glu_bwd")
    g_w_glu = _mm_cols_tn(gel, d_glu, ns=2 * SSM_WIDTH // N_CHIPS, name="g_w_glu")
    d_gel = _mm_cols_nt(d_glu, wts["w_glu"], tm=s, name="d_gel")

    def gelu_bwd(dg, y, u, dsk):
        th = jnp.tanh(GELU_C * (y + GELU_K * y * y * y))
        dy = dg * (0.5 * (1.0 + th) + 0.5 * y * (1.0 - th * th) * GELU_C * (1.0 + 3.0 * GELU_K * y * y))
        return dy, dy * dsk, _colsum(dy * u)

    d_y, d_u_skip, d_ssm_d = _rowwise(gelu_bwd, [d_gel, y_s5, u_f], [d_skip],
                                      [_sds((s, SSM_WIDTH), BF16), _sds((s, SSM_WIDTH))], [_sds((1, SSM_WIDTH))],
                                      tm=512, name="gelu_bwd")
    d_u, d_abar_r, d_abar_i, d_c_r, d_c_ni, d_bin_r, d_bin_i = _ssm_backward(
        d_y, d_u_skip, u_p, h_r, h_i, e_r, e_i, b_in_r, b_in_i, c_out_r, c_out_ni, a_r, a_i)
    d_u = _from_scan_rows(d_u)
    d_bbar_r = d_bin_r.reshape(SSM_GROUPS, SSM_GROUP, SSM_STATE).transpose(0, 2, 1)
    d_bbar_i = d_bin_i.reshape(SSM_GROUPS, SSM_GROUP, SSM_STATE).transpose(0, 2, 1)
    d_a_re, d_a_im, d_log_dt, d_b_re, d_b_im = ssm_vjp(
        (d_abar_r.reshape(SSM_GROUPS, SSM_STATE), d_abar_i.reshape(SSM_GROUPS, SSM_STATE), d_bbar_r, d_bbar_i))
    d_c_re = d_c_r.reshape(SSM_GROUPS, SSM_GROUP, SSM_STATE)
    d_c_im = -d_c_ni.reshape(SSM_GROUPS, SSM_GROUP, SSM_STATE)

    g_w_attn_br = _mm_cols_tn(attn, d_y_attn, ns=D_MODEL // N_CHIPS, name="g_w_attn_br")
    d_attn = _mm_cols_nt(d_y_attn, wts["w_attn_br"], tm=s, name="d_attn")
    dqkv = [_attention_bwd(g, proj, cos_f, sin_s, d_attn, attn, lse) for g in range(len(DILATIONS))]

    d_proj = jnp.concatenate([dqkv[g][j] for j in range(3) for g in range(len(DILATIONS))] + [d_u, d_gl],
                             axis=1)
    g_w_in = _mm_cols_tn(x, d_proj, ns=IN_WIDTH // N_CHIPS, name="g_w_in")

    def grad_x_after(after):
        dx_proj = _mm_cols_nt(d_proj, wts["w_in"], tm=1024, name="dx_proj", after=after)
        return _reorder_rows(dz1, dx_proj, to_phase=False, name="grad_x", scale=DN_ALPHA)

    big = {"w_in": g_w_in, "w_attn_br": g_w_attn_br, "w_ssm_br": g_w_ssm_br, "w_out": g_w_out, "w_glu": g_w_glu,
           "w_ff_gate": g_w_ff_gate, "w_ff_up": g_w_ff_up, "w_ff_down": g_w_ff_down}
    small_g = {"b_gate": jnp.concatenate([d_bg0, d_bg1], axis=0), "ssm_a_re": d_a_re, "ssm_a_im": d_a_im,
               "ssm_log_dt": d_log_dt, "ssm_b_re": d_b_re, "ssm_b_im": d_b_im, "ssm_c_re": d_c_re, "ssm_c_im": d_c_im,
               "ssm_d": d_ssm_d.reshape(SSM_WIDTH), "ln1_g": d_ln1_g, "ln1_b": d_ln1_b, "ln2_g": d_ln2_g,
               "ln2_b": d_ln2_b}
    marks = {"ln1_bwd": dz1, "scan_bwd": d_abar_r, "attention_bwd_0": dqkv[0][0]}
    return loss_v[0, 0], grad_x_after, big, small_g, marks


GATHER_ID, SWAP_ID, SCATTER_ID, JOIN_ID, EXCHANGE_ID = 1, 2, 3, 4, 5


def _place():
    return lax.axis_index("x"), lax.axis_index("y"), lax.axis_index("c")


def _other_chips(x, y):
    return [(1 - x, y), (x, 1 - y), (1 - x, 1 - y)]


def _handshake(peers):
    barrier = pltpu.get_barrier_semaphore()
    for peer in peers:
        pl.semaphore_signal(barrier, inc=1, device_id=peer, device_id_type=MESH)
    pl.semaphore_wait(barrier, len(peers))


def _sequencer(body, arrays, out_type, sems, collective_id, name):
    return pl.kernel(body, name=name, out_type=out_type,
                     mesh=plsc.ScalarSubcoreMesh(axis_name="sequencer", num_cores=1), scratch_types=sems,
                     compiler_params=pltpu.CompilerParams(collective_id=collective_id))(*arrays)


def _gather_weights(shards, *, name):
    nw = len(shards)

    def body(*refs):
        ins, outs = refs[:nw], refs[nw:2 * nw]
        send_sems, recv_sems, pass_send, pass_recv, local_sems = refs[2 * nw:]
        x, y, c = _place()
        chip = 2 * x + y
        chips = _other_chips(x, y)
        _handshake([(x, y, 1 - c)] + [(cx, cy, c) for cx, cy in chips])
        started = []
        for w in range(nw):
            hw = shards[w].shape[0] // 2
            mine = pl.ds(c * hw, hw)
            own = pltpu.make_async_copy(ins[w], outs[w].at[chip], local_sems.at[w])
            own.start()
            started.append(own)
            for j, (cx, cy) in enumerate(chips):
                cp = pltpu.make_async_remote_copy(
                    src_ref=ins[w].at[mine], dst_ref=outs[w].at[chip, mine], send_sem=send_sems.at[w, j],
                    recv_sem=recv_sems.at[w, j], device_id=(cx, cy, c), device_id_type=MESH)
                cp.start()
                started.append(cp)
        passed = []
        for w in range(nw):
            hw = shards[w].shape[0] // 2
            mine = pl.ds(c * hw, hw)
            for j, (cx, cy) in enumerate(chips):
                landed = outs[w].at[2 * cx + cy, mine]
                pltpu.make_async_remote_copy(
                    src_ref=ins[w].at[mine], dst_ref=landed, send_sem=send_sems.at[w, j],
                    recv_sem=recv_sems.at[w, j], device_id=(cx, cy, c), device_id_type=MESH).wait_recv()
                cp = pltpu.make_async_remote_copy(
                    src_ref=landed, dst_ref=landed, send_sem=pass_send.at[w, j], recv_sem=pass_recv.at[w, j],
                    device_id=(x, y, 1 - c), device_id_type=MESH)
                cp.start()
                passed.append(cp)
        for w in range(nw):
            hw = shards[w].shape[0] // 2
            theirs = pl.ds((1 - c) * hw, hw)
            for j, (cx, cy) in enumerate(chips):
                landed = outs[w].at[2 * cx + cy, theirs]
                pltpu.make_async_remote_copy(
                    src_ref=landed, dst_ref=landed, send_sem=pass_send.at[w, j], recv_sem=pass_recv.at[w, j],
                    device_id=(x, y, 1 - c), device_id_type=MESH).wait_recv()
        for cp in started[0::4]:
            cp.wait()
        for cp in [s for i, s in enumerate(started) if i % 4] + passed:
            cp.wait_send()

    sem = pltpu.SemaphoreType.DMA
    return _sequencer(body, shards, [_sds((N_CHIPS,) + a.shape, a.dtype) for a in shards],
                      [sem((nw, 3)), sem((nw, 3)), sem((nw, 3)), sem((nw, 3)), sem((nw,))], GATHER_ID, name)


def _swap_other_halves(grads, *, name):
    nw = len(grads)

    def body(*refs):
        ins, outs = refs[:nw], refs[nw:2 * nw]
        send_sems, recv_sems = refs[2 * nw:]
        x, y, c = _place()
        _handshake([(x, y, 1 - c)])
        cps = []
        for w in range(nw):
            hw = grads[w].shape[1] // 2
            cp = pltpu.make_async_remote_copy(
                src_ref=ins[w].at[:, pl.ds((1 - c) * hw, hw)], dst_ref=outs[w], send_sem=send_sems.at[w],
                recv_sem=recv_sems.at[w], device_id=(x, y, 1 - c), device_id_type=MESH)
            cp.start()
            cps.append(cp)
        for cp in cps:
            cp.wait()

    sem = pltpu.SemaphoreType.DMA
    return _sequencer(body, grads, [_sds((N_CHIPS, g.shape[1] // 2, g.shape[2]), g.dtype) for g in grads],
                      [sem((nw,)), sem((nw,))], SWAP_ID, name)


def _add_my_halves(core, grads, others, *, name, after=()):
    nw = len(grads)
    halves = [g.shape[1] // 2 for g in grads]

    def body(core_ref, *refs):
        outs = refs[2 * nw + len(after):]
        for g_ref, o_ref, out_ref in zip(refs[:nw], refs[nw:2 * nw], outs):
            out_ref[...] = (g_ref[...].astype(F32) + o_ref[...].astype(F32)).astype(out_ref.dtype)

    in_specs = [pl.BlockSpec((None, None, hw, g.shape[2]), lambda s, core_ref: (s, core_ref[0], 0, 0))
                for g, hw in zip(grads, halves)]
    in_specs += [pl.BlockSpec((None, hw, g.shape[2]), lambda s, core_ref: (s, 0, 0)) for g, hw in zip(grads, halves)]
    return pl.pallas_call(
        body,
        grid_spec=pltpu.PrefetchScalarGridSpec(
            num_scalar_prefetch=1, grid=(N_CHIPS,), in_specs=in_specs + [HBM_OPERAND] * len(after),
            out_specs=[pl.BlockSpec((None, hw, g.shape[2]), lambda s, core_ref: (s, 0, 0))
                       for g, hw in zip(grads, halves)]),
        out_shape=[_sds((N_CHIPS, hw, g.shape[2]), BF16) for g, hw in zip(grads, halves)],
        compiler_params=_cp(("parallel",)), name=name)(
            core, *[g.reshape(N_CHIPS, 2, hw, g.shape[2]) for g, hw in zip(grads, halves)], *others, *after)


def _scatter_partials(parts, *, name):
    nw = len(parts)

    def body(*refs):
        ins, outs = refs[:nw], refs[nw:2 * nw]
        send_sems, recv_sems = refs[2 * nw:]
        x, y, c = _place()
        _handshake([(cx, cy, c) for cx, cy in _other_chips(x, y)])
        cps = []
        for w in range(nw):
            for j, (cx, cy) in enumerate(_other_chips(x, y)):
                cp = pltpu.make_async_remote_copy(
                    src_ref=ins[w].at[2 * cx + cy], dst_ref=outs[w].at[j], send_sem=send_sems.at[w, j],
                    recv_sem=recv_sems.at[w, j], device_id=(cx, cy, c), device_id_type=MESH)
                cp.start()
                cps.append(cp)
        for cp in cps:
            cp.wait()

    sem = pltpu.SemaphoreType.DMA
    return _sequencer(body, parts, [_sds((3,) + p.shape[1:], p.dtype) for p in parts],
                      [sem((nw, 3)), sem((nw, 3))], SCATTER_ID, name)


SUM_STEPS = 2


def _sum_partials(chip, parts, recvd, *, name, after=()):
    nw = len(parts)
    rows = [p.shape[1] // SUM_STEPS for p in parts]

    def body(chip_ref, *refs):
        outs = refs[2 * nw + len(after):]
        for p_ref, r_ref, out_ref in zip(refs[:nw], refs[nw:2 * nw], outs):
            acc = p_ref[...].astype(F32)
            for j in range(3):
                acc = acc + r_ref[j].astype(F32)
            out_ref[...] = acc

    in_specs = [pl.BlockSpec((None, th, p.shape[2]), lambda i, chip_ref: (chip_ref[0], i, 0))
                for p, th in zip(parts, rows)]
    in_specs += [pl.BlockSpec((3, th, p.shape[2]), lambda i, chip_ref: (0, i, 0)) for p, th in zip(parts, rows)]
    return pl.pallas_call(
        body,
        grid_spec=pltpu.PrefetchScalarGridSpec(
            num_scalar_prefetch=1, grid=(SUM_STEPS,), in_specs=in_specs + [HBM_OPERAND] * len(after),
            out_specs=[pl.BlockSpec((th, p.shape[2]), lambda i, chip_ref: (i, 0)) for p, th in zip(parts, rows)]),
        out_shape=[_sds(p.shape[1:]) for p in parts], compiler_params=_cp(("parallel",)), name=name)(
            chip, *parts, *recvd, *after)


def _swap_reduced_halves(halves, *, name):
    nw = len(halves)

    def body(*refs):
        ins, outs = refs[:nw], refs[nw:2 * nw]
        send_sems, recv_sems = refs[2 * nw:]
        x, y, c = _place()
        _handshake([(x, y, 1 - c)])
        cps = []
        for w in range(nw):
            cp = pltpu.make_async_remote_copy(
                src_ref=ins[w], dst_ref=outs[w], send_sem=send_sems.at[w], recv_sem=recv_sems.at[w],
                device_id=(x, y, 1 - c), device_id_type=MESH)
            cp.start()
            cps.append(cp)
        for cp in cps:
            cp.wait()

    sem = pltpu.SemaphoreType.DMA
    return _sequencer(body, halves, [_sds(h.shape, h.dtype) for h in halves], [sem((nw,)), sem((nw,))], JOIN_ID, name)


def _exchange_rows(vec, *, name):
    def body(v_ref, slots, send_sems, recv_sems, local_sem):
        x, y, c = _place()
        me = 4 * x + 2 * y + c
        peers = []
        for mask in range(1, N_DEV):
            peers.append((1 - x if mask & 4 else x, 1 - y if mask & 2 else y, 1 - c if mask & 1 else c))
        _handshake(peers)
        own = pltpu.make_async_copy(v_ref, slots.at[me], local_sem)
        own.start()
        cps = []
        for k, peer in enumerate(peers):
            cp = pltpu.make_async_remote_copy(
                src_ref=v_ref, dst_ref=slots.at[me], send_sem=send_sems.at[k], recv_sem=recv_sems.at[k],
                device_id=peer, device_id_type=MESH)
            cp.start()
            cps.append(cp)
        for k, (px, py, pc) in enumerate(peers):
            pltpu.make_async_remote_copy(
                src_ref=v_ref, dst_ref=slots.at[4 * px + 2 * py + pc], send_sem=send_sems.at[k],
                recv_sem=recv_sems.at[k], device_id=(px, py, pc), device_id_type=MESH).wait_recv()
        for cp in cps:
            cp.wait_send()
        own.wait()

    sem = pltpu.SemaphoreType.DMA
    return _sequencer(body, [vec], [_sds((N_DEV,) + vec.shape)], [sem((N_DEV - 1,)), sem((N_DEV - 1,)), sem(())],
                      EXCHANGE_ID, name)[0]


def _sum_slots(slots, *, name, after=()):
    def body(s_ref, *rest):
        out_ref = rest[len(after)]
        acc = s_ref[0]
        for d in range(1, N_DEV):
            acc = acc + s_ref[d]
        out_ref[...] = acc

    vmem = pl.BlockSpec(memory_space=pltpu.VMEM)
    return pl.pallas_call(
        body, in_specs=[vmem] + [HBM_OPERAND] * len(after), out_specs=vmem, out_shape=_sds(slots.shape[1:]),
        compiler_params=pltpu.CompilerParams(vmem_limit_bytes=VMEM_LIMIT_BYTES), name=name)(slots, *after)


def _reduce_scatter_start(grads, core, *, tag, add_after=()):
    others = _swap_other_halves(grads, name="swap_other_halves_" + tag)
    parts = _add_my_halves(core, grads, others, name="add_my_halves_" + tag, after=add_after)
    return parts, _scatter_partials(parts, name="scatter_partials_" + tag)


def _reduce_scatter_finish(parts, recvd, chip, *, tag, sum_after=()):
    mine = _sum_partials(chip, parts, recvd, name="sum_partials_" + tag, after=sum_after)
    return mine, _swap_reduced_halves(mine, name="swap_reduced_halves_" + tag)


ADAM_BLOCK_ELEMS = 256 * 1024


def _adam_rows(rows, cols):
    tm = rows
    while tm * cols > ADAM_BLOCK_ELEMS and tm % 16 == 0:
        tm //= 2
    return tm


def _adam_step(wv, gv, mv, vv):
    m2 = ADAM_B1 * mv + (1.0 - ADAM_B1) * gv
    v2 = ADAM_B2 * vv + (1.0 - ADAM_B2) * (gv * gv)
    m_hat = m2 / (1.0 - ADAM_B1 ** ADAM_STEP)
    v_hat = v2 / (1.0 - ADAM_B2 ** ADAM_STEP)
    return -ADAM_LR * (m_hat / (jnp.sqrt(v_hat) + ADAM_EPS) + ADAM_WD * wv), m2, v2


def _adamw(w, g, m, v, *, name):
    rows, cols = w.shape
    return _rowwise(_adam_step, [w, g, m, v], [], [_sds((rows, cols))] * 3, tm=_adam_rows(rows, cols), name=name)


def _adamw_halves(core, w, g_mine, g_theirs, m, v, *, name, after=()):
    rows, cols = w.shape
    hw = rows // 2
    tm = _adam_rows(hw, cols)
    per_half = hw // tm

    def body(core_ref, w_ref, gm_ref, gt_ref, m_ref, v_ref, *rest):
        g_out, d_out, m_out, v_out = rest[len(after):]
        mine = (pl.program_id(0) // per_half) == core_ref[0]
        g = jnp.where(mine, gm_ref[...], gt_ref[...])
        d, m2, v2 = _adam_step(w_ref[...], g, m_ref[...], v_ref[...])
        g_out[...] = g
        d_out[...] = d
        m_out[...] = m2
        v_out[...] = v2

    full = pl.BlockSpec((tm, cols), lambda i, core_ref: (i, 0))
    half = pl.BlockSpec((tm, cols), lambda i, core_ref: (i % per_half, 0))
    return pl.pallas_call(
        body,
        grid_spec=pltpu.PrefetchScalarGridSpec(
            num_scalar_prefetch=1, grid=(rows // tm,),
            in_specs=[full, half, half, full, full] + [HBM_OPERAND] * len(after), out_specs=[full, full, full, full]),
        out_shape=[_sds((rows, cols))] * 4, compiler_params=_cp(("parallel",)), name=name)(
            core, w, g_mine, g_theirs, m, v, *after)


HELD_TRANSPOSED = ("w_ff_gate", "w_ff_up")


def _as_rows(name, arr):
    return arr[0].T if name in HELD_TRANSPOSED else arr[0]


def _from_rows(name, arr2d):
    return (arr2d.T if name in HELD_TRANSPOSED else arr2d)[None]


STORED_SWAPPED = ("ssm_b_re", "ssm_b_im")


def _as_stored(name, arr):
    return jnp.swapaxes(arr, -1, -2) if name in STORED_SWAPPED else arr


def _pack_rows(arrs):
    flat = jnp.concatenate([a.reshape(-1).astype(F32) for a in arrs])
    rows = -(-flat.shape[0] // 1024) * 8
    return jnp.pad(flat, (0, rows * 128 - flat.shape[0])).reshape(rows, 128)


def _unpack_rows(vec, shapes):
    flat = vec.reshape(-1)
    out, off = [], 0
    for shp in shapes:
        size = math.prod(shp)
        out.append(flat[off:off + size].reshape(shp))
        off += size
    return out


SMALL = ("b_gate", "ssm_a_re", "ssm_a_im", "ssm_log_dt", "ssm_b_re", "ssm_b_im", "ssm_c_re", "ssm_c_im", "ssm_d",
         "ln1_g", "ln1_b", "ln2_g", "ln2_b")
GATHER_GROUPS = (("w_in", ("w_in",)), ("mixer", ("w_attn_br", "w_ssm_br", "w_glu", "w_out")),
                 ("ffn", ("w_ff_gate", "w_ff_up", "w_ff_down")))
REDUCE_GROUPS = (("ffn", ("w_ff_down", "w_ff_gate", "w_ff_up")),
                 ("mixer", ("w_out", "w_ssm_br", "w_glu", "w_attn_br")), ("w_in", ("w_in",)))
WEIGHTS = ("w_in", "b_gate", "w_attn_br", "w_ssm_br", "w_out", "ssm_a_re", "ssm_a_im", "ssm_log_dt", "ssm_b_re",
           "ssm_b_im", "ssm_c_re", "ssm_c_im", "ssm_d", "w_glu", "ln1_g", "ln1_b", "w_ff_gate", "w_ff_up", "w_ff_down",
           "ln2_g", "ln2_b")


def kernel(x, w_in, b_gate, w_attn_br, w_ssm_br, w_out, ssm_a_re, ssm_a_im, ssm_log_dt, ssm_b_re, ssm_b_im, ssm_c_re, ssm_c_im, ssm_d, w_glu, ln1_g, ln1_b, w_ff_gate, w_ff_up, w_ff_down, ln2_g, ln2_b, loss_target, m_w_in, m_b_gate, m_w_attn_br, m_w_ssm_br, m_w_out, m_ssm_a_re, m_ssm_a_im, m_ssm_log_dt, m_ssm_b_re, m_ssm_b_im, m_ssm_c_re, m_ssm_c_im, m_ssm_d, m_w_glu, m_ln1_g, m_ln1_b, m_w_ff_gate, m_w_ff_up, m_w_ff_down, m_ln2_g, m_ln2_b, v_w_in, v_b_gate, v_w_attn_br, v_w_ssm_br, v_w_out, v_ssm_a_re, v_ssm_a_im, v_ssm_log_dt, v_ssm_b_re, v_ssm_b_im, v_ssm_c_re, v_ssm_c_im, v_ssm_d, v_w_glu, v_ln1_g, v_ln1_b, v_w_ff_gate, v_w_ff_up, v_w_ff_down, v_ln2_g, v_ln2_b):
    given = dict(locals())
    px, py, pc = _place()
    chip = 2 * px + py
    core_s = jnp.reshape(pc, (1,)).astype(jnp.int32)
    chip_s = jnp.reshape(chip, (1,)).astype(jnp.int32)

    wts = {}
    for tag, names in GATHER_GROUPS:
        wts.update(zip(names, _gather_weights([_as_rows(n, given[n]).astype(BF16) for n in names],
                                              name="gather_" + tag)))
    ncol = D_MODEL // N_CHIPS
    bg_mine = jnp.where(pc == 0, b_gate[0], jnp.zeros_like(b_gate[0]))
    bg_full = lax.dynamic_update_slice(jnp.zeros((2, D_MODEL), F32), bg_mine, (0, chip * ncol))
    bg_slots = _exchange_rows(bg_full.reshape(16, 128), name="exchange_gate_bias")
    bg_full = _sum_slots(bg_slots, name="sum_gate_bias").reshape(2, D_MODEL)
    small = {n: given[n][0] for n in SMALL if n.startswith("ssm")}
    small.update({n: given[n] for n in ("ln1_g", "ln1_b", "ln2_g", "ln2_b")})
    small["b_gate"] = bg_full

    loss_mine, grad_x_after, big_g, small_g, marks = _local_step(x[0], loss_target[0], wts, small)

    groups = dict(REDUCE_GROUPS)
    parts, recvd = {}, {}
    grads, delta, new_m, new_v = {}, {}, {}, {}

    def start(tag, add_after):
        parts[tag], recvd[tag] = _reduce_scatter_start([big_g[n] for n in groups[tag]], core_s, tag=tag,
                                                       add_after=add_after)

    def finish(tag, sum_after, adam_after):
        mine, theirs = _reduce_scatter_finish(parts[tag], recvd[tag], chip_s, tag=tag, sum_after=sum_after)
        for n, g_mine, g_theirs in zip(groups[tag], mine, theirs):
            res = _adamw_halves(core_s, _as_rows(n, given[n]), g_mine, g_theirs, _as_rows(n, given["m_" + n]),
                                _as_rows(n, given["v_" + n]), name="adamw_" + n, after=adam_after)
            grads[n], delta[n], new_m[n], new_v[n] = [_from_rows(n, r) for r in res]

    start("ffn", (marks["ln1_bwd"],))
    start("mixer", (marks["scan_bwd"],))
    finish("mixer", (marks["attention_bwd_0"],), (big_g["w_in"],))
    start("w_in", tuple(delta[n] for n in groups["mixer"]))
    in_flight = (parts["w_in"][0],)
    grad_x = grad_x_after(in_flight)
    finish("ffn", (marks["scan_bwd"],), in_flight)
    stored = [_as_stored(n, small_g[n]) for n in SMALL] + [loss_mine.reshape(1)]
    slots = _exchange_rows(_pack_rows(stored), name="exchange_small")
    summed = _unpack_rows(_sum_slots(slots, name="sum_small", after=in_flight), [a.shape for a in stored])
    loss = summed.pop()[0]
    for n, g in zip(SMALL, summed):
        g = _as_stored(n, g)
        if n == "b_gate":
            g = lax.dynamic_slice(g, (0, chip * ncol), (2, ncol))
        grads[n] = g.reshape(given[n].shape)
    packed = [_pack_rows([_as_stored(n, src[n]) for n in SMALL]) for src in
              (given, grads, {n: given["m_" + n] for n in SMALL}, {n: given["v_" + n] for n in SMALL})]
    shapes = [_as_stored(n, given[n]).shape for n in SMALL]
    small_out = _adamw(*packed, name="adamw_small")
    for out, vec in zip((delta, new_m, new_v), small_out):
        out.update((n, _as_stored(n, a)) for n, a in zip(SMALL, _unpack_rows(vec, shapes)))
    behind = [delta[n] for n in groups["ffn"]] + [small_out[0], grad_x]
    finish("w_in", tuple(behind), ())

    return (loss, grad_x.reshape(x.shape), *[grads[n] for n in WEIGHTS], *[delta[n] for n in WEIGHTS],
            *[new_m[n] for n in WEIGHTS], *[new_v[n] for n in WEIGHTS])
```

```python
import math

import jax
import jax.numpy as jnp
from jax import lax
from jax.experimental import pallas as pl
from jax.experimental.pallas import tpu as pltpu
from jax.experimental.pallas import tpu_sc as plsc

F32 = jnp.float32
BF16 = jnp.bfloat16
MESH = pl.DeviceIdType.MESH

D_MODEL = 1024
SEQ = 2048
HEAD_DIM = 64
ATTN_HEADS = 8
DILATIONS = (1, 4, 16)
ATTN_WIDTH = ATTN_HEADS * HEAD_DIM
QKV_WIDTH = 3 * ATTN_WIDTH
BLOCK = 128
ROPE_THETA = 10000.0
NEG_INF = -1e30
SSM_GROUP = 16
SSM_GROUPS = 32
SSM_WIDTH = 512
SSM_STATE = 64
SSM_LANES = SSM_GROUPS * SSM_STATE
SCAN_CHUNKS = 8
SCAN_STEPS = SEQ // SCAN_CHUNKS
IN_WIDTH = 3 * QKV_WIDTH + SSM_WIDTH + 2 * D_MODEL
D_FF = 2816
N_CHIPS = 4
N_DEV = 8
DN_ALPHA = 2.0 ** 0.25
LN_EPS = 1e-5
ADAM_LR = 0.001
ADAM_B1 = 0.9
ADAM_B2 = 0.999
ADAM_EPS = 1e-08
ADAM_WD = 0.01
ADAM_STEP = 10
GELU_C = math.sqrt(2.0 / math.pi)
GELU_K = 0.044715

VMEM_LIMIT_BYTES = 56 * 1024 * 1024


def _sds(shape, dtype=F32):
    return jax.ShapeDtypeStruct(tuple(shape), dtype)


def _cp(semantics=None):
    return pltpu.CompilerParams(dimension_semantics=semantics, vmem_limit_bytes=VMEM_LIMIT_BYTES)


HBM_OPERAND = pl.BlockSpec(memory_space=pl.ANY)


def _matmul(a, b, *, grid, a_spec, b_spec, o_spec, out_shape, dims, k_axis=None, name, after=()):
    nk = grid[k_axis] if k_axis is not None else 1
    o_block = tuple(d for d in o_spec.block_shape if d is not None)
    n_after = len(after)

    def body(a_ref, b_ref, *rest):
        o_ref, acc = rest[n_after], rest[n_after + 1:]
        part = lax.dot_general(a_ref[...].astype(BF16), b_ref[...].astype(BF16),
                               (((dims[0],), (dims[1],)), ((), ())), preferred_element_type=F32)
        if k_axis is None:
            o_ref[...] = part.astype(o_ref.dtype)
        else:
            k = pl.program_id(k_axis)

            @pl.when(k == 0)
            def _():
                acc[0][...] = part

            @pl.when(k > 0)
            def _():
                acc[0][...] += part

            @pl.when(k == nk - 1)
            def _():
                o_ref[...] = acc[0][...].astype(o_ref.dtype)

    sem = tuple("arbitrary" if ax == k_axis else "parallel" for ax in range(len(grid)))
    return pl.pallas_call(
        body, grid=grid, in_specs=[a_spec, b_spec] + [HBM_OPERAND] * n_after, out_specs=o_spec, out_shape=out_shape,
        scratch_shapes=[pltpu.VMEM(o_block, F32)] if k_axis is not None else [],
        compiler_params=_cp(sem), name=name)(a, b, *after)


def _mm_cols(a, wg, *, tm, name, out_dtype=F32):
    m, k = a.shape
    ns = wg.shape[2]
    return _matmul(a, wg, grid=(m // tm, N_CHIPS),
                   a_spec=pl.BlockSpec((tm, k), lambda i, s: (i, 0)),
                   b_spec=pl.BlockSpec((None, k, ns), lambda i, s: (s, 0, 0)),
                   o_spec=pl.BlockSpec((tm, ns), lambda i, s: (i, s)),
                   out_shape=_sds((m, N_CHIPS * ns), out_dtype), dims=(1, 0), name=name)


def _mm_cols_nt(dy, wg, *, tm, name, out_dtype=F32, after=()):
    k, ns = wg.shape[1], wg.shape[2]
    m = dy.shape[0]
    a_spec = pl.BlockSpec((tm, ns), lambda i, s: (i, s))
    return _matmul(dy, wg, grid=(m // tm, N_CHIPS), a_spec=a_spec,
                   b_spec=pl.BlockSpec((None, k, ns), lambda i, s: (s, 0, 0)),
                   o_spec=pl.BlockSpec((tm, k), lambda i, s: (i, 0)),
                   out_shape=_sds((m, k), out_dtype), dims=(1, 1), k_axis=1, name=name, after=after)


def _mm_cols_tn(a, dy, *, ns, name, after=()):
    m, k = a.shape
    return _matmul(a, dy, grid=(N_CHIPS,), a_spec=pl.BlockSpec((m, k), lambda s: (0, 0)),
                   b_spec=pl.BlockSpec((m, ns), lambda s: (0, s)),
                   o_spec=pl.BlockSpec((None, k, ns), lambda s: (s, 0, 0)),
                   out_shape=_sds((N_CHIPS, k, ns), BF16), dims=(0, 0), name=name, after=after)


def _mm_plain(a, b, *, tm, tn, name, out_dtype=F32, dims=(1, 0), tk=None):
    m = a.shape[1 - dims[0]]
    kk = a.shape[dims[0]]
    n = b.shape[1 - dims[1]]
    tk = kk if tk is None else tk
    nk = kk // tk

    def a_idx(i, j, k):
        return (i, k) if dims[0] == 1 else (k, i)

    def b_idx(i, j, k):
        return (k, j) if dims[1] == 0 else (j, k)

    a_blk = (tm, tk) if dims[0] == 1 else (tk, tm)
    b_blk = (tk, tn) if dims[1] == 0 else (tn, tk)
    return _matmul(a, b, grid=(m // tm, n // tn, nk),
                   a_spec=pl.BlockSpec(a_blk, a_idx), b_spec=pl.BlockSpec(b_blk, b_idx),
                   o_spec=pl.BlockSpec((tm, tn), lambda i, j, k: (i, j)),
                   out_shape=_sds((m, n), out_dtype), dims=dims, k_axis=2 if nk > 1 else None, name=name)


def _rowwise(fn, tiled, full, outs, accs=(), *, tm, name, after=()):
    args, in_specs = [], []
    for t in tiled:
        if isinstance(t, tuple):
            arr, w, cb = t
            in_specs.append(pl.BlockSpec((tm, w), lambda i, cb=cb: (i, cb)))
        else:
            arr = t
            in_specs.append(pl.BlockSpec((tm, arr.shape[1]), lambda i: (i, 0)))
        args.append(arr)
    rows = args[0].shape[0]
    for f in full:
        in_specs.append(pl.BlockSpec(f.shape, lambda i, nd=f.ndim: (0,) * nd))
        args.append(f)
    out_specs = [pl.BlockSpec((tm, o.shape[1]), lambda i: (i, 0)) for o in outs]
    out_specs += [pl.BlockSpec(a.shape, lambda i, nd=len(a.shape): (0,) * nd) for a in accs]
    n_in, n_out = len(args), len(outs)
    in_specs += [HBM_OPERAND] * len(after)
    first_out = n_in + len(after)

    def body(*refs):
        res = fn(*[r[...] for r in refs[:n_in]])
        res = res if isinstance(res, (tuple, list)) else (res,)
        for r, v in zip(refs[first_out:first_out + n_out], res[:n_out]):
            r[...] = v.astype(r.dtype)
        i = pl.program_id(0)
        for r, v in zip(refs[first_out + n_out:], res[n_out:]):
            @pl.when(i == 0)
            def _(r=r, v=v):
                r[...] = v

            @pl.when(i > 0)
            def _(r=r, v=v):
                r[...] += v

    res = pl.pallas_call(
        body, grid=(rows // tm,), in_specs=in_specs, out_specs=out_specs, out_shape=list(outs) + list(accs),
        compiler_params=_cp(("arbitrary",) if accs else ("parallel",)), name=name)(*args, *after)
    return res


def _colsum(v):
    return jnp.sum(v, axis=0, keepdims=True)


def _ln_stats(z):
    mu = jnp.mean(z, axis=-1, keepdims=True)
    zc = z - mu
    var = jnp.mean(zc * zc, axis=-1, keepdims=True)
    rstd = lax.rsqrt(var + LN_EPS)
    return zc * rstd, rstd


def _ln_bwd(dy, xhat, rstd, g):
    dxh = dy * g
    m1 = jnp.mean(dxh, axis=-1, keepdims=True)
    m2 = jnp.mean(dxh * xhat, axis=-1, keepdims=True)
    return rstd * (dxh - m1 - xhat * m2)


def _swap_halves(t):
    w = t.shape[-1]
    lane = lax.broadcasted_iota(jnp.int32, t.shape, t.ndim - 1)
    return jnp.where((lane % HEAD_DIM) < HEAD_DIM // 2, pltpu.roll(t, w - HEAD_DIM // 2, t.ndim - 1),
                     pltpu.roll(t, HEAD_DIM // 2, t.ndim - 1))


PHASES = max(DILATIONS)
PAIR = 2 * HEAD_DIM
UNITS = SEQ // BLOCK
UNIT_BATCH = 8
ROPE_ROWS = 256


def _to_phase_rows(t):
    return t.reshape(SEQ // PHASES, PHASES, t.shape[1]).transpose(1, 0, 2).reshape(t.shape)


def _reorder_rows(arr, plus=None, *, to_phase, name, scale=1.0):
    def body(*refs):
        o_ref = refs[-1]
        for rho in range(PHASES):
            phase = pl.ds(rho * BLOCK, BLOCK)
            strided = pl.ds(rho, BLOCK, stride=PHASES)
            src, dst = (strided, phase) if to_phase else (phase, strided)
            val = refs[0][src, :]
            if scale != 1.0:
                val = val * scale
            if plus is not None:
                val = val + refs[1][src, :]
            o_ref[dst, :] = val

    spec = pl.BlockSpec((SEQ, BLOCK), lambda j: (0, j))
    ins = [arr] if plus is None else [arr, plus]
    return pl.pallas_call(body, grid=(arr.shape[1] // BLOCK,), in_specs=[spec] * len(ins), out_specs=spec,
                          out_shape=_sds(arr.shape), compiler_params=_cp(("parallel",)), name=name)(*ins)


def _rope(t, cf, ss):
    return t * cf + _swap_halves(t) * ss


def _rope_transposed(d, cf, ss):
    return d * cf + _swap_halves(d * ss)


def _unit_pieces(u, dil):
    pieces, length = PHASES // dil, 8 * dil
    if dil == 1:
        rho, i = 0, u
    elif dil == PHASES:
        rho, i = u, 0
    else:
        rho, i = jnp.bitwise_and(u, dil - 1), jnp.right_shift(u, dil.bit_length() - 1)
    before = jnp.maximum(i - 1, 0)
    cur = [pl.multiple_of((rho + dil * k) * BLOCK + length * i, 8) for k in range(pieces)]
    prev = [pl.multiple_of((rho + dil * k) * BLOCK + length * before, 8) for k in range(pieces)]
    return i, cur, prev


def _load_tile(ref, starts, dil):
    return jnp.concatenate([ref[pl.ds(st, 8 * dil), :] for st in starts], axis=0)


def _store_tile(ref, starts, dil, val, head=None, accumulate=False):
    length = 8 * dil
    lanes = slice(None) if head is None else pl.ds(head * HEAD_DIM, HEAD_DIM)
    cols = slice(None) if head is None else slice(head * HEAD_DIM, (head + 1) * HEAD_DIM)
    for k, st in enumerate(starts):
        piece = val[k * length:(k + 1) * length, cols]
        if accumulate:
            ref[pl.ds(st, length), lanes] += piece
        else:
            ref[pl.ds(st, length), lanes] = piece


def _tile_position(idx, dil):
    pieces, length = PHASES // dil, 8 * dil
    return pieces * jnp.bitwise_and(idx, length - 1) + jnp.right_shift(idx, length.bit_length() - 1)


def _band_mask(i, dil):
    row = lax.broadcasted_iota(jnp.int32, (BLOCK, 2 * BLOCK), 0)
    col = lax.broadcasted_iota(jnp.int32, (BLOCK, 2 * BLOCK), 1)
    key_pos = _tile_position(jnp.bitwise_and(col, BLOCK - 1), dil) + jnp.where(col >= BLOCK, 0, -BLOCK)
    dist = _tile_position(row, dil) - key_pos
    return (dist >= 0) & (dist <= BLOCK) & ((col >= BLOCK) | (i > 0))


def _causal_mask():
    row = lax.broadcasted_iota(jnp.int32, (BLOCK, BLOCK), 0)
    col = lax.broadcasted_iota(jnp.int32, (BLOCK, BLOCK), 1)
    return row >= col


def _pair_views(col0):
    return [pl.BlockSpec((SEQ, PAIR), lambda hp, g=g: (0, col0 // PAIR + g * (ATTN_WIDTH // PAIR) + hp))
            for g in range(len(DILATIONS))]


def _rotate(in_refs, out_refs, cf_ref, ss_ref, scale):
    def step(t, carry):
        rows = pl.ds(pl.multiple_of(t * ROPE_ROWS, ROPE_ROWS), ROPE_ROWS)
        cf, ss = cf_ref[rows, :] * scale, ss_ref[rows, :] * scale
        for i_ref, o_ref in zip(in_refs, out_refs):
            o_ref[rows, :] = _rope(i_ref[rows, :], cf, ss)
        return carry

    lax.fori_loop(0, SEQ // ROPE_ROWS, step, 0)


def _attention_fwd(proj, cos_f, sin_s):
    ng = len(DILATIONS)

    def body(*refs):
        q_refs, k_refs, v_refs = refs[:ng], refs[ng:2 * ng], refs[2 * ng:3 * ng]
        cf_ref, ss_ref, attn_ref, lse_ref = refs[3 * ng:3 * ng + 4]
        scratch = refs[3 * ng + 4:]
        qr_refs, kr_refs = scratch[:ng], scratch[ng:]
        _rotate(q_refs, qr_refs, cf_ref, ss_ref, 1.0 / math.sqrt(HEAD_DIM))
        _rotate(k_refs, kr_refs, cf_ref, ss_ref, 1.0)
        first = lax.broadcasted_iota(jnp.int32, (BLOCK, PAIR), 1) < HEAD_DIM
        for g, dil in enumerate(DILATIONS):
            two_blocks = SEQ // dil > BLOCK

            def units(t, carry, g=g, dil=dil, two_blocks=two_blocks):
                picked = [_unit_pieces(t * UNIT_BATCH + j, dil) for j in range(UNIT_BATCH)]

                def tiles(ref, with_prev=False):
                    if with_prev and two_blocks:
                        return jnp.stack([jnp.concatenate([_load_tile(ref, prev, dil), _load_tile(ref, rows, dil)],
                                                          axis=0) for _, rows, prev in picked])
                    return jnp.stack([_load_tile(ref, rows, dil) for _, rows, _ in picked])

                qq = tiles(qr_refs[g]).astype(BF16)
                kk = tiles(kr_refs[g], True).astype(BF16)
                vv = tiles(v_refs[g], True).astype(BF16)
                if two_blocks:
                    valid = jnp.stack([_band_mask(i, dil) for i, _, _ in picked])
                else:
                    valid = _causal_mask()[None]
                mine = first[None]
                zero = jnp.zeros_like(qq)
                outs, lses = [], []
                for qh in (jnp.where(mine, qq, zero), jnp.where(mine, zero, qq)):
                    s = jnp.einsum("pqd,pkd->pqk", qh, kk, preferred_element_type=F32)
                    s = jnp.where(valid, s, NEG_INF)
                    m = jnp.max(s, axis=-1, keepdims=True)
                    p = jnp.exp(s - m)
                    l = jnp.sum(p, axis=-1, keepdims=True)
                    outs.append(jnp.einsum("pqk,pkd->pqd", p.astype(BF16), vv, preferred_element_type=F32) * (1.0 / l))
                    lses.append(m + jnp.log(l))
                o = jnp.where(mine, outs[0], outs[1])
                lse = jnp.where(mine, lses[0], lses[1])
                if g > 0:
                    lse_old = tiles(lse_ref)
                    m = jnp.maximum(lse_old, lse)
                    lse_new = m + jnp.log(jnp.exp(lse_old - m) + jnp.exp(lse - m))
                    o = tiles(attn_ref) * jnp.exp(lse_old - lse_new) + o * jnp.exp(lse - lse_new)
                    lse = lse_new
                for j, (_, rows, _) in enumerate(picked):
                    _store_tile(attn_ref, rows, dil, o[j])
                    _store_tile(lse_ref, rows, dil, lse[j])
                return carry

            lax.fori_loop(0, UNITS // UNIT_BATCH, units, 0)

    whole = pl.BlockSpec((SEQ, PAIR), lambda hp: (0, 0))
    out = pl.BlockSpec((SEQ, PAIR), lambda hp: (0, hp))
    return pl.pallas_call(
        body, grid=(ATTN_WIDTH // PAIR,),
        in_specs=_pair_views(0) + _pair_views(QKV_WIDTH) + _pair_views(2 * QKV_WIDTH) + [whole, whole],
        out_specs=[out, out], out_shape=[_sds((SEQ, ATTN_WIDTH)), _sds((SEQ, ATTN_WIDTH))],
        scratch_shapes=[pltpu.VMEM((SEQ, PAIR), F32)] * (2 * ng),
        compiler_params=_cp(("parallel",)), name="attention_fwd")(*([proj] * (3 * ng)), cos_f, sin_s)


def _attention_bwd(g, proj, cos_f, sin_s, d_attn, attn, lse):
    dil = DILATIONS[g]
    two_blocks = SEQ // dil > BLOCK

    def body(q_ref, k_ref, v_ref, cf_ref, ss_ref, do_ref, o_ref, lse_ref, dq_out, dk_out, dv_out,
             qr_ref, kr_ref, dq_acc, dk_acc, dv_acc):
        _rotate([q_ref], [qr_ref], cf_ref, ss_ref, 1.0 / math.sqrt(HEAD_DIM))
        _rotate([k_ref], [kr_ref], cf_ref, ss_ref, 1.0)
        dk_acc[...] = jnp.zeros_like(dk_acc)
        dv_acc[...] = jnp.zeros_like(dv_acc)
        nk = 2 * BLOCK if two_blocks else BLOCK
        first = lax.broadcasted_iota(jnp.int32, (BLOCK, PAIR), 1) < HEAD_DIM
        first_k = lax.broadcasted_iota(jnp.int32, (nk, PAIR), 1) < HEAD_DIM

        def units(t, carry):
            picked = [_unit_pieces(t * UNIT_BATCH + j, dil) for j in range(UNIT_BATCH)]

            def tiles(ref, with_prev=False):
                if with_prev and two_blocks:
                    return jnp.stack([jnp.concatenate([_load_tile(ref, prev, dil), _load_tile(ref, rows, dil)], axis=0)
                                      for _, rows, prev in picked])
                return jnp.stack([_load_tile(ref, rows, dil) for _, rows, _ in picked])

            qq = tiles(qr_ref).astype(BF16)
            kk = tiles(kr_ref, True).astype(BF16)
            vv = tiles(v_ref, True).astype(BF16)
            dof = tiles(do_ref)
            dd = dof * tiles(o_ref)
            lse3 = tiles(lse_ref)
            dob = dof.astype(BF16)
            if two_blocks:
                valid = jnp.stack([_band_mask(i, dil) for i, _, _ in picked])
            else:
                valid = _causal_mask()[None]
            zq, zf = jnp.zeros_like(qq), jnp.zeros_like(dd)
            dqs, dks, dvs = [], [], []
            for head in range(2):
                mine = first[None] if head == 0 else jnp.logical_not(first)[None]
                delta = jnp.sum(jnp.where(mine, dd, zf), axis=-1, keepdims=True)
                lse_h = lse3[:, :, head * HEAD_DIM:head * HEAD_DIM + 1]
                s = jnp.einsum("pqd,pkd->pqk", jnp.where(mine, qq, zq), kk, preferred_element_type=F32)
                p = jnp.where(valid, jnp.exp(s - lse_h), 0.0)
                dp = jnp.einsum("pqd,pkd->pqk", jnp.where(mine, dob, zq), vv, preferred_element_type=F32)
                ds = (p * (dp - delta)).astype(BF16)
                dqs.append(jnp.einsum("pqk,pkd->pqd", ds, kk, preferred_element_type=F32))
                dks.append(jnp.einsum("pqk,pqd->pkd", ds, qq, preferred_element_type=F32))
                dvs.append(jnp.einsum("pqk,pqd->pkd", p.astype(BF16), dob, preferred_element_type=F32))
            dq = jnp.where(first[None], dqs[0], dqs[1])
            dk = jnp.where(first_k[None], dks[0], dks[1])
            dv = jnp.where(first_k[None], dvs[0], dvs[1])
            for j, (_, rows, prev) in enumerate(picked):
                _store_tile(dq_acc, rows, dil, dq[j])
                _store_tile(dk_acc, rows, dil, dk[j, nk - BLOCK:], accumulate=True)
                _store_tile(dv_acc, rows, dil, dv[j, nk - BLOCK:], accumulate=True)
                if two_blocks:
                    _store_tile(dk_acc, prev, dil, dk[j, :BLOCK], accumulate=True)
                    _store_tile(dv_acc, prev, dil, dv[j, :BLOCK], accumulate=True)
            return carry

        lax.fori_loop(0, UNITS // UNIT_BATCH, units, 0)

        def finish(t, carry):
            rows = pl.ds(pl.multiple_of(t * ROPE_ROWS, ROPE_ROWS), ROPE_ROWS)
            cf, ss = cf_ref[rows, :], ss_ref[rows, :]
            dq = dq_acc[rows, :] * (1.0 / math.sqrt(HEAD_DIM))
            dq_out[rows, :] = _rope_transposed(dq, cf, ss).astype(BF16)
            dk_out[rows, :] = _rope_transposed(dk_acc[rows, :], cf, ss).astype(BF16)
            dv_out[rows, :] = dv_acc[rows, :].astype(BF16)
            return carry

        lax.fori_loop(0, SEQ // ROPE_ROWS, finish, 0)

    whole = pl.BlockSpec((SEQ, PAIR), lambda hp: (0, 0))
    pair = pl.BlockSpec((SEQ, PAIR), lambda hp: (0, hp))
    views = [_pair_views(col0)[g] for col0 in (0, QKV_WIDTH, 2 * QKV_WIDTH)]
    return pl.pallas_call(
        body, grid=(ATTN_WIDTH // PAIR,), in_specs=views + [whole, whole, pair, pair, pair],
        out_specs=[pair, pair, pair], out_shape=[_sds((SEQ, ATTN_WIDTH), BF16)] * 3,
        scratch_shapes=[pltpu.VMEM((SEQ, PAIR), F32)] * 5,
        compiler_params=_cp(("parallel",)), name=f"attention_bwd_{g}")(proj, proj, proj, cos_f, sin_s, d_attn, attn, lse)


def _cmul(ar, ai, br, bi):
    return ar * br - ai * bi, ar * bi + ai * br


def _pow256(ar, ai):
    for _ in range(8):
        ar, ai = _cmul(ar, ai, ar, ai)
    return ar, ai


def _chunk_carries(first_r, first_i, pr, pi, reverse):
    rows = lax.broadcasted_iota(jnp.int32, first_r.shape, 0)
    out_r = jnp.zeros_like(first_r)
    out_i = jnp.zeros_like(first_i)
    hr = jnp.zeros_like(first_r[0:1])
    hi = jnp.zeros_like(hr)
    order = range(SCAN_CHUNKS - 1, -1, -1) if reverse else range(SCAN_CHUNKS)
    for c in order:
        out_r = jnp.where(rows == c, hr, out_r)
        out_i = jnp.where(rows == c, hi, out_i)
        tr, ti = _cmul(pr[0:1], pi[0:1], hr, hi)
        hr = first_r[c:c + 1] + tr
        hi = first_i[c:c + 1] + ti
    return out_r, out_i


def _tile(j):
    return pl.ds(pl.multiple_of(j * SCAN_CHUNKS, SCAN_CHUNKS), SCAN_CHUNKS)


def _to_scan_rows(t):
    per = SCAN_STEPS // PHASES
    return t.reshape(PHASES, SCAN_CHUNKS, per, t.shape[1]).transpose(2, 0, 1, 3).reshape(t.shape)


def _from_scan_rows(t):
    per = SCAN_STEPS // PHASES
    return t.reshape(per, PHASES, SCAN_CHUNKS, t.shape[1]).transpose(1, 2, 0, 3).reshape(t.shape)


def _scan_in_place(hr_ref, hi_ref, a_r, a_i):
    def local(j, carry):
        tr, ti = _cmul(a_r, a_i, carry[0], carry[1])
        nr = tr + hr_ref[_tile(j), :]
        ni = ti + hi_ref[_tile(j), :]
        hr_ref[_tile(j), :] = nr
        hi_ref[_tile(j), :] = ni
        return nr, ni

    zero = jnp.zeros_like(a_r)
    last_r, last_i = lax.fori_loop(0, SCAN_STEPS, local, (zero, zero), unroll=4)
    pr, pi = _pow256(a_r, a_i)
    er, ei = _chunk_carries(last_r, last_i, pr, pi, reverse=False)

    def fix(j, carry):
        tr, ti = _cmul(carry[0], carry[1], er, ei)
        hr_ref[_tile(j), :] += tr
        hi_ref[_tile(j), :] += ti
        return _cmul(carry[0], carry[1], a_r, a_i)

    lax.fori_loop(0, SCAN_STEPS, fix, (a_r, a_i), unroll=4)
    return er, ei


def _reverse_scan_in_place(lr_ref, li_ref, hr_ref, hi_ref, er, ei, a_r, a_i):
    def local(t, carry):
        j = SCAN_STEPS - 1 - t
        tr, ti = _cmul(a_r, a_i, carry[0], carry[1])
        nr = tr + lr_ref[_tile(j), :]
        ni = ti + li_ref[_tile(j), :]
        lr_ref[_tile(j), :] = nr
        li_ref[_tile(j), :] = ni
        return nr, ni

    zero = jnp.zeros_like(a_r)
    first_r, first_i = lax.fori_loop(0, SCAN_STEPS, local, (zero, zero), unroll=4)
    pr, pi = _pow256(a_r, a_i)
    nxt_r, nxt_i = _chunk_carries(first_r, first_i, pr, pi, reverse=True)

    def accumulate(lam_r, lam_i, hp_r, hp_i, acc):
        return (acc[0] + lam_r * hp_r + lam_i * hp_i, acc[1] + lam_i * hp_r - lam_r * hp_i)

    def fix(t, carry):
        qr, qi, acc_r, acc_i = carry
        j = SCAN_STEPS - 1 - t
        tr, ti = _cmul(qr, qi, nxt_r, nxt_i)
        lam_r = lr_ref[_tile(j), :] + tr
        lam_i = li_ref[_tile(j), :] + ti
        lr_ref[_tile(j), :] = lam_r
        li_ref[_tile(j), :] = lam_i
        acc_r, acc_i = accumulate(lam_r, lam_i, hr_ref[_tile(j - 1), :], hi_ref[_tile(j - 1), :], (acc_r, acc_i))
        qr, qi = _cmul(qr, qi, a_r, a_i)
        return qr, qi, acc_r, acc_i

    qr, qi, acc_r, acc_i = lax.fori_loop(0, SCAN_STEPS - 1, fix, (a_r, a_i, zero, zero), unroll=4)
    tr, ti = _cmul(qr, qi, nxt_r, nxt_i)
    lam_r = lr_ref[_tile(0), :] + tr
    lam_i = li_ref[_tile(0), :] + ti
    lr_ref[_tile(0), :] = lam_r
    li_ref[_tile(0), :] = lam_i
    acc_r, acc_i = accumulate(lam_r, lam_i, er, ei, (acc_r, acc_i))
    return jnp.sum(acc_r, axis=0, keepdims=True), jnp.sum(acc_i, axis=0, keepdims=True)


def _rope_tables():
    half = HEAD_DIM // 2
    inv_freq = ROPE_THETA ** (-jnp.arange(half, dtype=F32) / half)
    ang = jnp.arange(SEQ, dtype=F32)[:, None] * inv_freq[None, :]
    cos, sin = jnp.cos(ang), jnp.sin(ang)
    cos_f = jnp.concatenate([cos, cos, cos, cos], axis=1)
    sin_s = jnp.concatenate([-sin, sin, -sin, sin], axis=1)
    return cos_f, sin_s


def _ssm_discretise(a_re, a_im, log_dt, b_re, b_im):
    lam = lax.complex(a_re, a_im)
    dt = jnp.exp(log_dt)[:, None]
    a_bar = jnp.exp(lam * dt)
    b_bar = ((a_bar - 1.0) / lam)[..., None] * lax.complex(b_re, b_im)
    return a_bar.real, a_bar.imag, b_bar.real, b_bar.imag


SSM_SLABS = 4
SLAB_GROUPS = SSM_GROUPS // SSM_SLABS
SLAB_IN = SSM_WIDTH // SSM_SLABS
SLAB_STATE = SSM_LANES // SSM_SLABS


def _slab_block_diag(blocks):
    _, r, c = blocks.shape
    eye = jnp.eye(SLAB_GROUPS, dtype=blocks.dtype)
    b5 = blocks.reshape(SSM_SLABS, SLAB_GROUPS, r, 1, c) * eye[None, :, None, :, None]
    return b5.reshape(SSM_SLABS, SLAB_GROUPS * r, SLAB_GROUPS * c)


def _diag_blocks(a, b):
    ra, cb = a.shape[1], b.shape[1]
    wa, wb = ra // SLAB_GROUPS, cb // SLAB_GROUPS
    d = lax.dot_general(a, b, (((0,), (0,)), ((), ())), preferred_element_type=F32)
    row_g = jnp.right_shift(lax.broadcasted_iota(jnp.int32, (ra, cb), 0), wa.bit_length() - 1)
    col_g = jnp.right_shift(lax.broadcasted_iota(jnp.int32, (ra, cb), 1), wb.bit_length() - 1)
    d = jnp.where(row_g == col_g, d, 0.0)
    fold = (jnp.bitwise_and(lax.broadcasted_iota(jnp.int32, (cb, wb), 0), wb - 1)
            == lax.broadcasted_iota(jnp.int32, (cb, wb), 1)).astype(F32)
    return jnp.dot(d, fold, preferred_element_type=F32, precision=lax.Precision.HIGHEST)


def _slab_specs():
    tok = pl.BlockSpec((SEQ, SLAB_IN), lambda j: (0, j))
    state = pl.BlockSpec((SEQ, SLAB_STATE), lambda j: (0, j))
    b_in = pl.BlockSpec((None, SLAB_IN, SLAB_STATE), lambda j: (j, 0, 0))
    c_out = pl.BlockSpec((None, SLAB_STATE, SLAB_IN), lambda j: (j, 0, 0))
    vec = pl.BlockSpec((1, SLAB_STATE), lambda j: (0, j))
    ent = pl.BlockSpec((SCAN_CHUNKS, SLAB_STATE), lambda j: (0, j))
    return tok, state, b_in, c_out, vec, ent


def _ssm_forward(u, b_in_r, b_in_i, c_out_r, c_out_ni, a_r, a_i):
    def body(u_ref, br_ref, bi_ref, cr_ref, ci_ref, ar_ref, ai_ref, y_ref, hr_ref, hi_ref, er_ref, ei_ref):
        uu = u_ref[...]
        hr_ref[...] = jnp.dot(uu, br_ref[...], preferred_element_type=F32)
        hi_ref[...] = jnp.dot(uu, bi_ref[...], preferred_element_type=F32)
        a_re = jnp.broadcast_to(ar_ref[...], (SCAN_CHUNKS, SLAB_STATE))
        a_im = jnp.broadcast_to(ai_ref[...], (SCAN_CHUNKS, SLAB_STATE))
        er_ref[...], ei_ref[...] = _scan_in_place(hr_ref, hi_ref, a_re, a_im)
        y_ref[...] = (jnp.dot(hr_ref[...].astype(BF16), cr_ref[...], preferred_element_type=F32)
                      + jnp.dot(hi_ref[...].astype(BF16), ci_ref[...], preferred_element_type=F32))

    tok, state, b_in, c_out, vec, ent = _slab_specs()
    return pl.pallas_call(
        body, grid=(SSM_SLABS,), in_specs=[tok, b_in, b_in, c_out, c_out, vec, vec],
        out_specs=[tok, state, state, ent, ent],
        out_shape=[_sds((SEQ, SSM_WIDTH)), _sds((SEQ, SSM_LANES)), _sds((SEQ, SSM_LANES)),
                   _sds((SCAN_CHUNKS, SSM_LANES)), _sds((SCAN_CHUNKS, SSM_LANES))],
        compiler_params=_cp(("parallel",)), name="ssm_forward")(u, b_in_r, b_in_i, c_out_r, c_out_ni, a_r, a_i)


def _ssm_backward(d_y, d_u_skip, u, h_r, h_i, e_r, e_i, b_in_r, b_in_i, c_out_r, c_out_ni, a_r, a_i):
    def body(dy_ref, skip_ref, u_ref, hr_ref, hi_ref, er_ref, ei_ref, br_ref, bi_ref, cr_ref, ci_ref, ar_ref, ai_ref,
             du_ref, dar_ref, dai_ref, dcr_ref, dci_ref, dbr_ref, dbi_ref, lr_ref, li_ref):
        dy = dy_ref[...]
        lr_ref[...] = _dot_nt(dy, cr_ref[...])
        li_ref[...] = _dot_nt(dy, ci_ref[...])
        a_re = jnp.broadcast_to(ar_ref[...], (SCAN_CHUNKS, SLAB_STATE))
        a_im = -jnp.broadcast_to(ai_ref[...], (SCAN_CHUNKS, SLAB_STATE))
        dar_ref[...], dai_ref[...] = _reverse_scan_in_place(lr_ref, li_ref, hr_ref, hi_ref, er_ref[...], ei_ref[...],
                                                            a_re, a_im)
        dcr_ref[...] = _diag_blocks(dy, hr_ref[...].astype(BF16))
        dci_ref[...] = _diag_blocks(dy, hi_ref[...].astype(BF16))
        lam_r, lam_i = lr_ref[...].astype(BF16), li_ref[...].astype(BF16)
        uu = u_ref[...]
        dbr_ref[...] = _diag_blocks(uu, lam_r)
        dbi_ref[...] = _diag_blocks(uu, lam_i)
        du = skip_ref[...] + _dot_nt(lam_r, br_ref[...]) + _dot_nt(lam_i, bi_ref[...])
        du_ref[...] = du.astype(BF16)

    tok, state, b_in, c_out, vec, ent = _slab_specs()
    db = pl.BlockSpec((SLAB_IN, SSM_STATE), lambda j: (j, 0))
    return pl.pallas_call(
        body, grid=(SSM_SLABS,), in_specs=[tok, tok, tok, state, state, ent, ent, b_in, b_in, c_out, c_out, vec, vec],
        out_specs=[tok, vec, vec, db, db, db, db],
        out_shape=[_sds((SEQ, SSM_WIDTH), BF16), _sds((1, SSM_LANES)), _sds((1, SSM_LANES))]
        + [_sds((SSM_WIDTH, SSM_STATE))] * 4,
        scratch_shapes=[pltpu.VMEM((SEQ, SLAB_STATE), F32)] * 2,
        compiler_params=_cp(("parallel",)), name="ssm_backward")(
            d_y, d_u_skip, u, h_r, h_i, e_r, e_i, b_in_r, b_in_i, c_out_r, c_out_ni, a_r, a_i)


FF_ROWS = 1024
FF_SHARD = D_FF // N_CHIPS


def _dot_nt(a, b):
    return lax.dot_general(a, b, (((1,), (1,)), ((), ())), preferred_element_type=F32)


def _ffn_up(h, w_gate_t, w_up_t):
    def body(h_ref, wg_ref, wu_ref, a_ref, b_ref, act_ref):
        hb = h_ref[...].astype(BF16)
        a = _dot_nt(hb, wg_ref[...])
        b = _dot_nt(hb, wu_ref[...])
        a_ref[...] = a
        b_ref[...] = b
        act_ref[...] = (a * jax.nn.sigmoid(a) * b).astype(BF16)

    w_spec = pl.BlockSpec((None, FF_SHARD, D_MODEL), lambda i, k: (k, 0, 0))
    o_spec = pl.BlockSpec((None, FF_ROWS, FF_SHARD), lambda i, k: (k, i, 0))
    shape = (N_CHIPS, SEQ, FF_SHARD)
    return pl.pallas_call(
        body, grid=(SEQ // FF_ROWS, N_CHIPS),
        in_specs=[pl.BlockSpec((FF_ROWS, D_MODEL), lambda i, k: (i, 0)), w_spec, w_spec],
        out_specs=[o_spec, o_spec, o_spec], out_shape=[_sds(shape), _sds(shape), _sds(shape, BF16)],
        compiler_params=_cp(("parallel", "parallel")), name="ffn_up")(h, w_gate_t, w_up_t)


def _ffn_down_ln2_loss(act, w_down, h, tgt, ln_g, ln_b):
    def body(act_ref, w_ref, h_ref, tgt_ref, g_ref, b_ref, dz_ref, loss_ref, dg_ref, db_ref, acc):
        i, k = pl.program_id(0), pl.program_id(1)
        part = jnp.dot(act_ref[...], w_ref[...], preferred_element_type=F32)

        @pl.when(k == 0)
        def _():
            acc[...] = part

        @pl.when(k > 0)
        def _():
            acc[...] += part

        @pl.when(k == N_CHIPS - 1)
        def _():
            g = g_ref[...]
            xhat, rstd = _ln_stats(DN_ALPHA * h_ref[...] + acc[...])
            err = xhat * g + b_ref[...] - tgt_ref[...]
            d_out = err * (1.0 / D_MODEL)
            dz_ref[...] = _ln_bwd(d_out, xhat, rstd, g)
            loss_rows = jnp.sum(err * err, axis=-1, keepdims=True) * (0.5 / D_MODEL)
            sums = (jnp.broadcast_to(jnp.sum(loss_rows, axis=0, keepdims=True), loss_ref.shape),
                    _colsum(d_out * xhat), _colsum(d_out))
            for ref, val in zip((loss_ref, dg_ref, db_ref), sums):
                @pl.when(i == 0)
                def _(ref=ref, val=val):
                    ref[...] = val

                @pl.when(i > 0)
                def _(ref=ref, val=val):
                    ref[...] += val

    row = pl.BlockSpec((FF_ROWS, D_MODEL), lambda i, k: (i, 0))
    vec = pl.BlockSpec((1, D_MODEL), lambda i, k: (0, 0))
    return pl.pallas_call(
        body, grid=(SEQ // FF_ROWS, N_CHIPS),
        in_specs=[pl.BlockSpec((None, FF_ROWS, FF_SHARD), lambda i, k: (k, i, 0)),
                  pl.BlockSpec((None, FF_SHARD, D_MODEL), lambda i, k: (k, 0, 0)), row, row, vec, vec],
        out_specs=[row, pl.BlockSpec((1, BLOCK), lambda i, k: (0, 0)), vec, vec],
        out_shape=[_sds((SEQ, D_MODEL)), _sds((1, BLOCK)), _sds((1, D_MODEL)), _sds((1, D_MODEL))],
        scratch_shapes=[pltpu.VMEM((FF_ROWS, D_MODEL), F32)],
        compiler_params=_cp(("arbitrary", "arbitrary")), name="ffn_down_ln2_loss")(act, w_down, h, tgt, ln_g, ln_b)


def _ffn_down_bwd(dz, w_down, a, b):
    def body(dz_ref, wd_ref, a_ref, b_ref, da_ref, db_ref):
        d_act = _dot_nt(dz_ref[...].astype(BF16), wd_ref[...])
        av = a_ref[...]
        sg = jax.nn.sigmoid(av)
        da_ref[...] = (d_act * b_ref[...] * sg * (1.0 + av * (1.0 - sg))).astype(BF16)
        db_ref[...] = (d_act * av * sg).astype(BF16)

    t_spec = pl.BlockSpec((None, FF_ROWS, FF_SHARD), lambda i, k: (k, i, 0))
    shape = (N_CHIPS, SEQ, FF_SHARD)
    return pl.pallas_call(
        body, grid=(SEQ // FF_ROWS, N_CHIPS),
        in_specs=[pl.BlockSpec((FF_ROWS, D_MODEL), lambda i, k: (i, 0)),
                  pl.BlockSpec((None, FF_SHARD, D_MODEL), lambda i, k: (k, 0, 0)), t_spec, t_spec],
        out_specs=[t_spec, t_spec], out_shape=[_sds(shape, BF16), _sds(shape, BF16)],
        compiler_params=_cp(("parallel", "parallel")), name="ffn_down_bwd")(dz, w_down, a, b)


def _ffn_dh(d_a, d_b, w_gate_t, w_up_t):
    def body(da_ref, db_ref, wg_ref, wu_ref, o_ref, acc):
        k = pl.program_id(1)
        part = (jnp.dot(da_ref[...], wg_ref[...], preferred_element_type=F32)
                + jnp.dot(db_ref[...], wu_ref[...], preferred_element_type=F32))

        @pl.when(k == 0)
        def _():
            acc[...] = part

        @pl.when(k > 0)
        def _():
            acc[...] += part

        @pl.when(k == N_CHIPS - 1)
        def _():
            o_ref[...] = acc[...]

    t_spec = pl.BlockSpec((None, FF_ROWS, FF_SHARD), lambda i, k: (k, i, 0))
    w_spec = pl.BlockSpec((None, FF_SHARD, D_MODEL), lambda i, k: (k, 0, 0))
    return pl.pallas_call(
        body, grid=(SEQ // FF_ROWS, N_CHIPS), in_specs=[t_spec, t_spec, w_spec, w_spec],
        out_specs=pl.BlockSpec((FF_ROWS, D_MODEL), lambda i, k: (i, 0)), out_shape=_sds((SEQ, D_MODEL)),
        scratch_shapes=[pltpu.VMEM((FF_ROWS, D_MODEL), F32)],
        compiler_params=_cp(("parallel", "arbitrary")), name="ffn_dh")(d_a, d_b, w_gate_t, w_up_t)


def _local_step(x, tgt, wts, small):
    s = SEQ
    cos_f, sin_s = [_to_phase_rows(t) for t in _rope_tables()]
    x = _reorder_rows(x, to_phase=True, name="phase_rows_x")
    tgt = _reorder_rows(tgt, to_phase=True, name="phase_rows_target")

    proj = _mm_cols(x, wts["w_in"], tm=1024, name="proj")

    attn, lse = _attention_fwd(proj, cos_f, sin_s)

    (abar_r, abar_i, bbar_r, bbar_i), ssm_vjp = jax.vjp(
        _ssm_discretise, small["ssm_a_re"], small["ssm_a_im"], small["ssm_log_dt"], small["ssm_b_re"], small["ssm_b_im"])
    b_in_r, b_in_i = [_slab_block_diag(b.transpose(0, 2, 1)).astype(BF16) for b in (bbar_r, bbar_i)]
    c_out_r = _slab_block_diag(small["ssm_c_re"].transpose(0, 2, 1)).astype(BF16)
    c_out_ni = _slab_block_diag(-small["ssm_c_im"].transpose(0, 2, 1)).astype(BF16)
    a_r, a_i = abar_r.reshape(1, SSM_LANES), abar_i.reshape(1, SSM_LANES)
    d_skip = small["ssm_d"].reshape(1, SSM_WIDTH)

    u_f = _to_scan_rows(proj[:, 3 * QKV_WIDTH:3 * QKV_WIDTH + SSM_WIDTH])
    u_p = u_f.astype(BF16)
    y_c, h_r, h_i, e_r, e_i = _ssm_forward(u_p, b_in_r, b_in_i, c_out_r, c_out_ni, a_r, a_i)

    def gelu_fwd(yc, u, dsk):
        y = yc + dsk * u
        return y, 0.5 * y * (1.0 + jnp.tanh(GELU_C * (y + GELU_K * y * y * y)))

    y_s5, gel = _rowwise(gelu_fwd, [y_c, u_f], [d_skip], [_sds((s, SSM_WIDTH)), _sds((s, SSM_WIDTH), BF16)],
                         tm=512, name="ssm_gelu")
    glu = _mm_cols(gel, wts["w_glu"], tm=s, name="glu")

    def glu_fwd(ga, gb):
        return ga * jax.nn.sigmoid(gb)

    (y_glu,) = _rowwise(glu_fwd, [(glu, SSM_WIDTH, 0), (glu, SSM_WIDTH, 1)], [], [_sds((s, SSM_WIDTH), BF16)],
                        tm=512, name="glu_gate")
    y_glu = _from_scan_rows(y_glu)

    gl0 = (proj, D_MODEL, (3 * QKV_WIDTH + SSM_WIDTH) // D_MODEL)
    gl1 = (proj, D_MODEL, (3 * QKV_WIDTH + SSM_WIDTH) // D_MODEL + 1)
    b_gate = small["b_gate"]
    w_out = wts["w_out"].reshape(D_MODEL, D_MODEL)

    def branch(t, wg):
        return jnp.concatenate([jnp.dot(t, wg[k], preferred_element_type=F32) for k in range(N_CHIPS)], axis=1)

    def mix_ln1(l0, l1, at, yg, xv, bg, wa, ws, wo, g, b):
        ya = branch(at.astype(BF16), wa)
        ys = branch(yg, ws)
        mixed = (jax.nn.sigmoid(l0 + bg[0:1]) * ya + jax.nn.sigmoid(l1 + bg[1:2]) * ys).astype(BF16)
        z = DN_ALPHA * xv + jnp.dot(mixed, wo, preferred_element_type=F32)
        xhat, _ = _ln_stats(z)
        return ya, ys, mixed, z, xhat * g + b

    y_attn, y_ssm, mixed, z1, h = _rowwise(
        mix_ln1, [gl0, gl1, attn, y_glu, x],
        [b_gate, wts["w_attn_br"], wts["w_ssm_br"], w_out, small["ln1_g"], small["ln1_b"]],
        [_sds((s, D_MODEL)), _sds((s, D_MODEL)), _sds((s, D_MODEL), BF16), _sds((s, D_MODEL)), _sds((s, D_MODEL))],
        tm=256, name="mix_ln1")

    nf = D_FF // N_CHIPS
    w_gate_t, w_up_t, w_down = wts["w_ff_gate"], wts["w_ff_up"], wts["w_ff_down"]
    ff_a, ff_b, act = _ffn_up(h, w_gate_t, w_up_t)
    dz2, loss_v, d_ln2_g, d_ln2_b = _ffn_down_ln2_loss(act, w_down, h, tgt, small["ln2_g"], small["ln2_b"])

    d_a, d_b = _ffn_down_bwd(dz2, w_down, ff_a, ff_b)

    def grad_rows(lhs, rhs, name):
        return _matmul(lhs, rhs, grid=(N_CHIPS,), a_spec=pl.BlockSpec((None, s, nf), lambda k: (k, 0, 0)),
                       b_spec=pl.BlockSpec((s, D_MODEL), lambda k: (0, 0)),
                       o_spec=pl.BlockSpec((None, nf, D_MODEL), lambda k: (k, 0, 0)),
                       out_shape=_sds((N_CHIPS, nf, D_MODEL), BF16), dims=(0, 0), name=name)

    g_w_ff_down = grad_rows(act, dz2, "g_w_ff_down")
    g_w_ff_gate = grad_rows(d_a, h, "g_w_ff_gate")
    g_w_ff_up = grad_rows(d_b, h, "g_w_ff_up")
    dh_ff = _ffn_dh(d_a, d_b, w_gate_t, w_up_t)

    def ln1_gate_bwd(dz, dff, z, l0, l1, ya, ys, g, bg, wo):
        xhat, rstd = _ln_stats(z)
        dh = DN_ALPHA * dz + dff
        dz_in = _ln_bwd(dh, xhat, rstd, g)
        dm = _dot_nt(dz_in.astype(BF16), wo)
        g0 = jax.nn.sigmoid(l0 + bg[0:1])
        g1 = jax.nn.sigmoid(l1 + bg[1:2])
        dl0 = dm * ya * g0 * (1.0 - g0)
        dl1 = dm * ys * g1 * (1.0 - g1)
        return (dz_in, dm * g0, dm * g1, jnp.concatenate([dl0, dl1], axis=1),
                _colsum(dh * xhat), _colsum(dh), _colsum(dl0), _colsum(dl1))

    dz1, d_y_attn, d_y_ssm, d_gl, d_ln1_g, d_ln1_b, d_bg0, d_bg1 = _rowwise(
        ln1_gate_bwd, [dz2, dh_ff, z1, gl0, gl1, y_attn, y_ssm], [small["ln1_g"], b_gate, w_out],
        [_sds((s, D_MODEL)), _sds((s, D_MODEL), BF16), _sds((s, D_MODEL), BF16), _sds((s, 2 * D_MODEL), BF16)],
        [_sds((1, D_MODEL))] * 4, tm=256, name="ln1_gate_bwd")
    g_w_out = _mm_plain(mixed, dz1, tm=D_MODEL, tn=512, dims=(0, 0), out_dtype=BF16, name="g_w_out")
    g_w_out = g_w_out.reshape(N_CHIPS, D_MODEL // N_CHIPS, D_MODEL)

    g_w_ssm_br = _mm_cols_tn(y_glu, d_y_ssm, ns=D_MODEL // N_CHIPS, name="g_w_ssm_br")
    d_y_glu = _to_scan_rows(_mm_cols_nt(d_y_ssm, wts["w_ssm_br"], tm=s, name="d_y_glu"))

    def glu_bwd(dy, ga, gb):
        sg = jax.nn.sigmoid(gb)
        return jnp.concatenate([dy * sg, dy * ga * sg * (1.0 - sg)], axis=1)

    (d_glu,) = _rowwise(glu_bwd, [d_y_glu, (glu, SSM_WIDTH, 0), (glu, SSM_WIDTH, 1)], [],
                        [_sds((s, 2 * SSM_WIDTH), BF16)], tm=512, name="glu_bwd")
    g_w_glu = _mm_cols_tn(gel, d_glu, ns=2 * SSM_WIDTH // N_CHIPS, name="g_w_glu")
    d_gel = _mm_cols_nt(d_glu, wts["w_glu"], tm=s, name="d_gel")

    def gelu_bwd(dg, y, u, dsk):
        th = jnp.tanh(GELU_C * (y + GELU_K * y * y * y))
        dy = dg * (0.5 * (1.0 + th) + 0.5 * y * (1.0 - th * th) * GELU_C * (1.0 + 3.0 * GELU_K * y * y))
        return dy, dy * dsk, _colsum(dy * u)

    d_y, d_u_skip, d_ssm_d = _rowwise(gelu_bwd, [d_gel, y_s5, u_f], [d_skip],
                                      [_sds((s, SSM_WIDTH), BF16), _sds((s, SSM_WIDTH))], [_sds((1, SSM_WIDTH))],
                                      tm=512, name="gelu_bwd")
    d_u, d_abar_r, d_abar_i, d_c_r, d_c_ni, d_bin_r, d_bin_i = _ssm_backward(
        d_y, d_u_skip, u_p, h_r, h_i, e_r, e_i, b_in_r, b_in_i, c_out_r, c_out_ni, a_r, a_i)
    d_u = _from_scan_rows(d_u)
    d_bbar_r = d_bin_r.reshape(SSM_GROUPS, SSM_GROUP, SSM_STATE).transpose(0, 2, 1)
    d_bbar_i = d_bin_i.reshape(SSM_GROUPS, SSM_GROUP, SSM_STATE).transpose(0, 2, 1)
    d_a_re, d_a_im, d_log_dt, d_b_re, d_b_im = ssm_vjp(
        (d_abar_r.reshape(SSM_GROUPS, SSM_STATE), d_abar_i.reshape(SSM_GROUPS, SSM_STATE), d_bbar_r, d_bbar_i))
    d_c_re = d_c_r.reshape(SSM_GROUPS, SSM_GROUP, SSM_STATE)
    d_c_im = -d_c_ni.reshape(SSM_GROUPS, SSM_GROUP, SSM_STATE)

    g_w_attn_br = _mm_cols_tn(attn, d_y_attn, ns=D_MODEL // N_CHIPS, name="g_w_attn_br")
    d_attn = _mm_cols_nt(d_y_attn, wts["w_attn_br"], tm=s, name="d_attn")
    dqkv = [_attention_bwd(g, proj, cos_f, sin_s, d_attn, attn, lse) for g in range(len(DILATIONS))]

    d_proj = jnp.concatenate([dqkv[g][j] for j in range(3) for g in range(len(DILATIONS))] + [d_u, d_gl],
                             axis=1)
    g_w_in = _mm_cols_tn(x, d_proj, ns=IN_WIDTH // N_CHIPS, name="g_w_in")

    def grad_x_after(after):
        dx_proj = _mm_cols_nt(d_proj, wts["w_in"], tm=1024, name="dx_proj", after=after)
        return _reorder_rows(dz1, dx_proj, to_phase=False, name="grad_x", scale=DN_ALPHA)

    big = {"w_in": g_w_in, "w_attn_br": g_w_attn_br, "w_ssm_br": g_w_ssm_br, "w_out": g_w_out, "w_glu": g_w_glu,
           "w_ff_gate": g_w_ff_gate, "w_ff_up": g_w_ff_up, "w_ff_down": g_w_ff_down}
    small_g = {"b_gate": jnp.concatenate([d_bg0, d_bg1], axis=0), "ssm_a_re": d_a_re, "ssm_a_im": d_a_im,
               "ssm_log_dt": d_log_dt, "ssm_b_re": d_b_re, "ssm_b_im": d_b_im, "ssm_c_re": d_c_re, "ssm_c_im": d_c_im,
               "ssm_d": d_ssm_d.reshape(SSM_WIDTH), "ln1_g": d_ln1_g, "ln1_b": d_ln1_b, "ln2_g": d_ln2_g,
               "ln2_b": d_ln2_b}
    marks = {"ln1_bwd": dz1, "scan_bwd": d_abar_r, "attention_bwd_0": dqkv[0][0]}
    return loss_v[0, 0], grad_x_after, big, small_g, marks


GATHER_ID, SWAP_ID, SCATTER_ID, JOIN_ID, EXCHANGE_ID = 1, 2, 3, 4, 5


def _place():
    return lax.axis_index("x"), lax.axis_index("y"), lax.axis_index("c")


def _other_chips(x, y):
    return [(1 - x, y), (x, 1 - y), (1 - x, 1 - y)]


def _handshake(peers):
    barrier = pltpu.get_barrier_semaphore()
    for peer in peers:
        pl.semaphore_signal(barrier, inc=1, device_id=peer, device_id_type=MESH)
    pl.semaphore_wait(barrier, len(peers))


def _sequencer(body, arrays, out_type, sems, collective_id, name):
    return pl.kernel(body, name=name, out_type=out_type,
                     mesh=plsc.ScalarSubcoreMesh(axis_name="sequencer", num_cores=1), scratch_types=sems,
                     compiler_params=pltpu.CompilerParams(collective_id=collective_id))(*arrays)


def _gather_weights(shards, *, name):
    nw = len(shards)

    def body(*refs):
        ins, outs = refs[:nw], refs[nw:2 * nw]
        send_sems, recv_sems, relay_send, relay_recv, pass_send, pass_recv, local_sems = refs[2 * nw:]
        x, y, c = _place()
        chip = 2 * x + y
        sibling = (x, y, 1 - c)
        nbr = [(1 - x, y, c), (x, 1 - y, c)]
        s_nbr = [2 * (1 - x) + y, 2 * x + (1 - y)]
        s_diag = 2 * (1 - x) + (1 - y)
        _handshake([sibling] + nbr)

        def copy(src, dst, send_sem, recv_sem, to):
            return pltpu.make_async_remote_copy(src_ref=src, dst_ref=dst, send_sem=send_sem, recv_sem=recv_sem,
                                                device_id=to, device_id_type=MESH)

        def rows(w, core, quarter=None):
            hw = shards[w].shape[0] // 2
            if quarter is None:
                return pl.ds(core * hw, hw)
            return pl.ds(core * hw + quarter * (hw // 2), hw // 2)

        own, sent = [], []
        for w in range(nw):
            cp = pltpu.make_async_copy(ins[w], outs[w].at[chip], local_sems.at[w])
            cp.start()
            own.append(cp)
            for d in range(2):
                cp = copy(ins[w].at[rows(w, c)], outs[w].at[chip, rows(w, c)], send_sems.at[w, d], recv_sems.at[w, d],
                          nbr[d])
                cp.start()
                sent.append(cp)
        for w in range(nw):
            for d in range(2):
                landed = outs[w].at[s_nbr[d], rows(w, c)]
                copy(landed, landed, send_sems.at[w, d], recv_sems.at[w, d], nbr[d]).wait_recv()
                quarter = outs[w].at[s_nbr[d], rows(w, c, d)]
                for cp in (copy(quarter, quarter, relay_send.at[w, d], relay_recv.at[w, d], nbr[1 - d]),
                           copy(landed, landed, pass_send.at[w, d], pass_recv.at[w, d], sibling)):
                    cp.start()
                    sent.append(cp)
        for w in range(nw):
            for d in range(2):
                quarter = outs[w].at[s_diag, rows(w, c, d)]
                copy(quarter, quarter, relay_send.at[w, d], relay_recv.at[w, d], nbr[1 - d]).wait_recv()
                cp = copy(quarter, quarter, pass_send.at[w, 2 + d], pass_recv.at[w, 2 + d], sibling)
                cp.start()
                sent.append(cp)
        for w in range(nw):
            for d in range(2):
                theirs = outs[w].at[s_nbr[d], rows(w, 1 - c)]
                copy(theirs, theirs, pass_send.at[w, d], pass_recv.at[w, d], sibling).wait_recv()
                theirs = outs[w].at[s_diag, rows(w, 1 - c, d)]
                copy(theirs, theirs, pass_send.at[w, 2 + d], pass_recv.at[w, 2 + d], sibling).wait_recv()
        for cp in sent:
            cp.wait_send()
        for cp in own:
            cp.wait()

    sem = pltpu.SemaphoreType.DMA
    return _sequencer(body, shards, [_sds((N_CHIPS,) + a.shape, a.dtype) for a in shards],
                      [sem((nw, 2)), sem((nw, 2)), sem((nw, 2)), sem((nw, 2)), sem((nw, 4)), sem((nw, 4)), sem((nw,))],
                      GATHER_ID, name)


def _swap_other_halves(grads, *, name):
    nw = len(grads)

    def body(*refs):
        ins, outs = refs[:nw], refs[nw:2 * nw]
        send_sems, recv_sems = refs[2 * nw:]
        x, y, c = _place()
        _handshake([(x, y, 1 - c)])
        cps = []
        for w in range(nw):
            hw = grads[w].shape[1] // 2
            cp = pltpu.make_async_remote_copy(
                src_ref=ins[w].at[:, pl.ds((1 - c) * hw, hw)], dst_ref=outs[w], send_sem=send_sems.at[w],
                recv_sem=recv_sems.at[w], device_id=(x, y, 1 - c), device_id_type=MESH)
            cp.start()
            cps.append(cp)
        for cp in cps:
            cp.wait()

    sem = pltpu.SemaphoreType.DMA
    return _sequencer(body, grads, [_sds((N_CHIPS, g.shape[1] // 2, g.shape[2]), g.dtype) for g in grads],
                      [sem((nw,)), sem((nw,))], SWAP_ID, name)


def _add_my_halves(core, grads, others, *, name, after=()):
    nw = len(grads)
    halves = [g.shape[1] // 2 for g in grads]

    def body(core_ref, *refs):
        outs = refs[2 * nw + len(after):]
        for g_ref, o_ref, out_ref in zip(refs[:nw], refs[nw:2 * nw], outs):
            out_ref[...] = (g_ref[...].astype(F32) + o_ref[...].astype(F32)).astype(out_ref.dtype)

    in_specs = [pl.BlockSpec((None, None, hw, g.shape[2]), lambda s, core_ref: (s, core_ref[0], 0, 0))
                for g, hw in zip(grads, halves)]
    in_specs += [pl.BlockSpec((None, hw, g.shape[2]), lambda s, core_ref: (s, 0, 0)) for g, hw in zip(grads, halves)]
    return pl.pallas_call(
        body,
        grid_spec=pltpu.PrefetchScalarGridSpec(
            num_scalar_prefetch=1, grid=(N_CHIPS,), in_specs=in_specs + [HBM_OPERAND] * len(after),
            out_specs=[pl.BlockSpec((None, hw, g.shape[2]), lambda s, core_ref: (s, 0, 0))
                       for g, hw in zip(grads, halves)]),
        out_shape=[_sds((N_CHIPS, hw, g.shape[2]), BF16) for g, hw in zip(grads, halves)],
        compiler_params=_cp(("parallel",)), name=name)(
            core, *[g.reshape(N_CHIPS, 2, hw, g.shape[2]) for g, hw in zip(grads, halves)], *others, *after)


def _scatter_partials(parts, *, name):
    nw = len(parts)

    def body(*refs):
        ins, outs = refs[:nw], refs[nw:2 * nw]
        send_sems, recv_sems = refs[2 * nw:]
        x, y, c = _place()
        _handshake([(cx, cy, c) for cx, cy in _other_chips(x, y)])
        cps = []
        for w in range(nw):
            for j, (cx, cy) in enumerate(_other_chips(x, y)):
                cp = pltpu.make_async_remote_copy(
                    src_ref=ins[w].at[2 * cx + cy], dst_ref=outs[w].at[j], send_sem=send_sems.at[w, j],
                    recv_sem=recv_sems.at[w, j], device_id=(cx, cy, c), device_id_type=MESH)
                cp.start()
                cps.append(cp)
        for cp in cps:
            cp.wait()

    sem = pltpu.SemaphoreType.DMA
    return _sequencer(body, parts, [_sds((3,) + p.shape[1:], p.dtype) for p in parts],
                      [sem((nw, 3)), sem((nw, 3))], SCATTER_ID, name)


SUM_STEPS = 2


def _sum_partials(chip, parts, recvd, *, name, after=()):
    nw = len(parts)
    rows = [p.shape[1] // SUM_STEPS for p in parts]

    def body(chip_ref, *refs):
        outs = refs[2 * nw + len(after):]
        for p_ref, r_ref, out_ref in zip(refs[:nw], refs[nw:2 * nw], outs):
            acc = p_ref[...].astype(F32)
            for j in range(3):
                acc = acc + r_ref[j].astype(F32)
            out_ref[...] = acc

    in_specs = [pl.BlockSpec((None, th, p.shape[2]), lambda i, chip_ref: (chip_ref[0], i, 0))
                for p, th in zip(parts, rows)]
    in_specs += [pl.BlockSpec((3, th, p.shape[2]), lambda i, chip_ref: (0, i, 0)) for p, th in zip(parts, rows)]
    return pl.pallas_call(
        body,
        grid_spec=pltpu.PrefetchScalarGridSpec(
            num_scalar_prefetch=1, grid=(SUM_STEPS,), in_specs=in_specs + [HBM_OPERAND] * len(after),
            out_specs=[pl.BlockSpec((th, p.shape[2]), lambda i, chip_ref: (i, 0)) for p, th in zip(parts, rows)]),
        out_shape=[_sds(p.shape[1:]) for p in parts], compiler_params=_cp(("parallel",)), name=name)(
            chip, *parts, *recvd, *after)


def _swap_reduced_halves(halves, *, name):
    nw = len(halves)

    def body(*refs):
        ins, outs = refs[:nw], refs[nw:2 * nw]
        send_sems, recv_sems = refs[2 * nw:]
        x, y, c = _place()
        _handshake([(x, y, 1 - c)])
        cps = []
        for w in range(nw):
            cp = pltpu.make_async_remote_copy(
                src_ref=ins[w], dst_ref=outs[w], send_sem=send_sems.at[w], recv_sem=recv_sems.at[w],
                device_id=(x, y, 1 - c), device_id_type=MESH)
            cp.start()
            cps.append(cp)
        for cp in cps:
            cp.wait()

    sem = pltpu.SemaphoreType.DMA
    return _sequencer(body, halves, [_sds(h.shape, h.dtype) for h in halves], [sem((nw,)), sem((nw,))], JOIN_ID, name)


def _exchange_rows(vec, *, name):
    def body(v_ref, slots, send_sems, recv_sems, local_sem):
        x, y, c = _place()
        me = 4 * x + 2 * y + c
        peers = []
        for mask in range(1, N_DEV):
            peers.append((1 - x if mask & 4 else x, 1 - y if mask & 2 else y, 1 - c if mask & 1 else c))
        _handshake(peers)
        own = pltpu.make_async_copy(v_ref, slots.at[me], local_sem)
        own.start()
        cps = []
        for k, peer in enumerate(peers):
            cp = pltpu.make_async_remote_copy(
                src_ref=v_ref, dst_ref=slots.at[me], send_sem=send_sems.at[k], recv_sem=recv_sems.at[k],
                device_id=peer, device_id_type=MESH)
            cp.start()
            cps.append(cp)
        for k, (px, py, pc) in enumerate(peers):
            pltpu.make_async_remote_copy(
                src_ref=v_ref, dst_ref=slots.at[4 * px + 2 * py + pc], send_sem=send_sems.at[k],
                recv_sem=recv_sems.at[k], device_id=(px, py, pc), device_id_type=MESH).wait_recv()
        for cp in cps:
            cp.wait_send()
        own.wait()

    sem = pltpu.SemaphoreType.DMA
    return _sequencer(body, [vec], [_sds((N_DEV,) + vec.shape)], [sem((N_DEV - 1,)), sem((N_DEV - 1,)), sem(())],
                      EXCHANGE_ID, name)[0]


def _sum_slots(slots, *, name, after=()):
    def body(s_ref, *rest):
        out_ref = rest[len(after)]
        acc = s_ref[0]
        for d in range(1, N_DEV):
            acc = acc + s_ref[d]
        out_ref[...] = acc

    vmem = pl.BlockSpec(memory_space=pltpu.VMEM)
    return pl.pallas_call(
        body, in_specs=[vmem] + [HBM_OPERAND] * len(after), out_specs=vmem, out_shape=_sds(slots.shape[1:]),
        compiler_params=pltpu.CompilerParams(vmem_limit_bytes=VMEM_LIMIT_BYTES), name=name)(slots, *after)


def _reduce_scatter_start(grads, core, *, tag, add_after=()):
    others = _swap_other_halves(grads, name="swap_other_halves_" + tag)
    parts = _add_my_halves(core, grads, others, name="add_my_halves_" + tag, after=add_after)
    return parts, _scatter_partials(parts, name="scatter_partials_" + tag)


def _reduce_scatter_finish(parts, recvd, chip, *, tag, sum_after=()):
    mine = _sum_partials(chip, parts, recvd, name="sum_partials_" + tag, after=sum_after)
    return mine, _swap_reduced_halves(mine, name="swap_reduced_halves_" + tag)


ADAM_BLOCK_ELEMS = 256 * 1024


def _adam_rows(rows, cols):
    tm = rows
    while tm * cols > ADAM_BLOCK_ELEMS and tm % 16 == 0:
        tm //= 2
    return tm


def _adam_step(wv, gv, mv, vv):
    m2 = ADAM_B1 * mv + (1.0 - ADAM_B1) * gv
    v2 = ADAM_B2 * vv + (1.0 - ADAM_B2) * (gv * gv)
    m_hat = m2 / (1.0 - ADAM_B1 ** ADAM_STEP)
    v_hat = v2 / (1.0 - ADAM_B2 ** ADAM_STEP)
    return -ADAM_LR * (m_hat / (jnp.sqrt(v_hat) + ADAM_EPS) + ADAM_WD * wv), m2, v2


def _adamw(w, g, m, v, *, name):
    rows, cols = w.shape
    return _rowwise(_adam_step, [w, g, m, v], [], [_sds((rows, cols))] * 3, tm=_adam_rows(rows, cols), name=name)


def _adamw_halves(core, w, g_mine, g_theirs, m, v, *, name, after=()):
    rows, cols = w.shape
    hw = rows // 2
    tm = _adam_rows(hw, cols)
    per_half = hw // tm

    def body(core_ref, w_ref, gm_ref, gt_ref, m_ref, v_ref, *rest):
        g_out, d_out, m_out, v_out = rest[len(after):]
        mine = (pl.program_id(0) // per_half) == core_ref[0]
        g = jnp.where(mine, gm_ref[...], gt_ref[...])
        d, m2, v2 = _adam_step(w_ref[...], g, m_ref[...], v_ref[...])
        g_out[...] = g
        d_out[...] = d
        m_out[...] = m2
        v_out[...] = v2

    full = pl.BlockSpec((tm, cols), lambda i, core_ref: (i, 0))
    half = pl.BlockSpec((tm, cols), lambda i, core_ref: (i % per_half, 0))
    return pl.pallas_call(
        body,
        grid_spec=pltpu.PrefetchScalarGridSpec(
            num_scalar_prefetch=1, grid=(rows // tm,),
            in_specs=[full, half, half, full, full] + [HBM_OPERAND] * len(after), out_specs=[full, full, full, full]),
        out_shape=[_sds((rows, cols))] * 4, compiler_params=_cp(("parallel",)), name=name)(
            core, w, g_mine, g_theirs, m, v, *after)


HELD_TRANSPOSED = ("w_ff_gate", "w_ff_up")


def _as_rows(name, arr):
    return arr[0].T if name in HELD_TRANSPOSED else arr[0]


def _from_rows(name, arr2d):
    return (arr2d.T if name in HELD_TRANSPOSED else arr2d)[None]


STORED_SWAPPED = ("ssm_b_re", "ssm_b_im")


def _as_stored(name, arr):
    return jnp.swapaxes(arr, -1, -2) if name in STORED_SWAPPED else arr


def _pack_rows(arrs):
    flat = jnp.concatenate([a.reshape(-1).astype(F32) for a in arrs])
    rows = -(-flat.shape[0] // 1024) * 8
    return jnp.pad(flat, (0, rows * 128 - flat.shape[0])).reshape(rows, 128)


def _unpack_rows(vec, shapes):
    flat = vec.reshape(-1)
    out, off = [], 0
    for shp in shapes:
        size = math.prod(shp)
        out.append(flat[off:off + size].reshape(shp))
        off += size
    return out


SMALL = ("b_gate", "ssm_a_re", "ssm_a_im", "ssm_log_dt", "ssm_b_re", "ssm_b_im", "ssm_c_re", "ssm_c_im", "ssm_d",
         "ln1_g", "ln1_b", "ln2_g", "ln2_b")
GATHER_GROUPS = (("w_in", ("w_in",)), ("mixer", ("w_attn_br", "w_ssm_br", "w_glu", "w_out")),
                 ("ffn", ("w_ff_gate", "w_ff_up", "w_ff_down")))
REDUCE_GROUPS = (("ffn", ("w_ff_down", "w_ff_gate", "w_ff_up")),
                 ("mixer", ("w_out", "w_ssm_br", "w_glu", "w_attn_br")), ("w_in", ("w_in",)))
WEIGHTS = ("w_in", "b_gate", "w_attn_br", "w_ssm_br", "w_out", "ssm_a_re", "ssm_a_im", "ssm_log_dt", "ssm_b_re",
           "ssm_b_im", "ssm_c_re", "ssm_c_im", "ssm_d", "w_glu", "ln1_g", "ln1_b", "w_ff_gate", "w_ff_up", "w_ff_down",
           "ln2_g", "ln2_b")


def kernel(x, w_in, b_gate, w_attn_br, w_ssm_br, w_out, ssm_a_re, ssm_a_im, ssm_log_dt, ssm_b_re, ssm_b_im, ssm_c_re, ssm_c_im, ssm_d, w_glu, ln1_g, ln1_b, w_ff_gate, w_ff_up, w_ff_down, ln2_g, ln2_b, loss_target, m_w_in, m_b_gate, m_w_attn_br, m_w_ssm_br, m_w_out, m_ssm_a_re, m_ssm_a_im, m_ssm_log_dt, m_ssm_b_re, m_ssm_b_im, m_ssm_c_re, m_ssm_c_im, m_ssm_d, m_w_glu, m_ln1_g, m_ln1_b, m_w_ff_gate, m_w_ff_up, m_w_ff_down, m_ln2_g, m_ln2_b, v_w_in, v_b_gate, v_w_attn_br, v_w_ssm_br, v_w_out, v_ssm_a_re, v_ssm_a_im, v_ssm_log_dt, v_ssm_b_re, v_ssm_b_im, v_ssm_c_re, v_ssm_c_im, v_ssm_d, v_w_glu, v_ln1_g, v_ln1_b, v_w_ff_gate, v_w_ff_up, v_w_ff_down, v_ln2_g, v_ln2_b):
    given = dict(locals())
    px, py, pc = _place()
    chip = 2 * px + py
    core_s = jnp.reshape(pc, (1,)).astype(jnp.int32)
    chip_s = jnp.reshape(chip, (1,)).astype(jnp.int32)

    wts = {}
    for tag, names in GATHER_GROUPS:
        wts.update(zip(names, _gather_weights([_as_rows(n, given[n]).astype(BF16) for n in names],
                                              name="gather_" + tag)))
    ncol = D_MODEL // N_CHIPS
    bg_mine = jnp.where(pc == 0, b_gate[0], jnp.zeros_like(b_gate[0]))
    bg_full = lax.dynamic_update_slice(jnp.zeros((2, D_MODEL), F32), bg_mine, (0, chip * ncol))
    bg_slots = _exchange_rows(bg_full.reshape(16, 128), name="exchange_gate_bias")
    bg_full = _sum_slots(bg_slots, name="sum_gate_bias").reshape(2, D_MODEL)
    small = {n: given[n][0] for n in SMALL if n.startswith("ssm")}
    small.update({n: given[n] for n in ("ln1_g", "ln1_b", "ln2_g", "ln2_b")})
    small["b_gate"] = bg_full

    loss_mine, grad_x_after, big_g, small_g, marks = _local_step(x[0], loss_target[0], wts, small)

    groups = dict(REDUCE_GROUPS)
    parts, recvd = {}, {}
    grads, delta, new_m, new_v = {}, {}, {}, {}

    def start(tag, add_after):
        parts[tag], recvd[tag] = _reduce_scatter_start([big_g[n] for n in groups[tag]], core_s, tag=tag,
                                                       add_after=add_after)

    def finish(tag, sum_after, adam_after):
        mine, theirs = _reduce_scatter_finish(parts[tag], recvd[tag], chip_s, tag=tag, sum_after=sum_after)
        for n, g_mine, g_theirs in zip(groups[tag], mine, theirs):
            res = _adamw_halves(core_s, _as_rows(n, given[n]), g_mine, g_theirs, _as_rows(n, given["m_" + n]),
                                _as_rows(n, given["v_" + n]), name="adamw_" + n, after=adam_after)
            grads[n], delta[n], new_m[n], new_v[n] = [_from_rows(n, r) for r in res]

    start("ffn", (marks["ln1_bwd"],))
    start("mixer", (marks["scan_bwd"],))
    finish("mixer", (marks["attention_bwd_0"],), (big_g["w_in"],))
    start("w_in", tuple(delta[n] for n in groups["mixer"]))
    in_flight = (parts["w_in"][0],)
    grad_x = grad_x_after(in_flight)
    finish("ffn", (marks["scan_bwd"],), in_flight)
    stored = [_as_stored(n, small_g[n]) for n in SMALL] + [loss_mine.reshape(1)]
    slots = _exchange_rows(_pack_rows(stored), name="exchange_small")
    summed = _unpack_rows(_sum_slots(slots, name="sum_small", after=in_flight), [a.shape for a in stored])
    loss = summed.pop()[0]
    for n, g in zip(SMALL, summed):
        g = _as_stored(n, g)
        if n == "b_gate":
            g = lax.dynamic_slice(g, (0, chip * ncol), (2, ncol))
        grads[n] = g.reshape(given[n].shape)
    packed = [_pack_rows([_as_stored(n, src[n]) for n in SMALL]) for src in
              (given, grads, {n: given["m_" + n] for n in SMALL}, {n: given["v_" + n] for n in SMALL})]
    shapes = [_as_stored(n, given[n]).shape for n in SMALL]
    small_out = _adamw(*packed, name="adamw_small")
    for out, vec in zip((delta, new_m, new_v), small_out):
        out.update((n, _as_stored(n, a)) for n, a in zip(SMALL, _unpack_rows(vec, shapes)))
    behind = [delta[n] for n in groups["ffn"]] + [small_out[0], grad_x]
    finish("w_in", tuple(behind), ())

    return (loss, grad_x.reshape(x.shape), *[grads[n] for n in WEIGHTS], *[delta[n] for n in WEIGHTS],
            *[new_m[n] for n in WEIGHTS], *[new_v[n] for n in WEIGHTS])
```

```python
import math

import jax
import jax.numpy as jnp
from jax import lax
from jax.experimental import pallas as pl
from jax.experimental.pallas import tpu as pltpu
from jax.experimental.pallas import tpu_sc as plsc

F32 = jnp.float32
BF16 = jnp.bfloat16
MESH = pl.DeviceIdType.MESH

D_MODEL = 1024
SEQ = 2048
HEAD_DIM = 64
ATTN_HEADS = 8
DILATIONS = (1, 4, 16)
ATTN_WIDTH = ATTN_HEADS * HEAD_DIM
QKV_WIDTH = 3 * ATTN_WIDTH
BLOCK = 128
ROPE_THETA = 10000.0
NEG_INF = -1e30
SSM_GROUP = 16
SSM_GROUPS = 32
SSM_WIDTH = 512
SSM_STATE = 64
SSM_LANES = SSM_GROUPS * SSM_STATE
SCAN_CHUNKS = 8
SCAN_STEPS = SEQ // SCAN_CHUNKS
IN_WIDTH = 3 * QKV_WIDTH + SSM_WIDTH + 2 * D_MODEL
D_FF = 2816
N_CHIPS = 4
N_DEV = 8
DN_ALPHA = 2.0 ** 0.25
LN_EPS = 1e-5
ADAM_LR = 0.001
ADAM_B1 = 0.9
ADAM_B2 = 0.999
ADAM_EPS = 1e-08
ADAM_WD = 0.01
ADAM_STEP = 10
GELU_C = math.sqrt(2.0 / math.pi)
GELU_K = 0.044715

VMEM_LIMIT_BYTES = 56 * 1024 * 1024


def _sds(shape, dtype=F32):
    return jax.ShapeDtypeStruct(tuple(shape), dtype)


def _cp(semantics=None):
    return pltpu.CompilerParams(dimension_semantics=semantics, vmem_limit_bytes=VMEM_LIMIT_BYTES)


HBM_OPERAND = pl.BlockSpec(memory_space=pl.ANY)


def _matmul(a, b, *, grid, a_spec, b_spec, o_spec, out_shape, dims, k_axis=None, name, after=()):
    nk = grid[k_axis] if k_axis is not None else 1
    o_block = tuple(d for d in o_spec.block_shape if d is not None)
    n_after = len(after)

    def body(a_ref, b_ref, *rest):
        o_ref, acc = rest[n_after], rest[n_after + 1:]
        part = lax.dot_general(a_ref[...].astype(BF16), b_ref[...].astype(BF16),
                               (((dims[0],), (dims[1],)), ((), ())), preferred_element_type=F32)
        if k_axis is None:
            o_ref[...] = part.astype(o_ref.dtype)
        else:
            k = pl.program_id(k_axis)

            @pl.when(k == 0)
            def _():
                acc[0][...] = part

            @pl.when(k > 0)
            def _():
                acc[0][...] += part

            @pl.when(k == nk - 1)
            def _():
                o_ref[...] = acc[0][...].astype(o_ref.dtype)

    sem = tuple("arbitrary" if ax == k_axis else "parallel" for ax in range(len(grid)))
    return pl.pallas_call(
        body, grid=grid, in_specs=[a_spec, b_spec] + [HBM_OPERAND] * n_after, out_specs=o_spec, out_shape=out_shape,
        scratch_shapes=[pltpu.VMEM(o_block, F32)] if k_axis is not None else [],
        compiler_params=_cp(sem), name=name)(a, b, *after)


def _mm_cols(a, wg, *, tm, name, out_dtype=F32):
    m, k = a.shape
    ns = wg.shape[2]
    return _matmul(a, wg, grid=(m // tm, N_CHIPS),
                   a_spec=pl.BlockSpec((tm, k), lambda i, s: (i, 0)),
                   b_spec=pl.BlockSpec((None, k, ns), lambda i, s: (s, 0, 0)),
                   o_spec=pl.BlockSpec((tm, ns), lambda i, s: (i, s)),
                   out_shape=_sds((m, N_CHIPS * ns), out_dtype), dims=(1, 0), name=name)


def _mm_cols_nt(dy, wg, *, tm, name, out_dtype=F32, after=()):
    k, ns = wg.shape[1], wg.shape[2]
    m = dy.shape[0]
    a_spec = pl.BlockSpec((tm, ns), lambda i, s: (i, s))
    return _matmul(dy, wg, grid=(m // tm, N_CHIPS), a_spec=a_spec,
                   b_spec=pl.BlockSpec((None, k, ns), lambda i, s: (s, 0, 0)),
                   o_spec=pl.BlockSpec((tm, k), lambda i, s: (i, 0)),
                   out_shape=_sds((m, k), out_dtype), dims=(1, 1), k_axis=1, name=name, after=after)


def _mm_cols_tn(a, dy, *, ns, name, after=()):
    m, k = a.shape
    return _matmul(a, dy, grid=(N_CHIPS,), a_spec=pl.BlockSpec((m, k), lambda s: (0, 0)),
                   b_spec=pl.BlockSpec((m, ns), lambda s: (0, s)),
                   o_spec=pl.BlockSpec((None, k, ns), lambda s: (s, 0, 0)),
                   out_shape=_sds((N_CHIPS, k, ns), BF16), dims=(0, 0), name=name, after=after)


def _mm_plain(a, b, *, tm, tn, name, out_dtype=F32, dims=(1, 0), tk=None):
    m = a.shape[1 - dims[0]]
    kk = a.shape[dims[0]]
    n = b.shape[1 - dims[1]]
    tk = kk if tk is None else tk
    nk = kk // tk

    def a_idx(i, j, k):
        return (i, k) if dims[0] == 1 else (k, i)

    def b_idx(i, j, k):
        return (k, j) if dims[1] == 0 else (j, k)

    a_blk = (tm, tk) if dims[0] == 1 else (tk, tm)
    b_blk = (tk, tn) if dims[1] == 0 else (tn, tk)
    return _matmul(a, b, grid=(m // tm, n // tn, nk),
                   a_spec=pl.BlockSpec(a_blk, a_idx), b_spec=pl.BlockSpec(b_blk, b_idx),
                   o_spec=pl.BlockSpec((tm, tn), lambda i, j, k: (i, j)),
                   out_shape=_sds((m, n), out_dtype), dims=dims, k_axis=2 if nk > 1 else None, name=name)


def _rowwise(fn, tiled, full, outs, accs=(), *, tm, name, after=()):
    args, in_specs = [], []
    for t in tiled:
        if isinstance(t, tuple):
            arr, w, cb = t
            in_specs.append(pl.BlockSpec((tm, w), lambda i, cb=cb: (i, cb)))
        else:
            arr = t
            in_specs.append(pl.BlockSpec((tm, arr.shape[1]), lambda i: (i, 0)))
        args.append(arr)
    rows = args[0].shape[0]
    for f in full:
        in_specs.append(pl.BlockSpec(f.shape, lambda i, nd=f.ndim: (0,) * nd))
        args.append(f)
    out_specs = [pl.BlockSpec((tm, o.shape[1]), lambda i: (i, 0)) for o in outs]
    out_specs += [pl.BlockSpec(a.shape, lambda i, nd=len(a.shape): (0,) * nd) for a in accs]
    n_in, n_out = len(args), len(outs)
    in_specs += [HBM_OPERAND] * len(after)
    first_out = n_in + len(after)

    def body(*refs):
        res = fn(*[r[...] for r in refs[:n_in]])
        res = res if isinstance(res, (tuple, list)) else (res,)
        for r, v in zip(refs[first_out:first_out + n_out], res[:n_out]):
            r[...] = v.astype(r.dtype)
        i = pl.program_id(0)
        for r, v in zip(refs[first_out + n_out:], res[n_out:]):
            @pl.when(i == 0)
            def _(r=r, v=v):
                r[...] = v

            @pl.when(i > 0)
            def _(r=r, v=v):
                r[...] += v

    res = pl.pallas_call(
        body, grid=(rows // tm,), in_specs=in_specs, out_specs=out_specs, out_shape=list(outs) + list(accs),
        compiler_params=_cp(("arbitrary",) if accs else ("parallel",)), name=name)(*args, *after)
    return res


def _colsum(v):
    return jnp.sum(v, axis=0, keepdims=True)


def _ln_stats(z):
    mu = jnp.mean(z, axis=-1, keepdims=True)
    zc = z - mu
    var = jnp.mean(zc * zc, axis=-1, keepdims=True)
    rstd = lax.rsqrt(var + LN_EPS)
    return zc * rstd, rstd


def _ln_bwd(dy, xhat, rstd, g):
    dxh = dy * g
    m1 = jnp.mean(dxh, axis=-1, keepdims=True)
    m2 = jnp.mean(dxh * xhat, axis=-1, keepdims=True)
    return rstd * (dxh - m1 - xhat * m2)


def _swap_halves(t):
    w = t.shape[-1]
    lane = lax.broadcasted_iota(jnp.int32, t.shape, t.ndim - 1)
    return jnp.where((lane % HEAD_DIM) < HEAD_DIM // 2, pltpu.roll(t, w - HEAD_DIM // 2, t.ndim - 1),
                     pltpu.roll(t, HEAD_DIM // 2, t.ndim - 1))


PHASES = max(DILATIONS)
PAIR = 2 * HEAD_DIM
UNITS = SEQ // BLOCK
UNIT_BATCH = 8
ROPE_ROWS = 256


def _to_phase_rows(t):
    return t.reshape(SEQ // PHASES, PHASES, t.shape[1]).transpose(1, 0, 2).reshape(t.shape)


def _reorder_rows(arr, plus=None, *, to_phase, name, scale=1.0):
    def body(*refs):
        o_ref = refs[-1]
        for rho in range(PHASES):
            phase = pl.ds(rho * BLOCK, BLOCK)
            strided = pl.ds(rho, BLOCK, stride=PHASES)
            src, dst = (strided, phase) if to_phase else (phase, strided)
            val = refs[0][src, :]
            if scale != 1.0:
                val = val * scale
            if plus is not None:
                val = val + refs[1][src, :]
            o_ref[dst, :] = val

    spec = pl.BlockSpec((SEQ, BLOCK), lambda j: (0, j))
    ins = [arr] if plus is None else [arr, plus]
    return pl.pallas_call(body, grid=(arr.shape[1] // BLOCK,), in_specs=[spec] * len(ins), out_specs=spec,
                          out_shape=_sds(arr.shape), compiler_params=_cp(("parallel",)), name=name)(*ins)


def _rope(t, cf, ss):
    return t * cf + _swap_halves(t) * ss


def _rope_transposed(d, cf, ss):
    return d * cf + _swap_halves(d * ss)


def _unit_pieces(u, dil):
    pieces, length = PHASES // dil, 8 * dil
    if dil == 1:
        rho, i = 0, u
    elif dil == PHASES:
        rho, i = u, 0
    else:
        rho, i = jnp.bitwise_and(u, dil - 1), jnp.right_shift(u, dil.bit_length() - 1)
    before = jnp.maximum(i - 1, 0)
    cur = [pl.multiple_of((rho + dil * k) * BLOCK + length * i, 8) for k in range(pieces)]
    prev = [pl.multiple_of((rho + dil * k) * BLOCK + length * before, 8) for k in range(pieces)]
    return i, cur, prev


def _load_tile(ref, starts, dil):
    return jnp.concatenate([ref[pl.ds(st, 8 * dil), :] for st in starts], axis=0)


def _store_tile(ref, starts, dil, val, head=None, accumulate=False):
    length = 8 * dil
    lanes = slice(None) if head is None else pl.ds(head * HEAD_DIM, HEAD_DIM)
    cols = slice(None) if head is None else slice(head * HEAD_DIM, (head + 1) * HEAD_DIM)
    for k, st in enumerate(starts):
        piece = val[k * length:(k + 1) * length, cols]
        if accumulate:
            ref[pl.ds(st, length), lanes] += piece
        else:
            ref[pl.ds(st, length), lanes] = piece


def _tile_position(idx, dil):
    pieces, length = PHASES // dil, 8 * dil
    return pieces * jnp.bitwise_and(idx, length - 1) + jnp.right_shift(idx, length.bit_length() - 1)


def _band_mask(i, dil):
    row = lax.broadcasted_iota(jnp.int32, (BLOCK, 2 * BLOCK), 0)
    col = lax.broadcasted_iota(jnp.int32, (BLOCK, 2 * BLOCK), 1)
    key_pos = _tile_position(jnp.bitwise_and(col, BLOCK - 1), dil) + jnp.where(col >= BLOCK, 0, -BLOCK)
    dist = _tile_position(row, dil) - key_pos
    return (dist >= 0) & (dist <= BLOCK) & ((col >= BLOCK) | (i > 0))


def _causal_mask():
    row = lax.broadcasted_iota(jnp.int32, (BLOCK, BLOCK), 0)
    col = lax.broadcasted_iota(jnp.int32, (BLOCK, BLOCK), 1)
    return row >= col


def _pair_views(col0):
    return [pl.BlockSpec((SEQ, PAIR), lambda hp, g=g: (0, col0 // PAIR + g * (ATTN_WIDTH // PAIR) + hp))
            for g in range(len(DILATIONS))]


def _rotate(in_refs, out_refs, cf_ref, ss_ref, scale):
    def step(t, carry):
        rows = pl.ds(pl.multiple_of(t * ROPE_ROWS, ROPE_ROWS), ROPE_ROWS)
        cf, ss = cf_ref[rows, :] * scale, ss_ref[rows, :] * scale
        for i_ref, o_ref in zip(in_refs, out_refs):
            o_ref[rows, :] = _rope(i_ref[rows, :], cf, ss)
        return carry

    lax.fori_loop(0, SEQ // ROPE_ROWS, step, 0)


def _attention_fwd(proj, cos_f, sin_s):
    ng = len(DILATIONS)

    def body(*refs):
        q_refs, k_refs, v_refs = refs[:ng], refs[ng:2 * ng], refs[2 * ng:3 * ng]
        cf_ref, ss_ref, attn_ref, lse_ref = refs[3 * ng:3 * ng + 4]
        scratch = refs[3 * ng + 4:]
        qr_refs, kr_refs = scratch[:ng], scratch[ng:]
        _rotate(q_refs, qr_refs, cf_ref, ss_ref, 1.0 / math.sqrt(HEAD_DIM))
        _rotate(k_refs, kr_refs, cf_ref, ss_ref, 1.0)
        first = lax.broadcasted_iota(jnp.int32, (BLOCK, PAIR), 1) < HEAD_DIM
        for g, dil in enumerate(DILATIONS):
            two_blocks = SEQ // dil > BLOCK

            def units(t, carry, g=g, dil=dil, two_blocks=two_blocks):
                picked = [_unit_pieces(t * UNIT_BATCH + j, dil) for j in range(UNIT_BATCH)]

                def tiles(ref, with_prev=False):
                    if with_prev and two_blocks:
                        return jnp.stack([jnp.concatenate([_load_tile(ref, prev, dil), _load_tile(ref, rows, dil)],
                                                          axis=0) for _, rows, prev in picked])
                    return jnp.stack([_load_tile(ref, rows, dil) for _, rows, _ in picked])

                qq = tiles(qr_refs[g]).astype(BF16)
                kk = tiles(kr_refs[g], True).astype(BF16)
                vv = tiles(v_refs[g], True).astype(BF16)
                if two_blocks:
                    valid = jnp.stack([_band_mask(i, dil) for i, _, _ in picked])
                else:
                    valid = _causal_mask()[None]
                mine = first[None]
                zero = jnp.zeros_like(qq)
                outs, lses = [], []
                for qh in (jnp.where(mine, qq, zero), jnp.where(mine, zero, qq)):
                    s = jnp.einsum("pqd,pkd->pqk", qh, kk, preferred_element_type=F32)
                    s = jnp.where(valid, s, NEG_INF)
                    m = jnp.max(s, axis=-1, keepdims=True)
                    p = jnp.exp(s - m)
                    l = jnp.sum(p, axis=-1, keepdims=True)
                    outs.append(jnp.einsum("pqk,pkd->pqd", p.astype(BF16), vv, preferred_element_type=F32) * (1.0 / l))
                    lses.append(m + jnp.log(l))
                o = jnp.where(mine, outs[0], outs[1])
                lse = jnp.where(mine, lses[0], lses[1])
                if g > 0:
                    lse_old = tiles(lse_ref)
                    m = jnp.maximum(lse_old, lse)
                    lse_new = m + jnp.log(jnp.exp(lse_old - m) + jnp.exp(lse - m))
                    o = tiles(attn_ref) * jnp.exp(lse_old - lse_new) + o * jnp.exp(lse - lse_new)
                    lse = lse_new
                for j, (_, rows, _) in enumerate(picked):
                    _store_tile(attn_ref, rows, dil, o[j])
                    _store_tile(lse_ref, rows, dil, lse[j])
                return carry

            lax.fori_loop(0, UNITS // UNIT_BATCH, units, 0)

    whole = pl.BlockSpec((SEQ, PAIR), lambda hp: (0, 0))
    out = pl.BlockSpec((SEQ, PAIR), lambda hp: (0, hp))
    return pl.pallas_call(
        body, grid=(ATTN_WIDTH // PAIR,),
        in_specs=_pair_views(0) + _pair_views(QKV_WIDTH) + _pair_views(2 * QKV_WIDTH) + [whole, whole],
        out_specs=[out, out], out_shape=[_sds((SEQ, ATTN_WIDTH)), _sds((SEQ, ATTN_WIDTH))],
        scratch_shapes=[pltpu.VMEM((SEQ, PAIR), F32)] * (2 * ng),
        compiler_params=_cp(("parallel",)), name="attention_fwd")(*([proj] * (3 * ng)), cos_f, sin_s)


def _attention_bwd(g, proj, cos_f, sin_s, d_attn, attn, lse):
    dil = DILATIONS[g]
    two_blocks = SEQ // dil > BLOCK

    def body(q_ref, k_ref, v_ref, cf_ref, ss_ref, do_ref, o_ref, lse_ref, dq_out, dk_out, dv_out,
             qr_ref, kr_ref, dq_acc, dk_acc, dv_acc):
        _rotate([q_ref], [qr_ref], cf_ref, ss_ref, 1.0 / math.sqrt(HEAD_DIM))
        _rotate([k_ref], [kr_ref], cf_ref, ss_ref, 1.0)
        dk_acc[...] = jnp.zeros_like(dk_acc)
        dv_acc[...] = jnp.zeros_like(dv_acc)
        nk = 2 * BLOCK if two_blocks else BLOCK
        first = lax.broadcasted_iota(jnp.int32, (BLOCK, PAIR), 1) < HEAD_DIM
        first_k = lax.broadcasted_iota(jnp.int32, (nk, PAIR), 1) < HEAD_DIM

        def units(t, carry):
            picked = [_unit_pieces(t * UNIT_BATCH + j, dil) for j in range(UNIT_BATCH)]

            def tiles(ref, with_prev=False):
                if with_prev and two_blocks:
                    return jnp.stack([jnp.concatenate([_load_tile(ref, prev, dil), _load_tile(ref, rows, dil)], axis=0)
                                      for _, rows, prev in picked])
                return jnp.stack([_load_tile(ref, rows, dil) for _, rows, _ in picked])

            qq = tiles(qr_ref).astype(BF16)
            kk = tiles(kr_ref, True).astype(BF16)
            vv = tiles(v_ref, True).astype(BF16)
            dof = tiles(do_ref)
            dd = dof * tiles(o_ref)
            lse3 = tiles(lse_ref)
            dob = dof.astype(BF16)
            if two_blocks:
                valid = jnp.stack([_band_mask(i, dil) for i, _, _ in picked])
            else:
                valid = _causal_mask()[None]
            zq, zf = jnp.zeros_like(qq), jnp.zeros_like(dd)
            dqs, dks, dvs = [], [], []
            for head in range(2):
                mine = first[None] if head == 0 else jnp.logical_not(first)[None]
                delta = jnp.sum(jnp.where(mine, dd, zf), axis=-1, keepdims=True)
                lse_h = lse3[:, :, head * HEAD_DIM:head * HEAD_DIM + 1]
                s = jnp.einsum("pqd,pkd->pqk", jnp.where(mine, qq, zq), kk, preferred_element_type=F32)
                p = jnp.where(valid, jnp.exp(s - lse_h), 0.0)
                dp = jnp.einsum("pqd,pkd->pqk", jnp.where(mine, dob, zq), vv, preferred_element_type=F32)
                ds = (p * (dp - delta)).astype(BF16)
                dqs.append(jnp.einsum("pqk,pkd->pqd", ds, kk, preferred_element_type=F32))
                dks.append(jnp.einsum("pqk,pqd->pkd", ds, qq, preferred_element_type=F32))
                dvs.append(jnp.einsum("pqk,pqd->pkd", p.astype(BF16), dob, preferred_element_type=F32))
            dq = jnp.where(first[None], dqs[0], dqs[1])
            dk = jnp.where(first_k[None], dks[0], dks[1])
            dv = jnp.where(first_k[None], dvs[0], dvs[1])
            for j, (_, rows, prev) in enumerate(picked):
                _store_tile(dq_acc, rows, dil, dq[j])
                _store_tile(dk_acc, rows, dil, dk[j, nk - BLOCK:], accumulate=True)
                _store_tile(dv_acc, rows, dil, dv[j, nk - BLOCK:], accumulate=True)
                if two_blocks:
                    _store_tile(dk_acc, prev, dil, dk[j, :BLOCK], accumulate=True)
                    _store_tile(dv_acc, prev, dil, dv[j, :BLOCK], accumulate=True)
            return carry

        lax.fori_loop(0, UNITS // UNIT_BATCH, units, 0)

        def finish(t, carry):
            rows = pl.ds(pl.multiple_of(t * ROPE_ROWS, ROPE_ROWS), ROPE_ROWS)
            cf, ss = cf_ref[rows, :], ss_ref[rows, :]
            dq = dq_acc[rows, :] * (1.0 / math.sqrt(HEAD_DIM))
            dq_out[rows, :] = _rope_transposed(dq, cf, ss).astype(BF16)
            dk_out[rows, :] = _rope_transposed(dk_acc[rows, :], cf, ss).astype(BF16)
            dv_out[rows, :] = dv_acc[rows, :].astype(BF16)
            return carry

        lax.fori_loop(0, SEQ // ROPE_ROWS, finish, 0)

    whole = pl.BlockSpec((SEQ, PAIR), lambda hp: (0, 0))
    pair = pl.BlockSpec((SEQ, PAIR), lambda hp: (0, hp))
    views = [_pair_views(col0)[g] for col0 in (0, QKV_WIDTH, 2 * QKV_WIDTH)]
    return pl.pallas_call(
        body, grid=(ATTN_WIDTH // PAIR,), in_specs=views + [whole, whole, pair, pair, pair],
        out_specs=[pair, pair, pair], out_shape=[_sds((SEQ, ATTN_WIDTH), BF16)] * 3,
        scratch_shapes=[pltpu.VMEM((SEQ, PAIR), F32)] * 5,
        compiler_params=_cp(("parallel",)), name=f"attention_bwd_{g}")(proj, proj, proj, cos_f, sin_s, d_attn, attn, lse)


def _cmul(ar, ai, br, bi):
    return ar * br - ai * bi, ar * bi + ai * br


def _pow256(ar, ai):
    for _ in range(8):
        ar, ai = _cmul(ar, ai, ar, ai)
    return ar, ai


def _chunk_carries(first_r, first_i, pr, pi, reverse):
    rows = lax.broadcasted_iota(jnp.int32, first_r.shape, 0)
    out_r = jnp.zeros_like(first_r)
    out_i = jnp.zeros_like(first_i)
    hr = jnp.zeros_like(first_r[0:1])
    hi = jnp.zeros_like(hr)
    order = range(SCAN_CHUNKS - 1, -1, -1) if reverse else range(SCAN_CHUNKS)
    for c in order:
        out_r = jnp.where(rows == c, hr, out_r)
        out_i = jnp.where(rows == c, hi, out_i)
        tr, ti = _cmul(pr[0:1], pi[0:1], hr, hi)
        hr = first_r[c:c + 1] + tr
        hi = first_i[c:c + 1] + ti
    return out_r, out_i


def _tile(j):
    return pl.ds(pl.multiple_of(j * SCAN_CHUNKS, SCAN_CHUNKS), SCAN_CHUNKS)


def _to_scan_rows(t):
    per = SCAN_STEPS // PHASES
    return t.reshape(PHASES, SCAN_CHUNKS, per, t.shape[1]).transpose(2, 0, 1, 3).reshape(t.shape)


def _from_scan_rows(t):
    per = SCAN_STEPS // PHASES
    return t.reshape(per, PHASES, SCAN_CHUNKS, t.shape[1]).transpose(1, 2, 0, 3).reshape(t.shape)


def _scan_in_place(hr_ref, hi_ref, a_r, a_i):
    def local(j, carry):
        tr, ti = _cmul(a_r, a_i, carry[0], carry[1])
        nr = tr + hr_ref[_tile(j), :]
        ni = ti + hi_ref[_tile(j), :]
        hr_ref[_tile(j), :] = nr
        hi_ref[_tile(j), :] = ni
        return nr, ni

    zero = jnp.zeros_like(a_r)
    last_r, last_i = lax.fori_loop(0, SCAN_STEPS, local, (zero, zero), unroll=4)
    pr, pi = _pow256(a_r, a_i)
    er, ei = _chunk_carries(last_r, last_i, pr, pi, reverse=False)

    def fix(j, carry):
        tr, ti = _cmul(carry[0], carry[1], er, ei)
        hr_ref[_tile(j), :] += tr
        hi_ref[_tile(j), :] += ti
        return _cmul(carry[0], carry[1], a_r, a_i)

    lax.fori_loop(0, SCAN_STEPS, fix, (a_r, a_i), unroll=4)
    return er, ei


def _reverse_scan_in_place(lr_ref, li_ref, hr_ref, hi_ref, er, ei, a_r, a_i):
    def local(t, carry):
        j = SCAN_STEPS - 1 - t
        tr, ti = _cmul(a_r, a_i, carry[0], carry[1])
        nr = tr + lr_ref[_tile(j), :]
        ni = ti + li_ref[_tile(j), :]
        lr_ref[_tile(j), :] = nr
        li_ref[_tile(j), :] = ni
        return nr, ni

    zero = jnp.zeros_like(a_r)
    first_r, first_i = lax.fori_loop(0, SCAN_STEPS, local, (zero, zero), unroll=4)
    pr, pi = _pow256(a_r, a_i)
    nxt_r, nxt_i = _chunk_carries(first_r, first_i, pr, pi, reverse=True)

    def accumulate(lam_r, lam_i, hp_r, hp_i, acc):
        return (acc[0] + lam_r * hp_r + lam_i * hp_i, acc[1] + lam_i * hp_r - lam_r * hp_i)

    def fix(t, carry):
        qr, qi, acc_r, acc_i = carry
        j = SCAN_STEPS - 1 - t
        tr, ti = _cmul(qr, qi, nxt_r, nxt_i)
        lam_r = lr_ref[_tile(j), :] + tr
        lam_i = li_ref[_tile(j), :] + ti
        lr_ref[_tile(j), :] = lam_r
        li_ref[_tile(j), :] = lam_i
        acc_r, acc_i = accumulate(lam_r, lam_i, hr_ref[_tile(j - 1), :], hi_ref[_tile(j - 1), :], (acc_r, acc_i))
        qr, qi = _cmul(qr, qi, a_r, a_i)
        return qr, qi, acc_r, acc_i

    qr, qi, acc_r, acc_i = lax.fori_loop(0, SCAN_STEPS - 1, fix, (a_r, a_i, zero, zero), unroll=4)
    tr, ti = _cmul(qr, qi, nxt_r, nxt_i)
    lam_r = lr_ref[_tile(0), :] + tr
    lam_i = li_ref[_tile(0), :] + ti
    lr_ref[_tile(0), :] = lam_r
    li_ref[_tile(0), :] = lam_i
    acc_r, acc_i = accumulate(lam_r, lam_i, er, ei, (acc_r, acc_i))
    return jnp.sum(acc_r, axis=0, keepdims=True), jnp.sum(acc_i, axis=0, keepdims=True)


def _rope_tables():
    half = HEAD_DIM // 2
    inv_freq = ROPE_THETA ** (-jnp.arange(half, dtype=F32) / half)
    ang = jnp.arange(SEQ, dtype=F32)[:, None] * inv_freq[None, :]
    cos, sin = jnp.cos(ang), jnp.sin(ang)
    cos_f = jnp.concatenate([cos, cos, cos, cos], axis=1)
    sin_s = jnp.concatenate([-sin, sin, -sin, sin], axis=1)
    return cos_f, sin_s


def _ssm_discretise(a_re, a_im, log_dt, b_re, b_im):
    lam = lax.complex(a_re, a_im)
    dt = jnp.exp(log_dt)[:, None]
    a_bar = jnp.exp(lam * dt)
    b_bar = ((a_bar - 1.0) / lam)[..., None] * lax.complex(b_re, b_im)
    return a_bar.real, a_bar.imag, b_bar.real, b_bar.imag


SSM_SLABS = 4
SLAB_GROUPS = SSM_GROUPS // SSM_SLABS
SLAB_IN = SSM_WIDTH // SSM_SLABS
SLAB_STATE = SSM_LANES // SSM_SLABS


def _slab_block_diag(blocks):
    _, r, c = blocks.shape
    eye = jnp.eye(SLAB_GROUPS, dtype=blocks.dtype)
    b5 = blocks.reshape(SSM_SLABS, SLAB_GROUPS, r, 1, c) * eye[None, :, None, :, None]
    return b5.reshape(SSM_SLABS, SLAB_GROUPS * r, SLAB_GROUPS * c)


def _diag_blocks(a, b):
    ra, cb = a.shape[1], b.shape[1]
    wa, wb = ra // SLAB_GROUPS, cb // SLAB_GROUPS
    d = lax.dot_general(a, b, (((0,), (0,)), ((), ())), preferred_element_type=F32)
    row_g = jnp.right_shift(lax.broadcasted_iota(jnp.int32, (ra, cb), 0), wa.bit_length() - 1)
    col_g = jnp.right_shift(lax.broadcasted_iota(jnp.int32, (ra, cb), 1), wb.bit_length() - 1)
    d = jnp.where(row_g == col_g, d, 0.0)
    fold = (jnp.bitwise_and(lax.broadcasted_iota(jnp.int32, (cb, wb), 0), wb - 1)
            == lax.broadcasted_iota(jnp.int32, (cb, wb), 1)).astype(F32)
    return jnp.dot(d, fold, preferred_element_type=F32, precision=lax.Precision.HIGHEST)


def _slab_specs():
    tok = pl.BlockSpec((SEQ, SLAB_IN), lambda j: (0, j))
    state = pl.BlockSpec((SEQ, SLAB_STATE), lambda j: (0, j))
    b_in = pl.BlockSpec((None, SLAB_IN, SLAB_STATE), lambda j: (j, 0, 0))
    c_out = pl.BlockSpec((None, SLAB_STATE, SLAB_IN), lambda j: (j, 0, 0))
    vec = pl.BlockSpec((1, SLAB_STATE), lambda j: (0, j))
    ent = pl.BlockSpec((SCAN_CHUNKS, SLAB_STATE), lambda j: (0, j))
    return tok, state, b_in, c_out, vec, ent


def _ssm_forward(u, b_in_r, b_in_i, c_out_r, c_out_ni, a_r, a_i):
    def body(u_ref, br_ref, bi_ref, cr_ref, ci_ref, ar_ref, ai_ref, y_ref, hr_ref, hi_ref, er_ref, ei_ref):
        uu = u_ref[...]
        hr_ref[...] = jnp.dot(uu, br_ref[...], preferred_element_type=F32)
        hi_ref[...] = jnp.dot(uu, bi_ref[...], preferred_element_type=F32)
        a_re = jnp.broadcast_to(ar_ref[...], (SCAN_CHUNKS, SLAB_STATE))
        a_im = jnp.broadcast_to(ai_ref[...], (SCAN_CHUNKS, SLAB_STATE))
        er_ref[...], ei_ref[...] = _scan_in_place(hr_ref, hi_ref, a_re, a_im)
        y_ref[...] = (jnp.dot(hr_ref[...].astype(BF16), cr_ref[...], preferred_element_type=F32)
                      + jnp.dot(hi_ref[...].astype(BF16), ci_ref[...], preferred_element_type=F32))

    tok, state, b_in, c_out, vec, ent = _slab_specs()
    return pl.pallas_call(
        body, grid=(SSM_SLABS,), in_specs=[tok, b_in, b_in, c_out, c_out, vec, vec],
        out_specs=[tok, state, state, ent, ent],
        out_shape=[_sds((SEQ, SSM_WIDTH)), _sds((SEQ, SSM_LANES)), _sds((SEQ, SSM_LANES)),
                   _sds((SCAN_CHUNKS, SSM_LANES)), _sds((SCAN_CHUNKS, SSM_LANES))],
        compiler_params=_cp(("parallel",)), name="ssm_forward")(u, b_in_r, b_in_i, c_out_r, c_out_ni, a_r, a_i)


def _ssm_backward(d_y, d_u_skip, u, h_r, h_i, e_r, e_i, b_in_r, b_in_i, c_out_r, c_out_ni, a_r, a_i):
    def body(dy_ref, skip_ref, u_ref, hr_ref, hi_ref, er_ref, ei_ref, br_ref, bi_ref, cr_ref, ci_ref, ar_ref, ai_ref,
             du_ref, dar_ref, dai_ref, dcr_ref, dci_ref, dbr_ref, dbi_ref, lr_ref, li_ref):
        dy = dy_ref[...]
        lr_ref[...] = _dot_nt(dy, cr_ref[...])
        li_ref[...] = _dot_nt(dy, ci_ref[...])
        a_re = jnp.broadcast_to(ar_ref[...], (SCAN_CHUNKS, SLAB_STATE))
        a_im = -jnp.broadcast_to(ai_ref[...], (SCAN_CHUNKS, SLAB_STATE))
        dar_ref[...], dai_ref[...] = _reverse_scan_in_place(lr_ref, li_ref, hr_ref, hi_ref, er_ref[...], ei_ref[...],
                                                            a_re, a_im)
        dcr_ref[...] = _diag_blocks(dy, hr_ref[...].astype(BF16))
        dci_ref[...] = _diag_blocks(dy, hi_ref[...].astype(BF16))
        lam_r, lam_i = lr_ref[...].astype(BF16), li_ref[...].astype(BF16)
        uu = u_ref[...]
        dbr_ref[...] = _diag_blocks(uu, lam_r)
        dbi_ref[...] = _diag_blocks(uu, lam_i)
        du = skip_ref[...] + _dot_nt(lam_r, br_ref[...]) + _dot_nt(lam_i, bi_ref[...])
        du_ref[...] = du.astype(BF16)

    tok, state, b_in, c_out, vec, ent = _slab_specs()
    db = pl.BlockSpec((SLAB_IN, SSM_STATE), lambda j: (j, 0))
    return pl.pallas_call(
        body, grid=(SSM_SLABS,), in_specs=[tok, tok, tok, state, state, ent, ent, b_in, b_in, c_out, c_out, vec, vec],
        out_specs=[tok, vec, vec, db, db, db, db],
        out_shape=[_sds((SEQ, SSM_WIDTH), BF16), _sds((1, SSM_LANES)), _sds((1, SSM_LANES))]
        + [_sds((SSM_WIDTH, SSM_STATE))] * 4,
        scratch_shapes=[pltpu.VMEM((SEQ, SLAB_STATE), F32)] * 2,
        compiler_params=_cp(("parallel",)), name="ssm_backward")(
            d_y, d_u_skip, u, h_r, h_i, e_r, e_i, b_in_r, b_in_i, c_out_r, c_out_ni, a_r, a_i)


FF_ROWS = 1024
FF_SHARD = D_FF // N_CHIPS


def _dot_nt(a, b):
    return lax.dot_general(a, b, (((1,), (1,)), ((), ())), preferred_element_type=F32)


def _ffn_up(h, w_gate_t, w_up_t):
    def body(h_ref, wg_ref, wu_ref, a_ref, b_ref, act_ref):
        hb = h_ref[...].astype(BF16)
        a = _dot_nt(hb, wg_ref[...])
        b = _dot_nt(hb, wu_ref[...])
        a_ref[...] = a
        b_ref[...] = b
        act_ref[...] = (a * jax.nn.sigmoid(a) * b).astype(BF16)

    w_spec = pl.BlockSpec((None, FF_SHARD, D_MODEL), lambda i, k: (k, 0, 0))
    o_spec = pl.BlockSpec((None, FF_ROWS, FF_SHARD), lambda i, k: (k, i, 0))
    shape = (N_CHIPS, SEQ, FF_SHARD)
    return pl.pallas_call(
        body, grid=(SEQ // FF_ROWS, N_CHIPS),
        in_specs=[pl.BlockSpec((FF_ROWS, D_MODEL), lambda i, k: (i, 0)), w_spec, w_spec],
        out_specs=[o_spec, o_spec, o_spec], out_shape=[_sds(shape), _sds(shape), _sds(shape, BF16)],
        compiler_params=_cp(("parallel", "parallel")), name="ffn_up")(h, w_gate_t, w_up_t)


def _ffn_down_ln2_loss(act, w_down, h, tgt, ln_g, ln_b):
    def body(act_ref, w_ref, h_ref, tgt_ref, g_ref, b_ref, dz_ref, loss_ref, dg_ref, db_ref, acc):
        i, k = pl.program_id(0), pl.program_id(1)
        part = jnp.dot(act_ref[...], w_ref[...], preferred_element_type=F32)

        @pl.when(k == 0)
        def _():
            acc[...] = part

        @pl.when(k > 0)
        def _():
            acc[...] += part

        @pl.when(k == N_CHIPS - 1)
        def _():
            g = g_ref[...]
            xhat, rstd = _ln_stats(DN_ALPHA * h_ref[...] + acc[...])
            err = xhat * g + b_ref[...] - tgt_ref[...]
            d_out = err * (1.0 / D_MODEL)
            dz_ref[...] = _ln_bwd(d_out, xhat, rstd, g)
            loss_rows = jnp.sum(err * err, axis=-1, keepdims=True) * (0.5 / D_MODEL)
            sums = (jnp.broadcast_to(jnp.sum(loss_rows, axis=0, keepdims=True), loss_ref.shape),
                    _colsum(d_out * xhat), _colsum(d_out))
            for ref, val in zip((loss_ref, dg_ref, db_ref), sums):
                @pl.when(i == 0)
                def _(ref=ref, val=val):
                    ref[...] = val

                @pl.when(i > 0)
                def _(ref=ref, val=val):
                    ref[...] += val

    row = pl.BlockSpec((FF_ROWS, D_MODEL), lambda i, k: (i, 0))
    vec = pl.BlockSpec((1, D_MODEL), lambda i, k: (0, 0))
    return pl.pallas_call(
        body, grid=(SEQ // FF_ROWS, N_CHIPS),
        in_specs=[pl.BlockSpec((None, FF_ROWS, FF_SHARD), lambda i, k: (k, i, 0)),
                  pl.BlockSpec((None, FF_SHARD, D_MODEL), lambda i, k: (k, 0, 0)), row, row, vec, vec],
        out_specs=[row, pl.BlockSpec((1, BLOCK), lambda i, k: (0, 0)), vec, vec],
        out_shape=[_sds((SEQ, D_MODEL)), _sds((1, BLOCK)), _sds((1, D_MODEL)), _sds((1, D_MODEL))],
        scratch_shapes=[pltpu.VMEM((FF_ROWS, D_MODEL), F32)],
        compiler_params=_cp(("arbitrary", "arbitrary")), name="ffn_down_ln2_loss")(act, w_down, h, tgt, ln_g, ln_b)


def _ffn_down_bwd(dz, w_down, a, b):
    def body(dz_ref, wd_ref, a_ref, b_ref, da_ref, db_ref):
        d_act = _dot_nt(dz_ref[...].astype(BF16), wd_ref[...])
        av = a_ref[...]
        sg = jax.nn.sigmoid(av)
        da_ref[...] = (d_act * b_ref[...] * sg * (1.0 + av * (1.0 - sg))).astype(BF16)
        db_ref[...] = (d_act * av * sg).astype(BF16)

    t_spec = pl.BlockSpec((None, FF_ROWS, FF_SHARD), lambda i, k: (k, i, 0))
    shape = (N_CHIPS, SEQ, FF_SHARD)
    return pl.pallas_call(
        body, grid=(SEQ // FF_ROWS, N_CHIPS),
        in_specs=[pl.BlockSpec((FF_ROWS, D_MODEL), lambda i, k: (i, 0)),
                  pl.BlockSpec((None, FF_SHARD, D_MODEL), lambda i, k: (k, 0, 0)), t_spec, t_spec],
        out_specs=[t_spec, t_spec], out_shape=[_sds(shape, BF16), _sds(shape, BF16)],
        compiler_params=_cp(("parallel", "parallel")), name="ffn_down_bwd")(dz, w_down, a, b)


def _ffn_dh(d_a, d_b, w_gate_t, w_up_t):
    def body(da_ref, db_ref, wg_ref, wu_ref, o_ref, acc):
        k = pl.program_id(1)
        part = (jnp.dot(da_ref[...], wg_ref[...], preferred_element_type=F32)
                + jnp.dot(db_ref[...], wu_ref[...], preferred_element_type=F32))

        @pl.when(k == 0)
        def _():
            acc[...] = part

        @pl.when(k > 0)
        def _():
            acc[...] += part

        @pl.when(k == N_CHIPS - 1)
        def _():
            o_ref[...] = acc[...]

    t_spec = pl.BlockSpec((None, FF_ROWS, FF_SHARD), lambda i, k: (k, i, 0))
    w_spec = pl.BlockSpec((None, FF_SHARD, D_MODEL), lambda i, k: (k, 0, 0))
    return pl.pallas_call(
        body, grid=(SEQ // FF_ROWS, N_CHIPS), in_specs=[t_spec, t_spec, w_spec, w_spec],
        out_specs=pl.BlockSpec((FF_ROWS, D_MODEL), lambda i, k: (i, 0)), out_shape=_sds((SEQ, D_MODEL)),
        scratch_shapes=[pltpu.VMEM((FF_ROWS, D_MODEL), F32)],
        compiler_params=_cp(("parallel", "arbitrary")), name="ffn_dh")(d_a, d_b, w_gate_t, w_up_t)


def _local_step(x, tgt, wts, small):
    s = SEQ
    cos_f, sin_s = [_to_phase_rows(t) for t in _rope_tables()]
    x = _reorder_rows(x, to_phase=True, name="phase_rows_x")
    tgt = _reorder_rows(tgt, to_phase=True, name="phase_rows_target")

    proj = _mm_cols(x, wts["w_in"], tm=1024, name="proj")

    attn, lse = _attention_fwd(proj, cos_f, sin_s)

    (abar_r, abar_i, bbar_r, bbar_i), ssm_vjp = jax.vjp(
        _ssm_discretise, small["ssm_a_re"], small["ssm_a_im"], small["ssm_log_dt"], small["ssm_b_re"], small["ssm_b_im"])
    b_in_r, b_in_i = [_slab_block_diag(b.transpose(0, 2, 1)).astype(BF16) for b in (bbar_r, bbar_i)]
    c_out_r = _slab_block_diag(small["ssm_c_re"].transpose(0, 2, 1)).astype(BF16)
    c_out_ni = _slab_block_diag(-small["ssm_c_im"].transpose(0, 2, 1)).astype(BF16)
    a_r, a_i = abar_r.reshape(1, SSM_LANES), abar_i.reshape(1, SSM_LANES)
    d_skip = small["ssm_d"].reshape(1, SSM_WIDTH)

    u_f = _to_scan_rows(proj[:, 3 * QKV_WIDTH:3 * QKV_WIDTH + SSM_WIDTH])
    u_p = u_f.astype(BF16)
    y_c, h_r, h_i, e_r, e_i = _ssm_forward(u_p, b_in_r, b_in_i, c_out_r, c_out_ni, a_r, a_i)

    def branch(t, wg):
        return jnp.concatenate([jnp.dot(t, wg[k], preferred_element_type=F32) for k in range(N_CHIPS)], axis=1)

    def branch_t(t, wg):
        ns = wg.shape[2]
        return sum(_dot_nt(t[:, k * ns:(k + 1) * ns], wg[k]) for k in range(N_CHIPS))

    def gelu_glu(yc, u, dsk, wg):
        y = yc + dsk * u
        gel = (0.5 * y * (1.0 + jnp.tanh(GELU_C * (y + GELU_K * y * y * y)))).astype(BF16)
        glu = branch(gel, wg)
        return y, gel, glu, glu[:, :SSM_WIDTH] * jax.nn.sigmoid(glu[:, SSM_WIDTH:])

    y_s5, gel, glu, y_glu = _rowwise(
        gelu_glu, [y_c, u_f], [d_skip, wts["w_glu"]],
        [_sds((s, SSM_WIDTH)), _sds((s, SSM_WIDTH), BF16), _sds((s, 2 * SSM_WIDTH)), _sds((s, SSM_WIDTH), BF16)],
        tm=512, name="ssm_gelu_glu")
    y_glu = _from_scan_rows(y_glu)

    gl0 = (proj, D_MODEL, (3 * QKV_WIDTH + SSM_WIDTH) // D_MODEL)
    gl1 = (proj, D_MODEL, (3 * QKV_WIDTH + SSM_WIDTH) // D_MODEL + 1)
    b_gate = small["b_gate"]
    w_out = wts["w_out"].reshape(D_MODEL, D_MODEL)

    def mix_ln1(l0, l1, at, yg, xv, bg, wa, ws, wo, g, b):
        ya = branch(at.astype(BF16), wa)
        ys = branch(yg, ws)
        mixed = (jax.nn.sigmoid(l0 + bg[0:1]) * ya + jax.nn.sigmoid(l1 + bg[1:2]) * ys).astype(BF16)
        z = DN_ALPHA * xv + jnp.dot(mixed, wo, preferred_element_type=F32)
        xhat, _ = _ln_stats(z)
        return ya, ys, mixed, z, xhat * g + b

    y_attn, y_ssm, mixed, z1, h = _rowwise(
        mix_ln1, [gl0, gl1, attn, y_glu, x],
        [b_gate, wts["w_attn_br"], wts["w_ssm_br"], w_out, small["ln1_g"], small["ln1_b"]],
        [_sds((s, D_MODEL)), _sds((s, D_MODEL)), _sds((s, D_MODEL), BF16), _sds((s, D_MODEL)), _sds((s, D_MODEL))],
        tm=256, name="mix_ln1")

    nf = D_FF // N_CHIPS
    w_gate_t, w_up_t, w_down = wts["w_ff_gate"], wts["w_ff_up"], wts["w_ff_down"]
    ff_a, ff_b, act = _ffn_up(h, w_gate_t, w_up_t)
    dz2, loss_v, d_ln2_g, d_ln2_b = _ffn_down_ln2_loss(act, w_down, h, tgt, small["ln2_g"], small["ln2_b"])

    d_a, d_b = _ffn_down_bwd(dz2, w_down, ff_a, ff_b)

    def grad_rows(lhs, rhs, name):
        return _matmul(lhs, rhs, grid=(N_CHIPS,), a_spec=pl.BlockSpec((None, s, nf), lambda k: (k, 0, 0)),
                       b_spec=pl.BlockSpec((s, D_MODEL), lambda k: (0, 0)),
                       o_spec=pl.BlockSpec((None, nf, D_MODEL), lambda k: (k, 0, 0)),
                       out_shape=_sds((N_CHIPS, nf, D_MODEL), BF16), dims=(0, 0), name=name)

    g_w_ff_down = grad_rows(act, dz2, "g_w_ff_down")
    g_w_ff_gate = grad_rows(d_a, h, "g_w_ff_gate")
    g_w_ff_up = grad_rows(d_b, h, "g_w_ff_up")
    dh_ff = _ffn_dh(d_a, d_b, w_gate_t, w_up_t)

    def ln1_gate_bwd(dz, dff, z, l0, l1, ya, ys, g, bg, wo, wa, ws):
        xhat, rstd = _ln_stats(z)
        dh = DN_ALPHA * dz + dff
        dz_in = _ln_bwd(dh, xhat, rstd, g)
        dm = _dot_nt(dz_in.astype(BF16), wo)
        g0 = jax.nn.sigmoid(l0 + bg[0:1])
        g1 = jax.nn.sigmoid(l1 + bg[1:2])
        dl0 = dm * ya * g0 * (1.0 - g0)
        dl1 = dm * ys * g1 * (1.0 - g1)
        dya, dys = (dm * g0).astype(BF16), (dm * g1).astype(BF16)
        return (dz_in, dya, dys, jnp.concatenate([dl0, dl1], axis=1), branch_t(dya, wa), branch_t(dys, ws),
                _colsum(dh * xhat), _colsum(dh), _colsum(dl0), _colsum(dl1))

    dz1, d_y_attn, d_y_ssm, d_gl, d_attn, d_y_glu, d_ln1_g, d_ln1_b, d_bg0, d_bg1 = _rowwise(
        ln1_gate_bwd, [dz2, dh_ff, z1, gl0, gl1, y_attn, y_ssm],
        [small["ln1_g"], b_gate, w_out, wts["w_attn_br"], wts["w_ssm_br"]],
        [_sds((s, D_MODEL)), _sds((s, D_MODEL), BF16), _sds((s, D_MODEL), BF16), _sds((s, 2 * D_MODEL), BF16),
         _sds((s, ATTN_WIDTH)), _sds((s, SSM_WIDTH))],
        [_sds((1, D_MODEL))] * 4, tm=256, name="ln1_gate_bwd")
    g_w_out = _mm_plain(mixed, dz1, tm=D_MODEL, tn=512, dims=(0, 0), out_dtype=BF16, name="g_w_out")
    g_w_out = g_w_out.reshape(N_CHIPS, D_MODEL // N_CHIPS, D_MODEL)

    g_w_ssm_br = _mm_cols_tn(y_glu, d_y_ssm, ns=D_MODEL // N_CHIPS, name="g_w_ssm_br")
    d_y_glu = _to_scan_rows(d_y_glu)

    def glu_gelu_bwd(dyg, gl, y, u, dsk, wg):
        ga, gb = gl[:, :SSM_WIDTH], gl[:, SSM_WIDTH:]
        sg = jax.nn.sigmoid(gb)
        d_gl = jnp.concatenate([dyg * sg, dyg * ga * sg * (1.0 - sg)], axis=1).astype(BF16)
        dg = branch_t(d_gl, wg)
        th = jnp.tanh(GELU_C * (y + GELU_K * y * y * y))
        dy = dg * (0.5 * (1.0 + th) + 0.5 * y * (1.0 - th * th) * GELU_C * (1.0 + 3.0 * GELU_K * y * y))
        return d_gl, dy, dy * dsk, _colsum(dy * u)

    d_glu, d_y, d_u_skip, d_ssm_d = _rowwise(
        glu_gelu_bwd, [d_y_glu, glu, y_s5, u_f], [d_skip, wts["w_glu"]],
        [_sds((s, 2 * SSM_WIDTH), BF16), _sds((s, SSM_WIDTH), BF16), _sds((s, SSM_WIDTH))], [_sds((1, SSM_WIDTH))],
        tm=512, name="glu_gelu_bwd")
    g_w_glu = _mm_cols_tn(gel, d_glu, ns=2 * SSM_WIDTH // N_CHIPS, name="g_w_glu")
    d_u, d_abar_r, d_abar_i, d_c_r, d_c_ni, d_bin_r, d_bin_i = _ssm_backward(
        d_y, d_u_skip, u_p, h_r, h_i, e_r, e_i, b_in_r, b_in_i, c_out_r, c_out_ni, a_r, a_i)
    d_u = _from_scan_rows(d_u)
    d_bbar_r = d_bin_r.reshape(SSM_GROUPS, SSM_GROUP, SSM_STATE).transpose(0, 2, 1)
    d_bbar_i = d_bin_i.reshape(SSM_GROUPS, SSM_GROUP, SSM_STATE).transpose(0, 2, 1)
    d_a_re, d_a_im, d_log_dt, d_b_re, d_b_im = ssm_vjp(
        (d_abar_r.reshape(SSM_GROUPS, SSM_STATE), d_abar_i.reshape(SSM_GROUPS, SSM_STATE), d_bbar_r, d_bbar_i))
    d_c_re = d_c_r.reshape(SSM_GROUPS, SSM_GROUP, SSM_STATE)
    d_c_im = -d_c_ni.reshape(SSM_GROUPS, SSM_GROUP, SSM_STATE)

    g_w_attn_br = _mm_cols_tn(attn, d_y_attn, ns=D_MODEL // N_CHIPS, name="g_w_attn_br")
    dqkv = [_attention_bwd(g, proj, cos_f, sin_s, d_attn, attn, lse) for g in range(len(DILATIONS))]

    d_proj = jnp.concatenate([dqkv[g][j] for j in range(3) for g in range(len(DILATIONS))] + [d_u, d_gl],
                             axis=1)
    g_w_in = _mm_cols_tn(x, d_proj, ns=IN_WIDTH // N_CHIPS, name="g_w_in")

    def grad_x_after(after):
        dx_proj = _mm_cols_nt(d_proj, wts["w_in"], tm=1024, name="dx_proj", after=after)
        return _reorder_rows(dz1, dx_proj, to_phase=False, name="grad_x", scale=DN_ALPHA)

    big = {"w_in": g_w_in, "w_attn_br": g_w_attn_br, "w_ssm_br": g_w_ssm_br, "w_out": g_w_out, "w_glu": g_w_glu,
           "w_ff_gate": g_w_ff_gate, "w_ff_up": g_w_ff_up, "w_ff_down": g_w_ff_down}
    small_g = {"b_gate": jnp.concatenate([d_bg0, d_bg1], axis=0), "ssm_a_re": d_a_re, "ssm_a_im": d_a_im,
               "ssm_log_dt": d_log_dt, "ssm_b_re": d_b_re, "ssm_b_im": d_b_im, "ssm_c_re": d_c_re, "ssm_c_im": d_c_im,
               "ssm_d": d_ssm_d.reshape(SSM_WIDTH), "ln1_g": d_ln1_g, "ln1_b": d_ln1_b, "ln2_g": d_ln2_g,
               "ln2_b": d_ln2_b}
    marks = {"ln1_bwd": dz1, "scan_bwd": d_abar_r, "attention_bwd_0": dqkv[0][0]}
    return loss_v[0, 0], grad_x_after, big, small_g, marks


GATHER_ID, SWAP_ID, SCATTER_ID, JOIN_ID, EXCHANGE_ID = 1, 2, 3, 4, 5


def _place():
    return lax.axis_index("x"), lax.axis_index("y"), lax.axis_index("c")


def _other_chips(x, y):
    return [(1 - x, y), (x, 1 - y), (1 - x, 1 - y)]


def _handshake(peers):
    barrier = pltpu.get_barrier_semaphore()
    for peer in peers:
        pl.semaphore_signal(barrier, inc=1, device_id=peer, device_id_type=MESH)
    pl.semaphore_wait(barrier, len(peers))


def _sequencer(body, arrays, out_type, sems, collective_id, name):
    return pl.kernel(body, name=name, out_type=out_type,
                     mesh=plsc.ScalarSubcoreMesh(axis_name="sequencer", num_cores=1), scratch_types=sems,
                     compiler_params=pltpu.CompilerParams(collective_id=collective_id))(*arrays)


def _gather_weights(shards, *, name):
    nw = len(shards)

    def body(*refs):
        ins, outs = refs[:nw], refs[nw:2 * nw]
        send_sems, recv_sems, pass_send, pass_recv, local_sems = refs[2 * nw:]
        x, y, c = _place()
        chip = 2 * x + y
        chips = _other_chips(x, y)
        _handshake([(x, y, 1 - c)] + [(cx, cy, c) for cx, cy in chips])
        started = []
        for w in range(nw):
            hw = shards[w].shape[0] // 2
            mine = pl.ds(c * hw, hw)
            own = pltpu.make_async_copy(ins[w], outs[w].at[chip], local_sems.at[w])
            own.start()
            started.append(own)
            for j, (cx, cy) in enumerate(chips):
                cp = pltpu.make_async_remote_copy(
                    src_ref=ins[w].at[mine], dst_ref=outs[w].at[chip, mine], send_sem=send_sems.at[w, j],
                    recv_sem=recv_sems.at[w, j], device_id=(cx, cy, c), device_id_type=MESH)
                cp.start()
                started.append(cp)
        passed = []
        for w in range(nw):
            hw = shards[w].shape[0] // 2
            mine = pl.ds(c * hw, hw)
            for j, (cx, cy) in enumerate(chips):
                landed = outs[w].at[2 * cx + cy, mine]
                pltpu.make_async_remote_copy(
                    src_ref=ins[w].at[mine], dst_ref=landed, send_sem=send_sems.at[w, j],
                    recv_sem=recv_sems.at[w, j], device_id=(cx, cy, c), device_id_type=MESH).wait_recv()
                cp = pltpu.make_async_remote_copy(
                    src_ref=landed, dst_ref=landed, send_sem=pass_send.at[w, j], recv_sem=pass_recv.at[w, j],
                    device_id=(x, y, 1 - c), device_id_type=MESH)
                cp.start()
                passed.append(cp)
        for w in range(nw):
            hw = shards[w].shape[0] // 2
            theirs = pl.ds((1 - c) * hw, hw)
            for j, (cx, cy) in enumerate(chips):
                landed = outs[w].at[2 * cx + cy, theirs]
                pltpu.make_async_remote_copy(
                    src_ref=landed, dst_ref=landed, send_sem=pass_send.at[w, j], recv_sem=pass_recv.at[w, j],
                    device_id=(x, y, 1 - c), device_id_type=MESH).wait_recv()
        for cp in started[0::4]:
            cp.wait()
        for cp in [s for i, s in enumerate(started) if i % 4] + passed:
            cp.wait_send()

    sem = pltpu.SemaphoreType.DMA
    return _sequencer(body, shards, [_sds((N_CHIPS,) + a.shape, a.dtype) for a in shards],
                      [sem((nw, 3)), sem((nw, 3)), sem((nw, 3)), sem((nw, 3)), sem((nw,))], GATHER_ID, name)


def _swap_other_halves(grads, *, name):
    nw = len(grads)

    def body(*refs):
        ins, outs = refs[:nw], refs[nw:2 * nw]
        send_sems, recv_sems = refs[2 * nw:]
        x, y, c = _place()
        _handshake([(x, y, 1 - c)])
        cps = []
        for w in range(nw):
            hw = grads[w].shape[1] // 2
            cp = pltpu.make_async_remote_copy(
                src_ref=ins[w].at[:, pl.ds((1 - c) * hw, hw)], dst_ref=outs[w], send_sem=send_sems.at[w],
                recv_sem=recv_sems.at[w], device_id=(x, y, 1 - c), device_id_type=MESH)
            cp.start()
            cps.append(cp)
        for cp in cps:
            cp.wait()

    sem = pltpu.SemaphoreType.DMA
    return _sequencer(body, grads, [_sds((N_CHIPS, g.shape[1] // 2, g.shape[2]), g.dtype) for g in grads],
                      [sem((nw,)), sem((nw,))], SWAP_ID, name)


def _add_my_halves(core, grads, others, *, name, after=()):
    nw = len(grads)
    halves = [g.shape[1] // 2 for g in grads]

    def body(core_ref, *refs):
        outs = refs[2 * nw + len(after):]
        for g_ref, o_ref, out_ref in zip(refs[:nw], refs[nw:2 * nw], outs):
            out_ref[...] = (g_ref[...].astype(F32) + o_ref[...].astype(F32)).astype(out_ref.dtype)

    in_specs = [pl.BlockSpec((None, None, hw, g.shape[2]), lambda s, core_ref: (s, core_ref[0], 0, 0))
                for g, hw in zip(grads, halves)]
    in_specs += [pl.BlockSpec((None, hw, g.shape[2]), lambda s, core_ref: (s, 0, 0)) for g, hw in zip(grads, halves)]
    return pl.pallas_call(
        body,
        grid_spec=pltpu.PrefetchScalarGridSpec(
            num_scalar_prefetch=1, grid=(N_CHIPS,), in_specs=in_specs + [HBM_OPERAND] * len(after),
            out_specs=[pl.BlockSpec((None, hw, g.shape[2]), lambda s, core_ref: (s, 0, 0))
                       for g, hw in zip(grads, halves)]),
        out_shape=[_sds((N_CHIPS, hw, g.shape[2]), BF16) for g, hw in zip(grads, halves)],
        compiler_params=_cp(("parallel",)), name=name)(
            core, *[g.reshape(N_CHIPS, 2, hw, g.shape[2]) for g, hw in zip(grads, halves)], *others, *after)


def _scatter_partials(parts, *, name):
    nw = len(parts)

    def body(*refs):
        ins, outs = refs[:nw], refs[nw:2 * nw]
        send_sems, recv_sems = refs[2 * nw:]
        x, y, c = _place()
        _handshake([(cx, cy, c) for cx, cy in _other_chips(x, y)])
        cps = []
        for w in range(nw):
            for j, (cx, cy) in enumerate(_other_chips(x, y)):
                cp = pltpu.make_async_remote_copy(
                    src_ref=ins[w].at[2 * cx + cy], dst_ref=outs[w].at[j], send_sem=send_sems.at[w, j],
                    recv_sem=recv_sems.at[w, j], device_id=(cx, cy, c), device_id_type=MESH)
                cp.start()
                cps.append(cp)
        for cp in cps:
            cp.wait()

    sem = pltpu.SemaphoreType.DMA
    return _sequencer(body, parts, [_sds((3,) + p.shape[1:], p.dtype) for p in parts],
                      [sem((nw, 3)), sem((nw, 3))], SCATTER_ID, name)


SUM_STEPS = 2


def _sum_partials(chip, parts, recvd, *, name, after=()):
    nw = len(parts)
    rows = [p.shape[1] // SUM_STEPS for p in parts]

    def body(chip_ref, *refs):
        outs = refs[2 * nw + len(after):]
        for p_ref, r_ref, out_ref in zip(refs[:nw], refs[nw:2 * nw], outs):
            acc = p_ref[...].astype(F32)
            for j in range(3):
                acc = acc + r_ref[j].astype(F32)
            out_ref[...] = acc

    in_specs = [pl.BlockSpec((None, th, p.shape[2]), lambda i, chip_ref: (chip_ref[0], i, 0))
                for p, th in zip(parts, rows)]
    in_specs += [pl.BlockSpec((3, th, p.shape[2]), lambda i, chip_ref: (0, i, 0)) for p, th in zip(parts, rows)]
    return pl.pallas_call(
        body,
        grid_spec=pltpu.PrefetchScalarGridSpec(
            num_scalar_prefetch=1, grid=(SUM_STEPS,), in_specs=in_specs + [HBM_OPERAND] * len(after),
            out_specs=[pl.BlockSpec((th, p.shape[2]), lambda i, chip_ref: (i, 0)) for p, th in zip(parts, rows)]),
        out_shape=[_sds(p.shape[1:]) for p in parts], compiler_params=_cp(("parallel",)), name=name)(
            chip, *parts, *recvd, *after)


def _swap_reduced_halves(halves, *, name):
    nw = len(halves)

    def body(*refs):
        ins, outs = refs[:nw], refs[nw:2 * nw]
        send_sems, recv_sems = refs[2 * nw:]
        x, y, c = _place()
        _handshake([(x, y, 1 - c)])
        cps = []
        for w in range(nw):
            cp = pltpu.make_async_remote_copy(
                src_ref=ins[w], dst_ref=outs[w], send_sem=send_sems.at[w], recv_sem=recv_sems.at[w],
                device_id=(x, y, 1 - c), device_id_type=MESH)
            cp.start()
            cps.append(cp)
        for cp in cps:
            cp.wait()

    sem = pltpu.SemaphoreType.DMA
    return _sequencer(body, halves, [_sds(h.shape, h.dtype) for h in halves], [sem((nw,)), sem((nw,))], JOIN_ID, name)


def _exchange_rows(vec, *, name):
    def body(v_ref, slots, send_sems, recv_sems, local_sem):
        x, y, c = _place()
        me = 4 * x + 2 * y + c
        peers = []
        for mask in range(1, N_DEV):
            peers.append((1 - x if mask & 4 else x, 1 - y if mask & 2 else y, 1 - c if mask & 1 else c))
        _handshake(peers)
        own = pltpu.make_async_copy(v_ref, slots.at[me], local_sem)
        own.start()
        cps = []
        for k, peer in enumerate(peers):
            cp = pltpu.make_async_remote_copy(
                src_ref=v_ref, dst_ref=slots.at[me], send_sem=send_sems.at[k], recv_sem=recv_sems.at[k],
                device_id=peer, device_id_type=MESH)
            cp.start()
            cps.append(cp)
        for k, (px, py, pc) in enumerate(peers):
            pltpu.make_async_remote_copy(
                src_ref=v_ref, dst_ref=slots.at[4 * px + 2 * py + pc], send_sem=send_sems.at[k],
                recv_sem=recv_sems.at[k], device_id=(px, py, pc), device_id_type=MESH).wait_recv()
        for cp in cps:
            cp.wait_send()
        own.wait()

    sem = pltpu.SemaphoreType.DMA
    return _sequencer(body, [vec], [_sds((N_DEV,) + vec.shape)], [sem((N_DEV - 1,)), sem((N_DEV - 1,)), sem(())],
                      EXCHANGE_ID, name)[0]


def _sum_slots(slots, *, name, after=()):
    def body(s_ref, *rest):
        out_ref = rest[len(after)]
        acc = s_ref[0]
        for d in range(1, N_DEV):
            acc = acc + s_ref[d]
        out_ref[...] = acc

    vmem = pl.BlockSpec(memory_space=pltpu.VMEM)
    return pl.pallas_call(
        body, in_specs=[vmem] + [HBM_OPERAND] * len(after), out_specs=vmem, out_shape=_sds(slots.shape[1:]),
        compiler_params=pltpu.CompilerParams(vmem_limit_bytes=VMEM_LIMIT_BYTES), name=name)(slots, *after)


def _reduce_scatter_start(grads, core, *, tag, add_after=()):
    others = _swap_other_halves(grads, name="swap_other_halves_" + tag)
    parts = _add_my_halves(core, grads, others, name="add_my_halves_" + tag, after=add_after)
    return parts, _scatter_partials(parts, name="scatter_partials_" + tag)


def _reduce_scatter_finish(parts, recvd, chip, *, tag, sum_after=()):
    mine = _sum_partials(chip, parts, recvd, name="sum_partials_" + tag, after=sum_after)
    return mine, _swap_reduced_halves(mine, name="swap_reduced_halves_" + tag)


ADAM_BLOCK_ELEMS = 256 * 1024


def _adam_rows(rows, cols):
    tm = rows
    while tm * cols > ADAM_BLOCK_ELEMS and tm % 16 == 0:
        tm //= 2
    return tm


def _adam_step(wv, gv, mv, vv):
    m2 = ADAM_B1 * mv + (1.0 - ADAM_B1) * gv
    v2 = ADAM_B2 * vv + (1.0 - ADAM_B2) * (gv * gv)
    m_hat = m2 / (1.0 - ADAM_B1 ** ADAM_STEP)
    v_hat = v2 / (1.0 - ADAM_B2 ** ADAM_STEP)
    return -ADAM_LR * (m_hat / (jnp.sqrt(v_hat) + ADAM_EPS) + ADAM_WD * wv), m2, v2


def _adamw(w, g, m, v, *, name):
    rows, cols = w.shape
    return _rowwise(_adam_step, [w, g, m, v], [], [_sds((rows, cols))] * 3, tm=_adam_rows(rows, cols), name=name)


def _adamw_halves(core, w, g_mine, g_theirs, m, v, *, name, after=()):
    rows, cols = w.shape
    hw = rows // 2
    tm = _adam_rows(hw, cols)
    per_half = hw // tm

    def body(core_ref, w_ref, gm_ref, gt_ref, m_ref, v_ref, *rest):
        g_out, d_out, m_out, v_out = rest[len(after):]
        mine = (pl.program_id(0) // per_half) == core_ref[0]
        g = jnp.where(mine, gm_ref[...], gt_ref[...])
        d, m2, v2 = _adam_step(w_ref[...], g, m_ref[...], v_ref[...])
        g_out[...] = g
        d_out[...] = d
        m_out[...] = m2
        v_out[...] = v2

    full = pl.BlockSpec((tm, cols), lambda i, core_ref: (i, 0))
    half = pl.BlockSpec((tm, cols), lambda i, core_ref: (i % per_half, 0))
    return pl.pallas_call(
        body,
        grid_spec=pltpu.PrefetchScalarGridSpec(
            num_scalar_prefetch=1, grid=(rows // tm,),
            in_specs=[full, half, half, full, full] + [HBM_OPERAND] * len(after), out_specs=[full, full, full, full]),
        out_shape=[_sds((rows, cols))] * 4, compiler_params=_cp(("parallel",)), name=name)(
            core, w, g_mine, g_theirs, m, v, *after)


HELD_TRANSPOSED = ("w_ff_gate", "w_ff_up")


def _as_rows(name, arr):
    return arr[0].T if name in HELD_TRANSPOSED else arr[0]


def _from_rows(name, arr2d):
    return (arr2d.T if name in HELD_TRANSPOSED else arr2d)[None]


STORED_SWAPPED = ("ssm_b_re", "ssm_b_im")


def _as_stored(name, arr):
    return jnp.swapaxes(arr, -1, -2) if name in STORED_SWAPPED else arr


def _pack_rows(arrs):
    flat = jnp.concatenate([a.reshape(-1).astype(F32) for a in arrs])
    rows = -(-flat.shape[0] // 1024) * 8
    return jnp.pad(flat, (0, rows * 128 - flat.shape[0])).reshape(rows, 128)


def _unpack_rows(vec, shapes):
    flat = vec.reshape(-1)
    out, off = [], 0
    for shp in shapes:
        size = math.prod(shp)
        out.append(flat[off:off + size].reshape(shp))
        off += size
    return out


SMALL = ("b_gate", "ssm_a_re", "ssm_a_im", "ssm_log_dt", "ssm_b_re", "ssm_b_im", "ssm_c_re", "ssm_c_im", "ssm_d",
         "ln1_g", "ln1_b", "ln2_g", "ln2_b")
GATHER_GROUPS = (("w_in", ("w_in",)), ("mixer", ("w_attn_br", "w_ssm_br", "w_glu", "w_out")),
                 ("ffn", ("w_ff_gate", "w_ff_up", "w_ff_down")))
REDUCE_GROUPS = (("ffn", ("w_ff_down", "w_ff_gate", "w_ff_up")),
                 ("mixer", ("w_out", "w_ssm_br", "w_glu", "w_attn_br")), ("w_in", ("w_in",)))
WEIGHTS = ("w_in", "b_gate", "w_attn_br", "w_ssm_br", "w_out", "ssm_a_re", "ssm_a_im", "ssm_log_dt", "ssm_b_re",
           "ssm_b_im", "ssm_c_re", "ssm_c_im", "ssm_d", "w_glu", "ln1_g", "ln1_b", "w_ff_gate", "w_ff_up", "w_ff_down",
           "ln2_g", "ln2_b")


def kernel(x, w_in, b_gate, w_attn_br, w_ssm_br, w_out, ssm_a_re, ssm_a_im, ssm_log_dt, ssm_b_re, ssm_b_im, ssm_c_re, ssm_c_im, ssm_d, w_glu, ln1_g, ln1_b, w_ff_gate, w_ff_up, w_ff_down, ln2_g, ln2_b, loss_target, m_w_in, m_b_gate, m_w_attn_br, m_w_ssm_br, m_w_out, m_ssm_a_re, m_ssm_a_im, m_ssm_log_dt, m_ssm_b_re, m_ssm_b_im, m_ssm_c_re, m_ssm_c_im, m_ssm_d, m_w_glu, m_ln1_g, m_ln1_b, m_w_ff_gate, m_w_ff_up, m_w_ff_down, m_ln2_g, m_ln2_b, v_w_in, v_b_gate, v_w_attn_br, v_w_ssm_br, v_w_out, v_ssm_a_re, v_ssm_a_im, v_ssm_log_dt, v_ssm_b_re, v_ssm_b_im, v_ssm_c_re, v_ssm_c_im, v_ssm_d, v_w_glu, v_ln1_g, v_ln1_b, v_w_ff_gate, v_w_ff_up, v_w_ff_down, v_ln2_g, v_ln2_b):
    given = dict(locals())
    px, py, pc = _place()
    chip = 2 * px + py
    core_s = jnp.reshape(pc, (1,)).astype(jnp.int32)
    chip_s = jnp.reshape(chip, (1,)).astype(jnp.int32)

    wts = {}
    for tag, names in GATHER_GROUPS:
        wts.update(zip(names, _gather_weights([_as_rows(n, given[n]).astype(BF16) for n in names],
                                              name="gather_" + tag)))
    ncol = D_MODEL // N_CHIPS
    bg_mine = jnp.where(pc == 0, b_gate[0], jnp.zeros_like(b_gate[0]))
    bg_full = lax.dynamic_update_slice(jnp.zeros((2, D_MODEL), F32), bg_mine, (0, chip * ncol))
    bg_slots = _exchange_rows(bg_full.reshape(16, 128), name="exchange_gate_bias")
    bg_full = _sum_slots(bg_slots, name="sum_gate_bias").reshape(2, D_MODEL)
    small = {n: given[n][0] for n in SMALL if n.startswith("ssm")}
    small.update({n: given[n] for n in ("ln1_g", "ln1_b", "ln2_g", "ln2_b")})
    small["b_gate"] = bg_full

    loss_mine, grad_x_after, big_g, small_g, marks = _local_step(x[0], loss_target[0], wts, small)

    groups = dict(REDUCE_GROUPS)
    parts, recvd = {}, {}
    grads, delta, new_m, new_v = {}, {}, {}, {}

    def start(tag, add_after):
        parts[tag], recvd[tag] = _reduce_scatter_start([big_g[n] for n in groups[tag]], core_s, tag=tag,
                                                       add_after=add_after)

    def finish(tag, sum_after, adam_after):
        mine, theirs = _reduce_scatter_finish(parts[tag], recvd[tag], chip_s, tag=tag, sum_after=sum_after)
        for n, g_mine, g_theirs in zip(groups[tag], mine, theirs):
            res = _adamw_halves(core_s, _as_rows(n, given[n]), g_mine, g_theirs, _as_rows(n, given["m_" + n]),
                                _as_rows(n, given["v_" + n]), name="adamw_" + n, after=adam_after)
            grads[n], delta[n], new_m[n], new_v[n] = [_from_rows(n, r) for r in res]

    start("ffn", (marks["ln1_bwd"],))
    start("mixer", (marks["scan_bwd"],))
    finish("mixer", (marks["attention_bwd_0"],), (big_g["w_in"],))
    start("w_in", tuple(delta[n] for n in groups["mixer"]))
    in_flight = (parts["w_in"][0],)
    grad_x = grad_x_after(in_flight)
    finish("ffn", (marks["scan_bwd"],), in_flight)
    stored = [_as_stored(n, small_g[n]) for n in SMALL] + [loss_mine.reshape(1)]
    slots = _exchange_rows(_pack_rows(stored), name="exchange_small")
    summed = _unpack_rows(_sum_slots(slots, name="sum_small", after=in_flight), [a.shape for a in stored])
    loss = summed.pop()[0]
    for n, g in zip(SMALL, summed):
        g = _as_stored(n, g)
        if n == "b_gate":
            g = lax.dynamic_slice(g, (0, chip * ncol), (2, ncol))
        grads[n] = g.reshape(given[n].shape)
    packed = [_pack_rows([_as_stored(n, src[n]) for n in SMALL]) for src in
              (given, grads, {n: given["m_" + n] for n in SMALL}, {n: given["v_" + n] for n in SMALL})]
    shapes = [_as_stored(n, given[n]).shape for n in SMALL]
    small_out = _adamw(*packed, name="adamw_small")
    for out, vec in zip((delta, new_m, new_v), small_out):
        out.update((n, _as_stored(n, a)) for n, a in zip(SMALL, _unpack_rows(vec, shapes)))
    behind = [delta[n] for n in groups["ffn"]] + [small_out[0], grad_x]
    finish("w_in", tuple(behind), ())

    return (loss, grad_x.reshape(x.shape), *[grads[n] for n in WEIGHTS], *[delta[n] for n in WEIGHTS],
            *[new_m[n] for n in WEIGHTS], *[new_v[n] for n in WEIGHTS])
```

```python
import math

import jax
import jax.numpy as jnp
from jax import lax
from jax.experimental import pallas as pl
from jax.experimental.pallas import tpu as pltpu
from jax.experimental.pallas import tpu_sc as plsc

F32 = jnp.float32
BF16 = jnp.bfloat16
MESH = pl.DeviceIdType.MESH

D_MODEL = 1024
SEQ = 2048
HEAD_DIM = 64
ATTN_HEADS = 8
DILATIONS = (1, 4, 16)
ATTN_WIDTH = ATTN_HEADS * HEAD_DIM
QKV_WIDTH = 3 * ATTN_WIDTH
BLOCK = 128
ROPE_THETA = 10000.0
NEG_INF = -1e30
SSM_GROUP = 16
SSM_GROUPS = 32
SSM_WIDTH = 512
SSM_STATE = 64
SSM_LANES = SSM_GROUPS * SSM_STATE
SCAN_CHUNKS = 8
SCAN_STEPS = SEQ // SCAN_CHUNKS
IN_WIDTH = 3 * QKV_WIDTH + SSM_WIDTH + 2 * D_MODEL
D_FF = 2816
N_CHIPS = 4
N_DEV = 8
DN_ALPHA = 2.0 ** 0.25
LN_EPS = 1e-5
ADAM_LR = 0.001
ADAM_B1 = 0.9
ADAM_B2 = 0.999
ADAM_EPS = 1e-08
ADAM_WD = 0.01
ADAM_STEP = 10
GELU_C = math.sqrt(2.0 / math.pi)
GELU_K = 0.044715

VMEM_LIMIT_BYTES = 56 * 1024 * 1024


def _sds(shape, dtype=F32):
    return jax.ShapeDtypeStruct(tuple(shape), dtype)


def _cp(semantics=None):
    return pltpu.CompilerParams(dimension_semantics=semantics, vmem_limit_bytes=VMEM_LIMIT_BYTES)


HBM_OPERAND = pl.BlockSpec(memory_space=pl.ANY)


def _matmul(a, b, *, grid, a_spec, b_spec, o_spec, out_shape, dims, k_axis=None, name, after=()):
    nk = grid[k_axis] if k_axis is not None else 1
    o_block = tuple(d for d in o_spec.block_shape if d is not None)
    n_after = len(after)

    def body(a_ref, b_ref, *rest):
        o_ref, acc = rest[n_after], rest[n_after + 1:]
        part = lax.dot_general(a_ref[...].astype(BF16), b_ref[...].astype(BF16),
                               (((dims[0],), (dims[1],)), ((), ())), preferred_element_type=F32)
        if k_axis is None:
            o_ref[...] = part.astype(o_ref.dtype)
        else:
            k = pl.program_id(k_axis)

            @pl.when(k == 0)
            def _():
                acc[0][...] = part

            @pl.when(k > 0)
            def _():
                acc[0][...] += part

            @pl.when(k == nk - 1)
            def _():
                o_ref[...] = acc[0][...].astype(o_ref.dtype)

    sem = tuple("arbitrary" if ax == k_axis else "parallel" for ax in range(len(grid)))
    return pl.pallas_call(
        body, grid=grid, in_specs=[a_spec, b_spec] + [HBM_OPERAND] * n_after, out_specs=o_spec, out_shape=out_shape,
        scratch_shapes=[pltpu.VMEM(o_block, F32)] if k_axis is not None else [],
        compiler_params=_cp(sem), name=name)(a, b, *after)


def _mm_cols(a, wg, *, tm, name, out_dtype=F32):
    m, k = a.shape
    ns = wg.shape[2]
    return _matmul(a, wg, grid=(m // tm, N_CHIPS),
                   a_spec=pl.BlockSpec((tm, k), lambda i, s: (i, 0)),
                   b_spec=pl.BlockSpec((None, k, ns), lambda i, s: (s, 0, 0)),
                   o_spec=pl.BlockSpec((tm, ns), lambda i, s: (i, s)),
                   out_shape=_sds((m, N_CHIPS * ns), out_dtype), dims=(1, 0), name=name)


def _mm_cols_nt(dy, wg, *, tm, name, out_dtype=F32, after=()):
    k, ns = wg.shape[1], wg.shape[2]
    m = dy.shape[0]
    a_spec = pl.BlockSpec((tm, ns), lambda i, s: (i, s))
    return _matmul(dy, wg, grid=(m // tm, N_CHIPS), a_spec=a_spec,
                   b_spec=pl.BlockSpec((None, k, ns), lambda i, s: (s, 0, 0)),
                   o_spec=pl.BlockSpec((tm, k), lambda i, s: (i, 0)),
                   out_shape=_sds((m, k), out_dtype), dims=(1, 1), k_axis=1, name=name, after=after)


def _mm_cols_tn(a, dy, *, ns, name, after=()):
    m, k = a.shape
    return _matmul(a, dy, grid=(N_CHIPS,), a_spec=pl.BlockSpec((m, k), lambda s: (0, 0)),
                   b_spec=pl.BlockSpec((m, ns), lambda s: (0, s)),
                   o_spec=pl.BlockSpec((None, k, ns), lambda s: (s, 0, 0)),
                   out_shape=_sds((N_CHIPS, k, ns), BF16), dims=(0, 0), name=name, after=after)


def _mm_plain(a, b, *, tm, tn, name, out_dtype=F32, dims=(1, 0), tk=None):
    m = a.shape[1 - dims[0]]
    kk = a.shape[dims[0]]
    n = b.shape[1 - dims[1]]
    tk = kk if tk is None else tk
    nk = kk // tk

    def a_idx(i, j, k):
        return (i, k) if dims[0] == 1 else (k, i)

    def b_idx(i, j, k):
        return (k, j) if dims[1] == 0 else (j, k)

    a_blk = (tm, tk) if dims[0] == 1 else (tk, tm)
    b_blk = (tk, tn) if dims[1] == 0 else (tn, tk)
    return _matmul(a, b, grid=(m // tm, n // tn, nk),
                   a_spec=pl.BlockSpec(a_blk, a_idx), b_spec=pl.BlockSpec(b_blk, b_idx),
                   o_spec=pl.BlockSpec((tm, tn), lambda i, j, k: (i, j)),
                   out_shape=_sds((m, n), out_dtype), dims=dims, k_axis=2 if nk > 1 else None, name=name)


def _rowwise(fn, tiled, full, outs, accs=(), *, tm, name, after=()):
    args, in_specs = [], []
    for t in tiled:
        if isinstance(t, tuple):
            arr, w, cb = t
            in_specs.append(pl.BlockSpec((tm, w), lambda i, cb=cb: (i, cb)))
        else:
            arr = t
            in_specs.append(pl.BlockSpec((tm, arr.shape[1]), lambda i: (i, 0)))
        args.append(arr)
    rows = args[0].shape[0]
    for f in full:
        in_specs.append(pl.BlockSpec(f.shape, lambda i, nd=f.ndim: (0,) * nd))
        args.append(f)
    out_specs = [pl.BlockSpec((tm, o.shape[1]), lambda i: (i, 0)) for o in outs]
    out_specs += [pl.BlockSpec(a.shape, lambda i, nd=len(a.shape): (0,) * nd) for a in accs]
    n_in, n_out = len(args), len(outs)
    in_specs += [HBM_OPERAND] * len(after)
    first_out = n_in + len(after)

    def body(*refs):
        res = fn(*[r[...] for r in refs[:n_in]])
        res = res if isinstance(res, (tuple, list)) else (res,)
        for r, v in zip(refs[first_out:first_out + n_out], res[:n_out]):
            r[...] = v.astype(r.dtype)
        i = pl.program_id(0)
        for r, v in zip(refs[first_out + n_out:], res[n_out:]):
            @pl.when(i == 0)
            def _(r=r, v=v):
                r[...] = v

            @pl.when(i > 0)
            def _(r=r, v=v):
                r[...] += v

    res = pl.pallas_call(
        body, grid=(rows // tm,), in_specs=in_specs, out_specs=out_specs, out_shape=list(outs) + list(accs),
        compiler_params=_cp(("arbitrary",) if accs else ("parallel",)), name=name)(*args, *after)
    return res


def _colsum(v):
    return jnp.sum(v, axis=0, keepdims=True)


def _ln_stats(z):
    mu = jnp.mean(z, axis=-1, keepdims=True)
    zc = z - mu
    var = jnp.mean(zc * zc, axis=-1, keepdims=True)
    rstd = lax.rsqrt(var + LN_EPS)
    return zc * rstd, rstd


def _ln_bwd(dy, xhat, rstd, g):
    dxh = dy * g
    m1 = jnp.mean(dxh, axis=-1, keepdims=True)
    m2 = jnp.mean(dxh * xhat, axis=-1, keepdims=True)
    return rstd * (dxh - m1 - xhat * m2)


def _swap_halves(t):
    w = t.shape[-1]
    lane = lax.broadcasted_iota(jnp.int32, t.shape, t.ndim - 1)
    return jnp.where((lane % HEAD_DIM) < HEAD_DIM // 2, pltpu.roll(t, w - HEAD_DIM // 2, t.ndim - 1),
                     pltpu.roll(t, HEAD_DIM // 2, t.ndim - 1))


PHASES = max(DILATIONS)
PAIR = 2 * HEAD_DIM
UNITS = SEQ // BLOCK
UNIT_BATCH = 8
ROPE_ROWS = 256


def _to_phase_rows(t):
    return t.reshape(SEQ // PHASES, PHASES, t.shape[1]).transpose(1, 0, 2).reshape(t.shape)


def _reorder_rows(arr, plus=None, *, to_phase, name, scale=1.0):
    def body(*refs):
        o_ref = refs[-1]
        for rho in range(PHASES):
            phase = pl.ds(rho * BLOCK, BLOCK)
            strided = pl.ds(rho, BLOCK, stride=PHASES)
            src, dst = (strided, phase) if to_phase else (phase, strided)
            val = refs[0][src, :]
            if scale != 1.0:
                val = val * scale
            if plus is not None:
                val = val + refs[1][src, :]
            o_ref[dst, :] = val

    spec = pl.BlockSpec((SEQ, BLOCK), lambda j: (0, j))
    ins = [arr] if plus is None else [arr, plus]
    return pl.pallas_call(body, grid=(arr.shape[1] // BLOCK,), in_specs=[spec] * len(ins), out_specs=spec,
                          out_shape=_sds(arr.shape), compiler_params=_cp(("parallel",)), name=name)(*ins)


def _rope(t, cf, ss):
    return t * cf + _swap_halves(t) * ss


def _rope_transposed(d, cf, ss):
    return d * cf + _swap_halves(d * ss)


def _unit_pieces(u, dil):
    pieces, length = PHASES // dil, 8 * dil
    if dil == 1:
        rho, i = 0, u
    elif dil == PHASES:
        rho, i = u, 0
    else:
        rho, i = jnp.bitwise_and(u, dil - 1), jnp.right_shift(u, dil.bit_length() - 1)
    before = jnp.maximum(i - 1, 0)
    cur = [pl.multiple_of((rho + dil * k) * BLOCK + length * i, 8) for k in range(pieces)]
    prev = [pl.multiple_of((rho + dil * k) * BLOCK + length * before, 8) for k in range(pieces)]
    return i, cur, prev


def _load_tile(ref, starts, dil):
    return jnp.concatenate([ref[pl.ds(st, 8 * dil), :] for st in starts], axis=0)


def _store_tile(ref, starts, dil, val, head=None, accumulate=False):
    length = 8 * dil
    lanes = slice(None) if head is None else pl.ds(head * HEAD_DIM, HEAD_DIM)
    cols = slice(None) if head is None else slice(head * HEAD_DIM, (head + 1) * HEAD_DIM)
    for k, st in enumerate(starts):
        piece = val[k * length:(k + 1) * length, cols]
        if accumulate:
            ref[pl.ds(st, length), lanes] += piece
        else:
            ref[pl.ds(st, length), lanes] = piece


def _tile_position(idx, dil):
    pieces, length = PHASES // dil, 8 * dil
    return pieces * jnp.bitwise_and(idx, length - 1) + jnp.right_shift(idx, length.bit_length() - 1)


def _band_mask(i, dil):
    row = lax.broadcasted_iota(jnp.int32, (BLOCK, 2 * BLOCK), 0)
    col = lax.broadcasted_iota(jnp.int32, (BLOCK, 2 * BLOCK), 1)
    key_pos = _tile_position(jnp.bitwise_and(col, BLOCK - 1), dil) + jnp.where(col >= BLOCK, 0, -BLOCK)
    dist = _tile_position(row, dil) - key_pos
    return (dist >= 0) & (dist <= BLOCK) & ((col >= BLOCK) | (i > 0))


def _causal_mask():
    row = lax.broadcasted_iota(jnp.int32, (BLOCK, BLOCK), 0)
    col = lax.broadcasted_iota(jnp.int32, (BLOCK, BLOCK), 1)
    return row >= col


def _pair_views(col0):
    return [pl.BlockSpec((SEQ, PAIR), lambda hp, g=g: (0, col0 // PAIR + g * (ATTN_WIDTH // PAIR) + hp))
            for g in range(len(DILATIONS))]


def _project_in(x, wg, cos_f, sin_s):
    ns = wg.shape[2]
    tiles = ns // PAIR

    def body(x_ref, w_ref, cf_ref, ss_ref, o_ref):
        shard = pl.program_id(1)
        acc = jnp.dot(x_ref[...].astype(BF16), w_ref[...], preferred_element_type=F32)
        cf, ss = cf_ref[...], ss_ref[...]

        def write(rotated, scaled):
            for t in range(tiles):
                lanes = slice(t * PAIR, (t + 1) * PAIR)
                val = acc[:, lanes]
                if t < rotated:
                    val = _rope(val, cf, ss)
                    if t < scaled:
                        val = val * (1.0 / math.sqrt(HEAD_DIM))
                o_ref[:, lanes] = val

        for s in range(N_CHIPS):
            rotated = min(max(2 * QKV_WIDTH - s * ns, 0), ns) // PAIR
            scaled = min(max(QKV_WIDTH - s * ns, 0), ns) // PAIR

            @pl.when(shard == s)
            def _(rotated=rotated, scaled=scaled):
                write(rotated, scaled)

    table = pl.BlockSpec((FF_ROWS, PAIR), lambda i, s: (i, 0))
    return pl.pallas_call(
        body, grid=(SEQ // FF_ROWS, N_CHIPS),
        in_specs=[pl.BlockSpec((FF_ROWS, D_MODEL), lambda i, s: (i, 0)),
                  pl.BlockSpec((None, D_MODEL, ns), lambda i, s: (s, 0, 0)), table, table],
        out_specs=pl.BlockSpec((FF_ROWS, ns), lambda i, s: (i, s)), out_shape=_sds((SEQ, N_CHIPS * ns)),
        compiler_params=_cp(("parallel", "parallel")), name="project_in")(x, wg, cos_f, sin_s)


def _attention_fwd(proj):
    ng = len(DILATIONS)

    def body(*refs):
        q_refs, k_refs, v_refs = refs[:ng], refs[ng:2 * ng], refs[2 * ng:3 * ng]
        attn_ref, lse_ref = refs[3 * ng:]
        qr_refs, kr_refs = q_refs, k_refs
        first = lax.broadcasted_iota(jnp.int32, (BLOCK, PAIR), 1) < HEAD_DIM
        for g, dil in enumerate(DILATIONS):
            two_blocks = SEQ // dil > BLOCK

            def units(t, carry, g=g, dil=dil, two_blocks=two_blocks):
                picked = [_unit_pieces(t * UNIT_BATCH + j, dil) for j in range(UNIT_BATCH)]

                def tiles(ref, with_prev=False):
                    if with_prev and two_blocks:
                        return jnp.stack([jnp.concatenate([_load_tile(ref, prev, dil), _load_tile(ref, rows, dil)],
                                                          axis=0) for _, rows, prev in picked])
                    return jnp.stack([_load_tile(ref, rows, dil) for _, rows, _ in picked])

                qq = tiles(qr_refs[g]).astype(BF16)
                kk = tiles(kr_refs[g], True).astype(BF16)
                vv = tiles(v_refs[g], True).astype(BF16)
                if two_blocks:
                    valid = jnp.stack([_band_mask(i, dil) for i, _, _ in picked])
                else:
                    valid = _causal_mask()[None]
                mine = first[None]
                zero = jnp.zeros_like(qq)
                outs, lses = [], []
                for qh in (jnp.where(mine, qq, zero), jnp.where(mine, zero, qq)):
                    s = jnp.einsum("pqd,pkd->pqk", qh, kk, preferred_element_type=F32)
                    s = jnp.where(valid, s, NEG_INF)
                    m = jnp.max(s, axis=-1, keepdims=True)
                    p = jnp.exp(s - m)
                    l = jnp.sum(p, axis=-1, keepdims=True)
                    outs.append(jnp.einsum("pqk,pkd->pqd", p.astype(BF16), vv, preferred_element_type=F32) * (1.0 / l))
                    lses.append(m + jnp.log(l))
                o = jnp.where(mine, outs[0], outs[1])
                lse = jnp.where(mine, lses[0], lses[1])
                if g > 0:
                    lse_old = tiles(lse_ref)
                    m = jnp.maximum(lse_old, lse)
                    lse_new = m + jnp.log(jnp.exp(lse_old - m) + jnp.exp(lse - m))
                    o = tiles(attn_ref) * jnp.exp(lse_old - lse_new) + o * jnp.exp(lse - lse_new)
                    lse = lse_new
                for j, (_, rows, _) in enumerate(picked):
                    _store_tile(attn_ref, rows, dil, o[j])
                    _store_tile(lse_ref, rows, dil, lse[j])
                return carry

            lax.fori_loop(0, UNITS // UNIT_BATCH, units, 0)

    out = pl.BlockSpec((SEQ, PAIR), lambda hp: (0, hp))
    return pl.pallas_call(
        body, grid=(ATTN_WIDTH // PAIR,),
        in_specs=_pair_views(0) + _pair_views(QKV_WIDTH) + _pair_views(2 * QKV_WIDTH),
        out_specs=[out, out], out_shape=[_sds((SEQ, ATTN_WIDTH)), _sds((SEQ, ATTN_WIDTH))],
        compiler_params=_cp(("parallel",)), name="attention_fwd")(*([proj] * (3 * ng)))


def _attention_bwd(g, proj, cos_f, sin_s, d_attn, attn, lse):
    dil = DILATIONS[g]
    two_blocks = SEQ // dil > BLOCK

    def body(qr_ref, kr_ref, v_ref, cf_ref, ss_ref, do_ref, o_ref, lse_ref, dq_out, dk_out, dv_out,
             dq_acc, dk_acc, dv_acc):
        dk_acc[...] = jnp.zeros_like(dk_acc)
        dv_acc[...] = jnp.zeros_like(dv_acc)
        nk = 2 * BLOCK if two_blocks else BLOCK
        first = lax.broadcasted_iota(jnp.int32, (BLOCK, PAIR), 1) < HEAD_DIM
        first_k = lax.broadcasted_iota(jnp.int32, (nk, PAIR), 1) < HEAD_DIM

        def units(t, carry):
            picked = [_unit_pieces(t * UNIT_BATCH + j, dil) for j in range(UNIT_BATCH)]

            def tiles(ref, with_prev=False):
                if with_prev and two_blocks:
                    return jnp.stack([jnp.concatenate([_load_tile(ref, prev, dil), _load_tile(ref, rows, dil)], axis=0)
                                      for _, rows, prev in picked])
                return jnp.stack([_load_tile(ref, rows, dil) for _, rows, _ in picked])

            qq = tiles(qr_ref).astype(BF16)
            kk = tiles(kr_ref, True).astype(BF16)
            vv = tiles(v_ref, True).astype(BF16)
            dof = tiles(do_ref)
            dd = dof * tiles(o_ref)
            lse3 = tiles(lse_ref)
            dob = dof.astype(BF16)
            if two_blocks:
                valid = jnp.stack([_band_mask(i, dil) for i, _, _ in picked])
            else:
                valid = _causal_mask()[None]
            zq, zf = jnp.zeros_like(qq), jnp.zeros_like(dd)
            dqs, dks, dvs = [], [], []
            for head in range(2):
                mine = first[None] if head == 0 else jnp.logical_not(first)[None]
                delta = jnp.sum(jnp.where(mine, dd, zf), axis=-1, keepdims=True)
                lse_h = lse3[:, :, head * HEAD_DIM:head * HEAD_DIM + 1]
                s = jnp.einsum("pqd,pkd->pqk", jnp.where(mine, qq, zq), kk, preferred_element_type=F32)
                p = jnp.where(valid, jnp.exp(s - lse_h), 0.0)
                dp = jnp.einsum("pqd,pkd->pqk", jnp.where(mine, dob, zq), vv, preferred_element_type=F32)
                ds = (p * (dp - delta)).astype(BF16)
                dqs.append(jnp.einsum("pqk,pkd->pqd", ds, kk, preferred_element_type=F32))
                dks.append(jnp.einsum("pqk,pqd->pkd", ds, qq, preferred_element_type=F32))
                dvs.append(jnp.einsum("pqk,pqd->pkd", p.astype(BF16), dob, preferred_element_type=F32))
            dq = jnp.where(first[None], dqs[0], dqs[1])
            dk = jnp.where(first_k[None], dks[0], dks[1])
            dv = jnp.where(first_k[None], dvs[0], dvs[1])
            for j, (_, rows, prev) in enumerate(picked):
                _store_tile(dq_acc, rows, dil, dq[j])
                _store_tile(dk_acc, rows, dil, dk[j, nk - BLOCK:], accumulate=True)
                _store_tile(dv_acc, rows, dil, dv[j, nk - BLOCK:], accumulate=True)
                if two_blocks:
                    _store_tile(dk_acc, prev, dil, dk[j, :BLOCK], accumulate=True)
                    _store_tile(dv_acc, prev, dil, dv[j, :BLOCK], accumulate=True)
            return carry

        lax.fori_loop(0, UNITS // UNIT_BATCH, units, 0)

        def finish(t, carry):
            rows = pl.ds(pl.multiple_of(t * ROPE_ROWS, ROPE_ROWS), ROPE_ROWS)
            cf, ss = cf_ref[rows, :], ss_ref[rows, :]
            dq = dq_acc[rows, :] * (1.0 / math.sqrt(HEAD_DIM))
            dq_out[rows, :] = _rope_transposed(dq, cf, ss).astype(BF16)
            dk_out[rows, :] = _rope_transposed(dk_acc[rows, :], cf, ss).astype(BF16)
            dv_out[rows, :] = dv_acc[rows, :].astype(BF16)
            return carry

        lax.fori_loop(0, SEQ // ROPE_ROWS, finish, 0)

    whole = pl.BlockSpec((SEQ, PAIR), lambda hp: (0, 0))
    pair = pl.BlockSpec((SEQ, PAIR), lambda hp: (0, hp))
    views = [_pair_views(col0)[g] for col0 in (0, QKV_WIDTH, 2 * QKV_WIDTH)]
    return pl.pallas_call(
        body, grid=(ATTN_WIDTH // PAIR,), in_specs=views + [whole, whole, pair, pair, pair],
        out_specs=[pair, pair, pair], out_shape=[_sds((SEQ, ATTN_WIDTH), BF16)] * 3,
        scratch_shapes=[pltpu.VMEM((SEQ, PAIR), F32)] * 3,
        compiler_params=_cp(("parallel",)), name=f"attention_bwd_{g}")(proj, proj, proj, cos_f, sin_s, d_attn, attn, lse)


def _cmul(ar, ai, br, bi):
    return ar * br - ai * bi, ar * bi + ai * br


def _pow256(ar, ai):
    for _ in range(8):
        ar, ai = _cmul(ar, ai, ar, ai)
    return ar, ai


def _chunk_carries(first_r, first_i, pr, pi, reverse):
    rows = lax.broadcasted_iota(jnp.int32, first_r.shape, 0)
    out_r = jnp.zeros_like(first_r)
    out_i = jnp.zeros_like(first_i)
    hr = jnp.zeros_like(first_r[0:1])
    hi = jnp.zeros_like(hr)
    order = range(SCAN_CHUNKS - 1, -1, -1) if reverse else range(SCAN_CHUNKS)
    for c in order:
        out_r = jnp.where(rows == c, hr, out_r)
        out_i = jnp.where(rows == c, hi, out_i)
        tr, ti = _cmul(pr[0:1], pi[0:1], hr, hi)
        hr = first_r[c:c + 1] + tr
        hi = first_i[c:c + 1] + ti
    return out_r, out_i


def _tile(j):
    return pl.ds(pl.multiple_of(j * SCAN_CHUNKS, SCAN_CHUNKS), SCAN_CHUNKS)


def _to_scan_rows(t):
    per = SCAN_STEPS // PHASES
    return t.reshape(PHASES, SCAN_CHUNKS, per, t.shape[1]).transpose(2, 0, 1, 3).reshape(t.shape)


def _from_scan_rows(t):
    per = SCAN_STEPS // PHASES
    return t.reshape(per, PHASES, SCAN_CHUNKS, t.shape[1]).transpose(1, 2, 0, 3).reshape(t.shape)


def _scan_in_place(hr_ref, hi_ref, a_r, a_i):
    def local(j, carry):
        tr, ti = _cmul(a_r, a_i, carry[0], carry[1])
        nr = tr + hr_ref[_tile(j), :]
        ni = ti + hi_ref[_tile(j), :]
        hr_ref[_tile(j), :] = nr
        hi_ref[_tile(j), :] = ni
        return nr, ni

    zero = jnp.zeros_like(a_r)
    last_r, last_i = lax.fori_loop(0, SCAN_STEPS, local, (zero, zero), unroll=4)
    pr, pi = _pow256(a_r, a_i)
    er, ei = _chunk_carries(last_r, last_i, pr, pi, reverse=False)

    def fix(j, carry):
        tr, ti = _cmul(carry[0], carry[1], er, ei)
        hr_ref[_tile(j), :] += tr
        hi_ref[_tile(j), :] += ti
        return _cmul(carry[0], carry[1], a_r, a_i)

    lax.fori_loop(0, SCAN_STEPS, fix, (a_r, a_i), unroll=4)
    return er, ei


def _reverse_scan_in_place(lr_ref, li_ref, hr_ref, hi_ref, er, ei, a_r, a_i):
    def local(t, carry):
        j = SCAN_STEPS - 1 - t
        tr, ti = _cmul(a_r, a_i, carry[0], carry[1])
        nr = tr + lr_ref[_tile(j), :]
        ni = ti + li_ref[_tile(j), :]
        lr_ref[_tile(j), :] = nr
        li_ref[_tile(j), :] = ni
        return nr, ni

    zero = jnp.zeros_like(a_r)
    first_r, first_i = lax.fori_loop(0, SCAN_STEPS, local, (zero, zero), unroll=4)
    pr, pi = _pow256(a_r, a_i)
    nxt_r, nxt_i = _chunk_carries(first_r, first_i, pr, pi, reverse=True)

    def accumulate(lam_r, lam_i, hp_r, hp_i, acc):
        return (acc[0] + lam_r * hp_r + lam_i * hp_i, acc[1] + lam_i * hp_r - lam_r * hp_i)

    def fix(t, carry):
        qr, qi, acc_r, acc_i = carry
        j = SCAN_STEPS - 1 - t
        tr, ti = _cmul(qr, qi, nxt_r, nxt_i)
        lam_r = lr_ref[_tile(j), :] + tr
        lam_i = li_ref[_tile(j), :] + ti
        lr_ref[_tile(j), :] = lam_r
        li_ref[_tile(j), :] = lam_i
        acc_r, acc_i = accumulate(lam_r, lam_i, hr_ref[_tile(j - 1), :], hi_ref[_tile(j - 1), :], (acc_r, acc_i))
        qr, qi = _cmul(qr, qi, a_r, a_i)
        return qr, qi, acc_r, acc_i

    qr, qi, acc_r, acc_i = lax.fori_loop(0, SCAN_STEPS - 1, fix, (a_r, a_i, zero, zero), unroll=4)
    tr, ti = _cmul(qr, qi, nxt_r, nxt_i)
    lam_r = lr_ref[_tile(0), :] + tr
    lam_i = li_ref[_tile(0), :] + ti
    lr_ref[_tile(0), :] = lam_r
    li_ref[_tile(0), :] = lam_i
    acc_r, acc_i = accumulate(lam_r, lam_i, er, ei, (acc_r, acc_i))
    return jnp.sum(acc_r, axis=0, keepdims=True), jnp.sum(acc_i, axis=0, keepdims=True)


def _rope_tables():
    half = HEAD_DIM // 2
    inv_freq = ROPE_THETA ** (-jnp.arange(half, dtype=F32) / half)
    ang = jnp.arange(SEQ, dtype=F32)[:, None] * inv_freq[None, :]
    cos, sin = jnp.cos(ang), jnp.sin(ang)
    cos_f = jnp.concatenate([cos, cos, cos, cos], axis=1)
    sin_s = jnp.concatenate([-sin, sin, -sin, sin], axis=1)
    return cos_f, sin_s


def _ssm_discretise(a_re, a_im, log_dt, b_re, b_im):
    lam = lax.complex(a_re, a_im)
    dt = jnp.exp(log_dt)[:, None]
    a_bar = jnp.exp(lam * dt)
    b_bar = ((a_bar - 1.0) / lam)[..., None] * lax.complex(b_re, b_im)
    return a_bar.real, a_bar.imag, b_bar.real, b_bar.imag


SSM_SLABS = 4
SLAB_GROUPS = SSM_GROUPS // SSM_SLABS
SLAB_IN = SSM_WIDTH // SSM_SLABS
SLAB_STATE = SSM_LANES // SSM_SLABS


def _slab_block_diag(blocks):
    _, r, c = blocks.shape
    eye = jnp.eye(SLAB_GROUPS, dtype=blocks.dtype)
    b5 = blocks.reshape(SSM_SLABS, SLAB_GROUPS, r, 1, c) * eye[None, :, None, :, None]
    return b5.reshape(SSM_SLABS, SLAB_GROUPS * r, SLAB_GROUPS * c)


def _diag_blocks(a, b):
    ra, cb = a.shape[1], b.shape[1]
    wa, wb = ra // SLAB_GROUPS, cb // SLAB_GROUPS
    d = lax.dot_general(a, b, (((0,), (0,)), ((), ())), preferred_element_type=F32)
    row_g = jnp.right_shift(lax.broadcasted_iota(jnp.int32, (ra, cb), 0), wa.bit_length() - 1)
    col_g = jnp.right_shift(lax.broadcasted_iota(jnp.int32, (ra, cb), 1), wb.bit_length() - 1)
    d = jnp.where(row_g == col_g, d, 0.0)
    fold = (jnp.bitwise_and(lax.broadcasted_iota(jnp.int32, (cb, wb), 0), wb - 1)
            == lax.broadcasted_iota(jnp.int32, (cb, wb), 1)).astype(F32)
    return jnp.dot(d, fold, preferred_element_type=F32, precision=lax.Precision.HIGHEST)


def _slab_specs():
    tok = pl.BlockSpec((SEQ, SLAB_IN), lambda j: (0, j))
    state = pl.BlockSpec((SEQ, SLAB_STATE), lambda j: (0, j))
    b_in = pl.BlockSpec((None, SLAB_IN, SLAB_STATE), lambda j: (j, 0, 0))
    c_out = pl.BlockSpec((None, SLAB_STATE, SLAB_IN), lambda j: (j, 0, 0))
    vec = pl.BlockSpec((1, SLAB_STATE), lambda j: (0, j))
    ent = pl.BlockSpec((SCAN_CHUNKS, SLAB_STATE), lambda j: (0, j))
    return tok, state, b_in, c_out, vec, ent


def _ssm_forward(u, b_in_r, b_in_i, c_out_r, c_out_ni, a_r, a_i):
    def body(u_ref, br_ref, bi_ref, cr_ref, ci_ref, ar_ref, ai_ref, y_ref, hr_ref, hi_ref, er_ref, ei_ref):
        uu = u_ref[...]
        hr_ref[...] = jnp.dot(uu, br_ref[...], preferred_element_type=F32)
        hi_ref[...] = jnp.dot(uu, bi_ref[...], preferred_element_type=F32)
        a_re = jnp.broadcast_to(ar_ref[...], (SCAN_CHUNKS, SLAB_STATE))
        a_im = jnp.broadcast_to(ai_ref[...], (SCAN_CHUNKS, SLAB_STATE))
        er_ref[...], ei_ref[...] = _scan_in_place(hr_ref, hi_ref, a_re, a_im)
        y_ref[...] = (jnp.dot(hr_ref[...].astype(BF16), cr_ref[...], preferred_element_type=F32)
                      + jnp.dot(hi_ref[...].astype(BF16), ci_ref[...], preferred_element_type=F32))

    tok, state, b_in, c_out, vec, ent = _slab_specs()
    return pl.pallas_call(
        body, grid=(SSM_SLABS,), in_specs=[tok, b_in, b_in, c_out, c_out, vec, vec],
        out_specs=[tok, state, state, ent, ent],
        out_shape=[_sds((SEQ, SSM_WIDTH)), _sds((SEQ, SSM_LANES)), _sds((SEQ, SSM_LANES)),
                   _sds((SCAN_CHUNKS, SSM_LANES)), _sds((SCAN_CHUNKS, SSM_LANES))],
        compiler_params=_cp(("parallel",)), name="ssm_forward")(u, b_in_r, b_in_i, c_out_r, c_out_ni, a_r, a_i)


def _ssm_backward(d_y, d_u_skip, u, h_r, h_i, e_r, e_i, b_in_r, b_in_i, c_out_r, c_out_ni, a_r, a_i):
    def body(dy_ref, skip_ref, u_ref, hr_ref, hi_ref, er_ref, ei_ref, br_ref, bi_ref, cr_ref, ci_ref, ar_ref, ai_ref,
             du_ref, dar_ref, dai_ref, dcr_ref, dci_ref, dbr_ref, dbi_ref, lr_ref, li_ref):
        dy = dy_ref[...]
        lr_ref[...] = _dot_nt(dy, cr_ref[...])
        li_ref[...] = _dot_nt(dy, ci_ref[...])
        a_re = jnp.broadcast_to(ar_ref[...], (SCAN_CHUNKS, SLAB_STATE))
        a_im = -jnp.broadcast_to(ai_ref[...], (SCAN_CHUNKS, SLAB_STATE))
        dar_ref[...], dai_ref[...] = _reverse_scan_in_place(lr_ref, li_ref, hr_ref, hi_ref, er_ref[...], ei_ref[...],
                                                            a_re, a_im)
        dcr_ref[...] = _diag_blocks(dy, hr_ref[...].astype(BF16))
        dci_ref[...] = _diag_blocks(dy, hi_ref[...].astype(BF16))
        lam_r, lam_i = lr_ref[...].astype(BF16), li_ref[...].astype(BF16)
        uu = u_ref[...]
        dbr_ref[...] = _diag_blocks(uu, lam_r)
        dbi_ref[...] = _diag_blocks(uu, lam_i)
        du = skip_ref[...] + _dot_nt(lam_r, br_ref[...]) + _dot_nt(lam_i, bi_ref[...])
        du_ref[...] = du.astype(BF16)

    tok, state, b_in, c_out, vec, ent = _slab_specs()
    db = pl.BlockSpec((SLAB_IN, SSM_STATE), lambda j: (j, 0))
    return pl.pallas_call(
        body, grid=(SSM_SLABS,), in_specs=[tok, tok, tok, state, state, ent, ent, b_in, b_in, c_out, c_out, vec, vec],
        out_specs=[tok, vec, vec, db, db, db, db],
        out_shape=[_sds((SEQ, SSM_WIDTH), BF16), _sds((1, SSM_LANES)), _sds((1, SSM_LANES))]
        + [_sds((SSM_WIDTH, SSM_STATE))] * 4,
        scratch_shapes=[pltpu.VMEM((SEQ, SLAB_STATE), F32)] * 2,
        compiler_params=_cp(("parallel",)), name="ssm_backward")(
            d_y, d_u_skip, u, h_r, h_i, e_r, e_i, b_in_r, b_in_i, c_out_r, c_out_ni, a_r, a_i)


FF_ROWS = 1024
FF_SHARD = D_FF // N_CHIPS


def _dot_nt(a, b):
    return lax.dot_general(a, b, (((1,), (1,)), ((), ())), preferred_element_type=F32)


def _ffn_up(h, w_gate_t, w_up_t):
    def body(h_ref, wg_ref, wu_ref, a_ref, b_ref, act_ref):
        hb = h_ref[...].astype(BF16)
        a = _dot_nt(hb, wg_ref[...])
        b = _dot_nt(hb, wu_ref[...])
        a_ref[...] = a
        b_ref[...] = b
        act_ref[...] = (a * jax.nn.sigmoid(a) * b).astype(BF16)

    w_spec = pl.BlockSpec((None, FF_SHARD, D_MODEL), lambda i, k: (k, 0, 0))
    o_spec = pl.BlockSpec((None, FF_ROWS, FF_SHARD), lambda i, k: (k, i, 0))
    shape = (N_CHIPS, SEQ, FF_SHARD)
    return pl.pallas_call(
        body, grid=(SEQ // FF_ROWS, N_CHIPS),
        in_specs=[pl.BlockSpec((FF_ROWS, D_MODEL), lambda i, k: (i, 0)), w_spec, w_spec],
        out_specs=[o_spec, o_spec, o_spec], out_shape=[_sds(shape), _sds(shape), _sds(shape, BF16)],
        compiler_params=_cp(("parallel", "parallel")), name="ffn_up")(h, w_gate_t, w_up_t)


def _ffn_down_ln2_loss(act, w_down, h, tgt, ln_g, ln_b):
    def body(act_ref, w_ref, h_ref, tgt_ref, g_ref, b_ref, dz_ref, loss_ref, dg_ref, db_ref, acc):
        i, k = pl.program_id(0), pl.program_id(1)
        part = jnp.dot(act_ref[...], w_ref[...], preferred_element_type=F32)

        @pl.when(k == 0)
        def _():
            acc[...] = part

        @pl.when(k > 0)
        def _():
            acc[...] += part

        @pl.when(k == N_CHIPS - 1)
        def _():
            g = g_ref[...]
            xhat, rstd = _ln_stats(DN_ALPHA * h_ref[...] + acc[...])
            err = xhat * g + b_ref[...] - tgt_ref[...]
            d_out = err * (1.0 / D_MODEL)
            dz_ref[...] = _ln_bwd(d_out, xhat, rstd, g)
            loss_rows = jnp.sum(err * err, axis=-1, keepdims=True) * (0.5 / D_MODEL)
            sums = (jnp.broadcast_to(jnp.sum(loss_rows, axis=0, keepdims=True), loss_ref.shape),
                    _colsum(d_out * xhat), _colsum(d_out))
            for ref, val in zip((loss_ref, dg_ref, db_ref), sums):
                @pl.when(i == 0)
                def _(ref=ref, val=val):
                    ref[...] = val

                @pl.when(i > 0)
                def _(ref=ref, val=val):
                    ref[...] += val

    row = pl.BlockSpec((FF_ROWS, D_MODEL), lambda i, k: (i, 0))
    vec = pl.BlockSpec((1, D_MODEL), lambda i, k: (0, 0))
    return pl.pallas_call(
        body, grid=(SEQ // FF_ROWS, N_CHIPS),
        in_specs=[pl.BlockSpec((None, FF_ROWS, FF_SHARD), lambda i, k: (k, i, 0)),
                  pl.BlockSpec((None, FF_SHARD, D_MODEL), lambda i, k: (k, 0, 0)), row, row, vec, vec],
        out_specs=[row, pl.BlockSpec((1, BLOCK), lambda i, k: (0, 0)), vec, vec],
        out_shape=[_sds((SEQ, D_MODEL)), _sds((1, BLOCK)), _sds((1, D_MODEL)), _sds((1, D_MODEL))],
        scratch_shapes=[pltpu.VMEM((FF_ROWS, D_MODEL), F32)],
        compiler_params=_cp(("arbitrary", "arbitrary")), name="ffn_down_ln2_loss")(act, w_down, h, tgt, ln_g, ln_b)


def _ffn_down_bwd(dz, w_down, a, b):
    def body(dz_ref, wd_ref, a_ref, b_ref, da_ref, db_ref):
        d_act = _dot_nt(dz_ref[...].astype(BF16), wd_ref[...])
        av = a_ref[...]
        sg = jax.nn.sigmoid(av)
        da_ref[...] = (d_act * b_ref[...] * sg * (1.0 + av * (1.0 - sg))).astype(BF16)
        db_ref[...] = (d_act * av * sg).astype(BF16)

    t_spec = pl.BlockSpec((None, FF_ROWS, FF_SHARD), lambda i, k: (k, i, 0))
    shape = (N_CHIPS, SEQ, FF_SHARD)
    return pl.pallas_call(
        body, grid=(SEQ // FF_ROWS, N_CHIPS),
        in_specs=[pl.BlockSpec((FF_ROWS, D_MODEL), lambda i, k: (i, 0)),
                  pl.BlockSpec((None, FF_SHARD, D_MODEL), lambda i, k: (k, 0, 0)), t_spec, t_spec],
        out_specs=[t_spec, t_spec], out_shape=[_sds(shape, BF16), _sds(shape, BF16)],
        compiler_params=_cp(("parallel", "parallel")), name="ffn_down_bwd")(dz, w_down, a, b)


def _ffn_dh(d_a, d_b, w_gate_t, w_up_t):
    def body(da_ref, db_ref, wg_ref, wu_ref, o_ref, acc):
        k = pl.program_id(1)
        part = (jnp.dot(da_ref[...], wg_ref[...], preferred_element_type=F32)
                + jnp.dot(db_ref[...], wu_ref[...], preferred_element_type=F32))

        @pl.when(k == 0)
        def _():
            acc[...] = part

        @pl.when(k > 0)
        def _():
            acc[...] += part

        @pl.when(k == N_CHIPS - 1)
        def _():
            o_ref[...] = acc[...]

    t_spec = pl.BlockSpec((None, FF_ROWS, FF_SHARD), lambda i, k: (k, i, 0))
    w_spec = pl.BlockSpec((None, FF_SHARD, D_MODEL), lambda i, k: (k, 0, 0))
    return pl.pallas_call(
        body, grid=(SEQ // FF_ROWS, N_CHIPS), in_specs=[t_spec, t_spec, w_spec, w_spec],
        out_specs=pl.BlockSpec((FF_ROWS, D_MODEL), lambda i, k: (i, 0)), out_shape=_sds((SEQ, D_MODEL)),
        scratch_shapes=[pltpu.VMEM((FF_ROWS, D_MODEL), F32)],
        compiler_params=_cp(("parallel", "arbitrary")), name="ffn_dh")(d_a, d_b, w_gate_t, w_up_t)


def _local_step(x, tgt, wts, small):
    s = SEQ
    cos_f, sin_s = [_to_phase_rows(t) for t in _rope_tables()]
    x = _reorder_rows(x, to_phase=True, name="phase_rows_x")
    tgt = _reorder_rows(tgt, to_phase=True, name="phase_rows_target")

    proj = _project_in(x, wts["w_in"], cos_f, sin_s)

    attn, lse = _attention_fwd(proj)

    (abar_r, abar_i, bbar_r, bbar_i), ssm_vjp = jax.vjp(
        _ssm_discretise, small["ssm_a_re"], small["ssm_a_im"], small["ssm_log_dt"], small["ssm_b_re"], small["ssm_b_im"])
    b_in_r, b_in_i = [_slab_block_diag(b.transpose(0, 2, 1)).astype(BF16) for b in (bbar_r, bbar_i)]
    c_out_r = _slab_block_diag(small["ssm_c_re"].transpose(0, 2, 1)).astype(BF16)
    c_out_ni = _slab_block_diag(-small["ssm_c_im"].transpose(0, 2, 1)).astype(BF16)
    a_r, a_i = abar_r.reshape(1, SSM_LANES), abar_i.reshape(1, SSM_LANES)
    d_skip = small["ssm_d"].reshape(1, SSM_WIDTH)

    u_f = _to_scan_rows(proj[:, 3 * QKV_WIDTH:3 * QKV_WIDTH + SSM_WIDTH])
    u_p = u_f.astype(BF16)
    y_c, h_r, h_i, e_r, e_i = _ssm_forward(u_p, b_in_r, b_in_i, c_out_r, c_out_ni, a_r, a_i)

    def branch(t, wg):
        return jnp.concatenate([jnp.dot(t, wg[k], preferred_element_type=F32) for k in range(N_CHIPS)], axis=1)

    def branch_t(t, wg):
        ns = wg.shape[2]
        return sum(_dot_nt(t[:, k * ns:(k + 1) * ns], wg[k]) for k in range(N_CHIPS))

    def gelu_glu(yc, u, dsk, wg):
        y = yc + dsk * u
        gel = (0.5 * y * (1.0 + jnp.tanh(GELU_C * (y + GELU_K * y * y * y)))).astype(BF16)
        glu = branch(gel, wg)
        return y, gel, glu, glu[:, :SSM_WIDTH] * jax.nn.sigmoid(glu[:, SSM_WIDTH:])

    y_s5, gel, glu, y_glu = _rowwise(
        gelu_glu, [y_c, u_f], [d_skip, wts["w_glu"]],
        [_sds((s, SSM_WIDTH)), _sds((s, SSM_WIDTH), BF16), _sds((s, 2 * SSM_WIDTH)), _sds((s, SSM_WIDTH), BF16)],
        tm=512, name="ssm_gelu_glu")
    y_glu = _from_scan_rows(y_glu)

    gl0 = (proj, D_MODEL, (3 * QKV_WIDTH + SSM_WIDTH) // D_MODEL)
    gl1 = (proj, D_MODEL, (3 * QKV_WIDTH + SSM_WIDTH) // D_MODEL + 1)
    b_gate = small["b_gate"]
    w_out = wts["w_out"].reshape(D_MODEL, D_MODEL)

    def mix_ln1(l0, l1, at, yg, xv, bg, wa, ws, wo, g, b):
        ya = branch(at.astype(BF16), wa)
        ys = branch(yg, ws)
        mixed = (jax.nn.sigmoid(l0 + bg[0:1]) * ya + jax.nn.sigmoid(l1 + bg[1:2]) * ys).astype(BF16)
        z = DN_ALPHA * xv + jnp.dot(mixed, wo, preferred_element_type=F32)
        xhat, _ = _ln_stats(z)
        return ya, ys, mixed, z, xhat * g + b

    y_attn, y_ssm, mixed, z1, h = _rowwise(
        mix_ln1, [gl0, gl1, attn, y_glu, x],
        [b_gate, wts["w_attn_br"], wts["w_ssm_br"], w_out, small["ln1_g"], small["ln1_b"]],
        [_sds((s, D_MODEL)), _sds((s, D_MODEL)), _sds((s, D_MODEL), BF16), _sds((s, D_MODEL)), _sds((s, D_MODEL))],
        tm=256, name="mix_ln1")

    nf = D_FF // N_CHIPS
    w_gate_t, w_up_t, w_down = wts["w_ff_gate"], wts["w_ff_up"], wts["w_ff_down"]
    ff_a, ff_b, act = _ffn_up(h, w_gate_t, w_up_t)
    dz2, loss_v, d_ln2_g, d_ln2_b = _ffn_down_ln2_loss(act, w_down, h, tgt, small["ln2_g"], small["ln2_b"])

    d_a, d_b = _ffn_down_bwd(dz2, w_down, ff_a, ff_b)

    def grad_rows(lhs, rhs, name):
        return _matmul(lhs, rhs, grid=(N_CHIPS,), a_spec=pl.BlockSpec((None, s, nf), lambda k: (k, 0, 0)),
                       b_spec=pl.BlockSpec((s, D_MODEL), lambda k: (0, 0)),
                       o_spec=pl.BlockSpec((None, nf, D_MODEL), lambda k: (k, 0, 0)),
                       out_shape=_sds((N_CHIPS, nf, D_MODEL), BF16), dims=(0, 0), name=name)

    g_w_ff_down = grad_rows(act, dz2, "g_w_ff_down")
    g_w_ff_gate = grad_rows(d_a, h, "g_w_ff_gate")
    g_w_ff_up = grad_rows(d_b, h, "g_w_ff_up")
    dh_ff = _ffn_dh(d_a, d_b, w_gate_t, w_up_t)

    def ln1_gate_bwd(dz, dff, z, l0, l1, ya, ys, g, bg, wo, wa, ws):
        xhat, rstd = _ln_stats(z)
        dh = DN_ALPHA * dz + dff
        dz_in = _ln_bwd(dh, xhat, rstd, g)
        dm = _dot_nt(dz_in.astype(BF16), wo)
        g0 = jax.nn.sigmoid(l0 + bg[0:1])
        g1 = jax.nn.sigmoid(l1 + bg[1:2])
        dl0 = dm * ya * g0 * (1.0 - g0)
        dl1 = dm * ys * g1 * (1.0 - g1)
        dya, dys = (dm * g0).astype(BF16), (dm * g1).astype(BF16)
        return (dz_in, dya, dys, jnp.concatenate([dl0, dl1], axis=1), branch_t(dya, wa), branch_t(dys, ws),
                _colsum(dh * xhat), _colsum(dh), _colsum(dl0), _colsum(dl1))

    dz1, d_y_attn, d_y_ssm, d_gl, d_attn, d_y_glu, d_ln1_g, d_ln1_b, d_bg0, d_bg1 = _rowwise(
        ln1_gate_bwd, [dz2, dh_ff, z1, gl0, gl1, y_attn, y_ssm],
        [small["ln1_g"], b_gate, w_out, wts["w_attn_br"], wts["w_ssm_br"]],
        [_sds((s, D_MODEL)), _sds((s, D_MODEL), BF16), _sds((s, D_MODEL), BF16), _sds((s, 2 * D_MODEL), BF16),
         _sds((s, ATTN_WIDTH)), _sds((s, SSM_WIDTH))],
        [_sds((1, D_MODEL))] * 4, tm=256, name="ln1_gate_bwd")
    g_w_out = _mm_plain(mixed, dz1, tm=D_MODEL, tn=512, dims=(0, 0), out_dtype=BF16, name="g_w_out")
    g_w_out = g_w_out.reshape(N_CHIPS, D_MODEL // N_CHIPS, D_MODEL)

    g_w_ssm_br = _mm_cols_tn(y_glu, d_y_ssm, ns=D_MODEL // N_CHIPS, name="g_w_ssm_br")
    d_y_glu = _to_scan_rows(d_y_glu)

    def glu_gelu_bwd(dyg, gl, y, u, dsk, wg):
        ga, gb = gl[:, :SSM_WIDTH], gl[:, SSM_WIDTH:]
        sg = jax.nn.sigmoid(gb)
        d_gl = jnp.concatenate([dyg * sg, dyg * ga * sg * (1.0 - sg)], axis=1).astype(BF16)
        dg = branch_t(d_gl, wg)
        th = jnp.tanh(GELU_C * (y + GELU_K * y * y * y))
        dy = dg * (0.5 * (1.0 + th) + 0.5 * y * (1.0 - th * th) * GELU_C * (1.0 + 3.0 * GELU_K * y * y))
        return d_gl, dy, dy * dsk, _colsum(dy * u)

    d_glu, d_y, d_u_skip, d_ssm_d = _rowwise(
        glu_gelu_bwd, [d_y_glu, glu, y_s5, u_f], [d_skip, wts["w_glu"]],
        [_sds((s, 2 * SSM_WIDTH), BF16), _sds((s, SSM_WIDTH), BF16), _sds((s, SSM_WIDTH))], [_sds((1, SSM_WIDTH))],
        tm=512, name="glu_gelu_bwd")
    g_w_glu = _mm_cols_tn(gel, d_glu, ns=2 * SSM_WIDTH // N_CHIPS, name="g_w_glu")
    d_u, d_abar_r, d_abar_i, d_c_r, d_c_ni, d_bin_r, d_bin_i = _ssm_backward(
        d_y, d_u_skip, u_p, h_r, h_i, e_r, e_i, b_in_r, b_in_i, c_out_r, c_out_ni, a_r, a_i)
    d_u = _from_scan_rows(d_u)
    d_bbar_r = d_bin_r.reshape(SSM_GROUPS, SSM_GROUP, SSM_STATE).transpose(0, 2, 1)
    d_bbar_i = d_bin_i.reshape(SSM_GROUPS, SSM_GROUP, SSM_STATE).transpose(0, 2, 1)
    d_a_re, d_a_im, d_log_dt, d_b_re, d_b_im = ssm_vjp(
        (d_abar_r.reshape(SSM_GROUPS, SSM_STATE), d_abar_i.reshape(SSM_GROUPS, SSM_STATE), d_bbar_r, d_bbar_i))
    d_c_re = d_c_r.reshape(SSM_GROUPS, SSM_GROUP, SSM_STATE)
    d_c_im = -d_c_ni.reshape(SSM_GROUPS, SSM_GROUP, SSM_STATE)

    g_w_attn_br = _mm_cols_tn(attn, d_y_attn, ns=D_MODEL // N_CHIPS, name="g_w_attn_br")
    dqkv = [_attention_bwd(g, proj, cos_f, sin_s, d_attn, attn, lse) for g in range(len(DILATIONS))]

    d_proj = jnp.concatenate([dqkv[g][j] for j in range(3) for g in range(len(DILATIONS))] + [d_u, d_gl],
                             axis=1)
    g_w_in = _mm_cols_tn(x, d_proj, ns=IN_WIDTH // N_CHIPS, name="g_w_in")

    def grad_x_after(after):
        dx_proj = _mm_cols_nt(d_proj, wts["w_in"], tm=1024, name="dx_proj", after=after)
        return _reorder_rows(dz1, dx_proj, to_phase=False, name="grad_x", scale=DN_ALPHA)

    big = {"w_in": g_w_in, "w_attn_br": g_w_attn_br, "w_ssm_br": g_w_ssm_br, "w_out": g_w_out, "w_glu": g_w_glu,
           "w_ff_gate": g_w_ff_gate, "w_ff_up": g_w_ff_up, "w_ff_down": g_w_ff_down}
    small_g = {"b_gate": jnp.concatenate([d_bg0, d_bg1], axis=0), "ssm_a_re": d_a_re, "ssm_a_im": d_a_im,
               "ssm_log_dt": d_log_dt, "ssm_b_re": d_b_re, "ssm_b_im": d_b_im, "ssm_c_re": d_c_re, "ssm_c_im": d_c_im,
               "ssm_d": d_ssm_d.reshape(SSM_WIDTH), "ln1_g": d_ln1_g, "ln1_b": d_ln1_b, "ln2_g": d_ln2_g,
               "ln2_b": d_ln2_b}
    marks = {"ln1_bwd": dz1, "scan_bwd": d_abar_r, "attention_bwd_0": dqkv[0][0]}
    return loss_v[0, 0], grad_x_after, big, small_g, marks


GATHER_ID, SWAP_ID, SCATTER_ID, JOIN_ID, EXCHANGE_ID = 1, 2, 3, 4, 5


def _place():
    return lax.axis_index("x"), lax.axis_index("y"), lax.axis_index("c")


def _other_chips(x, y):
    return [(1 - x, y), (x, 1 - y), (1 - x, 1 - y)]


def _handshake(peers):
    barrier = pltpu.get_barrier_semaphore()
    for peer in peers:
        pl.semaphore_signal(barrier, inc=1, device_id=peer, device_id_type=MESH)
    pl.semaphore_wait(barrier, len(peers))


def _sequencer(body, arrays, out_type, sems, collective_id, name):
    return pl.kernel(body, name=name, out_type=out_type,
                     mesh=plsc.ScalarSubcoreMesh(axis_name="sequencer", num_cores=1), scratch_types=sems,
                     compiler_params=pltpu.CompilerParams(collective_id=collective_id))(*arrays)


def _gather_weights(shards, *, name):
    nw = len(shards)

    def body(*refs):
        ins, outs = refs[:nw], refs[nw:2 * nw]
        send_sems, recv_sems, pass_send, pass_recv, local_sems = refs[2 * nw:]
        x, y, c = _place()
        chip = 2 * x + y
        chips = _other_chips(x, y)
        _handshake([(x, y, 1 - c)] + [(cx, cy, c) for cx, cy in chips])
        started = []
        for w in range(nw):
            hw = shards[w].shape[0] // 2
            mine = pl.ds(c * hw, hw)
            own = pltpu.make_async_copy(ins[w], outs[w].at[chip], local_sems.at[w])
            own.start()
            started.append(own)
            for j, (cx, cy) in enumerate(chips):
                cp = pltpu.make_async_remote_copy(
                    src_ref=ins[w].at[mine], dst_ref=outs[w].at[chip, mine], send_sem=send_sems.at[w, j],
                    recv_sem=recv_sems.at[w, j], device_id=(cx, cy, c), device_id_type=MESH)
                cp.start()
                started.append(cp)
        passed = []
        for w in range(nw):
            hw = shards[w].shape[0] // 2
            mine = pl.ds(c * hw, hw)
            for j, (cx, cy) in enumerate(chips):
                landed = outs[w].at[2 * cx + cy, mine]
                pltpu.make_async_remote_copy(
                    src_ref=ins[w].at[mine], dst_ref=landed, send_sem=send_sems.at[w, j],
                    recv_sem=recv_sems.at[w, j], device_id=(cx, cy, c), device_id_type=MESH).wait_recv()
                cp = pltpu.make_async_remote_copy(
                    src_ref=landed, dst_ref=landed, send_sem=pass_send.at[w, j], recv_sem=pass_recv.at[w, j],
                    device_id=(x, y, 1 - c), device_id_type=MESH)
                cp.start()
                passed.append(cp)
        for w in range(nw):
            hw = shards[w].shape[0] // 2
            theirs = pl.ds((1 - c) * hw, hw)
            for j, (cx, cy) in enumerate(chips):
                landed = outs[w].at[2 * cx + cy, theirs]
                pltpu.make_async_remote_copy(
                    src_ref=landed, dst_ref=landed, send_sem=pass_send.at[w, j], recv_sem=pass_recv.at[w, j],
                    device_id=(x, y, 1 - c), device_id_type=MESH).wait_recv()
        for cp in started[0::4]:
            cp.wait()
        for cp in [s for i, s in enumerate(started) if i % 4] + passed:
            cp.wait_send()

    sem = pltpu.SemaphoreType.DMA
    return _sequencer(body, shards, [_sds((N_CHIPS,) + a.shape, a.dtype) for a in shards],
                      [sem((nw, 3)), sem((nw, 3)), sem((nw, 3)), sem((nw, 3)), sem((nw,))], GATHER_ID, name)


def _swap_other_halves(grads, *, name):
    nw = len(grads)

    def body(*refs):
        ins, outs = refs[:nw], refs[nw:2 * nw]
        send_sems, recv_sems = refs[2 * nw:]
        x, y, c = _place()
        _handshake([(x, y, 1 - c)])
        cps = []
        for w in range(nw):
            hw = grads[w].shape[1] // 2
            cp = pltpu.make_async_remote_copy(
                src_ref=ins[w].at[:, pl.ds((1 - c) * hw, hw)], dst_ref=outs[w], send_sem=send_sems.at[w],
                recv_sem=recv_sems.at[w], device_id=(x, y, 1 - c), device_id_type=MESH)
            cp.start()
            cps.append(cp)
        for cp in cps:
            cp.wait()

    sem = pltpu.SemaphoreType.DMA
    return _sequencer(body, grads, [_sds((N_CHIPS, g.shape[1] // 2, g.shape[2]), g.dtype) for g in grads],
                      [sem((nw,)), sem((nw,))], SWAP_ID, name)


def _add_my_halves(core, grads, others, *, name, after=()):
    nw = len(grads)
    halves = [g.shape[1] // 2 for g in grads]

    def body(core_ref, *refs):
        outs = refs[2 * nw + len(after):]
        for g_ref, o_ref, out_ref in zip(refs[:nw], refs[nw:2 * nw], outs):
            out_ref[...] = (g_ref[...].astype(F32) + o_ref[...].astype(F32)).astype(out_ref.dtype)

    in_specs = [pl.BlockSpec((None, None, hw, g.shape[2]), lambda s, core_ref: (s, core_ref[0], 0, 0))
                for g, hw in zip(grads, halves)]
    in_specs += [pl.BlockSpec((None, hw, g.shape[2]), lambda s, core_ref: (s, 0, 0)) for g, hw in zip(grads, halves)]
    return pl.pallas_call(
        body,
        grid_spec=pltpu.PrefetchScalarGridSpec(
            num_scalar_prefetch=1, grid=(N_CHIPS,), in_specs=in_specs + [HBM_OPERAND] * len(after),
            out_specs=[pl.BlockSpec((None, hw, g.shape[2]), lambda s, core_ref: (s, 0, 0))
                       for g, hw in zip(grads, halves)]),
        out_shape=[_sds((N_CHIPS, hw, g.shape[2]), BF16) for g, hw in zip(grads, halves)],
        compiler_params=_cp(("parallel",)), name=name)(
            core, *[g.reshape(N_CHIPS, 2, hw, g.shape[2]) for g, hw in zip(grads, halves)], *others, *after)


def _scatter_partials(parts, *, name):
    nw = len(parts)

    def body(*refs):
        ins, outs = refs[:nw], refs[nw:2 * nw]
        send_sems, recv_sems = refs[2 * nw:]
        x, y, c = _place()
        _handshake([(cx, cy, c) for cx, cy in _other_chips(x, y)])
        cps = []
        for w in range(nw):
            for j, (cx, cy) in enumerate(_other_chips(x, y)):
                cp = pltpu.make_async_remote_copy(
                    src_ref=ins[w].at[2 * cx + cy], dst_ref=outs[w].at[j], send_sem=send_sems.at[w, j],
                    recv_sem=recv_sems.at[w, j], device_id=(cx, cy, c), device_id_type=MESH)
                cp.start()
                cps.append(cp)
        for cp in cps:
            cp.wait()

    sem = pltpu.SemaphoreType.DMA
    return _sequencer(body, parts, [_sds((3,) + p.shape[1:], p.dtype) for p in parts],
                      [sem((nw, 3)), sem((nw, 3))], SCATTER_ID, name)


SUM_STEPS = 2


def _sum_partials(chip, parts, recvd, *, name, after=()):
    nw = len(parts)
    rows = [p.shape[1] // SUM_STEPS for p in parts]

    def body(chip_ref, *refs):
        outs = refs[2 * nw + len(after):]
        for p_ref, r_ref, out_ref in zip(refs[:nw], refs[nw:2 * nw], outs):
            acc = p_ref[...].astype(F32)
            for j in range(3):
                acc = acc + r_ref[j].astype(F32)
            out_ref[...] = acc

    in_specs = [pl.BlockSpec((None, th, p.shape[2]), lambda i, chip_ref: (chip_ref[0], i, 0))
                for p, th in zip(parts, rows)]
    in_specs += [pl.BlockSpec((3, th, p.shape[2]), lambda i, chip_ref: (0, i, 0)) for p, th in zip(parts, rows)]
    return pl.pallas_call(
        body,
        grid_spec=pltpu.PrefetchScalarGridSpec(
            num_scalar_prefetch=1, grid=(SUM_STEPS,), in_specs=in_specs + [HBM_OPERAND] * len(after),
            out_specs=[pl.BlockSpec((th, p.shape[2]), lambda i, chip_ref: (i, 0)) for p, th in zip(parts, rows)]),
        out_shape=[_sds(p.shape[1:]) for p in parts], compiler_params=_cp(("parallel",)), name=name)(
            chip, *parts, *recvd, *after)


def _swap_reduced_halves(halves, *, name):
    nw = len(halves)

    def body(*refs):
        ins, outs = refs[:nw], refs[nw:2 * nw]
        send_sems, recv_sems = refs[2 * nw:]
        x, y, c = _place()
        _handshake([(x, y, 1 - c)])
        cps = []
        for w in range(nw):
            cp = pltpu.make_async_remote_copy(
                src_ref=ins[w], dst_ref=outs[w], send_sem=send_sems.at[w], recv_sem=recv_sems.at[w],
                device_id=(x, y, 1 - c), device_id_type=MESH)
            cp.start()
            cps.append(cp)
        for cp in cps:
            cp.wait()

    sem = pltpu.SemaphoreType.DMA
    return _sequencer(body, halves, [_sds(h.shape, h.dtype) for h in halves], [sem((nw,)), sem((nw,))], JOIN_ID, name)


def _exchange_rows(vec, *, name):
    def body(v_ref, slots, send_sems, recv_sems, local_sem):
        x, y, c = _place()
        me = 4 * x + 2 * y + c
        peers = []
        for mask in range(1, N_DEV):
            peers.append((1 - x if mask & 4 else x, 1 - y if mask & 2 else y, 1 - c if mask & 1 else c))
        _handshake(peers)
        own = pltpu.make_async_copy(v_ref, slots.at[me], local_sem)
        own.start()
        cps = []
        for k, peer in enumerate(peers):
            cp = pltpu.make_async_remote_copy(
                src_ref=v_ref, dst_ref=slots.at[me], send_sem=send_sems.at[k], recv_sem=recv_sems.at[k],
                device_id=peer, device_id_type=MESH)
            cp.start()
            cps.append(cp)
        for k, (px, py, pc) in enumerate(peers):
            pltpu.make_async_remote_copy(
                src_ref=v_ref, dst_ref=slots.at[4 * px + 2 * py + pc], send_sem=send_sems.at[k],
                recv_sem=recv_sems.at[k], device_id=(px, py, pc), device_id_type=MESH).wait_recv()
        for cp in cps:
            cp.wait_send()
        own.wait()

    sem = pltpu.SemaphoreType.DMA
    return _sequencer(body, [vec], [_sds((N_DEV,) + vec.shape)], [sem((N_DEV - 1,)), sem((N_DEV - 1,)), sem(())],
                      EXCHANGE_ID, name)[0]


def _sum_slots(slots, *, name, after=()):
    def body(s_ref, *rest):
        out_ref = rest[len(after)]
        acc = s_ref[0]
        for d in range(1, N_DEV):
            acc = acc + s_ref[d]
        out_ref[...] = acc

    vmem = pl.BlockSpec(memory_space=pltpu.VMEM)
    return pl.pallas_call(
        body, in_specs=[vmem] + [HBM_OPERAND] * len(after), out_specs=vmem, out_shape=_sds(slots.shape[1:]),
        compiler_params=pltpu.CompilerParams(vmem_limit_bytes=VMEM_LIMIT_BYTES), name=name)(slots, *after)


def _reduce_scatter_start(grads, core, *, tag, add_after=()):
    others = _swap_other_halves(grads, name="swap_other_halves_" + tag)
    parts = _add_my_halves(core, grads, others, name="add_my_halves_" + tag, after=add_after)
    return parts, _scatter_partials(parts, name="scatter_partials_" + tag)


def _reduce_scatter_finish(parts, recvd, chip, *, tag, sum_after=()):
    mine = _sum_partials(chip, parts, recvd, name="sum_partials_" + tag, after=sum_after)
    return mine, _swap_reduced_halves(mine, name="swap_reduced_halves_" + tag)


ADAM_BLOCK_ELEMS = 256 * 1024


def _adam_rows(rows, cols):
    tm = rows
    while tm * cols > ADAM_BLOCK_ELEMS and tm % 16 == 0:
        tm //= 2
    return tm


def _adam_step(wv, gv, mv, vv):
    m2 = ADAM_B1 * mv + (1.0 - ADAM_B1) * gv
    v2 = ADAM_B2 * vv + (1.0 - ADAM_B2) * (gv * gv)
    m_hat = m2 / (1.0 - ADAM_B1 ** ADAM_STEP)
    v_hat = v2 / (1.0 - ADAM_B2 ** ADAM_STEP)
    return -ADAM_LR * (m_hat / (jnp.sqrt(v_hat) + ADAM_EPS) + ADAM_WD * wv), m2, v2


def _adamw(w, g, m, v, *, name):
    rows, cols = w.shape
    return _rowwise(_adam_step, [w, g, m, v], [], [_sds((rows, cols))] * 3, tm=_adam_rows(rows, cols), name=name)


def _adamw_halves(core, w, g_mine, g_theirs, m, v, *, name, after=()):
    rows, cols = w.shape
    hw = rows // 2
    tm = _adam_rows(hw, cols)
    per_half = hw // tm

    def body(core_ref, w_ref, gm_ref, gt_ref, m_ref, v_ref, *rest):
        g_out, d_out, m_out, v_out = rest[len(after):]
        mine = (pl.program_id(0) // per_half) == core_ref[0]
        g = jnp.where(mine, gm_ref[...], gt_ref[...])
        d, m2, v2 = _adam_step(w_ref[...], g, m_ref[...], v_ref[...])
        g_out[...] = g
        d_out[...] = d
        m_out[...] = m2
        v_out[...] = v2

    full = pl.BlockSpec((tm, cols), lambda i, core_ref: (i, 0))
    half = pl.BlockSpec((tm, cols), lambda i, core_ref: (i % per_half, 0))
    return pl.pallas_call(
        body,
        grid_spec=pltpu.PrefetchScalarGridSpec(
            num_scalar_prefetch=1, grid=(rows // tm,),
            in_specs=[full, half, half, full, full] + [HBM_OPERAND] * len(after), out_specs=[full, full, full, full]),
        out_shape=[_sds((rows, cols))] * 4, compiler_params=_cp(("parallel",)), name=name)(
            core, w, g_mine, g_theirs, m, v, *after)


HELD_TRANSPOSED = ("w_ff_gate", "w_ff_up")


def _as_rows(name, arr):
    return arr[0].T if name in HELD_TRANSPOSED else arr[0]


def _from_rows(name, arr2d):
    return (arr2d.T if name in HELD_TRANSPOSED else arr2d)[None]


STORED_SWAPPED = ("ssm_b_re", "ssm_b_im")


def _as_stored(name, arr):
    return jnp.swapaxes(arr, -1, -2) if name in STORED_SWAPPED else arr


def _pack_rows(arrs):
    flat = jnp.concatenate([a.reshape(-1).astype(F32) for a in arrs])
    rows = -(-flat.shape[0] // 1024) * 8
    return jnp.pad(flat, (0, rows * 128 - flat.shape[0])).reshape(rows, 128)


def _unpack_rows(vec, shapes):
    flat = vec.reshape(-1)
    out, off = [], 0
    for shp in shapes:
        size = math.prod(shp)
        out.append(flat[off:off + size].reshape(shp))
        off += size
    return out


SMALL = ("b_gate", "ssm_a_re", "ssm_a_im", "ssm_log_dt", "ssm_b_re", "ssm_b_im", "ssm_c_re", "ssm_c_im", "ssm_d",
         "ln1_g", "ln1_b", "ln2_g", "ln2_b")
GATHER_GROUPS = (("w_in", ("w_in",)), ("mixer", ("w_attn_br", "w_ssm_br", "w_glu", "w_out")),
                 ("ffn", ("w_ff_gate", "w_ff_up", "w_ff_down")))
REDUCE_GROUPS = (("ffn", ("w_ff_down", "w_ff_gate", "w_ff_up")),
                 ("mixer", ("w_out", "w_ssm_br", "w_glu", "w_attn_br")), ("w_in", ("w_in",)))
WEIGHTS = ("w_in", "b_gate", "w_attn_br", "w_ssm_br", "w_out", "ssm_a_re", "ssm_a_im", "ssm_log_dt", "ssm_b_re",
           "ssm_b_im", "ssm_c_re", "ssm_c_im", "ssm_d", "w_glu", "ln1_g", "ln1_b", "w_ff_gate", "w_ff_up", "w_ff_down",
           "ln2_g", "ln2_b")


def kernel(x, w_in, b_gate, w_attn_br, w_ssm_br, w_out, ssm_a_re, ssm_a_im, ssm_log_dt, ssm_b_re, ssm_b_im, ssm_c_re, ssm_c_im, ssm_d, w_glu, ln1_g, ln1_b, w_ff_gate, w_ff_up, w_ff_down, ln2_g, ln2_b, loss_target, m_w_in, m_b_gate, m_w_attn_br, m_w_ssm_br, m_w_out, m_ssm_a_re, m_ssm_a_im, m_ssm_log_dt, m_ssm_b_re, m_ssm_b_im, m_ssm_c_re, m_ssm_c_im, m_ssm_d, m_w_glu, m_ln1_g, m_ln1_b, m_w_ff_gate, m_w_ff_up, m_w_ff_down, m_ln2_g, m_ln2_b, v_w_in, v_b_gate, v_w_attn_br, v_w_ssm_br, v_w_out, v_ssm_a_re, v_ssm_a_im, v_ssm_log_dt, v_ssm_b_re, v_ssm_b_im, v_ssm_c_re, v_ssm_c_im, v_ssm_d, v_w_glu, v_ln1_g, v_ln1_b, v_w_ff_gate, v_w_ff_up, v_w_ff_down, v_ln2_g, v_ln2_b):
    given = dict(locals())
    px, py, pc = _place()
    chip = 2 * px + py
    core_s = jnp.reshape(pc, (1,)).astype(jnp.int32)
    chip_s = jnp.reshape(chip, (1,)).astype(jnp.int32)

    wts = {}
    for tag, names in GATHER_GROUPS:
        wts.update(zip(names, _gather_weights([_as_rows(n, given[n]).astype(BF16) for n in names],
                                              name="gather_" + tag)))
    ncol = D_MODEL // N_CHIPS
    bg_mine = jnp.where(pc == 0, b_gate[0], jnp.zeros_like(b_gate[0]))
    bg_full = lax.dynamic_update_slice(jnp.zeros((2, D_MODEL), F32), bg_mine, (0, chip * ncol))
    bg_slots = _exchange_rows(bg_full.reshape(16, 128), name="exchange_gate_bias")
    bg_full = _sum_slots(bg_slots, name="sum_gate_bias").reshape(2, D_MODEL)
    small = {n: given[n][0] for n in SMALL if n.startswith("ssm")}
    small.update({n: given[n] for n in ("ln1_g", "ln1_b", "ln2_g", "ln2_b")})
    small["b_gate"] = bg_full

    loss_mine, grad_x_after, big_g, small_g, marks = _local_step(x[0], loss_target[0], wts, small)

    groups = dict(REDUCE_GROUPS)
    parts, recvd = {}, {}
    grads, delta, new_m, new_v = {}, {}, {}, {}

    def start(tag, add_after):
        parts[tag], recvd[tag] = _reduce_scatter_start([big_g[n] for n in groups[tag]], core_s, tag=tag,
                                                       add_after=add_after)

    def finish(tag, sum_after, adam_after):
        mine, theirs = _reduce_scatter_finish(parts[tag], recvd[tag], chip_s, tag=tag, sum_after=sum_after)
        for n, g_mine, g_theirs in zip(groups[tag], mine, theirs):
            res = _adamw_halves(core_s, _as_rows(n, given[n]), g_mine, g_theirs, _as_rows(n, given["m_" + n]),
                                _as_rows(n, given["v_" + n]), name="adamw_" + n, after=adam_after)
            grads[n], delta[n], new_m[n], new_v[n] = [_from_rows(n, r) for r in res]

    start("ffn", (marks["ln1_bwd"],))
    start("mixer", (marks["scan_bwd"],))
    finish("mixer", (marks["attention_bwd_0"],), (big_g["w_in"],))
    start("w_in", tuple(delta[n] for n in groups["mixer"]))
    in_flight = (parts["w_in"][0],)
    grad_x = grad_x_after(in_flight)
    finish("ffn", (marks["scan_bwd"],), in_flight)
    stored = [_as_stored(n, small_g[n]) for n in SMALL] + [loss_mine.reshape(1)]
    slots = _exchange_rows(_pack_rows(stored), name="exchange_small")
    summed = _unpack_rows(_sum_slots(slots, name="sum_small", after=in_flight), [a.shape for a in stored])
    loss = summed.pop()[0]
    for n, g in zip(SMALL, summed):
        g = _as_stored(n, g)
        if n == "b_gate":
            g = lax.dynamic_slice(g, (0, chip * ncol), (2, ncol))
        grads[n] = g.reshape(given[n].shape)
    packed = [_pack_rows([_as_stored(n, src[n]) for n in SMALL]) for src in
              (given, grads, {n: given["m_" + n] for n in SMALL}, {n: given["v_" + n] for n in SMALL})]
    shapes = [_as_stored(n, given[n]).shape for n in SMALL]
    small_out = _adamw(*packed, name="adamw_small")
    for out, vec in zip((delta, new_m, new_v), small_out):
        out.update((n, _as_stored(n, a)) for n, a in zip(SMALL, _unpack_rows(vec, shapes)))
    behind = [delta[n] for n in groups["ffn"]] + [small_out[0], grad_x]
    finish("w_in", tuple(behind), ())

    return (loss, grad_x.reshape(x.shape), *[grads[n] for n in WEIGHTS], *[delta[n] for n in WEIGHTS],
            *[new_m[n] for n in WEIGHTS], *[new_v[n] for n in WEIGHTS])
```

```python
import math

import jax
import jax.numpy as jnp
from jax import lax
from jax.experimental import pallas as pl
from jax.experimental.pallas import tpu as pltpu
from jax.experimental.pallas import tpu_sc as plsc

F32 = jnp.float32
BF16 = jnp.bfloat16
MESH = pl.DeviceIdType.MESH

D_MODEL = 1024
SEQ = 2048
HEAD_DIM = 64
ATTN_HEADS = 8
DILATIONS = (1, 4, 16)
ATTN_WIDTH = ATTN_HEADS * HEAD_DIM
QKV_WIDTH = 3 * ATTN_WIDTH
BLOCK = 128
ROPE_THETA = 10000.0
NEG_INF = -1e30
SSM_GROUP = 16
SSM_GROUPS = 32
SSM_WIDTH = 512
SSM_STATE = 64
SSM_LANES = SSM_GROUPS * SSM_STATE
SCAN_CHUNKS = 8
SCAN_STEPS = SEQ // SCAN_CHUNKS
IN_WIDTH = 3 * QKV_WIDTH + SSM_WIDTH + 2 * D_MODEL
D_FF = 2816
N_CHIPS = 4
N_DEV = 8
DN_ALPHA = 2.0 ** 0.25
LN_EPS = 1e-5
ADAM_LR = 0.001
ADAM_B1 = 0.9
ADAM_B2 = 0.999
ADAM_EPS = 1e-08
ADAM_WD = 0.01
ADAM_STEP = 10
GELU_C = math.sqrt(2.0 / math.pi)
GELU_K = 0.044715

VMEM_LIMIT_BYTES = 56 * 1024 * 1024


def _sds(shape, dtype=F32):
    return jax.ShapeDtypeStruct(tuple(shape), dtype)


def _cp(semantics=None):
    return pltpu.CompilerParams(dimension_semantics=semantics, vmem_limit_bytes=VMEM_LIMIT_BYTES)


HBM_OPERAND = pl.BlockSpec(memory_space=pl.ANY)


def _matmul(a, b, *, grid, a_spec, b_spec, o_spec, out_shape, dims, k_axis=None, name, after=()):
    nk = grid[k_axis] if k_axis is not None else 1
    o_block = tuple(d for d in o_spec.block_shape if d is not None)
    n_after = len(after)

    def body(a_ref, b_ref, *rest):
        o_ref, acc = rest[n_after], rest[n_after + 1:]
        part = lax.dot_general(a_ref[...].astype(BF16), b_ref[...].astype(BF16),
                               (((dims[0],), (dims[1],)), ((), ())), preferred_element_type=F32)
        if k_axis is None:
            o_ref[...] = part.astype(o_ref.dtype)
        else:
            k = pl.program_id(k_axis)

            @pl.when(k == 0)
            def _():
                acc[0][...] = part

            @pl.when(k > 0)
            def _():
                acc[0][...] += part

            @pl.when(k == nk - 1)
            def _():
                o_ref[...] = acc[0][...].astype(o_ref.dtype)

    sem = tuple("arbitrary" if ax == k_axis else "parallel" for ax in range(len(grid)))
    return pl.pallas_call(
        body, grid=grid, in_specs=[a_spec, b_spec] + [HBM_OPERAND] * n_after, out_specs=o_spec, out_shape=out_shape,
        scratch_shapes=[pltpu.VMEM(o_block, F32)] if k_axis is not None else [],
        compiler_params=_cp(sem), name=name)(a, b, *after)


def _mm_cols_nt(dy, wg, *, tm, name, out_dtype=F32, after=()):
    k, ns = wg.shape[1], wg.shape[2]
    m = dy.shape[0]
    a_spec = pl.BlockSpec((tm, ns), lambda i, s: (i, s))
    return _matmul(dy, wg, grid=(m // tm, N_CHIPS), a_spec=a_spec,
                   b_spec=pl.BlockSpec((None, k, ns), lambda i, s: (s, 0, 0)),
                   o_spec=pl.BlockSpec((tm, k), lambda i, s: (i, 0)),
                   out_shape=_sds((m, k), out_dtype), dims=(1, 1), k_axis=1, name=name, after=after)


def _mm_cols_tn(a, dy, *, ns, name, after=()):
    m, k = a.shape
    return _matmul(a, dy, grid=(N_CHIPS,), a_spec=pl.BlockSpec((m, k), lambda s: (0, 0)),
                   b_spec=pl.BlockSpec((m, ns), lambda s: (0, s)),
                   o_spec=pl.BlockSpec((None, k, ns), lambda s: (s, 0, 0)),
                   out_shape=_sds((N_CHIPS, k, ns), BF16), dims=(0, 0), name=name, after=after)


def _mm_rows_tn(a, dy, *, name):
    m, k = a.shape
    rows, n = k // N_CHIPS, dy.shape[1]
    return _matmul(a, dy, grid=(N_CHIPS,), a_spec=pl.BlockSpec((m, rows), lambda s: (0, s)),
                   b_spec=pl.BlockSpec((m, n), lambda s: (0, 0)),
                   o_spec=pl.BlockSpec((None, rows, n), lambda s: (s, 0, 0)),
                   out_shape=_sds((N_CHIPS, rows, n), BF16), dims=(0, 0), name=name)


def _rowwise(fn, tiled, full, outs, accs=(), *, tm, name, after=()):
    args, in_specs = [], []
    for t in tiled:
        if isinstance(t, tuple):
            arr, w, cb = t
            in_specs.append(pl.BlockSpec((tm, w), lambda i, cb=cb: (i, cb)))
        else:
            arr = t
            in_specs.append(pl.BlockSpec((tm, arr.shape[1]), lambda i: (i, 0)))
        args.append(arr)
    rows = args[0].shape[0]
    for f in full:
        in_specs.append(pl.BlockSpec(f.shape, lambda i, nd=f.ndim: (0,) * nd))
        args.append(f)
    out_specs = [pl.BlockSpec((tm, o.shape[1]), lambda i: (i, 0)) for o in outs]
    out_specs += [pl.BlockSpec(a.shape, lambda i, nd=len(a.shape): (0,) * nd) for a in accs]
    n_in, n_out = len(args), len(outs)
    in_specs += [HBM_OPERAND] * len(after)
    first_out = n_in + len(after)

    def body(*refs):
        res = fn(*[r[...] for r in refs[:n_in]])
        res = res if isinstance(res, (tuple, list)) else (res,)
        for r, v in zip(refs[first_out:first_out + n_out], res[:n_out]):
            r[...] = v.astype(r.dtype)
        i = pl.program_id(0)
        for r, v in zip(refs[first_out + n_out:], res[n_out:]):
            @pl.when(i == 0)
            def _(r=r, v=v):
                r[...] = v

            @pl.when(i > 0)
            def _(r=r, v=v):
                r[...] += v

    res = pl.pallas_call(
        body, grid=(rows // tm,), in_specs=in_specs, out_specs=out_specs, out_shape=list(outs) + list(accs),
        compiler_params=_cp(("arbitrary",) if accs else ("parallel",)), name=name)(*args, *after)
    return res


def _colsum(v):
    return jnp.sum(v, axis=0, keepdims=True)


def _ln_stats(z):
    mu = jnp.mean(z, axis=-1, keepdims=True)
    zc = z - mu
    var = jnp.mean(zc * zc, axis=-1, keepdims=True)
    rstd = lax.rsqrt(var + LN_EPS)
    return zc * rstd, rstd


def _ln_bwd(dy, xhat, rstd, g):
    dxh = dy * g
    m1 = jnp.mean(dxh, axis=-1, keepdims=True)
    m2 = jnp.mean(dxh * xhat, axis=-1, keepdims=True)
    return rstd * (dxh - m1 - xhat * m2)


def _swap_halves(t):
    w = t.shape[-1]
    lane = lax.broadcasted_iota(jnp.int32, t.shape, t.ndim - 1)
    return jnp.where((lane % HEAD_DIM) < HEAD_DIM // 2, pltpu.roll(t, w - HEAD_DIM // 2, t.ndim - 1),
                     pltpu.roll(t, HEAD_DIM // 2, t.ndim - 1))


PHASES = max(DILATIONS)
PAIR = 2 * HEAD_DIM
UNITS = SEQ // BLOCK
UNIT_BATCH = 8
ROPE_ROWS = 256


def _to_phase_rows(t):
    return t.reshape(SEQ // PHASES, PHASES, t.shape[1]).transpose(1, 0, 2).reshape(t.shape)


def _reorder_rows(arr, plus=None, *, to_phase, name, scale=1.0):
    def body(*refs):
        o_ref = refs[-1]
        for rho in range(PHASES):
            phase = pl.ds(rho * BLOCK, BLOCK)
            strided = pl.ds(rho, BLOCK, stride=PHASES)
            src, dst = (strided, phase) if to_phase else (phase, strided)
            val = refs[0][src, :]
            if scale != 1.0:
                val = val * scale
            if plus is not None:
                val = val + refs[1][src, :]
            o_ref[dst, :] = val

    spec = pl.BlockSpec((SEQ, BLOCK), lambda j: (0, j))
    ins = [arr] if plus is None else [arr, plus]
    return pl.pallas_call(body, grid=(arr.shape[1] // BLOCK,), in_specs=[spec] * len(ins), out_specs=spec,
                          out_shape=_sds(arr.shape), compiler_params=_cp(("parallel",)), name=name)(*ins)


def _rope(t, cf, ss):
    return t * cf + _swap_halves(t) * ss


def _rope_transposed(d, cf, ss):
    return d * cf + _swap_halves(d * ss)


def _unit_pieces(u, dil):
    pieces, length = PHASES // dil, 8 * dil
    if dil == 1:
        rho, i = 0, u
    elif dil == PHASES:
        rho, i = u, 0
    else:
        rho, i = jnp.bitwise_and(u, dil - 1), jnp.right_shift(u, dil.bit_length() - 1)
    before = jnp.maximum(i - 1, 0)
    cur = [pl.multiple_of((rho + dil * k) * BLOCK + length * i, 8) for k in range(pieces)]
    prev = [pl.multiple_of((rho + dil * k) * BLOCK + length * before, 8) for k in range(pieces)]
    return i, cur, prev


def _load_tile(ref, starts, dil):
    return jnp.concatenate([ref[pl.ds(st, 8 * dil), :] for st in starts], axis=0)


def _store_tile(ref, starts, dil, val, head=None, accumulate=False):
    length = 8 * dil
    lanes = slice(None) if head is None else pl.ds(head * HEAD_DIM, HEAD_DIM)
    cols = slice(None) if head is None else slice(head * HEAD_DIM, (head + 1) * HEAD_DIM)
    for k, st in enumerate(starts):
        piece = val[k * length:(k + 1) * length, cols]
        if accumulate:
            ref[pl.ds(st, length), lanes] += piece
        else:
            ref[pl.ds(st, length), lanes] = piece


def _tile_position(idx, dil):
    pieces, length = PHASES // dil, 8 * dil
    return pieces * jnp.bitwise_and(idx, length - 1) + jnp.right_shift(idx, length.bit_length() - 1)


def _band_mask(i, dil):
    row = lax.broadcasted_iota(jnp.int32, (BLOCK, 2 * BLOCK), 0)
    col = lax.broadcasted_iota(jnp.int32, (BLOCK, 2 * BLOCK), 1)
    key_pos = _tile_position(jnp.bitwise_and(col, BLOCK - 1), dil) + jnp.where(col >= BLOCK, 0, -BLOCK)
    dist = _tile_position(row, dil) - key_pos
    return (dist >= 0) & (dist <= BLOCK) & ((col >= BLOCK) | (i > 0))


def _causal_mask():
    row = lax.broadcasted_iota(jnp.int32, (BLOCK, BLOCK), 0)
    col = lax.broadcasted_iota(jnp.int32, (BLOCK, BLOCK), 1)
    return row >= col


def _pair_views(col0):
    return [pl.BlockSpec((SEQ, PAIR), lambda hp, g=g: (0, col0 // PAIR + g * (ATTN_WIDTH // PAIR) + hp))
            for g in range(len(DILATIONS))]


def _project_in(x, wg, cos_f, sin_s):
    ns = wg.shape[2]
    tiles = ns // PAIR

    def body(x_ref, w_ref, cf_ref, ss_ref, o_ref):
        shard = pl.program_id(1)
        acc = jnp.dot(x_ref[...].astype(BF16), w_ref[...], preferred_element_type=F32)
        cf, ss = cf_ref[...], ss_ref[...]

        def write(rotated, scaled):
            for t in range(tiles):
                lanes = slice(t * PAIR, (t + 1) * PAIR)
                val = acc[:, lanes]
                if t < rotated:
                    val = _rope(val, cf, ss)
                    if t < scaled:
                        val = val * (1.0 / math.sqrt(HEAD_DIM))
                o_ref[:, lanes] = val

        for s in range(N_CHIPS):
            rotated = min(max(2 * QKV_WIDTH - s * ns, 0), ns) // PAIR
            scaled = min(max(QKV_WIDTH - s * ns, 0), ns) // PAIR

            @pl.when(shard == s)
            def _(rotated=rotated, scaled=scaled):
                write(rotated, scaled)

    table = pl.BlockSpec((FF_ROWS, PAIR), lambda i, s: (i, 0))
    return pl.pallas_call(
        body, grid=(SEQ // FF_ROWS, N_CHIPS),
        in_specs=[pl.BlockSpec((FF_ROWS, D_MODEL), lambda i, s: (i, 0)),
                  pl.BlockSpec((None, D_MODEL, ns), lambda i, s: (s, 0, 0)), table, table],
        out_specs=pl.BlockSpec((FF_ROWS, ns), lambda i, s: (i, s)), out_shape=_sds((SEQ, N_CHIPS * ns)),
        compiler_params=_cp(("parallel", "parallel")), name="project_in")(x, wg, cos_f, sin_s)


def _attention_fwd(proj):
    ng = len(DILATIONS)

    def body(*refs):
        q_refs, k_refs, v_refs = refs[:ng], refs[ng:2 * ng], refs[2 * ng:3 * ng]
        attn_ref, lse_ref = refs[3 * ng:]
        qr_refs, kr_refs = q_refs, k_refs
        first = lax.broadcasted_iota(jnp.int32, (BLOCK, PAIR), 1) < HEAD_DIM
        for g, dil in enumerate(DILATIONS):
            two_blocks = SEQ // dil > BLOCK

            def units(t, carry, g=g, dil=dil, two_blocks=two_blocks):
                picked = [_unit_pieces(t * UNIT_BATCH + j, dil) for j in range(UNIT_BATCH)]

                def tiles(ref, with_prev=False):
                    if with_prev and two_blocks:
                        return jnp.stack([jnp.concatenate([_load_tile(ref, prev, dil), _load_tile(ref, rows, dil)],
                                                          axis=0) for _, rows, prev in picked])
                    return jnp.stack([_load_tile(ref, rows, dil) for _, rows, _ in picked])

                qq = tiles(qr_refs[g]).astype(BF16)
                kk = tiles(kr_refs[g], True).astype(BF16)
                vv = tiles(v_refs[g], True).astype(BF16)
                if two_blocks:
                    valid = jnp.stack([_band_mask(i, dil) for i, _, _ in picked])
                else:
                    valid = _causal_mask()[None]
                mine = first[None]
                zero = jnp.zeros_like(qq)
                outs, lses = [], []
                for qh in (jnp.where(mine, qq, zero), jnp.where(mine, zero, qq)):
                    s = jnp.einsum("pqd,pkd->pqk", qh, kk, preferred_element_type=F32)
                    s = jnp.where(valid, s, NEG_INF)
                    m = jnp.max(s, axis=-1, keepdims=True)
                    p = jnp.exp(s - m)
                    l = jnp.sum(p, axis=-1, keepdims=True)
                    outs.append(jnp.einsum("pqk,pkd->pqd", p.astype(BF16), vv, preferred_element_type=F32) * (1.0 / l))
                    lses.append(m + jnp.log(l))
                o = jnp.where(mine, outs[0], outs[1])
                lse = jnp.where(mine, lses[0], lses[1])
                if g > 0:
                    lse_old = tiles(lse_ref)
                    m = jnp.maximum(lse_old, lse)
                    lse_new = m + jnp.log(jnp.exp(lse_old - m) + jnp.exp(lse - m))
                    o = tiles(attn_ref) * jnp.exp(lse_old - lse_new) + o * jnp.exp(lse - lse_new)
                    lse = lse_new
                for j, (_, rows, _) in enumerate(picked):
                    _store_tile(attn_ref, rows, dil, o[j])
                    _store_tile(lse_ref, rows, dil, lse[j])
                return carry

            lax.fori_loop(0, UNITS // UNIT_BATCH, units, 0)

    out = pl.BlockSpec((SEQ, PAIR), lambda hp: (0, hp))
    return pl.pallas_call(
        body, grid=(ATTN_WIDTH // PAIR,),
        in_specs=_pair_views(0) + _pair_views(QKV_WIDTH) + _pair_views(2 * QKV_WIDTH),
        out_specs=[out, out], out_shape=[_sds((SEQ, ATTN_WIDTH)), _sds((SEQ, ATTN_WIDTH))],
        compiler_params=_cp(("parallel",)), name="attention_fwd")(*([proj] * (3 * ng)))


def _attention_bwd(g, proj, cos_f, sin_s, d_attn, attn, lse):
    dil = DILATIONS[g]
    two_blocks = SEQ // dil > BLOCK

    def body(qr_ref, kr_ref, v_ref, cf_ref, ss_ref, do_ref, o_ref, lse_ref, dq_out, dk_out, dv_out,
             dq_acc, dk_acc, dv_acc):
        dk_acc[...] = jnp.zeros_like(dk_acc)
        dv_acc[...] = jnp.zeros_like(dv_acc)
        nk = 2 * BLOCK if two_blocks else BLOCK
        first = lax.broadcasted_iota(jnp.int32, (BLOCK, PAIR), 1) < HEAD_DIM
        first_k = lax.broadcasted_iota(jnp.int32, (nk, PAIR), 1) < HEAD_DIM

        def units(t, carry):
            picked = [_unit_pieces(t * UNIT_BATCH + j, dil) for j in range(UNIT_BATCH)]

            def tiles(ref, with_prev=False):
                if with_prev and two_blocks:
                    return jnp.stack([jnp.concatenate([_load_tile(ref, prev, dil), _load_tile(ref, rows, dil)], axis=0)
                                      for _, rows, prev in picked])
                return jnp.stack([_load_tile(ref, rows, dil) for _, rows, _ in picked])

            qq = tiles(qr_ref).astype(BF16)
            kk = tiles(kr_ref, True).astype(BF16)
            vv = tiles(v_ref, True).astype(BF16)
            dof = tiles(do_ref)
            dd = dof * tiles(o_ref)
            lse3 = tiles(lse_ref)
            dob = dof.astype(BF16)
            if two_blocks:
                valid = jnp.stack([_band_mask(i, dil) for i, _, _ in picked])
            else:
                valid = _causal_mask()[None]
            zq, zf = jnp.zeros_like(qq), jnp.zeros_like(dd)
            dqs, dks, dvs = [], [], []
            for head in range(2):
                mine = first[None] if head == 0 else jnp.logical_not(first)[None]
                delta = jnp.sum(jnp.where(mine, dd, zf), axis=-1, keepdims=True)
                lse_h = lse3[:, :, head * HEAD_DIM:head * HEAD_DIM + 1]
                s = jnp.einsum("pqd,pkd->pqk", jnp.where(mine, qq, zq), kk, preferred_element_type=F32)
                p = jnp.where(valid, jnp.exp(s - lse_h), 0.0)
                dp = jnp.einsum("pqd,pkd->pqk", jnp.where(mine, dob, zq), vv, preferred_element_type=F32)
                ds = (p * (dp - delta)).astype(BF16)
                dqs.append(jnp.einsum("pqk,pkd->pqd", ds, kk, preferred_element_type=F32))
                dks.append(jnp.einsum("pqk,pqd->pkd", ds, qq, preferred_element_type=F32))
                dvs.append(jnp.einsum("pqk,pqd->pkd", p.astype(BF16), dob, preferred_element_type=F32))
            dq = jnp.where(first[None], dqs[0], dqs[1])
            dk = jnp.where(first_k[None], dks[0], dks[1])
            dv = jnp.where(first_k[None], dvs[0], dvs[1])
            for j, (_, rows, prev) in enumerate(picked):
                _store_tile(dq_acc, rows, dil, dq[j])
                _store_tile(dk_acc, rows, dil, dk[j, nk - BLOCK:], accumulate=True)
                _store_tile(dv_acc, rows, dil, dv[j, nk - BLOCK:], accumulate=True)
                if two_blocks:
                    _store_tile(dk_acc, prev, dil, dk[j, :BLOCK], accumulate=True)
                    _store_tile(dv_acc, prev, dil, dv[j, :BLOCK], accumulate=True)
            return carry

        lax.fori_loop(0, UNITS // UNIT_BATCH, units, 0)

        def finish(t, carry):
            rows = pl.ds(pl.multiple_of(t * ROPE_ROWS, ROPE_ROWS), ROPE_ROWS)
            cf, ss = cf_ref[rows, :], ss_ref[rows, :]
            dq = dq_acc[rows, :] * (1.0 / math.sqrt(HEAD_DIM))
            dq_out[rows, :] = _rope_transposed(dq, cf, ss).astype(BF16)
            dk_out[rows, :] = _rope_transposed(dk_acc[rows, :], cf, ss).astype(BF16)
            dv_out[rows, :] = dv_acc[rows, :].astype(BF16)
            return carry

        lax.fori_loop(0, SEQ // ROPE_ROWS, finish, 0)

    whole = pl.BlockSpec((SEQ, PAIR), lambda hp: (0, 0))
    pair = pl.BlockSpec((SEQ, PAIR), lambda hp: (0, hp))
    views = [_pair_views(col0)[g] for col0 in (0, QKV_WIDTH, 2 * QKV_WIDTH)]
    return pl.pallas_call(
        body, grid=(ATTN_WIDTH // PAIR,), in_specs=views + [whole, whole, pair, pair, pair],
        out_specs=[pair, pair, pair], out_shape=[_sds((SEQ, ATTN_WIDTH), BF16)] * 3,
        scratch_shapes=[pltpu.VMEM((SEQ, PAIR), F32)] * 3,
        compiler_params=_cp(("parallel",)), name=f"attention_bwd_{g}")(proj, proj, proj, cos_f, sin_s, d_attn, attn, lse)


def _cmul(ar, ai, br, bi):
    return ar * br - ai * bi, ar * bi + ai * br


def _pow256(ar, ai):
    for _ in range(8):
        ar, ai = _cmul(ar, ai, ar, ai)
    return ar, ai


def _chunk_carries(first_r, first_i, pr, pi, reverse):
    rows = lax.broadcasted_iota(jnp.int32, first_r.shape, 0)
    out_r = jnp.zeros_like(first_r)
    out_i = jnp.zeros_like(first_i)
    hr = jnp.zeros_like(first_r[0:1])
    hi = jnp.zeros_like(hr)
    order = range(SCAN_CHUNKS - 1, -1, -1) if reverse else range(SCAN_CHUNKS)
    for c in order:
        out_r = jnp.where(rows == c, hr, out_r)
        out_i = jnp.where(rows == c, hi, out_i)
        tr, ti = _cmul(pr[0:1], pi[0:1], hr, hi)
        hr = first_r[c:c + 1] + tr
        hi = first_i[c:c + 1] + ti
    return out_r, out_i


def _tile(j):
    return pl.ds(pl.multiple_of(j * SCAN_CHUNKS, SCAN_CHUNKS), SCAN_CHUNKS)


def _to_scan_rows(t):
    per = SCAN_STEPS // PHASES
    return t.reshape(PHASES, SCAN_CHUNKS, per, t.shape[1]).transpose(2, 0, 1, 3).reshape(t.shape)


def _from_scan_rows(t):
    per = SCAN_STEPS // PHASES
    return t.reshape(per, PHASES, SCAN_CHUNKS, t.shape[1]).transpose(1, 2, 0, 3).reshape(t.shape)


def _scan_in_place(hr_ref, hi_ref, a_r, a_i):
    def local(j, carry):
        tr, ti = _cmul(a_r, a_i, carry[0], carry[1])
        nr = tr + hr_ref[_tile(j), :]
        ni = ti + hi_ref[_tile(j), :]
        hr_ref[_tile(j), :] = nr
        hi_ref[_tile(j), :] = ni
        return nr, ni

    zero = jnp.zeros_like(a_r)
    last_r, last_i = lax.fori_loop(0, SCAN_STEPS, local, (zero, zero), unroll=4)
    pr, pi = _pow256(a_r, a_i)
    er, ei = _chunk_carries(last_r, last_i, pr, pi, reverse=False)

    def fix(j, carry):
        tr, ti = _cmul(carry[0], carry[1], er, ei)
        hr_ref[_tile(j), :] += tr
        hi_ref[_tile(j), :] += ti
        return _cmul(carry[0], carry[1], a_r, a_i)

    lax.fori_loop(0, SCAN_STEPS, fix, (a_r, a_i), unroll=4)
    return er, ei


def _reverse_scan_in_place(lr_ref, li_ref, hr_ref, hi_ref, er, ei, a_r, a_i):
    def local(t, carry):
        j = SCAN_STEPS - 1 - t
        tr, ti = _cmul(a_r, a_i, carry[0], carry[1])
        nr = tr + lr_ref[_tile(j), :]
        ni = ti + li_ref[_tile(j), :]
        lr_ref[_tile(j), :] = nr
        li_ref[_tile(j), :] = ni
        return nr, ni

    zero = jnp.zeros_like(a_r)
    first_r, first_i = lax.fori_loop(0, SCAN_STEPS, local, (zero, zero), unroll=4)
    pr, pi = _pow256(a_r, a_i)
    nxt_r, nxt_i = _chunk_carries(first_r, first_i, pr, pi, reverse=True)

    def accumulate(lam_r, lam_i, hp_r, hp_i, acc):
        return (acc[0] + lam_r * hp_r + lam_i * hp_i, acc[1] + lam_i * hp_r - lam_r * hp_i)

    def fix(t, carry):
        qr, qi, acc_r, acc_i = carry
        j = SCAN_STEPS - 1 - t
        tr, ti = _cmul(qr, qi, nxt_r, nxt_i)
        lam_r = lr_ref[_tile(j), :] + tr
        lam_i = li_ref[_tile(j), :] + ti
        lr_ref[_tile(j), :] = lam_r
        li_ref[_tile(j), :] = lam_i
        acc_r, acc_i = accumulate(lam_r, lam_i, hr_ref[_tile(j - 1), :], hi_ref[_tile(j - 1), :], (acc_r, acc_i))
        qr, qi = _cmul(qr, qi, a_r, a_i)
        return qr, qi, acc_r, acc_i

    qr, qi, acc_r, acc_i = lax.fori_loop(0, SCAN_STEPS - 1, fix, (a_r, a_i, zero, zero), unroll=4)
    tr, ti = _cmul(qr, qi, nxt_r, nxt_i)
    lam_r = lr_ref[_tile(0), :] + tr
    lam_i = li_ref[_tile(0), :] + ti
    lr_ref[_tile(0), :] = lam_r
    li_ref[_tile(0), :] = lam_i
    acc_r, acc_i = accumulate(lam_r, lam_i, er, ei, (acc_r, acc_i))
    return jnp.sum(acc_r, axis=0, keepdims=True), jnp.sum(acc_i, axis=0, keepdims=True)


def _rope_tables():
    half = HEAD_DIM // 2
    inv_freq = ROPE_THETA ** (-jnp.arange(half, dtype=F32) / half)
    ang = jnp.arange(SEQ, dtype=F32)[:, None] * inv_freq[None, :]
    cos, sin = jnp.cos(ang), jnp.sin(ang)
    cos_f = jnp.concatenate([cos, cos, cos, cos], axis=1)
    sin_s = jnp.concatenate([-sin, sin, -sin, sin], axis=1)
    return cos_f, sin_s


def _ssm_discretise(a_re, a_im, log_dt, b_re, b_im):
    lam = lax.complex(a_re, a_im)
    dt = jnp.exp(log_dt)[:, None]
    a_bar = jnp.exp(lam * dt)
    b_bar = ((a_bar - 1.0) / lam)[..., None] * lax.complex(b_re, b_im)
    return a_bar.real, a_bar.imag, b_bar.real, b_bar.imag


SSM_SLABS = 4
SLAB_GROUPS = SSM_GROUPS // SSM_SLABS
SLAB_IN = SSM_WIDTH // SSM_SLABS
SLAB_STATE = SSM_LANES // SSM_SLABS


def _slab_block_diag(blocks):
    _, r, c = blocks.shape
    eye = jnp.eye(SLAB_GROUPS, dtype=blocks.dtype)
    b5 = blocks.reshape(SSM_SLABS, SLAB_GROUPS, r, 1, c) * eye[None, :, None, :, None]
    return b5.reshape(SSM_SLABS, SLAB_GROUPS * r, SLAB_GROUPS * c)


def _diag_blocks(a, b):
    ra, cb = a.shape[1], b.shape[1]
    wa, wb = ra // SLAB_GROUPS, cb // SLAB_GROUPS
    d = lax.dot_general(a, b, (((0,), (0,)), ((), ())), preferred_element_type=F32)
    row_g = jnp.right_shift(lax.broadcasted_iota(jnp.int32, (ra, cb), 0), wa.bit_length() - 1)
    col_g = jnp.right_shift(lax.broadcasted_iota(jnp.int32, (ra, cb), 1), wb.bit_length() - 1)
    d = jnp.where(row_g == col_g, d, 0.0)
    fold = (jnp.bitwise_and(lax.broadcasted_iota(jnp.int32, (cb, wb), 0), wb - 1)
            == lax.broadcasted_iota(jnp.int32, (cb, wb), 1)).astype(F32)
    return jnp.dot(d, fold, preferred_element_type=F32, precision=lax.Precision.HIGHEST)


def _slab_specs():
    tok = pl.BlockSpec((SEQ, SLAB_IN), lambda j: (0, j))
    state = pl.BlockSpec((SEQ, SLAB_STATE), lambda j: (0, j))
    b_in = pl.BlockSpec((None, SLAB_IN, SLAB_STATE), lambda j: (j, 0, 0))
    c_out = pl.BlockSpec((None, SLAB_STATE, SLAB_IN), lambda j: (j, 0, 0))
    vec = pl.BlockSpec((1, SLAB_STATE), lambda j: (0, j))
    ent = pl.BlockSpec((SCAN_CHUNKS, SLAB_STATE), lambda j: (0, j))
    return tok, state, b_in, c_out, vec, ent


def _ssm_forward(u, b_in_r, b_in_i, c_out_r, c_out_ni, a_r, a_i):
    def body(u_ref, br_ref, bi_ref, cr_ref, ci_ref, ar_ref, ai_ref, y_ref, hr_ref, hi_ref, er_ref, ei_ref):
        uu = u_ref[...]
        hr_ref[...] = jnp.dot(uu, br_ref[...], preferred_element_type=F32)
        hi_ref[...] = jnp.dot(uu, bi_ref[...], preferred_element_type=F32)
        a_re = jnp.broadcast_to(ar_ref[...], (SCAN_CHUNKS, SLAB_STATE))
        a_im = jnp.broadcast_to(ai_ref[...], (SCAN_CHUNKS, SLAB_STATE))
        er_ref[...], ei_ref[...] = _scan_in_place(hr_ref, hi_ref, a_re, a_im)
        y_ref[...] = (jnp.dot(hr_ref[...].astype(BF16), cr_ref[...], preferred_element_type=F32)
                      + jnp.dot(hi_ref[...].astype(BF16), ci_ref[...], preferred_element_type=F32))

    tok, state, b_in, c_out, vec, ent = _slab_specs()
    return pl.pallas_call(
        body, grid=(SSM_SLABS,), in_specs=[tok, b_in, b_in, c_out, c_out, vec, vec],
        out_specs=[tok, state, state, ent, ent],
        out_shape=[_sds((SEQ, SSM_WIDTH)), _sds((SEQ, SSM_LANES)), _sds((SEQ, SSM_LANES)),
                   _sds((SCAN_CHUNKS, SSM_LANES)), _sds((SCAN_CHUNKS, SSM_LANES))],
        compiler_params=_cp(("parallel",)), name="ssm_forward")(u, b_in_r, b_in_i, c_out_r, c_out_ni, a_r, a_i)


def _ssm_backward(d_y, d_u_skip, u, h_r, h_i, e_r, e_i, b_in_r, b_in_i, c_out_r, c_out_ni, a_r, a_i):
    def body(dy_ref, skip_ref, u_ref, hr_ref, hi_ref, er_ref, ei_ref, br_ref, bi_ref, cr_ref, ci_ref, ar_ref, ai_ref,
             du_ref, dar_ref, dai_ref, dcr_ref, dci_ref, dbr_ref, dbi_ref, lr_ref, li_ref):
        dy = dy_ref[...]
        lr_ref[...] = _dot_nt(dy, cr_ref[...])
        li_ref[...] = _dot_nt(dy, ci_ref[...])
        a_re = jnp.broadcast_to(ar_ref[...], (SCAN_CHUNKS, SLAB_STATE))
        a_im = -jnp.broadcast_to(ai_ref[...], (SCAN_CHUNKS, SLAB_STATE))
        dar_ref[...], dai_ref[...] = _reverse_scan_in_place(lr_ref, li_ref, hr_ref, hi_ref, er_ref[...], ei_ref[...],
                                                            a_re, a_im)
        dcr_ref[...] = _diag_blocks(dy, hr_ref[...].astype(BF16))
        dci_ref[...] = _diag_blocks(dy, hi_ref[...].astype(BF16))
        lam_r, lam_i = lr_ref[...].astype(BF16), li_ref[...].astype(BF16)
        uu = u_ref[...]
        dbr_ref[...] = _diag_blocks(uu, lam_r)
        dbi_ref[...] = _diag_blocks(uu, lam_i)
        du = skip_ref[...] + _dot_nt(lam_r, br_ref[...]) + _dot_nt(lam_i, bi_ref[...])
        du_ref[...] = du.astype(BF16)

    tok, state, b_in, c_out, vec, ent = _slab_specs()
    db = pl.BlockSpec((SLAB_IN, SSM_STATE), lambda j: (j, 0))
    return pl.pallas_call(
        body, grid=(SSM_SLABS,), in_specs=[tok, tok, tok, state, state, ent, ent, b_in, b_in, c_out, c_out, vec, vec],
        out_specs=[tok, vec, vec, db, db, db, db],
        out_shape=[_sds((SEQ, SSM_WIDTH), BF16), _sds((1, SSM_LANES)), _sds((1, SSM_LANES))]
        + [_sds((SSM_WIDTH, SSM_STATE))] * 4,
        scratch_shapes=[pltpu.VMEM((SEQ, SLAB_STATE), F32)] * 2,
        compiler_params=_cp(("parallel",)), name="ssm_backward")(
            d_y, d_u_skip, u, h_r, h_i, e_r, e_i, b_in_r, b_in_i, c_out_r, c_out_ni, a_r, a_i)


FF_ROWS = 1024
FF_SHARD = D_FF // N_CHIPS


def _dot_nt(a, b):
    return lax.dot_general(a, b, (((1,), (1,)), ((), ())), preferred_element_type=F32)


def _ffn_up(h, w_gate_t, w_up_t):
    def body(h_ref, wg_ref, wu_ref, a_ref, b_ref, act_ref):
        hb = h_ref[...].astype(BF16)
        a = _dot_nt(hb, wg_ref[...])
        b = _dot_nt(hb, wu_ref[...])
        a_ref[...] = a
        b_ref[...] = b
        act_ref[...] = (a * jax.nn.sigmoid(a) * b).astype(BF16)

    w_spec = pl.BlockSpec((None, FF_SHARD, D_MODEL), lambda i, k: (k, 0, 0))
    o_spec = pl.BlockSpec((None, FF_ROWS, FF_SHARD), lambda i, k: (k, i, 0))
    shape = (N_CHIPS, SEQ, FF_SHARD)
    return pl.pallas_call(
        body, grid=(SEQ // FF_ROWS, N_CHIPS),
        in_specs=[pl.BlockSpec((FF_ROWS, D_MODEL), lambda i, k: (i, 0)), w_spec, w_spec],
        out_specs=[o_spec, o_spec, o_spec], out_shape=[_sds(shape), _sds(shape), _sds(shape, BF16)],
        compiler_params=_cp(("parallel", "parallel")), name="ffn_up")(h, w_gate_t, w_up_t)


def _ffn_down_ln2_loss(act, w_down, h, tgt, ln_g, ln_b):
    def body(act_ref, w_ref, h_ref, tgt_ref, g_ref, b_ref, dz_ref, loss_ref, dg_ref, db_ref, acc):
        i, k = pl.program_id(0), pl.program_id(1)
        part = jnp.dot(act_ref[...], w_ref[...], preferred_element_type=F32)

        @pl.when(k == 0)
        def _():
            acc[...] = part

        @pl.when(k > 0)
        def _():
            acc[...] += part

        @pl.when(k == N_CHIPS - 1)
        def _():
            g = g_ref[...]
            xhat, rstd = _ln_stats(DN_ALPHA * h_ref[...] + acc[...])
            err = xhat * g + b_ref[...] - tgt_ref[...]
            d_out = err * (1.0 / D_MODEL)
            dz_ref[...] = _ln_bwd(d_out, xhat, rstd, g)
            loss_rows = jnp.sum(err * err, axis=-1, keepdims=True) * (0.5 / D_MODEL)
            sums = (jnp.broadcast_to(jnp.sum(loss_rows, axis=0, keepdims=True), loss_ref.shape),
                    _colsum(d_out * xhat), _colsum(d_out))
            for ref, val in zip((loss_ref, dg_ref, db_ref), sums):
                @pl.when(i == 0)
                def _(ref=ref, val=val):
                    ref[...] = val

                @pl.when(i > 0)
                def _(ref=ref, val=val):
                    ref[...] += val

    row = pl.BlockSpec((FF_ROWS, D_MODEL), lambda i, k: (i, 0))
    vec = pl.BlockSpec((1, D_MODEL), lambda i, k: (0, 0))
    return pl.pallas_call(
        body, grid=(SEQ // FF_ROWS, N_CHIPS),
        in_specs=[pl.BlockSpec((None, FF_ROWS, FF_SHARD), lambda i, k: (k, i, 0)),
                  pl.BlockSpec((None, FF_SHARD, D_MODEL), lambda i, k: (k, 0, 0)), row, row, vec, vec],
        out_specs=[row, pl.BlockSpec((1, BLOCK), lambda i, k: (0, 0)), vec, vec],
        out_shape=[_sds((SEQ, D_MODEL)), _sds((1, BLOCK)), _sds((1, D_MODEL)), _sds((1, D_MODEL))],
        scratch_shapes=[pltpu.VMEM((FF_ROWS, D_MODEL), F32)],
        compiler_params=_cp(("arbitrary", "arbitrary")), name="ffn_down_ln2_loss")(act, w_down, h, tgt, ln_g, ln_b)


def _ffn_down_bwd(dz, w_down, a, b):
    def body(dz_ref, wd_ref, a_ref, b_ref, da_ref, db_ref):
        d_act = _dot_nt(dz_ref[...].astype(BF16), wd_ref[...])
        av = a_ref[...]
        sg = jax.nn.sigmoid(av)
        da_ref[...] = (d_act * b_ref[...] * sg * (1.0 + av * (1.0 - sg))).astype(BF16)
        db_ref[...] = (d_act * av * sg).astype(BF16)

    t_spec = pl.BlockSpec((None, FF_ROWS, FF_SHARD), lambda i, k: (k, i, 0))
    shape = (N_CHIPS, SEQ, FF_SHARD)
    return pl.pallas_call(
        body, grid=(SEQ // FF_ROWS, N_CHIPS),
        in_specs=[pl.BlockSpec((FF_ROWS, D_MODEL), lambda i, k: (i, 0)),
                  pl.BlockSpec((None, FF_SHARD, D_MODEL), lambda i, k: (k, 0, 0)), t_spec, t_spec],
        out_specs=[t_spec, t_spec], out_shape=[_sds(shape, BF16), _sds(shape, BF16)],
        compiler_params=_cp(("parallel", "parallel")), name="ffn_down_bwd")(dz, w_down, a, b)


def _ffn_dh(d_a, d_b, w_gate_t, w_up_t):
    def body(da_ref, db_ref, wg_ref, wu_ref, o_ref, acc):
        k = pl.program_id(1)
        part = (jnp.dot(da_ref[...], wg_ref[...], preferred_element_type=F32)
                + jnp.dot(db_ref[...], wu_ref[...], preferred_element_type=F32))

        @pl.when(k == 0)
        def _():
            acc[...] = part

        @pl.when(k > 0)
        def _():
            acc[...] += part

        @pl.when(k == N_CHIPS - 1)
        def _():
            o_ref[...] = acc[...]

    t_spec = pl.BlockSpec((None, FF_ROWS, FF_SHARD), lambda i, k: (k, i, 0))
    w_spec = pl.BlockSpec((None, FF_SHARD, D_MODEL), lambda i, k: (k, 0, 0))
    return pl.pallas_call(
        body, grid=(SEQ // FF_ROWS, N_CHIPS), in_specs=[t_spec, t_spec, w_spec, w_spec],
        out_specs=pl.BlockSpec((FF_ROWS, D_MODEL), lambda i, k: (i, 0)), out_shape=_sds((SEQ, D_MODEL)),
        scratch_shapes=[pltpu.VMEM((FF_ROWS, D_MODEL), F32)],
        compiler_params=_cp(("parallel", "arbitrary")), name="ffn_dh")(d_a, d_b, w_gate_t, w_up_t)


def _local_step(x, tgt, wts, small):
    s = SEQ
    cos_f, sin_s = [_to_phase_rows(t) for t in _rope_tables()]
    x = _reorder_rows(x, to_phase=True, name="phase_rows_x")
    tgt = _reorder_rows(tgt, to_phase=True, name="phase_rows_target")

    proj = _project_in(x, wts["w_in"], cos_f, sin_s)

    attn, lse = _attention_fwd(proj)

    (abar_r, abar_i, bbar_r, bbar_i), ssm_vjp = jax.vjp(
        _ssm_discretise, small["ssm_a_re"], small["ssm_a_im"], small["ssm_log_dt"], small["ssm_b_re"], small["ssm_b_im"])
    b_in_r, b_in_i = [_slab_block_diag(b.transpose(0, 2, 1)).astype(BF16) for b in (bbar_r, bbar_i)]
    c_out_r = _slab_block_diag(small["ssm_c_re"].transpose(0, 2, 1)).astype(BF16)
    c_out_ni = _slab_block_diag(-small["ssm_c_im"].transpose(0, 2, 1)).astype(BF16)
    a_r, a_i = abar_r.reshape(1, SSM_LANES), abar_i.reshape(1, SSM_LANES)
    d_skip = small["ssm_d"].reshape(1, SSM_WIDTH)

    u_f = _to_scan_rows(proj[:, 3 * QKV_WIDTH:3 * QKV_WIDTH + SSM_WIDTH])
    u_p = u_f.astype(BF16)
    y_c, h_r, h_i, e_r, e_i = _ssm_forward(u_p, b_in_r, b_in_i, c_out_r, c_out_ni, a_r, a_i)

    def branch(t, wg):
        return jnp.concatenate([jnp.dot(t, wg[k], preferred_element_type=F32) for k in range(N_CHIPS)], axis=1)

    def branch_t(t, wg):
        ns = wg.shape[2]
        return sum(_dot_nt(t[:, k * ns:(k + 1) * ns], wg[k]) for k in range(N_CHIPS))

    def gelu_glu(yc, u, dsk, wg):
        y = yc + dsk * u
        gel = (0.5 * y * (1.0 + jnp.tanh(GELU_C * (y + GELU_K * y * y * y)))).astype(BF16)
        glu = branch(gel, wg)
        return y, gel, glu, glu[:, :SSM_WIDTH] * jax.nn.sigmoid(glu[:, SSM_WIDTH:])

    y_s5, gel, glu, y_glu = _rowwise(
        gelu_glu, [y_c, u_f], [d_skip, wts["w_glu"]],
        [_sds((s, SSM_WIDTH)), _sds((s, SSM_WIDTH), BF16), _sds((s, 2 * SSM_WIDTH)), _sds((s, SSM_WIDTH), BF16)],
        tm=512, name="ssm_gelu_glu")
    y_glu = _from_scan_rows(y_glu)

    gl0 = (proj, D_MODEL, (3 * QKV_WIDTH + SSM_WIDTH) // D_MODEL)
    gl1 = (proj, D_MODEL, (3 * QKV_WIDTH + SSM_WIDTH) // D_MODEL + 1)
    b_gate = small["b_gate"]
    w_out = wts["w_out"].reshape(D_MODEL, D_MODEL)

    def mix_ln1(l0, l1, at, yg, xv, bg, wa, ws, wo, g, b):
        ya = branch(at.astype(BF16), wa)
        ys = branch(yg, ws)
        mixed = (jax.nn.sigmoid(l0 + bg[0:1]) * ya + jax.nn.sigmoid(l1 + bg[1:2]) * ys).astype(BF16)
        z = DN_ALPHA * xv + jnp.dot(mixed, wo, preferred_element_type=F32)
        xhat, _ = _ln_stats(z)
        return ya, ys, mixed, z, xhat * g + b

    y_attn, y_ssm, mixed, z1, h = _rowwise(
        mix_ln1, [gl0, gl1, attn, y_glu, x],
        [b_gate, wts["w_attn_br"], wts["w_ssm_br"], w_out, small["ln1_g"], small["ln1_b"]],
        [_sds((s, D_MODEL)), _sds((s, D_MODEL)), _sds((s, D_MODEL), BF16), _sds((s, D_MODEL)), _sds((s, D_MODEL))],
        tm=256, name="mix_ln1")

    nf = D_FF // N_CHIPS
    w_gate_t, w_up_t, w_down = wts["w_ff_gate"], wts["w_ff_up"], wts["w_ff_down"]
    ff_a, ff_b, act = _ffn_up(h, w_gate_t, w_up_t)
    dz2, loss_v, d_ln2_g, d_ln2_b = _ffn_down_ln2_loss(act, w_down, h, tgt, small["ln2_g"], small["ln2_b"])

    d_a, d_b = _ffn_down_bwd(dz2, w_down, ff_a, ff_b)

    def grad_rows(lhs, rhs, name):
        return _matmul(lhs, rhs, grid=(N_CHIPS,), a_spec=pl.BlockSpec((None, s, nf), lambda k: (k, 0, 0)),
                       b_spec=pl.BlockSpec((s, D_MODEL), lambda k: (0, 0)),
                       o_spec=pl.BlockSpec((None, nf, D_MODEL), lambda k: (k, 0, 0)),
                       out_shape=_sds((N_CHIPS, nf, D_MODEL), BF16), dims=(0, 0), name=name)

    g_w_ff_down = grad_rows(act, dz2, "g_w_ff_down")
    g_w_ff_gate = grad_rows(d_a, h, "g_w_ff_gate")
    g_w_ff_up = grad_rows(d_b, h, "g_w_ff_up")
    dh_ff = _ffn_dh(d_a, d_b, w_gate_t, w_up_t)

    def ln1_gate_bwd(dz, dff, z, l0, l1, ya, ys, g, bg, wo, wa, ws):
        xhat, rstd = _ln_stats(z)
        dh = DN_ALPHA * dz + dff
        dz_in = _ln_bwd(dh, xhat, rstd, g)
        dm = _dot_nt(dz_in.astype(BF16), wo)
        g0 = jax.nn.sigmoid(l0 + bg[0:1])
        g1 = jax.nn.sigmoid(l1 + bg[1:2])
        dl0 = dm * ya * g0 * (1.0 - g0)
        dl1 = dm * ys * g1 * (1.0 - g1)
        dya, dys = (dm * g0).astype(BF16), (dm * g1).astype(BF16)
        return (dz_in, dya, dys, jnp.concatenate([dl0, dl1], axis=1), branch_t(dya, wa), branch_t(dys, ws),
                _colsum(dh * xhat), _colsum(dh), _colsum(dl0), _colsum(dl1))

    dz1, d_y_attn, d_y_ssm, d_gl, d_attn, d_y_glu, d_ln1_g, d_ln1_b, d_bg0, d_bg1 = _rowwise(
        ln1_gate_bwd, [dz2, dh_ff, z1, gl0, gl1, y_attn, y_ssm],
        [small["ln1_g"], b_gate, w_out, wts["w_attn_br"], wts["w_ssm_br"]],
        [_sds((s, D_MODEL)), _sds((s, D_MODEL), BF16), _sds((s, D_MODEL), BF16), _sds((s, 2 * D_MODEL), BF16),
         _sds((s, ATTN_WIDTH)), _sds((s, SSM_WIDTH))],
        [_sds((1, D_MODEL))] * 4, tm=256, name="ln1_gate_bwd")
    g_w_out = _mm_rows_tn(mixed, dz1, name="g_w_out")

    g_w_ssm_br = _mm_cols_tn(y_glu, d_y_ssm, ns=D_MODEL // N_CHIPS, name="g_w_ssm_br")
    d_y_glu = _to_scan_rows(d_y_glu)

    def glu_gelu_bwd(dyg, gl, y, u, dsk, wg):
        ga, gb = gl[:, :SSM_WIDTH], gl[:, SSM_WIDTH:]
        sg = jax.nn.sigmoid(gb)
        d_gl = jnp.concatenate([dyg * sg, dyg * ga * sg * (1.0 - sg)], axis=1).astype(BF16)
        dg = branch_t(d_gl, wg)
        th = jnp.tanh(GELU_C * (y + GELU_K * y * y * y))
        dy = dg * (0.5 * (1.0 + th) + 0.5 * y * (1.0 - th * th) * GELU_C * (1.0 + 3.0 * GELU_K * y * y))
        return d_gl, dy, dy * dsk, _colsum(dy * u)

    d_glu, d_y, d_u_skip, d_ssm_d = _rowwise(
        glu_gelu_bwd, [d_y_glu, glu, y_s5, u_f], [d_skip, wts["w_glu"]],
        [_sds((s, 2 * SSM_WIDTH), BF16), _sds((s, SSM_WIDTH), BF16), _sds((s, SSM_WIDTH))], [_sds((1, SSM_WIDTH))],
        tm=512, name="glu_gelu_bwd")
    g_w_glu = _mm_cols_tn(gel, d_glu, ns=2 * SSM_WIDTH // N_CHIPS, name="g_w_glu")
    d_u, d_abar_r, d_abar_i, d_c_r, d_c_ni, d_bin_r, d_bin_i = _ssm_backward(
        d_y, d_u_skip, u_p, h_r, h_i, e_r, e_i, b_in_r, b_in_i, c_out_r, c_out_ni, a_r, a_i)
    d_u = _from_scan_rows(d_u)
    d_bbar_r = d_bin_r.reshape(SSM_GROUPS, SSM_GROUP, SSM_STATE).transpose(0, 2, 1)
    d_bbar_i = d_bin_i.reshape(SSM_GROUPS, SSM_GROUP, SSM_STATE).transpose(0, 2, 1)
    d_a_re, d_a_im, d_log_dt, d_b_re, d_b_im = ssm_vjp(
        (d_abar_r.reshape(SSM_GROUPS, SSM_STATE), d_abar_i.reshape(SSM_GROUPS, SSM_STATE), d_bbar_r, d_bbar_i))
    d_c_re = d_c_r.reshape(SSM_GROUPS, SSM_GROUP, SSM_STATE)
    d_c_im = -d_c_ni.reshape(SSM_GROUPS, SSM_GROUP, SSM_STATE)

    g_w_attn_br = _mm_cols_tn(attn, d_y_attn, ns=D_MODEL // N_CHIPS, name="g_w_attn_br")
    dqkv = [_attention_bwd(g, proj, cos_f, sin_s, d_attn, attn, lse) for g in range(len(DILATIONS))]

    d_proj = jnp.concatenate([dqkv[g][j] for j in range(3) for g in range(len(DILATIONS))] + [d_u, d_gl],
                             axis=1)
    g_w_in = _mm_cols_tn(x, d_proj, ns=IN_WIDTH // N_CHIPS, name="g_w_in")

    def grad_x_after(after):
        dx_proj = _mm_cols_nt(d_proj, wts["w_in"], tm=1024, name="dx_proj", after=after)
        return _reorder_rows(dz1, dx_proj, to_phase=False, name="grad_x", scale=DN_ALPHA)

    big = {"w_in": g_w_in, "w_attn_br": g_w_attn_br, "w_ssm_br": g_w_ssm_br, "w_out": g_w_out, "w_glu": g_w_glu,
           "w_ff_gate": g_w_ff_gate, "w_ff_up": g_w_ff_up, "w_ff_down": g_w_ff_down}
    small_g = {"b_gate": jnp.concatenate([d_bg0, d_bg1], axis=0), "ssm_a_re": d_a_re, "ssm_a_im": d_a_im,
               "ssm_log_dt": d_log_dt, "ssm_b_re": d_b_re, "ssm_b_im": d_b_im, "ssm_c_re": d_c_re, "ssm_c_im": d_c_im,
               "ssm_d": d_ssm_d.reshape(SSM_WIDTH), "ln1_g": d_ln1_g, "ln1_b": d_ln1_b, "ln2_g": d_ln2_g,
               "ln2_b": d_ln2_b}
    marks = {"ln1_bwd": dz1, "scan_bwd": d_abar_r, "attention_bwd_0": dqkv[0][0]}
    return loss_v[0, 0], grad_x_after, big, small_g, marks


GATHER_ID, SWAP_ID, SCATTER_ID, JOIN_ID, EXCHANGE_ID = 1, 2, 3, 4, 5


def _place():
    return lax.axis_index("x"), lax.axis_index("y"), lax.axis_index("c")


def _other_chips(x, y):
    return [(1 - x, y), (x, 1 - y), (1 - x, 1 - y)]


def _handshake(peers):
    barrier = pltpu.get_barrier_semaphore()
    for peer in peers:
        pl.semaphore_signal(barrier, inc=1, device_id=peer, device_id_type=MESH)
    pl.semaphore_wait(barrier, len(peers))


def _sequencer(body, arrays, out_type, sems, collective_id, name):
    return pl.kernel(body, name=name, out_type=out_type,
                     mesh=plsc.ScalarSubcoreMesh(axis_name="sequencer", num_cores=1), scratch_types=sems,
                     compiler_params=pltpu.CompilerParams(collective_id=collective_id))(*arrays)


def _gather_weights(shards, *, name):
    nw = len(shards)

    def body(*refs):
        ins, outs = refs[:nw], refs[nw:2 * nw]
        send_sems, recv_sems, pass_send, pass_recv, local_sems = refs[2 * nw:]
        x, y, c = _place()
        chip = 2 * x + y
        chips = _other_chips(x, y)
        _handshake([(x, y, 1 - c)] + [(cx, cy, c) for cx, cy in chips])
        started = []
        for w in range(nw):
            hw = shards[w].shape[0] // 2
            mine = pl.ds(c * hw, hw)
            own = pltpu.make_async_copy(ins[w], outs[w].at[chip], local_sems.at[w])
            own.start()
            started.append(own)
            for j, (cx, cy) in enumerate(chips):
                cp = pltpu.make_async_remote_copy(
                    src_ref=ins[w].at[mine], dst_ref=outs[w].at[chip, mine], send_sem=send_sems.at[w, j],
                    recv_sem=recv_sems.at[w, j], device_id=(cx, cy, c), device_id_type=MESH)
                cp.start()
                started.append(cp)
        passed = []
        for w in range(nw):
            hw = shards[w].shape[0] // 2
            mine = pl.ds(c * hw, hw)
            for j, (cx, cy) in enumerate(chips):
                landed = outs[w].at[2 * cx + cy, mine]
                pltpu.make_async_remote_copy(
                    src_ref=ins[w].at[mine], dst_ref=landed, send_sem=send_sems.at[w, j],
                    recv_sem=recv_sems.at[w, j], device_id=(cx, cy, c), device_id_type=MESH).wait_recv()
                cp = pltpu.make_async_remote_copy(
                    src_ref=landed, dst_ref=landed, send_sem=pass_send.at[w, j], recv_sem=pass_recv.at[w, j],
                    device_id=(x, y, 1 - c), device_id_type=MESH)
                cp.start()
                passed.append(cp)
        for w in range(nw):
            hw = shards[w].shape[0] // 2
            theirs = pl.ds((1 - c) * hw, hw)
            for j, (cx, cy) in enumerate(chips):
                landed = outs[w].at[2 * cx + cy, theirs]
                pltpu.make_async_remote_copy(
                    src_ref=landed, dst_ref=landed, send_sem=pass_send.at[w, j], recv_sem=pass_recv.at[w, j],
                    device_id=(x, y, 1 - c), device_id_type=MESH).wait_recv()
        for cp in started[0::4]:
            cp.wait()
        for cp in [s for i, s in enumerate(started) if i % 4] + passed:
            cp.wait_send()

    sem = pltpu.SemaphoreType.DMA
    return _sequencer(body, shards, [_sds((N_CHIPS,) + a.shape, a.dtype) for a in shards],
                      [sem((nw, 3)), sem((nw, 3)), sem((nw, 3)), sem((nw, 3)), sem((nw,))], GATHER_ID, name)


def _swap_other_halves(grads, *, name):
    nw = len(grads)

    def body(*refs):
        ins, outs = refs[:nw], refs[nw:2 * nw]
        send_sems, recv_sems = refs[2 * nw:]
        x, y, c = _place()
        _handshake([(x, y, 1 - c)])
        cps = []
        for w in range(nw):
            hw = grads[w].shape[1] // 2
            cp = pltpu.make_async_remote_copy(
                src_ref=ins[w].at[:, pl.ds((1 - c) * hw, hw)], dst_ref=outs[w], send_sem=send_sems.at[w],
                recv_sem=recv_sems.at[w], device_id=(x, y, 1 - c), device_id_type=MESH)
            cp.start()
            cps.append(cp)
        for cp in cps:
            cp.wait()

    sem = pltpu.SemaphoreType.DMA
    return _sequencer(body, grads, [_sds((N_CHIPS, g.shape[1] // 2, g.shape[2]), g.dtype) for g in grads],
                      [sem((nw,)), sem((nw,))], SWAP_ID, name)


def _add_my_halves(core, grads, others, *, name, after=()):
    nw = len(grads)
    halves = [g.shape[1] // 2 for g in grads]

    def body(core_ref, *refs):
        outs = refs[2 * nw + len(after):]
        for g_ref, o_ref, out_ref in zip(refs[:nw], refs[nw:2 * nw], outs):
            out_ref[...] = (g_ref[...].astype(F32) + o_ref[...].astype(F32)).astype(out_ref.dtype)

    in_specs = [pl.BlockSpec((None, None, hw, g.shape[2]), lambda s, core_ref: (s, core_ref[0], 0, 0))
                for g, hw in zip(grads, halves)]
    in_specs += [pl.BlockSpec((None, hw, g.shape[2]), lambda s, core_ref: (s, 0, 0)) for g, hw in zip(grads, halves)]
    return pl.pallas_call(
        body,
        grid_spec=pltpu.PrefetchScalarGridSpec(
            num_scalar_prefetch=1, grid=(N_CHIPS,), in_specs=in_specs + [HBM_OPERAND] * len(after),
            out_specs=[pl.BlockSpec((None, hw, g.shape[2]), lambda s, core_ref: (s, 0, 0))
                       for g, hw in zip(grads, halves)]),
        out_shape=[_sds((N_CHIPS, hw, g.shape[2]), BF16) for g, hw in zip(grads, halves)],
        compiler_params=_cp(("parallel",)), name=name)(
            core, *[g.reshape(N_CHIPS, 2, hw, g.shape[2]) for g, hw in zip(grads, halves)], *others, *after)


def _scatter_partials(parts, *, name):
    nw = len(parts)

    def body(*refs):
        ins, outs = refs[:nw], refs[nw:2 * nw]
        send_sems, recv_sems = refs[2 * nw:]
        x, y, c = _place()
        _handshake([(cx, cy, c) for cx, cy in _other_chips(x, y)])
        cps = []
        for w in range(nw):
            for j, (cx, cy) in enumerate(_other_chips(x, y)):
                cp = pltpu.make_async_remote_copy(
                    src_ref=ins[w].at[2 * cx + cy], dst_ref=outs[w].at[j], send_sem=send_sems.at[w, j],
                    recv_sem=recv_sems.at[w, j], device_id=(cx, cy, c), device_id_type=MESH)
                cp.start()
                cps.append(cp)
        for cp in cps:
            cp.wait()

    sem = pltpu.SemaphoreType.DMA
    return _sequencer(body, parts, [_sds((3,) + p.shape[1:], p.dtype) for p in parts],
                      [sem((nw, 3)), sem((nw, 3))], SCATTER_ID, name)


SUM_STEPS = 2


def _sum_partials(chip, parts, recvd, *, name, after=()):
    nw = len(parts)
    rows = [p.shape[1] // SUM_STEPS for p in parts]

    def body(chip_ref, *refs):
        outs = refs[2 * nw + len(after):]
        for p_ref, r_ref, out_ref in zip(refs[:nw], refs[nw:2 * nw], outs):
            acc = p_ref[...].astype(F32)
            for j in range(3):
                acc = acc + r_ref[j].astype(F32)
            out_ref[...] = acc

    in_specs = [pl.BlockSpec((None, th, p.shape[2]), lambda i, chip_ref: (chip_ref[0], i, 0))
                for p, th in zip(parts, rows)]
    in_specs += [pl.BlockSpec((3, th, p.shape[2]), lambda i, chip_ref: (0, i, 0)) for p, th in zip(parts, rows)]
    return pl.pallas_call(
        body,
        grid_spec=pltpu.PrefetchScalarGridSpec(
            num_scalar_prefetch=1, grid=(SUM_STEPS,), in_specs=in_specs + [HBM_OPERAND] * len(after),
            out_specs=[pl.BlockSpec((th, p.shape[2]), lambda i, chip_ref: (i, 0)) for p, th in zip(parts, rows)]),
        out_shape=[_sds(p.shape[1:]) for p in parts], compiler_params=_cp(("parallel",)), name=name)(
            chip, *parts, *recvd, *after)


def _swap_reduced_halves(halves, *, name):
    nw = len(halves)

    def body(*refs):
        ins, outs = refs[:nw], refs[nw:2 * nw]
        send_sems, recv_sems = refs[2 * nw:]
        x, y, c = _place()
        _handshake([(x, y, 1 - c)])
        cps = []
        for w in range(nw):
            cp = pltpu.make_async_remote_copy(
                src_ref=ins[w], dst_ref=outs[w], send_sem=send_sems.at[w], recv_sem=recv_sems.at[w],
                device_id=(x, y, 1 - c), device_id_type=MESH)
            cp.start()
            cps.append(cp)
        for cp in cps:
            cp.wait()

    sem = pltpu.SemaphoreType.DMA
    return _sequencer(body, halves, [_sds(h.shape, h.dtype) for h in halves], [sem((nw,)), sem((nw,))], JOIN_ID, name)


def _exchange_rows(vec, *, name):
    def body(v_ref, slots, send_sems, recv_sems, local_sem):
        x, y, c = _place()
        me = 4 * x + 2 * y + c
        peers = []
        for mask in range(1, N_DEV):
            peers.append((1 - x if mask & 4 else x, 1 - y if mask & 2 else y, 1 - c if mask & 1 else c))
        _handshake(peers)
        own = pltpu.make_async_copy(v_ref, slots.at[me], local_sem)
        own.start()
        cps = []
        for k, peer in enumerate(peers):
            cp = pltpu.make_async_remote_copy(
                src_ref=v_ref, dst_ref=slots.at[me], send_sem=send_sems.at[k], recv_sem=recv_sems.at[k],
                device_id=peer, device_id_type=MESH)
            cp.start()
            cps.append(cp)
        for k, (px, py, pc) in enumerate(peers):
            pltpu.make_async_remote_copy(
                src_ref=v_ref, dst_ref=slots.at[4 * px + 2 * py + pc], send_sem=send_sems.at[k],
                recv_sem=recv_sems.at[k], device_id=(px, py, pc), device_id_type=MESH).wait_recv()
        for cp in cps:
            cp.wait_send()
        own.wait()

    sem = pltpu.SemaphoreType.DMA
    return _sequencer(body, [vec], [_sds((N_DEV,) + vec.shape)], [sem((N_DEV - 1,)), sem((N_DEV - 1,)), sem(())],
                      EXCHANGE_ID, name)[0]


def _sum_slots(slots, *, name, after=()):
    def body(s_ref, *rest):
        out_ref = rest[len(after)]
        acc = s_ref[0]
        for d in range(1, N_DEV):
            acc = acc + s_ref[d]
        out_ref[...] = acc

    vmem = pl.BlockSpec(memory_space=pltpu.VMEM)
    return pl.pallas_call(
        body, in_specs=[vmem] + [HBM_OPERAND] * len(after), out_specs=vmem, out_shape=_sds(slots.shape[1:]),
        compiler_params=pltpu.CompilerParams(vmem_limit_bytes=VMEM_LIMIT_BYTES), name=name)(slots, *after)


def _reduce_scatter_start(grads, core, *, tag, add_after=()):
    others = _swap_other_halves(grads, name="swap_other_halves_" + tag)
    parts = _add_my_halves(core, grads, others, name="add_my_halves_" + tag, after=add_after)
    return parts, _scatter_partials(parts, name="scatter_partials_" + tag)


def _reduce_scatter_finish(parts, recvd, chip, *, tag, sum_after=()):
    mine = _sum_partials(chip, parts, recvd, name="sum_partials_" + tag, after=sum_after)
    return mine, _swap_reduced_halves(mine, name="swap_reduced_halves_" + tag)


ADAM_BLOCK_ELEMS = 256 * 1024


def _adam_rows(rows, cols):
    tm = rows
    while tm * cols > ADAM_BLOCK_ELEMS and tm % 16 == 0:
        tm //= 2
    return tm


def _adam_step(wv, gv, mv, vv):
    m2 = ADAM_B1 * mv + (1.0 - ADAM_B1) * gv
    v2 = ADAM_B2 * vv + (1.0 - ADAM_B2) * (gv * gv)
    m_hat = m2 / (1.0 - ADAM_B1 ** ADAM_STEP)
    v_hat = v2 / (1.0 - ADAM_B2 ** ADAM_STEP)
    return -ADAM_LR * (m_hat / (jnp.sqrt(v_hat) + ADAM_EPS) + ADAM_WD * wv), m2, v2


def _adamw(w, g, m, v, *, name):
    rows, cols = w.shape
    return _rowwise(_adam_step, [w, g, m, v], [], [_sds((rows, cols))] * 3, tm=_adam_rows(rows, cols), name=name)


def _adamw_halves(core, w, g_mine, g_theirs, m, v, *, name, after=()):
    rows, cols = w.shape
    hw = rows // 2
    tm = _adam_rows(hw, cols)
    per_half = hw // tm

    def body(core_ref, w_ref, gm_ref, gt_ref, m_ref, v_ref, *rest):
        g_out, d_out, m_out, v_out = rest[len(after):]
        mine = (pl.program_id(0) // per_half) == core_ref[0]
        g = jnp.where(mine, gm_ref[...], gt_ref[...])
        d, m2, v2 = _adam_step(w_ref[...], g, m_ref[...], v_ref[...])
        g_out[...] = g
        d_out[...] = d
        m_out[...] = m2
        v_out[...] = v2

    full = pl.BlockSpec((tm, cols), lambda i, core_ref: (i, 0))

    def half(wanted):
        def index(i, core_ref):
            in_use = ((i // per_half) == core_ref[0]) == wanted
            return (jnp.where(in_use, i % per_half, 0), 0)
        return pl.BlockSpec((tm, cols), index)

    return pl.pallas_call(
        body,
        grid_spec=pltpu.PrefetchScalarGridSpec(
            num_scalar_prefetch=1, grid=(rows // tm,),
            in_specs=[full, half(True), half(False), full, full] + [HBM_OPERAND] * len(after),
            out_specs=[full, full, full, full]),
        out_shape=[_sds((rows, cols))] * 4, compiler_params=_cp(("parallel",)), name=name)(
            core, w, g_mine, g_theirs, m, v, *after)


HELD_TRANSPOSED = ("w_ff_gate", "w_ff_up")


def _as_rows(name, arr):
    return arr[0].T if name in HELD_TRANSPOSED else arr[0]


def _from_rows(name, arr2d):
    return (arr2d.T if name in HELD_TRANSPOSED else arr2d)[None]


STORED_SWAPPED = ("ssm_b_re", "ssm_b_im")


def _as_stored(name, arr):
    return jnp.swapaxes(arr, -1, -2) if name in STORED_SWAPPED else arr


def _pack_rows(arrs):
    flat = jnp.concatenate([a.reshape(-1).astype(F32) for a in arrs])
    rows = -(-flat.shape[0] // 1024) * 8
    return jnp.pad(flat, (0, rows * 128 - flat.shape[0])).reshape(rows, 128)


def _unpack_rows(vec, shapes):
    flat = vec.reshape(-1)
    out, off = [], 0
    for shp in shapes:
        size = math.prod(shp)
        out.append(flat[off:off + size].reshape(shp))
        off += size
    return out


SMALL = ("b_gate", "ssm_a_re", "ssm_a_im", "ssm_log_dt", "ssm_b_re", "ssm_b_im", "ssm_c_re", "ssm_c_im", "ssm_d",
         "ln1_g", "ln1_b", "ln2_g", "ln2_b")
GATHER_GROUPS = (("w_in", ("w_in",)), ("mixer", ("w_attn_br", "w_ssm_br", "w_glu", "w_out")),
                 ("ffn", ("w_ff_gate", "w_ff_up", "w_ff_down")))
REDUCE_GROUPS = (("ffn", ("w_ff_down", "w_ff_gate", "w_ff_up")),
                 ("mixer", ("w_out", "w_ssm_br", "w_glu", "w_attn_br")), ("w_in", ("w_in",)))
WEIGHTS = ("w_in", "b_gate", "w_attn_br", "w_ssm_br", "w_out", "ssm_a_re", "ssm_a_im", "ssm_log_dt", "ssm_b_re",
           "ssm_b_im", "ssm_c_re", "ssm_c_im", "ssm_d", "w_glu", "ln1_g", "ln1_b", "w_ff_gate", "w_ff_up", "w_ff_down",
           "ln2_g", "ln2_b")


def kernel(x, w_in, b_gate, w_attn_br, w_ssm_br, w_out, ssm_a_re, ssm_a_im, ssm_log_dt, ssm_b_re, ssm_b_im, ssm_c_re, ssm_c_im, ssm_d, w_glu, ln1_g, ln1_b, w_ff_gate, w_ff_up, w_ff_down, ln2_g, ln2_b, loss_target, m_w_in, m_b_gate, m_w_attn_br, m_w_ssm_br, m_w_out, m_ssm_a_re, m_ssm_a_im, m_ssm_log_dt, m_ssm_b_re, m_ssm_b_im, m_ssm_c_re, m_ssm_c_im, m_ssm_d, m_w_glu, m_ln1_g, m_ln1_b, m_w_ff_gate, m_w_ff_up, m_w_ff_down, m_ln2_g, m_ln2_b, v_w_in, v_b_gate, v_w_attn_br, v_w_ssm_br, v_w_out, v_ssm_a_re, v_ssm_a_im, v_ssm_log_dt, v_ssm_b_re, v_ssm_b_im, v_ssm_c_re, v_ssm_c_im, v_ssm_d, v_w_glu, v_ln1_g, v_ln1_b, v_w_ff_gate, v_w_ff_up, v_w_ff_down, v_ln2_g, v_ln2_b):
    given = dict(locals())
    px, py, pc = _place()
    chip = 2 * px + py
    core_s = jnp.reshape(pc, (1,)).astype(jnp.int32)
    chip_s = jnp.reshape(chip, (1,)).astype(jnp.int32)

    wts = {}
    for tag, names in GATHER_GROUPS:
        wts.update(zip(names, _gather_weights([_as_rows(n, given[n]).astype(BF16) for n in names],
                                              name="gather_" + tag)))
    ncol = D_MODEL // N_CHIPS
    bg_mine = jnp.where(pc == 0, b_gate[0], jnp.zeros_like(b_gate[0]))
    bg_full = lax.dynamic_update_slice(jnp.zeros((2, D_MODEL), F32), bg_mine, (0, chip * ncol))
    bg_slots = _exchange_rows(bg_full.reshape(16, 128), name="exchange_gate_bias")
    bg_full = _sum_slots(bg_slots, name="sum_gate_bias").reshape(2, D_MODEL)
    small = {n: given[n][0] for n in SMALL if n.startswith("ssm")}
    small.update({n: given[n] for n in ("ln1_g", "ln1_b", "ln2_g", "ln2_b")})
    small["b_gate"] = bg_full

    loss_mine, grad_x_after, big_g, small_g, marks = _local_step(x[0], loss_target[0], wts, small)

    groups = dict(REDUCE_GROUPS)
    parts, recvd = {}, {}
    grads, delta, new_m, new_v = {}, {}, {}, {}

    def start(tag, add_after):
        parts[tag], recvd[tag] = _reduce_scatter_start([big_g[n] for n in groups[tag]], core_s, tag=tag,
                                                       add_after=add_after)

    def finish(tag, sum_after, adam_after):
        mine, theirs = _reduce_scatter_finish(parts[tag], recvd[tag], chip_s, tag=tag, sum_after=sum_after)
        for n, g_mine, g_theirs in zip(groups[tag], mine, theirs):
            res = _adamw_halves(core_s, _as_rows(n, given[n]), g_mine, g_theirs, _as_rows(n, given["m_" + n]),
                                _as_rows(n, given["v_" + n]), name="adamw_" + n, after=adam_after)
            grads[n], delta[n], new_m[n], new_v[n] = [_from_rows(n, r) for r in res]

    start("ffn", (marks["ln1_bwd"],))
    start("mixer", (marks["scan_bwd"],))
    finish("mixer", (marks["attention_bwd_0"],), (big_g["w_in"],))
    start("w_in", tuple(delta[n] for n in groups["mixer"]))
    in_flight = (parts["w_in"][0],)
    grad_x = grad_x_after(in_flight)
    finish("ffn", (marks["scan_bwd"],), in_flight)
    stored = [_as_stored(n, small_g[n]) for n in SMALL] + [loss_mine.reshape(1)]
    slots = _exchange_rows(_pack_rows(stored), name="exchange_small")
    summed = _unpack_rows(_sum_slots(slots, name="sum_small", after=in_flight), [a.shape for a in stored])
    loss = summed.pop()[0]
    for n, g in zip(SMALL, summed):
        g = _as_stored(n, g)
        if n == "b_gate":
            g = lax.dynamic_slice(g, (0, chip * ncol), (2, ncol))
        grads[n] = g.reshape(given[n].shape)
    packed = [_pack_rows([_as_stored(n, src[n]) for n in SMALL]) for src in
              (given, grads, {n: given["m_" + n] for n in SMALL}, {n: given["v_" + n] for n in SMALL})]
    shapes = [_as_stored(n, given[n]).shape for n in SMALL]
    small_out = _adamw(*packed, name="adamw_small")
    for out, vec in zip((delta, new_m, new_v), small_out):
        out.update((n, _as_stored(n, a)) for n, a in zip(SMALL, _unpack_rows(vec, shapes)))
    behind = [delta[n] for n in groups["ffn"]] + [small_out[0], grad_x]
    finish("w_in", tuple(behind), ())

    return (loss, grad_x.reshape(x.shape), *[grads[n] for n in WEIGHTS], *[delta[n] for n in WEIGHTS],
            *[new_m[n] for n in WEIGHTS], *[new_v[n] for n in WEIGHTS])
```

```python
import math

import jax
import jax.numpy as jnp
from jax import lax
from jax.experimental import pallas as pl
from jax.experimental.pallas import tpu as pltpu
from jax.experimental.pallas import tpu_sc as plsc

F32 = jnp.float32
BF16 = jnp.bfloat16
MESH = pl.DeviceIdType.MESH

D_MODEL = 1024
SEQ = 2048
HEAD_DIM = 64
ATTN_HEADS = 8
DILATIONS = (1, 4, 16)
ATTN_WIDTH = ATTN_HEADS * HEAD_DIM
QKV_WIDTH = 3 * ATTN_WIDTH
BLOCK = 128
ROPE_THETA = 10000.0
NEG_INF = -1e30
SSM_GROUP = 16
SSM_GROUPS = 32
SSM_WIDTH = 512
SSM_STATE = 64
SSM_LANES = SSM_GROUPS * SSM_STATE
SCAN_CHUNKS = 8
SCAN_STEPS = SEQ // SCAN_CHUNKS
IN_WIDTH = 3 * QKV_WIDTH + SSM_WIDTH + 2 * D_MODEL
D_FF = 2816
N_CHIPS = 4
N_DEV = 8
DN_ALPHA = 2.0 ** 0.25
LN_EPS = 1e-5
ADAM_LR = 0.001
ADAM_B1 = 0.9
ADAM_B2 = 0.999
ADAM_EPS = 1e-08
ADAM_WD = 0.01
ADAM_STEP = 10
GELU_C = math.sqrt(2.0 / math.pi)
GELU_K = 0.044715

VMEM_LIMIT_BYTES = 56 * 1024 * 1024


def _sds(shape, dtype=F32):
    return jax.ShapeDtypeStruct(tuple(shape), dtype)


def _cp(semantics=None):
    return pltpu.CompilerParams(dimension_semantics=semantics, vmem_limit_bytes=VMEM_LIMIT_BYTES)


HBM_OPERAND = pl.BlockSpec(memory_space=pl.ANY)


def _matmul(a, b, *, grid, a_spec, b_spec, o_spec, out_shape, dims, k_axis=None, name, after=()):
    nk = grid[k_axis] if k_axis is not None else 1
    o_block = tuple(d for d in o_spec.block_shape if d is not None)
    n_after = len(after)

    def body(a_ref, b_ref, *rest):
        o_ref, acc = rest[n_after], rest[n_after + 1:]
        part = lax.dot_general(a_ref[...].astype(BF16), b_ref[...].astype(BF16),
                               (((dims[0],), (dims[1],)), ((), ())), preferred_element_type=F32)
        if k_axis is None:
            o_ref[...] = part.astype(o_ref.dtype)
        else:
            k = pl.program_id(k_axis)

            @pl.when(k == 0)
            def _():
                acc[0][...] = part

            @pl.when(k > 0)
            def _():
                acc[0][...] += part

            @pl.when(k == nk - 1)
            def _():
                o_ref[...] = acc[0][...].astype(o_ref.dtype)

    sem = tuple("arbitrary" if ax == k_axis else "parallel" for ax in range(len(grid)))
    return pl.pallas_call(
        body, grid=grid, in_specs=[a_spec, b_spec] + [HBM_OPERAND] * n_after, out_specs=o_spec, out_shape=out_shape,
        scratch_shapes=[pltpu.VMEM(o_block, F32)] if k_axis is not None else [],
        compiler_params=_cp(sem), name=name)(a, b, *after)


def _mm_cols_nt(dy, wg, *, tm, name, out_dtype=F32, after=()):
    k, ns = wg.shape[1], wg.shape[2]
    m = dy.shape[0]
    a_spec = pl.BlockSpec((tm, ns), lambda i, s: (i, s))
    return _matmul(dy, wg, grid=(m // tm, N_CHIPS), a_spec=a_spec,
                   b_spec=pl.BlockSpec((None, k, ns), lambda i, s: (s, 0, 0)),
                   o_spec=pl.BlockSpec((tm, k), lambda i, s: (i, 0)),
                   out_shape=_sds((m, k), out_dtype), dims=(1, 1), k_axis=1, name=name, after=after)


def _mm_cols_tn(a, dy, *, ns, name, after=()):
    m, k = a.shape
    return _matmul(a, dy, grid=(N_CHIPS,), a_spec=pl.BlockSpec((m, k), lambda s: (0, 0)),
                   b_spec=pl.BlockSpec((m, ns), lambda s: (0, s)),
                   o_spec=pl.BlockSpec((None, k, ns), lambda s: (s, 0, 0)),
                   out_shape=_sds((N_CHIPS, k, ns), BF16), dims=(0, 0), name=name, after=after)


def _mm_rows_tn(a, dy, *, name):
    m, k = a.shape
    rows, n = k // N_CHIPS, dy.shape[1]
    return _matmul(a, dy, grid=(N_CHIPS,), a_spec=pl.BlockSpec((m, rows), lambda s: (0, s)),
                   b_spec=pl.BlockSpec((m, n), lambda s: (0, 0)),
                   o_spec=pl.BlockSpec((None, rows, n), lambda s: (s, 0, 0)),
                   out_shape=_sds((N_CHIPS, rows, n), BF16), dims=(0, 0), name=name)


def _rowwise(fn, tiled, full, outs, accs=(), *, tm, name, after=()):
    args, in_specs = [], []
    for t in tiled:
        if isinstance(t, tuple):
            arr, w, cb = t
            in_specs.append(pl.BlockSpec((tm, w), lambda i, cb=cb: (i, cb)))
        else:
            arr = t
            in_specs.append(pl.BlockSpec((tm, arr.shape[1]), lambda i: (i, 0)))
        args.append(arr)
    rows = args[0].shape[0]
    for f in full:
        in_specs.append(pl.BlockSpec(f.shape, lambda i, nd=f.ndim: (0,) * nd))
        args.append(f)
    out_specs = [pl.BlockSpec((tm, o.shape[1]), lambda i: (i, 0)) for o in outs]
    out_specs += [pl.BlockSpec(a.shape, lambda i, nd=len(a.shape): (0,) * nd) for a in accs]
    n_in, n_out = len(args), len(outs)
    in_specs += [HBM_OPERAND] * len(after)
    first_out = n_in + len(after)

    def body(*refs):
        res = fn(*[r[...] for r in refs[:n_in]])
        res = res if isinstance(res, (tuple, list)) else (res,)
        for r, v in zip(refs[first_out:first_out + n_out], res[:n_out]):
            r[...] = v.astype(r.dtype)
        i = pl.program_id(0)
        for r, v in zip(refs[first_out + n_out:], res[n_out:]):
            @pl.when(i == 0)
            def _(r=r, v=v):
                r[...] = v

            @pl.when(i > 0)
            def _(r=r, v=v):
                r[...] += v

    res = pl.pallas_call(
        body, grid=(rows // tm,), in_specs=in_specs, out_specs=out_specs, out_shape=list(outs) + list(accs),
        compiler_params=_cp(("arbitrary",) if accs else ("parallel",)), name=name)(*args, *after)
    return res


def _colsum(v):
    return jnp.sum(v, axis=0, keepdims=True)


def _ln_stats(z):
    mu = jnp.mean(z, axis=-1, keepdims=True)
    zc = z - mu
    var = jnp.mean(zc * zc, axis=-1, keepdims=True)
    rstd = lax.rsqrt(var + LN_EPS)
    return zc * rstd, rstd


def _ln_bwd(dy, xhat, rstd, g):
    dxh = dy * g
    m1 = jnp.mean(dxh, axis=-1, keepdims=True)
    m2 = jnp.mean(dxh * xhat, axis=-1, keepdims=True)
    return rstd * (dxh - m1 - xhat * m2)


def _swap_halves(t):
    w = t.shape[-1]
    lane = lax.broadcasted_iota(jnp.int32, t.shape, t.ndim - 1)
    return jnp.where((lane % HEAD_DIM) < HEAD_DIM // 2, pltpu.roll(t, w - HEAD_DIM // 2, t.ndim - 1),
                     pltpu.roll(t, HEAD_DIM // 2, t.ndim - 1))


PHASES = max(DILATIONS)
PAIR = 2 * HEAD_DIM
UNITS = SEQ // BLOCK
UNIT_BATCH = 8
ROPE_ROWS = 256


def _to_phase_rows(t):
    return t.reshape(SEQ // PHASES, PHASES, t.shape[1]).transpose(1, 0, 2).reshape(t.shape)


def _reorder_rows(arr, plus=None, *, to_phase, name, scale=1.0):
    def body(*refs):
        o_ref = refs[-1]
        for rho in range(PHASES):
            phase = pl.ds(rho * BLOCK, BLOCK)
            strided = pl.ds(rho, BLOCK, stride=PHASES)
            src, dst = (strided, phase) if to_phase else (phase, strided)
            val = refs[0][src, :]
            if scale != 1.0:
                val = val * scale
            if plus is not None:
                val = val + refs[1][src, :]
            o_ref[dst, :] = val

    spec = pl.BlockSpec((SEQ, BLOCK), lambda j: (0, j))
    ins = [arr] if plus is None else [arr, plus]
    return pl.pallas_call(body, grid=(arr.shape[1] // BLOCK,), in_specs=[spec] * len(ins), out_specs=spec,
                          out_shape=_sds(arr.shape), compiler_params=_cp(("parallel",)), name=name)(*ins)


def _rope(t, cf, ss):
    return t * cf + _swap_halves(t) * ss


def _rope_transposed(d, cf, ss):
    return d * cf + _swap_halves(d * ss)


def _unit_pieces(u, dil):
    pieces, length = PHASES // dil, 8 * dil
    if dil == 1:
        rho, i = 0, u
    elif dil == PHASES:
        rho, i = u, 0
    else:
        rho, i = jnp.bitwise_and(u, dil - 1), jnp.right_shift(u, dil.bit_length() - 1)
    before = jnp.maximum(i - 1, 0)
    cur = [pl.multiple_of((rho + dil * k) * BLOCK + length * i, 8) for k in range(pieces)]
    prev = [pl.multiple_of((rho + dil * k) * BLOCK + length * before, 8) for k in range(pieces)]
    return i, cur, prev


def _load_tile(ref, starts, dil):
    return jnp.concatenate([ref[pl.ds(st, 8 * dil), :] for st in starts], axis=0)


def _store_tile(ref, starts, dil, val, head=None, accumulate=False):
    length = 8 * dil
    lanes = slice(None) if head is None else pl.ds(head * HEAD_DIM, HEAD_DIM)
    cols = slice(None) if head is None else slice(head * HEAD_DIM, (head + 1) * HEAD_DIM)
    for k, st in enumerate(starts):
        piece = val[k * length:(k + 1) * length, cols]
        if accumulate:
            ref[pl.ds(st, length), lanes] += piece
        else:
            ref[pl.ds(st, length), lanes] = piece


def _tile_position(idx, dil):
    pieces, length = PHASES // dil, 8 * dil
    return pieces * jnp.bitwise_and(idx, length - 1) + jnp.right_shift(idx, length.bit_length() - 1)


def _band_mask(i, dil):
    row = lax.broadcasted_iota(jnp.int32, (BLOCK, 2 * BLOCK), 0)
    col = lax.broadcasted_iota(jnp.int32, (BLOCK, 2 * BLOCK), 1)
    key_pos = _tile_position(jnp.bitwise_and(col, BLOCK - 1), dil) + jnp.where(col >= BLOCK, 0, -BLOCK)
    dist = _tile_position(row, dil) - key_pos
    return (dist >= 0) & (dist <= BLOCK) & ((col >= BLOCK) | (i > 0))


def _causal_mask():
    row = lax.broadcasted_iota(jnp.int32, (BLOCK, BLOCK), 0)
    col = lax.broadcasted_iota(jnp.int32, (BLOCK, BLOCK), 1)
    return row >= col


def _pair_views(col0):
    return [pl.BlockSpec((SEQ, PAIR), lambda hp, g=g: (0, col0 // PAIR + g * (ATTN_WIDTH // PAIR) + hp))
            for g in range(len(DILATIONS))]


def _project_in(x, wg, cos_f, sin_s):
    ns = wg.shape[2]
    tiles = ns // PAIR

    def body(x_ref, w_ref, cf_ref, ss_ref, o_ref):
        shard = pl.program_id(1)
        xb = x_ref[...].astype(BF16)
        cf, ss = cf_ref[...], ss_ref[...]

        def write(rotated, scaled):
            for t0 in range(0, tiles, 2):
                strip = jnp.dot(xb, w_ref[:, t0 * PAIR:(t0 + 2) * PAIR], preferred_element_type=F32)
                for t in (t0, t0 + 1):
                    val = strip[:, (t - t0) * PAIR:(t - t0 + 1) * PAIR]
                    if t < rotated:
                        val = _rope(val, cf, ss)
                        if t < scaled:
                            val = val * (1.0 / math.sqrt(HEAD_DIM))
                    o_ref[:, t * PAIR:(t + 1) * PAIR] = val

        for s in range(N_CHIPS):
            rotated = min(max(2 * QKV_WIDTH - s * ns, 0), ns) // PAIR
            scaled = min(max(QKV_WIDTH - s * ns, 0), ns) // PAIR

            @pl.when(shard == s)
            def _(rotated=rotated, scaled=scaled):
                write(rotated, scaled)

    table = pl.BlockSpec((FF_ROWS, PAIR), lambda i, s: (i, 0))
    return pl.pallas_call(
        body, grid=(SEQ // FF_ROWS, N_CHIPS),
        in_specs=[pl.BlockSpec((FF_ROWS, D_MODEL), lambda i, s: (i, 0)),
                  pl.BlockSpec((None, D_MODEL, ns), lambda i, s: (s, 0, 0)), table, table],
        out_specs=pl.BlockSpec((FF_ROWS, ns), lambda i, s: (i, s)), out_shape=_sds((SEQ, N_CHIPS * ns)),
        compiler_params=_cp(("parallel", "parallel")), name="project_in")(x, wg, cos_f, sin_s)


def _attention_fwd(proj):
    ng = len(DILATIONS)

    def body(*refs):
        q_refs, k_refs, v_refs = refs[:ng], refs[ng:2 * ng], refs[2 * ng:3 * ng]
        attn_ref, lse_ref = refs[3 * ng:]
        qr_refs, kr_refs = q_refs, k_refs
        first = lax.broadcasted_iota(jnp.int32, (BLOCK, PAIR), 1) < HEAD_DIM
        for g, dil in enumerate(DILATIONS):
            two_blocks = SEQ // dil > BLOCK

            def units(t, carry, g=g, dil=dil, two_blocks=two_blocks):
                picked = [_unit_pieces(t * UNIT_BATCH + j, dil) for j in range(UNIT_BATCH)]

                def tiles(ref, with_prev=False):
                    if with_prev and two_blocks:
                        return jnp.stack([jnp.concatenate([_load_tile(ref, prev, dil), _load_tile(ref, rows, dil)],
                                                          axis=0) for _, rows, prev in picked])
                    return jnp.stack([_load_tile(ref, rows, dil) for _, rows, _ in picked])

                qq = tiles(qr_refs[g]).astype(BF16)
                kk = tiles(kr_refs[g], True).astype(BF16)
                vv = tiles(v_refs[g], True).astype(BF16)
                if two_blocks:
                    valid = jnp.stack([_band_mask(i, dil) for i, _, _ in picked])
                else:
                    valid = _causal_mask()[None]
                mine = first[None]
                zero = jnp.zeros_like(qq)
                outs, lses = [], []
                for qh in (jnp.where(mine, qq, zero), jnp.where(mine, zero, qq)):
                    s = jnp.einsum("pqd,pkd->pqk", qh, kk, preferred_element_type=F32)
                    s = jnp.where(valid, s, NEG_INF)
                    m = jnp.max(s, axis=-1, keepdims=True)
                    p = jnp.exp(s - m)
                    l = jnp.sum(p, axis=-1, keepdims=True)
                    outs.append(jnp.einsum("pqk,pkd->pqd", p.astype(BF16), vv, preferred_element_type=F32) * (1.0 / l))
                    lses.append(m + jnp.log(l))
                o = jnp.where(mine, outs[0], outs[1])
                lse = jnp.where(mine, lses[0], lses[1])
                if g > 0:
                    lse_old = tiles(lse_ref)
                    m = jnp.maximum(lse_old, lse)
                    lse_new = m + jnp.log(jnp.exp(lse_old - m) + jnp.exp(lse - m))
                    o = tiles(attn_ref) * jnp.exp(lse_old - lse_new) + o * jnp.exp(lse - lse_new)
                    lse = lse_new
                for j, (_, rows, _) in enumerate(picked):
                    _store_tile(attn_ref, rows, dil, o[j])
                    _store_tile(lse_ref, rows, dil, lse[j])
                return carry

            lax.fori_loop(0, UNITS // UNIT_BATCH, units, 0)

    out = pl.BlockSpec((SEQ, PAIR), lambda hp: (0, hp))
    return pl.pallas_call(
        body, grid=(ATTN_WIDTH // PAIR,),
        in_specs=_pair_views(0) + _pair_views(QKV_WIDTH) + _pair_views(2 * QKV_WIDTH),
        out_specs=[out, out], out_shape=[_sds((SEQ, ATTN_WIDTH)), _sds((SEQ, ATTN_WIDTH))],
        compiler_params=_cp(("parallel",)), name="attention_fwd")(*([proj] * (3 * ng)))


def _attention_bwd(g, proj, cos_f, sin_s, d_attn, attn, lse):
    dil = DILATIONS[g]
    two_blocks = SEQ // dil > BLOCK

    def body(qr_ref, kr_ref, v_ref, cf_ref, ss_ref, do_ref, o_ref, lse_ref, dq_out, dk_out, dv_out,
             dq_acc, dk_acc, dv_acc):
        dk_acc[...] = jnp.zeros_like(dk_acc)
        dv_acc[...] = jnp.zeros_like(dv_acc)
        nk = 2 * BLOCK if two_blocks else BLOCK
        first = lax.broadcasted_iota(jnp.int32, (BLOCK, PAIR), 1) < HEAD_DIM
        first_k = lax.broadcasted_iota(jnp.int32, (nk, PAIR), 1) < HEAD_DIM

        def units(t, carry):
            picked = [_unit_pieces(t * UNIT_BATCH + j, dil) for j in range(UNIT_BATCH)]

            def tiles(ref, with_prev=False):
                if with_prev and two_blocks:
                    return jnp.stack([jnp.concatenate([_load_tile(ref, prev, dil), _load_tile(ref, rows, dil)], axis=0)
                                      for _, rows, prev in picked])
                return jnp.stack([_load_tile(ref, rows, dil) for _, rows, _ in picked])

            qq = tiles(qr_ref).astype(BF16)
            kk = tiles(kr_ref, True).astype(BF16)
            vv = tiles(v_ref, True).astype(BF16)
            dof = tiles(do_ref)
            dd = dof * tiles(o_ref)
            lse3 = tiles(lse_ref)
            dob = dof.astype(BF16)
            if two_blocks:
                valid = jnp.stack([_band_mask(i, dil) for i, _, _ in picked])
            else:
                valid = _causal_mask()[None]
            zq, zf = jnp.zeros_like(qq), jnp.zeros_like(dd)
            dqs, dks, dvs = [], [], []
            for head in range(2):
                mine = first[None] if head == 0 else jnp.logical_not(first)[None]
                delta = jnp.sum(jnp.where(mine, dd, zf), axis=-1, keepdims=True)
                lse_h = lse3[:, :, head * HEAD_DIM:head * HEAD_DIM + 1]
                s = jnp.einsum("pqd,pkd->pqk", jnp.where(mine, qq, zq), kk, preferred_element_type=F32)
                p = jnp.where(valid, jnp.exp(s - lse_h), 0.0)
                dp = jnp.einsum("pqd,pkd->pqk", jnp.where(mine, dob, zq), vv, preferred_element_type=F32)
                ds = (p * (dp - delta)).astype(BF16)
                dqs.append(jnp.einsum("pqk,pkd->pqd", ds, kk, preferred_element_type=F32))
                dks.append(jnp.einsum("pqk,pqd->pkd", ds, qq, preferred_element_type=F32))
                dvs.append(jnp.einsum("pqk,pqd->pkd", p.astype(BF16), dob, preferred_element_type=F32))
            dq = jnp.where(first[None], dqs[0], dqs[1])
            dk = jnp.where(first_k[None], dks[0], dks[1])
            dv = jnp.where(first_k[None], dvs[0], dvs[1])
            for j, (_, rows, prev) in enumerate(picked):
                _store_tile(dq_acc, rows, dil, dq[j])
                _store_tile(dk_acc, rows, dil, dk[j, nk - BLOCK:], accumulate=True)
                _store_tile(dv_acc, rows, dil, dv[j, nk - BLOCK:], accumulate=True)
                if two_blocks:
                    _store_tile(dk_acc, prev, dil, dk[j, :BLOCK], accumulate=True)
                    _store_tile(dv_acc, prev, dil, dv[j, :BLOCK], accumulate=True)
            return carry

        lax.fori_loop(0, UNITS // UNIT_BATCH, units, 0)

        def finish(t, carry):
            rows = pl.ds(pl.multiple_of(t * ROPE_ROWS, ROPE_ROWS), ROPE_ROWS)
            cf, ss = cf_ref[rows, :], ss_ref[rows, :]
            dq = dq_acc[rows, :] * (1.0 / math.sqrt(HEAD_DIM))
            dq_out[rows, :] = _rope_transposed(dq, cf, ss).astype(BF16)
            dk_out[rows, :] = _rope_transposed(dk_acc[rows, :], cf, ss).astype(BF16)
            dv_out[rows, :] = dv_acc[rows, :].astype(BF16)
            return carry

        lax.fori_loop(0, SEQ // ROPE_ROWS, finish, 0)

    whole = pl.BlockSpec((SEQ, PAIR), lambda hp: (0, 0))
    pair = pl.BlockSpec((SEQ, PAIR), lambda hp: (0, hp))
    views = [_pair_views(col0)[g] for col0 in (0, QKV_WIDTH, 2 * QKV_WIDTH)]
    return pl.pallas_call(
        body, grid=(ATTN_WIDTH // PAIR,), in_specs=views + [whole, whole, pair, pair, pair],
        out_specs=[pair, pair, pair], out_shape=[_sds((SEQ, ATTN_WIDTH), BF16)] * 3,
        scratch_shapes=[pltpu.VMEM((SEQ, PAIR), F32)] * 3,
        compiler_params=_cp(("parallel",)), name=f"attention_bwd_{g}")(proj, proj, proj, cos_f, sin_s, d_attn, attn, lse)


def _cmul(ar, ai, br, bi):
    return ar * br - ai * bi, ar * bi + ai * br


def _pow256(ar, ai):
    for _ in range(8):
        ar, ai = _cmul(ar, ai, ar, ai)
    return ar, ai


def _chunk_carries(first_r, first_i, pr, pi, reverse):
    rows = lax.broadcasted_iota(jnp.int32, first_r.shape, 0)
    out_r = jnp.zeros_like(first_r)
    out_i = jnp.zeros_like(first_i)
    hr = jnp.zeros_like(first_r[0:1])
    hi = jnp.zeros_like(hr)
    order = range(SCAN_CHUNKS - 1, -1, -1) if reverse else range(SCAN_CHUNKS)
    for c in order:
        out_r = jnp.where(rows == c, hr, out_r)
        out_i = jnp.where(rows == c, hi, out_i)
        tr, ti = _cmul(pr[0:1], pi[0:1], hr, hi)
        hr = first_r[c:c + 1] + tr
        hi = first_i[c:c + 1] + ti
    return out_r, out_i


def _tile(j):
    return pl.ds(pl.multiple_of(j * SCAN_CHUNKS, SCAN_CHUNKS), SCAN_CHUNKS)


def _to_scan_rows(t):
    per = SCAN_STEPS // PHASES
    return t.reshape(PHASES, SCAN_CHUNKS, per, t.shape[1]).transpose(2, 0, 1, 3).reshape(t.shape)


def _from_scan_rows(t):
    per = SCAN_STEPS // PHASES
    return t.reshape(per, PHASES, SCAN_CHUNKS, t.shape[1]).transpose(1, 2, 0, 3).reshape(t.shape)


def _scan_in_place(hr_ref, hi_ref, a_r, a_i):
    def local(j, carry):
        tr, ti = _cmul(a_r, a_i, carry[0], carry[1])
        nr = tr + hr_ref[_tile(j), :]
        ni = ti + hi_ref[_tile(j), :]
        hr_ref[_tile(j), :] = nr
        hi_ref[_tile(j), :] = ni
        return nr, ni

    zero = jnp.zeros_like(a_r)
    last_r, last_i = lax.fori_loop(0, SCAN_STEPS, local, (zero, zero), unroll=4)
    pr, pi = _pow256(a_r, a_i)
    er, ei = _chunk_carries(last_r, last_i, pr, pi, reverse=False)

    def fix(j, carry):
        tr, ti = _cmul(carry[0], carry[1], er, ei)
        hr_ref[_tile(j), :] += tr
        hi_ref[_tile(j), :] += ti
        return _cmul(carry[0], carry[1], a_r, a_i)

    lax.fori_loop(0, SCAN_STEPS, fix, (a_r, a_i), unroll=4)
    return er, ei


def _reverse_scan_in_place(lr_ref, li_ref, hr_ref, hi_ref, er, ei, a_r, a_i):
    def local(t, carry):
        j = SCAN_STEPS - 1 - t
        tr, ti = _cmul(a_r, a_i, carry[0], carry[1])
        nr = tr + lr_ref[_tile(j), :]
        ni = ti + li_ref[_tile(j), :]
        lr_ref[_tile(j), :] = nr
        li_ref[_tile(j), :] = ni
        return nr, ni

    zero = jnp.zeros_like(a_r)
    first_r, first_i = lax.fori_loop(0, SCAN_STEPS, local, (zero, zero), unroll=4)
    pr, pi = _pow256(a_r, a_i)
    nxt_r, nxt_i = _chunk_carries(first_r, first_i, pr, pi, reverse=True)

    def accumulate(lam_r, lam_i, hp_r, hp_i, acc):
        return (acc[0] + lam_r * hp_r + lam_i * hp_i, acc[1] + lam_i * hp_r - lam_r * hp_i)

    def fix(t, carry):
        qr, qi, acc_r, acc_i = carry
        j = SCAN_STEPS - 1 - t
        tr, ti = _cmul(qr, qi, nxt_r, nxt_i)
        lam_r = lr_ref[_tile(j), :] + tr
        lam_i = li_ref[_tile(j), :] + ti
        lr_ref[_tile(j), :] = lam_r
        li_ref[_tile(j), :] = lam_i
        acc_r, acc_i = accumulate(lam_r, lam_i, hr_ref[_tile(j - 1), :], hi_ref[_tile(j - 1), :], (acc_r, acc_i))
        qr, qi = _cmul(qr, qi, a_r, a_i)
        return qr, qi, acc_r, acc_i

    qr, qi, acc_r, acc_i = lax.fori_loop(0, SCAN_STEPS - 1, fix, (a_r, a_i, zero, zero), unroll=4)
    tr, ti = _cmul(qr, qi, nxt_r, nxt_i)
    lam_r = lr_ref[_tile(0), :] + tr
    lam_i = li_ref[_tile(0), :] + ti
    lr_ref[_tile(0), :] = lam_r
    li_ref[_tile(0), :] = lam_i
    acc_r, acc_i = accumulate(lam_r, lam_i, er, ei, (acc_r, acc_i))
    return jnp.sum(acc_r, axis=0, keepdims=True), jnp.sum(acc_i, axis=0, keepdims=True)


def _rope_tables():
    half = HEAD_DIM // 2
    inv_freq = ROPE_THETA ** (-jnp.arange(half, dtype=F32) / half)
    ang = jnp.arange(SEQ, dtype=F32)[:, None] * inv_freq[None, :]
    cos, sin = jnp.cos(ang), jnp.sin(ang)
    cos_f = jnp.concatenate([cos, cos, cos, cos], axis=1)
    sin_s = jnp.concatenate([-sin, sin, -sin, sin], axis=1)
    return cos_f, sin_s


def _ssm_discretise(a_re, a_im, log_dt, b_re, b_im):
    lam = lax.complex(a_re, a_im)
    dt = jnp.exp(log_dt)[:, None]
    a_bar = jnp.exp(lam * dt)
    b_bar = ((a_bar - 1.0) / lam)[..., None] * lax.complex(b_re, b_im)
    return a_bar.real, a_bar.imag, b_bar.real, b_bar.imag


SSM_SLABS = 4
SLAB_GROUPS = SSM_GROUPS // SSM_SLABS
SLAB_IN = SSM_WIDTH // SSM_SLABS
SLAB_STATE = SSM_LANES // SSM_SLABS


def _slab_block_diag(blocks):
    _, r, c = blocks.shape
    eye = jnp.eye(SLAB_GROUPS, dtype=blocks.dtype)
    b5 = blocks.reshape(SSM_SLABS, SLAB_GROUPS, r, 1, c) * eye[None, :, None, :, None]
    return b5.reshape(SSM_SLABS, SLAB_GROUPS * r, SLAB_GROUPS * c)


def _diag_blocks(a, b):
    ra, cb = a.shape[1], b.shape[1]
    wa, wb = ra // SLAB_GROUPS, cb // SLAB_GROUPS
    d = lax.dot_general(a, b, (((0,), (0,)), ((), ())), preferred_element_type=F32)
    row_g = jnp.right_shift(lax.broadcasted_iota(jnp.int32, (ra, cb), 0), wa.bit_length() - 1)
    col_g = jnp.right_shift(lax.broadcasted_iota(jnp.int32, (ra, cb), 1), wb.bit_length() - 1)
    d = jnp.where(row_g == col_g, d, 0.0)
    fold = (jnp.bitwise_and(lax.broadcasted_iota(jnp.int32, (cb, wb), 0), wb - 1)
            == lax.broadcasted_iota(jnp.int32, (cb, wb), 1)).astype(F32)
    return jnp.dot(d, fold, preferred_element_type=F32, precision=lax.Precision.HIGHEST)


def _slab_specs():
    tok = pl.BlockSpec((SEQ, SLAB_IN), lambda j: (0, j))
    state = pl.BlockSpec((SEQ, SLAB_STATE), lambda j: (0, j))
    b_in = pl.BlockSpec((None, SLAB_IN, SLAB_STATE), lambda j: (j, 0, 0))
    c_out = pl.BlockSpec((None, SLAB_STATE, SLAB_IN), lambda j: (j, 0, 0))
    vec = pl.BlockSpec((1, SLAB_STATE), lambda j: (0, j))
    ent = pl.BlockSpec((SCAN_CHUNKS, SLAB_STATE), lambda j: (0, j))
    return tok, state, b_in, c_out, vec, ent


def _ssm_forward(u, b_in_r, b_in_i, c_out_r, c_out_ni, a_r, a_i):
    def body(u_ref, br_ref, bi_ref, cr_ref, ci_ref, ar_ref, ai_ref, y_ref, hr_ref, hi_ref, er_ref, ei_ref):
        uu = u_ref[...]
        hr_ref[...] = jnp.dot(uu, br_ref[...], preferred_element_type=F32)
        hi_ref[...] = jnp.dot(uu, bi_ref[...], preferred_element_type=F32)
        a_re = jnp.broadcast_to(ar_ref[...], (SCAN_CHUNKS, SLAB_STATE))
        a_im = jnp.broadcast_to(ai_ref[...], (SCAN_CHUNKS, SLAB_STATE))
        er_ref[...], ei_ref[...] = _scan_in_place(hr_ref, hi_ref, a_re, a_im)
        y_ref[...] = (jnp.dot(hr_ref[...].astype(BF16), cr_ref[...], preferred_element_type=F32)
                      + jnp.dot(hi_ref[...].astype(BF16), ci_ref[...], preferred_element_type=F32))

    tok, state, b_in, c_out, vec, ent = _slab_specs()
    return pl.pallas_call(
        body, grid=(SSM_SLABS,), in_specs=[tok, b_in, b_in, c_out, c_out, vec, vec],
        out_specs=[tok, state, state, ent, ent],
        out_shape=[_sds((SEQ, SSM_WIDTH)), _sds((SEQ, SSM_LANES)), _sds((SEQ, SSM_LANES)),
                   _sds((SCAN_CHUNKS, SSM_LANES)), _sds((SCAN_CHUNKS, SSM_LANES))],
        compiler_params=_cp(("parallel",)), name="ssm_forward")(u, b_in_r, b_in_i, c_out_r, c_out_ni, a_r, a_i)


def _ssm_backward(d_y, d_u_skip, u, h_r, h_i, e_r, e_i, b_in_r, b_in_i, c_out_r, c_out_ni, a_r, a_i):
    def body(dy_ref, skip_ref, u_ref, hr_ref, hi_ref, er_ref, ei_ref, br_ref, bi_ref, cr_ref, ci_ref, ar_ref, ai_ref,
             du_ref, dar_ref, dai_ref, dcr_ref, dci_ref, dbr_ref, dbi_ref, lr_ref, li_ref):
        dy = dy_ref[...]
        lr_ref[...] = _dot_nt(dy, cr_ref[...])
        li_ref[...] = _dot_nt(dy, ci_ref[...])
        a_re = jnp.broadcast_to(ar_ref[...], (SCAN_CHUNKS, SLAB_STATE))
        a_im = -jnp.broadcast_to(ai_ref[...], (SCAN_CHUNKS, SLAB_STATE))
        dar_ref[...], dai_ref[...] = _reverse_scan_in_place(lr_ref, li_ref, hr_ref, hi_ref, er_ref[...], ei_ref[...],
                                                            a_re, a_im)
        dcr_ref[...] = _diag_blocks(dy, hr_ref[...].astype(BF16))
        dci_ref[...] = _diag_blocks(dy, hi_ref[...].astype(BF16))
        lam_r, lam_i = lr_ref[...].astype(BF16), li_ref[...].astype(BF16)
        uu = u_ref[...]
        dbr_ref[...] = _diag_blocks(uu, lam_r)
        dbi_ref[...] = _diag_blocks(uu, lam_i)
        du = skip_ref[...] + _dot_nt(lam_r, br_ref[...]) + _dot_nt(lam_i, bi_ref[...])
        du_ref[...] = du.astype(BF16)

    tok, state, b_in, c_out, vec, ent = _slab_specs()
    db = pl.BlockSpec((SLAB_IN, SSM_STATE), lambda j: (j, 0))
    return pl.pallas_call(
        body, grid=(SSM_SLABS,), in_specs=[tok, tok, tok, state, state, ent, ent, b_in, b_in, c_out, c_out, vec, vec],
        out_specs=[tok, vec, vec, db, db, db, db],
        out_shape=[_sds((SEQ, SSM_WIDTH), BF16), _sds((1, SSM_LANES)), _sds((1, SSM_LANES))]
        + [_sds((SSM_WIDTH, SSM_STATE))] * 4,
        scratch_shapes=[pltpu.VMEM((SEQ, SLAB_STATE), F32)] * 2,
        compiler_params=_cp(("parallel",)), name="ssm_backward")(
            d_y, d_u_skip, u, h_r, h_i, e_r, e_i, b_in_r, b_in_i, c_out_r, c_out_ni, a_r, a_i)


FF_ROWS = 1024
FF_SHARD = D_FF // N_CHIPS


def _dot_nt(a, b):
    return lax.dot_general(a, b, (((1,), (1,)), ((), ())), preferred_element_type=F32)


def _ffn_up(h, w_gate_t, w_up_t):
    def body(h_ref, wg_ref, wu_ref, a_ref, b_ref, act_ref):
        hb = h_ref[...].astype(BF16)
        a = _dot_nt(hb, wg_ref[...])
        b = _dot_nt(hb, wu_ref[...])
        a_ref[...] = a
        b_ref[...] = b
        act_ref[...] = (a * jax.nn.sigmoid(a) * b).astype(BF16)

    w_spec = pl.BlockSpec((None, FF_SHARD, D_MODEL), lambda i, k: (k, 0, 0))
    o_spec = pl.BlockSpec((None, FF_ROWS, FF_SHARD), lambda i, k: (k, i, 0))
    shape = (N_CHIPS, SEQ, FF_SHARD)
    return pl.pallas_call(
        body, grid=(SEQ // FF_ROWS, N_CHIPS),
        in_specs=[pl.BlockSpec((FF_ROWS, D_MODEL), lambda i, k: (i, 0)), w_spec, w_spec],
        out_specs=[o_spec, o_spec, o_spec], out_shape=[_sds(shape), _sds(shape), _sds(shape, BF16)],
        compiler_params=_cp(("parallel", "parallel")), name="ffn_up")(h, w_gate_t, w_up_t)


def _ffn_down_ln2_loss(act, w_down, h, tgt, ln_g, ln_b):
    def body(act_ref, w_ref, h_ref, tgt_ref, g_ref, b_ref, dz_ref, loss_ref, dg_ref, db_ref, acc):
        i, k = pl.program_id(0), pl.program_id(1)
        part = jnp.dot(act_ref[...], w_ref[...], preferred_element_type=F32)

        @pl.when(k == 0)
        def _():
            acc[...] = part

        @pl.when(k > 0)
        def _():
            acc[...] += part

        @pl.when(k == N_CHIPS - 1)
        def _():
            g = g_ref[...]
            xhat, rstd = _ln_stats(DN_ALPHA * h_ref[...] + acc[...])
            err = xhat * g + b_ref[...] - tgt_ref[...]
            d_out = err * (1.0 / D_MODEL)
            dz_ref[...] = _ln_bwd(d_out, xhat, rstd, g)
            loss_rows = jnp.sum(err * err, axis=-1, keepdims=True) * (0.5 / D_MODEL)
            sums = (jnp.broadcast_to(jnp.sum(loss_rows, axis=0, keepdims=True), loss_ref.shape),
                    _colsum(d_out * xhat), _colsum(d_out))
            for ref, val in zip((loss_ref, dg_ref, db_ref), sums):
                @pl.when(i == 0)
                def _(ref=ref, val=val):
                    ref[...] = val

                @pl.when(i > 0)
                def _(ref=ref, val=val):
                    ref[...] += val

    row = pl.BlockSpec((FF_ROWS, D_MODEL), lambda i, k: (i, 0))
    vec = pl.BlockSpec((1, D_MODEL), lambda i, k: (0, 0))
    return pl.pallas_call(
        body, grid=(SEQ // FF_ROWS, N_CHIPS),
        in_specs=[pl.BlockSpec((None, FF_ROWS, FF_SHARD), lambda i, k: (k, i, 0)),
                  pl.BlockSpec((None, FF_SHARD, D_MODEL), lambda i, k: (k, 0, 0)), row, row, vec, vec],
        out_specs=[row, pl.BlockSpec((1, BLOCK), lambda i, k: (0, 0)), vec, vec],
        out_shape=[_sds((SEQ, D_MODEL)), _sds((1, BLOCK)), _sds((1, D_MODEL)), _sds((1, D_MODEL))],
        scratch_shapes=[pltpu.VMEM((FF_ROWS, D_MODEL), F32)],
        compiler_params=_cp(("arbitrary", "arbitrary")), name="ffn_down_ln2_loss")(act, w_down, h, tgt, ln_g, ln_b)


def _ffn_down_bwd(dz, w_down, a, b):
    def body(dz_ref, wd_ref, a_ref, b_ref, da_ref, db_ref):
        d_act = _dot_nt(dz_ref[...].astype(BF16), wd_ref[...])
        av = a_ref[...]
        sg = jax.nn.sigmoid(av)
        da_ref[...] = (d_act * b_ref[...] * sg * (1.0 + av * (1.0 - sg))).astype(BF16)
        db_ref[...] = (d_act * av * sg).astype(BF16)

    t_spec = pl.BlockSpec((None, FF_ROWS, FF_SHARD), lambda i, k: (k, i, 0))
    shape = (N_CHIPS, SEQ, FF_SHARD)
    return pl.pallas_call(
        body, grid=(SEQ // FF_ROWS, N_CHIPS),
        in_specs=[pl.BlockSpec((FF_ROWS, D_MODEL), lambda i, k: (i, 0)),
                  pl.BlockSpec((None, FF_SHARD, D_MODEL), lambda i, k: (k, 0, 0)), t_spec, t_spec],
        out_specs=[t_spec, t_spec], out_shape=[_sds(shape, BF16), _sds(shape, BF16)],
        compiler_params=_cp(("parallel", "parallel")), name="ffn_down_bwd")(dz, w_down, a, b)


def _ffn_dh(d_a, d_b, w_gate_t, w_up_t):
    def body(da_ref, db_ref, wg_ref, wu_ref, o_ref, acc):
        k = pl.program_id(1)
        part = (jnp.dot(da_ref[...], wg_ref[...], preferred_element_type=F32)
                + jnp.dot(db_ref[...], wu_ref[...], preferred_element_type=F32))

        @pl.when(k == 0)
        def _():
            acc[...] = part

        @pl.when(k > 0)
        def _():
            acc[...] += part

        @pl.when(k == N_CHIPS - 1)
        def _():
            o_ref[...] = acc[...]

    t_spec = pl.BlockSpec((None, FF_ROWS, FF_SHARD), lambda i, k: (k, i, 0))
    w_spec = pl.BlockSpec((None, FF_SHARD, D_MODEL), lambda i, k: (k, 0, 0))
    return pl.pallas_call(
        body, grid=(SEQ // FF_ROWS, N_CHIPS), in_specs=[t_spec, t_spec, w_spec, w_spec],
        out_specs=pl.BlockSpec((FF_ROWS, D_MODEL), lambda i, k: (i, 0)), out_shape=_sds((SEQ, D_MODEL)),
        scratch_shapes=[pltpu.VMEM((FF_ROWS, D_MODEL), F32)],
        compiler_params=_cp(("parallel", "arbitrary")), name="ffn_dh")(d_a, d_b, w_gate_t, w_up_t)


def _local_step(x, tgt, wts, small):
    s = SEQ
    cos_f, sin_s = [_to_phase_rows(t) for t in _rope_tables()]
    x = _reorder_rows(x, to_phase=True, name="phase_rows_x")
    tgt = _reorder_rows(tgt, to_phase=True, name="phase_rows_target")

    proj = _project_in(x, wts["w_in"], cos_f, sin_s)

    attn, lse = _attention_fwd(proj)

    (abar_r, abar_i, bbar_r, bbar_i), ssm_vjp = jax.vjp(
        _ssm_discretise, small["ssm_a_re"], small["ssm_a_im"], small["ssm_log_dt"], small["ssm_b_re"], small["ssm_b_im"])
    b_in_r, b_in_i = [_slab_block_diag(b.transpose(0, 2, 1)).astype(BF16) for b in (bbar_r, bbar_i)]
    c_out_r = _slab_block_diag(small["ssm_c_re"].transpose(0, 2, 1)).astype(BF16)
    c_out_ni = _slab_block_diag(-small["ssm_c_im"].transpose(0, 2, 1)).astype(BF16)
    a_r, a_i = abar_r.reshape(1, SSM_LANES), abar_i.reshape(1, SSM_LANES)
    d_skip = small["ssm_d"].reshape(1, SSM_WIDTH)

    u_f = _to_scan_rows(proj[:, 3 * QKV_WIDTH:3 * QKV_WIDTH + SSM_WIDTH])
    u_p = u_f.astype(BF16)
    y_c, h_r, h_i, e_r, e_i = _ssm_forward(u_p, b_in_r, b_in_i, c_out_r, c_out_ni, a_r, a_i)

    def branch(t, wg):
        return jnp.concatenate([jnp.dot(t, wg[k], preferred_element_type=F32) for k in range(N_CHIPS)], axis=1)

    def branch_t(t, wg):
        ns = wg.shape[2]
        return sum(_dot_nt(t[:, k * ns:(k + 1) * ns], wg[k]) for k in range(N_CHIPS))

    def gelu_glu(yc, u, dsk, wg):
        y = yc + dsk * u
        gel = (0.5 * y * (1.0 + jnp.tanh(GELU_C * (y + GELU_K * y * y * y)))).astype(BF16)
        glu = branch(gel, wg)
        return y, gel, glu, glu[:, :SSM_WIDTH] * jax.nn.sigmoid(glu[:, SSM_WIDTH:])

    y_s5, gel, glu, y_glu = _rowwise(
        gelu_glu, [y_c, u_f], [d_skip, wts["w_glu"]],
        [_sds((s, SSM_WIDTH)), _sds((s, SSM_WIDTH), BF16), _sds((s, 2 * SSM_WIDTH)), _sds((s, SSM_WIDTH), BF16)],
        tm=512, name="ssm_gelu_glu")
    y_glu = _from_scan_rows(y_glu)

    gl0 = (proj, D_MODEL, (3 * QKV_WIDTH + SSM_WIDTH) // D_MODEL)
    gl1 = (proj, D_MODEL, (3 * QKV_WIDTH + SSM_WIDTH) // D_MODEL + 1)
    b_gate = small["b_gate"]
    w_out = wts["w_out"].reshape(D_MODEL, D_MODEL)

    def mix_ln1(l0, l1, at, yg, xv, bg, wa, ws, wo, g, b):
        ya = branch(at.astype(BF16), wa)
        ys = branch(yg, ws)
        mixed = (jax.nn.sigmoid(l0 + bg[0:1]) * ya + jax.nn.sigmoid(l1 + bg[1:2]) * ys).astype(BF16)
        z = DN_ALPHA * xv + jnp.dot(mixed, wo, preferred_element_type=F32)
        xhat, _ = _ln_stats(z)
        return ya, ys, mixed, z, xhat * g + b

    y_attn, y_ssm, mixed, z1, h = _rowwise(
        mix_ln1, [gl0, gl1, attn, y_glu, x],
        [b_gate, wts["w_attn_br"], wts["w_ssm_br"], w_out, small["ln1_g"], small["ln1_b"]],
        [_sds((s, D_MODEL)), _sds((s, D_MODEL)), _sds((s, D_MODEL), BF16), _sds((s, D_MODEL)), _sds((s, D_MODEL))],
        tm=256, name="mix_ln1")

    nf = D_FF // N_CHIPS
    w_gate_t, w_up_t, w_down = wts["w_ff_gate"], wts["w_ff_up"], wts["w_ff_down"]
    ff_a, ff_b, act = _ffn_up(h, w_gate_t, w_up_t)
    dz2, loss_v, d_ln2_g, d_ln2_b = _ffn_down_ln2_loss(act, w_down, h, tgt, small["ln2_g"], small["ln2_b"])

    d_a, d_b = _ffn_down_bwd(dz2, w_down, ff_a, ff_b)

    def grad_rows(lhs, rhs, name):
        return _matmul(lhs, rhs, grid=(N_CHIPS,), a_spec=pl.BlockSpec((None, s, nf), lambda k: (k, 0, 0)),
                       b_spec=pl.BlockSpec((s, D_MODEL), lambda k: (0, 0)),
                       o_spec=pl.BlockSpec((None, nf, D_MODEL), lambda k: (k, 0, 0)),
                       out_shape=_sds((N_CHIPS, nf, D_MODEL), BF16), dims=(0, 0), name=name)

    g_w_ff_down = grad_rows(act, dz2, "g_w_ff_down")
    g_w_ff_gate = grad_rows(d_a, h, "g_w_ff_gate")
    g_w_ff_up = grad_rows(d_b, h, "g_w_ff_up")
    dh_ff = _ffn_dh(d_a, d_b, w_gate_t, w_up_t)

    def ln1_gate_bwd(dz, dff, z, l0, l1, ya, ys, g, bg, wo, wa, ws):
        xhat, rstd = _ln_stats(z)
        dh = DN_ALPHA * dz + dff
        dz_in = _ln_bwd(dh, xhat, rstd, g)
        dm = _dot_nt(dz_in.astype(BF16), wo)
        g0 = jax.nn.sigmoid(l0 + bg[0:1])
        g1 = jax.nn.sigmoid(l1 + bg[1:2])
        dl0 = dm * ya * g0 * (1.0 - g0)
        dl1 = dm * ys * g1 * (1.0 - g1)
        dya, dys = (dm * g0).astype(BF16), (dm * g1).astype(BF16)
        return (dz_in, dya, dys, jnp.concatenate([dl0, dl1], axis=1), branch_t(dya, wa), branch_t(dys, ws),
                _colsum(dh * xhat), _colsum(dh), _colsum(dl0), _colsum(dl1))

    dz1, d_y_attn, d_y_ssm, d_gl, d_attn, d_y_glu, d_ln1_g, d_ln1_b, d_bg0, d_bg1 = _rowwise(
        ln1_gate_bwd, [dz2, dh_ff, z1, gl0, gl1, y_attn, y_ssm],
        [small["ln1_g"], b_gate, w_out, wts["w_attn_br"], wts["w_ssm_br"]],
        [_sds((s, D_MODEL)), _sds((s, D_MODEL), BF16), _sds((s, D_MODEL), BF16), _sds((s, 2 * D_MODEL), BF16),
         _sds((s, ATTN_WIDTH)), _sds((s, SSM_WIDTH))],
        [_sds((1, D_MODEL))] * 4, tm=256, name="ln1_gate_bwd")
    g_w_out = _mm_rows_tn(mixed, dz1, name="g_w_out")

    g_w_ssm_br = _mm_cols_tn(y_glu, d_y_ssm, ns=D_MODEL // N_CHIPS, name="g_w_ssm_br")
    d_y_glu = _to_scan_rows(d_y_glu)

    def glu_gelu_bwd(dyg, gl, y, u, dsk, wg):
        ga, gb = gl[:, :SSM_WIDTH], gl[:, SSM_WIDTH:]
        sg = jax.nn.sigmoid(gb)
        d_gl = jnp.concatenate([dyg * sg, dyg * ga * sg * (1.0 - sg)], axis=1).astype(BF16)
        dg = branch_t(d_gl, wg)
        th = jnp.tanh(GELU_C * (y + GELU_K * y * y * y))
        dy = dg * (0.5 * (1.0 + th) + 0.5 * y * (1.0 - th * th) * GELU_C * (1.0 + 3.0 * GELU_K * y * y))
        return d_gl, dy, dy * dsk, _colsum(dy * u)

    d_glu, d_y, d_u_skip, d_ssm_d = _rowwise(
        glu_gelu_bwd, [d_y_glu, glu, y_s5, u_f], [d_skip, wts["w_glu"]],
        [_sds((s, 2 * SSM_WIDTH), BF16), _sds((s, SSM_WIDTH), BF16), _sds((s, SSM_WIDTH))], [_sds((1, SSM_WIDTH))],
        tm=512, name="glu_gelu_bwd")
    g_w_glu = _mm_cols_tn(gel, d_glu, ns=2 * SSM_WIDTH // N_CHIPS, name="g_w_glu")
    d_u, d_abar_r, d_abar_i, d_c_r, d_c_ni, d_bin_r, d_bin_i = _ssm_backward(
        d_y, d_u_skip, u_p, h_r, h_i, e_r, e_i, b_in_r, b_in_i, c_out_r, c_out_ni, a_r, a_i)
    d_u = _from_scan_rows(d_u)
    d_bbar_r = d_bin_r.reshape(SSM_GROUPS, SSM_GROUP, SSM_STATE).transpose(0, 2, 1)
    d_bbar_i = d_bin_i.reshape(SSM_GROUPS, SSM_GROUP, SSM_STATE).transpose(0, 2, 1)
    d_a_re, d_a_im, d_log_dt, d_b_re, d_b_im = ssm_vjp(
        (d_abar_r.reshape(SSM_GROUPS, SSM_STATE), d_abar_i.reshape(SSM_GROUPS, SSM_STATE), d_bbar_r, d_bbar_i))
    d_c_re = d_c_r.reshape(SSM_GROUPS, SSM_GROUP, SSM_STATE)
    d_c_im = -d_c_ni.reshape(SSM_GROUPS, SSM_GROUP, SSM_STATE)

    g_w_attn_br = _mm_cols_tn(attn, d_y_attn, ns=D_MODEL // N_CHIPS, name="g_w_attn_br")
    dqkv = [_attention_bwd(g, proj, cos_f, sin_s, d_attn, attn, lse) for g in range(len(DILATIONS))]

    d_proj = jnp.concatenate([dqkv[g][j] for j in range(3) for g in range(len(DILATIONS))] + [d_u, d_gl],
                             axis=1)
    g_w_in = _mm_cols_tn(x, d_proj, ns=IN_WIDTH // N_CHIPS, name="g_w_in")

    def grad_x_after(after):
        dx_proj = _mm_cols_nt(d_proj, wts["w_in"], tm=1024, name="dx_proj", after=after)
        return _reorder_rows(dz1, dx_proj, to_phase=False, name="grad_x", scale=DN_ALPHA)

    big = {"w_in": g_w_in, "w_attn_br": g_w_attn_br, "w_ssm_br": g_w_ssm_br, "w_out": g_w_out, "w_glu": g_w_glu,
           "w_ff_gate": g_w_ff_gate, "w_ff_up": g_w_ff_up, "w_ff_down": g_w_ff_down}
    small_g = {"b_gate": jnp.concatenate([d_bg0, d_bg1], axis=0), "ssm_a_re": d_a_re, "ssm_a_im": d_a_im,
               "ssm_log_dt": d_log_dt, "ssm_b_re": d_b_re, "ssm_b_im": d_b_im, "ssm_c_re": d_c_re, "ssm_c_im": d_c_im,
               "ssm_d": d_ssm_d.reshape(SSM_WIDTH), "ln1_g": d_ln1_g, "ln1_b": d_ln1_b, "ln2_g": d_ln2_g,
               "ln2_b": d_ln2_b}
    marks = {"ln1_bwd": dz1, "scan_bwd": d_abar_r, "attention_bwd_0": dqkv[0][0]}
    return loss_v[0, 0], grad_x_after, big, small_g, marks


GATHER_ID, SWAP_ID, SCATTER_ID, JOIN_ID, EXCHANGE_ID = 1, 2, 3, 4, 5


def _place():
    return lax.axis_index("x"), lax.axis_index("y"), lax.axis_index("c")


def _other_chips(x, y):
    return [(1 - x, y), (x, 1 - y), (1 - x, 1 - y)]


def _handshake(peers):
    barrier = pltpu.get_barrier_semaphore()
    for peer in peers:
        pl.semaphore_signal(barrier, inc=1, device_id=peer, device_id_type=MESH)
    pl.semaphore_wait(barrier, len(peers))


def _sequencer(body, arrays, out_type, sems, collective_id, name):
    return pl.kernel(body, name=name, out_type=out_type,
                     mesh=plsc.ScalarSubcoreMesh(axis_name="sequencer", num_cores=1), scratch_types=sems,
                     compiler_params=pltpu.CompilerParams(collective_id=collective_id))(*arrays)


def _gather_weights(shards, *, name):
    nw = len(shards)

    def body(*refs):
        ins, outs = refs[:nw], refs[nw:2 * nw]
        send_sems, recv_sems, pass_send, pass_recv, local_sems = refs[2 * nw:]
        x, y, c = _place()
        chip = 2 * x + y
        chips = _other_chips(x, y)
        _handshake([(x, y, 1 - c)] + [(cx, cy, c) for cx, cy in chips])
        started = []
        for w in range(nw):
            hw = shards[w].shape[0] // 2
            mine = pl.ds(c * hw, hw)
            own = pltpu.make_async_copy(ins[w], outs[w].at[chip], local_sems.at[w])
            own.start()
            started.append(own)
            for j, (cx, cy) in enumerate(chips):
                cp = pltpu.make_async_remote_copy(
                    src_ref=ins[w].at[mine], dst_ref=outs[w].at[chip, mine], send_sem=send_sems.at[w, j],
                    recv_sem=recv_sems.at[w, j], device_id=(cx, cy, c), device_id_type=MESH)
                cp.start()
                started.append(cp)
        passed = []
        for w in range(nw):
            hw = shards[w].shape[0] // 2
            mine = pl.ds(c * hw, hw)
            for j, (cx, cy) in enumerate(chips):
                landed = outs[w].at[2 * cx + cy, mine]
                pltpu.make_async_remote_copy(
                    src_ref=ins[w].at[mine], dst_ref=landed, send_sem=send_sems.at[w, j],
                    recv_sem=recv_sems.at[w, j], device_id=(cx, cy, c), device_id_type=MESH).wait_recv()
                cp = pltpu.make_async_remote_copy(
                    src_ref=landed, dst_ref=landed, send_sem=pass_send.at[w, j], recv_sem=pass_recv.at[w, j],
                    device_id=(x, y, 1 - c), device_id_type=MESH)
                cp.start()
                passed.append(cp)
        for w in range(nw):
            hw = shards[w].shape[0] // 2
            theirs = pl.ds((1 - c) * hw, hw)
            for j, (cx, cy) in enumerate(chips):
                landed = outs[w].at[2 * cx + cy, theirs]
                pltpu.make_async_remote_copy(
                    src_ref=landed, dst_ref=landed, send_sem=pass_send.at[w, j], recv_sem=pass_recv.at[w, j],
                    device_id=(x, y, 1 - c), device_id_type=MESH).wait_recv()
        for cp in started[0::4]:
            cp.wait()
        for cp in [s for i, s in enumerate(started) if i % 4] + passed:
            cp.wait_send()

    sem = pltpu.SemaphoreType.DMA
    return _sequencer(body, shards, [_sds((N_CHIPS,) + a.shape, a.dtype) for a in shards],
                      [sem((nw, 3)), sem((nw, 3)), sem((nw, 3)), sem((nw, 3)), sem((nw,))], GATHER_ID, name)


def _swap_other_halves(grads, *, name):
    nw = len(grads)

    def body(*refs):
        ins, outs = refs[:nw], refs[nw:2 * nw]
        send_sems, recv_sems = refs[2 * nw:]
        x, y, c = _place()
        _handshake([(x, y, 1 - c)])
        cps = []
        for w in range(nw):
            hw = grads[w].shape[1] // 2
            cp = pltpu.make_async_remote_copy(
                src_ref=ins[w].at[:, pl.ds((1 - c) * hw, hw)], dst_ref=outs[w], send_sem=send_sems.at[w],
                recv_sem=recv_sems.at[w], device_id=(x, y, 1 - c), device_id_type=MESH)
            cp.start()
            cps.append(cp)
        for cp in cps:
            cp.wait()

    sem = pltpu.SemaphoreType.DMA
    return _sequencer(body, grads, [_sds((N_CHIPS, g.shape[1] // 2, g.shape[2]), g.dtype) for g in grads],
                      [sem((nw,)), sem((nw,))], SWAP_ID, name)


def _add_my_halves(core, grads, others, *, name, after=()):
    nw = len(grads)
    halves = [g.shape[1] // 2 for g in grads]

    def body(core_ref, *refs):
        outs = refs[2 * nw + len(after):]
        for g_ref, o_ref, out_ref in zip(refs[:nw], refs[nw:2 * nw], outs):
            out_ref[...] = (g_ref[...].astype(F32) + o_ref[...].astype(F32)).astype(out_ref.dtype)

    in_specs = [pl.BlockSpec((None, None, hw, g.shape[2]), lambda s, core_ref: (s, core_ref[0], 0, 0))
                for g, hw in zip(grads, halves)]
    in_specs += [pl.BlockSpec((None, hw, g.shape[2]), lambda s, core_ref: (s, 0, 0)) for g, hw in zip(grads, halves)]
    return pl.pallas_call(
        body,
        grid_spec=pltpu.PrefetchScalarGridSpec(
            num_scalar_prefetch=1, grid=(N_CHIPS,), in_specs=in_specs + [HBM_OPERAND] * len(after),
            out_specs=[pl.BlockSpec((None, hw, g.shape[2]), lambda s, core_ref: (s, 0, 0))
                       for g, hw in zip(grads, halves)]),
        out_shape=[_sds((N_CHIPS, hw, g.shape[2]), BF16) for g, hw in zip(grads, halves)],
        compiler_params=_cp(("parallel",)), name=name)(
            core, *[g.reshape(N_CHIPS, 2, hw, g.shape[2]) for g, hw in zip(grads, halves)], *others, *after)


def _scatter_partials(parts, *, name):
    nw = len(parts)

    def body(*refs):
        ins, outs = refs[:nw], refs[nw:2 * nw]
        send_sems, recv_sems = refs[2 * nw:]
        x, y, c = _place()
        _handshake([(cx, cy, c) for cx, cy in _other_chips(x, y)])
        cps = []
        for w in range(nw):
            for j, (cx, cy) in enumerate(_other_chips(x, y)):
                cp = pltpu.make_async_remote_copy(
                    src_ref=ins[w].at[2 * cx + cy], dst_ref=outs[w].at[j], send_sem=send_sems.at[w, j],
                    recv_sem=recv_sems.at[w, j], device_id=(cx, cy, c), device_id_type=MESH)
                cp.start()
                cps.append(cp)
        for cp in cps:
            cp.wait()

    sem = pltpu.SemaphoreType.DMA
    return _sequencer(body, parts, [_sds((3,) + p.shape[1:], p.dtype) for p in parts],
                      [sem((nw, 3)), sem((nw, 3))], SCATTER_ID, name)


SUM_STEPS = 2


def _sum_partials(chip, parts, recvd, *, name, after=()):
    nw = len(parts)
    rows = [p.shape[1] // SUM_STEPS for p in parts]

    def body(chip_ref, *refs):
        outs = refs[2 * nw + len(after):]
        for p_ref, r_ref, out_ref in zip(refs[:nw], refs[nw:2 * nw], outs):
            acc = p_ref[...].astype(F32)
            for j in range(3):
                acc = acc + r_ref[j].astype(F32)
            out_ref[...] = acc

    in_specs = [pl.BlockSpec((None, th, p.shape[2]), lambda i, chip_ref: (chip_ref[0], i, 0))
                for p, th in zip(parts, rows)]
    in_specs += [pl.BlockSpec((3, th, p.shape[2]), lambda i, chip_ref: (0, i, 0)) for p, th in zip(parts, rows)]
    return pl.pallas_call(
        body,
        grid_spec=pltpu.PrefetchScalarGridSpec(
            num_scalar_prefetch=1, grid=(SUM_STEPS,), in_specs=in_specs + [HBM_OPERAND] * len(after),
            out_specs=[pl.BlockSpec((th, p.shape[2]), lambda i, chip_ref: (i, 0)) for p, th in zip(parts, rows)]),
        out_shape=[_sds(p.shape[1:]) for p in parts], compiler_params=_cp(("parallel",)), name=name)(
            chip, *parts, *recvd, *after)


def _swap_reduced_halves(halves, *, name):
    nw = len(halves)

    def body(*refs):
        ins, outs = refs[:nw], refs[nw:2 * nw]
        send_sems, recv_sems = refs[2 * nw:]
        x, y, c = _place()
        _handshake([(x, y, 1 - c)])
        cps = []
        for w in range(nw):
            cp = pltpu.make_async_remote_copy(
                src_ref=ins[w], dst_ref=outs[w], send_sem=send_sems.at[w], recv_sem=recv_sems.at[w],
                device_id=(x, y, 1 - c), device_id_type=MESH)
            cp.start()
            cps.append(cp)
        for cp in cps:
            cp.wait()

    sem = pltpu.SemaphoreType.DMA
    return _sequencer(body, halves, [_sds(h.shape, h.dtype) for h in halves], [sem((nw,)), sem((nw,))], JOIN_ID, name)


def _exchange_rows(vec, *, name):
    def body(v_ref, slots, send_sems, recv_sems, local_sem):
        x, y, c = _place()
        me = 4 * x + 2 * y + c
        peers = []
        for mask in range(1, N_DEV):
            peers.append((1 - x if mask & 4 else x, 1 - y if mask & 2 else y, 1 - c if mask & 1 else c))
        _handshake(peers)
        own = pltpu.make_async_copy(v_ref, slots.at[me], local_sem)
        own.start()
        cps = []
        for k, peer in enumerate(peers):
            cp = pltpu.make_async_remote_copy(
                src_ref=v_ref, dst_ref=slots.at[me], send_sem=send_sems.at[k], recv_sem=recv_sems.at[k],
                device_id=peer, device_id_type=MESH)
            cp.start()
            cps.append(cp)
        for k, (px, py, pc) in enumerate(peers):
            pltpu.make_async_remote_copy(
                src_ref=v_ref, dst_ref=slots.at[4 * px + 2 * py + pc], send_sem=send_sems.at[k],
                recv_sem=recv_sems.at[k], device_id=(px, py, pc), device_id_type=MESH).wait_recv()
        for cp in cps:
            cp.wait_send()
        own.wait()

    sem = pltpu.SemaphoreType.DMA
    return _sequencer(body, [vec], [_sds((N_DEV,) + vec.shape)], [sem((N_DEV - 1,)), sem((N_DEV - 1,)), sem(())],
                      EXCHANGE_ID, name)[0]


def _sum_slots(slots, *, name, after=()):
    def body(s_ref, *rest):
        out_ref = rest[len(after)]
        acc = s_ref[0]
        for d in range(1, N_DEV):
            acc = acc + s_ref[d]
        out_ref[...] = acc

    vmem = pl.BlockSpec(memory_space=pltpu.VMEM)
    return pl.pallas_call(
        body, in_specs=[vmem] + [HBM_OPERAND] * len(after), out_specs=vmem, out_shape=_sds(slots.shape[1:]),
        compiler_params=pltpu.CompilerParams(vmem_limit_bytes=VMEM_LIMIT_BYTES), name=name)(slots, *after)


def _reduce_scatter_start(grads, core, *, tag, add_after=()):
    others = _swap_other_halves(grads, name="swap_other_halves_" + tag)
    parts = _add_my_halves(core, grads, others, name="add_my_halves_" + tag, after=add_after)
    return parts, _scatter_partials(parts, name="scatter_partials_" + tag)


def _reduce_scatter_finish(parts, recvd, chip, *, tag, sum_after=()):
    mine = _sum_partials(chip, parts, recvd, name="sum_partials_" + tag, after=sum_after)
    return mine, _swap_reduced_halves(mine, name="swap_reduced_halves_" + tag)


ADAM_BLOCK_ELEMS = 256 * 1024


def _adam_rows(rows, cols):
    tm = rows
    while tm * cols > ADAM_BLOCK_ELEMS and tm % 16 == 0:
        tm //= 2
    return tm


def _adam_step(wv, gv, mv, vv):
    m2 = ADAM_B1 * mv + (1.0 - ADAM_B1) * gv
    v2 = ADAM_B2 * vv + (1.0 - ADAM_B2) * (gv * gv)
    m_hat = m2 / (1.0 - ADAM_B1 ** ADAM_STEP)
    v_hat = v2 / (1.0 - ADAM_B2 ** ADAM_STEP)
    return -ADAM_LR * (m_hat / (jnp.sqrt(v_hat) + ADAM_EPS) + ADAM_WD * wv), m2, v2


def _adamw(w, g, m, v, *, name):
    rows, cols = w.shape
    return _rowwise(_adam_step, [w, g, m, v], [], [_sds((rows, cols))] * 3, tm=_adam_rows(rows, cols), name=name)


def _adamw_halves(core, w, g_mine, g_theirs, m, v, *, name, after=()):
    rows, cols = w.shape
    hw = rows // 2
    tm = _adam_rows(hw, cols)
    per_half = hw // tm

    def body(core_ref, w_ref, gm_ref, gt_ref, m_ref, v_ref, *rest):
        g_out, d_out, m_out, v_out = rest[len(after):]
        mine = (pl.program_id(0) // per_half) == core_ref[0]
        g = jnp.where(mine, gm_ref[...], gt_ref[...])
        d, m2, v2 = _adam_step(w_ref[...], g, m_ref[...], v_ref[...])
        g_out[...] = g
        d_out[...] = d
        m_out[...] = m2
        v_out[...] = v2

    full = pl.BlockSpec((tm, cols), lambda i, core_ref: (i, 0))

    def half(wanted):
        def index(i, core_ref):
            in_use = ((i // per_half) == core_ref[0]) == wanted
            return (jnp.where(in_use, i % per_half, 0), 0)
        return pl.BlockSpec((tm, cols), index)

    return pl.pallas_call(
        body,
        grid_spec=pltpu.PrefetchScalarGridSpec(
            num_scalar_prefetch=1, grid=(rows // tm,),
            in_specs=[full, half(True), half(False), full, full] + [HBM_OPERAND] * len(after),
            out_specs=[full, full, full, full]),
        out_shape=[_sds((rows, cols))] * 4, compiler_params=_cp(("parallel",)), name=name)(
            core, w, g_mine, g_theirs, m, v, *after)


HELD_TRANSPOSED = ("w_ff_gate", "w_ff_up")


def _as_rows(name, arr):
    return arr[0].T if name in HELD_TRANSPOSED else arr[0]


def _from_rows(name, arr2d):
    return (arr2d.T if name in HELD_TRANSPOSED else arr2d)[None]


STORED_SWAPPED = ("ssm_b_re", "ssm_b_im")


def _as_stored(name, arr):
    return jnp.swapaxes(arr, -1, -2) if name in STORED_SWAPPED else arr


def _pack_rows(arrs):
    flat = jnp.concatenate([a.reshape(-1).astype(F32) for a in arrs])
    rows = -(-flat.shape[0] // 1024) * 8
    return jnp.pad(flat, (0, rows * 128 - flat.shape[0])).reshape(rows, 128)


def _unpack_rows(vec, shapes):
    flat = vec.reshape(-1)
    out, off = [], 0
    for shp in shapes:
        size = math.prod(shp)
        out.append(flat[off:off + size].reshape(shp))
        off += size
    return out


SMALL = ("b_gate", "ssm_a_re", "ssm_a_im", "ssm_log_dt", "ssm_b_re", "ssm_b_im", "ssm_c_re", "ssm_c_im", "ssm_d",
         "ln1_g", "ln1_b", "ln2_g", "ln2_b")
GATHER_GROUPS = (("w_in", ("w_in",)), ("mixer", ("w_attn_br", "w_ssm_br", "w_glu", "w_out")),
                 ("ffn", ("w_ff_gate", "w_ff_up", "w_ff_down")))
REDUCE_GROUPS = (("ffn", ("w_ff_down", "w_ff_gate", "w_ff_up")),
                 ("mixer", ("w_out", "w_ssm_br", "w_glu", "w_attn_br")), ("w_in", ("w_in",)))
WEIGHTS = ("w_in", "b_gate", "w_attn_br", "w_ssm_br", "w_out", "ssm_a_re", "ssm_a_im", "ssm_log_dt", "ssm_b_re",
           "ssm_b_im", "ssm_c_re", "ssm_c_im", "ssm_d", "w_glu", "ln1_g", "ln1_b", "w_ff_gate", "w_ff_up", "w_ff_down",
           "ln2_g", "ln2_b")


def kernel(x, w_in, b_gate, w_attn_br, w_ssm_br, w_out, ssm_a_re, ssm_a_im, ssm_log_dt, ssm_b_re, ssm_b_im, ssm_c_re, ssm_c_im, ssm_d, w_glu, ln1_g, ln1_b, w_ff_gate, w_ff_up, w_ff_down, ln2_g, ln2_b, loss_target, m_w_in, m_b_gate, m_w_attn_br, m_w_ssm_br, m_w_out, m_ssm_a_re, m_ssm_a_im, m_ssm_log_dt, m_ssm_b_re, m_ssm_b_im, m_ssm_c_re, m_ssm_c_im, m_ssm_d, m_w_glu, m_ln1_g, m_ln1_b, m_w_ff_gate, m_w_ff_up, m_w_ff_down, m_ln2_g, m_ln2_b, v_w_in, v_b_gate, v_w_attn_br, v_w_ssm_br, v_w_out, v_ssm_a_re, v_ssm_a_im, v_ssm_log_dt, v_ssm_b_re, v_ssm_b_im, v_ssm_c_re, v_ssm_c_im, v_ssm_d, v_w_glu, v_ln1_g, v_ln1_b, v_w_ff_gate, v_w_ff_up, v_w_ff_down, v_ln2_g, v_ln2_b):
    given = dict(locals())
    px, py, pc = _place()
    chip = 2 * px + py
    core_s = jnp.reshape(pc, (1,)).astype(jnp.int32)
    chip_s = jnp.reshape(chip, (1,)).astype(jnp.int32)

    wts = {}
    for tag, names in GATHER_GROUPS:
        wts.update(zip(names, _gather_weights([_as_rows(n, given[n]).astype(BF16) for n in names],
                                              name="gather_" + tag)))
    ncol = D_MODEL // N_CHIPS
    bg_mine = jnp.where(pc == 0, b_gate[0], jnp.zeros_like(b_gate[0]))
    bg_full = lax.dynamic_update_slice(jnp.zeros((2, D_MODEL), F32), bg_mine, (0, chip * ncol))
    bg_slots = _exchange_rows(bg_full.reshape(16, 128), name="exchange_gate_bias")
    bg_full = _sum_slots(bg_slots, name="sum_gate_bias").reshape(2, D_MODEL)
    small = {n: given[n][0] for n in SMALL if n.startswith("ssm")}
    small.update({n: given[n] for n in ("ln1_g", "ln1_b", "ln2_g", "ln2_b")})
    small["b_gate"] = bg_full

    loss_mine, grad_x_after, big_g, small_g, marks = _local_step(x[0], loss_target[0], wts, small)

    groups = dict(REDUCE_GROUPS)
    parts, recvd = {}, {}
    grads, delta, new_m, new_v = {}, {}, {}, {}

    def start(tag, add_after):
        parts[tag], recvd[tag] = _reduce_scatter_start([big_g[n] for n in groups[tag]], core_s, tag=tag,
                                                       add_after=add_after)

    def finish(tag, sum_after, adam_after):
        mine, theirs = _reduce_scatter_finish(parts[tag], recvd[tag], chip_s, tag=tag, sum_after=sum_after)
        for n, g_mine, g_theirs in zip(groups[tag], mine, theirs):
            res = _adamw_halves(core_s, _as_rows(n, given[n]), g_mine, g_theirs, _as_rows(n, given["m_" + n]),
                                _as_rows(n, given["v_" + n]), name="adamw_" + n, after=adam_after)
            grads[n], delta[n], new_m[n], new_v[n] = [_from_rows(n, r) for r in res]

    start("ffn", (marks["ln1_bwd"],))
    start("mixer", (marks["scan_bwd"],))
    finish("mixer", (marks["attention_bwd_0"],), (big_g["w_in"],))
    start("w_in", tuple(delta[n] for n in groups["mixer"]))
    in_flight = (parts["w_in"][0],)
    grad_x = grad_x_after(in_flight)
    finish("ffn", (marks["scan_bwd"],), in_flight)
    stored = [_as_stored(n, small_g[n]) for n in SMALL] + [loss_mine.reshape(1)]
    slots = _exchange_rows(_pack_rows(stored), name="exchange_small")
    summed = _unpack_rows(_sum_slots(slots, name="sum_small", after=in_flight), [a.shape for a in stored])
    loss = summed.pop()[0]
    for n, g in zip(SMALL, summed):
        g = _as_stored(n, g)
        if n == "b_gate":
            g = lax.dynamic_slice(g, (0, chip * ncol), (2, ncol))
        grads[n] = g.reshape(given[n].shape)
    packed = [_pack_rows([_as_stored(n, src[n]) for n in SMALL]) for src in
              (given, grads, {n: given["m_" + n] for n in SMALL}, {n: given["v_" + n] for n in SMALL})]
    shapes = [_as_stored(n, given[n]).shape for n in SMALL]
    small_out = _adamw(*packed, name="adamw_small")
    for out, vec in zip((delta, new_m, new_v), small_out):
        out.update((n, _as_stored(n, a)) for n, a in zip(SMALL, _unpack_rows(vec, shapes)))
    behind = [delta[n] for n in groups["ffn"]] + [small_out[0], grad_x]
    finish("w_in", tuple(behind), ())

    return (loss, grad_x.reshape(x.shape), *[grads[n] for n in WEIGHTS], *[delta[n] for n in WEIGHTS],
            *[new_m[n] for n in WEIGHTS], *[new_v[n] for n in WEIGHTS])
```

```python
import math

import jax
import jax.numpy as jnp
from jax import lax
from jax.experimental import pallas as pl
from jax.experimental.pallas import tpu as pltpu
from jax.experimental.pallas import tpu_sc as plsc

F32 = jnp.float32
BF16 = jnp.bfloat16
MESH = pl.DeviceIdType.MESH

D_MODEL = 1024
SEQ = 2048
HEAD_DIM = 64
ATTN_HEADS = 8
DILATIONS = (1, 4, 16)
ATTN_WIDTH = ATTN_HEADS * HEAD_DIM
QKV_WIDTH = 3 * ATTN_WIDTH
BLOCK = 128
ROPE_THETA = 10000.0
NEG_INF = -1e30
SSM_GROUP = 16
SSM_GROUPS = 32
SSM_WIDTH = 512
SSM_STATE = 64
SSM_LANES = SSM_GROUPS * SSM_STATE
SCAN_CHUNKS = 8
SCAN_STEPS = SEQ // SCAN_CHUNKS
IN_WIDTH = 3 * QKV_WIDTH + SSM_WIDTH + 2 * D_MODEL
D_FF = 2816
N_CHIPS = 4
N_DEV = 8
DN_ALPHA = 2.0 ** 0.25
LN_EPS = 1e-5
ADAM_LR = 0.001
ADAM_B1 = 0.9
ADAM_B2 = 0.999
ADAM_EPS = 1e-08
ADAM_WD = 0.01
ADAM_STEP = 10
GELU_C = math.sqrt(2.0 / math.pi)
GELU_K = 0.044715

VMEM_LIMIT_BYTES = 56 * 1024 * 1024


def _sds(shape, dtype=F32):
    return jax.ShapeDtypeStruct(tuple(shape), dtype)


def _cp(semantics=None):
    return pltpu.CompilerParams(dimension_semantics=semantics, vmem_limit_bytes=VMEM_LIMIT_BYTES)


HBM_OPERAND = pl.BlockSpec(memory_space=pl.ANY)


def _matmul(a, b, *, grid, a_spec, b_spec, o_spec, out_shape, dims, k_axis=None, name, after=()):
    nk = grid[k_axis] if k_axis is not None else 1
    o_block = tuple(d for d in o_spec.block_shape if d is not None)
    n_after = len(after)

    def body(a_ref, b_ref, *rest):
        o_ref, acc = rest[n_after], rest[n_after + 1:]
        part = lax.dot_general(a_ref[...].astype(BF16), b_ref[...].astype(BF16),
                               (((dims[0],), (dims[1],)), ((), ())), preferred_element_type=F32)
        if k_axis is None:
            o_ref[...] = part.astype(o_ref.dtype)
        else:
            k = pl.program_id(k_axis)

            @pl.when(k == 0)
            def _():
                acc[0][...] = part

            @pl.when(k > 0)
            def _():
                acc[0][...] += part

            @pl.when(k == nk - 1)
            def _():
                o_ref[...] = acc[0][...].astype(o_ref.dtype)

    sem = tuple("arbitrary" if ax == k_axis else "parallel" for ax in range(len(grid)))
    return pl.pallas_call(
        body, grid=grid, in_specs=[a_spec, b_spec] + [HBM_OPERAND] * n_after, out_specs=o_spec, out_shape=out_shape,
        scratch_shapes=[pltpu.VMEM(o_block, F32)] if k_axis is not None else [],
        compiler_params=_cp(sem), name=name)(a, b, *after)


def _mm_cols_nt(dy, wg, *, tm, name, out_dtype=F32, after=()):
    k, ns = wg.shape[1], wg.shape[2]
    m = dy.shape[0]
    a_spec = pl.BlockSpec((tm, ns), lambda i, s: (i, s))
    return _matmul(dy, wg, grid=(m // tm, N_CHIPS), a_spec=a_spec,
                   b_spec=pl.BlockSpec((None, k, ns), lambda i, s: (s, 0, 0)),
                   o_spec=pl.BlockSpec((tm, k), lambda i, s: (i, 0)),
                   out_shape=_sds((m, k), out_dtype), dims=(1, 1), k_axis=1, name=name, after=after)


def _mm_cols_tn(a, dy, *, ns, name, after=()):
    m, k = a.shape
    return _matmul(a, dy, grid=(N_CHIPS,), a_spec=pl.BlockSpec((m, k), lambda s: (0, 0)),
                   b_spec=pl.BlockSpec((m, ns), lambda s: (0, s)),
                   o_spec=pl.BlockSpec((None, k, ns), lambda s: (s, 0, 0)),
                   out_shape=_sds((N_CHIPS, k, ns), BF16), dims=(0, 0), name=name, after=after)


def _mm_rows_tn(a, dy, *, name):
    m, k = a.shape
    rows, n = k // N_CHIPS, dy.shape[1]
    return _matmul(a, dy, grid=(N_CHIPS,), a_spec=pl.BlockSpec((m, rows), lambda s: (0, s)),
                   b_spec=pl.BlockSpec((m, n), lambda s: (0, 0)),
                   o_spec=pl.BlockSpec((None, rows, n), lambda s: (s, 0, 0)),
                   out_shape=_sds((N_CHIPS, rows, n), BF16), dims=(0, 0), name=name)


def _rowwise(fn, tiled, full, outs, accs=(), *, tm, name, after=()):
    args, in_specs = [], []
    for t in tiled:
        if isinstance(t, tuple):
            arr, w, cb = t
            in_specs.append(pl.BlockSpec((tm, w), lambda i, cb=cb: (i, cb)))
        else:
            arr = t
            in_specs.append(pl.BlockSpec((tm, arr.shape[1]), lambda i: (i, 0)))
        args.append(arr)
    rows = args[0].shape[0]
    for f in full:
        in_specs.append(pl.BlockSpec(f.shape, lambda i, nd=f.ndim: (0,) * nd))
        args.append(f)
    out_specs = [pl.BlockSpec((tm, o.shape[1]), lambda i: (i, 0)) for o in outs]
    out_specs += [pl.BlockSpec(a.shape, lambda i, nd=len(a.shape): (0,) * nd) for a in accs]
    n_in, n_out = len(args), len(outs)
    in_specs += [HBM_OPERAND] * len(after)
    first_out = n_in + len(after)

    def body(*refs):
        res = fn(*[r[...] for r in refs[:n_in]])
        res = res if isinstance(res, (tuple, list)) else (res,)
        for r, v in zip(refs[first_out:first_out + n_out], res[:n_out]):
            r[...] = v.astype(r.dtype)
        i = pl.program_id(0)
        for r, v in zip(refs[first_out + n_out:], res[n_out:]):
            @pl.when(i == 0)
            def _(r=r, v=v):
                r[...] = v

            @pl.when(i > 0)
            def _(r=r, v=v):
                r[...] += v

    res = pl.pallas_call(
        body, grid=(rows // tm,), in_specs=in_specs, out_specs=out_specs, out_shape=list(outs) + list(accs),
        compiler_params=_cp(("arbitrary",) if accs else ("parallel",)), name=name)(*args, *after)
    return res


def _colsum(v):
    return jnp.sum(v, axis=0, keepdims=True)


def _ln_stats(z):
    mu = jnp.mean(z, axis=-1, keepdims=True)
    zc = z - mu
    var = jnp.mean(zc * zc, axis=-1, keepdims=True)
    rstd = lax.rsqrt(var + LN_EPS)
    return zc * rstd, rstd


def _ln_bwd(dy, xhat, rstd, g):
    dxh = dy * g
    m1 = jnp.mean(dxh, axis=-1, keepdims=True)
    m2 = jnp.mean(dxh * xhat, axis=-1, keepdims=True)
    return rstd * (dxh - m1 - xhat * m2)


def _swap_halves(t):
    w = t.shape[-1]
    lane = lax.broadcasted_iota(jnp.int32, t.shape, t.ndim - 1)
    return jnp.where((lane % HEAD_DIM) < HEAD_DIM // 2, pltpu.roll(t, w - HEAD_DIM // 2, t.ndim - 1),
                     pltpu.roll(t, HEAD_DIM // 2, t.ndim - 1))


PHASES = max(DILATIONS)
PAIR = 2 * HEAD_DIM
UNITS = SEQ // BLOCK
UNIT_BATCH = 16
ROPE_ROWS = 256


def _to_phase_rows(t):
    return t.reshape(SEQ // PHASES, PHASES, t.shape[1]).transpose(1, 0, 2).reshape(t.shape)


def _reorder_rows(arr, plus=None, *, to_phase, name, scale=1.0):
    def body(*refs):
        o_ref = refs[-1]
        for rho in range(PHASES):
            phase = pl.ds(rho * BLOCK, BLOCK)
            strided = pl.ds(rho, BLOCK, stride=PHASES)
            src, dst = (strided, phase) if to_phase else (phase, strided)
            val = refs[0][src, :]
            if scale != 1.0:
                val = val * scale
            if plus is not None:
                val = val + refs[1][src, :]
            o_ref[dst, :] = val

    spec = pl.BlockSpec((SEQ, BLOCK), lambda j: (0, j))
    ins = [arr] if plus is None else [arr, plus]
    return pl.pallas_call(body, grid=(arr.shape[1] // BLOCK,), in_specs=[spec] * len(ins), out_specs=spec,
                          out_shape=_sds(arr.shape), compiler_params=_cp(("parallel",)), name=name)(*ins)


def _rope(t, cf, ss):
    return t * cf + _swap_halves(t) * ss


def _rope_transposed(d, cf, ss):
    return d * cf + _swap_halves(d * ss)


def _unit_pieces(u, dil):
    pieces, length = PHASES // dil, 8 * dil
    if dil == 1:
        rho, i = 0, u
    elif dil == PHASES:
        rho, i = u, 0
    else:
        rho, i = jnp.bitwise_and(u, dil - 1), jnp.right_shift(u, dil.bit_length() - 1)
    before = jnp.maximum(i - 1, 0)
    cur = [pl.multiple_of((rho + dil * k) * BLOCK + length * i, 8) for k in range(pieces)]
    prev = [pl.multiple_of((rho + dil * k) * BLOCK + length * before, 8) for k in range(pieces)]
    return i, cur, prev


def _load_tile(ref, starts, dil):
    return jnp.concatenate([ref[pl.ds(st, 8 * dil), :] for st in starts], axis=0)


def _store_tile(ref, starts, dil, val, head=None, accumulate=False):
    length = 8 * dil
    lanes = slice(None) if head is None else pl.ds(head * HEAD_DIM, HEAD_DIM)
    cols = slice(None) if head is None else slice(head * HEAD_DIM, (head + 1) * HEAD_DIM)
    for k, st in enumerate(starts):
        piece = val[k * length:(k + 1) * length, cols]
        if accumulate:
            ref[pl.ds(st, length), lanes] += piece
        else:
            ref[pl.ds(st, length), lanes] = piece


def _tile_position(idx, dil):
    pieces, length = PHASES // dil, 8 * dil
    return pieces * jnp.bitwise_and(idx, length - 1) + jnp.right_shift(idx, length.bit_length() - 1)


def _band_mask(i, dil):
    row = lax.broadcasted_iota(jnp.int32, (BLOCK, 2 * BLOCK), 0)
    col = lax.broadcasted_iota(jnp.int32, (BLOCK, 2 * BLOCK), 1)
    key_pos = _tile_position(jnp.bitwise_and(col, BLOCK - 1), dil) + jnp.where(col >= BLOCK, 0, -BLOCK)
    dist = _tile_position(row, dil) - key_pos
    return (dist >= 0) & (dist <= BLOCK) & ((col >= BLOCK) | (i > 0))


def _causal_mask():
    row = lax.broadcasted_iota(jnp.int32, (BLOCK, BLOCK), 0)
    col = lax.broadcasted_iota(jnp.int32, (BLOCK, BLOCK), 1)
    return row >= col


def _pair_views(col0):
    return [pl.BlockSpec((SEQ, PAIR), lambda hp, g=g: (0, col0 // PAIR + g * (ATTN_WIDTH // PAIR) + hp))
            for g in range(len(DILATIONS))]


def _project_in(x, wg, cos_f, sin_s):
    ns = wg.shape[2]
    tiles = ns // PAIR

    def body(x_ref, w_ref, cf_ref, ss_ref, o_ref):
        shard = pl.program_id(1)
        xb = x_ref[...].astype(BF16)
        cf, ss = cf_ref[...], ss_ref[...]

        def write(rotated, scaled):
            for t0 in range(0, tiles, 2):
                strip = jnp.dot(xb, w_ref[:, t0 * PAIR:(t0 + 2) * PAIR], preferred_element_type=F32)
                for t in (t0, t0 + 1):
                    val = strip[:, (t - t0) * PAIR:(t - t0 + 1) * PAIR]
                    if t < rotated:
                        val = _rope(val, cf, ss)
                        if t < scaled:
                            val = val * (1.0 / math.sqrt(HEAD_DIM))
                    o_ref[:, t * PAIR:(t + 1) * PAIR] = val

        for s in range(N_CHIPS):
            rotated = min(max(2 * QKV_WIDTH - s * ns, 0), ns) // PAIR
            scaled = min(max(QKV_WIDTH - s * ns, 0), ns) // PAIR

            @pl.when(shard == s)
            def _(rotated=rotated, scaled=scaled):
                write(rotated, scaled)

    table = pl.BlockSpec((FF_ROWS, PAIR), lambda i, s: (i, 0))
    return pl.pallas_call(
        body, grid=(SEQ // FF_ROWS, N_CHIPS),
        in_specs=[pl.BlockSpec((FF_ROWS, D_MODEL), lambda i, s: (i, 0)),
                  pl.BlockSpec((None, D_MODEL, ns), lambda i, s: (s, 0, 0)), table, table],
        out_specs=pl.BlockSpec((FF_ROWS, ns), lambda i, s: (i, s)), out_shape=_sds((SEQ, N_CHIPS * ns)),
        compiler_params=_cp(("parallel", "parallel")), name="project_in")(x, wg, cos_f, sin_s)


def _attention_fwd(proj):
    ng = len(DILATIONS)

    def body(*refs):
        q_refs, k_refs, v_refs = refs[:ng], refs[ng:2 * ng], refs[2 * ng:3 * ng]
        attn_ref, lse_ref = refs[3 * ng:]
        qr_refs, kr_refs = q_refs, k_refs
        first = lax.broadcasted_iota(jnp.int32, (BLOCK, PAIR), 1) < HEAD_DIM
        for g, dil in enumerate(DILATIONS):
            two_blocks = SEQ // dil > BLOCK

            def units(t, carry, g=g, dil=dil, two_blocks=two_blocks):
                picked = [_unit_pieces(t * UNIT_BATCH + j, dil) for j in range(UNIT_BATCH)]

                def tiles(ref, with_prev=False):
                    if with_prev and two_blocks:
                        return jnp.stack([jnp.concatenate([_load_tile(ref, prev, dil), _load_tile(ref, rows, dil)],
                                                          axis=0) for _, rows, prev in picked])
                    return jnp.stack([_load_tile(ref, rows, dil) for _, rows, _ in picked])

                qq = tiles(qr_refs[g]).astype(BF16)
                kk = tiles(kr_refs[g], True).astype(BF16)
                vv = tiles(v_refs[g], True).astype(BF16)
                if two_blocks:
                    valid = jnp.stack([_band_mask(i, dil) for i, _, _ in picked])
                else:
                    valid = _causal_mask()[None]
                mine = first[None]
                zero = jnp.zeros_like(qq)
                outs, lses = [], []
                for qh in (jnp.where(mine, qq, zero), jnp.where(mine, zero, qq)):
                    s = jnp.einsum("pqd,pkd->pqk", qh, kk, preferred_element_type=F32)
                    s = jnp.where(valid, s, NEG_INF)
                    m = jnp.max(s, axis=-1, keepdims=True)
                    p = jnp.exp(s - m)
                    l = jnp.sum(p, axis=-1, keepdims=True)
                    outs.append(jnp.einsum("pqk,pkd->pqd", p.astype(BF16), vv, preferred_element_type=F32) * (1.0 / l))
                    lses.append(m + jnp.log(l))
                o = jnp.where(mine, outs[0], outs[1])
                lse = jnp.where(mine, lses[0], lses[1])
                if g > 0:
                    lse_old = tiles(lse_ref)
                    m = jnp.maximum(lse_old, lse)
                    lse_new = m + jnp.log(jnp.exp(lse_old - m) + jnp.exp(lse - m))
                    o = tiles(attn_ref) * jnp.exp(lse_old - lse_new) + o * jnp.exp(lse - lse_new)
                    lse = lse_new
                for j, (_, rows, _) in enumerate(picked):
                    _store_tile(attn_ref, rows, dil, o[j])
                    _store_tile(lse_ref, rows, dil, lse[j])
                return carry

            lax.fori_loop(0, UNITS // UNIT_BATCH, units, 0)

    out = pl.BlockSpec((SEQ, PAIR), lambda hp: (0, hp))
    return pl.pallas_call(
        body, grid=(ATTN_WIDTH // PAIR,),
        in_specs=_pair_views(0) + _pair_views(QKV_WIDTH) + _pair_views(2 * QKV_WIDTH),
        out_specs=[out, out], out_shape=[_sds((SEQ, ATTN_WIDTH)), _sds((SEQ, ATTN_WIDTH))],
        compiler_params=_cp(("parallel",)), name="attention_fwd")(*([proj] * (3 * ng)))


def _attention_bwd(g, proj, cos_f, sin_s, d_attn, attn, lse):
    dil = DILATIONS[g]
    two_blocks = SEQ // dil > BLOCK

    def body(qr_ref, kr_ref, v_ref, cf_ref, ss_ref, do_ref, o_ref, lse_ref, dq_out, dk_out, dv_out,
             dq_acc, dk_acc, dv_acc):
        dk_acc[...] = jnp.zeros_like(dk_acc)
        dv_acc[...] = jnp.zeros_like(dv_acc)
        nk = 2 * BLOCK if two_blocks else BLOCK
        first = lax.broadcasted_iota(jnp.int32, (BLOCK, PAIR), 1) < HEAD_DIM
        first_k = lax.broadcasted_iota(jnp.int32, (nk, PAIR), 1) < HEAD_DIM

        def units(t, carry):
            picked = [_unit_pieces(t * UNIT_BATCH + j, dil) for j in range(UNIT_BATCH)]

            def tiles(ref, with_prev=False):
                if with_prev and two_blocks:
                    return jnp.stack([jnp.concatenate([_load_tile(ref, prev, dil), _load_tile(ref, rows, dil)], axis=0)
                                      for _, rows, prev in picked])
                return jnp.stack([_load_tile(ref, rows, dil) for _, rows, _ in picked])

            qq = tiles(qr_ref).astype(BF16)
            kk = tiles(kr_ref, True).astype(BF16)
            vv = tiles(v_ref, True).astype(BF16)
            dof = tiles(do_ref)
            dd = dof * tiles(o_ref)
            lse3 = tiles(lse_ref)
            dob = dof.astype(BF16)
            if two_blocks:
                valid = jnp.stack([_band_mask(i, dil) for i, _, _ in picked])
            else:
                valid = _causal_mask()[None]
            zq, zf = jnp.zeros_like(qq), jnp.zeros_like(dd)
            dqs, dks, dvs = [], [], []
            for head in range(2):
                mine = first[None] if head == 0 else jnp.logical_not(first)[None]
                delta = jnp.sum(jnp.where(mine, dd, zf), axis=-1, keepdims=True)
                lse_h = lse3[:, :, head * HEAD_DIM:head * HEAD_DIM + 1]
                s = jnp.einsum("pqd,pkd->pqk", jnp.where(mine, qq, zq), kk, preferred_element_type=F32)
                p = jnp.where(valid, jnp.exp(s - lse_h), 0.0)
                dp = jnp.einsum("pqd,pkd->pqk", jnp.where(mine, dob, zq), vv, preferred_element_type=F32)
                ds = (p * (dp - delta)).astype(BF16)
                dqs.append(jnp.einsum("pqk,pkd->pqd", ds, kk, preferred_element_type=F32))
                dks.append(jnp.einsum("pqk,pqd->pkd", ds, qq, preferred_element_type=F32))
                dvs.append(jnp.einsum("pqk,pqd->pkd", p.astype(BF16), dob, preferred_element_type=F32))
            dq = jnp.where(first[None], dqs[0], dqs[1])
            dk = jnp.where(first_k[None], dks[0], dks[1])
            dv = jnp.where(first_k[None], dvs[0], dvs[1])
            for j, (_, rows, prev) in enumerate(picked):
                _store_tile(dq_acc, rows, dil, dq[j])
                _store_tile(dk_acc, rows, dil, dk[j, nk - BLOCK:], accumulate=True)
                _store_tile(dv_acc, rows, dil, dv[j, nk - BLOCK:], accumulate=True)
                if two_blocks:
                    _store_tile(dk_acc, prev, dil, dk[j, :BLOCK], accumulate=True)
                    _store_tile(dv_acc, prev, dil, dv[j, :BLOCK], accumulate=True)
            return carry

        lax.fori_loop(0, UNITS // UNIT_BATCH, units, 0)

        def finish(t, carry):
            rows = pl.ds(pl.multiple_of(t * ROPE_ROWS, ROPE_ROWS), ROPE_ROWS)
            cf, ss = cf_ref[rows, :], ss_ref[rows, :]
            dq = dq_acc[rows, :] * (1.0 / math.sqrt(HEAD_DIM))
            dq_out[rows, :] = _rope_transposed(dq, cf, ss).astype(BF16)
            dk_out[rows, :] = _rope_transposed(dk_acc[rows, :], cf, ss).astype(BF16)
            dv_out[rows, :] = dv_acc[rows, :].astype(BF16)
            return carry

        lax.fori_loop(0, SEQ // ROPE_ROWS, finish, 0)

    whole = pl.BlockSpec((SEQ, PAIR), lambda hp: (0, 0))
    pair = pl.BlockSpec((SEQ, PAIR), lambda hp: (0, hp))
    views = [_pair_views(col0)[g] for col0 in (0, QKV_WIDTH, 2 * QKV_WIDTH)]
    return pl.pallas_call(
        body, grid=(ATTN_WIDTH // PAIR,), in_specs=views + [whole, whole, pair, pair, pair],
        out_specs=[pair, pair, pair], out_shape=[_sds((SEQ, ATTN_WIDTH), BF16)] * 3,
        scratch_shapes=[pltpu.VMEM((SEQ, PAIR), F32)] * 3,
        compiler_params=_cp(("parallel",)), name=f"attention_bwd_{g}")(proj, proj, proj, cos_f, sin_s, d_attn, attn, lse)


def _cmul(ar, ai, br, bi):
    return ar * br - ai * bi, ar * bi + ai * br


def _pow256(ar, ai):
    for _ in range(8):
        ar, ai = _cmul(ar, ai, ar, ai)
    return ar, ai


def _chunk_carries(first_r, first_i, pr, pi, reverse):
    rows = lax.broadcasted_iota(jnp.int32, first_r.shape, 0)
    out_r = jnp.zeros_like(first_r)
    out_i = jnp.zeros_like(first_i)
    hr = jnp.zeros_like(first_r[0:1])
    hi = jnp.zeros_like(hr)
    order = range(SCAN_CHUNKS - 1, -1, -1) if reverse else range(SCAN_CHUNKS)
    for c in order:
        out_r = jnp.where(rows == c, hr, out_r)
        out_i = jnp.where(rows == c, hi, out_i)
        tr, ti = _cmul(pr[0:1], pi[0:1], hr, hi)
        hr = first_r[c:c + 1] + tr
        hi = first_i[c:c + 1] + ti
    return out_r, out_i


def _tile(j):
    return pl.ds(pl.multiple_of(j * SCAN_CHUNKS, SCAN_CHUNKS), SCAN_CHUNKS)


def _to_scan_rows(t):
    per = SCAN_STEPS // PHASES
    return t.reshape(PHASES, SCAN_CHUNKS, per, t.shape[1]).transpose(2, 0, 1, 3).reshape(t.shape)


def _from_scan_rows(t):
    per = SCAN_STEPS // PHASES
    return t.reshape(per, PHASES, SCAN_CHUNKS, t.shape[1]).transpose(1, 2, 0, 3).reshape(t.shape)


def _scan_in_place(hr_ref, hi_ref, a_r, a_i):
    def local(j, carry):
        tr, ti = _cmul(a_r, a_i, carry[0], carry[1])
        nr = tr + hr_ref[_tile(j), :]
        ni = ti + hi_ref[_tile(j), :]
        hr_ref[_tile(j), :] = nr
        hi_ref[_tile(j), :] = ni
        return nr, ni

    zero = jnp.zeros_like(a_r)
    last_r, last_i = lax.fori_loop(0, SCAN_STEPS, local, (zero, zero), unroll=4)
    pr, pi = _pow256(a_r, a_i)
    er, ei = _chunk_carries(last_r, last_i, pr, pi, reverse=False)

    def fix(j, carry):
        tr, ti = _cmul(carry[0], carry[1], er, ei)
        hr_ref[_tile(j), :] += tr
        hi_ref[_tile(j), :] += ti
        return _cmul(carry[0], carry[1], a_r, a_i)

    lax.fori_loop(0, SCAN_STEPS, fix, (a_r, a_i), unroll=4)
    return er, ei


def _reverse_scan_in_place(lr_ref, li_ref, hr_ref, hi_ref, er, ei, a_r, a_i):
    def local(t, carry):
        j = SCAN_STEPS - 1 - t
        tr, ti = _cmul(a_r, a_i, carry[0], carry[1])
        nr = tr + lr_ref[_tile(j), :]
        ni = ti + li_ref[_tile(j), :]
        lr_ref[_tile(j), :] = nr
        li_ref[_tile(j), :] = ni
        return nr, ni

    zero = jnp.zeros_like(a_r)
    first_r, first_i = lax.fori_loop(0, SCAN_STEPS, local, (zero, zero), unroll=4)
    pr, pi = _pow256(a_r, a_i)
    nxt_r, nxt_i = _chunk_carries(first_r, first_i, pr, pi, reverse=True)

    def accumulate(lam_r, lam_i, hp_r, hp_i, acc):
        return (acc[0] + lam_r * hp_r + lam_i * hp_i, acc[1] + lam_i * hp_r - lam_r * hp_i)

    def fix(t, carry):
        qr, qi, acc_r, acc_i = carry
        j = SCAN_STEPS - 1 - t
        tr, ti = _cmul(qr, qi, nxt_r, nxt_i)
        lam_r = lr_ref[_tile(j), :] + tr
        lam_i = li_ref[_tile(j), :] + ti
        lr_ref[_tile(j), :] = lam_r
        li_ref[_tile(j), :] = lam_i
        acc_r, acc_i = accumulate(lam_r, lam_i, hr_ref[_tile(j - 1), :], hi_ref[_tile(j - 1), :], (acc_r, acc_i))
        qr, qi = _cmul(qr, qi, a_r, a_i)
        return qr, qi, acc_r, acc_i

    qr, qi, acc_r, acc_i = lax.fori_loop(0, SCAN_STEPS - 1, fix, (a_r, a_i, zero, zero), unroll=4)
    tr, ti = _cmul(qr, qi, nxt_r, nxt_i)
    lam_r = lr_ref[_tile(0), :] + tr
    lam_i = li_ref[_tile(0), :] + ti
    lr_ref[_tile(0), :] = lam_r
    li_ref[_tile(0), :] = lam_i
    acc_r, acc_i = accumulate(lam_r, lam_i, er, ei, (acc_r, acc_i))
    return jnp.sum(acc_r, axis=0, keepdims=True), jnp.sum(acc_i, axis=0, keepdims=True)


def _rope_tables():
    half = HEAD_DIM // 2
    inv_freq = ROPE_THETA ** (-jnp.arange(half, dtype=F32) / half)
    ang = jnp.arange(SEQ, dtype=F32)[:, None] * inv_freq[None, :]
    cos, sin = jnp.cos(ang), jnp.sin(ang)
    cos_f = jnp.concatenate([cos, cos, cos, cos], axis=1)
    sin_s = jnp.concatenate([-sin, sin, -sin, sin], axis=1)
    return cos_f, sin_s


def _ssm_discretise(a_re, a_im, log_dt, b_re, b_im):
    lam = lax.complex(a_re, a_im)
    dt = jnp.exp(log_dt)[:, None]
    a_bar = jnp.exp(lam * dt)
    b_bar = ((a_bar - 1.0) / lam)[..., None] * lax.complex(b_re, b_im)
    return a_bar.real, a_bar.imag, b_bar.real, b_bar.imag


SSM_SLABS = 4
SLAB_GROUPS = SSM_GROUPS // SSM_SLABS
SLAB_IN = SSM_WIDTH // SSM_SLABS
SLAB_STATE = SSM_LANES // SSM_SLABS


def _slab_block_diag(blocks):
    _, r, c = blocks.shape
    eye = jnp.eye(SLAB_GROUPS, dtype=blocks.dtype)
    b5 = blocks.reshape(SSM_SLABS, SLAB_GROUPS, r, 1, c) * eye[None, :, None, :, None]
    return b5.reshape(SSM_SLABS, SLAB_GROUPS * r, SLAB_GROUPS * c)


def _diag_blocks(a, b):
    ra, cb = a.shape[1], b.shape[1]
    wa, wb = ra // SLAB_GROUPS, cb // SLAB_GROUPS
    d = lax.dot_general(a, b, (((0,), (0,)), ((), ())), preferred_element_type=F32)
    row_g = jnp.right_shift(lax.broadcasted_iota(jnp.int32, (ra, cb), 0), wa.bit_length() - 1)
    col_g = jnp.right_shift(lax.broadcasted_iota(jnp.int32, (ra, cb), 1), wb.bit_length() - 1)
    d = jnp.where(row_g == col_g, d, 0.0)
    fold = (jnp.bitwise_and(lax.broadcasted_iota(jnp.int32, (cb, wb), 0), wb - 1)
            == lax.broadcasted_iota(jnp.int32, (cb, wb), 1)).astype(F32)
    return jnp.dot(d, fold, preferred_element_type=F32, precision=lax.Precision.HIGHEST)


def _slab_specs():
    tok = pl.BlockSpec((SEQ, SLAB_IN), lambda j: (0, j))
    state = pl.BlockSpec((SEQ, SLAB_STATE), lambda j: (0, j))
    b_in = pl.BlockSpec((None, SLAB_IN, SLAB_STATE), lambda j: (j, 0, 0))
    c_out = pl.BlockSpec((None, SLAB_STATE, SLAB_IN), lambda j: (j, 0, 0))
    vec = pl.BlockSpec((1, SLAB_STATE), lambda j: (0, j))
    ent = pl.BlockSpec((SCAN_CHUNKS, SLAB_STATE), lambda j: (0, j))
    return tok, state, b_in, c_out, vec, ent


def _ssm_forward(u, b_in_r, b_in_i, c_out_r, c_out_ni, a_r, a_i):
    def body(u_ref, br_ref, bi_ref, cr_ref, ci_ref, ar_ref, ai_ref, y_ref, hr_ref, hi_ref, er_ref, ei_ref):
        uu = u_ref[...]
        hr_ref[...] = jnp.dot(uu, br_ref[...], preferred_element_type=F32)
        hi_ref[...] = jnp.dot(uu, bi_ref[...], preferred_element_type=F32)
        a_re = jnp.broadcast_to(ar_ref[...], (SCAN_CHUNKS, SLAB_STATE))
        a_im = jnp.broadcast_to(ai_ref[...], (SCAN_CHUNKS, SLAB_STATE))
        er_ref[...], ei_ref[...] = _scan_in_place(hr_ref, hi_ref, a_re, a_im)
        y_ref[...] = (jnp.dot(hr_ref[...].astype(BF16), cr_ref[...], preferred_element_type=F32)
                      + jnp.dot(hi_ref[...].astype(BF16), ci_ref[...], preferred_element_type=F32))

    tok, state, b_in, c_out, vec, ent = _slab_specs()
    return pl.pallas_call(
        body, grid=(SSM_SLABS,), in_specs=[tok, b_in, b_in, c_out, c_out, vec, vec],
        out_specs=[tok, state, state, ent, ent],
        out_shape=[_sds((SEQ, SSM_WIDTH)), _sds((SEQ, SSM_LANES)), _sds((SEQ, SSM_LANES)),
                   _sds((SCAN_CHUNKS, SSM_LANES)), _sds((SCAN_CHUNKS, SSM_LANES))],
        compiler_params=_cp(("parallel",)), name="ssm_forward")(u, b_in_r, b_in_i, c_out_r, c_out_ni, a_r, a_i)


def _ssm_backward(d_y, d_u_skip, u, h_r, h_i, e_r, e_i, b_in_r, b_in_i, c_out_r, c_out_ni, a_r, a_i):
    def body(dy_ref, skip_ref, u_ref, hr_ref, hi_ref, er_ref, ei_ref, br_ref, bi_ref, cr_ref, ci_ref, ar_ref, ai_ref,
             du_ref, dar_ref, dai_ref, dcr_ref, dci_ref, dbr_ref, dbi_ref, lr_ref, li_ref):
        dy = dy_ref[...]
        lr_ref[...] = _dot_nt(dy, cr_ref[...])
        li_ref[...] = _dot_nt(dy, ci_ref[...])
        a_re = jnp.broadcast_to(ar_ref[...], (SCAN_CHUNKS, SLAB_STATE))
        a_im = -jnp.broadcast_to(ai_ref[...], (SCAN_CHUNKS, SLAB_STATE))
        dar_ref[...], dai_ref[...] = _reverse_scan_in_place(lr_ref, li_ref, hr_ref, hi_ref, er_ref[...], ei_ref[...],
                                                            a_re, a_im)
        dcr_ref[...] = _diag_blocks(dy, hr_ref[...].astype(BF16))
        dci_ref[...] = _diag_blocks(dy, hi_ref[...].astype(BF16))
        lam_r, lam_i = lr_ref[...].astype(BF16), li_ref[...].astype(BF16)
        uu = u_ref[...]
        dbr_ref[...] = _diag_blocks(uu, lam_r)
        dbi_ref[...] = _diag_blocks(uu, lam_i)
        du = skip_ref[...] + _dot_nt(lam_r, br_ref[...]) + _dot_nt(lam_i, bi_ref[...])
        du_ref[...] = du.astype(BF16)

    tok, state, b_in, c_out, vec, ent = _slab_specs()
    db = pl.BlockSpec((SLAB_IN, SSM_STATE), lambda j: (j, 0))
    return pl.pallas_call(
        body, grid=(SSM_SLABS,), in_specs=[tok, tok, tok, state, state, ent, ent, b_in, b_in, c_out, c_out, vec, vec],
        out_specs=[tok, vec, vec, db, db, db, db],
        out_shape=[_sds((SEQ, SSM_WIDTH), BF16), _sds((1, SSM_LANES)), _sds((1, SSM_LANES))]
        + [_sds((SSM_WIDTH, SSM_STATE))] * 4,
        scratch_shapes=[pltpu.VMEM((SEQ, SLAB_STATE), F32)] * 2,
        compiler_params=_cp(("parallel",)), name="ssm_backward")(
            d_y, d_u_skip, u, h_r, h_i, e_r, e_i, b_in_r, b_in_i, c_out_r, c_out_ni, a_r, a_i)


FF_ROWS = 1024
FF_SHARD = D_FF // N_CHIPS


def _dot_nt(a, b):
    return lax.dot_general(a, b, (((1,), (1,)), ((), ())), preferred_element_type=F32)


def _ffn_up(h, w_gate_t, w_up_t):
    def body(h_ref, wg_ref, wu_ref, a_ref, b_ref, act_ref):
        hb = h_ref[...].astype(BF16)
        a = _dot_nt(hb, wg_ref[...])
        b = _dot_nt(hb, wu_ref[...])
        a_ref[...] = a
        b_ref[...] = b
        act_ref[...] = (a * jax.nn.sigmoid(a) * b).astype(BF16)

    w_spec = pl.BlockSpec((None, FF_SHARD, D_MODEL), lambda i, k: (k, 0, 0))
    o_spec = pl.BlockSpec((None, FF_ROWS, FF_SHARD), lambda i, k: (k, i, 0))
    shape = (N_CHIPS, SEQ, FF_SHARD)
    return pl.pallas_call(
        body, grid=(SEQ // FF_ROWS, N_CHIPS),
        in_specs=[pl.BlockSpec((FF_ROWS, D_MODEL), lambda i, k: (i, 0)), w_spec, w_spec],
        out_specs=[o_spec, o_spec, o_spec], out_shape=[_sds(shape), _sds(shape), _sds(shape, BF16)],
        compiler_params=_cp(("parallel", "parallel")), name="ffn_up")(h, w_gate_t, w_up_t)


def _ffn_down_ln2_loss(act, w_down, h, tgt, ln_g, ln_b):
    def body(act_ref, w_ref, h_ref, tgt_ref, g_ref, b_ref, dz_ref, loss_ref, dg_ref, db_ref, acc):
        i, k = pl.program_id(0), pl.program_id(1)
        part = jnp.dot(act_ref[...], w_ref[...], preferred_element_type=F32)

        @pl.when(k == 0)
        def _():
            acc[...] = part

        @pl.when(k > 0)
        def _():
            acc[...] += part

        @pl.when(k == N_CHIPS - 1)
        def _():
            g = g_ref[...]
            xhat, rstd = _ln_stats(DN_ALPHA * h_ref[...] + acc[...])
            err = xhat * g + b_ref[...] - tgt_ref[...]
            d_out = err * (1.0 / D_MODEL)
            dz_ref[...] = _ln_bwd(d_out, xhat, rstd, g)
            loss_rows = jnp.sum(err * err, axis=-1, keepdims=True) * (0.5 / D_MODEL)
            sums = (jnp.broadcast_to(jnp.sum(loss_rows, axis=0, keepdims=True), loss_ref.shape),
                    _colsum(d_out * xhat), _colsum(d_out))
            for ref, val in zip((loss_ref, dg_ref, db_ref), sums):
                @pl.when(i == 0)
                def _(ref=ref, val=val):
                    ref[...] = val

                @pl.when(i > 0)
                def _(ref=ref, val=val):
                    ref[...] += val

    row = pl.BlockSpec((FF_ROWS, D_MODEL), lambda i, k: (i, 0))
    vec = pl.BlockSpec((1, D_MODEL), lambda i, k: (0, 0))
    return pl.pallas_call(
        body, grid=(SEQ // FF_ROWS, N_CHIPS),
        in_specs=[pl.BlockSpec((None, FF_ROWS, FF_SHARD), lambda i, k: (k, i, 0)),
                  pl.BlockSpec((None, FF_SHARD, D_MODEL), lambda i, k: (k, 0, 0)), row, row, vec, vec],
        out_specs=[row, pl.BlockSpec((1, BLOCK), lambda i, k: (0, 0)), vec, vec],
        out_shape=[_sds((SEQ, D_MODEL)), _sds((1, BLOCK)), _sds((1, D_MODEL)), _sds((1, D_MODEL))],
        scratch_shapes=[pltpu.VMEM((FF_ROWS, D_MODEL), F32)],
        compiler_params=_cp(("arbitrary", "arbitrary")), name="ffn_down_ln2_loss")(act, w_down, h, tgt, ln_g, ln_b)


def _ffn_down_bwd(dz, w_down, a, b):
    def body(dz_ref, wd_ref, a_ref, b_ref, da_ref, db_ref):
        d_act = _dot_nt(dz_ref[...].astype(BF16), wd_ref[...])
        av = a_ref[...]
        sg = jax.nn.sigmoid(av)
        da_ref[...] = (d_act * b_ref[...] * sg * (1.0 + av * (1.0 - sg))).astype(BF16)
        db_ref[...] = (d_act * av * sg).astype(BF16)

    t_spec = pl.BlockSpec((None, FF_ROWS, FF_SHARD), lambda i, k: (k, i, 0))
    shape = (N_CHIPS, SEQ, FF_SHARD)
    return pl.pallas_call(
        body, grid=(SEQ // FF_ROWS, N_CHIPS),
        in_specs=[pl.BlockSpec((FF_ROWS, D_MODEL), lambda i, k: (i, 0)),
                  pl.BlockSpec((None, FF_SHARD, D_MODEL), lambda i, k: (k, 0, 0)), t_spec, t_spec],
        out_specs=[t_spec, t_spec], out_shape=[_sds(shape, BF16), _sds(shape, BF16)],
        compiler_params=_cp(("parallel", "parallel")), name="ffn_down_bwd")(dz, w_down, a, b)


def _ffn_dh(d_a, d_b, w_gate_t, w_up_t):
    def body(da_ref, db_ref, wg_ref, wu_ref, o_ref, acc):
        k = pl.program_id(1)
        part = (jnp.dot(da_ref[...], wg_ref[...], preferred_element_type=F32)
                + jnp.dot(db_ref[...], wu_ref[...], preferred_element_type=F32))

        @pl.when(k == 0)
        def _():
            acc[...] = part

        @pl.when(k > 0)
        def _():
            acc[...] += part

        @pl.when(k == N_CHIPS - 1)
        def _():
            o_ref[...] = acc[...]

    t_spec = pl.BlockSpec((None, FF_ROWS, FF_SHARD), lambda i, k: (k, i, 0))
    w_spec = pl.BlockSpec((None, FF_SHARD, D_MODEL), lambda i, k: (k, 0, 0))
    return pl.pallas_call(
        body, grid=(SEQ // FF_ROWS, N_CHIPS), in_specs=[t_spec, t_spec, w_spec, w_spec],
        out_specs=pl.BlockSpec((FF_ROWS, D_MODEL), lambda i, k: (i, 0)), out_shape=_sds((SEQ, D_MODEL)),
        scratch_shapes=[pltpu.VMEM((FF_ROWS, D_MODEL), F32)],
        compiler_params=_cp(("parallel", "arbitrary")), name="ffn_dh")(d_a, d_b, w_gate_t, w_up_t)


def _local_step(x, tgt, wts, small):
    s = SEQ
    cos_f, sin_s = [_to_phase_rows(t) for t in _rope_tables()]
    x = _reorder_rows(x, to_phase=True, name="phase_rows_x")
    tgt = _reorder_rows(tgt, to_phase=True, name="phase_rows_target")

    proj = _project_in(x, wts["w_in"], cos_f, sin_s)

    attn, lse = _attention_fwd(proj)

    (abar_r, abar_i, bbar_r, bbar_i), ssm_vjp = jax.vjp(
        _ssm_discretise, small["ssm_a_re"], small["ssm_a_im"], small["ssm_log_dt"], small["ssm_b_re"], small["ssm_b_im"])
    b_in_r, b_in_i = [_slab_block_diag(b.transpose(0, 2, 1)).astype(BF16) for b in (bbar_r, bbar_i)]
    c_out_r = _slab_block_diag(small["ssm_c_re"].transpose(0, 2, 1)).astype(BF16)
    c_out_ni = _slab_block_diag(-small["ssm_c_im"].transpose(0, 2, 1)).astype(BF16)
    a_r, a_i = abar_r.reshape(1, SSM_LANES), abar_i.reshape(1, SSM_LANES)
    d_skip = small["ssm_d"].reshape(1, SSM_WIDTH)

    u_f = _to_scan_rows(proj[:, 3 * QKV_WIDTH:3 * QKV_WIDTH + SSM_WIDTH])
    u_p = u_f.astype(BF16)
    y_c, h_r, h_i, e_r, e_i = _ssm_forward(u_p, b_in_r, b_in_i, c_out_r, c_out_ni, a_r, a_i)

    def branch(t, wg):
        return jnp.concatenate([jnp.dot(t, wg[k], preferred_element_type=F32) for k in range(N_CHIPS)], axis=1)

    def branch_t(t, wg):
        ns = wg.shape[2]
        return sum(_dot_nt(t[:, k * ns:(k + 1) * ns], wg[k]) for k in range(N_CHIPS))

    def gelu_glu(yc, u, dsk, wg):
        y = yc + dsk * u
        gel = (0.5 * y * (1.0 + jnp.tanh(GELU_C * (y + GELU_K * y * y * y)))).astype(BF16)
        glu = branch(gel, wg)
        return y, gel, glu, glu[:, :SSM_WIDTH] * jax.nn.sigmoid(glu[:, SSM_WIDTH:])

    y_s5, gel, glu, y_glu = _rowwise(
        gelu_glu, [y_c, u_f], [d_skip, wts["w_glu"]],
        [_sds((s, SSM_WIDTH)), _sds((s, SSM_WIDTH), BF16), _sds((s, 2 * SSM_WIDTH)), _sds((s, SSM_WIDTH), BF16)],
        tm=512, name="ssm_gelu_glu")
    y_glu = _from_scan_rows(y_glu)

    gl0 = (proj, D_MODEL, (3 * QKV_WIDTH + SSM_WIDTH) // D_MODEL)
    gl1 = (proj, D_MODEL, (3 * QKV_WIDTH + SSM_WIDTH) // D_MODEL + 1)
    b_gate = small["b_gate"]
    w_out = wts["w_out"].reshape(D_MODEL, D_MODEL)

    def mix_ln1(l0, l1, at, yg, xv, bg, wa, ws, wo, g, b):
        ya = branch(at.astype(BF16), wa)
        ys = branch(yg, ws)
        mixed = (jax.nn.sigmoid(l0 + bg[0:1]) * ya + jax.nn.sigmoid(l1 + bg[1:2]) * ys).astype(BF16)
        z = DN_ALPHA * xv + jnp.dot(mixed, wo, preferred_element_type=F32)
        xhat, _ = _ln_stats(z)
        return ya, ys, mixed, z, xhat * g + b

    y_attn, y_ssm, mixed, z1, h = _rowwise(
        mix_ln1, [gl0, gl1, attn, y_glu, x],
        [b_gate, wts["w_attn_br"], wts["w_ssm_br"], w_out, small["ln1_g"], small["ln1_b"]],
        [_sds((s, D_MODEL)), _sds((s, D_MODEL)), _sds((s, D_MODEL), BF16), _sds((s, D_MODEL)), _sds((s, D_MODEL))],
        tm=256, name="mix_ln1")

    nf = D_FF // N_CHIPS
    w_gate_t, w_up_t, w_down = wts["w_ff_gate"], wts["w_ff_up"], wts["w_ff_down"]
    ff_a, ff_b, act = _ffn_up(h, w_gate_t, w_up_t)
    dz2, loss_v, d_ln2_g, d_ln2_b = _ffn_down_ln2_loss(act, w_down, h, tgt, small["ln2_g"], small["ln2_b"])

    d_a, d_b = _ffn_down_bwd(dz2, w_down, ff_a, ff_b)

    def grad_rows(lhs, rhs, name):
        return _matmul(lhs, rhs, grid=(N_CHIPS,), a_spec=pl.BlockSpec((None, s, nf), lambda k: (k, 0, 0)),
                       b_spec=pl.BlockSpec((s, D_MODEL), lambda k: (0, 0)),
                       o_spec=pl.BlockSpec((None, nf, D_MODEL), lambda k: (k, 0, 0)),
                       out_shape=_sds((N_CHIPS, nf, D_MODEL), BF16), dims=(0, 0), name=name)

    g_w_ff_down = grad_rows(act, dz2, "g_w_ff_down")
    g_w_ff_gate = grad_rows(d_a, h, "g_w_ff_gate")
    g_w_ff_up = grad_rows(d_b, h, "g_w_ff_up")
    dh_ff = _ffn_dh(d_a, d_b, w_gate_t, w_up_t)

    def ln1_gate_bwd(dz, dff, z, l0, l1, ya, ys, g, bg, wo, wa, ws):
        xhat, rstd = _ln_stats(z)
        dh = DN_ALPHA * dz + dff
        dz_in = _ln_bwd(dh, xhat, rstd, g)
        dm = _dot_nt(dz_in.astype(BF16), wo)
        g0 = jax.nn.sigmoid(l0 + bg[0:1])
        g1 = jax.nn.sigmoid(l1 + bg[1:2])
        dl0 = dm * ya * g0 * (1.0 - g0)
        dl1 = dm * ys * g1 * (1.0 - g1)
        dya, dys = (dm * g0).astype(BF16), (dm * g1).astype(BF16)
        return (dz_in, dya, dys, jnp.concatenate([dl0, dl1], axis=1), branch_t(dya, wa), branch_t(dys, ws),
                _colsum(dh * xhat), _colsum(dh), _colsum(dl0), _colsum(dl1))

    dz1, d_y_attn, d_y_ssm, d_gl, d_attn, d_y_glu, d_ln1_g, d_ln1_b, d_bg0, d_bg1 = _rowwise(
        ln1_gate_bwd, [dz2, dh_ff, z1, gl0, gl1, y_attn, y_ssm],
        [small["ln1_g"], b_gate, w_out, wts["w_attn_br"], wts["w_ssm_br"]],
        [_sds((s, D_MODEL)), _sds((s, D_MODEL), BF16), _sds((s, D_MODEL), BF16), _sds((s, 2 * D_MODEL), BF16),
         _sds((s, ATTN_WIDTH)), _sds((s, SSM_WIDTH))],
        [_sds((1, D_MODEL))] * 4, tm=256, name="ln1_gate_bwd")
    g_w_out = _mm_rows_tn(mixed, dz1, name="g_w_out")

    g_w_ssm_br = _mm_cols_tn(y_glu, d_y_ssm, ns=D_MODEL // N_CHIPS, name="g_w_ssm_br")
    d_y_glu = _to_scan_rows(d_y_glu)

    def glu_gelu_bwd(dyg, gl, y, u, dsk, wg):
        ga, gb = gl[:, :SSM_WIDTH], gl[:, SSM_WIDTH:]
        sg = jax.nn.sigmoid(gb)
        d_gl = jnp.concatenate([dyg * sg, dyg * ga * sg * (1.0 - sg)], axis=1).astype(BF16)
        dg = branch_t(d_gl, wg)
        th = jnp.tanh(GELU_C * (y + GELU_K * y * y * y))
        dy = dg * (0.5 * (1.0 + th) + 0.5 * y * (1.0 - th * th) * GELU_C * (1.0 + 3.0 * GELU_K * y * y))
        return d_gl, dy, dy * dsk, _colsum(dy * u)

    d_glu, d_y, d_u_skip, d_ssm_d = _rowwise(
        glu_gelu_bwd, [d_y_glu, glu, y_s5, u_f], [d_skip, wts["w_glu"]],
        [_sds((s, 2 * SSM_WIDTH), BF16), _sds((s, SSM_WIDTH), BF16), _sds((s, SSM_WIDTH))], [_sds((1, SSM_WIDTH))],
        tm=512, name="glu_gelu_bwd")
    g_w_glu = _mm_cols_tn(gel, d_glu, ns=2 * SSM_WIDTH // N_CHIPS, name="g_w_glu")
    d_u, d_abar_r, d_abar_i, d_c_r, d_c_ni, d_bin_r, d_bin_i = _ssm_backward(
        d_y, d_u_skip, u_p, h_r, h_i, e_r, e_i, b_in_r, b_in_i, c_out_r, c_out_ni, a_r, a_i)
    d_u = _from_scan_rows(d_u)
    d_bbar_r = d_bin_r.reshape(SSM_GROUPS, SSM_GROUP, SSM_STATE).transpose(0, 2, 1)
    d_bbar_i = d_bin_i.reshape(SSM_GROUPS, SSM_GROUP, SSM_STATE).transpose(0, 2, 1)
    d_a_re, d_a_im, d_log_dt, d_b_re, d_b_im = ssm_vjp(
        (d_abar_r.reshape(SSM_GROUPS, SSM_STATE), d_abar_i.reshape(SSM_GROUPS, SSM_STATE), d_bbar_r, d_bbar_i))
    d_c_re = d_c_r.reshape(SSM_GROUPS, SSM_GROUP, SSM_STATE)
    d_c_im = -d_c_ni.reshape(SSM_GROUPS, SSM_GROUP, SSM_STATE)

    g_w_attn_br = _mm_cols_tn(attn, d_y_attn, ns=D_MODEL // N_CHIPS, name="g_w_attn_br")
    dqkv = [_attention_bwd(g, proj, cos_f, sin_s, d_attn, attn, lse) for g in range(len(DILATIONS))]

    d_proj = jnp.concatenate([dqkv[g][j] for j in range(3) for g in range(len(DILATIONS))] + [d_u, d_gl],
                             axis=1)
    g_w_in = _mm_cols_tn(x, d_proj, ns=IN_WIDTH // N_CHIPS, name="g_w_in")

    def grad_x_after(after):
        dx_proj = _mm_cols_nt(d_proj, wts["w_in"], tm=1024, name="dx_proj", after=after)
        return _reorder_rows(dz1, dx_proj, to_phase=False, name="grad_x", scale=DN_ALPHA)

    big = {"w_in": g_w_in, "w_attn_br": g_w_attn_br, "w_ssm_br": g_w_ssm_br, "w_out": g_w_out, "w_glu": g_w_glu,
           "w_ff_gate": g_w_ff_gate, "w_ff_up": g_w_ff_up, "w_ff_down": g_w_ff_down}
    small_g = {"b_gate": jnp.concatenate([d_bg0, d_bg1], axis=0), "ssm_a_re": d_a_re, "ssm_a_im": d_a_im,
               "ssm_log_dt": d_log_dt, "ssm_b_re": d_b_re, "ssm_b_im": d_b_im, "ssm_c_re": d_c_re, "ssm_c_im": d_c_im,
               "ssm_d": d_ssm_d.reshape(SSM_WIDTH), "ln1_g": d_ln1_g, "ln1_b": d_ln1_b, "ln2_g": d_ln2_g,
               "ln2_b": d_ln2_b}
    marks = {"ln1_bwd": dz1, "scan_bwd": d_abar_r, "attention_bwd_0": dqkv[0][0]}
    return loss_v[0, 0], grad_x_after, big, small_g, marks


GATHER_ID, SWAP_ID, SCATTER_ID, JOIN_ID, EXCHANGE_ID = 1, 2, 3, 4, 5


def _place():
    return lax.axis_index("x"), lax.axis_index("y"), lax.axis_index("c")


def _other_chips(x, y):
    return [(1 - x, y), (x, 1 - y), (1 - x, 1 - y)]


def _handshake(peers):
    barrier = pltpu.get_barrier_semaphore()
    for peer in peers:
        pl.semaphore_signal(barrier, inc=1, device_id=peer, device_id_type=MESH)
    pl.semaphore_wait(barrier, len(peers))


def _sequencer(body, arrays, out_type, sems, collective_id, name):
    return pl.kernel(body, name=name, out_type=out_type,
                     mesh=plsc.ScalarSubcoreMesh(axis_name="sequencer", num_cores=1), scratch_types=sems,
                     compiler_params=pltpu.CompilerParams(collective_id=collective_id))(*arrays)


def _gather_weights(shards, *, name):
    nw = len(shards)

    def body(*refs):
        ins, outs = refs[:nw], refs[nw:2 * nw]
        send_sems, recv_sems, pass_send, pass_recv, local_sems = refs[2 * nw:]
        x, y, c = _place()
        chip = 2 * x + y
        chips = _other_chips(x, y)
        _handshake([(x, y, 1 - c)] + [(cx, cy, c) for cx, cy in chips])
        started = []
        for w in range(nw):
            hw = shards[w].shape[0] // 2
            mine = pl.ds(c * hw, hw)
            own = pltpu.make_async_copy(ins[w], outs[w].at[chip], local_sems.at[w])
            own.start()
            started.append(own)
            for j, (cx, cy) in enumerate(chips):
                cp = pltpu.make_async_remote_copy(
                    src_ref=ins[w].at[mine], dst_ref=outs[w].at[chip, mine], send_sem=send_sems.at[w, j],
                    recv_sem=recv_sems.at[w, j], device_id=(cx, cy, c), device_id_type=MESH)
                cp.start()
                started.append(cp)
        passed = []
        for w in range(nw):
            hw = shards[w].shape[0] // 2
            mine = pl.ds(c * hw, hw)
            for j, (cx, cy) in enumerate(chips):
                landed = outs[w].at[2 * cx + cy, mine]
                pltpu.make_async_remote_copy(
                    src_ref=ins[w].at[mine], dst_ref=landed, send_sem=send_sems.at[w, j],
                    recv_sem=recv_sems.at[w, j], device_id=(cx, cy, c), device_id_type=MESH).wait_recv()
                cp = pltpu.make_async_remote_copy(
                    src_ref=landed, dst_ref=landed, send_sem=pass_send.at[w, j], recv_sem=pass_recv.at[w, j],
                    device_id=(x, y, 1 - c), device_id_type=MESH)
                cp.start()
                passed.append(cp)
        for w in range(nw):
            hw = shards[w].shape[0] // 2
            theirs = pl.ds((1 - c) * hw, hw)
            for j, (cx, cy) in enumerate(chips):
                landed = outs[w].at[2 * cx + cy, theirs]
                pltpu.make_async_remote_copy(
                    src_ref=landed, dst_ref=landed, send_sem=pass_send.at[w, j], recv_sem=pass_recv.at[w, j],
                    device_id=(x, y, 1 - c), device_id_type=MESH).wait_recv()
        for cp in started[0::4]:
            cp.wait()
        for cp in [s for i, s in enumerate(started) if i % 4] + passed:
            cp.wait_send()

    sem = pltpu.SemaphoreType.DMA
    return _sequencer(body, shards, [_sds((N_CHIPS,) + a.shape, a.dtype) for a in shards],
                      [sem((nw, 3)), sem((nw, 3)), sem((nw, 3)), sem((nw, 3)), sem((nw,))], GATHER_ID, name)


def _swap_other_halves(grads, *, name):
    nw = len(grads)

    def body(*refs):
        ins, outs = refs[:nw], refs[nw:2 * nw]
        send_sems, recv_sems = refs[2 * nw:]
        x, y, c = _place()
        _handshake([(x, y, 1 - c)])
        cps = []
        for w in range(nw):
            hw = grads[w].shape[1] // 2
            cp = pltpu.make_async_remote_copy(
                src_ref=ins[w].at[:, pl.ds((1 - c) * hw, hw)], dst_ref=outs[w], send_sem=send_sems.at[w],
                recv_sem=recv_sems.at[w], device_id=(x, y, 1 - c), device_id_type=MESH)
            cp.start()
            cps.append(cp)
        for cp in cps:
            cp.wait()

    sem = pltpu.SemaphoreType.DMA
    return _sequencer(body, grads, [_sds((N_CHIPS, g.shape[1] // 2, g.shape[2]), g.dtype) for g in grads],
                      [sem((nw,)), sem((nw,))], SWAP_ID, name)


def _add_my_halves(core, grads, others, *, name, after=()):
    nw = len(grads)
    halves = [g.shape[1] // 2 for g in grads]

    def body(core_ref, *refs):
        outs = refs[2 * nw + len(after):]
        for g_ref, o_ref, out_ref in zip(refs[:nw], refs[nw:2 * nw], outs):
            out_ref[...] = (g_ref[...].astype(F32) + o_ref[...].astype(F32)).astype(out_ref.dtype)

    in_specs = [pl.BlockSpec((None, None, hw, g.shape[2]), lambda s, core_ref: (s, core_ref[0], 0, 0))
                for g, hw in zip(grads, halves)]
    in_specs += [pl.BlockSpec((None, hw, g.shape[2]), lambda s, core_ref: (s, 0, 0)) for g, hw in zip(grads, halves)]
    return pl.pallas_call(
        body,
        grid_spec=pltpu.PrefetchScalarGridSpec(
            num_scalar_prefetch=1, grid=(N_CHIPS,), in_specs=in_specs + [HBM_OPERAND] * len(after),
            out_specs=[pl.BlockSpec((None, hw, g.shape[2]), lambda s, core_ref: (s, 0, 0))
                       for g, hw in zip(grads, halves)]),
        out_shape=[_sds((N_CHIPS, hw, g.shape[2]), BF16) for g, hw in zip(grads, halves)],
        compiler_params=_cp(("parallel",)), name=name)(
            core, *[g.reshape(N_CHIPS, 2, hw, g.shape[2]) for g, hw in zip(grads, halves)], *others, *after)


def _scatter_partials(parts, *, name):
    nw = len(parts)

    def body(*refs):
        ins, outs = refs[:nw], refs[nw:2 * nw]
        send_sems, recv_sems = refs[2 * nw:]
        x, y, c = _place()
        _handshake([(cx, cy, c) for cx, cy in _other_chips(x, y)])
        cps = []
        for w in range(nw):
            for j, (cx, cy) in enumerate(_other_chips(x, y)):
                cp = pltpu.make_async_remote_copy(
                    src_ref=ins[w].at[2 * cx + cy], dst_ref=outs[w].at[j], send_sem=send_sems.at[w, j],
                    recv_sem=recv_sems.at[w, j], device_id=(cx, cy, c), device_id_type=MESH)
                cp.start()
                cps.append(cp)
        for cp in cps:
            cp.wait()

    sem = pltpu.SemaphoreType.DMA
    return _sequencer(body, parts, [_sds((3,) + p.shape[1:], p.dtype) for p in parts],
                      [sem((nw, 3)), sem((nw, 3))], SCATTER_ID, name)


SUM_STEPS = 2


def _sum_partials(chip, parts, recvd, *, name, after=()):
    nw = len(parts)
    rows = [p.shape[1] // SUM_STEPS for p in parts]

    def body(chip_ref, *refs):
        outs = refs[2 * nw + len(after):]
        for p_ref, r_ref, out_ref in zip(refs[:nw], refs[nw:2 * nw], outs):
            acc = p_ref[...].astype(F32)
            for j in range(3):
                acc = acc + r_ref[j].astype(F32)
            out_ref[...] = acc

    in_specs = [pl.BlockSpec((None, th, p.shape[2]), lambda i, chip_ref: (chip_ref[0], i, 0))
                for p, th in zip(parts, rows)]
    in_specs += [pl.BlockSpec((3, th, p.shape[2]), lambda i, chip_ref: (0, i, 0)) for p, th in zip(parts, rows)]
    return pl.pallas_call(
        body,
        grid_spec=pltpu.PrefetchScalarGridSpec(
            num_scalar_prefetch=1, grid=(SUM_STEPS,), in_specs=in_specs + [HBM_OPERAND] * len(after),
            out_specs=[pl.BlockSpec((th, p.shape[2]), lambda i, chip_ref: (i, 0)) for p, th in zip(parts, rows)]),
        out_shape=[_sds(p.shape[1:]) for p in parts], compiler_params=_cp(("parallel",)), name=name)(
            chip, *parts, *recvd, *after)


def _swap_reduced_halves(halves, *, name):
    nw = len(halves)

    def body(*refs):
        ins, outs = refs[:nw], refs[nw:2 * nw]
        send_sems, recv_sems = refs[2 * nw:]
        x, y, c = _place()
        _handshake([(x, y, 1 - c)])
        cps = []
        for w in range(nw):
            cp = pltpu.make_async_remote_copy(
                src_ref=ins[w], dst_ref=outs[w], send_sem=send_sems.at[w], recv_sem=recv_sems.at[w],
                device_id=(x, y, 1 - c), device_id_type=MESH)
            cp.start()
            cps.append(cp)
        for cp in cps:
            cp.wait()

    sem = pltpu.SemaphoreType.DMA
    return _sequencer(body, halves, [_sds(h.shape, h.dtype) for h in halves], [sem((nw,)), sem((nw,))], JOIN_ID, name)


def _exchange_rows(vec, *, name):
    def body(v_ref, slots, send_sems, recv_sems, local_sem):
        x, y, c = _place()
        me = 4 * x + 2 * y + c
        peers = []
        for mask in range(1, N_DEV):
            peers.append((1 - x if mask & 4 else x, 1 - y if mask & 2 else y, 1 - c if mask & 1 else c))
        _handshake(peers)
        own = pltpu.make_async_copy(v_ref, slots.at[me], local_sem)
        own.start()
        cps = []
        for k, peer in enumerate(peers):
            cp = pltpu.make_async_remote_copy(
                src_ref=v_ref, dst_ref=slots.at[me], send_sem=send_sems.at[k], recv_sem=recv_sems.at[k],
                device_id=peer, device_id_type=MESH)
            cp.start()
            cps.append(cp)
        for k, (px, py, pc) in enumerate(peers):
            pltpu.make_async_remote_copy(
                src_ref=v_ref, dst_ref=slots.at[4 * px + 2 * py + pc], send_sem=send_sems.at[k],
                recv_sem=recv_sems.at[k], device_id=(px, py, pc), device_id_type=MESH).wait_recv()
        for cp in cps:
            cp.wait_send()
        own.wait()

    sem = pltpu.SemaphoreType.DMA
    return _sequencer(body, [vec], [_sds((N_DEV,) + vec.shape)], [sem((N_DEV - 1,)), sem((N_DEV - 1,)), sem(())],
                      EXCHANGE_ID, name)[0]


def _sum_slots(slots, *, name, after=()):
    def body(s_ref, *rest):
        out_ref = rest[len(after)]
        acc = s_ref[0]
        for d in range(1, N_DEV):
            acc = acc + s_ref[d]
        out_ref[...] = acc

    vmem = pl.BlockSpec(memory_space=pltpu.VMEM)
    return pl.pallas_call(
        body, in_specs=[vmem] + [HBM_OPERAND] * len(after), out_specs=vmem, out_shape=_sds(slots.shape[1:]),
        compiler_params=pltpu.CompilerParams(vmem_limit_bytes=VMEM_LIMIT_BYTES), name=name)(slots, *after)


def _reduce_scatter_start(grads, core, *, tag, add_after=()):
    others = _swap_other_halves(grads, name="swap_other_halves_" + tag)
    parts = _add_my_halves(core, grads, others, name="add_my_halves_" + tag, after=add_after)
    return parts, _scatter_partials(parts, name="scatter_partials_" + tag)


def _reduce_scatter_finish(parts, recvd, chip, *, tag, sum_after=()):
    mine = _sum_partials(chip, parts, recvd, name="sum_partials_" + tag, after=sum_after)
    return mine, _swap_reduced_halves(mine, name="swap_reduced_halves_" + tag)


ADAM_BLOCK_ELEMS = 256 * 1024


def _adam_rows(rows, cols):
    tm = rows
    while tm * cols > ADAM_BLOCK_ELEMS and tm % 16 == 0:
        tm //= 2
    return tm


def _adam_step(wv, gv, mv, vv):
    m2 = ADAM_B1 * mv + (1.0 - ADAM_B1) * gv
    v2 = ADAM_B2 * vv + (1.0 - ADAM_B2) * (gv * gv)
    m_hat = m2 / (1.0 - ADAM_B1 ** ADAM_STEP)
    v_hat = v2 / (1.0 - ADAM_B2 ** ADAM_STEP)
    return -ADAM_LR * (m_hat / (jnp.sqrt(v_hat) + ADAM_EPS) + ADAM_WD * wv), m2, v2


def _adamw(w, g, m, v, *, name):
    rows, cols = w.shape
    return _rowwise(_adam_step, [w, g, m, v], [], [_sds((rows, cols))] * 3, tm=_adam_rows(rows, cols), name=name)


def _adamw_halves(core, w, g_mine, g_theirs, m, v, *, name, after=()):
    rows, cols = w.shape
    hw = rows // 2
    tm = _adam_rows(hw, cols)
    per_half = hw // tm

    def body(core_ref, w_ref, gm_ref, gt_ref, m_ref, v_ref, *rest):
        g_out, d_out, m_out, v_out = rest[len(after):]
        mine = (pl.program_id(0) // per_half) == core_ref[0]
        g = jnp.where(mine, gm_ref[...], gt_ref[...])
        d, m2, v2 = _adam_step(w_ref[...], g, m_ref[...], v_ref[...])
        g_out[...] = g
        d_out[...] = d
        m_out[...] = m2
        v_out[...] = v2

    full = pl.BlockSpec((tm, cols), lambda i, core_ref: (i, 0))

    def half(wanted):
        def index(i, core_ref):
            in_use = ((i // per_half) == core_ref[0]) == wanted
            return (jnp.where(in_use, i % per_half, 0), 0)
        return pl.BlockSpec((tm, cols), index)

    return pl.pallas_call(
        body,
        grid_spec=pltpu.PrefetchScalarGridSpec(
            num_scalar_prefetch=1, grid=(rows // tm,),
            in_specs=[full, half(True), half(False), full, full] + [HBM_OPERAND] * len(after),
            out_specs=[full, full, full, full]),
        out_shape=[_sds((rows, cols))] * 4, compiler_params=_cp(("parallel",)), name=name)(
            core, w, g_mine, g_theirs, m, v, *after)


HELD_TRANSPOSED = ("w_ff_gate", "w_ff_up")


def _as_rows(name, arr):
    return arr[0].T if name in HELD_TRANSPOSED else arr[0]


def _from_rows(name, arr2d):
    return (arr2d.T if name in HELD_TRANSPOSED else arr2d)[None]


STORED_SWAPPED = ("ssm_b_re", "ssm_b_im")


def _as_stored(name, arr):
    return jnp.swapaxes(arr, -1, -2) if name in STORED_SWAPPED else arr


def _pack_rows(arrs):
    flat = jnp.concatenate([a.reshape(-1).astype(F32) for a in arrs])
    rows = -(-flat.shape[0] // 1024) * 8
    return jnp.pad(flat, (0, rows * 128 - flat.shape[0])).reshape(rows, 128)


def _unpack_rows(vec, shapes):
    flat = vec.reshape(-1)
    out, off = [], 0
    for shp in shapes:
        size = math.prod(shp)
        out.append(flat[off:off + size].reshape(shp))
        off += size
    return out


SMALL = ("b_gate", "ssm_a_re", "ssm_a_im", "ssm_log_dt", "ssm_b_re", "ssm_b_im", "ssm_c_re", "ssm_c_im", "ssm_d",
         "ln1_g", "ln1_b", "ln2_g", "ln2_b")
GATHER_GROUPS = (("w_in", ("w_in",)), ("mixer", ("w_attn_br", "w_ssm_br", "w_glu", "w_out")),
                 ("ffn", ("w_ff_gate", "w_ff_up", "w_ff_down")))
REDUCE_GROUPS = (("ffn", ("w_ff_down", "w_ff_gate", "w_ff_up")),
                 ("mixer", ("w_out", "w_ssm_br", "w_glu", "w_attn_br")), ("w_in", ("w_in",)))
WEIGHTS = ("w_in", "b_gate", "w_attn_br", "w_ssm_br", "w_out", "ssm_a_re", "ssm_a_im", "ssm_log_dt", "ssm_b_re",
           "ssm_b_im", "ssm_c_re", "ssm_c_im", "ssm_d", "w_glu", "ln1_g", "ln1_b", "w_ff_gate", "w_ff_up", "w_ff_down",
           "ln2_g", "ln2_b")


def kernel(x, w_in, b_gate, w_attn_br, w_ssm_br, w_out, ssm_a_re, ssm_a_im, ssm_log_dt, ssm_b_re, ssm_b_im, ssm_c_re, ssm_c_im, ssm_d, w_glu, ln1_g, ln1_b, w_ff_gate, w_ff_up, w_ff_down, ln2_g, ln2_b, loss_target, m_w_in, m_b_gate, m_w_attn_br, m_w_ssm_br, m_w_out, m_ssm_a_re, m_ssm_a_im, m_ssm_log_dt, m_ssm_b_re, m_ssm_b_im, m_ssm_c_re, m_ssm_c_im, m_ssm_d, m_w_glu, m_ln1_g, m_ln1_b, m_w_ff_gate, m_w_ff_up, m_w_ff_down, m_ln2_g, m_ln2_b, v_w_in, v_b_gate, v_w_attn_br, v_w_ssm_br, v_w_out, v_ssm_a_re, v_ssm_a_im, v_ssm_log_dt, v_ssm_b_re, v_ssm_b_im, v_ssm_c_re, v_ssm_c_im, v_ssm_d, v_w_glu, v_ln1_g, v_ln1_b, v_w_ff_gate, v_w_ff_up, v_w_ff_down, v_ln2_g, v_ln2_b):
    given = dict(locals())
    px, py, pc = _place()
    chip = 2 * px + py
    core_s = jnp.reshape(pc, (1,)).astype(jnp.int32)
    chip_s = jnp.reshape(chip, (1,)).astype(jnp.int32)

    wts = {}
    for tag, names in GATHER_GROUPS:
        wts.update(zip(names, _gather_weights([_as_rows(n, given[n]).astype(BF16) for n in names],
                                              name="gather_" + tag)))
    ncol = D_MODEL // N_CHIPS
    bg_mine = jnp.where(pc == 0, b_gate[0], jnp.zeros_like(b_gate[0]))
    bg_full = lax.dynamic_update_slice(jnp.zeros((2, D_MODEL), F32), bg_mine, (0, chip * ncol))
    bg_slots = _exchange_rows(bg_full.reshape(16, 128), name="exchange_gate_bias")
    bg_full = _sum_slots(bg_slots, name="sum_gate_bias").reshape(2, D_MODEL)
    small = {n: given[n][0] for n in SMALL if n.startswith("ssm")}
    small.update({n: given[n] for n in ("ln1_g", "ln1_b", "ln2_g", "ln2_b")})
    small["b_gate"] = bg_full

    loss_mine, grad_x_after, big_g, small_g, marks = _local_step(x[0], loss_target[0], wts, small)

    groups = dict(REDUCE_GROUPS)
    parts, recvd = {}, {}
    grads, delta, new_m, new_v = {}, {}, {}, {}

    def start(tag, add_after):
        parts[tag], recvd[tag] = _reduce_scatter_start([big_g[n] for n in groups[tag]], core_s, tag=tag,
                                                       add_after=add_after)

    def finish(tag, sum_after, adam_after):
        mine, theirs = _reduce_scatter_finish(parts[tag], recvd[tag], chip_s, tag=tag, sum_after=sum_after)
        for n, g_mine, g_theirs in zip(groups[tag], mine, theirs):
            res = _adamw_halves(core_s, _as_rows(n, given[n]), g_mine, g_theirs, _as_rows(n, given["m_" + n]),
                                _as_rows(n, given["v_" + n]), name="adamw_" + n, after=adam_after)
            grads[n], delta[n], new_m[n], new_v[n] = [_from_rows(n, r) for r in res]

    start("ffn", (marks["ln1_bwd"],))
    start("mixer", (marks["scan_bwd"],))
    finish("mixer", (marks["attention_bwd_0"],), (big_g["w_in"],))
    start("w_in", tuple(delta[n] for n in groups["mixer"]))
    in_flight = (parts["w_in"][0],)
    grad_x = grad_x_after(in_flight)
    finish("ffn", (marks["scan_bwd"],), in_flight)
    stored = [_as_stored(n, small_g[n]) for n in SMALL] + [loss_mine.reshape(1)]
    slots = _exchange_rows(_pack_rows(stored), name="exchange_small")
    summed = _unpack_rows(_sum_slots(slots, name="sum_small", after=tuple(delta[n] for n in groups["ffn"])),
                          [a.shape for a in stored])
    loss = summed.pop()[0]
    for n, g in zip(SMALL, summed):
        g = _as_stored(n, g)
        if n == "b_gate":
            g = lax.dynamic_slice(g, (0, chip * ncol), (2, ncol))
        grads[n] = g.reshape(given[n].shape)
    packed = [_pack_rows([_as_stored(n, src[n]) for n in SMALL]) for src in
              (given, grads, {n: given["m_" + n] for n in SMALL}, {n: given["v_" + n] for n in SMALL})]
    shapes = [_as_stored(n, given[n]).shape for n in SMALL]
    small_out = _adamw(*packed, name="adamw_small")
    for out, vec in zip((delta, new_m, new_v), small_out):
        out.update((n, _as_stored(n, a)) for n, a in zip(SMALL, _unpack_rows(vec, shapes)))
    behind = [delta[n] for n in groups["ffn"]] + [small_out[0], grad_x]
    finish("w_in", tuple(behind), ())

    return (loss, grad_x.reshape(x.shape), *[grads[n] for n in WEIGHTS], *[delta[n] for n in WEIGHTS],
            *[new_m[n] for n in WEIGHTS], *[new_v[n] for n in WEIGHTS])
```

```python
import math

import jax
import jax.numpy as jnp
from jax import lax
from jax.experimental import pallas as pl
from jax.experimental.pallas import tpu as pltpu
from jax.experimental.pallas import tpu_sc as plsc

F32 = jnp.float32
BF16 = jnp.bfloat16
MESH = pl.DeviceIdType.MESH

D_MODEL = 1024
SEQ = 2048
HEAD_DIM = 64
ATTN_HEADS = 8
DILATIONS = (1, 4, 16)
ATTN_WIDTH = ATTN_HEADS * HEAD_DIM
QKV_WIDTH = 3 * ATTN_WIDTH
BLOCK = 128
ROPE_THETA = 10000.0
NEG_INF = -1e30
SSM_GROUP = 16
SSM_GROUPS = 32
SSM_WIDTH = 512
SSM_STATE = 64
SSM_LANES = SSM_GROUPS * SSM_STATE
SCAN_CHUNKS = 8
SCAN_STEPS = SEQ // SCAN_CHUNKS
IN_WIDTH = 3 * QKV_WIDTH + SSM_WIDTH + 2 * D_MODEL
D_FF = 2816
N_CHIPS = 4
N_DEV = 8
DN_ALPHA = 2.0 ** 0.25
LN_EPS = 1e-5
ADAM_LR = 0.001
ADAM_B1 = 0.9
ADAM_B2 = 0.999
ADAM_EPS = 1e-08
ADAM_WD = 0.01
ADAM_STEP = 10
GELU_C = math.sqrt(2.0 / math.pi)
GELU_K = 0.044715

VMEM_LIMIT_BYTES = 56 * 1024 * 1024


def _sds(shape, dtype=F32):
    return jax.ShapeDtypeStruct(tuple(shape), dtype)


def _cp(semantics=None):
    return pltpu.CompilerParams(dimension_semantics=semantics, vmem_limit_bytes=VMEM_LIMIT_BYTES)


HBM_OPERAND = pl.BlockSpec(memory_space=pl.ANY)


def _matmul(a, b, *, grid, a_spec, b_spec, o_spec, out_shape, dims, k_axis=None, name, after=()):
    nk = grid[k_axis] if k_axis is not None else 1
    o_block = tuple(d for d in o_spec.block_shape if d is not None)
    n_after = len(after)

    def body(a_ref, b_ref, *rest):
        o_ref, acc = rest[n_after], rest[n_after + 1:]
        part = lax.dot_general(a_ref[...].astype(BF16), b_ref[...].astype(BF16),
                               (((dims[0],), (dims[1],)), ((), ())), preferred_element_type=F32)
        if k_axis is None:
            o_ref[...] = part.astype(o_ref.dtype)
        else:
            k = pl.program_id(k_axis)

            @pl.when(k == 0)
            def _():
                acc[0][...] = part

            @pl.when(k > 0)
            def _():
                acc[0][...] += part

            @pl.when(k == nk - 1)
            def _():
                o_ref[...] = acc[0][...].astype(o_ref.dtype)

    sem = tuple("arbitrary" if ax == k_axis else "parallel" for ax in range(len(grid)))
    return pl.pallas_call(
        body, grid=grid, in_specs=[a_spec, b_spec] + [HBM_OPERAND] * n_after, out_specs=o_spec, out_shape=out_shape,
        scratch_shapes=[pltpu.VMEM(o_block, F32)] if k_axis is not None else [],
        compiler_params=_cp(sem), name=name)(a, b, *after)


def _mm_cols_nt(dy, wg, *, tm, name, out_dtype=F32, after=()):
    k, ns = wg.shape[1], wg.shape[2]
    m = dy.shape[0]
    a_spec = pl.BlockSpec((tm, ns), lambda i, s: (i, s))
    return _matmul(dy, wg, grid=(m // tm, N_CHIPS), a_spec=a_spec,
                   b_spec=pl.BlockSpec((None, k, ns), lambda i, s: (s, 0, 0)),
                   o_spec=pl.BlockSpec((tm, k), lambda i, s: (i, 0)),
                   out_shape=_sds((m, k), out_dtype), dims=(1, 1), k_axis=1, name=name, after=after)


def _mm_cols_tn(a, dy, *, ns, name, after=()):
    m, k = a.shape
    return _matmul(a, dy, grid=(N_CHIPS,), a_spec=pl.BlockSpec((m, k), lambda s: (0, 0)),
                   b_spec=pl.BlockSpec((m, ns), lambda s: (0, s)),
                   o_spec=pl.BlockSpec((None, k, ns), lambda s: (s, 0, 0)),
                   out_shape=_sds((N_CHIPS, k, ns), BF16), dims=(0, 0), name=name, after=after)


def _mm_rows_tn(a, dy, *, name):
    m, k = a.shape
    rows, n = k // N_CHIPS, dy.shape[1]
    return _matmul(a, dy, grid=(N_CHIPS,), a_spec=pl.BlockSpec((m, rows), lambda s: (0, s)),
                   b_spec=pl.BlockSpec((m, n), lambda s: (0, 0)),
                   o_spec=pl.BlockSpec((None, rows, n), lambda s: (s, 0, 0)),
                   out_shape=_sds((N_CHIPS, rows, n), BF16), dims=(0, 0), name=name)


def _rowwise(fn, tiled, full, outs, accs=(), *, tm, name, after=()):
    args, in_specs = [], []
    for t in tiled:
        if isinstance(t, tuple):
            arr, w, cb = t
            in_specs.append(pl.BlockSpec((tm, w), lambda i, cb=cb: (i, cb)))
        else:
            arr = t
            in_specs.append(pl.BlockSpec((tm, arr.shape[1]), lambda i: (i, 0)))
        args.append(arr)
    rows = args[0].shape[0]
    for f in full:
        in_specs.append(pl.BlockSpec(f.shape, lambda i, nd=f.ndim: (0,) * nd))
        args.append(f)
    out_specs = [pl.BlockSpec((tm, o.shape[1]), lambda i: (i, 0)) for o in outs]
    out_specs += [pl.BlockSpec(a.shape, lambda i, nd=len(a.shape): (0,) * nd) for a in accs]
    n_in, n_out = len(args), len(outs)
    in_specs += [HBM_OPERAND] * len(after)
    first_out = n_in + len(after)

    def body(*refs):
        res = fn(*[r[...] for r in refs[:n_in]])
        res = res if isinstance(res, (tuple, list)) else (res,)
        for r, v in zip(refs[first_out:first_out + n_out], res[:n_out]):
            r[...] = v.astype(r.dtype)
        i = pl.program_id(0)
        for r, v in zip(refs[first_out + n_out:], res[n_out:]):
            @pl.when(i == 0)
            def _(r=r, v=v):
                r[...] = v

            @pl.when(i > 0)
            def _(r=r, v=v):
                r[...] += v

    res = pl.pallas_call(
        body, grid=(rows // tm,), in_specs=in_specs, out_specs=out_specs, out_shape=list(outs) + list(accs),
        compiler_params=_cp(("arbitrary",) if accs else ("parallel",)), name=name)(*args, *after)
    return res


def _colsum(v):
    return jnp.sum(v, axis=0, keepdims=True)


def _ln_stats(z):
    mu = jnp.mean(z, axis=-1, keepdims=True)
    zc = z - mu
    var = jnp.mean(zc * zc, axis=-1, keepdims=True)
    rstd = lax.rsqrt(var + LN_EPS)
    return zc * rstd, rstd


def _ln_bwd(dy, xhat, rstd, g):
    dxh = dy * g
    m1 = jnp.mean(dxh, axis=-1, keepdims=True)
    m2 = jnp.mean(dxh * xhat, axis=-1, keepdims=True)
    return rstd * (dxh - m1 - xhat * m2)


def _swap_halves(t):
    w = t.shape[-1]
    lane = lax.broadcasted_iota(jnp.int32, t.shape, t.ndim - 1)
    return jnp.where((lane % HEAD_DIM) < HEAD_DIM // 2, pltpu.roll(t, w - HEAD_DIM // 2, t.ndim - 1),
                     pltpu.roll(t, HEAD_DIM // 2, t.ndim - 1))


PHASES = max(DILATIONS)
PAIR = 2 * HEAD_DIM
UNITS = SEQ // BLOCK
UNIT_BATCH = 16
ROPE_ROWS = 256


def _to_phase_rows(t):
    return t.reshape(SEQ // PHASES, PHASES, t.shape[1]).transpose(1, 0, 2).reshape(t.shape)


def _reorder_rows(arr, plus=None, *, to_phase, name, scale=1.0):
    def body(*refs):
        o_ref = refs[-1]
        for rho in range(PHASES):
            phase = pl.ds(rho * BLOCK, BLOCK)
            strided = pl.ds(rho, BLOCK, stride=PHASES)
            src, dst = (strided, phase) if to_phase else (phase, strided)
            val = refs[0][src, :]
            if scale != 1.0:
                val = val * scale
            if plus is not None:
                val = val + refs[1][src, :]
            o_ref[dst, :] = val

    spec = pl.BlockSpec((SEQ, BLOCK), lambda j: (0, j))
    ins = [arr] if plus is None else [arr, plus]
    return pl.pallas_call(body, grid=(arr.shape[1] // BLOCK,), in_specs=[spec] * len(ins), out_specs=spec,
                          out_shape=_sds(arr.shape), compiler_params=_cp(("parallel",)), name=name)(*ins)


def _rope(t, cf, ss):
    return t * cf + _swap_halves(t) * ss


def _rope_transposed(d, cf, ss):
    return d * cf + _swap_halves(d * ss)


def _unit_pieces(u, dil):
    pieces, length = PHASES // dil, 8 * dil
    if dil == 1:
        rho, i = 0, u
    elif dil == PHASES:
        rho, i = u, 0
    else:
        rho, i = jnp.bitwise_and(u, dil - 1), jnp.right_shift(u, dil.bit_length() - 1)
    before = jnp.maximum(i - 1, 0)
    cur = [pl.multiple_of((rho + dil * k) * BLOCK + length * i, 8) for k in range(pieces)]
    prev = [pl.multiple_of((rho + dil * k) * BLOCK + length * before, 8) for k in range(pieces)]
    return i, cur, prev


def _load_tile(ref, starts, dil):
    return jnp.concatenate([ref[pl.ds(st, 8 * dil), :] for st in starts], axis=0)


def _store_tile(ref, starts, dil, val, head=None, accumulate=False):
    length = 8 * dil
    lanes = slice(None) if head is None else pl.ds(head * HEAD_DIM, HEAD_DIM)
    cols = slice(None) if head is None else slice(head * HEAD_DIM, (head + 1) * HEAD_DIM)
    for k, st in enumerate(starts):
        piece = val[k * length:(k + 1) * length, cols]
        if accumulate:
            ref[pl.ds(st, length), lanes] += piece
        else:
            ref[pl.ds(st, length), lanes] = piece


def _tile_position(idx, dil):
    pieces, length = PHASES // dil, 8 * dil
    return pieces * jnp.bitwise_and(idx, length - 1) + jnp.right_shift(idx, length.bit_length() - 1)


def _band_mask(i, dil):
    row = lax.broadcasted_iota(jnp.int32, (BLOCK, 2 * BLOCK), 0)
    col = lax.broadcasted_iota(jnp.int32, (BLOCK, 2 * BLOCK), 1)
    key_pos = _tile_position(jnp.bitwise_and(col, BLOCK - 1), dil) + jnp.where(col >= BLOCK, 0, -BLOCK)
    dist = _tile_position(row, dil) - key_pos
    return (dist >= 0) & (dist <= BLOCK) & ((col >= BLOCK) | (i > 0))


def _causal_mask():
    row = lax.broadcasted_iota(jnp.int32, (BLOCK, BLOCK), 0)
    col = lax.broadcasted_iota(jnp.int32, (BLOCK, BLOCK), 1)
    return row >= col


def _pair_views(col0):
    return [pl.BlockSpec((SEQ, PAIR), lambda hp, g=g: (0, col0 // PAIR + g * (ATTN_WIDTH // PAIR) + hp))
            for g in range(len(DILATIONS))]


def _project_in(x, wg, cos_f, sin_s):
    ns = wg.shape[2]
    tiles = ns // PAIR

    def body(x_ref, w_ref, cf_ref, ss_ref, o_ref):
        shard = pl.program_id(1)
        xb = x_ref[...].astype(BF16)
        cf, ss = cf_ref[...], ss_ref[...]

        def write(rotated, scaled):
            for t0 in range(0, tiles, 2):
                strip = jnp.dot(xb, w_ref[:, t0 * PAIR:(t0 + 2) * PAIR], preferred_element_type=F32)
                for t in (t0, t0 + 1):
                    val = strip[:, (t - t0) * PAIR:(t - t0 + 1) * PAIR]
                    if t < rotated:
                        val = _rope(val, cf, ss)
                        if t < scaled:
                            val = val * (1.0 / math.sqrt(HEAD_DIM))
                    o_ref[:, t * PAIR:(t + 1) * PAIR] = val

        for s in range(N_CHIPS):
            rotated = min(max(2 * QKV_WIDTH - s * ns, 0), ns) // PAIR
            scaled = min(max(QKV_WIDTH - s * ns, 0), ns) // PAIR

            @pl.when(shard == s)
            def _(rotated=rotated, scaled=scaled):
                write(rotated, scaled)

    table = pl.BlockSpec((FF_ROWS, PAIR), lambda i, s: (i, 0))
    return pl.pallas_call(
        body, grid=(SEQ // FF_ROWS, N_CHIPS),
        in_specs=[pl.BlockSpec((FF_ROWS, D_MODEL), lambda i, s: (i, 0)),
                  pl.BlockSpec((None, D_MODEL, ns), lambda i, s: (s, 0, 0)), table, table],
        out_specs=pl.BlockSpec((FF_ROWS, ns), lambda i, s: (i, s)), out_shape=_sds((SEQ, N_CHIPS * ns)),
        compiler_params=_cp(("parallel", "parallel")), name="project_in")(x, wg, cos_f, sin_s)


def _attention_fwd(proj):
    ng = len(DILATIONS)

    def body(*refs):
        q_refs, k_refs, v_refs = refs[:ng], refs[ng:2 * ng], refs[2 * ng:3 * ng]
        attn_ref, lse_ref = refs[3 * ng:]
        qr_refs, kr_refs = q_refs, k_refs
        first = lax.broadcasted_iota(jnp.int32, (BLOCK, PAIR), 1) < HEAD_DIM
        for g, dil in enumerate(DILATIONS):
            two_blocks = SEQ // dil > BLOCK

            def units(t, carry, g=g, dil=dil, two_blocks=two_blocks):
                picked = [_unit_pieces(t * UNIT_BATCH + j, dil) for j in range(UNIT_BATCH)]

                def tiles(ref, with_prev=False):
                    if with_prev and two_blocks:
                        return jnp.stack([jnp.concatenate([_load_tile(ref, prev, dil), _load_tile(ref, rows, dil)],
                                                          axis=0) for _, rows, prev in picked])
                    return jnp.stack([_load_tile(ref, rows, dil) for _, rows, _ in picked])

                qq = tiles(qr_refs[g]).astype(BF16)
                kk = tiles(kr_refs[g], True).astype(BF16)
                vv = tiles(v_refs[g], True).astype(BF16)
                if two_blocks:
                    valid = jnp.stack([_band_mask(i, dil) for i, _, _ in picked])
                else:
                    valid = _causal_mask()[None]
                mine = first[None]
                zero = jnp.zeros_like(qq)
                outs, lses = [], []
                for qh in (jnp.where(mine, qq, zero), jnp.where(mine, zero, qq)):
                    s = jnp.einsum("pqd,pkd->pqk", qh, kk, preferred_element_type=F32)
                    s = jnp.where(valid, s, NEG_INF)
                    m = jnp.max(s, axis=-1, keepdims=True)
                    p = jnp.exp(s - m)
                    l = jnp.sum(p, axis=-1, keepdims=True)
                    outs.append(jnp.einsum("pqk,pkd->pqd", p.astype(BF16), vv, preferred_element_type=F32) * (1.0 / l))
                    lses.append(m + jnp.log(l))
                o = jnp.where(mine, outs[0], outs[1])
                lse = jnp.where(mine, lses[0], lses[1])
                if g > 0:
                    lse_old = tiles(lse_ref)
                    m = jnp.maximum(lse_old, lse)
                    lse_new = m + jnp.log(jnp.exp(lse_old - m) + jnp.exp(lse - m))
                    o = tiles(attn_ref) * jnp.exp(lse_old - lse_new) + o * jnp.exp(lse - lse_new)
                    lse = lse_new
                for j, (_, rows, _) in enumerate(picked):
                    _store_tile(attn_ref, rows, dil, o[j])
                    _store_tile(lse_ref, rows, dil, lse[j])
                return carry

            lax.fori_loop(0, UNITS // UNIT_BATCH, units, 0)

    out = pl.BlockSpec((SEQ, PAIR), lambda hp: (0, hp))
    return pl.pallas_call(
        body, grid=(ATTN_WIDTH // PAIR,),
        in_specs=_pair_views(0) + _pair_views(QKV_WIDTH) + _pair_views(2 * QKV_WIDTH),
        out_specs=[out, out], out_shape=[_sds((SEQ, ATTN_WIDTH)), _sds((SEQ, ATTN_WIDTH))],
        compiler_params=_cp(("parallel",)), name="attention_fwd")(*([proj] * (3 * ng)))


def _attention_bwd(proj, cos_f, sin_s, d_attn, attn, lse):
    def group(dil, qr_ref, kr_ref, v_ref, cf_ref, ss_ref, do_ref, o_ref, lse_ref, dq_out, dk_out, dv_out,
              dq_acc, dk_acc, dv_acc):
        two_blocks = SEQ // dil > BLOCK
        dk_acc[...] = jnp.zeros_like(dk_acc)
        dv_acc[...] = jnp.zeros_like(dv_acc)
        nk = 2 * BLOCK if two_blocks else BLOCK
        first = lax.broadcasted_iota(jnp.int32, (BLOCK, PAIR), 1) < HEAD_DIM
        first_k = lax.broadcasted_iota(jnp.int32, (nk, PAIR), 1) < HEAD_DIM

        def units(t, carry):
            picked = [_unit_pieces(t * UNIT_BATCH + j, dil) for j in range(UNIT_BATCH)]

            def tiles(ref, with_prev=False):
                if with_prev and two_blocks:
                    return jnp.stack([jnp.concatenate([_load_tile(ref, prev, dil), _load_tile(ref, rows, dil)], axis=0)
                                      for _, rows, prev in picked])
                return jnp.stack([_load_tile(ref, rows, dil) for _, rows, _ in picked])

            qq = tiles(qr_ref).astype(BF16)
            kk = tiles(kr_ref, True).astype(BF16)
            vv = tiles(v_ref, True).astype(BF16)
            dof = tiles(do_ref)
            dd = dof * tiles(o_ref)
            lse3 = tiles(lse_ref)
            dob = dof.astype(BF16)
            if two_blocks:
                valid = jnp.stack([_band_mask(i, dil) for i, _, _ in picked])
            else:
                valid = _causal_mask()[None]
            zq, zf = jnp.zeros_like(qq), jnp.zeros_like(dd)
            dqs, dks, dvs = [], [], []
            for head in range(2):
                mine = first[None] if head == 0 else jnp.logical_not(first)[None]
                delta = jnp.sum(jnp.where(mine, dd, zf), axis=-1, keepdims=True)
                lse_h = lse3[:, :, head * HEAD_DIM:head * HEAD_DIM + 1]
                s = jnp.einsum("pqd,pkd->pqk", jnp.where(mine, qq, zq), kk, preferred_element_type=F32)
                p = jnp.where(valid, jnp.exp(s - lse_h), 0.0)
                dp = jnp.einsum("pqd,pkd->pqk", jnp.where(mine, dob, zq), vv, preferred_element_type=F32)
                ds = (p * (dp - delta)).astype(BF16)
                dqs.append(jnp.einsum("pqk,pkd->pqd", ds, kk, preferred_element_type=F32))
                dks.append(jnp.einsum("pqk,pqd->pkd", ds, qq, preferred_element_type=F32))
                dvs.append(jnp.einsum("pqk,pqd->pkd", p.astype(BF16), dob, preferred_element_type=F32))
            dq = jnp.where(first[None], dqs[0], dqs[1])
            dk = jnp.where(first_k[None], dks[0], dks[1])
            dv = jnp.where(first_k[None], dvs[0], dvs[1])
            for j, (_, rows, prev) in enumerate(picked):
                _store_tile(dq_acc, rows, dil, dq[j])
                _store_tile(dk_acc, rows, dil, dk[j, nk - BLOCK:], accumulate=True)
                _store_tile(dv_acc, rows, dil, dv[j, nk - BLOCK:], accumulate=True)
                if two_blocks:
                    _store_tile(dk_acc, prev, dil, dk[j, :BLOCK], accumulate=True)
                    _store_tile(dv_acc, prev, dil, dv[j, :BLOCK], accumulate=True)
            return carry

        lax.fori_loop(0, UNITS // UNIT_BATCH, units, 0)

        def finish(t, carry):
            rows = pl.ds(pl.multiple_of(t * ROPE_ROWS, ROPE_ROWS), ROPE_ROWS)
            cf, ss = cf_ref[rows, :], ss_ref[rows, :]
            dq = dq_acc[rows, :] * (1.0 / math.sqrt(HEAD_DIM))
            dq_out[rows, :] = _rope_transposed(dq, cf, ss).astype(BF16)
            dk_out[rows, :] = _rope_transposed(dk_acc[rows, :], cf, ss).astype(BF16)
            dv_out[rows, :] = dv_acc[rows, :].astype(BF16)
            return carry

        lax.fori_loop(0, SEQ // ROPE_ROWS, finish, 0)

    def body(*refs):
        for g, dil in enumerate(DILATIONS):
            @pl.when(pl.program_id(0) == g)
            def _(dil=dil):
                group(dil, *refs)

    pairs = ATTN_WIDTH // PAIR
    whole = pl.BlockSpec((SEQ, PAIR), lambda g, hp: (0, 0))
    pair = pl.BlockSpec((SEQ, PAIR), lambda g, hp: (0, hp))
    grouped = pl.BlockSpec((SEQ, PAIR), lambda g, hp: (0, g * pairs + hp))
    views = [pl.BlockSpec((SEQ, PAIR), lambda g, hp, c0=col0 // PAIR: (0, c0 + g * pairs + hp))
             for col0 in (0, QKV_WIDTH, 2 * QKV_WIDTH)]
    return pl.pallas_call(
        body, grid=(len(DILATIONS), pairs), in_specs=views + [whole, whole, pair, pair, pair],
        out_specs=[grouped, grouped, grouped], out_shape=[_sds((SEQ, QKV_WIDTH), BF16)] * 3,
        scratch_shapes=[pltpu.VMEM((SEQ, PAIR), F32)] * 3,
        compiler_params=_cp(("parallel", "parallel")), name="attention_bwd")(
            proj, proj, proj, cos_f, sin_s, d_attn, attn, lse)


def _cmul(ar, ai, br, bi):
    return ar * br - ai * bi, ar * bi + ai * br


def _pow256(ar, ai):
    for _ in range(8):
        ar, ai = _cmul(ar, ai, ar, ai)
    return ar, ai


def _chunk_carries(first_r, first_i, pr, pi, reverse):
    rows = lax.broadcasted_iota(jnp.int32, first_r.shape, 0)
    out_r = jnp.zeros_like(first_r)
    out_i = jnp.zeros_like(first_i)
    hr = jnp.zeros_like(first_r[0:1])
    hi = jnp.zeros_like(hr)
    order = range(SCAN_CHUNKS - 1, -1, -1) if reverse else range(SCAN_CHUNKS)
    for c in order:
        out_r = jnp.where(rows == c, hr, out_r)
        out_i = jnp.where(rows == c, hi, out_i)
        tr, ti = _cmul(pr[0:1], pi[0:1], hr, hi)
        hr = first_r[c:c + 1] + tr
        hi = first_i[c:c + 1] + ti
    return out_r, out_i


def _tile(j):
    return pl.ds(pl.multiple_of(j * SCAN_CHUNKS, SCAN_CHUNKS), SCAN_CHUNKS)


def _to_scan_rows(t):
    per = SCAN_STEPS // PHASES
    return t.reshape(PHASES, SCAN_CHUNKS, per, t.shape[1]).transpose(2, 0, 1, 3).reshape(t.shape)


def _from_scan_rows(t):
    per = SCAN_STEPS // PHASES
    return t.reshape(per, PHASES, SCAN_CHUNKS, t.shape[1]).transpose(1, 2, 0, 3).reshape(t.shape)


def _scan_in_place(hr_ref, hi_ref, a_r, a_i):
    def local(j, carry):
        tr, ti = _cmul(a_r, a_i, carry[0], carry[1])
        nr = tr + hr_ref[_tile(j), :]
        ni = ti + hi_ref[_tile(j), :]
        hr_ref[_tile(j), :] = nr
        hi_ref[_tile(j), :] = ni
        return nr, ni

    zero = jnp.zeros_like(a_r)
    last_r, last_i = lax.fori_loop(0, SCAN_STEPS, local, (zero, zero), unroll=4)
    pr, pi = _pow256(a_r, a_i)
    er, ei = _chunk_carries(last_r, last_i, pr, pi, reverse=False)

    def fix(j, carry):
        tr, ti = _cmul(carry[0], carry[1], er, ei)
        hr_ref[_tile(j), :] += tr
        hi_ref[_tile(j), :] += ti
        return _cmul(carry[0], carry[1], a_r, a_i)

    lax.fori_loop(0, SCAN_STEPS, fix, (a_r, a_i), unroll=4)
    return er, ei


def _reverse_scan_in_place(lr_ref, li_ref, hr_ref, hi_ref, er, ei, a_r, a_i):
    def local(t, carry):
        j = SCAN_STEPS - 1 - t
        tr, ti = _cmul(a_r, a_i, carry[0], carry[1])
        nr = tr + lr_ref[_tile(j), :]
        ni = ti + li_ref[_tile(j), :]
        lr_ref[_tile(j), :] = nr
        li_ref[_tile(j), :] = ni
        return nr, ni

    zero = jnp.zeros_like(a_r)
    first_r, first_i = lax.fori_loop(0, SCAN_STEPS, local, (zero, zero), unroll=4)
    pr, pi = _pow256(a_r, a_i)
    nxt_r, nxt_i = _chunk_carries(first_r, first_i, pr, pi, reverse=True)

    def accumulate(lam_r, lam_i, hp_r, hp_i, acc):
        return (acc[0] + lam_r * hp_r + lam_i * hp_i, acc[1] + lam_i * hp_r - lam_r * hp_i)

    def fix(t, carry):
        qr, qi, acc_r, acc_i = carry
        j = SCAN_STEPS - 1 - t
        tr, ti = _cmul(qr, qi, nxt_r, nxt_i)
        lam_r = lr_ref[_tile(j), :] + tr
        lam_i = li_ref[_tile(j), :] + ti
        lr_ref[_tile(j), :] = lam_r
        li_ref[_tile(j), :] = lam_i
        acc_r, acc_i = accumulate(lam_r, lam_i, hr_ref[_tile(j - 1), :], hi_ref[_tile(j - 1), :], (acc_r, acc_i))
        qr, qi = _cmul(qr, qi, a_r, a_i)
        return qr, qi, acc_r, acc_i

    qr, qi, acc_r, acc_i = lax.fori_loop(0, SCAN_STEPS - 1, fix, (a_r, a_i, zero, zero), unroll=4)
    tr, ti = _cmul(qr, qi, nxt_r, nxt_i)
    lam_r = lr_ref[_tile(0), :] + tr
    lam_i = li_ref[_tile(0), :] + ti
    lr_ref[_tile(0), :] = lam_r
    li_ref[_tile(0), :] = lam_i
    acc_r, acc_i = accumulate(lam_r, lam_i, er, ei, (acc_r, acc_i))
    return jnp.sum(acc_r, axis=0, keepdims=True), jnp.sum(acc_i, axis=0, keepdims=True)


def _rope_tables():
    half = HEAD_DIM // 2
    inv_freq = ROPE_THETA ** (-jnp.arange(half, dtype=F32) / half)
    ang = jnp.arange(SEQ, dtype=F32)[:, None] * inv_freq[None, :]
    cos, sin = jnp.cos(ang), jnp.sin(ang)
    cos_f = jnp.concatenate([cos, cos, cos, cos], axis=1)
    sin_s = jnp.concatenate([-sin, sin, -sin, sin], axis=1)
    return cos_f, sin_s


def _ssm_discretise(a_re, a_im, log_dt, b_re, b_im):
    lam = lax.complex(a_re, a_im)
    dt = jnp.exp(log_dt)[:, None]
    a_bar = jnp.exp(lam * dt)
    b_bar = ((a_bar - 1.0) / lam)[..., None] * lax.complex(b_re, b_im)
    return a_bar.real, a_bar.imag, b_bar.real, b_bar.imag


SSM_SLABS = 4
SLAB_GROUPS = SSM_GROUPS // SSM_SLABS
SLAB_IN = SSM_WIDTH // SSM_SLABS
SLAB_STATE = SSM_LANES // SSM_SLABS


def _slab_block_diag(blocks):
    _, r, c = blocks.shape
    eye = jnp.eye(SLAB_GROUPS, dtype=blocks.dtype)
    b5 = blocks.reshape(SSM_SLABS, SLAB_GROUPS, r, 1, c) * eye[None, :, None, :, None]
    return b5.reshape(SSM_SLABS, SLAB_GROUPS * r, SLAB_GROUPS * c)


def _diag_blocks(a, b):
    ra, cb = a.shape[1], b.shape[1]
    wa, wb = ra // SLAB_GROUPS, cb // SLAB_GROUPS
    d = lax.dot_general(a, b, (((0,), (0,)), ((), ())), preferred_element_type=F32)
    row_g = jnp.right_shift(lax.broadcasted_iota(jnp.int32, (ra, cb), 0), wa.bit_length() - 1)
    col_g = jnp.right_shift(lax.broadcasted_iota(jnp.int32, (ra, cb), 1), wb.bit_length() - 1)
    d = jnp.where(row_g == col_g, d, 0.0)
    fold = (jnp.bitwise_and(lax.broadcasted_iota(jnp.int32, (cb, wb), 0), wb - 1)
            == lax.broadcasted_iota(jnp.int32, (cb, wb), 1)).astype(F32)
    return jnp.dot(d, fold, preferred_element_type=F32, precision=lax.Precision.HIGHEST)


def _slab_specs():
    tok = pl.BlockSpec((SEQ, SLAB_IN), lambda j: (0, j))
    state = pl.BlockSpec((SEQ, SLAB_STATE), lambda j: (0, j))
    b_in = pl.BlockSpec((None, SLAB_IN, SLAB_STATE), lambda j: (j, 0, 0))
    c_out = pl.BlockSpec((None, SLAB_STATE, SLAB_IN), lambda j: (j, 0, 0))
    vec = pl.BlockSpec((1, SLAB_STATE), lambda j: (0, j))
    ent = pl.BlockSpec((SCAN_CHUNKS, SLAB_STATE), lambda j: (0, j))
    return tok, state, b_in, c_out, vec, ent


def _ssm_forward(u, b_in_r, b_in_i, c_out_r, c_out_ni, a_r, a_i):
    def body(u_ref, br_ref, bi_ref, cr_ref, ci_ref, ar_ref, ai_ref, y_ref, hr_ref, hi_ref, er_ref, ei_ref):
        uu = u_ref[...]
        hr_ref[...] = jnp.dot(uu, br_ref[...], preferred_element_type=F32)
        hi_ref[...] = jnp.dot(uu, bi_ref[...], preferred_element_type=F32)
        a_re = jnp.broadcast_to(ar_ref[...], (SCAN_CHUNKS, SLAB_STATE))
        a_im = jnp.broadcast_to(ai_ref[...], (SCAN_CHUNKS, SLAB_STATE))
        er_ref[...], ei_ref[...] = _scan_in_place(hr_ref, hi_ref, a_re, a_im)
        y_ref[...] = (jnp.dot(hr_ref[...].astype(BF16), cr_ref[...], preferred_element_type=F32)
                      + jnp.dot(hi_ref[...].astype(BF16), ci_ref[...], preferred_element_type=F32))

    tok, state, b_in, c_out, vec, ent = _slab_specs()
    return pl.pallas_call(
        body, grid=(SSM_SLABS,), in_specs=[tok, b_in, b_in, c_out, c_out, vec, vec],
        out_specs=[tok, state, state, ent, ent],
        out_shape=[_sds((SEQ, SSM_WIDTH)), _sds((SEQ, SSM_LANES)), _sds((SEQ, SSM_LANES)),
                   _sds((SCAN_CHUNKS, SSM_LANES)), _sds((SCAN_CHUNKS, SSM_LANES))],
        compiler_params=_cp(("parallel",)), name="ssm_forward")(u, b_in_r, b_in_i, c_out_r, c_out_ni, a_r, a_i)


def _ssm_backward(d_y, d_u_skip, u, h_r, h_i, e_r, e_i, b_in_r, b_in_i, c_out_r, c_out_ni, a_r, a_i):
    def body(dy_ref, skip_ref, u_ref, hr_ref, hi_ref, er_ref, ei_ref, br_ref, bi_ref, cr_ref, ci_ref, ar_ref, ai_ref,
             du_ref, dar_ref, dai_ref, dcr_ref, dci_ref, dbr_ref, dbi_ref, lr_ref, li_ref):
        dy = dy_ref[...]
        lr_ref[...] = _dot_nt(dy, cr_ref[...])
        li_ref[...] = _dot_nt(dy, ci_ref[...])
        a_re = jnp.broadcast_to(ar_ref[...], (SCAN_CHUNKS, SLAB_STATE))
        a_im = -jnp.broadcast_to(ai_ref[...], (SCAN_CHUNKS, SLAB_STATE))
        dar_ref[...], dai_ref[...] = _reverse_scan_in_place(lr_ref, li_ref, hr_ref, hi_ref, er_ref[...], ei_ref[...],
                                                            a_re, a_im)
        dcr_ref[...] = _diag_blocks(dy, hr_ref[...].astype(BF16))
        dci_ref[...] = _diag_blocks(dy, hi_ref[...].astype(BF16))
        lam_r, lam_i = lr_ref[...].astype(BF16), li_ref[...].astype(BF16)
        uu = u_ref[...]
        dbr_ref[...] = _diag_blocks(uu, lam_r)
        dbi_ref[...] = _diag_blocks(uu, lam_i)
        du = skip_ref[...] + _dot_nt(lam_r, br_ref[...]) + _dot_nt(lam_i, bi_ref[...])
        du_ref[...] = du.astype(BF16)

    tok, state, b_in, c_out, vec, ent = _slab_specs()
    db = pl.BlockSpec((SLAB_IN, SSM_STATE), lambda j: (j, 0))
    return pl.pallas_call(
        body, grid=(SSM_SLABS,), in_specs=[tok, tok, tok, state, state, ent, ent, b_in, b_in, c_out, c_out, vec, vec],
        out_specs=[tok, vec, vec, db, db, db, db],
        out_shape=[_sds((SEQ, SSM_WIDTH), BF16), _sds((1, SSM_LANES)), _sds((1, SSM_LANES))]
        + [_sds((SSM_WIDTH, SSM_STATE))] * 4,
        scratch_shapes=[pltpu.VMEM((SEQ, SLAB_STATE), F32)] * 2,
        compiler_params=_cp(("parallel",)), name="ssm_backward")(
            d_y, d_u_skip, u, h_r, h_i, e_r, e_i, b_in_r, b_in_i, c_out_r, c_out_ni, a_r, a_i)


FF_ROWS = 1024
FF_SHARD = D_FF // N_CHIPS


def _dot_nt(a, b):
    return lax.dot_general(a, b, (((1,), (1,)), ((), ())), preferred_element_type=F32)


def _ffn_up(h, w_gate_t, w_up_t):
    def body(h_ref, wg_ref, wu_ref, a_ref, b_ref, act_ref):
        hb = h_ref[...].astype(BF16)
        a = _dot_nt(hb, wg_ref[...])
        b = _dot_nt(hb, wu_ref[...])
        a_ref[...] = a
        b_ref[...] = b
        act_ref[...] = (a * jax.nn.sigmoid(a) * b).astype(BF16)

    w_spec = pl.BlockSpec((None, FF_SHARD, D_MODEL), lambda i, k: (k, 0, 0))
    o_spec = pl.BlockSpec((None, FF_ROWS, FF_SHARD), lambda i, k: (k, i, 0))
    shape = (N_CHIPS, SEQ, FF_SHARD)
    return pl.pallas_call(
        body, grid=(SEQ // FF_ROWS, N_CHIPS),
        in_specs=[pl.BlockSpec((FF_ROWS, D_MODEL), lambda i, k: (i, 0)), w_spec, w_spec],
        out_specs=[o_spec, o_spec, o_spec], out_shape=[_sds(shape), _sds(shape), _sds(shape, BF16)],
        compiler_params=_cp(("parallel", "parallel")), name="ffn_up")(h, w_gate_t, w_up_t)


def _ffn_down_ln2_loss(act, w_down, h, tgt, ln_g, ln_b):
    def body(act_ref, w_ref, h_ref, tgt_ref, g_ref, b_ref, dz_ref, loss_ref, dg_ref, db_ref, acc):
        i, k = pl.program_id(0), pl.program_id(1)
        part = jnp.dot(act_ref[...], w_ref[...], preferred_element_type=F32)

        @pl.when(k == 0)
        def _():
            acc[...] = part

        @pl.when(k > 0)
        def _():
            acc[...] += part

        @pl.when(k == N_CHIPS - 1)
        def _():
            g = g_ref[...]
            xhat, rstd = _ln_stats(DN_ALPHA * h_ref[...] + acc[...])
            err = xhat * g + b_ref[...] - tgt_ref[...]
            d_out = err * (1.0 / D_MODEL)
            dz_ref[...] = _ln_bwd(d_out, xhat, rstd, g)
            loss_rows = jnp.sum(err * err, axis=-1, keepdims=True) * (0.5 / D_MODEL)
            sums = (jnp.broadcast_to(jnp.sum(loss_rows, axis=0, keepdims=True), loss_ref.shape),
                    _colsum(d_out * xhat), _colsum(d_out))
            for ref, val in zip((loss_ref, dg_ref, db_ref), sums):
                @pl.when(i == 0)
                def _(ref=ref, val=val):
                    ref[...] = val

                @pl.when(i > 0)
                def _(ref=ref, val=val):
                    ref[...] += val

    row = pl.BlockSpec((FF_ROWS, D_MODEL), lambda i, k: (i, 0))
    vec = pl.BlockSpec((1, D_MODEL), lambda i, k: (0, 0))
    return pl.pallas_call(
        body, grid=(SEQ // FF_ROWS, N_CHIPS),
        in_specs=[pl.BlockSpec((None, FF_ROWS, FF_SHARD), lambda i, k: (k, i, 0)),
                  pl.BlockSpec((None, FF_SHARD, D_MODEL), lambda i, k: (k, 0, 0)), row, row, vec, vec],
        out_specs=[row, pl.BlockSpec((1, BLOCK), lambda i, k: (0, 0)), vec, vec],
        out_shape=[_sds((SEQ, D_MODEL)), _sds((1, BLOCK)), _sds((1, D_MODEL)), _sds((1, D_MODEL))],
        scratch_shapes=[pltpu.VMEM((FF_ROWS, D_MODEL), F32)],
        compiler_params=_cp(("arbitrary", "arbitrary")), name="ffn_down_ln2_loss")(act, w_down, h, tgt, ln_g, ln_b)


def _ffn_down_bwd(dz, w_down, a, b):
    def body(dz_ref, wd_ref, a_ref, b_ref, da_ref, db_ref):
        d_act = _dot_nt(dz_ref[...].astype(BF16), wd_ref[...])
        av = a_ref[...]
        sg = jax.nn.sigmoid(av)
        da_ref[...] = (d_act * b_ref[...] * sg * (1.0 + av * (1.0 - sg))).astype(BF16)
        db_ref[...] = (d_act * av * sg).astype(BF16)

    t_spec = pl.BlockSpec((None, FF_ROWS, FF_SHARD), lambda i, k: (k, i, 0))
    shape = (N_CHIPS, SEQ, FF_SHARD)
    return pl.pallas_call(
        body, grid=(SEQ // FF_ROWS, N_CHIPS),
        in_specs=[pl.BlockSpec((FF_ROWS, D_MODEL), lambda i, k: (i, 0)),
                  pl.BlockSpec((None, FF_SHARD, D_MODEL), lambda i, k: (k, 0, 0)), t_spec, t_spec],
        out_specs=[t_spec, t_spec], out_shape=[_sds(shape, BF16), _sds(shape, BF16)],
        compiler_params=_cp(("parallel", "parallel")), name="ffn_down_bwd")(dz, w_down, a, b)


def _ffn_dh(d_a, d_b, w_gate_t, w_up_t):
    def body(da_ref, db_ref, wg_ref, wu_ref, o_ref, acc):
        k = pl.program_id(1)
        part = (jnp.dot(da_ref[...], wg_ref[...], preferred_element_type=F32)
                + jnp.dot(db_ref[...], wu_ref[...], preferred_element_type=F32))

        @pl.when(k == 0)
        def _():
            acc[...] = part

        @pl.when(k > 0)
        def _():
            acc[...] += part

        @pl.when(k == N_CHIPS - 1)
        def _():
            o_ref[...] = acc[...]

    t_spec = pl.BlockSpec((None, FF_ROWS, FF_SHARD), lambda i, k: (k, i, 0))
    w_spec = pl.BlockSpec((None, FF_SHARD, D_MODEL), lambda i, k: (k, 0, 0))
    return pl.pallas_call(
        body, grid=(SEQ // FF_ROWS, N_CHIPS), in_specs=[t_spec, t_spec, w_spec, w_spec],
        out_specs=pl.BlockSpec((FF_ROWS, D_MODEL), lambda i, k: (i, 0)), out_shape=_sds((SEQ, D_MODEL)),
        scratch_shapes=[pltpu.VMEM((FF_ROWS, D_MODEL), F32)],
        compiler_params=_cp(("parallel", "arbitrary")), name="ffn_dh")(d_a, d_b, w_gate_t, w_up_t)


def _local_step(x, tgt, wts, small):
    s = SEQ
    cos_f, sin_s = [_to_phase_rows(t) for t in _rope_tables()]
    x = _reorder_rows(x, to_phase=True, name="phase_rows_x")
    tgt = _reorder_rows(tgt, to_phase=True, name="phase_rows_target")

    proj = _project_in(x, wts["w_in"], cos_f, sin_s)

    attn, lse = _attention_fwd(proj)

    (abar_r, abar_i, bbar_r, bbar_i), ssm_vjp = jax.vjp(
        _ssm_discretise, small["ssm_a_re"], small["ssm_a_im"], small["ssm_log_dt"], small["ssm_b_re"], small["ssm_b_im"])
    b_in_r, b_in_i = [_slab_block_diag(b.transpose(0, 2, 1)).astype(BF16) for b in (bbar_r, bbar_i)]
    c_out_r = _slab_block_diag(small["ssm_c_re"].transpose(0, 2, 1)).astype(BF16)
    c_out_ni = _slab_block_diag(-small["ssm_c_im"].transpose(0, 2, 1)).astype(BF16)
    a_r, a_i = abar_r.reshape(1, SSM_LANES), abar_i.reshape(1, SSM_LANES)
    d_skip = small["ssm_d"].reshape(1, SSM_WIDTH)

    u_f = _to_scan_rows(proj[:, 3 * QKV_WIDTH:3 * QKV_WIDTH + SSM_WIDTH])
    u_p = u_f.astype(BF16)
    y_c, h_r, h_i, e_r, e_i = _ssm_forward(u_p, b_in_r, b_in_i, c_out_r, c_out_ni, a_r, a_i)

    def branch(t, wg):
        return jnp.concatenate([jnp.dot(t, wg[k], preferred_element_type=F32) for k in range(N_CHIPS)], axis=1)

    def branch_t(t, wg):
        ns = wg.shape[2]
        return sum(_dot_nt(t[:, k * ns:(k + 1) * ns], wg[k]) for k in range(N_CHIPS))

    def gelu_glu(yc, u, dsk, wg):
        y = yc + dsk * u
        gel = (0.5 * y * (1.0 + jnp.tanh(GELU_C * (y + GELU_K * y * y * y)))).astype(BF16)
        glu = branch(gel, wg)
        return y, gel, glu, glu[:, :SSM_WIDTH] * jax.nn.sigmoid(glu[:, SSM_WIDTH:])

    y_s5, gel, glu, y_glu = _rowwise(
        gelu_glu, [y_c, u_f], [d_skip, wts["w_glu"]],
        [_sds((s, SSM_WIDTH)), _sds((s, SSM_WIDTH), BF16), _sds((s, 2 * SSM_WIDTH)), _sds((s, SSM_WIDTH), BF16)],
        tm=512, name="ssm_gelu_glu")
    y_glu = _from_scan_rows(y_glu)

    gl0 = (proj, D_MODEL, (3 * QKV_WIDTH + SSM_WIDTH) // D_MODEL)
    gl1 = (proj, D_MODEL, (3 * QKV_WIDTH + SSM_WIDTH) // D_MODEL + 1)
    b_gate = small["b_gate"]
    w_out = wts["w_out"].reshape(D_MODEL, D_MODEL)

    def mix_ln1(l0, l1, at, yg, xv, bg, wa, ws, wo, g, b):
        ya = branch(at.astype(BF16), wa)
        ys = branch(yg, ws)
        mixed = (jax.nn.sigmoid(l0 + bg[0:1]) * ya + jax.nn.sigmoid(l1 + bg[1:2]) * ys).astype(BF16)
        z = DN_ALPHA * xv + jnp.dot(mixed, wo, preferred_element_type=F32)
        xhat, _ = _ln_stats(z)
        return ya, ys, mixed, z, xhat * g + b

    y_attn, y_ssm, mixed, z1, h = _rowwise(
        mix_ln1, [gl0, gl1, attn, y_glu, x],
        [b_gate, wts["w_attn_br"], wts["w_ssm_br"], w_out, small["ln1_g"], small["ln1_b"]],
        [_sds((s, D_MODEL)), _sds((s, D_MODEL)), _sds((s, D_MODEL), BF16), _sds((s, D_MODEL)), _sds((s, D_MODEL))],
        tm=256, name="mix_ln1")

    nf = D_FF // N_CHIPS
    w_gate_t, w_up_t, w_down = wts["w_ff_gate"], wts["w_ff_up"], wts["w_ff_down"]
    ff_a, ff_b, act = _ffn_up(h, w_gate_t, w_up_t)
    dz2, loss_v, d_ln2_g, d_ln2_b = _ffn_down_ln2_loss(act, w_down, h, tgt, small["ln2_g"], small["ln2_b"])

    d_a, d_b = _ffn_down_bwd(dz2, w_down, ff_a, ff_b)

    def grad_rows(lhs, rhs, name):
        return _matmul(lhs, rhs, grid=(N_CHIPS,), a_spec=pl.BlockSpec((None, s, nf), lambda k: (k, 0, 0)),
                       b_spec=pl.BlockSpec((s, D_MODEL), lambda k: (0, 0)),
                       o_spec=pl.BlockSpec((None, nf, D_MODEL), lambda k: (k, 0, 0)),
                       out_shape=_sds((N_CHIPS, nf, D_MODEL), BF16), dims=(0, 0), name=name)

    g_w_ff_down = grad_rows(act, dz2, "g_w_ff_down")
    g_w_ff_gate = grad_rows(d_a, h, "g_w_ff_gate")
    g_w_ff_up = grad_rows(d_b, h, "g_w_ff_up")
    dh_ff = _ffn_dh(d_a, d_b, w_gate_t, w_up_t)

    def ln1_gate_bwd(dz, dff, z, l0, l1, ya, ys, g, bg, wo, wa, ws):
        xhat, rstd = _ln_stats(z)
        dh = DN_ALPHA * dz + dff
        dz_in = _ln_bwd(dh, xhat, rstd, g)
        dm = _dot_nt(dz_in.astype(BF16), wo)
        g0 = jax.nn.sigmoid(l0 + bg[0:1])
        g1 = jax.nn.sigmoid(l1 + bg[1:2])
        dl0 = dm * ya * g0 * (1.0 - g0)
        dl1 = dm * ys * g1 * (1.0 - g1)
        dya, dys = (dm * g0).astype(BF16), (dm * g1).astype(BF16)
        return (dz_in, dya, dys, jnp.concatenate([dl0, dl1], axis=1), branch_t(dya, wa), branch_t(dys, ws),
                _colsum(dh * xhat), _colsum(dh), _colsum(dl0), _colsum(dl1))

    dz1, d_y_attn, d_y_ssm, d_gl, d_attn, d_y_glu, d_ln1_g, d_ln1_b, d_bg0, d_bg1 = _rowwise(
        ln1_gate_bwd, [dz2, dh_ff, z1, gl0, gl1, y_attn, y_ssm],
        [small["ln1_g"], b_gate, w_out, wts["w_attn_br"], wts["w_ssm_br"]],
        [_sds((s, D_MODEL)), _sds((s, D_MODEL), BF16), _sds((s, D_MODEL), BF16), _sds((s, 2 * D_MODEL), BF16),
         _sds((s, ATTN_WIDTH)), _sds((s, SSM_WIDTH))],
        [_sds((1, D_MODEL))] * 4, tm=256, name="ln1_gate_bwd")
    g_w_out = _mm_rows_tn(mixed, dz1, name="g_w_out")

    g_w_ssm_br = _mm_cols_tn(y_glu, d_y_ssm, ns=D_MODEL // N_CHIPS, name="g_w_ssm_br")
    d_y_glu = _to_scan_rows(d_y_glu)

    def glu_gelu_bwd(dyg, gl, y, u, dsk, wg):
        ga, gb = gl[:, :SSM_WIDTH], gl[:, SSM_WIDTH:]
        sg = jax.nn.sigmoid(gb)
        d_gl = jnp.concatenate([dyg * sg, dyg * ga * sg * (1.0 - sg)], axis=1).astype(BF16)
        dg = branch_t(d_gl, wg)
        th = jnp.tanh(GELU_C * (y + GELU_K * y * y * y))
        dy = dg * (0.5 * (1.0 + th) + 0.5 * y * (1.0 - th * th) * GELU_C * (1.0 + 3.0 * GELU_K * y * y))
        return d_gl, dy, dy * dsk, _colsum(dy * u)

    d_glu, d_y, d_u_skip, d_ssm_d = _rowwise(
        glu_gelu_bwd, [d_y_glu, glu, y_s5, u_f], [d_skip, wts["w_glu"]],
        [_sds((s, 2 * SSM_WIDTH), BF16), _sds((s, SSM_WIDTH), BF16), _sds((s, SSM_WIDTH))], [_sds((1, SSM_WIDTH))],
        tm=512, name="glu_gelu_bwd")
    g_w_glu = _mm_cols_tn(gel, d_glu, ns=2 * SSM_WIDTH // N_CHIPS, name="g_w_glu")
    d_u, d_abar_r, d_abar_i, d_c_r, d_c_ni, d_bin_r, d_bin_i = _ssm_backward(
        d_y, d_u_skip, u_p, h_r, h_i, e_r, e_i, b_in_r, b_in_i, c_out_r, c_out_ni, a_r, a_i)
    d_u = _from_scan_rows(d_u)
    d_bbar_r = d_bin_r.reshape(SSM_GROUPS, SSM_GROUP, SSM_STATE).transpose(0, 2, 1)
    d_bbar_i = d_bin_i.reshape(SSM_GROUPS, SSM_GROUP, SSM_STATE).transpose(0, 2, 1)
    d_a_re, d_a_im, d_log_dt, d_b_re, d_b_im = ssm_vjp(
        (d_abar_r.reshape(SSM_GROUPS, SSM_STATE), d_abar_i.reshape(SSM_GROUPS, SSM_STATE), d_bbar_r, d_bbar_i))
    d_c_re = d_c_r.reshape(SSM_GROUPS, SSM_GROUP, SSM_STATE)
    d_c_im = -d_c_ni.reshape(SSM_GROUPS, SSM_GROUP, SSM_STATE)

    g_w_attn_br = _mm_cols_tn(attn, d_y_attn, ns=D_MODEL // N_CHIPS, name="g_w_attn_br")
    d_q, d_k, d_v = _attention_bwd(proj, cos_f, sin_s, d_attn, attn, lse)

    d_proj = jnp.concatenate([d_q, d_k, d_v, d_u, d_gl], axis=1)
    g_w_in = _mm_cols_tn(x, d_proj, ns=IN_WIDTH // N_CHIPS, name="g_w_in")

    def grad_x_after(after):
        dx_proj = _mm_cols_nt(d_proj, wts["w_in"], tm=1024, name="dx_proj", after=after)
        return _reorder_rows(dz1, dx_proj, to_phase=False, name="grad_x", scale=DN_ALPHA)

    big = {"w_in": g_w_in, "w_attn_br": g_w_attn_br, "w_ssm_br": g_w_ssm_br, "w_out": g_w_out, "w_glu": g_w_glu,
           "w_ff_gate": g_w_ff_gate, "w_ff_up": g_w_ff_up, "w_ff_down": g_w_ff_down}
    small_g = {"b_gate": jnp.concatenate([d_bg0, d_bg1], axis=0), "ssm_a_re": d_a_re, "ssm_a_im": d_a_im,
               "ssm_log_dt": d_log_dt, "ssm_b_re": d_b_re, "ssm_b_im": d_b_im, "ssm_c_re": d_c_re, "ssm_c_im": d_c_im,
               "ssm_d": d_ssm_d.reshape(SSM_WIDTH), "ln1_g": d_ln1_g, "ln1_b": d_ln1_b, "ln2_g": d_ln2_g,
               "ln2_b": d_ln2_b}
    marks = {"ln1_bwd": dz1, "scan_bwd": d_abar_r, "attention_bwd": d_q}
    return loss_v[0, 0], grad_x_after, big, small_g, marks


GATHER_ID, SWAP_ID, SCATTER_ID, JOIN_ID, EXCHANGE_ID = 1, 2, 3, 4, 5


def _place():
    return lax.axis_index("x"), lax.axis_index("y"), lax.axis_index("c")


def _other_chips(x, y):
    return [(1 - x, y), (x, 1 - y), (1 - x, 1 - y)]


def _handshake(peers):
    barrier = pltpu.get_barrier_semaphore()
    for peer in peers:
        pl.semaphore_signal(barrier, inc=1, device_id=peer, device_id_type=MESH)
    pl.semaphore_wait(barrier, len(peers))


def _sequencer(body, arrays, out_type, sems, collective_id, name):
    return pl.kernel(body, name=name, out_type=out_type,
                     mesh=plsc.ScalarSubcoreMesh(axis_name="sequencer", num_cores=1), scratch_types=sems,
                     compiler_params=pltpu.CompilerParams(collective_id=collective_id))(*arrays)


def _gather_weights(shards, *, name):
    nw = len(shards)

    def body(*refs):
        ins, outs = refs[:nw], refs[nw:2 * nw]
        send_sems, recv_sems, pass_send, pass_recv, local_sems = refs[2 * nw:]
        x, y, c = _place()
        chip = 2 * x + y
        chips = _other_chips(x, y)
        _handshake([(x, y, 1 - c)] + [(cx, cy, c) for cx, cy in chips])
        started = []
        for w in range(nw):
            hw = shards[w].shape[0] // 2
            mine = pl.ds(c * hw, hw)
            own = pltpu.make_async_copy(ins[w], outs[w].at[chip], local_sems.at[w])
            own.start()
            started.append(own)
            for j, (cx, cy) in enumerate(chips):
                cp = pltpu.make_async_remote_copy(
                    src_ref=ins[w].at[mine], dst_ref=outs[w].at[chip, mine], send_sem=send_sems.at[w, j],
                    recv_sem=recv_sems.at[w, j], device_id=(cx, cy, c), device_id_type=MESH)
                cp.start()
                started.append(cp)
        passed = []
        for w in range(nw):
            hw = shards[w].shape[0] // 2
            mine = pl.ds(c * hw, hw)
            for j, (cx, cy) in enumerate(chips):
                landed = outs[w].at[2 * cx + cy, mine]
                pltpu.make_async_remote_copy(
                    src_ref=ins[w].at[mine], dst_ref=landed, send_sem=send_sems.at[w, j],
                    recv_sem=recv_sems.at[w, j], device_id=(cx, cy, c), device_id_type=MESH).wait_recv()
                cp = pltpu.make_async_remote_copy(
                    src_ref=landed, dst_ref=landed, send_sem=pass_send.at[w, j], recv_sem=pass_recv.at[w, j],
                    device_id=(x, y, 1 - c), device_id_type=MESH)
                cp.start()
                passed.append(cp)
        for w in range(nw):
            hw = shards[w].shape[0] // 2
            theirs = pl.ds((1 - c) * hw, hw)
            for j, (cx, cy) in enumerate(chips):
                landed = outs[w].at[2 * cx + cy, theirs]
                pltpu.make_async_remote_copy(
                    src_ref=landed, dst_ref=landed, send_sem=pass_send.at[w, j], recv_sem=pass_recv.at[w, j],
                    device_id=(x, y, 1 - c), device_id_type=MESH).wait_recv()
        for cp in started[0::4]:
            cp.wait()
        for cp in [s for i, s in enumerate(started) if i % 4] + passed:
            cp.wait_send()

    sem = pltpu.SemaphoreType.DMA
    return _sequencer(body, shards, [_sds((N_CHIPS,) + a.shape, a.dtype) for a in shards],
                      [sem((nw, 3)), sem((nw, 3)), sem((nw, 3)), sem((nw, 3)), sem((nw,))], GATHER_ID, name)


def _swap_other_halves(grads, *, name):
    nw = len(grads)

    def body(*refs):
        ins, outs = refs[:nw], refs[nw:2 * nw]
        send_sems, recv_sems = refs[2 * nw:]
        x, y, c = _place()
        _handshake([(x, y, 1 - c)])
        cps = []
        for w in range(nw):
            hw = grads[w].shape[1] // 2
            cp = pltpu.make_async_remote_copy(
                src_ref=ins[w].at[:, pl.ds((1 - c) * hw, hw)], dst_ref=outs[w], send_sem=send_sems.at[w],
                recv_sem=recv_sems.at[w], device_id=(x, y, 1 - c), device_id_type=MESH)
            cp.start()
            cps.append(cp)
        for cp in cps:
            cp.wait()

    sem = pltpu.SemaphoreType.DMA
    return _sequencer(body, grads, [_sds((N_CHIPS, g.shape[1] // 2, g.shape[2]), g.dtype) for g in grads],
                      [sem((nw,)), sem((nw,))], SWAP_ID, name)


def _add_my_halves(core, grads, others, *, name, after=()):
    nw = len(grads)
    halves = [g.shape[1] // 2 for g in grads]

    def body(core_ref, *refs):
        outs = refs[2 * nw + len(after):]
        for g_ref, o_ref, out_ref in zip(refs[:nw], refs[nw:2 * nw], outs):
            out_ref[...] = (g_ref[...].astype(F32) + o_ref[...].astype(F32)).astype(out_ref.dtype)

    in_specs = [pl.BlockSpec((None, None, hw, g.shape[2]), lambda s, core_ref: (s, core_ref[0], 0, 0))
                for g, hw in zip(grads, halves)]
    in_specs += [pl.BlockSpec((None, hw, g.shape[2]), lambda s, core_ref: (s, 0, 0)) for g, hw in zip(grads, halves)]
    return pl.pallas_call(
        body,
        grid_spec=pltpu.PrefetchScalarGridSpec(
            num_scalar_prefetch=1, grid=(N_CHIPS,), in_specs=in_specs + [HBM_OPERAND] * len(after),
            out_specs=[pl.BlockSpec((None, hw, g.shape[2]), lambda s, core_ref: (s, 0, 0))
                       for g, hw in zip(grads, halves)]),
        out_shape=[_sds((N_CHIPS, hw, g.shape[2]), BF16) for g, hw in zip(grads, halves)],
        compiler_params=_cp(("parallel",)), name=name)(
            core, *[g.reshape(N_CHIPS, 2, hw, g.shape[2]) for g, hw in zip(grads, halves)], *others, *after)


def _scatter_partials(parts, *, name):
    nw = len(parts)

    def body(*refs):
        ins, outs = refs[:nw], refs[nw:2 * nw]
        send_sems, recv_sems = refs[2 * nw:]
        x, y, c = _place()
        _handshake([(cx, cy, c) for cx, cy in _other_chips(x, y)])
        cps = []
        for w in range(nw):
            for j, (cx, cy) in enumerate(_other_chips(x, y)):
                cp = pltpu.make_async_remote_copy(
                    src_ref=ins[w].at[2 * cx + cy], dst_ref=outs[w].at[j], send_sem=send_sems.at[w, j],
                    recv_sem=recv_sems.at[w, j], device_id=(cx, cy, c), device_id_type=MESH)
                cp.start()
                cps.append(cp)
        for cp in cps:
            cp.wait()

    sem = pltpu.SemaphoreType.DMA
    return _sequencer(body, parts, [_sds((3,) + p.shape[1:], p.dtype) for p in parts],
                      [sem((nw, 3)), sem((nw, 3))], SCATTER_ID, name)


SUM_STEPS = 2


def _sum_partials(chip, parts, recvd, *, name, after=()):
    nw = len(parts)
    rows = [p.shape[1] // SUM_STEPS for p in parts]

    def body(chip_ref, *refs):
        outs = refs[2 * nw + len(after):]
        for p_ref, r_ref, out_ref in zip(refs[:nw], refs[nw:2 * nw], outs):
            acc = p_ref[...].astype(F32)
            for j in range(3):
                acc = acc + r_ref[j].astype(F32)
            out_ref[...] = acc

    in_specs = [pl.BlockSpec((None, th, p.shape[2]), lambda i, chip_ref: (chip_ref[0], i, 0))
                for p, th in zip(parts, rows)]
    in_specs += [pl.BlockSpec((3, th, p.shape[2]), lambda i, chip_ref: (0, i, 0)) for p, th in zip(parts, rows)]
    return pl.pallas_call(
        body,
        grid_spec=pltpu.PrefetchScalarGridSpec(
            num_scalar_prefetch=1, grid=(SUM_STEPS,), in_specs=in_specs + [HBM_OPERAND] * len(after),
            out_specs=[pl.BlockSpec((th, p.shape[2]), lambda i, chip_ref: (i, 0)) for p, th in zip(parts, rows)]),
        out_shape=[_sds(p.shape[1:]) for p in parts], compiler_params=_cp(("parallel",)), name=name)(
            chip, *parts, *recvd, *after)


def _swap_reduced_halves(halves, *, name):
    nw = len(halves)

    def body(*refs):
        ins, outs = refs[:nw], refs[nw:2 * nw]
        send_sems, recv_sems = refs[2 * nw:]
        x, y, c = _place()
        _handshake([(x, y, 1 - c)])
        cps = []
        for w in range(nw):
            cp = pltpu.make_async_remote_copy(
                src_ref=ins[w], dst_ref=outs[w], send_sem=send_sems.at[w], recv_sem=recv_sems.at[w],
                device_id=(x, y, 1 - c), device_id_type=MESH)
            cp.start()
            cps.append(cp)
        for cp in cps:
            cp.wait()

    sem = pltpu.SemaphoreType.DMA
    return _sequencer(body, halves, [_sds(h.shape, h.dtype) for h in halves], [sem((nw,)), sem((nw,))], JOIN_ID, name)


def _exchange_rows(vec, *, name):
    def body(v_ref, slots, send_sems, recv_sems, local_sem):
        x, y, c = _place()
        me = 4 * x + 2 * y + c
        peers = []
        for mask in range(1, N_DEV):
            peers.append((1 - x if mask & 4 else x, 1 - y if mask & 2 else y, 1 - c if mask & 1 else c))
        _handshake(peers)
        own = pltpu.make_async_copy(v_ref, slots.at[me], local_sem)
        own.start()
        cps = []
        for k, peer in enumerate(peers):
            cp = pltpu.make_async_remote_copy(
                src_ref=v_ref, dst_ref=slots.at[me], send_sem=send_sems.at[k], recv_sem=recv_sems.at[k],
                device_id=peer, device_id_type=MESH)
            cp.start()
            cps.append(cp)
        for k, (px, py, pc) in enumerate(peers):
            pltpu.make_async_remote_copy(
                src_ref=v_ref, dst_ref=slots.at[4 * px + 2 * py + pc], send_sem=send_sems.at[k],
                recv_sem=recv_sems.at[k], device_id=(px, py, pc), device_id_type=MESH).wait_recv()
        for cp in cps:
            cp.wait_send()
        own.wait()

    sem = pltpu.SemaphoreType.DMA
    return _sequencer(body, [vec], [_sds((N_DEV,) + vec.shape)], [sem((N_DEV - 1,)), sem((N_DEV - 1,)), sem(())],
                      EXCHANGE_ID, name)[0]


def _sum_slots(slots, *, name, after=()):
    def body(s_ref, *rest):
        out_ref = rest[len(after)]
        acc = s_ref[0]
        for d in range(1, N_DEV):
            acc = acc + s_ref[d]
        out_ref[...] = acc

    vmem = pl.BlockSpec(memory_space=pltpu.VMEM)
    return pl.pallas_call(
        body, in_specs=[vmem] + [HBM_OPERAND] * len(after), out_specs=vmem, out_shape=_sds(slots.shape[1:]),
        compiler_params=pltpu.CompilerParams(vmem_limit_bytes=VMEM_LIMIT_BYTES), name=name)(slots, *after)


def _reduce_scatter_start(grads, core, *, tag, add_after=()):
    others = _swap_other_halves(grads, name="swap_other_halves_" + tag)
    parts = _add_my_halves(core, grads, others, name="add_my_halves_" + tag, after=add_after)
    return parts, _scatter_partials(parts, name="scatter_partials_" + tag)


def _reduce_scatter_finish(parts, recvd, chip, *, tag, sum_after=()):
    mine = _sum_partials(chip, parts, recvd, name="sum_partials_" + tag, after=sum_after)
    return mine, _swap_reduced_halves(mine, name="swap_reduced_halves_" + tag)


ADAM_BLOCK_ELEMS = 256 * 1024


def _adam_rows(rows, cols):
    tm = rows
    while tm * cols > ADAM_BLOCK_ELEMS and tm % 16 == 0:
        tm //= 2
    return tm


def _adam_step(wv, gv, mv, vv):
    m2 = ADAM_B1 * mv + (1.0 - ADAM_B1) * gv
    v2 = ADAM_B2 * vv + (1.0 - ADAM_B2) * (gv * gv)
    m_hat = m2 / (1.0 - ADAM_B1 ** ADAM_STEP)
    v_hat = v2 / (1.0 - ADAM_B2 ** ADAM_STEP)
    return -ADAM_LR * (m_hat / (jnp.sqrt(v_hat) + ADAM_EPS) + ADAM_WD * wv), m2, v2


def _adamw(w, g, m, v, *, name):
    rows, cols = w.shape
    return _rowwise(_adam_step, [w, g, m, v], [], [_sds((rows, cols))] * 3, tm=_adam_rows(rows, cols), name=name)


def _adamw_halves(core, w, g_mine, g_theirs, m, v, *, name, after=()):
    rows, cols = w.shape
    hw = rows // 2
    tm = _adam_rows(hw, cols)
    per_half = hw // tm

    def body(core_ref, w_ref, gm_ref, gt_ref, m_ref, v_ref, *rest):
        g_out, d_out, m_out, v_out = rest[len(after):]
        mine = (pl.program_id(0) // per_half) == core_ref[0]
        g = jnp.where(mine, gm_ref[...], gt_ref[...])
        d, m2, v2 = _adam_step(w_ref[...], g, m_ref[...], v_ref[...])
        g_out[...] = g
        d_out[...] = d
        m_out[...] = m2
        v_out[...] = v2

    full = pl.BlockSpec((tm, cols), lambda i, core_ref: (i, 0))

    def half(wanted):
        def index(i, core_ref):
            in_use = ((i // per_half) == core_ref[0]) == wanted
            return (jnp.where(in_use, i % per_half, 0), 0)
        return pl.BlockSpec((tm, cols), index)

    return pl.pallas_call(
        body,
        grid_spec=pltpu.PrefetchScalarGridSpec(
            num_scalar_prefetch=1, grid=(rows // tm,),
            in_specs=[full, half(True), half(False), full, full] + [HBM_OPERAND] * len(after),
            out_specs=[full, full, full, full]),
        out_shape=[_sds((rows, cols))] * 4, compiler_params=_cp(("parallel",)), name=name)(
            core, w, g_mine, g_theirs, m, v, *after)


HELD_TRANSPOSED = ("w_ff_gate", "w_ff_up")


def _as_rows(name, arr):
    return arr[0].T if name in HELD_TRANSPOSED else arr[0]


def _from_rows(name, arr2d):
    return (arr2d.T if name in HELD_TRANSPOSED else arr2d)[None]


STORED_SWAPPED = ("ssm_b_re", "ssm_b_im")


def _as_stored(name, arr):
    return jnp.swapaxes(arr, -1, -2) if name in STORED_SWAPPED else arr


def _pack_rows(arrs):
    flat = jnp.concatenate([a.reshape(-1).astype(F32) for a in arrs])
    rows = -(-flat.shape[0] // 1024) * 8
    return jnp.pad(flat, (0, rows * 128 - flat.shape[0])).reshape(rows, 128)


def _unpack_rows(vec, shapes):
    flat = vec.reshape(-1)
    out, off = [], 0
    for shp in shapes:
        size = math.prod(shp)
        out.append(flat[off:off + size].reshape(shp))
        off += size
    return out


SMALL = ("b_gate", "ssm_a_re", "ssm_a_im", "ssm_log_dt", "ssm_b_re", "ssm_b_im", "ssm_c_re", "ssm_c_im", "ssm_d",
         "ln1_g", "ln1_b", "ln2_g", "ln2_b")
GATHER_GROUPS = (("w_in", ("w_in",)), ("mixer", ("w_attn_br", "w_ssm_br", "w_glu", "w_out")),
                 ("ffn", ("w_ff_gate", "w_ff_up", "w_ff_down")))
REDUCE_GROUPS = (("ffn", ("w_ff_down", "w_ff_gate", "w_ff_up")),
                 ("mixer", ("w_out", "w_ssm_br", "w_glu", "w_attn_br")), ("w_in", ("w_in",)))
WEIGHTS = ("w_in", "b_gate", "w_attn_br", "w_ssm_br", "w_out", "ssm_a_re", "ssm_a_im", "ssm_log_dt", "ssm_b_re",
           "ssm_b_im", "ssm_c_re", "ssm_c_im", "ssm_d", "w_glu", "ln1_g", "ln1_b", "w_ff_gate", "w_ff_up", "w_ff_down",
           "ln2_g", "ln2_b")


def kernel(x, w_in, b_gate, w_attn_br, w_ssm_br, w_out, ssm_a_re, ssm_a_im, ssm_log_dt, ssm_b_re, ssm_b_im, ssm_c_re, ssm_c_im, ssm_d, w_glu, ln1_g, ln1_b, w_ff_gate, w_ff_up, w_ff_down, ln2_g, ln2_b, loss_target, m_w_in, m_b_gate, m_w_attn_br, m_w_ssm_br, m_w_out, m_ssm_a_re, m_ssm_a_im, m_ssm_log_dt, m_ssm_b_re, m_ssm_b_im, m_ssm_c_re, m_ssm_c_im, m_ssm_d, m_w_glu, m_ln1_g, m_ln1_b, m_w_ff_gate, m_w_ff_up, m_w_ff_down, m_ln2_g, m_ln2_b, v_w_in, v_b_gate, v_w_attn_br, v_w_ssm_br, v_w_out, v_ssm_a_re, v_ssm_a_im, v_ssm_log_dt, v_ssm_b_re, v_ssm_b_im, v_ssm_c_re, v_ssm_c_im, v_ssm_d, v_w_glu, v_ln1_g, v_ln1_b, v_w_ff_gate, v_w_ff_up, v_w_ff_down, v_ln2_g, v_ln2_b):
    given = dict(locals())
    px, py, pc = _place()
    chip = 2 * px + py
    core_s = jnp.reshape(pc, (1,)).astype(jnp.int32)
    chip_s = jnp.reshape(chip, (1,)).astype(jnp.int32)

    wts = {}
    for tag, names in GATHER_GROUPS:
        wts.update(zip(names, _gather_weights([_as_rows(n, given[n]).astype(BF16) for n in names],
                                              name="gather_" + tag)))
    ncol = D_MODEL // N_CHIPS
    bg_mine = jnp.where(pc == 0, b_gate[0], jnp.zeros_like(b_gate[0]))
    bg_full = lax.dynamic_update_slice(jnp.zeros((2, D_MODEL), F32), bg_mine, (0, chip * ncol))
    bg_slots = _exchange_rows(bg_full.reshape(16, 128), name="exchange_gate_bias")
    bg_full = _sum_slots(bg_slots, name="sum_gate_bias").reshape(2, D_MODEL)
    small = {n: given[n][0] for n in SMALL if n.startswith("ssm")}
    small.update({n: given[n] for n in ("ln1_g", "ln1_b", "ln2_g", "ln2_b")})
    small["b_gate"] = bg_full

    loss_mine, grad_x_after, big_g, small_g, marks = _local_step(x[0], loss_target[0], wts, small)

    groups = dict(REDUCE_GROUPS)
    parts, recvd = {}, {}
    grads, delta, new_m, new_v = {}, {}, {}, {}

    def start(tag, add_after):
        parts[tag], recvd[tag] = _reduce_scatter_start([big_g[n] for n in groups[tag]], core_s, tag=tag,
                                                       add_after=add_after)

    def finish(tag, sum_after, adam_after):
        mine, theirs = _reduce_scatter_finish(parts[tag], recvd[tag], chip_s, tag=tag, sum_after=sum_after)
        for n, g_mine, g_theirs in zip(groups[tag], mine, theirs):
            res = _adamw_halves(core_s, _as_rows(n, given[n]), g_mine, g_theirs, _as_rows(n, given["m_" + n]),
                                _as_rows(n, given["v_" + n]), name="adamw_" + n, after=adam_after)
            grads[n], delta[n], new_m[n], new_v[n] = [_from_rows(n, r) for r in res]

    start("ffn", (marks["ln1_bwd"],))
    start("mixer", (marks["scan_bwd"],))
    finish("mixer", (marks["attention_bwd"],), (big_g["w_in"],))
    start("w_in", tuple(delta[n] for n in groups["mixer"]))
    in_flight = (parts["w_in"][0],)
    grad_x = grad_x_after(in_flight)
    finish("ffn", (marks["scan_bwd"],), in_flight)
    stored = [_as_stored(n, small_g[n]) for n in SMALL] + [loss_mine.reshape(1)]
    slots = _exchange_rows(_pack_rows(stored), name="exchange_small")
    summed = _unpack_rows(_sum_slots(slots, name="sum_small", after=tuple(delta[n] for n in groups["ffn"])),
                          [a.shape for a in stored])
    loss = summed.pop()[0]
    for n, g in zip(SMALL, summed):
        g = _as_stored(n, g)
        if n == "b_gate":
            g = lax.dynamic_slice(g, (0, chip * ncol), (2, ncol))
        grads[n] = g.reshape(given[n].shape)
    packed = [_pack_rows([_as_stored(n, src[n]) for n in SMALL]) for src in
              (given, grads, {n: given["m_" + n] for n in SMALL}, {n: given["v_" + n] for n in SMALL})]
    shapes = [_as_stored(n, given[n]).shape for n in SMALL]
    small_out = _adamw(*packed, name="adamw_small")
    for out, vec in zip((delta, new_m, new_v), small_out):
        out.update((n, _as_stored(n, a)) for n, a in zip(SMALL, _unpack_rows(vec, shapes)))
    behind = [delta[n] for n in groups["ffn"]] + [small_out[0], grad_x]
    finish("w_in", tuple(behind), ())

    return (loss, grad_x.reshape(x.shape), *[grads[n] for n in WEIGHTS], *[delta[n] for n in WEIGHTS],
            *[new_m[n] for n in WEIGHTS], *[new_v[n] for n in WEIGHTS])
```

```python
import math

import jax
import jax.numpy as jnp
from jax import lax
from jax.experimental import pallas as pl
from jax.experimental.pallas import tpu as pltpu
from jax.experimental.pallas import tpu_sc as plsc

F32 = jnp.float32
BF16 = jnp.bfloat16
MESH = pl.DeviceIdType.MESH

D_MODEL = 1024
SEQ = 2048
HEAD_DIM = 64
ATTN_HEADS = 8
DILATIONS = (1, 4, 16)
ATTN_WIDTH = ATTN_HEADS * HEAD_DIM
QKV_WIDTH = 3 * ATTN_WIDTH
BLOCK = 128
ROPE_THETA = 10000.0
NEG_INF = -1e30
SSM_GROUP = 16
SSM_GROUPS = 32
SSM_WIDTH = 512
SSM_STATE = 64
SSM_LANES = SSM_GROUPS * SSM_STATE
SCAN_CHUNKS = 8
SCAN_STEPS = SEQ // SCAN_CHUNKS
IN_WIDTH = 3 * QKV_WIDTH + SSM_WIDTH + 2 * D_MODEL
D_FF = 2816
N_CHIPS = 4
N_DEV = 8
DN_ALPHA = 2.0 ** 0.25
LN_EPS = 1e-5
ADAM_LR = 0.001
ADAM_B1 = 0.9
ADAM_B2 = 0.999
ADAM_EPS = 1e-08
ADAM_WD = 0.01
ADAM_STEP = 10
GELU_C = math.sqrt(2.0 / math.pi)
GELU_K = 0.044715

VMEM_LIMIT_BYTES = 56 * 1024 * 1024


def _sds(shape, dtype=F32):
    return jax.ShapeDtypeStruct(tuple(shape), dtype)


def _cp(semantics=None):
    return pltpu.CompilerParams(dimension_semantics=semantics, vmem_limit_bytes=VMEM_LIMIT_BYTES)


HBM_OPERAND = pl.BlockSpec(memory_space=pl.ANY)


def _matmul(a, b, *, grid, a_spec, b_spec, o_spec, out_shape, dims, k_axis=None, name, after=()):
    nk = grid[k_axis] if k_axis is not None else 1
    o_block = tuple(d for d in o_spec.block_shape if d is not None)
    n_after = len(after)

    def body(a_ref, b_ref, *rest):
        o_ref, acc = rest[n_after], rest[n_after + 1:]
        part = lax.dot_general(a_ref[...].astype(BF16), b_ref[...].astype(BF16),
                               (((dims[0],), (dims[1],)), ((), ())), preferred_element_type=F32)
        if k_axis is None:
            o_ref[...] = part.astype(o_ref.dtype)
        else:
            k = pl.program_id(k_axis)

            @pl.when(k == 0)
            def _():
                acc[0][...] = part

            @pl.when(k > 0)
            def _():
                acc[0][...] += part

            @pl.when(k == nk - 1)
            def _():
                o_ref[...] = acc[0][...].astype(o_ref.dtype)

    sem = tuple("arbitrary" if ax == k_axis else "parallel" for ax in range(len(grid)))
    return pl.pallas_call(
        body, grid=grid, in_specs=[a_spec, b_spec] + [HBM_OPERAND] * n_after, out_specs=o_spec, out_shape=out_shape,
        scratch_shapes=[pltpu.VMEM(o_block, F32)] if k_axis is not None else [],
        compiler_params=_cp(sem), name=name)(a, b, *after)


def _mm_cols_nt(dy, wg, *, tm, name, out_dtype=F32, after=()):
    k, ns = wg.shape[1], wg.shape[2]
    m = dy.shape[0]
    a_spec = pl.BlockSpec((tm, ns), lambda i, s: (i, s))
    return _matmul(dy, wg, grid=(m // tm, N_CHIPS), a_spec=a_spec,
                   b_spec=pl.BlockSpec((None, k, ns), lambda i, s: (s, 0, 0)),
                   o_spec=pl.BlockSpec((tm, k), lambda i, s: (i, 0)),
                   out_shape=_sds((m, k), out_dtype), dims=(1, 1), k_axis=1, name=name, after=after)


def _mm_cols_tn(a, dy, *, ns, name, after=()):
    m, k = a.shape
    return _matmul(a, dy, grid=(N_CHIPS,), a_spec=pl.BlockSpec((m, k), lambda s: (0, 0)),
                   b_spec=pl.BlockSpec((m, ns), lambda s: (0, s)),
                   o_spec=pl.BlockSpec((None, k, ns), lambda s: (s, 0, 0)),
                   out_shape=_sds((N_CHIPS, k, ns), BF16), dims=(0, 0), name=name, after=after)


def _mm_rows_tn(a, dy, *, name):
    m, k = a.shape
    rows, n = k // N_CHIPS, dy.shape[1]
    return _matmul(a, dy, grid=(N_CHIPS,), a_spec=pl.BlockSpec((m, rows), lambda s: (0, s)),
                   b_spec=pl.BlockSpec((m, n), lambda s: (0, 0)),
                   o_spec=pl.BlockSpec((None, rows, n), lambda s: (s, 0, 0)),
                   out_shape=_sds((N_CHIPS, rows, n), BF16), dims=(0, 0), name=name)


def _rowwise(fn, tiled, full, outs, accs=(), *, tm, name, after=()):
    args, in_specs = [], []
    for t in tiled:
        if isinstance(t, tuple):
            arr, w, cb = t
            in_specs.append(pl.BlockSpec((tm, w), lambda i, cb=cb: (i, cb)))
        else:
            arr = t
            in_specs.append(pl.BlockSpec((tm, arr.shape[1]), lambda i: (i, 0)))
        args.append(arr)
    rows = args[0].shape[0]
    for f in full:
        in_specs.append(pl.BlockSpec(f.shape, lambda i, nd=f.ndim: (0,) * nd))
        args.append(f)
    out_specs = [pl.BlockSpec((tm, o.shape[1]), lambda i: (i, 0)) for o in outs]
    out_specs += [pl.BlockSpec(a.shape, lambda i, nd=len(a.shape): (0,) * nd) for a in accs]
    n_in, n_out = len(args), len(outs)
    in_specs += [HBM_OPERAND] * len(after)
    first_out = n_in + len(after)

    def body(*refs):
        res = fn(*[r[...] for r in refs[:n_in]])
        res = res if isinstance(res, (tuple, list)) else (res,)
        for r, v in zip(refs[first_out:first_out + n_out], res[:n_out]):
            r[...] = v.astype(r.dtype)
        i = pl.program_id(0)
        for r, v in zip(refs[first_out + n_out:], res[n_out:]):
            @pl.when(i == 0)
            def _(r=r, v=v):
                r[...] = v

            @pl.when(i > 0)
            def _(r=r, v=v):
                r[...] += v

    res = pl.pallas_call(
        body, grid=(rows // tm,), in_specs=in_specs, out_specs=out_specs, out_shape=list(outs) + list(accs),
        compiler_params=_cp(("arbitrary",) if accs else ("parallel",)), name=name)(*args, *after)
    return res


def _colsum(v):
    return jnp.sum(v, axis=0, keepdims=True)


def _ln_stats(z):
    mu = jnp.mean(z, axis=-1, keepdims=True)
    zc = z - mu
    var = jnp.mean(zc * zc, axis=-1, keepdims=True)
    rstd = lax.rsqrt(var + LN_EPS)
    return zc * rstd, rstd


def _ln_bwd(dy, xhat, rstd, g):
    dxh = dy * g
    m1 = jnp.mean(dxh, axis=-1, keepdims=True)
    m2 = jnp.mean(dxh * xhat, axis=-1, keepdims=True)
    return rstd * (dxh - m1 - xhat * m2)


def _swap_halves(t):
    w = t.shape[-1]
    lane = lax.broadcasted_iota(jnp.int32, t.shape, t.ndim - 1)
    return jnp.where((lane % HEAD_DIM) < HEAD_DIM // 2, pltpu.roll(t, w - HEAD_DIM // 2, t.ndim - 1),
                     pltpu.roll(t, HEAD_DIM // 2, t.ndim - 1))


PHASES = max(DILATIONS)
PAIR = 2 * HEAD_DIM
UNITS = SEQ // BLOCK
UNIT_BATCH = 16
ROPE_ROWS = 256


def _to_phase_rows(t):
    return t.reshape(SEQ // PHASES, PHASES, t.shape[1]).transpose(1, 0, 2).reshape(t.shape)


def _reorder_rows(arr, plus=None, *, to_phase, name, scale=1.0):
    def body(*refs):
        o_ref = refs[-1]
        for rho in range(PHASES):
            phase = pl.ds(rho * BLOCK, BLOCK)
            strided = pl.ds(rho, BLOCK, stride=PHASES)
            src, dst = (strided, phase) if to_phase else (phase, strided)
            val = refs[0][src, :]
            if scale != 1.0:
                val = val * scale
            if plus is not None:
                val = val + refs[1][src, :]
            o_ref[dst, :] = val

    spec = pl.BlockSpec((SEQ, BLOCK), lambda j: (0, j))
    ins = [arr] if plus is None else [arr, plus]
    return pl.pallas_call(body, grid=(arr.shape[1] // BLOCK,), in_specs=[spec] * len(ins), out_specs=spec,
                          out_shape=_sds(arr.shape), compiler_params=_cp(("parallel",)), name=name)(*ins)


def _rope(t, cf, ss):
    return t * cf + _swap_halves(t) * ss


def _rope_transposed(d, cf, ss):
    return d * cf + _swap_halves(d * ss)


def _unit_pieces(u, dil):
    pieces, length = PHASES // dil, 8 * dil
    if dil == 1:
        rho, i = 0, u
    elif dil == PHASES:
        rho, i = u, 0
    else:
        rho, i = jnp.bitwise_and(u, dil - 1), jnp.right_shift(u, dil.bit_length() - 1)
    before = jnp.maximum(i - 1, 0)
    cur = [pl.multiple_of((rho + dil * k) * BLOCK + length * i, 8) for k in range(pieces)]
    prev = [pl.multiple_of((rho + dil * k) * BLOCK + length * before, 8) for k in range(pieces)]
    return i, cur, prev


def _load_tile(ref, starts, dil):
    return jnp.concatenate([ref[pl.ds(st, 8 * dil), :] for st in starts], axis=0)


def _store_tile(ref, starts, dil, val, head=None, accumulate=False):
    length = 8 * dil
    lanes = slice(None) if head is None else pl.ds(head * HEAD_DIM, HEAD_DIM)
    cols = slice(None) if head is None else slice(head * HEAD_DIM, (head + 1) * HEAD_DIM)
    for k, st in enumerate(starts):
        piece = val[k * length:(k + 1) * length, cols]
        if accumulate:
            ref[pl.ds(st, length), lanes] += piece
        else:
            ref[pl.ds(st, length), lanes] = piece


def _tile_position(idx, dil):
    pieces, length = PHASES // dil, 8 * dil
    return pieces * jnp.bitwise_and(idx, length - 1) + jnp.right_shift(idx, length.bit_length() - 1)


def _band_mask(i, dil):
    row = lax.broadcasted_iota(jnp.int32, (BLOCK, 2 * BLOCK), 0)
    col = lax.broadcasted_iota(jnp.int32, (BLOCK, 2 * BLOCK), 1)
    key_pos = _tile_position(jnp.bitwise_and(col, BLOCK - 1), dil) + jnp.where(col >= BLOCK, 0, -BLOCK)
    dist = _tile_position(row, dil) - key_pos
    return (dist >= 0) & (dist <= BLOCK) & ((col >= BLOCK) | (i > 0))


def _causal_mask():
    row = lax.broadcasted_iota(jnp.int32, (BLOCK, BLOCK), 0)
    col = lax.broadcasted_iota(jnp.int32, (BLOCK, BLOCK), 1)
    return row >= col


def _pair_views(col0):
    return [pl.BlockSpec((SEQ, PAIR), lambda hp, g=g: (0, col0 // PAIR + g * (ATTN_WIDTH // PAIR) + hp))
            for g in range(len(DILATIONS))]


def _project_in(x, wg, cos_f, sin_s):
    ns = wg.shape[2]
    tiles = ns // PAIR

    def body(x_ref, w_ref, cf_ref, ss_ref, o_ref):
        shard = pl.program_id(1)
        xb = x_ref[...].astype(BF16)
        cf, ss = cf_ref[...], ss_ref[...]

        def write(rotated, scaled):
            for t0 in range(0, tiles, 2):
                strip = jnp.dot(xb, w_ref[:, t0 * PAIR:(t0 + 2) * PAIR], preferred_element_type=F32)
                for t in (t0, t0 + 1):
                    val = strip[:, (t - t0) * PAIR:(t - t0 + 1) * PAIR]
                    if t < rotated:
                        val = _rope(val, cf, ss)
                        if t < scaled:
                            val = val * (1.0 / math.sqrt(HEAD_DIM))
                    o_ref[:, t * PAIR:(t + 1) * PAIR] = val

        for s in range(N_CHIPS):
            rotated = min(max(2 * QKV_WIDTH - s * ns, 0), ns) // PAIR
            scaled = min(max(QKV_WIDTH - s * ns, 0), ns) // PAIR

            @pl.when(shard == s)
            def _(rotated=rotated, scaled=scaled):
                write(rotated, scaled)

    table = pl.BlockSpec((FF_ROWS, PAIR), lambda i, s: (i, 0))
    return pl.pallas_call(
        body, grid=(SEQ // FF_ROWS, N_CHIPS),
        in_specs=[pl.BlockSpec((FF_ROWS, D_MODEL), lambda i, s: (i, 0)),
                  pl.BlockSpec((None, D_MODEL, ns), lambda i, s: (s, 0, 0)), table, table],
        out_specs=pl.BlockSpec((FF_ROWS, ns), lambda i, s: (i, s)), out_shape=_sds((SEQ, N_CHIPS * ns)),
        compiler_params=_cp(("parallel", "parallel")), name="project_in")(x, wg, cos_f, sin_s)


def _attention_fwd(proj):
    ng = len(DILATIONS)

    def body(*refs):
        q_refs, k_refs, v_refs = refs[:ng], refs[ng:2 * ng], refs[2 * ng:3 * ng]
        attn_ref, lse_ref = refs[3 * ng:]
        qr_refs, kr_refs = q_refs, k_refs
        first = lax.broadcasted_iota(jnp.int32, (BLOCK, PAIR), 1) < HEAD_DIM
        for g, dil in enumerate(DILATIONS):
            two_blocks = SEQ // dil > BLOCK

            def units(t, carry, g=g, dil=dil, two_blocks=two_blocks):
                picked = [_unit_pieces(t * UNIT_BATCH + j, dil) for j in range(UNIT_BATCH)]

                def tiles(ref, with_prev=False):
                    if with_prev and two_blocks:
                        return jnp.stack([jnp.concatenate([_load_tile(ref, prev, dil), _load_tile(ref, rows, dil)],
                                                          axis=0) for _, rows, prev in picked])
                    return jnp.stack([_load_tile(ref, rows, dil) for _, rows, _ in picked])

                qq = tiles(qr_refs[g]).astype(BF16)
                kk = tiles(kr_refs[g], True).astype(BF16)
                vv = tiles(v_refs[g], True).astype(BF16)
                if two_blocks:
                    valid = jnp.stack([_band_mask(i, dil) for i, _, _ in picked])
                else:
                    valid = _causal_mask()[None]
                mine = first[None]
                zero = jnp.zeros_like(qq)
                outs, lses = [], []
                for qh in (jnp.where(mine, qq, zero), jnp.where(mine, zero, qq)):
                    s = jnp.einsum("pqd,pkd->pqk", qh, kk, preferred_element_type=F32)
                    s = jnp.where(valid, s, NEG_INF)
                    m = jnp.max(s, axis=-1, keepdims=True)
                    p = jnp.exp(s - m)
                    l = jnp.sum(p, axis=-1, keepdims=True)
                    outs.append(jnp.einsum("pqk,pkd->pqd", p.astype(BF16), vv, preferred_element_type=F32) * (1.0 / l))
                    lses.append(m + jnp.log(l))
                o = jnp.where(mine, outs[0], outs[1])
                lse = jnp.where(mine, lses[0], lses[1])
                if g > 0:
                    lse_old = tiles(lse_ref)
                    m = jnp.maximum(lse_old, lse)
                    lse_new = m + jnp.log(jnp.exp(lse_old - m) + jnp.exp(lse - m))
                    o = tiles(attn_ref) * jnp.exp(lse_old - lse_new) + o * jnp.exp(lse - lse_new)
                    lse = lse_new
                for j, (_, rows, _) in enumerate(picked):
                    _store_tile(attn_ref, rows, dil, o[j])
                    _store_tile(lse_ref, rows, dil, lse[j])
                return carry

            lax.fori_loop(0, UNITS // UNIT_BATCH, units, 0)

    out = pl.BlockSpec((SEQ, PAIR), lambda hp: (0, hp))
    return pl.pallas_call(
        body, grid=(ATTN_WIDTH // PAIR,),
        in_specs=_pair_views(0) + _pair_views(QKV_WIDTH) + _pair_views(2 * QKV_WIDTH),
        out_specs=[out, out], out_shape=[_sds((SEQ, ATTN_WIDTH)), _sds((SEQ, ATTN_WIDTH))],
        compiler_params=_cp(("parallel",)), name="attention_fwd")(*([proj] * (3 * ng)))


def _attention_bwd(proj, cos_f, sin_s, d_attn, attn, lse):
    def group(dil, qr_ref, kr_ref, v_ref, cf_ref, ss_ref, do_ref, o_ref, lse_ref, dq_out, dk_out, dv_out,
              dq_acc, dk_acc, dv_acc):
        two_blocks = SEQ // dil > BLOCK
        dk_acc[...] = jnp.zeros_like(dk_acc)
        dv_acc[...] = jnp.zeros_like(dv_acc)
        nk = 2 * BLOCK if two_blocks else BLOCK
        first = lax.broadcasted_iota(jnp.int32, (BLOCK, PAIR), 1) < HEAD_DIM
        first_k = lax.broadcasted_iota(jnp.int32, (nk, PAIR), 1) < HEAD_DIM

        def units(t, carry):
            picked = [_unit_pieces(t * UNIT_BATCH + j, dil) for j in range(UNIT_BATCH)]

            def tiles(ref, with_prev=False):
                if with_prev and two_blocks:
                    return jnp.stack([jnp.concatenate([_load_tile(ref, prev, dil), _load_tile(ref, rows, dil)], axis=0)
                                      for _, rows, prev in picked])
                return jnp.stack([_load_tile(ref, rows, dil) for _, rows, _ in picked])

            qq = tiles(qr_ref).astype(BF16)
            kk = tiles(kr_ref, True).astype(BF16)
            vv = tiles(v_ref, True).astype(BF16)
            dof = tiles(do_ref)
            dd = dof * tiles(o_ref)
            lse3 = tiles(lse_ref)
            dob = dof.astype(BF16)
            if two_blocks:
                valid = jnp.stack([_band_mask(i, dil) for i, _, _ in picked])
            else:
                valid = _causal_mask()[None]
            zq, zf = jnp.zeros_like(qq), jnp.zeros_like(dd)
            dqs, dks, dvs = [], [], []
            for head in range(2):
                mine = first[None] if head == 0 else jnp.logical_not(first)[None]
                delta = jnp.sum(jnp.where(mine, dd, zf), axis=-1, keepdims=True)
                lse_h = lse3[:, :, head * HEAD_DIM:head * HEAD_DIM + 1]
                s = jnp.einsum("pqd,pkd->pqk", jnp.where(mine, qq, zq), kk, preferred_element_type=F32)
                p = jnp.where(valid, jnp.exp(s - lse_h), 0.0)
                dp = jnp.einsum("pqd,pkd->pqk", jnp.where(mine, dob, zq), vv, preferred_element_type=F32)
                ds = (p * (dp - delta)).astype(BF16)
                dqs.append(jnp.einsum("pqk,pkd->pqd", ds, kk, preferred_element_type=F32))
                dks.append(jnp.einsum("pqk,pqd->pkd", ds, qq, preferred_element_type=F32))
                dvs.append(jnp.einsum("pqk,pqd->pkd", p.astype(BF16), dob, preferred_element_type=F32))
            dq = jnp.where(first[None], dqs[0], dqs[1])
            dk = jnp.where(first_k[None], dks[0], dks[1])
            dv = jnp.where(first_k[None], dvs[0], dvs[1])
            for j, (_, rows, prev) in enumerate(picked):
                _store_tile(dq_acc, rows, dil, dq[j])
                _store_tile(dk_acc, rows, dil, dk[j, nk - BLOCK:], accumulate=True)
                _store_tile(dv_acc, rows, dil, dv[j, nk - BLOCK:], accumulate=True)
                if two_blocks:
                    _store_tile(dk_acc, prev, dil, dk[j, :BLOCK], accumulate=True)
                    _store_tile(dv_acc, prev, dil, dv[j, :BLOCK], accumulate=True)
            return carry

        lax.fori_loop(0, UNITS // UNIT_BATCH, units, 0)

        def finish(t, carry):
            rows = pl.ds(pl.multiple_of(t * ROPE_ROWS, ROPE_ROWS), ROPE_ROWS)
            cf, ss = cf_ref[rows, :], ss_ref[rows, :]
            dq = dq_acc[rows, :] * (1.0 / math.sqrt(HEAD_DIM))
            dq_out[rows, :] = _rope_transposed(dq, cf, ss).astype(BF16)
            dk_out[rows, :] = _rope_transposed(dk_acc[rows, :], cf, ss).astype(BF16)
            dv_out[rows, :] = dv_acc[rows, :].astype(BF16)
            return carry

        lax.fori_loop(0, SEQ // ROPE_ROWS, finish, 0)

    def body(*refs):
        for g, dil in enumerate(DILATIONS):
            @pl.when(pl.program_id(0) == g)
            def _(dil=dil):
                group(dil, *refs)

    pairs = ATTN_WIDTH // PAIR
    whole = pl.BlockSpec((SEQ, PAIR), lambda g, hp: (0, 0))
    pair = pl.BlockSpec((SEQ, PAIR), lambda g, hp: (0, hp))
    grouped = pl.BlockSpec((SEQ, PAIR), lambda g, hp: (0, g * pairs + hp))
    views = [pl.BlockSpec((SEQ, PAIR), lambda g, hp, c0=col0 // PAIR: (0, c0 + g * pairs + hp))
             for col0 in (0, QKV_WIDTH, 2 * QKV_WIDTH)]
    return pl.pallas_call(
        body, grid=(len(DILATIONS), pairs), in_specs=views + [whole, whole, pair, pair, pair],
        out_specs=[grouped, grouped, grouped], out_shape=[_sds((SEQ, QKV_WIDTH), BF16)] * 3,
        scratch_shapes=[pltpu.VMEM((SEQ, PAIR), F32)] * 3,
        compiler_params=_cp(("parallel", "parallel")), name="attention_bwd")(
            proj, proj, proj, cos_f, sin_s, d_attn, attn, lse)


def _cmul(ar, ai, br, bi):
    return ar * br - ai * bi, ar * bi + ai * br


def _pow256(ar, ai):
    for _ in range(8):
        ar, ai = _cmul(ar, ai, ar, ai)
    return ar, ai


def _chunk_carries(first_r, first_i, pr, pi, reverse):
    rows = lax.broadcasted_iota(jnp.int32, first_r.shape, 0)
    out_r = jnp.zeros_like(first_r)
    out_i = jnp.zeros_like(first_i)
    hr = jnp.zeros_like(first_r[0:1])
    hi = jnp.zeros_like(hr)
    order = range(SCAN_CHUNKS - 1, -1, -1) if reverse else range(SCAN_CHUNKS)
    for c in order:
        out_r = jnp.where(rows == c, hr, out_r)
        out_i = jnp.where(rows == c, hi, out_i)
        tr, ti = _cmul(pr[0:1], pi[0:1], hr, hi)
        hr = first_r[c:c + 1] + tr
        hi = first_i[c:c + 1] + ti
    return out_r, out_i


def _tile(j):
    return pl.ds(pl.multiple_of(j * SCAN_CHUNKS, SCAN_CHUNKS), SCAN_CHUNKS)


def _to_scan_rows(t):
    per = SCAN_STEPS // PHASES
    return t.reshape(PHASES, SCAN_CHUNKS, per, t.shape[1]).transpose(2, 0, 1, 3).reshape(t.shape)


def _from_scan_rows(t):
    per = SCAN_STEPS // PHASES
    return t.reshape(per, PHASES, SCAN_CHUNKS, t.shape[1]).transpose(1, 2, 0, 3).reshape(t.shape)


def _scan_in_place(hr_ref, hi_ref, a_r, a_i):
    def local(j, carry):
        tr, ti = _cmul(a_r, a_i, carry[0], carry[1])
        nr = tr + hr_ref[_tile(j), :]
        ni = ti + hi_ref[_tile(j), :]
        hr_ref[_tile(j), :] = nr
        hi_ref[_tile(j), :] = ni
        return nr, ni

    zero = jnp.zeros_like(a_r)
    last_r, last_i = lax.fori_loop(0, SCAN_STEPS, local, (zero, zero), unroll=4)
    pr, pi = _pow256(a_r, a_i)
    er, ei = _chunk_carries(last_r, last_i, pr, pi, reverse=False)

    def fix(j, carry):
        tr, ti = _cmul(carry[0], carry[1], er, ei)
        hr_ref[_tile(j), :] += tr
        hi_ref[_tile(j), :] += ti
        return _cmul(carry[0], carry[1], a_r, a_i)

    lax.fori_loop(0, SCAN_STEPS, fix, (a_r, a_i), unroll=4)
    return er, ei


def _reverse_scan_in_place(lr_ref, li_ref, hr_ref, hi_ref, er, ei, a_r, a_i):
    def local(t, carry):
        j = SCAN_STEPS - 1 - t
        tr, ti = _cmul(a_r, a_i, carry[0], carry[1])
        nr = tr + lr_ref[_tile(j), :]
        ni = ti + li_ref[_tile(j), :]
        lr_ref[_tile(j), :] = nr
        li_ref[_tile(j), :] = ni
        return nr, ni

    zero = jnp.zeros_like(a_r)
    first_r, first_i = lax.fori_loop(0, SCAN_STEPS, local, (zero, zero), unroll=4)
    pr, pi = _pow256(a_r, a_i)
    nxt_r, nxt_i = _chunk_carries(first_r, first_i, pr, pi, reverse=True)

    def accumulate(lam_r, lam_i, hp_r, hp_i, acc):
        return (acc[0] + lam_r * hp_r + lam_i * hp_i, acc[1] + lam_i * hp_r - lam_r * hp_i)

    def fix(t, carry):
        qr, qi, acc_r, acc_i = carry
        j = SCAN_STEPS - 1 - t
        tr, ti = _cmul(qr, qi, nxt_r, nxt_i)
        lam_r = lr_ref[_tile(j), :] + tr
        lam_i = li_ref[_tile(j), :] + ti
        lr_ref[_tile(j), :] = lam_r
        li_ref[_tile(j), :] = lam_i
        acc_r, acc_i = accumulate(lam_r, lam_i, hr_ref[_tile(j - 1), :], hi_ref[_tile(j - 1), :], (acc_r, acc_i))
        qr, qi = _cmul(qr, qi, a_r, a_i)
        return qr, qi, acc_r, acc_i

    qr, qi, acc_r, acc_i = lax.fori_loop(0, SCAN_STEPS - 1, fix, (a_r, a_i, zero, zero), unroll=4)
    tr, ti = _cmul(qr, qi, nxt_r, nxt_i)
    lam_r = lr_ref[_tile(0), :] + tr
    lam_i = li_ref[_tile(0), :] + ti
    lr_ref[_tile(0), :] = lam_r
    li_ref[_tile(0), :] = lam_i
    acc_r, acc_i = accumulate(lam_r, lam_i, er, ei, (acc_r, acc_i))
    return jnp.sum(acc_r, axis=0, keepdims=True), jnp.sum(acc_i, axis=0, keepdims=True)


def _rope_tables():
    half = HEAD_DIM // 2
    inv_freq = ROPE_THETA ** (-jnp.arange(half, dtype=F32) / half)
    ang = jnp.arange(SEQ, dtype=F32)[:, None] * inv_freq[None, :]
    cos, sin = jnp.cos(ang), jnp.sin(ang)
    cos_f = jnp.concatenate([cos, cos, cos, cos], axis=1)
    sin_s = jnp.concatenate([-sin, sin, -sin, sin], axis=1)
    return cos_f, sin_s


def _ssm_discretise(a_re, a_im, log_dt, b_re, b_im):
    lam = lax.complex(a_re, a_im)
    dt = jnp.exp(log_dt)[:, None]
    a_bar = jnp.exp(lam * dt)
    b_bar = ((a_bar - 1.0) / lam)[..., None] * lax.complex(b_re, b_im)
    return a_bar.real, a_bar.imag, b_bar.real, b_bar.imag


SSM_SLABS = 4
SLAB_GROUPS = SSM_GROUPS // SSM_SLABS
SLAB_IN = SSM_WIDTH // SSM_SLABS
SLAB_STATE = SSM_LANES // SSM_SLABS


def _slab_block_diag(blocks):
    _, r, c = blocks.shape
    eye = jnp.eye(SLAB_GROUPS, dtype=blocks.dtype)
    b5 = blocks.reshape(SSM_SLABS, SLAB_GROUPS, r, 1, c) * eye[None, :, None, :, None]
    return b5.reshape(SSM_SLABS, SLAB_GROUPS * r, SLAB_GROUPS * c)


def _diag_blocks(a, b):
    ra, cb = a.shape[1], b.shape[1]
    wa, wb = ra // SLAB_GROUPS, cb // SLAB_GROUPS
    d = lax.dot_general(a, b, (((0,), (0,)), ((), ())), preferred_element_type=F32)
    row_g = jnp.right_shift(lax.broadcasted_iota(jnp.int32, (ra, cb), 0), wa.bit_length() - 1)
    col_g = jnp.right_shift(lax.broadcasted_iota(jnp.int32, (ra, cb), 1), wb.bit_length() - 1)
    d = jnp.where(row_g == col_g, d, 0.0)
    fold = (jnp.bitwise_and(lax.broadcasted_iota(jnp.int32, (cb, wb), 0), wb - 1)
            == lax.broadcasted_iota(jnp.int32, (cb, wb), 1)).astype(F32)
    return jnp.dot(d, fold, preferred_element_type=F32, precision=lax.Precision.HIGHEST)


def _slab_specs():
    tok = pl.BlockSpec((SEQ, SLAB_IN), lambda j: (0, j))
    state = pl.BlockSpec((SEQ, SLAB_STATE), lambda j: (0, j))
    b_in = pl.BlockSpec((None, SLAB_IN, SLAB_STATE), lambda j: (j, 0, 0))
    c_out = pl.BlockSpec((None, SLAB_STATE, SLAB_IN), lambda j: (j, 0, 0))
    vec = pl.BlockSpec((1, SLAB_STATE), lambda j: (0, j))
    ent = pl.BlockSpec((SCAN_CHUNKS, SLAB_STATE), lambda j: (0, j))
    return tok, state, b_in, c_out, vec, ent


def _ssm_forward(u, b_in_r, b_in_i, c_out_r, c_out_ni, a_r, a_i):
    def body(u_ref, br_ref, bi_ref, cr_ref, ci_ref, ar_ref, ai_ref, y_ref, hr_ref, hi_ref, er_ref, ei_ref):
        uu = u_ref[...]
        hr_ref[...] = jnp.dot(uu, br_ref[...], preferred_element_type=F32)
        hi_ref[...] = jnp.dot(uu, bi_ref[...], preferred_element_type=F32)
        a_re = jnp.broadcast_to(ar_ref[...], (SCAN_CHUNKS, SLAB_STATE))
        a_im = jnp.broadcast_to(ai_ref[...], (SCAN_CHUNKS, SLAB_STATE))
        er_ref[...], ei_ref[...] = _scan_in_place(hr_ref, hi_ref, a_re, a_im)
        y_ref[...] = (jnp.dot(hr_ref[...].astype(BF16), cr_ref[...], preferred_element_type=F32)
                      + jnp.dot(hi_ref[...].astype(BF16), ci_ref[...], preferred_element_type=F32))

    tok, state, b_in, c_out, vec, ent = _slab_specs()
    return pl.pallas_call(
        body, grid=(SSM_SLABS,), in_specs=[tok, b_in, b_in, c_out, c_out, vec, vec],
        out_specs=[tok, state, state, ent, ent],
        out_shape=[_sds((SEQ, SSM_WIDTH)), _sds((SEQ, SSM_LANES)), _sds((SEQ, SSM_LANES)),
                   _sds((SCAN_CHUNKS, SSM_LANES)), _sds((SCAN_CHUNKS, SSM_LANES))],
        compiler_params=_cp(("parallel",)), name="ssm_forward")(u, b_in_r, b_in_i, c_out_r, c_out_ni, a_r, a_i)


def _ssm_backward(d_y, d_u_skip, u, h_r, h_i, e_r, e_i, b_in_r, b_in_i, c_out_r, c_out_ni, a_r, a_i):
    def body(dy_ref, skip_ref, u_ref, hr_ref, hi_ref, er_ref, ei_ref, br_ref, bi_ref, cr_ref, ci_ref, ar_ref, ai_ref,
             du_ref, dar_ref, dai_ref, dcr_ref, dci_ref, dbr_ref, dbi_ref, lr_ref, li_ref):
        dy = dy_ref[...]
        lr_ref[...] = _dot_nt(dy, cr_ref[...])
        li_ref[...] = _dot_nt(dy, ci_ref[...])
        a_re = jnp.broadcast_to(ar_ref[...], (SCAN_CHUNKS, SLAB_STATE))
        a_im = -jnp.broadcast_to(ai_ref[...], (SCAN_CHUNKS, SLAB_STATE))
        dar_ref[...], dai_ref[...] = _reverse_scan_in_place(lr_ref, li_ref, hr_ref, hi_ref, er_ref[...], ei_ref[...],
                                                            a_re, a_im)
        dcr_ref[...] = _diag_blocks(dy, hr_ref[...].astype(BF16))
        dci_ref[...] = _diag_blocks(dy, hi_ref[...].astype(BF16))
        lam_r, lam_i = lr_ref[...].astype(BF16), li_ref[...].astype(BF16)
        uu = u_ref[...]
        dbr_ref[...] = _diag_blocks(uu, lam_r)
        dbi_ref[...] = _diag_blocks(uu, lam_i)
        du = skip_ref[...] + _dot_nt(lam_r, br_ref[...]) + _dot_nt(lam_i, bi_ref[...])
        du_ref[...] = du.astype(BF16)

    tok, state, b_in, c_out, vec, ent = _slab_specs()
    db = pl.BlockSpec((SLAB_IN, SSM_STATE), lambda j: (j, 0))
    return pl.pallas_call(
        body, grid=(SSM_SLABS,), in_specs=[tok, tok, tok, state, state, ent, ent, b_in, b_in, c_out, c_out, vec, vec],
        out_specs=[tok, vec, vec, db, db, db, db],
        out_shape=[_sds((SEQ, SSM_WIDTH), BF16), _sds((1, SSM_LANES)), _sds((1, SSM_LANES))]
        + [_sds((SSM_WIDTH, SSM_STATE))] * 4,
        scratch_shapes=[pltpu.VMEM((SEQ, SLAB_STATE), F32)] * 2,
        compiler_params=_cp(("parallel",)), name="ssm_backward")(
            d_y, d_u_skip, u, h_r, h_i, e_r, e_i, b_in_r, b_in_i, c_out_r, c_out_ni, a_r, a_i)


FF_ROWS = 1024
FF_SHARD = D_FF // N_CHIPS


def _dot_nt(a, b):
    return lax.dot_general(a, b, (((1,), (1,)), ((), ())), preferred_element_type=F32)


def _ffn_up(h, w_gate_t, w_up_t):
    def body(h_ref, wg_ref, wu_ref, a_ref, b_ref, act_ref):
        hb = h_ref[...].astype(BF16)
        a = _dot_nt(hb, wg_ref[...])
        b = _dot_nt(hb, wu_ref[...])
        a_ref[...] = a
        b_ref[...] = b
        act_ref[...] = (a * jax.nn.sigmoid(a) * b).astype(BF16)

    w_spec = pl.BlockSpec((None, FF_SHARD, D_MODEL), lambda i, k: (k, 0, 0))
    o_spec = pl.BlockSpec((None, FF_ROWS, FF_SHARD), lambda i, k: (k, i, 0))
    shape = (N_CHIPS, SEQ, FF_SHARD)
    return pl.pallas_call(
        body, grid=(SEQ // FF_ROWS, N_CHIPS),
        in_specs=[pl.BlockSpec((FF_ROWS, D_MODEL), lambda i, k: (i, 0)), w_spec, w_spec],
        out_specs=[o_spec, o_spec, o_spec], out_shape=[_sds(shape), _sds(shape), _sds(shape, BF16)],
        compiler_params=_cp(("parallel", "parallel")), name="ffn_up")(h, w_gate_t, w_up_t)


def _ffn_down_ln2_loss(act, w_down, h, tgt, ln_g, ln_b):
    def body(act_ref, w_ref, h_ref, tgt_ref, g_ref, b_ref, dz_ref, loss_ref, dg_ref, db_ref, acc):
        i, k = pl.program_id(0), pl.program_id(1)
        part = jnp.dot(act_ref[...], w_ref[...], preferred_element_type=F32)

        @pl.when(k == 0)
        def _():
            acc[...] = part

        @pl.when(k > 0)
        def _():
            acc[...] += part

        @pl.when(k == N_CHIPS - 1)
        def _():
            g = g_ref[...]
            xhat, rstd = _ln_stats(DN_ALPHA * h_ref[...] + acc[...])
            err = xhat * g + b_ref[...] - tgt_ref[...]
            d_out = err * (1.0 / D_MODEL)
            dz_ref[...] = _ln_bwd(d_out, xhat, rstd, g)
            loss_rows = jnp.sum(err * err, axis=-1, keepdims=True) * (0.5 / D_MODEL)
            sums = (jnp.broadcast_to(jnp.sum(loss_rows, axis=0, keepdims=True), loss_ref.shape),
                    _colsum(d_out * xhat), _colsum(d_out))
            for ref, val in zip((loss_ref, dg_ref, db_ref), sums):
                @pl.when(i == 0)
                def _(ref=ref, val=val):
                    ref[...] = val

                @pl.when(i > 0)
                def _(ref=ref, val=val):
                    ref[...] += val

    row = pl.BlockSpec((FF_ROWS, D_MODEL), lambda i, k: (i, 0))
    vec = pl.BlockSpec((1, D_MODEL), lambda i, k: (0, 0))
    return pl.pallas_call(
        body, grid=(SEQ // FF_ROWS, N_CHIPS),
        in_specs=[pl.BlockSpec((None, FF_ROWS, FF_SHARD), lambda i, k: (k, i, 0)),
                  pl.BlockSpec((None, FF_SHARD, D_MODEL), lambda i, k: (k, 0, 0)), row, row, vec, vec],
        out_specs=[row, pl.BlockSpec((1, BLOCK), lambda i, k: (0, 0)), vec, vec],
        out_shape=[_sds((SEQ, D_MODEL)), _sds((1, BLOCK)), _sds((1, D_MODEL)), _sds((1, D_MODEL))],
        scratch_shapes=[pltpu.VMEM((FF_ROWS, D_MODEL), F32)],
        compiler_params=_cp(("arbitrary", "arbitrary")), name="ffn_down_ln2_loss")(act, w_down, h, tgt, ln_g, ln_b)


def _ffn_down_bwd(dz, w_down, a, b):
    def body(dz_ref, wd_ref, a_ref, b_ref, da_ref, db_ref):
        d_act = _dot_nt(dz_ref[...].astype(BF16), wd_ref[...])
        av = a_ref[...]
        sg = jax.nn.sigmoid(av)
        da_ref[...] = (d_act * b_ref[...] * sg * (1.0 + av * (1.0 - sg))).astype(BF16)
        db_ref[...] = (d_act * av * sg).astype(BF16)

    t_spec = pl.BlockSpec((None, FF_ROWS, FF_SHARD), lambda i, k: (k, i, 0))
    shape = (N_CHIPS, SEQ, FF_SHARD)
    return pl.pallas_call(
        body, grid=(SEQ // FF_ROWS, N_CHIPS),
        in_specs=[pl.BlockSpec((FF_ROWS, D_MODEL), lambda i, k: (i, 0)),
                  pl.BlockSpec((None, FF_SHARD, D_MODEL), lambda i, k: (k, 0, 0)), t_spec, t_spec],
        out_specs=[t_spec, t_spec], out_shape=[_sds(shape, BF16), _sds(shape, BF16)],
        compiler_params=_cp(("parallel", "parallel")), name="ffn_down_bwd")(dz, w_down, a, b)


def _ffn_dh(d_a, d_b, w_gate_t, w_up_t):
    def body(da_ref, db_ref, wg_ref, wu_ref, o_ref, acc):
        k = pl.program_id(1)
        part = (jnp.dot(da_ref[...], wg_ref[...], preferred_element_type=F32)
                + jnp.dot(db_ref[...], wu_ref[...], preferred_element_type=F32))

        @pl.when(k == 0)
        def _():
            acc[...] = part

        @pl.when(k > 0)
        def _():
            acc[...] += part

        @pl.when(k == N_CHIPS - 1)
        def _():
            o_ref[...] = acc[...]

    t_spec = pl.BlockSpec((None, FF_ROWS, FF_SHARD), lambda i, k: (k, i, 0))
    w_spec = pl.BlockSpec((None, FF_SHARD, D_MODEL), lambda i, k: (k, 0, 0))
    return pl.pallas_call(
        body, grid=(SEQ // FF_ROWS, N_CHIPS), in_specs=[t_spec, t_spec, w_spec, w_spec],
        out_specs=pl.BlockSpec((FF_ROWS, D_MODEL), lambda i, k: (i, 0)), out_shape=_sds((SEQ, D_MODEL)),
        scratch_shapes=[pltpu.VMEM((FF_ROWS, D_MODEL), F32)],
        compiler_params=_cp(("parallel", "arbitrary")), name="ffn_dh")(d_a, d_b, w_gate_t, w_up_t)


def _local_step(x, tgt, wts, small):
    s = SEQ
    cos_f, sin_s = [_to_phase_rows(t) for t in _rope_tables()]
    x = _reorder_rows(x, to_phase=True, name="phase_rows_x")
    tgt = _reorder_rows(tgt, to_phase=True, name="phase_rows_target")

    proj = _project_in(x, wts["w_in"], cos_f, sin_s)

    attn, lse = _attention_fwd(proj)

    (abar_r, abar_i, bbar_r, bbar_i), ssm_vjp = jax.vjp(
        _ssm_discretise, small["ssm_a_re"], small["ssm_a_im"], small["ssm_log_dt"], small["ssm_b_re"], small["ssm_b_im"])
    b_in_r, b_in_i = [_slab_block_diag(b.transpose(0, 2, 1)).astype(BF16) for b in (bbar_r, bbar_i)]
    c_out_r = _slab_block_diag(small["ssm_c_re"].transpose(0, 2, 1)).astype(BF16)
    c_out_ni = _slab_block_diag(-small["ssm_c_im"].transpose(0, 2, 1)).astype(BF16)
    a_r, a_i = abar_r.reshape(1, SSM_LANES), abar_i.reshape(1, SSM_LANES)
    d_skip = small["ssm_d"].reshape(1, SSM_WIDTH)

    u_f = _to_scan_rows(proj[:, 3 * QKV_WIDTH:3 * QKV_WIDTH + SSM_WIDTH])
    u_p = u_f.astype(BF16)
    y_c, h_r, h_i, e_r, e_i = _ssm_forward(u_p, b_in_r, b_in_i, c_out_r, c_out_ni, a_r, a_i)

    def branch(t, wg):
        return jnp.concatenate([jnp.dot(t, wg[k], preferred_element_type=F32) for k in range(N_CHIPS)], axis=1)

    def branch_t(t, wg):
        ns = wg.shape[2]
        return sum(_dot_nt(t[:, k * ns:(k + 1) * ns], wg[k]) for k in range(N_CHIPS))

    def gelu_glu(yc, u, dsk, wg):
        y = yc + dsk * u
        gel = (0.5 * y * (1.0 + jnp.tanh(GELU_C * (y + GELU_K * y * y * y)))).astype(BF16)
        glu = branch(gel, wg)
        return y, gel, glu, glu[:, :SSM_WIDTH] * jax.nn.sigmoid(glu[:, SSM_WIDTH:])

    y_s5, gel, glu, y_glu = _rowwise(
        gelu_glu, [y_c, u_f], [d_skip, wts["w_glu"]],
        [_sds((s, SSM_WIDTH)), _sds((s, SSM_WIDTH), BF16), _sds((s, 2 * SSM_WIDTH)), _sds((s, SSM_WIDTH), BF16)],
        tm=512, name="ssm_gelu_glu")
    y_glu = _from_scan_rows(y_glu)

    gl0 = (proj, D_MODEL, (3 * QKV_WIDTH + SSM_WIDTH) // D_MODEL)
    gl1 = (proj, D_MODEL, (3 * QKV_WIDTH + SSM_WIDTH) // D_MODEL + 1)
    b_gate = small["b_gate"]
    w_out = wts["w_out"].reshape(D_MODEL, D_MODEL)

    def mix_ln1(l0, l1, at, yg, xv, bg, wa, ws, wo, g, b):
        ya = branch(at.astype(BF16), wa)
        ys = branch(yg, ws)
        mixed = (jax.nn.sigmoid(l0 + bg[0:1]) * ya + jax.nn.sigmoid(l1 + bg[1:2]) * ys).astype(BF16)
        z = DN_ALPHA * xv + jnp.dot(mixed, wo, preferred_element_type=F32)
        xhat, _ = _ln_stats(z)
        return ya, ys, mixed, z, xhat * g + b

    y_attn, y_ssm, mixed, z1, h = _rowwise(
        mix_ln1, [gl0, gl1, attn, y_glu, x],
        [b_gate, wts["w_attn_br"], wts["w_ssm_br"], w_out, small["ln1_g"], small["ln1_b"]],
        [_sds((s, D_MODEL)), _sds((s, D_MODEL)), _sds((s, D_MODEL), BF16), _sds((s, D_MODEL)), _sds((s, D_MODEL))],
        tm=256, name="mix_ln1")

    nf = D_FF // N_CHIPS
    w_gate_t, w_up_t, w_down = wts["w_ff_gate"], wts["w_ff_up"], wts["w_ff_down"]
    ff_a, ff_b, act = _ffn_up(h, w_gate_t, w_up_t)
    dz2, loss_v, d_ln2_g, d_ln2_b = _ffn_down_ln2_loss(act, w_down, h, tgt, small["ln2_g"], small["ln2_b"])

    d_a, d_b = _ffn_down_bwd(dz2, w_down, ff_a, ff_b)

    def grad_rows(lhs, rhs, name):
        return _matmul(lhs, rhs, grid=(N_CHIPS,), a_spec=pl.BlockSpec((None, s, nf), lambda k: (k, 0, 0)),
                       b_spec=pl.BlockSpec((s, D_MODEL), lambda k: (0, 0)),
                       o_spec=pl.BlockSpec((None, nf, D_MODEL), lambda k: (k, 0, 0)),
                       out_shape=_sds((N_CHIPS, nf, D_MODEL), BF16), dims=(0, 0), name=name)

    g_w_ff_down = grad_rows(act, dz2, "g_w_ff_down")
    g_w_ff_gate = grad_rows(d_a, h, "g_w_ff_gate")
    g_w_ff_up = grad_rows(d_b, h, "g_w_ff_up")
    dh_ff = _ffn_dh(d_a, d_b, w_gate_t, w_up_t)

    def ln1_gate_bwd(dz, dff, z, l0, l1, ya, ys, g, bg, wo, wa, ws):
        xhat, rstd = _ln_stats(z)
        dh = DN_ALPHA * dz + dff
        dz_in = _ln_bwd(dh, xhat, rstd, g)
        dm = _dot_nt(dz_in.astype(BF16), wo)
        g0 = jax.nn.sigmoid(l0 + bg[0:1])
        g1 = jax.nn.sigmoid(l1 + bg[1:2])
        dl0 = dm * ya * g0 * (1.0 - g0)
        dl1 = dm * ys * g1 * (1.0 - g1)
        dya, dys = (dm * g0).astype(BF16), (dm * g1).astype(BF16)
        return (dz_in, dya, dys, jnp.concatenate([dl0, dl1], axis=1), branch_t(dya, wa), branch_t(dys, ws),
                _colsum(dh * xhat), _colsum(dh), _colsum(dl0), _colsum(dl1))

    dz1, d_y_attn, d_y_ssm, d_gl, d_attn, d_y_glu, d_ln1_g, d_ln1_b, d_bg0, d_bg1 = _rowwise(
        ln1_gate_bwd, [dz2, dh_ff, z1, gl0, gl1, y_attn, y_ssm],
        [small["ln1_g"], b_gate, w_out, wts["w_attn_br"], wts["w_ssm_br"]],
        [_sds((s, D_MODEL)), _sds((s, D_MODEL), BF16), _sds((s, D_MODEL), BF16), _sds((s, 2 * D_MODEL), BF16),
         _sds((s, ATTN_WIDTH)), _sds((s, SSM_WIDTH))],
        [_sds((1, D_MODEL))] * 4, tm=256, name="ln1_gate_bwd")
    g_w_out = _mm_rows_tn(mixed, dz1, name="g_w_out")

    g_w_ssm_br = _mm_cols_tn(y_glu, d_y_ssm, ns=D_MODEL // N_CHIPS, name="g_w_ssm_br")
    d_y_glu = _to_scan_rows(d_y_glu)

    def glu_gelu_bwd(dyg, gl, y, u, dsk, wg):
        ga, gb = gl[:, :SSM_WIDTH], gl[:, SSM_WIDTH:]
        sg = jax.nn.sigmoid(gb)
        d_gl = jnp.concatenate([dyg * sg, dyg * ga * sg * (1.0 - sg)], axis=1).astype(BF16)
        dg = branch_t(d_gl, wg)
        th = jnp.tanh(GELU_C * (y + GELU_K * y * y * y))
        dy = dg * (0.5 * (1.0 + th) + 0.5 * y * (1.0 - th * th) * GELU_C * (1.0 + 3.0 * GELU_K * y * y))
        return d_gl, dy, dy * dsk, _colsum(dy * u)

    d_glu, d_y, d_u_skip, d_ssm_d = _rowwise(
        glu_gelu_bwd, [d_y_glu, glu, y_s5, u_f], [d_skip, wts["w_glu"]],
        [_sds((s, 2 * SSM_WIDTH), BF16), _sds((s, SSM_WIDTH), BF16), _sds((s, SSM_WIDTH))], [_sds((1, SSM_WIDTH))],
        tm=512, name="glu_gelu_bwd")
    g_w_glu = _mm_cols_tn(gel, d_glu, ns=2 * SSM_WIDTH // N_CHIPS, name="g_w_glu")
    d_u, d_abar_r, d_abar_i, d_c_r, d_c_ni, d_bin_r, d_bin_i = _ssm_backward(
        d_y, d_u_skip, u_p, h_r, h_i, e_r, e_i, b_in_r, b_in_i, c_out_r, c_out_ni, a_r, a_i)
    d_u = _from_scan_rows(d_u)
    d_bbar_r = d_bin_r.reshape(SSM_GROUPS, SSM_GROUP, SSM_STATE).transpose(0, 2, 1)
    d_bbar_i = d_bin_i.reshape(SSM_GROUPS, SSM_GROUP, SSM_STATE).transpose(0, 2, 1)
    d_a_re, d_a_im, d_log_dt, d_b_re, d_b_im = ssm_vjp(
        (d_abar_r.reshape(SSM_GROUPS, SSM_STATE), d_abar_i.reshape(SSM_GROUPS, SSM_STATE), d_bbar_r, d_bbar_i))
    d_c_re = d_c_r.reshape(SSM_GROUPS, SSM_GROUP, SSM_STATE)
    d_c_im = -d_c_ni.reshape(SSM_GROUPS, SSM_GROUP, SSM_STATE)

    g_w_attn_br = _mm_cols_tn(attn, d_y_attn, ns=D_MODEL // N_CHIPS, name="g_w_attn_br")
    d_q, d_k, d_v = _attention_bwd(proj, cos_f, sin_s, d_attn, attn, lse)

    d_proj = jnp.concatenate([d_q, d_k, d_v, d_u, d_gl], axis=1)
    g_w_in = _mm_cols_tn(x, d_proj, ns=IN_WIDTH // N_CHIPS, name="g_w_in")

    def grad_x_after(after):
        dx_proj = _mm_cols_nt(d_proj, wts["w_in"], tm=1024, name="dx_proj", after=after)
        return _reorder_rows(dz1, dx_proj, to_phase=False, name="grad_x", scale=DN_ALPHA)

    big = {"w_in": g_w_in, "w_attn_br": g_w_attn_br, "w_ssm_br": g_w_ssm_br, "w_out": g_w_out, "w_glu": g_w_glu,
           "w_ff_gate": g_w_ff_gate, "w_ff_up": g_w_ff_up, "w_ff_down": g_w_ff_down}
    small_g = {"b_gate": jnp.concatenate([d_bg0, d_bg1], axis=0), "ssm_a_re": d_a_re, "ssm_a_im": d_a_im,
               "ssm_log_dt": d_log_dt, "ssm_b_re": d_b_re, "ssm_b_im": d_b_im, "ssm_c_re": d_c_re, "ssm_c_im": d_c_im,
               "ssm_d": d_ssm_d.reshape(SSM_WIDTH), "ln1_g": d_ln1_g, "ln1_b": d_ln1_b, "ln2_g": d_ln2_g,
               "ln2_b": d_ln2_b}
    marks = {"ln1_bwd": dz1, "scan_bwd": d_abar_r, "attention_bwd": d_q}
    return loss_v[0, 0], grad_x_after, big, small_g, marks


GATHER_ID, SWAP_ID, SCATTER_ID, JOIN_ID, EXCHANGE_ID = 1, 2, 3, 4, 5


def _place():
    return lax.axis_index("x"), lax.axis_index("y"), lax.axis_index("c")


def _other_chips(x, y):
    return [(1 - x, y), (x, 1 - y), (1 - x, 1 - y)]


def _handshake(peers):
    barrier = pltpu.get_barrier_semaphore()
    for peer in peers:
        pl.semaphore_signal(barrier, inc=1, device_id=peer, device_id_type=MESH)
    pl.semaphore_wait(barrier, len(peers))


def _sequencer(body, arrays, out_type, sems, collective_id, name):
    return pl.kernel(body, name=name, out_type=out_type,
                     mesh=plsc.ScalarSubcoreMesh(axis_name="sequencer", num_cores=1), scratch_types=sems,
                     compiler_params=pltpu.CompilerParams(collective_id=collective_id))(*arrays)


def _gather_weights(shards, *, name):
    nw = len(shards)

    def body(*refs):
        ins, outs = refs[:nw], refs[nw:2 * nw]
        send_sems, recv_sems, pass_send, pass_recv, local_sems = refs[2 * nw:]
        x, y, c = _place()
        chip = 2 * x + y
        chips = _other_chips(x, y)
        _handshake([(x, y, 1 - c)] + [(cx, cy, c) for cx, cy in chips])
        started = []
        for w in range(nw):
            hw = shards[w].shape[0] // 2
            mine = pl.ds(c * hw, hw)
            own = pltpu.make_async_copy(ins[w], outs[w].at[chip], local_sems.at[w])
            own.start()
            started.append(own)
            for j, (cx, cy) in enumerate(chips):
                cp = pltpu.make_async_remote_copy(
                    src_ref=ins[w].at[mine], dst_ref=outs[w].at[chip, mine], send_sem=send_sems.at[w, j],
                    recv_sem=recv_sems.at[w, j], device_id=(cx, cy, c), device_id_type=MESH)
                cp.start()
                started.append(cp)
        passed = []
        for w in range(nw):
            hw = shards[w].shape[0] // 2
            mine = pl.ds(c * hw, hw)
            for j, (cx, cy) in enumerate(chips):
                landed = outs[w].at[2 * cx + cy, mine]
                pltpu.make_async_remote_copy(
                    src_ref=ins[w].at[mine], dst_ref=landed, send_sem=send_sems.at[w, j],
                    recv_sem=recv_sems.at[w, j], device_id=(cx, cy, c), device_id_type=MESH).wait_recv()
                cp = pltpu.make_async_remote_copy(
                    src_ref=landed, dst_ref=landed, send_sem=pass_send.at[w, j], recv_sem=pass_recv.at[w, j],
                    device_id=(x, y, 1 - c), device_id_type=MESH)
                cp.start()
                passed.append(cp)
        for w in range(nw):
            hw = shards[w].shape[0] // 2
            theirs = pl.ds((1 - c) * hw, hw)
            for j, (cx, cy) in enumerate(chips):
                landed = outs[w].at[2 * cx + cy, theirs]
                pltpu.make_async_remote_copy(
                    src_ref=landed, dst_ref=landed, send_sem=pass_send.at[w, j], recv_sem=pass_recv.at[w, j],
                    device_id=(x, y, 1 - c), device_id_type=MESH).wait_recv()
        for cp in started[0::4]:
            cp.wait()
        for cp in [s for i, s in enumerate(started) if i % 4] + passed:
            cp.wait_send()

    sem = pltpu.SemaphoreType.DMA
    return _sequencer(body, shards, [_sds((N_CHIPS,) + a.shape, a.dtype) for a in shards],
                      [sem((nw, 3)), sem((nw, 3)), sem((nw, 3)), sem((nw, 3)), sem((nw,))], GATHER_ID, name)


def _swap_other_halves(grads, *, name):
    nw = len(grads)

    def body(*refs):
        ins, outs = refs[:nw], refs[nw:2 * nw]
        send_sems, recv_sems = refs[2 * nw:]
        x, y, c = _place()
        _handshake([(x, y, 1 - c)])
        cps = []
        for w in range(nw):
            hw = grads[w].shape[1] // 2
            cp = pltpu.make_async_remote_copy(
                src_ref=ins[w].at[:, pl.ds((1 - c) * hw, hw)], dst_ref=outs[w], send_sem=send_sems.at[w],
                recv_sem=recv_sems.at[w], device_id=(x, y, 1 - c), device_id_type=MESH)
            cp.start()
            cps.append(cp)
        for cp in cps:
            cp.wait()

    sem = pltpu.SemaphoreType.DMA
    return _sequencer(body, grads, [_sds((N_CHIPS, g.shape[1] // 2, g.shape[2]), g.dtype) for g in grads],
                      [sem((nw,)), sem((nw,))], SWAP_ID, name)


def _add_my_halves(core, grads, others, *, name, after=()):
    nw = len(grads)
    halves = [g.shape[1] // 2 for g in grads]

    def body(core_ref, *refs):
        outs = refs[2 * nw + len(after):]
        for g_ref, o_ref, out_ref in zip(refs[:nw], refs[nw:2 * nw], outs):
            out_ref[...] = (g_ref[...].astype(F32) + o_ref[...].astype(F32)).astype(out_ref.dtype)

    in_specs = [pl.BlockSpec((None, None, hw, g.shape[2]), lambda s, core_ref: (s, core_ref[0], 0, 0))
                for g, hw in zip(grads, halves)]
    in_specs += [pl.BlockSpec((None, hw, g.shape[2]), lambda s, core_ref: (s, 0, 0)) for g, hw in zip(grads, halves)]
    return pl.pallas_call(
        body,
        grid_spec=pltpu.PrefetchScalarGridSpec(
            num_scalar_prefetch=1, grid=(N_CHIPS,), in_specs=in_specs + [HBM_OPERAND] * len(after),
            out_specs=[pl.BlockSpec((None, hw, g.shape[2]), lambda s, core_ref: (s, 0, 0))
                       for g, hw in zip(grads, halves)]),
        out_shape=[_sds((N_CHIPS, hw, g.shape[2]), BF16) for g, hw in zip(grads, halves)],
        compiler_params=_cp(("parallel",)), name=name)(
            core, *[g.reshape(N_CHIPS, 2, hw, g.shape[2]) for g, hw in zip(grads, halves)], *others, *after)


def _scatter_partials(parts, *, name):
    nw = len(parts)

    def body(*refs):
        ins, outs = refs[:nw], refs[nw:2 * nw]
        send_sems, recv_sems = refs[2 * nw:]
        x, y, c = _place()
        _handshake([(cx, cy, c) for cx, cy in _other_chips(x, y)])
        cps = []
        for w in range(nw):
            for j, (cx, cy) in enumerate(_other_chips(x, y)):
                cp = pltpu.make_async_remote_copy(
                    src_ref=ins[w].at[2 * cx + cy], dst_ref=outs[w].at[j], send_sem=send_sems.at[w, j],
                    recv_sem=recv_sems.at[w, j], device_id=(cx, cy, c), device_id_type=MESH)
                cp.start()
                cps.append(cp)
        for cp in cps:
            cp.wait()

    sem = pltpu.SemaphoreType.DMA
    return _sequencer(body, parts, [_sds((3,) + p.shape[1:], p.dtype) for p in parts],
                      [sem((nw, 3)), sem((nw, 3))], SCATTER_ID, name)


SUM_STEPS = 2


def _sum_partials(chip, parts, recvd, *, name, after=()):
    nw = len(parts)
    rows = [p.shape[1] // SUM_STEPS for p in parts]

    def body(chip_ref, *refs):
        outs = refs[2 * nw + len(after):]
        for p_ref, r_ref, out_ref in zip(refs[:nw], refs[nw:2 * nw], outs):
            acc = p_ref[...].astype(F32)
            for j in range(3):
                acc = acc + r_ref[j].astype(F32)
            out_ref[...] = acc

    in_specs = [pl.BlockSpec((None, th, p.shape[2]), lambda i, chip_ref: (chip_ref[0], i, 0))
                for p, th in zip(parts, rows)]
    in_specs += [pl.BlockSpec((3, th, p.shape[2]), lambda i, chip_ref: (0, i, 0)) for p, th in zip(parts, rows)]
    return pl.pallas_call(
        body,
        grid_spec=pltpu.PrefetchScalarGridSpec(
            num_scalar_prefetch=1, grid=(SUM_STEPS,), in_specs=in_specs + [HBM_OPERAND] * len(after),
            out_specs=[pl.BlockSpec((th, p.shape[2]), lambda i, chip_ref: (i, 0)) for p, th in zip(parts, rows)]),
        out_shape=[_sds(p.shape[1:]) for p in parts], compiler_params=_cp(("parallel",)), name=name)(
            chip, *parts, *recvd, *after)


def _swap_reduced_halves(halves, *, name):
    nw = len(halves)

    def body(*refs):
        ins, outs = refs[:nw], refs[nw:2 * nw]
        send_sems, recv_sems = refs[2 * nw:]
        x, y, c = _place()
        _handshake([(x, y, 1 - c)])
        cps = []
        for w in range(nw):
            cp = pltpu.make_async_remote_copy(
                src_ref=ins[w], dst_ref=outs[w], send_sem=send_sems.at[w], recv_sem=recv_sems.at[w],
                device_id=(x, y, 1 - c), device_id_type=MESH)
            cp.start()
            cps.append(cp)
        for cp in cps:
            cp.wait()

    sem = pltpu.SemaphoreType.DMA
    return _sequencer(body, halves, [_sds(h.shape, h.dtype) for h in halves], [sem((nw,)), sem((nw,))], JOIN_ID, name)


def _exchange_rows(vec, *, name):
    def body(v_ref, slots, send_sems, recv_sems, local_sem):
        x, y, c = _place()
        me = 4 * x + 2 * y + c
        peers = []
        for mask in range(1, N_DEV):
            peers.append((1 - x if mask & 4 else x, 1 - y if mask & 2 else y, 1 - c if mask & 1 else c))
        _handshake(peers)
        own = pltpu.make_async_copy(v_ref, slots.at[me], local_sem)
        own.start()
        cps = []
        for k, peer in enumerate(peers):
            cp = pltpu.make_async_remote_copy(
                src_ref=v_ref, dst_ref=slots.at[me], send_sem=send_sems.at[k], recv_sem=recv_sems.at[k],
                device_id=peer, device_id_type=MESH)
            cp.start()
            cps.append(cp)
        for k, (px, py, pc) in enumerate(peers):
            pltpu.make_async_remote_copy(
                src_ref=v_ref, dst_ref=slots.at[4 * px + 2 * py + pc], send_sem=send_sems.at[k],
                recv_sem=recv_sems.at[k], device_id=(px, py, pc), device_id_type=MESH).wait_recv()
        for cp in cps:
            cp.wait_send()
        own.wait()

    sem = pltpu.SemaphoreType.DMA
    return _sequencer(body, [vec], [_sds((N_DEV,) + vec.shape)], [sem((N_DEV - 1,)), sem((N_DEV - 1,)), sem(())],
                      EXCHANGE_ID, name)[0]


def _sum_slots(slots, *, name, after=()):
    def body(s_ref, *rest):
        out_ref = rest[len(after)]
        acc = s_ref[0]
        for d in range(1, N_DEV):
            acc = acc + s_ref[d]
        out_ref[...] = acc

    vmem = pl.BlockSpec(memory_space=pltpu.VMEM)
    return pl.pallas_call(
        body, in_specs=[vmem] + [HBM_OPERAND] * len(after), out_specs=vmem, out_shape=_sds(slots.shape[1:]),
        compiler_params=pltpu.CompilerParams(vmem_limit_bytes=VMEM_LIMIT_BYTES), name=name)(slots, *after)


def _reduce_scatter_start(grads, core, *, tag, add_after=()):
    others = _swap_other_halves(grads, name="swap_other_halves_" + tag)
    parts = _add_my_halves(core, grads, others, name="add_my_halves_" + tag, after=add_after)
    return parts, _scatter_partials(parts, name="scatter_partials_" + tag)


def _reduce_scatter_finish(parts, recvd, chip, *, tag, sum_after=()):
    mine = _sum_partials(chip, parts, recvd, name="sum_partials_" + tag, after=sum_after)
    return mine, _swap_reduced_halves(mine, name="swap_reduced_halves_" + tag)


ADAM_BLOCK_ELEMS = 256 * 1024


def _adam_rows(rows, cols):
    tm = rows
    while tm * cols > ADAM_BLOCK_ELEMS and tm % 16 == 0:
        tm //= 2
    return tm


def _adam_step(wv, gv, mv, vv):
    m2 = ADAM_B1 * mv + (1.0 - ADAM_B1) * gv
    v2 = ADAM_B2 * vv + (1.0 - ADAM_B2) * (gv * gv)
    m_hat = m2 / (1.0 - ADAM_B1 ** ADAM_STEP)
    v_hat = v2 / (1.0 - ADAM_B2 ** ADAM_STEP)
    return -ADAM_LR * (m_hat / (jnp.sqrt(v_hat) + ADAM_EPS) + ADAM_WD * wv), m2, v2


def _adamw_each(ws, gs, ms, vs, *, name):
    n = len(ws)

    def body(*refs):
        ins, outs = refs[:4 * n], refs[4 * n:]
        for i in range(n):
            res = _adam_step(*(ins[k * n + i][...] for k in range(4)))
            for k in range(3):
                outs[k * n + i][...] = res[k]

    out = pl.pallas_call(body, out_shape=[_sds(w.shape) for w in ws] * 3, name=name)(*ws, *gs, *ms, *vs)
    return out[:n], out[n:2 * n], out[2 * n:]


def _adamw_halves(core, w, g_mine, g_theirs, m, v, *, name, after=()):
    rows, cols = w.shape
    hw = rows // 2
    tm = _adam_rows(hw, cols)
    per_half = hw // tm

    def body(core_ref, w_ref, gm_ref, gt_ref, m_ref, v_ref, *rest):
        g_out, d_out, m_out, v_out = rest[len(after):]
        mine = (pl.program_id(0) // per_half) == core_ref[0]
        g = jnp.where(mine, gm_ref[...], gt_ref[...])
        d, m2, v2 = _adam_step(w_ref[...], g, m_ref[...], v_ref[...])
        g_out[...] = g
        d_out[...] = d
        m_out[...] = m2
        v_out[...] = v2

    full = pl.BlockSpec((tm, cols), lambda i, core_ref: (i, 0))

    def half(wanted):
        def index(i, core_ref):
            in_use = ((i // per_half) == core_ref[0]) == wanted
            return (jnp.where(in_use, i % per_half, 0), 0)
        return pl.BlockSpec((tm, cols), index)

    return pl.pallas_call(
        body,
        grid_spec=pltpu.PrefetchScalarGridSpec(
            num_scalar_prefetch=1, grid=(rows // tm,),
            in_specs=[full, half(True), half(False), full, full] + [HBM_OPERAND] * len(after),
            out_specs=[full, full, full, full]),
        out_shape=[_sds((rows, cols))] * 4, compiler_params=_cp(("parallel",)), name=name)(
            core, w, g_mine, g_theirs, m, v, *after)


HELD_TRANSPOSED = ("w_ff_gate", "w_ff_up")


def _as_rows(name, arr):
    return arr[0].T if name in HELD_TRANSPOSED else arr[0]


def _from_rows(name, arr2d):
    return (arr2d.T if name in HELD_TRANSPOSED else arr2d)[None]


STORED_SWAPPED = ("ssm_b_re", "ssm_b_im")


def _as_stored(name, arr):
    return jnp.swapaxes(arr, -1, -2) if name in STORED_SWAPPED else arr


def _pack_rows(arrs):
    flat = jnp.concatenate([a.reshape(-1).astype(F32) for a in arrs])
    rows = -(-flat.shape[0] // 1024) * 8
    return jnp.pad(flat, (0, rows * 128 - flat.shape[0])).reshape(rows, 128)


def _unpack_rows(vec, shapes):
    flat = vec.reshape(-1)
    out, off = [], 0
    for shp in shapes:
        size = math.prod(shp)
        out.append(flat[off:off + size].reshape(shp))
        off += size
    return out


SMALL = ("b_gate", "ssm_a_re", "ssm_a_im", "ssm_log_dt", "ssm_b_re", "ssm_b_im", "ssm_c_re", "ssm_c_im", "ssm_d",
         "ln1_g", "ln1_b", "ln2_g", "ln2_b")
GATHER_GROUPS = (("w_in", ("w_in",)), ("mixer", ("w_attn_br", "w_ssm_br", "w_glu", "w_out")),
                 ("ffn_up", ("w_ff_gate", "w_ff_up")), ("ffn_down", ("w_ff_down",)))
REDUCE_GROUPS = (("ffn", ("w_ff_down", "w_ff_gate", "w_ff_up")),
                 ("mixer", ("w_out", "w_ssm_br", "w_glu", "w_attn_br")), ("w_in", ("w_in",)))
WEIGHTS = ("w_in", "b_gate", "w_attn_br", "w_ssm_br", "w_out", "ssm_a_re", "ssm_a_im", "ssm_log_dt", "ssm_b_re",
           "ssm_b_im", "ssm_c_re", "ssm_c_im", "ssm_d", "w_glu", "ln1_g", "ln1_b", "w_ff_gate", "w_ff_up", "w_ff_down",
           "ln2_g", "ln2_b")


def kernel(x, w_in, b_gate, w_attn_br, w_ssm_br, w_out, ssm_a_re, ssm_a_im, ssm_log_dt, ssm_b_re, ssm_b_im, ssm_c_re, ssm_c_im, ssm_d, w_glu, ln1_g, ln1_b, w_ff_gate, w_ff_up, w_ff_down, ln2_g, ln2_b, loss_target, m_w_in, m_b_gate, m_w_attn_br, m_w_ssm_br, m_w_out, m_ssm_a_re, m_ssm_a_im, m_ssm_log_dt, m_ssm_b_re, m_ssm_b_im, m_ssm_c_re, m_ssm_c_im, m_ssm_d, m_w_glu, m_ln1_g, m_ln1_b, m_w_ff_gate, m_w_ff_up, m_w_ff_down, m_ln2_g, m_ln2_b, v_w_in, v_b_gate, v_w_attn_br, v_w_ssm_br, v_w_out, v_ssm_a_re, v_ssm_a_im, v_ssm_log_dt, v_ssm_b_re, v_ssm_b_im, v_ssm_c_re, v_ssm_c_im, v_ssm_d, v_w_glu, v_ln1_g, v_ln1_b, v_w_ff_gate, v_w_ff_up, v_w_ff_down, v_ln2_g, v_ln2_b):
    given = dict(locals())
    px, py, pc = _place()
    chip = 2 * px + py
    core_s = jnp.reshape(pc, (1,)).astype(jnp.int32)
    chip_s = jnp.reshape(chip, (1,)).astype(jnp.int32)

    wts = {}
    for tag, names in GATHER_GROUPS:
        wts.update(zip(names, _gather_weights([_as_rows(n, given[n]).astype(BF16) for n in names],
                                              name="gather_" + tag)))
    ncol = D_MODEL // N_CHIPS
    bg_mine = jnp.where(pc == 0, b_gate[0], jnp.zeros_like(b_gate[0]))
    bg_full = lax.dynamic_update_slice(jnp.zeros((2, D_MODEL), F32), bg_mine, (0, chip * ncol))
    bg_slots = _exchange_rows(bg_full.reshape(16, 128), name="exchange_gate_bias")
    bg_full = _sum_slots(bg_slots, name="sum_gate_bias").reshape(2, D_MODEL)
    small = {n: given[n][0] for n in SMALL if n.startswith("ssm")}
    small.update({n: given[n] for n in ("ln1_g", "ln1_b", "ln2_g", "ln2_b")})
    small["b_gate"] = bg_full

    loss_mine, grad_x_after, big_g, small_g, marks = _local_step(x[0], loss_target[0], wts, small)

    groups = dict(REDUCE_GROUPS)
    parts, recvd = {}, {}
    grads, delta, new_m, new_v, done = {}, {}, {}, {}, {}

    def start(tag, add_after):
        parts[tag], recvd[tag] = _reduce_scatter_start([big_g[n] for n in groups[tag]], core_s, tag=tag,
                                                       add_after=add_after)

    def finish(tag, sum_after, adam_after):
        mine, theirs = _reduce_scatter_finish(parts[tag], recvd[tag], chip_s, tag=tag, sum_after=sum_after)
        for n, g_mine, g_theirs in zip(groups[tag], mine, theirs):
            res = _adamw_halves(core_s, _as_rows(n, given[n]), g_mine, g_theirs, _as_rows(n, given["m_" + n]),
                                _as_rows(n, given["v_" + n]), name="adamw_" + n, after=adam_after)
            done[n] = res[1]
            grads[n], delta[n], new_m[n], new_v[n] = [_from_rows(n, r) for r in res]

    start("ffn", (marks["ln1_bwd"],))
    start("mixer", (marks["scan_bwd"],))
    finish("mixer", (marks["attention_bwd"],), (big_g["w_in"],))
    start("w_in", tuple(done[n] for n in groups["mixer"]))
    in_flight = (parts["w_in"][0],)
    grad_x = grad_x_after(in_flight)
    finish("ffn", (marks["scan_bwd"],), in_flight)
    stored = [_as_stored(n, small_g[n]) for n in SMALL] + [loss_mine.reshape(1)]
    slots = _exchange_rows(_pack_rows(stored), name="exchange_small")
    summed = _unpack_rows(_sum_slots(slots, name="sum_small", after=tuple(done[n] for n in groups["ffn"])),
                          [a.shape for a in stored])
    loss = summed.pop()[0]
    at = SMALL.index("b_gate")
    summed[at] = lax.dynamic_slice(summed[at], (0, chip * ncol), (2, ncol))
    summed = [g.reshape(1, -1) if g.ndim == 1 else g for g in summed]
    held = [[_as_stored(n, given[prefix + n]).reshape(g.shape) for n, g in zip(SMALL, summed)]
            for prefix in ("", "m_", "v_")]
    small_out = _adamw_each(held[0], summed, held[1], held[2], name="adamw_small")
    for out, arrs in zip((grads, delta, new_m, new_v), (summed, *small_out)):
        out.update((n, _as_stored(n, a).reshape(given[n].shape)) for n, a in zip(SMALL, arrs))
    behind = [done[n] for n in groups["ffn"]] + [small_out[0][0], grad_x]
    finish("w_in", tuple(behind), ())

    return (loss, grad_x.reshape(x.shape), *[grads[n] for n in WEIGHTS], *[delta[n] for n in WEIGHTS],
            *[new_m[n] for n in WEIGHTS], *[new_v[n] for n in WEIGHTS])
```

```python
import math

import jax
import jax.numpy as jnp
from jax import lax
from jax.experimental import pallas as pl
from jax.experimental.pallas import tpu as pltpu
from jax.experimental.pallas import tpu_sc as plsc

F32 = jnp.float32
BF16 = jnp.bfloat16
MESH = pl.DeviceIdType.MESH

D_MODEL = 1024
SEQ = 2048
HEAD_DIM = 64
ATTN_HEADS = 8
DILATIONS = (1, 4, 16)
ATTN_WIDTH = ATTN_HEADS * HEAD_DIM
QKV_WIDTH = 3 * ATTN_WIDTH
BLOCK = 128
ROPE_THETA = 10000.0
NEG_INF = -1e30
SSM_GROUP = 16
SSM_GROUPS = 32
SSM_WIDTH = 512
SSM_STATE = 64
SSM_LANES = SSM_GROUPS * SSM_STATE
SCAN_CHUNKS = 8
SCAN_STEPS = SEQ // SCAN_CHUNKS
IN_WIDTH = 3 * QKV_WIDTH + SSM_WIDTH + 2 * D_MODEL
D_FF = 2816
N_CHIPS = 4
N_DEV = 8
DN_ALPHA = 2.0 ** 0.25
LN_EPS = 1e-5
ADAM_LR = 0.001
ADAM_B1 = 0.9
ADAM_B2 = 0.999
ADAM_EPS = 1e-08
ADAM_WD = 0.01
ADAM_STEP = 10
GELU_C = math.sqrt(2.0 / math.pi)
GELU_K = 0.044715

VMEM_LIMIT_BYTES = 56 * 1024 * 1024


def _sds(shape, dtype=F32):
    return jax.ShapeDtypeStruct(tuple(shape), dtype)


def _cp(semantics=None):
    return pltpu.CompilerParams(dimension_semantics=semantics, vmem_limit_bytes=VMEM_LIMIT_BYTES)


HBM_OPERAND = pl.BlockSpec(memory_space=pl.ANY)


def _matmul(a, b, *, grid, a_spec, b_spec, o_spec, out_shape, dims, k_axis=None, name, after=()):
    nk = grid[k_axis] if k_axis is not None else 1
    o_block = tuple(d for d in o_spec.block_shape if d is not None)
    n_after = len(after)

    def body(a_ref, b_ref, *rest):
        o_ref, acc = rest[n_after], rest[n_after + 1:]
        part = lax.dot_general(a_ref[...].astype(BF16), b_ref[...].astype(BF16),
                               (((dims[0],), (dims[1],)), ((), ())), preferred_element_type=F32)
        if k_axis is None:
            o_ref[...] = part.astype(o_ref.dtype)
        else:
            k = pl.program_id(k_axis)

            @pl.when(k == 0)
            def _():
                acc[0][...] = part

            @pl.when(k > 0)
            def _():
                acc[0][...] += part

            @pl.when(k == nk - 1)
            def _():
                o_ref[...] = acc[0][...].astype(o_ref.dtype)

    sem = tuple("arbitrary" if ax == k_axis else "parallel" for ax in range(len(grid)))
    return pl.pallas_call(
        body, grid=grid, in_specs=[a_spec, b_spec] + [HBM_OPERAND] * n_after, out_specs=o_spec, out_shape=out_shape,
        scratch_shapes=[pltpu.VMEM(o_block, F32)] if k_axis is not None else [],
        compiler_params=_cp(sem), name=name)(a, b, *after)


def _mm_cols_nt(dy, wg, *, tm, name, out_dtype=F32, after=()):
    k, ns = wg.shape[1], wg.shape[2]
    m = dy.shape[0]
    a_spec = pl.BlockSpec((tm, ns), lambda i, s: (i, s))
    return _matmul(dy, wg, grid=(m // tm, N_CHIPS), a_spec=a_spec,
                   b_spec=pl.BlockSpec((None, k, ns), lambda i, s: (s, 0, 0)),
                   o_spec=pl.BlockSpec((tm, k), lambda i, s: (i, 0)),
                   out_shape=_sds((m, k), out_dtype), dims=(1, 1), k_axis=1, name=name, after=after)


def _mm_cols_tn(a, dy, *, ns, name, after=()):
    m, k = a.shape
    return _matmul(a, dy, grid=(N_CHIPS,), a_spec=pl.BlockSpec((m, k), lambda s: (0, 0)),
                   b_spec=pl.BlockSpec((m, ns), lambda s: (0, s)),
                   o_spec=pl.BlockSpec((None, k, ns), lambda s: (s, 0, 0)),
                   out_shape=_sds((N_CHIPS, k, ns), BF16), dims=(0, 0), name=name, after=after)


def _mm_rows_tn(a, dy, *, name):
    m, k = a.shape
    rows, n = k // N_CHIPS, dy.shape[1]
    return _matmul(a, dy, grid=(N_CHIPS,), a_spec=pl.BlockSpec((m, rows), lambda s: (0, s)),
                   b_spec=pl.BlockSpec((m, n), lambda s: (0, 0)),
                   o_spec=pl.BlockSpec((None, rows, n), lambda s: (s, 0, 0)),
                   out_shape=_sds((N_CHIPS, rows, n), BF16), dims=(0, 0), name=name)


def _rowwise(fn, tiled, full, outs, accs=(), *, tm, name, after=()):
    args, in_specs = [], []
    for t in tiled:
        if isinstance(t, tuple):
            arr, w, cb = t
            in_specs.append(pl.BlockSpec((tm, w), lambda i, cb=cb: (i, cb)))
        else:
            arr = t
            in_specs.append(pl.BlockSpec((tm, arr.shape[1]), lambda i: (i, 0)))
        args.append(arr)
    rows = args[0].shape[0]
    for f in full:
        in_specs.append(pl.BlockSpec(f.shape, lambda i, nd=f.ndim: (0,) * nd))
        args.append(f)
    out_specs = [pl.BlockSpec((tm, o.shape[1]), lambda i: (i, 0)) for o in outs]
    out_specs += [pl.BlockSpec(a.shape, lambda i, nd=len(a.shape): (0,) * nd) for a in accs]
    n_in, n_out = len(args), len(outs)
    in_specs += [HBM_OPERAND] * len(after)
    first_out = n_in + len(after)

    def body(*refs):
        res = fn(*[r[...] for r in refs[:n_in]])
        res = res if isinstance(res, (tuple, list)) else (res,)
        for r, v in zip(refs[first_out:first_out + n_out], res[:n_out]):
            r[...] = v.astype(r.dtype)
        i = pl.program_id(0)
        for r, v in zip(refs[first_out + n_out:], res[n_out:]):
            @pl.when(i == 0)
            def _(r=r, v=v):
                r[...] = v

            @pl.when(i > 0)
            def _(r=r, v=v):
                r[...] += v

    res = pl.pallas_call(
        body, grid=(rows // tm,), in_specs=in_specs, out_specs=out_specs, out_shape=list(outs) + list(accs),
        compiler_params=_cp(("arbitrary",) if accs else ("parallel",)), name=name)(*args, *after)
    return res


def _colsum(v):
    return jnp.sum(v, axis=0, keepdims=True)


def _ln_stats(z):
    mu = jnp.mean(z, axis=-1, keepdims=True)
    zc = z - mu
    var = jnp.mean(zc * zc, axis=-1, keepdims=True)
    rstd = lax.rsqrt(var + LN_EPS)
    return zc * rstd, rstd


def _ln_bwd(dy, xhat, rstd, g):
    dxh = dy * g
    m1 = jnp.mean(dxh, axis=-1, keepdims=True)
    m2 = jnp.mean(dxh * xhat, axis=-1, keepdims=True)
    return rstd * (dxh - m1 - xhat * m2)


def _swap_halves(t):
    w = t.shape[-1]
    lane = lax.broadcasted_iota(jnp.int32, t.shape, t.ndim - 1)
    return jnp.where((lane % HEAD_DIM) < HEAD_DIM // 2, pltpu.roll(t, w - HEAD_DIM // 2, t.ndim - 1),
                     pltpu.roll(t, HEAD_DIM // 2, t.ndim - 1))


PHASES = max(DILATIONS)
PAIR = 2 * HEAD_DIM
UNITS = SEQ // BLOCK
UNIT_BATCH = 16
ROPE_ROWS = 256


def _to_phase_rows(t):
    return t.reshape(SEQ // PHASES, PHASES, t.shape[1]).transpose(1, 0, 2).reshape(t.shape)


def _reorder_rows(arr, plus=None, *, to_phase, name, scale=1.0):
    def body(*refs):
        o_ref = refs[-1]
        for rho in range(PHASES):
            phase = pl.ds(rho * BLOCK, BLOCK)
            strided = pl.ds(rho, BLOCK, stride=PHASES)
            src, dst = (strided, phase) if to_phase else (phase, strided)
            val = refs[0][src, :]
            if scale != 1.0:
                val = val * scale
            if plus is not None:
                val = val + refs[1][src, :]
            o_ref[dst, :] = val

    spec = pl.BlockSpec((SEQ, BLOCK), lambda j: (0, j))
    ins = [arr] if plus is None else [arr, plus]
    return pl.pallas_call(body, grid=(arr.shape[1] // BLOCK,), in_specs=[spec] * len(ins), out_specs=spec,
                          out_shape=_sds(arr.shape), compiler_params=_cp(("parallel",)), name=name)(*ins)


def _rope(t, cf, ss):
    return t * cf + _swap_halves(t) * ss


def _rope_transposed(d, cf, ss):
    return d * cf + _swap_halves(d * ss)


def _unit_pieces(u, dil):
    pieces, length = PHASES // dil, 8 * dil
    if dil == 1:
        rho, i = 0, u
    elif dil == PHASES:
        rho, i = u, 0
    else:
        rho, i = jnp.bitwise_and(u, dil - 1), jnp.right_shift(u, dil.bit_length() - 1)
    before = jnp.maximum(i - 1, 0)
    cur = [pl.multiple_of((rho + dil * k) * BLOCK + length * i, 8) for k in range(pieces)]
    prev = [pl.multiple_of((rho + dil * k) * BLOCK + length * before, 8) for k in range(pieces)]
    return i, cur, prev


def _load_tile(ref, starts, dil):
    return jnp.concatenate([ref[pl.ds(st, 8 * dil), :] for st in starts], axis=0)


def _store_tile(ref, starts, dil, val, head=None, accumulate=False):
    length = 8 * dil
    lanes = slice(None) if head is None else pl.ds(head * HEAD_DIM, HEAD_DIM)
    cols = slice(None) if head is None else slice(head * HEAD_DIM, (head + 1) * HEAD_DIM)
    for k, st in enumerate(starts):
        piece = val[k * length:(k + 1) * length, cols]
        if accumulate:
            ref[pl.ds(st, length), lanes] += piece
        else:
            ref[pl.ds(st, length), lanes] = piece


def _tile_position(idx, dil):
    pieces, length = PHASES // dil, 8 * dil
    return pieces * jnp.bitwise_and(idx, length - 1) + jnp.right_shift(idx, length.bit_length() - 1)


def _band_mask(i, dil):
    row = lax.broadcasted_iota(jnp.int32, (BLOCK, 2 * BLOCK), 0)
    col = lax.broadcasted_iota(jnp.int32, (BLOCK, 2 * BLOCK), 1)
    key_pos = _tile_position(jnp.bitwise_and(col, BLOCK - 1), dil) + jnp.where(col >= BLOCK, 0, -BLOCK)
    dist = _tile_position(row, dil) - key_pos
    return (dist >= 0) & (dist <= BLOCK) & ((col >= BLOCK) | (i > 0))


def _causal_mask():
    row = lax.broadcasted_iota(jnp.int32, (BLOCK, BLOCK), 0)
    col = lax.broadcasted_iota(jnp.int32, (BLOCK, BLOCK), 1)
    return row >= col


def _pair_views(col0):
    return [pl.BlockSpec((SEQ, PAIR), lambda hp, g=g: (0, col0 // PAIR + g * (ATTN_WIDTH // PAIR) + hp))
            for g in range(len(DILATIONS))]


def _project_in(x, wg, cos_f, sin_s):
    ns = wg.shape[2]
    tiles = ns // PAIR

    def body(x_ref, w_ref, cf_ref, ss_ref, o_ref):
        shard = pl.program_id(1)
        xb = x_ref[...].astype(BF16)
        cf, ss = cf_ref[...], ss_ref[...]

        def write(rotated, scaled):
            for t0 in range(0, tiles, 2):
                strip = jnp.dot(xb, w_ref[:, t0 * PAIR:(t0 + 2) * PAIR], preferred_element_type=F32)
                for t in (t0, t0 + 1):
                    val = strip[:, (t - t0) * PAIR:(t - t0 + 1) * PAIR]
                    if t < rotated:
                        val = _rope(val, cf, ss)
                        if t < scaled:
                            val = val * (1.0 / math.sqrt(HEAD_DIM))
                    o_ref[:, t * PAIR:(t + 1) * PAIR] = val

        for s in range(N_CHIPS):
            rotated = min(max(2 * QKV_WIDTH - s * ns, 0), ns) // PAIR
            scaled = min(max(QKV_WIDTH - s * ns, 0), ns) // PAIR

            @pl.when(shard == s)
            def _(rotated=rotated, scaled=scaled):
                write(rotated, scaled)

    table = pl.BlockSpec((FF_ROWS, PAIR), lambda i, s: (i, 0))
    return pl.pallas_call(
        body, grid=(SEQ // FF_ROWS, N_CHIPS),
        in_specs=[pl.BlockSpec((FF_ROWS, D_MODEL), lambda i, s: (i, 0)),
                  pl.BlockSpec((None, D_MODEL, ns), lambda i, s: (s, 0, 0)), table, table],
        out_specs=pl.BlockSpec((FF_ROWS, ns), lambda i, s: (i, s)), out_shape=_sds((SEQ, N_CHIPS * ns)),
        compiler_params=_cp(("parallel", "parallel")), name="project_in")(x, wg, cos_f, sin_s)


def _attention_fwd(proj):
    ng = len(DILATIONS)

    def body(*refs):
        q_refs, k_refs, v_refs = refs[:ng], refs[ng:2 * ng], refs[2 * ng:3 * ng]
        attn_ref, lse_ref = refs[3 * ng:]
        qr_refs, kr_refs = q_refs, k_refs
        first = lax.broadcasted_iota(jnp.int32, (BLOCK, PAIR), 1) < HEAD_DIM
        for g, dil in enumerate(DILATIONS):
            two_blocks = SEQ // dil > BLOCK

            def units(t, carry, g=g, dil=dil, two_blocks=two_blocks):
                picked = [_unit_pieces(t * UNIT_BATCH + j, dil) for j in range(UNIT_BATCH)]

                def tiles(ref, with_prev=False):
                    if with_prev and two_blocks:
                        return jnp.stack([jnp.concatenate([_load_tile(ref, prev, dil), _load_tile(ref, rows, dil)],
                                                          axis=0) for _, rows, prev in picked])
                    return jnp.stack([_load_tile(ref, rows, dil) for _, rows, _ in picked])

                qq = tiles(qr_refs[g]).astype(BF16)
                kk = tiles(kr_refs[g], True).astype(BF16)
                vv = tiles(v_refs[g], True).astype(BF16)
                if two_blocks:
                    valid = jnp.stack([_band_mask(i, dil) for i, _, _ in picked])
                else:
                    valid = _causal_mask()[None]
                mine = first[None]
                zero = jnp.zeros_like(qq)
                outs, lses = [], []
                for qh in (jnp.where(mine, qq, zero), jnp.where(mine, zero, qq)):
                    s = jnp.einsum("pqd,pkd->pqk", qh, kk, preferred_element_type=F32)
                    s = jnp.where(valid, s, NEG_INF)
                    m = jnp.max(s, axis=-1, keepdims=True)
                    p = jnp.exp(s - m)
                    l = jnp.sum(p, axis=-1, keepdims=True)
                    outs.append(jnp.einsum("pqk,pkd->pqd", p.astype(BF16), vv, preferred_element_type=F32) * (1.0 / l))
                    lses.append(m + jnp.log(l))
                o = jnp.where(mine, outs[0], outs[1])
                lse = jnp.where(mine, lses[0], lses[1])
                if g > 0:
                    lse_old = tiles(lse_ref)
                    m = jnp.maximum(lse_old, lse)
                    lse_new = m + jnp.log(jnp.exp(lse_old - m) + jnp.exp(lse - m))
                    o = tiles(attn_ref) * jnp.exp(lse_old - lse_new) + o * jnp.exp(lse - lse_new)
                    lse = lse_new
                for j, (_, rows, _) in enumerate(picked):
                    _store_tile(attn_ref, rows, dil, o[j])
                    _store_tile(lse_ref, rows, dil, lse[j])
                return carry

            lax.fori_loop(0, UNITS // UNIT_BATCH, units, 0)

    out = pl.BlockSpec((SEQ, PAIR), lambda hp: (0, hp))
    return pl.pallas_call(
        body, grid=(ATTN_WIDTH // PAIR,),
        in_specs=_pair_views(0) + _pair_views(QKV_WIDTH) + _pair_views(2 * QKV_WIDTH),
        out_specs=[out, out], out_shape=[_sds((SEQ, ATTN_WIDTH)), _sds((SEQ, ATTN_WIDTH))],
        compiler_params=_cp(("parallel",)), name="attention_fwd")(*([proj] * (3 * ng)))


def _attention_bwd(proj, cos_f, sin_s, d_attn, attn, lse):
    def group(dil, qr_ref, kr_ref, v_ref, cf_ref, ss_ref, do_ref, o_ref, lse_ref, dq_out, dk_out, dv_out,
              dq_acc, dk_acc, dv_acc):
        two_blocks = SEQ // dil > BLOCK
        dk_acc[...] = jnp.zeros_like(dk_acc)
        dv_acc[...] = jnp.zeros_like(dv_acc)
        nk = 2 * BLOCK if two_blocks else BLOCK
        first = lax.broadcasted_iota(jnp.int32, (BLOCK, PAIR), 1) < HEAD_DIM
        first_k = lax.broadcasted_iota(jnp.int32, (nk, PAIR), 1) < HEAD_DIM

        def units(t, carry):
            picked = [_unit_pieces(t * UNIT_BATCH + j, dil) for j in range(UNIT_BATCH)]

            def tiles(ref, with_prev=False):
                if with_prev and two_blocks:
                    return jnp.stack([jnp.concatenate([_load_tile(ref, prev, dil), _load_tile(ref, rows, dil)], axis=0)
                                      for _, rows, prev in picked])
                return jnp.stack([_load_tile(ref, rows, dil) for _, rows, _ in picked])

            qq = tiles(qr_ref).astype(BF16)
            kk = tiles(kr_ref, True).astype(BF16)
            vv = tiles(v_ref, True).astype(BF16)
            dof = tiles(do_ref)
            dd = dof * tiles(o_ref)
            lse3 = tiles(lse_ref)
            dob = dof.astype(BF16)
            if two_blocks:
                valid = jnp.stack([_band_mask(i, dil) for i, _, _ in picked])
            else:
                valid = _causal_mask()[None]
            zq, zf = jnp.zeros_like(qq), jnp.zeros_like(dd)
            dqs, dks, dvs = [], [], []
            for head in range(2):
                mine = first[None] if head == 0 else jnp.logical_not(first)[None]
                delta = jnp.sum(jnp.where(mine, dd, zf), axis=-1, keepdims=True)
                lse_h = lse3[:, :, head * HEAD_DIM:head * HEAD_DIM + 1]
                s = jnp.einsum("pqd,pkd->pqk", jnp.where(mine, qq, zq), kk, preferred_element_type=F32)
                p = jnp.where(valid, jnp.exp(s - lse_h), 0.0)
                dp = jnp.einsum("pqd,pkd->pqk", jnp.where(mine, dob, zq), vv, preferred_element_type=F32)
                ds = (p * (dp - delta)).astype(BF16)
                dqs.append(jnp.einsum("pqk,pkd->pqd", ds, kk, preferred_element_type=F32))
                dks.append(jnp.einsum("pqk,pqd->pkd", ds, qq, preferred_element_type=F32))
                dvs.append(jnp.einsum("pqk,pqd->pkd", p.astype(BF16), dob, preferred_element_type=F32))
            dq = jnp.where(first[None], dqs[0], dqs[1])
            dk = jnp.where(first_k[None], dks[0], dks[1])
            dv = jnp.where(first_k[None], dvs[0], dvs[1])
            for j, (_, rows, prev) in enumerate(picked):
                _store_tile(dq_acc, rows, dil, dq[j])
                _store_tile(dk_acc, rows, dil, dk[j, nk - BLOCK:], accumulate=True)
                _store_tile(dv_acc, rows, dil, dv[j, nk - BLOCK:], accumulate=True)
                if two_blocks:
                    _store_tile(dk_acc, prev, dil, dk[j, :BLOCK], accumulate=True)
                    _store_tile(dv_acc, prev, dil, dv[j, :BLOCK], accumulate=True)
            return carry

        lax.fori_loop(0, UNITS // UNIT_BATCH, units, 0)

        def finish(t, carry):
            rows = pl.ds(pl.multiple_of(t * ROPE_ROWS, ROPE_ROWS), ROPE_ROWS)
            cf, ss = cf_ref[rows, :], ss_ref[rows, :]
            dq = dq_acc[rows, :] * (1.0 / math.sqrt(HEAD_DIM))
            dq_out[rows, :] = _rope_transposed(dq, cf, ss).astype(BF16)
            dk_out[rows, :] = _rope_transposed(dk_acc[rows, :], cf, ss).astype(BF16)
            dv_out[rows, :] = dv_acc[rows, :].astype(BF16)
            return carry

        lax.fori_loop(0, SEQ // ROPE_ROWS, finish, 0)

    def body(*refs):
        for g, dil in enumerate(DILATIONS):
            @pl.when(pl.program_id(0) == g)
            def _(dil=dil):
                group(dil, *refs)

    pairs = ATTN_WIDTH // PAIR
    whole = pl.BlockSpec((SEQ, PAIR), lambda g, hp: (0, 0))
    pair = pl.BlockSpec((SEQ, PAIR), lambda g, hp: (0, hp))
    grouped = pl.BlockSpec((SEQ, PAIR), lambda g, hp: (0, g * pairs + hp))
    views = [pl.BlockSpec((SEQ, PAIR), lambda g, hp, c0=col0 // PAIR: (0, c0 + g * pairs + hp))
             for col0 in (0, QKV_WIDTH, 2 * QKV_WIDTH)]
    return pl.pallas_call(
        body, grid=(len(DILATIONS), pairs), in_specs=views + [whole, whole, pair, pair, pair],
        out_specs=[grouped, grouped, grouped], out_shape=[_sds((SEQ, QKV_WIDTH), BF16)] * 3,
        scratch_shapes=[pltpu.VMEM((SEQ, PAIR), F32)] * 3,
        compiler_params=_cp(("parallel", "parallel")), name="attention_bwd")(
            proj, proj, proj, cos_f, sin_s, d_attn, attn, lse)


def _cmul(ar, ai, br, bi):
    return ar * br - ai * bi, ar * bi + ai * br


def _pow256(ar, ai):
    for _ in range(8):
        ar, ai = _cmul(ar, ai, ar, ai)
    return ar, ai


def _chunk_carries(first_r, first_i, pr, pi, reverse):
    rows = lax.broadcasted_iota(jnp.int32, first_r.shape, 0)
    out_r = jnp.zeros_like(first_r)
    out_i = jnp.zeros_like(first_i)
    hr = jnp.zeros_like(first_r[0:1])
    hi = jnp.zeros_like(hr)
    order = range(SCAN_CHUNKS - 1, -1, -1) if reverse else range(SCAN_CHUNKS)
    for c in order:
        out_r = jnp.where(rows == c, hr, out_r)
        out_i = jnp.where(rows == c, hi, out_i)
        tr, ti = _cmul(pr[0:1], pi[0:1], hr, hi)
        hr = first_r[c:c + 1] + tr
        hi = first_i[c:c + 1] + ti
    return out_r, out_i


def _tile(j):
    return pl.ds(pl.multiple_of(j * SCAN_CHUNKS, SCAN_CHUNKS), SCAN_CHUNKS)


def _to_scan_rows(t):
    per = SCAN_STEPS // PHASES
    return t.reshape(PHASES, SCAN_CHUNKS, per, t.shape[1]).transpose(2, 0, 1, 3).reshape(t.shape)


def _from_scan_rows(t):
    per = SCAN_STEPS // PHASES
    return t.reshape(per, PHASES, SCAN_CHUNKS, t.shape[1]).transpose(1, 2, 0, 3).reshape(t.shape)


def _scan_in_place(hr_ref, hi_ref, a_r, a_i):
    def local(j, carry):
        tr, ti = _cmul(a_r, a_i, carry[0], carry[1])
        nr = tr + hr_ref[_tile(j), :]
        ni = ti + hi_ref[_tile(j), :]
        hr_ref[_tile(j), :] = nr
        hi_ref[_tile(j), :] = ni
        return nr, ni

    zero = jnp.zeros_like(a_r)
    last_r, last_i = lax.fori_loop(0, SCAN_STEPS, local, (zero, zero), unroll=4)
    pr, pi = _pow256(a_r, a_i)
    er, ei = _chunk_carries(last_r, last_i, pr, pi, reverse=False)

    def fix(j, carry):
        tr, ti = _cmul(carry[0], carry[1], er, ei)
        hr_ref[_tile(j), :] += tr
        hi_ref[_tile(j), :] += ti
        return _cmul(carry[0], carry[1], a_r, a_i)

    lax.fori_loop(0, SCAN_STEPS, fix, (a_r, a_i), unroll=4)
    return er, ei


def _reverse_scan_in_place(lr_ref, li_ref, hr_ref, hi_ref, er, ei, a_r, a_i):
    def local(t, carry):
        j = SCAN_STEPS - 1 - t
        tr, ti = _cmul(a_r, a_i, carry[0], carry[1])
        nr = tr + lr_ref[_tile(j), :]
        ni = ti + li_ref[_tile(j), :]
        lr_ref[_tile(j), :] = nr
        li_ref[_tile(j), :] = ni
        return nr, ni

    zero = jnp.zeros_like(a_r)
    first_r, first_i = lax.fori_loop(0, SCAN_STEPS, local, (zero, zero), unroll=4)
    pr, pi = _pow256(a_r, a_i)
    nxt_r, nxt_i = _chunk_carries(first_r, first_i, pr, pi, reverse=True)

    def accumulate(lam_r, lam_i, hp_r, hp_i, acc):
        return (acc[0] + lam_r * hp_r + lam_i * hp_i, acc[1] + lam_i * hp_r - lam_r * hp_i)

    def fix(t, carry):
        qr, qi, acc_r, acc_i = carry
        j = SCAN_STEPS - 1 - t
        tr, ti = _cmul(qr, qi, nxt_r, nxt_i)
        lam_r = lr_ref[_tile(j), :] + tr
        lam_i = li_ref[_tile(j), :] + ti
        lr_ref[_tile(j), :] = lam_r
        li_ref[_tile(j), :] = lam_i
        acc_r, acc_i = accumulate(lam_r, lam_i, hr_ref[_tile(j - 1), :], hi_ref[_tile(j - 1), :], (acc_r, acc_i))
        qr, qi = _cmul(qr, qi, a_r, a_i)
        return qr, qi, acc_r, acc_i

    qr, qi, acc_r, acc_i = lax.fori_loop(0, SCAN_STEPS - 1, fix, (a_r, a_i, zero, zero), unroll=4)
    tr, ti = _cmul(qr, qi, nxt_r, nxt_i)
    lam_r = lr_ref[_tile(0), :] + tr
    lam_i = li_ref[_tile(0), :] + ti
    lr_ref[_tile(0), :] = lam_r
    li_ref[_tile(0), :] = lam_i
    acc_r, acc_i = accumulate(lam_r, lam_i, er, ei, (acc_r, acc_i))
    return jnp.sum(acc_r, axis=0, keepdims=True), jnp.sum(acc_i, axis=0, keepdims=True)


def _rope_tables():
    half = HEAD_DIM // 2
    inv_freq = ROPE_THETA ** (-jnp.arange(half, dtype=F32) / half)
    ang = jnp.arange(SEQ, dtype=F32)[:, None] * inv_freq[None, :]
    cos, sin = jnp.cos(ang), jnp.sin(ang)
    cos_f = jnp.concatenate([cos, cos, cos, cos], axis=1)
    sin_s = jnp.concatenate([-sin, sin, -sin, sin], axis=1)
    return cos_f, sin_s


def _ssm_discretise(a_re, a_im, log_dt, b_re, b_im):
    lam = lax.complex(a_re, a_im)
    dt = jnp.exp(log_dt)[:, None]
    a_bar = jnp.exp(lam * dt)
    b_bar = ((a_bar - 1.0) / lam)[..., None] * lax.complex(b_re, b_im)
    return a_bar.real, a_bar.imag, b_bar.real, b_bar.imag


SSM_SLABS = 4
SLAB_GROUPS = SSM_GROUPS // SSM_SLABS
SLAB_IN = SSM_WIDTH // SSM_SLABS
SLAB_STATE = SSM_LANES // SSM_SLABS


def _slab_block_diag(blocks):
    _, r, c = blocks.shape
    eye = jnp.eye(SLAB_GROUPS, dtype=blocks.dtype)
    b5 = blocks.reshape(SSM_SLABS, SLAB_GROUPS, r, 1, c) * eye[None, :, None, :, None]
    return b5.reshape(SSM_SLABS, SLAB_GROUPS * r, SLAB_GROUPS * c)


def _diag_blocks(a, b):
    ra, cb = a.shape[1], b.shape[1]
    wa, wb = ra // SLAB_GROUPS, cb // SLAB_GROUPS
    d = lax.dot_general(a, b, (((0,), (0,)), ((), ())), preferred_element_type=F32)
    row_g = jnp.right_shift(lax.broadcasted_iota(jnp.int32, (ra, cb), 0), wa.bit_length() - 1)
    col_g = jnp.right_shift(lax.broadcasted_iota(jnp.int32, (ra, cb), 1), wb.bit_length() - 1)
    d = jnp.where(row_g == col_g, d, 0.0)
    fold = (jnp.bitwise_and(lax.broadcasted_iota(jnp.int32, (cb, wb), 0), wb - 1)
            == lax.broadcasted_iota(jnp.int32, (cb, wb), 1)).astype(F32)
    return jnp.dot(d, fold, preferred_element_type=F32, precision=lax.Precision.HIGHEST)


def _slab_specs():
    tok = pl.BlockSpec((SEQ, SLAB_IN), lambda j: (0, j))
    state = pl.BlockSpec((SEQ, SLAB_STATE), lambda j: (0, j))
    b_in = pl.BlockSpec((None, SLAB_IN, SLAB_STATE), lambda j: (j, 0, 0))
    c_out = pl.BlockSpec((None, SLAB_STATE, SLAB_IN), lambda j: (j, 0, 0))
    vec = pl.BlockSpec((1, SLAB_STATE), lambda j: (0, j))
    ent = pl.BlockSpec((SCAN_CHUNKS, SLAB_STATE), lambda j: (0, j))
    return tok, state, b_in, c_out, vec, ent


def _ssm_forward(u, b_in_r, b_in_i, c_out_r, c_out_ni, a_r, a_i):
    def body(u_ref, br_ref, bi_ref, cr_ref, ci_ref, ar_ref, ai_ref, y_ref, hr_ref, hi_ref, er_ref, ei_ref):
        uu = u_ref[...]
        hr_ref[...] = jnp.dot(uu, br_ref[...], preferred_element_type=F32)
        hi_ref[...] = jnp.dot(uu, bi_ref[...], preferred_element_type=F32)
        a_re = jnp.broadcast_to(ar_ref[...], (SCAN_CHUNKS, SLAB_STATE))
        a_im = jnp.broadcast_to(ai_ref[...], (SCAN_CHUNKS, SLAB_STATE))
        er_ref[...], ei_ref[...] = _scan_in_place(hr_ref, hi_ref, a_re, a_im)
        y_ref[...] = (jnp.dot(hr_ref[...].astype(BF16), cr_ref[...], preferred_element_type=F32)
                      + jnp.dot(hi_ref[...].astype(BF16), ci_ref[...], preferred_element_type=F32))

    tok, state, b_in, c_out, vec, ent = _slab_specs()
    return pl.pallas_call(
        body, grid=(SSM_SLABS,), in_specs=[tok, b_in, b_in, c_out, c_out, vec, vec],
        out_specs=[tok, state, state, ent, ent],
        out_shape=[_sds((SEQ, SSM_WIDTH)), _sds((SEQ, SSM_LANES)), _sds((SEQ, SSM_LANES)),
                   _sds((SCAN_CHUNKS, SSM_LANES)), _sds((SCAN_CHUNKS, SSM_LANES))],
        compiler_params=_cp(("parallel",)), name="ssm_forward")(u, b_in_r, b_in_i, c_out_r, c_out_ni, a_r, a_i)


def _ssm_backward(d_y, d_u_skip, u, h_r, h_i, e_r, e_i, b_in_r, b_in_i, c_out_r, c_out_ni, a_r, a_i):
    def body(dy_ref, skip_ref, u_ref, hr_ref, hi_ref, er_ref, ei_ref, br_ref, bi_ref, cr_ref, ci_ref, ar_ref, ai_ref,
             du_ref, dar_ref, dai_ref, dcr_ref, dci_ref, dbr_ref, dbi_ref, lr_ref, li_ref):
        dy = dy_ref[...]
        lr_ref[...] = _dot_nt(dy, cr_ref[...])
        li_ref[...] = _dot_nt(dy, ci_ref[...])
        a_re = jnp.broadcast_to(ar_ref[...], (SCAN_CHUNKS, SLAB_STATE))
        a_im = -jnp.broadcast_to(ai_ref[...], (SCAN_CHUNKS, SLAB_STATE))
        dar_ref[...], dai_ref[...] = _reverse_scan_in_place(lr_ref, li_ref, hr_ref, hi_ref, er_ref[...], ei_ref[...],
                                                            a_re, a_im)
        dcr_ref[...] = _diag_blocks(dy, hr_ref[...].astype(BF16))
        dci_ref[...] = _diag_blocks(dy, hi_ref[...].astype(BF16))
        lam_r, lam_i = lr_ref[...].astype(BF16), li_ref[...].astype(BF16)
        uu = u_ref[...]
        dbr_ref[...] = _diag_blocks(uu, lam_r)
        dbi_ref[...] = _diag_blocks(uu, lam_i)
        du = skip_ref[...] + _dot_nt(lam_r, br_ref[...]) + _dot_nt(lam_i, bi_ref[...])
        du_ref[...] = du.astype(BF16)

    tok, state, b_in, c_out, vec, ent = _slab_specs()
    db = pl.BlockSpec((SLAB_IN, SSM_STATE), lambda j: (j, 0))
    return pl.pallas_call(
        body, grid=(SSM_SLABS,), in_specs=[tok, tok, tok, state, state, ent, ent, b_in, b_in, c_out, c_out, vec, vec],
        out_specs=[tok, vec, vec, db, db, db, db],
        out_shape=[_sds((SEQ, SSM_WIDTH), BF16), _sds((1, SSM_LANES)), _sds((1, SSM_LANES))]
        + [_sds((SSM_WIDTH, SSM_STATE))] * 4,
        scratch_shapes=[pltpu.VMEM((SEQ, SLAB_STATE), F32)] * 2,
        compiler_params=_cp(("parallel",)), name="ssm_backward")(
            d_y, d_u_skip, u, h_r, h_i, e_r, e_i, b_in_r, b_in_i, c_out_r, c_out_ni, a_r, a_i)


FF_ROWS = 1024
FF_SHARD = D_FF // N_CHIPS


def _dot_nt(a, b):
    return lax.dot_general(a, b, (((1,), (1,)), ((), ())), preferred_element_type=F32)


def _ffn_up(h, w_gate_t, w_up_t):
    def body(h_ref, wg_ref, wu_ref, a_ref, b_ref, act_ref):
        hb = h_ref[...].astype(BF16)
        a = _dot_nt(hb, wg_ref[...])
        b = _dot_nt(hb, wu_ref[...])
        a_ref[...] = a
        b_ref[...] = b
        act_ref[...] = (a * jax.nn.sigmoid(a) * b).astype(BF16)

    w_spec = pl.BlockSpec((None, FF_SHARD, D_MODEL), lambda i, k: (k, 0, 0))
    o_spec = pl.BlockSpec((None, FF_ROWS, FF_SHARD), lambda i, k: (k, i, 0))
    shape = (N_CHIPS, SEQ, FF_SHARD)
    return pl.pallas_call(
        body, grid=(SEQ // FF_ROWS, N_CHIPS),
        in_specs=[pl.BlockSpec((FF_ROWS, D_MODEL), lambda i, k: (i, 0)), w_spec, w_spec],
        out_specs=[o_spec, o_spec, o_spec], out_shape=[_sds(shape), _sds(shape), _sds(shape, BF16)],
        compiler_params=_cp(("parallel", "parallel")), name="ffn_up")(h, w_gate_t, w_up_t)


def _ffn_down_ln2_loss(act, w_down, h, tgt, ln_g, ln_b):
    def body(act_ref, w_ref, h_ref, tgt_ref, g_ref, b_ref, dz_ref, loss_ref, dg_ref, db_ref, acc):
        i, k = pl.program_id(0), pl.program_id(1)
        part = jnp.dot(act_ref[...], w_ref[...], preferred_element_type=F32)

        @pl.when(k == 0)
        def _():
            acc[...] = part

        @pl.when(k > 0)
        def _():
            acc[...] += part

        @pl.when(k == N_CHIPS - 1)
        def _():
            g = g_ref[...]
            xhat, rstd = _ln_stats(DN_ALPHA * h_ref[...] + acc[...])
            err = xhat * g + b_ref[...] - tgt_ref[...]
            d_out = err * (1.0 / D_MODEL)
            dz_ref[...] = _ln_bwd(d_out, xhat, rstd, g)
            loss_rows = jnp.sum(err * err, axis=-1, keepdims=True) * (0.5 / D_MODEL)
            sums = (jnp.broadcast_to(jnp.sum(loss_rows, axis=0, keepdims=True), loss_ref.shape),
                    _colsum(d_out * xhat), _colsum(d_out))
            for ref, val in zip((loss_ref, dg_ref, db_ref), sums):
                @pl.when(i == 0)
                def _(ref=ref, val=val):
                    ref[...] = val

                @pl.when(i > 0)
                def _(ref=ref, val=val):
                    ref[...] += val

    row = pl.BlockSpec((FF_ROWS, D_MODEL), lambda i, k: (i, 0))
    vec = pl.BlockSpec((1, D_MODEL), lambda i, k: (0, 0))
    return pl.pallas_call(
        body, grid=(SEQ // FF_ROWS, N_CHIPS),
        in_specs=[pl.BlockSpec((None, FF_ROWS, FF_SHARD), lambda i, k: (k, i, 0)),
                  pl.BlockSpec((None, FF_SHARD, D_MODEL), lambda i, k: (k, 0, 0)), row, row, vec, vec],
        out_specs=[row, pl.BlockSpec((1, BLOCK), lambda i, k: (0, 0)), vec, vec],
        out_shape=[_sds((SEQ, D_MODEL)), _sds((1, BLOCK)), _sds((1, D_MODEL)), _sds((1, D_MODEL))],
        scratch_shapes=[pltpu.VMEM((FF_ROWS, D_MODEL), F32)],
        compiler_params=_cp(("arbitrary", "arbitrary")), name="ffn_down_ln2_loss")(act, w_down, h, tgt, ln_g, ln_b)


def _ffn_down_bwd(dz, w_down, a, b):
    def body(dz_ref, wd_ref, a_ref, b_ref, da_ref, db_ref):
        d_act = _dot_nt(dz_ref[...].astype(BF16), wd_ref[...])
        av = a_ref[...]
        sg = jax.nn.sigmoid(av)
        da_ref[...] = (d_act * b_ref[...] * sg * (1.0 + av * (1.0 - sg))).astype(BF16)
        db_ref[...] = (d_act * av * sg).astype(BF16)

    t_spec = pl.BlockSpec((None, FF_ROWS, FF_SHARD), lambda i, k: (k, i, 0))
    shape = (N_CHIPS, SEQ, FF_SHARD)
    return pl.pallas_call(
        body, grid=(SEQ // FF_ROWS, N_CHIPS),
        in_specs=[pl.BlockSpec((FF_ROWS, D_MODEL), lambda i, k: (i, 0)),
                  pl.BlockSpec((None, FF_SHARD, D_MODEL), lambda i, k: (k, 0, 0)), t_spec, t_spec],
        out_specs=[t_spec, t_spec], out_shape=[_sds(shape, BF16), _sds(shape, BF16)],
        compiler_params=_cp(("parallel", "parallel")), name="ffn_down_bwd")(dz, w_down, a, b)


def _ffn_dh(d_a, d_b, w_gate_t, w_up_t):
    def body(da_ref, db_ref, wg_ref, wu_ref, o_ref, acc):
        k = pl.program_id(1)
        part = (jnp.dot(da_ref[...], wg_ref[...], preferred_element_type=F32)
                + jnp.dot(db_ref[...], wu_ref[...], preferred_element_type=F32))

        @pl.when(k == 0)
        def _():
            acc[...] = part

        @pl.when(k > 0)
        def _():
            acc[...] += part

        @pl.when(k == N_CHIPS - 1)
        def _():
            o_ref[...] = acc[...]

    t_spec = pl.BlockSpec((None, FF_ROWS, FF_SHARD), lambda i, k: (k, i, 0))
    w_spec = pl.BlockSpec((None, FF_SHARD, D_MODEL), lambda i, k: (k, 0, 0))
    return pl.pallas_call(
        body, grid=(SEQ // FF_ROWS, N_CHIPS), in_specs=[t_spec, t_spec, w_spec, w_spec],
        out_specs=pl.BlockSpec((FF_ROWS, D_MODEL), lambda i, k: (i, 0)), out_shape=_sds((SEQ, D_MODEL)),
        scratch_shapes=[pltpu.VMEM((FF_ROWS, D_MODEL), F32)],
        compiler_params=_cp(("parallel", "arbitrary")), name="ffn_dh")(d_a, d_b, w_gate_t, w_up_t)


def _local_step(x, tgt, wts, small, ffn_grads, mixer_grads, small_grads):
    s = SEQ
    cos_f, sin_s = [_to_phase_rows(t) for t in _rope_tables()]
    x = _reorder_rows(x, to_phase=True, name="phase_rows_x")
    tgt = _reorder_rows(tgt, to_phase=True, name="phase_rows_target")

    proj = _project_in(x, wts["w_in"], cos_f, sin_s)

    attn, lse = _attention_fwd(proj)

    (abar_r, abar_i, bbar_r, bbar_i), ssm_vjp = jax.vjp(
        _ssm_discretise, small["ssm_a_re"], small["ssm_a_im"], small["ssm_log_dt"], small["ssm_b_re"], small["ssm_b_im"])
    b_in_r, b_in_i = [_slab_block_diag(b.transpose(0, 2, 1)).astype(BF16) for b in (bbar_r, bbar_i)]
    c_out_r = _slab_block_diag(small["ssm_c_re"].transpose(0, 2, 1)).astype(BF16)
    c_out_ni = _slab_block_diag(-small["ssm_c_im"].transpose(0, 2, 1)).astype(BF16)
    a_r, a_i = abar_r.reshape(1, SSM_LANES), abar_i.reshape(1, SSM_LANES)
    d_skip = small["ssm_d"].reshape(1, SSM_WIDTH)

    u_f = _to_scan_rows(proj[:, 3 * QKV_WIDTH:3 * QKV_WIDTH + SSM_WIDTH])
    u_p = u_f.astype(BF16)
    y_c, h_r, h_i, e_r, e_i = _ssm_forward(u_p, b_in_r, b_in_i, c_out_r, c_out_ni, a_r, a_i)

    def branch(t, wg):
        return jnp.concatenate([jnp.dot(t, wg[k], preferred_element_type=F32) for k in range(N_CHIPS)], axis=1)

    def branch_t(t, wg):
        ns = wg.shape[2]
        return sum(_dot_nt(t[:, k * ns:(k + 1) * ns], wg[k]) for k in range(N_CHIPS))

    def gelu_glu(yc, u, dsk, wg):
        y = yc + dsk * u
        gel = (0.5 * y * (1.0 + jnp.tanh(GELU_C * (y + GELU_K * y * y * y)))).astype(BF16)
        glu = branch(gel, wg)
        return y, gel, glu, glu[:, :SSM_WIDTH] * jax.nn.sigmoid(glu[:, SSM_WIDTH:])

    y_s5, gel, glu, y_glu = _rowwise(
        gelu_glu, [y_c, u_f], [d_skip, wts["w_glu"]],
        [_sds((s, SSM_WIDTH)), _sds((s, SSM_WIDTH), BF16), _sds((s, 2 * SSM_WIDTH)), _sds((s, SSM_WIDTH), BF16)],
        tm=512, name="ssm_gelu_glu")
    y_glu = _from_scan_rows(y_glu)

    gl0 = (proj, D_MODEL, (3 * QKV_WIDTH + SSM_WIDTH) // D_MODEL)
    gl1 = (proj, D_MODEL, (3 * QKV_WIDTH + SSM_WIDTH) // D_MODEL + 1)
    b_gate = small["b_gate"]
    w_out = wts["w_out"].reshape(D_MODEL, D_MODEL)

    def mix_ln1(l0, l1, at, yg, xv, bg, wa, ws, wo, g, b):
        ya = branch(at.astype(BF16), wa)
        ys = branch(yg, ws)
        mixed = (jax.nn.sigmoid(l0 + bg[0:1]) * ya + jax.nn.sigmoid(l1 + bg[1:2]) * ys).astype(BF16)
        z = DN_ALPHA * xv + jnp.dot(mixed, wo, preferred_element_type=F32)
        xhat, _ = _ln_stats(z)
        return ya, ys, mixed, z, xhat * g + b

    y_attn, y_ssm, mixed, z1, h = _rowwise(
        mix_ln1, [gl0, gl1, attn, y_glu, x],
        [b_gate, wts["w_attn_br"], wts["w_ssm_br"], w_out, small["ln1_g"], small["ln1_b"]],
        [_sds((s, D_MODEL)), _sds((s, D_MODEL)), _sds((s, D_MODEL), BF16), _sds((s, D_MODEL)), _sds((s, D_MODEL))],
        tm=256, name="mix_ln1")

    nf = D_FF // N_CHIPS
    w_gate_t, w_up_t, w_down = wts["w_ff_gate"], wts["w_ff_up"], wts["w_ff_down"]
    ff_a, ff_b, act = _ffn_up(h, w_gate_t, w_up_t)
    dz2, loss_v, d_ln2_g, d_ln2_b = _ffn_down_ln2_loss(act, w_down, h, tgt, small["ln2_g"], small["ln2_b"])

    d_a, d_b = _ffn_down_bwd(dz2, w_down, ff_a, ff_b)

    def grad_rows(lhs, rhs, name):
        return _matmul(lhs, rhs, grid=(N_CHIPS,), a_spec=pl.BlockSpec((None, s, nf), lambda k: (k, 0, 0)),
                       b_spec=pl.BlockSpec((s, D_MODEL), lambda k: (0, 0)),
                       o_spec=pl.BlockSpec((None, nf, D_MODEL), lambda k: (k, 0, 0)),
                       out_shape=_sds((N_CHIPS, nf, D_MODEL), BF16), dims=(0, 0), name=name)

    g_w_ff_down = grad_rows(act, dz2, "g_w_ff_down")
    g_w_ff_gate = grad_rows(d_a, h, "g_w_ff_gate")
    g_w_ff_up = grad_rows(d_b, h, "g_w_ff_up")
    dh_ff = _ffn_dh(d_a, d_b, w_gate_t, w_up_t)

    def ln1_gate_bwd(dz, dff, z, l0, l1, ya, ys, g, bg, wo, wa, ws):
        xhat, rstd = _ln_stats(z)
        dh = DN_ALPHA * dz + dff
        dz_in = _ln_bwd(dh, xhat, rstd, g)
        dm = _dot_nt(dz_in.astype(BF16), wo)
        g0 = jax.nn.sigmoid(l0 + bg[0:1])
        g1 = jax.nn.sigmoid(l1 + bg[1:2])
        dl0 = dm * ya * g0 * (1.0 - g0)
        dl1 = dm * ys * g1 * (1.0 - g1)
        dya, dys = (dm * g0).astype(BF16), (dm * g1).astype(BF16)
        return (dz_in, dya, dys, jnp.concatenate([dl0, dl1], axis=1), branch_t(dya, wa), branch_t(dys, ws),
                _colsum(dh * xhat), _colsum(dh), _colsum(dl0), _colsum(dl1))

    dz1, d_y_attn, d_y_ssm, d_gl, d_attn, d_y_glu, d_ln1_g, d_ln1_b, d_bg0, d_bg1 = _rowwise(
        ln1_gate_bwd, [dz2, dh_ff, z1, gl0, gl1, y_attn, y_ssm],
        [small["ln1_g"], b_gate, w_out, wts["w_attn_br"], wts["w_ssm_br"]],
        [_sds((s, D_MODEL)), _sds((s, D_MODEL), BF16), _sds((s, D_MODEL), BF16), _sds((s, 2 * D_MODEL), BF16),
         _sds((s, ATTN_WIDTH)), _sds((s, SSM_WIDTH))],
        [_sds((1, D_MODEL))] * 4, tm=256, name="ln1_gate_bwd", after=(g_w_ff_down, g_w_ff_gate, g_w_ff_up))
    ffn_sent = ffn_grads({"w_ff_down": g_w_ff_down, "w_ff_gate": g_w_ff_gate, "w_ff_up": g_w_ff_up}, dz1)
    g_w_out = _mm_rows_tn(mixed, dz1, name="g_w_out")

    g_w_ssm_br = _mm_cols_tn(y_glu, d_y_ssm, ns=D_MODEL // N_CHIPS, name="g_w_ssm_br")
    d_y_glu = _to_scan_rows(d_y_glu)

    def glu_gelu_bwd(dyg, gl, y, u, dsk, wg):
        ga, gb = gl[:, :SSM_WIDTH], gl[:, SSM_WIDTH:]
        sg = jax.nn.sigmoid(gb)
        d_gl = jnp.concatenate([dyg * sg, dyg * ga * sg * (1.0 - sg)], axis=1).astype(BF16)
        dg = branch_t(d_gl, wg)
        th = jnp.tanh(GELU_C * (y + GELU_K * y * y * y))
        dy = dg * (0.5 * (1.0 + th) + 0.5 * y * (1.0 - th * th) * GELU_C * (1.0 + 3.0 * GELU_K * y * y))
        return d_gl, dy, dy * dsk, _colsum(dy * u)

    d_glu, d_y, d_u_skip, d_ssm_d = _rowwise(
        glu_gelu_bwd, [d_y_glu, glu, y_s5, u_f], [d_skip, wts["w_glu"]],
        [_sds((s, 2 * SSM_WIDTH), BF16), _sds((s, SSM_WIDTH), BF16), _sds((s, SSM_WIDTH))], [_sds((1, SSM_WIDTH))],
        tm=512, name="glu_gelu_bwd", after=tuple(ffn_sent))
    g_w_glu = _mm_cols_tn(gel, d_glu, ns=2 * SSM_WIDTH // N_CHIPS, name="g_w_glu")
    d_u, d_abar_r, d_abar_i, d_c_r, d_c_ni, d_bin_r, d_bin_i = _ssm_backward(
        d_y, d_u_skip, u_p, h_r, h_i, e_r, e_i, b_in_r, b_in_i, c_out_r, c_out_ni, a_r, a_i)
    d_u = _from_scan_rows(d_u)
    d_bbar_r = d_bin_r.reshape(SSM_GROUPS, SSM_GROUP, SSM_STATE).transpose(0, 2, 1)
    d_bbar_i = d_bin_i.reshape(SSM_GROUPS, SSM_GROUP, SSM_STATE).transpose(0, 2, 1)
    d_a_re, d_a_im, d_log_dt, d_b_re, d_b_im = ssm_vjp(
        (d_abar_r.reshape(SSM_GROUPS, SSM_STATE), d_abar_i.reshape(SSM_GROUPS, SSM_STATE), d_bbar_r, d_bbar_i))
    d_c_re = d_c_r.reshape(SSM_GROUPS, SSM_GROUP, SSM_STATE)
    d_c_im = -d_c_ni.reshape(SSM_GROUPS, SSM_GROUP, SSM_STATE)

    g_w_attn_br = _mm_cols_tn(attn, d_y_attn, ns=D_MODEL // N_CHIPS, name="g_w_attn_br")
    mixer_grads({"w_out": g_w_out, "w_ssm_br": g_w_ssm_br, "w_glu": g_w_glu, "w_attn_br": g_w_attn_br}, d_abar_r)
    small_g = {"b_gate": jnp.concatenate([d_bg0, d_bg1], axis=0), "ssm_a_re": d_a_re, "ssm_a_im": d_a_im,
               "ssm_log_dt": d_log_dt, "ssm_b_re": d_b_re, "ssm_b_im": d_b_im, "ssm_c_re": d_c_re, "ssm_c_im": d_c_im,
               "ssm_d": d_ssm_d.reshape(SSM_WIDTH), "ln1_g": d_ln1_g, "ln1_b": d_ln1_b, "ln2_g": d_ln2_g,
               "ln2_b": d_ln2_b}
    shared = small_grads(small_g, loss_v[0, 0])
    d_q, d_k, d_v = _attention_bwd(proj, cos_f, sin_s, d_attn, attn, lse)

    d_proj = jnp.concatenate([d_q, d_k, d_v, d_u, d_gl], axis=1)
    g_w_in = _mm_cols_tn(x, d_proj, ns=IN_WIDTH // N_CHIPS, name="g_w_in", after=tuple(shared))

    def grad_x_after(after):
        dx_proj = _mm_cols_nt(d_proj, wts["w_in"], tm=1024, name="dx_proj", after=after)
        return _reorder_rows(dz1, dx_proj, to_phase=False, name="grad_x", scale=DN_ALPHA)

    return grad_x_after, g_w_in, d_q


GATHER_ID, SWAP_ID, SCATTER_ID, JOIN_ID, EXCHANGE_ID = 1, 2, 3, 4, 5


def _place():
    return lax.axis_index("x"), lax.axis_index("y"), lax.axis_index("c")


def _other_chips(x, y):
    return [(1 - x, y), (x, 1 - y), (1 - x, 1 - y)]


def _handshake(peers):
    barrier = pltpu.get_barrier_semaphore()
    for peer in peers:
        pl.semaphore_signal(barrier, inc=1, device_id=peer, device_id_type=MESH)
    pl.semaphore_wait(barrier, len(peers))


def _sequencer(body, arrays, out_type, sems, collective_id, name):
    return pl.kernel(body, name=name, out_type=out_type,
                     mesh=plsc.ScalarSubcoreMesh(axis_name="sequencer", num_cores=1), scratch_types=sems,
                     compiler_params=pltpu.CompilerParams(collective_id=collective_id))(*arrays)


def _gather_weights(shards, *, name):
    nw = len(shards)

    def body(*refs):
        ins, outs = refs[:nw], refs[nw:2 * nw]
        send_sems, recv_sems, pass_send, pass_recv, local_sems = refs[2 * nw:]
        x, y, c = _place()
        chip = 2 * x + y
        chips = _other_chips(x, y)
        _handshake([(x, y, 1 - c)] + [(cx, cy, c) for cx, cy in chips])
        started = []
        for w in range(nw):
            hw = shards[w].shape[0] // 2
            mine = pl.ds(c * hw, hw)
            own = pltpu.make_async_copy(ins[w], outs[w].at[chip], local_sems.at[w])
            own.start()
            started.append(own)
            for j, (cx, cy) in enumerate(chips):
                cp = pltpu.make_async_remote_copy(
                    src_ref=ins[w].at[mine], dst_ref=outs[w].at[chip, mine], send_sem=send_sems.at[w, j],
                    recv_sem=recv_sems.at[w, j], device_id=(cx, cy, c), device_id_type=MESH)
                cp.start()
                started.append(cp)
        passed = []
        for w in range(nw):
            hw = shards[w].shape[0] // 2
            mine = pl.ds(c * hw, hw)
            for j, (cx, cy) in enumerate(chips):
                landed = outs[w].at[2 * cx + cy, mine]
                pltpu.make_async_remote_copy(
                    src_ref=ins[w].at[mine], dst_ref=landed, send_sem=send_sems.at[w, j],
                    recv_sem=recv_sems.at[w, j], device_id=(cx, cy, c), device_id_type=MESH).wait_recv()
                cp = pltpu.make_async_remote_copy(
                    src_ref=landed, dst_ref=landed, send_sem=pass_send.at[w, j], recv_sem=pass_recv.at[w, j],
                    device_id=(x, y, 1 - c), device_id_type=MESH)
                cp.start()
                passed.append(cp)
        for w in range(nw):
            hw = shards[w].shape[0] // 2
            theirs = pl.ds((1 - c) * hw, hw)
            for j, (cx, cy) in enumerate(chips):
                landed = outs[w].at[2 * cx + cy, theirs]
                pltpu.make_async_remote_copy(
                    src_ref=landed, dst_ref=landed, send_sem=pass_send.at[w, j], recv_sem=pass_recv.at[w, j],
                    device_id=(x, y, 1 - c), device_id_type=MESH).wait_recv()
        for cp in started[0::4]:
            cp.wait()
        for cp in [s for i, s in enumerate(started) if i % 4] + passed:
            cp.wait_send()

    sem = pltpu.SemaphoreType.DMA
    return _sequencer(body, shards, [_sds((N_CHIPS,) + a.shape, a.dtype) for a in shards],
                      [sem((nw, 3)), sem((nw, 3)), sem((nw, 3)), sem((nw, 3)), sem((nw,))], GATHER_ID, name)


def _swap_other_halves(grads, *, name):
    nw = len(grads)

    def body(*refs):
        ins, outs = refs[:nw], refs[nw:2 * nw]
        send_sems, recv_sems = refs[2 * nw:]
        x, y, c = _place()
        _handshake([(x, y, 1 - c)])
        cps = []
        for w in range(nw):
            hw = grads[w].shape[1] // 2
            cp = pltpu.make_async_remote_copy(
                src_ref=ins[w].at[:, pl.ds((1 - c) * hw, hw)], dst_ref=outs[w], send_sem=send_sems.at[w],
                recv_sem=recv_sems.at[w], device_id=(x, y, 1 - c), device_id_type=MESH)
            cp.start()
            cps.append(cp)
        for cp in cps:
            cp.wait()

    sem = pltpu.SemaphoreType.DMA
    return _sequencer(body, grads, [_sds((N_CHIPS, g.shape[1] // 2, g.shape[2]), g.dtype) for g in grads],
                      [sem((nw,)), sem((nw,))], SWAP_ID, name)


def _add_my_halves(core, grads, others, *, name, after=()):
    nw = len(grads)
    halves = [g.shape[1] // 2 for g in grads]

    def body(core_ref, *refs):
        outs = refs[2 * nw + len(after):]
        for g_ref, o_ref, out_ref in zip(refs[:nw], refs[nw:2 * nw], outs):
            out_ref[...] = (g_ref[...].astype(F32) + o_ref[...].astype(F32)).astype(out_ref.dtype)

    in_specs = [pl.BlockSpec((None, None, hw, g.shape[2]), lambda s, core_ref: (s, core_ref[0], 0, 0))
                for g, hw in zip(grads, halves)]
    in_specs += [pl.BlockSpec((None, hw, g.shape[2]), lambda s, core_ref: (s, 0, 0)) for g, hw in zip(grads, halves)]
    return pl.pallas_call(
        body,
        grid_spec=pltpu.PrefetchScalarGridSpec(
            num_scalar_prefetch=1, grid=(N_CHIPS,), in_specs=in_specs + [HBM_OPERAND] * len(after),
            out_specs=[pl.BlockSpec((None, hw, g.shape[2]), lambda s, core_ref: (s, 0, 0))
                       for g, hw in zip(grads, halves)]),
        out_shape=[_sds((N_CHIPS, hw, g.shape[2]), BF16) for g, hw in zip(grads, halves)],
        compiler_params=_cp(("parallel",)), name=name)(
            core, *[g.reshape(N_CHIPS, 2, hw, g.shape[2]) for g, hw in zip(grads, halves)], *others, *after)


def _scatter_partials(parts, *, name):
    nw = len(parts)

    def body(*refs):
        ins, outs = refs[:nw], refs[nw:2 * nw]
        send_sems, recv_sems = refs[2 * nw:]
        x, y, c = _place()
        _handshake([(cx, cy, c) for cx, cy in _other_chips(x, y)])
        cps = []
        for w in range(nw):
            for j, (cx, cy) in enumerate(_other_chips(x, y)):
                cp = pltpu.make_async_remote_copy(
                    src_ref=ins[w].at[2 * cx + cy], dst_ref=outs[w].at[j], send_sem=send_sems.at[w, j],
                    recv_sem=recv_sems.at[w, j], device_id=(cx, cy, c), device_id_type=MESH)
                cp.start()
                cps.append(cp)
        for cp in cps:
            cp.wait()

    sem = pltpu.SemaphoreType.DMA
    return _sequencer(body, parts, [_sds((3,) + p.shape[1:], p.dtype) for p in parts],
                      [sem((nw, 3)), sem((nw, 3))], SCATTER_ID, name)


SUM_STEPS = 2


def _sum_partials(chip, parts, recvd, *, name, after=()):
    nw = len(parts)
    rows = [p.shape[1] // SUM_STEPS for p in parts]

    def body(chip_ref, *refs):
        outs = refs[2 * nw + len(after):]
        for p_ref, r_ref, out_ref in zip(refs[:nw], refs[nw:2 * nw], outs):
            acc = p_ref[...].astype(F32)
            for j in range(3):
                acc = acc + r_ref[j].astype(F32)
            out_ref[...] = acc

    in_specs = [pl.BlockSpec((None, th, p.shape[2]), lambda i, chip_ref: (chip_ref[0], i, 0))
                for p, th in zip(parts, rows)]
    in_specs += [pl.BlockSpec((3, th, p.shape[2]), lambda i, chip_ref: (0, i, 0)) for p, th in zip(parts, rows)]
    return pl.pallas_call(
        body,
        grid_spec=pltpu.PrefetchScalarGridSpec(
            num_scalar_prefetch=1, grid=(SUM_STEPS,), in_specs=in_specs + [HBM_OPERAND] * len(after),
            out_specs=[pl.BlockSpec((th, p.shape[2]), lambda i, chip_ref: (i, 0)) for p, th in zip(parts, rows)]),
        out_shape=[_sds(p.shape[1:]) for p in parts], compiler_params=_cp(("parallel",)), name=name)(
            chip, *parts, *recvd, *after)


def _swap_reduced_halves(halves, *, name):
    nw = len(halves)

    def body(*refs):
        ins, outs = refs[:nw], refs[nw:2 * nw]
        send_sems, recv_sems = refs[2 * nw:]
        x, y, c = _place()
        _handshake([(x, y, 1 - c)])
        cps = []
        for w in range(nw):
            cp = pltpu.make_async_remote_copy(
                src_ref=ins[w], dst_ref=outs[w], send_sem=send_sems.at[w], recv_sem=recv_sems.at[w],
                device_id=(x, y, 1 - c), device_id_type=MESH)
            cp.start()
            cps.append(cp)
        for cp in cps:
            cp.wait()

    sem = pltpu.SemaphoreType.DMA
    return _sequencer(body, halves, [_sds(h.shape, h.dtype) for h in halves], [sem((nw,)), sem((nw,))], JOIN_ID, name)


def _exchange_rows(vec, *, name):
    def body(v_ref, slots, send_sems, recv_sems, local_sem):
        x, y, c = _place()
        me = 4 * x + 2 * y + c
        peers = []
        for mask in range(1, N_DEV):
            peers.append((1 - x if mask & 4 else x, 1 - y if mask & 2 else y, 1 - c if mask & 1 else c))
        _handshake(peers)
        own = pltpu.make_async_copy(v_ref, slots.at[me], local_sem)
        own.start()
        cps = []
        for k, peer in enumerate(peers):
            cp = pltpu.make_async_remote_copy(
                src_ref=v_ref, dst_ref=slots.at[me], send_sem=send_sems.at[k], recv_sem=recv_sems.at[k],
                device_id=peer, device_id_type=MESH)
            cp.start()
            cps.append(cp)
        for k, (px, py, pc) in enumerate(peers):
            pltpu.make_async_remote_copy(
                src_ref=v_ref, dst_ref=slots.at[4 * px + 2 * py + pc], send_sem=send_sems.at[k],
                recv_sem=recv_sems.at[k], device_id=(px, py, pc), device_id_type=MESH).wait_recv()
        for cp in cps:
            cp.wait_send()
        own.wait()

    sem = pltpu.SemaphoreType.DMA
    return _sequencer(body, [vec], [_sds((N_DEV,) + vec.shape)], [sem((N_DEV - 1,)), sem((N_DEV - 1,)), sem(())],
                      EXCHANGE_ID, name)[0]


def _sum_slots(slots, *, name, after=()):
    def body(s_ref, *rest):
        out_ref = rest[len(after)]
        acc = s_ref[0]
        for d in range(1, N_DEV):
            acc = acc + s_ref[d]
        out_ref[...] = acc

    vmem = pl.BlockSpec(memory_space=pltpu.VMEM)
    return pl.pallas_call(
        body, in_specs=[vmem] + [HBM_OPERAND] * len(after), out_specs=vmem, out_shape=_sds(slots.shape[1:]),
        compiler_params=pltpu.CompilerParams(vmem_limit_bytes=VMEM_LIMIT_BYTES), name=name)(slots, *after)


def _reduce_scatter_start(grads, core, *, tag, add_after=()):
    others = _swap_other_halves(grads, name="swap_other_halves_" + tag)
    parts = _add_my_halves(core, grads, others, name="add_my_halves_" + tag, after=add_after)
    return parts, _scatter_partials(parts, name="scatter_partials_" + tag)


def _reduce_scatter_finish(parts, recvd, chip, *, tag, sum_after=()):
    mine = _sum_partials(chip, parts, recvd, name="sum_partials_" + tag, after=sum_after)
    return mine, _swap_reduced_halves(mine, name="swap_reduced_halves_" + tag)


ADAM_BLOCK_ELEMS = 256 * 1024


def _adam_rows(rows, cols):
    tm = rows
    while tm * cols > ADAM_BLOCK_ELEMS and tm % 16 == 0:
        tm //= 2
    return tm


def _adam_step(wv, gv, mv, vv):
    m2 = ADAM_B1 * mv + (1.0 - ADAM_B1) * gv
    v2 = ADAM_B2 * vv + (1.0 - ADAM_B2) * (gv * gv)
    m_hat = m2 / (1.0 - ADAM_B1 ** ADAM_STEP)
    v_hat = v2 / (1.0 - ADAM_B2 ** ADAM_STEP)
    return -ADAM_LR * (m_hat / (jnp.sqrt(v_hat) + ADAM_EPS) + ADAM_WD * wv), m2, v2


def _adamw_each(ws, gs, ms, vs, *, name, after=()):
    n = len(ws)
    whole = pl.BlockSpec(memory_space=pltpu.VMEM)

    def body(*refs):
        ins, outs = refs[:4 * n], refs[4 * n + len(after):]
        for i in range(n):
            res = _adam_step(*(ins[k * n + i][...] for k in range(4)))
            for k in range(3):
                outs[k * n + i][...] = res[k]

    out = pl.pallas_call(body, in_specs=[whole] * (4 * n) + [HBM_OPERAND] * len(after),
                         out_shape=[_sds(w.shape) for w in ws] * 3, name=name)(*ws, *gs, *ms, *vs, *after)
    return out[:n], out[n:2 * n], out[2 * n:]


def _adamw_halves(core, w, g_mine, g_theirs, m, v, *, name, after=()):
    rows, cols = w.shape
    hw = rows // 2
    tm = _adam_rows(hw, cols)
    per_half = hw // tm

    def body(core_ref, w_ref, gm_ref, gt_ref, m_ref, v_ref, *rest):
        g_out, d_out, m_out, v_out = rest[len(after):]
        mine = (pl.program_id(0) // per_half) == core_ref[0]
        g = jnp.where(mine, gm_ref[...], gt_ref[...])
        d, m2, v2 = _adam_step(w_ref[...], g, m_ref[...], v_ref[...])
        g_out[...] = g
        d_out[...] = d
        m_out[...] = m2
        v_out[...] = v2

    full = pl.BlockSpec((tm, cols), lambda i, core_ref: (i, 0))

    def half(wanted):
        def index(i, core_ref):
            in_use = ((i // per_half) == core_ref[0]) == wanted
            return (jnp.where(in_use, i % per_half, 0), 0)
        return pl.BlockSpec((tm, cols), index)

    return pl.pallas_call(
        body,
        grid_spec=pltpu.PrefetchScalarGridSpec(
            num_scalar_prefetch=1, grid=(rows // tm,),
            in_specs=[full, half(True), half(False), full, full] + [HBM_OPERAND] * len(after),
            out_specs=[full, full, full, full]),
        out_shape=[_sds((rows, cols))] * 4, compiler_params=_cp(("parallel",)), name=name)(
            core, w, g_mine, g_theirs, m, v, *after)


HELD_TRANSPOSED = ("w_ff_gate", "w_ff_up")


def _as_rows(name, arr):
    return arr[0].T if name in HELD_TRANSPOSED else arr[0]


def _from_rows(name, arr2d):
    return (arr2d.T if name in HELD_TRANSPOSED else arr2d)[None]


STORED_SWAPPED = ("ssm_b_re", "ssm_b_im")


def _as_stored(name, arr):
    return jnp.swapaxes(arr, -1, -2) if name in STORED_SWAPPED else arr


def _pack_rows(arrs):
    flat = jnp.concatenate([a.reshape(-1).astype(F32) for a in arrs])
    rows = -(-flat.shape[0] // 1024) * 8
    return jnp.pad(flat, (0, rows * 128 - flat.shape[0])).reshape(rows, 128)


def _unpack_rows(vec, shapes):
    flat = vec.reshape(-1)
    out, off = [], 0
    for shp in shapes:
        size = math.prod(shp)
        out.append(flat[off:off + size].reshape(shp))
        off += size
    return out


SMALL = ("b_gate", "ssm_a_re", "ssm_a_im", "ssm_log_dt", "ssm_b_re", "ssm_b_im", "ssm_c_re", "ssm_c_im", "ssm_d",
         "ln1_g", "ln1_b", "ln2_g", "ln2_b")
GATHER_GROUPS = (("w_in", ("w_in",)), ("mixer", ("w_attn_br", "w_ssm_br", "w_glu", "w_out")),
                 ("ffn_up", ("w_ff_gate", "w_ff_up")), ("ffn_down", ("w_ff_down",)))
REDUCE_GROUPS = (("ffn", ("w_ff_down", "w_ff_gate", "w_ff_up")),
                 ("mixer", ("w_out", "w_ssm_br", "w_glu", "w_attn_br")), ("w_in", ("w_in",)))
WEIGHTS = ("w_in", "b_gate", "w_attn_br", "w_ssm_br", "w_out", "ssm_a_re", "ssm_a_im", "ssm_log_dt", "ssm_b_re",
           "ssm_b_im", "ssm_c_re", "ssm_c_im", "ssm_d", "w_glu", "ln1_g", "ln1_b", "w_ff_gate", "w_ff_up", "w_ff_down",
           "ln2_g", "ln2_b")


def kernel(x, w_in, b_gate, w_attn_br, w_ssm_br, w_out, ssm_a_re, ssm_a_im, ssm_log_dt, ssm_b_re, ssm_b_im, ssm_c_re, ssm_c_im, ssm_d, w_glu, ln1_g, ln1_b, w_ff_gate, w_ff_up, w_ff_down, ln2_g, ln2_b, loss_target, m_w_in, m_b_gate, m_w_attn_br, m_w_ssm_br, m_w_out, m_ssm_a_re, m_ssm_a_im, m_ssm_log_dt, m_ssm_b_re, m_ssm_b_im, m_ssm_c_re, m_ssm_c_im, m_ssm_d, m_w_glu, m_ln1_g, m_ln1_b, m_w_ff_gate, m_w_ff_up, m_w_ff_down, m_ln2_g, m_ln2_b, v_w_in, v_b_gate, v_w_attn_br, v_w_ssm_br, v_w_out, v_ssm_a_re, v_ssm_a_im, v_ssm_log_dt, v_ssm_b_re, v_ssm_b_im, v_ssm_c_re, v_ssm_c_im, v_ssm_d, v_w_glu, v_ln1_g, v_ln1_b, v_w_ff_gate, v_w_ff_up, v_w_ff_down, v_ln2_g, v_ln2_b):
    given = dict(locals())
    px, py, pc = _place()
    chip = 2 * px + py
    core_s = jnp.reshape(pc, (1,)).astype(jnp.int32)
    chip_s = jnp.reshape(chip, (1,)).astype(jnp.int32)

    wts = {}
    for tag, names in GATHER_GROUPS:
        wts.update(zip(names, _gather_weights([_as_rows(n, given[n]).astype(BF16) for n in names],
                                              name="gather_" + tag)))
    ncol = D_MODEL // N_CHIPS
    bg_mine = jnp.where(pc == 0, b_gate[0], jnp.zeros_like(b_gate[0]))
    bg_full = lax.dynamic_update_slice(jnp.zeros((2, D_MODEL), F32), bg_mine, (0, chip * ncol))
    bg_slots = _exchange_rows(bg_full.reshape(16, 128), name="exchange_gate_bias")
    bg_full = _sum_slots(bg_slots, name="sum_gate_bias").reshape(2, D_MODEL)
    small = {n: given[n][0] for n in SMALL if n.startswith("ssm")}
    small.update({n: given[n] for n in ("ln1_g", "ln1_b", "ln2_g", "ln2_b")})
    small["b_gate"] = bg_full

    groups = dict(REDUCE_GROUPS)
    parts, recvd, reduced, sent = {}, {}, {}, {}
    grads, delta, new_m, new_v, done = {}, {}, {}, {}, {}

    def start(tag, big_g, add_after):
        parts[tag], recvd[tag] = _reduce_scatter_start([big_g[n] for n in groups[tag]], core_s, tag=tag,
                                                       add_after=add_after)
        return parts[tag]

    def reduce_sum(tag, after):
        reduced[tag] = _reduce_scatter_finish(parts[tag], recvd[tag], chip_s, tag=tag, sum_after=after)
        return reduced[tag][0]

    def adam(tag, after):
        for n, g_mine, g_theirs in zip(groups[tag], *reduced[tag]):
            res = _adamw_halves(core_s, _as_rows(n, given[n]), g_mine, g_theirs, _as_rows(n, given["m_" + n]),
                                _as_rows(n, given["v_" + n]), name="adamw_" + n, after=after)
            done[n] = res[1]
            grads[n], delta[n], new_m[n], new_v[n] = [_from_rows(n, r) for r in res]

    def ffn_grads(big_g, norm_bwd):
        return start("ffn", big_g, (norm_bwd,))

    def mixer_grads(big_g, scan_bwd):
        return start("mixer", big_g, (scan_bwd, *reduce_sum("ffn", (scan_bwd,))))

    def small_grads(small_g, loss_mine):
        sent["stored"] = [_as_stored(n, small_g[n]) for n in SMALL] + [loss_mine.reshape(1)]
        packed = _pack_rows(sent["stored"])
        sent["slots"] = _exchange_rows(packed, name="exchange_small")
        return (packed,)

    grad_x_after, g_w_in, attention_bwd = _local_step(x[0], loss_target[0], wts, small,
                                                      ffn_grads, mixer_grads, small_grads)

    reduce_sum("mixer", (attention_bwd,))
    summed = _sum_slots(sent["slots"], name="sum_small", after=(g_w_in,))
    adam("mixer", (g_w_in,))
    start("w_in", {"w_in": g_w_in}, (summed, *[done[n] for n in groups["mixer"]]))
    in_flight = (parts["w_in"][0],)
    grad_x = grad_x_after(in_flight)
    adam("ffn", in_flight)
    summed = _unpack_rows(summed, [a.shape for a in sent["stored"]])
    loss = summed.pop()[0]
    at = SMALL.index("b_gate")
    summed[at] = lax.dynamic_slice(summed[at], (0, chip * ncol), (2, ncol))
    summed = [g.reshape(1, -1) if g.ndim == 1 else g for g in summed]
    held = [[_as_stored(n, given[prefix + n]).reshape(g.shape) for n, g in zip(SMALL, summed)]
            for prefix in ("", "m_", "v_")]
    small_out = _adamw_each(held[0], summed, held[1], held[2], name="adamw_small", after=in_flight)
    for out, arrs in zip((grads, delta, new_m, new_v), (summed, *small_out)):
        out.update((n, _as_stored(n, a).reshape(given[n].shape)) for n, a in zip(SMALL, arrs))
    reduce_sum("w_in", (*[done[n] for n in groups["ffn"]], small_out[0][0], grad_x))
    adam("w_in", ())

    return (loss, grad_x.reshape(x.shape), *[grads[n] for n in WEIGHTS], *[delta[n] for n in WEIGHTS],
            *[new_m[n] for n in WEIGHTS], *[new_v[n] for n in WEIGHTS])
```

```python
import math

import jax
import jax.numpy as jnp
from jax import lax
from jax.experimental import pallas as pl
from jax.experimental.pallas import tpu as pltpu
from jax.experimental.pallas import tpu_sc as plsc

F32 = jnp.float32
BF16 = jnp.bfloat16
MESH = pl.DeviceIdType.MESH

D_MODEL = 1024
SEQ = 2048
HEAD_DIM = 64
ATTN_HEADS = 8
DILATIONS = (1, 4, 16)
ATTN_WIDTH = ATTN_HEADS * HEAD_DIM
QKV_WIDTH = 3 * ATTN_WIDTH
BLOCK = 128
ROPE_THETA = 10000.0
NEG_INF = -1e30
SSM_GROUP = 16
SSM_GROUPS = 32
SSM_WIDTH = 512
SSM_STATE = 64
SSM_LANES = SSM_GROUPS * SSM_STATE
SCAN_CHUNKS = 8
SCAN_STEPS = SEQ // SCAN_CHUNKS
IN_WIDTH = 3 * QKV_WIDTH + SSM_WIDTH + 2 * D_MODEL
D_FF = 2816
N_CHIPS = 4
N_DEV = 8
DN_ALPHA = 2.0 ** 0.25
LN_EPS = 1e-5
ADAM_LR = 0.001
ADAM_B1 = 0.9
ADAM_B2 = 0.999
ADAM_EPS = 1e-08
ADAM_WD = 0.01
ADAM_STEP = 10
GELU_C = math.sqrt(2.0 / math.pi)
GELU_K = 0.044715

VMEM_LIMIT_BYTES = 56 * 1024 * 1024


def _sds(shape, dtype=F32):
    return jax.ShapeDtypeStruct(tuple(shape), dtype)


def _cp(semantics=None):
    return pltpu.CompilerParams(dimension_semantics=semantics, vmem_limit_bytes=VMEM_LIMIT_BYTES)


HBM_OPERAND = pl.BlockSpec(memory_space=pl.ANY)


def _matmul(a, b, *, grid, a_spec, b_spec, o_spec, out_shape, dims, k_axis=None, name, after=()):
    nk = grid[k_axis] if k_axis is not None else 1
    o_block = tuple(d for d in o_spec.block_shape if d is not None)
    n_after = len(after)

    def body(a_ref, b_ref, *rest):
        o_ref, acc = rest[n_after], rest[n_after + 1:]
        part = lax.dot_general(a_ref[...].astype(BF16), b_ref[...].astype(BF16),
                               (((dims[0],), (dims[1],)), ((), ())), preferred_element_type=F32)
        if k_axis is None:
            o_ref[...] = part.astype(o_ref.dtype)
        else:
            k = pl.program_id(k_axis)

            @pl.when(k == 0)
            def _():
                acc[0][...] = part

            @pl.when(k > 0)
            def _():
                acc[0][...] += part

            @pl.when(k == nk - 1)
            def _():
                o_ref[...] = acc[0][...].astype(o_ref.dtype)

    sem = tuple("arbitrary" if ax == k_axis else "parallel" for ax in range(len(grid)))
    return pl.pallas_call(
        body, grid=grid, in_specs=[a_spec, b_spec] + [HBM_OPERAND] * n_after, out_specs=o_spec, out_shape=out_shape,
        scratch_shapes=[pltpu.VMEM(o_block, F32)] if k_axis is not None else [],
        compiler_params=_cp(sem), name=name)(a, b, *after)


def _mm_cols_nt(dy, wg, *, tm, name, out_dtype=F32, after=()):
    k, ns = wg.shape[1], wg.shape[2]
    m = dy.shape[0]
    a_spec = pl.BlockSpec((tm, ns), lambda i, s: (i, s))
    return _matmul(dy, wg, grid=(m // tm, N_CHIPS), a_spec=a_spec,
                   b_spec=pl.BlockSpec((None, k, ns), lambda i, s: (s, 0, 0)),
                   o_spec=pl.BlockSpec((tm, k), lambda i, s: (i, 0)),
                   out_shape=_sds((m, k), out_dtype), dims=(1, 1), k_axis=1, name=name, after=after)


def _mm_cols_tn(a, dy, *, ns, name, after=()):
    m, k = a.shape
    return _matmul(a, dy, grid=(N_CHIPS,), a_spec=pl.BlockSpec((m, k), lambda s: (0, 0)),
                   b_spec=pl.BlockSpec((m, ns), lambda s: (0, s)),
                   o_spec=pl.BlockSpec((None, k, ns), lambda s: (s, 0, 0)),
                   out_shape=_sds((N_CHIPS, k, ns), BF16), dims=(0, 0), name=name, after=after)


def _mm_rows_tn(a, dy, *, name):
    m, k = a.shape
    rows, n = k // N_CHIPS, dy.shape[1]
    return _matmul(a, dy, grid=(N_CHIPS,), a_spec=pl.BlockSpec((m, rows), lambda s: (0, s)),
                   b_spec=pl.BlockSpec((m, n), lambda s: (0, 0)),
                   o_spec=pl.BlockSpec((None, rows, n), lambda s: (s, 0, 0)),
                   out_shape=_sds((N_CHIPS, rows, n), BF16), dims=(0, 0), name=name)


def _rowwise(fn, tiled, full, outs, accs=(), *, tm, name, after=()):
    args, in_specs = [], []
    for t in tiled:
        if isinstance(t, tuple):
            arr, w, cb = t
            in_specs.append(pl.BlockSpec((tm, w), lambda i, cb=cb: (i, cb)))
        else:
            arr = t
            in_specs.append(pl.BlockSpec((tm, arr.shape[1]), lambda i: (i, 0)))
        args.append(arr)
    rows = args[0].shape[0]
    for f in full:
        in_specs.append(pl.BlockSpec(f.shape, lambda i, nd=f.ndim: (0,) * nd))
        args.append(f)
    out_specs = [pl.BlockSpec((tm, o.shape[1]), lambda i: (i, 0)) for o in outs]
    out_specs += [pl.BlockSpec(a.shape, lambda i, nd=len(a.shape): (0,) * nd) for a in accs]
    n_in, n_out = len(args), len(outs)
    in_specs += [HBM_OPERAND] * len(after)
    first_out = n_in + len(after)

    def body(*refs):
        res = fn(*[r[...] for r in refs[:n_in]])
        res = res if isinstance(res, (tuple, list)) else (res,)
        for r, v in zip(refs[first_out:first_out + n_out], res[:n_out]):
            r[...] = v.astype(r.dtype)
        i = pl.program_id(0)
        for r, v in zip(refs[first_out + n_out:], res[n_out:]):
            @pl.when(i == 0)
            def _(r=r, v=v):
                r[...] = v

            @pl.when(i > 0)
            def _(r=r, v=v):
                r[...] += v

    res = pl.pallas_call(
        body, grid=(rows // tm,), in_specs=in_specs, out_specs=out_specs, out_shape=list(outs) + list(accs),
        compiler_params=_cp(("arbitrary",) if accs else ("parallel",)), name=name)(*args, *after)
    return res


def _colsum(v):
    return jnp.sum(v, axis=0, keepdims=True)


def _ln_stats(z):
    mu = jnp.mean(z, axis=-1, keepdims=True)
    zc = z - mu
    var = jnp.mean(zc * zc, axis=-1, keepdims=True)
    rstd = lax.rsqrt(var + LN_EPS)
    return zc * rstd, rstd


def _ln_bwd(dy, xhat, rstd, g):
    dxh = dy * g
    m1 = jnp.mean(dxh, axis=-1, keepdims=True)
    m2 = jnp.mean(dxh * xhat, axis=-1, keepdims=True)
    return rstd * (dxh - m1 - xhat * m2)


def _swap_halves(t):
    w = t.shape[-1]
    lane = lax.broadcasted_iota(jnp.int32, t.shape, t.ndim - 1)
    return jnp.where((lane % HEAD_DIM) < HEAD_DIM // 2, pltpu.roll(t, w - HEAD_DIM // 2, t.ndim - 1),
                     pltpu.roll(t, HEAD_DIM // 2, t.ndim - 1))


PHASES = max(DILATIONS)
PAIR = 2 * HEAD_DIM
UNITS = SEQ // BLOCK
UNIT_BATCH = 16
ROPE_ROWS = 256


def _to_phase_rows(t):
    return t.reshape(SEQ // PHASES, PHASES, t.shape[1]).transpose(1, 0, 2).reshape(t.shape)


def _reorder_rows(arr, plus=None, *, to_phase, name, scale=1.0):
    def body(*refs):
        o_ref = refs[-1]
        for rho in range(PHASES):
            phase = pl.ds(rho * BLOCK, BLOCK)
            strided = pl.ds(rho, BLOCK, stride=PHASES)
            src, dst = (strided, phase) if to_phase else (phase, strided)
            val = refs[0][src, :]
            if scale != 1.0:
                val = val * scale
            if plus is not None:
                val = val + refs[1][src, :]
            o_ref[dst, :] = val

    spec = pl.BlockSpec((SEQ, BLOCK), lambda j: (0, j))
    ins = [arr] if plus is None else [arr, plus]
    return pl.pallas_call(body, grid=(arr.shape[1] // BLOCK,), in_specs=[spec] * len(ins), out_specs=spec,
                          out_shape=_sds(arr.shape), compiler_params=_cp(("parallel",)), name=name)(*ins)


def _rope(t, cf, ss):
    return t * cf + _swap_halves(t) * ss


def _rope_transposed(d, cf, ss):
    return d * cf + _swap_halves(d * ss)


def _unit_pieces(u, dil):
    pieces, length = PHASES // dil, 8 * dil
    if dil == 1:
        rho, i = 0, u
    elif dil == PHASES:
        rho, i = u, 0
    else:
        rho, i = jnp.bitwise_and(u, dil - 1), jnp.right_shift(u, dil.bit_length() - 1)
    before = jnp.maximum(i - 1, 0)
    cur = [pl.multiple_of((rho + dil * k) * BLOCK + length * i, 8) for k in range(pieces)]
    prev = [pl.multiple_of((rho + dil * k) * BLOCK + length * before, 8) for k in range(pieces)]
    return i, cur, prev


def _load_tile(ref, starts, dil):
    return jnp.concatenate([ref[pl.ds(st, 8 * dil), :] for st in starts], axis=0)


def _store_tile(ref, starts, dil, val, head=None, accumulate=False):
    length = 8 * dil
    lanes = slice(None) if head is None else pl.ds(head * HEAD_DIM, HEAD_DIM)
    cols = slice(None) if head is None else slice(head * HEAD_DIM, (head + 1) * HEAD_DIM)
    for k, st in enumerate(starts):
        piece = val[k * length:(k + 1) * length, cols]
        if accumulate:
            ref[pl.ds(st, length), lanes] += piece
        else:
            ref[pl.ds(st, length), lanes] = piece


def _tile_position(idx, dil):
    pieces, length = PHASES // dil, 8 * dil
    return pieces * jnp.bitwise_and(idx, length - 1) + jnp.right_shift(idx, length.bit_length() - 1)


def _band_mask(i, dil):
    row = lax.broadcasted_iota(jnp.int32, (BLOCK, 2 * BLOCK), 0)
    col = lax.broadcasted_iota(jnp.int32, (BLOCK, 2 * BLOCK), 1)
    key_pos = _tile_position(jnp.bitwise_and(col, BLOCK - 1), dil) + jnp.where(col >= BLOCK, 0, -BLOCK)
    dist = _tile_position(row, dil) - key_pos
    return (dist >= 0) & (dist <= BLOCK) & ((col >= BLOCK) | (i > 0))


def _causal_mask():
    row = lax.broadcasted_iota(jnp.int32, (BLOCK, BLOCK), 0)
    col = lax.broadcasted_iota(jnp.int32, (BLOCK, BLOCK), 1)
    return row >= col


def _pair_views(col0):
    return [pl.BlockSpec((SEQ, PAIR), lambda hp, g=g: (0, col0 // PAIR + g * (ATTN_WIDTH // PAIR) + hp))
            for g in range(len(DILATIONS))]


def _project_in(x, wg, cos_f, sin_s):
    ns = wg.shape[2]
    tiles = ns // PAIR

    def body(x_ref, w_ref, cf_ref, ss_ref, o_ref):
        shard = pl.program_id(1)
        xb = x_ref[...].astype(BF16)
        cf, ss = cf_ref[...], ss_ref[...]

        def write(rotated, scaled):
            for t0 in range(0, tiles, 2):
                strip = jnp.dot(xb, w_ref[:, t0 * PAIR:(t0 + 2) * PAIR], preferred_element_type=F32)
                for t in (t0, t0 + 1):
                    val = strip[:, (t - t0) * PAIR:(t - t0 + 1) * PAIR]
                    if t < rotated:
                        val = _rope(val, cf, ss)
                        if t < scaled:
                            val = val * (1.0 / math.sqrt(HEAD_DIM))
                    o_ref[:, t * PAIR:(t + 1) * PAIR] = val

        for s in range(N_CHIPS):
            rotated = min(max(2 * QKV_WIDTH - s * ns, 0), ns) // PAIR
            scaled = min(max(QKV_WIDTH - s * ns, 0), ns) // PAIR

            @pl.when(shard == s)
            def _(rotated=rotated, scaled=scaled):
                write(rotated, scaled)

    table = pl.BlockSpec((FF_ROWS, PAIR), lambda i, s: (i, 0))
    return pl.pallas_call(
        body, grid=(SEQ // FF_ROWS, N_CHIPS),
        in_specs=[pl.BlockSpec((FF_ROWS, D_MODEL), lambda i, s: (i, 0)),
                  pl.BlockSpec((None, D_MODEL, ns), lambda i, s: (s, 0, 0)), table, table],
        out_specs=pl.BlockSpec((FF_ROWS, ns), lambda i, s: (i, s)), out_shape=_sds((SEQ, N_CHIPS * ns)),
        compiler_params=_cp(("parallel", "parallel")), name="project_in")(x, wg, cos_f, sin_s)


def _attention_fwd(proj):
    ng = len(DILATIONS)

    def body(*refs):
        q_refs, k_refs, v_refs = refs[:ng], refs[ng:2 * ng], refs[2 * ng:3 * ng]
        attn_ref, lse_ref = refs[3 * ng:]
        qr_refs, kr_refs = q_refs, k_refs
        first = lax.broadcasted_iota(jnp.int32, (BLOCK, PAIR), 1) < HEAD_DIM
        for g, dil in enumerate(DILATIONS):
            two_blocks = SEQ // dil > BLOCK

            def units(t, carry, g=g, dil=dil, two_blocks=two_blocks):
                picked = [_unit_pieces(t * UNIT_BATCH + j, dil) for j in range(UNIT_BATCH)]

                def tiles(ref, with_prev=False):
                    if with_prev and two_blocks:
                        return jnp.stack([jnp.concatenate([_load_tile(ref, prev, dil), _load_tile(ref, rows, dil)],
                                                          axis=0) for _, rows, prev in picked])
                    return jnp.stack([_load_tile(ref, rows, dil) for _, rows, _ in picked])

                qq = tiles(qr_refs[g]).astype(BF16)
                kk = tiles(kr_refs[g], True).astype(BF16)
                vv = tiles(v_refs[g], True).astype(BF16)
                if two_blocks:
                    valid = jnp.stack([_band_mask(i, dil) for i, _, _ in picked])
                else:
                    valid = _causal_mask()[None]
                mine = first[None]
                zero = jnp.zeros_like(qq)
                outs, lses = [], []
                for qh in (jnp.where(mine, qq, zero), jnp.where(mine, zero, qq)):
                    s = jnp.einsum("pqd,pkd->pqk", qh, kk, preferred_element_type=F32)
                    s = jnp.where(valid, s, NEG_INF)
                    m = jnp.max(s, axis=-1, keepdims=True)
                    p = jnp.exp(s - m)
                    l = jnp.sum(p, axis=-1, keepdims=True)
                    outs.append(jnp.einsum("pqk,pkd->pqd", p.astype(BF16), vv, preferred_element_type=F32) * (1.0 / l))
                    lses.append(m + jnp.log(l))
                o = jnp.where(mine, outs[0], outs[1])
                lse = jnp.where(mine, lses[0], lses[1])
                if g > 0:
                    lse_old = tiles(lse_ref)
                    m = jnp.maximum(lse_old, lse)
                    lse_new = m + jnp.log(jnp.exp(lse_old - m) + jnp.exp(lse - m))
                    o = tiles(attn_ref) * jnp.exp(lse_old - lse_new) + o * jnp.exp(lse - lse_new)
                    lse = lse_new
                for j, (_, rows, _) in enumerate(picked):
                    _store_tile(attn_ref, rows, dil, o[j])
                    _store_tile(lse_ref, rows, dil, lse[j])
                return carry

            lax.fori_loop(0, UNITS // UNIT_BATCH, units, 0)

    out = pl.BlockSpec((SEQ, PAIR), lambda hp: (0, hp))
    return pl.pallas_call(
        body, grid=(ATTN_WIDTH // PAIR,),
        in_specs=_pair_views(0) + _pair_views(QKV_WIDTH) + _pair_views(2 * QKV_WIDTH),
        out_specs=[out, out], out_shape=[_sds((SEQ, ATTN_WIDTH)), _sds((SEQ, ATTN_WIDTH))],
        compiler_params=_cp(("parallel",)), name="attention_fwd")(*([proj] * (3 * ng)))


def _attention_bwd(proj, cos_f, sin_s, d_attn, attn, lse, d_u, d_gl):
    pairs = ATTN_WIDTH // PAIR
    last = len(DILATIONS) * pairs - 1

    def accumulate(dil, qr_ref, kr_ref, v_ref, do_ref, o_ref, lse_ref, dq_acc, dk_acc, dv_acc):
        two_blocks = SEQ // dil > BLOCK
        dk_acc[...] = jnp.zeros_like(dk_acc)
        dv_acc[...] = jnp.zeros_like(dv_acc)
        nk = 2 * BLOCK if two_blocks else BLOCK
        first = lax.broadcasted_iota(jnp.int32, (BLOCK, PAIR), 1) < HEAD_DIM
        first_k = lax.broadcasted_iota(jnp.int32, (nk, PAIR), 1) < HEAD_DIM

        def units(t, carry):
            picked = [_unit_pieces(t * UNIT_BATCH + j, dil) for j in range(UNIT_BATCH)]

            def tiles(ref, with_prev=False):
                if with_prev and two_blocks:
                    return jnp.stack([jnp.concatenate([_load_tile(ref, prev, dil), _load_tile(ref, rows, dil)], axis=0)
                                      for _, rows, prev in picked])
                return jnp.stack([_load_tile(ref, rows, dil) for _, rows, _ in picked])

            qq = tiles(qr_ref).astype(BF16)
            kk = tiles(kr_ref, True).astype(BF16)
            vv = tiles(v_ref, True).astype(BF16)
            dof = tiles(do_ref)
            dd = dof * tiles(o_ref)
            lse3 = tiles(lse_ref)
            dob = dof.astype(BF16)
            if two_blocks:
                valid = jnp.stack([_band_mask(i, dil) for i, _, _ in picked])
            else:
                valid = _causal_mask()[None]
            zq, zf = jnp.zeros_like(qq), jnp.zeros_like(dd)
            dqs, dks, dvs = [], [], []
            for head in range(2):
                mine = first[None] if head == 0 else jnp.logical_not(first)[None]
                delta = jnp.sum(jnp.where(mine, dd, zf), axis=-1, keepdims=True)
                lse_h = lse3[:, :, head * HEAD_DIM:head * HEAD_DIM + 1]
                s = jnp.einsum("pqd,pkd->pqk", jnp.where(mine, qq, zq), kk, preferred_element_type=F32)
                p = jnp.where(valid, jnp.exp(s - lse_h), 0.0)
                dp = jnp.einsum("pqd,pkd->pqk", jnp.where(mine, dob, zq), vv, preferred_element_type=F32)
                ds = (p * (dp - delta)).astype(BF16)
                dqs.append(jnp.einsum("pqk,pkd->pqd", ds, kk, preferred_element_type=F32))
                dks.append(jnp.einsum("pqk,pqd->pkd", ds, qq, preferred_element_type=F32))
                dvs.append(jnp.einsum("pqk,pqd->pkd", p.astype(BF16), dob, preferred_element_type=F32))
            dq = jnp.where(first[None], dqs[0], dqs[1])
            dk = jnp.where(first_k[None], dks[0], dks[1])
            dv = jnp.where(first_k[None], dvs[0], dvs[1])
            for j, (_, rows, prev) in enumerate(picked):
                _store_tile(dq_acc, rows, dil, dq[j])
                _store_tile(dk_acc, rows, dil, dk[j, nk - BLOCK:], accumulate=True)
                _store_tile(dv_acc, rows, dil, dv[j, nk - BLOCK:], accumulate=True)
                if two_blocks:
                    _store_tile(dk_acc, prev, dil, dk[j, :BLOCK], accumulate=True)
                    _store_tile(dv_acc, prev, dil, dv[j, :BLOCK], accumulate=True)
            return carry

        lax.fori_loop(0, UNITS // UNIT_BATCH, units, 0)

    def body(qr_ref, kr_ref, v_ref, cf_ref, ss_ref, do_ref, o_ref, lse_ref, du_ref, dgl_ref, out_ref,
             dq_acc, dk_acc, dv_acc, dq_buf, dk_buf, dv_buf, sems):
        step = pl.program_id(0) * pairs + pl.program_id(1)
        tails = [pltpu.make_async_copy(du_ref, out_ref.at[:, pl.ds(3 * QKV_WIDTH, SSM_WIDTH)], sems.at[3]),
                 pltpu.make_async_copy(dgl_ref, out_ref.at[:, pl.ds(3 * QKV_WIDTH + SSM_WIDTH, 2 * D_MODEL)],
                                       sems.at[4])]

        def columns(at):
            return [pltpu.make_async_copy(
                buf, out_ref.at[:, pl.ds(pl.multiple_of(j * QKV_WIDTH + at * PAIR, PAIR), PAIR)], sems.at[j])
                for j, buf in enumerate((dq_buf, dk_buf, dv_buf))]

        @pl.when(step == 0)
        def _():
            for cp in tails:
                cp.start()

        for g, dil in enumerate(DILATIONS):
            @pl.when(pl.program_id(0) == g)
            def _(dil=dil):
                accumulate(dil, qr_ref, kr_ref, v_ref, do_ref, o_ref, lse_ref, dq_acc, dk_acc, dv_acc)

        @pl.when(step > 0)
        def _():
            for cp in columns(step - 1):
                cp.wait()

        def finish(t, carry):
            rows = pl.ds(pl.multiple_of(t * ROPE_ROWS, ROPE_ROWS), ROPE_ROWS)
            cf, ss = cf_ref[rows, :], ss_ref[rows, :]
            dq = dq_acc[rows, :] * (1.0 / math.sqrt(HEAD_DIM))
            dq_buf[rows, :] = _rope_transposed(dq, cf, ss).astype(BF16)
            dk_buf[rows, :] = _rope_transposed(dk_acc[rows, :], cf, ss).astype(BF16)
            dv_buf[rows, :] = dv_acc[rows, :].astype(BF16)
            return carry

        lax.fori_loop(0, SEQ // ROPE_ROWS, finish, 0)
        for cp in columns(step):
            cp.start()

        @pl.when(step == last)
        def _():
            for cp in columns(step) + tails:
                cp.wait()

    whole = pl.BlockSpec((SEQ, PAIR), lambda g, hp: (0, 0))
    pair = pl.BlockSpec((SEQ, PAIR), lambda g, hp: (0, hp))
    views = [pl.BlockSpec((SEQ, PAIR), lambda g, hp, c0=col0 // PAIR: (0, c0 + g * pairs + hp))
             for col0 in (0, QKV_WIDTH, 2 * QKV_WIDTH)]
    return pl.pallas_call(
        body, grid=(len(DILATIONS), pairs),
        in_specs=views + [whole, whole, pair, pair, pair, HBM_OPERAND, HBM_OPERAND],
        out_specs=HBM_OPERAND, out_shape=_sds((SEQ, IN_WIDTH), BF16),
        scratch_shapes=[pltpu.VMEM((SEQ, PAIR), F32)] * 3 + [pltpu.VMEM((SEQ, PAIR), BF16)] * 3
        + [pltpu.SemaphoreType.DMA((5,))],
        compiler_params=_cp(("arbitrary", "arbitrary")), name="attention_bwd")(
            proj, proj, proj, cos_f, sin_s, d_attn, attn, lse, d_u, d_gl)


def _cmul(ar, ai, br, bi):
    return ar * br - ai * bi, ar * bi + ai * br


def _pow256(ar, ai):
    for _ in range(8):
        ar, ai = _cmul(ar, ai, ar, ai)
    return ar, ai


def _chunk_carries(first_r, first_i, pr, pi, reverse):
    rows = lax.broadcasted_iota(jnp.int32, first_r.shape, 0)
    out_r = jnp.zeros_like(first_r)
    out_i = jnp.zeros_like(first_i)
    hr = jnp.zeros_like(first_r[0:1])
    hi = jnp.zeros_like(hr)
    order = range(SCAN_CHUNKS - 1, -1, -1) if reverse else range(SCAN_CHUNKS)
    for c in order:
        out_r = jnp.where(rows == c, hr, out_r)
        out_i = jnp.where(rows == c, hi, out_i)
        tr, ti = _cmul(pr[0:1], pi[0:1], hr, hi)
        hr = first_r[c:c + 1] + tr
        hi = first_i[c:c + 1] + ti
    return out_r, out_i


def _tile(j):
    return pl.ds(pl.multiple_of(j * SCAN_CHUNKS, SCAN_CHUNKS), SCAN_CHUNKS)


def _to_scan_rows(t):
    per = SCAN_STEPS // PHASES
    return t.reshape(PHASES, SCAN_CHUNKS, per, t.shape[1]).transpose(2, 0, 1, 3).reshape(t.shape)


def _from_scan_rows(t):
    per = SCAN_STEPS // PHASES
    return t.reshape(per, PHASES, SCAN_CHUNKS, t.shape[1]).transpose(1, 2, 0, 3).reshape(t.shape)


def _scan_in_place(hr_ref, hi_ref, a_r, a_i):
    def local(j, carry):
        tr, ti = _cmul(a_r, a_i, carry[0], carry[1])
        nr = tr + hr_ref[_tile(j), :]
        ni = ti + hi_ref[_tile(j), :]
        hr_ref[_tile(j), :] = nr
        hi_ref[_tile(j), :] = ni
        return nr, ni

    zero = jnp.zeros_like(a_r)
    last_r, last_i = lax.fori_loop(0, SCAN_STEPS, local, (zero, zero), unroll=4)
    pr, pi = _pow256(a_r, a_i)
    er, ei = _chunk_carries(last_r, last_i, pr, pi, reverse=False)

    def fix(j, carry):
        tr, ti = _cmul(carry[0], carry[1], er, ei)
        hr_ref[_tile(j), :] += tr
        hi_ref[_tile(j), :] += ti
        return _cmul(carry[0], carry[1], a_r, a_i)

    lax.fori_loop(0, SCAN_STEPS, fix, (a_r, a_i), unroll=4)
    return er, ei


def _reverse_scan_in_place(lr_ref, li_ref, hr_ref, hi_ref, er, ei, a_r, a_i):
    def local(t, carry):
        j = SCAN_STEPS - 1 - t
        tr, ti = _cmul(a_r, a_i, carry[0], carry[1])
        nr = tr + lr_ref[_tile(j), :]
        ni = ti + li_ref[_tile(j), :]
        lr_ref[_tile(j), :] = nr
        li_ref[_tile(j), :] = ni
        return nr, ni

    zero = jnp.zeros_like(a_r)
    first_r, first_i = lax.fori_loop(0, SCAN_STEPS, local, (zero, zero), unroll=4)
    pr, pi = _pow256(a_r, a_i)
    nxt_r, nxt_i = _chunk_carries(first_r, first_i, pr, pi, reverse=True)

    def accumulate(lam_r, lam_i, hp_r, hp_i, acc):
        return (acc[0] + lam_r * hp_r + lam_i * hp_i, acc[1] + lam_i * hp_r - lam_r * hp_i)

    def fix(t, carry):
        qr, qi, acc_r, acc_i = carry
        j = SCAN_STEPS - 1 - t
        tr, ti = _cmul(qr, qi, nxt_r, nxt_i)
        lam_r = lr_ref[_tile(j), :] + tr
        lam_i = li_ref[_tile(j), :] + ti
        lr_ref[_tile(j), :] = lam_r
        li_ref[_tile(j), :] = lam_i
        acc_r, acc_i = accumulate(lam_r, lam_i, hr_ref[_tile(j - 1), :], hi_ref[_tile(j - 1), :], (acc_r, acc_i))
        qr, qi = _cmul(qr, qi, a_r, a_i)
        return qr, qi, acc_r, acc_i

    qr, qi, acc_r, acc_i = lax.fori_loop(0, SCAN_STEPS - 1, fix, (a_r, a_i, zero, zero), unroll=4)
    tr, ti = _cmul(qr, qi, nxt_r, nxt_i)
    lam_r = lr_ref[_tile(0), :] + tr
    lam_i = li_ref[_tile(0), :] + ti
    lr_ref[_tile(0), :] = lam_r
    li_ref[_tile(0), :] = lam_i
    acc_r, acc_i = accumulate(lam_r, lam_i, er, ei, (acc_r, acc_i))
    return jnp.sum(acc_r, axis=0, keepdims=True), jnp.sum(acc_i, axis=0, keepdims=True)


def _rope_tables():
    half = HEAD_DIM // 2
    inv_freq = ROPE_THETA ** (-jnp.arange(half, dtype=F32) / half)
    ang = jnp.arange(SEQ, dtype=F32)[:, None] * inv_freq[None, :]
    cos, sin = jnp.cos(ang), jnp.sin(ang)
    cos_f = jnp.concatenate([cos, cos, cos, cos], axis=1)
    sin_s = jnp.concatenate([-sin, sin, -sin, sin], axis=1)
    return cos_f, sin_s


def _ssm_discretise(a_re, a_im, log_dt, b_re, b_im):
    lam = lax.complex(a_re, a_im)
    dt = jnp.exp(log_dt)[:, None]
    a_bar = jnp.exp(lam * dt)
    b_bar = ((a_bar - 1.0) / lam)[..., None] * lax.complex(b_re, b_im)
    return a_bar.real, a_bar.imag, b_bar.real, b_bar.imag


SSM_SLABS = 4
SLAB_GROUPS = SSM_GROUPS // SSM_SLABS
SLAB_IN = SSM_WIDTH // SSM_SLABS
SLAB_STATE = SSM_LANES // SSM_SLABS


def _slab_block_diag(blocks):
    _, r, c = blocks.shape
    eye = jnp.eye(SLAB_GROUPS, dtype=blocks.dtype)
    b5 = blocks.reshape(SSM_SLABS, SLAB_GROUPS, r, 1, c) * eye[None, :, None, :, None]
    return b5.reshape(SSM_SLABS, SLAB_GROUPS * r, SLAB_GROUPS * c)


def _diag_blocks(a, b):
    ra, cb = a.shape[1], b.shape[1]
    wa, wb = ra // SLAB_GROUPS, cb // SLAB_GROUPS
    d = lax.dot_general(a, b, (((0,), (0,)), ((), ())), preferred_element_type=F32)
    row_g = jnp.right_shift(lax.broadcasted_iota(jnp.int32, (ra, cb), 0), wa.bit_length() - 1)
    col_g = jnp.right_shift(lax.broadcasted_iota(jnp.int32, (ra, cb), 1), wb.bit_length() - 1)
    d = jnp.where(row_g == col_g, d, 0.0)
    fold = (jnp.bitwise_and(lax.broadcasted_iota(jnp.int32, (cb, wb), 0), wb - 1)
            == lax.broadcasted_iota(jnp.int32, (cb, wb), 1)).astype(F32)
    return jnp.dot(d, fold, preferred_element_type=F32, precision=lax.Precision.HIGHEST)


def _slab_specs():
    tok = pl.BlockSpec((SEQ, SLAB_IN), lambda j: (0, j))
    state = pl.BlockSpec((SEQ, SLAB_STATE), lambda j: (0, j))
    b_in = pl.BlockSpec((None, SLAB_IN, SLAB_STATE), lambda j: (j, 0, 0))
    c_out = pl.BlockSpec((None, SLAB_STATE, SLAB_IN), lambda j: (j, 0, 0))
    vec = pl.BlockSpec((1, SLAB_STATE), lambda j: (0, j))
    ent = pl.BlockSpec((SCAN_CHUNKS, SLAB_STATE), lambda j: (0, j))
    return tok, state, b_in, c_out, vec, ent


def _ssm_forward(u, b_in_r, b_in_i, c_out_r, c_out_ni, a_r, a_i):
    def body(u_ref, br_ref, bi_ref, cr_ref, ci_ref, ar_ref, ai_ref, y_ref, hr_ref, hi_ref, er_ref, ei_ref):
        uu = u_ref[...]
        hr_ref[...] = jnp.dot(uu, br_ref[...], preferred_element_type=F32)
        hi_ref[...] = jnp.dot(uu, bi_ref[...], preferred_element_type=F32)
        a_re = jnp.broadcast_to(ar_ref[...], (SCAN_CHUNKS, SLAB_STATE))
        a_im = jnp.broadcast_to(ai_ref[...], (SCAN_CHUNKS, SLAB_STATE))
        er_ref[...], ei_ref[...] = _scan_in_place(hr_ref, hi_ref, a_re, a_im)
        y_ref[...] = (jnp.dot(hr_ref[...].astype(BF16), cr_ref[...], preferred_element_type=F32)
                      + jnp.dot(hi_ref[...].astype(BF16), ci_ref[...], preferred_element_type=F32))

    tok, state, b_in, c_out, vec, ent = _slab_specs()
    return pl.pallas_call(
        body, grid=(SSM_SLABS,), in_specs=[tok, b_in, b_in, c_out, c_out, vec, vec],
        out_specs=[tok, state, state, ent, ent],
        out_shape=[_sds((SEQ, SSM_WIDTH)), _sds((SEQ, SSM_LANES)), _sds((SEQ, SSM_LANES)),
                   _sds((SCAN_CHUNKS, SSM_LANES)), _sds((SCAN_CHUNKS, SSM_LANES))],
        compiler_params=_cp(("parallel",)), name="ssm_forward")(u, b_in_r, b_in_i, c_out_r, c_out_ni, a_r, a_i)


def _ssm_backward(d_y, d_u_skip, u, h_r, h_i, e_r, e_i, b_in_r, b_in_i, c_out_r, c_out_ni, a_r, a_i):
    def body(dy_ref, skip_ref, u_ref, hr_ref, hi_ref, er_ref, ei_ref, br_ref, bi_ref, cr_ref, ci_ref, ar_ref, ai_ref,
             du_ref, dar_ref, dai_ref, dcr_ref, dci_ref, dbr_ref, dbi_ref, lr_ref, li_ref):
        dy = dy_ref[...]
        lr_ref[...] = _dot_nt(dy, cr_ref[...])
        li_ref[...] = _dot_nt(dy, ci_ref[...])
        a_re = jnp.broadcast_to(ar_ref[...], (SCAN_CHUNKS, SLAB_STATE))
        a_im = -jnp.broadcast_to(ai_ref[...], (SCAN_CHUNKS, SLAB_STATE))
        dar_ref[...], dai_ref[...] = _reverse_scan_in_place(lr_ref, li_ref, hr_ref, hi_ref, er_ref[...], ei_ref[...],
                                                            a_re, a_im)
        dcr_ref[...] = _diag_blocks(dy, hr_ref[...].astype(BF16))
        dci_ref[...] = _diag_blocks(dy, hi_ref[...].astype(BF16))
        lam_r, lam_i = lr_ref[...].astype(BF16), li_ref[...].astype(BF16)
        uu = u_ref[...]
        dbr_ref[...] = _diag_blocks(uu, lam_r)
        dbi_ref[...] = _diag_blocks(uu, lam_i)
        du = skip_ref[...] + _dot_nt(lam_r, br_ref[...]) + _dot_nt(lam_i, bi_ref[...])
        du_ref[...] = du.astype(BF16)

    tok, state, b_in, c_out, vec, ent = _slab_specs()
    db = pl.BlockSpec((SLAB_IN, SSM_STATE), lambda j: (j, 0))
    return pl.pallas_call(
        body, grid=(SSM_SLABS,), in_specs=[tok, tok, tok, state, state, ent, ent, b_in, b_in, c_out, c_out, vec, vec],
        out_specs=[tok, vec, vec, db, db, db, db],
        out_shape=[_sds((SEQ, SSM_WIDTH), BF16), _sds((1, SSM_LANES)), _sds((1, SSM_LANES))]
        + [_sds((SSM_WIDTH, SSM_STATE))] * 4,
        scratch_shapes=[pltpu.VMEM((SEQ, SLAB_STATE), F32)] * 2,
        compiler_params=_cp(("parallel",)), name="ssm_backward")(
            d_y, d_u_skip, u, h_r, h_i, e_r, e_i, b_in_r, b_in_i, c_out_r, c_out_ni, a_r, a_i)


FF_ROWS = 1024
FF_SHARD = D_FF // N_CHIPS


def _dot_nt(a, b):
    return lax.dot_general(a, b, (((1,), (1,)), ((), ())), preferred_element_type=F32)


def _ffn_up(h, w_gate_t, w_up_t):
    def body(h_ref, wg_ref, wu_ref, a_ref, b_ref, act_ref):
        hb = h_ref[...].astype(BF16)
        a = _dot_nt(hb, wg_ref[...])
        b = _dot_nt(hb, wu_ref[...])
        a_ref[...] = a
        b_ref[...] = b
        act_ref[...] = (a * jax.nn.sigmoid(a) * b).astype(BF16)

    w_spec = pl.BlockSpec((None, FF_SHARD, D_MODEL), lambda i, k: (k, 0, 0))
    o_spec = pl.BlockSpec((None, FF_ROWS, FF_SHARD), lambda i, k: (k, i, 0))
    shape = (N_CHIPS, SEQ, FF_SHARD)
    return pl.pallas_call(
        body, grid=(SEQ // FF_ROWS, N_CHIPS),
        in_specs=[pl.BlockSpec((FF_ROWS, D_MODEL), lambda i, k: (i, 0)), w_spec, w_spec],
        out_specs=[o_spec, o_spec, o_spec], out_shape=[_sds(shape), _sds(shape), _sds(shape, BF16)],
        compiler_params=_cp(("parallel", "parallel")), name="ffn_up")(h, w_gate_t, w_up_t)


def _ffn_down_ln2_loss(act, w_down, h, tgt, ln_g, ln_b):
    def body(act_ref, w_ref, h_ref, tgt_ref, g_ref, b_ref, dz_ref, loss_ref, dg_ref, db_ref, acc):
        i, k = pl.program_id(0), pl.program_id(1)
        part = jnp.dot(act_ref[...], w_ref[...], preferred_element_type=F32)

        @pl.when(k == 0)
        def _():
            acc[...] = part

        @pl.when(k > 0)
        def _():
            acc[...] += part

        @pl.when(k == N_CHIPS - 1)
        def _():
            g = g_ref[...]
            xhat, rstd = _ln_stats(DN_ALPHA * h_ref[...] + acc[...])
            err = xhat * g + b_ref[...] - tgt_ref[...]
            d_out = err * (1.0 / D_MODEL)
            dz_ref[...] = _ln_bwd(d_out, xhat, rstd, g)
            loss_rows = jnp.sum(err * err, axis=-1, keepdims=True) * (0.5 / D_MODEL)
            sums = (jnp.broadcast_to(jnp.sum(loss_rows, axis=0, keepdims=True), loss_ref.shape),
                    _colsum(d_out * xhat), _colsum(d_out))
            for ref, val in zip((loss_ref, dg_ref, db_ref), sums):
                @pl.when(i == 0)
                def _(ref=ref, val=val):
                    ref[...] = val

                @pl.when(i > 0)
                def _(ref=ref, val=val):
                    ref[...] += val

    row = pl.BlockSpec((FF_ROWS, D_MODEL), lambda i, k: (i, 0))
    vec = pl.BlockSpec((1, D_MODEL), lambda i, k: (0, 0))
    return pl.pallas_call(
        body, grid=(SEQ // FF_ROWS, N_CHIPS),
        in_specs=[pl.BlockSpec((None, FF_ROWS, FF_SHARD), lambda i, k: (k, i, 0)),
                  pl.BlockSpec((None, FF_SHARD, D_MODEL), lambda i, k: (k, 0, 0)), row, row, vec, vec],
        out_specs=[row, pl.BlockSpec((1, BLOCK), lambda i, k: (0, 0)), vec, vec],
        out_shape=[_sds((SEQ, D_MODEL)), _sds((1, BLOCK)), _sds((1, D_MODEL)), _sds((1, D_MODEL))],
        scratch_shapes=[pltpu.VMEM((FF_ROWS, D_MODEL), F32)],
        compiler_params=_cp(("arbitrary", "arbitrary")), name="ffn_down_ln2_loss")(act, w_down, h, tgt, ln_g, ln_b)


def _ffn_down_bwd(dz, w_down, a, b):
    def body(dz_ref, wd_ref, a_ref, b_ref, da_ref, db_ref):
        d_act = _dot_nt(dz_ref[...].astype(BF16), wd_ref[...])
        av = a_ref[...]
        sg = jax.nn.sigmoid(av)
        da_ref[...] = (d_act * b_ref[...] * sg * (1.0 + av * (1.0 - sg))).astype(BF16)
        db_ref[...] = (d_act * av * sg).astype(BF16)

    t_spec = pl.BlockSpec((None, FF_ROWS, FF_SHARD), lambda i, k: (k, i, 0))
    shape = (N_CHIPS, SEQ, FF_SHARD)
    return pl.pallas_call(
        body, grid=(SEQ // FF_ROWS, N_CHIPS),
        in_specs=[pl.BlockSpec((FF_ROWS, D_MODEL), lambda i, k: (i, 0)),
                  pl.BlockSpec((None, FF_SHARD, D_MODEL), lambda i, k: (k, 0, 0)), t_spec, t_spec],
        out_specs=[t_spec, t_spec], out_shape=[_sds(shape, BF16), _sds(shape, BF16)],
        compiler_params=_cp(("parallel", "parallel")), name="ffn_down_bwd")(dz, w_down, a, b)


def _ffn_dh(d_a, d_b, w_gate_t, w_up_t):
    def body(da_ref, db_ref, wg_ref, wu_ref, o_ref, acc):
        k = pl.program_id(1)
        part = (jnp.dot(da_ref[...], wg_ref[...], preferred_element_type=F32)
                + jnp.dot(db_ref[...], wu_ref[...], preferred_element_type=F32))

        @pl.when(k == 0)
        def _():
            acc[...] = part

        @pl.when(k > 0)
        def _():
            acc[...] += part

        @pl.when(k == N_CHIPS - 1)
        def _():
            o_ref[...] = acc[...]

    t_spec = pl.BlockSpec((None, FF_ROWS, FF_SHARD), lambda i, k: (k, i, 0))
    w_spec = pl.BlockSpec((None, FF_SHARD, D_MODEL), lambda i, k: (k, 0, 0))
    return pl.pallas_call(
        body, grid=(SEQ // FF_ROWS, N_CHIPS), in_specs=[t_spec, t_spec, w_spec, w_spec],
        out_specs=pl.BlockSpec((FF_ROWS, D_MODEL), lambda i, k: (i, 0)), out_shape=_sds((SEQ, D_MODEL)),
        scratch_shapes=[pltpu.VMEM((FF_ROWS, D_MODEL), F32)],
        compiler_params=_cp(("parallel", "arbitrary")), name="ffn_dh")(d_a, d_b, w_gate_t, w_up_t)


def _local_step(x, tgt, wts, small, ffn_grads, mixer_grads, small_grads):
    s = SEQ
    cos_f, sin_s = [_to_phase_rows(t) for t in _rope_tables()]
    x = _reorder_rows(x, to_phase=True, name="phase_rows_x")
    tgt = _reorder_rows(tgt, to_phase=True, name="phase_rows_target")

    proj = _project_in(x, wts["w_in"], cos_f, sin_s)

    attn, lse = _attention_fwd(proj)

    (abar_r, abar_i, bbar_r, bbar_i), ssm_vjp = jax.vjp(
        _ssm_discretise, small["ssm_a_re"], small["ssm_a_im"], small["ssm_log_dt"], small["ssm_b_re"], small["ssm_b_im"])
    b_in_r, b_in_i = [_slab_block_diag(b.transpose(0, 2, 1)).astype(BF16) for b in (bbar_r, bbar_i)]
    c_out_r = _slab_block_diag(small["ssm_c_re"].transpose(0, 2, 1)).astype(BF16)
    c_out_ni = _slab_block_diag(-small["ssm_c_im"].transpose(0, 2, 1)).astype(BF16)
    a_r, a_i = abar_r.reshape(1, SSM_LANES), abar_i.reshape(1, SSM_LANES)
    d_skip = small["ssm_d"].reshape(1, SSM_WIDTH)

    u_f = _to_scan_rows(proj[:, 3 * QKV_WIDTH:3 * QKV_WIDTH + SSM_WIDTH])
    u_p = u_f.astype(BF16)
    y_c, h_r, h_i, e_r, e_i = _ssm_forward(u_p, b_in_r, b_in_i, c_out_r, c_out_ni, a_r, a_i)

    def branch(t, wg):
        return jnp.concatenate([jnp.dot(t, wg[k], preferred_element_type=F32) for k in range(N_CHIPS)], axis=1)

    def branch_t(t, wg):
        ns = wg.shape[2]
        return sum(_dot_nt(t[:, k * ns:(k + 1) * ns], wg[k]) for k in range(N_CHIPS))

    def gelu_glu(yc, u, dsk, wg):
        y = yc + dsk * u
        gel = (0.5 * y * (1.0 + jnp.tanh(GELU_C * (y + GELU_K * y * y * y)))).astype(BF16)
        glu = branch(gel, wg)
        return y, gel, glu, glu[:, :SSM_WIDTH] * jax.nn.sigmoid(glu[:, SSM_WIDTH:])

    y_s5, gel, glu, y_glu = _rowwise(
        gelu_glu, [y_c, u_f], [d_skip, wts["w_glu"]],
        [_sds((s, SSM_WIDTH)), _sds((s, SSM_WIDTH), BF16), _sds((s, 2 * SSM_WIDTH)), _sds((s, SSM_WIDTH), BF16)],
        tm=512, name="ssm_gelu_glu")
    y_glu = _from_scan_rows(y_glu)

    gl0 = (proj, D_MODEL, (3 * QKV_WIDTH + SSM_WIDTH) // D_MODEL)
    gl1 = (proj, D_MODEL, (3 * QKV_WIDTH + SSM_WIDTH) // D_MODEL + 1)
    b_gate = small["b_gate"]
    w_out = wts["w_out"].reshape(D_MODEL, D_MODEL)

    def mix_ln1(l0, l1, at, yg, xv, bg, wa, ws, wo, g, b):
        ya = branch(at.astype(BF16), wa)
        ys = branch(yg, ws)
        mixed = (jax.nn.sigmoid(l0 + bg[0:1]) * ya + jax.nn.sigmoid(l1 + bg[1:2]) * ys).astype(BF16)
        z = DN_ALPHA * xv + jnp.dot(mixed, wo, preferred_element_type=F32)
        xhat, _ = _ln_stats(z)
        return ya, ys, mixed, z, xhat * g + b

    y_attn, y_ssm, mixed, z1, h = _rowwise(
        mix_ln1, [gl0, gl1, attn, y_glu, x],
        [b_gate, wts["w_attn_br"], wts["w_ssm_br"], w_out, small["ln1_g"], small["ln1_b"]],
        [_sds((s, D_MODEL)), _sds((s, D_MODEL)), _sds((s, D_MODEL), BF16), _sds((s, D_MODEL)), _sds((s, D_MODEL))],
        tm=256, name="mix_ln1")

    nf = D_FF // N_CHIPS
    w_gate_t, w_up_t, w_down = wts["w_ff_gate"], wts["w_ff_up"], wts["w_ff_down"]
    ff_a, ff_b, act = _ffn_up(h, w_gate_t, w_up_t)
    dz2, loss_v, d_ln2_g, d_ln2_b = _ffn_down_ln2_loss(act, w_down, h, tgt, small["ln2_g"], small["ln2_b"])

    d_a, d_b = _ffn_down_bwd(dz2, w_down, ff_a, ff_b)

    def grad_rows(lhs, rhs, name):
        return _matmul(lhs, rhs, grid=(N_CHIPS,), a_spec=pl.BlockSpec((None, s, nf), lambda k: (k, 0, 0)),
                       b_spec=pl.BlockSpec((s, D_MODEL), lambda k: (0, 0)),
                       o_spec=pl.BlockSpec((None, nf, D_MODEL), lambda k: (k, 0, 0)),
                       out_shape=_sds((N_CHIPS, nf, D_MODEL), BF16), dims=(0, 0), name=name)

    g_w_ff_down = grad_rows(act, dz2, "g_w_ff_down")
    g_w_ff_gate = grad_rows(d_a, h, "g_w_ff_gate")
    g_w_ff_up = grad_rows(d_b, h, "g_w_ff_up")
    dh_ff = _ffn_dh(d_a, d_b, w_gate_t, w_up_t)

    def ln1_gate_bwd(dz, dff, z, l0, l1, ya, ys, g, bg, wo, wa, ws):
        xhat, rstd = _ln_stats(z)
        dh = DN_ALPHA * dz + dff
        dz_in = _ln_bwd(dh, xhat, rstd, g)
        dm = _dot_nt(dz_in.astype(BF16), wo)
        g0 = jax.nn.sigmoid(l0 + bg[0:1])
        g1 = jax.nn.sigmoid(l1 + bg[1:2])
        dl0 = dm * ya * g0 * (1.0 - g0)
        dl1 = dm * ys * g1 * (1.0 - g1)
        dya, dys = (dm * g0).astype(BF16), (dm * g1).astype(BF16)
        return (dz_in, dya, dys, jnp.concatenate([dl0, dl1], axis=1), branch_t(dya, wa), branch_t(dys, ws),
                _colsum(dh * xhat), _colsum(dh), _colsum(dl0), _colsum(dl1))

    dz1, d_y_attn, d_y_ssm, d_gl, d_attn, d_y_glu, d_ln1_g, d_ln1_b, d_bg0, d_bg1 = _rowwise(
        ln1_gate_bwd, [dz2, dh_ff, z1, gl0, gl1, y_attn, y_ssm],
        [small["ln1_g"], b_gate, w_out, wts["w_attn_br"], wts["w_ssm_br"]],
        [_sds((s, D_MODEL)), _sds((s, D_MODEL), BF16), _sds((s, D_MODEL), BF16), _sds((s, 2 * D_MODEL), BF16),
         _sds((s, ATTN_WIDTH)), _sds((s, SSM_WIDTH))],
        [_sds((1, D_MODEL))] * 4, tm=256, name="ln1_gate_bwd", after=(g_w_ff_down, g_w_ff_gate, g_w_ff_up))
    ffn_sent = ffn_grads({"w_ff_down": g_w_ff_down, "w_ff_gate": g_w_ff_gate, "w_ff_up": g_w_ff_up}, dz1)
    g_w_out = _mm_rows_tn(mixed, dz1, name="g_w_out")

    g_w_ssm_br = _mm_cols_tn(y_glu, d_y_ssm, ns=D_MODEL // N_CHIPS, name="g_w_ssm_br")
    d_y_glu = _to_scan_rows(d_y_glu)

    def glu_gelu_bwd(dyg, gl, y, u, dsk, wg):
        ga, gb = gl[:, :SSM_WIDTH], gl[:, SSM_WIDTH:]
        sg = jax.nn.sigmoid(gb)
        d_gl = jnp.concatenate([dyg * sg, dyg * ga * sg * (1.0 - sg)], axis=1).astype(BF16)
        dg = branch_t(d_gl, wg)
        th = jnp.tanh(GELU_C * (y + GELU_K * y * y * y))
        dy = dg * (0.5 * (1.0 + th) + 0.5 * y * (1.0 - th * th) * GELU_C * (1.0 + 3.0 * GELU_K * y * y))
        return d_gl, dy, dy * dsk, _colsum(dy * u)

    d_glu, d_y, d_u_skip, d_ssm_d = _rowwise(
        glu_gelu_bwd, [d_y_glu, glu, y_s5, u_f], [d_skip, wts["w_glu"]],
        [_sds((s, 2 * SSM_WIDTH), BF16), _sds((s, SSM_WIDTH), BF16), _sds((s, SSM_WIDTH))], [_sds((1, SSM_WIDTH))],
        tm=512, name="glu_gelu_bwd", after=tuple(ffn_sent))
    g_w_glu = _mm_cols_tn(gel, d_glu, ns=2 * SSM_WIDTH // N_CHIPS, name="g_w_glu")
    d_u, d_abar_r, d_abar_i, d_c_r, d_c_ni, d_bin_r, d_bin_i = _ssm_backward(
        d_y, d_u_skip, u_p, h_r, h_i, e_r, e_i, b_in_r, b_in_i, c_out_r, c_out_ni, a_r, a_i)
    d_u = _from_scan_rows(d_u)
    d_bbar_r = d_bin_r.reshape(SSM_GROUPS, SSM_GROUP, SSM_STATE).transpose(0, 2, 1)
    d_bbar_i = d_bin_i.reshape(SSM_GROUPS, SSM_GROUP, SSM_STATE).transpose(0, 2, 1)
    d_a_re, d_a_im, d_log_dt, d_b_re, d_b_im = ssm_vjp(
        (d_abar_r.reshape(SSM_GROUPS, SSM_STATE), d_abar_i.reshape(SSM_GROUPS, SSM_STATE), d_bbar_r, d_bbar_i))
    d_c_re = d_c_r.reshape(SSM_GROUPS, SSM_GROUP, SSM_STATE)
    d_c_im = -d_c_ni.reshape(SSM_GROUPS, SSM_GROUP, SSM_STATE)

    g_w_attn_br = _mm_cols_tn(attn, d_y_attn, ns=D_MODEL // N_CHIPS, name="g_w_attn_br")
    mixer_grads({"w_out": g_w_out, "w_ssm_br": g_w_ssm_br, "w_glu": g_w_glu, "w_attn_br": g_w_attn_br}, d_abar_r)
    small_g = {"b_gate": jnp.concatenate([d_bg0, d_bg1], axis=0), "ssm_a_re": d_a_re, "ssm_a_im": d_a_im,
               "ssm_log_dt": d_log_dt, "ssm_b_re": d_b_re, "ssm_b_im": d_b_im, "ssm_c_re": d_c_re, "ssm_c_im": d_c_im,
               "ssm_d": d_ssm_d.reshape(SSM_WIDTH), "ln1_g": d_ln1_g, "ln1_b": d_ln1_b, "ln2_g": d_ln2_g,
               "ln2_b": d_ln2_b}
    shared = small_grads(small_g, loss_v[0, 0])
    d_proj = _attention_bwd(proj, cos_f, sin_s, d_attn, attn, lse, d_u, d_gl)

    g_w_in = _mm_cols_tn(x, d_proj, ns=IN_WIDTH // N_CHIPS, name="g_w_in", after=tuple(shared))

    def grad_x_after(after):
        dx_proj = _mm_cols_nt(d_proj, wts["w_in"], tm=1024, name="dx_proj", after=after)
        return _reorder_rows(dz1, dx_proj, to_phase=False, name="grad_x", scale=DN_ALPHA)

    return grad_x_after, g_w_in, d_proj


GATHER_ID, SWAP_ID, SCATTER_ID, JOIN_ID, EXCHANGE_ID = 1, 2, 3, 4, 5


def _place():
    return lax.axis_index("x"), lax.axis_index("y"), lax.axis_index("c")


def _other_chips(x, y):
    return [(1 - x, y), (x, 1 - y), (1 - x, 1 - y)]


def _handshake(peers):
    barrier = pltpu.get_barrier_semaphore()
    for peer in peers:
        pl.semaphore_signal(barrier, inc=1, device_id=peer, device_id_type=MESH)
    pl.semaphore_wait(barrier, len(peers))


def _sequencer(body, arrays, out_type, sems, collective_id, name):
    return pl.kernel(body, name=name, out_type=out_type,
                     mesh=plsc.ScalarSubcoreMesh(axis_name="sequencer", num_cores=1), scratch_types=sems,
                     compiler_params=pltpu.CompilerParams(collective_id=collective_id))(*arrays)


def _gather_weights(shards, *, name):
    nw = len(shards)

    def body(*refs):
        ins, outs = refs[:nw], refs[nw:2 * nw]
        send_sems, recv_sems, pass_send, pass_recv, local_sems = refs[2 * nw:]
        x, y, c = _place()
        chip = 2 * x + y
        chips = _other_chips(x, y)
        _handshake([(x, y, 1 - c)] + [(cx, cy, c) for cx, cy in chips])
        started = []
        for w in range(nw):
            hw = shards[w].shape[0] // 2
            mine = pl.ds(c * hw, hw)
            own = pltpu.make_async_copy(ins[w], outs[w].at[chip], local_sems.at[w])
            own.start()
            started.append(own)
            for j, (cx, cy) in enumerate(chips):
                cp = pltpu.make_async_remote_copy(
                    src_ref=ins[w].at[mine], dst_ref=outs[w].at[chip, mine], send_sem=send_sems.at[w, j],
                    recv_sem=recv_sems.at[w, j], device_id=(cx, cy, c), device_id_type=MESH)
                cp.start()
                started.append(cp)
        passed = []
        for w in range(nw):
            hw = shards[w].shape[0] // 2
            mine = pl.ds(c * hw, hw)
            for j, (cx, cy) in enumerate(chips):
                landed = outs[w].at[2 * cx + cy, mine]
                pltpu.make_async_remote_copy(
                    src_ref=ins[w].at[mine], dst_ref=landed, send_sem=send_sems.at[w, j],
                    recv_sem=recv_sems.at[w, j], device_id=(cx, cy, c), device_id_type=MESH).wait_recv()
                cp = pltpu.make_async_remote_copy(
                    src_ref=landed, dst_ref=landed, send_sem=pass_send.at[w, j], recv_sem=pass_recv.at[w, j],
                    device_id=(x, y, 1 - c), device_id_type=MESH)
                cp.start()
                passed.append(cp)
        for w in range(nw):
            hw = shards[w].shape[0] // 2
            theirs = pl.ds((1 - c) * hw, hw)
            for j, (cx, cy) in enumerate(chips):
                landed = outs[w].at[2 * cx + cy, theirs]
                pltpu.make_async_remote_copy(
                    src_ref=landed, dst_ref=landed, send_sem=pass_send.at[w, j], recv_sem=pass_recv.at[w, j],
                    device_id=(x, y, 1 - c), device_id_type=MESH).wait_recv()
        for cp in started[0::4]:
            cp.wait()
        for cp in [s for i, s in enumerate(started) if i % 4] + passed:
            cp.wait_send()

    sem = pltpu.SemaphoreType.DMA
    return _sequencer(body, shards, [_sds((N_CHIPS,) + a.shape, a.dtype) for a in shards],
                      [sem((nw, 3)), sem((nw, 3)), sem((nw, 3)), sem((nw, 3)), sem((nw,))], GATHER_ID, name)


def _swap_other_halves(grads, *, name):
    nw = len(grads)

    def body(*refs):
        ins, outs = refs[:nw], refs[nw:2 * nw]
        send_sems, recv_sems = refs[2 * nw:]
        x, y, c = _place()
        _handshake([(x, y, 1 - c)])
        cps = []
        for w in range(nw):
            hw = grads[w].shape[1] // 2
            cp = pltpu.make_async_remote_copy(
                src_ref=ins[w].at[:, pl.ds((1 - c) * hw, hw)], dst_ref=outs[w], send_sem=send_sems.at[w],
                recv_sem=recv_sems.at[w], device_id=(x, y, 1 - c), device_id_type=MESH)
            cp.start()
            cps.append(cp)
        for cp in cps:
            cp.wait()

    sem = pltpu.SemaphoreType.DMA
    return _sequencer(body, grads, [_sds((N_CHIPS, g.shape[1] // 2, g.shape[2]), g.dtype) for g in grads],
                      [sem((nw,)), sem((nw,))], SWAP_ID, name)


def _add_my_halves(core, grads, others, *, name, after=()):
    nw = len(grads)
    halves = [g.shape[1] // 2 for g in grads]

    def body(core_ref, *refs):
        outs = refs[2 * nw + len(after):]
        for g_ref, o_ref, out_ref in zip(refs[:nw], refs[nw:2 * nw], outs):
            out_ref[...] = (g_ref[...].astype(F32) + o_ref[...].astype(F32)).astype(out_ref.dtype)

    in_specs = [pl.BlockSpec((None, None, hw, g.shape[2]), lambda s, core_ref: (s, core_ref[0], 0, 0))
                for g, hw in zip(grads, halves)]
    in_specs += [pl.BlockSpec((None, hw, g.shape[2]), lambda s, core_ref: (s, 0, 0)) for g, hw in zip(grads, halves)]
    return pl.pallas_call(
        body,
        grid_spec=pltpu.PrefetchScalarGridSpec(
            num_scalar_prefetch=1, grid=(N_CHIPS,), in_specs=in_specs + [HBM_OPERAND] * len(after),
            out_specs=[pl.BlockSpec((None, hw, g.shape[2]), lambda s, core_ref: (s, 0, 0))
                       for g, hw in zip(grads, halves)]),
        out_shape=[_sds((N_CHIPS, hw, g.shape[2]), BF16) for g, hw in zip(grads, halves)],
        compiler_params=_cp(("parallel",)), name=name)(
            core, *[g.reshape(N_CHIPS, 2, hw, g.shape[2]) for g, hw in zip(grads, halves)], *others, *after)


def _scatter_partials(parts, *, name):
    nw = len(parts)

    def body(*refs):
        ins, outs = refs[:nw], refs[nw:2 * nw]
        send_sems, recv_sems = refs[2 * nw:]
        x, y, c = _place()
        _handshake([(cx, cy, c) for cx, cy in _other_chips(x, y)])
        cps = []
        for w in range(nw):
            for j, (cx, cy) in enumerate(_other_chips(x, y)):
                cp = pltpu.make_async_remote_copy(
                    src_ref=ins[w].at[2 * cx + cy], dst_ref=outs[w].at[j], send_sem=send_sems.at[w, j],
                    recv_sem=recv_sems.at[w, j], device_id=(cx, cy, c), device_id_type=MESH)
                cp.start()
                cps.append(cp)
        for cp in cps:
            cp.wait()

    sem = pltpu.SemaphoreType.DMA
    return _sequencer(body, parts, [_sds((3,) + p.shape[1:], p.dtype) for p in parts],
                      [sem((nw, 3)), sem((nw, 3))], SCATTER_ID, name)


SUM_STEPS = 2


def _sum_partials(chip, parts, recvd, *, name, after=()):
    nw = len(parts)
    rows = [p.shape[1] // SUM_STEPS for p in parts]

    def body(chip_ref, *refs):
        outs = refs[2 * nw + len(after):]
        for p_ref, r_ref, out_ref in zip(refs[:nw], refs[nw:2 * nw], outs):
            acc = p_ref[...].astype(F32)
            for j in range(3):
                acc = acc + r_ref[j].astype(F32)
            out_ref[...] = acc

    in_specs = [pl.BlockSpec((None, th, p.shape[2]), lambda i, chip_ref: (chip_ref[0], i, 0))
                for p, th in zip(parts, rows)]
    in_specs += [pl.BlockSpec((3, th, p.shape[2]), lambda i, chip_ref: (0, i, 0)) for p, th in zip(parts, rows)]
    return pl.pallas_call(
        body,
        grid_spec=pltpu.PrefetchScalarGridSpec(
            num_scalar_prefetch=1, grid=(SUM_STEPS,), in_specs=in_specs + [HBM_OPERAND] * len(after),
            out_specs=[pl.BlockSpec((th, p.shape[2]), lambda i, chip_ref: (i, 0)) for p, th in zip(parts, rows)]),
        out_shape=[_sds(p.shape[1:]) for p in parts], compiler_params=_cp(("parallel",)), name=name)(
            chip, *parts, *recvd, *after)


def _swap_reduced_halves(halves, *, name):
    nw = len(halves)

    def body(*refs):
        ins, outs = refs[:nw], refs[nw:2 * nw]
        send_sems, recv_sems = refs[2 * nw:]
        x, y, c = _place()
        _handshake([(x, y, 1 - c)])
        cps = []
        for w in range(nw):
            cp = pltpu.make_async_remote_copy(
                src_ref=ins[w], dst_ref=outs[w], send_sem=send_sems.at[w], recv_sem=recv_sems.at[w],
                device_id=(x, y, 1 - c), device_id_type=MESH)
            cp.start()
            cps.append(cp)
        for cp in cps:
            cp.wait()

    sem = pltpu.SemaphoreType.DMA
    return _sequencer(body, halves, [_sds(h.shape, h.dtype) for h in halves], [sem((nw,)), sem((nw,))], JOIN_ID, name)


def _exchange_rows(vec, *, name):
    def body(v_ref, slots, send_sems, recv_sems, local_sem):
        x, y, c = _place()
        me = 4 * x + 2 * y + c
        peers = []
        for mask in range(1, N_DEV):
            peers.append((1 - x if mask & 4 else x, 1 - y if mask & 2 else y, 1 - c if mask & 1 else c))
        _handshake(peers)
        own = pltpu.make_async_copy(v_ref, slots.at[me], local_sem)
        own.start()
        cps = []
        for k, peer in enumerate(peers):
            cp = pltpu.make_async_remote_copy(
                src_ref=v_ref, dst_ref=slots.at[me], send_sem=send_sems.at[k], recv_sem=recv_sems.at[k],
                device_id=peer, device_id_type=MESH)
            cp.start()
            cps.append(cp)
        for k, (px, py, pc) in enumerate(peers):
            pltpu.make_async_remote_copy(
                src_ref=v_ref, dst_ref=slots.at[4 * px + 2 * py + pc], send_sem=send_sems.at[k],
                recv_sem=recv_sems.at[k], device_id=(px, py, pc), device_id_type=MESH).wait_recv()
        for cp in cps:
            cp.wait_send()
        own.wait()

    sem = pltpu.SemaphoreType.DMA
    return _sequencer(body, [vec], [_sds((N_DEV,) + vec.shape)], [sem((N_DEV - 1,)), sem((N_DEV - 1,)), sem(())],
                      EXCHANGE_ID, name)[0]


def _sum_slots(slots, *, name, after=()):
    def body(s_ref, *rest):
        out_ref = rest[len(after)]
        acc = s_ref[0]
        for d in range(1, N_DEV):
            acc = acc + s_ref[d]
        out_ref[...] = acc

    vmem = pl.BlockSpec(memory_space=pltpu.VMEM)
    return pl.pallas_call(
        body, in_specs=[vmem] + [HBM_OPERAND] * len(after), out_specs=vmem, out_shape=_sds(slots.shape[1:]),
        compiler_params=pltpu.CompilerParams(vmem_limit_bytes=VMEM_LIMIT_BYTES), name=name)(slots, *after)


def _reduce_scatter_start(grads, core, *, tag, add_after=()):
    others = _swap_other_halves(grads, name="swap_other_halves_" + tag)
    parts = _add_my_halves(core, grads, others, name="add_my_halves_" + tag, after=add_after)
    return parts, _scatter_partials(parts, name="scatter_partials_" + tag)


def _reduce_scatter_finish(parts, recvd, chip, *, tag, sum_after=()):
    mine = _sum_partials(chip, parts, recvd, name="sum_partials_" + tag, after=sum_after)
    return mine, _swap_reduced_halves(mine, name="swap_reduced_halves_" + tag)


ADAM_BLOCK_ELEMS = 256 * 1024


def _adam_rows(rows, cols):
    tm = rows
    while tm * cols > ADAM_BLOCK_ELEMS and tm % 16 == 0:
        tm //= 2
    return tm


def _adam_step(wv, gv, mv, vv):
    m2 = ADAM_B1 * mv + (1.0 - ADAM_B1) * gv
    v2 = ADAM_B2 * vv + (1.0 - ADAM_B2) * (gv * gv)
    m_hat = m2 / (1.0 - ADAM_B1 ** ADAM_STEP)
    v_hat = v2 / (1.0 - ADAM_B2 ** ADAM_STEP)
    return -ADAM_LR * (m_hat / (jnp.sqrt(v_hat) + ADAM_EPS) + ADAM_WD * wv), m2, v2


def _adamw_each(ws, gs, ms, vs, *, name, after=()):
    n = len(ws)
    whole = pl.BlockSpec(memory_space=pltpu.VMEM)

    def body(*refs):
        ins, outs = refs[:4 * n], refs[4 * n + len(after):]
        for i in range(n):
            res = _adam_step(*(ins[k * n + i][...] for k in range(4)))
            for k in range(3):
                outs[k * n + i][...] = res[k]

    out = pl.pallas_call(body, in_specs=[whole] * (4 * n) + [HBM_OPERAND] * len(after),
                         out_shape=[_sds(w.shape) for w in ws] * 3, name=name)(*ws, *gs, *ms, *vs, *after)
    return out[:n], out[n:2 * n], out[2 * n:]


def _adamw_halves(core, w, g_mine, g_theirs, m, v, *, name, after=()):
    rows, cols = w.shape
    hw = rows // 2
    tm = _adam_rows(hw, cols)
    per_half = hw // tm

    def body(core_ref, w_ref, gm_ref, gt_ref, m_ref, v_ref, *rest):
        g_out, d_out, m_out, v_out = rest[len(after):]
        mine = (pl.program_id(0) // per_half) == core_ref[0]
        g = jnp.where(mine, gm_ref[...], gt_ref[...])
        d, m2, v2 = _adam_step(w_ref[...], g, m_ref[...], v_ref[...])
        g_out[...] = g
        d_out[...] = d
        m_out[...] = m2
        v_out[...] = v2

    full = pl.BlockSpec((tm, cols), lambda i, core_ref: (i, 0))

    def half(wanted):
        def index(i, core_ref):
            in_use = ((i // per_half) == core_ref[0]) == wanted
            return (jnp.where(in_use, i % per_half, 0), 0)
        return pl.BlockSpec((tm, cols), index)

    return pl.pallas_call(
        body,
        grid_spec=pltpu.PrefetchScalarGridSpec(
            num_scalar_prefetch=1, grid=(rows // tm,),
            in_specs=[full, half(True), half(False), full, full] + [HBM_OPERAND] * len(after),
            out_specs=[full, full, full, full]),
        out_shape=[_sds((rows, cols))] * 4, compiler_params=_cp(("parallel",)), name=name)(
            core, w, g_mine, g_theirs, m, v, *after)


HELD_TRANSPOSED = ("w_ff_gate", "w_ff_up")


def _as_rows(name, arr):
    return arr[0].T if name in HELD_TRANSPOSED else arr[0]


def _from_rows(name, arr2d):
    return (arr2d.T if name in HELD_TRANSPOSED else arr2d)[None]


STORED_SWAPPED = ("ssm_b_re", "ssm_b_im")


def _as_stored(name, arr):
    return jnp.swapaxes(arr, -1, -2) if name in STORED_SWAPPED else arr


def _pack_rows(arrs):
    flat = jnp.concatenate([a.reshape(-1).astype(F32) for a in arrs])
    rows = -(-flat.shape[0] // 1024) * 8
    return jnp.pad(flat, (0, rows * 128 - flat.shape[0])).reshape(rows, 128)


def _unpack_rows(vec, shapes):
    flat = vec.reshape(-1)
    out, off = [], 0
    for shp in shapes:
        size = math.prod(shp)
        out.append(flat[off:off + size].reshape(shp))
        off += size
    return out


SMALL = ("b_gate", "ssm_a_re", "ssm_a_im", "ssm_log_dt", "ssm_b_re", "ssm_b_im", "ssm_c_re", "ssm_c_im", "ssm_d",
         "ln1_g", "ln1_b", "ln2_g", "ln2_b")
GATHER_GROUPS = (("w_in", ("w_in",)), ("mixer", ("w_attn_br", "w_ssm_br", "w_glu", "w_out")),
                 ("ffn_up", ("w_ff_gate", "w_ff_up")), ("ffn_down", ("w_ff_down",)))
REDUCE_GROUPS = (("ffn", ("w_ff_down", "w_ff_gate", "w_ff_up")),
                 ("mixer", ("w_out", "w_ssm_br", "w_glu", "w_attn_br")), ("w_in", ("w_in",)))
WEIGHTS = ("w_in", "b_gate", "w_attn_br", "w_ssm_br", "w_out", "ssm_a_re", "ssm_a_im", "ssm_log_dt", "ssm_b_re",
           "ssm_b_im", "ssm_c_re", "ssm_c_im", "ssm_d", "w_glu", "ln1_g", "ln1_b", "w_ff_gate", "w_ff_up", "w_ff_down",
           "ln2_g", "ln2_b")


def kernel(x, w_in, b_gate, w_attn_br, w_ssm_br, w_out, ssm_a_re, ssm_a_im, ssm_log_dt, ssm_b_re, ssm_b_im, ssm_c_re, ssm_c_im, ssm_d, w_glu, ln1_g, ln1_b, w_ff_gate, w_ff_up, w_ff_down, ln2_g, ln2_b, loss_target, m_w_in, m_b_gate, m_w_attn_br, m_w_ssm_br, m_w_out, m_ssm_a_re, m_ssm_a_im, m_ssm_log_dt, m_ssm_b_re, m_ssm_b_im, m_ssm_c_re, m_ssm_c_im, m_ssm_d, m_w_glu, m_ln1_g, m_ln1_b, m_w_ff_gate, m_w_ff_up, m_w_ff_down, m_ln2_g, m_ln2_b, v_w_in, v_b_gate, v_w_attn_br, v_w_ssm_br, v_w_out, v_ssm_a_re, v_ssm_a_im, v_ssm_log_dt, v_ssm_b_re, v_ssm_b_im, v_ssm_c_re, v_ssm_c_im, v_ssm_d, v_w_glu, v_ln1_g, v_ln1_b, v_w_ff_gate, v_w_ff_up, v_w_ff_down, v_ln2_g, v_ln2_b):
    given = dict(locals())
    px, py, pc = _place()
    chip = 2 * px + py
    core_s = jnp.reshape(pc, (1,)).astype(jnp.int32)
    chip_s = jnp.reshape(chip, (1,)).astype(jnp.int32)

    wts = {}
    for tag, names in GATHER_GROUPS:
        wts.update(zip(names, _gather_weights([_as_rows(n, given[n]).astype(BF16) for n in names],
                                              name="gather_" + tag)))
    ncol = D_MODEL // N_CHIPS
    bg_mine = jnp.where(pc == 0, b_gate[0], jnp.zeros_like(b_gate[0]))
    bg_full = lax.dynamic_update_slice(jnp.zeros((2, D_MODEL), F32), bg_mine, (0, chip * ncol))
    bg_slots = _exchange_rows(bg_full.reshape(16, 128), name="exchange_gate_bias")
    bg_full = _sum_slots(bg_slots, name="sum_gate_bias").reshape(2, D_MODEL)
    small = {n: given[n][0] for n in SMALL if n.startswith("ssm")}
    small.update({n: given[n] for n in ("ln1_g", "ln1_b", "ln2_g", "ln2_b")})
    small["b_gate"] = bg_full

    groups = dict(REDUCE_GROUPS)
    parts, recvd, reduced, sent = {}, {}, {}, {}
    grads, delta, new_m, new_v, done = {}, {}, {}, {}, {}

    def start(tag, big_g, add_after):
        parts[tag], recvd[tag] = _reduce_scatter_start([big_g[n] for n in groups[tag]], core_s, tag=tag,
                                                       add_after=add_after)
        return parts[tag]

    def reduce_sum(tag, after):
        reduced[tag] = _reduce_scatter_finish(parts[tag], recvd[tag], chip_s, tag=tag, sum_after=after)
        return reduced[tag][0]

    def adam(tag, after):
        for n, g_mine, g_theirs in zip(groups[tag], *reduced[tag]):
            res = _adamw_halves(core_s, _as_rows(n, given[n]), g_mine, g_theirs, _as_rows(n, given["m_" + n]),
                                _as_rows(n, given["v_" + n]), name="adamw_" + n, after=after)
            done[n] = res[1]
            grads[n], delta[n], new_m[n], new_v[n] = [_from_rows(n, r) for r in res]

    def ffn_grads(big_g, norm_bwd):
        return start("ffn", big_g, (norm_bwd,))

    def mixer_grads(big_g, scan_bwd):
        return start("mixer", big_g, (scan_bwd, *reduce_sum("ffn", (scan_bwd,))))

    def small_grads(small_g, loss_mine):
        sent["stored"] = [_as_stored(n, small_g[n]) for n in SMALL] + [loss_mine.reshape(1)]
        packed = _pack_rows(sent["stored"])
        sent["slots"] = _exchange_rows(packed, name="exchange_small")
        return (packed,)

    grad_x_after, g_w_in, attention_bwd = _local_step(x[0], loss_target[0], wts, small,
                                                      ffn_grads, mixer_grads, small_grads)

    reduce_sum("mixer", (attention_bwd,))
    summed = _sum_slots(sent["slots"], name="sum_small", after=(g_w_in,))
    adam("mixer", (g_w_in,))
    start("w_in", {"w_in": g_w_in}, (summed, *[done[n] for n in groups["mixer"]]))
    in_flight = (parts["w_in"][0],)
    grad_x = grad_x_after(in_flight)
    adam("ffn", in_flight)
    summed = _unpack_rows(summed, [a.shape for a in sent["stored"]])
    loss = summed.pop()[0]
    at = SMALL.index("b_gate")
    summed[at] = lax.dynamic_slice(summed[at], (0, chip * ncol), (2, ncol))
    summed = [g.reshape(1, -1) if g.ndim == 1 else g for g in summed]
    held = [[_as_stored(n, given[prefix + n]).reshape(g.shape) for n, g in zip(SMALL, summed)]
            for prefix in ("", "m_", "v_")]
    small_out = _adamw_each(held[0], summed, held[1], held[2], name="adamw_small", after=in_flight)
    for out, arrs in zip((grads, delta, new_m, new_v), (summed, *small_out)):
        out.update((n, _as_stored(n, a).reshape(given[n].shape)) for n, a in zip(SMALL, arrs))
    reduce_sum("w_in", (*[done[n] for n in groups["ffn"]], small_out[0][0], grad_x))
    adam("w_in", ())

    return (loss, grad_x.reshape(x.shape), *[grads[n] for n in WEIGHTS], *[delta[n] for n in WEIGHTS],
            *[new_m[n] for n in WEIGHTS], *[new_v[n] for n in WEIGHTS])
```

```python
import math

import jax
import jax.numpy as jnp
from jax import lax
from jax.experimental import pallas as pl
from jax.experimental.pallas import tpu as pltpu
from jax.experimental.pallas import tpu_sc as plsc

F32 = jnp.float32
BF16 = jnp.bfloat16
MESH = pl.DeviceIdType.MESH

D_MODEL = 1024
SEQ = 2048
HEAD_DIM = 64
ATTN_HEADS = 8
DILATIONS = (1, 4, 16)
ATTN_WIDTH = ATTN_HEADS * HEAD_DIM
QKV_WIDTH = 3 * ATTN_WIDTH
BLOCK = 128
ROPE_THETA = 10000.0
NEG_INF = -1e30
SSM_GROUP = 16
SSM_GROUPS = 32
SSM_WIDTH = 512
SSM_STATE = 64
SSM_LANES = SSM_GROUPS * SSM_STATE
SCAN_CHUNKS = 8
SCAN_STEPS = SEQ // SCAN_CHUNKS
IN_WIDTH = 3 * QKV_WIDTH + SSM_WIDTH + 2 * D_MODEL
D_FF = 2816
N_CHIPS = 4
N_DEV = 8
DN_ALPHA = 2.0 ** 0.25
LN_EPS = 1e-5
ADAM_LR = 0.001
ADAM_B1 = 0.9
ADAM_B2 = 0.999
ADAM_EPS = 1e-08
ADAM_WD = 0.01
ADAM_STEP = 10
GELU_C = math.sqrt(2.0 / math.pi)
GELU_K = 0.044715

VMEM_LIMIT_BYTES = 56 * 1024 * 1024


def _sds(shape, dtype=F32):
    return jax.ShapeDtypeStruct(tuple(shape), dtype)


def _cp(semantics=None):
    return pltpu.CompilerParams(dimension_semantics=semantics, vmem_limit_bytes=VMEM_LIMIT_BYTES)


HBM_OPERAND = pl.BlockSpec(memory_space=pl.ANY)


def _matmul(a, b, *, grid, a_spec, b_spec, o_spec, out_shape, dims, k_axis=None, name, after=()):
    nk = grid[k_axis] if k_axis is not None else 1
    o_block = tuple(d for d in o_spec.block_shape if d is not None)
    n_after = len(after)

    def body(a_ref, b_ref, *rest):
        o_ref, acc = rest[n_after], rest[n_after + 1:]
        part = lax.dot_general(a_ref[...].astype(BF16), b_ref[...].astype(BF16),
                               (((dims[0],), (dims[1],)), ((), ())), preferred_element_type=F32)
        if k_axis is None:
            o_ref[...] = part.astype(o_ref.dtype)
        else:
            k = pl.program_id(k_axis)

            @pl.when(k == 0)
            def _():
                acc[0][...] = part

            @pl.when(k > 0)
            def _():
                acc[0][...] += part

            @pl.when(k == nk - 1)
            def _():
                o_ref[...] = acc[0][...].astype(o_ref.dtype)

    sem = tuple("arbitrary" if ax == k_axis else "parallel" for ax in range(len(grid)))
    return pl.pallas_call(
        body, grid=grid, in_specs=[a_spec, b_spec] + [HBM_OPERAND] * n_after, out_specs=o_spec, out_shape=out_shape,
        scratch_shapes=[pltpu.VMEM(o_block, F32)] if k_axis is not None else [],
        compiler_params=_cp(sem), name=name)(a, b, *after)


def _mm_cols_nt(dy, wg, *, tm, name, out_dtype=F32, after=()):
    k, ns = wg.shape[1], wg.shape[2]
    m = dy.shape[0]
    a_spec = pl.BlockSpec((tm, ns), lambda i, s: (i, s))
    return _matmul(dy, wg, grid=(m // tm, N_CHIPS), a_spec=a_spec,
                   b_spec=pl.BlockSpec((None, k, ns), lambda i, s: (s, 0, 0)),
                   o_spec=pl.BlockSpec((tm, k), lambda i, s: (i, 0)),
                   out_shape=_sds((m, k), out_dtype), dims=(1, 1), k_axis=1, name=name, after=after)


def _mm_cols_tn(a, dy, *, ns, name, after=()):
    m, k = a.shape
    return _matmul(a, dy, grid=(N_CHIPS,), a_spec=pl.BlockSpec((m, k), lambda s: (0, 0)),
                   b_spec=pl.BlockSpec((m, ns), lambda s: (0, s)),
                   o_spec=pl.BlockSpec((None, k, ns), lambda s: (s, 0, 0)),
                   out_shape=_sds((N_CHIPS, k, ns), BF16), dims=(0, 0), name=name, after=after)


def _mm_rows_tn(a, dy, *, name):
    m, k = a.shape
    rows, n = k // N_CHIPS, dy.shape[1]
    return _matmul(a, dy, grid=(N_CHIPS,), a_spec=pl.BlockSpec((m, rows), lambda s: (0, s)),
                   b_spec=pl.BlockSpec((m, n), lambda s: (0, 0)),
                   o_spec=pl.BlockSpec((None, rows, n), lambda s: (s, 0, 0)),
                   out_shape=_sds((N_CHIPS, rows, n), BF16), dims=(0, 0), name=name)


def _rowwise(fn, tiled, full, outs, accs=(), *, tm, name, after=()):
    args, in_specs = [], []
    for t in tiled:
        if isinstance(t, tuple):
            arr, w, cb = t
            in_specs.append(pl.BlockSpec((tm, w), lambda i, cb=cb: (i, cb)))
        else:
            arr = t
            in_specs.append(pl.BlockSpec((tm, arr.shape[1]), lambda i: (i, 0)))
        args.append(arr)
    rows = args[0].shape[0]
    for f in full:
        in_specs.append(pl.BlockSpec(f.shape, lambda i, nd=f.ndim: (0,) * nd))
        args.append(f)
    out_specs = [pl.BlockSpec((tm, o.shape[1]), lambda i: (i, 0)) for o in outs]
    out_specs += [pl.BlockSpec(a.shape, lambda i, nd=len(a.shape): (0,) * nd) for a in accs]
    n_in, n_out = len(args), len(outs)
    in_specs += [HBM_OPERAND] * len(after)
    first_out = n_in + len(after)

    def body(*refs):
        res = fn(*[r[...] for r in refs[:n_in]])
        res = res if isinstance(res, (tuple, list)) else (res,)
        for r, v in zip(refs[first_out:first_out + n_out], res[:n_out]):
            r[...] = v.astype(r.dtype)
        i = pl.program_id(0)
        for r, v in zip(refs[first_out + n_out:], res[n_out:]):
            @pl.when(i == 0)
            def _(r=r, v=v):
                r[...] = v

            @pl.when(i > 0)
            def _(r=r, v=v):
                r[...] += v

    res = pl.pallas_call(
        body, grid=(rows // tm,), in_specs=in_specs, out_specs=out_specs, out_shape=list(outs) + list(accs),
        compiler_params=_cp(("arbitrary",) if accs else ("parallel",)), name=name)(*args, *after)
    return res


def _colsum(v):
    return jnp.sum(v, axis=0, keepdims=True)


def _ln_stats(z):
    mu = jnp.mean(z, axis=-1, keepdims=True)
    zc = z - mu
    var = jnp.mean(zc * zc, axis=-1, keepdims=True)
    rstd = lax.rsqrt(var + LN_EPS)
    return zc * rstd, rstd


def _ln_bwd(dy, xhat, rstd, g):
    dxh = dy * g
    m1 = jnp.mean(dxh, axis=-1, keepdims=True)
    m2 = jnp.mean(dxh * xhat, axis=-1, keepdims=True)
    return rstd * (dxh - m1 - xhat * m2)


def _swap_halves(t):
    w = t.shape[-1]
    lane = lax.broadcasted_iota(jnp.int32, t.shape, t.ndim - 1)
    return jnp.where((lane % HEAD_DIM) < HEAD_DIM // 2, pltpu.roll(t, w - HEAD_DIM // 2, t.ndim - 1),
                     pltpu.roll(t, HEAD_DIM // 2, t.ndim - 1))


PHASES = max(DILATIONS)
PAIR = 2 * HEAD_DIM
UNITS = SEQ // BLOCK
UNIT_BATCH = 16
ROPE_ROWS = 256
TAIL_COLS = 256


def _to_phase_rows(t):
    return t.reshape(SEQ // PHASES, PHASES, t.shape[1]).transpose(1, 0, 2).reshape(t.shape)


def _reorder_rows(arr, plus=None, *, to_phase, name, scale=1.0):
    def body(*refs):
        o_ref = refs[-1]
        for rho in range(PHASES):
            phase = pl.ds(rho * BLOCK, BLOCK)
            strided = pl.ds(rho, BLOCK, stride=PHASES)
            src, dst = (strided, phase) if to_phase else (phase, strided)
            val = refs[0][src, :]
            if scale != 1.0:
                val = val * scale
            if plus is not None:
                val = val + refs[1][src, :]
            o_ref[dst, :] = val

    spec = pl.BlockSpec((SEQ, BLOCK), lambda j: (0, j))
    ins = [arr] if plus is None else [arr, plus]
    return pl.pallas_call(body, grid=(arr.shape[1] // BLOCK,), in_specs=[spec] * len(ins), out_specs=spec,
                          out_shape=_sds(arr.shape), compiler_params=_cp(("parallel",)), name=name)(*ins)


def _rope(t, cf, ss):
    return t * cf + _swap_halves(t) * ss


def _rope_transposed(d, cf, ss):
    return d * cf + _swap_halves(d * ss)


def _unit_pieces(u, dil):
    pieces, length = PHASES // dil, 8 * dil
    if dil == 1:
        rho, i = 0, u
    elif dil == PHASES:
        rho, i = u, 0
    else:
        rho, i = jnp.bitwise_and(u, dil - 1), jnp.right_shift(u, dil.bit_length() - 1)
    before = jnp.maximum(i - 1, 0)
    cur = [pl.multiple_of((rho + dil * k) * BLOCK + length * i, 8) for k in range(pieces)]
    prev = [pl.multiple_of((rho + dil * k) * BLOCK + length * before, 8) for k in range(pieces)]
    return i, cur, prev


def _load_tile(ref, starts, dil):
    return jnp.concatenate([ref[pl.ds(st, 8 * dil), :] for st in starts], axis=0)


def _store_tile(ref, starts, dil, val, head=None, accumulate=False):
    length = 8 * dil
    lanes = slice(None) if head is None else pl.ds(head * HEAD_DIM, HEAD_DIM)
    cols = slice(None) if head is None else slice(head * HEAD_DIM, (head + 1) * HEAD_DIM)
    for k, st in enumerate(starts):
        piece = val[k * length:(k + 1) * length, cols]
        if accumulate:
            ref[pl.ds(st, length), lanes] += piece
        else:
            ref[pl.ds(st, length), lanes] = piece


def _tile_position(idx, dil):
    pieces, length = PHASES // dil, 8 * dil
    return pieces * jnp.bitwise_and(idx, length - 1) + jnp.right_shift(idx, length.bit_length() - 1)


def _band_mask(i, dil):
    row = lax.broadcasted_iota(jnp.int32, (BLOCK, 2 * BLOCK), 0)
    col = lax.broadcasted_iota(jnp.int32, (BLOCK, 2 * BLOCK), 1)
    key_pos = _tile_position(jnp.bitwise_and(col, BLOCK - 1), dil) + jnp.where(col >= BLOCK, 0, -BLOCK)
    dist = _tile_position(row, dil) - key_pos
    return (dist >= 0) & (dist <= BLOCK) & ((col >= BLOCK) | (i > 0))


def _causal_mask():
    row = lax.broadcasted_iota(jnp.int32, (BLOCK, BLOCK), 0)
    col = lax.broadcasted_iota(jnp.int32, (BLOCK, BLOCK), 1)
    return row >= col


def _pair_views(col0):
    return [pl.BlockSpec((SEQ, PAIR), lambda hp, g=g: (0, col0 // PAIR + g * (ATTN_WIDTH // PAIR) + hp))
            for g in range(len(DILATIONS))]


def _project_in(x, wg, cos_f, sin_s):
    ns = wg.shape[2]
    tiles = ns // PAIR

    def body(x_ref, w_ref, cf_ref, ss_ref, o_ref):
        shard = pl.program_id(1)
        xb = x_ref[...].astype(BF16)
        cf, ss = cf_ref[...], ss_ref[...]

        def write(rotated, scaled):
            for t0 in range(0, tiles, 2):
                strip = jnp.dot(xb, w_ref[:, t0 * PAIR:(t0 + 2) * PAIR], preferred_element_type=F32)
                for t in (t0, t0 + 1):
                    val = strip[:, (t - t0) * PAIR:(t - t0 + 1) * PAIR]
                    if t < rotated:
                        val = _rope(val, cf, ss)
                        if t < scaled:
                            val = val * (1.0 / math.sqrt(HEAD_DIM))
                    o_ref[:, t * PAIR:(t + 1) * PAIR] = val

        for s in range(N_CHIPS):
            rotated = min(max(2 * QKV_WIDTH - s * ns, 0), ns) // PAIR
            scaled = min(max(QKV_WIDTH - s * ns, 0), ns) // PAIR

            @pl.when(shard == s)
            def _(rotated=rotated, scaled=scaled):
                write(rotated, scaled)

    table = pl.BlockSpec((FF_ROWS, PAIR), lambda i, s: (i, 0))
    return pl.pallas_call(
        body, grid=(SEQ // FF_ROWS, N_CHIPS),
        in_specs=[pl.BlockSpec((FF_ROWS, D_MODEL), lambda i, s: (i, 0)),
                  pl.BlockSpec((None, D_MODEL, ns), lambda i, s: (s, 0, 0)), table, table],
        out_specs=pl.BlockSpec((FF_ROWS, ns), lambda i, s: (i, s)), out_shape=_sds((SEQ, N_CHIPS * ns)),
        compiler_params=_cp(("parallel", "parallel")), name="project_in")(x, wg, cos_f, sin_s)


def _attention_fwd(proj):
    ng = len(DILATIONS)

    def body(*refs):
        q_refs, k_refs, v_refs = refs[:ng], refs[ng:2 * ng], refs[2 * ng:3 * ng]
        attn_ref, lse_ref = refs[3 * ng:]
        qr_refs, kr_refs = q_refs, k_refs
        first = lax.broadcasted_iota(jnp.int32, (BLOCK, PAIR), 1) < HEAD_DIM
        for g, dil in enumerate(DILATIONS):
            two_blocks = SEQ // dil > BLOCK

            def units(t, carry, g=g, dil=dil, two_blocks=two_blocks):
                picked = [_unit_pieces(t * UNIT_BATCH + j, dil) for j in range(UNIT_BATCH)]

                def tiles(ref, with_prev=False):
                    if with_prev and two_blocks:
                        return jnp.stack([jnp.concatenate([_load_tile(ref, prev, dil), _load_tile(ref, rows, dil)],
                                                          axis=0) for _, rows, prev in picked])
                    return jnp.stack([_load_tile(ref, rows, dil) for _, rows, _ in picked])

                qq = tiles(qr_refs[g]).astype(BF16)
                kk = tiles(kr_refs[g], True).astype(BF16)
                vv = tiles(v_refs[g], True).astype(BF16)
                if two_blocks:
                    valid = jnp.stack([_band_mask(i, dil) for i, _, _ in picked])
                else:
                    valid = _causal_mask()[None]
                mine = first[None]
                zero = jnp.zeros_like(qq)
                outs, lses = [], []
                for qh in (jnp.where(mine, qq, zero), jnp.where(mine, zero, qq)):
                    s = jnp.einsum("pqd,pkd->pqk", qh, kk, preferred_element_type=F32)
                    s = jnp.where(valid, s, NEG_INF)
                    m = jnp.max(s, axis=-1, keepdims=True)
                    p = jnp.exp(s - m)
                    l = jnp.sum(p, axis=-1, keepdims=True)
                    outs.append(jnp.einsum("pqk,pkd->pqd", p.astype(BF16), vv, preferred_element_type=F32) * (1.0 / l))
                    lses.append(m + jnp.log(l))
                o = jnp.where(mine, outs[0], outs[1])
                lse = jnp.where(mine, lses[0], lses[1])
                if g > 0:
                    lse_old = tiles(lse_ref)
                    m = jnp.maximum(lse_old, lse)
                    lse_new = m + jnp.log(jnp.exp(lse_old - m) + jnp.exp(lse - m))
                    o = tiles(attn_ref) * jnp.exp(lse_old - lse_new) + o * jnp.exp(lse - lse_new)
                    lse = lse_new
                for j, (_, rows, _) in enumerate(picked):
                    _store_tile(attn_ref, rows, dil, o[j])
                    _store_tile(lse_ref, rows, dil, lse[j])
                return carry

            lax.fori_loop(0, UNITS // UNIT_BATCH, units, 0)

    out = pl.BlockSpec((SEQ, PAIR), lambda hp: (0, hp))
    return pl.pallas_call(
        body, grid=(ATTN_WIDTH // PAIR,),
        in_specs=_pair_views(0) + _pair_views(QKV_WIDTH) + _pair_views(2 * QKV_WIDTH),
        out_specs=[out, out], out_shape=[_sds((SEQ, ATTN_WIDTH)), _sds((SEQ, ATTN_WIDTH))],
        compiler_params=_cp(("parallel",)), name="attention_fwd")(*([proj] * (3 * ng)))


def _attention_bwd(proj, cos_f, sin_s, d_attn, attn, lse, d_u, d_gl):
    pairs = ATTN_WIDTH // PAIR
    last = len(DILATIONS) * pairs - 1

    def accumulate(dil, qr_ref, kr_ref, v_ref, do_ref, o_ref, lse_ref, dq_acc, dk_acc, dv_acc):
        two_blocks = SEQ // dil > BLOCK
        dk_acc[...] = jnp.zeros_like(dk_acc)
        dv_acc[...] = jnp.zeros_like(dv_acc)
        nk = 2 * BLOCK if two_blocks else BLOCK
        first = lax.broadcasted_iota(jnp.int32, (BLOCK, PAIR), 1) < HEAD_DIM
        first_k = lax.broadcasted_iota(jnp.int32, (nk, PAIR), 1) < HEAD_DIM

        def units(t, carry):
            picked = [_unit_pieces(t * UNIT_BATCH + j, dil) for j in range(UNIT_BATCH)]

            def tiles(ref, with_prev=False):
                if with_prev and two_blocks:
                    return jnp.stack([jnp.concatenate([_load_tile(ref, prev, dil), _load_tile(ref, rows, dil)], axis=0)
                                      for _, rows, prev in picked])
                return jnp.stack([_load_tile(ref, rows, dil) for _, rows, _ in picked])

            qq = tiles(qr_ref).astype(BF16)
            kk = tiles(kr_ref, True).astype(BF16)
            vv = tiles(v_ref, True).astype(BF16)
            dof = tiles(do_ref)
            dd = dof * tiles(o_ref)
            lse3 = tiles(lse_ref)
            dob = dof.astype(BF16)
            if two_blocks:
                valid = jnp.stack([_band_mask(i, dil) for i, _, _ in picked])
            else:
                valid = _causal_mask()[None]
            zq, zf = jnp.zeros_like(qq), jnp.zeros_like(dd)
            dqs, dks, dvs = [], [], []
            for head in range(2):
                mine = first[None] if head == 0 else jnp.logical_not(first)[None]
                delta = jnp.sum(jnp.where(mine, dd, zf), axis=-1, keepdims=True)
                lse_h = lse3[:, :, head * HEAD_DIM:head * HEAD_DIM + 1]
                s = jnp.einsum("pqd,pkd->pqk", jnp.where(mine, qq, zq), kk, preferred_element_type=F32)
                p = jnp.where(valid, jnp.exp(s - lse_h), 0.0)
                dp = jnp.einsum("pqd,pkd->pqk", jnp.where(mine, dob, zq), vv, preferred_element_type=F32)
                ds = (p * (dp - delta)).astype(BF16)
                dqs.append(jnp.einsum("pqk,pkd->pqd", ds, kk, preferred_element_type=F32))
                dks.append(jnp.einsum("pqk,pqd->pkd", ds, qq, preferred_element_type=F32))
                dvs.append(jnp.einsum("pqk,pqd->pkd", p.astype(BF16), dob, preferred_element_type=F32))
            dq = jnp.where(first[None], dqs[0], dqs[1])
            dk = jnp.where(first_k[None], dks[0], dks[1])
            dv = jnp.where(first_k[None], dvs[0], dvs[1])
            for j, (_, rows, prev) in enumerate(picked):
                _store_tile(dq_acc, rows, dil, dq[j])
                _store_tile(dk_acc, rows, dil, dk[j, nk - BLOCK:], accumulate=True)
                _store_tile(dv_acc, rows, dil, dv[j, nk - BLOCK:], accumulate=True)
                if two_blocks:
                    _store_tile(dk_acc, prev, dil, dk[j, :BLOCK], accumulate=True)
                    _store_tile(dv_acc, prev, dil, dv[j, :BLOCK], accumulate=True)
            return carry

        lax.fori_loop(0, UNITS // UNIT_BATCH, units, 0)

    def body(qr_ref, kr_ref, v_ref, cf_ref, ss_ref, do_ref, o_ref, lse_ref, du_ref, dgl_ref, out_ref,
             dq_acc, dk_acc, dv_acc, dq_buf, dk_buf, dv_buf, sems):
        step = pl.program_id(0) * pairs + pl.program_id(1)

        def columns(at):
            return [pltpu.make_async_copy(
                buf, out_ref.at[:, pl.ds(pl.multiple_of(j * QKV_WIDTH + at * PAIR, PAIR), PAIR)], sems.at[j])
                for j, buf in enumerate((dq_buf, dk_buf, dv_buf))]

        def tail(ref, col0, at):
            cols = pl.ds(pl.multiple_of(col0 + at * TAIL_COLS, TAIL_COLS), TAIL_COLS)
            return pltpu.make_async_copy(ref, out_ref.at[:, cols], sems.at[3])

        gl_steps, u_steps = 2 * D_MODEL // TAIL_COLS, SSM_WIDTH // TAIL_COLS
        from_gl = step < gl_steps
        from_u = jnp.logical_and(step >= gl_steps, step < gl_steps + u_steps)
        tail_gl = tail(dgl_ref, 3 * QKV_WIDTH + SSM_WIDTH, step)
        tail_u = tail(du_ref, 3 * QKV_WIDTH, step - gl_steps)
        pl.when(from_gl)(tail_gl.start)
        pl.when(from_u)(tail_u.start)

        for g, dil in enumerate(DILATIONS):
            @pl.when(pl.program_id(0) == g)
            def _(dil=dil):
                accumulate(dil, qr_ref, kr_ref, v_ref, do_ref, o_ref, lse_ref, dq_acc, dk_acc, dv_acc)

        @pl.when(step > 0)
        def _():
            for cp in columns(step - 1):
                cp.wait()

        def finish(t, carry):
            rows = pl.ds(pl.multiple_of(t * ROPE_ROWS, ROPE_ROWS), ROPE_ROWS)
            cf, ss = cf_ref[rows, :], ss_ref[rows, :]
            dq = dq_acc[rows, :] * (1.0 / math.sqrt(HEAD_DIM))
            dq_buf[rows, :] = _rope_transposed(dq, cf, ss).astype(BF16)
            dk_buf[rows, :] = _rope_transposed(dk_acc[rows, :], cf, ss).astype(BF16)
            dv_buf[rows, :] = dv_acc[rows, :].astype(BF16)
            return carry

        lax.fori_loop(0, SEQ // ROPE_ROWS, finish, 0)
        for cp in columns(step):
            cp.start()
        pl.when(from_gl)(tail_gl.wait)
        pl.when(from_u)(tail_u.wait)

        @pl.when(step == last)
        def _():
            for cp in columns(step):
                cp.wait()

    whole = pl.BlockSpec((SEQ, PAIR), lambda g, hp: (0, 0))
    pair = pl.BlockSpec((SEQ, PAIR), lambda g, hp: (0, hp))
    views = [pl.BlockSpec((SEQ, PAIR), lambda g, hp, c0=col0 // PAIR: (0, c0 + g * pairs + hp))
             for col0 in (0, QKV_WIDTH, 2 * QKV_WIDTH)]
    gl_blocks, u_blocks = 2 * D_MODEL // TAIL_COLS, SSM_WIDTH // TAIL_COLS
    assert gl_blocks + u_blocks <= last + 1
    gl_spec = pl.BlockSpec((SEQ, TAIL_COLS), lambda g, hp: (0, jnp.minimum(g * pairs + hp, gl_blocks - 1)))
    u_spec = pl.BlockSpec((SEQ, TAIL_COLS),
                          lambda g, hp: (0, jnp.clip(g * pairs + hp - gl_blocks, 0, u_blocks - 1)))
    return pl.pallas_call(
        body, grid=(len(DILATIONS), pairs),
        in_specs=views + [whole, whole, pair, pair, pair, u_spec, gl_spec],
        out_specs=HBM_OPERAND, out_shape=_sds((SEQ, IN_WIDTH), BF16),
        scratch_shapes=[pltpu.VMEM((SEQ, PAIR), F32)] * 3 + [pltpu.VMEM((SEQ, PAIR), BF16)] * 3
        + [pltpu.SemaphoreType.DMA((4,))],
        compiler_params=_cp(("arbitrary", "arbitrary")), name="attention_bwd")(
            proj, proj, proj, cos_f, sin_s, d_attn, attn, lse, d_u, d_gl)


def _cmul(ar, ai, br, bi):
    return ar * br - ai * bi, ar * bi + ai * br


def _pow256(ar, ai):
    for _ in range(8):
        ar, ai = _cmul(ar, ai, ar, ai)
    return ar, ai


def _chunk_carries(first_r, first_i, pr, pi, reverse):
    rows = lax.broadcasted_iota(jnp.int32, first_r.shape, 0)
    out_r = jnp.zeros_like(first_r)
    out_i = jnp.zeros_like(first_i)
    hr = jnp.zeros_like(first_r[0:1])
    hi = jnp.zeros_like(hr)
    order = range(SCAN_CHUNKS - 1, -1, -1) if reverse else range(SCAN_CHUNKS)
    for c in order:
        out_r = jnp.where(rows == c, hr, out_r)
        out_i = jnp.where(rows == c, hi, out_i)
        tr, ti = _cmul(pr[0:1], pi[0:1], hr, hi)
        hr = first_r[c:c + 1] + tr
        hi = first_i[c:c + 1] + ti
    return out_r, out_i


def _tile(j):
    return pl.ds(pl.multiple_of(j * SCAN_CHUNKS, SCAN_CHUNKS), SCAN_CHUNKS)


def _to_scan_rows(t):
    per = SCAN_STEPS // PHASES
    return t.reshape(PHASES, SCAN_CHUNKS, per, t.shape[1]).transpose(2, 0, 1, 3).reshape(t.shape)


def _from_scan_rows(t):
    per = SCAN_STEPS // PHASES
    return t.reshape(per, PHASES, SCAN_CHUNKS, t.shape[1]).transpose(1, 2, 0, 3).reshape(t.shape)


def _scan_in_place(hr_ref, hi_ref, a_r, a_i):
    def local(j, carry):
        tr, ti = _cmul(a_r, a_i, carry[0], carry[1])
        nr = tr + hr_ref[_tile(j), :]
        ni = ti + hi_ref[_tile(j), :]
        hr_ref[_tile(j), :] = nr
        hi_ref[_tile(j), :] = ni
        return nr, ni

    zero = jnp.zeros_like(a_r)
    last_r, last_i = lax.fori_loop(0, SCAN_STEPS, local, (zero, zero), unroll=4)
    pr, pi = _pow256(a_r, a_i)
    er, ei = _chunk_carries(last_r, last_i, pr, pi, reverse=False)

    def fix(j, carry):
        tr, ti = _cmul(carry[0], carry[1], er, ei)
        hr_ref[_tile(j), :] += tr
        hi_ref[_tile(j), :] += ti
        return _cmul(carry[0], carry[1], a_r, a_i)

    lax.fori_loop(0, SCAN_STEPS, fix, (a_r, a_i), unroll=4)
    return er, ei


def _reverse_scan_in_place(lr_ref, li_ref, hr_ref, hi_ref, er, ei, a_r, a_i):
    def local(t, carry):
        j = SCAN_STEPS - 1 - t
        tr, ti = _cmul(a_r, a_i, carry[0], carry[1])
        nr = tr + lr_ref[_tile(j), :]
        ni = ti + li_ref[_tile(j), :]
        lr_ref[_tile(j), :] = nr
        li_ref[_tile(j), :] = ni
        return nr, ni

    zero = jnp.zeros_like(a_r)
    first_r, first_i = lax.fori_loop(0, SCAN_STEPS, local, (zero, zero), unroll=4)
    pr, pi = _pow256(a_r, a_i)
    nxt_r, nxt_i = _chunk_carries(first_r, first_i, pr, pi, reverse=True)

    def accumulate(lam_r, lam_i, hp_r, hp_i, acc):
        return (acc[0] + lam_r * hp_r + lam_i * hp_i, acc[1] + lam_i * hp_r - lam_r * hp_i)

    def fix(t, carry):
        qr, qi, acc_r, acc_i = carry
        j = SCAN_STEPS - 1 - t
        tr, ti = _cmul(qr, qi, nxt_r, nxt_i)
        lam_r = lr_ref[_tile(j), :] + tr
        lam_i = li_ref[_tile(j), :] + ti
        lr_ref[_tile(j), :] = lam_r
        li_ref[_tile(j), :] = lam_i
        acc_r, acc_i = accumulate(lam_r, lam_i, hr_ref[_tile(j - 1), :], hi_ref[_tile(j - 1), :], (acc_r, acc_i))
        qr, qi = _cmul(qr, qi, a_r, a_i)
        return qr, qi, acc_r, acc_i

    qr, qi, acc_r, acc_i = lax.fori_loop(0, SCAN_STEPS - 1, fix, (a_r, a_i, zero, zero), unroll=4)
    tr, ti = _cmul(qr, qi, nxt_r, nxt_i)
    lam_r = lr_ref[_tile(0), :] + tr
    lam_i = li_ref[_tile(0), :] + ti
    lr_ref[_tile(0), :] = lam_r
    li_ref[_tile(0), :] = lam_i
    acc_r, acc_i = accumulate(lam_r, lam_i, er, ei, (acc_r, acc_i))
    return jnp.sum(acc_r, axis=0, keepdims=True), jnp.sum(acc_i, axis=0, keepdims=True)


def _rope_tables():
    half = HEAD_DIM // 2
    inv_freq = ROPE_THETA ** (-jnp.arange(half, dtype=F32) / half)
    ang = jnp.arange(SEQ, dtype=F32)[:, None] * inv_freq[None, :]
    cos, sin = jnp.cos(ang), jnp.sin(ang)
    cos_f = jnp.concatenate([cos, cos, cos, cos], axis=1)
    sin_s = jnp.concatenate([-sin, sin, -sin, sin], axis=1)
    return cos_f, sin_s


def _ssm_discretise(a_re, a_im, log_dt, b_re, b_im):
    lam = lax.complex(a_re, a_im)
    dt = jnp.exp(log_dt)[:, None]
    a_bar = jnp.exp(lam * dt)
    b_bar = ((a_bar - 1.0) / lam)[..., None] * lax.complex(b_re, b_im)
    return a_bar.real, a_bar.imag, b_bar.real, b_bar.imag


SSM_SLABS = 4
SLAB_GROUPS = SSM_GROUPS // SSM_SLABS
SLAB_IN = SSM_WIDTH // SSM_SLABS
SLAB_STATE = SSM_LANES // SSM_SLABS


def _slab_block_diag(blocks):
    _, r, c = blocks.shape
    eye = jnp.eye(SLAB_GROUPS, dtype=blocks.dtype)
    b5 = blocks.reshape(SSM_SLABS, SLAB_GROUPS, r, 1, c) * eye[None, :, None, :, None]
    return b5.reshape(SSM_SLABS, SLAB_GROUPS * r, SLAB_GROUPS * c)


def _diag_blocks(a, b):
    ra, cb = a.shape[1], b.shape[1]
    wa, wb = ra // SLAB_GROUPS, cb // SLAB_GROUPS
    d = lax.dot_general(a, b, (((0,), (0,)), ((), ())), preferred_element_type=F32)
    row_g = jnp.right_shift(lax.broadcasted_iota(jnp.int32, (ra, cb), 0), wa.bit_length() - 1)
    col_g = jnp.right_shift(lax.broadcasted_iota(jnp.int32, (ra, cb), 1), wb.bit_length() - 1)
    d = jnp.where(row_g == col_g, d, 0.0)
    fold = (jnp.bitwise_and(lax.broadcasted_iota(jnp.int32, (cb, wb), 0), wb - 1)
            == lax.broadcasted_iota(jnp.int32, (cb, wb), 1)).astype(F32)
    return jnp.dot(d, fold, preferred_element_type=F32, precision=lax.Precision.HIGHEST)


def _slab_specs():
    tok = pl.BlockSpec((SEQ, SLAB_IN), lambda j: (0, j))
    state = pl.BlockSpec((SEQ, SLAB_STATE), lambda j: (0, j))
    b_in = pl.BlockSpec((None, SLAB_IN, SLAB_STATE), lambda j: (j, 0, 0))
    c_out = pl.BlockSpec((None, SLAB_STATE, SLAB_IN), lambda j: (j, 0, 0))
    vec = pl.BlockSpec((1, SLAB_STATE), lambda j: (0, j))
    ent = pl.BlockSpec((SCAN_CHUNKS, SLAB_STATE), lambda j: (0, j))
    return tok, state, b_in, c_out, vec, ent


def _ssm_forward(u, b_in_r, b_in_i, c_out_r, c_out_ni, a_r, a_i):
    def body(u_ref, br_ref, bi_ref, cr_ref, ci_ref, ar_ref, ai_ref, y_ref, hr_ref, hi_ref, er_ref, ei_ref):
        uu = u_ref[...]
        hr_ref[...] = jnp.dot(uu, br_ref[...], preferred_element_type=F32)
        hi_ref[...] = jnp.dot(uu, bi_ref[...], preferred_element_type=F32)
        a_re = jnp.broadcast_to(ar_ref[...], (SCAN_CHUNKS, SLAB_STATE))
        a_im = jnp.broadcast_to(ai_ref[...], (SCAN_CHUNKS, SLAB_STATE))
        er_ref[...], ei_ref[...] = _scan_in_place(hr_ref, hi_ref, a_re, a_im)
        y_ref[...] = (jnp.dot(hr_ref[...].astype(BF16), cr_ref[...], preferred_element_type=F32)
                      + jnp.dot(hi_ref[...].astype(BF16), ci_ref[...], preferred_element_type=F32))

    tok, state, b_in, c_out, vec, ent = _slab_specs()
    return pl.pallas_call(
        body, grid=(SSM_SLABS,), in_specs=[tok, b_in, b_in, c_out, c_out, vec, vec],
        out_specs=[tok, state, state, ent, ent],
        out_shape=[_sds((SEQ, SSM_WIDTH)), _sds((SEQ, SSM_LANES)), _sds((SEQ, SSM_LANES)),
                   _sds((SCAN_CHUNKS, SSM_LANES)), _sds((SCAN_CHUNKS, SSM_LANES))],
        compiler_params=_cp(("parallel",)), name="ssm_forward")(u, b_in_r, b_in_i, c_out_r, c_out_ni, a_r, a_i)


def _ssm_backward(d_y, d_u_skip, u, h_r, h_i, e_r, e_i, b_in_r, b_in_i, c_out_r, c_out_ni, a_r, a_i):
    def body(dy_ref, skip_ref, u_ref, hr_ref, hi_ref, er_ref, ei_ref, br_ref, bi_ref, cr_ref, ci_ref, ar_ref, ai_ref,
             du_ref, dar_ref, dai_ref, dcr_ref, dci_ref, dbr_ref, dbi_ref, lr_ref, li_ref):
        dy = dy_ref[...]
        lr_ref[...] = _dot_nt(dy, cr_ref[...])
        li_ref[...] = _dot_nt(dy, ci_ref[...])
        a_re = jnp.broadcast_to(ar_ref[...], (SCAN_CHUNKS, SLAB_STATE))
        a_im = -jnp.broadcast_to(ai_ref[...], (SCAN_CHUNKS, SLAB_STATE))
        dar_ref[...], dai_ref[...] = _reverse_scan_in_place(lr_ref, li_ref, hr_ref, hi_ref, er_ref[...], ei_ref[...],
                                                            a_re, a_im)
        dcr_ref[...] = _diag_blocks(dy, hr_ref[...].astype(BF16))
        dci_ref[...] = _diag_blocks(dy, hi_ref[...].astype(BF16))
        lam_r, lam_i = lr_ref[...].astype(BF16), li_ref[...].astype(BF16)
        uu = u_ref[...]
        dbr_ref[...] = _diag_blocks(uu, lam_r)
        dbi_ref[...] = _diag_blocks(uu, lam_i)
        du = skip_ref[...] + _dot_nt(lam_r, br_ref[...]) + _dot_nt(lam_i, bi_ref[...])
        du_ref[...] = du.astype(BF16)

    tok, state, b_in, c_out, vec, ent = _slab_specs()
    db = pl.BlockSpec((SLAB_IN, SSM_STATE), lambda j: (j, 0))
    return pl.pallas_call(
        body, grid=(SSM_SLABS,), in_specs=[tok, tok, tok, state, state, ent, ent, b_in, b_in, c_out, c_out, vec, vec],
        out_specs=[tok, vec, vec, db, db, db, db],
        out_shape=[_sds((SEQ, SSM_WIDTH), BF16), _sds((1, SSM_LANES)), _sds((1, SSM_LANES))]
        + [_sds((SSM_WIDTH, SSM_STATE))] * 4,
        scratch_shapes=[pltpu.VMEM((SEQ, SLAB_STATE), F32)] * 2,
        compiler_params=_cp(("parallel",)), name="ssm_backward")(
            d_y, d_u_skip, u, h_r, h_i, e_r, e_i, b_in_r, b_in_i, c_out_r, c_out_ni, a_r, a_i)


FF_ROWS = 1024
FF_SHARD = D_FF // N_CHIPS


def _dot_nt(a, b):
    return lax.dot_general(a, b, (((1,), (1,)), ((), ())), preferred_element_type=F32)


def _ffn_up(h, w_gate_t, w_up_t):
    def body(h_ref, wg_ref, wu_ref, a_ref, b_ref, act_ref):
        hb = h_ref[...].astype(BF16)
        a = _dot_nt(hb, wg_ref[...])
        b = _dot_nt(hb, wu_ref[...])
        a_ref[...] = a
        b_ref[...] = b
        act_ref[...] = (a * jax.nn.sigmoid(a) * b).astype(BF16)

    w_spec = pl.BlockSpec((None, FF_SHARD, D_MODEL), lambda i, k: (k, 0, 0))
    o_spec = pl.BlockSpec((None, FF_ROWS, FF_SHARD), lambda i, k: (k, i, 0))
    shape = (N_CHIPS, SEQ, FF_SHARD)
    return pl.pallas_call(
        body, grid=(SEQ // FF_ROWS, N_CHIPS),
        in_specs=[pl.BlockSpec((FF_ROWS, D_MODEL), lambda i, k: (i, 0)), w_spec, w_spec],
        out_specs=[o_spec, o_spec, o_spec], out_shape=[_sds(shape), _sds(shape), _sds(shape, BF16)],
        compiler_params=_cp(("parallel", "parallel")), name="ffn_up")(h, w_gate_t, w_up_t)


def _ffn_down_ln2_loss(act, w_down, h, tgt, ln_g, ln_b):
    def body(act_ref, w_ref, h_ref, tgt_ref, g_ref, b_ref, dz_ref, loss_ref, dg_ref, db_ref, acc):
        i, k = pl.program_id(0), pl.program_id(1)
        part = jnp.dot(act_ref[...], w_ref[...], preferred_element_type=F32)

        @pl.when(k == 0)
        def _():
            acc[...] = part

        @pl.when(k > 0)
        def _():
            acc[...] += part

        @pl.when(k == N_CHIPS - 1)
        def _():
            g = g_ref[...]
            xhat, rstd = _ln_stats(DN_ALPHA * h_ref[...] + acc[...])
            err = xhat * g + b_ref[...] - tgt_ref[...]
            d_out = err * (1.0 / D_MODEL)
            dz_ref[...] = _ln_bwd(d_out, xhat, rstd, g)
            loss_rows = jnp.sum(err * err, axis=-1, keepdims=True) * (0.5 / D_MODEL)
            sums = (jnp.broadcast_to(jnp.sum(loss_rows, axis=0, keepdims=True), loss_ref.shape),
                    _colsum(d_out * xhat), _colsum(d_out))
            for ref, val in zip((loss_ref, dg_ref, db_ref), sums):
                @pl.when(i == 0)
                def _(ref=ref, val=val):
                    ref[...] = val

                @pl.when(i > 0)
                def _(ref=ref, val=val):
                    ref[...] += val

    row = pl.BlockSpec((FF_ROWS, D_MODEL), lambda i, k: (i, 0))
    vec = pl.BlockSpec((1, D_MODEL), lambda i, k: (0, 0))
    return pl.pallas_call(
        body, grid=(SEQ // FF_ROWS, N_CHIPS),
        in_specs=[pl.BlockSpec((None, FF_ROWS, FF_SHARD), lambda i, k: (k, i, 0)),
                  pl.BlockSpec((None, FF_SHARD, D_MODEL), lambda i, k: (k, 0, 0)), row, row, vec, vec],
        out_specs=[row, pl.BlockSpec((1, BLOCK), lambda i, k: (0, 0)), vec, vec],
        out_shape=[_sds((SEQ, D_MODEL)), _sds((1, BLOCK)), _sds((1, D_MODEL)), _sds((1, D_MODEL))],
        scratch_shapes=[pltpu.VMEM((FF_ROWS, D_MODEL), F32)],
        compiler_params=_cp(("arbitrary", "arbitrary")), name="ffn_down_ln2_loss")(act, w_down, h, tgt, ln_g, ln_b)


def _ffn_down_bwd(dz, w_down, a, b):
    def body(dz_ref, wd_ref, a_ref, b_ref, da_ref, db_ref):
        d_act = _dot_nt(dz_ref[...].astype(BF16), wd_ref[...])
        av = a_ref[...]
        sg = jax.nn.sigmoid(av)
        da_ref[...] = (d_act * b_ref[...] * sg * (1.0 + av * (1.0 - sg))).astype(BF16)
        db_ref[...] = (d_act * av * sg).astype(BF16)

    t_spec = pl.BlockSpec((None, FF_ROWS, FF_SHARD), lambda i, k: (k, i, 0))
    shape = (N_CHIPS, SEQ, FF_SHARD)
    return pl.pallas_call(
        body, grid=(SEQ // FF_ROWS, N_CHIPS),
        in_specs=[pl.BlockSpec((FF_ROWS, D_MODEL), lambda i, k: (i, 0)),
                  pl.BlockSpec((None, FF_SHARD, D_MODEL), lambda i, k: (k, 0, 0)), t_spec, t_spec],
        out_specs=[t_spec, t_spec], out_shape=[_sds(shape, BF16), _sds(shape, BF16)],
        compiler_params=_cp(("parallel", "parallel")), name="ffn_down_bwd")(dz, w_down, a, b)


def _ffn_dh(d_a, d_b, w_gate_t, w_up_t):
    def body(da_ref, db_ref, wg_ref, wu_ref, o_ref, acc):
        k = pl.program_id(1)
        part = (jnp.dot(da_ref[...], wg_ref[...], preferred_element_type=F32)
                + jnp.dot(db_ref[...], wu_ref[...], preferred_element_type=F32))

        @pl.when(k == 0)
        def _():
            acc[...] = part

        @pl.when(k > 0)
        def _():
            acc[...] += part

        @pl.when(k == N_CHIPS - 1)
        def _():
            o_ref[...] = acc[...]

    t_spec = pl.BlockSpec((None, FF_ROWS, FF_SHARD), lambda i, k: (k, i, 0))
    w_spec = pl.BlockSpec((None, FF_SHARD, D_MODEL), lambda i, k: (k, 0, 0))
    return pl.pallas_call(
        body, grid=(SEQ // FF_ROWS, N_CHIPS), in_specs=[t_spec, t_spec, w_spec, w_spec],
        out_specs=pl.BlockSpec((FF_ROWS, D_MODEL), lambda i, k: (i, 0)), out_shape=_sds((SEQ, D_MODEL)),
        scratch_shapes=[pltpu.VMEM((FF_ROWS, D_MODEL), F32)],
        compiler_params=_cp(("parallel", "arbitrary")), name="ffn_dh")(d_a, d_b, w_gate_t, w_up_t)


def _local_step(x, tgt, wts, small, ffn_grads, mixer_grads, small_grads):
    s = SEQ
    cos_f, sin_s = [_to_phase_rows(t) for t in _rope_tables()]
    x = _reorder_rows(x, to_phase=True, name="phase_rows_x")
    tgt = _reorder_rows(tgt, to_phase=True, name="phase_rows_target")

    proj = _project_in(x, wts["w_in"], cos_f, sin_s)

    attn, lse = _attention_fwd(proj)

    (abar_r, abar_i, bbar_r, bbar_i), ssm_vjp = jax.vjp(
        _ssm_discretise, small["ssm_a_re"], small["ssm_a_im"], small["ssm_log_dt"], small["ssm_b_re"], small["ssm_b_im"])
    b_in_r, b_in_i = [_slab_block_diag(b.transpose(0, 2, 1)).astype(BF16) for b in (bbar_r, bbar_i)]
    c_out_r = _slab_block_diag(small["ssm_c_re"].transpose(0, 2, 1)).astype(BF16)
    c_out_ni = _slab_block_diag(-small["ssm_c_im"].transpose(0, 2, 1)).astype(BF16)
    a_r, a_i = abar_r.reshape(1, SSM_LANES), abar_i.reshape(1, SSM_LANES)
    d_skip = small["ssm_d"].reshape(1, SSM_WIDTH)

    u_f = _to_scan_rows(proj[:, 3 * QKV_WIDTH:3 * QKV_WIDTH + SSM_WIDTH])
    u_p = u_f.astype(BF16)
    y_c, h_r, h_i, e_r, e_i = _ssm_forward(u_p, b_in_r, b_in_i, c_out_r, c_out_ni, a_r, a_i)

    def branch(t, wg):
        return jnp.concatenate([jnp.dot(t, wg[k], preferred_element_type=F32) for k in range(N_CHIPS)], axis=1)

    def branch_t(t, wg):
        ns = wg.shape[2]
        return sum(_dot_nt(t[:, k * ns:(k + 1) * ns], wg[k]) for k in range(N_CHIPS))

    def gelu_glu(yc, u, dsk, wg):
        y = yc + dsk * u
        gel = (0.5 * y * (1.0 + jnp.tanh(GELU_C * (y + GELU_K * y * y * y)))).astype(BF16)
        glu = branch(gel, wg)
        return y, gel, glu, glu[:, :SSM_WIDTH] * jax.nn.sigmoid(glu[:, SSM_WIDTH:])

    y_s5, gel, glu, y_glu = _rowwise(
        gelu_glu, [y_c, u_f], [d_skip, wts["w_glu"]],
        [_sds((s, SSM_WIDTH)), _sds((s, SSM_WIDTH), BF16), _sds((s, 2 * SSM_WIDTH)), _sds((s, SSM_WIDTH), BF16)],
        tm=512, name="ssm_gelu_glu")
    y_glu = _from_scan_rows(y_glu)

    gl0 = (proj, D_MODEL, (3 * QKV_WIDTH + SSM_WIDTH) // D_MODEL)
    gl1 = (proj, D_MODEL, (3 * QKV_WIDTH + SSM_WIDTH) // D_MODEL + 1)
    b_gate = small["b_gate"]
    w_out = wts["w_out"].reshape(D_MODEL, D_MODEL)

    def mix_ln1(l0, l1, at, yg, xv, bg, wa, ws, wo, g, b):
        ya = branch(at.astype(BF16), wa)
        ys = branch(yg, ws)
        mixed = (jax.nn.sigmoid(l0 + bg[0:1]) * ya + jax.nn.sigmoid(l1 + bg[1:2]) * ys).astype(BF16)
        z = DN_ALPHA * xv + jnp.dot(mixed, wo, preferred_element_type=F32)
        xhat, _ = _ln_stats(z)
        return ya, ys, mixed, z, xhat * g + b

    y_attn, y_ssm, mixed, z1, h = _rowwise(
        mix_ln1, [gl0, gl1, attn, y_glu, x],
        [b_gate, wts["w_attn_br"], wts["w_ssm_br"], w_out, small["ln1_g"], small["ln1_b"]],
        [_sds((s, D_MODEL)), _sds((s, D_MODEL)), _sds((s, D_MODEL), BF16), _sds((s, D_MODEL)), _sds((s, D_MODEL))],
        tm=256, name="mix_ln1")

    nf = D_FF // N_CHIPS
    w_gate_t, w_up_t, w_down = wts["w_ff_gate"], wts["w_ff_up"], wts["w_ff_down"]
    ff_a, ff_b, act = _ffn_up(h, w_gate_t, w_up_t)
    dz2, loss_v, d_ln2_g, d_ln2_b = _ffn_down_ln2_loss(act, w_down, h, tgt, small["ln2_g"], small["ln2_b"])

    d_a, d_b = _ffn_down_bwd(dz2, w_down, ff_a, ff_b)

    def grad_rows(lhs, rhs, name):
        return _matmul(lhs, rhs, grid=(N_CHIPS,), a_spec=pl.BlockSpec((None, s, nf), lambda k: (k, 0, 0)),
                       b_spec=pl.BlockSpec((s, D_MODEL), lambda k: (0, 0)),
                       o_spec=pl.BlockSpec((None, nf, D_MODEL), lambda k: (k, 0, 0)),
                       out_shape=_sds((N_CHIPS, nf, D_MODEL), BF16), dims=(0, 0), name=name)

    g_w_ff_down = grad_rows(act, dz2, "g_w_ff_down")
    g_w_ff_gate = grad_rows(d_a, h, "g_w_ff_gate")
    g_w_ff_up = grad_rows(d_b, h, "g_w_ff_up")
    dh_ff = _ffn_dh(d_a, d_b, w_gate_t, w_up_t)

    def ln1_gate_bwd(dz, dff, z, l0, l1, ya, ys, g, bg, wo, wa, ws):
        xhat, rstd = _ln_stats(z)
        dh = DN_ALPHA * dz + dff
        dz_in = _ln_bwd(dh, xhat, rstd, g)
        dm = _dot_nt(dz_in.astype(BF16), wo)
        g0 = jax.nn.sigmoid(l0 + bg[0:1])
        g1 = jax.nn.sigmoid(l1 + bg[1:2])
        dl0 = dm * ya * g0 * (1.0 - g0)
        dl1 = dm * ys * g1 * (1.0 - g1)
        dya, dys = (dm * g0).astype(BF16), (dm * g1).astype(BF16)
        return (dz_in, dya, dys, jnp.concatenate([dl0, dl1], axis=1), branch_t(dya, wa), branch_t(dys, ws),
                _colsum(dh * xhat), _colsum(dh), _colsum(dl0), _colsum(dl1))

    dz1, d_y_attn, d_y_ssm, d_gl, d_attn, d_y_glu, d_ln1_g, d_ln1_b, d_bg0, d_bg1 = _rowwise(
        ln1_gate_bwd, [dz2, dh_ff, z1, gl0, gl1, y_attn, y_ssm],
        [small["ln1_g"], b_gate, w_out, wts["w_attn_br"], wts["w_ssm_br"]],
        [_sds((s, D_MODEL)), _sds((s, D_MODEL), BF16), _sds((s, D_MODEL), BF16), _sds((s, 2 * D_MODEL), BF16),
         _sds((s, ATTN_WIDTH)), _sds((s, SSM_WIDTH))],
        [_sds((1, D_MODEL))] * 4, tm=256, name="ln1_gate_bwd", after=(g_w_ff_down, g_w_ff_gate, g_w_ff_up))
    ffn_sent = ffn_grads({"w_ff_down": g_w_ff_down, "w_ff_gate": g_w_ff_gate, "w_ff_up": g_w_ff_up}, dz1)
    g_w_out = _mm_rows_tn(mixed, dz1, name="g_w_out")

    g_w_ssm_br = _mm_cols_tn(y_glu, d_y_ssm, ns=D_MODEL // N_CHIPS, name="g_w_ssm_br")
    d_y_glu = _to_scan_rows(d_y_glu)

    def glu_gelu_bwd(dyg, gl, y, u, dsk, wg):
        ga, gb = gl[:, :SSM_WIDTH], gl[:, SSM_WIDTH:]
        sg = jax.nn.sigmoid(gb)
        d_gl = jnp.concatenate([dyg * sg, dyg * ga * sg * (1.0 - sg)], axis=1).astype(BF16)
        dg = branch_t(d_gl, wg)
        th = jnp.tanh(GELU_C * (y + GELU_K * y * y * y))
        dy = dg * (0.5 * (1.0 + th) + 0.5 * y * (1.0 - th * th) * GELU_C * (1.0 + 3.0 * GELU_K * y * y))
        return d_gl, dy, dy * dsk, _colsum(dy * u)

    d_glu, d_y, d_u_skip, d_ssm_d = _rowwise(
        glu_gelu_bwd, [d_y_glu, glu, y_s5, u_f], [d_skip, wts["w_glu"]],
        [_sds((s, 2 * SSM_WIDTH), BF16), _sds((s, SSM_WIDTH), BF16), _sds((s, SSM_WIDTH))], [_sds((1, SSM_WIDTH))],
        tm=512, name="glu_gelu_bwd", after=tuple(ffn_sent))
    g_w_glu = _mm_cols_tn(gel, d_glu, ns=2 * SSM_WIDTH // N_CHIPS, name="g_w_glu")
    d_u, d_abar_r, d_abar_i, d_c_r, d_c_ni, d_bin_r, d_bin_i = _ssm_backward(
        d_y, d_u_skip, u_p, h_r, h_i, e_r, e_i, b_in_r, b_in_i, c_out_r, c_out_ni, a_r, a_i)
    d_u = _from_scan_rows(d_u)
    d_bbar_r = d_bin_r.reshape(SSM_GROUPS, SSM_GROUP, SSM_STATE).transpose(0, 2, 1)
    d_bbar_i = d_bin_i.reshape(SSM_GROUPS, SSM_GROUP, SSM_STATE).transpose(0, 2, 1)
    d_a_re, d_a_im, d_log_dt, d_b_re, d_b_im = ssm_vjp(
        (d_abar_r.reshape(SSM_GROUPS, SSM_STATE), d_abar_i.reshape(SSM_GROUPS, SSM_STATE), d_bbar_r, d_bbar_i))
    d_c_re = d_c_r.reshape(SSM_GROUPS, SSM_GROUP, SSM_STATE)
    d_c_im = -d_c_ni.reshape(SSM_GROUPS, SSM_GROUP, SSM_STATE)

    g_w_attn_br = _mm_cols_tn(attn, d_y_attn, ns=D_MODEL // N_CHIPS, name="g_w_attn_br")
    mixer_grads({"w_out": g_w_out, "w_ssm_br": g_w_ssm_br, "w_glu": g_w_glu, "w_attn_br": g_w_attn_br}, d_abar_r)
    small_g = {"b_gate": jnp.concatenate([d_bg0, d_bg1], axis=0), "ssm_a_re": d_a_re, "ssm_a_im": d_a_im,
               "ssm_log_dt": d_log_dt, "ssm_b_re": d_b_re, "ssm_b_im": d_b_im, "ssm_c_re": d_c_re, "ssm_c_im": d_c_im,
               "ssm_d": d_ssm_d.reshape(SSM_WIDTH), "ln1_g": d_ln1_g, "ln1_b": d_ln1_b, "ln2_g": d_ln2_g,
               "ln2_b": d_ln2_b}
    shared = small_grads(small_g, loss_v[0, 0])
    d_proj = _attention_bwd(proj, cos_f, sin_s, d_attn, attn, lse, d_u, d_gl)

    g_w_in = _mm_cols_tn(x, d_proj, ns=IN_WIDTH // N_CHIPS, name="g_w_in", after=tuple(shared))

    def grad_x_after(after):
        dx_proj = _mm_cols_nt(d_proj, wts["w_in"], tm=1024, name="dx_proj", after=after)
        return _reorder_rows(dz1, dx_proj, to_phase=False, name="grad_x", scale=DN_ALPHA)

    return grad_x_after, g_w_in, d_proj


GATHER_ID, SWAP_ID, SCATTER_ID, JOIN_ID, EXCHANGE_ID = 1, 2, 3, 4, 5


def _place():
    return lax.axis_index("x"), lax.axis_index("y"), lax.axis_index("c")


def _other_chips(x, y):
    return [(1 - x, y), (x, 1 - y), (1 - x, 1 - y)]


def _handshake(peers):
    barrier = pltpu.get_barrier_semaphore()
    for peer in peers:
        pl.semaphore_signal(barrier, inc=1, device_id=peer, device_id_type=MESH)
    pl.semaphore_wait(barrier, len(peers))


def _sequencer(body, arrays, out_type, sems, collective_id, name):
    return pl.kernel(body, name=name, out_type=out_type,
                     mesh=plsc.ScalarSubcoreMesh(axis_name="sequencer", num_cores=1), scratch_types=sems,
                     compiler_params=pltpu.CompilerParams(collective_id=collective_id))(*arrays)


def _gather_weights(shards, *, name):
    nw = len(shards)

    def body(*refs):
        ins, outs = refs[:nw], refs[nw:2 * nw]
        send_sems, recv_sems, pass_send, pass_recv, local_sems = refs[2 * nw:]
        x, y, c = _place()
        chip = 2 * x + y
        chips = _other_chips(x, y)
        _handshake([(x, y, 1 - c)] + [(cx, cy, c) for cx, cy in chips])
        started = []
        for w in range(nw):
            hw = shards[w].shape[0] // 2
            mine = pl.ds(c * hw, hw)
            own = pltpu.make_async_copy(ins[w], outs[w].at[chip], local_sems.at[w])
            own.start()
            started.append(own)
            for j, (cx, cy) in enumerate(chips):
                cp = pltpu.make_async_remote_copy(
                    src_ref=ins[w].at[mine], dst_ref=outs[w].at[chip, mine], send_sem=send_sems.at[w, j],
                    recv_sem=recv_sems.at[w, j], device_id=(cx, cy, c), device_id_type=MESH)
                cp.start()
                started.append(cp)
        passed = []
        for w in range(nw):
            hw = shards[w].shape[0] // 2
            mine = pl.ds(c * hw, hw)
            for j, (cx, cy) in enumerate(chips):
                landed = outs[w].at[2 * cx + cy, mine]
                pltpu.make_async_remote_copy(
                    src_ref=ins[w].at[mine], dst_ref=landed, send_sem=send_sems.at[w, j],
                    recv_sem=recv_sems.at[w, j], device_id=(cx, cy, c), device_id_type=MESH).wait_recv()
                cp = pltpu.make_async_remote_copy(
                    src_ref=landed, dst_ref=landed, send_sem=pass_send.at[w, j], recv_sem=pass_recv.at[w, j],
                    device_id=(x, y, 1 - c), device_id_type=MESH)
                cp.start()
                passed.append(cp)
        for w in range(nw):
            hw = shards[w].shape[0] // 2
            theirs = pl.ds((1 - c) * hw, hw)
            for j, (cx, cy) in enumerate(chips):
                landed = outs[w].at[2 * cx + cy, theirs]
                pltpu.make_async_remote_copy(
                    src_ref=landed, dst_ref=landed, send_sem=pass_send.at[w, j], recv_sem=pass_recv.at[w, j],
                    device_id=(x, y, 1 - c), device_id_type=MESH).wait_recv()
        for cp in started[0::4]:
            cp.wait()
        for cp in [s for i, s in enumerate(started) if i % 4] + passed:
            cp.wait_send()

    sem = pltpu.SemaphoreType.DMA
    return _sequencer(body, shards, [_sds((N_CHIPS,) + a.shape, a.dtype) for a in shards],
                      [sem((nw, 3)), sem((nw, 3)), sem((nw, 3)), sem((nw, 3)), sem((nw,))], GATHER_ID, name)


def _swap_other_halves(grads, *, name):
    nw = len(grads)

    def body(*refs):
        ins, outs = refs[:nw], refs[nw:2 * nw]
        send_sems, recv_sems = refs[2 * nw:]
        x, y, c = _place()
        _handshake([(x, y, 1 - c)])
        cps = []
        for w in range(nw):
            hw = grads[w].shape[1] // 2
            cp = pltpu.make_async_remote_copy(
                src_ref=ins[w].at[:, pl.ds((1 - c) * hw, hw)], dst_ref=outs[w], send_sem=send_sems.at[w],
                recv_sem=recv_sems.at[w], device_id=(x, y, 1 - c), device_id_type=MESH)
            cp.start()
            cps.append(cp)
        for cp in cps:
            cp.wait()

    sem = pltpu.SemaphoreType.DMA
    return _sequencer(body, grads, [_sds((N_CHIPS, g.shape[1] // 2, g.shape[2]), g.dtype) for g in grads],
                      [sem((nw,)), sem((nw,))], SWAP_ID, name)


def _add_my_halves(core, grads, others, *, name, after=()):
    nw = len(grads)
    halves = [g.shape[1] // 2 for g in grads]

    def body(core_ref, *refs):
        outs = refs[2 * nw + len(after):]
        for g_ref, o_ref, out_ref in zip(refs[:nw], refs[nw:2 * nw], outs):
            out_ref[...] = (g_ref[...].astype(F32) + o_ref[...].astype(F32)).astype(out_ref.dtype)

    in_specs = [pl.BlockSpec((None, None, hw, g.shape[2]), lambda s, core_ref: (s, core_ref[0], 0, 0))
                for g, hw in zip(grads, halves)]
    in_specs += [pl.BlockSpec((None, hw, g.shape[2]), lambda s, core_ref: (s, 0, 0)) for g, hw in zip(grads, halves)]
    return pl.pallas_call(
        body,
        grid_spec=pltpu.PrefetchScalarGridSpec(
            num_scalar_prefetch=1, grid=(N_CHIPS,), in_specs=in_specs + [HBM_OPERAND] * len(after),
            out_specs=[pl.BlockSpec((None, hw, g.shape[2]), lambda s, core_ref: (s, 0, 0))
                       for g, hw in zip(grads, halves)]),
        out_shape=[_sds((N_CHIPS, hw, g.shape[2]), BF16) for g, hw in zip(grads, halves)],
        compiler_params=_cp(("parallel",)), name=name)(
            core, *[g.reshape(N_CHIPS, 2, hw, g.shape[2]) for g, hw in zip(grads, halves)], *others, *after)


def _scatter_partials(parts, *, name):
    nw = len(parts)

    def body(*refs):
        ins, outs = refs[:nw], refs[nw:2 * nw]
        send_sems, recv_sems = refs[2 * nw:]
        x, y, c = _place()
        _handshake([(cx, cy, c) for cx, cy in _other_chips(x, y)])
        cps = []
        for w in range(nw):
            for j, (cx, cy) in enumerate(_other_chips(x, y)):
                cp = pltpu.make_async_remote_copy(
                    src_ref=ins[w].at[2 * cx + cy], dst_ref=outs[w].at[j], send_sem=send_sems.at[w, j],
                    recv_sem=recv_sems.at[w, j], device_id=(cx, cy, c), device_id_type=MESH)
                cp.start()
                cps.append(cp)
        for cp in cps:
            cp.wait()

    sem = pltpu.SemaphoreType.DMA
    return _sequencer(body, parts, [_sds((3,) + p.shape[1:], p.dtype) for p in parts],
                      [sem((nw, 3)), sem((nw, 3))], SCATTER_ID, name)


SUM_STEPS = 2


def _sum_partials(chip, parts, recvd, *, name, after=()):
    nw = len(parts)
    rows = [p.shape[1] // SUM_STEPS for p in parts]

    def body(chip_ref, *refs):
        outs = refs[2 * nw + len(after):]
        for p_ref, r_ref, out_ref in zip(refs[:nw], refs[nw:2 * nw], outs):
            acc = p_ref[...].astype(F32)
            for j in range(3):
                acc = acc + r_ref[j].astype(F32)
            out_ref[...] = acc

    in_specs = [pl.BlockSpec((None, th, p.shape[2]), lambda i, chip_ref: (chip_ref[0], i, 0))
                for p, th in zip(parts, rows)]
    in_specs += [pl.BlockSpec((3, th, p.shape[2]), lambda i, chip_ref: (0, i, 0)) for p, th in zip(parts, rows)]
    return pl.pallas_call(
        body,
        grid_spec=pltpu.PrefetchScalarGridSpec(
            num_scalar_prefetch=1, grid=(SUM_STEPS,), in_specs=in_specs + [HBM_OPERAND] * len(after),
            out_specs=[pl.BlockSpec((th, p.shape[2]), lambda i, chip_ref: (i, 0)) for p, th in zip(parts, rows)]),
        out_shape=[_sds(p.shape[1:]) for p in parts], compiler_params=_cp(("parallel",)), name=name)(
            chip, *parts, *recvd, *after)


def _swap_reduced_halves(halves, *, name):
    nw = len(halves)

    def body(*refs):
        ins, outs = refs[:nw], refs[nw:2 * nw]
        send_sems, recv_sems = refs[2 * nw:]
        x, y, c = _place()
        _handshake([(x, y, 1 - c)])
        cps = []
        for w in range(nw):
            cp = pltpu.make_async_remote_copy(
                src_ref=ins[w], dst_ref=outs[w], send_sem=send_sems.at[w], recv_sem=recv_sems.at[w],
                device_id=(x, y, 1 - c), device_id_type=MESH)
            cp.start()
            cps.append(cp)
        for cp in cps:
            cp.wait()

    sem = pltpu.SemaphoreType.DMA
    return _sequencer(body, halves, [_sds(h.shape, h.dtype) for h in halves], [sem((nw,)), sem((nw,))], JOIN_ID, name)


def _exchange_rows(vec, *, name):
    def body(v_ref, slots, send_sems, recv_sems, local_sem):
        x, y, c = _place()
        me = 4 * x + 2 * y + c
        peers = []
        for mask in range(1, N_DEV):
            peers.append((1 - x if mask & 4 else x, 1 - y if mask & 2 else y, 1 - c if mask & 1 else c))
        _handshake(peers)
        own = pltpu.make_async_copy(v_ref, slots.at[me], local_sem)
        own.start()
        cps = []
        for k, peer in enumerate(peers):
            cp = pltpu.make_async_remote_copy(
                src_ref=v_ref, dst_ref=slots.at[me], send_sem=send_sems.at[k], recv_sem=recv_sems.at[k],
                device_id=peer, device_id_type=MESH)
            cp.start()
            cps.append(cp)
        for k, (px, py, pc) in enumerate(peers):
            pltpu.make_async_remote_copy(
                src_ref=v_ref, dst_ref=slots.at[4 * px + 2 * py + pc], send_sem=send_sems.at[k],
                recv_sem=recv_sems.at[k], device_id=(px, py, pc), device_id_type=MESH).wait_recv()
        for cp in cps:
            cp.wait_send()
        own.wait()

    sem = pltpu.SemaphoreType.DMA
    return _sequencer(body, [vec], [_sds((N_DEV,) + vec.shape)], [sem((N_DEV - 1,)), sem((N_DEV - 1,)), sem(())],
                      EXCHANGE_ID, name)[0]


def _sum_slots(slots, *, name, after=()):
    def body(s_ref, *rest):
        out_ref = rest[len(after)]
        acc = s_ref[0]
        for d in range(1, N_DEV):
            acc = acc + s_ref[d]
        out_ref[...] = acc

    vmem = pl.BlockSpec(memory_space=pltpu.VMEM)
    return pl.pallas_call(
        body, in_specs=[vmem] + [HBM_OPERAND] * len(after), out_specs=vmem, out_shape=_sds(slots.shape[1:]),
        compiler_params=pltpu.CompilerParams(vmem_limit_bytes=VMEM_LIMIT_BYTES), name=name)(slots, *after)


def _reduce_scatter_start(grads, core, *, tag, add_after=()):
    others = _swap_other_halves(grads, name="swap_other_halves_" + tag)
    parts = _add_my_halves(core, grads, others, name="add_my_halves_" + tag, after=add_after)
    return parts, _scatter_partials(parts, name="scatter_partials_" + tag)


def _reduce_scatter_finish(parts, recvd, chip, *, tag, sum_after=()):
    mine = _sum_partials(chip, parts, recvd, name="sum_partials_" + tag, after=sum_after)
    return mine, _swap_reduced_halves(mine, name="swap_reduced_halves_" + tag)


ADAM_BLOCK_ELEMS = 256 * 1024


def _adam_rows(rows, cols):
    tm = rows
    while tm * cols > ADAM_BLOCK_ELEMS and tm % 16 == 0:
        tm //= 2
    return tm


def _adam_step(wv, gv, mv, vv):
    m2 = ADAM_B1 * mv + (1.0 - ADAM_B1) * gv
    v2 = ADAM_B2 * vv + (1.0 - ADAM_B2) * (gv * gv)
    m_hat = m2 / (1.0 - ADAM_B1 ** ADAM_STEP)
    v_hat = v2 / (1.0 - ADAM_B2 ** ADAM_STEP)
    return -ADAM_LR * (m_hat / (jnp.sqrt(v_hat) + ADAM_EPS) + ADAM_WD * wv), m2, v2


def _adamw_each(ws, gs, ms, vs, *, name, after=()):
    n = len(ws)
    whole = pl.BlockSpec(memory_space=pltpu.VMEM)

    def body(*refs):
        ins, outs = refs[:4 * n], refs[4 * n + len(after):]
        for i in range(n):
            res = _adam_step(*(ins[k * n + i][...] for k in range(4)))
            for k in range(3):
                outs[k * n + i][...] = res[k]

    out = pl.pallas_call(body, in_specs=[whole] * (4 * n) + [HBM_OPERAND] * len(after),
                         out_shape=[_sds(w.shape) for w in ws] * 3, name=name)(*ws, *gs, *ms, *vs, *after)
    return out[:n], out[n:2 * n], out[2 * n:]


def _adamw_halves(core, w, g_mine, g_theirs, m, v, *, name, after=()):
    rows, cols = w.shape
    hw = rows // 2
    tm = _adam_rows(hw, cols)
    per_half = hw // tm

    def body(core_ref, w_ref, gm_ref, gt_ref, m_ref, v_ref, *rest):
        g_out, d_out, m_out, v_out = rest[len(after):]
        mine = (pl.program_id(0) // per_half) == core_ref[0]
        g = jnp.where(mine, gm_ref[...], gt_ref[...])
        d, m2, v2 = _adam_step(w_ref[...], g, m_ref[...], v_ref[...])
        g_out[...] = g
        d_out[...] = d
        m_out[...] = m2
        v_out[...] = v2

    full = pl.BlockSpec((tm, cols), lambda i, core_ref: (i, 0))

    def half(wanted):
        def index(i, core_ref):
            in_use = ((i // per_half) == core_ref[0]) == wanted
            return (jnp.where(in_use, i % per_half, 0), 0)
        return pl.BlockSpec((tm, cols), index)

    return pl.pallas_call(
        body,
        grid_spec=pltpu.PrefetchScalarGridSpec(
            num_scalar_prefetch=1, grid=(rows // tm,),
            in_specs=[full, half(True), half(False), full, full] + [HBM_OPERAND] * len(after),
            out_specs=[full, full, full, full]),
        out_shape=[_sds((rows, cols))] * 4, compiler_params=_cp(("parallel",)), name=name)(
            core, w, g_mine, g_theirs, m, v, *after)


HELD_TRANSPOSED = ("w_ff_gate", "w_ff_up")


def _as_rows(name, arr):
    return arr[0].T if name in HELD_TRANSPOSED else arr[0]


def _from_rows(name, arr2d):
    return (arr2d.T if name in HELD_TRANSPOSED else arr2d)[None]


STORED_SWAPPED = ("ssm_b_re", "ssm_b_im")


def _as_stored(name, arr):
    return jnp.swapaxes(arr, -1, -2) if name in STORED_SWAPPED else arr


def _pack_rows(arrs):
    flat = jnp.concatenate([a.reshape(-1).astype(F32) for a in arrs])
    rows = -(-flat.shape[0] // 1024) * 8
    return jnp.pad(flat, (0, rows * 128 - flat.shape[0])).reshape(rows, 128)


def _unpack_rows(vec, shapes):
    flat = vec.reshape(-1)
    out, off = [], 0
    for shp in shapes:
        size = math.prod(shp)
        out.append(flat[off:off + size].reshape(shp))
        off += size
    return out


SMALL = ("b_gate", "ssm_a_re", "ssm_a_im", "ssm_log_dt", "ssm_b_re", "ssm_b_im", "ssm_c_re", "ssm_c_im", "ssm_d",
         "ln1_g", "ln1_b", "ln2_g", "ln2_b")
GATHER_GROUPS = (("w_in", ("w_in",)), ("mixer", ("w_attn_br", "w_ssm_br", "w_glu", "w_out")),
                 ("ffn_up", ("w_ff_gate", "w_ff_up")), ("ffn_down", ("w_ff_down",)))
REDUCE_GROUPS = (("ffn", ("w_ff_down", "w_ff_gate", "w_ff_up")),
                 ("mixer", ("w_out", "w_ssm_br", "w_glu", "w_attn_br")), ("w_in", ("w_in",)))
WEIGHTS = ("w_in", "b_gate", "w_attn_br", "w_ssm_br", "w_out", "ssm_a_re", "ssm_a_im", "ssm_log_dt", "ssm_b_re",
           "ssm_b_im", "ssm_c_re", "ssm_c_im", "ssm_d", "w_glu", "ln1_g", "ln1_b", "w_ff_gate", "w_ff_up", "w_ff_down",
           "ln2_g", "ln2_b")


def kernel(x, w_in, b_gate, w_attn_br, w_ssm_br, w_out, ssm_a_re, ssm_a_im, ssm_log_dt, ssm_b_re, ssm_b_im, ssm_c_re, ssm_c_im, ssm_d, w_glu, ln1_g, ln1_b, w_ff_gate, w_ff_up, w_ff_down, ln2_g, ln2_b, loss_target, m_w_in, m_b_gate, m_w_attn_br, m_w_ssm_br, m_w_out, m_ssm_a_re, m_ssm_a_im, m_ssm_log_dt, m_ssm_b_re, m_ssm_b_im, m_ssm_c_re, m_ssm_c_im, m_ssm_d, m_w_glu, m_ln1_g, m_ln1_b, m_w_ff_gate, m_w_ff_up, m_w_ff_down, m_ln2_g, m_ln2_b, v_w_in, v_b_gate, v_w_attn_br, v_w_ssm_br, v_w_out, v_ssm_a_re, v_ssm_a_im, v_ssm_log_dt, v_ssm_b_re, v_ssm_b_im, v_ssm_c_re, v_ssm_c_im, v_ssm_d, v_w_glu, v_ln1_g, v_ln1_b, v_w_ff_gate, v_w_ff_up, v_w_ff_down, v_ln2_g, v_ln2_b):
    given = dict(locals())
    px, py, pc = _place()
    chip = 2 * px + py
    core_s = jnp.reshape(pc, (1,)).astype(jnp.int32)
    chip_s = jnp.reshape(chip, (1,)).astype(jnp.int32)

    wts = {}
    for tag, names in GATHER_GROUPS:
        wts.update(zip(names, _gather_weights([_as_rows(n, given[n]).astype(BF16) for n in names],
                                              name="gather_" + tag)))
    ncol = D_MODEL // N_CHIPS
    bg_mine = jnp.where(pc == 0, b_gate[0], jnp.zeros_like(b_gate[0]))
    bg_full = lax.dynamic_update_slice(jnp.zeros((2, D_MODEL), F32), bg_mine, (0, chip * ncol))
    bg_slots = _exchange_rows(bg_full.reshape(16, 128), name="exchange_gate_bias")
    bg_full = _sum_slots(bg_slots, name="sum_gate_bias").reshape(2, D_MODEL)
    small = {n: given[n][0] for n in SMALL if n.startswith("ssm")}
    small.update({n: given[n] for n in ("ln1_g", "ln1_b", "ln2_g", "ln2_b")})
    small["b_gate"] = bg_full

    groups = dict(REDUCE_GROUPS)
    parts, recvd, reduced, sent = {}, {}, {}, {}
    grads, delta, new_m, new_v, done = {}, {}, {}, {}, {}

    def start(tag, big_g, add_after):
        parts[tag], recvd[tag] = _reduce_scatter_start([big_g[n] for n in groups[tag]], core_s, tag=tag,
                                                       add_after=add_after)
        return parts[tag]

    def reduce_sum(tag, after):
        reduced[tag] = _reduce_scatter_finish(parts[tag], recvd[tag], chip_s, tag=tag, sum_after=after)
        return reduced[tag][0]

    def adam(tag, after):
        for n, g_mine, g_theirs in zip(groups[tag], *reduced[tag]):
            res = _adamw_halves(core_s, _as_rows(n, given[n]), g_mine, g_theirs, _as_rows(n, given["m_" + n]),
                                _as_rows(n, given["v_" + n]), name="adamw_" + n, after=after)
            done[n] = res[1]
            grads[n], delta[n], new_m[n], new_v[n] = [_from_rows(n, r) for r in res]

    def ffn_grads(big_g, norm_bwd):
        return start("ffn", big_g, (norm_bwd,))

    def mixer_grads(big_g, scan_bwd):
        return start("mixer", big_g, (scan_bwd, *reduce_sum("ffn", (scan_bwd,))))

    def small_grads(small_g, loss_mine):
        sent["stored"] = [_as_stored(n, small_g[n]) for n in SMALL] + [loss_mine.reshape(1)]
        packed = _pack_rows(sent["stored"])
        sent["slots"] = _exchange_rows(packed, name="exchange_small")
        return (packed,)

    grad_x_after, g_w_in, attention_bwd = _local_step(x[0], loss_target[0], wts, small,
                                                      ffn_grads, mixer_grads, small_grads)

    reduce_sum("mixer", (attention_bwd,))
    summed = _sum_slots(sent["slots"], name="sum_small", after=(g_w_in,))
    adam("mixer", (g_w_in,))
    start("w_in", {"w_in": g_w_in}, (summed, *[done[n] for n in groups["mixer"]]))
    in_flight = (parts["w_in"][0],)
    grad_x = grad_x_after(in_flight)
    adam("ffn", in_flight)
    summed = _unpack_rows(summed, [a.shape for a in sent["stored"]])
    loss = summed.pop()[0]
    at = SMALL.index("b_gate")
    summed[at] = lax.dynamic_slice(summed[at], (0, chip * ncol), (2, ncol))
    summed = [g.reshape(1, -1) if g.ndim == 1 else g for g in summed]
    held = [[_as_stored(n, given[prefix + n]).reshape(g.shape) for n, g in zip(SMALL, summed)]
            for prefix in ("", "m_", "v_")]
    small_out = _adamw_each(held[0], summed, held[1], held[2], name="adamw_small", after=in_flight)
    for out, arrs in zip((grads, delta, new_m, new_v), (summed, *small_out)):
        out.update((n, _as_stored(n, a).reshape(given[n].shape)) for n, a in zip(SMALL, arrs))
    reduce_sum("w_in", (*[done[n] for n in groups["ffn"]], small_out[0][0], grad_x))
    adam("w_in", ())

    return (loss, grad_x.reshape(x.shape), *[grads[n] for n in WEIGHTS], *[delta[n] for n in WEIGHTS],
            *[new_m[n] for n in WEIGHTS], *[new_v[n] for n in WEIGHTS])
```

```python
import math

import jax
import jax.numpy as jnp
from jax import lax
from jax.experimental import pallas as pl
from jax.experimental.pallas import tpu as pltpu
from jax.experimental.pallas import tpu_sc as plsc

F32 = jnp.float32
BF16 = jnp.bfloat16
MESH = pl.DeviceIdType.MESH

D_MODEL = 1024
SEQ = 2048
HEAD_DIM = 64
ATTN_HEADS = 8
DILATIONS = (1, 4, 16)
ATTN_WIDTH = ATTN_HEADS * HEAD_DIM
QKV_WIDTH = 3 * ATTN_WIDTH
BLOCK = 128
ROPE_THETA = 10000.0
NEG_INF = -1e30
SSM_GROUP = 16
SSM_GROUPS = 32
SSM_WIDTH = 512
SSM_STATE = 64
SSM_LANES = SSM_GROUPS * SSM_STATE
SCAN_CHUNKS = 8
SCAN_STEPS = SEQ // SCAN_CHUNKS
IN_WIDTH = 3 * QKV_WIDTH + SSM_WIDTH + 2 * D_MODEL
D_FF = 2816
N_CHIPS = 4
N_DEV = 8
DN_ALPHA = 2.0 ** 0.25
LN_EPS = 1e-5
ADAM_LR = 0.001
ADAM_B1 = 0.9
ADAM_B2 = 0.999
ADAM_EPS = 1e-08
ADAM_WD = 0.01
ADAM_STEP = 10
GELU_C = math.sqrt(2.0 / math.pi)
GELU_K = 0.044715

VMEM_LIMIT_BYTES = 56 * 1024 * 1024


def _sds(shape, dtype=F32):
    return jax.ShapeDtypeStruct(tuple(shape), dtype)


def _cp(semantics=None):
    return pltpu.CompilerParams(dimension_semantics=semantics, vmem_limit_bytes=VMEM_LIMIT_BYTES)


HBM_OPERAND = pl.BlockSpec(memory_space=pl.ANY)


def _matmul(a, b, *, grid, a_spec, b_spec, o_spec, out_shape, dims, k_axis=None, name, after=()):
    nk = grid[k_axis] if k_axis is not None else 1
    o_block = tuple(d for d in o_spec.block_shape if d is not None)
    n_after = len(after)

    def body(a_ref, b_ref, *rest):
        o_ref, acc = rest[n_after], rest[n_after + 1:]
        part = lax.dot_general(a_ref[...].astype(BF16), b_ref[...].astype(BF16),
                               (((dims[0],), (dims[1],)), ((), ())), preferred_element_type=F32)
        if k_axis is None:
            o_ref[...] = part.astype(o_ref.dtype)
        else:
            k = pl.program_id(k_axis)

            @pl.when(k == 0)
            def _():
                acc[0][...] = part

            @pl.when(k > 0)
            def _():
                acc[0][...] += part

            @pl.when(k == nk - 1)
            def _():
                o_ref[...] = acc[0][...].astype(o_ref.dtype)

    sem = tuple("arbitrary" if ax == k_axis else "parallel" for ax in range(len(grid)))
    return pl.pallas_call(
        body, grid=grid, in_specs=[a_spec, b_spec] + [HBM_OPERAND] * n_after, out_specs=o_spec, out_shape=out_shape,
        scratch_shapes=[pltpu.VMEM(o_block, F32)] if k_axis is not None else [],
        compiler_params=_cp(sem), name=name)(a, b, *after)


def _mm_cols_nt(dy, wg, *, tm, name, out_dtype=F32, after=()):
    k, ns = wg.shape[1], wg.shape[2]
    m = dy.shape[0]
    a_spec = pl.BlockSpec((tm, ns), lambda i, s: (i, s))
    return _matmul(dy, wg, grid=(m // tm, N_CHIPS), a_spec=a_spec,
                   b_spec=pl.BlockSpec((None, k, ns), lambda i, s: (s, 0, 0)),
                   o_spec=pl.BlockSpec((tm, k), lambda i, s: (i, 0)),
                   out_shape=_sds((m, k), out_dtype), dims=(1, 1), k_axis=1, name=name, after=after)


def _mm_cols_tn(a, dy, *, ns, name, after=()):
    m, k = a.shape
    return _matmul(a, dy, grid=(N_CHIPS,), a_spec=pl.BlockSpec((m, k), lambda s: (0, 0)),
                   b_spec=pl.BlockSpec((m, ns), lambda s: (0, s)),
                   o_spec=pl.BlockSpec((None, k, ns), lambda s: (s, 0, 0)),
                   out_shape=_sds((N_CHIPS, k, ns), BF16), dims=(0, 0), name=name, after=after)


def _mm_rows_tn(a, dy, *, name):
    m, k = a.shape
    rows, n = k // N_CHIPS, dy.shape[1]
    return _matmul(a, dy, grid=(N_CHIPS,), a_spec=pl.BlockSpec((m, rows), lambda s: (0, s)),
                   b_spec=pl.BlockSpec((m, n), lambda s: (0, 0)),
                   o_spec=pl.BlockSpec((None, rows, n), lambda s: (s, 0, 0)),
                   out_shape=_sds((N_CHIPS, rows, n), BF16), dims=(0, 0), name=name)


def _rowwise(fn, tiled, full, outs, accs=(), *, tm, name, after=()):
    args, in_specs = [], []
    for t in tiled:
        if isinstance(t, tuple):
            arr, w, cb = t
            in_specs.append(pl.BlockSpec((tm, w), lambda i, cb=cb: (i, cb)))
        else:
            arr = t
            in_specs.append(pl.BlockSpec((tm, arr.shape[1]), lambda i: (i, 0)))
        args.append(arr)
    rows = args[0].shape[0]
    for f in full:
        in_specs.append(pl.BlockSpec(f.shape, lambda i, nd=f.ndim: (0,) * nd))
        args.append(f)
    out_specs = [pl.BlockSpec((tm, o.shape[1]), lambda i: (i, 0)) for o in outs]
    out_specs += [pl.BlockSpec(a.shape, lambda i, nd=len(a.shape): (0,) * nd) for a in accs]
    n_in, n_out = len(args), len(outs)
    in_specs += [HBM_OPERAND] * len(after)
    first_out = n_in + len(after)

    def body(*refs):
        res = fn(*[r[...] for r in refs[:n_in]])
        res = res if isinstance(res, (tuple, list)) else (res,)
        for r, v in zip(refs[first_out:first_out + n_out], res[:n_out]):
            r[...] = v.astype(r.dtype)
        i = pl.program_id(0)
        for r, v in zip(refs[first_out + n_out:], res[n_out:]):
            @pl.when(i == 0)
            def _(r=r, v=v):
                r[...] = v

            @pl.when(i > 0)
            def _(r=r, v=v):
                r[...] += v

    res = pl.pallas_call(
        body, grid=(rows // tm,), in_specs=in_specs, out_specs=out_specs, out_shape=list(outs) + list(accs),
        compiler_params=_cp(("arbitrary",) if accs else ("parallel",)), name=name)(*args, *after)
    return res


def _colsum(v):
    return jnp.sum(v, axis=0, keepdims=True)


def _ln_stats(z):
    mu = jnp.mean(z, axis=-1, keepdims=True)
    zc = z - mu
    var = jnp.mean(zc * zc, axis=-1, keepdims=True)
    rstd = lax.rsqrt(var + LN_EPS)
    return zc * rstd, rstd


def _ln_bwd(dy, xhat, rstd, g):
    dxh = dy * g
    m1 = jnp.mean(dxh, axis=-1, keepdims=True)
    m2 = jnp.mean(dxh * xhat, axis=-1, keepdims=True)
    return rstd * (dxh - m1 - xhat * m2)


def _swap_halves(t):
    w = t.shape[-1]
    lane = lax.broadcasted_iota(jnp.int32, t.shape, t.ndim - 1)
    return jnp.where((lane % HEAD_DIM) < HEAD_DIM // 2, pltpu.roll(t, w - HEAD_DIM // 2, t.ndim - 1),
                     pltpu.roll(t, HEAD_DIM // 2, t.ndim - 1))


PHASES = max(DILATIONS)
PAIR = 2 * HEAD_DIM
UNITS = SEQ // BLOCK
UNIT_BATCH = 16
ROPE_ROWS = 256
TAIL_COLS = 256


def _to_phase_rows(t):
    return t.reshape(SEQ // PHASES, PHASES, t.shape[1]).transpose(1, 0, 2).reshape(t.shape)


def _reorder_rows(arr, plus=None, *, to_phase, name, scale=1.0):
    def body(*refs):
        o_ref = refs[-1]
        for rho in range(PHASES):
            phase = pl.ds(rho * BLOCK, BLOCK)
            strided = pl.ds(rho, BLOCK, stride=PHASES)
            src, dst = (strided, phase) if to_phase else (phase, strided)
            val = refs[0][src, :]
            if scale != 1.0:
                val = val * scale
            if plus is not None:
                val = val + refs[1][src, :]
            o_ref[dst, :] = val

    spec = pl.BlockSpec((SEQ, BLOCK), lambda j: (0, j))
    ins = [arr] if plus is None else [arr, plus]
    return pl.pallas_call(body, grid=(arr.shape[1] // BLOCK,), in_specs=[spec] * len(ins), out_specs=spec,
                          out_shape=_sds(arr.shape), compiler_params=_cp(("parallel",)), name=name)(*ins)


def _rope(t, cf, ss):
    return t * cf + _swap_halves(t) * ss


def _rope_transposed(d, cf, ss):
    return d * cf + _swap_halves(d * ss)


def _unit_pieces(u, dil):
    pieces, length = PHASES // dil, 8 * dil
    if dil == 1:
        rho, i = 0, u
    elif dil == PHASES:
        rho, i = u, 0
    else:
        rho, i = jnp.bitwise_and(u, dil - 1), jnp.right_shift(u, dil.bit_length() - 1)
    before = jnp.maximum(i - 1, 0)
    cur = [pl.multiple_of((rho + dil * k) * BLOCK + length * i, 8) for k in range(pieces)]
    prev = [pl.multiple_of((rho + dil * k) * BLOCK + length * before, 8) for k in range(pieces)]
    return i, cur, prev


def _load_tile(ref, starts, dil):
    return jnp.concatenate([ref[pl.ds(st, 8 * dil), :] for st in starts], axis=0)


def _store_tile(ref, starts, dil, val, head=None, accumulate=False):
    length = 8 * dil
    lanes = slice(None) if head is None else pl.ds(head * HEAD_DIM, HEAD_DIM)
    cols = slice(None) if head is None else slice(head * HEAD_DIM, (head + 1) * HEAD_DIM)
    for k, st in enumerate(starts):
        piece = val[k * length:(k + 1) * length, cols]
        if accumulate:
            ref[pl.ds(st, length), lanes] += piece
        else:
            ref[pl.ds(st, length), lanes] = piece


def _tile_position(idx, dil):
    pieces, length = PHASES // dil, 8 * dil
    return pieces * jnp.bitwise_and(idx, length - 1) + jnp.right_shift(idx, length.bit_length() - 1)


def _band_mask(i, dil):
    row = lax.broadcasted_iota(jnp.int32, (BLOCK, 2 * BLOCK), 0)
    col = lax.broadcasted_iota(jnp.int32, (BLOCK, 2 * BLOCK), 1)
    key_pos = _tile_position(jnp.bitwise_and(col, BLOCK - 1), dil) + jnp.where(col >= BLOCK, 0, -BLOCK)
    dist = _tile_position(row, dil) - key_pos
    return (dist >= 0) & (dist <= BLOCK) & ((col >= BLOCK) | (i > 0))


def _causal_mask():
    row = lax.broadcasted_iota(jnp.int32, (BLOCK, BLOCK), 0)
    col = lax.broadcasted_iota(jnp.int32, (BLOCK, BLOCK), 1)
    return row >= col


def _pair_views(col0):
    return [pl.BlockSpec((SEQ, PAIR), lambda hp, g=g: (0, col0 // PAIR + g * (ATTN_WIDTH // PAIR) + hp))
            for g in range(len(DILATIONS))]


def _project_in(x, wg, cos_f, sin_s):
    ns = wg.shape[2]
    tiles = ns // PAIR

    def body(x_ref, w_ref, cf_ref, ss_ref, o_ref):
        shard = pl.program_id(1)
        xb = x_ref[...].astype(BF16)
        cf, ss = cf_ref[...], ss_ref[...]

        def write(rotated, scaled):
            for t0 in range(0, tiles, 2):
                strip = jnp.dot(xb, w_ref[:, t0 * PAIR:(t0 + 2) * PAIR], preferred_element_type=F32)
                for t in (t0, t0 + 1):
                    val = strip[:, (t - t0) * PAIR:(t - t0 + 1) * PAIR]
                    if t < rotated:
                        val = _rope(val, cf, ss)
                        if t < scaled:
                            val = val * (1.0 / math.sqrt(HEAD_DIM))
                    o_ref[:, t * PAIR:(t + 1) * PAIR] = val

        for s in range(N_CHIPS):
            rotated = min(max(2 * QKV_WIDTH - s * ns, 0), ns) // PAIR
            scaled = min(max(QKV_WIDTH - s * ns, 0), ns) // PAIR

            @pl.when(shard == s)
            def _(rotated=rotated, scaled=scaled):
                write(rotated, scaled)

    table = pl.BlockSpec((FF_ROWS, PAIR), lambda i, s: (i, 0))
    return pl.pallas_call(
        body, grid=(SEQ // FF_ROWS, N_CHIPS),
        in_specs=[pl.BlockSpec((FF_ROWS, D_MODEL), lambda i, s: (i, 0)),
                  pl.BlockSpec((None, D_MODEL, ns), lambda i, s: (s, 0, 0)), table, table],
        out_specs=pl.BlockSpec((FF_ROWS, ns), lambda i, s: (i, s)), out_shape=_sds((SEQ, N_CHIPS * ns)),
        compiler_params=_cp(("parallel", "parallel")), name="project_in")(x, wg, cos_f, sin_s)


def _attention_fwd(proj):
    ng = len(DILATIONS)

    def body(*refs):
        q_refs, k_refs, v_refs = refs[:ng], refs[ng:2 * ng], refs[2 * ng:3 * ng]
        attn_ref, lse_ref = refs[3 * ng:]
        qr_refs, kr_refs = q_refs, k_refs
        first = lax.broadcasted_iota(jnp.int32, (BLOCK, PAIR), 1) < HEAD_DIM
        for g, dil in enumerate(DILATIONS):
            two_blocks = SEQ // dil > BLOCK

            def units(t, carry, g=g, dil=dil, two_blocks=two_blocks):
                picked = [_unit_pieces(t * UNIT_BATCH + j, dil) for j in range(UNIT_BATCH)]

                def tiles(ref, with_prev=False):
                    if with_prev and two_blocks:
                        return jnp.stack([jnp.concatenate([_load_tile(ref, prev, dil), _load_tile(ref, rows, dil)],
                                                          axis=0) for _, rows, prev in picked])
                    return jnp.stack([_load_tile(ref, rows, dil) for _, rows, _ in picked])

                qq = tiles(qr_refs[g]).astype(BF16)
                kk = tiles(kr_refs[g], True).astype(BF16)
                vv = tiles(v_refs[g], True).astype(BF16)
                if two_blocks:
                    valid = jnp.stack([_band_mask(i, dil) for i, _, _ in picked])
                else:
                    valid = _causal_mask()[None]
                mine = first[None]
                zero = jnp.zeros_like(qq)
                outs, lses = [], []
                for qh in (jnp.where(mine, qq, zero), jnp.where(mine, zero, qq)):
                    s = jnp.einsum("pqd,pkd->pqk", qh, kk, preferred_element_type=F32)
                    s = jnp.where(valid, s, NEG_INF)
                    m = jnp.max(s, axis=-1, keepdims=True)
                    p = jnp.exp(s - m)
                    l = jnp.sum(p, axis=-1, keepdims=True)
                    outs.append(jnp.einsum("pqk,pkd->pqd", p.astype(BF16), vv, preferred_element_type=F32) * (1.0 / l))
                    lses.append(m + jnp.log(l))
                o = jnp.where(mine, outs[0], outs[1])
                lse = jnp.where(mine, lses[0], lses[1])
                if g > 0:
                    lse_old = tiles(lse_ref)
                    m = jnp.maximum(lse_old, lse)
                    lse_new = m + jnp.log(jnp.exp(lse_old - m) + jnp.exp(lse - m))
                    o = tiles(attn_ref) * jnp.exp(lse_old - lse_new) + o * jnp.exp(lse - lse_new)
                    lse = lse_new
                for j, (_, rows, _) in enumerate(picked):
                    _store_tile(attn_ref, rows, dil, o[j])
                    _store_tile(lse_ref, rows, dil, lse[j])
                return carry

            lax.fori_loop(0, UNITS // UNIT_BATCH, units, 0)

    out = pl.BlockSpec((SEQ, PAIR), lambda hp: (0, hp))
    return pl.pallas_call(
        body, grid=(ATTN_WIDTH // PAIR,),
        in_specs=_pair_views(0) + _pair_views(QKV_WIDTH) + _pair_views(2 * QKV_WIDTH),
        out_specs=[out, out], out_shape=[_sds((SEQ, ATTN_WIDTH)), _sds((SEQ, ATTN_WIDTH))],
        compiler_params=_cp(("parallel",)), name="attention_fwd")(*([proj] * (3 * ng)))


def _attention_bwd(proj, cos_f, sin_s, d_attn, attn, lse, d_u, d_gl):
    pairs = ATTN_WIDTH // PAIR
    last = len(DILATIONS) * pairs - 1

    def accumulate(dil, qr_ref, kr_ref, v_ref, do_ref, o_ref, lse_ref, dq_acc, dk_acc, dv_acc):
        two_blocks = SEQ // dil > BLOCK
        dk_acc[...] = jnp.zeros_like(dk_acc)
        dv_acc[...] = jnp.zeros_like(dv_acc)
        nk = 2 * BLOCK if two_blocks else BLOCK
        first = lax.broadcasted_iota(jnp.int32, (BLOCK, PAIR), 1) < HEAD_DIM
        first_k = lax.broadcasted_iota(jnp.int32, (nk, PAIR), 1) < HEAD_DIM

        def units(t, carry):
            picked = [_unit_pieces(t * UNIT_BATCH + j, dil) for j in range(UNIT_BATCH)]

            def tiles(ref, with_prev=False):
                if with_prev and two_blocks:
                    return jnp.stack([jnp.concatenate([_load_tile(ref, prev, dil), _load_tile(ref, rows, dil)], axis=0)
                                      for _, rows, prev in picked])
                return jnp.stack([_load_tile(ref, rows, dil) for _, rows, _ in picked])

            qq = tiles(qr_ref).astype(BF16)
            kk = tiles(kr_ref, True).astype(BF16)
            vv = tiles(v_ref, True).astype(BF16)
            dof = tiles(do_ref)
            dd = dof * tiles(o_ref)
            lse3 = tiles(lse_ref)
            dob = dof.astype(BF16)
            if two_blocks:
                valid = jnp.stack([_band_mask(i, dil) for i, _, _ in picked])
            else:
                valid = _causal_mask()[None]
            zq, zf = jnp.zeros_like(qq), jnp.zeros_like(dd)
            dqs, dks, dvs = [], [], []
            for head in range(2):
                mine = first[None] if head == 0 else jnp.logical_not(first)[None]
                delta = jnp.sum(jnp.where(mine, dd, zf), axis=-1, keepdims=True)
                lse_h = lse3[:, :, head * HEAD_DIM:head * HEAD_DIM + 1]
                s = jnp.einsum("pqd,pkd->pqk", jnp.where(mine, qq, zq), kk, preferred_element_type=F32)
                p = jnp.where(valid, jnp.exp(s - lse_h), 0.0)
                dp = jnp.einsum("pqd,pkd->pqk", jnp.where(mine, dob, zq), vv, preferred_element_type=F32)
                ds = (p * (dp - delta)).astype(BF16)
                dqs.append(jnp.einsum("pqk,pkd->pqd", ds, kk, preferred_element_type=F32))
                dks.append(jnp.einsum("pqk,pqd->pkd", ds, qq, preferred_element_type=F32))
                dvs.append(jnp.einsum("pqk,pqd->pkd", p.astype(BF16), dob, preferred_element_type=F32))
            dq = jnp.where(first[None], dqs[0], dqs[1])
            dk = jnp.where(first_k[None], dks[0], dks[1])
            dv = jnp.where(first_k[None], dvs[0], dvs[1])
            for j, (_, rows, prev) in enumerate(picked):
                _store_tile(dq_acc, rows, dil, dq[j])
                _store_tile(dk_acc, rows, dil, dk[j, nk - BLOCK:], accumulate=True)
                _store_tile(dv_acc, rows, dil, dv[j, nk - BLOCK:], accumulate=True)
                if two_blocks:
                    _store_tile(dk_acc, prev, dil, dk[j, :BLOCK], accumulate=True)
                    _store_tile(dv_acc, prev, dil, dv[j, :BLOCK], accumulate=True)
            return carry

        lax.fori_loop(0, UNITS // UNIT_BATCH, units, 0)

    def body(qr_ref, kr_ref, v_ref, cf_ref, ss_ref, do_ref, o_ref, lse_ref, du_ref, dgl_ref, out_ref,
             dq_acc, dk_acc, dv_acc, dq_buf, dk_buf, dv_buf, sems):
        step = pl.program_id(0) * pairs + pl.program_id(1)

        def columns(at):
            return [pltpu.make_async_copy(
                buf, out_ref.at[:, pl.ds(pl.multiple_of(j * QKV_WIDTH + at * PAIR, PAIR), PAIR)], sems.at[j])
                for j, buf in enumerate((dq_buf, dk_buf, dv_buf))]

        def tail(ref, col0, at):
            cols = pl.ds(pl.multiple_of(col0 + at * TAIL_COLS, TAIL_COLS), TAIL_COLS)
            return pltpu.make_async_copy(ref, out_ref.at[:, cols], sems.at[3])

        gl_steps, u_steps = 2 * D_MODEL // TAIL_COLS, SSM_WIDTH // TAIL_COLS
        from_gl = step < gl_steps
        from_u = jnp.logical_and(step >= gl_steps, step < gl_steps + u_steps)
        tail_gl = tail(dgl_ref, 3 * QKV_WIDTH + SSM_WIDTH, step)
        tail_u = tail(du_ref, 3 * QKV_WIDTH, step - gl_steps)
        pl.when(from_gl)(tail_gl.start)
        pl.when(from_u)(tail_u.start)

        for g, dil in enumerate(DILATIONS):
            @pl.when(pl.program_id(0) == g)
            def _(dil=dil):
                accumulate(dil, qr_ref, kr_ref, v_ref, do_ref, o_ref, lse_ref, dq_acc, dk_acc, dv_acc)

        @pl.when(step > 0)
        def _():
            for cp in columns(step - 1):
                cp.wait()

        def finish(t, carry):
            rows = pl.ds(pl.multiple_of(t * ROPE_ROWS, ROPE_ROWS), ROPE_ROWS)
            cf, ss = cf_ref[rows, :], ss_ref[rows, :]
            dq = dq_acc[rows, :] * (1.0 / math.sqrt(HEAD_DIM))
            dq_buf[rows, :] = _rope_transposed(dq, cf, ss).astype(BF16)
            dk_buf[rows, :] = _rope_transposed(dk_acc[rows, :], cf, ss).astype(BF16)
            dv_buf[rows, :] = dv_acc[rows, :].astype(BF16)
            return carry

        lax.fori_loop(0, SEQ // ROPE_ROWS, finish, 0)
        for cp in columns(step):
            cp.start()
        pl.when(from_gl)(tail_gl.wait)
        pl.when(from_u)(tail_u.wait)

        @pl.when(step == last)
        def _():
            for cp in columns(step):
                cp.wait()

    whole = pl.BlockSpec((SEQ, PAIR), lambda g, hp: (0, 0))
    pair = pl.BlockSpec((SEQ, PAIR), lambda g, hp: (0, hp))
    views = [pl.BlockSpec((SEQ, PAIR), lambda g, hp, c0=col0 // PAIR: (0, c0 + g * pairs + hp))
             for col0 in (0, QKV_WIDTH, 2 * QKV_WIDTH)]
    gl_blocks, u_blocks = 2 * D_MODEL // TAIL_COLS, SSM_WIDTH // TAIL_COLS
    assert gl_blocks + u_blocks <= last + 1
    gl_spec = pl.BlockSpec((SEQ, TAIL_COLS), lambda g, hp: (0, jnp.minimum(g * pairs + hp, gl_blocks - 1)))
    u_spec = pl.BlockSpec((SEQ, TAIL_COLS),
                          lambda g, hp: (0, jnp.clip(g * pairs + hp - gl_blocks, 0, u_blocks - 1)))
    return pl.pallas_call(
        body, grid=(len(DILATIONS), pairs),
        in_specs=views + [whole, whole, pair, pair, pair, u_spec, gl_spec],
        out_specs=HBM_OPERAND, out_shape=_sds((SEQ, IN_WIDTH), BF16),
        scratch_shapes=[pltpu.VMEM((SEQ, PAIR), F32)] * 3 + [pltpu.VMEM((SEQ, PAIR), BF16)] * 3
        + [pltpu.SemaphoreType.DMA((4,))],
        compiler_params=_cp(("arbitrary", "arbitrary")), name="attention_bwd")(
            proj, proj, proj, cos_f, sin_s, d_attn, attn, lse, d_u, d_gl)


def _cmul(ar, ai, br, bi):
    return ar * br - ai * bi, ar * bi + ai * br


def _pow256(ar, ai):
    for _ in range(8):
        ar, ai = _cmul(ar, ai, ar, ai)
    return ar, ai


def _chunk_carries(first_r, first_i, pr, pi, reverse):
    rows = lax.broadcasted_iota(jnp.int32, first_r.shape, 0)
    out_r = jnp.zeros_like(first_r)
    out_i = jnp.zeros_like(first_i)
    hr = jnp.zeros_like(first_r[0:1])
    hi = jnp.zeros_like(hr)
    order = range(SCAN_CHUNKS - 1, -1, -1) if reverse else range(SCAN_CHUNKS)
    for c in order:
        out_r = jnp.where(rows == c, hr, out_r)
        out_i = jnp.where(rows == c, hi, out_i)
        tr, ti = _cmul(pr[0:1], pi[0:1], hr, hi)
        hr = first_r[c:c + 1] + tr
        hi = first_i[c:c + 1] + ti
    return out_r, out_i


def _tile(j):
    return pl.ds(pl.multiple_of(j * SCAN_CHUNKS, SCAN_CHUNKS), SCAN_CHUNKS)


def _to_scan_rows(t):
    per = SCAN_STEPS // PHASES
    return t.reshape(PHASES, SCAN_CHUNKS, per, t.shape[1]).transpose(2, 0, 1, 3).reshape(t.shape)


def _from_scan_rows(t):
    per = SCAN_STEPS // PHASES
    return t.reshape(per, PHASES, SCAN_CHUNKS, t.shape[1]).transpose(1, 2, 0, 3).reshape(t.shape)


def _scan_in_place(hr_ref, hi_ref, a_r, a_i):
    def local(j, carry):
        tr, ti = _cmul(a_r, a_i, carry[0], carry[1])
        nr = tr + hr_ref[_tile(j), :]
        ni = ti + hi_ref[_tile(j), :]
        hr_ref[_tile(j), :] = nr
        hi_ref[_tile(j), :] = ni
        return nr, ni

    zero = jnp.zeros_like(a_r)
    last_r, last_i = lax.fori_loop(0, SCAN_STEPS, local, (zero, zero), unroll=4)
    pr, pi = _pow256(a_r, a_i)
    er, ei = _chunk_carries(last_r, last_i, pr, pi, reverse=False)

    def fix(j, carry):
        tr, ti = _cmul(carry[0], carry[1], er, ei)
        hr_ref[_tile(j), :] += tr
        hi_ref[_tile(j), :] += ti
        return _cmul(carry[0], carry[1], a_r, a_i)

    lax.fori_loop(0, SCAN_STEPS, fix, (a_r, a_i), unroll=4)
    return er, ei


def _reverse_scan_in_place(lr_ref, li_ref, hr_ref, hi_ref, er, ei, a_r, a_i):
    def local(t, carry):
        j = SCAN_STEPS - 1 - t
        tr, ti = _cmul(a_r, a_i, carry[0], carry[1])
        nr = tr + lr_ref[_tile(j), :]
        ni = ti + li_ref[_tile(j), :]
        lr_ref[_tile(j), :] = nr
        li_ref[_tile(j), :] = ni
        return nr, ni

    zero = jnp.zeros_like(a_r)
    first_r, first_i = lax.fori_loop(0, SCAN_STEPS, local, (zero, zero), unroll=4)
    pr, pi = _pow256(a_r, a_i)
    nxt_r, nxt_i = _chunk_carries(first_r, first_i, pr, pi, reverse=True)

    def accumulate(lam_r, lam_i, hp_r, hp_i, acc):
        return (acc[0] + lam_r * hp_r + lam_i * hp_i, acc[1] + lam_i * hp_r - lam_r * hp_i)

    def fix(t, carry):
        qr, qi, acc_r, acc_i = carry
        j = SCAN_STEPS - 1 - t
        tr, ti = _cmul(qr, qi, nxt_r, nxt_i)
        lam_r = lr_ref[_tile(j), :] + tr
        lam_i = li_ref[_tile(j), :] + ti
        lr_ref[_tile(j), :] = lam_r
        li_ref[_tile(j), :] = lam_i
        acc_r, acc_i = accumulate(lam_r, lam_i, hr_ref[_tile(j - 1), :], hi_ref[_tile(j - 1), :], (acc_r, acc_i))
        qr, qi = _cmul(qr, qi, a_r, a_i)
        return qr, qi, acc_r, acc_i

    qr, qi, acc_r, acc_i = lax.fori_loop(0, SCAN_STEPS - 1, fix, (a_r, a_i, zero, zero), unroll=4)
    tr, ti = _cmul(qr, qi, nxt_r, nxt_i)
    lam_r = lr_ref[_tile(0), :] + tr
    lam_i = li_ref[_tile(0), :] + ti
    lr_ref[_tile(0), :] = lam_r
    li_ref[_tile(0), :] = lam_i
    acc_r, acc_i = accumulate(lam_r, lam_i, er, ei, (acc_r, acc_i))
    return jnp.sum(acc_r, axis=0, keepdims=True), jnp.sum(acc_i, axis=0, keepdims=True)


def _rope_tables():
    half = HEAD_DIM // 2
    inv_freq = ROPE_THETA ** (-jnp.arange(half, dtype=F32) / half)
    ang = jnp.arange(SEQ, dtype=F32)[:, None] * inv_freq[None, :]
    cos, sin = jnp.cos(ang), jnp.sin(ang)
    cos_f = jnp.concatenate([cos, cos, cos, cos], axis=1)
    sin_s = jnp.concatenate([-sin, sin, -sin, sin], axis=1)
    return cos_f, sin_s


def _ssm_discretise(a_re, a_im, log_dt, b_re, b_im):
    lam = lax.complex(a_re, a_im)
    dt = jnp.exp(log_dt)[:, None]
    a_bar = jnp.exp(lam * dt)
    b_bar = ((a_bar - 1.0) / lam)[..., None] * lax.complex(b_re, b_im)
    return a_bar.real, a_bar.imag, b_bar.real, b_bar.imag


SSM_SLABS = 4
SLAB_GROUPS = SSM_GROUPS // SSM_SLABS
SLAB_IN = SSM_WIDTH // SSM_SLABS
SLAB_STATE = SSM_LANES // SSM_SLABS


def _slab_block_diag(blocks):
    _, r, c = blocks.shape
    eye = jnp.eye(SLAB_GROUPS, dtype=blocks.dtype)
    b5 = blocks.reshape(SSM_SLABS, SLAB_GROUPS, r, 1, c) * eye[None, :, None, :, None]
    return b5.reshape(SSM_SLABS, SLAB_GROUPS * r, SLAB_GROUPS * c)


def _diag_blocks(a, b):
    ra, cb = a.shape[1], b.shape[1]
    wa, wb = ra // SLAB_GROUPS, cb // SLAB_GROUPS
    d = lax.dot_general(a, b, (((0,), (0,)), ((), ())), preferred_element_type=F32)
    row_g = jnp.right_shift(lax.broadcasted_iota(jnp.int32, (ra, cb), 0), wa.bit_length() - 1)
    col_g = jnp.right_shift(lax.broadcasted_iota(jnp.int32, (ra, cb), 1), wb.bit_length() - 1)
    d = jnp.where(row_g == col_g, d, 0.0)
    fold = (jnp.bitwise_and(lax.broadcasted_iota(jnp.int32, (cb, wb), 0), wb - 1)
            == lax.broadcasted_iota(jnp.int32, (cb, wb), 1)).astype(F32)
    return jnp.dot(d, fold, preferred_element_type=F32, precision=lax.Precision.HIGHEST)


def _slab_specs():
    tok = pl.BlockSpec((SEQ, SLAB_IN), lambda j: (0, j))
    state = pl.BlockSpec((SEQ, SLAB_STATE), lambda j: (0, j))
    b_in = pl.BlockSpec((None, SLAB_IN, SLAB_STATE), lambda j: (j, 0, 0))
    c_out = pl.BlockSpec((None, SLAB_STATE, SLAB_IN), lambda j: (j, 0, 0))
    vec = pl.BlockSpec((1, SLAB_STATE), lambda j: (0, j))
    ent = pl.BlockSpec((SCAN_CHUNKS, SLAB_STATE), lambda j: (0, j))
    return tok, state, b_in, c_out, vec, ent


def _ssm_forward(u, b_in_r, b_in_i, c_out_r, c_out_ni, a_r, a_i):
    def body(u_ref, br_ref, bi_ref, cr_ref, ci_ref, ar_ref, ai_ref, y_ref, hr_ref, hi_ref, er_ref, ei_ref):
        uu = u_ref[...]
        hr_ref[...] = jnp.dot(uu, br_ref[...], preferred_element_type=F32)
        hi_ref[...] = jnp.dot(uu, bi_ref[...], preferred_element_type=F32)
        a_re = jnp.broadcast_to(ar_ref[...], (SCAN_CHUNKS, SLAB_STATE))
        a_im = jnp.broadcast_to(ai_ref[...], (SCAN_CHUNKS, SLAB_STATE))
        er_ref[...], ei_ref[...] = _scan_in_place(hr_ref, hi_ref, a_re, a_im)
        y_ref[...] = (jnp.dot(hr_ref[...].astype(BF16), cr_ref[...], preferred_element_type=F32)
                      + jnp.dot(hi_ref[...].astype(BF16), ci_ref[...], preferred_element_type=F32))

    tok, state, b_in, c_out, vec, ent = _slab_specs()
    return pl.pallas_call(
        body, grid=(SSM_SLABS,), in_specs=[tok, b_in, b_in, c_out, c_out, vec, vec],
        out_specs=[tok, state, state, ent, ent],
        out_shape=[_sds((SEQ, SSM_WIDTH)), _sds((SEQ, SSM_LANES)), _sds((SEQ, SSM_LANES)),
                   _sds((SCAN_CHUNKS, SSM_LANES)), _sds((SCAN_CHUNKS, SSM_LANES))],
        compiler_params=_cp(("parallel",)), name="ssm_forward")(u, b_in_r, b_in_i, c_out_r, c_out_ni, a_r, a_i)


def _ssm_backward(d_y, d_u_skip, u, h_r, h_i, e_r, e_i, b_in_r, b_in_i, c_out_r, c_out_ni, a_r, a_i):
    def body(dy_ref, skip_ref, u_ref, hr_ref, hi_ref, er_ref, ei_ref, br_ref, bi_ref, cr_ref, ci_ref, ar_ref, ai_ref,
             du_ref, dar_ref, dai_ref, dcr_ref, dci_ref, dbr_ref, dbi_ref, lr_ref, li_ref):
        dy = dy_ref[...]
        lr_ref[...] = _dot_nt(dy, cr_ref[...])
        li_ref[...] = _dot_nt(dy, ci_ref[...])
        a_re = jnp.broadcast_to(ar_ref[...], (SCAN_CHUNKS, SLAB_STATE))
        a_im = -jnp.broadcast_to(ai_ref[...], (SCAN_CHUNKS, SLAB_STATE))
        dar_ref[...], dai_ref[...] = _reverse_scan_in_place(lr_ref, li_ref, hr_ref, hi_ref, er_ref[...], ei_ref[...],
                                                            a_re, a_im)
        dcr_ref[...] = _diag_blocks(dy, hr_ref[...].astype(BF16))
        dci_ref[...] = _diag_blocks(dy, hi_ref[...].astype(BF16))
        lam_r, lam_i = lr_ref[...].astype(BF16), li_ref[...].astype(BF16)
        uu = u_ref[...]
        dbr_ref[...] = _diag_blocks(uu, lam_r)
        dbi_ref[...] = _diag_blocks(uu, lam_i)
        du = skip_ref[...] + _dot_nt(lam_r, br_ref[...]) + _dot_nt(lam_i, bi_ref[...])
        du_ref[...] = du.astype(BF16)

    tok, state, b_in, c_out, vec, ent = _slab_specs()
    db = pl.BlockSpec((SLAB_IN, SSM_STATE), lambda j: (j, 0))
    return pl.pallas_call(
        body, grid=(SSM_SLABS,), in_specs=[tok, tok, tok, state, state, ent, ent, b_in, b_in, c_out, c_out, vec, vec],
        out_specs=[tok, vec, vec, db, db, db, db],
        out_shape=[_sds((SEQ, SSM_WIDTH), BF16), _sds((1, SSM_LANES)), _sds((1, SSM_LANES))]
        + [_sds((SSM_WIDTH, SSM_STATE))] * 4,
        scratch_shapes=[pltpu.VMEM((SEQ, SLAB_STATE), F32)] * 2,
        compiler_params=_cp(("parallel",)), name="ssm_backward")(
            d_y, d_u_skip, u, h_r, h_i, e_r, e_i, b_in_r, b_in_i, c_out_r, c_out_ni, a_r, a_i)


FF_ROWS = 1024
FF_SHARD = D_FF // N_CHIPS


def _dot_nt(a, b):
    return lax.dot_general(a, b, (((1,), (1,)), ((), ())), preferred_element_type=F32)


def _ffn_up(h, w_gate_t, w_up_t):
    def body(h_ref, wg_ref, wu_ref, a_ref, b_ref, act_ref):
        hb = h_ref[...].astype(BF16)
        a = _dot_nt(hb, wg_ref[...])
        b = _dot_nt(hb, wu_ref[...])
        a_ref[...] = a
        b_ref[...] = b
        act_ref[...] = (a * jax.nn.sigmoid(a) * b).astype(BF16)

    w_spec = pl.BlockSpec((None, FF_SHARD, D_MODEL), lambda i, k: (k, 0, 0))
    o_spec = pl.BlockSpec((None, FF_ROWS, FF_SHARD), lambda i, k: (k, i, 0))
    shape = (N_CHIPS, SEQ, FF_SHARD)
    return pl.pallas_call(
        body, grid=(SEQ // FF_ROWS, N_CHIPS),
        in_specs=[pl.BlockSpec((FF_ROWS, D_MODEL), lambda i, k: (i, 0)), w_spec, w_spec],
        out_specs=[o_spec, o_spec, o_spec], out_shape=[_sds(shape), _sds(shape), _sds(shape, BF16)],
        compiler_params=_cp(("parallel", "parallel")), name="ffn_up")(h, w_gate_t, w_up_t)


def _ffn_down_ln2_loss(act, w_down, h, tgt, ln_g, ln_b):
    def body(act_ref, w_ref, h_ref, tgt_ref, g_ref, b_ref, dz_ref, loss_ref, dg_ref, db_ref, acc):
        i, k = pl.program_id(0), pl.program_id(1)
        part = jnp.dot(act_ref[...], w_ref[...], preferred_element_type=F32)

        @pl.when(k == 0)
        def _():
            acc[...] = part

        @pl.when(k > 0)
        def _():
            acc[...] += part

        @pl.when(k == N_CHIPS - 1)
        def _():
            g = g_ref[...]
            xhat, rstd = _ln_stats(DN_ALPHA * h_ref[...] + acc[...])
            err = xhat * g + b_ref[...] - tgt_ref[...]
            d_out = err * (1.0 / D_MODEL)
            dz_ref[...] = _ln_bwd(d_out, xhat, rstd, g)
            loss_rows = jnp.sum(err * err, axis=-1, keepdims=True) * (0.5 / D_MODEL)
            sums = (jnp.broadcast_to(jnp.sum(loss_rows, axis=0, keepdims=True), loss_ref.shape),
                    _colsum(d_out * xhat), _colsum(d_out))
            for ref, val in zip((loss_ref, dg_ref, db_ref), sums):
                @pl.when(i == 0)
                def _(ref=ref, val=val):
                    ref[...] = val

                @pl.when(i > 0)
                def _(ref=ref, val=val):
                    ref[...] += val

    row = pl.BlockSpec((FF_ROWS, D_MODEL), lambda i, k: (i, 0))
    vec = pl.BlockSpec((1, D_MODEL), lambda i, k: (0, 0))
    return pl.pallas_call(
        body, grid=(SEQ // FF_ROWS, N_CHIPS),
        in_specs=[pl.BlockSpec((None, FF_ROWS, FF_SHARD), lambda i, k: (k, i, 0)),
                  pl.BlockSpec((None, FF_SHARD, D_MODEL), lambda i, k: (k, 0, 0)), row, row, vec, vec],
        out_specs=[row, pl.BlockSpec((1, BLOCK), lambda i, k: (0, 0)), vec, vec],
        out_shape=[_sds((SEQ, D_MODEL)), _sds((1, BLOCK)), _sds((1, D_MODEL)), _sds((1, D_MODEL))],
        scratch_shapes=[pltpu.VMEM((FF_ROWS, D_MODEL), F32)],
        compiler_params=_cp(("arbitrary", "arbitrary")), name="ffn_down_ln2_loss")(act, w_down, h, tgt, ln_g, ln_b)


def _ffn_down_bwd(dz, w_down, a, b):
    def body(dz_ref, wd_ref, a_ref, b_ref, da_ref, db_ref):
        d_act = _dot_nt(dz_ref[...].astype(BF16), wd_ref[...])
        av = a_ref[...]
        sg = jax.nn.sigmoid(av)
        da_ref[...] = (d_act * b_ref[...] * sg * (1.0 + av * (1.0 - sg))).astype(BF16)
        db_ref[...] = (d_act * av * sg).astype(BF16)

    t_spec = pl.BlockSpec((None, FF_ROWS, FF_SHARD), lambda i, k: (k, i, 0))
    shape = (N_CHIPS, SEQ, FF_SHARD)
    return pl.pallas_call(
        body, grid=(SEQ // FF_ROWS, N_CHIPS),
        in_specs=[pl.BlockSpec((FF_ROWS, D_MODEL), lambda i, k: (i, 0)),
                  pl.BlockSpec((None, FF_SHARD, D_MODEL), lambda i, k: (k, 0, 0)), t_spec, t_spec],
        out_specs=[t_spec, t_spec], out_shape=[_sds(shape, BF16), _sds(shape, BF16)],
        compiler_params=_cp(("parallel", "parallel")), name="ffn_down_bwd")(dz, w_down, a, b)


def _ffn_dh(d_a, d_b, w_gate_t, w_up_t):
    def body(da_ref, db_ref, wg_ref, wu_ref, o_ref, acc):
        k = pl.program_id(1)
        part = (jnp.dot(da_ref[...], wg_ref[...], preferred_element_type=F32)
                + jnp.dot(db_ref[...], wu_ref[...], preferred_element_type=F32))

        @pl.when(k == 0)
        def _():
            acc[...] = part

        @pl.when(k > 0)
        def _():
            acc[...] += part

        @pl.when(k == N_CHIPS - 1)
        def _():
            o_ref[...] = acc[...]

    t_spec = pl.BlockSpec((None, FF_ROWS, FF_SHARD), lambda i, k: (k, i, 0))
    w_spec = pl.BlockSpec((None, FF_SHARD, D_MODEL), lambda i, k: (k, 0, 0))
    return pl.pallas_call(
        body, grid=(SEQ // FF_ROWS, N_CHIPS), in_specs=[t_spec, t_spec, w_spec, w_spec],
        out_specs=pl.BlockSpec((FF_ROWS, D_MODEL), lambda i, k: (i, 0)), out_shape=_sds((SEQ, D_MODEL)),
        scratch_shapes=[pltpu.VMEM((FF_ROWS, D_MODEL), F32)],
        compiler_params=_cp(("parallel", "arbitrary")), name="ffn_dh")(d_a, d_b, w_gate_t, w_up_t)


def _local_step(x, tgt, wts, small, ffn_grads, mixer_grads, small_grads):
    s = SEQ
    cos_f, sin_s = [_to_phase_rows(t) for t in _rope_tables()]
    x = _reorder_rows(x, to_phase=True, name="phase_rows_x")
    tgt = _reorder_rows(tgt, to_phase=True, name="phase_rows_target")

    proj = _project_in(x, wts["w_in"], cos_f, sin_s)

    attn, lse = _attention_fwd(proj)

    (abar_r, abar_i, bbar_r, bbar_i), ssm_vjp = jax.vjp(
        _ssm_discretise, small["ssm_a_re"], small["ssm_a_im"], small["ssm_log_dt"], small["ssm_b_re"], small["ssm_b_im"])
    b_in_r, b_in_i = [_slab_block_diag(b.transpose(0, 2, 1)).astype(BF16) for b in (bbar_r, bbar_i)]
    c_out_r = _slab_block_diag(small["ssm_c_re"].transpose(0, 2, 1)).astype(BF16)
    c_out_ni = _slab_block_diag(-small["ssm_c_im"].transpose(0, 2, 1)).astype(BF16)
    a_r, a_i = abar_r.reshape(1, SSM_LANES), abar_i.reshape(1, SSM_LANES)
    d_skip = small["ssm_d"].reshape(1, SSM_WIDTH)

    u_f = _to_scan_rows(proj[:, 3 * QKV_WIDTH:3 * QKV_WIDTH + SSM_WIDTH])
    u_p = u_f.astype(BF16)
    y_c, h_r, h_i, e_r, e_i = _ssm_forward(u_p, b_in_r, b_in_i, c_out_r, c_out_ni, a_r, a_i)

    def branch(t, wg):
        return jnp.concatenate([jnp.dot(t, wg[k], preferred_element_type=F32) for k in range(N_CHIPS)], axis=1)

    def branch_t(t, wg):
        ns = wg.shape[2]
        return sum(_dot_nt(t[:, k * ns:(k + 1) * ns], wg[k]) for k in range(N_CHIPS))

    def gelu_glu(yc, u, dsk, wg):
        y = yc + dsk * u
        gel = (0.5 * y * (1.0 + jnp.tanh(GELU_C * (y + GELU_K * y * y * y)))).astype(BF16)
        glu = branch(gel, wg)
        return y, gel, glu, glu[:, :SSM_WIDTH] * jax.nn.sigmoid(glu[:, SSM_WIDTH:])

    y_s5, gel, glu, y_glu = _rowwise(
        gelu_glu, [y_c, u_f], [d_skip, wts["w_glu"]],
        [_sds((s, SSM_WIDTH)), _sds((s, SSM_WIDTH), BF16), _sds((s, 2 * SSM_WIDTH)), _sds((s, SSM_WIDTH), BF16)],
        tm=512, name="ssm_gelu_glu")
    y_glu = _from_scan_rows(y_glu)

    gl0 = (proj, D_MODEL, (3 * QKV_WIDTH + SSM_WIDTH) // D_MODEL)
    gl1 = (proj, D_MODEL, (3 * QKV_WIDTH + SSM_WIDTH) // D_MODEL + 1)
    b_gate = small["b_gate"]
    w_out = wts["w_out"].reshape(D_MODEL, D_MODEL)

    def mix_ln1(l0, l1, at, yg, xv, bg, wa, ws, wo, g, b):
        ya = branch(at.astype(BF16), wa)
        ys = branch(yg, ws)
        mixed = (jax.nn.sigmoid(l0 + bg[0:1]) * ya + jax.nn.sigmoid(l1 + bg[1:2]) * ys).astype(BF16)
        z = DN_ALPHA * xv + jnp.dot(mixed, wo, preferred_element_type=F32)
        xhat, _ = _ln_stats(z)
        return ya, ys, mixed, z, xhat * g + b

    y_attn, y_ssm, mixed, z1, h = _rowwise(
        mix_ln1, [gl0, gl1, attn, y_glu, x],
        [b_gate, wts["w_attn_br"], wts["w_ssm_br"], w_out, small["ln1_g"], small["ln1_b"]],
        [_sds((s, D_MODEL)), _sds((s, D_MODEL)), _sds((s, D_MODEL), BF16), _sds((s, D_MODEL)), _sds((s, D_MODEL))],
        tm=256, name="mix_ln1")

    nf = D_FF // N_CHIPS
    w_gate_t, w_up_t, w_down = wts["w_ff_gate"], wts["w_ff_up"], wts["w_ff_down"]
    ff_a, ff_b, act = _ffn_up(h, w_gate_t, w_up_t)
    dz2, loss_v, d_ln2_g, d_ln2_b = _ffn_down_ln2_loss(act, w_down, h, tgt, small["ln2_g"], small["ln2_b"])

    d_a, d_b = _ffn_down_bwd(dz2, w_down, ff_a, ff_b)

    def grad_rows(lhs, rhs, name):
        return _matmul(lhs, rhs, grid=(N_CHIPS,), a_spec=pl.BlockSpec((None, s, nf), lambda k: (k, 0, 0)),
                       b_spec=pl.BlockSpec((s, D_MODEL), lambda k: (0, 0)),
                       o_spec=pl.BlockSpec((None, nf, D_MODEL), lambda k: (k, 0, 0)),
                       out_shape=_sds((N_CHIPS, nf, D_MODEL), BF16), dims=(0, 0), name=name)

    g_w_ff_down = grad_rows(act, dz2, "g_w_ff_down")
    g_w_ff_gate = grad_rows(d_a, h, "g_w_ff_gate")
    g_w_ff_up = grad_rows(d_b, h, "g_w_ff_up")
    dh_ff = _ffn_dh(d_a, d_b, w_gate_t, w_up_t)

    def ln1_gate_bwd(dz, dff, z, l0, l1, ya, ys, g, bg, wo, wa, ws):
        xhat, rstd = _ln_stats(z)
        dh = DN_ALPHA * dz + dff
        dz_in = _ln_bwd(dh, xhat, rstd, g)
        dm = _dot_nt(dz_in.astype(BF16), wo)
        g0 = jax.nn.sigmoid(l0 + bg[0:1])
        g1 = jax.nn.sigmoid(l1 + bg[1:2])
        dl0 = dm * ya * g0 * (1.0 - g0)
        dl1 = dm * ys * g1 * (1.0 - g1)
        dya, dys = (dm * g0).astype(BF16), (dm * g1).astype(BF16)
        return (dz_in, dya, dys, jnp.concatenate([dl0, dl1], axis=1), branch_t(dya, wa), branch_t(dys, ws),
                _colsum(dh * xhat), _colsum(dh), _colsum(dl0), _colsum(dl1))

    dz1, d_y_attn, d_y_ssm, d_gl, d_attn, d_y_glu, d_ln1_g, d_ln1_b, d_bg0, d_bg1 = _rowwise(
        ln1_gate_bwd, [dz2, dh_ff, z1, gl0, gl1, y_attn, y_ssm],
        [small["ln1_g"], b_gate, w_out, wts["w_attn_br"], wts["w_ssm_br"]],
        [_sds((s, D_MODEL)), _sds((s, D_MODEL), BF16), _sds((s, D_MODEL), BF16), _sds((s, 2 * D_MODEL), BF16),
         _sds((s, ATTN_WIDTH)), _sds((s, SSM_WIDTH))],
        [_sds((1, D_MODEL))] * 4, tm=256, name="ln1_gate_bwd", after=(g_w_ff_down, g_w_ff_gate, g_w_ff_up))
    ffn_sent = ffn_grads({"w_ff_down": g_w_ff_down, "w_ff_gate": g_w_ff_gate, "w_ff_up": g_w_ff_up}, dz1)
    g_w_out = _mm_rows_tn(mixed, dz1, name="g_w_out")

    g_w_ssm_br = _mm_cols_tn(y_glu, d_y_ssm, ns=D_MODEL // N_CHIPS, name="g_w_ssm_br")
    d_y_glu = _to_scan_rows(d_y_glu)

    def glu_gelu_bwd(dyg, gl, y, u, dsk, wg):
        ga, gb = gl[:, :SSM_WIDTH], gl[:, SSM_WIDTH:]
        sg = jax.nn.sigmoid(gb)
        d_gl = jnp.concatenate([dyg * sg, dyg * ga * sg * (1.0 - sg)], axis=1).astype(BF16)
        dg = branch_t(d_gl, wg)
        th = jnp.tanh(GELU_C * (y + GELU_K * y * y * y))
        dy = dg * (0.5 * (1.0 + th) + 0.5 * y * (1.0 - th * th) * GELU_C * (1.0 + 3.0 * GELU_K * y * y))
        return d_gl, dy, dy * dsk, _colsum(dy * u)

    d_glu, d_y, d_u_skip, d_ssm_d = _rowwise(
        glu_gelu_bwd, [d_y_glu, glu, y_s5, u_f], [d_skip, wts["w_glu"]],
        [_sds((s, 2 * SSM_WIDTH), BF16), _sds((s, SSM_WIDTH), BF16), _sds((s, SSM_WIDTH))], [_sds((1, SSM_WIDTH))],
        tm=512, name="glu_gelu_bwd", after=tuple(ffn_sent))
    g_w_glu = _mm_cols_tn(gel, d_glu, ns=2 * SSM_WIDTH // N_CHIPS, name="g_w_glu")
    d_u, d_abar_r, d_abar_i, d_c_r, d_c_ni, d_bin_r, d_bin_i = _ssm_backward(
        d_y, d_u_skip, u_p, h_r, h_i, e_r, e_i, b_in_r, b_in_i, c_out_r, c_out_ni, a_r, a_i)
    d_u = _from_scan_rows(d_u)
    d_bbar_r = d_bin_r.reshape(SSM_GROUPS, SSM_GROUP, SSM_STATE).transpose(0, 2, 1)
    d_bbar_i = d_bin_i.reshape(SSM_GROUPS, SSM_GROUP, SSM_STATE).transpose(0, 2, 1)
    d_a_re, d_a_im, d_log_dt, d_b_re, d_b_im = ssm_vjp(
        (d_abar_r.reshape(SSM_GROUPS, SSM_STATE), d_abar_i.reshape(SSM_GROUPS, SSM_STATE), d_bbar_r, d_bbar_i))
    d_c_re = d_c_r.reshape(SSM_GROUPS, SSM_GROUP, SSM_STATE)
    d_c_im = -d_c_ni.reshape(SSM_GROUPS, SSM_GROUP, SSM_STATE)

    g_w_attn_br = _mm_cols_tn(attn, d_y_attn, ns=D_MODEL // N_CHIPS, name="g_w_attn_br")
    mixer_grads({"w_out": g_w_out, "w_ssm_br": g_w_ssm_br, "w_glu": g_w_glu, "w_attn_br": g_w_attn_br}, d_abar_r)
    small_g = {"b_gate": jnp.concatenate([d_bg0, d_bg1], axis=0), "ssm_a_re": d_a_re, "ssm_a_im": d_a_im,
               "ssm_log_dt": d_log_dt, "ssm_b_re": d_b_re, "ssm_b_im": d_b_im, "ssm_c_re": d_c_re, "ssm_c_im": d_c_im,
               "ssm_d": d_ssm_d.reshape(SSM_WIDTH), "ln1_g": d_ln1_g, "ln1_b": d_ln1_b, "ln2_g": d_ln2_g,
               "ln2_b": d_ln2_b}
    shared = small_grads(small_g, loss_v[0, 0])
    d_proj = _attention_bwd(proj, cos_f, sin_s, d_attn, attn, lse, d_u, d_gl)

    g_w_in = _mm_cols_tn(x, d_proj, ns=IN_WIDTH // N_CHIPS, name="g_w_in", after=tuple(shared))

    def grad_x_after(after):
        dx_proj = _mm_cols_nt(d_proj, wts["w_in"], tm=1024, name="dx_proj", after=after)
        return _reorder_rows(dz1, dx_proj, to_phase=False, name="grad_x", scale=DN_ALPHA)

    return grad_x_after, g_w_in, d_proj


GATHER_ID, SWAP_ID, SCATTER_ID, JOIN_ID, EXCHANGE_ID = 1, 2, 3, 4, 5


def _place():
    return lax.axis_index("x"), lax.axis_index("y"), lax.axis_index("c")


def _other_chips(x, y):
    return [(1 - x, y), (x, 1 - y), (1 - x, 1 - y)]


def _handshake(peers):
    barrier = pltpu.get_barrier_semaphore()
    for peer in peers:
        pl.semaphore_signal(barrier, inc=1, device_id=peer, device_id_type=MESH)
    pl.semaphore_wait(barrier, len(peers))


def _sequencer(body, arrays, out_type, sems, collective_id, name):
    return pl.kernel(body, name=name, out_type=out_type,
                     mesh=plsc.ScalarSubcoreMesh(axis_name="sequencer", num_cores=1), scratch_types=sems,
                     compiler_params=pltpu.CompilerParams(collective_id=collective_id))(*arrays)


def _gather_weights(shards, *, name, own_slot=True):
    nw = len(shards)

    def body(*refs):
        ins, outs = refs[:nw], refs[nw:2 * nw]
        send_sems, recv_sems, pass_send, pass_recv, local_sems = refs[2 * nw:]
        x, y, c = _place()
        chip = 2 * x + y
        chips = _other_chips(x, y)
        _handshake([(x, y, 1 - c)] + [(cx, cy, c) for cx, cy in chips])
        started, local = [], []
        for w in range(nw):
            hw = shards[w].shape[0] // 2
            mine = pl.ds(c * hw, hw)
            if own_slot:
                own = pltpu.make_async_copy(ins[w], outs[w].at[chip], local_sems.at[w])
                own.start()
                local.append(own)
            for j, (cx, cy) in enumerate(chips):
                cp = pltpu.make_async_remote_copy(
                    src_ref=ins[w].at[mine], dst_ref=outs[w].at[chip, mine], send_sem=send_sems.at[w, j],
                    recv_sem=recv_sems.at[w, j], device_id=(cx, cy, c), device_id_type=MESH)
                cp.start()
                started.append(cp)
        passed = []
        for w in range(nw):
            hw = shards[w].shape[0] // 2
            mine = pl.ds(c * hw, hw)
            for j, (cx, cy) in enumerate(chips):
                landed = outs[w].at[2 * cx + cy, mine]
                pltpu.make_async_remote_copy(
                    src_ref=ins[w].at[mine], dst_ref=landed, send_sem=send_sems.at[w, j],
                    recv_sem=recv_sems.at[w, j], device_id=(cx, cy, c), device_id_type=MESH).wait_recv()
                cp = pltpu.make_async_remote_copy(
                    src_ref=landed, dst_ref=landed, send_sem=pass_send.at[w, j], recv_sem=pass_recv.at[w, j],
                    device_id=(x, y, 1 - c), device_id_type=MESH)
                cp.start()
                passed.append(cp)
        for w in range(nw):
            hw = shards[w].shape[0] // 2
            theirs = pl.ds((1 - c) * hw, hw)
            for j, (cx, cy) in enumerate(chips):
                landed = outs[w].at[2 * cx + cy, theirs]
                pltpu.make_async_remote_copy(
                    src_ref=landed, dst_ref=landed, send_sem=pass_send.at[w, j], recv_sem=pass_recv.at[w, j],
                    device_id=(x, y, 1 - c), device_id_type=MESH).wait_recv()
        for cp in local:
            cp.wait()
        for cp in started + passed:
            cp.wait_send()

    sem = pltpu.SemaphoreType.DMA
    return _sequencer(body, shards, [_sds((N_CHIPS,) + a.shape, a.dtype) for a in shards],
                      [sem((nw, 3)), sem((nw, 3)), sem((nw, 3)), sem((nw, 3)), sem((nw,))], GATHER_ID, name)


def _swap_other_halves(grads, *, name):
    nw = len(grads)

    def body(*refs):
        ins, outs = refs[:nw], refs[nw:2 * nw]
        send_sems, recv_sems = refs[2 * nw:]
        x, y, c = _place()
        _handshake([(x, y, 1 - c)])
        cps = []
        for w in range(nw):
            hw = grads[w].shape[1] // 2
            cp = pltpu.make_async_remote_copy(
                src_ref=ins[w].at[:, pl.ds((1 - c) * hw, hw)], dst_ref=outs[w], send_sem=send_sems.at[w],
                recv_sem=recv_sems.at[w], device_id=(x, y, 1 - c), device_id_type=MESH)
            cp.start()
            cps.append(cp)
        for cp in cps:
            cp.wait()

    sem = pltpu.SemaphoreType.DMA
    return _sequencer(body, grads, [_sds((N_CHIPS, g.shape[1] // 2, g.shape[2]), g.dtype) for g in grads],
                      [sem((nw,)), sem((nw,))], SWAP_ID, name)


def _add_my_halves(core, grads, others, *, name, after=()):
    nw = len(grads)
    halves = [g.shape[1] // 2 for g in grads]

    def body(core_ref, *refs):
        outs = refs[2 * nw + len(after):]
        for g_ref, o_ref, out_ref in zip(refs[:nw], refs[nw:2 * nw], outs):
            out_ref[...] = (g_ref[...].astype(F32) + o_ref[...].astype(F32)).astype(out_ref.dtype)

    in_specs = [pl.BlockSpec((None, None, hw, g.shape[2]), lambda s, core_ref: (s, core_ref[0], 0, 0))
                for g, hw in zip(grads, halves)]
    in_specs += [pl.BlockSpec((None, hw, g.shape[2]), lambda s, core_ref: (s, 0, 0)) for g, hw in zip(grads, halves)]
    return pl.pallas_call(
        body,
        grid_spec=pltpu.PrefetchScalarGridSpec(
            num_scalar_prefetch=1, grid=(N_CHIPS,), in_specs=in_specs + [HBM_OPERAND] * len(after),
            out_specs=[pl.BlockSpec((None, hw, g.shape[2]), lambda s, core_ref: (s, 0, 0))
                       for g, hw in zip(grads, halves)]),
        out_shape=[_sds((N_CHIPS, hw, g.shape[2]), BF16) for g, hw in zip(grads, halves)],
        compiler_params=_cp(("parallel",)), name=name)(
            core, *[g.reshape(N_CHIPS, 2, hw, g.shape[2]) for g, hw in zip(grads, halves)], *others, *after)


def _scatter_partials(parts, *, name):
    nw = len(parts)

    def body(*refs):
        ins, outs = refs[:nw], refs[nw:2 * nw]
        send_sems, recv_sems = refs[2 * nw:]
        x, y, c = _place()
        _handshake([(cx, cy, c) for cx, cy in _other_chips(x, y)])
        cps = []
        for w in range(nw):
            for j, (cx, cy) in enumerate(_other_chips(x, y)):
                cp = pltpu.make_async_remote_copy(
                    src_ref=ins[w].at[2 * cx + cy], dst_ref=outs[w].at[j], send_sem=send_sems.at[w, j],
                    recv_sem=recv_sems.at[w, j], device_id=(cx, cy, c), device_id_type=MESH)
                cp.start()
                cps.append(cp)
        for cp in cps:
            cp.wait()

    sem = pltpu.SemaphoreType.DMA
    return _sequencer(body, parts, [_sds((3,) + p.shape[1:], p.dtype) for p in parts],
                      [sem((nw, 3)), sem((nw, 3))], SCATTER_ID, name)


SUM_STEPS = 2


def _sum_partials(chip, parts, recvd, *, name, after=()):
    nw = len(parts)
    rows = [p.shape[1] // SUM_STEPS for p in parts]

    def body(chip_ref, *refs):
        outs = refs[2 * nw + len(after):]
        for p_ref, r_ref, out_ref in zip(refs[:nw], refs[nw:2 * nw], outs):
            acc = p_ref[...].astype(F32)
            for j in range(3):
                acc = acc + r_ref[j].astype(F32)
            out_ref[...] = acc

    in_specs = [pl.BlockSpec((None, th, p.shape[2]), lambda i, chip_ref: (chip_ref[0], i, 0))
                for p, th in zip(parts, rows)]
    in_specs += [pl.BlockSpec((3, th, p.shape[2]), lambda i, chip_ref: (0, i, 0)) for p, th in zip(parts, rows)]
    return pl.pallas_call(
        body,
        grid_spec=pltpu.PrefetchScalarGridSpec(
            num_scalar_prefetch=1, grid=(SUM_STEPS,), in_specs=in_specs + [HBM_OPERAND] * len(after),
            out_specs=[pl.BlockSpec((th, p.shape[2]), lambda i, chip_ref: (i, 0)) for p, th in zip(parts, rows)]),
        out_shape=[_sds(p.shape[1:]) for p in parts], compiler_params=_cp(("parallel",)), name=name)(
            chip, *parts, *recvd, *after)


def _swap_reduced_halves(halves, *, name):
    nw = len(halves)

    def body(*refs):
        ins, outs = refs[:nw], refs[nw:2 * nw]
        send_sems, recv_sems = refs[2 * nw:]
        x, y, c = _place()
        _handshake([(x, y, 1 - c)])
        cps = []
        for w in range(nw):
            cp = pltpu.make_async_remote_copy(
                src_ref=ins[w], dst_ref=outs[w], send_sem=send_sems.at[w], recv_sem=recv_sems.at[w],
                device_id=(x, y, 1 - c), device_id_type=MESH)
            cp.start()
            cps.append(cp)
        for cp in cps:
            cp.wait()

    sem = pltpu.SemaphoreType.DMA
    return _sequencer(body, halves, [_sds(h.shape, h.dtype) for h in halves], [sem((nw,)), sem((nw,))], JOIN_ID, name)


def _exchange_rows(vec, *, name):
    def body(v_ref, slots, send_sems, recv_sems, local_sem):
        x, y, c = _place()
        me = 4 * x + 2 * y + c
        peers = []
        for mask in range(1, N_DEV):
            peers.append((1 - x if mask & 4 else x, 1 - y if mask & 2 else y, 1 - c if mask & 1 else c))
        _handshake(peers)
        own = pltpu.make_async_copy(v_ref, slots.at[me], local_sem)
        own.start()
        cps = []
        for k, peer in enumerate(peers):
            cp = pltpu.make_async_remote_copy(
                src_ref=v_ref, dst_ref=slots.at[me], send_sem=send_sems.at[k], recv_sem=recv_sems.at[k],
                device_id=peer, device_id_type=MESH)
            cp.start()
            cps.append(cp)
        for k, (px, py, pc) in enumerate(peers):
            pltpu.make_async_remote_copy(
                src_ref=v_ref, dst_ref=slots.at[4 * px + 2 * py + pc], send_sem=send_sems.at[k],
                recv_sem=recv_sems.at[k], device_id=(px, py, pc), device_id_type=MESH).wait_recv()
        for cp in cps:
            cp.wait_send()
        own.wait()

    sem = pltpu.SemaphoreType.DMA
    return _sequencer(body, [vec], [_sds((N_DEV,) + vec.shape)], [sem((N_DEV - 1,)), sem((N_DEV - 1,)), sem(())],
                      EXCHANGE_ID, name)[0]


def _sum_slots(slots, *, name, after=()):
    def body(s_ref, *rest):
        out_ref = rest[len(after)]
        acc = s_ref[0]
        for d in range(1, N_DEV):
            acc = acc + s_ref[d]
        out_ref[...] = acc

    vmem = pl.BlockSpec(memory_space=pltpu.VMEM)
    return pl.pallas_call(
        body, in_specs=[vmem] + [HBM_OPERAND] * len(after), out_specs=vmem, out_shape=_sds(slots.shape[1:]),
        compiler_params=pltpu.CompilerParams(vmem_limit_bytes=VMEM_LIMIT_BYTES), name=name)(slots, *after)


def _reduce_scatter_start(grads, core, *, tag, add_after=()):
    others = _swap_other_halves(grads, name="swap_other_halves_" + tag)
    parts = _add_my_halves(core, grads, others, name="add_my_halves_" + tag, after=add_after)
    return parts, _scatter_partials(parts, name="scatter_partials_" + tag)


def _reduce_scatter_finish(parts, recvd, chip, *, tag, sum_after=()):
    mine = _sum_partials(chip, parts, recvd, name="sum_partials_" + tag, after=sum_after)
    return mine, _swap_reduced_halves(mine, name="swap_reduced_halves_" + tag)


ADAM_BLOCK_ELEMS = 256 * 1024


def _adam_rows(rows, cols):
    tm = rows
    while tm * cols > ADAM_BLOCK_ELEMS and tm % 16 == 0:
        tm //= 2
    return tm


def _adam_step(wv, gv, mv, vv):
    m2 = ADAM_B1 * mv + (1.0 - ADAM_B1) * gv
    v2 = ADAM_B2 * vv + (1.0 - ADAM_B2) * (gv * gv)
    m_hat = m2 / (1.0 - ADAM_B1 ** ADAM_STEP)
    v_hat = v2 / (1.0 - ADAM_B2 ** ADAM_STEP)
    return -ADAM_LR * (m_hat / (jnp.sqrt(v_hat) + ADAM_EPS) + ADAM_WD * wv), m2, v2


def _adamw_each(ws, gs, ms, vs, *, name, after=()):
    n = len(ws)
    whole = pl.BlockSpec(memory_space=pltpu.VMEM)

    def body(*refs):
        ins, outs = refs[:4 * n], refs[4 * n + len(after):]
        for i in range(n):
            res = _adam_step(*(ins[k * n + i][...] for k in range(4)))
            for k in range(3):
                outs[k * n + i][...] = res[k]

    out = pl.pallas_call(body, in_specs=[whole] * (4 * n) + [HBM_OPERAND] * len(after),
                         out_shape=[_sds(w.shape) for w in ws] * 3, name=name)(*ws, *gs, *ms, *vs, *after)
    return out[:n], out[n:2 * n], out[2 * n:]


def _adamw_halves(core, w, g_mine, g_theirs, m, v, *, name, after=()):
    rows, cols = w.shape
    hw = rows // 2
    tm = _adam_rows(hw, cols)
    per_half = hw // tm

    def body(core_ref, w_ref, gm_ref, gt_ref, m_ref, v_ref, *rest):
        g_out, d_out, m_out, v_out = rest[len(after):]
        mine = (pl.program_id(0) // per_half) == core_ref[0]
        g = jnp.where(mine, gm_ref[...], gt_ref[...])
        d, m2, v2 = _adam_step(w_ref[...], g, m_ref[...], v_ref[...])
        g_out[...] = g
        d_out[...] = d
        m_out[...] = m2
        v_out[...] = v2

    full = pl.BlockSpec((tm, cols), lambda i, core_ref: (i, 0))

    def half(wanted):
        def index(i, core_ref):
            in_use = ((i // per_half) == core_ref[0]) == wanted
            return (jnp.where(in_use, i % per_half, 0), 0)
        return pl.BlockSpec((tm, cols), index)

    return pl.pallas_call(
        body,
        grid_spec=pltpu.PrefetchScalarGridSpec(
            num_scalar_prefetch=1, grid=(rows // tm,),
            in_specs=[full, half(True), half(False), full, full] + [HBM_OPERAND] * len(after),
            out_specs=[full, full, full, full]),
        out_shape=[_sds((rows, cols))] * 4, compiler_params=_cp(("parallel",)), name=name)(
            core, w, g_mine, g_theirs, m, v, *after)


HELD_TRANSPOSED = ("w_ff_gate", "w_ff_up")


def _as_rows(name, arr):
    return arr[0].T if name in HELD_TRANSPOSED else arr[0]


def _from_rows(name, arr2d):
    return (arr2d.T if name in HELD_TRANSPOSED else arr2d)[None]


STORED_SWAPPED = ("ssm_b_re", "ssm_b_im")


def _as_stored(name, arr):
    return jnp.swapaxes(arr, -1, -2) if name in STORED_SWAPPED else arr


def _pack_rows(arrs):
    flat = jnp.concatenate([a.reshape(-1).astype(F32) for a in arrs])
    rows = -(-flat.shape[0] // 1024) * 8
    return jnp.pad(flat, (0, rows * 128 - flat.shape[0])).reshape(rows, 128)


def _unpack_rows(vec, shapes):
    flat = vec.reshape(-1)
    out, off = [], 0
    for shp in shapes:
        size = math.prod(shp)
        out.append(flat[off:off + size].reshape(shp))
        off += size
    return out


SMALL = ("b_gate", "ssm_a_re", "ssm_a_im", "ssm_log_dt", "ssm_b_re", "ssm_b_im", "ssm_c_re", "ssm_c_im", "ssm_d",
         "ln1_g", "ln1_b", "ln2_g", "ln2_b")
GATHER_GROUPS = (("w_in", ("w_in",)), ("mixer", ("w_attn_br", "w_ssm_br", "w_glu", "w_out")),
                 ("ffn_up", ("w_ff_gate", "w_ff_up")), ("ffn_down", ("w_ff_down",)))
REDUCE_GROUPS = (("ffn", ("w_ff_down", "w_ff_gate", "w_ff_up")),
                 ("mixer", ("w_out", "w_ssm_br", "w_glu", "w_attn_br")), ("w_in", ("w_in",)))
WEIGHTS = ("w_in", "b_gate", "w_attn_br", "w_ssm_br", "w_out", "ssm_a_re", "ssm_a_im", "ssm_log_dt", "ssm_b_re",
           "ssm_b_im", "ssm_c_re", "ssm_c_im", "ssm_d", "w_glu", "ln1_g", "ln1_b", "w_ff_gate", "w_ff_up", "w_ff_down",
           "ln2_g", "ln2_b")


def kernel(x, w_in, b_gate, w_attn_br, w_ssm_br, w_out, ssm_a_re, ssm_a_im, ssm_log_dt, ssm_b_re, ssm_b_im, ssm_c_re, ssm_c_im, ssm_d, w_glu, ln1_g, ln1_b, w_ff_gate, w_ff_up, w_ff_down, ln2_g, ln2_b, loss_target, m_w_in, m_b_gate, m_w_attn_br, m_w_ssm_br, m_w_out, m_ssm_a_re, m_ssm_a_im, m_ssm_log_dt, m_ssm_b_re, m_ssm_b_im, m_ssm_c_re, m_ssm_c_im, m_ssm_d, m_w_glu, m_ln1_g, m_ln1_b, m_w_ff_gate, m_w_ff_up, m_w_ff_down, m_ln2_g, m_ln2_b, v_w_in, v_b_gate, v_w_attn_br, v_w_ssm_br, v_w_out, v_ssm_a_re, v_ssm_a_im, v_ssm_log_dt, v_ssm_b_re, v_ssm_b_im, v_ssm_c_re, v_ssm_c_im, v_ssm_d, v_w_glu, v_ln1_g, v_ln1_b, v_w_ff_gate, v_w_ff_up, v_w_ff_down, v_ln2_g, v_ln2_b):
    given = dict(locals())
    px, py, pc = _place()
    chip = 2 * px + py
    core_s = jnp.reshape(pc, (1,)).astype(jnp.int32)
    chip_s = jnp.reshape(chip, (1,)).astype(jnp.int32)

    wts = {}
    for tag, names in GATHER_GROUPS:
        shards = [_as_rows(n, given[n]).astype(BF16) for n in names]
        first = tag == GATHER_GROUPS[0][0]
        slots = _gather_weights(shards, name="gather_" + tag, own_slot=not first)
        if first:
            slots = [lax.dynamic_update_slice(g, s[None], (chip, 0, 0)) for g, s in zip(slots, shards)]
        wts.update(zip(names, slots))
    ncol = D_MODEL // N_CHIPS
    bg_mine = jnp.where(pc == 0, b_gate[0], jnp.zeros_like(b_gate[0]))
    bg_full = lax.dynamic_update_slice(jnp.zeros((2, D_MODEL), F32), bg_mine, (0, chip * ncol))
    bg_slots = _exchange_rows(bg_full.reshape(16, 128), name="exchange_gate_bias")
    bg_full = _sum_slots(bg_slots, name="sum_gate_bias").reshape(2, D_MODEL)
    small = {n: given[n][0] for n in SMALL if n.startswith("ssm")}
    small.update({n: given[n] for n in ("ln1_g", "ln1_b", "ln2_g", "ln2_b")})
    small["b_gate"] = bg_full

    groups = dict(REDUCE_GROUPS)
    parts, recvd, reduced, sent = {}, {}, {}, {}
    grads, delta, new_m, new_v, done = {}, {}, {}, {}, {}

    def start(tag, big_g, add_after):
        parts[tag], recvd[tag] = _reduce_scatter_start([big_g[n] for n in groups[tag]], core_s, tag=tag,
                                                       add_after=add_after)
        return parts[tag]

    def reduce_sum(tag, after):
        reduced[tag] = _reduce_scatter_finish(parts[tag], recvd[tag], chip_s, tag=tag, sum_after=after)
        return reduced[tag][0]

    def adam(tag, after):
        for n, g_mine, g_theirs in zip(groups[tag], *reduced[tag]):
            res = _adamw_halves(core_s, _as_rows(n, given[n]), g_mine, g_theirs, _as_rows(n, given["m_" + n]),
                                _as_rows(n, given["v_" + n]), name="adamw_" + n, after=after)
            done[n] = res[1]
            grads[n], delta[n], new_m[n], new_v[n] = [_from_rows(n, r) for r in res]

    def ffn_grads(big_g, norm_bwd):
        return start("ffn", big_g, (norm_bwd,))

    def mixer_grads(big_g, scan_bwd):
        return start("mixer", big_g, (scan_bwd, *reduce_sum("ffn", (scan_bwd,))))

    def small_grads(small_g, loss_mine):
        sent["stored"] = [_as_stored(n, small_g[n]) for n in SMALL] + [loss_mine.reshape(1)]
        packed = _pack_rows(sent["stored"])
        sent["slots"] = _exchange_rows(packed, name="exchange_small")
        return (packed,)

    grad_x_after, g_w_in, attention_bwd = _local_step(x[0], loss_target[0], wts, small,
                                                      ffn_grads, mixer_grads, small_grads)

    reduce_sum("mixer", (attention_bwd,))
    summed = _sum_slots(sent["slots"], name="sum_small", after=(g_w_in,))
    adam("mixer", (g_w_in,))
    start("w_in", {"w_in": g_w_in}, (summed, *[done[n] for n in groups["mixer"]]))
    in_flight = (parts["w_in"][0],)
    grad_x = grad_x_after(in_flight)
    adam("ffn", in_flight)
    summed = _unpack_rows(summed, [a.shape for a in sent["stored"]])
    loss = summed.pop()[0]
    at = SMALL.index("b_gate")
    summed[at] = lax.dynamic_slice(summed[at], (0, chip * ncol), (2, ncol))
    summed = [g.reshape(1, -1) if g.ndim == 1 else g for g in summed]
    held = [[_as_stored(n, given[prefix + n]).reshape(g.shape) for n, g in zip(SMALL, summed)]
            for prefix in ("", "m_", "v_")]
    small_out = _adamw_each(held[0], summed, held[1], held[2], name="adamw_small", after=in_flight)
    for out, arrs in zip((grads, delta, new_m, new_v), (summed, *small_out)):
        out.update((n, _as_stored(n, a).reshape(given[n].shape)) for n, a in zip(SMALL, arrs))
    reduce_sum("w_in", (*[done[n] for n in groups["ffn"]], small_out[0][0], grad_x))
    adam("w_in", ())

    return (loss, grad_x.reshape(x.shape), *[grads[n] for n in WEIGHTS], *[delta[n] for n in WEIGHTS],
            *[new_m[n] for n in WEIGHTS], *[new_v[n] for n in WEIGHTS])
```

```python
import math

import jax
import jax.numpy as jnp
from jax import lax
from jax.experimental import pallas as pl
from jax.experimental.pallas import tpu as pltpu
from jax.experimental.pallas import tpu_sc as plsc

F32 = jnp.float32
BF16 = jnp.bfloat16
MESH = pl.DeviceIdType.MESH

D_MODEL = 1024
SEQ = 2048
HEAD_DIM = 64
ATTN_HEADS = 8
DILATIONS = (1, 4, 16)
ATTN_WIDTH = ATTN_HEADS * HEAD_DIM
QKV_WIDTH = 3 * ATTN_WIDTH
BLOCK = 128
ROPE_THETA = 10000.0
NEG_INF = -1e30
SSM_GROUP = 16
SSM_GROUPS = 32
SSM_WIDTH = 512
SSM_STATE = 64
SSM_LANES = SSM_GROUPS * SSM_STATE
SCAN_CHUNKS = 8
SCAN_STEPS = SEQ // SCAN_CHUNKS
IN_WIDTH = 3 * QKV_WIDTH + SSM_WIDTH + 2 * D_MODEL
D_FF = 2816
N_CHIPS = 4
N_DEV = 8
DN_ALPHA = 2.0 ** 0.25
LN_EPS = 1e-5
ADAM_LR = 0.001
ADAM_B1 = 0.9
ADAM_B2 = 0.999
ADAM_EPS = 1e-08
ADAM_WD = 0.01
ADAM_STEP = 10
GELU_C = math.sqrt(2.0 / math.pi)
GELU_K = 0.044715

VMEM_LIMIT_BYTES = 56 * 1024 * 1024


def _sds(shape, dtype=F32):
    return jax.ShapeDtypeStruct(tuple(shape), dtype)


def _cp(semantics=None):
    return pltpu.CompilerParams(dimension_semantics=semantics, vmem_limit_bytes=VMEM_LIMIT_BYTES)


HBM_OPERAND = pl.BlockSpec(memory_space=pl.ANY)


def _matmul(a, b, *, grid, a_spec, b_spec, o_spec, out_shape, dims, k_axis=None, name, after=()):
    nk = grid[k_axis] if k_axis is not None else 1
    o_block = tuple(d for d in o_spec.block_shape if d is not None)
    n_after = len(after)

    def body(a_ref, b_ref, *rest):
        o_ref, acc = rest[n_after], rest[n_after + 1:]
        part = lax.dot_general(a_ref[...].astype(BF16), b_ref[...].astype(BF16),
                               (((dims[0],), (dims[1],)), ((), ())), preferred_element_type=F32)
        if k_axis is None:
            o_ref[...] = part.astype(o_ref.dtype)
        else:
            k = pl.program_id(k_axis)

            @pl.when(k == 0)
            def _():
                acc[0][...] = part

            @pl.when(k > 0)
            def _():
                acc[0][...] += part

            @pl.when(k == nk - 1)
            def _():
                o_ref[...] = acc[0][...].astype(o_ref.dtype)

    sem = tuple("arbitrary" if ax == k_axis else "parallel" for ax in range(len(grid)))
    return pl.pallas_call(
        body, grid=grid, in_specs=[a_spec, b_spec] + [HBM_OPERAND] * n_after, out_specs=o_spec, out_shape=out_shape,
        scratch_shapes=[pltpu.VMEM(o_block, F32)] if k_axis is not None else [],
        compiler_params=_cp(sem), name=name)(a, b, *after)


def _mm_cols_nt(dy, wg, *, tm, name, out_dtype=F32, after=()):
    k, ns = wg.shape[1], wg.shape[2]
    m = dy.shape[0]
    a_spec = pl.BlockSpec((tm, ns), lambda i, s: (i, s))
    return _matmul(dy, wg, grid=(m // tm, N_CHIPS), a_spec=a_spec,
                   b_spec=pl.BlockSpec((None, k, ns), lambda i, s: (s, 0, 0)),
                   o_spec=pl.BlockSpec((tm, k), lambda i, s: (i, 0)),
                   out_shape=_sds((m, k), out_dtype), dims=(1, 1), k_axis=1, name=name, after=after)


def _mm_cols_tn(a, dy, *, ns, name, after=()):
    m, k = a.shape
    return _matmul(a, dy, grid=(N_CHIPS,), a_spec=pl.BlockSpec((m, k), lambda s: (0, 0)),
                   b_spec=pl.BlockSpec((m, ns), lambda s: (0, s)),
                   o_spec=pl.BlockSpec((None, k, ns), lambda s: (s, 0, 0)),
                   out_shape=_sds((N_CHIPS, k, ns), BF16), dims=(0, 0), name=name, after=after)


def _mm_rows_tn(a, dy, *, name):
    m, k = a.shape
    rows, n = k // N_CHIPS, dy.shape[1]
    return _matmul(a, dy, grid=(N_CHIPS,), a_spec=pl.BlockSpec((m, rows), lambda s: (0, s)),
                   b_spec=pl.BlockSpec((m, n), lambda s: (0, 0)),
                   o_spec=pl.BlockSpec((None, rows, n), lambda s: (s, 0, 0)),
                   out_shape=_sds((N_CHIPS, rows, n), BF16), dims=(0, 0), name=name)


def _rowwise(fn, tiled, full, outs, accs=(), *, tm, name, after=()):
    args, in_specs = [], []
    for t in tiled:
        if isinstance(t, tuple):
            arr, w, cb = t
            in_specs.append(pl.BlockSpec((tm, w), lambda i, cb=cb: (i, cb)))
        else:
            arr = t
            in_specs.append(pl.BlockSpec((tm, arr.shape[1]), lambda i: (i, 0)))
        args.append(arr)
    rows = args[0].shape[0]
    for f in full:
        in_specs.append(pl.BlockSpec(f.shape, lambda i, nd=f.ndim: (0,) * nd))
        args.append(f)
    out_specs = [pl.BlockSpec((tm, o.shape[1]), lambda i: (i, 0)) for o in outs]
    out_specs += [pl.BlockSpec(a.shape, lambda i, nd=len(a.shape): (0,) * nd) for a in accs]
    n_in, n_out = len(args), len(outs)
    in_specs += [HBM_OPERAND] * len(after)
    first_out = n_in + len(after)

    def body(*refs):
        res = fn(*[r[...] for r in refs[:n_in]])
        res = res if isinstance(res, (tuple, list)) else (res,)
        for r, v in zip(refs[first_out:first_out + n_out], res[:n_out]):
            r[...] = v.astype(r.dtype)
        i = pl.program_id(0)
        for r, v in zip(refs[first_out + n_out:], res[n_out:]):
            @pl.when(i == 0)
            def _(r=r, v=v):
                r[...] = v

            @pl.when(i > 0)
            def _(r=r, v=v):
                r[...] += v

    res = pl.pallas_call(
        body, grid=(rows // tm,), in_specs=in_specs, out_specs=out_specs, out_shape=list(outs) + list(accs),
        compiler_params=_cp(("arbitrary",) if accs else ("parallel",)), name=name)(*args, *after)
    return res


def _colsum(v):
    return jnp.sum(v, axis=0, keepdims=True)


def _ln_stats(z):
    mu = jnp.mean(z, axis=-1, keepdims=True)
    zc = z - mu
    var = jnp.mean(zc * zc, axis=-1, keepdims=True)
    rstd = lax.rsqrt(var + LN_EPS)
    return zc * rstd, rstd


def _ln_bwd(dy, xhat, rstd, g):
    dxh = dy * g
    m1 = jnp.mean(dxh, axis=-1, keepdims=True)
    m2 = jnp.mean(dxh * xhat, axis=-1, keepdims=True)
    return rstd * (dxh - m1 - xhat * m2)


def _swap_halves(t):
    w = t.shape[-1]
    lane = lax.broadcasted_iota(jnp.int32, t.shape, t.ndim - 1)
    return jnp.where((lane % HEAD_DIM) < HEAD_DIM // 2, pltpu.roll(t, w - HEAD_DIM // 2, t.ndim - 1),
                     pltpu.roll(t, HEAD_DIM // 2, t.ndim - 1))


PHASES = max(DILATIONS)
PAIR = 2 * HEAD_DIM
UNITS = SEQ // BLOCK
UNIT_BATCH = 16
ROPE_ROWS = 256
TAIL_COLS = 256


def _to_phase_rows(t):
    return t.reshape(SEQ // PHASES, PHASES, t.shape[1]).transpose(1, 0, 2).reshape(t.shape)


def _reorder_rows(arr, plus=None, *, to_phase, name, scale=1.0):
    def body(*refs):
        o_ref = refs[-1]
        for rho in range(PHASES):
            phase = pl.ds(rho * BLOCK, BLOCK)
            strided = pl.ds(rho, BLOCK, stride=PHASES)
            src, dst = (strided, phase) if to_phase else (phase, strided)
            val = refs[0][src, :]
            if scale != 1.0:
                val = val * scale
            if plus is not None:
                val = val + refs[1][src, :]
            o_ref[dst, :] = val

    spec = pl.BlockSpec((SEQ, BLOCK), lambda j: (0, j))
    ins = [arr] if plus is None else [arr, plus]
    return pl.pallas_call(body, grid=(arr.shape[1] // BLOCK,), in_specs=[spec] * len(ins), out_specs=spec,
                          out_shape=_sds(arr.shape), compiler_params=_cp(("parallel",)), name=name)(*ins)


def _rope(t, cf, ss):
    return t * cf + _swap_halves(t) * ss


def _rope_transposed(d, cf, ss):
    return d * cf + _swap_halves(d * ss)


def _unit_pieces(u, dil):
    pieces, length = PHASES // dil, 8 * dil
    if dil == 1:
        rho, i = 0, u
    elif dil == PHASES:
        rho, i = u, 0
    else:
        rho, i = jnp.bitwise_and(u, dil - 1), jnp.right_shift(u, dil.bit_length() - 1)
    before = jnp.maximum(i - 1, 0)
    cur = [pl.multiple_of((rho + dil * k) * BLOCK + length * i, 8) for k in range(pieces)]
    prev = [pl.multiple_of((rho + dil * k) * BLOCK + length * before, 8) for k in range(pieces)]
    return i, cur, prev


def _load_tile(ref, starts, dil):
    return jnp.concatenate([ref[pl.ds(st, 8 * dil), :] for st in starts], axis=0)


def _store_tile(ref, starts, dil, val, head=None, accumulate=False):
    length = 8 * dil
    lanes = slice(None) if head is None else pl.ds(head * HEAD_DIM, HEAD_DIM)
    cols = slice(None) if head is None else slice(head * HEAD_DIM, (head + 1) * HEAD_DIM)
    for k, st in enumerate(starts):
        piece = val[k * length:(k + 1) * length, cols]
        if accumulate:
            ref[pl.ds(st, length), lanes] += piece
        else:
            ref[pl.ds(st, length), lanes] = piece


def _tile_position(idx, dil):
    pieces, length = PHASES // dil, 8 * dil
    return pieces * jnp.bitwise_and(idx, length - 1) + jnp.right_shift(idx, length.bit_length() - 1)


def _band_mask(i, dil):
    row = lax.broadcasted_iota(jnp.int32, (BLOCK, 2 * BLOCK), 0)
    col = lax.broadcasted_iota(jnp.int32, (BLOCK, 2 * BLOCK), 1)
    key_pos = _tile_position(jnp.bitwise_and(col, BLOCK - 1), dil) + jnp.where(col >= BLOCK, 0, -BLOCK)
    dist = _tile_position(row, dil) - key_pos
    return (dist >= 0) & (dist <= BLOCK) & ((col >= BLOCK) | (i > 0))


def _causal_mask():
    row = lax.broadcasted_iota(jnp.int32, (BLOCK, BLOCK), 0)
    col = lax.broadcasted_iota(jnp.int32, (BLOCK, BLOCK), 1)
    return row >= col


def _pair_views(col0):
    return [pl.BlockSpec((SEQ, PAIR), lambda hp, g=g: (0, col0 // PAIR + g * (ATTN_WIDTH // PAIR) + hp))
            for g in range(len(DILATIONS))]


def _project_in(x, wg, cos_f, sin_s):
    ns = wg.shape[2]
    tiles = ns // PAIR

    def body(x_ref, w_ref, cf_ref, ss_ref, o_ref):
        shard = pl.program_id(1)
        xb = x_ref[...].astype(BF16)
        cf, ss = cf_ref[...], ss_ref[...]

        def write(rotated, scaled):
            for t0 in range(0, tiles, 2):
                strip = jnp.dot(xb, w_ref[:, t0 * PAIR:(t0 + 2) * PAIR], preferred_element_type=F32)
                for t in (t0, t0 + 1):
                    val = strip[:, (t - t0) * PAIR:(t - t0 + 1) * PAIR]
                    if t < rotated:
                        val = _rope(val, cf, ss)
                        if t < scaled:
                            val = val * (1.0 / math.sqrt(HEAD_DIM))
                    o_ref[:, t * PAIR:(t + 1) * PAIR] = val

        for s in range(N_CHIPS):
            rotated = min(max(2 * QKV_WIDTH - s * ns, 0), ns) // PAIR
            scaled = min(max(QKV_WIDTH - s * ns, 0), ns) // PAIR

            @pl.when(shard == s)
            def _(rotated=rotated, scaled=scaled):
                write(rotated, scaled)

    table = pl.BlockSpec((FF_ROWS, PAIR), lambda i, s: (i, 0))
    return pl.pallas_call(
        body, grid=(SEQ // FF_ROWS, N_CHIPS),
        in_specs=[pl.BlockSpec((FF_ROWS, D_MODEL), lambda i, s: (i, 0)),
                  pl.BlockSpec((None, D_MODEL, ns), lambda i, s: (s, 0, 0)), table, table],
        out_specs=pl.BlockSpec((FF_ROWS, ns), lambda i, s: (i, s)), out_shape=_sds((SEQ, N_CHIPS * ns)),
        compiler_params=_cp(("parallel", "parallel")), name="project_in")(x, wg, cos_f, sin_s)


def _attention_fwd(proj):
    ng = len(DILATIONS)

    def body(*refs):
        q_refs, k_refs, v_refs = refs[:ng], refs[ng:2 * ng], refs[2 * ng:3 * ng]
        attn_ref, lse_ref = refs[3 * ng:]
        qr_refs, kr_refs = q_refs, k_refs
        first = lax.broadcasted_iota(jnp.int32, (BLOCK, PAIR), 1) < HEAD_DIM
        for g, dil in enumerate(DILATIONS):
            two_blocks = SEQ // dil > BLOCK

            def units(t, carry, g=g, dil=dil, two_blocks=two_blocks):
                picked = [_unit_pieces(t * UNIT_BATCH + j, dil) for j in range(UNIT_BATCH)]

                def tiles(ref, with_prev=False):
                    if with_prev and two_blocks:
                        return jnp.stack([jnp.concatenate([_load_tile(ref, prev, dil), _load_tile(ref, rows, dil)],
                                                          axis=0) for _, rows, prev in picked])
                    return jnp.stack([_load_tile(ref, rows, dil) for _, rows, _ in picked])

                qq = tiles(qr_refs[g]).astype(BF16)
                kk = tiles(kr_refs[g], True).astype(BF16)
                vv = tiles(v_refs[g], True).astype(BF16)
                if two_blocks:
                    valid = jnp.stack([_band_mask(i, dil) for i, _, _ in picked])
                else:
                    valid = _causal_mask()[None]
                mine = first[None]
                zero = jnp.zeros_like(qq)
                outs, lses = [], []
                for qh in (jnp.where(mine, qq, zero), jnp.where(mine, zero, qq)):
                    s = jnp.einsum("pqd,pkd->pqk", qh, kk, preferred_element_type=F32)
                    s = jnp.where(valid, s, NEG_INF)
                    m = jnp.max(s, axis=-1, keepdims=True)
                    p = jnp.exp(s - m)
                    l = jnp.sum(p, axis=-1, keepdims=True)
                    outs.append(jnp.einsum("pqk,pkd->pqd", p.astype(BF16), vv, preferred_element_type=F32) * (1.0 / l))
                    lses.append(m + jnp.log(l))
                o = jnp.where(mine, outs[0], outs[1])
                lse = jnp.where(mine, lses[0], lses[1])
                if g > 0:
                    lse_old = tiles(lse_ref)
                    m = jnp.maximum(lse_old, lse)
                    lse_new = m + jnp.log(jnp.exp(lse_old - m) + jnp.exp(lse - m))
                    o = tiles(attn_ref) * jnp.exp(lse_old - lse_new) + o * jnp.exp(lse - lse_new)
                    lse = lse_new
                for j, (_, rows, _) in enumerate(picked):
                    _store_tile(attn_ref, rows, dil, o[j])
                    _store_tile(lse_ref, rows, dil, lse[j])
                return carry

            lax.fori_loop(0, UNITS // UNIT_BATCH, units, 0)

    out = pl.BlockSpec((SEQ, PAIR), lambda hp: (0, hp))
    return pl.pallas_call(
        body, grid=(ATTN_WIDTH // PAIR,),
        in_specs=_pair_views(0) + _pair_views(QKV_WIDTH) + _pair_views(2 * QKV_WIDTH),
        out_specs=[out, out], out_shape=[_sds((SEQ, ATTN_WIDTH)), _sds((SEQ, ATTN_WIDTH))],
        compiler_params=_cp(("parallel",)), name="attention_fwd")(*([proj] * (3 * ng)))


def _attention_bwd(proj, cos_f, sin_s, d_attn, attn, lse, d_u, d_gl):
    pairs = ATTN_WIDTH // PAIR
    last = len(DILATIONS) * pairs - 1

    def accumulate(dil, qr_ref, kr_ref, v_ref, do_ref, o_ref, lse_ref, dq_acc, dk_acc, dv_acc):
        two_blocks = SEQ // dil > BLOCK
        dk_acc[...] = jnp.zeros_like(dk_acc)
        dv_acc[...] = jnp.zeros_like(dv_acc)
        nk = 2 * BLOCK if two_blocks else BLOCK
        first = lax.broadcasted_iota(jnp.int32, (BLOCK, PAIR), 1) < HEAD_DIM
        first_k = lax.broadcasted_iota(jnp.int32, (nk, PAIR), 1) < HEAD_DIM

        def units(t, carry):
            picked = [_unit_pieces(t * UNIT_BATCH + j, dil) for j in range(UNIT_BATCH)]

            def tiles(ref, with_prev=False):
                if with_prev and two_blocks:
                    return jnp.stack([jnp.concatenate([_load_tile(ref, prev, dil), _load_tile(ref, rows, dil)], axis=0)
                                      for _, rows, prev in picked])
                return jnp.stack([_load_tile(ref, rows, dil) for _, rows, _ in picked])

            qq = tiles(qr_ref).astype(BF16)
            kk = tiles(kr_ref, True).astype(BF16)
            vv = tiles(v_ref, True).astype(BF16)
            dof = tiles(do_ref)
            dd = dof * tiles(o_ref)
            lse3 = tiles(lse_ref)
            dob = dof.astype(BF16)
            if two_blocks:
                valid = jnp.stack([_band_mask(i, dil) for i, _, _ in picked])
            else:
                valid = _causal_mask()[None]
            zq, zf = jnp.zeros_like(qq), jnp.zeros_like(dd)
            dqs, dks, dvs = [], [], []
            for head in range(2):
                mine = first[None] if head == 0 else jnp.logical_not(first)[None]
                delta = jnp.sum(jnp.where(mine, dd, zf), axis=-1, keepdims=True)
                lse_h = lse3[:, :, head * HEAD_DIM:head * HEAD_DIM + 1]
                s = jnp.einsum("pqd,pkd->pqk", jnp.where(mine, qq, zq), kk, preferred_element_type=F32)
                p = jnp.where(valid, jnp.exp(s - lse_h), 0.0)
                dp = jnp.einsum("pqd,pkd->pqk", jnp.where(mine, dob, zq), vv, preferred_element_type=F32)
                ds = (p * (dp - delta)).astype(BF16)
                dqs.append(jnp.einsum("pqk,pkd->pqd", ds, kk, preferred_element_type=F32))
                dks.append(jnp.einsum("pqk,pqd->pkd", ds, qq, preferred_element_type=F32))
                dvs.append(jnp.einsum("pqk,pqd->pkd", p.astype(BF16), dob, preferred_element_type=F32))
            dq = jnp.where(first[None], dqs[0], dqs[1])
            dk = jnp.where(first_k[None], dks[0], dks[1])
            dv = jnp.where(first_k[None], dvs[0], dvs[1])
            for j, (_, rows, prev) in enumerate(picked):
                _store_tile(dq_acc, rows, dil, dq[j])
                _store_tile(dk_acc, rows, dil, dk[j, nk - BLOCK:], accumulate=True)
                _store_tile(dv_acc, rows, dil, dv[j, nk - BLOCK:], accumulate=True)
                if two_blocks:
                    _store_tile(dk_acc, prev, dil, dk[j, :BLOCK], accumulate=True)
                    _store_tile(dv_acc, prev, dil, dv[j, :BLOCK], accumulate=True)
            return carry

        lax.fori_loop(0, UNITS // UNIT_BATCH, units, 0)

    def body(qr_ref, kr_ref, v_ref, cf_ref, ss_ref, do_ref, o_ref, lse_ref, du_ref, dgl_ref, out_ref,
             dq_acc, dk_acc, dv_acc, dq_buf, dk_buf, dv_buf, sems):
        step = pl.program_id(0) * pairs + pl.program_id(1)

        def columns(at):
            return [pltpu.make_async_copy(
                buf, out_ref.at[:, pl.ds(pl.multiple_of(j * QKV_WIDTH + at * PAIR, PAIR), PAIR)], sems.at[j])
                for j, buf in enumerate((dq_buf, dk_buf, dv_buf))]

        def tail(ref, col0, at):
            cols = pl.ds(pl.multiple_of(col0 + at * TAIL_COLS, TAIL_COLS), TAIL_COLS)
            return pltpu.make_async_copy(ref, out_ref.at[:, cols], sems.at[3])

        gl_steps, u_steps = 2 * D_MODEL // TAIL_COLS, SSM_WIDTH // TAIL_COLS
        from_gl = step < gl_steps
        from_u = jnp.logical_and(step >= gl_steps, step < gl_steps + u_steps)
        tail_gl = tail(dgl_ref, 3 * QKV_WIDTH + SSM_WIDTH, step)
        tail_u = tail(du_ref, 3 * QKV_WIDTH, step - gl_steps)
        pl.when(from_gl)(tail_gl.start)
        pl.when(from_u)(tail_u.start)

        for g, dil in enumerate(DILATIONS):
            @pl.when(pl.program_id(0) == g)
            def _(dil=dil):
                accumulate(dil, qr_ref, kr_ref, v_ref, do_ref, o_ref, lse_ref, dq_acc, dk_acc, dv_acc)

        @pl.when(step > 0)
        def _():
            for cp in columns(step - 1):
                cp.wait()

        def finish(t, carry):
            rows = pl.ds(pl.multiple_of(t * ROPE_ROWS, ROPE_ROWS), ROPE_ROWS)
            cf, ss = cf_ref[rows, :], ss_ref[rows, :]
            dq = dq_acc[rows, :] * (1.0 / math.sqrt(HEAD_DIM))
            dq_buf[rows, :] = _rope_transposed(dq, cf, ss).astype(BF16)
            dk_buf[rows, :] = _rope_transposed(dk_acc[rows, :], cf, ss).astype(BF16)
            dv_buf[rows, :] = dv_acc[rows, :].astype(BF16)
            return carry

        lax.fori_loop(0, SEQ // ROPE_ROWS, finish, 0)
        for cp in columns(step):
            cp.start()
        pl.when(from_gl)(tail_gl.wait)
        pl.when(from_u)(tail_u.wait)

        @pl.when(step == last)
        def _():
            for cp in columns(step):
                cp.wait()

    whole = pl.BlockSpec((SEQ, PAIR), lambda g, hp: (0, 0))
    pair = pl.BlockSpec((SEQ, PAIR), lambda g, hp: (0, hp))
    views = [pl.BlockSpec((SEQ, PAIR), lambda g, hp, c0=col0 // PAIR: (0, c0 + g * pairs + hp))
             for col0 in (0, QKV_WIDTH, 2 * QKV_WIDTH)]
    gl_blocks, u_blocks = 2 * D_MODEL // TAIL_COLS, SSM_WIDTH // TAIL_COLS
    assert gl_blocks + u_blocks <= last + 1
    gl_spec = pl.BlockSpec((SEQ, TAIL_COLS), lambda g, hp: (0, jnp.minimum(g * pairs + hp, gl_blocks - 1)))
    u_spec = pl.BlockSpec((SEQ, TAIL_COLS),
                          lambda g, hp: (0, jnp.clip(g * pairs + hp - gl_blocks, 0, u_blocks - 1)))
    return pl.pallas_call(
        body, grid=(len(DILATIONS), pairs),
        in_specs=views + [whole, whole, pair, pair, pair, u_spec, gl_spec],
        out_specs=HBM_OPERAND, out_shape=_sds((SEQ, IN_WIDTH), BF16),
        scratch_shapes=[pltpu.VMEM((SEQ, PAIR), F32)] * 3 + [pltpu.VMEM((SEQ, PAIR), BF16)] * 3
        + [pltpu.SemaphoreType.DMA((4,))],
        compiler_params=_cp(("arbitrary", "arbitrary")), name="attention_bwd")(
            proj, proj, proj, cos_f, sin_s, d_attn, attn, lse, d_u, d_gl)


def _cmul(ar, ai, br, bi):
    return ar * br - ai * bi, ar * bi + ai * br


def _pow256(ar, ai):
    for _ in range(8):
        ar, ai = _cmul(ar, ai, ar, ai)
    return ar, ai


def _chunk_carries(first_r, first_i, pr, pi, reverse):
    rows = lax.broadcasted_iota(jnp.int32, first_r.shape, 0)
    out_r = jnp.zeros_like(first_r)
    out_i = jnp.zeros_like(first_i)
    hr = jnp.zeros_like(first_r[0:1])
    hi = jnp.zeros_like(hr)
    order = range(SCAN_CHUNKS - 1, -1, -1) if reverse else range(SCAN_CHUNKS)
    for c in order:
        out_r = jnp.where(rows == c, hr, out_r)
        out_i = jnp.where(rows == c, hi, out_i)
        tr, ti = _cmul(pr[0:1], pi[0:1], hr, hi)
        hr = first_r[c:c + 1] + tr
        hi = first_i[c:c + 1] + ti
    return out_r, out_i


def _tile(j):
    return pl.ds(pl.multiple_of(j * SCAN_CHUNKS, SCAN_CHUNKS), SCAN_CHUNKS)


def _to_scan_rows(t):
    per = SCAN_STEPS // PHASES
    return t.reshape(PHASES, SCAN_CHUNKS, per, t.shape[1]).transpose(2, 0, 1, 3).reshape(t.shape)


def _from_scan_rows(t):
    per = SCAN_STEPS // PHASES
    return t.reshape(per, PHASES, SCAN_CHUNKS, t.shape[1]).transpose(1, 2, 0, 3).reshape(t.shape)


def _scan_in_place(hr_ref, hi_ref, a_r, a_i):
    def local(j, carry):
        tr, ti = _cmul(a_r, a_i, carry[0], carry[1])
        nr = tr + hr_ref[_tile(j), :]
        ni = ti + hi_ref[_tile(j), :]
        hr_ref[_tile(j), :] = nr
        hi_ref[_tile(j), :] = ni
        return nr, ni

    zero = jnp.zeros_like(a_r)
    last_r, last_i = lax.fori_loop(0, SCAN_STEPS, local, (zero, zero), unroll=4)
    pr, pi = _pow256(a_r, a_i)
    er, ei = _chunk_carries(last_r, last_i, pr, pi, reverse=False)

    def fix(j, carry):
        tr, ti = _cmul(carry[0], carry[1], er, ei)
        hr_ref[_tile(j), :] += tr
        hi_ref[_tile(j), :] += ti
        return _cmul(carry[0], carry[1], a_r, a_i)

    lax.fori_loop(0, SCAN_STEPS, fix, (a_r, a_i), unroll=4)
    return er, ei


def _reverse_scan_in_place(lr_ref, li_ref, hr_ref, hi_ref, er, ei, a_r, a_i):
    def local(t, carry):
        j = SCAN_STEPS - 1 - t
        tr, ti = _cmul(a_r, a_i, carry[0], carry[1])
        nr = tr + lr_ref[_tile(j), :]
        ni = ti + li_ref[_tile(j), :]
        lr_ref[_tile(j), :] = nr
        li_ref[_tile(j), :] = ni
        return nr, ni

    zero = jnp.zeros_like(a_r)
    first_r, first_i = lax.fori_loop(0, SCAN_STEPS, local, (zero, zero), unroll=4)
    pr, pi = _pow256(a_r, a_i)
    nxt_r, nxt_i = _chunk_carries(first_r, first_i, pr, pi, reverse=True)

    def accumulate(lam_r, lam_i, hp_r, hp_i, acc):
        return (acc[0] + lam_r * hp_r + lam_i * hp_i, acc[1] + lam_i * hp_r - lam_r * hp_i)

    def fix(t, carry):
        qr, qi, acc_r, acc_i = carry
        j = SCAN_STEPS - 1 - t
        tr, ti = _cmul(qr, qi, nxt_r, nxt_i)
        lam_r = lr_ref[_tile(j), :] + tr
        lam_i = li_ref[_tile(j), :] + ti
        lr_ref[_tile(j), :] = lam_r
        li_ref[_tile(j), :] = lam_i
        acc_r, acc_i = accumulate(lam_r, lam_i, hr_ref[_tile(j - 1), :], hi_ref[_tile(j - 1), :], (acc_r, acc_i))
        qr, qi = _cmul(qr, qi, a_r, a_i)
        return qr, qi, acc_r, acc_i

    qr, qi, acc_r, acc_i = lax.fori_loop(0, SCAN_STEPS - 1, fix, (a_r, a_i, zero, zero), unroll=4)
    tr, ti = _cmul(qr, qi, nxt_r, nxt_i)
    lam_r = lr_ref[_tile(0), :] + tr
    lam_i = li_ref[_tile(0), :] + ti
    lr_ref[_tile(0), :] = lam_r
    li_ref[_tile(0), :] = lam_i
    acc_r, acc_i = accumulate(lam_r, lam_i, er, ei, (acc_r, acc_i))
    return jnp.sum(acc_r, axis=0, keepdims=True), jnp.sum(acc_i, axis=0, keepdims=True)


def _rope_tables():
    half = HEAD_DIM // 2
    inv_freq = ROPE_THETA ** (-jnp.arange(half, dtype=F32) / half)
    ang = jnp.arange(SEQ, dtype=F32)[:, None] * inv_freq[None, :]
    cos, sin = jnp.cos(ang), jnp.sin(ang)
    cos_f = jnp.concatenate([cos, cos, cos, cos], axis=1)
    sin_s = jnp.concatenate([-sin, sin, -sin, sin], axis=1)
    return cos_f, sin_s


def _ssm_discretise(a_re, a_im, log_dt, b_re, b_im):
    lam = lax.complex(a_re, a_im)
    dt = jnp.exp(log_dt)[:, None]
    a_bar = jnp.exp(lam * dt)
    b_bar = ((a_bar - 1.0) / lam)[..., None] * lax.complex(b_re, b_im)
    return a_bar.real, a_bar.imag, b_bar.real, b_bar.imag


SSM_SLABS = 4
SLAB_GROUPS = SSM_GROUPS // SSM_SLABS
SLAB_IN = SSM_WIDTH // SSM_SLABS
SLAB_STATE = SSM_LANES // SSM_SLABS


def _slab_block_diag(blocks):
    _, r, c = blocks.shape
    eye = jnp.eye(SLAB_GROUPS, dtype=blocks.dtype)
    b5 = blocks.reshape(SSM_SLABS, SLAB_GROUPS, r, 1, c) * eye[None, :, None, :, None]
    return b5.reshape(SSM_SLABS, SLAB_GROUPS * r, SLAB_GROUPS * c)


def _diag_blocks(a, b):
    ra, cb = a.shape[1], b.shape[1]
    wa, wb = ra // SLAB_GROUPS, cb // SLAB_GROUPS
    d = lax.dot_general(a, b, (((0,), (0,)), ((), ())), preferred_element_type=F32)
    row_g = jnp.right_shift(lax.broadcasted_iota(jnp.int32, (ra, cb), 0), wa.bit_length() - 1)
    col_g = jnp.right_shift(lax.broadcasted_iota(jnp.int32, (ra, cb), 1), wb.bit_length() - 1)
    d = jnp.where(row_g == col_g, d, 0.0)
    fold = (jnp.bitwise_and(lax.broadcasted_iota(jnp.int32, (cb, wb), 0), wb - 1)
            == lax.broadcasted_iota(jnp.int32, (cb, wb), 1)).astype(F32)
    return jnp.dot(d, fold, preferred_element_type=F32, precision=lax.Precision.HIGHEST)


def _slab_specs():
    tok = pl.BlockSpec((SEQ, SLAB_IN), lambda j: (0, j))
    state = pl.BlockSpec((SEQ, SLAB_STATE), lambda j: (0, j))
    b_in = pl.BlockSpec((None, SLAB_IN, SLAB_STATE), lambda j: (j, 0, 0))
    c_out = pl.BlockSpec((None, SLAB_STATE, SLAB_IN), lambda j: (j, 0, 0))
    vec = pl.BlockSpec((1, SLAB_STATE), lambda j: (0, j))
    ent = pl.BlockSpec((SCAN_CHUNKS, SLAB_STATE), lambda j: (0, j))
    return tok, state, b_in, c_out, vec, ent


def _ssm_forward(u, b_in_r, b_in_i, c_out_r, c_out_ni, a_r, a_i):
    def body(u_ref, br_ref, bi_ref, cr_ref, ci_ref, ar_ref, ai_ref, y_ref, hr_ref, hi_ref, er_ref, ei_ref):
        uu = u_ref[...]
        hr_ref[...] = jnp.dot(uu, br_ref[...], preferred_element_type=F32)
        hi_ref[...] = jnp.dot(uu, bi_ref[...], preferred_element_type=F32)
        a_re = jnp.broadcast_to(ar_ref[...], (SCAN_CHUNKS, SLAB_STATE))
        a_im = jnp.broadcast_to(ai_ref[...], (SCAN_CHUNKS, SLAB_STATE))
        er_ref[...], ei_ref[...] = _scan_in_place(hr_ref, hi_ref, a_re, a_im)
        y_ref[...] = (jnp.dot(hr_ref[...].astype(BF16), cr_ref[...], preferred_element_type=F32)
                      + jnp.dot(hi_ref[...].astype(BF16), ci_ref[...], preferred_element_type=F32))

    tok, state, b_in, c_out, vec, ent = _slab_specs()
    return pl.pallas_call(
        body, grid=(SSM_SLABS,), in_specs=[tok, b_in, b_in, c_out, c_out, vec, vec],
        out_specs=[tok, state, state, ent, ent],
        out_shape=[_sds((SEQ, SSM_WIDTH)), _sds((SEQ, SSM_LANES)), _sds((SEQ, SSM_LANES)),
                   _sds((SCAN_CHUNKS, SSM_LANES)), _sds((SCAN_CHUNKS, SSM_LANES))],
        compiler_params=_cp(("parallel",)), name="ssm_forward")(u, b_in_r, b_in_i, c_out_r, c_out_ni, a_r, a_i)


def _ssm_backward(d_y, d_u_skip, u, h_r, h_i, e_r, e_i, b_in_r, b_in_i, c_out_r, c_out_ni, a_r, a_i):
    def body(dy_ref, skip_ref, u_ref, hr_ref, hi_ref, er_ref, ei_ref, br_ref, bi_ref, cr_ref, ci_ref, ar_ref, ai_ref,
             du_ref, dar_ref, dai_ref, dcr_ref, dci_ref, dbr_ref, dbi_ref, lr_ref, li_ref):
        dy = dy_ref[...]
        lr_ref[...] = _dot_nt(dy, cr_ref[...])
        li_ref[...] = _dot_nt(dy, ci_ref[...])
        a_re = jnp.broadcast_to(ar_ref[...], (SCAN_CHUNKS, SLAB_STATE))
        a_im = -jnp.broadcast_to(ai_ref[...], (SCAN_CHUNKS, SLAB_STATE))
        dar_ref[...], dai_ref[...] = _reverse_scan_in_place(lr_ref, li_ref, hr_ref, hi_ref, er_ref[...], ei_ref[...],
                                                            a_re, a_im)
        dcr_ref[...] = _diag_blocks(dy, hr_ref[...].astype(BF16))
        dci_ref[...] = _diag_blocks(dy, hi_ref[...].astype(BF16))
        lam_r, lam_i = lr_ref[...].astype(BF16), li_ref[...].astype(BF16)
        uu = u_ref[...]
        dbr_ref[...] = _diag_blocks(uu, lam_r)
        dbi_ref[...] = _diag_blocks(uu, lam_i)
        du = skip_ref[...] + _dot_nt(lam_r, br_ref[...]) + _dot_nt(lam_i, bi_ref[...])
        du_ref[...] = du.astype(BF16)

    tok, state, b_in, c_out, vec, ent = _slab_specs()
    db = pl.BlockSpec((SLAB_IN, SSM_STATE), lambda j: (j, 0))
    return pl.pallas_call(
        body, grid=(SSM_SLABS,), in_specs=[tok, tok, tok, state, state, ent, ent, b_in, b_in, c_out, c_out, vec, vec],
        out_specs=[tok, vec, vec, db, db, db, db],
        out_shape=[_sds((SEQ, SSM_WIDTH), BF16), _sds((1, SSM_LANES)), _sds((1, SSM_LANES))]
        + [_sds((SSM_WIDTH, SSM_STATE))] * 4,
        scratch_shapes=[pltpu.VMEM((SEQ, SLAB_STATE), F32)] * 2,
        compiler_params=_cp(("parallel",)), name="ssm_backward")(
            d_y, d_u_skip, u, h_r, h_i, e_r, e_i, b_in_r, b_in_i, c_out_r, c_out_ni, a_r, a_i)


FF_ROWS = 1024
FF_SHARD = D_FF // N_CHIPS
NORM_ROWS = 64


def _dot_nt(a, b):
    return lax.dot_general(a, b, (((1,), (1,)), ((), ())), preferred_element_type=F32)


def _ffn_up(h, w_gate_t, w_up_t):
    def body(h_ref, wg_ref, wu_ref, a_ref, b_ref, act_ref):
        hb = h_ref[...].astype(BF16)
        a = _dot_nt(hb, wg_ref[...])
        b = _dot_nt(hb, wu_ref[...])
        a_ref[...] = a
        b_ref[...] = b
        act_ref[...] = (a * jax.nn.sigmoid(a) * b).astype(BF16)

    w_spec = pl.BlockSpec((None, FF_SHARD, D_MODEL), lambda i, k: (k, 0, 0))
    o_spec = pl.BlockSpec((None, FF_ROWS, FF_SHARD), lambda i, k: (k, i, 0))
    shape = (N_CHIPS, SEQ, FF_SHARD)
    return pl.pallas_call(
        body, grid=(SEQ // FF_ROWS, N_CHIPS),
        in_specs=[pl.BlockSpec((FF_ROWS, D_MODEL), lambda i, k: (i, 0)), w_spec, w_spec],
        out_specs=[o_spec, o_spec, o_spec], out_shape=[_sds(shape), _sds(shape), _sds(shape, BF16)],
        compiler_params=_cp(("parallel", "parallel")), name="ffn_up")(h, w_gate_t, w_up_t)


def _ffn_down_ln2_loss(act, w_down, h, tgt, ln_g, ln_b):
    def body(act_ref, w_ref, h_ref, tgt_ref, g_ref, b_ref, dz_ref, loss_ref, dg_ref, db_ref, acc):
        i, k = pl.program_id(0), pl.program_id(1)
        part = jnp.dot(act_ref[...], w_ref[...], preferred_element_type=F32)

        @pl.when(k == 0)
        def _():
            acc[...] = part

        @pl.when(k > 0)
        def _():
            acc[...] += part

        @pl.when(k == N_CHIPS - 1)
        def _():
            g, b = g_ref[...], b_ref[...]

            def norm_rows(t, sums):
                rows = pl.ds(pl.multiple_of(t * NORM_ROWS, NORM_ROWS), NORM_ROWS)
                xhat, rstd = _ln_stats(DN_ALPHA * h_ref[rows, :] + acc[rows, :])
                err = xhat * g + b - tgt_ref[rows, :]
                d_out = err * (1.0 / D_MODEL)
                dz_ref[rows, :] = _ln_bwd(d_out, xhat, rstd, g)
                loss_rows = jnp.sum(err * err, axis=-1, keepdims=True) * (0.5 / D_MODEL)
                return (sums[0] + jnp.sum(loss_rows, axis=0, keepdims=True), sums[1] + _colsum(d_out * xhat),
                        sums[2] + _colsum(d_out))

            zeros = jnp.zeros((1, D_MODEL), F32)
            sums = lax.fori_loop(0, FF_ROWS // NORM_ROWS, norm_rows, (jnp.zeros((1, 1), F32), zeros, zeros))
            sums = (jnp.broadcast_to(sums[0], loss_ref.shape), sums[1], sums[2])
            for ref, val in zip((loss_ref, dg_ref, db_ref), sums):
                @pl.when(i == 0)
                def _(ref=ref, val=val):
                    ref[...] = val

                @pl.when(i > 0)
                def _(ref=ref, val=val):
                    ref[...] += val

    row = pl.BlockSpec((FF_ROWS, D_MODEL), lambda i, k: (i, 0))
    vec = pl.BlockSpec((1, D_MODEL), lambda i, k: (0, 0))
    return pl.pallas_call(
        body, grid=(SEQ // FF_ROWS, N_CHIPS),
        in_specs=[pl.BlockSpec((None, FF_ROWS, FF_SHARD), lambda i, k: (k, i, 0)),
                  pl.BlockSpec((None, FF_SHARD, D_MODEL), lambda i, k: (k, 0, 0)), row, row, vec, vec],
        out_specs=[row, pl.BlockSpec((1, BLOCK), lambda i, k: (0, 0)), vec, vec],
        out_shape=[_sds((SEQ, D_MODEL)), _sds((1, BLOCK)), _sds((1, D_MODEL)), _sds((1, D_MODEL))],
        scratch_shapes=[pltpu.VMEM((FF_ROWS, D_MODEL), F32)],
        compiler_params=_cp(("arbitrary", "arbitrary")), name="ffn_down_ln2_loss")(act, w_down, h, tgt, ln_g, ln_b)


def _ffn_down_bwd(dz, w_down, a, b):
    def body(dz_ref, wd_ref, a_ref, b_ref, da_ref, db_ref):
        d_act = _dot_nt(dz_ref[...].astype(BF16), wd_ref[...])
        av = a_ref[...]
        sg = jax.nn.sigmoid(av)
        da_ref[...] = (d_act * b_ref[...] * sg * (1.0 + av * (1.0 - sg))).astype(BF16)
        db_ref[...] = (d_act * av * sg).astype(BF16)

    t_spec = pl.BlockSpec((None, FF_ROWS, FF_SHARD), lambda i, k: (k, i, 0))
    shape = (N_CHIPS, SEQ, FF_SHARD)
    return pl.pallas_call(
        body, grid=(SEQ // FF_ROWS, N_CHIPS),
        in_specs=[pl.BlockSpec((FF_ROWS, D_MODEL), lambda i, k: (i, 0)),
                  pl.BlockSpec((None, FF_SHARD, D_MODEL), lambda i, k: (k, 0, 0)), t_spec, t_spec],
        out_specs=[t_spec, t_spec], out_shape=[_sds(shape, BF16), _sds(shape, BF16)],
        compiler_params=_cp(("parallel", "parallel")), name="ffn_down_bwd")(dz, w_down, a, b)


def _ffn_dh(d_a, d_b, w_gate_t, w_up_t):
    def body(da_ref, db_ref, wg_ref, wu_ref, o_ref, acc):
        k = pl.program_id(1)
        part = (jnp.dot(da_ref[...], wg_ref[...], preferred_element_type=F32)
                + jnp.dot(db_ref[...], wu_ref[...], preferred_element_type=F32))

        @pl.when(k == 0)
        def _():
            acc[...] = part

        @pl.when(k > 0)
        def _():
            acc[...] += part

        @pl.when(k == N_CHIPS - 1)
        def _():
            o_ref[...] = acc[...]

    t_spec = pl.BlockSpec((None, FF_ROWS, FF_SHARD), lambda i, k: (k, i, 0))
    w_spec = pl.BlockSpec((None, FF_SHARD, D_MODEL), lambda i, k: (k, 0, 0))
    return pl.pallas_call(
        body, grid=(SEQ // FF_ROWS, N_CHIPS), in_specs=[t_spec, t_spec, w_spec, w_spec],
        out_specs=pl.BlockSpec((FF_ROWS, D_MODEL), lambda i, k: (i, 0)), out_shape=_sds((SEQ, D_MODEL)),
        scratch_shapes=[pltpu.VMEM((FF_ROWS, D_MODEL), F32)],
        compiler_params=_cp(("parallel", "arbitrary")), name="ffn_dh")(d_a, d_b, w_gate_t, w_up_t)


def _local_step(x, tgt, wts, small, ffn_grads, mixer_grads, small_grads):
    s = SEQ
    cos_f, sin_s = [_to_phase_rows(t) for t in _rope_tables()]
    x = _reorder_rows(x, to_phase=True, name="phase_rows_x")
    tgt = _reorder_rows(tgt, to_phase=True, name="phase_rows_target")

    proj = _project_in(x, wts["w_in"], cos_f, sin_s)

    attn, lse = _attention_fwd(proj)

    (abar_r, abar_i, bbar_r, bbar_i), ssm_vjp = jax.vjp(
        _ssm_discretise, small["ssm_a_re"], small["ssm_a_im"], small["ssm_log_dt"], small["ssm_b_re"], small["ssm_b_im"])
    b_in_r, b_in_i = [_slab_block_diag(b.transpose(0, 2, 1)).astype(BF16) for b in (bbar_r, bbar_i)]
    c_out_r = _slab_block_diag(small["ssm_c_re"].transpose(0, 2, 1)).astype(BF16)
    c_out_ni = _slab_block_diag(-small["ssm_c_im"].transpose(0, 2, 1)).astype(BF16)
    a_r, a_i = abar_r.reshape(1, SSM_LANES), abar_i.reshape(1, SSM_LANES)
    d_skip = small["ssm_d"].reshape(1, SSM_WIDTH)

    u_f = _to_scan_rows(proj[:, 3 * QKV_WIDTH:3 * QKV_WIDTH + SSM_WIDTH])
    u_p = u_f.astype(BF16)
    y_c, h_r, h_i, e_r, e_i = _ssm_forward(u_p, b_in_r, b_in_i, c_out_r, c_out_ni, a_r, a_i)

    def branch(t, wg):
        return jnp.concatenate([jnp.dot(t, wg[k], preferred_element_type=F32) for k in range(N_CHIPS)], axis=1)

    def branch_t(t, wg):
        ns = wg.shape[2]
        return sum(_dot_nt(t[:, k * ns:(k + 1) * ns], wg[k]) for k in range(N_CHIPS))

    def gelu_glu(yc, u, dsk, wg):
        y = yc + dsk * u
        gel = (0.5 * y * (1.0 + jnp.tanh(GELU_C * (y + GELU_K * y * y * y)))).astype(BF16)
        glu = branch(gel, wg)
        return y, gel, glu, glu[:, :SSM_WIDTH] * jax.nn.sigmoid(glu[:, SSM_WIDTH:])

    y_s5, gel, glu, y_glu = _rowwise(
        gelu_glu, [y_c, u_f], [d_skip, wts["w_glu"]],
        [_sds((s, SSM_WIDTH)), _sds((s, SSM_WIDTH), BF16), _sds((s, 2 * SSM_WIDTH)), _sds((s, SSM_WIDTH), BF16)],
        tm=512, name="ssm_gelu_glu")
    y_glu = _from_scan_rows(y_glu)

    gl0 = (proj, D_MODEL, (3 * QKV_WIDTH + SSM_WIDTH) // D_MODEL)
    gl1 = (proj, D_MODEL, (3 * QKV_WIDTH + SSM_WIDTH) // D_MODEL + 1)
    b_gate = small["b_gate"]
    w_out = wts["w_out"].reshape(D_MODEL, D_MODEL)

    def mix_ln1(l0, l1, at, yg, xv, bg, wa, ws, wo, g, b):
        ya = branch(at.astype(BF16), wa)
        ys = branch(yg, ws)
        mixed = (jax.nn.sigmoid(l0 + bg[0:1]) * ya + jax.nn.sigmoid(l1 + bg[1:2]) * ys).astype(BF16)
        z = DN_ALPHA * xv + jnp.dot(mixed, wo, preferred_element_type=F32)
        xhat, _ = _ln_stats(z)
        return ya, ys, mixed, z, xhat * g + b

    y_attn, y_ssm, mixed, z1, h = _rowwise(
        mix_ln1, [gl0, gl1, attn, y_glu, x],
        [b_gate, wts["w_attn_br"], wts["w_ssm_br"], w_out, small["ln1_g"], small["ln1_b"]],
        [_sds((s, D_MODEL)), _sds((s, D_MODEL)), _sds((s, D_MODEL), BF16), _sds((s, D_MODEL)), _sds((s, D_MODEL))],
        tm=256, name="mix_ln1")

    nf = D_FF // N_CHIPS
    w_gate_t, w_up_t, w_down = wts["w_ff_gate"], wts["w_ff_up"], wts["w_ff_down"]
    ff_a, ff_b, act = _ffn_up(h, w_gate_t, w_up_t)
    dz2, loss_v, d_ln2_g, d_ln2_b = _ffn_down_ln2_loss(act, w_down, h, tgt, small["ln2_g"], small["ln2_b"])

    d_a, d_b = _ffn_down_bwd(dz2, w_down, ff_a, ff_b)

    def grad_rows(lhs, rhs, name):
        return _matmul(lhs, rhs, grid=(N_CHIPS,), a_spec=pl.BlockSpec((None, s, nf), lambda k: (k, 0, 0)),
                       b_spec=pl.BlockSpec((s, D_MODEL), lambda k: (0, 0)),
                       o_spec=pl.BlockSpec((None, nf, D_MODEL), lambda k: (k, 0, 0)),
                       out_shape=_sds((N_CHIPS, nf, D_MODEL), BF16), dims=(0, 0), name=name)

    g_w_ff_down = grad_rows(act, dz2, "g_w_ff_down")
    g_w_ff_gate = grad_rows(d_a, h, "g_w_ff_gate")
    g_w_ff_up = grad_rows(d_b, h, "g_w_ff_up")
    dh_ff = _ffn_dh(d_a, d_b, w_gate_t, w_up_t)

    def ln1_gate_bwd(dz, dff, z, l0, l1, ya, ys, g, bg, wo, wa, ws):
        xhat, rstd = _ln_stats(z)
        dh = DN_ALPHA * dz + dff
        dz_in = _ln_bwd(dh, xhat, rstd, g)
        dm = _dot_nt(dz_in.astype(BF16), wo)
        g0 = jax.nn.sigmoid(l0 + bg[0:1])
        g1 = jax.nn.sigmoid(l1 + bg[1:2])
        dl0 = dm * ya * g0 * (1.0 - g0)
        dl1 = dm * ys * g1 * (1.0 - g1)
        dya, dys = (dm * g0).astype(BF16), (dm * g1).astype(BF16)
        return (dz_in, dya, dys, jnp.concatenate([dl0, dl1], axis=1), branch_t(dya, wa), branch_t(dys, ws),
                _colsum(dh * xhat), _colsum(dh), _colsum(dl0), _colsum(dl1))

    dz1, d_y_attn, d_y_ssm, d_gl, d_attn, d_y_glu, d_ln1_g, d_ln1_b, d_bg0, d_bg1 = _rowwise(
        ln1_gate_bwd, [dz2, dh_ff, z1, gl0, gl1, y_attn, y_ssm],
        [small["ln1_g"], b_gate, w_out, wts["w_attn_br"], wts["w_ssm_br"]],
        [_sds((s, D_MODEL)), _sds((s, D_MODEL), BF16), _sds((s, D_MODEL), BF16), _sds((s, 2 * D_MODEL), BF16),
         _sds((s, ATTN_WIDTH)), _sds((s, SSM_WIDTH))],
        [_sds((1, D_MODEL))] * 4, tm=256, name="ln1_gate_bwd", after=(g_w_ff_down, g_w_ff_gate, g_w_ff_up))
    ffn_sent = ffn_grads({"w_ff_down": g_w_ff_down, "w_ff_gate": g_w_ff_gate, "w_ff_up": g_w_ff_up}, dz1)
    g_w_out = _mm_rows_tn(mixed, dz1, name="g_w_out")

    g_w_ssm_br = _mm_cols_tn(y_glu, d_y_ssm, ns=D_MODEL // N_CHIPS, name="g_w_ssm_br")
    d_y_glu = _to_scan_rows(d_y_glu)

    def glu_gelu_bwd(dyg, gl, y, u, dsk, wg):
        ga, gb = gl[:, :SSM_WIDTH], gl[:, SSM_WIDTH:]
        sg = jax.nn.sigmoid(gb)
        d_gl = jnp.concatenate([dyg * sg, dyg * ga * sg * (1.0 - sg)], axis=1).astype(BF16)
        dg = branch_t(d_gl, wg)
        th = jnp.tanh(GELU_C * (y + GELU_K * y * y * y))
        dy = dg * (0.5 * (1.0 + th) + 0.5 * y * (1.0 - th * th) * GELU_C * (1.0 + 3.0 * GELU_K * y * y))
        return d_gl, dy, dy * dsk, _colsum(dy * u)

    d_glu, d_y, d_u_skip, d_ssm_d = _rowwise(
        glu_gelu_bwd, [d_y_glu, glu, y_s5, u_f], [d_skip, wts["w_glu"]],
        [_sds((s, 2 * SSM_WIDTH), BF16), _sds((s, SSM_WIDTH), BF16), _sds((s, SSM_WIDTH))], [_sds((1, SSM_WIDTH))],
        tm=512, name="glu_gelu_bwd", after=tuple(ffn_sent))
    g_w_glu = _mm_cols_tn(gel, d_glu, ns=2 * SSM_WIDTH // N_CHIPS, name="g_w_glu")
    d_u, d_abar_r, d_abar_i, d_c_r, d_c_ni, d_bin_r, d_bin_i = _ssm_backward(
        d_y, d_u_skip, u_p, h_r, h_i, e_r, e_i, b_in_r, b_in_i, c_out_r, c_out_ni, a_r, a_i)
    d_u = _from_scan_rows(d_u)
    d_bbar_r = d_bin_r.reshape(SSM_GROUPS, SSM_GROUP, SSM_STATE).transpose(0, 2, 1)
    d_bbar_i = d_bin_i.reshape(SSM_GROUPS, SSM_GROUP, SSM_STATE).transpose(0, 2, 1)
    d_a_re, d_a_im, d_log_dt, d_b_re, d_b_im = ssm_vjp(
        (d_abar_r.reshape(SSM_GROUPS, SSM_STATE), d_abar_i.reshape(SSM_GROUPS, SSM_STATE), d_bbar_r, d_bbar_i))
    d_c_re = d_c_r.reshape(SSM_GROUPS, SSM_GROUP, SSM_STATE)
    d_c_im = -d_c_ni.reshape(SSM_GROUPS, SSM_GROUP, SSM_STATE)

    g_w_attn_br = _mm_cols_tn(attn, d_y_attn, ns=D_MODEL // N_CHIPS, name="g_w_attn_br")
    mixer_grads({"w_out": g_w_out, "w_ssm_br": g_w_ssm_br, "w_glu": g_w_glu, "w_attn_br": g_w_attn_br}, d_abar_r)
    small_g = {"b_gate": jnp.concatenate([d_bg0, d_bg1], axis=0), "ssm_a_re": d_a_re, "ssm_a_im": d_a_im,
               "ssm_log_dt": d_log_dt, "ssm_b_re": d_b_re, "ssm_b_im": d_b_im, "ssm_c_re": d_c_re, "ssm_c_im": d_c_im,
               "ssm_d": d_ssm_d.reshape(SSM_WIDTH), "ln1_g": d_ln1_g, "ln1_b": d_ln1_b, "ln2_g": d_ln2_g,
               "ln2_b": d_ln2_b}
    shared = small_grads(small_g, loss_v[0, 0])
    d_proj = _attention_bwd(proj, cos_f, sin_s, d_attn, attn, lse, d_u, d_gl)

    g_w_in = _mm_cols_tn(x, d_proj, ns=IN_WIDTH // N_CHIPS, name="g_w_in", after=tuple(shared))

    def grad_x_after(after):
        dx_proj = _mm_cols_nt(d_proj, wts["w_in"], tm=1024, name="dx_proj", after=after)
        return _reorder_rows(dz1, dx_proj, to_phase=False, name="grad_x", scale=DN_ALPHA)

    return grad_x_after, g_w_in, d_proj


GATHER_ID, SWAP_ID, SCATTER_ID, JOIN_ID, EXCHANGE_ID = 1, 2, 3, 4, 5


def _place():
    return lax.axis_index("x"), lax.axis_index("y"), lax.axis_index("c")


def _other_chips(x, y):
    return [(1 - x, y), (x, 1 - y), (1 - x, 1 - y)]


def _handshake(peers):
    barrier = pltpu.get_barrier_semaphore()
    for peer in peers:
        pl.semaphore_signal(barrier, inc=1, device_id=peer, device_id_type=MESH)
    pl.semaphore_wait(barrier, len(peers))


def _sequencer(body, arrays, out_type, sems, collective_id, name):
    return pl.kernel(body, name=name, out_type=out_type,
                     mesh=plsc.ScalarSubcoreMesh(axis_name="sequencer", num_cores=1), scratch_types=sems,
                     compiler_params=pltpu.CompilerParams(collective_id=collective_id))(*arrays)


def _gather_weights(shards, *, name, own_slot=True):
    nw = len(shards)

    def body(*refs):
        ins, outs = refs[:nw], refs[nw:2 * nw]
        send_sems, recv_sems, pass_send, pass_recv, local_sems = refs[2 * nw:]
        x, y, c = _place()
        chip = 2 * x + y
        chips = _other_chips(x, y)
        _handshake([(x, y, 1 - c)] + [(cx, cy, c) for cx, cy in chips])
        started, local = [], []
        for w in range(nw):
            hw = shards[w].shape[0] // 2
            mine = pl.ds(c * hw, hw)
            if own_slot:
                own = pltpu.make_async_copy(ins[w], outs[w].at[chip], local_sems.at[w])
                own.start()
                local.append(own)
            for j, (cx, cy) in enumerate(chips):
                cp = pltpu.make_async_remote_copy(
                    src_ref=ins[w].at[mine], dst_ref=outs[w].at[chip, mine], send_sem=send_sems.at[w, j],
                    recv_sem=recv_sems.at[w, j], device_id=(cx, cy, c), device_id_type=MESH)
                cp.start()
                started.append(cp)
        passed = []
        for w in range(nw):
            hw = shards[w].shape[0] // 2
            mine = pl.ds(c * hw, hw)
            for j, (cx, cy) in enumerate(chips):
                landed = outs[w].at[2 * cx + cy, mine]
                pltpu.make_async_remote_copy(
                    src_ref=ins[w].at[mine], dst_ref=landed, send_sem=send_sems.at[w, j],
                    recv_sem=recv_sems.at[w, j], device_id=(cx, cy, c), device_id_type=MESH).wait_recv()
                cp = pltpu.make_async_remote_copy(
                    src_ref=landed, dst_ref=landed, send_sem=pass_send.at[w, j], recv_sem=pass_recv.at[w, j],
                    device_id=(x, y, 1 - c), device_id_type=MESH)
                cp.start()
                passed.append(cp)
        for w in range(nw):
            hw = shards[w].shape[0] // 2
            theirs = pl.ds((1 - c) * hw, hw)
            for j, (cx, cy) in enumerate(chips):
                landed = outs[w].at[2 * cx + cy, theirs]
                pltpu.make_async_remote_copy(
                    src_ref=landed, dst_ref=landed, send_sem=pass_send.at[w, j], recv_sem=pass_recv.at[w, j],
                    device_id=(x, y, 1 - c), device_id_type=MESH).wait_recv()
        for cp in local:
            cp.wait()
        for cp in started + passed:
            cp.wait_send()

    sem = pltpu.SemaphoreType.DMA
    return _sequencer(body, shards, [_sds((N_CHIPS,) + a.shape, a.dtype) for a in shards],
                      [sem((nw, 3)), sem((nw, 3)), sem((nw, 3)), sem((nw, 3)), sem((nw,))], GATHER_ID, name)


def _swap_other_halves(grads, *, name):
    nw = len(grads)

    def body(*refs):
        ins, outs = refs[:nw], refs[nw:2 * nw]
        send_sems, recv_sems = refs[2 * nw:]
        x, y, c = _place()
        _handshake([(x, y, 1 - c)])
        cps = []
        for w in range(nw):
            hw = grads[w].shape[1] // 2
            cp = pltpu.make_async_remote_copy(
                src_ref=ins[w].at[:, pl.ds((1 - c) * hw, hw)], dst_ref=outs[w], send_sem=send_sems.at[w],
                recv_sem=recv_sems.at[w], device_id=(x, y, 1 - c), device_id_type=MESH)
            cp.start()
            cps.append(cp)
        for cp in cps:
            cp.wait()

    sem = pltpu.SemaphoreType.DMA
    return _sequencer(body, grads, [_sds((N_CHIPS, g.shape[1] // 2, g.shape[2]), g.dtype) for g in grads],
                      [sem((nw,)), sem((nw,))], SWAP_ID, name)


def _add_my_halves(core, grads, others, *, name, after=()):
    nw = len(grads)
    halves = [g.shape[1] // 2 for g in grads]

    def body(core_ref, *refs):
        outs = refs[2 * nw + len(after):]
        for g_ref, o_ref, out_ref in zip(refs[:nw], refs[nw:2 * nw], outs):
            out_ref[...] = (g_ref[...].astype(F32) + o_ref[...].astype(F32)).astype(out_ref.dtype)

    in_specs = [pl.BlockSpec((None, None, hw, g.shape[2]), lambda s, core_ref: (s, core_ref[0], 0, 0))
                for g, hw in zip(grads, halves)]
    in_specs += [pl.BlockSpec((None, hw, g.shape[2]), lambda s, core_ref: (s, 0, 0)) for g, hw in zip(grads, halves)]
    return pl.pallas_call(
        body,
        grid_spec=pltpu.PrefetchScalarGridSpec(
            num_scalar_prefetch=1, grid=(N_CHIPS,), in_specs=in_specs + [HBM_OPERAND] * len(after),
            out_specs=[pl.BlockSpec((None, hw, g.shape[2]), lambda s, core_ref: (s, 0, 0))
                       for g, hw in zip(grads, halves)]),
        out_shape=[_sds((N_CHIPS, hw, g.shape[2]), BF16) for g, hw in zip(grads, halves)],
        compiler_params=_cp(("parallel",)), name=name)(
            core, *[g.reshape(N_CHIPS, 2, hw, g.shape[2]) for g, hw in zip(grads, halves)], *others, *after)


def _scatter_partials(parts, *, name):
    nw = len(parts)

    def body(*refs):
        ins, outs = refs[:nw], refs[nw:2 * nw]
        send_sems, recv_sems = refs[2 * nw:]
        x, y, c = _place()
        _handshake([(cx, cy, c) for cx, cy in _other_chips(x, y)])
        cps = []
        for w in range(nw):
            for j, (cx, cy) in enumerate(_other_chips(x, y)):
                cp = pltpu.make_async_remote_copy(
                    src_ref=ins[w].at[2 * cx + cy], dst_ref=outs[w].at[j], send_sem=send_sems.at[w, j],
                    recv_sem=recv_sems.at[w, j], device_id=(cx, cy, c), device_id_type=MESH)
                cp.start()
                cps.append(cp)
        for cp in cps:
            cp.wait()

    sem = pltpu.SemaphoreType.DMA
    return _sequencer(body, parts, [_sds((3,) + p.shape[1:], p.dtype) for p in parts],
                      [sem((nw, 3)), sem((nw, 3))], SCATTER_ID, name)


SUM_STEPS = 2


def _sum_partials(chip, parts, recvd, *, name, after=()):
    nw = len(parts)
    rows = [p.shape[1] // SUM_STEPS for p in parts]

    def body(chip_ref, *refs):
        outs = refs[2 * nw + len(after):]
        for p_ref, r_ref, out_ref in zip(refs[:nw], refs[nw:2 * nw], outs):
            acc = p_ref[...].astype(F32)
            for j in range(3):
                acc = acc + r_ref[j].astype(F32)
            out_ref[...] = acc

    in_specs = [pl.BlockSpec((None, th, p.shape[2]), lambda i, chip_ref: (chip_ref[0], i, 0))
                for p, th in zip(parts, rows)]
    in_specs += [pl.BlockSpec((3, th, p.shape[2]), lambda i, chip_ref: (0, i, 0)) for p, th in zip(parts, rows)]
    return pl.pallas_call(
        body,
        grid_spec=pltpu.PrefetchScalarGridSpec(
            num_scalar_prefetch=1, grid=(SUM_STEPS,), in_specs=in_specs + [HBM_OPERAND] * len(after),
            out_specs=[pl.BlockSpec((th, p.shape[2]), lambda i, chip_ref: (i, 0)) for p, th in zip(parts, rows)]),
        out_shape=[_sds(p.shape[1:]) for p in parts], compiler_params=_cp(("parallel",)), name=name)(
            chip, *parts, *recvd, *after)


def _swap_reduced_halves(halves, *, name):
    nw = len(halves)

    def body(*refs):
        ins, outs = refs[:nw], refs[nw:2 * nw]
        send_sems, recv_sems = refs[2 * nw:]
        x, y, c = _place()
        _handshake([(x, y, 1 - c)])
        cps = []
        for w in range(nw):
            cp = pltpu.make_async_remote_copy(
                src_ref=ins[w], dst_ref=outs[w], send_sem=send_sems.at[w], recv_sem=recv_sems.at[w],
                device_id=(x, y, 1 - c), device_id_type=MESH)
            cp.start()
            cps.append(cp)
        for cp in cps:
            cp.wait()

    sem = pltpu.SemaphoreType.DMA
    return _sequencer(body, halves, [_sds(h.shape, h.dtype) for h in halves], [sem((nw,)), sem((nw,))], JOIN_ID, name)


def _exchange_rows(vec, *, name):
    def body(v_ref, slots, send_sems, recv_sems, local_sem):
        x, y, c = _place()
        me = 4 * x + 2 * y + c
        peers = []
        for mask in range(1, N_DEV):
            peers.append((1 - x if mask & 4 else x, 1 - y if mask & 2 else y, 1 - c if mask & 1 else c))
        _handshake(peers)
        own = pltpu.make_async_copy(v_ref, slots.at[me], local_sem)
        own.start()
        cps = []
        for k, peer in enumerate(peers):
            cp = pltpu.make_async_remote_copy(
                src_ref=v_ref, dst_ref=slots.at[me], send_sem=send_sems.at[k], recv_sem=recv_sems.at[k],
                device_id=peer, device_id_type=MESH)
            cp.start()
            cps.append(cp)
        for k, (px, py, pc) in enumerate(peers):
            pltpu.make_async_remote_copy(
                src_ref=v_ref, dst_ref=slots.at[4 * px + 2 * py + pc], send_sem=send_sems.at[k],
                recv_sem=recv_sems.at[k], device_id=(px, py, pc), device_id_type=MESH).wait_recv()
        for cp in cps:
            cp.wait_send()
        own.wait()

    sem = pltpu.SemaphoreType.DMA
    return _sequencer(body, [vec], [_sds((N_DEV,) + vec.shape)], [sem((N_DEV - 1,)), sem((N_DEV - 1,)), sem(())],
                      EXCHANGE_ID, name)[0]


def _sum_slots(slots, *, name, after=()):
    def body(s_ref, *rest):
        out_ref = rest[len(after)]
        acc = s_ref[0]
        for d in range(1, N_DEV):
            acc = acc + s_ref[d]
        out_ref[...] = acc

    vmem = pl.BlockSpec(memory_space=pltpu.VMEM)
    return pl.pallas_call(
        body, in_specs=[vmem] + [HBM_OPERAND] * len(after), out_specs=vmem, out_shape=_sds(slots.shape[1:]),
        compiler_params=pltpu.CompilerParams(vmem_limit_bytes=VMEM_LIMIT_BYTES), name=name)(slots, *after)


def _reduce_scatter_start(grads, core, *, tag, add_after=()):
    others = _swap_other_halves(grads, name="swap_other_halves_" + tag)
    parts = _add_my_halves(core, grads, others, name="add_my_halves_" + tag, after=add_after)
    return parts, _scatter_partials(parts, name="scatter_partials_" + tag)


def _reduce_scatter_finish(parts, recvd, chip, *, tag, sum_after=()):
    mine = _sum_partials(chip, parts, recvd, name="sum_partials_" + tag, after=sum_after)
    return mine, _swap_reduced_halves(mine, name="swap_reduced_halves_" + tag)


ADAM_BLOCK_ELEMS = 256 * 1024


def _adam_rows(rows, cols):
    tm = rows
    while tm * cols > ADAM_BLOCK_ELEMS and tm % 16 == 0:
        tm //= 2
    return tm


def _adam_step(wv, gv, mv, vv):
    m2 = ADAM_B1 * mv + (1.0 - ADAM_B1) * gv
    v2 = ADAM_B2 * vv + (1.0 - ADAM_B2) * (gv * gv)
    m_hat = m2 / (1.0 - ADAM_B1 ** ADAM_STEP)
    v_hat = v2 / (1.0 - ADAM_B2 ** ADAM_STEP)
    return -ADAM_LR * (m_hat / (jnp.sqrt(v_hat) + ADAM_EPS) + ADAM_WD * wv), m2, v2


def _adamw_each(ws, gs, ms, vs, *, name, after=()):
    n = len(ws)
    whole = pl.BlockSpec(memory_space=pltpu.VMEM)

    def body(*refs):
        ins, outs = refs[:4 * n], refs[4 * n + len(after):]
        for i in range(n):
            res = _adam_step(*(ins[k * n + i][...] for k in range(4)))
            for k in range(3):
                outs[k * n + i][...] = res[k]

    out = pl.pallas_call(body, in_specs=[whole] * (4 * n) + [HBM_OPERAND] * len(after),
                         out_shape=[_sds(w.shape) for w in ws] * 3, name=name)(*ws, *gs, *ms, *vs, *after)
    return out[:n], out[n:2 * n], out[2 * n:]


def _adamw_halves(core, w, g_mine, g_theirs, m, v, *, name, after=()):
    rows, cols = w.shape
    hw = rows // 2
    tm = _adam_rows(hw, cols)
    per_half = hw // tm

    def body(core_ref, w_ref, gm_ref, gt_ref, m_ref, v_ref, *rest):
        g_out, d_out, m_out, v_out = rest[len(after):]
        mine = (pl.program_id(0) // per_half) == core_ref[0]
        g = jnp.where(mine, gm_ref[...], gt_ref[...])
        d, m2, v2 = _adam_step(w_ref[...], g, m_ref[...], v_ref[...])
        g_out[...] = g
        d_out[...] = d
        m_out[...] = m2
        v_out[...] = v2

    full = pl.BlockSpec((tm, cols), lambda i, core_ref: (i, 0))

    def half(wanted):
        def index(i, core_ref):
            in_use = ((i // per_half) == core_ref[0]) == wanted
            return (jnp.where(in_use, i % per_half, 0), 0)
        return pl.BlockSpec((tm, cols), index)

    return pl.pallas_call(
        body,
        grid_spec=pltpu.PrefetchScalarGridSpec(
            num_scalar_prefetch=1, grid=(rows // tm,),
            in_specs=[full, half(True), half(False), full, full] + [HBM_OPERAND] * len(after),
            out_specs=[full, full, full, full]),
        out_shape=[_sds((rows, cols))] * 4, compiler_params=_cp(("parallel",)), name=name)(
            core, w, g_mine, g_theirs, m, v, *after)


HELD_TRANSPOSED = ("w_ff_gate", "w_ff_up")


def _as_rows(name, arr):
    return arr[0].T if name in HELD_TRANSPOSED else arr[0]


def _from_rows(name, arr2d):
    return (arr2d.T if name in HELD_TRANSPOSED else arr2d)[None]


STORED_SWAPPED = ("ssm_b_re", "ssm_b_im")


def _as_stored(name, arr):
    return jnp.swapaxes(arr, -1, -2) if name in STORED_SWAPPED else arr


def _pack_rows(arrs):
    flat = jnp.concatenate([a.reshape(-1).astype(F32) for a in arrs])
    rows = -(-flat.shape[0] // 1024) * 8
    return jnp.pad(flat, (0, rows * 128 - flat.shape[0])).reshape(rows, 128)


def _unpack_rows(vec, shapes):
    flat = vec.reshape(-1)
    out, off = [], 0
    for shp in shapes:
        size = math.prod(shp)
        out.append(flat[off:off + size].reshape(shp))
        off += size
    return out


SMALL = ("b_gate", "ssm_a_re", "ssm_a_im", "ssm_log_dt", "ssm_b_re", "ssm_b_im", "ssm_c_re", "ssm_c_im", "ssm_d",
         "ln1_g", "ln1_b", "ln2_g", "ln2_b")
GATHER_GROUPS = (("w_in", ("w_in",)), ("mixer", ("w_attn_br", "w_ssm_br", "w_glu", "w_out")),
                 ("ffn_up", ("w_ff_gate", "w_ff_up")), ("ffn_down", ("w_ff_down",)))
REDUCE_GROUPS = (("ffn", ("w_ff_down", "w_ff_gate", "w_ff_up")),
                 ("mixer", ("w_out", "w_ssm_br", "w_glu", "w_attn_br")), ("w_in", ("w_in",)))
WEIGHTS = ("w_in", "b_gate", "w_attn_br", "w_ssm_br", "w_out", "ssm_a_re", "ssm_a_im", "ssm_log_dt", "ssm_b_re",
           "ssm_b_im", "ssm_c_re", "ssm_c_im", "ssm_d", "w_glu", "ln1_g", "ln1_b", "w_ff_gate", "w_ff_up", "w_ff_down",
           "ln2_g", "ln2_b")


def kernel(x, w_in, b_gate, w_attn_br, w_ssm_br, w_out, ssm_a_re, ssm_a_im, ssm_log_dt, ssm_b_re, ssm_b_im, ssm_c_re, ssm_c_im, ssm_d, w_glu, ln1_g, ln1_b, w_ff_gate, w_ff_up, w_ff_down, ln2_g, ln2_b, loss_target, m_w_in, m_b_gate, m_w_attn_br, m_w_ssm_br, m_w_out, m_ssm_a_re, m_ssm_a_im, m_ssm_log_dt, m_ssm_b_re, m_ssm_b_im, m_ssm_c_re, m_ssm_c_im, m_ssm_d, m_w_glu, m_ln1_g, m_ln1_b, m_w_ff_gate, m_w_ff_up, m_w_ff_down, m_ln2_g, m_ln2_b, v_w_in, v_b_gate, v_w_attn_br, v_w_ssm_br, v_w_out, v_ssm_a_re, v_ssm_a_im, v_ssm_log_dt, v_ssm_b_re, v_ssm_b_im, v_ssm_c_re, v_ssm_c_im, v_ssm_d, v_w_glu, v_ln1_g, v_ln1_b, v_w_ff_gate, v_w_ff_up, v_w_ff_down, v_ln2_g, v_ln2_b):
    given = dict(locals())
    px, py, pc = _place()
    chip = 2 * px + py
    core_s = jnp.reshape(pc, (1,)).astype(jnp.int32)
    chip_s = jnp.reshape(chip, (1,)).astype(jnp.int32)

    wts = {}
    for tag, names in GATHER_GROUPS:
        shards = [_as_rows(n, given[n]).astype(BF16) for n in names]
        first = tag == GATHER_GROUPS[0][0]
        slots = _gather_weights(shards, name="gather_" + tag, own_slot=not first)
        if first:
            slots = [lax.dynamic_update_slice(g, s[None], (chip, 0, 0)) for g, s in zip(slots, shards)]
        wts.update(zip(names, slots))
    ncol = D_MODEL // N_CHIPS
    bg_mine = jnp.where(pc == 0, b_gate[0], jnp.zeros_like(b_gate[0]))
    bg_full = lax.dynamic_update_slice(jnp.zeros((2, D_MODEL), F32), bg_mine, (0, chip * ncol))
    bg_slots = _exchange_rows(bg_full.reshape(16, 128), name="exchange_gate_bias")
    bg_full = _sum_slots(bg_slots, name="sum_gate_bias").reshape(2, D_MODEL)
    small = {n: given[n][0] for n in SMALL if n.startswith("ssm")}
    small.update({n: given[n] for n in ("ln1_g", "ln1_b", "ln2_g", "ln2_b")})
    small["b_gate"] = bg_full

    groups = dict(REDUCE_GROUPS)
    parts, recvd, reduced, sent = {}, {}, {}, {}
    grads, delta, new_m, new_v, done = {}, {}, {}, {}, {}

    def start(tag, big_g, add_after):
        parts[tag], recvd[tag] = _reduce_scatter_start([big_g[n] for n in groups[tag]], core_s, tag=tag,
                                                       add_after=add_after)
        return parts[tag]

    def reduce_sum(tag, after):
        reduced[tag] = _reduce_scatter_finish(parts[tag], recvd[tag], chip_s, tag=tag, sum_after=after)
        return reduced[tag][0]

    def adam(tag, after, only=None):
        for n, g_mine, g_theirs in zip(groups[tag], *reduced[tag]):
            if only is not None and n not in only:
                continue
            res = _adamw_halves(core_s, _as_rows(n, given[n]), g_mine, g_theirs, _as_rows(n, given["m_" + n]),
                                _as_rows(n, given["v_" + n]), name="adamw_" + n, after=after)
            done[n] = res[1]
            grads[n], delta[n], new_m[n], new_v[n] = [_from_rows(n, r) for r in res]

    def ffn_grads(big_g, norm_bwd):
        return start("ffn", big_g, (norm_bwd,))

    def mixer_grads(big_g, scan_bwd):
        return start("mixer", big_g, (scan_bwd, *reduce_sum("ffn", (scan_bwd,))))

    def small_grads(small_g, loss_mine):
        sent["stored"] = [_as_stored(n, small_g[n]) for n in SMALL] + [loss_mine.reshape(1)]
        packed = _pack_rows(sent["stored"])
        sent["slots"] = _exchange_rows(packed, name="exchange_small")
        return (packed,)

    grad_x_after, g_w_in, attention_bwd = _local_step(x[0], loss_target[0], wts, small,
                                                      ffn_grads, mixer_grads, small_grads)

    reduce_sum("mixer", (attention_bwd,))
    summed = _sum_slots(sent["slots"], name="sum_small", after=(g_w_in,))
    adam("mixer", (g_w_in,))
    start("w_in", {"w_in": g_w_in}, (summed, *[done[n] for n in groups["mixer"]]))
    in_flight = (parts["w_in"][0],)
    grad_x = grad_x_after(in_flight)
    behind_scatter, behind_swap = groups["ffn"][:-1], groups["ffn"][-1:]
    adam("ffn", in_flight, behind_scatter)
    w_in_summed = tuple(reduce_sum("w_in", (*[done[n] for n in behind_scatter], grad_x)))
    adam("ffn", w_in_summed, behind_swap)
    summed = _unpack_rows(summed, [a.shape for a in sent["stored"]])
    loss = summed.pop()[0]
    at = SMALL.index("b_gate")
    summed[at] = lax.dynamic_slice(summed[at], (0, chip * ncol), (2, ncol))
    summed = [g.reshape(1, -1) if g.ndim == 1 else g for g in summed]
    held = [[_as_stored(n, given[prefix + n]).reshape(g.shape) for n, g in zip(SMALL, summed)]
            for prefix in ("", "m_", "v_")]
    small_out = _adamw_each(held[0], summed, held[1], held[2], name="adamw_small", after=w_in_summed)
    for out, arrs in zip((grads, delta, new_m, new_v), (summed, *small_out)):
        out.update((n, _as_stored(n, a).reshape(given[n].shape)) for n, a in zip(SMALL, arrs))
    adam("w_in", (*[done[n] for n in behind_swap], small_out[0][0]))

    return (loss, grad_x.reshape(x.shape), *[grads[n] for n in WEIGHTS], *[delta[n] for n in WEIGHTS],
            *[new_m[n] for n in WEIGHTS], *[new_v[n] for n in WEIGHTS])
```

```python
import math

import jax
import jax.numpy as jnp
from jax import lax
from jax.experimental import pallas as pl
from jax.experimental.pallas import tpu as pltpu
from jax.experimental.pallas import tpu_sc as plsc

F32 = jnp.float32
BF16 = jnp.bfloat16
MESH = pl.DeviceIdType.MESH

D_MODEL = 1024
SEQ = 2048
HEAD_DIM = 64
ATTN_HEADS = 8
DILATIONS = (1, 4, 16)
ATTN_WIDTH = ATTN_HEADS * HEAD_DIM
QKV_WIDTH = 3 * ATTN_WIDTH
BLOCK = 128
ROPE_THETA = 10000.0
NEG_INF = -1e30
SSM_GROUP = 16
SSM_GROUPS = 32
SSM_WIDTH = 512
SSM_STATE = 64
SSM_LANES = SSM_GROUPS * SSM_STATE
SCAN_CHUNKS = 8
SCAN_STEPS = SEQ // SCAN_CHUNKS
IN_WIDTH = 3 * QKV_WIDTH + SSM_WIDTH + 2 * D_MODEL
D_FF = 2816
N_CHIPS = 4
N_DEV = 8
DN_ALPHA = 2.0 ** 0.25
LN_EPS = 1e-5
ADAM_LR = 0.001
ADAM_B1 = 0.9
ADAM_B2 = 0.999
ADAM_EPS = 1e-08
ADAM_WD = 0.01
ADAM_STEP = 10
GELU_C = math.sqrt(2.0 / math.pi)
GELU_K = 0.044715

VMEM_LIMIT_BYTES = 56 * 1024 * 1024


def _sds(shape, dtype=F32):
    return jax.ShapeDtypeStruct(tuple(shape), dtype)


def _cp(semantics=None):
    return pltpu.CompilerParams(dimension_semantics=semantics, vmem_limit_bytes=VMEM_LIMIT_BYTES)


HBM_OPERAND = pl.BlockSpec(memory_space=pl.ANY)


def _matmul(a, b, *, grid, a_spec, b_spec, o_spec, out_shape, dims, k_axis=None, name, after=()):
    nk = grid[k_axis] if k_axis is not None else 1
    o_block = tuple(d for d in o_spec.block_shape if d is not None)
    n_after = len(after)

    def body(a_ref, b_ref, *rest):
        o_ref, acc = rest[n_after], rest[n_after + 1:]
        part = lax.dot_general(a_ref[...].astype(BF16), b_ref[...].astype(BF16),
                               (((dims[0],), (dims[1],)), ((), ())), preferred_element_type=F32)
        if k_axis is None:
            o_ref[...] = part.astype(o_ref.dtype)
        else:
            k = pl.program_id(k_axis)

            @pl.when(k == 0)
            def _():
                acc[0][...] = part

            @pl.when(k > 0)
            def _():
                acc[0][...] += part

            @pl.when(k == nk - 1)
            def _():
                o_ref[...] = acc[0][...].astype(o_ref.dtype)

    sem = tuple("arbitrary" if ax == k_axis else "parallel" for ax in range(len(grid)))
    return pl.pallas_call(
        body, grid=grid, in_specs=[a_spec, b_spec] + [HBM_OPERAND] * n_after, out_specs=o_spec, out_shape=out_shape,
        scratch_shapes=[pltpu.VMEM(o_block, F32)] if k_axis is not None else [],
        compiler_params=_cp(sem), name=name)(a, b, *after)


def _mm_cols_nt(dy, wg, *, tm, name, out_dtype=F32, after=()):
    k, ns = wg.shape[1], wg.shape[2]
    m = dy.shape[0]
    a_spec = pl.BlockSpec((tm, ns), lambda i, s: (i, s))
    return _matmul(dy, wg, grid=(m // tm, N_CHIPS), a_spec=a_spec,
                   b_spec=pl.BlockSpec((None, k, ns), lambda i, s: (s, 0, 0)),
                   o_spec=pl.BlockSpec((tm, k), lambda i, s: (i, 0)),
                   out_shape=_sds((m, k), out_dtype), dims=(1, 1), k_axis=1, name=name, after=after)


def _mm_cols_tn(a, dy, *, ns, name, after=()):
    m, k = a.shape
    return _matmul(a, dy, grid=(N_CHIPS,), a_spec=pl.BlockSpec((m, k), lambda s: (0, 0)),
                   b_spec=pl.BlockSpec((m, ns), lambda s: (0, s)),
                   o_spec=pl.BlockSpec((None, k, ns), lambda s: (s, 0, 0)),
                   out_shape=_sds((N_CHIPS, k, ns), BF16), dims=(0, 0), name=name, after=after)


def _mm_rows_tn(a, dy, *, name):
    m, k = a.shape
    rows, n = k // N_CHIPS, dy.shape[1]
    return _matmul(a, dy, grid=(N_CHIPS,), a_spec=pl.BlockSpec((m, rows), lambda s: (0, s)),
                   b_spec=pl.BlockSpec((m, n), lambda s: (0, 0)),
                   o_spec=pl.BlockSpec((None, rows, n), lambda s: (s, 0, 0)),
                   out_shape=_sds((N_CHIPS, rows, n), BF16), dims=(0, 0), name=name)


def _rowwise(fn, tiled, full, outs, accs=(), *, tm, name, after=()):
    args, in_specs = [], []
    for t in tiled:
        if isinstance(t, tuple):
            arr, w, cb = t
            in_specs.append(pl.BlockSpec((tm, w), lambda i, cb=cb: (i, cb)))
        else:
            arr = t
            in_specs.append(pl.BlockSpec((tm, arr.shape[1]), lambda i: (i, 0)))
        args.append(arr)
    rows = args[0].shape[0]
    for f in full:
        in_specs.append(pl.BlockSpec(f.shape, lambda i, nd=f.ndim: (0,) * nd))
        args.append(f)
    out_specs = [pl.BlockSpec((tm, o.shape[1]), lambda i: (i, 0)) for o in outs]
    out_specs += [pl.BlockSpec(a.shape, lambda i, nd=len(a.shape): (0,) * nd) for a in accs]
    n_in, n_out = len(args), len(outs)
    in_specs += [HBM_OPERAND] * len(after)
    first_out = n_in + len(after)

    def body(*refs):
        res = fn(*[r[...] for r in refs[:n_in]])
        res = res if isinstance(res, (tuple, list)) else (res,)
        for r, v in zip(refs[first_out:first_out + n_out], res[:n_out]):
            r[...] = v.astype(r.dtype)
        i = pl.program_id(0)
        for r, v in zip(refs[first_out + n_out:], res[n_out:]):
            @pl.when(i == 0)
            def _(r=r, v=v):
                r[...] = v

            @pl.when(i > 0)
            def _(r=r, v=v):
                r[...] += v

    res = pl.pallas_call(
        body, grid=(rows // tm,), in_specs=in_specs, out_specs=out_specs, out_shape=list(outs) + list(accs),
        compiler_params=_cp(("arbitrary",) if accs else ("parallel",)), name=name)(*args, *after)
    return res


def _colsum(v):
    return jnp.sum(v, axis=0, keepdims=True)


def _ln_stats(z):
    mu = jnp.mean(z, axis=-1, keepdims=True)
    zc = z - mu
    var = jnp.mean(zc * zc, axis=-1, keepdims=True)
    rstd = lax.rsqrt(var + LN_EPS)
    return zc * rstd, rstd


def _ln_bwd(dy, xhat, rstd, g):
    dxh = dy * g
    m1 = jnp.mean(dxh, axis=-1, keepdims=True)
    m2 = jnp.mean(dxh * xhat, axis=-1, keepdims=True)
    return rstd * (dxh - m1 - xhat * m2)


def _swap_halves(t):
    w = t.shape[-1]
    lane = lax.broadcasted_iota(jnp.int32, t.shape, t.ndim - 1)
    return jnp.where((lane % HEAD_DIM) < HEAD_DIM // 2, pltpu.roll(t, w - HEAD_DIM // 2, t.ndim - 1),
                     pltpu.roll(t, HEAD_DIM // 2, t.ndim - 1))


PHASES = max(DILATIONS)
PAIR = 2 * HEAD_DIM
UNITS = SEQ // BLOCK
UNIT_BATCH = 16
ROPE_ROWS = 256
TAIL_COLS = 256


def _to_phase_rows(t):
    return t.reshape(SEQ // PHASES, PHASES, t.shape[1]).transpose(1, 0, 2).reshape(t.shape)


def _reorder_rows(arr, plus=None, *, to_phase, name, scale=1.0):
    def body(*refs):
        o_ref = refs[-1]
        for rho in range(PHASES):
            phase = pl.ds(rho * BLOCK, BLOCK)
            strided = pl.ds(rho, BLOCK, stride=PHASES)
            src, dst = (strided, phase) if to_phase else (phase, strided)
            val = refs[0][src, :]
            if scale != 1.0:
                val = val * scale
            if plus is not None:
                val = val + refs[1][src, :]
            o_ref[dst, :] = val

    spec = pl.BlockSpec((SEQ, BLOCK), lambda j: (0, j))
    ins = [arr] if plus is None else [arr, plus]
    return pl.pallas_call(body, grid=(arr.shape[1] // BLOCK,), in_specs=[spec] * len(ins), out_specs=spec,
                          out_shape=_sds(arr.shape), compiler_params=_cp(("parallel",)), name=name)(*ins)


def _rope(t, cf, ss):
    return t * cf + _swap_halves(t) * ss


def _rope_transposed(d, cf, ss):
    return d * cf + _swap_halves(d * ss)


def _unit_pieces(u, dil):
    pieces, length = PHASES // dil, 8 * dil
    if dil == 1:
        rho, i = 0, u
    elif dil == PHASES:
        rho, i = u, 0
    else:
        rho, i = jnp.bitwise_and(u, dil - 1), jnp.right_shift(u, dil.bit_length() - 1)
    before = jnp.maximum(i - 1, 0)
    cur = [pl.multiple_of((rho + dil * k) * BLOCK + length * i, 8) for k in range(pieces)]
    prev = [pl.multiple_of((rho + dil * k) * BLOCK + length * before, 8) for k in range(pieces)]
    return i, cur, prev


def _load_tile(ref, starts, dil):
    return jnp.concatenate([ref[pl.ds(st, 8 * dil), :] for st in starts], axis=0)


def _store_tile(ref, starts, dil, val, head=None, accumulate=False):
    length = 8 * dil
    lanes = slice(None) if head is None else pl.ds(head * HEAD_DIM, HEAD_DIM)
    cols = slice(None) if head is None else slice(head * HEAD_DIM, (head + 1) * HEAD_DIM)
    for k, st in enumerate(starts):
        piece = val[k * length:(k + 1) * length, cols]
        if accumulate:
            ref[pl.ds(st, length), lanes] += piece
        else:
            ref[pl.ds(st, length), lanes] = piece


def _tile_position(idx, dil):
    pieces, length = PHASES // dil, 8 * dil
    return pieces * jnp.bitwise_and(idx, length - 1) + jnp.right_shift(idx, length.bit_length() - 1)


def _band_mask(i, dil):
    row = lax.broadcasted_iota(jnp.int32, (BLOCK, 2 * BLOCK), 0)
    col = lax.broadcasted_iota(jnp.int32, (BLOCK, 2 * BLOCK), 1)
    key_pos = _tile_position(jnp.bitwise_and(col, BLOCK - 1), dil) + jnp.where(col >= BLOCK, 0, -BLOCK)
    dist = _tile_position(row, dil) - key_pos
    return (dist >= 0) & (dist <= BLOCK) & ((col >= BLOCK) | (i > 0))


def _causal_mask():
    row = lax.broadcasted_iota(jnp.int32, (BLOCK, BLOCK), 0)
    col = lax.broadcasted_iota(jnp.int32, (BLOCK, BLOCK), 1)
    return row >= col


def _pair_views(col0):
    return [pl.BlockSpec((SEQ, PAIR), lambda hp, g=g: (0, col0 // PAIR + g * (ATTN_WIDTH // PAIR) + hp))
            for g in range(len(DILATIONS))]


def _project_shard(shard, x_ref, w_ref, cf_ref, ss_ref, o_ref, skip=None):
    ns = w_ref.shape[1]
    tiles = ns // PAIR
    xb = x_ref[...].astype(BF16)
    cf, ss = cf_ref[...], ss_ref[...]

    def write(rotated, scaled):
        for t0 in range(0, tiles, 2):
            strip = jnp.dot(xb, w_ref[:, t0 * PAIR:(t0 + 2) * PAIR], preferred_element_type=F32)
            for t in (t0, t0 + 1):
                val = strip[:, (t - t0) * PAIR:(t - t0 + 1) * PAIR]
                if t < rotated:
                    val = _rope(val, cf, ss)
                    if t < scaled:
                        val = val * (1.0 / math.sqrt(HEAD_DIM))
                o_ref[:, t * PAIR:(t + 1) * PAIR] = val

    for s in range(N_CHIPS):
        rotated = min(max(2 * QKV_WIDTH - s * ns, 0), ns) // PAIR
        scaled = min(max(QKV_WIDTH - s * ns, 0), ns) // PAIR
        wanted = shard == s if skip is None else jnp.logical_and(shard == s, jnp.logical_not(skip))

        @pl.when(wanted)
        def _(rotated=rotated, scaled=scaled):
            write(rotated, scaled)


def _project_in_own(chip, x, w_own, cos_f, sin_s):
    ns = w_own.shape[1]

    def body(chip_ref, x_ref, w_ref, cf_ref, ss_ref, o_ref):
        _project_shard(chip_ref[0], x_ref, w_ref, cf_ref, ss_ref, o_ref)

    table = pl.BlockSpec((FF_ROWS, PAIR), lambda i, chip_ref: (i, 0))
    return pl.pallas_call(
        body,
        grid_spec=pltpu.PrefetchScalarGridSpec(
            num_scalar_prefetch=1, grid=(SEQ // FF_ROWS,),
            in_specs=[pl.BlockSpec((FF_ROWS, D_MODEL), lambda i, chip_ref: (i, 0)),
                      pl.BlockSpec((D_MODEL, ns), lambda i, chip_ref: (0, 0)), table, table],
            out_specs=pl.BlockSpec((FF_ROWS, ns), lambda i, chip_ref: (i, 0))),
        out_shape=_sds((SEQ, ns)), compiler_params=_cp(("parallel",)), name="project_in_own")(
            chip, x, w_own, cos_f, sin_s)


def _project_in(chip, x, wg, cos_f, sin_s, own):
    ns = wg.shape[2]

    def body(chip_ref, x_ref, w_ref, cf_ref, ss_ref, own_ref, o_ref):
        shard = pl.program_id(1)
        mine = shard == chip_ref[0]

        @pl.when(mine)
        def _():
            o_ref[...] = own_ref[...]

        _project_shard(shard, x_ref, w_ref, cf_ref, ss_ref, o_ref, skip=mine)

    def gathered(i, s, chip_ref):
        return (jnp.where(s == chip_ref[0], (s + 1) % N_CHIPS, s), 0, 0)

    table = pl.BlockSpec((FF_ROWS, PAIR), lambda i, s, chip_ref: (i, 0))
    return pl.pallas_call(
        body,
        grid_spec=pltpu.PrefetchScalarGridSpec(
            num_scalar_prefetch=1, grid=(SEQ // FF_ROWS, N_CHIPS),
            in_specs=[pl.BlockSpec((FF_ROWS, D_MODEL), lambda i, s, chip_ref: (i, 0)),
                      pl.BlockSpec((None, D_MODEL, ns), gathered), table, table,
                      pl.BlockSpec((FF_ROWS, ns), lambda i, s, chip_ref: (i, 0))],
            out_specs=pl.BlockSpec((FF_ROWS, ns), lambda i, s, chip_ref: (i, s))),
        out_shape=_sds((SEQ, N_CHIPS * ns)), compiler_params=_cp(("parallel", "parallel")), name="project_in")(
            chip, x, wg, cos_f, sin_s, own)


def _attention_fwd(proj):
    ng = len(DILATIONS)

    def body(*refs):
        q_refs, k_refs, v_refs = refs[:ng], refs[ng:2 * ng], refs[2 * ng:3 * ng]
        attn_ref, lse_ref = refs[3 * ng:]
        qr_refs, kr_refs = q_refs, k_refs
        first = lax.broadcasted_iota(jnp.int32, (BLOCK, PAIR), 1) < HEAD_DIM
        for g, dil in enumerate(DILATIONS):
            two_blocks = SEQ // dil > BLOCK

            def units(t, carry, g=g, dil=dil, two_blocks=two_blocks):
                picked = [_unit_pieces(t * UNIT_BATCH + j, dil) for j in range(UNIT_BATCH)]

                def tiles(ref, with_prev=False):
                    if with_prev and two_blocks:
                        return jnp.stack([jnp.concatenate([_load_tile(ref, prev, dil), _load_tile(ref, rows, dil)],
                                                          axis=0) for _, rows, prev in picked])
                    return jnp.stack([_load_tile(ref, rows, dil) for _, rows, _ in picked])

                qq = tiles(qr_refs[g]).astype(BF16)
                kk = tiles(kr_refs[g], True).astype(BF16)
                vv = tiles(v_refs[g], True).astype(BF16)
                if two_blocks:
                    valid = jnp.stack([_band_mask(i, dil) for i, _, _ in picked])
                else:
                    valid = _causal_mask()[None]
                mine = first[None]
                zero = jnp.zeros_like(qq)
                outs, lses = [], []
                for qh in (jnp.where(mine, qq, zero), jnp.where(mine, zero, qq)):
                    s = jnp.einsum("pqd,pkd->pqk", qh, kk, preferred_element_type=F32)
                    s = jnp.where(valid, s, NEG_INF)
                    m = jnp.max(s, axis=-1, keepdims=True)
                    p = jnp.exp(s - m)
                    l = jnp.sum(p, axis=-1, keepdims=True)
                    outs.append(jnp.einsum("pqk,pkd->pqd", p.astype(BF16), vv, preferred_element_type=F32) * (1.0 / l))
                    lses.append(m + jnp.log(l))
                o = jnp.where(mine, outs[0], outs[1])
                lse = jnp.where(mine, lses[0], lses[1])
                if g > 0:
                    lse_old = tiles(lse_ref)
                    m = jnp.maximum(lse_old, lse)
                    lse_new = m + jnp.log(jnp.exp(lse_old - m) + jnp.exp(lse - m))
                    o = tiles(attn_ref) * jnp.exp(lse_old - lse_new) + o * jnp.exp(lse - lse_new)
                    lse = lse_new
                for j, (_, rows, _) in enumerate(picked):
                    _store_tile(attn_ref, rows, dil, o[j])
                    _store_tile(lse_ref, rows, dil, lse[j])
                return carry

            lax.fori_loop(0, UNITS // UNIT_BATCH, units, 0)

    out = pl.BlockSpec((SEQ, PAIR), lambda hp: (0, hp))
    return pl.pallas_call(
        body, grid=(ATTN_WIDTH // PAIR,),
        in_specs=_pair_views(0) + _pair_views(QKV_WIDTH) + _pair_views(2 * QKV_WIDTH),
        out_specs=[out, out], out_shape=[_sds((SEQ, ATTN_WIDTH)), _sds((SEQ, ATTN_WIDTH))],
        compiler_params=_cp(("parallel",)), name="attention_fwd")(*([proj] * (3 * ng)))


def _attention_bwd(proj, cos_f, sin_s, d_attn, attn, lse, d_u, d_gl):
    pairs = ATTN_WIDTH // PAIR
    last = len(DILATIONS) * pairs - 1

    def accumulate(dil, qr_ref, kr_ref, v_ref, do_ref, o_ref, lse_ref, dq_acc, dk_acc, dv_acc):
        two_blocks = SEQ // dil > BLOCK
        dk_acc[...] = jnp.zeros_like(dk_acc)
        dv_acc[...] = jnp.zeros_like(dv_acc)
        nk = 2 * BLOCK if two_blocks else BLOCK
        first = lax.broadcasted_iota(jnp.int32, (BLOCK, PAIR), 1) < HEAD_DIM
        first_k = lax.broadcasted_iota(jnp.int32, (nk, PAIR), 1) < HEAD_DIM

        def units(t, carry):
            picked = [_unit_pieces(t * UNIT_BATCH + j, dil) for j in range(UNIT_BATCH)]

            def tiles(ref, with_prev=False):
                if with_prev and two_blocks:
                    return jnp.stack([jnp.concatenate([_load_tile(ref, prev, dil), _load_tile(ref, rows, dil)], axis=0)
                                      for _, rows, prev in picked])
                return jnp.stack([_load_tile(ref, rows, dil) for _, rows, _ in picked])

            qq = tiles(qr_ref).astype(BF16)
            kk = tiles(kr_ref, True).astype(BF16)
            vv = tiles(v_ref, True).astype(BF16)
            dof = tiles(do_ref)
            dd = dof * tiles(o_ref)
            lse3 = tiles(lse_ref)
            dob = dof.astype(BF16)
            if two_blocks:
                valid = jnp.stack([_band_mask(i, dil) for i, _, _ in picked])
            else:
                valid = _causal_mask()[None]
            zq, zf = jnp.zeros_like(qq), jnp.zeros_like(dd)
            dqs, dks, dvs = [], [], []
            for head in range(2):
                mine = first[None] if head == 0 else jnp.logical_not(first)[None]
                delta = jnp.sum(jnp.where(mine, dd, zf), axis=-1, keepdims=True)
                lse_h = lse3[:, :, head * HEAD_DIM:head * HEAD_DIM + 1]
                s = jnp.einsum("pqd,pkd->pqk", jnp.where(mine, qq, zq), kk, preferred_element_type=F32)
                p = jnp.where(valid, jnp.exp(s - lse_h), 0.0)
                dp = jnp.einsum("pqd,pkd->pqk", jnp.where(mine, dob, zq), vv, preferred_element_type=F32)
                ds = (p * (dp - delta)).astype(BF16)
                dqs.append(jnp.einsum("pqk,pkd->pqd", ds, kk, preferred_element_type=F32))
                dks.append(jnp.einsum("pqk,pqd->pkd", ds, qq, preferred_element_type=F32))
                dvs.append(jnp.einsum("pqk,pqd->pkd", p.astype(BF16), dob, preferred_element_type=F32))
            dq = jnp.where(first[None], dqs[0], dqs[1])
            dk = jnp.where(first_k[None], dks[0], dks[1])
            dv = jnp.where(first_k[None], dvs[0], dvs[1])
            for j, (_, rows, prev) in enumerate(picked):
                _store_tile(dq_acc, rows, dil, dq[j])
                _store_tile(dk_acc, rows, dil, dk[j, nk - BLOCK:], accumulate=True)
                _store_tile(dv_acc, rows, dil, dv[j, nk - BLOCK:], accumulate=True)
                if two_blocks:
                    _store_tile(dk_acc, prev, dil, dk[j, :BLOCK], accumulate=True)
                    _store_tile(dv_acc, prev, dil, dv[j, :BLOCK], accumulate=True)
            return carry

        lax.fori_loop(0, UNITS // UNIT_BATCH, units, 0)

    def body(qr_ref, kr_ref, v_ref, cf_ref, ss_ref, do_ref, o_ref, lse_ref, du_ref, dgl_ref, out_ref,
             dq_acc, dk_acc, dv_acc, dq_buf, dk_buf, dv_buf, sems):
        step = pl.program_id(0) * pairs + pl.program_id(1)

        def columns(at):
            return [pltpu.make_async_copy(
                buf, out_ref.at[:, pl.ds(pl.multiple_of(j * QKV_WIDTH + at * PAIR, PAIR), PAIR)], sems.at[j])
                for j, buf in enumerate((dq_buf, dk_buf, dv_buf))]

        def tail(ref, col0, at):
            cols = pl.ds(pl.multiple_of(col0 + at * TAIL_COLS, TAIL_COLS), TAIL_COLS)
            return pltpu.make_async_copy(ref, out_ref.at[:, cols], sems.at[3])

        gl_steps, u_steps = 2 * D_MODEL // TAIL_COLS, SSM_WIDTH // TAIL_COLS
        from_gl = step < gl_steps
        from_u = jnp.logical_and(step >= gl_steps, step < gl_steps + u_steps)
        tail_gl = tail(dgl_ref, 3 * QKV_WIDTH + SSM_WIDTH, step)
        tail_u = tail(du_ref, 3 * QKV_WIDTH, step - gl_steps)
        pl.when(from_gl)(tail_gl.start)
        pl.when(from_u)(tail_u.start)

        for g, dil in enumerate(DILATIONS):
            @pl.when(pl.program_id(0) == g)
            def _(dil=dil):
                accumulate(dil, qr_ref, kr_ref, v_ref, do_ref, o_ref, lse_ref, dq_acc, dk_acc, dv_acc)

        @pl.when(step > 0)
        def _():
            for cp in columns(step - 1):
                cp.wait()

        def finish(t, carry):
            rows = pl.ds(pl.multiple_of(t * ROPE_ROWS, ROPE_ROWS), ROPE_ROWS)
            cf, ss = cf_ref[rows, :], ss_ref[rows, :]
            dq = dq_acc[rows, :] * (1.0 / math.sqrt(HEAD_DIM))
            dq_buf[rows, :] = _rope_transposed(dq, cf, ss).astype(BF16)
            dk_buf[rows, :] = _rope_transposed(dk_acc[rows, :], cf, ss).astype(BF16)
            dv_buf[rows, :] = dv_acc[rows, :].astype(BF16)
            return carry

        lax.fori_loop(0, SEQ // ROPE_ROWS, finish, 0)
        for cp in columns(step):
            cp.start()
        pl.when(from_gl)(tail_gl.wait)
        pl.when(from_u)(tail_u.wait)

        @pl.when(step == last)
        def _():
            for cp in columns(step):
                cp.wait()

    whole = pl.BlockSpec((SEQ, PAIR), lambda g, hp: (0, 0))
    pair = pl.BlockSpec((SEQ, PAIR), lambda g, hp: (0, hp))
    views = [pl.BlockSpec((SEQ, PAIR), lambda g, hp, c0=col0 // PAIR: (0, c0 + g * pairs + hp))
             for col0 in (0, QKV_WIDTH, 2 * QKV_WIDTH)]
    gl_blocks, u_blocks = 2 * D_MODEL // TAIL_COLS, SSM_WIDTH // TAIL_COLS
    assert gl_blocks + u_blocks <= last + 1
    gl_spec = pl.BlockSpec((SEQ, TAIL_COLS), lambda g, hp: (0, jnp.minimum(g * pairs + hp, gl_blocks - 1)))
    u_spec = pl.BlockSpec((SEQ, TAIL_COLS),
                          lambda g, hp: (0, jnp.clip(g * pairs + hp - gl_blocks, 0, u_blocks - 1)))
    return pl.pallas_call(
        body, grid=(len(DILATIONS), pairs),
        in_specs=views + [whole, whole, pair, pair, pair, u_spec, gl_spec],
        out_specs=HBM_OPERAND, out_shape=_sds((SEQ, IN_WIDTH), BF16),
        scratch_shapes=[pltpu.VMEM((SEQ, PAIR), F32)] * 3 + [pltpu.VMEM((SEQ, PAIR), BF16)] * 3
        + [pltpu.SemaphoreType.DMA((4,))],
        compiler_params=_cp(("arbitrary", "arbitrary")), name="attention_bwd")(
            proj, proj, proj, cos_f, sin_s, d_attn, attn, lse, d_u, d_gl)


def _cmul(ar, ai, br, bi):
    return ar * br - ai * bi, ar * bi + ai * br


def _pow256(ar, ai):
    for _ in range(8):
        ar, ai = _cmul(ar, ai, ar, ai)
    return ar, ai


def _chunk_carries(first_r, first_i, pr, pi, reverse):
    rows = lax.broadcasted_iota(jnp.int32, first_r.shape, 0)
    out_r = jnp.zeros_like(first_r)
    out_i = jnp.zeros_like(first_i)
    hr = jnp.zeros_like(first_r[0:1])
    hi = jnp.zeros_like(hr)
    order = range(SCAN_CHUNKS - 1, -1, -1) if reverse else range(SCAN_CHUNKS)
    for c in order:
        out_r = jnp.where(rows == c, hr, out_r)
        out_i = jnp.where(rows == c, hi, out_i)
        tr, ti = _cmul(pr[0:1], pi[0:1], hr, hi)
        hr = first_r[c:c + 1] + tr
        hi = first_i[c:c + 1] + ti
    return out_r, out_i


def _tile(j):
    return pl.ds(pl.multiple_of(j * SCAN_CHUNKS, SCAN_CHUNKS), SCAN_CHUNKS)


def _to_scan_rows(t):
    per = SCAN_STEPS // PHASES
    return t.reshape(PHASES, SCAN_CHUNKS, per, t.shape[1]).transpose(2, 0, 1, 3).reshape(t.shape)


def _from_scan_rows(t):
    per = SCAN_STEPS // PHASES
    return t.reshape(per, PHASES, SCAN_CHUNKS, t.shape[1]).transpose(1, 2, 0, 3).reshape(t.shape)


def _scan_in_place(hr_ref, hi_ref, a_r, a_i):
    def local(j, carry):
        tr, ti = _cmul(a_r, a_i, carry[0], carry[1])
        nr = tr + hr_ref[_tile(j), :]
        ni = ti + hi_ref[_tile(j), :]
        hr_ref[_tile(j), :] = nr
        hi_ref[_tile(j), :] = ni
        return nr, ni

    zero = jnp.zeros_like(a_r)
    last_r, last_i = lax.fori_loop(0, SCAN_STEPS, local, (zero, zero), unroll=4)
    pr, pi = _pow256(a_r, a_i)
    er, ei = _chunk_carries(last_r, last_i, pr, pi, reverse=False)

    def fix(j, carry):
        tr, ti = _cmul(carry[0], carry[1], er, ei)
        hr_ref[_tile(j), :] += tr
        hi_ref[_tile(j), :] += ti
        return _cmul(carry[0], carry[1], a_r, a_i)

    lax.fori_loop(0, SCAN_STEPS, fix, (a_r, a_i), unroll=4)
    return er, ei


def _reverse_scan_in_place(lr_ref, li_ref, hr_ref, hi_ref, er, ei, a_r, a_i):
    def local(t, carry):
        j = SCAN_STEPS - 1 - t
        tr, ti = _cmul(a_r, a_i, carry[0], carry[1])
        nr = tr + lr_ref[_tile(j), :]
        ni = ti + li_ref[_tile(j), :]
        lr_ref[_tile(j), :] = nr
        li_ref[_tile(j), :] = ni
        return nr, ni

    zero = jnp.zeros_like(a_r)
    first_r, first_i = lax.fori_loop(0, SCAN_STEPS, local, (zero, zero), unroll=4)
    pr, pi = _pow256(a_r, a_i)
    nxt_r, nxt_i = _chunk_carries(first_r, first_i, pr, pi, reverse=True)

    def accumulate(lam_r, lam_i, hp_r, hp_i, acc):
        return (acc[0] + lam_r * hp_r + lam_i * hp_i, acc[1] + lam_i * hp_r - lam_r * hp_i)

    def fix(t, carry):
        qr, qi, acc_r, acc_i = carry
        j = SCAN_STEPS - 1 - t
        tr, ti = _cmul(qr, qi, nxt_r, nxt_i)
        lam_r = lr_ref[_tile(j), :] + tr
        lam_i = li_ref[_tile(j), :] + ti
        lr_ref[_tile(j), :] = lam_r
        li_ref[_tile(j), :] = lam_i
        acc_r, acc_i = accumulate(lam_r, lam_i, hr_ref[_tile(j - 1), :], hi_ref[_tile(j - 1), :], (acc_r, acc_i))
        qr, qi = _cmul(qr, qi, a_r, a_i)
        return qr, qi, acc_r, acc_i

    qr, qi, acc_r, acc_i = lax.fori_loop(0, SCAN_STEPS - 1, fix, (a_r, a_i, zero, zero), unroll=4)
    tr, ti = _cmul(qr, qi, nxt_r, nxt_i)
    lam_r = lr_ref[_tile(0), :] + tr
    lam_i = li_ref[_tile(0), :] + ti
    lr_ref[_tile(0), :] = lam_r
    li_ref[_tile(0), :] = lam_i
    acc_r, acc_i = accumulate(lam_r, lam_i, er, ei, (acc_r, acc_i))
    return jnp.sum(acc_r, axis=0, keepdims=True), jnp.sum(acc_i, axis=0, keepdims=True)


def _rope_tables():
    half = HEAD_DIM // 2
    inv_freq = ROPE_THETA ** (-jnp.arange(half, dtype=F32) / half)
    ang = jnp.arange(SEQ, dtype=F32)[:, None] * inv_freq[None, :]
    cos, sin = jnp.cos(ang), jnp.sin(ang)
    cos_f = jnp.concatenate([cos, cos, cos, cos], axis=1)
    sin_s = jnp.concatenate([-sin, sin, -sin, sin], axis=1)
    return cos_f, sin_s


def _ssm_discretise(a_re, a_im, log_dt, b_re, b_im):
    lam = lax.complex(a_re, a_im)
    dt = jnp.exp(log_dt)[:, None]
    a_bar = jnp.exp(lam * dt)
    b_bar = ((a_bar - 1.0) / lam)[..., None] * lax.complex(b_re, b_im)
    return a_bar.real, a_bar.imag, b_bar.real, b_bar.imag


SSM_SLABS = 4
SLAB_GROUPS = SSM_GROUPS // SSM_SLABS
SLAB_IN = SSM_WIDTH // SSM_SLABS
SLAB_STATE = SSM_LANES // SSM_SLABS


def _slab_block_diag(blocks):
    _, r, c = blocks.shape
    eye = jnp.eye(SLAB_GROUPS, dtype=blocks.dtype)
    b5 = blocks.reshape(SSM_SLABS, SLAB_GROUPS, r, 1, c) * eye[None, :, None, :, None]
    return b5.reshape(SSM_SLABS, SLAB_GROUPS * r, SLAB_GROUPS * c)


def _diag_blocks(a, b):
    ra, cb = a.shape[1], b.shape[1]
    wa, wb = ra // SLAB_GROUPS, cb // SLAB_GROUPS
    d = lax.dot_general(a, b, (((0,), (0,)), ((), ())), preferred_element_type=F32)
    row_g = jnp.right_shift(lax.broadcasted_iota(jnp.int32, (ra, cb), 0), wa.bit_length() - 1)
    col_g = jnp.right_shift(lax.broadcasted_iota(jnp.int32, (ra, cb), 1), wb.bit_length() - 1)
    d = jnp.where(row_g == col_g, d, 0.0)
    fold = (jnp.bitwise_and(lax.broadcasted_iota(jnp.int32, (cb, wb), 0), wb - 1)
            == lax.broadcasted_iota(jnp.int32, (cb, wb), 1)).astype(F32)
    return jnp.dot(d, fold, preferred_element_type=F32, precision=lax.Precision.HIGHEST)


def _slab_specs():
    tok = pl.BlockSpec((SEQ, SLAB_IN), lambda j: (0, j))
    state = pl.BlockSpec((SEQ, SLAB_STATE), lambda j: (0, j))
    b_in = pl.BlockSpec((None, SLAB_IN, SLAB_STATE), lambda j: (j, 0, 0))
    c_out = pl.BlockSpec((None, SLAB_STATE, SLAB_IN), lambda j: (j, 0, 0))
    vec = pl.BlockSpec((1, SLAB_STATE), lambda j: (0, j))
    ent = pl.BlockSpec((SCAN_CHUNKS, SLAB_STATE), lambda j: (0, j))
    return tok, state, b_in, c_out, vec, ent


def _ssm_forward(u, b_in_r, b_in_i, c_out_r, c_out_ni, a_r, a_i):
    def body(u_ref, br_ref, bi_ref, cr_ref, ci_ref, ar_ref, ai_ref, y_ref, hr_ref, hi_ref, er_ref, ei_ref):
        uu = u_ref[...]
        hr_ref[...] = jnp.dot(uu, br_ref[...], preferred_element_type=F32)
        hi_ref[...] = jnp.dot(uu, bi_ref[...], preferred_element_type=F32)
        a_re = jnp.broadcast_to(ar_ref[...], (SCAN_CHUNKS, SLAB_STATE))
        a_im = jnp.broadcast_to(ai_ref[...], (SCAN_CHUNKS, SLAB_STATE))
        er_ref[...], ei_ref[...] = _scan_in_place(hr_ref, hi_ref, a_re, a_im)
        y_ref[...] = (jnp.dot(hr_ref[...].astype(BF16), cr_ref[...], preferred_element_type=F32)
                      + jnp.dot(hi_ref[...].astype(BF16), ci_ref[...], preferred_element_type=F32))

    tok, state, b_in, c_out, vec, ent = _slab_specs()
    return pl.pallas_call(
        body, grid=(SSM_SLABS,), in_specs=[tok, b_in, b_in, c_out, c_out, vec, vec],
        out_specs=[tok, state, state, ent, ent],
        out_shape=[_sds((SEQ, SSM_WIDTH)), _sds((SEQ, SSM_LANES)), _sds((SEQ, SSM_LANES)),
                   _sds((SCAN_CHUNKS, SSM_LANES)), _sds((SCAN_CHUNKS, SSM_LANES))],
        compiler_params=_cp(("parallel",)), name="ssm_forward")(u, b_in_r, b_in_i, c_out_r, c_out_ni, a_r, a_i)


def _ssm_backward(d_y, d_u_skip, u, h_r, h_i, e_r, e_i, b_in_r, b_in_i, c_out_r, c_out_ni, a_r, a_i):
    def body(dy_ref, skip_ref, u_ref, hr_ref, hi_ref, er_ref, ei_ref, br_ref, bi_ref, cr_ref, ci_ref, ar_ref, ai_ref,
             du_ref, dar_ref, dai_ref, dcr_ref, dci_ref, dbr_ref, dbi_ref, lr_ref, li_ref):
        dy = dy_ref[...]
        lr_ref[...] = _dot_nt(dy, cr_ref[...])
        li_ref[...] = _dot_nt(dy, ci_ref[...])
        a_re = jnp.broadcast_to(ar_ref[...], (SCAN_CHUNKS, SLAB_STATE))
        a_im = -jnp.broadcast_to(ai_ref[...], (SCAN_CHUNKS, SLAB_STATE))
        dar_ref[...], dai_ref[...] = _reverse_scan_in_place(lr_ref, li_ref, hr_ref, hi_ref, er_ref[...], ei_ref[...],
                                                            a_re, a_im)
        dcr_ref[...] = _diag_blocks(dy, hr_ref[...].astype(BF16))
        dci_ref[...] = _diag_blocks(dy, hi_ref[...].astype(BF16))
        lam_r, lam_i = lr_ref[...].astype(BF16), li_ref[...].astype(BF16)
        uu = u_ref[...]
        dbr_ref[...] = _diag_blocks(uu, lam_r)
        dbi_ref[...] = _diag_blocks(uu, lam_i)
        du = skip_ref[...] + _dot_nt(lam_r, br_ref[...]) + _dot_nt(lam_i, bi_ref[...])
        du_ref[...] = du.astype(BF16)

    tok, state, b_in, c_out, vec, ent = _slab_specs()
    db = pl.BlockSpec((SLAB_IN, SSM_STATE), lambda j: (j, 0))
    return pl.pallas_call(
        body, grid=(SSM_SLABS,), in_specs=[tok, tok, tok, state, state, ent, ent, b_in, b_in, c_out, c_out, vec, vec],
        out_specs=[tok, vec, vec, db, db, db, db],
        out_shape=[_sds((SEQ, SSM_WIDTH), BF16), _sds((1, SSM_LANES)), _sds((1, SSM_LANES))]
        + [_sds((SSM_WIDTH, SSM_STATE))] * 4,
        scratch_shapes=[pltpu.VMEM((SEQ, SLAB_STATE), F32)] * 2,
        compiler_params=_cp(("parallel",)), name="ssm_backward")(
            d_y, d_u_skip, u, h_r, h_i, e_r, e_i, b_in_r, b_in_i, c_out_r, c_out_ni, a_r, a_i)


FF_ROWS = 1024
FF_SHARD = D_FF // N_CHIPS


def _dot_nt(a, b):
    return lax.dot_general(a, b, (((1,), (1,)), ((), ())), preferred_element_type=F32)


def _ffn_up(h, w_gate_t, w_up_t):
    def body(h_ref, wg_ref, wu_ref, a_ref, b_ref, act_ref):
        hb = h_ref[...].astype(BF16)
        a = _dot_nt(hb, wg_ref[...])
        b = _dot_nt(hb, wu_ref[...])
        a_ref[...] = a
        b_ref[...] = b
        act_ref[...] = (a * jax.nn.sigmoid(a) * b).astype(BF16)

    w_spec = pl.BlockSpec((None, FF_SHARD, D_MODEL), lambda i, k: (k, 0, 0))
    o_spec = pl.BlockSpec((None, FF_ROWS, FF_SHARD), lambda i, k: (k, i, 0))
    shape = (N_CHIPS, SEQ, FF_SHARD)
    return pl.pallas_call(
        body, grid=(SEQ // FF_ROWS, N_CHIPS),
        in_specs=[pl.BlockSpec((FF_ROWS, D_MODEL), lambda i, k: (i, 0)), w_spec, w_spec],
        out_specs=[o_spec, o_spec, o_spec], out_shape=[_sds(shape), _sds(shape), _sds(shape, BF16)],
        compiler_params=_cp(("parallel", "parallel")), name="ffn_up")(h, w_gate_t, w_up_t)


def _ffn_down_ln2_loss(act, w_down, h, tgt, ln_g, ln_b):
    def body(act_ref, w_ref, h_ref, tgt_ref, g_ref, b_ref, dz_ref, loss_ref, dg_ref, db_ref, acc):
        i, k = pl.program_id(0), pl.program_id(1)
        part = jnp.dot(act_ref[...], w_ref[...], preferred_element_type=F32)

        @pl.when(k == 0)
        def _():
            acc[...] = part

        @pl.when(k > 0)
        def _():
            acc[...] += part

        @pl.when(k == N_CHIPS - 1)
        def _():
            g = g_ref[...]
            xhat, rstd = _ln_stats(DN_ALPHA * h_ref[...] + acc[...])
            err = xhat * g + b_ref[...] - tgt_ref[...]
            d_out = err * (1.0 / D_MODEL)
            dz_ref[...] = _ln_bwd(d_out, xhat, rstd, g)
            loss_rows = jnp.sum(err * err, axis=-1, keepdims=True) * (0.5 / D_MODEL)
            sums = (jnp.broadcast_to(jnp.sum(loss_rows, axis=0, keepdims=True), loss_ref.shape),
                    _colsum(d_out * xhat), _colsum(d_out))
            for ref, val in zip((loss_ref, dg_ref, db_ref), sums):
                @pl.when(i == 0)
                def _(ref=ref, val=val):
                    ref[...] = val

                @pl.when(i > 0)
                def _(ref=ref, val=val):
                    ref[...] += val

    row = pl.BlockSpec((FF_ROWS, D_MODEL), lambda i, k: (i, 0))
    vec = pl.BlockSpec((1, D_MODEL), lambda i, k: (0, 0))
    return pl.pallas_call(
        body, grid=(SEQ // FF_ROWS, N_CHIPS),
        in_specs=[pl.BlockSpec((None, FF_ROWS, FF_SHARD), lambda i, k: (k, i, 0)),
                  pl.BlockSpec((None, FF_SHARD, D_MODEL), lambda i, k: (k, 0, 0)), row, row, vec, vec],
        out_specs=[row, pl.BlockSpec((1, BLOCK), lambda i, k: (0, 0)), vec, vec],
        out_shape=[_sds((SEQ, D_MODEL)), _sds((1, BLOCK)), _sds((1, D_MODEL)), _sds((1, D_MODEL))],
        scratch_shapes=[pltpu.VMEM((FF_ROWS, D_MODEL), F32)],
        compiler_params=_cp(("arbitrary", "arbitrary")), name="ffn_down_ln2_loss")(act, w_down, h, tgt, ln_g, ln_b)


def _ffn_down_bwd(dz, w_down, a, b):
    def body(dz_ref, wd_ref, a_ref, b_ref, da_ref, db_ref):
        d_act = _dot_nt(dz_ref[...].astype(BF16), wd_ref[...])
        av = a_ref[...]
        sg = jax.nn.sigmoid(av)
        da_ref[...] = (d_act * b_ref[...] * sg * (1.0 + av * (1.0 - sg))).astype(BF16)
        db_ref[...] = (d_act * av * sg).astype(BF16)

    t_spec = pl.BlockSpec((None, FF_ROWS, FF_SHARD), lambda i, k: (k, i, 0))
    shape = (N_CHIPS, SEQ, FF_SHARD)
    return pl.pallas_call(
        body, grid=(SEQ // FF_ROWS, N_CHIPS),
        in_specs=[pl.BlockSpec((FF_ROWS, D_MODEL), lambda i, k: (i, 0)),
                  pl.BlockSpec((None, FF_SHARD, D_MODEL), lambda i, k: (k, 0, 0)), t_spec, t_spec],
        out_specs=[t_spec, t_spec], out_shape=[_sds(shape, BF16), _sds(shape, BF16)],
        compiler_params=_cp(("parallel", "parallel")), name="ffn_down_bwd")(dz, w_down, a, b)


def _ffn_dh(d_a, d_b, w_gate_t, w_up_t):
    def body(da_ref, db_ref, wg_ref, wu_ref, o_ref, acc):
        k = pl.program_id(1)
        part = (jnp.dot(da_ref[...], wg_ref[...], preferred_element_type=F32)
                + jnp.dot(db_ref[...], wu_ref[...], preferred_element_type=F32))

        @pl.when(k == 0)
        def _():
            acc[...] = part

        @pl.when(k > 0)
        def _():
            acc[...] += part

        @pl.when(k == N_CHIPS - 1)
        def _():
            o_ref[...] = acc[...]

    t_spec = pl.BlockSpec((None, FF_ROWS, FF_SHARD), lambda i, k: (k, i, 0))
    w_spec = pl.BlockSpec((None, FF_SHARD, D_MODEL), lambda i, k: (k, 0, 0))
    return pl.pallas_call(
        body, grid=(SEQ // FF_ROWS, N_CHIPS), in_specs=[t_spec, t_spec, w_spec, w_spec],
        out_specs=pl.BlockSpec((FF_ROWS, D_MODEL), lambda i, k: (i, 0)), out_shape=_sds((SEQ, D_MODEL)),
        scratch_shapes=[pltpu.VMEM((FF_ROWS, D_MODEL), F32)],
        compiler_params=_cp(("parallel", "arbitrary")), name="ffn_dh")(d_a, d_b, w_gate_t, w_up_t)


def _local_step(x, tgt, wts, small, ffn_grads, mixer_grads, small_grads):
    s = SEQ
    cos_f, sin_s = [_to_phase_rows(t) for t in _rope_tables()]
    x = _reorder_rows(x, to_phase=True, name="phase_rows_x")
    tgt = _reorder_rows(tgt, to_phase=True, name="phase_rows_target")

    chip, w_own, w_others = wts["w_in_parts"]
    proj_own = _project_in_own(chip, x, w_own, cos_f, sin_s)
    proj = _project_in(chip, x, w_others, cos_f, sin_s, proj_own)

    attn, lse = _attention_fwd(proj)

    (abar_r, abar_i, bbar_r, bbar_i), ssm_vjp = jax.vjp(
        _ssm_discretise, small["ssm_a_re"], small["ssm_a_im"], small["ssm_log_dt"], small["ssm_b_re"], small["ssm_b_im"])
    b_in_r, b_in_i = [_slab_block_diag(b.transpose(0, 2, 1)).astype(BF16) for b in (bbar_r, bbar_i)]
    c_out_r = _slab_block_diag(small["ssm_c_re"].transpose(0, 2, 1)).astype(BF16)
    c_out_ni = _slab_block_diag(-small["ssm_c_im"].transpose(0, 2, 1)).astype(BF16)
    a_r, a_i = abar_r.reshape(1, SSM_LANES), abar_i.reshape(1, SSM_LANES)
    d_skip = small["ssm_d"].reshape(1, SSM_WIDTH)

    u_f = _to_scan_rows(proj[:, 3 * QKV_WIDTH:3 * QKV_WIDTH + SSM_WIDTH])
    u_p = u_f.astype(BF16)
    y_c, h_r, h_i, e_r, e_i = _ssm_forward(u_p, b_in_r, b_in_i, c_out_r, c_out_ni, a_r, a_i)

    def branch(t, wg):
        return jnp.concatenate([jnp.dot(t, wg[k], preferred_element_type=F32) for k in range(N_CHIPS)], axis=1)

    def branch_t(t, wg):
        ns = wg.shape[2]
        return sum(_dot_nt(t[:, k * ns:(k + 1) * ns], wg[k]) for k in range(N_CHIPS))

    def gelu_glu(yc, u, dsk, wg):
        y = yc + dsk * u
        gel = (0.5 * y * (1.0 + jnp.tanh(GELU_C * (y + GELU_K * y * y * y)))).astype(BF16)
        glu = branch(gel, wg)
        return y, gel, glu, glu[:, :SSM_WIDTH] * jax.nn.sigmoid(glu[:, SSM_WIDTH:])

    y_s5, gel, glu, y_glu = _rowwise(
        gelu_glu, [y_c, u_f], [d_skip, wts["w_glu"]],
        [_sds((s, SSM_WIDTH)), _sds((s, SSM_WIDTH), BF16), _sds((s, 2 * SSM_WIDTH)), _sds((s, SSM_WIDTH), BF16)],
        tm=512, name="ssm_gelu_glu")
    y_glu = _from_scan_rows(y_glu)

    gl0 = (proj, D_MODEL, (3 * QKV_WIDTH + SSM_WIDTH) // D_MODEL)
    gl1 = (proj, D_MODEL, (3 * QKV_WIDTH + SSM_WIDTH) // D_MODEL + 1)
    b_gate = small["b_gate"]
    w_out = wts["w_out"].reshape(D_MODEL, D_MODEL)

    def mix_ln1(l0, l1, at, yg, xv, bg, wa, ws, wo, g, b):
        ya = branch(at.astype(BF16), wa)
        ys = branch(yg, ws)
        mixed = (jax.nn.sigmoid(l0 + bg[0:1]) * ya + jax.nn.sigmoid(l1 + bg[1:2]) * ys).astype(BF16)
        z = DN_ALPHA * xv + jnp.dot(mixed, wo, preferred_element_type=F32)
        xhat, _ = _ln_stats(z)
        return ya, ys, mixed, z, xhat * g + b

    y_attn, y_ssm, mixed, z1, h = _rowwise(
        mix_ln1, [gl0, gl1, attn, y_glu, x],
        [b_gate, wts["w_attn_br"], wts["w_ssm_br"], w_out, small["ln1_g"], small["ln1_b"]],
        [_sds((s, D_MODEL)), _sds((s, D_MODEL)), _sds((s, D_MODEL), BF16), _sds((s, D_MODEL)), _sds((s, D_MODEL))],
        tm=256, name="mix_ln1")

    nf = D_FF // N_CHIPS
    w_gate_t, w_up_t, w_down = wts["w_ff_gate"], wts["w_ff_up"], wts["w_ff_down"]
    ff_a, ff_b, act = _ffn_up(h, w_gate_t, w_up_t)
    dz2, loss_v, d_ln2_g, d_ln2_b = _ffn_down_ln2_loss(act, w_down, h, tgt, small["ln2_g"], small["ln2_b"])

    d_a, d_b = _ffn_down_bwd(dz2, w_down, ff_a, ff_b)

    def grad_rows(lhs, rhs, name):
        return _matmul(lhs, rhs, grid=(N_CHIPS,), a_spec=pl.BlockSpec((None, s, nf), lambda k: (k, 0, 0)),
                       b_spec=pl.BlockSpec((s, D_MODEL), lambda k: (0, 0)),
                       o_spec=pl.BlockSpec((None, nf, D_MODEL), lambda k: (k, 0, 0)),
                       out_shape=_sds((N_CHIPS, nf, D_MODEL), BF16), dims=(0, 0), name=name)

    g_w_ff_down = grad_rows(act, dz2, "g_w_ff_down")
    g_w_ff_gate = grad_rows(d_a, h, "g_w_ff_gate")
    g_w_ff_up = grad_rows(d_b, h, "g_w_ff_up")
    dh_ff = _ffn_dh(d_a, d_b, w_gate_t, w_up_t)

    def ln1_gate_bwd(dz, dff, z, l0, l1, ya, ys, g, bg, wo, wa, ws):
        xhat, rstd = _ln_stats(z)
        dh = DN_ALPHA * dz + dff
        dz_in = _ln_bwd(dh, xhat, rstd, g)
        dm = _dot_nt(dz_in.astype(BF16), wo)
        g0 = jax.nn.sigmoid(l0 + bg[0:1])
        g1 = jax.nn.sigmoid(l1 + bg[1:2])
        dl0 = dm * ya * g0 * (1.0 - g0)
        dl1 = dm * ys * g1 * (1.0 - g1)
        dya, dys = (dm * g0).astype(BF16), (dm * g1).astype(BF16)
        return (dz_in, dya, dys, jnp.concatenate([dl0, dl1], axis=1), branch_t(dya, wa), branch_t(dys, ws),
                _colsum(dh * xhat), _colsum(dh), _colsum(dl0), _colsum(dl1))

    dz1, d_y_attn, d_y_ssm, d_gl, d_attn, d_y_glu, d_ln1_g, d_ln1_b, d_bg0, d_bg1 = _rowwise(
        ln1_gate_bwd, [dz2, dh_ff, z1, gl0, gl1, y_attn, y_ssm],
        [small["ln1_g"], b_gate, w_out, wts["w_attn_br"], wts["w_ssm_br"]],
        [_sds((s, D_MODEL)), _sds((s, D_MODEL), BF16), _sds((s, D_MODEL), BF16), _sds((s, 2 * D_MODEL), BF16),
         _sds((s, ATTN_WIDTH)), _sds((s, SSM_WIDTH))],
        [_sds((1, D_MODEL))] * 4, tm=256, name="ln1_gate_bwd", after=(g_w_ff_down, g_w_ff_gate, g_w_ff_up))
    ffn_sent = ffn_grads({"w_ff_down": g_w_ff_down, "w_ff_gate": g_w_ff_gate, "w_ff_up": g_w_ff_up}, dz1)
    g_w_out = _mm_rows_tn(mixed, dz1, name="g_w_out")

    g_w_ssm_br = _mm_cols_tn(y_glu, d_y_ssm, ns=D_MODEL // N_CHIPS, name="g_w_ssm_br")
    d_y_glu = _to_scan_rows(d_y_glu)

    def glu_gelu_bwd(dyg, gl, y, u, dsk, wg):
        ga, gb = gl[:, :SSM_WIDTH], gl[:, SSM_WIDTH:]
        sg = jax.nn.sigmoid(gb)
        d_gl = jnp.concatenate([dyg * sg, dyg * ga * sg * (1.0 - sg)], axis=1).astype(BF16)
        dg = branch_t(d_gl, wg)
        th = jnp.tanh(GELU_C * (y + GELU_K * y * y * y))
        dy = dg * (0.5 * (1.0 + th) + 0.5 * y * (1.0 - th * th) * GELU_C * (1.0 + 3.0 * GELU_K * y * y))
        return d_gl, dy, dy * dsk, _colsum(dy * u)

    d_glu, d_y, d_u_skip, d_ssm_d = _rowwise(
        glu_gelu_bwd, [d_y_glu, glu, y_s5, u_f], [d_skip, wts["w_glu"]],
        [_sds((s, 2 * SSM_WIDTH), BF16), _sds((s, SSM_WIDTH), BF16), _sds((s, SSM_WIDTH))], [_sds((1, SSM_WIDTH))],
        tm=512, name="glu_gelu_bwd", after=tuple(ffn_sent))
    g_w_glu = _mm_cols_tn(gel, d_glu, ns=2 * SSM_WIDTH // N_CHIPS, name="g_w_glu")
    d_u, d_abar_r, d_abar_i, d_c_r, d_c_ni, d_bin_r, d_bin_i = _ssm_backward(
        d_y, d_u_skip, u_p, h_r, h_i, e_r, e_i, b_in_r, b_in_i, c_out_r, c_out_ni, a_r, a_i)
    d_u = _from_scan_rows(d_u)
    d_bbar_r = d_bin_r.reshape(SSM_GROUPS, SSM_GROUP, SSM_STATE).transpose(0, 2, 1)
    d_bbar_i = d_bin_i.reshape(SSM_GROUPS, SSM_GROUP, SSM_STATE).transpose(0, 2, 1)
    d_a_re, d_a_im, d_log_dt, d_b_re, d_b_im = ssm_vjp(
        (d_abar_r.reshape(SSM_GROUPS, SSM_STATE), d_abar_i.reshape(SSM_GROUPS, SSM_STATE), d_bbar_r, d_bbar_i))
    d_c_re = d_c_r.reshape(SSM_GROUPS, SSM_GROUP, SSM_STATE)
    d_c_im = -d_c_ni.reshape(SSM_GROUPS, SSM_GROUP, SSM_STATE)

    g_w_attn_br = _mm_cols_tn(attn, d_y_attn, ns=D_MODEL // N_CHIPS, name="g_w_attn_br")
    mixer_grads({"w_out": g_w_out, "w_ssm_br": g_w_ssm_br, "w_glu": g_w_glu, "w_attn_br": g_w_attn_br}, d_abar_r)
    small_g = {"b_gate": jnp.concatenate([d_bg0, d_bg1], axis=0), "ssm_a_re": d_a_re, "ssm_a_im": d_a_im,
               "ssm_log_dt": d_log_dt, "ssm_b_re": d_b_re, "ssm_b_im": d_b_im, "ssm_c_re": d_c_re, "ssm_c_im": d_c_im,
               "ssm_d": d_ssm_d.reshape(SSM_WIDTH), "ln1_g": d_ln1_g, "ln1_b": d_ln1_b, "ln2_g": d_ln2_g,
               "ln2_b": d_ln2_b}
    shared = small_grads(small_g, loss_v[0, 0])
    d_proj = _attention_bwd(proj, cos_f, sin_s, d_attn, attn, lse, d_u, d_gl)

    g_w_in = _mm_cols_tn(x, d_proj, ns=IN_WIDTH // N_CHIPS, name="g_w_in", after=tuple(shared))

    def grad_x_after(after):
        dx_proj = _mm_cols_nt(d_proj, wts["w_in"], tm=1024, name="dx_proj", after=after)
        return _reorder_rows(dz1, dx_proj, to_phase=False, name="grad_x", scale=DN_ALPHA)

    return grad_x_after, g_w_in, d_proj


GATHER_ID, SWAP_ID, SCATTER_ID, JOIN_ID, EXCHANGE_ID = 1, 2, 3, 4, 5


def _place():
    return lax.axis_index("x"), lax.axis_index("y"), lax.axis_index("c")


def _other_chips(x, y):
    return [(1 - x, y), (x, 1 - y), (1 - x, 1 - y)]


def _handshake(peers):
    barrier = pltpu.get_barrier_semaphore()
    for peer in peers:
        pl.semaphore_signal(barrier, inc=1, device_id=peer, device_id_type=MESH)
    pl.semaphore_wait(barrier, len(peers))


def _sequencer(body, arrays, out_type, sems, collective_id, name):
    return pl.kernel(body, name=name, out_type=out_type,
                     mesh=plsc.ScalarSubcoreMesh(axis_name="sequencer", num_cores=1), scratch_types=sems,
                     compiler_params=pltpu.CompilerParams(collective_id=collective_id))(*arrays)


def _gather_weights(shards, *, name, own_slot=True):
    nw = len(shards)

    def body(*refs):
        ins, outs = refs[:nw], refs[nw:2 * nw]
        send_sems, recv_sems, pass_send, pass_recv, local_sems = refs[2 * nw:]
        x, y, c = _place()
        chip = 2 * x + y
        chips = _other_chips(x, y)
        _handshake([(x, y, 1 - c)] + [(cx, cy, c) for cx, cy in chips])
        started, local = [], []
        for w in range(nw):
            hw = shards[w].shape[0] // 2
            mine = pl.ds(c * hw, hw)
            if own_slot:
                own = pltpu.make_async_copy(ins[w], outs[w].at[chip], local_sems.at[w])
                own.start()
                local.append(own)
            for j, (cx, cy) in enumerate(chips):
                cp = pltpu.make_async_remote_copy(
                    src_ref=ins[w].at[mine], dst_ref=outs[w].at[chip, mine], send_sem=send_sems.at[w, j],
                    recv_sem=recv_sems.at[w, j], device_id=(cx, cy, c), device_id_type=MESH)
                cp.start()
                started.append(cp)
        passed = []
        for w in range(nw):
            hw = shards[w].shape[0] // 2
            mine = pl.ds(c * hw, hw)
            for j, (cx, cy) in enumerate(chips):
                landed = outs[w].at[2 * cx + cy, mine]
                pltpu.make_async_remote_copy(
                    src_ref=ins[w].at[mine], dst_ref=landed, send_sem=send_sems.at[w, j],
                    recv_sem=recv_sems.at[w, j], device_id=(cx, cy, c), device_id_type=MESH).wait_recv()
                cp = pltpu.make_async_remote_copy(
                    src_ref=landed, dst_ref=landed, send_sem=pass_send.at[w, j], recv_sem=pass_recv.at[w, j],
                    device_id=(x, y, 1 - c), device_id_type=MESH)
                cp.start()
                passed.append(cp)
        for w in range(nw):
            hw = shards[w].shape[0] // 2
            theirs = pl.ds((1 - c) * hw, hw)
            for j, (cx, cy) in enumerate(chips):
                landed = outs[w].at[2 * cx + cy, theirs]
                pltpu.make_async_remote_copy(
                    src_ref=landed, dst_ref=landed, send_sem=pass_send.at[w, j], recv_sem=pass_recv.at[w, j],
                    device_id=(x, y, 1 - c), device_id_type=MESH).wait_recv()
        for cp in local:
            cp.wait()
        for cp in started + passed:
            cp.wait_send()

    sem = pltpu.SemaphoreType.DMA
    return _sequencer(body, shards, [_sds((N_CHIPS,) + a.shape, a.dtype) for a in shards],
                      [sem((nw, 3)), sem((nw, 3)), sem((nw, 3)), sem((nw, 3)), sem((nw,))], GATHER_ID, name)


def _swap_other_halves(grads, *, name):
    nw = len(grads)

    def body(*refs):
        ins, outs = refs[:nw], refs[nw:2 * nw]
        send_sems, recv_sems = refs[2 * nw:]
        x, y, c = _place()
        _handshake([(x, y, 1 - c)])
        cps = []
        for w in range(nw):
            hw = grads[w].shape[1] // 2
            cp = pltpu.make_async_remote_copy(
                src_ref=ins[w].at[:, pl.ds((1 - c) * hw, hw)], dst_ref=outs[w], send_sem=send_sems.at[w],
                recv_sem=recv_sems.at[w], device_id=(x, y, 1 - c), device_id_type=MESH)
            cp.start()
            cps.append(cp)
        for cp in cps:
            cp.wait()

    sem = pltpu.SemaphoreType.DMA
    return _sequencer(body, grads, [_sds((N_CHIPS, g.shape[1] // 2, g.shape[2]), g.dtype) for g in grads],
                      [sem((nw,)), sem((nw,))], SWAP_ID, name)


def _add_my_halves(core, grads, others, *, name, after=()):
    nw = len(grads)
    halves = [g.shape[1] // 2 for g in grads]

    def body(core_ref, *refs):
        outs = refs[2 * nw + len(after):]
        for g_ref, o_ref, out_ref in zip(refs[:nw], refs[nw:2 * nw], outs):
            out_ref[...] = (g_ref[...].astype(F32) + o_ref[...].astype(F32)).astype(out_ref.dtype)

    in_specs = [pl.BlockSpec((None, None, hw, g.shape[2]), lambda s, core_ref: (s, core_ref[0], 0, 0))
                for g, hw in zip(grads, halves)]
    in_specs += [pl.BlockSpec((None, hw, g.shape[2]), lambda s, core_ref: (s, 0, 0)) for g, hw in zip(grads, halves)]
    return pl.pallas_call(
        body,
        grid_spec=pltpu.PrefetchScalarGridSpec(
            num_scalar_prefetch=1, grid=(N_CHIPS,), in_specs=in_specs + [HBM_OPERAND] * len(after),
            out_specs=[pl.BlockSpec((None, hw, g.shape[2]), lambda s, core_ref: (s, 0, 0))
                       for g, hw in zip(grads, halves)]),
        out_shape=[_sds((N_CHIPS, hw, g.shape[2]), BF16) for g, hw in zip(grads, halves)],
        compiler_params=_cp(("parallel",)), name=name)(
            core, *[g.reshape(N_CHIPS, 2, hw, g.shape[2]) for g, hw in zip(grads, halves)], *others, *after)


def _scatter_partials(parts, *, name):
    nw = len(parts)

    def body(*refs):
        ins, outs = refs[:nw], refs[nw:2 * nw]
        send_sems, recv_sems = refs[2 * nw:]
        x, y, c = _place()
        _handshake([(cx, cy, c) for cx, cy in _other_chips(x, y)])
        cps = []
        for w in range(nw):
            for j, (cx, cy) in enumerate(_other_chips(x, y)):
                cp = pltpu.make_async_remote_copy(
                    src_ref=ins[w].at[2 * cx + cy], dst_ref=outs[w].at[j], send_sem=send_sems.at[w, j],
                    recv_sem=recv_sems.at[w, j], device_id=(cx, cy, c), device_id_type=MESH)
                cp.start()
                cps.append(cp)
        for cp in cps:
            cp.wait()

    sem = pltpu.SemaphoreType.DMA
    return _sequencer(body, parts, [_sds((3,) + p.shape[1:], p.dtype) for p in parts],
                      [sem((nw, 3)), sem((nw, 3))], SCATTER_ID, name)


SUM_STEPS = 2


def _sum_partials(chip, parts, recvd, *, name, after=()):
    nw = len(parts)
    rows = [p.shape[1] // SUM_STEPS for p in parts]

    def body(chip_ref, *refs):
        outs = refs[2 * nw + len(after):]
        for p_ref, r_ref, out_ref in zip(refs[:nw], refs[nw:2 * nw], outs):
            acc = p_ref[...].astype(F32)
            for j in range(3):
                acc = acc + r_ref[j].astype(F32)
            out_ref[...] = acc

    in_specs = [pl.BlockSpec((None, th, p.shape[2]), lambda i, chip_ref: (chip_ref[0], i, 0))
                for p, th in zip(parts, rows)]
    in_specs += [pl.BlockSpec((3, th, p.shape[2]), lambda i, chip_ref: (0, i, 0)) for p, th in zip(parts, rows)]
    return pl.pallas_call(
        body,
        grid_spec=pltpu.PrefetchScalarGridSpec(
            num_scalar_prefetch=1, grid=(SUM_STEPS,), in_specs=in_specs + [HBM_OPERAND] * len(after),
            out_specs=[pl.BlockSpec((th, p.shape[2]), lambda i, chip_ref: (i, 0)) for p, th in zip(parts, rows)]),
        out_shape=[_sds(p.shape[1:]) for p in parts], compiler_params=_cp(("parallel",)), name=name)(
            chip, *parts, *recvd, *after)


def _swap_reduced_halves(halves, *, name):
    nw = len(halves)

    def body(*refs):
        ins, outs = refs[:nw], refs[nw:2 * nw]
        send_sems, recv_sems = refs[2 * nw:]
        x, y, c = _place()
        _handshake([(x, y, 1 - c)])
        cps = []
        for w in range(nw):
            cp = pltpu.make_async_remote_copy(
                src_ref=ins[w], dst_ref=outs[w], send_sem=send_sems.at[w], recv_sem=recv_sems.at[w],
                device_id=(x, y, 1 - c), device_id_type=MESH)
            cp.start()
            cps.append(cp)
        for cp in cps:
            cp.wait()

    sem = pltpu.SemaphoreType.DMA
    return _sequencer(body, halves, [_sds(h.shape, h.dtype) for h in halves], [sem((nw,)), sem((nw,))], JOIN_ID, name)


def _exchange_rows(vec, *, name):
    def body(v_ref, slots, send_sems, recv_sems, local_sem):
        x, y, c = _place()
        me = 4 * x + 2 * y + c
        peers = []
        for mask in range(1, N_DEV):
            peers.append((1 - x if mask & 4 else x, 1 - y if mask & 2 else y, 1 - c if mask & 1 else c))
        _handshake(peers)
        own = pltpu.make_async_copy(v_ref, slots.at[me], local_sem)
        own.start()
        cps = []
        for k, peer in enumerate(peers):
            cp = pltpu.make_async_remote_copy(
                src_ref=v_ref, dst_ref=slots.at[me], send_sem=send_sems.at[k], recv_sem=recv_sems.at[k],
                device_id=peer, device_id_type=MESH)
            cp.start()
            cps.append(cp)
        for k, (px, py, pc) in enumerate(peers):
            pltpu.make_async_remote_copy(
                src_ref=v_ref, dst_ref=slots.at[4 * px + 2 * py + pc], send_sem=send_sems.at[k],
                recv_sem=recv_sems.at[k], device_id=(px, py, pc), device_id_type=MESH).wait_recv()
        for cp in cps:
            cp.wait_send()
        own.wait()

    sem = pltpu.SemaphoreType.DMA
    return _sequencer(body, [vec], [_sds((N_DEV,) + vec.shape)], [sem((N_DEV - 1,)), sem((N_DEV - 1,)), sem(())],
                      EXCHANGE_ID, name)[0]


def _sum_slots(slots, *, name, after=()):
    def body(s_ref, *rest):
        out_ref = rest[len(after)]
        acc = s_ref[0]
        for d in range(1, N_DEV):
            acc = acc + s_ref[d]
        out_ref[...] = acc

    vmem = pl.BlockSpec(memory_space=pltpu.VMEM)
    return pl.pallas_call(
        body, in_specs=[vmem] + [HBM_OPERAND] * len(after), out_specs=vmem, out_shape=_sds(slots.shape[1:]),
        compiler_params=pltpu.CompilerParams(vmem_limit_bytes=VMEM_LIMIT_BYTES), name=name)(slots, *after)


def _reduce_scatter_start(grads, core, *, tag, add_after=()):
    others = _swap_other_halves(grads, name="swap_other_halves_" + tag)
    parts = _add_my_halves(core, grads, others, name="add_my_halves_" + tag, after=add_after)
    return parts, _scatter_partials(parts, name="scatter_partials_" + tag)


def _reduce_scatter_finish(parts, recvd, chip, *, tag, sum_after=()):
    mine = _sum_partials(chip, parts, recvd, name="sum_partials_" + tag, after=sum_after)
    return mine, _swap_reduced_halves(mine, name="swap_reduced_halves_" + tag)


ADAM_BLOCK_ELEMS = 256 * 1024


def _adam_rows(rows, cols):
    tm = rows
    while tm * cols > ADAM_BLOCK_ELEMS and tm % 16 == 0:
        tm //= 2
    return tm


def _adam_step(wv, gv, mv, vv):
    m2 = ADAM_B1 * mv + (1.0 - ADAM_B1) * gv
    v2 = ADAM_B2 * vv + (1.0 - ADAM_B2) * (gv * gv)
    m_hat = m2 / (1.0 - ADAM_B1 ** ADAM_STEP)
    v_hat = v2 / (1.0 - ADAM_B2 ** ADAM_STEP)
    return -ADAM_LR * (m_hat / (jnp.sqrt(v_hat) + ADAM_EPS) + ADAM_WD * wv), m2, v2


def _adamw_each(ws, gs, ms, vs, *, name, after=()):
    n = len(ws)
    whole = pl.BlockSpec(memory_space=pltpu.VMEM)

    def body(*refs):
        ins, outs = refs[:4 * n], refs[4 * n + len(after):]
        for i in range(n):
            res = _adam_step(*(ins[k * n + i][...] for k in range(4)))
            for k in range(3):
                outs[k * n + i][...] = res[k]

    out = pl.pallas_call(body, in_specs=[whole] * (4 * n) + [HBM_OPERAND] * len(after),
                         out_shape=[_sds(w.shape) for w in ws] * 3, name=name)(*ws, *gs, *ms, *vs, *after)
    return out[:n], out[n:2 * n], out[2 * n:]


def _adamw_halves(core, w, g_mine, g_theirs, m, v, *, name, after=()):
    rows, cols = w.shape
    hw = rows // 2
    tm = _adam_rows(hw, cols)
    per_half = hw // tm

    def body(core_ref, w_ref, gm_ref, gt_ref, m_ref, v_ref, *rest):
        g_out, d_out, m_out, v_out = rest[len(after):]
        mine = (pl.program_id(0) // per_half) == core_ref[0]
        g = jnp.where(mine, gm_ref[...], gt_ref[...])
        d, m2, v2 = _adam_step(w_ref[...], g, m_ref[...], v_ref[...])
        g_out[...] = g
        d_out[...] = d
        m_out[...] = m2
        v_out[...] = v2

    full = pl.BlockSpec((tm, cols), lambda i, core_ref: (i, 0))

    def half(wanted):
        def index(i, core_ref):
            in_use = ((i // per_half) == core_ref[0]) == wanted
            return (jnp.where(in_use, i % per_half, 0), 0)
        return pl.BlockSpec((tm, cols), index)

    return pl.pallas_call(
        body,
        grid_spec=pltpu.PrefetchScalarGridSpec(
            num_scalar_prefetch=1, grid=(rows // tm,),
            in_specs=[full, half(True), half(False), full, full] + [HBM_OPERAND] * len(after),
            out_specs=[full, full, full, full]),
        out_shape=[_sds((rows, cols))] * 4, compiler_params=_cp(("parallel",)), name=name)(
            core, w, g_mine, g_theirs, m, v, *after)


HELD_TRANSPOSED = ("w_ff_gate", "w_ff_up")


def _as_rows(name, arr):
    return arr[0].T if name in HELD_TRANSPOSED else arr[0]


def _from_rows(name, arr2d):
    return (arr2d.T if name in HELD_TRANSPOSED else arr2d)[None]


STORED_SWAPPED = ("ssm_b_re", "ssm_b_im")


def _as_stored(name, arr):
    return jnp.swapaxes(arr, -1, -2) if name in STORED_SWAPPED else arr


def _pack_rows(arrs):
    flat = jnp.concatenate([a.reshape(-1).astype(F32) for a in arrs])
    rows = -(-flat.shape[0] // 1024) * 8
    return jnp.pad(flat, (0, rows * 128 - flat.shape[0])).reshape(rows, 128)


def _unpack_rows(vec, shapes):
    flat = vec.reshape(-1)
    out, off = [], 0
    for shp in shapes:
        size = math.prod(shp)
        out.append(flat[off:off + size].reshape(shp))
        off += size
    return out


SMALL = ("b_gate", "ssm_a_re", "ssm_a_im", "ssm_log_dt", "ssm_b_re", "ssm_b_im", "ssm_c_re", "ssm_c_im", "ssm_d",
         "ln1_g", "ln1_b", "ln2_g", "ln2_b")
GATHER_GROUPS = (("w_in", ("w_in",)), ("mixer", ("w_attn_br", "w_ssm_br", "w_glu", "w_out")),
                 ("ffn_up", ("w_ff_gate", "w_ff_up")), ("ffn_down", ("w_ff_down",)))
REDUCE_GROUPS = (("ffn", ("w_ff_down", "w_ff_gate", "w_ff_up")),
                 ("mixer", ("w_out", "w_ssm_br", "w_glu", "w_attn_br")), ("w_in", ("w_in",)))
WEIGHTS = ("w_in", "b_gate", "w_attn_br", "w_ssm_br", "w_out", "ssm_a_re", "ssm_a_im", "ssm_log_dt", "ssm_b_re",
           "ssm_b_im", "ssm_c_re", "ssm_c_im", "ssm_d", "w_glu", "ln1_g", "ln1_b", "w_ff_gate", "w_ff_up", "w_ff_down",
           "ln2_g", "ln2_b")


def kernel(x, w_in, b_gate, w_attn_br, w_ssm_br, w_out, ssm_a_re, ssm_a_im, ssm_log_dt, ssm_b_re, ssm_b_im, ssm_c_re, ssm_c_im, ssm_d, w_glu, ln1_g, ln1_b, w_ff_gate, w_ff_up, w_ff_down, ln2_g, ln2_b, loss_target, m_w_in, m_b_gate, m_w_attn_br, m_w_ssm_br, m_w_out, m_ssm_a_re, m_ssm_a_im, m_ssm_log_dt, m_ssm_b_re, m_ssm_b_im, m_ssm_c_re, m_ssm_c_im, m_ssm_d, m_w_glu, m_ln1_g, m_ln1_b, m_w_ff_gate, m_w_ff_up, m_w_ff_down, m_ln2_g, m_ln2_b, v_w_in, v_b_gate, v_w_attn_br, v_w_ssm_br, v_w_out, v_ssm_a_re, v_ssm_a_im, v_ssm_log_dt, v_ssm_b_re, v_ssm_b_im, v_ssm_c_re, v_ssm_c_im, v_ssm_d, v_w_glu, v_ln1_g, v_ln1_b, v_w_ff_gate, v_w_ff_up, v_w_ff_down, v_ln2_g, v_ln2_b):
    given = dict(locals())
    px, py, pc = _place()
    chip = 2 * px + py
    core_s = jnp.reshape(pc, (1,)).astype(jnp.int32)
    chip_s = jnp.reshape(chip, (1,)).astype(jnp.int32)

    wts = {}
    for tag, names in GATHER_GROUPS:
        shards = [_as_rows(n, given[n]).astype(BF16) for n in names]
        first = tag == GATHER_GROUPS[0][0]
        slots = _gather_weights(shards, name="gather_" + tag, own_slot=not first)
        if first:
            wts["w_in_parts"] = (chip_s, shards[0], slots[0])
            slots = [lax.dynamic_update_slice(g, s[None], (chip, 0, 0)) for g, s in zip(slots, shards)]
        wts.update(zip(names, slots))
    ncol = D_MODEL // N_CHIPS
    bg_mine = jnp.where(pc == 0, b_gate[0], jnp.zeros_like(b_gate[0]))
    bg_full = lax.dynamic_update_slice(jnp.zeros((2, D_MODEL), F32), bg_mine, (0, chip * ncol))
    bg_slots = _exchange_rows(bg_full.reshape(16, 128), name="exchange_gate_bias")
    bg_full = _sum_slots(bg_slots, name="sum_gate_bias").reshape(2, D_MODEL)
    small = {n: given[n][0] for n in SMALL if n.startswith("ssm")}
    small.update({n: given[n] for n in ("ln1_g", "ln1_b", "ln2_g", "ln2_b")})
    small["b_gate"] = bg_full

    groups = dict(REDUCE_GROUPS)
    parts, recvd, reduced, sent = {}, {}, {}, {}
    grads, delta, new_m, new_v, done = {}, {}, {}, {}, {}

    def start(tag, big_g, add_after):
        parts[tag], recvd[tag] = _reduce_scatter_start([big_g[n] for n in groups[tag]], core_s, tag=tag,
                                                       add_after=add_after)
        return parts[tag]

    def reduce_sum(tag, after):
        reduced[tag] = _reduce_scatter_finish(parts[tag], recvd[tag], chip_s, tag=tag, sum_after=after)
        return reduced[tag][0]

    def adam(tag, after):
        for n, g_mine, g_theirs in zip(groups[tag], *reduced[tag]):
            res = _adamw_halves(core_s, _as_rows(n, given[n]), g_mine, g_theirs, _as_rows(n, given["m_" + n]),
                                _as_rows(n, given["v_" + n]), name="adamw_" + n, after=after)
            done[n] = res[1]
            grads[n], delta[n], new_m[n], new_v[n] = [_from_rows(n, r) for r in res]

    def ffn_grads(big_g, norm_bwd):
        return start("ffn", big_g, (norm_bwd,))

    def mixer_grads(big_g, scan_bwd):
        return start("mixer", big_g, (scan_bwd, *reduce_sum("ffn", (scan_bwd,))))

    def small_grads(small_g, loss_mine):
        sent["stored"] = [_as_stored(n, small_g[n]) for n in SMALL] + [loss_mine.reshape(1)]
        packed = _pack_rows(sent["stored"])
        sent["slots"] = _exchange_rows(packed, name="exchange_small")
        return (packed,)

    grad_x_after, g_w_in, attention_bwd = _local_step(x[0], loss_target[0], wts, small,
                                                      ffn_grads, mixer_grads, small_grads)

    reduce_sum("mixer", (attention_bwd,))
    summed = _sum_slots(sent["slots"], name="sum_small", after=(g_w_in,))
    adam("mixer", (g_w_in,))
    start("w_in", {"w_in": g_w_in}, (summed, *[done[n] for n in groups["mixer"]]))
    in_flight = (parts["w_in"][0],)
    grad_x = grad_x_after(in_flight)
    adam("ffn", in_flight)
    summed = _unpack_rows(summed, [a.shape for a in sent["stored"]])
    loss = summed.pop()[0]
    at = SMALL.index("b_gate")
    summed[at] = lax.dynamic_slice(summed[at], (0, chip * ncol), (2, ncol))
    summed = [g.reshape(1, -1) if g.ndim == 1 else g for g in summed]
    held = [[_as_stored(n, given[prefix + n]).reshape(g.shape) for n, g in zip(SMALL, summed)]
            for prefix in ("", "m_", "v_")]
    small_out = _adamw_each(held[0], summed, held[1], held[2], name="adamw_small", after=in_flight)
    for out, arrs in zip((grads, delta, new_m, new_v), (summed, *small_out)):
        out.update((n, _as_stored(n, a).reshape(given[n].shape)) for n, a in zip(SMALL, arrs))
    reduce_sum("w_in", (*[done[n] for n in groups["ffn"]], small_out[0][0], grad_x))
    adam("w_in", ())

    return (loss, grad_x.reshape(x.shape), *[grads[n] for n in WEIGHTS], *[delta[n] for n in WEIGHTS],
            *[new_m[n] for n in WEIGHTS], *[new_v[n] for n in WEIGHTS])
```

```python
import math

import jax
import jax.numpy as jnp
from jax import lax
from jax.experimental import pallas as pl
from jax.experimental.pallas import tpu as pltpu
from jax.experimental.pallas import tpu_sc as plsc

F32 = jnp.float32
BF16 = jnp.bfloat16
MESH = pl.DeviceIdType.MESH

D_MODEL = 1024
SEQ = 2048
HEAD_DIM = 64
ATTN_HEADS = 8
DILATIONS = (1, 4, 16)
ATTN_WIDTH = ATTN_HEADS * HEAD_DIM
QKV_WIDTH = 3 * ATTN_WIDTH
BLOCK = 128
ROPE_THETA = 10000.0
NEG_INF = -1e30
SSM_GROUP = 16
SSM_GROUPS = 32
SSM_WIDTH = 512
SSM_STATE = 64
SSM_LANES = SSM_GROUPS * SSM_STATE
SCAN_CHUNKS = 8
SCAN_STEPS = SEQ // SCAN_CHUNKS
IN_WIDTH = 3 * QKV_WIDTH + SSM_WIDTH + 2 * D_MODEL
D_FF = 2816
N_CHIPS = 4
N_DEV = 8
DN_ALPHA = 2.0 ** 0.25
LN_EPS = 1e-5
ADAM_LR = 0.001
ADAM_B1 = 0.9
ADAM_B2 = 0.999
ADAM_EPS = 1e-08
ADAM_WD = 0.01
ADAM_STEP = 10
GELU_C = math.sqrt(2.0 / math.pi)
GELU_K = 0.044715

VMEM_LIMIT_BYTES = 56 * 1024 * 1024


def _sds(shape, dtype=F32):
    return jax.ShapeDtypeStruct(tuple(shape), dtype)


def _cp(semantics=None):
    return pltpu.CompilerParams(dimension_semantics=semantics, vmem_limit_bytes=VMEM_LIMIT_BYTES)


HBM_OPERAND = pl.BlockSpec(memory_space=pl.ANY)


def _matmul(a, b, *, grid, a_spec, b_spec, o_spec, out_shape, dims, k_axis=None, name, after=()):
    nk = grid[k_axis] if k_axis is not None else 1
    o_block = tuple(d for d in o_spec.block_shape if d is not None)
    n_after = len(after)

    def body(a_ref, b_ref, *rest):
        o_ref, acc = rest[n_after], rest[n_after + 1:]
        part = lax.dot_general(a_ref[...].astype(BF16), b_ref[...].astype(BF16),
                               (((dims[0],), (dims[1],)), ((), ())), preferred_element_type=F32)
        if k_axis is None:
            o_ref[...] = part.astype(o_ref.dtype)
        else:
            k = pl.program_id(k_axis)

            @pl.when(k == 0)
            def _():
                acc[0][...] = part

            @pl.when(k > 0)
            def _():
                acc[0][...] += part

            @pl.when(k == nk - 1)
            def _():
                o_ref[...] = acc[0][...].astype(o_ref.dtype)

    sem = tuple("arbitrary" if ax == k_axis else "parallel" for ax in range(len(grid)))
    return pl.pallas_call(
        body, grid=grid, in_specs=[a_spec, b_spec] + [HBM_OPERAND] * n_after, out_specs=o_spec, out_shape=out_shape,
        scratch_shapes=[pltpu.VMEM(o_block, F32)] if k_axis is not None else [],
        compiler_params=_cp(sem), name=name)(a, b, *after)


def _mm_cols_nt(dy, wg, *, tm, name, out_dtype=F32, after=()):
    k, ns = wg.shape[1], wg.shape[2]
    m = dy.shape[0]
    a_spec = pl.BlockSpec((tm, ns), lambda i, s: (i, s))
    return _matmul(dy, wg, grid=(m // tm, N_CHIPS), a_spec=a_spec,
                   b_spec=pl.BlockSpec((None, k, ns), lambda i, s: (s, 0, 0)),
                   o_spec=pl.BlockSpec((tm, k), lambda i, s: (i, 0)),
                   out_shape=_sds((m, k), out_dtype), dims=(1, 1), k_axis=1, name=name, after=after)


def _mm_cols_tn(a, dy, *, ns, name, after=()):
    m, k = a.shape
    return _matmul(a, dy, grid=(N_CHIPS,), a_spec=pl.BlockSpec((m, k), lambda s: (0, 0)),
                   b_spec=pl.BlockSpec((m, ns), lambda s: (0, s)),
                   o_spec=pl.BlockSpec((None, k, ns), lambda s: (s, 0, 0)),
                   out_shape=_sds((N_CHIPS, k, ns), BF16), dims=(0, 0), name=name, after=after)


def _mm_rows_tn(a, dy, *, name):
    m, k = a.shape
    rows, n = k // N_CHIPS, dy.shape[1]
    return _matmul(a, dy, grid=(N_CHIPS,), a_spec=pl.BlockSpec((m, rows), lambda s: (0, s)),
                   b_spec=pl.BlockSpec((m, n), lambda s: (0, 0)),
                   o_spec=pl.BlockSpec((None, rows, n), lambda s: (s, 0, 0)),
                   out_shape=_sds((N_CHIPS, rows, n), BF16), dims=(0, 0), name=name)


def _rowwise(fn, tiled, full, outs, accs=(), *, tm, name, after=()):
    args, in_specs = [], []
    for t in tiled:
        if isinstance(t, tuple):
            arr, w, cb = t
            in_specs.append(pl.BlockSpec((tm, w), lambda i, cb=cb: (i, cb)))
        else:
            arr = t
            in_specs.append(pl.BlockSpec((tm, arr.shape[1]), lambda i: (i, 0)))
        args.append(arr)
    rows = args[0].shape[0]
    for f in full:
        in_specs.append(pl.BlockSpec(f.shape, lambda i, nd=f.ndim: (0,) * nd))
        args.append(f)
    out_specs = [pl.BlockSpec((tm, o.shape[1]), lambda i: (i, 0)) for o in outs]
    out_specs += [pl.BlockSpec(a.shape, lambda i, nd=len(a.shape): (0,) * nd) for a in accs]
    n_in, n_out = len(args), len(outs)
    in_specs += [HBM_OPERAND] * len(after)
    first_out = n_in + len(after)

    def body(*refs):
        res = fn(*[r[...] for r in refs[:n_in]])
        res = res if isinstance(res, (tuple, list)) else (res,)
        for r, v in zip(refs[first_out:first_out + n_out], res[:n_out]):
            r[...] = v.astype(r.dtype)
        i = pl.program_id(0)
        for r, v in zip(refs[first_out + n_out:], res[n_out:]):
            @pl.when(i == 0)
            def _(r=r, v=v):
                r[...] = v

            @pl.when(i > 0)
            def _(r=r, v=v):
                r[...] += v

    res = pl.pallas_call(
        body, grid=(rows // tm,), in_specs=in_specs, out_specs=out_specs, out_shape=list(outs) + list(accs),
        compiler_params=_cp(("arbitrary",) if accs else ("parallel",)), name=name)(*args, *after)
    return res


def _colsum(v):
    return jnp.sum(v, axis=0, keepdims=True)


def _ln_stats(z):
    mu = jnp.mean(z, axis=-1, keepdims=True)
    zc = z - mu
    var = jnp.mean(zc * zc, axis=-1, keepdims=True)
    rstd = lax.rsqrt(var + LN_EPS)
    return zc * rstd, rstd


def _ln_bwd(dy, xhat, rstd, g):
    dxh = dy * g
    m1 = jnp.mean(dxh, axis=-1, keepdims=True)
    m2 = jnp.mean(dxh * xhat, axis=-1, keepdims=True)
    return rstd * (dxh - m1 - xhat * m2)


def _swap_halves(t):
    w = t.shape[-1]
    lane = lax.broadcasted_iota(jnp.int32, t.shape, t.ndim - 1)
    return jnp.where((lane % HEAD_DIM) < HEAD_DIM // 2, pltpu.roll(t, w - HEAD_DIM // 2, t.ndim - 1),
                     pltpu.roll(t, HEAD_DIM // 2, t.ndim - 1))


PHASES = max(DILATIONS)
PAIR = 2 * HEAD_DIM
UNITS = SEQ // BLOCK
UNIT_BATCH = 16
ROPE_ROWS = 256
TAIL_COLS = 256


def _to_phase_rows(t):
    return t.reshape(SEQ // PHASES, PHASES, t.shape[1]).transpose(1, 0, 2).reshape(t.shape)


def _reorder_rows(arr, plus=None, *, to_phase, name, scale=1.0):
    def body(*refs):
        o_ref = refs[-1]
        for rho in range(PHASES):
            phase = pl.ds(rho * BLOCK, BLOCK)
            strided = pl.ds(rho, BLOCK, stride=PHASES)
            src, dst = (strided, phase) if to_phase else (phase, strided)
            val = refs[0][src, :]
            if scale != 1.0:
                val = val * scale
            if plus is not None:
                val = val + refs[1][src, :]
            o_ref[dst, :] = val

    spec = pl.BlockSpec((SEQ, BLOCK), lambda j: (0, j))
    ins = [arr] if plus is None else [arr, plus]
    return pl.pallas_call(body, grid=(arr.shape[1] // BLOCK,), in_specs=[spec] * len(ins), out_specs=spec,
                          out_shape=_sds(arr.shape), compiler_params=_cp(("parallel",)), name=name)(*ins)


def _rope(t, cf, ss):
    return t * cf + _swap_halves(t) * ss


def _rope_transposed(d, cf, ss):
    return d * cf + _swap_halves(d * ss)


def _unit_pieces(u, dil):
    pieces, length = PHASES // dil, 8 * dil
    if dil == 1:
        rho, i = 0, u
    elif dil == PHASES:
        rho, i = u, 0
    else:
        rho, i = jnp.bitwise_and(u, dil - 1), jnp.right_shift(u, dil.bit_length() - 1)
    before = jnp.maximum(i - 1, 0)
    cur = [pl.multiple_of((rho + dil * k) * BLOCK + length * i, 8) for k in range(pieces)]
    prev = [pl.multiple_of((rho + dil * k) * BLOCK + length * before, 8) for k in range(pieces)]
    return i, cur, prev


def _load_tile(ref, starts, dil):
    return jnp.concatenate([ref[pl.ds(st, 8 * dil), :] for st in starts], axis=0)


def _store_tile(ref, starts, dil, val, head=None, accumulate=False):
    length = 8 * dil
    lanes = slice(None) if head is None else pl.ds(head * HEAD_DIM, HEAD_DIM)
    cols = slice(None) if head is None else slice(head * HEAD_DIM, (head + 1) * HEAD_DIM)
    for k, st in enumerate(starts):
        piece = val[k * length:(k + 1) * length, cols]
        if accumulate:
            ref[pl.ds(st, length), lanes] += piece
        else:
            ref[pl.ds(st, length), lanes] = piece


def _tile_position(idx, dil):
    pieces, length = PHASES // dil, 8 * dil
    return pieces * jnp.bitwise_and(idx, length - 1) + jnp.right_shift(idx, length.bit_length() - 1)


def _band_mask(i, dil):
    row = lax.broadcasted_iota(jnp.int32, (BLOCK, 2 * BLOCK), 0)
    col = lax.broadcasted_iota(jnp.int32, (BLOCK, 2 * BLOCK), 1)
    key_pos = _tile_position(jnp.bitwise_and(col, BLOCK - 1), dil) + jnp.where(col >= BLOCK, 0, -BLOCK)
    dist = _tile_position(row, dil) - key_pos
    return (dist >= 0) & (dist <= BLOCK) & ((col >= BLOCK) | (i > 0))


def _causal_mask():
    row = lax.broadcasted_iota(jnp.int32, (BLOCK, BLOCK), 0)
    col = lax.broadcasted_iota(jnp.int32, (BLOCK, BLOCK), 1)
    return row >= col


def _pair_views(col0):
    return [pl.BlockSpec((SEQ, PAIR), lambda hp, g=g: (0, col0 // PAIR + g * (ATTN_WIDTH // PAIR) + hp))
            for g in range(len(DILATIONS))]


def _project_shard(shard, x_ref, w_ref, cf_ref, ss_ref, o_ref):
    ns = w_ref.shape[1]
    tiles = ns // PAIR
    xb = x_ref[...].astype(BF16)
    cf, ss = cf_ref[...], ss_ref[...]

    def write(rotated, scaled):
        for t0 in range(0, tiles, 2):
            strip = jnp.dot(xb, w_ref[:, t0 * PAIR:(t0 + 2) * PAIR], preferred_element_type=F32)
            for t in (t0, t0 + 1):
                val = strip[:, (t - t0) * PAIR:(t - t0 + 1) * PAIR]
                if t < rotated:
                    val = _rope(val, cf, ss)
                    if t < scaled:
                        val = val * (1.0 / math.sqrt(HEAD_DIM))
                o_ref[:, t * PAIR:(t + 1) * PAIR] = val

    for s in range(N_CHIPS):
        rotated = min(max(2 * QKV_WIDTH - s * ns, 0), ns) // PAIR
        scaled = min(max(QKV_WIDTH - s * ns, 0), ns) // PAIR
        @pl.when(shard == s)
        def _(rotated=rotated, scaled=scaled):
            write(rotated, scaled)


def _project_in_own(chip, x, w_own, cos_f, sin_s):
    ns = w_own.shape[1]

    def body(chip_ref, x_ref, w_ref, cf_ref, ss_ref, o_ref):
        _project_shard(chip_ref[0], x_ref, w_ref, cf_ref, ss_ref, o_ref)

    table = pl.BlockSpec((FF_ROWS, PAIR), lambda i, chip_ref: (i, 0))
    return pl.pallas_call(
        body,
        grid_spec=pltpu.PrefetchScalarGridSpec(
            num_scalar_prefetch=1, grid=(SEQ // FF_ROWS,),
            in_specs=[pl.BlockSpec((FF_ROWS, D_MODEL), lambda i, chip_ref: (i, 0)),
                      pl.BlockSpec((D_MODEL, ns), lambda i, chip_ref: (0, 0)), table, table],
            out_specs=pl.BlockSpec((FF_ROWS, ns), lambda i, chip_ref: (i, chip_ref[0]))),
        out_shape=_sds((SEQ, N_CHIPS * ns)), compiler_params=_cp(("parallel",)), name="project_in_own")(
            chip, x, w_own, cos_f, sin_s)


def _project_in(chip, x, wg, cos_f, sin_s, started):
    ns = wg.shape[2]

    def other(j, chip_ref):
        return (chip_ref[0] + 1 + j) % N_CHIPS

    def body(chip_ref, x_ref, w_ref, cf_ref, ss_ref, started_ref, o_ref):
        _project_shard(other(pl.program_id(1), chip_ref), x_ref, w_ref, cf_ref, ss_ref, o_ref)

    table = pl.BlockSpec((FF_ROWS, PAIR), lambda i, j, chip_ref: (i, 0))
    return pl.pallas_call(
        body,
        grid_spec=pltpu.PrefetchScalarGridSpec(
            num_scalar_prefetch=1, grid=(SEQ // FF_ROWS, N_CHIPS - 1),
            in_specs=[pl.BlockSpec((FF_ROWS, D_MODEL), lambda i, j, chip_ref: (i, 0)),
                      pl.BlockSpec((None, D_MODEL, ns), lambda i, j, chip_ref: (other(j, chip_ref), 0, 0)),
                      table, table, HBM_OPERAND],
            out_specs=pl.BlockSpec((FF_ROWS, ns), lambda i, j, chip_ref: (i, other(j, chip_ref)))),
        out_shape=_sds((SEQ, N_CHIPS * ns)), input_output_aliases={5: 0},
        compiler_params=_cp(("parallel", "parallel")), name="project_in")(chip, x, wg, cos_f, sin_s, started)


def _attention_fwd(proj):
    ng = len(DILATIONS)

    def body(*refs):
        q_refs, k_refs, v_refs = refs[:ng], refs[ng:2 * ng], refs[2 * ng:3 * ng]
        attn_ref, lse_ref = refs[3 * ng:]
        qr_refs, kr_refs = q_refs, k_refs
        first = lax.broadcasted_iota(jnp.int32, (BLOCK, PAIR), 1) < HEAD_DIM
        for g, dil in enumerate(DILATIONS):
            two_blocks = SEQ // dil > BLOCK

            def units(t, carry, g=g, dil=dil, two_blocks=two_blocks):
                picked = [_unit_pieces(t * UNIT_BATCH + j, dil) for j in range(UNIT_BATCH)]

                def tiles(ref, with_prev=False):
                    if with_prev and two_blocks:
                        return jnp.stack([jnp.concatenate([_load_tile(ref, prev, dil), _load_tile(ref, rows, dil)],
                                                          axis=0) for _, rows, prev in picked])
                    return jnp.stack([_load_tile(ref, rows, dil) for _, rows, _ in picked])

                qq = tiles(qr_refs[g]).astype(BF16)
                kk = tiles(kr_refs[g], True).astype(BF16)
                vv = tiles(v_refs[g], True).astype(BF16)
                if two_blocks:
                    valid = jnp.stack([_band_mask(i, dil) for i, _, _ in picked])
                else:
                    valid = _causal_mask()[None]
                mine = first[None]
                zero = jnp.zeros_like(qq)
                outs, lses = [], []
                for qh in (jnp.where(mine, qq, zero), jnp.where(mine, zero, qq)):
                    s = jnp.einsum("pqd,pkd->pqk", qh, kk, preferred_element_type=F32)
                    s = jnp.where(valid, s, NEG_INF)
                    m = jnp.max(s, axis=-1, keepdims=True)
                    p = jnp.exp(s - m)
                    l = jnp.sum(p, axis=-1, keepdims=True)
                    outs.append(jnp.einsum("pqk,pkd->pqd", p.astype(BF16), vv, preferred_element_type=F32) * (1.0 / l))
                    lses.append(m + jnp.log(l))
                o = jnp.where(mine, outs[0], outs[1])
                lse = jnp.where(mine, lses[0], lses[1])
                if g > 0:
                    lse_old = tiles(lse_ref)
                    m = jnp.maximum(lse_old, lse)
                    lse_new = m + jnp.log(jnp.exp(lse_old - m) + jnp.exp(lse - m))
                    o = tiles(attn_ref) * jnp.exp(lse_old - lse_new) + o * jnp.exp(lse - lse_new)
                    lse = lse_new
                for j, (_, rows, _) in enumerate(picked):
                    _store_tile(attn_ref, rows, dil, o[j])
                    _store_tile(lse_ref, rows, dil, lse[j])
                return carry

            lax.fori_loop(0, UNITS // UNIT_BATCH, units, 0)

    out = pl.BlockSpec((SEQ, PAIR), lambda hp: (0, hp))
    return pl.pallas_call(
        body, grid=(ATTN_WIDTH // PAIR,),
        in_specs=_pair_views(0) + _pair_views(QKV_WIDTH) + _pair_views(2 * QKV_WIDTH),
        out_specs=[out, out], out_shape=[_sds((SEQ, ATTN_WIDTH)), _sds((SEQ, ATTN_WIDTH))],
        compiler_params=_cp(("parallel",)), name="attention_fwd")(*([proj] * (3 * ng)))


def _attention_bwd(proj, cos_f, sin_s, d_attn, attn, lse, d_u, d_gl):
    pairs = ATTN_WIDTH // PAIR
    last = len(DILATIONS) * pairs - 1

    def accumulate(dil, qr_ref, kr_ref, v_ref, do_ref, o_ref, lse_ref, dq_acc, dk_acc, dv_acc):
        two_blocks = SEQ // dil > BLOCK
        dk_acc[...] = jnp.zeros_like(dk_acc)
        dv_acc[...] = jnp.zeros_like(dv_acc)
        nk = 2 * BLOCK if two_blocks else BLOCK
        first = lax.broadcasted_iota(jnp.int32, (BLOCK, PAIR), 1) < HEAD_DIM
        first_k = lax.broadcasted_iota(jnp.int32, (nk, PAIR), 1) < HEAD_DIM

        def units(t, carry):
            picked = [_unit_pieces(t * UNIT_BATCH + j, dil) for j in range(UNIT_BATCH)]

            def tiles(ref, with_prev=False):
                if with_prev and two_blocks:
                    return jnp.stack([jnp.concatenate([_load_tile(ref, prev, dil), _load_tile(ref, rows, dil)], axis=0)
                                      for _, rows, prev in picked])
                return jnp.stack([_load_tile(ref, rows, dil) for _, rows, _ in picked])

            qq = tiles(qr_ref).astype(BF16)
            kk = tiles(kr_ref, True).astype(BF16)
            vv = tiles(v_ref, True).astype(BF16)
            dof = tiles(do_ref)
            dd = dof * tiles(o_ref)
            lse3 = tiles(lse_ref)
            dob = dof.astype(BF16)
            if two_blocks:
                valid = jnp.stack([_band_mask(i, dil) for i, _, _ in picked])
            else:
                valid = _causal_mask()[None]
            zq, zf = jnp.zeros_like(qq), jnp.zeros_like(dd)
            dqs, dks, dvs = [], [], []
            for head in range(2):
                mine = first[None] if head == 0 else jnp.logical_not(first)[None]
                delta = jnp.sum(jnp.where(mine, dd, zf), axis=-1, keepdims=True)
                lse_h = lse3[:, :, head * HEAD_DIM:head * HEAD_DIM + 1]
                s = jnp.einsum("pqd,pkd->pqk", jnp.where(mine, qq, zq), kk, preferred_element_type=F32)
                p = jnp.where(valid, jnp.exp(s - lse_h), 0.0)
                dp = jnp.einsum("pqd,pkd->pqk", jnp.where(mine, dob, zq), vv, preferred_element_type=F32)
                ds = (p * (dp - delta)).astype(BF16)
                dqs.append(jnp.einsum("pqk,pkd->pqd", ds, kk, preferred_element_type=F32))
                dks.append(jnp.einsum("pqk,pqd->pkd", ds, qq, preferred_element_type=F32))
                dvs.append(jnp.einsum("pqk,pqd->pkd", p.astype(BF16), dob, preferred_element_type=F32))
            dq = jnp.where(first[None], dqs[0], dqs[1])
            dk = jnp.where(first_k[None], dks[0], dks[1])
            dv = jnp.where(first_k[None], dvs[0], dvs[1])
            for j, (_, rows, prev) in enumerate(picked):
                _store_tile(dq_acc, rows, dil, dq[j])
                _store_tile(dk_acc, rows, dil, dk[j, nk - BLOCK:], accumulate=True)
                _store_tile(dv_acc, rows, dil, dv[j, nk - BLOCK:], accumulate=True)
                if two_blocks:
                    _store_tile(dk_acc, prev, dil, dk[j, :BLOCK], accumulate=True)
                    _store_tile(dv_acc, prev, dil, dv[j, :BLOCK], accumulate=True)
            return carry

        lax.fori_loop(0, UNITS // UNIT_BATCH, units, 0)

    def body(qr_ref, kr_ref, v_ref, cf_ref, ss_ref, do_ref, o_ref, lse_ref, du_ref, dgl_ref, out_ref,
             dq_acc, dk_acc, dv_acc, dq_buf, dk_buf, dv_buf, sems):
        step = pl.program_id(0) * pairs + pl.program_id(1)

        def columns(at):
            return [pltpu.make_async_copy(
                buf, out_ref.at[:, pl.ds(pl.multiple_of(j * QKV_WIDTH + at * PAIR, PAIR), PAIR)], sems.at[j])
                for j, buf in enumerate((dq_buf, dk_buf, dv_buf))]

        def tail(ref, col0, at):
            cols = pl.ds(pl.multiple_of(col0 + at * TAIL_COLS, TAIL_COLS), TAIL_COLS)
            return pltpu.make_async_copy(ref, out_ref.at[:, cols], sems.at[3])

        gl_steps, u_steps = 2 * D_MODEL // TAIL_COLS, SSM_WIDTH // TAIL_COLS
        from_gl = step < gl_steps
        from_u = jnp.logical_and(step >= gl_steps, step < gl_steps + u_steps)
        tail_gl = tail(dgl_ref, 3 * QKV_WIDTH + SSM_WIDTH, step)
        tail_u = tail(du_ref, 3 * QKV_WIDTH, step - gl_steps)
        pl.when(from_gl)(tail_gl.start)
        pl.when(from_u)(tail_u.start)

        for g, dil in enumerate(DILATIONS):
            @pl.when(pl.program_id(0) == g)
            def _(dil=dil):
                accumulate(dil, qr_ref, kr_ref, v_ref, do_ref, o_ref, lse_ref, dq_acc, dk_acc, dv_acc)

        @pl.when(step > 0)
        def _():
            for cp in columns(step - 1):
                cp.wait()

        def finish(t, carry):
            rows = pl.ds(pl.multiple_of(t * ROPE_ROWS, ROPE_ROWS), ROPE_ROWS)
            cf, ss = cf_ref[rows, :], ss_ref[rows, :]
            dq = dq_acc[rows, :] * (1.0 / math.sqrt(HEAD_DIM))
            dq_buf[rows, :] = _rope_transposed(dq, cf, ss).astype(BF16)
            dk_buf[rows, :] = _rope_transposed(dk_acc[rows, :], cf, ss).astype(BF16)
            dv_buf[rows, :] = dv_acc[rows, :].astype(BF16)
            return carry

        lax.fori_loop(0, SEQ // ROPE_ROWS, finish, 0)
        for cp in columns(step):
            cp.start()
        pl.when(from_gl)(tail_gl.wait)
        pl.when(from_u)(tail_u.wait)

        @pl.when(step == last)
        def _():
            for cp in columns(step):
                cp.wait()

    whole = pl.BlockSpec((SEQ, PAIR), lambda g, hp: (0, 0))
    pair = pl.BlockSpec((SEQ, PAIR), lambda g, hp: (0, hp))
    views = [pl.BlockSpec((SEQ, PAIR), lambda g, hp, c0=col0 // PAIR: (0, c0 + g * pairs + hp))
             for col0 in (0, QKV_WIDTH, 2 * QKV_WIDTH)]
    gl_blocks, u_blocks = 2 * D_MODEL // TAIL_COLS, SSM_WIDTH // TAIL_COLS
    assert gl_blocks + u_blocks <= last + 1
    gl_spec = pl.BlockSpec((SEQ, TAIL_COLS), lambda g, hp: (0, jnp.minimum(g * pairs + hp, gl_blocks - 1)))
    u_spec = pl.BlockSpec((SEQ, TAIL_COLS),
                          lambda g, hp: (0, jnp.clip(g * pairs + hp - gl_blocks, 0, u_blocks - 1)))
    return pl.pallas_call(
        body, grid=(len(DILATIONS), pairs),
        in_specs=views + [whole, whole, pair, pair, pair, u_spec, gl_spec],
        out_specs=HBM_OPERAND, out_shape=_sds((SEQ, IN_WIDTH), BF16),
        scratch_shapes=[pltpu.VMEM((SEQ, PAIR), F32)] * 3 + [pltpu.VMEM((SEQ, PAIR), BF16)] * 3
        + [pltpu.SemaphoreType.DMA((4,))],
        compiler_params=_cp(("arbitrary", "arbitrary")), name="attention_bwd")(
            proj, proj, proj, cos_f, sin_s, d_attn, attn, lse, d_u, d_gl)


def _cmul(ar, ai, br, bi):
    return ar * br - ai * bi, ar * bi + ai * br


def _pow256(ar, ai):
    for _ in range(8):
        ar, ai = _cmul(ar, ai, ar, ai)
    return ar, ai


def _chunk_carries(first_r, first_i, pr, pi, reverse):
    rows = lax.broadcasted_iota(jnp.int32, first_r.shape, 0)
    out_r = jnp.zeros_like(first_r)
    out_i = jnp.zeros_like(first_i)
    hr = jnp.zeros_like(first_r[0:1])
    hi = jnp.zeros_like(hr)
    order = range(SCAN_CHUNKS - 1, -1, -1) if reverse else range(SCAN_CHUNKS)
    for c in order:
        out_r = jnp.where(rows == c, hr, out_r)
        out_i = jnp.where(rows == c, hi, out_i)
        tr, ti = _cmul(pr[0:1], pi[0:1], hr, hi)
        hr = first_r[c:c + 1] + tr
        hi = first_i[c:c + 1] + ti
    return out_r, out_i


def _tile(j):
    return pl.ds(pl.multiple_of(j * SCAN_CHUNKS, SCAN_CHUNKS), SCAN_CHUNKS)


def _to_scan_rows(t):
    per = SCAN_STEPS // PHASES
    return t.reshape(PHASES, SCAN_CHUNKS, per, t.shape[1]).transpose(2, 0, 1, 3).reshape(t.shape)


def _from_scan_rows(t):
    per = SCAN_STEPS // PHASES
    return t.reshape(per, PHASES, SCAN_CHUNKS, t.shape[1]).transpose(1, 2, 0, 3).reshape(t.shape)


def _scan_in_place(hr_ref, hi_ref, a_r, a_i):
    def local(j, carry):
        tr, ti = _cmul(a_r, a_i, carry[0], carry[1])
        nr = tr + hr_ref[_tile(j), :]
        ni = ti + hi_ref[_tile(j), :]
        hr_ref[_tile(j), :] = nr
        hi_ref[_tile(j), :] = ni
        return nr, ni

    zero = jnp.zeros_like(a_r)
    last_r, last_i = lax.fori_loop(0, SCAN_STEPS, local, (zero, zero), unroll=4)
    pr, pi = _pow256(a_r, a_i)
    er, ei = _chunk_carries(last_r, last_i, pr, pi, reverse=False)

    def fix(j, carry):
        tr, ti = _cmul(carry[0], carry[1], er, ei)
        hr_ref[_tile(j), :] += tr
        hi_ref[_tile(j), :] += ti
        return _cmul(carry[0], carry[1], a_r, a_i)

    lax.fori_loop(0, SCAN_STEPS, fix, (a_r, a_i), unroll=4)
    return er, ei


def _reverse_scan_in_place(lr_ref, li_ref, hr_ref, hi_ref, er, ei, a_r, a_i):
    def local(t, carry):
        j = SCAN_STEPS - 1 - t
        tr, ti = _cmul(a_r, a_i, carry[0], carry[1])
        nr = tr + lr_ref[_tile(j), :]
        ni = ti + li_ref[_tile(j), :]
        lr_ref[_tile(j), :] = nr
        li_ref[_tile(j), :] = ni
        return nr, ni

    zero = jnp.zeros_like(a_r)
    first_r, first_i = lax.fori_loop(0, SCAN_STEPS, local, (zero, zero), unroll=4)
    pr, pi = _pow256(a_r, a_i)
    nxt_r, nxt_i = _chunk_carries(first_r, first_i, pr, pi, reverse=True)

    def accumulate(lam_r, lam_i, hp_r, hp_i, acc):
        return (acc[0] + lam_r * hp_r + lam_i * hp_i, acc[1] + lam_i * hp_r - lam_r * hp_i)

    def fix(t, carry):
        qr, qi, acc_r, acc_i = carry
        j = SCAN_STEPS - 1 - t
        tr, ti = _cmul(qr, qi, nxt_r, nxt_i)
        lam_r = lr_ref[_tile(j), :] + tr
        lam_i = li_ref[_tile(j), :] + ti
        lr_ref[_tile(j), :] = lam_r
        li_ref[_tile(j), :] = lam_i
        acc_r, acc_i = accumulate(lam_r, lam_i, hr_ref[_tile(j - 1), :], hi_ref[_tile(j - 1), :], (acc_r, acc_i))
        qr, qi = _cmul(qr, qi, a_r, a_i)
        return qr, qi, acc_r, acc_i

    qr, qi, acc_r, acc_i = lax.fori_loop(0, SCAN_STEPS - 1, fix, (a_r, a_i, zero, zero), unroll=4)
    tr, ti = _cmul(qr, qi, nxt_r, nxt_i)
    lam_r = lr_ref[_tile(0), :] + tr
    lam_i = li_ref[_tile(0), :] + ti
    lr_ref[_tile(0), :] = lam_r
    li_ref[_tile(0), :] = lam_i
    acc_r, acc_i = accumulate(lam_r, lam_i, er, ei, (acc_r, acc_i))
    return jnp.sum(acc_r, axis=0, keepdims=True), jnp.sum(acc_i, axis=0, keepdims=True)


def _rope_tables():
    half = HEAD_DIM // 2
    inv_freq = ROPE_THETA ** (-jnp.arange(half, dtype=F32) / half)
    ang = jnp.arange(SEQ, dtype=F32)[:, None] * inv_freq[None, :]
    cos, sin = jnp.cos(ang), jnp.sin(ang)
    cos_f = jnp.concatenate([cos, cos, cos, cos], axis=1)
    sin_s = jnp.concatenate([-sin, sin, -sin, sin], axis=1)
    return cos_f, sin_s


def _ssm_discretise(a_re, a_im, log_dt, b_re, b_im):
    lam = lax.complex(a_re, a_im)
    dt = jnp.exp(log_dt)[:, None]
    a_bar = jnp.exp(lam * dt)
    b_bar = ((a_bar - 1.0) / lam)[..., None] * lax.complex(b_re, b_im)
    return a_bar.real, a_bar.imag, b_bar.real, b_bar.imag


SSM_SLABS = 4
SLAB_GROUPS = SSM_GROUPS // SSM_SLABS
SLAB_IN = SSM_WIDTH // SSM_SLABS
SLAB_STATE = SSM_LANES // SSM_SLABS


def _slab_block_diag(blocks):
    _, r, c = blocks.shape
    eye = jnp.eye(SLAB_GROUPS, dtype=blocks.dtype)
    b5 = blocks.reshape(SSM_SLABS, SLAB_GROUPS, r, 1, c) * eye[None, :, None, :, None]
    return b5.reshape(SSM_SLABS, SLAB_GROUPS * r, SLAB_GROUPS * c)


def _diag_blocks(a, b):
    ra, cb = a.shape[1], b.shape[1]
    wa, wb = ra // SLAB_GROUPS, cb // SLAB_GROUPS
    d = lax.dot_general(a, b, (((0,), (0,)), ((), ())), preferred_element_type=F32)
    row_g = jnp.right_shift(lax.broadcasted_iota(jnp.int32, (ra, cb), 0), wa.bit_length() - 1)
    col_g = jnp.right_shift(lax.broadcasted_iota(jnp.int32, (ra, cb), 1), wb.bit_length() - 1)
    d = jnp.where(row_g == col_g, d, 0.0)
    fold = (jnp.bitwise_and(lax.broadcasted_iota(jnp.int32, (cb, wb), 0), wb - 1)
            == lax.broadcasted_iota(jnp.int32, (cb, wb), 1)).astype(F32)
    return jnp.dot(d, fold, preferred_element_type=F32, precision=lax.Precision.HIGHEST)


def _slab_specs():
    tok = pl.BlockSpec((SEQ, SLAB_IN), lambda j: (0, j))
    state = pl.BlockSpec((SEQ, SLAB_STATE), lambda j: (0, j))
    b_in = pl.BlockSpec((None, SLAB_IN, SLAB_STATE), lambda j: (j, 0, 0))
    c_out = pl.BlockSpec((None, SLAB_STATE, SLAB_IN), lambda j: (j, 0, 0))
    vec = pl.BlockSpec((1, SLAB_STATE), lambda j: (0, j))
    ent = pl.BlockSpec((SCAN_CHUNKS, SLAB_STATE), lambda j: (0, j))
    return tok, state, b_in, c_out, vec, ent


def _ssm_forward(u, b_in_r, b_in_i, c_out_r, c_out_ni, a_r, a_i):
    def body(u_ref, br_ref, bi_ref, cr_ref, ci_ref, ar_ref, ai_ref, y_ref, hr_ref, hi_ref, er_ref, ei_ref):
        uu = u_ref[...]
        hr_ref[...] = jnp.dot(uu, br_ref[...], preferred_element_type=F32)
        hi_ref[...] = jnp.dot(uu, bi_ref[...], preferred_element_type=F32)
        a_re = jnp.broadcast_to(ar_ref[...], (SCAN_CHUNKS, SLAB_STATE))
        a_im = jnp.broadcast_to(ai_ref[...], (SCAN_CHUNKS, SLAB_STATE))
        er_ref[...], ei_ref[...] = _scan_in_place(hr_ref, hi_ref, a_re, a_im)
        y_ref[...] = (jnp.dot(hr_ref[...].astype(BF16), cr_ref[...], preferred_element_type=F32)
                      + jnp.dot(hi_ref[...].astype(BF16), ci_ref[...], preferred_element_type=F32))

    tok, state, b_in, c_out, vec, ent = _slab_specs()
    return pl.pallas_call(
        body, grid=(SSM_SLABS,), in_specs=[tok, b_in, b_in, c_out, c_out, vec, vec],
        out_specs=[tok, state, state, ent, ent],
        out_shape=[_sds((SEQ, SSM_WIDTH)), _sds((SEQ, SSM_LANES)), _sds((SEQ, SSM_LANES)),
                   _sds((SCAN_CHUNKS, SSM_LANES)), _sds((SCAN_CHUNKS, SSM_LANES))],
        compiler_params=_cp(("parallel",)), name="ssm_forward")(u, b_in_r, b_in_i, c_out_r, c_out_ni, a_r, a_i)


def _ssm_backward(d_y, d_u_skip, u, h_r, h_i, e_r, e_i, b_in_r, b_in_i, c_out_r, c_out_ni, a_r, a_i):
    def body(dy_ref, skip_ref, u_ref, hr_ref, hi_ref, er_ref, ei_ref, br_ref, bi_ref, cr_ref, ci_ref, ar_ref, ai_ref,
             du_ref, dar_ref, dai_ref, dcr_ref, dci_ref, dbr_ref, dbi_ref, lr_ref, li_ref):
        dy = dy_ref[...]
        lr_ref[...] = _dot_nt(dy, cr_ref[...])
        li_ref[...] = _dot_nt(dy, ci_ref[...])
        a_re = jnp.broadcast_to(ar_ref[...], (SCAN_CHUNKS, SLAB_STATE))
        a_im = -jnp.broadcast_to(ai_ref[...], (SCAN_CHUNKS, SLAB_STATE))
        dar_ref[...], dai_ref[...] = _reverse_scan_in_place(lr_ref, li_ref, hr_ref, hi_ref, er_ref[...], ei_ref[...],
                                                            a_re, a_im)
        dcr_ref[...] = _diag_blocks(dy, hr_ref[...].astype(BF16))
        dci_ref[...] = _diag_blocks(dy, hi_ref[...].astype(BF16))
        lam_r, lam_i = lr_ref[...].astype(BF16), li_ref[...].astype(BF16)
        uu = u_ref[...]
        dbr_ref[...] = _diag_blocks(uu, lam_r)
        dbi_ref[...] = _diag_blocks(uu, lam_i)
        du = skip_ref[...] + _dot_nt(lam_r, br_ref[...]) + _dot_nt(lam_i, bi_ref[...])
        du_ref[...] = du.astype(BF16)

    tok, state, b_in, c_out, vec, ent = _slab_specs()
    db = pl.BlockSpec((SLAB_IN, SSM_STATE), lambda j: (j, 0))
    return pl.pallas_call(
        body, grid=(SSM_SLABS,), in_specs=[tok, tok, tok, state, state, ent, ent, b_in, b_in, c_out, c_out, vec, vec],
        out_specs=[tok, vec, vec, db, db, db, db],
        out_shape=[_sds((SEQ, SSM_WIDTH), BF16), _sds((1, SSM_LANES)), _sds((1, SSM_LANES))]
        + [_sds((SSM_WIDTH, SSM_STATE))] * 4,
        scratch_shapes=[pltpu.VMEM((SEQ, SLAB_STATE), F32)] * 2,
        compiler_params=_cp(("parallel",)), name="ssm_backward")(
            d_y, d_u_skip, u, h_r, h_i, e_r, e_i, b_in_r, b_in_i, c_out_r, c_out_ni, a_r, a_i)


FF_ROWS = 1024
FF_SHARD = D_FF // N_CHIPS


def _dot_nt(a, b):
    return lax.dot_general(a, b, (((1,), (1,)), ((), ())), preferred_element_type=F32)


def _ffn_up(h, w_gate_t, w_up_t):
    def body(h_ref, wg_ref, wu_ref, a_ref, b_ref, act_ref):
        hb = h_ref[...].astype(BF16)
        a = _dot_nt(hb, wg_ref[...])
        b = _dot_nt(hb, wu_ref[...])
        a_ref[...] = a
        b_ref[...] = b
        act_ref[...] = (a * jax.nn.sigmoid(a) * b).astype(BF16)

    w_spec = pl.BlockSpec((None, FF_SHARD, D_MODEL), lambda i, k: (k, 0, 0))
    o_spec = pl.BlockSpec((None, FF_ROWS, FF_SHARD), lambda i, k: (k, i, 0))
    shape = (N_CHIPS, SEQ, FF_SHARD)
    return pl.pallas_call(
        body, grid=(SEQ // FF_ROWS, N_CHIPS),
        in_specs=[pl.BlockSpec((FF_ROWS, D_MODEL), lambda i, k: (i, 0)), w_spec, w_spec],
        out_specs=[o_spec, o_spec, o_spec], out_shape=[_sds(shape), _sds(shape), _sds(shape, BF16)],
        compiler_params=_cp(("parallel", "parallel")), name="ffn_up")(h, w_gate_t, w_up_t)


def _ffn_down_ln2_loss(act, w_down, h, tgt, ln_g, ln_b):
    def body(act_ref, w_ref, h_ref, tgt_ref, g_ref, b_ref, dz_ref, loss_ref, dg_ref, db_ref, acc):
        i, k = pl.program_id(0), pl.program_id(1)
        part = jnp.dot(act_ref[...], w_ref[...], preferred_element_type=F32)

        @pl.when(k == 0)
        def _():
            acc[...] = part

        @pl.when(k > 0)
        def _():
            acc[...] += part

        @pl.when(k == N_CHIPS - 1)
        def _():
            g = g_ref[...]
            xhat, rstd = _ln_stats(DN_ALPHA * h_ref[...] + acc[...])
            err = xhat * g + b_ref[...] - tgt_ref[...]
            d_out = err * (1.0 / D_MODEL)
            dz_ref[...] = _ln_bwd(d_out, xhat, rstd, g)
            loss_rows = jnp.sum(err * err, axis=-1, keepdims=True) * (0.5 / D_MODEL)
            sums = (jnp.broadcast_to(jnp.sum(loss_rows, axis=0, keepdims=True), loss_ref.shape),
                    _colsum(d_out * xhat), _colsum(d_out))
            for ref, val in zip((loss_ref, dg_ref, db_ref), sums):
                @pl.when(i == 0)
                def _(ref=ref, val=val):
                    ref[...] = val

                @pl.when(i > 0)
                def _(ref=ref, val=val):
                    ref[...] += val

    row = pl.BlockSpec((FF_ROWS, D_MODEL), lambda i, k: (i, 0))
    vec = pl.BlockSpec((1, D_MODEL), lambda i, k: (0, 0))
    return pl.pallas_call(
        body, grid=(SEQ // FF_ROWS, N_CHIPS),
        in_specs=[pl.BlockSpec((None, FF_ROWS, FF_SHARD), lambda i, k: (k, i, 0)),
                  pl.BlockSpec((None, FF_SHARD, D_MODEL), lambda i, k: (k, 0, 0)), row, row, vec, vec],
        out_specs=[row, pl.BlockSpec((1, BLOCK), lambda i, k: (0, 0)), vec, vec],
        out_shape=[_sds((SEQ, D_MODEL)), _sds((1, BLOCK)), _sds((1, D_MODEL)), _sds((1, D_MODEL))],
        scratch_shapes=[pltpu.VMEM((FF_ROWS, D_MODEL), F32)],
        compiler_params=_cp(("arbitrary", "arbitrary")), name="ffn_down_ln2_loss")(act, w_down, h, tgt, ln_g, ln_b)


def _ffn_down_bwd(dz, w_down, a, b):
    def body(dz_ref, wd_ref, a_ref, b_ref, da_ref, db_ref):
        d_act = _dot_nt(dz_ref[...].astype(BF16), wd_ref[...])
        av = a_ref[...]
        sg = jax.nn.sigmoid(av)
        da_ref[...] = (d_act * b_ref[...] * sg * (1.0 + av * (1.0 - sg))).astype(BF16)
        db_ref[...] = (d_act * av * sg).astype(BF16)

    t_spec = pl.BlockSpec((None, FF_ROWS, FF_SHARD), lambda i, k: (k, i, 0))
    shape = (N_CHIPS, SEQ, FF_SHARD)
    return pl.pallas_call(
        body, grid=(SEQ // FF_ROWS, N_CHIPS),
        in_specs=[pl.BlockSpec((FF_ROWS, D_MODEL), lambda i, k: (i, 0)),
                  pl.BlockSpec((None, FF_SHARD, D_MODEL), lambda i, k: (k, 0, 0)), t_spec, t_spec],
        out_specs=[t_spec, t_spec], out_shape=[_sds(shape, BF16), _sds(shape, BF16)],
        compiler_params=_cp(("parallel", "parallel")), name="ffn_down_bwd")(dz, w_down, a, b)


def _ffn_dh(d_a, d_b, w_gate_t, w_up_t):
    def body(da_ref, db_ref, wg_ref, wu_ref, o_ref, acc):
        k = pl.program_id(1)
        part = (jnp.dot(da_ref[...], wg_ref[...], preferred_element_type=F32)
                + jnp.dot(db_ref[...], wu_ref[...], preferred_element_type=F32))

        @pl.when(k == 0)
        def _():
            acc[...] = part

        @pl.when(k > 0)
        def _():
            acc[...] += part

        @pl.when(k == N_CHIPS - 1)
        def _():
            o_ref[...] = acc[...]

    t_spec = pl.BlockSpec((None, FF_ROWS, FF_SHARD), lambda i, k: (k, i, 0))
    w_spec = pl.BlockSpec((None, FF_SHARD, D_MODEL), lambda i, k: (k, 0, 0))
    return pl.pallas_call(
        body, grid=(SEQ // FF_ROWS, N_CHIPS), in_specs=[t_spec, t_spec, w_spec, w_spec],
        out_specs=pl.BlockSpec((FF_ROWS, D_MODEL), lambda i, k: (i, 0)), out_shape=_sds((SEQ, D_MODEL)),
        scratch_shapes=[pltpu.VMEM((FF_ROWS, D_MODEL), F32)],
        compiler_params=_cp(("parallel", "arbitrary")), name="ffn_dh")(d_a, d_b, w_gate_t, w_up_t)


def _local_step(x, tgt, wts, small, ffn_grads, mixer_grads, small_grads):
    s = SEQ
    cos_f, sin_s = [_to_phase_rows(t) for t in _rope_tables()]
    x = _reorder_rows(x, to_phase=True, name="phase_rows_x")
    tgt = _reorder_rows(tgt, to_phase=True, name="phase_rows_target")

    chip, w_own, w_others = wts["w_in_parts"]
    proj_own = _project_in_own(chip, x, w_own, cos_f, sin_s)
    proj = _project_in(chip, x, w_others, cos_f, sin_s, proj_own)

    attn, lse = _attention_fwd(proj)

    (abar_r, abar_i, bbar_r, bbar_i), ssm_vjp = jax.vjp(
        _ssm_discretise, small["ssm_a_re"], small["ssm_a_im"], small["ssm_log_dt"], small["ssm_b_re"], small["ssm_b_im"])
    b_in_r, b_in_i = [_slab_block_diag(b.transpose(0, 2, 1)).astype(BF16) for b in (bbar_r, bbar_i)]
    c_out_r = _slab_block_diag(small["ssm_c_re"].transpose(0, 2, 1)).astype(BF16)
    c_out_ni = _slab_block_diag(-small["ssm_c_im"].transpose(0, 2, 1)).astype(BF16)
    a_r, a_i = abar_r.reshape(1, SSM_LANES), abar_i.reshape(1, SSM_LANES)
    d_skip = small["ssm_d"].reshape(1, SSM_WIDTH)

    u_f = _to_scan_rows(proj[:, 3 * QKV_WIDTH:3 * QKV_WIDTH + SSM_WIDTH])
    u_p = u_f.astype(BF16)
    y_c, h_r, h_i, e_r, e_i = _ssm_forward(u_p, b_in_r, b_in_i, c_out_r, c_out_ni, a_r, a_i)

    def branch(t, wg):
        return jnp.concatenate([jnp.dot(t, wg[k], preferred_element_type=F32) for k in range(N_CHIPS)], axis=1)

    def branch_t(t, wg):
        ns = wg.shape[2]
        return sum(_dot_nt(t[:, k * ns:(k + 1) * ns], wg[k]) for k in range(N_CHIPS))

    def gelu_glu(yc, u, dsk, wg):
        y = yc + dsk * u
        gel = (0.5 * y * (1.0 + jnp.tanh(GELU_C * (y + GELU_K * y * y * y)))).astype(BF16)
        glu = branch(gel, wg)
        return y, gel, glu, glu[:, :SSM_WIDTH] * jax.nn.sigmoid(glu[:, SSM_WIDTH:])

    y_s5, gel, glu, y_glu = _rowwise(
        gelu_glu, [y_c, u_f], [d_skip, wts["w_glu"]],
        [_sds((s, SSM_WIDTH)), _sds((s, SSM_WIDTH), BF16), _sds((s, 2 * SSM_WIDTH)), _sds((s, SSM_WIDTH), BF16)],
        tm=512, name="ssm_gelu_glu")
    y_glu = _from_scan_rows(y_glu)

    gl0 = (proj, D_MODEL, (3 * QKV_WIDTH + SSM_WIDTH) // D_MODEL)
    gl1 = (proj, D_MODEL, (3 * QKV_WIDTH + SSM_WIDTH) // D_MODEL + 1)
    b_gate = small["b_gate"]
    w_out = wts["w_out"].reshape(D_MODEL, D_MODEL)

    def mix_ln1(l0, l1, at, yg, xv, bg, wa, ws, wo, g, b):
        ya = branch(at.astype(BF16), wa)
        ys = branch(yg, ws)
        mixed = (jax.nn.sigmoid(l0 + bg[0:1]) * ya + jax.nn.sigmoid(l1 + bg[1:2]) * ys).astype(BF16)
        z = DN_ALPHA * xv + jnp.dot(mixed, wo, preferred_element_type=F32)
        xhat, _ = _ln_stats(z)
        return ya, ys, mixed, z, xhat * g + b

    y_attn, y_ssm, mixed, z1, h = _rowwise(
        mix_ln1, [gl0, gl1, attn, y_glu, x],
        [b_gate, wts["w_attn_br"], wts["w_ssm_br"], w_out, small["ln1_g"], small["ln1_b"]],
        [_sds((s, D_MODEL)), _sds((s, D_MODEL)), _sds((s, D_MODEL), BF16), _sds((s, D_MODEL)), _sds((s, D_MODEL))],
        tm=256, name="mix_ln1")

    nf = D_FF // N_CHIPS
    w_gate_t, w_up_t, w_down = wts["w_ff_gate"], wts["w_ff_up"], wts["w_ff_down"]
    ff_a, ff_b, act = _ffn_up(h, w_gate_t, w_up_t)
    dz2, loss_v, d_ln2_g, d_ln2_b = _ffn_down_ln2_loss(act, w_down, h, tgt, small["ln2_g"], small["ln2_b"])

    d_a, d_b = _ffn_down_bwd(dz2, w_down, ff_a, ff_b)

    def grad_rows(lhs, rhs, name):
        return _matmul(lhs, rhs, grid=(N_CHIPS,), a_spec=pl.BlockSpec((None, s, nf), lambda k: (k, 0, 0)),
                       b_spec=pl.BlockSpec((s, D_MODEL), lambda k: (0, 0)),
                       o_spec=pl.BlockSpec((None, nf, D_MODEL), lambda k: (k, 0, 0)),
                       out_shape=_sds((N_CHIPS, nf, D_MODEL), BF16), dims=(0, 0), name=name)

    g_w_ff_down = grad_rows(act, dz2, "g_w_ff_down")
    g_w_ff_gate = grad_rows(d_a, h, "g_w_ff_gate")
    g_w_ff_up = grad_rows(d_b, h, "g_w_ff_up")
    dh_ff = _ffn_dh(d_a, d_b, w_gate_t, w_up_t)

    def ln1_gate_bwd(dz, dff, z, l0, l1, ya, ys, g, bg, wo, wa, ws):
        xhat, rstd = _ln_stats(z)
        dh = DN_ALPHA * dz + dff
        dz_in = _ln_bwd(dh, xhat, rstd, g)
        dm = _dot_nt(dz_in.astype(BF16), wo)
        g0 = jax.nn.sigmoid(l0 + bg[0:1])
        g1 = jax.nn.sigmoid(l1 + bg[1:2])
        dl0 = dm * ya * g0 * (1.0 - g0)
        dl1 = dm * ys * g1 * (1.0 - g1)
        dya, dys = (dm * g0).astype(BF16), (dm * g1).astype(BF16)
        return (dz_in, dya, dys, jnp.concatenate([dl0, dl1], axis=1), branch_t(dya, wa), branch_t(dys, ws),
                _colsum(dh * xhat), _colsum(dh), _colsum(dl0), _colsum(dl1))

    dz1, d_y_attn, d_y_ssm, d_gl, d_attn, d_y_glu, d_ln1_g, d_ln1_b, d_bg0, d_bg1 = _rowwise(
        ln1_gate_bwd, [dz2, dh_ff, z1, gl0, gl1, y_attn, y_ssm],
        [small["ln1_g"], b_gate, w_out, wts["w_attn_br"], wts["w_ssm_br"]],
        [_sds((s, D_MODEL)), _sds((s, D_MODEL), BF16), _sds((s, D_MODEL), BF16), _sds((s, 2 * D_MODEL), BF16),
         _sds((s, ATTN_WIDTH)), _sds((s, SSM_WIDTH))],
        [_sds((1, D_MODEL))] * 4, tm=256, name="ln1_gate_bwd", after=(g_w_ff_down, g_w_ff_gate, g_w_ff_up))
    ffn_sent = ffn_grads({"w_ff_down": g_w_ff_down, "w_ff_gate": g_w_ff_gate, "w_ff_up": g_w_ff_up}, dz1)
    g_w_out = _mm_rows_tn(mixed, dz1, name="g_w_out")

    g_w_ssm_br = _mm_cols_tn(y_glu, d_y_ssm, ns=D_MODEL // N_CHIPS, name="g_w_ssm_br")
    d_y_glu = _to_scan_rows(d_y_glu)

    def glu_gelu_bwd(dyg, gl, y, u, dsk, wg):
        ga, gb = gl[:, :SSM_WIDTH], gl[:, SSM_WIDTH:]
        sg = jax.nn.sigmoid(gb)
        d_gl = jnp.concatenate([dyg * sg, dyg * ga * sg * (1.0 - sg)], axis=1).astype(BF16)
        dg = branch_t(d_gl, wg)
        th = jnp.tanh(GELU_C * (y + GELU_K * y * y * y))
        dy = dg * (0.5 * (1.0 + th) + 0.5 * y * (1.0 - th * th) * GELU_C * (1.0 + 3.0 * GELU_K * y * y))
        return d_gl, dy, dy * dsk, _colsum(dy * u)

    d_glu, d_y, d_u_skip, d_ssm_d = _rowwise(
        glu_gelu_bwd, [d_y_glu, glu, y_s5, u_f], [d_skip, wts["w_glu"]],
        [_sds((s, 2 * SSM_WIDTH), BF16), _sds((s, SSM_WIDTH), BF16), _sds((s, SSM_WIDTH))], [_sds((1, SSM_WIDTH))],
        tm=512, name="glu_gelu_bwd", after=tuple(ffn_sent))
    g_w_glu = _mm_cols_tn(gel, d_glu, ns=2 * SSM_WIDTH // N_CHIPS, name="g_w_glu")
    d_u, d_abar_r, d_abar_i, d_c_r, d_c_ni, d_bin_r, d_bin_i = _ssm_backward(
        d_y, d_u_skip, u_p, h_r, h_i, e_r, e_i, b_in_r, b_in_i, c_out_r, c_out_ni, a_r, a_i)
    d_u = _from_scan_rows(d_u)
    d_bbar_r = d_bin_r.reshape(SSM_GROUPS, SSM_GROUP, SSM_STATE).transpose(0, 2, 1)
    d_bbar_i = d_bin_i.reshape(SSM_GROUPS, SSM_GROUP, SSM_STATE).transpose(0, 2, 1)
    d_a_re, d_a_im, d_log_dt, d_b_re, d_b_im = ssm_vjp(
        (d_abar_r.reshape(SSM_GROUPS, SSM_STATE), d_abar_i.reshape(SSM_GROUPS, SSM_STATE), d_bbar_r, d_bbar_i))
    d_c_re = d_c_r.reshape(SSM_GROUPS, SSM_GROUP, SSM_STATE)
    d_c_im = -d_c_ni.reshape(SSM_GROUPS, SSM_GROUP, SSM_STATE)

    g_w_attn_br = _mm_cols_tn(attn, d_y_attn, ns=D_MODEL // N_CHIPS, name="g_w_attn_br")
    mixer_grads({"w_out": g_w_out, "w_ssm_br": g_w_ssm_br, "w_glu": g_w_glu, "w_attn_br": g_w_attn_br}, d_abar_r)
    small_g = {"b_gate": jnp.concatenate([d_bg0, d_bg1], axis=0), "ssm_a_re": d_a_re, "ssm_a_im": d_a_im,
               "ssm_log_dt": d_log_dt, "ssm_b_re": d_b_re, "ssm_b_im": d_b_im, "ssm_c_re": d_c_re, "ssm_c_im": d_c_im,
               "ssm_d": d_ssm_d.reshape(SSM_WIDTH), "ln1_g": d_ln1_g, "ln1_b": d_ln1_b, "ln2_g": d_ln2_g,
               "ln2_b": d_ln2_b}
    shared = small_grads(small_g, loss_v[0, 0])
    d_proj = _attention_bwd(proj, cos_f, sin_s, d_attn, attn, lse, d_u, d_gl)

    g_w_in = _mm_cols_tn(x, d_proj, ns=IN_WIDTH // N_CHIPS, name="g_w_in", after=tuple(shared))

    def grad_x_after(after):
        dx_proj = _mm_cols_nt(d_proj, wts["w_in"], tm=1024, name="dx_proj", after=after)
        return _reorder_rows(dz1, dx_proj, to_phase=False, name="grad_x", scale=DN_ALPHA)

    return grad_x_after, g_w_in, d_proj


GATHER_ID, SWAP_ID, SCATTER_ID, JOIN_ID, EXCHANGE_ID = 1, 2, 3, 4, 5


def _place():
    return lax.axis_index("x"), lax.axis_index("y"), lax.axis_index("c")


def _other_chips(x, y):
    return [(1 - x, y), (x, 1 - y), (1 - x, 1 - y)]


def _handshake(peers):
    barrier = pltpu.get_barrier_semaphore()
    for peer in peers:
        pl.semaphore_signal(barrier, inc=1, device_id=peer, device_id_type=MESH)
    pl.semaphore_wait(barrier, len(peers))


def _sequencer(body, arrays, out_type, sems, collective_id, name):
    return pl.kernel(body, name=name, out_type=out_type,
                     mesh=plsc.ScalarSubcoreMesh(axis_name="sequencer", num_cores=1), scratch_types=sems,
                     compiler_params=pltpu.CompilerParams(collective_id=collective_id))(*arrays)


def _gather_weights(shards, *, name, own_slot=True):
    nw = len(shards)

    def body(*refs):
        ins, outs = refs[:nw], refs[nw:2 * nw]
        send_sems, recv_sems, pass_send, pass_recv, local_sems = refs[2 * nw:]
        x, y, c = _place()
        chip = 2 * x + y
        chips = _other_chips(x, y)
        _handshake([(x, y, 1 - c)] + [(cx, cy, c) for cx, cy in chips])
        started, local = [], []
        for w in range(nw):
            hw = shards[w].shape[0] // 2
            mine = pl.ds(c * hw, hw)
            if own_slot:
                own = pltpu.make_async_copy(ins[w], outs[w].at[chip], local_sems.at[w])
                own.start()
                local.append(own)
            for j, (cx, cy) in enumerate(chips):
                cp = pltpu.make_async_remote_copy(
                    src_ref=ins[w].at[mine], dst_ref=outs[w].at[chip, mine], send_sem=send_sems.at[w, j],
                    recv_sem=recv_sems.at[w, j], device_id=(cx, cy, c), device_id_type=MESH)
                cp.start()
                started.append(cp)
        passed = []
        for w in range(nw):
            hw = shards[w].shape[0] // 2
            mine = pl.ds(c * hw, hw)
            for j, (cx, cy) in enumerate(chips):
                landed = outs[w].at[2 * cx + cy, mine]
                pltpu.make_async_remote_copy(
                    src_ref=ins[w].at[mine], dst_ref=landed, send_sem=send_sems.at[w, j],
                    recv_sem=recv_sems.at[w, j], device_id=(cx, cy, c), device_id_type=MESH).wait_recv()
                cp = pltpu.make_async_remote_copy(
                    src_ref=landed, dst_ref=landed, send_sem=pass_send.at[w, j], recv_sem=pass_recv.at[w, j],
                    device_id=(x, y, 1 - c), device_id_type=MESH)
                cp.start()
                passed.append(cp)
        for w in range(nw):
            hw = shards[w].shape[0] // 2
            theirs = pl.ds((1 - c) * hw, hw)
            for j, (cx, cy) in enumerate(chips):
                landed = outs[w].at[2 * cx + cy, theirs]
                pltpu.make_async_remote_copy(
                    src_ref=landed, dst_ref=landed, send_sem=pass_send.at[w, j], recv_sem=pass_recv.at[w, j],
                    device_id=(x, y, 1 - c), device_id_type=MESH).wait_recv()
        for cp in local:
            cp.wait()
        for cp in started + passed:
            cp.wait_send()

    sem = pltpu.SemaphoreType.DMA
    return _sequencer(body, shards, [_sds((N_CHIPS,) + a.shape, a.dtype) for a in shards],
                      [sem((nw, 3)), sem((nw, 3)), sem((nw, 3)), sem((nw, 3)), sem((nw,))], GATHER_ID, name)


def _swap_other_halves(grads, *, name):
    nw = len(grads)

    def body(*refs):
        ins, outs = refs[:nw], refs[nw:2 * nw]
        send_sems, recv_sems = refs[2 * nw:]
        x, y, c = _place()
        _handshake([(x, y, 1 - c)])
        cps = []
        for w in range(nw):
            hw = grads[w].shape[1] // 2
            cp = pltpu.make_async_remote_copy(
                src_ref=ins[w].at[:, pl.ds((1 - c) * hw, hw)], dst_ref=outs[w], send_sem=send_sems.at[w],
                recv_sem=recv_sems.at[w], device_id=(x, y, 1 - c), device_id_type=MESH)
            cp.start()
            cps.append(cp)
        for cp in cps:
            cp.wait()

    sem = pltpu.SemaphoreType.DMA
    return _sequencer(body, grads, [_sds((N_CHIPS, g.shape[1] // 2, g.shape[2]), g.dtype) for g in grads],
                      [sem((nw,)), sem((nw,))], SWAP_ID, name)


def _add_my_halves(core, grads, others, *, name, after=()):
    nw = len(grads)
    halves = [g.shape[1] // 2 for g in grads]

    def body(core_ref, *refs):
        outs = refs[2 * nw + len(after):]
        for g_ref, o_ref, out_ref in zip(refs[:nw], refs[nw:2 * nw], outs):
            out_ref[...] = (g_ref[...].astype(F32) + o_ref[...].astype(F32)).astype(out_ref.dtype)

    in_specs = [pl.BlockSpec((None, None, hw, g.shape[2]), lambda s, core_ref: (s, core_ref[0], 0, 0))
                for g, hw in zip(grads, halves)]
    in_specs += [pl.BlockSpec((None, hw, g.shape[2]), lambda s, core_ref: (s, 0, 0)) for g, hw in zip(grads, halves)]
    return pl.pallas_call(
        body,
        grid_spec=pltpu.PrefetchScalarGridSpec(
            num_scalar_prefetch=1, grid=(N_CHIPS,), in_specs=in_specs + [HBM_OPERAND] * len(after),
            out_specs=[pl.BlockSpec((None, hw, g.shape[2]), lambda s, core_ref: (s, 0, 0))
                       for g, hw in zip(grads, halves)]),
        out_shape=[_sds((N_CHIPS, hw, g.shape[2]), BF16) for g, hw in zip(grads, halves)],
        compiler_params=_cp(("parallel",)), name=name)(
            core, *[g.reshape(N_CHIPS, 2, hw, g.shape[2]) for g, hw in zip(grads, halves)], *others, *after)


def _scatter_partials(parts, *, name):
    nw = len(parts)

    def body(*refs):
        ins, outs = refs[:nw], refs[nw:2 * nw]
        send_sems, recv_sems = refs[2 * nw:]
        x, y, c = _place()
        _handshake([(cx, cy, c) for cx, cy in _other_chips(x, y)])
        cps = []
        for w in range(nw):
            for j, (cx, cy) in enumerate(_other_chips(x, y)):
                cp = pltpu.make_async_remote_copy(
                    src_ref=ins[w].at[2 * cx + cy], dst_ref=outs[w].at[j], send_sem=send_sems.at[w, j],
                    recv_sem=recv_sems.at[w, j], device_id=(cx, cy, c), device_id_type=MESH)
                cp.start()
                cps.append(cp)
        for cp in cps:
            cp.wait()

    sem = pltpu.SemaphoreType.DMA
    return _sequencer(body, parts, [_sds((3,) + p.shape[1:], p.dtype) for p in parts],
                      [sem((nw, 3)), sem((nw, 3))], SCATTER_ID, name)


SUM_STEPS = 2


def _sum_partials(chip, parts, recvd, *, name, after=()):
    nw = len(parts)
    rows = [p.shape[1] // SUM_STEPS for p in parts]

    def body(chip_ref, *refs):
        outs = refs[2 * nw + len(after):]
        for p_ref, r_ref, out_ref in zip(refs[:nw], refs[nw:2 * nw], outs):
            acc = p_ref[...].astype(F32)
            for j in range(3):
                acc = acc + r_ref[j].astype(F32)
            out_ref[...] = acc

    in_specs = [pl.BlockSpec((None, th, p.shape[2]), lambda i, chip_ref: (chip_ref[0], i, 0))
                for p, th in zip(parts, rows)]
    in_specs += [pl.BlockSpec((3, th, p.shape[2]), lambda i, chip_ref: (0, i, 0)) for p, th in zip(parts, rows)]
    return pl.pallas_call(
        body,
        grid_spec=pltpu.PrefetchScalarGridSpec(
            num_scalar_prefetch=1, grid=(SUM_STEPS,), in_specs=in_specs + [HBM_OPERAND] * len(after),
            out_specs=[pl.BlockSpec((th, p.shape[2]), lambda i, chip_ref: (i, 0)) for p, th in zip(parts, rows)]),
        out_shape=[_sds(p.shape[1:]) for p in parts], compiler_params=_cp(("parallel",)), name=name)(
            chip, *parts, *recvd, *after)


def _swap_reduced_halves(halves, *, name):
    nw = len(halves)

    def body(*refs):
        ins, outs = refs[:nw], refs[nw:2 * nw]
        send_sems, recv_sems = refs[2 * nw:]
        x, y, c = _place()
        _handshake([(x, y, 1 - c)])
        cps = []
        for w in range(nw):
            cp = pltpu.make_async_remote_copy(
                src_ref=ins[w], dst_ref=outs[w], send_sem=send_sems.at[w], recv_sem=recv_sems.at[w],
                device_id=(x, y, 1 - c), device_id_type=MESH)
            cp.start()
            cps.append(cp)
        for cp in cps:
            cp.wait()

    sem = pltpu.SemaphoreType.DMA
    return _sequencer(body, halves, [_sds(h.shape, h.dtype) for h in halves], [sem((nw,)), sem((nw,))], JOIN_ID, name)


def _exchange_rows(vec, *, name):
    def body(v_ref, slots, send_sems, recv_sems, local_sem):
        x, y, c = _place()
        me = 4 * x + 2 * y + c
        peers = []
        for mask in range(1, N_DEV):
            peers.append((1 - x if mask & 4 else x, 1 - y if mask & 2 else y, 1 - c if mask & 1 else c))
        _handshake(peers)
        own = pltpu.make_async_copy(v_ref, slots.at[me], local_sem)
        own.start()
        cps = []
        for k, peer in enumerate(peers):
            cp = pltpu.make_async_remote_copy(
                src_ref=v_ref, dst_ref=slots.at[me], send_sem=send_sems.at[k], recv_sem=recv_sems.at[k],
                device_id=peer, device_id_type=MESH)
            cp.start()
            cps.append(cp)
        for k, (px, py, pc) in enumerate(peers):
            pltpu.make_async_remote_copy(
                src_ref=v_ref, dst_ref=slots.at[4 * px + 2 * py + pc], send_sem=send_sems.at[k],
                recv_sem=recv_sems.at[k], device_id=(px, py, pc), device_id_type=MESH).wait_recv()
        for cp in cps:
            cp.wait_send()
        own.wait()

    sem = pltpu.SemaphoreType.DMA
    return _sequencer(body, [vec], [_sds((N_DEV,) + vec.shape)], [sem((N_DEV - 1,)), sem((N_DEV - 1,)), sem(())],
                      EXCHANGE_ID, name)[0]


def _sum_slots(slots, *, name, after=()):
    def body(s_ref, *rest):
        out_ref = rest[len(after)]
        acc = s_ref[0]
        for d in range(1, N_DEV):
            acc = acc + s_ref[d]
        out_ref[...] = acc

    vmem = pl.BlockSpec(memory_space=pltpu.VMEM)
    return pl.pallas_call(
        body, in_specs=[vmem] + [HBM_OPERAND] * len(after), out_specs=vmem, out_shape=_sds(slots.shape[1:]),
        compiler_params=pltpu.CompilerParams(vmem_limit_bytes=VMEM_LIMIT_BYTES), name=name)(slots, *after)


def _reduce_scatter_start(grads, core, *, tag, add_after=()):
    others = _swap_other_halves(grads, name="swap_other_halves_" + tag)
    parts = _add_my_halves(core, grads, others, name="add_my_halves_" + tag, after=add_after)
    return parts, _scatter_partials(parts, name="scatter_partials_" + tag)


def _reduce_scatter_finish(parts, recvd, chip, *, tag, sum_after=()):
    mine = _sum_partials(chip, parts, recvd, name="sum_partials_" + tag, after=sum_after)
    return mine, _swap_reduced_halves(mine, name="swap_reduced_halves_" + tag)


ADAM_BLOCK_ELEMS = 256 * 1024


def _adam_rows(rows, cols):
    tm = rows
    while tm * cols > ADAM_BLOCK_ELEMS and tm % 16 == 0:
        tm //= 2
    return tm


def _adam_step(wv, gv, mv, vv):
    m2 = ADAM_B1 * mv + (1.0 - ADAM_B1) * gv
    v2 = ADAM_B2 * vv + (1.0 - ADAM_B2) * (gv * gv)
    m_hat = m2 / (1.0 - ADAM_B1 ** ADAM_STEP)
    v_hat = v2 / (1.0 - ADAM_B2 ** ADAM_STEP)
    return -ADAM_LR * (m_hat / (jnp.sqrt(v_hat) + ADAM_EPS) + ADAM_WD * wv), m2, v2


def _adamw_each(ws, gs, ms, vs, *, name, after=()):
    n = len(ws)
    whole = pl.BlockSpec(memory_space=pltpu.VMEM)

    def body(*refs):
        ins, outs = refs[:4 * n], refs[4 * n + len(after):]
        for i in range(n):
            res = _adam_step(*(ins[k * n + i][...] for k in range(4)))
            for k in range(3):
                outs[k * n + i][...] = res[k]

    out = pl.pallas_call(body, in_specs=[whole] * (4 * n) + [HBM_OPERAND] * len(after),
                         out_shape=[_sds(w.shape) for w in ws] * 3, name=name)(*ws, *gs, *ms, *vs, *after)
    return out[:n], out[n:2 * n], out[2 * n:]


def _adamw_halves(core, w, g_mine, g_theirs, m, v, *, name, after=()):
    rows, cols = w.shape
    hw = rows // 2
    tm = _adam_rows(hw, cols)
    per_half = hw // tm

    def body(core_ref, w_ref, gm_ref, gt_ref, m_ref, v_ref, *rest):
        g_out, d_out, m_out, v_out = rest[len(after):]
        mine = (pl.program_id(0) // per_half) == core_ref[0]
        g = jnp.where(mine, gm_ref[...], gt_ref[...])
        d, m2, v2 = _adam_step(w_ref[...], g, m_ref[...], v_ref[...])
        g_out[...] = g
        d_out[...] = d
        m_out[...] = m2
        v_out[...] = v2

    full = pl.BlockSpec((tm, cols), lambda i, core_ref: (i, 0))

    def half(wanted):
        def index(i, core_ref):
            in_use = ((i // per_half) == core_ref[0]) == wanted
            return (jnp.where(in_use, i % per_half, 0), 0)
        return pl.BlockSpec((tm, cols), index)

    return pl.pallas_call(
        body,
        grid_spec=pltpu.PrefetchScalarGridSpec(
            num_scalar_prefetch=1, grid=(rows // tm,),
            in_specs=[full, half(True), half(False), full, full] + [HBM_OPERAND] * len(after),
            out_specs=[full, full, full, full]),
        out_shape=[_sds((rows, cols))] * 4, compiler_params=_cp(("parallel",)), name=name)(
            core, w, g_mine, g_theirs, m, v, *after)


HELD_TRANSPOSED = ("w_ff_gate", "w_ff_up")


def _as_rows(name, arr):
    return arr[0].T if name in HELD_TRANSPOSED else arr[0]


def _from_rows(name, arr2d):
    return (arr2d.T if name in HELD_TRANSPOSED else arr2d)[None]


STORED_SWAPPED = ("ssm_b_re", "ssm_b_im")


def _as_stored(name, arr):
    return jnp.swapaxes(arr, -1, -2) if name in STORED_SWAPPED else arr


def _pack_rows(arrs):
    flat = jnp.concatenate([a.reshape(-1).astype(F32) for a in arrs])
    rows = -(-flat.shape[0] // 1024) * 8
    return jnp.pad(flat, (0, rows * 128 - flat.shape[0])).reshape(rows, 128)


def _unpack_rows(vec, shapes):
    flat = vec.reshape(-1)
    out, off = [], 0
    for shp in shapes:
        size = math.prod(shp)
        out.append(flat[off:off + size].reshape(shp))
        off += size
    return out


SMALL = ("b_gate", "ssm_a_re", "ssm_a_im", "ssm_log_dt", "ssm_b_re", "ssm_b_im", "ssm_c_re", "ssm_c_im", "ssm_d",
         "ln1_g", "ln1_b", "ln2_g", "ln2_b")
GATHER_GROUPS = (("w_in", ("w_in",)), ("mixer", ("w_attn_br", "w_ssm_br", "w_glu", "w_out")),
                 ("ffn_up", ("w_ff_gate", "w_ff_up")), ("ffn_down", ("w_ff_down",)))
REDUCE_GROUPS = (("ffn", ("w_ff_down", "w_ff_gate", "w_ff_up")),
                 ("mixer", ("w_out", "w_ssm_br", "w_glu", "w_attn_br")), ("w_in", ("w_in",)))
WEIGHTS = ("w_in", "b_gate", "w_attn_br", "w_ssm_br", "w_out", "ssm_a_re", "ssm_a_im", "ssm_log_dt", "ssm_b_re",
           "ssm_b_im", "ssm_c_re", "ssm_c_im", "ssm_d", "w_glu", "ln1_g", "ln1_b", "w_ff_gate", "w_ff_up", "w_ff_down",
           "ln2_g", "ln2_b")


def kernel(x, w_in, b_gate, w_attn_br, w_ssm_br, w_out, ssm_a_re, ssm_a_im, ssm_log_dt, ssm_b_re, ssm_b_im, ssm_c_re, ssm_c_im, ssm_d, w_glu, ln1_g, ln1_b, w_ff_gate, w_ff_up, w_ff_down, ln2_g, ln2_b, loss_target, m_w_in, m_b_gate, m_w_attn_br, m_w_ssm_br, m_w_out, m_ssm_a_re, m_ssm_a_im, m_ssm_log_dt, m_ssm_b_re, m_ssm_b_im, m_ssm_c_re, m_ssm_c_im, m_ssm_d, m_w_glu, m_ln1_g, m_ln1_b, m_w_ff_gate, m_w_ff_up, m_w_ff_down, m_ln2_g, m_ln2_b, v_w_in, v_b_gate, v_w_attn_br, v_w_ssm_br, v_w_out, v_ssm_a_re, v_ssm_a_im, v_ssm_log_dt, v_ssm_b_re, v_ssm_b_im, v_ssm_c_re, v_ssm_c_im, v_ssm_d, v_w_glu, v_ln1_g, v_ln1_b, v_w_ff_gate, v_w_ff_up, v_w_ff_down, v_ln2_g, v_ln2_b):
    given = dict(locals())
    px, py, pc = _place()
    chip = 2 * px + py
    core_s = jnp.reshape(pc, (1,)).astype(jnp.int32)
    chip_s = jnp.reshape(chip, (1,)).astype(jnp.int32)

    wts = {}
    for tag, names in GATHER_GROUPS:
        shards = [_as_rows(n, given[n]).astype(BF16) for n in names]
        first = tag == GATHER_GROUPS[0][0]
        slots = _gather_weights(shards, name="gather_" + tag, own_slot=not first)
        if first:
            wts["w_in_parts"] = (chip_s, shards[0], slots[0])
            slots = [lax.dynamic_update_slice(g, s[None], (chip, 0, 0)) for g, s in zip(slots, shards)]
        wts.update(zip(names, slots))
    ncol = D_MODEL // N_CHIPS
    bg_mine = jnp.where(pc == 0, b_gate[0], jnp.zeros_like(b_gate[0]))
    bg_full = lax.dynamic_update_slice(jnp.zeros((2, D_MODEL), F32), bg_mine, (0, chip * ncol))
    bg_slots = _exchange_rows(bg_full.reshape(16, 128), name="exchange_gate_bias")
    bg_full = _sum_slots(bg_slots, name="sum_gate_bias").reshape(2, D_MODEL)
    small = {n: given[n][0] for n in SMALL if n.startswith("ssm")}
    small.update({n: given[n] for n in ("ln1_g", "ln1_b", "ln2_g", "ln2_b")})
    small["b_gate"] = bg_full

    groups = dict(REDUCE_GROUPS)
    parts, recvd, reduced, sent = {}, {}, {}, {}
    grads, delta, new_m, new_v, done = {}, {}, {}, {}, {}

    def start(tag, big_g, add_after):
        parts[tag], recvd[tag] = _reduce_scatter_start([big_g[n] for n in groups[tag]], core_s, tag=tag,
                                                       add_after=add_after)
        return parts[tag]

    def reduce_sum(tag, after):
        reduced[tag] = _reduce_scatter_finish(parts[tag], recvd[tag], chip_s, tag=tag, sum_after=after)
        return reduced[tag][0]

    def adam(tag, after):
        for n, g_mine, g_theirs in zip(groups[tag], *reduced[tag]):
            res = _adamw_halves(core_s, _as_rows(n, given[n]), g_mine, g_theirs, _as_rows(n, given["m_" + n]),
                                _as_rows(n, given["v_" + n]), name="adamw_" + n, after=after)
            done[n] = res[1]
            grads[n], delta[n], new_m[n], new_v[n] = [_from_rows(n, r) for r in res]

    def ffn_grads(big_g, norm_bwd):
        return start("ffn", big_g, (norm_bwd,))

    def mixer_grads(big_g, scan_bwd):
        return start("mixer", big_g, (scan_bwd, *reduce_sum("ffn", (scan_bwd,))))

    def small_grads(small_g, loss_mine):
        sent["stored"] = [_as_stored(n, small_g[n]) for n in SMALL] + [loss_mine.reshape(1)]
        packed = _pack_rows(sent["stored"])
        sent["slots"] = _exchange_rows(packed, name="exchange_small")
        return (packed,)

    grad_x_after, g_w_in, attention_bwd = _local_step(x[0], loss_target[0], wts, small,
                                                      ffn_grads, mixer_grads, small_grads)

    reduce_sum("mixer", (attention_bwd,))
    summed = _sum_slots(sent["slots"], name="sum_small", after=(g_w_in,))
    adam("mixer", (g_w_in,))
    start("w_in", {"w_in": g_w_in}, (summed, *[done[n] for n in groups["mixer"]]))
    in_flight = (parts["w_in"][0],)
    grad_x = grad_x_after(in_flight)
    adam("ffn", in_flight)
    summed = _unpack_rows(summed, [a.shape for a in sent["stored"]])
    loss = summed.pop()[0]
    at = SMALL.index("b_gate")
    summed[at] = lax.dynamic_slice(summed[at], (0, chip * ncol), (2, ncol))
    summed = [g.reshape(1, -1) if g.ndim == 1 else g for g in summed]
    held = [[_as_stored(n, given[prefix + n]).reshape(g.shape) for n, g in zip(SMALL, summed)]
            for prefix in ("", "m_", "v_")]
    small_out = _adamw_each(held[0], summed, held[1], held[2], name="adamw_small", after=in_flight)
    for out, arrs in zip((grads, delta, new_m, new_v), (summed, *small_out)):
        out.update((n, _as_stored(n, a).reshape(given[n].shape)) for n, a in zip(SMALL, arrs))
    reduce_sum("w_in", (*[done[n] for n in groups["ffn"]], small_out[0][0], grad_x))
    adam("w_in", ())

    return (loss, grad_x.reshape(x.shape), *[grads[n] for n in WEIGHTS], *[delta[n] for n in WEIGHTS],
            *[new_m[n] for n in WEIGHTS], *[new_v[n] for n in WEIGHTS])
```

```python
import math

import jax
import jax.numpy as jnp
from jax import lax
from jax.experimental import pallas as pl
from jax.experimental.pallas import tpu as pltpu
from jax.experimental.pallas import tpu_sc as plsc

F32 = jnp.float32
BF16 = jnp.bfloat16
MESH = pl.DeviceIdType.MESH

D_MODEL = 1024
SEQ = 2048
HEAD_DIM = 64
ATTN_HEADS = 8
DILATIONS = (1, 4, 16)
ATTN_WIDTH = ATTN_HEADS * HEAD_DIM
QKV_WIDTH = 3 * ATTN_WIDTH
BLOCK = 128
ROPE_THETA = 10000.0
NEG_INF = -1e30
SSM_GROUP = 16
SSM_GROUPS = 32
SSM_WIDTH = 512
SSM_STATE = 64
SSM_LANES = SSM_GROUPS * SSM_STATE
SCAN_CHUNKS = 8
SCAN_STEPS = SEQ // SCAN_CHUNKS
IN_WIDTH = 3 * QKV_WIDTH + SSM_WIDTH + 2 * D_MODEL
D_FF = 2816
N_CHIPS = 4
N_DEV = 8
DN_ALPHA = 2.0 ** 0.25
LN_EPS = 1e-5
ADAM_LR = 0.001
ADAM_B1 = 0.9
ADAM_B2 = 0.999
ADAM_EPS = 1e-08
ADAM_WD = 0.01
ADAM_STEP = 10
GELU_C = math.sqrt(2.0 / math.pi)
GELU_K = 0.044715

VMEM_LIMIT_BYTES = 56 * 1024 * 1024


def _sds(shape, dtype=F32):
    return jax.ShapeDtypeStruct(tuple(shape), dtype)


def _cp(semantics=None):
    return pltpu.CompilerParams(dimension_semantics=semantics, vmem_limit_bytes=VMEM_LIMIT_BYTES)


HBM_OPERAND = pl.BlockSpec(memory_space=pl.ANY)


def _matmul(a, b, *, grid, a_spec, b_spec, o_spec, out_shape, dims, k_axis=None, name, after=()):
    nk = grid[k_axis] if k_axis is not None else 1
    o_block = tuple(d for d in o_spec.block_shape if d is not None)
    n_after = len(after)

    def body(a_ref, b_ref, *rest):
        o_ref, acc = rest[n_after], rest[n_after + 1:]
        part = lax.dot_general(a_ref[...].astype(BF16), b_ref[...].astype(BF16),
                               (((dims[0],), (dims[1],)), ((), ())), preferred_element_type=F32)
        if k_axis is None:
            o_ref[...] = part.astype(o_ref.dtype)
        else:
            k = pl.program_id(k_axis)

            @pl.when(k == 0)
            def _():
                acc[0][...] = part

            @pl.when(k > 0)
            def _():
                acc[0][...] += part

            @pl.when(k == nk - 1)
            def _():
                o_ref[...] = acc[0][...].astype(o_ref.dtype)

    sem = tuple("arbitrary" if ax == k_axis else "parallel" for ax in range(len(grid)))
    return pl.pallas_call(
        body, grid=grid, in_specs=[a_spec, b_spec] + [HBM_OPERAND] * n_after, out_specs=o_spec, out_shape=out_shape,
        scratch_shapes=[pltpu.VMEM(o_block, F32)] if k_axis is not None else [],
        compiler_params=_cp(sem), name=name)(a, b, *after)


def _mm_cols_nt(dy, wg, *, tm, name, out_dtype=F32, after=()):
    k, ns = wg.shape[1], wg.shape[2]
    m = dy.shape[0]
    a_spec = pl.BlockSpec((tm, ns), lambda i, s: (i, s))
    return _matmul(dy, wg, grid=(m // tm, N_CHIPS), a_spec=a_spec,
                   b_spec=pl.BlockSpec((None, k, ns), lambda i, s: (s, 0, 0)),
                   o_spec=pl.BlockSpec((tm, k), lambda i, s: (i, 0)),
                   out_shape=_sds((m, k), out_dtype), dims=(1, 1), k_axis=1, name=name, after=after)


def _mm_cols_tn(a, dy, *, ns, name, after=()):
    m, k = a.shape
    return _matmul(a, dy, grid=(N_CHIPS,), a_spec=pl.BlockSpec((m, k), lambda s: (0, 0)),
                   b_spec=pl.BlockSpec((m, ns), lambda s: (0, s)),
                   o_spec=pl.BlockSpec((None, k, ns), lambda s: (s, 0, 0)),
                   out_shape=_sds((N_CHIPS, k, ns), BF16), dims=(0, 0), name=name, after=after)


def _mm_rows_tn(a, dy, *, name):
    m, k = a.shape
    rows, n = k // N_CHIPS, dy.shape[1]
    return _matmul(a, dy, grid=(N_CHIPS,), a_spec=pl.BlockSpec((m, rows), lambda s: (0, s)),
                   b_spec=pl.BlockSpec((m, n), lambda s: (0, 0)),
                   o_spec=pl.BlockSpec((None, rows, n), lambda s: (s, 0, 0)),
                   out_shape=_sds((N_CHIPS, rows, n), BF16), dims=(0, 0), name=name)


def _rowwise(fn, tiled, full, outs, accs=(), *, tm, name, after=()):
    args, in_specs = [], []
    for t in tiled:
        if isinstance(t, tuple):
            arr, w, cb = t
            in_specs.append(pl.BlockSpec((tm, w), lambda i, cb=cb: (i, cb)))
        else:
            arr = t
            in_specs.append(pl.BlockSpec((tm, arr.shape[1]), lambda i: (i, 0)))
        args.append(arr)
    rows = args[0].shape[0]
    for f in full:
        in_specs.append(pl.BlockSpec(f.shape, lambda i, nd=f.ndim: (0,) * nd))
        args.append(f)
    out_specs = [pl.BlockSpec((tm, o.shape[1]), lambda i: (i, 0)) for o in outs]
    out_specs += [pl.BlockSpec(a.shape, lambda i, nd=len(a.shape): (0,) * nd) for a in accs]
    n_in, n_out = len(args), len(outs)
    in_specs += [HBM_OPERAND] * len(after)
    first_out = n_in + len(after)

    def body(*refs):
        res = fn(*[r[...] for r in refs[:n_in]])
        res = res if isinstance(res, (tuple, list)) else (res,)
        for r, v in zip(refs[first_out:first_out + n_out], res[:n_out]):
            r[...] = v.astype(r.dtype)
        i = pl.program_id(0)
        for r, v in zip(refs[first_out + n_out:], res[n_out:]):
            @pl.when(i == 0)
            def _(r=r, v=v):
                r[...] = v

            @pl.when(i > 0)
            def _(r=r, v=v):
                r[...] += v

    res = pl.pallas_call(
        body, grid=(rows // tm,), in_specs=in_specs, out_specs=out_specs, out_shape=list(outs) + list(accs),
        compiler_params=_cp(("arbitrary",) if accs else ("parallel",)), name=name)(*args, *after)
    return res


def _colsum(v):
    return jnp.sum(v, axis=0, keepdims=True)


def _ln_stats(z):
    mu = jnp.mean(z, axis=-1, keepdims=True)
    zc = z - mu
    var = jnp.mean(zc * zc, axis=-1, keepdims=True)
    rstd = lax.rsqrt(var + LN_EPS)
    return zc * rstd, rstd


def _ln_bwd(dy, xhat, rstd, g):
    dxh = dy * g
    m1 = jnp.mean(dxh, axis=-1, keepdims=True)
    m2 = jnp.mean(dxh * xhat, axis=-1, keepdims=True)
    return rstd * (dxh - m1 - xhat * m2)


def _swap_halves(t):
    w = t.shape[-1]
    lane = lax.broadcasted_iota(jnp.int32, t.shape, t.ndim - 1)
    return jnp.where((lane % HEAD_DIM) < HEAD_DIM // 2, pltpu.roll(t, w - HEAD_DIM // 2, t.ndim - 1),
                     pltpu.roll(t, HEAD_DIM // 2, t.ndim - 1))


PHASES = max(DILATIONS)
PAIR = 2 * HEAD_DIM
UNITS = SEQ // BLOCK
UNIT_BATCH = 16
ROPE_ROWS = 256
TAIL_COLS = 256


def _to_phase_rows(t):
    return t.reshape(SEQ // PHASES, PHASES, t.shape[1]).transpose(1, 0, 2).reshape(t.shape)


def _reorder_rows(arr, plus=None, *, to_phase, name, scale=1.0):
    def body(*refs):
        o_ref = refs[-1]
        for rho in range(PHASES):
            phase = pl.ds(rho * BLOCK, BLOCK)
            strided = pl.ds(rho, BLOCK, stride=PHASES)
            src, dst = (strided, phase) if to_phase else (phase, strided)
            val = refs[0][src, :]
            if scale != 1.0:
                val = val * scale
            if plus is not None:
                val = val + refs[1][src, :]
            o_ref[dst, :] = val

    spec = pl.BlockSpec((SEQ, BLOCK), lambda j: (0, j))
    ins = [arr] if plus is None else [arr, plus]
    return pl.pallas_call(body, grid=(arr.shape[1] // BLOCK,), in_specs=[spec] * len(ins), out_specs=spec,
                          out_shape=_sds(arr.shape), compiler_params=_cp(("parallel",)), name=name)(*ins)


def _rope(t, cf, ss):
    return t * cf + _swap_halves(t) * ss


def _rope_transposed(d, cf, ss):
    return d * cf + _swap_halves(d * ss)


def _unit_pieces(u, dil):
    pieces, length = PHASES // dil, 8 * dil
    if dil == 1:
        rho, i = 0, u
    elif dil == PHASES:
        rho, i = u, 0
    else:
        rho, i = jnp.bitwise_and(u, dil - 1), jnp.right_shift(u, dil.bit_length() - 1)
    before = jnp.maximum(i - 1, 0)
    cur = [pl.multiple_of((rho + dil * k) * BLOCK + length * i, 8) for k in range(pieces)]
    prev = [pl.multiple_of((rho + dil * k) * BLOCK + length * before, 8) for k in range(pieces)]
    return i, cur, prev


def _load_tile(ref, starts, dil):
    return jnp.concatenate([ref[pl.ds(st, 8 * dil), :] for st in starts], axis=0)


def _store_tile(ref, starts, dil, val, head=None, accumulate=False):
    length = 8 * dil
    lanes = slice(None) if head is None else pl.ds(head * HEAD_DIM, HEAD_DIM)
    cols = slice(None) if head is None else slice(head * HEAD_DIM, (head + 1) * HEAD_DIM)
    for k, st in enumerate(starts):
        piece = val[k * length:(k + 1) * length, cols]
        if accumulate:
            ref[pl.ds(st, length), lanes] += piece
        else:
            ref[pl.ds(st, length), lanes] = piece


def _tile_position(idx, dil):
    pieces, length = PHASES // dil, 8 * dil
    return pieces * jnp.bitwise_and(idx, length - 1) + jnp.right_shift(idx, length.bit_length() - 1)


def _band_mask(i, dil):
    row = lax.broadcasted_iota(jnp.int32, (BLOCK, 2 * BLOCK), 0)
    col = lax.broadcasted_iota(jnp.int32, (BLOCK, 2 * BLOCK), 1)
    key_pos = _tile_position(jnp.bitwise_and(col, BLOCK - 1), dil) + jnp.where(col >= BLOCK, 0, -BLOCK)
    dist = _tile_position(row, dil) - key_pos
    return (dist >= 0) & (dist <= BLOCK) & ((col >= BLOCK) | (i > 0))


def _causal_mask():
    row = lax.broadcasted_iota(jnp.int32, (BLOCK, BLOCK), 0)
    col = lax.broadcasted_iota(jnp.int32, (BLOCK, BLOCK), 1)
    return row >= col


def _pair_views(col0):
    return [pl.BlockSpec((SEQ, PAIR), lambda hp, g=g: (0, col0 // PAIR + g * (ATTN_WIDTH // PAIR) + hp))
            for g in range(len(DILATIONS))]


def _project_shard(shard, x_ref, w_ref, cf_ref, ss_ref, o_ref):
    ns = w_ref.shape[1]
    tiles = ns // PAIR
    xb = x_ref[...].astype(BF16)
    cf, ss = cf_ref[...], ss_ref[...]

    def write(rotated, scaled):
        for t0 in range(0, tiles, 2):
            strip = jnp.dot(xb, w_ref[:, t0 * PAIR:(t0 + 2) * PAIR], preferred_element_type=F32)
            for t in (t0, t0 + 1):
                val = strip[:, (t - t0) * PAIR:(t - t0 + 1) * PAIR]
                if t < rotated:
                    val = _rope(val, cf, ss)
                    if t < scaled:
                        val = val * (1.0 / math.sqrt(HEAD_DIM))
                o_ref[:, t * PAIR:(t + 1) * PAIR] = val

    for s in range(N_CHIPS):
        rotated = min(max(2 * QKV_WIDTH - s * ns, 0), ns) // PAIR
        scaled = min(max(QKV_WIDTH - s * ns, 0), ns) // PAIR
        @pl.when(shard == s)
        def _(rotated=rotated, scaled=scaled):
            write(rotated, scaled)


def _project_in_own(chip, x, w_own, cos_f, sin_s):
    ns = w_own.shape[1]

    def body(chip_ref, x_ref, w_ref, cf_ref, ss_ref, o_ref):
        _project_shard(chip_ref[0], x_ref, w_ref, cf_ref, ss_ref, o_ref)

    table = pl.BlockSpec((FF_ROWS, PAIR), lambda i, chip_ref: (i, 0))
    return pl.pallas_call(
        body,
        grid_spec=pltpu.PrefetchScalarGridSpec(
            num_scalar_prefetch=1, grid=(SEQ // FF_ROWS,),
            in_specs=[pl.BlockSpec((FF_ROWS, D_MODEL), lambda i, chip_ref: (i, 0)),
                      pl.BlockSpec((D_MODEL, ns), lambda i, chip_ref: (0, 0)), table, table],
            out_specs=pl.BlockSpec((FF_ROWS, ns), lambda i, chip_ref: (i, chip_ref[0]))),
        out_shape=_sds((SEQ, N_CHIPS * ns)), compiler_params=_cp(("parallel",)), name="project_in_own")(
            chip, x, w_own, cos_f, sin_s)


def _project_in(chip, x, wg, cos_f, sin_s, started):
    ns = wg.shape[2]

    def other(j, chip_ref):
        return (chip_ref[0] + 1 + j) % N_CHIPS

    def body(chip_ref, x_ref, w_ref, cf_ref, ss_ref, started_ref, o_ref):
        _project_shard(other(pl.program_id(1), chip_ref), x_ref, w_ref, cf_ref, ss_ref, o_ref)

    table = pl.BlockSpec((FF_ROWS, PAIR), lambda i, j, chip_ref: (i, 0))
    return pl.pallas_call(
        body,
        grid_spec=pltpu.PrefetchScalarGridSpec(
            num_scalar_prefetch=1, grid=(SEQ // FF_ROWS, N_CHIPS - 1),
            in_specs=[pl.BlockSpec((FF_ROWS, D_MODEL), lambda i, j, chip_ref: (i, 0)),
                      pl.BlockSpec((None, D_MODEL, ns), lambda i, j, chip_ref: (other(j, chip_ref), 0, 0)),
                      table, table, HBM_OPERAND],
            out_specs=pl.BlockSpec((FF_ROWS, ns), lambda i, j, chip_ref: (i, other(j, chip_ref)))),
        out_shape=_sds((SEQ, N_CHIPS * ns)), input_output_aliases={5: 0},
        compiler_params=_cp(("parallel", "parallel")), name="project_in")(chip, x, wg, cos_f, sin_s, started)


def _attention_fwd(proj):
    ng = len(DILATIONS)

    def body(*refs):
        q_refs, k_refs, v_refs = refs[:ng], refs[ng:2 * ng], refs[2 * ng:3 * ng]
        attn_ref, lse_ref = refs[3 * ng:]
        qr_refs, kr_refs = q_refs, k_refs
        first = lax.broadcasted_iota(jnp.int32, (BLOCK, PAIR), 1) < HEAD_DIM
        for g, dil in enumerate(DILATIONS):
            two_blocks = SEQ // dil > BLOCK

            def units(t, carry, g=g, dil=dil, two_blocks=two_blocks):
                picked = [_unit_pieces(t * UNIT_BATCH + j, dil) for j in range(UNIT_BATCH)]

                def tiles(ref, with_prev=False):
                    if with_prev and two_blocks:
                        return jnp.stack([jnp.concatenate([_load_tile(ref, prev, dil), _load_tile(ref, rows, dil)],
                                                          axis=0) for _, rows, prev in picked])
                    return jnp.stack([_load_tile(ref, rows, dil) for _, rows, _ in picked])

                qq = tiles(qr_refs[g]).astype(BF16)
                kk = tiles(kr_refs[g], True).astype(BF16)
                vv = tiles(v_refs[g], True).astype(BF16)
                if two_blocks:
                    valid = jnp.stack([_band_mask(i, dil) for i, _, _ in picked])
                else:
                    valid = _causal_mask()[None]
                mine = first[None]
                zero = jnp.zeros_like(qq)
                outs, lses = [], []
                for qh in (jnp.where(mine, qq, zero), jnp.where(mine, zero, qq)):
                    s = jnp.einsum("pqd,pkd->pqk", qh, kk, preferred_element_type=F32)
                    s = jnp.where(valid, s, NEG_INF)
                    m = jnp.max(s, axis=-1, keepdims=True)
                    p = jnp.exp(s - m)
                    l = jnp.sum(p, axis=-1, keepdims=True)
                    outs.append(jnp.einsum("pqk,pkd->pqd", p.astype(BF16), vv, preferred_element_type=F32) * (1.0 / l))
                    lses.append(m + jnp.log(l))
                o = jnp.where(mine, outs[0], outs[1])
                lse = jnp.where(mine, lses[0], lses[1])
                if g > 0:
                    lse_old = tiles(lse_ref)
                    m = jnp.maximum(lse_old, lse)
                    lse_new = m + jnp.log(jnp.exp(lse_old - m) + jnp.exp(lse - m))
                    o = tiles(attn_ref) * jnp.exp(lse_old - lse_new) + o * jnp.exp(lse - lse_new)
                    lse = lse_new
                for j, (_, rows, _) in enumerate(picked):
                    _store_tile(attn_ref, rows, dil, o[j])
                    _store_tile(lse_ref, rows, dil, lse[j])
                return carry

            lax.fori_loop(0, UNITS // UNIT_BATCH, units, 0)

    out = pl.BlockSpec((SEQ, PAIR), lambda hp: (0, hp))
    return pl.pallas_call(
        body, grid=(ATTN_WIDTH // PAIR,),
        in_specs=_pair_views(0) + _pair_views(QKV_WIDTH) + _pair_views(2 * QKV_WIDTH),
        out_specs=[out, out], out_shape=[_sds((SEQ, ATTN_WIDTH)), _sds((SEQ, ATTN_WIDTH))],
        compiler_params=_cp(("parallel",)), name="attention_fwd")(*([proj] * (3 * ng)))


def _attention_bwd(proj, cos_f, sin_s, d_attn, attn, lse, d_u, d_gl):
    pairs = ATTN_WIDTH // PAIR
    last = len(DILATIONS) * pairs - 1

    def accumulate(dil, qr_ref, kr_ref, v_ref, do_ref, o_ref, lse_ref, dq_acc, dk_acc, dv_acc):
        two_blocks = SEQ // dil > BLOCK
        dk_acc[...] = jnp.zeros_like(dk_acc)
        dv_acc[...] = jnp.zeros_like(dv_acc)
        nk = 2 * BLOCK if two_blocks else BLOCK
        first = lax.broadcasted_iota(jnp.int32, (BLOCK, PAIR), 1) < HEAD_DIM
        first_k = lax.broadcasted_iota(jnp.int32, (nk, PAIR), 1) < HEAD_DIM

        def units(t, carry):
            picked = [_unit_pieces(t * UNIT_BATCH + j, dil) for j in range(UNIT_BATCH)]

            def tiles(ref, with_prev=False):
                if with_prev and two_blocks:
                    return jnp.stack([jnp.concatenate([_load_tile(ref, prev, dil), _load_tile(ref, rows, dil)], axis=0)
                                      for _, rows, prev in picked])
                return jnp.stack([_load_tile(ref, rows, dil) for _, rows, _ in picked])

            qq = tiles(qr_ref).astype(BF16)
            kk = tiles(kr_ref, True).astype(BF16)
            vv = tiles(v_ref, True).astype(BF16)
            dof = tiles(do_ref)
            dd = dof * tiles(o_ref)
            lse3 = tiles(lse_ref)
            dob = dof.astype(BF16)
            if two_blocks:
                valid = jnp.stack([_band_mask(i, dil) for i, _, _ in picked])
            else:
                valid = _causal_mask()[None]
            zq, zf = jnp.zeros_like(qq), jnp.zeros_like(dd)
            dqs, dks, dvs = [], [], []
            for head in range(2):
                mine = first[None] if head == 0 else jnp.logical_not(first)[None]
                delta = jnp.sum(jnp.where(mine, dd, zf), axis=-1, keepdims=True)
                lse_h = lse3[:, :, head * HEAD_DIM:head * HEAD_DIM + 1]
                s = jnp.einsum("pqd,pkd->pqk", jnp.where(mine, qq, zq), kk, preferred_element_type=F32)
                p = jnp.where(valid, jnp.exp(s - lse_h), 0.0)
                dp = jnp.einsum("pqd,pkd->pqk", jnp.where(mine, dob, zq), vv, preferred_element_type=F32)
                ds = (p * (dp - delta)).astype(BF16)
                dqs.append(jnp.einsum("pqk,pkd->pqd", ds, kk, preferred_element_type=F32))
                dks.append(jnp.einsum("pqk,pqd->pkd", ds, qq, preferred_element_type=F32))
                dvs.append(jnp.einsum("pqk,pqd->pkd", p.astype(BF16), dob, preferred_element_type=F32))
            dq = jnp.where(first[None], dqs[0], dqs[1])
            dk = jnp.where(first_k[None], dks[0], dks[1])
            dv = jnp.where(first_k[None], dvs[0], dvs[1])
            for j, (_, rows, prev) in enumerate(picked):
                _store_tile(dq_acc, rows, dil, dq[j])
                _store_tile(dk_acc, rows, dil, dk[j, nk - BLOCK:], accumulate=True)
                _store_tile(dv_acc, rows, dil, dv[j, nk - BLOCK:], accumulate=True)
                if two_blocks:
                    _store_tile(dk_acc, prev, dil, dk[j, :BLOCK], accumulate=True)
                    _store_tile(dv_acc, prev, dil, dv[j, :BLOCK], accumulate=True)
            return carry

        lax.fori_loop(0, UNITS // UNIT_BATCH, units, 0)

    def body(qr_ref, kr_ref, v_ref, cf_ref, ss_ref, do_ref, o_ref, lse_ref, du_ref, dgl_ref, out_ref,
             dq_acc, dk_acc, dv_acc, dq_buf, dk_buf, dv_buf, sems):
        step = pl.program_id(0) * pairs + pl.program_id(1)

        def columns(at):
            return [pltpu.make_async_copy(
                buf, out_ref.at[:, pl.ds(pl.multiple_of(j * QKV_WIDTH + at * PAIR, PAIR), PAIR)], sems.at[j])
                for j, buf in enumerate((dq_buf, dk_buf, dv_buf))]

        def tail(ref, col0, at):
            cols = pl.ds(pl.multiple_of(col0 + at * TAIL_COLS, TAIL_COLS), TAIL_COLS)
            return pltpu.make_async_copy(ref, out_ref.at[:, cols], sems.at[3])

        gl_steps, u_steps = 2 * D_MODEL // TAIL_COLS, SSM_WIDTH // TAIL_COLS
        from_gl = step < gl_steps
        from_u = jnp.logical_and(step >= gl_steps, step < gl_steps + u_steps)
        tail_gl = tail(dgl_ref, 3 * QKV_WIDTH + SSM_WIDTH, step)
        tail_u = tail(du_ref, 3 * QKV_WIDTH, step - gl_steps)
        pl.when(from_gl)(tail_gl.start)
        pl.when(from_u)(tail_u.start)

        for g, dil in enumerate(DILATIONS):
            @pl.when(pl.program_id(0) == g)
            def _(dil=dil):
                accumulate(dil, qr_ref, kr_ref, v_ref, do_ref, o_ref, lse_ref, dq_acc, dk_acc, dv_acc)

        @pl.when(step > 0)
        def _():
            for cp in columns(step - 1):
                cp.wait()

        def finish(t, carry):
            rows = pl.ds(pl.multiple_of(t * ROPE_ROWS, ROPE_ROWS), ROPE_ROWS)
            cf, ss = cf_ref[rows, :], ss_ref[rows, :]
            dq = dq_acc[rows, :] * (1.0 / math.sqrt(HEAD_DIM))
            dq_buf[rows, :] = _rope_transposed(dq, cf, ss).astype(BF16)
            dk_buf[rows, :] = _rope_transposed(dk_acc[rows, :], cf, ss).astype(BF16)
            dv_buf[rows, :] = dv_acc[rows, :].astype(BF16)
            return carry

        lax.fori_loop(0, SEQ // ROPE_ROWS, finish, 0)
        for cp in columns(step):
            cp.start()
        pl.when(from_gl)(tail_gl.wait)
        pl.when(from_u)(tail_u.wait)

        @pl.when(step == last)
        def _():
            for cp in columns(step):
                cp.wait()

    whole = pl.BlockSpec((SEQ, PAIR), lambda g, hp: (0, 0))
    pair = pl.BlockSpec((SEQ, PAIR), lambda g, hp: (0, hp))
    views = [pl.BlockSpec((SEQ, PAIR), lambda g, hp, c0=col0 // PAIR: (0, c0 + g * pairs + hp))
             for col0 in (0, QKV_WIDTH, 2 * QKV_WIDTH)]
    gl_blocks, u_blocks = 2 * D_MODEL // TAIL_COLS, SSM_WIDTH // TAIL_COLS
    assert gl_blocks + u_blocks <= last + 1
    gl_spec = pl.BlockSpec((SEQ, TAIL_COLS), lambda g, hp: (0, jnp.minimum(g * pairs + hp, gl_blocks - 1)))
    u_spec = pl.BlockSpec((SEQ, TAIL_COLS),
                          lambda g, hp: (0, jnp.clip(g * pairs + hp - gl_blocks, 0, u_blocks - 1)))
    return pl.pallas_call(
        body, grid=(len(DILATIONS), pairs),
        in_specs=views + [whole, whole, pair, pair, pair, u_spec, gl_spec],
        out_specs=HBM_OPERAND, out_shape=_sds((SEQ, IN_WIDTH), BF16),
        scratch_shapes=[pltpu.VMEM((SEQ, PAIR), F32)] * 3 + [pltpu.VMEM((SEQ, PAIR), BF16)] * 3
        + [pltpu.SemaphoreType.DMA((4,))],
        compiler_params=_cp(("arbitrary", "arbitrary")), name="attention_bwd")(
            proj, proj, proj, cos_f, sin_s, d_attn, attn, lse, d_u, d_gl)


def _cmul(ar, ai, br, bi):
    return ar * br - ai * bi, ar * bi + ai * br


def _pow256(ar, ai):
    for _ in range(8):
        ar, ai = _cmul(ar, ai, ar, ai)
    return ar, ai


def _chunk_carries(first_r, first_i, pr, pi, reverse):
    rows = lax.broadcasted_iota(jnp.int32, first_r.shape, 0)
    out_r = jnp.zeros_like(first_r)
    out_i = jnp.zeros_like(first_i)
    hr = jnp.zeros_like(first_r[0:1])
    hi = jnp.zeros_like(hr)
    order = range(SCAN_CHUNKS - 1, -1, -1) if reverse else range(SCAN_CHUNKS)
    for c in order:
        out_r = jnp.where(rows == c, hr, out_r)
        out_i = jnp.where(rows == c, hi, out_i)
        tr, ti = _cmul(pr[0:1], pi[0:1], hr, hi)
        hr = first_r[c:c + 1] + tr
        hi = first_i[c:c + 1] + ti
    return out_r, out_i


def _tile(j):
    return pl.ds(pl.multiple_of(j * SCAN_CHUNKS, SCAN_CHUNKS), SCAN_CHUNKS)


def _to_scan_rows(t):
    per = SCAN_STEPS // PHASES
    return t.reshape(PHASES, SCAN_CHUNKS, per, t.shape[1]).transpose(2, 0, 1, 3).reshape(t.shape)


def _from_scan_rows(t):
    per = SCAN_STEPS // PHASES
    return t.reshape(per, PHASES, SCAN_CHUNKS, t.shape[1]).transpose(1, 2, 0, 3).reshape(t.shape)


def _scan_in_place(hr_ref, hi_ref, a_r, a_i):
    def local(j, carry):
        tr, ti = _cmul(a_r, a_i, carry[0], carry[1])
        nr = tr + hr_ref[_tile(j), :]
        ni = ti + hi_ref[_tile(j), :]
        hr_ref[_tile(j), :] = nr
        hi_ref[_tile(j), :] = ni
        return nr, ni

    zero = jnp.zeros_like(a_r)
    last_r, last_i = lax.fori_loop(0, SCAN_STEPS, local, (zero, zero), unroll=4)
    pr, pi = _pow256(a_r, a_i)
    er, ei = _chunk_carries(last_r, last_i, pr, pi, reverse=False)

    def fix(j, carry):
        tr, ti = _cmul(carry[0], carry[1], er, ei)
        hr_ref[_tile(j), :] += tr
        hi_ref[_tile(j), :] += ti
        return _cmul(carry[0], carry[1], a_r, a_i)

    lax.fori_loop(0, SCAN_STEPS, fix, (a_r, a_i), unroll=4)
    return er, ei


def _reverse_scan_in_place(lr_ref, li_ref, hr_ref, hi_ref, er, ei, a_r, a_i):
    def local(t, carry):
        j = SCAN_STEPS - 1 - t
        tr, ti = _cmul(a_r, a_i, carry[0], carry[1])
        nr = tr + lr_ref[_tile(j), :]
        ni = ti + li_ref[_tile(j), :]
        lr_ref[_tile(j), :] = nr
        li_ref[_tile(j), :] = ni
        return nr, ni

    zero = jnp.zeros_like(a_r)
    first_r, first_i = lax.fori_loop(0, SCAN_STEPS, local, (zero, zero), unroll=4)
    pr, pi = _pow256(a_r, a_i)
    nxt_r, nxt_i = _chunk_carries(first_r, first_i, pr, pi, reverse=True)

    def accumulate(lam_r, lam_i, hp_r, hp_i, acc):
        return (acc[0] + lam_r * hp_r + lam_i * hp_i, acc[1] + lam_i * hp_r - lam_r * hp_i)

    def fix(t, carry):
        qr, qi, acc_r, acc_i = carry
        j = SCAN_STEPS - 1 - t
        tr, ti = _cmul(qr, qi, nxt_r, nxt_i)
        lam_r = lr_ref[_tile(j), :] + tr
        lam_i = li_ref[_tile(j), :] + ti
        lr_ref[_tile(j), :] = lam_r
        li_ref[_tile(j), :] = lam_i
        acc_r, acc_i = accumulate(lam_r, lam_i, hr_ref[_tile(j - 1), :], hi_ref[_tile(j - 1), :], (acc_r, acc_i))
        qr, qi = _cmul(qr, qi, a_r, a_i)
        return qr, qi, acc_r, acc_i

    qr, qi, acc_r, acc_i = lax.fori_loop(0, SCAN_STEPS - 1, fix, (a_r, a_i, zero, zero), unroll=4)
    tr, ti = _cmul(qr, qi, nxt_r, nxt_i)
    lam_r = lr_ref[_tile(0), :] + tr
    lam_i = li_ref[_tile(0), :] + ti
    lr_ref[_tile(0), :] = lam_r
    li_ref[_tile(0), :] = lam_i
    acc_r, acc_i = accumulate(lam_r, lam_i, er, ei, (acc_r, acc_i))
    return jnp.sum(acc_r, axis=0, keepdims=True), jnp.sum(acc_i, axis=0, keepdims=True)


def _rope_tables():
    half = HEAD_DIM // 2
    inv_freq = ROPE_THETA ** (-jnp.arange(half, dtype=F32) / half)
    ang = jnp.arange(SEQ, dtype=F32)[:, None] * inv_freq[None, :]
    cos, sin = jnp.cos(ang), jnp.sin(ang)
    cos_f = jnp.concatenate([cos, cos, cos, cos], axis=1)
    sin_s = jnp.concatenate([-sin, sin, -sin, sin], axis=1)
    return cos_f, sin_s


def _ssm_discretise(a_re, a_im, log_dt, b_re, b_im):
    lam = lax.complex(a_re, a_im)
    dt = jnp.exp(log_dt)[:, None]
    a_bar = jnp.exp(lam * dt)
    b_bar = ((a_bar - 1.0) / lam)[..., None] * lax.complex(b_re, b_im)
    return a_bar.real, a_bar.imag, b_bar.real, b_bar.imag


SSM_SLABS = 4
SLAB_GROUPS = SSM_GROUPS // SSM_SLABS
SLAB_IN = SSM_WIDTH // SSM_SLABS
SLAB_STATE = SSM_LANES // SSM_SLABS


def _slab_block_diag(blocks):
    _, r, c = blocks.shape
    eye = jnp.eye(SLAB_GROUPS, dtype=blocks.dtype)
    b5 = blocks.reshape(SSM_SLABS, SLAB_GROUPS, r, 1, c) * eye[None, :, None, :, None]
    return b5.reshape(SSM_SLABS, SLAB_GROUPS * r, SLAB_GROUPS * c)


def _diag_blocks(a, b):
    ra, cb = a.shape[1], b.shape[1]
    wa, wb = ra // SLAB_GROUPS, cb // SLAB_GROUPS
    d = lax.dot_general(a, b, (((0,), (0,)), ((), ())), preferred_element_type=F32)
    row_g = jnp.right_shift(lax.broadcasted_iota(jnp.int32, (ra, cb), 0), wa.bit_length() - 1)
    col_g = jnp.right_shift(lax.broadcasted_iota(jnp.int32, (ra, cb), 1), wb.bit_length() - 1)
    d = jnp.where(row_g == col_g, d, 0.0)
    fold = (jnp.bitwise_and(lax.broadcasted_iota(jnp.int32, (cb, wb), 0), wb - 1)
            == lax.broadcasted_iota(jnp.int32, (cb, wb), 1)).astype(F32)
    return jnp.dot(d, fold, preferred_element_type=F32, precision=lax.Precision.HIGHEST)


def _slab_specs():
    tok = pl.BlockSpec((SEQ, SLAB_IN), lambda j: (0, j))
    state = pl.BlockSpec((SEQ, SLAB_STATE), lambda j: (0, j))
    b_in = pl.BlockSpec((None, SLAB_IN, SLAB_STATE), lambda j: (j, 0, 0))
    c_out = pl.BlockSpec((None, SLAB_STATE, SLAB_IN), lambda j: (j, 0, 0))
    vec = pl.BlockSpec((1, SLAB_STATE), lambda j: (0, j))
    ent = pl.BlockSpec((SCAN_CHUNKS, SLAB_STATE), lambda j: (0, j))
    return tok, state, b_in, c_out, vec, ent


def _ssm_forward(u, b_in_r, b_in_i, c_out_r, c_out_ni, a_r, a_i):
    def body(u_ref, br_ref, bi_ref, cr_ref, ci_ref, ar_ref, ai_ref, y_ref, hr_ref, hi_ref, er_ref, ei_ref):
        uu = u_ref[...]
        hr_ref[...] = jnp.dot(uu, br_ref[...], preferred_element_type=F32)
        hi_ref[...] = jnp.dot(uu, bi_ref[...], preferred_element_type=F32)
        a_re = jnp.broadcast_to(ar_ref[...], (SCAN_CHUNKS, SLAB_STATE))
        a_im = jnp.broadcast_to(ai_ref[...], (SCAN_CHUNKS, SLAB_STATE))
        er_ref[...], ei_ref[...] = _scan_in_place(hr_ref, hi_ref, a_re, a_im)
        y_ref[...] = (jnp.dot(hr_ref[...].astype(BF16), cr_ref[...], preferred_element_type=F32)
                      + jnp.dot(hi_ref[...].astype(BF16), ci_ref[...], preferred_element_type=F32))

    tok, state, b_in, c_out, vec, ent = _slab_specs()
    return pl.pallas_call(
        body, grid=(SSM_SLABS,), in_specs=[tok, b_in, b_in, c_out, c_out, vec, vec],
        out_specs=[tok, state, state, ent, ent],
        out_shape=[_sds((SEQ, SSM_WIDTH)), _sds((SEQ, SSM_LANES)), _sds((SEQ, SSM_LANES)),
                   _sds((SCAN_CHUNKS, SSM_LANES)), _sds((SCAN_CHUNKS, SSM_LANES))],
        compiler_params=_cp(("parallel",)), name="ssm_forward")(u, b_in_r, b_in_i, c_out_r, c_out_ni, a_r, a_i)


def _ssm_backward(d_y, d_u_skip, u, h_r, h_i, e_r, e_i, b_in_r, b_in_i, c_out_r, c_out_ni, a_r, a_i):
    def body(dy_ref, skip_ref, u_ref, hr_ref, hi_ref, er_ref, ei_ref, br_ref, bi_ref, cr_ref, ci_ref, ar_ref, ai_ref,
             du_ref, dar_ref, dai_ref, dcr_ref, dci_ref, dbr_ref, dbi_ref, lr_ref, li_ref):
        dy = dy_ref[...]
        lr_ref[...] = _dot_nt(dy, cr_ref[...])
        li_ref[...] = _dot_nt(dy, ci_ref[...])
        a_re = jnp.broadcast_to(ar_ref[...], (SCAN_CHUNKS, SLAB_STATE))
        a_im = -jnp.broadcast_to(ai_ref[...], (SCAN_CHUNKS, SLAB_STATE))
        dar_ref[...], dai_ref[...] = _reverse_scan_in_place(lr_ref, li_ref, hr_ref, hi_ref, er_ref[...], ei_ref[...],
                                                            a_re, a_im)
        dcr_ref[...] = _diag_blocks(dy, hr_ref[...].astype(BF16))
        dci_ref[...] = _diag_blocks(dy, hi_ref[...].astype(BF16))
        lam_r, lam_i = lr_ref[...].astype(BF16), li_ref[...].astype(BF16)
        uu = u_ref[...]
        dbr_ref[...] = _diag_blocks(uu, lam_r)
        dbi_ref[...] = _diag_blocks(uu, lam_i)
        du = skip_ref[...] + _dot_nt(lam_r, br_ref[...]) + _dot_nt(lam_i, bi_ref[...])
        du_ref[...] = du.astype(BF16)

    tok, state, b_in, c_out, vec, ent = _slab_specs()
    db = pl.BlockSpec((SLAB_IN, SSM_STATE), lambda j: (j, 0))
    return pl.pallas_call(
        body, grid=(SSM_SLABS,), in_specs=[tok, tok, tok, state, state, ent, ent, b_in, b_in, c_out, c_out, vec, vec],
        out_specs=[tok, vec, vec, db, db, db, db],
        out_shape=[_sds((SEQ, SSM_WIDTH), BF16), _sds((1, SSM_LANES)), _sds((1, SSM_LANES))]
        + [_sds((SSM_WIDTH, SSM_STATE))] * 4,
        scratch_shapes=[pltpu.VMEM((SEQ, SLAB_STATE), F32)] * 2,
        compiler_params=_cp(("parallel",)), name="ssm_backward")(
            d_y, d_u_skip, u, h_r, h_i, e_r, e_i, b_in_r, b_in_i, c_out_r, c_out_ni, a_r, a_i)


FF_ROWS = 1024
FF_SHARD = D_FF // N_CHIPS


def _dot_nt(a, b):
    return lax.dot_general(a, b, (((1,), (1,)), ((), ())), preferred_element_type=F32)


def _ffn_up(h, w_gate_t, w_up_t):
    def body(h_ref, wg_ref, wu_ref, a_ref, b_ref, act_ref):
        hb = h_ref[...].astype(BF16)
        a = _dot_nt(hb, wg_ref[...])
        b = _dot_nt(hb, wu_ref[...])
        a_ref[...] = a
        b_ref[...] = b
        act_ref[...] = (a * jax.nn.sigmoid(a) * b).astype(BF16)

    w_spec = pl.BlockSpec((None, FF_SHARD, D_MODEL), lambda i, k: (k, 0, 0))
    o_spec = pl.BlockSpec((None, FF_ROWS, FF_SHARD), lambda i, k: (k, i, 0))
    shape = (N_CHIPS, SEQ, FF_SHARD)
    return pl.pallas_call(
        body, grid=(SEQ // FF_ROWS, N_CHIPS),
        in_specs=[pl.BlockSpec((FF_ROWS, D_MODEL), lambda i, k: (i, 0)), w_spec, w_spec],
        out_specs=[o_spec, o_spec, o_spec], out_shape=[_sds(shape), _sds(shape), _sds(shape, BF16)],
        compiler_params=_cp(("parallel", "parallel")), name="ffn_up")(h, w_gate_t, w_up_t)


def _ffn_down_ln2_loss(act, w_down, h, tgt, ln_g, ln_b):
    def body(act_ref, w_ref, h_ref, tgt_ref, g_ref, b_ref, dz_ref, loss_ref, dg_ref, db_ref, acc):
        i, k = pl.program_id(0), pl.program_id(1)
        part = jnp.dot(act_ref[...], w_ref[...], preferred_element_type=F32)

        @pl.when(k == 0)
        def _():
            acc[...] = part

        @pl.when(k > 0)
        def _():
            acc[...] += part

        @pl.when(k == N_CHIPS - 1)
        def _():
            g = g_ref[...]
            xhat, rstd = _ln_stats(DN_ALPHA * h_ref[...] + acc[...])
            err = xhat * g + b_ref[...] - tgt_ref[...]
            d_out = err * (1.0 / D_MODEL)
            dz_ref[...] = _ln_bwd(d_out, xhat, rstd, g)
            loss_rows = jnp.sum(err * err, axis=-1, keepdims=True) * (0.5 / D_MODEL)
            sums = (jnp.broadcast_to(jnp.sum(loss_rows, axis=0, keepdims=True), loss_ref.shape),
                    _colsum(d_out * xhat), _colsum(d_out))
            for ref, val in zip((loss_ref, dg_ref, db_ref), sums):
                @pl.when(i == 0)
                def _(ref=ref, val=val):
                    ref[...] = val

                @pl.when(i > 0)
                def _(ref=ref, val=val):
                    ref[...] += val

    row = pl.BlockSpec((FF_ROWS, D_MODEL), lambda i, k: (i, 0))
    vec = pl.BlockSpec((1, D_MODEL), lambda i, k: (0, 0))
    return pl.pallas_call(
        body, grid=(SEQ // FF_ROWS, N_CHIPS),
        in_specs=[pl.BlockSpec((None, FF_ROWS, FF_SHARD), lambda i, k: (k, i, 0)),
                  pl.BlockSpec((None, FF_SHARD, D_MODEL), lambda i, k: (k, 0, 0)), row, row, vec, vec],
        out_specs=[row, pl.BlockSpec((1, BLOCK), lambda i, k: (0, 0)), vec, vec],
        out_shape=[_sds((SEQ, D_MODEL)), _sds((1, BLOCK)), _sds((1, D_MODEL)), _sds((1, D_MODEL))],
        scratch_shapes=[pltpu.VMEM((FF_ROWS, D_MODEL), F32)],
        compiler_params=_cp(("arbitrary", "arbitrary")), name="ffn_down_ln2_loss")(act, w_down, h, tgt, ln_g, ln_b)


def _ffn_down_bwd(dz, w_down, a, b):
    def body(dz_ref, wd_ref, a_ref, b_ref, da_ref, db_ref):
        d_act = _dot_nt(dz_ref[...].astype(BF16), wd_ref[...])
        av = a_ref[...]
        sg = jax.nn.sigmoid(av)
        da_ref[...] = (d_act * b_ref[...] * sg * (1.0 + av * (1.0 - sg))).astype(BF16)
        db_ref[...] = (d_act * av * sg).astype(BF16)

    t_spec = pl.BlockSpec((None, FF_ROWS, FF_SHARD), lambda i, k: (k, i, 0))
    shape = (N_CHIPS, SEQ, FF_SHARD)
    return pl.pallas_call(
        body, grid=(SEQ // FF_ROWS, N_CHIPS),
        in_specs=[pl.BlockSpec((FF_ROWS, D_MODEL), lambda i, k: (i, 0)),
                  pl.BlockSpec((None, FF_SHARD, D_MODEL), lambda i, k: (k, 0, 0)), t_spec, t_spec],
        out_specs=[t_spec, t_spec], out_shape=[_sds(shape, BF16), _sds(shape, BF16)],
        compiler_params=_cp(("parallel", "parallel")), name="ffn_down_bwd")(dz, w_down, a, b)


def _ffn_dh(d_a, d_b, w_gate_t, w_up_t):
    def body(da_ref, db_ref, wg_ref, wu_ref, o_ref, acc):
        k = pl.program_id(1)
        part = (jnp.dot(da_ref[...], wg_ref[...], preferred_element_type=F32)
                + jnp.dot(db_ref[...], wu_ref[...], preferred_element_type=F32))

        @pl.when(k == 0)
        def _():
            acc[...] = part

        @pl.when(k > 0)
        def _():
            acc[...] += part

        @pl.when(k == N_CHIPS - 1)
        def _():
            o_ref[...] = acc[...]

    t_spec = pl.BlockSpec((None, FF_ROWS, FF_SHARD), lambda i, k: (k, i, 0))
    w_spec = pl.BlockSpec((None, FF_SHARD, D_MODEL), lambda i, k: (k, 0, 0))
    return pl.pallas_call(
        body, grid=(SEQ // FF_ROWS, N_CHIPS), in_specs=[t_spec, t_spec, w_spec, w_spec],
        out_specs=pl.BlockSpec((FF_ROWS, D_MODEL), lambda i, k: (i, 0)), out_shape=_sds((SEQ, D_MODEL)),
        scratch_shapes=[pltpu.VMEM((FF_ROWS, D_MODEL), F32)],
        compiler_params=_cp(("parallel", "arbitrary")), name="ffn_dh")(d_a, d_b, w_gate_t, w_up_t)


def _local_step(x, tgt, wts, small, ffn_grads, mixer_grads, small_grads):
    s = SEQ
    cos_f, sin_s = [_to_phase_rows(t) for t in _rope_tables()]
    x = _reorder_rows(x, to_phase=True, name="phase_rows_x")
    tgt = _reorder_rows(tgt, to_phase=True, name="phase_rows_target")

    chip, w_own, w_others = wts["w_in_parts"]
    proj_own = _project_in_own(chip, x, w_own, cos_f, sin_s)
    proj = _project_in(chip, x, w_others, cos_f, sin_s, proj_own)

    attn, lse = _attention_fwd(proj)

    (abar_r, abar_i, bbar_r, bbar_i), ssm_vjp = jax.vjp(
        _ssm_discretise, small["ssm_a_re"], small["ssm_a_im"], small["ssm_log_dt"], small["ssm_b_re"], small["ssm_b_im"])
    b_in_r, b_in_i = [_slab_block_diag(b.transpose(0, 2, 1)).astype(BF16) for b in (bbar_r, bbar_i)]
    c_out_r = _slab_block_diag(small["ssm_c_re"].transpose(0, 2, 1)).astype(BF16)
    c_out_ni = _slab_block_diag(-small["ssm_c_im"].transpose(0, 2, 1)).astype(BF16)
    a_r, a_i = abar_r.reshape(1, SSM_LANES), abar_i.reshape(1, SSM_LANES)
    d_skip = small["ssm_d"].reshape(1, SSM_WIDTH)

    u_f = _to_scan_rows(proj[:, 3 * QKV_WIDTH:3 * QKV_WIDTH + SSM_WIDTH])
    u_p = u_f.astype(BF16)
    y_c, h_r, h_i, e_r, e_i = _ssm_forward(u_p, b_in_r, b_in_i, c_out_r, c_out_ni, a_r, a_i)

    def branch(t, wg):
        return jnp.concatenate([jnp.dot(t, wg[k], preferred_element_type=F32) for k in range(N_CHIPS)], axis=1)

    def branch_t(t, wg):
        ns = wg.shape[2]
        return sum(_dot_nt(t[:, k * ns:(k + 1) * ns], wg[k]) for k in range(N_CHIPS))

    def gelu_glu(yc, u, dsk, wg):
        y = yc + dsk * u
        gel = (0.5 * y * (1.0 + jnp.tanh(GELU_C * (y + GELU_K * y * y * y)))).astype(BF16)
        glu = branch(gel, wg)
        return y, gel, glu, glu[:, :SSM_WIDTH] * jax.nn.sigmoid(glu[:, SSM_WIDTH:])

    y_s5, gel, glu, y_glu = _rowwise(
        gelu_glu, [y_c, u_f], [d_skip, wts["w_glu"]],
        [_sds((s, SSM_WIDTH)), _sds((s, SSM_WIDTH), BF16), _sds((s, 2 * SSM_WIDTH)), _sds((s, SSM_WIDTH), BF16)],
        tm=512, name="ssm_gelu_glu")
    y_glu = _from_scan_rows(y_glu)

    gl0 = (proj, D_MODEL, (3 * QKV_WIDTH + SSM_WIDTH) // D_MODEL)
    gl1 = (proj, D_MODEL, (3 * QKV_WIDTH + SSM_WIDTH) // D_MODEL + 1)
    b_gate = small["b_gate"]
    w_out = wts["w_out"].reshape(D_MODEL, D_MODEL)

    def mix_ln1(l0, l1, at, yg, xv, bg, wa, ws, wo, g, b):
        ya = branch(at.astype(BF16), wa)
        ys = branch(yg, ws)
        mixed = (jax.nn.sigmoid(l0 + bg[0:1]) * ya + jax.nn.sigmoid(l1 + bg[1:2]) * ys).astype(BF16)
        z = DN_ALPHA * xv + jnp.dot(mixed, wo, preferred_element_type=F32)
        xhat, _ = _ln_stats(z)
        return ya, ys, mixed, z, xhat * g + b

    y_attn, y_ssm, mixed, z1, h = _rowwise(
        mix_ln1, [gl0, gl1, attn, y_glu, x],
        [b_gate, wts["w_attn_br"], wts["w_ssm_br"], w_out, small["ln1_g"], small["ln1_b"]],
        [_sds((s, D_MODEL)), _sds((s, D_MODEL)), _sds((s, D_MODEL), BF16), _sds((s, D_MODEL)), _sds((s, D_MODEL))],
        tm=256, name="mix_ln1")

    nf = D_FF // N_CHIPS
    w_gate_t, w_up_t, w_down = wts["w_ff_gate"], wts["w_ff_up"], wts["w_ff_down"]
    ff_a, ff_b, act = _ffn_up(h, w_gate_t, w_up_t)
    dz2, loss_v, d_ln2_g, d_ln2_b = _ffn_down_ln2_loss(act, w_down, h, tgt, small["ln2_g"], small["ln2_b"])

    d_a, d_b = _ffn_down_bwd(dz2, w_down, ff_a, ff_b)

    def grad_rows(lhs, rhs, name):
        return _matmul(lhs, rhs, grid=(N_CHIPS,), a_spec=pl.BlockSpec((None, s, nf), lambda k: (k, 0, 0)),
                       b_spec=pl.BlockSpec((s, D_MODEL), lambda k: (0, 0)),
                       o_spec=pl.BlockSpec((None, nf, D_MODEL), lambda k: (k, 0, 0)),
                       out_shape=_sds((N_CHIPS, nf, D_MODEL), BF16), dims=(0, 0), name=name)

    g_w_ff_down = grad_rows(act, dz2, "g_w_ff_down")
    g_w_ff_gate = grad_rows(d_a, h, "g_w_ff_gate")
    g_w_ff_up = grad_rows(d_b, h, "g_w_ff_up")
    dh_ff = _ffn_dh(d_a, d_b, w_gate_t, w_up_t)

    def ln1_gate_bwd(dz, dff, z, l0, l1, ya, ys, g, bg, wo, wa, ws):
        xhat, rstd = _ln_stats(z)
        dh = DN_ALPHA * dz + dff
        dz_in = _ln_bwd(dh, xhat, rstd, g)
        dm = _dot_nt(dz_in.astype(BF16), wo)
        g0 = jax.nn.sigmoid(l0 + bg[0:1])
        g1 = jax.nn.sigmoid(l1 + bg[1:2])
        dl0 = dm * ya * g0 * (1.0 - g0)
        dl1 = dm * ys * g1 * (1.0 - g1)
        dya, dys = (dm * g0).astype(BF16), (dm * g1).astype(BF16)
        return (dz_in, dya, dys, jnp.concatenate([dl0, dl1], axis=1), branch_t(dya, wa), branch_t(dys, ws),
                _colsum(dh * xhat), _colsum(dh), _colsum(dl0), _colsum(dl1))

    dz1, d_y_attn, d_y_ssm, d_gl, d_attn, d_y_glu, d_ln1_g, d_ln1_b, d_bg0, d_bg1 = _rowwise(
        ln1_gate_bwd, [dz2, dh_ff, z1, gl0, gl1, y_attn, y_ssm],
        [small["ln1_g"], b_gate, w_out, wts["w_attn_br"], wts["w_ssm_br"]],
        [_sds((s, D_MODEL)), _sds((s, D_MODEL), BF16), _sds((s, D_MODEL), BF16), _sds((s, 2 * D_MODEL), BF16),
         _sds((s, ATTN_WIDTH)), _sds((s, SSM_WIDTH))],
        [_sds((1, D_MODEL))] * 4, tm=256, name="ln1_gate_bwd", after=(g_w_ff_down, g_w_ff_gate, g_w_ff_up))
    ffn_sent = ffn_grads({"w_ff_down": g_w_ff_down, "w_ff_gate": g_w_ff_gate, "w_ff_up": g_w_ff_up}, dz1)
    g_w_out = _mm_rows_tn(mixed, dz1, name="g_w_out")

    g_w_ssm_br = _mm_cols_tn(y_glu, d_y_ssm, ns=D_MODEL // N_CHIPS, name="g_w_ssm_br")
    d_y_glu = _to_scan_rows(d_y_glu)

    def glu_gelu_bwd(dyg, gl, y, u, dsk, wg):
        ga, gb = gl[:, :SSM_WIDTH], gl[:, SSM_WIDTH:]
        sg = jax.nn.sigmoid(gb)
        d_gl = jnp.concatenate([dyg * sg, dyg * ga * sg * (1.0 - sg)], axis=1).astype(BF16)
        dg = branch_t(d_gl, wg)
        th = jnp.tanh(GELU_C * (y + GELU_K * y * y * y))
        dy = dg * (0.5 * (1.0 + th) + 0.5 * y * (1.0 - th * th) * GELU_C * (1.0 + 3.0 * GELU_K * y * y))
        return d_gl, dy, dy * dsk, _colsum(dy * u)

    d_glu, d_y, d_u_skip, d_ssm_d = _rowwise(
        glu_gelu_bwd, [d_y_glu, glu, y_s5, u_f], [d_skip, wts["w_glu"]],
        [_sds((s, 2 * SSM_WIDTH), BF16), _sds((s, SSM_WIDTH), BF16), _sds((s, SSM_WIDTH))], [_sds((1, SSM_WIDTH))],
        tm=512, name="glu_gelu_bwd", after=tuple(ffn_sent))
    g_w_glu = _mm_cols_tn(gel, d_glu, ns=2 * SSM_WIDTH // N_CHIPS, name="g_w_glu")
    d_u, d_abar_r, d_abar_i, d_c_r, d_c_ni, d_bin_r, d_bin_i = _ssm_backward(
        d_y, d_u_skip, u_p, h_r, h_i, e_r, e_i, b_in_r, b_in_i, c_out_r, c_out_ni, a_r, a_i)
    d_u = _from_scan_rows(d_u)
    d_bbar_r = d_bin_r.reshape(SSM_GROUPS, SSM_GROUP, SSM_STATE).transpose(0, 2, 1)
    d_bbar_i = d_bin_i.reshape(SSM_GROUPS, SSM_GROUP, SSM_STATE).transpose(0, 2, 1)
    d_a_re, d_a_im, d_log_dt, d_b_re, d_b_im = ssm_vjp(
        (d_abar_r.reshape(SSM_GROUPS, SSM_STATE), d_abar_i.reshape(SSM_GROUPS, SSM_STATE), d_bbar_r, d_bbar_i))
    d_c_re = d_c_r.reshape(SSM_GROUPS, SSM_GROUP, SSM_STATE)
    d_c_im = -d_c_ni.reshape(SSM_GROUPS, SSM_GROUP, SSM_STATE)

    g_w_attn_br = _mm_cols_tn(attn, d_y_attn, ns=D_MODEL // N_CHIPS, name="g_w_attn_br")
    mixer_grads({"w_out": g_w_out, "w_ssm_br": g_w_ssm_br, "w_glu": g_w_glu, "w_attn_br": g_w_attn_br}, d_abar_r)
    small_g = {"b_gate": jnp.concatenate([d_bg0, d_bg1], axis=0), "ssm_a_re": d_a_re, "ssm_a_im": d_a_im,
               "ssm_log_dt": d_log_dt, "ssm_b_re": d_b_re, "ssm_b_im": d_b_im, "ssm_c_re": d_c_re, "ssm_c_im": d_c_im,
               "ssm_d": d_ssm_d.reshape(SSM_WIDTH), "ln1_g": d_ln1_g, "ln1_b": d_ln1_b, "ln2_g": d_ln2_g,
               "ln2_b": d_ln2_b}
    shared = small_grads(small_g, loss_v[0, 0])
    d_proj = _attention_bwd(proj, cos_f, sin_s, d_attn, attn, lse, d_u, d_gl)

    g_w_in = _mm_cols_tn(x, d_proj, ns=IN_WIDTH // N_CHIPS, name="g_w_in", after=tuple(shared))

    def grad_x_after(after):
        dx_proj = _mm_cols_nt(d_proj, wts["w_in"], tm=1024, name="dx_proj", after=after)
        return _reorder_rows(dz1, dx_proj, to_phase=False, name="grad_x", scale=DN_ALPHA)

    return grad_x_after, g_w_in, d_proj


GATHER_ID, SWAP_ID, SCATTER_ID, JOIN_ID, EXCHANGE_ID = 1, 2, 3, 4, 5


def _place():
    return lax.axis_index("x"), lax.axis_index("y"), lax.axis_index("c")


def _other_chips(x, y):
    return [(1 - x, y), (x, 1 - y), (1 - x, 1 - y)]


def _handshake(peers):
    barrier = pltpu.get_barrier_semaphore()
    for peer in peers:
        pl.semaphore_signal(barrier, inc=1, device_id=peer, device_id_type=MESH)
    pl.semaphore_wait(barrier, len(peers))


def _sequencer(body, arrays, out_type, sems, collective_id, name):
    return pl.kernel(body, name=name, out_type=out_type,
                     mesh=plsc.ScalarSubcoreMesh(axis_name="sequencer", num_cores=1), scratch_types=sems,
                     compiler_params=pltpu.CompilerParams(collective_id=collective_id))(*arrays)


def _gather_weights(shards, *, name, own_slot=True):
    nw = len(shards)

    def body(*refs):
        ins, outs = refs[:nw], refs[nw:2 * nw]
        send_sems, recv_sems, pass_send, pass_recv, local_sems = refs[2 * nw:]
        x, y, c = _place()
        chip = 2 * x + y
        chips = _other_chips(x, y)
        _handshake([(x, y, 1 - c)] + [(cx, cy, c) for cx, cy in chips])
        started, local = [], []
        for w in range(nw):
            hw = shards[w].shape[0] // 2
            mine = pl.ds(c * hw, hw)
            if own_slot:
                own = pltpu.make_async_copy(ins[w], outs[w].at[chip], local_sems.at[w])
                own.start()
                local.append(own)
            for j, (cx, cy) in enumerate(chips):
                cp = pltpu.make_async_remote_copy(
                    src_ref=ins[w].at[mine], dst_ref=outs[w].at[chip, mine], send_sem=send_sems.at[w, j],
                    recv_sem=recv_sems.at[w, j], device_id=(cx, cy, c), device_id_type=MESH)
                cp.start()
                started.append(cp)
        passed = []
        for w in range(nw):
            hw = shards[w].shape[0] // 2
            mine = pl.ds(c * hw, hw)
            for j, (cx, cy) in enumerate(chips):
                landed = outs[w].at[2 * cx + cy, mine]
                pltpu.make_async_remote_copy(
                    src_ref=ins[w].at[mine], dst_ref=landed, send_sem=send_sems.at[w, j],
                    recv_sem=recv_sems.at[w, j], device_id=(cx, cy, c), device_id_type=MESH).wait_recv()
                cp = pltpu.make_async_remote_copy(
                    src_ref=landed, dst_ref=landed, send_sem=pass_send.at[w, j], recv_sem=pass_recv.at[w, j],
                    device_id=(x, y, 1 - c), device_id_type=MESH)
                cp.start()
                passed.append(cp)
        for w in range(nw):
            hw = shards[w].shape[0] // 2
            theirs = pl.ds((1 - c) * hw, hw)
            for j, (cx, cy) in enumerate(chips):
                landed = outs[w].at[2 * cx + cy, theirs]
                pltpu.make_async_remote_copy(
                    src_ref=landed, dst_ref=landed, send_sem=pass_send.at[w, j], recv_sem=pass_recv.at[w, j],
                    device_id=(x, y, 1 - c), device_id_type=MESH).wait_recv()
        for cp in local:
            cp.wait()
        for cp in started + passed:
            cp.wait_send()

    sem = pltpu.SemaphoreType.DMA
    return _sequencer(body, shards, [_sds((N_CHIPS,) + a.shape, a.dtype) for a in shards],
                      [sem((nw, 3)), sem((nw, 3)), sem((nw, 3)), sem((nw, 3)), sem((nw,))], GATHER_ID, name)


def _swap_other_halves(grads, *, name):
    nw = len(grads)

    def body(*refs):
        ins, outs = refs[:nw], refs[nw:2 * nw]
        send_sems, recv_sems = refs[2 * nw:]
        x, y, c = _place()
        _handshake([(x, y, 1 - c)])
        cps = []
        for w in range(nw):
            hw = grads[w].shape[1] // 2
            cp = pltpu.make_async_remote_copy(
                src_ref=ins[w].at[:, pl.ds((1 - c) * hw, hw)], dst_ref=outs[w], send_sem=send_sems.at[w],
                recv_sem=recv_sems.at[w], device_id=(x, y, 1 - c), device_id_type=MESH)
            cp.start()
            cps.append(cp)
        for cp in cps:
            cp.wait()

    sem = pltpu.SemaphoreType.DMA
    return _sequencer(body, grads, [_sds((N_CHIPS, g.shape[1] // 2, g.shape[2]), g.dtype) for g in grads],
                      [sem((nw,)), sem((nw,))], SWAP_ID, name)


def _add_my_halves(core, grads, others, *, name, after=()):
    nw = len(grads)
    halves = [g.shape[1] // 2 for g in grads]

    def body(core_ref, *refs):
        outs = refs[2 * nw + len(after):]
        for g_ref, o_ref, out_ref in zip(refs[:nw], refs[nw:2 * nw], outs):
            out_ref[...] = (g_ref[...].astype(F32) + o_ref[...].astype(F32)).astype(out_ref.dtype)

    in_specs = [pl.BlockSpec((None, None, hw, g.shape[2]), lambda s, core_ref: (s, core_ref[0], 0, 0))
                for g, hw in zip(grads, halves)]
    in_specs += [pl.BlockSpec((None, hw, g.shape[2]), lambda s, core_ref: (s, 0, 0)) for g, hw in zip(grads, halves)]
    return pl.pallas_call(
        body,
        grid_spec=pltpu.PrefetchScalarGridSpec(
            num_scalar_prefetch=1, grid=(N_CHIPS,), in_specs=in_specs + [HBM_OPERAND] * len(after),
            out_specs=[pl.BlockSpec((None, hw, g.shape[2]), lambda s, core_ref: (s, 0, 0))
                       for g, hw in zip(grads, halves)]),
        out_shape=[_sds((N_CHIPS, hw, g.shape[2]), BF16) for g, hw in zip(grads, halves)],
        compiler_params=_cp(("parallel",)), name=name)(
            core, *[g.reshape(N_CHIPS, 2, hw, g.shape[2]) for g, hw in zip(grads, halves)], *others, *after)


def _scatter_partials(parts, *, name):
    nw = len(parts)

    def body(*refs):
        ins, outs = refs[:nw], refs[nw:2 * nw]
        send_sems, recv_sems = refs[2 * nw:]
        x, y, c = _place()
        _handshake([(cx, cy, c) for cx, cy in _other_chips(x, y)])
        cps = []
        for w in range(nw):
            for j, (cx, cy) in enumerate(_other_chips(x, y)):
                cp = pltpu.make_async_remote_copy(
                    src_ref=ins[w].at[2 * cx + cy], dst_ref=outs[w].at[j], send_sem=send_sems.at[w, j],
                    recv_sem=recv_sems.at[w, j], device_id=(cx, cy, c), device_id_type=MESH)
                cp.start()
                cps.append(cp)
        for cp in cps:
            cp.wait()

    sem = pltpu.SemaphoreType.DMA
    return _sequencer(body, parts, [_sds((3,) + p.shape[1:], p.dtype) for p in parts],
                      [sem((nw, 3)), sem((nw, 3))], SCATTER_ID, name)


SUM_STEPS = 2


def _sum_partials(chip, parts, recvd, *, name, after=()):
    nw = len(parts)
    rows = [p.shape[1] // SUM_STEPS for p in parts]

    def body(chip_ref, *refs):
        outs = refs[2 * nw + len(after):]
        for p_ref, r_ref, out_ref in zip(refs[:nw], refs[nw:2 * nw], outs):
            acc = p_ref[...].astype(F32)
            for j in range(3):
                acc = acc + r_ref[j].astype(F32)
            out_ref[...] = acc

    in_specs = [pl.BlockSpec((None, th, p.shape[2]), lambda i, chip_ref: (chip_ref[0], i, 0))
                for p, th in zip(parts, rows)]
    in_specs += [pl.BlockSpec((3, th, p.shape[2]), lambda i, chip_ref: (0, i, 0)) for p, th in zip(parts, rows)]
    return pl.pallas_call(
        body,
        grid_spec=pltpu.PrefetchScalarGridSpec(
            num_scalar_prefetch=1, grid=(SUM_STEPS,), in_specs=in_specs + [HBM_OPERAND] * len(after),
            out_specs=[pl.BlockSpec((th, p.shape[2]), lambda i, chip_ref: (i, 0)) for p, th in zip(parts, rows)]),
        out_shape=[_sds(p.shape[1:]) for p in parts], compiler_params=_cp(("parallel",)), name=name)(
            chip, *parts, *recvd, *after)


def _swap_reduced_halves(halves, *, name):
    nw = len(halves)

    def body(*refs):
        ins, outs = refs[:nw], refs[nw:2 * nw]
        send_sems, recv_sems = refs[2 * nw:]
        x, y, c = _place()
        _handshake([(x, y, 1 - c)])
        cps = []
        for w in range(nw):
            cp = pltpu.make_async_remote_copy(
                src_ref=ins[w], dst_ref=outs[w], send_sem=send_sems.at[w], recv_sem=recv_sems.at[w],
                device_id=(x, y, 1 - c), device_id_type=MESH)
            cp.start()
            cps.append(cp)
        for cp in cps:
            cp.wait()

    sem = pltpu.SemaphoreType.DMA
    return _sequencer(body, halves, [_sds(h.shape, h.dtype) for h in halves], [sem((nw,)), sem((nw,))], JOIN_ID, name)


def _exchange_each(arrs, *, name):
    n = len(arrs)

    def body(*refs):
        ins, outs = refs[:n], refs[n:2 * n]
        send_sems, recv_sems, local_sems = refs[2 * n:]
        x, y, c = _place()
        me = 4 * x + 2 * y + c
        peers = []
        for mask in range(1, N_DEV):
            peers.append((1 - x if mask & 4 else x, 1 - y if mask & 2 else y, 1 - c if mask & 1 else c))
        _handshake(peers)
        cps = []
        for a in range(n):
            own = pltpu.make_async_copy(ins[a], outs[a].at[me], local_sems.at[a])
            own.start()
            cps.append(own)
        sent = []
        for a in range(n):
            for k, peer in enumerate(peers):
                cp = pltpu.make_async_remote_copy(
                    src_ref=ins[a], dst_ref=outs[a].at[me], send_sem=send_sems.at[a, k],
                    recv_sem=recv_sems.at[a, k], device_id=peer, device_id_type=MESH)
                cp.start()
                sent.append(cp)
        for a in range(n):
            for k, (px, py, pc) in enumerate(peers):
                pltpu.make_async_remote_copy(
                    src_ref=ins[a], dst_ref=outs[a].at[4 * px + 2 * py + pc], send_sem=send_sems.at[a, k],
                    recv_sem=recv_sems.at[a, k], device_id=(px, py, pc), device_id_type=MESH).wait_recv()
        for cp in sent:
            cp.wait_send()
        for cp in cps:
            cp.wait()

    sem = pltpu.SemaphoreType.DMA
    return _sequencer(body, arrs, [_sds((N_DEV,) + a.shape) for a in arrs],
                      [sem((n, N_DEV - 1)), sem((n, N_DEV - 1)), sem((n,))], EXCHANGE_ID, name)


def _sum_slots(slots, *, name, after=()):
    n = len(slots)

    def body(*refs):
        for s_ref, out_ref in zip(refs[:n], refs[n + len(after):]):
            acc = s_ref[0]
            for d in range(1, N_DEV):
                acc = acc + s_ref[d]
            out_ref[...] = acc

    vmem = pl.BlockSpec(memory_space=pltpu.VMEM)
    return pl.pallas_call(
        body, in_specs=[vmem] * n + [HBM_OPERAND] * len(after), out_specs=[vmem] * n,
        out_shape=[_sds(s.shape[1:]) for s in slots],
        compiler_params=pltpu.CompilerParams(vmem_limit_bytes=VMEM_LIMIT_BYTES), name=name)(*slots, *after)


def _reduce_scatter_start(grads, core, *, tag, add_after=()):
    others = _swap_other_halves(grads, name="swap_other_halves_" + tag)
    parts = _add_my_halves(core, grads, others, name="add_my_halves_" + tag, after=add_after)
    return parts, _scatter_partials(parts, name="scatter_partials_" + tag)


def _reduce_scatter_finish(parts, recvd, chip, *, tag, sum_after=()):
    mine = _sum_partials(chip, parts, recvd, name="sum_partials_" + tag, after=sum_after)
    return mine, _swap_reduced_halves(mine, name="swap_reduced_halves_" + tag)


ADAM_BLOCK_ELEMS = 256 * 1024


def _adam_rows(rows, cols):
    tm = rows
    while tm * cols > ADAM_BLOCK_ELEMS and tm % 16 == 0:
        tm //= 2
    return tm


def _adam_step(wv, gv, mv, vv):
    m2 = ADAM_B1 * mv + (1.0 - ADAM_B1) * gv
    v2 = ADAM_B2 * vv + (1.0 - ADAM_B2) * (gv * gv)
    m_hat = m2 / (1.0 - ADAM_B1 ** ADAM_STEP)
    v_hat = v2 / (1.0 - ADAM_B2 ** ADAM_STEP)
    return -ADAM_LR * (m_hat / (jnp.sqrt(v_hat) + ADAM_EPS) + ADAM_WD * wv), m2, v2


def _adamw_each(ws, gs, ms, vs, *, name, after=()):
    n = len(ws)
    whole = pl.BlockSpec(memory_space=pltpu.VMEM)

    def body(*refs):
        ins, outs = refs[:4 * n], refs[4 * n + len(after):]
        for i in range(n):
            res = _adam_step(*(ins[k * n + i][...] for k in range(4)))
            for k in range(3):
                outs[k * n + i][...] = res[k]

    out = pl.pallas_call(body, in_specs=[whole] * (4 * n) + [HBM_OPERAND] * len(after),
                         out_shape=[_sds(w.shape) for w in ws] * 3, name=name)(*ws, *gs, *ms, *vs, *after)
    return out[:n], out[n:2 * n], out[2 * n:]


def _adamw_halves(core, w, g_mine, g_theirs, m, v, *, name, after=()):
    rows, cols = w.shape
    hw = rows // 2
    tm = _adam_rows(hw, cols)
    per_half = hw // tm

    def body(core_ref, w_ref, gm_ref, gt_ref, m_ref, v_ref, *rest):
        g_out, d_out, m_out, v_out = rest[len(after):]
        mine = (pl.program_id(0) // per_half) == core_ref[0]
        g = jnp.where(mine, gm_ref[...], gt_ref[...])
        d, m2, v2 = _adam_step(w_ref[...], g, m_ref[...], v_ref[...])
        g_out[...] = g
        d_out[...] = d
        m_out[...] = m2
        v_out[...] = v2

    full = pl.BlockSpec((tm, cols), lambda i, core_ref: (i, 0))

    def half(wanted):
        def index(i, core_ref):
            in_use = ((i // per_half) == core_ref[0]) == wanted
            return (jnp.where(in_use, i % per_half, 0), 0)
        return pl.BlockSpec((tm, cols), index)

    return pl.pallas_call(
        body,
        grid_spec=pltpu.PrefetchScalarGridSpec(
            num_scalar_prefetch=1, grid=(rows // tm,),
            in_specs=[full, half(True), half(False), full, full] + [HBM_OPERAND] * len(after),
            out_specs=[full, full, full, full]),
        out_shape=[_sds((rows, cols))] * 4, compiler_params=_cp(("parallel",)), name=name)(
            core, w, g_mine, g_theirs, m, v, *after)


HELD_TRANSPOSED = ("w_ff_gate", "w_ff_up")


def _as_rows(name, arr):
    return arr[0].T if name in HELD_TRANSPOSED else arr[0]


def _from_rows(name, arr2d):
    return (arr2d.T if name in HELD_TRANSPOSED else arr2d)[None]


STORED_SWAPPED = ("ssm_b_re", "ssm_b_im")


def _as_stored(name, arr):
    return jnp.swapaxes(arr, -1, -2) if name in STORED_SWAPPED else arr


SMALL = ("b_gate", "ssm_a_re", "ssm_a_im", "ssm_log_dt", "ssm_b_re", "ssm_b_im", "ssm_c_re", "ssm_c_im", "ssm_d",
         "ln1_g", "ln1_b", "ln2_g", "ln2_b")
GATHER_GROUPS = (("w_in", ("w_in",)), ("mixer", ("w_attn_br", "w_ssm_br", "w_glu", "w_out")),
                 ("ffn_up", ("w_ff_gate", "w_ff_up")), ("ffn_down", ("w_ff_down",)))
REDUCE_GROUPS = (("ffn", ("w_ff_down", "w_ff_gate", "w_ff_up")),
                 ("mixer", ("w_out", "w_ssm_br", "w_glu", "w_attn_br")), ("w_in", ("w_in",)))
WEIGHTS = ("w_in", "b_gate", "w_attn_br", "w_ssm_br", "w_out", "ssm_a_re", "ssm_a_im", "ssm_log_dt", "ssm_b_re",
           "ssm_b_im", "ssm_c_re", "ssm_c_im", "ssm_d", "w_glu", "ln1_g", "ln1_b", "w_ff_gate", "w_ff_up", "w_ff_down",
           "ln2_g", "ln2_b")


def kernel(x, w_in, b_gate, w_attn_br, w_ssm_br, w_out, ssm_a_re, ssm_a_im, ssm_log_dt, ssm_b_re, ssm_b_im, ssm_c_re, ssm_c_im, ssm_d, w_glu, ln1_g, ln1_b, w_ff_gate, w_ff_up, w_ff_down, ln2_g, ln2_b, loss_target, m_w_in, m_b_gate, m_w_attn_br, m_w_ssm_br, m_w_out, m_ssm_a_re, m_ssm_a_im, m_ssm_log_dt, m_ssm_b_re, m_ssm_b_im, m_ssm_c_re, m_ssm_c_im, m_ssm_d, m_w_glu, m_ln1_g, m_ln1_b, m_w_ff_gate, m_w_ff_up, m_w_ff_down, m_ln2_g, m_ln2_b, v_w_in, v_b_gate, v_w_attn_br, v_w_ssm_br, v_w_out, v_ssm_a_re, v_ssm_a_im, v_ssm_log_dt, v_ssm_b_re, v_ssm_b_im, v_ssm_c_re, v_ssm_c_im, v_ssm_d, v_w_glu, v_ln1_g, v_ln1_b, v_w_ff_gate, v_w_ff_up, v_w_ff_down, v_ln2_g, v_ln2_b):
    given = dict(locals())
    px, py, pc = _place()
    chip = 2 * px + py
    core_s = jnp.reshape(pc, (1,)).astype(jnp.int32)
    chip_s = jnp.reshape(chip, (1,)).astype(jnp.int32)

    wts = {}
    for tag, names in GATHER_GROUPS:
        shards = [_as_rows(n, given[n]).astype(BF16) for n in names]
        first = tag == GATHER_GROUPS[0][0]
        slots = _gather_weights(shards, name="gather_" + tag, own_slot=not first)
        if first:
            wts["w_in_parts"] = (chip_s, shards[0], slots[0])
            slots = [lax.dynamic_update_slice(g, s[None], (chip, 0, 0)) for g, s in zip(slots, shards)]
        wts.update(zip(names, slots))
    ncol = D_MODEL // N_CHIPS
    bg_mine = jnp.where(pc == 0, b_gate[0], jnp.zeros_like(b_gate[0]))
    bg_full = lax.dynamic_update_slice(jnp.zeros((2, D_MODEL), F32), bg_mine, (0, chip * ncol))
    bg_slots = _exchange_each([bg_full], name="exchange_gate_bias")
    bg_full = _sum_slots(bg_slots, name="sum_gate_bias")[0]
    small = {n: given[n][0] for n in SMALL if n.startswith("ssm")}
    small.update({n: given[n] for n in ("ln1_g", "ln1_b", "ln2_g", "ln2_b")})
    small["b_gate"] = bg_full

    groups = dict(REDUCE_GROUPS)
    parts, recvd, reduced, sent = {}, {}, {}, {}
    grads, delta, new_m, new_v, done = {}, {}, {}, {}, {}

    def start(tag, big_g, add_after):
        parts[tag], recvd[tag] = _reduce_scatter_start([big_g[n] for n in groups[tag]], core_s, tag=tag,
                                                       add_after=add_after)
        return parts[tag]

    def reduce_sum(tag, after):
        reduced[tag] = _reduce_scatter_finish(parts[tag], recvd[tag], chip_s, tag=tag, sum_after=after)
        return reduced[tag][0]

    def adam(tag, after):
        for n, g_mine, g_theirs in zip(groups[tag], *reduced[tag]):
            res = _adamw_halves(core_s, _as_rows(n, given[n]), g_mine, g_theirs, _as_rows(n, given["m_" + n]),
                                _as_rows(n, given["v_" + n]), name="adamw_" + n, after=after)
            done[n] = res[1]
            grads[n], delta[n], new_m[n], new_v[n] = [_from_rows(n, r) for r in res]

    def ffn_grads(big_g, norm_bwd):
        return start("ffn", big_g, (norm_bwd,))

    def mixer_grads(big_g, scan_bwd):
        return start("mixer", big_g, (scan_bwd, *reduce_sum("ffn", (scan_bwd,))))

    def small_grads(small_g, loss_mine):
        stored = [_as_stored(n, small_g[n]) for n in SMALL] + [loss_mine]
        stored = [g.reshape(1, -1) if g.ndim < 2 else g for g in stored]
        sent["slots"] = _exchange_each(stored, name="exchange_small")
        return stored

    grad_x_after, g_w_in, attention_bwd = _local_step(x[0], loss_target[0], wts, small,
                                                      ffn_grads, mixer_grads, small_grads)

    reduce_sum("mixer", (attention_bwd,))
    summed = list(_sum_slots(sent["slots"], name="sum_small", after=(g_w_in,)))
    adam("mixer", (g_w_in,))
    start("w_in", {"w_in": g_w_in}, (summed[0], *[done[n] for n in groups["mixer"]]))
    in_flight = (parts["w_in"][0],)
    grad_x = grad_x_after(in_flight)
    adam("ffn", in_flight)
    loss = summed.pop()[0, 0]
    at = SMALL.index("b_gate")
    summed[at] = lax.dynamic_slice(summed[at], (0, chip * ncol), (2, ncol))
    held = [[_as_stored(n, given[prefix + n]).reshape(g.shape) for n, g in zip(SMALL, summed)]
            for prefix in ("", "m_", "v_")]
    small_out = _adamw_each(held[0], summed, held[1], held[2], name="adamw_small", after=in_flight)
    for out, arrs in zip((grads, delta, new_m, new_v), (summed, *small_out)):
        out.update((n, _as_stored(n, a).reshape(given[n].shape)) for n, a in zip(SMALL, arrs))
    reduce_sum("w_in", (*[done[n] for n in groups["ffn"]], small_out[0][0], grad_x))
    adam("w_in", ())

    return (loss, grad_x.reshape(x.shape), *[grads[n] for n in WEIGHTS], *[delta[n] for n in WEIGHTS],
            *[new_m[n] for n in WEIGHTS], *[new_v[n] for n in WEIGHTS])
```

```python
import math

import jax
import jax.numpy as jnp
from jax import lax
from jax.experimental import pallas as pl
from jax.experimental.pallas import tpu as pltpu
from jax.experimental.pallas import tpu_sc as plsc

F32 = jnp.float32
BF16 = jnp.bfloat16
MESH = pl.DeviceIdType.MESH

D_MODEL = 1024
SEQ = 2048
HEAD_DIM = 64
ATTN_HEADS = 8
DILATIONS = (1, 4, 16)
ATTN_WIDTH = ATTN_HEADS * HEAD_DIM
QKV_WIDTH = 3 * ATTN_WIDTH
BLOCK = 128
ROPE_THETA = 10000.0
NEG_INF = -1e30
SSM_GROUP = 16
SSM_GROUPS = 32
SSM_WIDTH = 512
SSM_STATE = 64
SSM_LANES = SSM_GROUPS * SSM_STATE
SCAN_CHUNKS = 8
SCAN_STEPS = SEQ // SCAN_CHUNKS
IN_WIDTH = 3 * QKV_WIDTH + SSM_WIDTH + 2 * D_MODEL
D_FF = 2816
N_CHIPS = 4
N_DEV = 8
DN_ALPHA = 2.0 ** 0.25
LN_EPS = 1e-5
ADAM_LR = 0.001
ADAM_B1 = 0.9
ADAM_B2 = 0.999
ADAM_EPS = 1e-08
ADAM_WD = 0.01
ADAM_STEP = 10
GELU_C = math.sqrt(2.0 / math.pi)
GELU_K = 0.044715

VMEM_LIMIT_BYTES = 56 * 1024 * 1024


def _sds(shape, dtype=F32):
    return jax.ShapeDtypeStruct(tuple(shape), dtype)


def _cp(semantics=None):
    return pltpu.CompilerParams(dimension_semantics=semantics, vmem_limit_bytes=VMEM_LIMIT_BYTES)


HBM_OPERAND = pl.BlockSpec(memory_space=pl.ANY)


def _matmul(a, b, *, grid, a_spec, b_spec, o_spec, out_shape, dims, k_axis=None, name, after=()):
    nk = grid[k_axis] if k_axis is not None else 1
    o_block = tuple(d for d in o_spec.block_shape if d is not None)
    n_after = len(after)

    def body(a_ref, b_ref, *rest):
        o_ref, acc = rest[n_after], rest[n_after + 1:]
        part = lax.dot_general(a_ref[...].astype(BF16), b_ref[...].astype(BF16),
                               (((dims[0],), (dims[1],)), ((), ())), preferred_element_type=F32)
        if k_axis is None:
            o_ref[...] = part.astype(o_ref.dtype)
        else:
            k = pl.program_id(k_axis)

            @pl.when(k == 0)
            def _():
                acc[0][...] = part

            @pl.when(k > 0)
            def _():
                acc[0][...] += part

            @pl.when(k == nk - 1)
            def _():
                o_ref[...] = acc[0][...].astype(o_ref.dtype)

    sem = tuple("arbitrary" if ax == k_axis else "parallel" for ax in range(len(grid)))
    return pl.pallas_call(
        body, grid=grid, in_specs=[a_spec, b_spec] + [HBM_OPERAND] * n_after, out_specs=o_spec, out_shape=out_shape,
        scratch_shapes=[pltpu.VMEM(o_block, F32)] if k_axis is not None else [],
        compiler_params=_cp(sem), name=name)(a, b, *after)


def _mm_cols_nt(dy, wg, *, tm, name, out_dtype=F32, after=()):
    k, ns = wg.shape[1], wg.shape[2]
    m = dy.shape[0]
    a_spec = pl.BlockSpec((tm, ns), lambda i, s: (i, s))
    return _matmul(dy, wg, grid=(m // tm, N_CHIPS), a_spec=a_spec,
                   b_spec=pl.BlockSpec((None, k, ns), lambda i, s: (s, 0, 0)),
                   o_spec=pl.BlockSpec((tm, k), lambda i, s: (i, 0)),
                   out_shape=_sds((m, k), out_dtype), dims=(1, 1), k_axis=1, name=name, after=after)


def _mm_cols_tn(a, dy, *, ns, name, after=()):
    m, k = a.shape
    return _matmul(a, dy, grid=(N_CHIPS,), a_spec=pl.BlockSpec((m, k), lambda s: (0, 0)),
                   b_spec=pl.BlockSpec((m, ns), lambda s: (0, s)),
                   o_spec=pl.BlockSpec((None, k, ns), lambda s: (s, 0, 0)),
                   out_shape=_sds((N_CHIPS, k, ns), BF16), dims=(0, 0), name=name, after=after)


def _mm_rows_tn(a, dy, *, name):
    m, k = a.shape
    rows, n = k // N_CHIPS, dy.shape[1]
    return _matmul(a, dy, grid=(N_CHIPS,), a_spec=pl.BlockSpec((m, rows), lambda s: (0, s)),
                   b_spec=pl.BlockSpec((m, n), lambda s: (0, 0)),
                   o_spec=pl.BlockSpec((None, rows, n), lambda s: (s, 0, 0)),
                   out_shape=_sds((N_CHIPS, rows, n), BF16), dims=(0, 0), name=name)


def _rowwise(fn, tiled, full, outs, accs=(), *, tm, name, after=()):
    args, in_specs = [], []
    for t in tiled:
        if isinstance(t, tuple):
            arr, w, cb = t
            in_specs.append(pl.BlockSpec((tm, w), lambda i, cb=cb: (i, cb)))
        else:
            arr = t
            in_specs.append(pl.BlockSpec((tm, arr.shape[1]), lambda i: (i, 0)))
        args.append(arr)
    rows = args[0].shape[0]
    for f in full:
        in_specs.append(pl.BlockSpec(f.shape, lambda i, nd=f.ndim: (0,) * nd))
        args.append(f)
    out_specs = [pl.BlockSpec((tm, o.shape[1]), lambda i: (i, 0)) for o in outs]
    out_specs += [pl.BlockSpec(a.shape, lambda i, nd=len(a.shape): (0,) * nd) for a in accs]
    n_in, n_out = len(args), len(outs)
    in_specs += [HBM_OPERAND] * len(after)
    first_out = n_in + len(after)

    def body(*refs):
        res = fn(*[r[...] for r in refs[:n_in]])
        res = res if isinstance(res, (tuple, list)) else (res,)
        for r, v in zip(refs[first_out:first_out + n_out], res[:n_out]):
            r[...] = v.astype(r.dtype)
        i = pl.program_id(0)
        for r, v in zip(refs[first_out + n_out:], res[n_out:]):
            @pl.when(i == 0)
            def _(r=r, v=v):
                r[...] = v

            @pl.when(i > 0)
            def _(r=r, v=v):
                r[...] += v

    res = pl.pallas_call(
        body, grid=(rows // tm,), in_specs=in_specs, out_specs=out_specs, out_shape=list(outs) + list(accs),
        compiler_params=_cp(("arbitrary",) if accs else ("parallel",)), name=name)(*args, *after)
    return res


def _colsum(v):
    return jnp.sum(v, axis=0, keepdims=True)


def _ln_stats(z):
    mu = jnp.mean(z, axis=-1, keepdims=True)
    zc = z - mu
    var = jnp.mean(zc * zc, axis=-1, keepdims=True)
    rstd = lax.rsqrt(var + LN_EPS)
    return zc * rstd, rstd


def _ln_bwd(dy, xhat, rstd, g):
    dxh = dy * g
    m1 = jnp.mean(dxh, axis=-1, keepdims=True)
    m2 = jnp.mean(dxh * xhat, axis=-1, keepdims=True)
    return rstd * (dxh - m1 - xhat * m2)


def _swap_halves(t):
    w = t.shape[-1]
    lane = lax.broadcasted_iota(jnp.int32, t.shape, t.ndim - 1)
    return jnp.where((lane % HEAD_DIM) < HEAD_DIM // 2, pltpu.roll(t, w - HEAD_DIM // 2, t.ndim - 1),
                     pltpu.roll(t, HEAD_DIM // 2, t.ndim - 1))


PHASES = max(DILATIONS)
PAIR = 2 * HEAD_DIM
UNITS = SEQ // BLOCK
UNIT_BATCH = 16
ROPE_ROWS = 256
TAIL_COLS = 256


def _to_phase_rows(t):
    return t.reshape(SEQ // PHASES, PHASES, t.shape[1]).transpose(1, 0, 2).reshape(t.shape)


def _reorder_rows(arr, plus=None, *, to_phase, name, scale=1.0):
    def body(*refs):
        o_ref = refs[-1]
        for rho in range(PHASES):
            phase = pl.ds(rho * BLOCK, BLOCK)
            strided = pl.ds(rho, BLOCK, stride=PHASES)
            src, dst = (strided, phase) if to_phase else (phase, strided)
            val = refs[0][src, :]
            if scale != 1.0:
                val = val * scale
            if plus is not None:
                val = val + refs[1][src, :]
            o_ref[dst, :] = val

    spec = pl.BlockSpec((SEQ, BLOCK), lambda j: (0, j))
    ins = [arr] if plus is None else [arr, plus]
    return pl.pallas_call(body, grid=(arr.shape[1] // BLOCK,), in_specs=[spec] * len(ins), out_specs=spec,
                          out_shape=_sds(arr.shape), compiler_params=_cp(("parallel",)), name=name)(*ins)


def _rope(t, cf, ss):
    return t * cf + _swap_halves(t) * ss


def _rope_transposed(d, cf, ss):
    return d * cf + _swap_halves(d * ss)


def _unit_pieces(u, dil):
    pieces, length = PHASES // dil, 8 * dil
    if dil == 1:
        rho, i = 0, u
    elif dil == PHASES:
        rho, i = u, 0
    else:
        rho, i = jnp.bitwise_and(u, dil - 1), jnp.right_shift(u, dil.bit_length() - 1)
    before = jnp.maximum(i - 1, 0)
    cur = [pl.multiple_of((rho + dil * k) * BLOCK + length * i, 8) for k in range(pieces)]
    prev = [pl.multiple_of((rho + dil * k) * BLOCK + length * before, 8) for k in range(pieces)]
    return i, cur, prev


def _load_tile(ref, starts, dil):
    return jnp.concatenate([ref[pl.ds(st, 8 * dil), :] for st in starts], axis=0)


def _store_tile(ref, starts, dil, val, head=None, accumulate=False):
    length = 8 * dil
    lanes = slice(None) if head is None else pl.ds(head * HEAD_DIM, HEAD_DIM)
    cols = slice(None) if head is None else slice(head * HEAD_DIM, (head + 1) * HEAD_DIM)
    for k, st in enumerate(starts):
        piece = val[k * length:(k + 1) * length, cols]
        if accumulate:
            ref[pl.ds(st, length), lanes] += piece
        else:
            ref[pl.ds(st, length), lanes] = piece


def _tile_position(idx, dil):
    pieces, length = PHASES // dil, 8 * dil
    return pieces * jnp.bitwise_and(idx, length - 1) + jnp.right_shift(idx, length.bit_length() - 1)


def _band_mask(i, dil):
    row = lax.broadcasted_iota(jnp.int32, (BLOCK, 2 * BLOCK), 0)
    col = lax.broadcasted_iota(jnp.int32, (BLOCK, 2 * BLOCK), 1)
    key_pos = _tile_position(jnp.bitwise_and(col, BLOCK - 1), dil) + jnp.where(col >= BLOCK, 0, -BLOCK)
    dist = _tile_position(row, dil) - key_pos
    return (dist >= 0) & (dist <= BLOCK) & ((col >= BLOCK) | (i > 0))


def _causal_mask():
    row = lax.broadcasted_iota(jnp.int32, (BLOCK, BLOCK), 0)
    col = lax.broadcasted_iota(jnp.int32, (BLOCK, BLOCK), 1)
    return row >= col


def _pair_views(col0):
    return [pl.BlockSpec((SEQ, PAIR), lambda hp, g=g: (0, col0 // PAIR + g * (ATTN_WIDTH // PAIR) + hp))
            for g in range(len(DILATIONS))]


def _project_shard(shard, x_ref, w_ref, cf_ref, ss_ref, o_ref):
    ns = w_ref.shape[1]
    tiles = ns // PAIR
    xb = x_ref[...].astype(BF16)
    cf, ss = cf_ref[...], ss_ref[...]

    def write(rotated, scaled):
        for t0 in range(0, tiles, 2):
            strip = jnp.dot(xb, w_ref[:, t0 * PAIR:(t0 + 2) * PAIR], preferred_element_type=F32)
            for t in (t0, t0 + 1):
                val = strip[:, (t - t0) * PAIR:(t - t0 + 1) * PAIR]
                if t < rotated:
                    val = _rope(val, cf, ss)
                    if t < scaled:
                        val = val * (1.0 / math.sqrt(HEAD_DIM))
                o_ref[:, t * PAIR:(t + 1) * PAIR] = val

    for s in range(N_CHIPS):
        rotated = min(max(2 * QKV_WIDTH - s * ns, 0), ns) // PAIR
        scaled = min(max(QKV_WIDTH - s * ns, 0), ns) // PAIR
        @pl.when(shard == s)
        def _(rotated=rotated, scaled=scaled):
            write(rotated, scaled)


def _project_in_own(chip, x, w_own, cos_f, sin_s):
    ns = w_own.shape[1]

    def body(chip_ref, x_ref, w_ref, cf_ref, ss_ref, o_ref):
        _project_shard(chip_ref[0], x_ref, w_ref, cf_ref, ss_ref, o_ref)

    table = pl.BlockSpec((FF_ROWS, PAIR), lambda i, chip_ref: (i, 0))
    return pl.pallas_call(
        body,
        grid_spec=pltpu.PrefetchScalarGridSpec(
            num_scalar_prefetch=1, grid=(SEQ // FF_ROWS,),
            in_specs=[pl.BlockSpec((FF_ROWS, D_MODEL), lambda i, chip_ref: (i, 0)),
                      pl.BlockSpec((D_MODEL, ns), lambda i, chip_ref: (0, 0)), table, table],
            out_specs=pl.BlockSpec((FF_ROWS, ns), lambda i, chip_ref: (i, chip_ref[0]))),
        out_shape=_sds((SEQ, N_CHIPS * ns)), compiler_params=_cp(("parallel",)), name="project_in_own")(
            chip, x, w_own, cos_f, sin_s)


def _project_in(chip, x, wg, cos_f, sin_s, started):
    ns = wg.shape[2]

    def other(j, chip_ref):
        return (chip_ref[0] + 1 + j) % N_CHIPS

    def body(chip_ref, x_ref, w_ref, cf_ref, ss_ref, started_ref, o_ref):
        _project_shard(other(pl.program_id(1), chip_ref), x_ref, w_ref, cf_ref, ss_ref, o_ref)

    table = pl.BlockSpec((FF_ROWS, PAIR), lambda i, j, chip_ref: (i, 0))
    return pl.pallas_call(
        body,
        grid_spec=pltpu.PrefetchScalarGridSpec(
            num_scalar_prefetch=1, grid=(SEQ // FF_ROWS, N_CHIPS - 1),
            in_specs=[pl.BlockSpec((FF_ROWS, D_MODEL), lambda i, j, chip_ref: (i, 0)),
                      pl.BlockSpec((None, D_MODEL, ns), lambda i, j, chip_ref: (other(j, chip_ref), 0, 0)),
                      table, table, HBM_OPERAND],
            out_specs=pl.BlockSpec((FF_ROWS, ns), lambda i, j, chip_ref: (i, other(j, chip_ref)))),
        out_shape=_sds((SEQ, N_CHIPS * ns)), input_output_aliases={5: 0},
        compiler_params=_cp(("parallel", "parallel")), name="project_in")(chip, x, wg, cos_f, sin_s, started)


def _attention_fwd(proj):
    ng = len(DILATIONS)

    def body(*refs):
        q_refs, k_refs, v_refs = refs[:ng], refs[ng:2 * ng], refs[2 * ng:3 * ng]
        attn_ref, lse_ref = refs[3 * ng:]
        qr_refs, kr_refs = q_refs, k_refs
        first = lax.broadcasted_iota(jnp.int32, (BLOCK, PAIR), 1) < HEAD_DIM
        for g, dil in enumerate(DILATIONS):
            two_blocks = SEQ // dil > BLOCK

            def units(t, carry, g=g, dil=dil, two_blocks=two_blocks):
                picked = [_unit_pieces(t * UNIT_BATCH + j, dil) for j in range(UNIT_BATCH)]

                def tiles(ref, with_prev=False):
                    if with_prev and two_blocks:
                        return jnp.stack([jnp.concatenate([_load_tile(ref, prev, dil), _load_tile(ref, rows, dil)],
                                                          axis=0) for _, rows, prev in picked])
                    return jnp.stack([_load_tile(ref, rows, dil) for _, rows, _ in picked])

                qq = tiles(qr_refs[g]).astype(BF16)
                kk = tiles(kr_refs[g], True).astype(BF16)
                vv = tiles(v_refs[g], True).astype(BF16)
                if two_blocks:
                    valid = jnp.stack([_band_mask(i, dil) for i, _, _ in picked])
                else:
                    valid = _causal_mask()[None]
                mine = first[None]
                zero = jnp.zeros_like(qq)
                outs, lses = [], []
                for qh in (jnp.where(mine, qq, zero), jnp.where(mine, zero, qq)):
                    s = jnp.einsum("pqd,pkd->pqk", qh, kk, preferred_element_type=F32)
                    s = jnp.where(valid, s, NEG_INF)
                    m = jnp.max(s, axis=-1, keepdims=True)
                    p = jnp.exp(s - m)
                    l = jnp.sum(p, axis=-1, keepdims=True)
                    outs.append(jnp.einsum("pqk,pkd->pqd", p.astype(BF16), vv, preferred_element_type=F32) * (1.0 / l))
                    lses.append(m + jnp.log(l))
                o = jnp.where(mine, outs[0], outs[1])
                lse = jnp.where(mine, lses[0], lses[1])
                if g > 0:
                    lse_old = tiles(lse_ref)
                    m = jnp.maximum(lse_old, lse)
                    lse_new = m + jnp.log(jnp.exp(lse_old - m) + jnp.exp(lse - m))
                    o = tiles(attn_ref) * jnp.exp(lse_old - lse_new) + o * jnp.exp(lse - lse_new)
                    lse = lse_new
                for j, (_, rows, _) in enumerate(picked):
                    _store_tile(attn_ref, rows, dil, o[j])
                    _store_tile(lse_ref, rows, dil, lse[j])
                return carry

            lax.fori_loop(0, UNITS // UNIT_BATCH, units, 0)

    out = pl.BlockSpec((SEQ, PAIR), lambda hp: (0, hp))
    return pl.pallas_call(
        body, grid=(ATTN_WIDTH // PAIR,),
        in_specs=_pair_views(0) + _pair_views(QKV_WIDTH) + _pair_views(2 * QKV_WIDTH),
        out_specs=[out, out], out_shape=[_sds((SEQ, ATTN_WIDTH)), _sds((SEQ, ATTN_WIDTH))],
        compiler_params=_cp(("parallel",)), name="attention_fwd")(*([proj] * (3 * ng)))


def _attention_bwd(proj, cos_f, sin_s, d_attn, attn, lse, d_u, d_gl):
    pairs = ATTN_WIDTH // PAIR
    last = len(DILATIONS) * pairs - 1

    def accumulate(dil, qr_ref, kr_ref, v_ref, do_ref, o_ref, lse_ref, dq_acc, dk_acc, dv_acc):
        two_blocks = SEQ // dil > BLOCK
        dk_acc[...] = jnp.zeros_like(dk_acc)
        dv_acc[...] = jnp.zeros_like(dv_acc)
        nk = 2 * BLOCK if two_blocks else BLOCK
        first = lax.broadcasted_iota(jnp.int32, (BLOCK, PAIR), 1) < HEAD_DIM
        first_k = lax.broadcasted_iota(jnp.int32, (nk, PAIR), 1) < HEAD_DIM

        def units(t, carry):
            picked = [_unit_pieces(t * UNIT_BATCH + j, dil) for j in range(UNIT_BATCH)]

            def tiles(ref, with_prev=False):
                if with_prev and two_blocks:
                    return jnp.stack([jnp.concatenate([_load_tile(ref, prev, dil), _load_tile(ref, rows, dil)], axis=0)
                                      for _, rows, prev in picked])
                return jnp.stack([_load_tile(ref, rows, dil) for _, rows, _ in picked])

            qq = tiles(qr_ref).astype(BF16)
            kk = tiles(kr_ref, True).astype(BF16)
            vv = tiles(v_ref, True).astype(BF16)
            dof = tiles(do_ref)
            dd = dof * tiles(o_ref)
            lse3 = tiles(lse_ref)
            dob = dof.astype(BF16)
            if two_blocks:
                valid = jnp.stack([_band_mask(i, dil) for i, _, _ in picked])
            else:
                valid = _causal_mask()[None]
            zq, zf = jnp.zeros_like(qq), jnp.zeros_like(dd)
            dqs, dks, dvs = [], [], []
            for head in range(2):
                mine = first[None] if head == 0 else jnp.logical_not(first)[None]
                delta = jnp.sum(jnp.where(mine, dd, zf), axis=-1, keepdims=True)
                lse_h = lse3[:, :, head * HEAD_DIM:head * HEAD_DIM + 1]
                s = jnp.einsum("pqd,pkd->pqk", jnp.where(mine, qq, zq), kk, preferred_element_type=F32)
                p = jnp.where(valid, jnp.exp(s - lse_h), 0.0)
                dp = jnp.einsum("pqd,pkd->pqk", jnp.where(mine, dob, zq), vv, preferred_element_type=F32)
                ds = (p * (dp - delta)).astype(BF16)
                dqs.append(jnp.einsum("pqk,pkd->pqd", ds, kk, preferred_element_type=F32))
                dks.append(jnp.einsum("pqk,pqd->pkd", ds, qq, preferred_element_type=F32))
                dvs.append(jnp.einsum("pqk,pqd->pkd", p.astype(BF16), dob, preferred_element_type=F32))
            dq = jnp.where(first[None], dqs[0], dqs[1])
            dk = jnp.where(first_k[None], dks[0], dks[1])
            dv = jnp.where(first_k[None], dvs[0], dvs[1])
            for j, (_, rows, prev) in enumerate(picked):
                _store_tile(dq_acc, rows, dil, dq[j])
                _store_tile(dk_acc, rows, dil, dk[j, nk - BLOCK:], accumulate=True)
                _store_tile(dv_acc, rows, dil, dv[j, nk - BLOCK:], accumulate=True)
                if two_blocks:
                    _store_tile(dk_acc, prev, dil, dk[j, :BLOCK], accumulate=True)
                    _store_tile(dv_acc, prev, dil, dv[j, :BLOCK], accumulate=True)
            return carry

        lax.fori_loop(0, UNITS // UNIT_BATCH, units, 0)

    def body(qr_ref, kr_ref, v_ref, cf_ref, ss_ref, do_ref, o_ref, lse_ref, du_ref, dgl_ref, out_ref,
             dq_acc, dk_acc, dv_acc, dq_buf, dk_buf, dv_buf, sems):
        step = pl.program_id(0) * pairs + pl.program_id(1)

        def columns(at):
            return [pltpu.make_async_copy(
                buf, out_ref.at[:, pl.ds(pl.multiple_of(j * QKV_WIDTH + at * PAIR, PAIR), PAIR)], sems.at[j])
                for j, buf in enumerate((dq_buf, dk_buf, dv_buf))]

        def tail(ref, col0, at):
            cols = pl.ds(pl.multiple_of(col0 + at * TAIL_COLS, TAIL_COLS), TAIL_COLS)
            return pltpu.make_async_copy(ref, out_ref.at[:, cols], sems.at[3])

        gl_steps, u_steps = 2 * D_MODEL // TAIL_COLS, SSM_WIDTH // TAIL_COLS
        from_gl = step < gl_steps
        from_u = jnp.logical_and(step >= gl_steps, step < gl_steps + u_steps)
        tail_gl = tail(dgl_ref, 3 * QKV_WIDTH + SSM_WIDTH, step)
        tail_u = tail(du_ref, 3 * QKV_WIDTH, step - gl_steps)
        pl.when(from_gl)(tail_gl.start)
        pl.when(from_u)(tail_u.start)

        for g, dil in enumerate(DILATIONS):
            @pl.when(pl.program_id(0) == g)
            def _(dil=dil):
                accumulate(dil, qr_ref, kr_ref, v_ref, do_ref, o_ref, lse_ref, dq_acc, dk_acc, dv_acc)

        @pl.when(step > 0)
        def _():
            for cp in columns(step - 1):
                cp.wait()

        def finish(t, carry):
            rows = pl.ds(pl.multiple_of(t * ROPE_ROWS, ROPE_ROWS), ROPE_ROWS)
            cf, ss = cf_ref[rows, :], ss_ref[rows, :]
            dq = dq_acc[rows, :] * (1.0 / math.sqrt(HEAD_DIM))
            dq_buf[rows, :] = _rope_transposed(dq, cf, ss).astype(BF16)
            dk_buf[rows, :] = _rope_transposed(dk_acc[rows, :], cf, ss).astype(BF16)
            dv_buf[rows, :] = dv_acc[rows, :].astype(BF16)
            return carry

        lax.fori_loop(0, SEQ // ROPE_ROWS, finish, 0)
        for cp in columns(step):
            cp.start()
        pl.when(from_gl)(tail_gl.wait)
        pl.when(from_u)(tail_u.wait)

        @pl.when(step == last)
        def _():
            for cp in columns(step):
                cp.wait()

    whole = pl.BlockSpec((SEQ, PAIR), lambda g, hp: (0, 0))
    pair = pl.BlockSpec((SEQ, PAIR), lambda g, hp: (0, hp))
    views = [pl.BlockSpec((SEQ, PAIR), lambda g, hp, c0=col0 // PAIR: (0, c0 + g * pairs + hp))
             for col0 in (0, QKV_WIDTH, 2 * QKV_WIDTH)]
    gl_blocks, u_blocks = 2 * D_MODEL // TAIL_COLS, SSM_WIDTH // TAIL_COLS
    assert gl_blocks + u_blocks <= last + 1
    gl_spec = pl.BlockSpec((SEQ, TAIL_COLS), lambda g, hp: (0, jnp.minimum(g * pairs + hp, gl_blocks - 1)))
    u_spec = pl.BlockSpec((SEQ, TAIL_COLS),
                          lambda g, hp: (0, jnp.clip(g * pairs + hp - gl_blocks, 0, u_blocks - 1)))
    return pl.pallas_call(
        body, grid=(len(DILATIONS), pairs),
        in_specs=views + [whole, whole, pair, pair, pair, u_spec, gl_spec],
        out_specs=HBM_OPERAND, out_shape=_sds((SEQ, IN_WIDTH), BF16),
        scratch_shapes=[pltpu.VMEM((SEQ, PAIR), F32)] * 3 + [pltpu.VMEM((SEQ, PAIR), BF16)] * 3
        + [pltpu.SemaphoreType.DMA((4,))],
        compiler_params=_cp(("arbitrary", "arbitrary")), name="attention_bwd")(
            proj, proj, proj, cos_f, sin_s, d_attn, attn, lse, d_u, d_gl)


def _cmul(ar, ai, br, bi):
    return ar * br - ai * bi, ar * bi + ai * br


def _pow256(ar, ai):
    for _ in range(8):
        ar, ai = _cmul(ar, ai, ar, ai)
    return ar, ai


def _chunk_carries(first_r, first_i, pr, pi, reverse):
    rows = lax.broadcasted_iota(jnp.int32, first_r.shape, 0)
    out_r = jnp.zeros_like(first_r)
    out_i = jnp.zeros_like(first_i)
    hr = jnp.zeros_like(first_r[0:1])
    hi = jnp.zeros_like(hr)
    order = range(SCAN_CHUNKS - 1, -1, -1) if reverse else range(SCAN_CHUNKS)
    for c in order:
        out_r = jnp.where(rows == c, hr, out_r)
        out_i = jnp.where(rows == c, hi, out_i)
        tr, ti = _cmul(pr[0:1], pi[0:1], hr, hi)
        hr = first_r[c:c + 1] + tr
        hi = first_i[c:c + 1] + ti
    return out_r, out_i


def _tile(j):
    return pl.ds(pl.multiple_of(j * SCAN_CHUNKS, SCAN_CHUNKS), SCAN_CHUNKS)


def _to_scan_rows(t):
    per = SCAN_STEPS // PHASES
    return t.reshape(PHASES, SCAN_CHUNKS, per, t.shape[1]).transpose(2, 0, 1, 3).reshape(t.shape)


def _from_scan_rows(t):
    per = SCAN_STEPS // PHASES
    return t.reshape(per, PHASES, SCAN_CHUNKS, t.shape[1]).transpose(1, 2, 0, 3).reshape(t.shape)


def _scan_in_place(hr_ref, hi_ref, a_r, a_i):
    def local(j, carry):
        tr, ti = _cmul(a_r, a_i, carry[0], carry[1])
        nr = tr + hr_ref[_tile(j), :]
        ni = ti + hi_ref[_tile(j), :]
        hr_ref[_tile(j), :] = nr
        hi_ref[_tile(j), :] = ni
        return nr, ni

    zero = jnp.zeros_like(a_r)
    last_r, last_i = lax.fori_loop(0, SCAN_STEPS, local, (zero, zero), unroll=4)
    pr, pi = _pow256(a_r, a_i)
    er, ei = _chunk_carries(last_r, last_i, pr, pi, reverse=False)

    def fix(j, carry):
        tr, ti = _cmul(carry[0], carry[1], er, ei)
        hr_ref[_tile(j), :] += tr
        hi_ref[_tile(j), :] += ti
        return _cmul(carry[0], carry[1], a_r, a_i)

    lax.fori_loop(0, SCAN_STEPS, fix, (a_r, a_i), unroll=4)
    return er, ei


def _reverse_scan_in_place(lr_ref, li_ref, hr_ref, hi_ref, er, ei, a_r, a_i):
    def local(t, carry):
        j = SCAN_STEPS - 1 - t
        tr, ti = _cmul(a_r, a_i, carry[0], carry[1])
        nr = tr + lr_ref[_tile(j), :]
        ni = ti + li_ref[_tile(j), :]
        lr_ref[_tile(j), :] = nr
        li_ref[_tile(j), :] = ni
        return nr, ni

    zero = jnp.zeros_like(a_r)
    first_r, first_i = lax.fori_loop(0, SCAN_STEPS, local, (zero, zero), unroll=4)
    pr, pi = _pow256(a_r, a_i)
    nxt_r, nxt_i = _chunk_carries(first_r, first_i, pr, pi, reverse=True)

    def accumulate(lam_r, lam_i, hp_r, hp_i, acc):
        return (acc[0] + lam_r * hp_r + lam_i * hp_i, acc[1] + lam_i * hp_r - lam_r * hp_i)

    def fix(t, carry):
        qr, qi, acc_r, acc_i = carry
        j = SCAN_STEPS - 1 - t
        tr, ti = _cmul(qr, qi, nxt_r, nxt_i)
        lam_r = lr_ref[_tile(j), :] + tr
        lam_i = li_ref[_tile(j), :] + ti
        lr_ref[_tile(j), :] = lam_r
        li_ref[_tile(j), :] = lam_i
        acc_r, acc_i = accumulate(lam_r, lam_i, hr_ref[_tile(j - 1), :], hi_ref[_tile(j - 1), :], (acc_r, acc_i))
        qr, qi = _cmul(qr, qi, a_r, a_i)
        return qr, qi, acc_r, acc_i

    qr, qi, acc_r, acc_i = lax.fori_loop(0, SCAN_STEPS - 1, fix, (a_r, a_i, zero, zero), unroll=4)
    tr, ti = _cmul(qr, qi, nxt_r, nxt_i)
    lam_r = lr_ref[_tile(0), :] + tr
    lam_i = li_ref[_tile(0), :] + ti
    lr_ref[_tile(0), :] = lam_r
    li_ref[_tile(0), :] = lam_i
    acc_r, acc_i = accumulate(lam_r, lam_i, er, ei, (acc_r, acc_i))
    return jnp.sum(acc_r, axis=0, keepdims=True), jnp.sum(acc_i, axis=0, keepdims=True)


def _rope_tables():
    half = HEAD_DIM // 2
    inv_freq = ROPE_THETA ** (-jnp.arange(half, dtype=F32) / half)
    ang = jnp.arange(SEQ, dtype=F32)[:, None] * inv_freq[None, :]
    cos, sin = jnp.cos(ang), jnp.sin(ang)
    cos_f = jnp.concatenate([cos, cos, cos, cos], axis=1)
    sin_s = jnp.concatenate([-sin, sin, -sin, sin], axis=1)
    return cos_f, sin_s


def _ssm_discretise(a_re, a_im, log_dt, b_re, b_im):
    lam = lax.complex(a_re, a_im)
    dt = jnp.exp(log_dt)[:, None]
    a_bar = jnp.exp(lam * dt)
    b_bar = ((a_bar - 1.0) / lam)[..., None] * lax.complex(b_re, b_im)
    return a_bar.real, a_bar.imag, b_bar.real, b_bar.imag


SSM_SLABS = 4
SLAB_GROUPS = SSM_GROUPS // SSM_SLABS
SLAB_IN = SSM_WIDTH // SSM_SLABS
SLAB_STATE = SSM_LANES // SSM_SLABS


def _slab_block_diag(blocks):
    _, r, c = blocks.shape
    eye = jnp.eye(SLAB_GROUPS, dtype=blocks.dtype)
    b5 = blocks.reshape(SSM_SLABS, SLAB_GROUPS, r, 1, c) * eye[None, :, None, :, None]
    return b5.reshape(SSM_SLABS, SLAB_GROUPS * r, SLAB_GROUPS * c)


def _diag_blocks(a, b):
    ra, cb = a.shape[1], b.shape[1]
    wa, wb = ra // SLAB_GROUPS, cb // SLAB_GROUPS
    d = lax.dot_general(a, b, (((0,), (0,)), ((), ())), preferred_element_type=F32)
    row_g = jnp.right_shift(lax.broadcasted_iota(jnp.int32, (ra, cb), 0), wa.bit_length() - 1)
    col_g = jnp.right_shift(lax.broadcasted_iota(jnp.int32, (ra, cb), 1), wb.bit_length() - 1)
    d = jnp.where(row_g == col_g, d, 0.0)
    fold = (jnp.bitwise_and(lax.broadcasted_iota(jnp.int32, (cb, wb), 0), wb - 1)
            == lax.broadcasted_iota(jnp.int32, (cb, wb), 1)).astype(F32)
    return jnp.dot(d, fold, preferred_element_type=F32, precision=lax.Precision.HIGHEST)


def _slab_specs():
    tok = pl.BlockSpec((SEQ, SLAB_IN), lambda j: (0, j))
    state = pl.BlockSpec((SEQ, SLAB_STATE), lambda j: (0, j))
    b_in = pl.BlockSpec((None, SLAB_IN, SLAB_STATE), lambda j: (j, 0, 0))
    c_out = pl.BlockSpec((None, SLAB_STATE, SLAB_IN), lambda j: (j, 0, 0))
    vec = pl.BlockSpec((1, SLAB_STATE), lambda j: (0, j))
    ent = pl.BlockSpec((SCAN_CHUNKS, SLAB_STATE), lambda j: (0, j))
    return tok, state, b_in, c_out, vec, ent


def _ssm_forward(u, b_in_r, b_in_i, c_out_r, c_out_ni, a_r, a_i):
    def body(u_ref, br_ref, bi_ref, cr_ref, ci_ref, ar_ref, ai_ref, y_ref, hr_ref, hi_ref, er_ref, ei_ref):
        uu = u_ref[...]
        hr_ref[...] = jnp.dot(uu, br_ref[...], preferred_element_type=F32)
        hi_ref[...] = jnp.dot(uu, bi_ref[...], preferred_element_type=F32)
        a_re = jnp.broadcast_to(ar_ref[...], (SCAN_CHUNKS, SLAB_STATE))
        a_im = jnp.broadcast_to(ai_ref[...], (SCAN_CHUNKS, SLAB_STATE))
        er_ref[...], ei_ref[...] = _scan_in_place(hr_ref, hi_ref, a_re, a_im)
        y_ref[...] = (jnp.dot(hr_ref[...].astype(BF16), cr_ref[...], preferred_element_type=F32)
                      + jnp.dot(hi_ref[...].astype(BF16), ci_ref[...], preferred_element_type=F32))

    tok, state, b_in, c_out, vec, ent = _slab_specs()
    return pl.pallas_call(
        body, grid=(SSM_SLABS,), in_specs=[tok, b_in, b_in, c_out, c_out, vec, vec],
        out_specs=[tok, state, state, ent, ent],
        out_shape=[_sds((SEQ, SSM_WIDTH)), _sds((SEQ, SSM_LANES)), _sds((SEQ, SSM_LANES)),
                   _sds((SCAN_CHUNKS, SSM_LANES)), _sds((SCAN_CHUNKS, SSM_LANES))],
        compiler_params=_cp(("parallel",)), name="ssm_forward")(u, b_in_r, b_in_i, c_out_r, c_out_ni, a_r, a_i)


def _ssm_backward(d_y, d_u_skip, u, h_r, h_i, e_r, e_i, b_in_r, b_in_i, c_out_r, c_out_ni, a_r, a_i):
    def body(dy_ref, skip_ref, u_ref, hr_ref, hi_ref, er_ref, ei_ref, br_ref, bi_ref, cr_ref, ci_ref, ar_ref, ai_ref,
             du_ref, dar_ref, dai_ref, dcr_ref, dci_ref, dbr_ref, dbi_ref, lr_ref, li_ref):
        dy = dy_ref[...]
        lr_ref[...] = _dot_nt(dy, cr_ref[...])
        li_ref[...] = _dot_nt(dy, ci_ref[...])
        a_re = jnp.broadcast_to(ar_ref[...], (SCAN_CHUNKS, SLAB_STATE))
        a_im = -jnp.broadcast_to(ai_ref[...], (SCAN_CHUNKS, SLAB_STATE))
        dar_ref[...], dai_ref[...] = _reverse_scan_in_place(lr_ref, li_ref, hr_ref, hi_ref, er_ref[...], ei_ref[...],
                                                            a_re, a_im)
        dcr_ref[...] = _diag_blocks(dy, hr_ref[...].astype(BF16))
        dci_ref[...] = _diag_blocks(dy, hi_ref[...].astype(BF16))
        lam_r, lam_i = lr_ref[...].astype(BF16), li_ref[...].astype(BF16)
        uu = u_ref[...]
        dbr_ref[...] = _diag_blocks(uu, lam_r)
        dbi_ref[...] = _diag_blocks(uu, lam_i)
        du = skip_ref[...] + _dot_nt(lam_r, br_ref[...]) + _dot_nt(lam_i, bi_ref[...])
        du_ref[...] = du.astype(BF16)

    tok, state, b_in, c_out, vec, ent = _slab_specs()
    db = pl.BlockSpec((SLAB_IN, SSM_STATE), lambda j: (j, 0))
    return pl.pallas_call(
        body, grid=(SSM_SLABS,), in_specs=[tok, tok, tok, state, state, ent, ent, b_in, b_in, c_out, c_out, vec, vec],
        out_specs=[tok, vec, vec, db, db, db, db],
        out_shape=[_sds((SEQ, SSM_WIDTH), BF16), _sds((1, SSM_LANES)), _sds((1, SSM_LANES))]
        + [_sds((SSM_WIDTH, SSM_STATE))] * 4,
        scratch_shapes=[pltpu.VMEM((SEQ, SLAB_STATE), F32)] * 2,
        compiler_params=_cp(("parallel",)), name="ssm_backward")(
            d_y, d_u_skip, u, h_r, h_i, e_r, e_i, b_in_r, b_in_i, c_out_r, c_out_ni, a_r, a_i)


FF_ROWS = 1024
FF_SHARD = D_FF // N_CHIPS


def _dot_nt(a, b):
    return lax.dot_general(a, b, (((1,), (1,)), ((), ())), preferred_element_type=F32)


def _ffn_up(h, w_gate_t, w_up_t):
    def body(h_ref, wg_ref, wu_ref, a_ref, b_ref, act_ref):
        hb = h_ref[...].astype(BF16)
        a = _dot_nt(hb, wg_ref[...])
        b = _dot_nt(hb, wu_ref[...])
        a_ref[...] = a
        b_ref[...] = b
        act_ref[...] = (a * jax.nn.sigmoid(a) * b).astype(BF16)

    w_spec = pl.BlockSpec((None, FF_SHARD, D_MODEL), lambda i, k: (k, 0, 0))
    o_spec = pl.BlockSpec((None, FF_ROWS, FF_SHARD), lambda i, k: (k, i, 0))
    shape = (N_CHIPS, SEQ, FF_SHARD)
    return pl.pallas_call(
        body, grid=(SEQ // FF_ROWS, N_CHIPS),
        in_specs=[pl.BlockSpec((FF_ROWS, D_MODEL), lambda i, k: (i, 0)), w_spec, w_spec],
        out_specs=[o_spec, o_spec, o_spec], out_shape=[_sds(shape), _sds(shape), _sds(shape, BF16)],
        compiler_params=_cp(("parallel", "parallel")), name="ffn_up")(h, w_gate_t, w_up_t)


def _ffn_down_ln2_loss(act, w_down, h, tgt, ln_g, ln_b):
    def body(act_ref, w_ref, h_ref, tgt_ref, g_ref, b_ref, dz_ref, loss_ref, dg_ref, db_ref, acc):
        i, k = pl.program_id(0), pl.program_id(1)
        part = jnp.dot(act_ref[...], w_ref[...], preferred_element_type=F32)

        @pl.when(k == 0)
        def _():
            acc[...] = part

        @pl.when(k > 0)
        def _():
            acc[...] += part

        @pl.when(k == N_CHIPS - 1)
        def _():
            g = g_ref[...]
            xhat, rstd = _ln_stats(DN_ALPHA * h_ref[...] + acc[...])
            err = xhat * g + b_ref[...] - tgt_ref[...]
            d_out = err * (1.0 / D_MODEL)
            dz_ref[...] = _ln_bwd(d_out, xhat, rstd, g)
            loss_rows = jnp.sum(err * err, axis=-1, keepdims=True) * (0.5 / D_MODEL)
            sums = (jnp.broadcast_to(jnp.sum(loss_rows, axis=0, keepdims=True), loss_ref.shape),
                    _colsum(d_out * xhat), _colsum(d_out))
            for ref, val in zip((loss_ref, dg_ref, db_ref), sums):
                @pl.when(i == 0)
                def _(ref=ref, val=val):
                    ref[...] = val

                @pl.when(i > 0)
                def _(ref=ref, val=val):
                    ref[...] += val

    row = pl.BlockSpec((FF_ROWS, D_MODEL), lambda i, k: (i, 0))
    vec = pl.BlockSpec((1, D_MODEL), lambda i, k: (0, 0))
    return pl.pallas_call(
        body, grid=(SEQ // FF_ROWS, N_CHIPS),
        in_specs=[pl.BlockSpec((None, FF_ROWS, FF_SHARD), lambda i, k: (k, i, 0)),
                  pl.BlockSpec((None, FF_SHARD, D_MODEL), lambda i, k: (k, 0, 0)), row, row, vec, vec],
        out_specs=[row, pl.BlockSpec((1, BLOCK), lambda i, k: (0, 0)), vec, vec],
        out_shape=[_sds((SEQ, D_MODEL)), _sds((1, BLOCK)), _sds((1, D_MODEL)), _sds((1, D_MODEL))],
        scratch_shapes=[pltpu.VMEM((FF_ROWS, D_MODEL), F32)],
        compiler_params=_cp(("arbitrary", "arbitrary")), name="ffn_down_ln2_loss")(act, w_down, h, tgt, ln_g, ln_b)


def _ffn_down_bwd(dz, w_down, a, b):
    def body(dz_ref, wd_ref, a_ref, b_ref, da_ref, db_ref):
        d_act = _dot_nt(dz_ref[...].astype(BF16), wd_ref[...])
        av = a_ref[...]
        sg = jax.nn.sigmoid(av)
        da_ref[...] = (d_act * b_ref[...] * sg * (1.0 + av * (1.0 - sg))).astype(BF16)
        db_ref[...] = (d_act * av * sg).astype(BF16)

    t_spec = pl.BlockSpec((None, FF_ROWS, FF_SHARD), lambda i, k: (k, i, 0))
    shape = (N_CHIPS, SEQ, FF_SHARD)
    return pl.pallas_call(
        body, grid=(SEQ // FF_ROWS, N_CHIPS),
        in_specs=[pl.BlockSpec((FF_ROWS, D_MODEL), lambda i, k: (i, 0)),
                  pl.BlockSpec((None, FF_SHARD, D_MODEL), lambda i, k: (k, 0, 0)), t_spec, t_spec],
        out_specs=[t_spec, t_spec], out_shape=[_sds(shape, BF16), _sds(shape, BF16)],
        compiler_params=_cp(("parallel", "parallel")), name="ffn_down_bwd")(dz, w_down, a, b)


def _ffn_dh(d_a, d_b, w_gate_t, w_up_t):
    def body(da_ref, db_ref, wg_ref, wu_ref, o_ref, acc):
        k = pl.program_id(1)
        part = (jnp.dot(da_ref[...], wg_ref[...], preferred_element_type=F32)
                + jnp.dot(db_ref[...], wu_ref[...], preferred_element_type=F32))

        @pl.when(k == 0)
        def _():
            acc[...] = part

        @pl.when(k > 0)
        def _():
            acc[...] += part

        @pl.when(k == N_CHIPS - 1)
        def _():
            o_ref[...] = acc[...]

    t_spec = pl.BlockSpec((None, FF_ROWS, FF_SHARD), lambda i, k: (k, i, 0))
    w_spec = pl.BlockSpec((None, FF_SHARD, D_MODEL), lambda i, k: (k, 0, 0))
    return pl.pallas_call(
        body, grid=(SEQ // FF_ROWS, N_CHIPS), in_specs=[t_spec, t_spec, w_spec, w_spec],
        out_specs=pl.BlockSpec((FF_ROWS, D_MODEL), lambda i, k: (i, 0)), out_shape=_sds((SEQ, D_MODEL)),
        scratch_shapes=[pltpu.VMEM((FF_ROWS, D_MODEL), F32)],
        compiler_params=_cp(("parallel", "arbitrary")), name="ffn_dh")(d_a, d_b, w_gate_t, w_up_t)


def _local_step(x, tgt, wts, small, ffn_grads, mixer_grads, small_grads):
    s = SEQ
    cos_f, sin_s = [_to_phase_rows(t) for t in _rope_tables()]
    x = _reorder_rows(x, to_phase=True, name="phase_rows_x")
    tgt = _reorder_rows(tgt, to_phase=True, name="phase_rows_target")

    chip, w_own, w_others = wts["w_in_parts"]
    proj_own = _project_in_own(chip, x, w_own, cos_f, sin_s)
    proj = _project_in(chip, x, w_others, cos_f, sin_s, proj_own)

    attn, lse = _attention_fwd(proj)

    (abar_r, abar_i, bbar_r, bbar_i), ssm_vjp = jax.vjp(
        _ssm_discretise, small["ssm_a_re"], small["ssm_a_im"], small["ssm_log_dt"], small["ssm_b_re"], small["ssm_b_im"])
    b_in_r, b_in_i = [_slab_block_diag(b.transpose(0, 2, 1)).astype(BF16) for b in (bbar_r, bbar_i)]
    c_out_r = _slab_block_diag(small["ssm_c_re"].transpose(0, 2, 1)).astype(BF16)
    c_out_ni = _slab_block_diag(-small["ssm_c_im"].transpose(0, 2, 1)).astype(BF16)
    a_r, a_i = abar_r.reshape(1, SSM_LANES), abar_i.reshape(1, SSM_LANES)
    d_skip = small["ssm_d"].reshape(1, SSM_WIDTH)

    u_f = _to_scan_rows(proj[:, 3 * QKV_WIDTH:3 * QKV_WIDTH + SSM_WIDTH])
    u_p = u_f.astype(BF16)
    y_c, h_r, h_i, e_r, e_i = _ssm_forward(u_p, b_in_r, b_in_i, c_out_r, c_out_ni, a_r, a_i)

    def branch(t, wg):
        return jnp.concatenate([jnp.dot(t, wg[k], preferred_element_type=F32) for k in range(N_CHIPS)], axis=1)

    def branch_t(t, wg):
        ns = wg.shape[2]
        return sum(_dot_nt(t[:, k * ns:(k + 1) * ns], wg[k]) for k in range(N_CHIPS))

    def gelu_glu(yc, u, dsk, wg):
        y = yc + dsk * u
        gel = (0.5 * y * (1.0 + jnp.tanh(GELU_C * (y + GELU_K * y * y * y)))).astype(BF16)
        glu = branch(gel, wg)
        return y, gel, glu, glu[:, :SSM_WIDTH] * jax.nn.sigmoid(glu[:, SSM_WIDTH:])

    y_s5, gel, glu, y_glu = _rowwise(
        gelu_glu, [y_c, u_f], [d_skip, wts["w_glu"]],
        [_sds((s, SSM_WIDTH)), _sds((s, SSM_WIDTH), BF16), _sds((s, 2 * SSM_WIDTH)), _sds((s, SSM_WIDTH), BF16)],
        tm=512, name="ssm_gelu_glu")
    y_glu = _from_scan_rows(y_glu)

    gl0 = (proj, D_MODEL, (3 * QKV_WIDTH + SSM_WIDTH) // D_MODEL)
    gl1 = (proj, D_MODEL, (3 * QKV_WIDTH + SSM_WIDTH) // D_MODEL + 1)
    b_gate = small["b_gate"]
    w_out = wts["w_out"].reshape(D_MODEL, D_MODEL)

    def mix_ln1(l0, l1, at, yg, xv, bg, wa, ws, wo, g, b):
        ya = branch(at.astype(BF16), wa)
        ys = branch(yg, ws)
        mixed = (jax.nn.sigmoid(l0 + bg[0:1]) * ya + jax.nn.sigmoid(l1 + bg[1:2]) * ys).astype(BF16)
        z = DN_ALPHA * xv + jnp.dot(mixed, wo, preferred_element_type=F32)
        xhat, _ = _ln_stats(z)
        return ya, ys, mixed, z, xhat * g + b

    y_attn, y_ssm, mixed, z1, h = _rowwise(
        mix_ln1, [gl0, gl1, attn, y_glu, x],
        [b_gate, wts["w_attn_br"], wts["w_ssm_br"], w_out, small["ln1_g"], small["ln1_b"]],
        [_sds((s, D_MODEL)), _sds((s, D_MODEL)), _sds((s, D_MODEL), BF16), _sds((s, D_MODEL)), _sds((s, D_MODEL))],
        tm=256, name="mix_ln1")

    nf = D_FF // N_CHIPS
    w_gate_t, w_up_t, w_down = wts["w_ff_gate"], wts["w_ff_up"], wts["w_ff_down"]
    ff_a, ff_b, act = _ffn_up(h, w_gate_t, w_up_t)
    dz2, loss_v, d_ln2_g, d_ln2_b = _ffn_down_ln2_loss(act, w_down, h, tgt, small["ln2_g"], small["ln2_b"])

    d_a, d_b = _ffn_down_bwd(dz2, w_down, ff_a, ff_b)

    def grad_rows(lhs, rhs, name):
        return _matmul(lhs, rhs, grid=(N_CHIPS,), a_spec=pl.BlockSpec((None, s, nf), lambda k: (k, 0, 0)),
                       b_spec=pl.BlockSpec((s, D_MODEL), lambda k: (0, 0)),
                       o_spec=pl.BlockSpec((None, nf, D_MODEL), lambda k: (k, 0, 0)),
                       out_shape=_sds((N_CHIPS, nf, D_MODEL), BF16), dims=(0, 0), name=name)

    g_w_ff_down = grad_rows(act, dz2, "g_w_ff_down")
    g_w_ff_gate = grad_rows(d_a, h, "g_w_ff_gate")
    g_w_ff_up = grad_rows(d_b, h, "g_w_ff_up")
    dh_ff = _ffn_dh(d_a, d_b, w_gate_t, w_up_t)

    def ln1_gate_bwd(dz, dff, z, l0, l1, ya, ys, g, bg, wo, wa, ws):
        xhat, rstd = _ln_stats(z)
        dh = DN_ALPHA * dz + dff
        dz_in = _ln_bwd(dh, xhat, rstd, g)
        dm = _dot_nt(dz_in.astype(BF16), wo)
        g0 = jax.nn.sigmoid(l0 + bg[0:1])
        g1 = jax.nn.sigmoid(l1 + bg[1:2])
        dl0 = dm * ya * g0 * (1.0 - g0)
        dl1 = dm * ys * g1 * (1.0 - g1)
        dya, dys = (dm * g0).astype(BF16), (dm * g1).astype(BF16)
        return (dz_in, dya, dys, jnp.concatenate([dl0, dl1], axis=1), branch_t(dya, wa), branch_t(dys, ws),
                _colsum(dh * xhat), _colsum(dh), _colsum(dl0), _colsum(dl1))

    dz1, d_y_attn, d_y_ssm, d_gl, d_attn, d_y_glu, d_ln1_g, d_ln1_b, d_bg0, d_bg1 = _rowwise(
        ln1_gate_bwd, [dz2, dh_ff, z1, gl0, gl1, y_attn, y_ssm],
        [small["ln1_g"], b_gate, w_out, wts["w_attn_br"], wts["w_ssm_br"]],
        [_sds((s, D_MODEL)), _sds((s, D_MODEL), BF16), _sds((s, D_MODEL), BF16), _sds((s, 2 * D_MODEL), BF16),
         _sds((s, ATTN_WIDTH)), _sds((s, SSM_WIDTH))],
        [_sds((1, D_MODEL))] * 4, tm=256, name="ln1_gate_bwd", after=(g_w_ff_down, g_w_ff_gate, g_w_ff_up))
    ffn_sent = ffn_grads({"w_ff_down": g_w_ff_down, "w_ff_gate": g_w_ff_gate, "w_ff_up": g_w_ff_up}, dz1)
    g_w_out = _mm_rows_tn(mixed, dz1, name="g_w_out")

    g_w_ssm_br = _mm_cols_tn(y_glu, d_y_ssm, ns=D_MODEL // N_CHIPS, name="g_w_ssm_br")
    d_y_glu = _to_scan_rows(d_y_glu)

    def glu_gelu_bwd(dyg, gl, y, u, dsk, wg):
        ga, gb = gl[:, :SSM_WIDTH], gl[:, SSM_WIDTH:]
        sg = jax.nn.sigmoid(gb)
        d_gl = jnp.concatenate([dyg * sg, dyg * ga * sg * (1.0 - sg)], axis=1).astype(BF16)
        dg = branch_t(d_gl, wg)
        th = jnp.tanh(GELU_C * (y + GELU_K * y * y * y))
        dy = dg * (0.5 * (1.0 + th) + 0.5 * y * (1.0 - th * th) * GELU_C * (1.0 + 3.0 * GELU_K * y * y))
        return d_gl, dy, dy * dsk, _colsum(dy * u)

    d_glu, d_y, d_u_skip, d_ssm_d = _rowwise(
        glu_gelu_bwd, [d_y_glu, glu, y_s5, u_f], [d_skip, wts["w_glu"]],
        [_sds((s, 2 * SSM_WIDTH), BF16), _sds((s, SSM_WIDTH), BF16), _sds((s, SSM_WIDTH))], [_sds((1, SSM_WIDTH))],
        tm=512, name="glu_gelu_bwd", after=tuple(ffn_sent))
    g_w_glu = _mm_cols_tn(gel, d_glu, ns=2 * SSM_WIDTH // N_CHIPS, name="g_w_glu")
    d_u, d_abar_r, d_abar_i, d_c_r, d_c_ni, d_bin_r, d_bin_i = _ssm_backward(
        d_y, d_u_skip, u_p, h_r, h_i, e_r, e_i, b_in_r, b_in_i, c_out_r, c_out_ni, a_r, a_i)
    d_u = _from_scan_rows(d_u)
    d_bbar_r = d_bin_r.reshape(SSM_GROUPS, SSM_GROUP, SSM_STATE).transpose(0, 2, 1)
    d_bbar_i = d_bin_i.reshape(SSM_GROUPS, SSM_GROUP, SSM_STATE).transpose(0, 2, 1)
    d_a_re, d_a_im, d_log_dt, d_b_re, d_b_im = ssm_vjp(
        (d_abar_r.reshape(SSM_GROUPS, SSM_STATE), d_abar_i.reshape(SSM_GROUPS, SSM_STATE), d_bbar_r, d_bbar_i))
    d_c_re = d_c_r.reshape(SSM_GROUPS, SSM_GROUP, SSM_STATE)
    d_c_im = -d_c_ni.reshape(SSM_GROUPS, SSM_GROUP, SSM_STATE)

    g_w_attn_br = _mm_cols_tn(attn, d_y_attn, ns=D_MODEL // N_CHIPS, name="g_w_attn_br")
    mixer_grads({"w_out": g_w_out, "w_ssm_br": g_w_ssm_br, "w_glu": g_w_glu, "w_attn_br": g_w_attn_br}, d_abar_r)
    small_g = {"b_gate": jnp.concatenate([d_bg0, d_bg1], axis=0), "ssm_a_re": d_a_re, "ssm_a_im": d_a_im,
               "ssm_log_dt": d_log_dt, "ssm_b_re": d_b_re, "ssm_b_im": d_b_im, "ssm_c_re": d_c_re, "ssm_c_im": d_c_im,
               "ssm_d": d_ssm_d.reshape(SSM_WIDTH), "ln1_g": d_ln1_g, "ln1_b": d_ln1_b, "ln2_g": d_ln2_g,
               "ln2_b": d_ln2_b}
    shared = small_grads(small_g, loss_v[0, 0])
    d_proj = _attention_bwd(proj, cos_f, sin_s, d_attn, attn, lse, d_u, d_gl)

    g_w_in = _mm_cols_tn(x, d_proj, ns=IN_WIDTH // N_CHIPS, name="g_w_in", after=tuple(shared))

    def grad_x_after(after):
        dx_proj = _mm_cols_nt(d_proj, wts["w_in"], tm=1024, name="dx_proj", after=after)
        return _reorder_rows(dz1, dx_proj, to_phase=False, name="grad_x", scale=DN_ALPHA)

    return grad_x_after, g_w_in, d_proj


GATHER_ID, SWAP_ID, SCATTER_ID, JOIN_ID, EXCHANGE_ID = 1, 2, 3, 4, 5


def _place():
    return lax.axis_index("x"), lax.axis_index("y"), lax.axis_index("c")


def _other_chips(x, y):
    return [(1 - x, y), (x, 1 - y), (1 - x, 1 - y)]


def _handshake(peers):
    barrier = pltpu.get_barrier_semaphore()
    for peer in peers:
        pl.semaphore_signal(barrier, inc=1, device_id=peer, device_id_type=MESH)
    pl.semaphore_wait(barrier, len(peers))


def _sequencer(body, arrays, out_type, sems, collective_id, name):
    return pl.kernel(body, name=name, out_type=out_type,
                     mesh=plsc.ScalarSubcoreMesh(axis_name="sequencer", num_cores=1), scratch_types=sems,
                     compiler_params=pltpu.CompilerParams(collective_id=collective_id))(*arrays)


def _gather_weights(shards, *, name, own_slot=True):
    nw = len(shards)

    def body(*refs):
        ins, outs = refs[:nw], refs[nw:2 * nw]
        send_sems, recv_sems, pass_send, pass_recv, local_sems = refs[2 * nw:]
        x, y, c = _place()
        chip = 2 * x + y
        chips = _other_chips(x, y)
        _handshake([(x, y, 1 - c)] + [(cx, cy, c) for cx, cy in chips])
        started, local = [], []
        for w in range(nw):
            hw = shards[w].shape[0] // 2
            mine = pl.ds(c * hw, hw)
            if own_slot:
                own = pltpu.make_async_copy(ins[w], outs[w].at[chip], local_sems.at[w])
                own.start()
                local.append(own)
            for j, (cx, cy) in enumerate(chips):
                cp = pltpu.make_async_remote_copy(
                    src_ref=ins[w].at[mine], dst_ref=outs[w].at[chip, mine], send_sem=send_sems.at[w, j],
                    recv_sem=recv_sems.at[w, j], device_id=(cx, cy, c), device_id_type=MESH)
                cp.start()
                started.append(cp)
        passed = []
        for w in range(nw):
            hw = shards[w].shape[0] // 2
            mine = pl.ds(c * hw, hw)
            for j, (cx, cy) in enumerate(chips):
                landed = outs[w].at[2 * cx + cy, mine]
                pltpu.make_async_remote_copy(
                    src_ref=ins[w].at[mine], dst_ref=landed, send_sem=send_sems.at[w, j],
                    recv_sem=recv_sems.at[w, j], device_id=(cx, cy, c), device_id_type=MESH).wait_recv()
                cp = pltpu.make_async_remote_copy(
                    src_ref=landed, dst_ref=landed, send_sem=pass_send.at[w, j], recv_sem=pass_recv.at[w, j],
                    device_id=(x, y, 1 - c), device_id_type=MESH)
                cp.start()
                passed.append(cp)
        for w in range(nw):
            hw = shards[w].shape[0] // 2
            theirs = pl.ds((1 - c) * hw, hw)
            for j, (cx, cy) in enumerate(chips):
                landed = outs[w].at[2 * cx + cy, theirs]
                pltpu.make_async_remote_copy(
                    src_ref=landed, dst_ref=landed, send_sem=pass_send.at[w, j], recv_sem=pass_recv.at[w, j],
                    device_id=(x, y, 1 - c), device_id_type=MESH).wait_recv()
        for cp in local:
            cp.wait()
        for cp in started + passed:
            cp.wait_send()

    sem = pltpu.SemaphoreType.DMA
    return _sequencer(body, shards, [_sds((N_CHIPS,) + a.shape, a.dtype) for a in shards],
                      [sem((nw, 3)), sem((nw, 3)), sem((nw, 3)), sem((nw, 3)), sem((nw,))], GATHER_ID, name)


def _swap_other_halves(grads, *, name):
    nw = len(grads)

    def body(*refs):
        ins, outs = refs[:nw], refs[nw:2 * nw]
        send_sems, recv_sems = refs[2 * nw:]
        x, y, c = _place()
        _handshake([(x, y, 1 - c)])
        cps = []
        for w in range(nw):
            hw = grads[w].shape[1] // 2
            cp = pltpu.make_async_remote_copy(
                src_ref=ins[w].at[:, pl.ds((1 - c) * hw, hw)], dst_ref=outs[w], send_sem=send_sems.at[w],
                recv_sem=recv_sems.at[w], device_id=(x, y, 1 - c), device_id_type=MESH)
            cp.start()
            cps.append(cp)
        for cp in cps:
            cp.wait()

    sem = pltpu.SemaphoreType.DMA
    return _sequencer(body, grads, [_sds((N_CHIPS, g.shape[1] // 2, g.shape[2]), g.dtype) for g in grads],
                      [sem((nw,)), sem((nw,))], SWAP_ID, name)


def _add_my_halves(core, grads, others, *, name, after=()):
    nw = len(grads)
    halves = [g.shape[1] // 2 for g in grads]

    def body(core_ref, *refs):
        outs = refs[2 * nw + len(after):]
        for g_ref, o_ref, out_ref in zip(refs[:nw], refs[nw:2 * nw], outs):
            out_ref[...] = (g_ref[...].astype(F32) + o_ref[...].astype(F32)).astype(out_ref.dtype)

    in_specs = [pl.BlockSpec((None, None, hw, g.shape[2]), lambda s, core_ref: (s, core_ref[0], 0, 0))
                for g, hw in zip(grads, halves)]
    in_specs += [pl.BlockSpec((None, hw, g.shape[2]), lambda s, core_ref: (s, 0, 0)) for g, hw in zip(grads, halves)]
    return pl.pallas_call(
        body,
        grid_spec=pltpu.PrefetchScalarGridSpec(
            num_scalar_prefetch=1, grid=(N_CHIPS,), in_specs=in_specs + [HBM_OPERAND] * len(after),
            out_specs=[pl.BlockSpec((None, hw, g.shape[2]), lambda s, core_ref: (s, 0, 0))
                       for g, hw in zip(grads, halves)]),
        out_shape=[_sds((N_CHIPS, hw, g.shape[2]), BF16) for g, hw in zip(grads, halves)],
        compiler_params=_cp(("parallel",)), name=name)(
            core, *[g.reshape(N_CHIPS, 2, hw, g.shape[2]) for g, hw in zip(grads, halves)], *others, *after)


def _scatter_partials(parts, *, name):
    nw = len(parts)

    def body(*refs):
        ins, outs = refs[:nw], refs[nw:2 * nw]
        send_sems, recv_sems = refs[2 * nw:]
        x, y, c = _place()
        _handshake([(cx, cy, c) for cx, cy in _other_chips(x, y)])
        cps = []
        for w in range(nw):
            for j, (cx, cy) in enumerate(_other_chips(x, y)):
                cp = pltpu.make_async_remote_copy(
                    src_ref=ins[w].at[2 * cx + cy], dst_ref=outs[w].at[j], send_sem=send_sems.at[w, j],
                    recv_sem=recv_sems.at[w, j], device_id=(cx, cy, c), device_id_type=MESH)
                cp.start()
                cps.append(cp)
        for cp in cps:
            cp.wait()

    sem = pltpu.SemaphoreType.DMA
    return _sequencer(body, parts, [_sds((3,) + p.shape[1:], p.dtype) for p in parts],
                      [sem((nw, 3)), sem((nw, 3))], SCATTER_ID, name)


SUM_STEPS = 2


def _sum_partials(chip, parts, recvd, *, name, after=()):
    nw = len(parts)
    rows = [p.shape[1] // SUM_STEPS for p in parts]

    def body(chip_ref, *refs):
        outs = refs[2 * nw + len(after):]
        for p_ref, r_ref, out_ref in zip(refs[:nw], refs[nw:2 * nw], outs):
            acc = p_ref[...].astype(F32)
            for j in range(3):
                acc = acc + r_ref[j].astype(F32)
            out_ref[...] = acc

    in_specs = [pl.BlockSpec((None, th, p.shape[2]), lambda i, chip_ref: (chip_ref[0], i, 0))
                for p, th in zip(parts, rows)]
    in_specs += [pl.BlockSpec((3, th, p.shape[2]), lambda i, chip_ref: (0, i, 0)) for p, th in zip(parts, rows)]
    return pl.pallas_call(
        body,
        grid_spec=pltpu.PrefetchScalarGridSpec(
            num_scalar_prefetch=1, grid=(SUM_STEPS,), in_specs=in_specs + [HBM_OPERAND] * len(after),
            out_specs=[pl.BlockSpec((th, p.shape[2]), lambda i, chip_ref: (i, 0)) for p, th in zip(parts, rows)]),
        out_shape=[_sds(p.shape[1:]) for p in parts], compiler_params=_cp(("parallel",)), name=name)(
            chip, *parts, *recvd, *after)


def _swap_reduced_halves(halves, *, name):
    nw = len(halves)

    def body(*refs):
        ins, outs = refs[:nw], refs[nw:2 * nw]
        send_sems, recv_sems = refs[2 * nw:]
        x, y, c = _place()
        _handshake([(x, y, 1 - c)])
        cps = []
        for w in range(nw):
            cp = pltpu.make_async_remote_copy(
                src_ref=ins[w], dst_ref=outs[w], send_sem=send_sems.at[w], recv_sem=recv_sems.at[w],
                device_id=(x, y, 1 - c), device_id_type=MESH)
            cp.start()
            cps.append(cp)
        for cp in cps:
            cp.wait()

    sem = pltpu.SemaphoreType.DMA
    return _sequencer(body, halves, [_sds(h.shape, h.dtype) for h in halves], [sem((nw,)), sem((nw,))], JOIN_ID, name)


def _exchange_each(arrs, *, name):
    n = len(arrs)

    def body(*refs):
        ins, outs = refs[:n], refs[n:2 * n]
        send_sems, recv_sems, local_sems = refs[2 * n:]
        x, y, c = _place()
        me = 4 * x + 2 * y + c
        peers = []
        for mask in range(1, N_DEV):
            peers.append((1 - x if mask & 4 else x, 1 - y if mask & 2 else y, 1 - c if mask & 1 else c))
        _handshake(peers)
        cps = []
        for a in range(n):
            own = pltpu.make_async_copy(ins[a], outs[a].at[me], local_sems.at[a])
            own.start()
            cps.append(own)
        sent = []
        for a in range(n):
            for k, peer in enumerate(peers):
                cp = pltpu.make_async_remote_copy(
                    src_ref=ins[a], dst_ref=outs[a].at[me], send_sem=send_sems.at[a, k],
                    recv_sem=recv_sems.at[a, k], device_id=peer, device_id_type=MESH)
                cp.start()
                sent.append(cp)
        for a in range(n):
            for k, (px, py, pc) in enumerate(peers):
                pltpu.make_async_remote_copy(
                    src_ref=ins[a], dst_ref=outs[a].at[4 * px + 2 * py + pc], send_sem=send_sems.at[a, k],
                    recv_sem=recv_sems.at[a, k], device_id=(px, py, pc), device_id_type=MESH).wait_recv()
        for cp in sent:
            cp.wait_send()
        for cp in cps:
            cp.wait()

    sem = pltpu.SemaphoreType.DMA
    return _sequencer(body, arrs, [_sds((N_DEV,) + a.shape) for a in arrs],
                      [sem((n, N_DEV - 1)), sem((n, N_DEV - 1)), sem((n,))], EXCHANGE_ID, name)


def _sum_slots(slots, *, name, after=()):
    n = len(slots)

    def body(*refs):
        for s_ref, out_ref in zip(refs[:n], refs[n + len(after):]):
            acc = s_ref[0]
            for d in range(1, N_DEV):
                acc = acc + s_ref[d]
            out_ref[...] = acc

    vmem = pl.BlockSpec(memory_space=pltpu.VMEM)
    return pl.pallas_call(
        body, in_specs=[vmem] * n + [HBM_OPERAND] * len(after), out_specs=[vmem] * n,
        out_shape=[_sds(s.shape[1:]) for s in slots],
        compiler_params=pltpu.CompilerParams(vmem_limit_bytes=VMEM_LIMIT_BYTES), name=name)(*slots, *after)


def _reduce_scatter_start(grads, core, *, tag, add_after=()):
    others = _swap_other_halves(grads, name="swap_other_halves_" + tag)
    parts = _add_my_halves(core, grads, others, name="add_my_halves_" + tag, after=add_after)
    return parts, _scatter_partials(parts, name="scatter_partials_" + tag)


def _reduce_scatter_finish(parts, recvd, chip, *, tag, sum_after=()):
    mine = _sum_partials(chip, parts, recvd, name="sum_partials_" + tag, after=sum_after)
    return mine, _swap_reduced_halves(mine, name="swap_reduced_halves_" + tag)


ADAM_BLOCK_ELEMS = 256 * 1024


def _adam_rows(rows, cols):
    tm = rows
    while tm * cols > ADAM_BLOCK_ELEMS and tm % 16 == 0:
        tm //= 2
    return tm


def _adam_step(wv, gv, mv, vv):
    m2 = ADAM_B1 * mv + (1.0 - ADAM_B1) * gv
    v2 = ADAM_B2 * vv + (1.0 - ADAM_B2) * (gv * gv)
    m_hat = m2 / (1.0 - ADAM_B1 ** ADAM_STEP)
    v_hat = v2 / (1.0 - ADAM_B2 ** ADAM_STEP)
    return -ADAM_LR * (m_hat / (jnp.sqrt(v_hat) + ADAM_EPS) + ADAM_WD * wv), m2, v2


def _adamw_each(ws, gs, ms, vs, *, name, after=()):
    n = len(ws)
    whole = pl.BlockSpec(memory_space=pltpu.VMEM)

    def body(*refs):
        ins, outs = refs[:4 * n], refs[4 * n + len(after):]
        for i in range(n):
            res = _adam_step(*(ins[k * n + i][...] for k in range(4)))
            for k in range(3):
                outs[k * n + i][...] = res[k]

    out = pl.pallas_call(body, in_specs=[whole] * (4 * n) + [HBM_OPERAND] * len(after),
                         out_shape=[_sds(w.shape) for w in ws] * 3, name=name)(*ws, *gs, *ms, *vs, *after)
    return out[:n], out[n:2 * n], out[2 * n:]


def _adamw_halves(core, w, g_mine, g_theirs, m, v, *, name, after=()):
    rows, cols = w.shape
    hw = rows // 2
    tm = _adam_rows(hw, cols)
    per_half = hw // tm

    def body(core_ref, w_ref, gm_ref, gt_ref, m_ref, v_ref, *rest):
        g_out, d_out, m_out, v_out = rest[len(after):]
        mine = (pl.program_id(0) // per_half) == core_ref[0]
        g = jnp.where(mine, gm_ref[...], gt_ref[...])
        d, m2, v2 = _adam_step(w_ref[...], g, m_ref[...], v_ref[...])
        g_out[...] = g
        d_out[...] = d
        m_out[...] = m2
        v_out[...] = v2

    full = pl.BlockSpec((tm, cols), lambda i, core_ref: (i, 0))

    def half(wanted):
        def index(i, core_ref):
            in_use = ((i // per_half) == core_ref[0]) == wanted
            return (jnp.where(in_use, i % per_half, 0), 0)
        return pl.BlockSpec((tm, cols), index)

    return pl.pallas_call(
        body,
        grid_spec=pltpu.PrefetchScalarGridSpec(
            num_scalar_prefetch=1, grid=(rows // tm,),
            in_specs=[full, half(True), half(False), full, full] + [HBM_OPERAND] * len(after),
            out_specs=[full, full, full, full]),
        out_shape=[_sds((rows, cols))] * 4, compiler_params=_cp(("parallel",)), name=name)(
            core, w, g_mine, g_theirs, m, v, *after)


HELD_TRANSPOSED = ("w_ff_gate", "w_ff_up")


def _as_rows(name, arr):
    return arr[0].T if name in HELD_TRANSPOSED else arr[0]


def _from_rows(name, arr2d):
    return (arr2d.T if name in HELD_TRANSPOSED else arr2d)[None]


STORED_SWAPPED = ("ssm_b_re", "ssm_b_im")


def _as_stored(name, arr):
    return jnp.swapaxes(arr, -1, -2) if name in STORED_SWAPPED else arr


def _pack_rows(arrs):
    flat = jnp.concatenate([a.reshape(-1).astype(F32) for a in arrs])
    rows = -(-flat.shape[0] // 1024) * 8
    return jnp.pad(flat, (0, rows * 128 - flat.shape[0])).reshape(rows, 128)


def _unpack_rows(vec, shapes):
    flat = vec.reshape(-1)
    out, off = [], 0
    for shp in shapes:
        size = math.prod(shp)
        out.append(flat[off:off + size].reshape(shp))
        off += size
    return out


SMALL = ("b_gate", "ssm_a_re", "ssm_a_im", "ssm_log_dt", "ssm_b_re", "ssm_b_im", "ssm_c_re", "ssm_c_im", "ssm_d",
         "ln1_g", "ln1_b", "ln2_g", "ln2_b")
GATHER_GROUPS = (("w_in", ("w_in",)), ("mixer", ("w_attn_br", "w_ssm_br", "w_glu", "w_out")),
                 ("ffn_up", ("w_ff_gate", "w_ff_up")), ("ffn_down", ("w_ff_down",)))
REDUCE_GROUPS = (("ffn", ("w_ff_down", "w_ff_gate", "w_ff_up")),
                 ("mixer", ("w_out", "w_ssm_br", "w_glu", "w_attn_br")), ("w_in", ("w_in",)))
WEIGHTS = ("w_in", "b_gate", "w_attn_br", "w_ssm_br", "w_out", "ssm_a_re", "ssm_a_im", "ssm_log_dt", "ssm_b_re",
           "ssm_b_im", "ssm_c_re", "ssm_c_im", "ssm_d", "w_glu", "ln1_g", "ln1_b", "w_ff_gate", "w_ff_up", "w_ff_down",
           "ln2_g", "ln2_b")


def kernel(x, w_in, b_gate, w_attn_br, w_ssm_br, w_out, ssm_a_re, ssm_a_im, ssm_log_dt, ssm_b_re, ssm_b_im, ssm_c_re, ssm_c_im, ssm_d, w_glu, ln1_g, ln1_b, w_ff_gate, w_ff_up, w_ff_down, ln2_g, ln2_b, loss_target, m_w_in, m_b_gate, m_w_attn_br, m_w_ssm_br, m_w_out, m_ssm_a_re, m_ssm_a_im, m_ssm_log_dt, m_ssm_b_re, m_ssm_b_im, m_ssm_c_re, m_ssm_c_im, m_ssm_d, m_w_glu, m_ln1_g, m_ln1_b, m_w_ff_gate, m_w_ff_up, m_w_ff_down, m_ln2_g, m_ln2_b, v_w_in, v_b_gate, v_w_attn_br, v_w_ssm_br, v_w_out, v_ssm_a_re, v_ssm_a_im, v_ssm_log_dt, v_ssm_b_re, v_ssm_b_im, v_ssm_c_re, v_ssm_c_im, v_ssm_d, v_w_glu, v_ln1_g, v_ln1_b, v_w_ff_gate, v_w_ff_up, v_w_ff_down, v_ln2_g, v_ln2_b):
    given = dict(locals())
    px, py, pc = _place()
    chip = 2 * px + py
    core_s = jnp.reshape(pc, (1,)).astype(jnp.int32)
    chip_s = jnp.reshape(chip, (1,)).astype(jnp.int32)

    wts = {}
    for tag, names in GATHER_GROUPS:
        shards = [_as_rows(n, given[n]).astype(BF16) for n in names]
        first = tag == GATHER_GROUPS[0][0]
        slots = _gather_weights(shards, name="gather_" + tag, own_slot=not first)
        if first:
            wts["w_in_parts"] = (chip_s, shards[0], slots[0])
            slots = [lax.dynamic_update_slice(g, s[None], (chip, 0, 0)) for g, s in zip(slots, shards)]
        wts.update(zip(names, slots))
    ncol = D_MODEL // N_CHIPS
    bg_mine = jnp.where(pc == 0, b_gate[0], jnp.zeros_like(b_gate[0]))
    bg_full = lax.dynamic_update_slice(jnp.zeros((2, D_MODEL), F32), bg_mine, (0, chip * ncol))
    bg_slots = _exchange_each([bg_full], name="exchange_gate_bias")
    bg_full = _sum_slots(bg_slots, name="sum_gate_bias")[0]
    small = {n: given[n][0] for n in SMALL if n.startswith("ssm")}
    small.update({n: given[n] for n in ("ln1_g", "ln1_b", "ln2_g", "ln2_b")})
    small["b_gate"] = bg_full

    groups = dict(REDUCE_GROUPS)
    parts, recvd, reduced, sent = {}, {}, {}, {}
    grads, delta, new_m, new_v, done = {}, {}, {}, {}, {}

    def start(tag, big_g, add_after):
        parts[tag], recvd[tag] = _reduce_scatter_start([big_g[n] for n in groups[tag]], core_s, tag=tag,
                                                       add_after=add_after)
        return parts[tag]

    def reduce_sum(tag, after):
        reduced[tag] = _reduce_scatter_finish(parts[tag], recvd[tag], chip_s, tag=tag, sum_after=after)
        return reduced[tag][0]

    def adam(tag, after):
        for n, g_mine, g_theirs in zip(groups[tag], *reduced[tag]):
            res = _adamw_halves(core_s, _as_rows(n, given[n]), g_mine, g_theirs, _as_rows(n, given["m_" + n]),
                                _as_rows(n, given["v_" + n]), name="adamw_" + n, after=after)
            done[n] = res[1]
            grads[n], delta[n], new_m[n], new_v[n] = [_from_rows(n, r) for r in res]

    def ffn_grads(big_g, norm_bwd):
        return start("ffn", big_g, (norm_bwd,))

    def mixer_grads(big_g, scan_bwd):
        return start("mixer", big_g, (scan_bwd, *reduce_sum("ffn", (scan_bwd,))))

    def small_grads(small_g, loss_mine):
        stored = [_as_stored(n, small_g[n]) for n in SMALL] + [loss_mine.reshape(1)]
        sent["shapes"] = [a.shape for a in stored]
        sent["whole"] = [i for i, a in enumerate(stored) if a.ndim == 3]
        arrs = [stored[i] for i in sent["whole"]]
        arrs.append(_pack_rows([a for i, a in enumerate(stored) if i not in sent["whole"]]))
        sent["slots"] = _exchange_each(arrs, name="exchange_small")
        return arrs

    grad_x_after, g_w_in, attention_bwd = _local_step(x[0], loss_target[0], wts, small,
                                                      ffn_grads, mixer_grads, small_grads)

    reduce_sum("mixer", (attention_bwd,))
    sums = list(_sum_slots(sent["slots"], name="sum_small", after=(g_w_in,)))
    adam("mixer", (g_w_in,))
    start("w_in", {"w_in": g_w_in}, (sums[0], *[done[n] for n in groups["mixer"]]))
    in_flight = (parts["w_in"][0],)
    grad_x = grad_x_after(in_flight)
    adam("ffn", in_flight)
    rest = _unpack_rows(sums.pop(), [s for i, s in enumerate(sent["shapes"]) if i not in sent["whole"]])
    summed = [sums.pop(0) if i in sent["whole"] else rest.pop(0) for i in range(len(sent["shapes"]))]
    loss = summed.pop()[0]
    at = SMALL.index("b_gate")
    summed[at] = lax.dynamic_slice(summed[at], (0, chip * ncol), (2, ncol))
    summed = [g.reshape(1, -1) if g.ndim == 1 else g for g in summed]
    held = [[_as_stored(n, given[prefix + n]).reshape(g.shape) for n, g in zip(SMALL, summed)]
            for prefix in ("", "m_", "v_")]
    small_out = _adamw_each(held[0], summed, held[1], held[2], name="adamw_small", after=in_flight)
    for out, arrs in zip((grads, delta, new_m, new_v), (summed, *small_out)):
        out.update((n, _as_stored(n, a).reshape(given[n].shape)) for n, a in zip(SMALL, arrs))
    reduce_sum("w_in", (*[done[n] for n in groups["ffn"]], small_out[0][0], grad_x))
    adam("w_in", ())

    return (loss, grad_x.reshape(x.shape), *[grads[n] for n in WEIGHTS], *[delta[n] for n in WEIGHTS],
            *[new_m[n] for n in WEIGHTS], *[new_v[n] for n in WEIGHTS])
```

```python
import math

import jax
import jax.numpy as jnp
from jax import lax
from jax.experimental import pallas as pl
from jax.experimental.pallas import tpu as pltpu
from jax.experimental.pallas import tpu_sc as plsc

F32 = jnp.float32
BF16 = jnp.bfloat16
MESH = pl.DeviceIdType.MESH

D_MODEL = 1024
SEQ = 2048
HEAD_DIM = 64
ATTN_HEADS = 8
DILATIONS = (1, 4, 16)
ATTN_WIDTH = ATTN_HEADS * HEAD_DIM
QKV_WIDTH = 3 * ATTN_WIDTH
BLOCK = 128
ROPE_THETA = 10000.0
NEG_INF = -1e30
SSM_GROUP = 16
SSM_GROUPS = 32
SSM_WIDTH = 512
SSM_STATE = 64
SSM_LANES = SSM_GROUPS * SSM_STATE
SCAN_CHUNKS = 8
SCAN_STEPS = SEQ // SCAN_CHUNKS
IN_WIDTH = 3 * QKV_WIDTH + SSM_WIDTH + 2 * D_MODEL
D_FF = 2816
N_CHIPS = 4
N_DEV = 8
DN_ALPHA = 2.0 ** 0.25
LN_EPS = 1e-5
ADAM_LR = 0.001
ADAM_B1 = 0.9
ADAM_B2 = 0.999
ADAM_EPS = 1e-08
ADAM_WD = 0.01
ADAM_STEP = 10
GELU_C = math.sqrt(2.0 / math.pi)
GELU_K = 0.044715

VMEM_LIMIT_BYTES = 56 * 1024 * 1024


def _sds(shape, dtype=F32):
    return jax.ShapeDtypeStruct(tuple(shape), dtype)


def _cp(semantics=None):
    return pltpu.CompilerParams(dimension_semantics=semantics, vmem_limit_bytes=VMEM_LIMIT_BYTES)


HBM_OPERAND = pl.BlockSpec(memory_space=pl.ANY)


def _matmul(a, b, *, grid, a_spec, b_spec, o_spec, out_shape, dims, k_axis=None, name, after=()):
    nk = grid[k_axis] if k_axis is not None else 1
    o_block = tuple(d for d in o_spec.block_shape if d is not None)
    n_after = len(after)

    def body(a_ref, b_ref, *rest):
        o_ref, acc = rest[n_after], rest[n_after + 1:]
        part = lax.dot_general(a_ref[...].astype(BF16), b_ref[...].astype(BF16),
                               (((dims[0],), (dims[1],)), ((), ())), preferred_element_type=F32)
        if k_axis is None:
            o_ref[...] = part.astype(o_ref.dtype)
        else:
            k = pl.program_id(k_axis)

            @pl.when(k == 0)
            def _():
                acc[0][...] = part

            @pl.when(k > 0)
            def _():
                acc[0][...] += part

            @pl.when(k == nk - 1)
            def _():
                o_ref[...] = acc[0][...].astype(o_ref.dtype)

    sem = tuple("arbitrary" if ax == k_axis else "parallel" for ax in range(len(grid)))
    return pl.pallas_call(
        body, grid=grid, in_specs=[a_spec, b_spec] + [HBM_OPERAND] * n_after, out_specs=o_spec, out_shape=out_shape,
        scratch_shapes=[pltpu.VMEM(o_block, F32)] if k_axis is not None else [],
        compiler_params=_cp(sem), name=name)(a, b, *after)


def _grad_layer_input(chip, dy, wg, w_own, dz, *, tm, name, after=()):
    k, ns = wg.shape[1], wg.shape[2]
    m = dy.shape[0]

    def body(chip_ref, dy_ref, w_ref, own_ref, dz_ref, *rest):
        o_ref, acc = rest[len(after)], rest[len(after) + 1]
        s = pl.program_id(1)
        mine = s == chip_ref[0]

        @pl.when(s == 0)
        def _():
            acc[...] = DN_ALPHA * dz_ref[...]

        def add(w_block):
            acc[...] += lax.dot_general(dy_ref[...], w_block[...], (((1,), (1,)), ((), ())),
                                        preferred_element_type=F32)

        pl.when(mine)(lambda: add(own_ref))
        pl.when(jnp.logical_not(mine))(lambda: add(w_ref))

        @pl.when(s == N_CHIPS - 1)
        def _():
            o_ref[...] = acc[...]

    def gathered(i, s, chip_ref):
        neighbour = jnp.where(s == N_CHIPS - 1, s - 1, s + 1)
        return (jnp.where(s == chip_ref[0], neighbour, s), 0, 0)

    rows =pl.BlockSpec((tm, k), lambda i, s, chip_ref: (i, 0))
    return pl.pallas_call(
        body,
        grid_spec=pltpu.PrefetchScalarGridSpec(
            num_scalar_prefetch=1, grid=(m // tm, N_CHIPS),
            in_specs=[pl.BlockSpec((tm, ns), lambda i, s, chip_ref: (i, s)), pl.BlockSpec((None, k, ns), gathered),
                      pl.BlockSpec((k, ns), lambda i, s, chip_ref: (0, 0)), rows] + [HBM_OPERAND] * len(after),
            out_specs=rows, scratch_shapes=[pltpu.VMEM((tm, k), F32)]),
        out_shape=_sds((m, k)), compiler_params=_cp(("parallel", "arbitrary")), name=name)(
            chip, dy, wg, w_own, dz, *after)


def _mm_cols_tn(a, dy, *, ns, name, after=()):
    m, k = a.shape
    return _matmul(a, dy, grid=(N_CHIPS,), a_spec=pl.BlockSpec((m, k), lambda s: (0, 0)),
                   b_spec=pl.BlockSpec((m, ns), lambda s: (0, s)),
                   o_spec=pl.BlockSpec((None, k, ns), lambda s: (s, 0, 0)),
                   out_shape=_sds((N_CHIPS, k, ns), BF16), dims=(0, 0), name=name, after=after)


def _mm_rows_tn(a, dy, *, name):
    m, k = a.shape
    rows, n = k // N_CHIPS, dy.shape[1]
    return _matmul(a, dy, grid=(N_CHIPS,), a_spec=pl.BlockSpec((m, rows), lambda s: (0, s)),
                   b_spec=pl.BlockSpec((m, n), lambda s: (0, 0)),
                   o_spec=pl.BlockSpec((None, rows, n), lambda s: (s, 0, 0)),
                   out_shape=_sds((N_CHIPS, rows, n), BF16), dims=(0, 0), name=name)


def _rowwise(fn, tiled, full, outs, accs=(), *, tm, name, after=()):
    args, in_specs = [], []
    for t in tiled:
        if isinstance(t, tuple):
            arr, w, cb = t
            in_specs.append(pl.BlockSpec((tm, w), lambda i, cb=cb: (i, cb)))
        else:
            arr = t
            in_specs.append(pl.BlockSpec((tm, arr.shape[1]), lambda i: (i, 0)))
        args.append(arr)
    rows = args[0].shape[0]
    for f in full:
        in_specs.append(pl.BlockSpec(f.shape, lambda i, nd=f.ndim: (0,) * nd))
        args.append(f)
    out_specs = [pl.BlockSpec((tm, o.shape[1]), lambda i: (i, 0)) for o in outs]
    out_specs += [pl.BlockSpec(a.shape, lambda i, nd=len(a.shape): (0,) * nd) for a in accs]
    n_in, n_out = len(args), len(outs)
    in_specs += [HBM_OPERAND] * len(after)
    first_out = n_in + len(after)

    def body(*refs):
        res = fn(*[r[...] for r in refs[:n_in]])
        res = res if isinstance(res, (tuple, list)) else (res,)
        for r, v in zip(refs[first_out:first_out + n_out], res[:n_out]):
            r[...] = v.astype(r.dtype)
        i = pl.program_id(0)
        for r, v in zip(refs[first_out + n_out:], res[n_out:]):
            @pl.when(i == 0)
            def _(r=r, v=v):
                r[...] = v

            @pl.when(i > 0)
            def _(r=r, v=v):
                r[...] += v

    res = pl.pallas_call(
        body, grid=(rows // tm,), in_specs=in_specs, out_specs=out_specs, out_shape=list(outs) + list(accs),
        compiler_params=_cp(("arbitrary",) if accs else ("parallel",)), name=name)(*args, *after)
    return res


def _colsum(v):
    return jnp.sum(v, axis=0, keepdims=True)


def _ln_stats(z):
    mu = jnp.mean(z, axis=-1, keepdims=True)
    zc = z - mu
    var = jnp.mean(zc * zc, axis=-1, keepdims=True)
    rstd = lax.rsqrt(var + LN_EPS)
    return zc * rstd, rstd


def _ln_bwd(dy, xhat, rstd, g):
    dxh = dy * g
    m1 = jnp.mean(dxh, axis=-1, keepdims=True)
    m2 = jnp.mean(dxh * xhat, axis=-1, keepdims=True)
    return rstd * (dxh - m1 - xhat * m2)


def _swap_halves(t):
    w = t.shape[-1]
    lane = lax.broadcasted_iota(jnp.int32, t.shape, t.ndim - 1)
    return jnp.where((lane % HEAD_DIM) < HEAD_DIM // 2, pltpu.roll(t, w - HEAD_DIM // 2, t.ndim - 1),
                     pltpu.roll(t, HEAD_DIM // 2, t.ndim - 1))


PHASES = max(DILATIONS)
PAIR = 2 * HEAD_DIM
UNITS = SEQ // BLOCK
UNIT_BATCH = 16
ROPE_ROWS = 256
TAIL_COLS = 256


def _to_phase_rows(t):
    return t.reshape(SEQ // PHASES, PHASES, t.shape[1]).transpose(1, 0, 2).reshape(t.shape)


def _reorder_rows(arr, *, to_phase, name):
    def body(a_ref, o_ref):
        for rho in range(PHASES):
            phase = pl.ds(rho * BLOCK, BLOCK)
            strided = pl.ds(rho, BLOCK, stride=PHASES)
            src, dst = (strided, phase) if to_phase else (phase, strided)
            o_ref[dst, :] = a_ref[src, :]

    spec = pl.BlockSpec((SEQ, BLOCK), lambda j: (0, j))
    return pl.pallas_call(body, grid=(arr.shape[1] // BLOCK,), in_specs=[spec], out_specs=spec,
                          out_shape=_sds(arr.shape), compiler_params=_cp(("parallel",)), name=name)(arr)


def _rope(t, cf, ss):
    return t * cf + _swap_halves(t) * ss


def _rope_transposed(d, cf, ss):
    return d * cf + _swap_halves(d * ss)


def _unit_pieces(u, dil):
    pieces, length = PHASES // dil, 8 * dil
    if dil == 1:
        rho, i = 0, u
    elif dil == PHASES:
        rho, i = u, 0
    else:
        rho, i = jnp.bitwise_and(u, dil - 1), jnp.right_shift(u, dil.bit_length() - 1)
    before = jnp.maximum(i - 1, 0)
    cur = [pl.multiple_of((rho + dil * k) * BLOCK + length * i, 8) for k in range(pieces)]
    prev = [pl.multiple_of((rho + dil * k) * BLOCK + length * before, 8) for k in range(pieces)]
    return i, cur, prev


def _load_tile(ref, starts, dil):
    return jnp.concatenate([ref[pl.ds(st, 8 * dil), :] for st in starts], axis=0)


def _store_tile(ref, starts, dil, val, head=None, accumulate=False):
    length = 8 * dil
    lanes = slice(None) if head is None else pl.ds(head * HEAD_DIM, HEAD_DIM)
    cols = slice(None) if head is None else slice(head * HEAD_DIM, (head + 1) * HEAD_DIM)
    for k, st in enumerate(starts):
        piece = val[k * length:(k + 1) * length, cols]
        if accumulate:
            ref[pl.ds(st, length), lanes] += piece
        else:
            ref[pl.ds(st, length), lanes] = piece


def _tile_position(idx, dil):
    pieces, length = PHASES // dil, 8 * dil
    return pieces * jnp.bitwise_and(idx, length - 1) + jnp.right_shift(idx, length.bit_length() - 1)


def _band_mask(i, dil):
    row = lax.broadcasted_iota(jnp.int32, (BLOCK, 2 * BLOCK), 0)
    col = lax.broadcasted_iota(jnp.int32, (BLOCK, 2 * BLOCK), 1)
    key_pos = _tile_position(jnp.bitwise_and(col, BLOCK - 1), dil) + jnp.where(col >= BLOCK, 0, -BLOCK)
    dist = _tile_position(row, dil) - key_pos
    return (dist >= 0) & (dist <= BLOCK) & ((col >= BLOCK) | (i > 0))


def _causal_mask():
    row = lax.broadcasted_iota(jnp.int32, (BLOCK, BLOCK), 0)
    col = lax.broadcasted_iota(jnp.int32, (BLOCK, BLOCK), 1)
    return row >= col


def _pair_views(col0):
    return [pl.BlockSpec((SEQ, PAIR), lambda hp, g=g: (0, col0 // PAIR + g * (ATTN_WIDTH // PAIR) + hp))
            for g in range(len(DILATIONS))]


def _project_shard(shard, x_ref, w_ref, cf_ref, ss_ref, o_ref):
    ns = w_ref.shape[1]
    tiles = ns // PAIR
    xb = x_ref[...].astype(BF16)
    cf, ss = cf_ref[...], ss_ref[...]

    def write(rotated, scaled):
        for t0 in range(0, tiles, 2):
            strip = jnp.dot(xb, w_ref[:, t0 * PAIR:(t0 + 2) * PAIR], preferred_element_type=F32)
            for t in (t0, t0 + 1):
                val = strip[:, (t - t0) * PAIR:(t - t0 + 1) * PAIR]
                if t < rotated:
                    val = _rope(val, cf, ss)
                    if t < scaled:
                        val = val * (1.0 / math.sqrt(HEAD_DIM))
                o_ref[:, t * PAIR:(t + 1) * PAIR] = val

    for s in range(N_CHIPS):
        rotated = min(max(2 * QKV_WIDTH - s * ns, 0), ns) // PAIR
        scaled = min(max(QKV_WIDTH - s * ns, 0), ns) // PAIR
        @pl.when(shard == s)
        def _(rotated=rotated, scaled=scaled):
            write(rotated, scaled)


def _project_in_own(chip, x, w_own, cos_f, sin_s):
    ns = w_own.shape[1]

    def body(chip_ref, x_ref, w_ref, cf_ref, ss_ref, o_ref):
        _project_shard(chip_ref[0], x_ref, w_ref, cf_ref, ss_ref, o_ref)

    table = pl.BlockSpec((FF_ROWS, PAIR), lambda i, chip_ref: (i, 0))
    return pl.pallas_call(
        body,
        grid_spec=pltpu.PrefetchScalarGridSpec(
            num_scalar_prefetch=1, grid=(SEQ // FF_ROWS,),
            in_specs=[pl.BlockSpec((FF_ROWS, D_MODEL), lambda i, chip_ref: (i, 0)),
                      pl.BlockSpec((D_MODEL, ns), lambda i, chip_ref: (0, 0)), table, table],
            out_specs=pl.BlockSpec((FF_ROWS, ns), lambda i, chip_ref: (i, chip_ref[0]))),
        out_shape=_sds((SEQ, N_CHIPS * ns)), compiler_params=_cp(("parallel",)), name="project_in_own")(
            chip, x, w_own, cos_f, sin_s)


def _project_in(chip, x, wg, cos_f, sin_s, started):
    ns = wg.shape[2]

    def other(j, chip_ref):
        return (chip_ref[0] + 1 + j) % N_CHIPS

    def body(chip_ref, x_ref, w_ref, cf_ref, ss_ref, started_ref, o_ref):
        _project_shard(other(pl.program_id(1), chip_ref), x_ref, w_ref, cf_ref, ss_ref, o_ref)

    table = pl.BlockSpec((FF_ROWS, PAIR), lambda i, j, chip_ref: (i, 0))
    return pl.pallas_call(
        body,
        grid_spec=pltpu.PrefetchScalarGridSpec(
            num_scalar_prefetch=1, grid=(SEQ // FF_ROWS, N_CHIPS - 1),
            in_specs=[pl.BlockSpec((FF_ROWS, D_MODEL), lambda i, j, chip_ref: (i, 0)),
                      pl.BlockSpec((None, D_MODEL, ns), lambda i, j, chip_ref: (other(j, chip_ref), 0, 0)),
                      table, table, HBM_OPERAND],
            out_specs=pl.BlockSpec((FF_ROWS, ns), lambda i, j, chip_ref: (i, other(j, chip_ref)))),
        out_shape=_sds((SEQ, N_CHIPS * ns)), input_output_aliases={5: 0},
        compiler_params=_cp(("parallel", "parallel")), name="project_in")(chip, x, wg, cos_f, sin_s, started)


def _attention_fwd(proj):
    ng = len(DILATIONS)

    def body(*refs):
        q_refs, k_refs, v_refs = refs[:ng], refs[ng:2 * ng], refs[2 * ng:3 * ng]
        attn_ref, lse_ref = refs[3 * ng:]
        qr_refs, kr_refs = q_refs, k_refs
        first = lax.broadcasted_iota(jnp.int32, (BLOCK, PAIR), 1) < HEAD_DIM
        for g, dil in enumerate(DILATIONS):
            two_blocks = SEQ // dil > BLOCK

            def units(t, carry, g=g, dil=dil, two_blocks=two_blocks):
                picked = [_unit_pieces(t * UNIT_BATCH + j, dil) for j in range(UNIT_BATCH)]

                def tiles(ref, with_prev=False):
                    if with_prev and two_blocks:
                        return jnp.stack([jnp.concatenate([_load_tile(ref, prev, dil), _load_tile(ref, rows, dil)],
                                                          axis=0) for _, rows, prev in picked])
                    return jnp.stack([_load_tile(ref, rows, dil) for _, rows, _ in picked])

                qq = tiles(qr_refs[g]).astype(BF16)
                kk = tiles(kr_refs[g], True).astype(BF16)
                vv = tiles(v_refs[g], True).astype(BF16)
                if two_blocks:
                    valid = jnp.stack([_band_mask(i, dil) for i, _, _ in picked])
                else:
                    valid = _causal_mask()[None]
                mine = first[None]
                zero = jnp.zeros_like(qq)
                outs, lses = [], []
                for qh in (jnp.where(mine, qq, zero), jnp.where(mine, zero, qq)):
                    s = jnp.einsum("pqd,pkd->pqk", qh, kk, preferred_element_type=F32)
                    s = jnp.where(valid, s, NEG_INF)
                    m = jnp.max(s, axis=-1, keepdims=True)
                    p = jnp.exp(s - m)
                    l = jnp.sum(p, axis=-1, keepdims=True)
                    outs.append(jnp.einsum("pqk,pkd->pqd", p.astype(BF16), vv, preferred_element_type=F32) * (1.0 / l))
                    lses.append(m + jnp.log(l))
                o = jnp.where(mine, outs[0], outs[1])
                lse = jnp.where(mine, lses[0], lses[1])
                if g > 0:
                    lse_old = tiles(lse_ref)
                    m = jnp.maximum(lse_old, lse)
                    lse_new = m + jnp.log(jnp.exp(lse_old - m) + jnp.exp(lse - m))
                    o = tiles(attn_ref) * jnp.exp(lse_old - lse_new) + o * jnp.exp(lse - lse_new)
                    lse = lse_new
                for j, (_, rows, _) in enumerate(picked):
                    _store_tile(attn_ref, rows, dil, o[j])
                    _store_tile(lse_ref, rows, dil, lse[j])
                return carry

            lax.fori_loop(0, UNITS // UNIT_BATCH, units, 0)

    out = pl.BlockSpec((SEQ, PAIR), lambda hp: (0, hp))
    return pl.pallas_call(
        body, grid=(ATTN_WIDTH // PAIR,),
        in_specs=_pair_views(0) + _pair_views(QKV_WIDTH) + _pair_views(2 * QKV_WIDTH),
        out_specs=[out, out], out_shape=[_sds((SEQ, ATTN_WIDTH)), _sds((SEQ, ATTN_WIDTH))],
        compiler_params=_cp(("parallel",)), name="attention_fwd")(*([proj] * (3 * ng)))


def _attention_bwd(proj, cos_f, sin_s, d_attn, attn, lse, d_u, d_gl):
    pairs = ATTN_WIDTH // PAIR
    last = len(DILATIONS) * pairs - 1

    def accumulate(dil, qr_ref, kr_ref, v_ref, do_ref, o_ref, lse_ref, dq_acc, dk_acc, dv_acc):
        two_blocks = SEQ // dil > BLOCK
        dk_acc[...] = jnp.zeros_like(dk_acc)
        dv_acc[...] = jnp.zeros_like(dv_acc)
        nk = 2 * BLOCK if two_blocks else BLOCK
        first = lax.broadcasted_iota(jnp.int32, (BLOCK, PAIR), 1) < HEAD_DIM
        first_k = lax.broadcasted_iota(jnp.int32, (nk, PAIR), 1) < HEAD_DIM

        def units(t, carry):
            picked = [_unit_pieces(t * UNIT_BATCH + j, dil) for j in range(UNIT_BATCH)]

            def tiles(ref, with_prev=False):
                if with_prev and two_blocks:
                    return jnp.stack([jnp.concatenate([_load_tile(ref, prev, dil), _load_tile(ref, rows, dil)], axis=0)
                                      for _, rows, prev in picked])
                return jnp.stack([_load_tile(ref, rows, dil) for _, rows, _ in picked])

            qq = tiles(qr_ref).astype(BF16)
            kk = tiles(kr_ref, True).astype(BF16)
            vv = tiles(v_ref, True).astype(BF16)
            dof = tiles(do_ref)
            dd = dof * tiles(o_ref)
            lse3 = tiles(lse_ref)
            dob = dof.astype(BF16)
            if two_blocks:
                valid = jnp.stack([_band_mask(i, dil) for i, _, _ in picked])
            else:
                valid = _causal_mask()[None]
            zq, zf = jnp.zeros_like(qq), jnp.zeros_like(dd)
            dqs, dks, dvs = [], [], []
            for head in range(2):
                mine = first[None] if head == 0 else jnp.logical_not(first)[None]
                delta = jnp.sum(jnp.where(mine, dd, zf), axis=-1, keepdims=True)
                lse_h = lse3[:, :, head * HEAD_DIM:head * HEAD_DIM + 1]
                s = jnp.einsum("pqd,pkd->pqk", jnp.where(mine, qq, zq), kk, preferred_element_type=F32)
                p = jnp.where(valid, jnp.exp(s - lse_h), 0.0)
                dp = jnp.einsum("pqd,pkd->pqk", jnp.where(mine, dob, zq), vv, preferred_element_type=F32)
                ds = (p * (dp - delta)).astype(BF16)
                dqs.append(jnp.einsum("pqk,pkd->pqd", ds, kk, preferred_element_type=F32))
                dks.append(jnp.einsum("pqk,pqd->pkd", ds, qq, preferred_element_type=F32))
                dvs.append(jnp.einsum("pqk,pqd->pkd", p.astype(BF16), dob, preferred_element_type=F32))
            dq = jnp.where(first[None], dqs[0], dqs[1])
            dk = jnp.where(first_k[None], dks[0], dks[1])
            dv = jnp.where(first_k[None], dvs[0], dvs[1])
            for j, (_, rows, prev) in enumerate(picked):
                _store_tile(dq_acc, rows, dil, dq[j])
                _store_tile(dk_acc, rows, dil, dk[j, nk - BLOCK:], accumulate=True)
                _store_tile(dv_acc, rows, dil, dv[j, nk - BLOCK:], accumulate=True)
                if two_blocks:
                    _store_tile(dk_acc, prev, dil, dk[j, :BLOCK], accumulate=True)
                    _store_tile(dv_acc, prev, dil, dv[j, :BLOCK], accumulate=True)
            return carry

        lax.fori_loop(0, UNITS // UNIT_BATCH, units, 0)

    def body(qr_ref, kr_ref, v_ref, cf_ref, ss_ref, do_ref, o_ref, lse_ref, du_ref, dgl_ref, out_ref,
             dq_acc, dk_acc, dv_acc, dq_buf, dk_buf, dv_buf, sems):
        step = pl.program_id(0) * pairs + pl.program_id(1)

        def columns(at):
            return [pltpu.make_async_copy(
                buf, out_ref.at[:, pl.ds(pl.multiple_of(j * QKV_WIDTH + at * PAIR, PAIR), PAIR)], sems.at[j])
                for j, buf in enumerate((dq_buf, dk_buf, dv_buf))]

        def tail(ref, col0, at):
            cols = pl.ds(pl.multiple_of(col0 + at * TAIL_COLS, TAIL_COLS), TAIL_COLS)
            return pltpu.make_async_copy(ref, out_ref.at[:, cols], sems.at[3])

        gl_steps, u_steps = 2 * D_MODEL // TAIL_COLS, SSM_WIDTH // TAIL_COLS
        from_gl = step < gl_steps
        from_u = jnp.logical_and(step >= gl_steps, step < gl_steps + u_steps)
        tail_gl = tail(dgl_ref, 3 * QKV_WIDTH + SSM_WIDTH, step)
        tail_u = tail(du_ref, 3 * QKV_WIDTH, step - gl_steps)
        pl.when(from_gl)(tail_gl.start)
        pl.when(from_u)(tail_u.start)

        for g, dil in enumerate(DILATIONS):
            @pl.when(pl.program_id(0) == g)
            def _(dil=dil):
                accumulate(dil, qr_ref, kr_ref, v_ref, do_ref, o_ref, lse_ref, dq_acc, dk_acc, dv_acc)

        @pl.when(step > 0)
        def _():
            for cp in columns(step - 1):
                cp.wait()

        def finish(t, carry):
            rows = pl.ds(pl.multiple_of(t * ROPE_ROWS, ROPE_ROWS), ROPE_ROWS)
            cf, ss = cf_ref[rows, :], ss_ref[rows, :]
            dq = dq_acc[rows, :] * (1.0 / math.sqrt(HEAD_DIM))
            dq_buf[rows, :] = _rope_transposed(dq, cf, ss).astype(BF16)
            dk_buf[rows, :] = _rope_transposed(dk_acc[rows, :], cf, ss).astype(BF16)
            dv_buf[rows, :] = dv_acc[rows, :].astype(BF16)
            return carry

        lax.fori_loop(0, SEQ // ROPE_ROWS, finish, 0)
        for cp in columns(step):
            cp.start()
        pl.when(from_gl)(tail_gl.wait)
        pl.when(from_u)(tail_u.wait)

        @pl.when(step == last)
        def _():
            for cp in columns(step):
                cp.wait()

    whole = pl.BlockSpec((SEQ, PAIR), lambda g, hp: (0, 0))
    pair = pl.BlockSpec((SEQ, PAIR), lambda g, hp: (0, hp))
    views = [pl.BlockSpec((SEQ, PAIR), lambda g, hp, c0=col0 // PAIR: (0, c0 + g * pairs + hp))
             for col0 in (0, QKV_WIDTH, 2 * QKV_WIDTH)]
    gl_blocks, u_blocks = 2 * D_MODEL // TAIL_COLS, SSM_WIDTH // TAIL_COLS
    assert gl_blocks + u_blocks <= last + 1
    gl_spec = pl.BlockSpec((SEQ, TAIL_COLS), lambda g, hp: (0, jnp.minimum(g * pairs + hp, gl_blocks - 1)))
    u_spec = pl.BlockSpec((SEQ, TAIL_COLS),
                          lambda g, hp: (0, jnp.clip(g * pairs + hp - gl_blocks, 0, u_blocks - 1)))
    return pl.pallas_call(
        body, grid=(len(DILATIONS), pairs),
        in_specs=views + [whole, whole, pair, pair, pair, u_spec, gl_spec],
        out_specs=HBM_OPERAND, out_shape=_sds((SEQ, IN_WIDTH), BF16),
        scratch_shapes=[pltpu.VMEM((SEQ, PAIR), F32)] * 3 + [pltpu.VMEM((SEQ, PAIR), BF16)] * 3
        + [pltpu.SemaphoreType.DMA((4,))],
        compiler_params=_cp(("arbitrary", "arbitrary")), name="attention_bwd")(
            proj, proj, proj, cos_f, sin_s, d_attn, attn, lse, d_u, d_gl)


def _cmul(ar, ai, br, bi):
    return ar * br - ai * bi, ar * bi + ai * br


def _pow256(ar, ai):
    for _ in range(8):
        ar, ai = _cmul(ar, ai, ar, ai)
    return ar, ai


def _chunk_carries(first_r, first_i, pr, pi, reverse):
    rows = lax.broadcasted_iota(jnp.int32, first_r.shape, 0)
    out_r = jnp.zeros_like(first_r)
    out_i = jnp.zeros_like(first_i)
    hr = jnp.zeros_like(first_r[0:1])
    hi = jnp.zeros_like(hr)
    order = range(SCAN_CHUNKS - 1, -1, -1) if reverse else range(SCAN_CHUNKS)
    for c in order:
        out_r = jnp.where(rows == c, hr, out_r)
        out_i = jnp.where(rows == c, hi, out_i)
        tr, ti = _cmul(pr[0:1], pi[0:1], hr, hi)
        hr = first_r[c:c + 1] + tr
        hi = first_i[c:c + 1] + ti
    return out_r, out_i


def _tile(j):
    return pl.ds(pl.multiple_of(j * SCAN_CHUNKS, SCAN_CHUNKS), SCAN_CHUNKS)


def _to_scan_rows(t):
    per = SCAN_STEPS // PHASES
    return t.reshape(PHASES, SCAN_CHUNKS, per, t.shape[1]).transpose(2, 0, 1, 3).reshape(t.shape)


def _from_scan_rows(t):
    per = SCAN_STEPS // PHASES
    return t.reshape(per, PHASES, SCAN_CHUNKS, t.shape[1]).transpose(1, 2, 0, 3).reshape(t.shape)


def _scan_in_place(hr_ref, hi_ref, a_r, a_i):
    def local(j, carry):
        tr, ti = _cmul(a_r, a_i, carry[0], carry[1])
        nr = tr + hr_ref[_tile(j), :]
        ni = ti + hi_ref[_tile(j), :]
        hr_ref[_tile(j), :] = nr
        hi_ref[_tile(j), :] = ni
        return nr, ni

    zero = jnp.zeros_like(a_r)
    last_r, last_i = lax.fori_loop(0, SCAN_STEPS, local, (zero, zero), unroll=4)
    pr, pi = _pow256(a_r, a_i)
    er, ei = _chunk_carries(last_r, last_i, pr, pi, reverse=False)

    def fix(j, carry):
        tr, ti = _cmul(carry[0], carry[1], er, ei)
        hr_ref[_tile(j), :] += tr
        hi_ref[_tile(j), :] += ti
        return _cmul(carry[0], carry[1], a_r, a_i)

    lax.fori_loop(0, SCAN_STEPS, fix, (a_r, a_i), unroll=4)
    return er, ei


def _reverse_scan_in_place(lr_ref, li_ref, hr_ref, hi_ref, er, ei, a_r, a_i):
    def local(t, carry):
        j = SCAN_STEPS - 1 - t
        tr, ti = _cmul(a_r, a_i, carry[0], carry[1])
        nr = tr + lr_ref[_tile(j), :]
        ni = ti + li_ref[_tile(j), :]
        lr_ref[_tile(j), :] = nr
        li_ref[_tile(j), :] = ni
        return nr, ni

    zero = jnp.zeros_like(a_r)
    first_r, first_i = lax.fori_loop(0, SCAN_STEPS, local, (zero, zero), unroll=4)
    pr, pi = _pow256(a_r, a_i)
    nxt_r, nxt_i = _chunk_carries(first_r, first_i, pr, pi, reverse=True)

    def accumulate(lam_r, lam_i, hp_r, hp_i, acc):
        return (acc[0] + lam_r * hp_r + lam_i * hp_i, acc[1] + lam_i * hp_r - lam_r * hp_i)

    def fix(t, carry):
        qr, qi, acc_r, acc_i = carry
        j = SCAN_STEPS - 1 - t
        tr, ti = _cmul(qr, qi, nxt_r, nxt_i)
        lam_r = lr_ref[_tile(j), :] + tr
        lam_i = li_ref[_tile(j), :] + ti
        lr_ref[_tile(j), :] = lam_r
        li_ref[_tile(j), :] = lam_i
        acc_r, acc_i = accumulate(lam_r, lam_i, hr_ref[_tile(j - 1), :], hi_ref[_tile(j - 1), :], (acc_r, acc_i))
        qr, qi = _cmul(qr, qi, a_r, a_i)
        return qr, qi, acc_r, acc_i

    qr, qi, acc_r, acc_i = lax.fori_loop(0, SCAN_STEPS - 1, fix, (a_r, a_i, zero, zero), unroll=4)
    tr, ti = _cmul(qr, qi, nxt_r, nxt_i)
    lam_r = lr_ref[_tile(0), :] + tr
    lam_i = li_ref[_tile(0), :] + ti
    lr_ref[_tile(0), :] = lam_r
    li_ref[_tile(0), :] = lam_i
    acc_r, acc_i = accumulate(lam_r, lam_i, er, ei, (acc_r, acc_i))
    return jnp.sum(acc_r, axis=0, keepdims=True), jnp.sum(acc_i, axis=0, keepdims=True)


def _rope_tables():
    half = HEAD_DIM // 2
    inv_freq = ROPE_THETA ** (-jnp.arange(half, dtype=F32) / half)
    ang = jnp.arange(SEQ, dtype=F32)[:, None] * inv_freq[None, :]
    cos, sin = jnp.cos(ang), jnp.sin(ang)
    cos_f = jnp.concatenate([cos, cos, cos, cos], axis=1)
    sin_s = jnp.concatenate([-sin, sin, -sin, sin], axis=1)
    return cos_f, sin_s


def _ssm_discretise(a_re, a_im, log_dt, b_re, b_im):
    lam = lax.complex(a_re, a_im)
    dt = jnp.exp(log_dt)[:, None]
    a_bar = jnp.exp(lam * dt)
    b_bar = ((a_bar - 1.0) / lam)[..., None] * lax.complex(b_re, b_im)
    return a_bar.real, a_bar.imag, b_bar.real, b_bar.imag


SSM_SLABS = 4
SLAB_GROUPS = SSM_GROUPS // SSM_SLABS
SLAB_IN = SSM_WIDTH // SSM_SLABS
SLAB_STATE = SSM_LANES // SSM_SLABS


def _slab_block_diag(blocks):
    _, r, c = blocks.shape
    eye = jnp.eye(SLAB_GROUPS, dtype=blocks.dtype)
    b5 = blocks.reshape(SSM_SLABS, SLAB_GROUPS, r, 1, c) * eye[None, :, None, :, None]
    return b5.reshape(SSM_SLABS, SLAB_GROUPS * r, SLAB_GROUPS * c)


def _diag_blocks(a, b):
    ra, cb = a.shape[1], b.shape[1]
    wa, wb = ra // SLAB_GROUPS, cb // SLAB_GROUPS
    d = lax.dot_general(a, b, (((0,), (0,)), ((), ())), preferred_element_type=F32)
    row_g = jnp.right_shift(lax.broadcasted_iota(jnp.int32, (ra, cb), 0), wa.bit_length() - 1)
    col_g = jnp.right_shift(lax.broadcasted_iota(jnp.int32, (ra, cb), 1), wb.bit_length() - 1)
    d = jnp.where(row_g == col_g, d, 0.0)
    fold = (jnp.bitwise_and(lax.broadcasted_iota(jnp.int32, (cb, wb), 0), wb - 1)
            == lax.broadcasted_iota(jnp.int32, (cb, wb), 1)).astype(F32)
    return jnp.dot(d, fold, preferred_element_type=F32, precision=lax.Precision.HIGHEST)


def _slab_specs():
    tok = pl.BlockSpec((SEQ, SLAB_IN), lambda j: (0, j))
    state = pl.BlockSpec((SEQ, SLAB_STATE), lambda j: (0, j))
    b_in = pl.BlockSpec((None, SLAB_IN, SLAB_STATE), lambda j: (j, 0, 0))
    c_out = pl.BlockSpec((None, SLAB_STATE, SLAB_IN), lambda j: (j, 0, 0))
    vec = pl.BlockSpec((1, SLAB_STATE), lambda j: (0, j))
    ent = pl.BlockSpec((SCAN_CHUNKS, SLAB_STATE), lambda j: (0, j))
    return tok, state, b_in, c_out, vec, ent


def _ssm_forward(u, b_in_r, b_in_i, c_out_r, c_out_ni, a_r, a_i):
    def body(u_ref, br_ref, bi_ref, cr_ref, ci_ref, ar_ref, ai_ref, y_ref, hr_ref, hi_ref, er_ref, ei_ref):
        uu = u_ref[...]
        hr_ref[...] = jnp.dot(uu, br_ref[...], preferred_element_type=F32)
        hi_ref[...] = jnp.dot(uu, bi_ref[...], preferred_element_type=F32)
        a_re = jnp.broadcast_to(ar_ref[...], (SCAN_CHUNKS, SLAB_STATE))
        a_im = jnp.broadcast_to(ai_ref[...], (SCAN_CHUNKS, SLAB_STATE))
        er_ref[...], ei_ref[...] = _scan_in_place(hr_ref, hi_ref, a_re, a_im)
        y_ref[...] = (jnp.dot(hr_ref[...].astype(BF16), cr_ref[...], preferred_element_type=F32)
                      + jnp.dot(hi_ref[...].astype(BF16), ci_ref[...], preferred_element_type=F32))

    tok, state, b_in, c_out, vec, ent = _slab_specs()
    return pl.pallas_call(
        body, grid=(SSM_SLABS,), in_specs=[tok, b_in, b_in, c_out, c_out, vec, vec],
        out_specs=[tok, state, state, ent, ent],
        out_shape=[_sds((SEQ, SSM_WIDTH)), _sds((SEQ, SSM_LANES)), _sds((SEQ, SSM_LANES)),
                   _sds((SCAN_CHUNKS, SSM_LANES)), _sds((SCAN_CHUNKS, SSM_LANES))],
        compiler_params=_cp(("parallel",)), name="ssm_forward")(u, b_in_r, b_in_i, c_out_r, c_out_ni, a_r, a_i)


def _ssm_backward(d_y, d_u_skip, u, h_r, h_i, e_r, e_i, b_in_r, b_in_i, c_out_r, c_out_ni, a_r, a_i):
    def body(dy_ref, skip_ref, u_ref, hr_ref, hi_ref, er_ref, ei_ref, br_ref, bi_ref, cr_ref, ci_ref, ar_ref, ai_ref,
             du_ref, dar_ref, dai_ref, dcr_ref, dci_ref, dbr_ref, dbi_ref, lr_ref, li_ref):
        dy = dy_ref[...]
        lr_ref[...] = _dot_nt(dy, cr_ref[...])
        li_ref[...] = _dot_nt(dy, ci_ref[...])
        a_re = jnp.broadcast_to(ar_ref[...], (SCAN_CHUNKS, SLAB_STATE))
        a_im = -jnp.broadcast_to(ai_ref[...], (SCAN_CHUNKS, SLAB_STATE))
        dar_ref[...], dai_ref[...] = _reverse_scan_in_place(lr_ref, li_ref, hr_ref, hi_ref, er_ref[...], ei_ref[...],
                                                            a_re, a_im)
        dcr_ref[...] = _diag_blocks(dy, hr_ref[...].astype(BF16))
        dci_ref[...] = _diag_blocks(dy, hi_ref[...].astype(BF16))
        lam_r, lam_i = lr_ref[...].astype(BF16), li_ref[...].astype(BF16)
        uu = u_ref[...]
        dbr_ref[...] = _diag_blocks(uu, lam_r)
        dbi_ref[...] = _diag_blocks(uu, lam_i)
        du = skip_ref[...] + _dot_nt(lam_r, br_ref[...]) + _dot_nt(lam_i, bi_ref[...])
        du_ref[...] = du.astype(BF16)

    tok, state, b_in, c_out, vec, ent = _slab_specs()
    db = pl.BlockSpec((SLAB_IN, SSM_STATE), lambda j: (j, 0))
    return pl.pallas_call(
        body, grid=(SSM_SLABS,), in_specs=[tok, tok, tok, state, state, ent, ent, b_in, b_in, c_out, c_out, vec, vec],
        out_specs=[tok, vec, vec, db, db, db, db],
        out_shape=[_sds((SEQ, SSM_WIDTH), BF16), _sds((1, SSM_LANES)), _sds((1, SSM_LANES))]
        + [_sds((SSM_WIDTH, SSM_STATE))] * 4,
        scratch_shapes=[pltpu.VMEM((SEQ, SLAB_STATE), F32)] * 2,
        compiler_params=_cp(("parallel",)), name="ssm_backward")(
            d_y, d_u_skip, u, h_r, h_i, e_r, e_i, b_in_r, b_in_i, c_out_r, c_out_ni, a_r, a_i)


FF_ROWS = 1024
FF_SHARD = D_FF // N_CHIPS


def _dot_nt(a, b):
    return lax.dot_general(a, b, (((1,), (1,)), ((), ())), preferred_element_type=F32)


def _ffn_up(h, w_gate_t, w_up_t):
    def body(h_ref, wg_ref, wu_ref, a_ref, b_ref, act_ref):
        hb = h_ref[...].astype(BF16)
        a = _dot_nt(hb, wg_ref[...])
        b = _dot_nt(hb, wu_ref[...])
        a_ref[...] = a
        b_ref[...] = b
        act_ref[...] = (a * jax.nn.sigmoid(a) * b).astype(BF16)

    w_spec = pl.BlockSpec((None, FF_SHARD, D_MODEL), lambda i, k: (k, 0, 0))
    o_spec = pl.BlockSpec((None, FF_ROWS, FF_SHARD), lambda i, k: (k, i, 0))
    shape = (N_CHIPS, SEQ, FF_SHARD)
    return pl.pallas_call(
        body, grid=(SEQ // FF_ROWS, N_CHIPS),
        in_specs=[pl.BlockSpec((FF_ROWS, D_MODEL), lambda i, k: (i, 0)), w_spec, w_spec],
        out_specs=[o_spec, o_spec, o_spec], out_shape=[_sds(shape), _sds(shape), _sds(shape, BF16)],
        compiler_params=_cp(("parallel", "parallel")), name="ffn_up")(h, w_gate_t, w_up_t)


def _ffn_down_ln2_loss(act, w_down, h, tgt, ln_g, ln_b):
    def body(act_ref, w_ref, h_ref, tgt_ref, g_ref, b_ref, dz_ref, loss_ref, dg_ref, db_ref, acc):
        i, k = pl.program_id(0), pl.program_id(1)
        part = jnp.dot(act_ref[...], w_ref[...], preferred_element_type=F32)

        @pl.when(k == 0)
        def _():
            acc[...] = part

        @pl.when(k > 0)
        def _():
            acc[...] += part

        @pl.when(k == N_CHIPS - 1)
        def _():
            g = g_ref[...]
            xhat, rstd = _ln_stats(DN_ALPHA * h_ref[...] + acc[...])
            err = xhat * g + b_ref[...] - tgt_ref[...]
            d_out = err * (1.0 / D_MODEL)
            dz_ref[...] = _ln_bwd(d_out, xhat, rstd, g)
            loss_rows = jnp.sum(err * err, axis=-1, keepdims=True) * (0.5 / D_MODEL)
            sums = (jnp.broadcast_to(jnp.sum(loss_rows, axis=0, keepdims=True), loss_ref.shape),
                    _colsum(d_out * xhat), _colsum(d_out))
            for ref, val in zip((loss_ref, dg_ref, db_ref), sums):
                @pl.when(i == 0)
                def _(ref=ref, val=val):
                    ref[...] = val

                @pl.when(i > 0)
                def _(ref=ref, val=val):
                    ref[...] += val

    row = pl.BlockSpec((FF_ROWS, D_MODEL), lambda i, k: (i, 0))
    vec = pl.BlockSpec((1, D_MODEL), lambda i, k: (0, 0))
    return pl.pallas_call(
        body, grid=(SEQ // FF_ROWS, N_CHIPS),
        in_specs=[pl.BlockSpec((None, FF_ROWS, FF_SHARD), lambda i, k: (k, i, 0)),
                  pl.BlockSpec((None, FF_SHARD, D_MODEL), lambda i, k: (k, 0, 0)), row, row, vec, vec],
        out_specs=[row, pl.BlockSpec((1, BLOCK), lambda i, k: (0, 0)), vec, vec],
        out_shape=[_sds((SEQ, D_MODEL)), _sds((1, BLOCK)), _sds((1, D_MODEL)), _sds((1, D_MODEL))],
        scratch_shapes=[pltpu.VMEM((FF_ROWS, D_MODEL), F32)],
        compiler_params=_cp(("arbitrary", "arbitrary")), name="ffn_down_ln2_loss")(act, w_down, h, tgt, ln_g, ln_b)


def _ffn_down_bwd(dz, w_down, a, b):
    def body(dz_ref, wd_ref, a_ref, b_ref, da_ref, db_ref):
        d_act = _dot_nt(dz_ref[...].astype(BF16), wd_ref[...])
        av = a_ref[...]
        sg = jax.nn.sigmoid(av)
        da_ref[...] = (d_act * b_ref[...] * sg * (1.0 + av * (1.0 - sg))).astype(BF16)
        db_ref[...] = (d_act * av * sg).astype(BF16)

    t_spec = pl.BlockSpec((None, FF_ROWS, FF_SHARD), lambda i, k: (k, i, 0))
    shape = (N_CHIPS, SEQ, FF_SHARD)
    return pl.pallas_call(
        body, grid=(SEQ // FF_ROWS, N_CHIPS),
        in_specs=[pl.BlockSpec((FF_ROWS, D_MODEL), lambda i, k: (i, 0)),
                  pl.BlockSpec((None, FF_SHARD, D_MODEL), lambda i, k: (k, 0, 0)), t_spec, t_spec],
        out_specs=[t_spec, t_spec], out_shape=[_sds(shape, BF16), _sds(shape, BF16)],
        compiler_params=_cp(("parallel", "parallel")), name="ffn_down_bwd")(dz, w_down, a, b)


def _ffn_dh(d_a, d_b, w_gate_t, w_up_t):
    def body(da_ref, db_ref, wg_ref, wu_ref, o_ref, acc):
        k = pl.program_id(1)
        part = (jnp.dot(da_ref[...], wg_ref[...], preferred_element_type=F32)
                + jnp.dot(db_ref[...], wu_ref[...], preferred_element_type=F32))

        @pl.when(k == 0)
        def _():
            acc[...] = part

        @pl.when(k > 0)
        def _():
            acc[...] += part

        @pl.when(k == N_CHIPS - 1)
        def _():
            o_ref[...] = acc[...]

    t_spec = pl.BlockSpec((None, FF_ROWS, FF_SHARD), lambda i, k: (k, i, 0))
    w_spec = pl.BlockSpec((None, FF_SHARD, D_MODEL), lambda i, k: (k, 0, 0))
    return pl.pallas_call(
        body, grid=(SEQ // FF_ROWS, N_CHIPS), in_specs=[t_spec, t_spec, w_spec, w_spec],
        out_specs=pl.BlockSpec((FF_ROWS, D_MODEL), lambda i, k: (i, 0)), out_shape=_sds((SEQ, D_MODEL)),
        scratch_shapes=[pltpu.VMEM((FF_ROWS, D_MODEL), F32)],
        compiler_params=_cp(("parallel", "arbitrary")), name="ffn_dh")(d_a, d_b, w_gate_t, w_up_t)


def _local_step(x, tgt, wts, small, ffn_grads, mixer_grads, small_grads):
    s = SEQ
    cos_f, sin_s = [_to_phase_rows(t) for t in _rope_tables()]
    x = _reorder_rows(x, to_phase=True, name="phase_rows_x")
    tgt = _reorder_rows(tgt, to_phase=True, name="phase_rows_target")

    chip, w_own, w_others = wts["w_in_parts"]
    proj_own = _project_in_own(chip, x, w_own, cos_f, sin_s)
    proj = _project_in(chip, x, w_others, cos_f, sin_s, proj_own)

    attn, lse = _attention_fwd(proj)

    (abar_r, abar_i, bbar_r, bbar_i), ssm_vjp = jax.vjp(
        _ssm_discretise, small["ssm_a_re"], small["ssm_a_im"], small["ssm_log_dt"], small["ssm_b_re"], small["ssm_b_im"])
    b_in_r, b_in_i = [_slab_block_diag(b.transpose(0, 2, 1)).astype(BF16) for b in (bbar_r, bbar_i)]
    c_out_r = _slab_block_diag(small["ssm_c_re"].transpose(0, 2, 1)).astype(BF16)
    c_out_ni = _slab_block_diag(-small["ssm_c_im"].transpose(0, 2, 1)).astype(BF16)
    a_r, a_i = abar_r.reshape(1, SSM_LANES), abar_i.reshape(1, SSM_LANES)
    d_skip = small["ssm_d"].reshape(1, SSM_WIDTH)

    u_f = _to_scan_rows(proj[:, 3 * QKV_WIDTH:3 * QKV_WIDTH + SSM_WIDTH])
    u_p = u_f.astype(BF16)
    y_c, h_r, h_i, e_r, e_i = _ssm_forward(u_p, b_in_r, b_in_i, c_out_r, c_out_ni, a_r, a_i)

    def branch(t, wg):
        return jnp.concatenate([jnp.dot(t, wg[k], preferred_element_type=F32) for k in range(N_CHIPS)], axis=1)

    def branch_t(t, wg):
        ns = wg.shape[2]
        return sum(_dot_nt(t[:, k * ns:(k + 1) * ns], wg[k]) for k in range(N_CHIPS))

    def gelu_glu(yc, u, dsk, wg):
        y = yc + dsk * u
        gel = (0.5 * y * (1.0 + jnp.tanh(GELU_C * (y + GELU_K * y * y * y)))).astype(BF16)
        glu = branch(gel, wg)
        return y, gel, glu, glu[:, :SSM_WIDTH] * jax.nn.sigmoid(glu[:, SSM_WIDTH:])

    y_s5, gel, glu, y_glu = _rowwise(
        gelu_glu, [y_c, u_f], [d_skip, wts["w_glu"]],
        [_sds((s, SSM_WIDTH)), _sds((s, SSM_WIDTH), BF16), _sds((s, 2 * SSM_WIDTH)), _sds((s, SSM_WIDTH), BF16)],
        tm=512, name="ssm_gelu_glu")
    y_glu = _from_scan_rows(y_glu)

    gl0 = (proj, D_MODEL, (3 * QKV_WIDTH + SSM_WIDTH) // D_MODEL)
    gl1 = (proj, D_MODEL, (3 * QKV_WIDTH + SSM_WIDTH) // D_MODEL + 1)
    b_gate = small["b_gate"]
    w_out = wts["w_out"].reshape(D_MODEL, D_MODEL)

    def mix_ln1(l0, l1, at, yg, xv, bg, wa, ws, wo, g, b):
        ya = branch(at.astype(BF16), wa)
        ys = branch(yg, ws)
        mixed = (jax.nn.sigmoid(l0 + bg[0:1]) * ya + jax.nn.sigmoid(l1 + bg[1:2]) * ys).astype(BF16)
        z = DN_ALPHA * xv + jnp.dot(mixed, wo, preferred_element_type=F32)
        xhat, _ = _ln_stats(z)
        return ya, ys, mixed, z, xhat * g + b

    y_attn, y_ssm, mixed, z1, h = _rowwise(
        mix_ln1, [gl0, gl1, attn, y_glu, x],
        [b_gate, wts["w_attn_br"], wts["w_ssm_br"], w_out, small["ln1_g"], small["ln1_b"]],
        [_sds((s, D_MODEL)), _sds((s, D_MODEL)), _sds((s, D_MODEL), BF16), _sds((s, D_MODEL)), _sds((s, D_MODEL))],
        tm=256, name="mix_ln1")

    nf = D_FF // N_CHIPS
    w_gate_t, w_up_t, w_down = wts["w_ff_gate"], wts["w_ff_up"], wts["w_ff_down"]
    ff_a, ff_b, act = _ffn_up(h, w_gate_t, w_up_t)
    dz2, loss_v, d_ln2_g, d_ln2_b = _ffn_down_ln2_loss(act, w_down, h, tgt, small["ln2_g"], small["ln2_b"])

    d_a, d_b = _ffn_down_bwd(dz2, w_down, ff_a, ff_b)

    def grad_rows(lhs, rhs, name):
        return _matmul(lhs, rhs, grid=(N_CHIPS,), a_spec=pl.BlockSpec((None, s, nf), lambda k: (k, 0, 0)),
                       b_spec=pl.BlockSpec((s, D_MODEL), lambda k: (0, 0)),
                       o_spec=pl.BlockSpec((None, nf, D_MODEL), lambda k: (k, 0, 0)),
                       out_shape=_sds((N_CHIPS, nf, D_MODEL), BF16), dims=(0, 0), name=name)

    g_w_ff_down = grad_rows(act, dz2, "g_w_ff_down")
    g_w_ff_gate = grad_rows(d_a, h, "g_w_ff_gate")
    g_w_ff_up = grad_rows(d_b, h, "g_w_ff_up")
    dh_ff = _ffn_dh(d_a, d_b, w_gate_t, w_up_t)

    def ln1_gate_bwd(dz, dff, z, l0, l1, ya, ys, g, bg, wo, wa, ws):
        xhat, rstd = _ln_stats(z)
        dh = DN_ALPHA * dz + dff
        dz_in = _ln_bwd(dh, xhat, rstd, g)
        dm = _dot_nt(dz_in.astype(BF16), wo)
        g0 = jax.nn.sigmoid(l0 + bg[0:1])
        g1 = jax.nn.sigmoid(l1 + bg[1:2])
        dl0 = dm * ya * g0 * (1.0 - g0)
        dl1 = dm * ys * g1 * (1.0 - g1)
        dya, dys = (dm * g0).astype(BF16), (dm * g1).astype(BF16)
        return (dz_in, dya, dys, jnp.concatenate([dl0, dl1], axis=1), branch_t(dya, wa), branch_t(dys, ws),
                _colsum(dh * xhat), _colsum(dh), _colsum(dl0), _colsum(dl1))

    dz1, d_y_attn, d_y_ssm, d_gl, d_attn, d_y_glu, d_ln1_g, d_ln1_b, d_bg0, d_bg1 = _rowwise(
        ln1_gate_bwd, [dz2, dh_ff, z1, gl0, gl1, y_attn, y_ssm],
        [small["ln1_g"], b_gate, w_out, wts["w_attn_br"], wts["w_ssm_br"]],
        [_sds((s, D_MODEL)), _sds((s, D_MODEL), BF16), _sds((s, D_MODEL), BF16), _sds((s, 2 * D_MODEL), BF16),
         _sds((s, ATTN_WIDTH)), _sds((s, SSM_WIDTH))],
        [_sds((1, D_MODEL))] * 4, tm=256, name="ln1_gate_bwd", after=(g_w_ff_down, g_w_ff_gate, g_w_ff_up))
    ffn_sent = ffn_grads({"w_ff_down": g_w_ff_down, "w_ff_gate": g_w_ff_gate, "w_ff_up": g_w_ff_up}, dz1)
    g_w_out = _mm_rows_tn(mixed, dz1, name="g_w_out")

    g_w_ssm_br = _mm_cols_tn(y_glu, d_y_ssm, ns=D_MODEL // N_CHIPS, name="g_w_ssm_br")
    d_y_glu = _to_scan_rows(d_y_glu)

    def glu_gelu_bwd(dyg, gl, y, u, dsk, wg):
        ga, gb = gl[:, :SSM_WIDTH], gl[:, SSM_WIDTH:]
        sg = jax.nn.sigmoid(gb)
        d_gl = jnp.concatenate([dyg * sg, dyg * ga * sg * (1.0 - sg)], axis=1).astype(BF16)
        dg = branch_t(d_gl, wg)
        th = jnp.tanh(GELU_C * (y + GELU_K * y * y * y))
        dy = dg * (0.5 * (1.0 + th) + 0.5 * y * (1.0 - th * th) * GELU_C * (1.0 + 3.0 * GELU_K * y * y))
        return d_gl, dy, dy * dsk, _colsum(dy * u)

    d_glu, d_y, d_u_skip, d_ssm_d = _rowwise(
        glu_gelu_bwd, [d_y_glu, glu, y_s5, u_f], [d_skip, wts["w_glu"]],
        [_sds((s, 2 * SSM_WIDTH), BF16), _sds((s, SSM_WIDTH), BF16), _sds((s, SSM_WIDTH))], [_sds((1, SSM_WIDTH))],
        tm=512, name="glu_gelu_bwd", after=tuple(ffn_sent))
    g_w_glu = _mm_cols_tn(gel, d_glu, ns=2 * SSM_WIDTH // N_CHIPS, name="g_w_glu")
    d_u, d_abar_r, d_abar_i, d_c_r, d_c_ni, d_bin_r, d_bin_i = _ssm_backward(
        d_y, d_u_skip, u_p, h_r, h_i, e_r, e_i, b_in_r, b_in_i, c_out_r, c_out_ni, a_r, a_i)
    d_u = _from_scan_rows(d_u)
    d_bbar_r = d_bin_r.reshape(SSM_GROUPS, SSM_GROUP, SSM_STATE).transpose(0, 2, 1)
    d_bbar_i = d_bin_i.reshape(SSM_GROUPS, SSM_GROUP, SSM_STATE).transpose(0, 2, 1)
    d_a_re, d_a_im, d_log_dt, d_b_re, d_b_im = ssm_vjp(
        (d_abar_r.reshape(SSM_GROUPS, SSM_STATE), d_abar_i.reshape(SSM_GROUPS, SSM_STATE), d_bbar_r, d_bbar_i))
    d_c_re = d_c_r.reshape(SSM_GROUPS, SSM_GROUP, SSM_STATE)
    d_c_im = -d_c_ni.reshape(SSM_GROUPS, SSM_GROUP, SSM_STATE)

    g_w_attn_br = _mm_cols_tn(attn, d_y_attn, ns=D_MODEL // N_CHIPS, name="g_w_attn_br")
    mixer_grads({"w_out": g_w_out, "w_ssm_br": g_w_ssm_br, "w_glu": g_w_glu, "w_attn_br": g_w_attn_br}, d_abar_r)
    small_g = {"b_gate": jnp.concatenate([d_bg0, d_bg1], axis=0), "ssm_a_re": d_a_re, "ssm_a_im": d_a_im,
               "ssm_log_dt": d_log_dt, "ssm_b_re": d_b_re, "ssm_b_im": d_b_im, "ssm_c_re": d_c_re, "ssm_c_im": d_c_im,
               "ssm_d": d_ssm_d.reshape(SSM_WIDTH), "ln1_g": d_ln1_g, "ln1_b": d_ln1_b, "ln2_g": d_ln2_g,
               "ln2_b": d_ln2_b}
    shared = small_grads(small_g, loss_v[0, 0])
    d_proj = _attention_bwd(proj, cos_f, sin_s, d_attn, attn, lse, d_u, d_gl)

    g_w_in = _mm_cols_tn(x, d_proj, ns=IN_WIDTH // N_CHIPS, name="g_w_in", after=tuple(shared))

    def grad_x_after(after):
        dx = _grad_layer_input(chip, d_proj, w_others, w_own, dz1, tm=FF_ROWS, name="dx_proj", after=after)
        return _reorder_rows(dx, to_phase=False, name="grad_x")

    return grad_x_after, g_w_in, d_proj


GATHER_ID, SWAP_ID, SCATTER_ID, JOIN_ID, EXCHANGE_ID = 1, 2, 3, 4, 5


def _place():
    return lax.axis_index("x"), lax.axis_index("y"), lax.axis_index("c")


def _other_chips(x, y):
    return [(1 - x, y), (x, 1 - y), (1 - x, 1 - y)]


def _handshake(peers):
    barrier = pltpu.get_barrier_semaphore()
    for peer in peers:
        pl.semaphore_signal(barrier, inc=1, device_id=peer, device_id_type=MESH)
    pl.semaphore_wait(barrier, len(peers))


def _sequencer(body, arrays, out_type, sems, collective_id, name):
    return pl.kernel(body, name=name, out_type=out_type,
                     mesh=plsc.ScalarSubcoreMesh(axis_name="sequencer", num_cores=1), scratch_types=sems,
                     compiler_params=pltpu.CompilerParams(collective_id=collective_id))(*arrays)


def _gather_weights(shards, *, name, own_slot=True):
    nw = len(shards)

    def body(*refs):
        ins, outs = refs[:nw], refs[nw:2 * nw]
        send_sems, recv_sems, pass_send, pass_recv, local_sems = refs[2 * nw:]
        x, y, c = _place()
        chip = 2 * x + y
        chips = _other_chips(x, y)
        _handshake([(x, y, 1 - c)] + [(cx, cy, c) for cx, cy in chips])
        started, local = [], []
        for w in range(nw):
            hw = shards[w].shape[0] // 2
            mine = pl.ds(c * hw, hw)
            if own_slot:
                own = pltpu.make_async_copy(ins[w], outs[w].at[chip], local_sems.at[w])
                own.start()
                local.append(own)
            for j, (cx, cy) in enumerate(chips):
                cp = pltpu.make_async_remote_copy(
                    src_ref=ins[w].at[mine], dst_ref=outs[w].at[chip, mine], send_sem=send_sems.at[w, j],
                    recv_sem=recv_sems.at[w, j], device_id=(cx, cy, c), device_id_type=MESH)
                cp.start()
                started.append(cp)
        passed = []
        for w in range(nw):
            hw = shards[w].shape[0] // 2
            mine = pl.ds(c * hw, hw)
            for j, (cx, cy) in enumerate(chips):
                landed = outs[w].at[2 * cx + cy, mine]
                pltpu.make_async_remote_copy(
                    src_ref=ins[w].at[mine], dst_ref=landed, send_sem=send_sems.at[w, j],
                    recv_sem=recv_sems.at[w, j], device_id=(cx, cy, c), device_id_type=MESH).wait_recv()
                cp = pltpu.make_async_remote_copy(
                    src_ref=landed, dst_ref=landed, send_sem=pass_send.at[w, j], recv_sem=pass_recv.at[w, j],
                    device_id=(x, y, 1 - c), device_id_type=MESH)
                cp.start()
                passed.append(cp)
        for w in range(nw):
            hw = shards[w].shape[0] // 2
            theirs = pl.ds((1 - c) * hw, hw)
            for j, (cx, cy) in enumerate(chips):
                landed = outs[w].at[2 * cx + cy, theirs]
                pltpu.make_async_remote_copy(
                    src_ref=landed, dst_ref=landed, send_sem=pass_send.at[w, j], recv_sem=pass_recv.at[w, j],
                    device_id=(x, y, 1 - c), device_id_type=MESH).wait_recv()
        for cp in local:
            cp.wait()
        for cp in started + passed:
            cp.wait_send()

    sem = pltpu.SemaphoreType.DMA
    return _sequencer(body, shards, [_sds((N_CHIPS,) + a.shape, a.dtype) for a in shards],
                      [sem((nw, 3)), sem((nw, 3)), sem((nw, 3)), sem((nw, 3)), sem((nw,))], GATHER_ID, name)


def _swap_other_halves(grads, *, name):
    nw = len(grads)

    def body(*refs):
        ins, outs = refs[:nw], refs[nw:2 * nw]
        send_sems, recv_sems = refs[2 * nw:]
        x, y, c = _place()
        _handshake([(x, y, 1 - c)])
        cps = []
        for w in range(nw):
            hw = grads[w].shape[1] // 2
            cp = pltpu.make_async_remote_copy(
                src_ref=ins[w].at[:, pl.ds((1 - c) * hw, hw)], dst_ref=outs[w], send_sem=send_sems.at[w],
                recv_sem=recv_sems.at[w], device_id=(x, y, 1 - c), device_id_type=MESH)
            cp.start()
            cps.append(cp)
        for cp in cps:
            cp.wait()

    sem = pltpu.SemaphoreType.DMA
    return _sequencer(body, grads, [_sds((N_CHIPS, g.shape[1] // 2, g.shape[2]), g.dtype) for g in grads],
                      [sem((nw,)), sem((nw,))], SWAP_ID, name)


def _add_my_halves(core, grads, others, *, name, after=()):
    nw = len(grads)
    halves = [g.shape[1] // 2 for g in grads]

    def body(core_ref, *refs):
        outs = refs[2 * nw + len(after):]
        for g_ref, o_ref, out_ref in zip(refs[:nw], refs[nw:2 * nw], outs):
            out_ref[...] = (g_ref[...].astype(F32) + o_ref[...].astype(F32)).astype(out_ref.dtype)

    in_specs = [pl.BlockSpec((None, None, hw, g.shape[2]), lambda s, core_ref: (s, core_ref[0], 0, 0))
                for g, hw in zip(grads, halves)]
    in_specs += [pl.BlockSpec((None, hw, g.shape[2]), lambda s, core_ref: (s, 0, 0)) for g, hw in zip(grads, halves)]
    return pl.pallas_call(
        body,
        grid_spec=pltpu.PrefetchScalarGridSpec(
            num_scalar_prefetch=1, grid=(N_CHIPS,), in_specs=in_specs + [HBM_OPERAND] * len(after),
            out_specs=[pl.BlockSpec((None, hw, g.shape[2]), lambda s, core_ref: (s, 0, 0))
                       for g, hw in zip(grads, halves)]),
        out_shape=[_sds((N_CHIPS, hw, g.shape[2]), BF16) for g, hw in zip(grads, halves)],
        compiler_params=_cp(("parallel",)), name=name)(
            core, *[g.reshape(N_CHIPS, 2, hw, g.shape[2]) for g, hw in zip(grads, halves)], *others, *after)


def _scatter_partials(parts, *, name):
    nw = len(parts)

    def body(*refs):
        ins, outs = refs[:nw], refs[nw:2 * nw]
        send_sems, recv_sems = refs[2 * nw:]
        x, y, c = _place()
        _handshake([(cx, cy, c) for cx, cy in _other_chips(x, y)])
        cps = []
        for w in range(nw):
            for j, (cx, cy) in enumerate(_other_chips(x, y)):
                cp = pltpu.make_async_remote_copy(
                    src_ref=ins[w].at[2 * cx + cy], dst_ref=outs[w].at[j], send_sem=send_sems.at[w, j],
                    recv_sem=recv_sems.at[w, j], device_id=(cx, cy, c), device_id_type=MESH)
                cp.start()
                cps.append(cp)
        for cp in cps:
            cp.wait()

    sem = pltpu.SemaphoreType.DMA
    return _sequencer(body, parts, [_sds((3,) + p.shape[1:], p.dtype) for p in parts],
                      [sem((nw, 3)), sem((nw, 3))], SCATTER_ID, name)


SUM_STEPS = 2


def _sum_partials(chip, parts, recvd, *, name, after=()):
    nw = len(parts)
    rows = [p.shape[1] // SUM_STEPS for p in parts]

    def body(chip_ref, *refs):
        outs = refs[2 * nw + len(after):]
        for p_ref, r_ref, out_ref in zip(refs[:nw], refs[nw:2 * nw], outs):
            acc = p_ref[...].astype(F32)
            for j in range(3):
                acc = acc + r_ref[j].astype(F32)
            out_ref[...] = acc

    in_specs = [pl.BlockSpec((None, th, p.shape[2]), lambda i, chip_ref: (chip_ref[0], i, 0))
                for p, th in zip(parts, rows)]
    in_specs += [pl.BlockSpec((3, th, p.shape[2]), lambda i, chip_ref: (0, i, 0)) for p, th in zip(parts, rows)]
    return pl.pallas_call(
        body,
        grid_spec=pltpu.PrefetchScalarGridSpec(
            num_scalar_prefetch=1, grid=(SUM_STEPS,), in_specs=in_specs + [HBM_OPERAND] * len(after),
            out_specs=[pl.BlockSpec((th, p.shape[2]), lambda i, chip_ref: (i, 0)) for p, th in zip(parts, rows)]),
        out_shape=[_sds(p.shape[1:]) for p in parts], compiler_params=_cp(("parallel",)), name=name)(
            chip, *parts, *recvd, *after)


def _swap_reduced_halves(halves, *, name):
    nw = len(halves)

    def body(*refs):
        ins, outs = refs[:nw], refs[nw:2 * nw]
        send_sems, recv_sems = refs[2 * nw:]
        x, y, c = _place()
        _handshake([(x, y, 1 - c)])
        cps = []
        for w in range(nw):
            cp = pltpu.make_async_remote_copy(
                src_ref=ins[w], dst_ref=outs[w], send_sem=send_sems.at[w], recv_sem=recv_sems.at[w],
                device_id=(x, y, 1 - c), device_id_type=MESH)
            cp.start()
            cps.append(cp)
        for cp in cps:
            cp.wait()

    sem = pltpu.SemaphoreType.DMA
    return _sequencer(body, halves, [_sds(h.shape, h.dtype) for h in halves], [sem((nw,)), sem((nw,))], JOIN_ID, name)


def _exchange_rows(vec, *, name):
    def body(v_ref, slots, send_sems, recv_sems, local_sem):
        x, y, c = _place()
        me = 4 * x + 2 * y + c
        peers = []
        for mask in range(1, N_DEV):
            peers.append((1 - x if mask & 4 else x, 1 - y if mask & 2 else y, 1 - c if mask & 1 else c))
        _handshake(peers)
        own = pltpu.make_async_copy(v_ref, slots.at[me], local_sem)
        own.start()
        cps = []
        for k, peer in enumerate(peers):
            cp = pltpu.make_async_remote_copy(
                src_ref=v_ref, dst_ref=slots.at[me], send_sem=send_sems.at[k], recv_sem=recv_sems.at[k],
                device_id=peer, device_id_type=MESH)
            cp.start()
            cps.append(cp)
        for k, (px, py, pc) in enumerate(peers):
            pltpu.make_async_remote_copy(
                src_ref=v_ref, dst_ref=slots.at[4 * px + 2 * py + pc], send_sem=send_sems.at[k],
                recv_sem=recv_sems.at[k], device_id=(px, py, pc), device_id_type=MESH).wait_recv()
        for cp in cps:
            cp.wait_send()
        own.wait()

    sem = pltpu.SemaphoreType.DMA
    return _sequencer(body, [vec], [_sds((N_DEV,) + vec.shape)], [sem((N_DEV - 1,)), sem((N_DEV - 1,)), sem(())],
                      EXCHANGE_ID, name)[0]


def _sum_slots(slots, *, name, after=()):
    def body(s_ref, *rest):
        out_ref = rest[len(after)]
        acc = s_ref[0]
        for d in range(1, N_DEV):
            acc = acc + s_ref[d]
        out_ref[...] = acc

    vmem = pl.BlockSpec(memory_space=pltpu.VMEM)
    return pl.pallas_call(
        body, in_specs=[vmem] + [HBM_OPERAND] * len(after), out_specs=vmem, out_shape=_sds(slots.shape[1:]),
        compiler_params=pltpu.CompilerParams(vmem_limit_bytes=VMEM_LIMIT_BYTES), name=name)(slots, *after)


def _reduce_scatter_start(grads, core, *, tag, add_after=()):
    others = _swap_other_halves(grads, name="swap_other_halves_" + tag)
    parts = _add_my_halves(core, grads, others, name="add_my_halves_" + tag, after=add_after)
    return parts, _scatter_partials(parts, name="scatter_partials_" + tag)


def _reduce_scatter_finish(parts, recvd, chip, *, tag, sum_after=()):
    mine = _sum_partials(chip, parts, recvd, name="sum_partials_" + tag, after=sum_after)
    return mine, _swap_reduced_halves(mine, name="swap_reduced_halves_" + tag)


ADAM_BLOCK_ELEMS = 256 * 1024


def _adam_rows(rows, cols):
    tm = rows
    while tm * cols > ADAM_BLOCK_ELEMS and tm % 16 == 0:
        tm //= 2
    return tm


def _adam_step(wv, gv, mv, vv):
    m2 = ADAM_B1 * mv + (1.0 - ADAM_B1) * gv
    v2 = ADAM_B2 * vv + (1.0 - ADAM_B2) * (gv * gv)
    m_hat = m2 / (1.0 - ADAM_B1 ** ADAM_STEP)
    v_hat = v2 / (1.0 - ADAM_B2 ** ADAM_STEP)
    return -ADAM_LR * (m_hat / (jnp.sqrt(v_hat) + ADAM_EPS) + ADAM_WD * wv), m2, v2


def _adamw_each(ws, gs, ms, vs, *, name, after=()):
    n = len(ws)
    whole = pl.BlockSpec(memory_space=pltpu.VMEM)

    def body(*refs):
        ins, outs = refs[:4 * n], refs[4 * n + len(after):]
        for i in range(n):
            res = _adam_step(*(ins[k * n + i][...] for k in range(4)))
            for k in range(3):
                outs[k * n + i][...] = res[k]

    out = pl.pallas_call(body, in_specs=[whole] * (4 * n) + [HBM_OPERAND] * len(after),
                         out_shape=[_sds(w.shape) for w in ws] * 3, name=name)(*ws, *gs, *ms, *vs, *after)
    return out[:n], out[n:2 * n], out[2 * n:]


def _adamw_halves(core, w, g_mine, g_theirs, m, v, *, name, after=()):
    rows, cols = w.shape
    hw = rows // 2
    tm = _adam_rows(hw, cols)
    per_half = hw // tm

    def body(core_ref, w_ref, gm_ref, gt_ref, m_ref, v_ref, *rest):
        g_out, d_out, m_out, v_out = rest[len(after):]
        mine = (pl.program_id(0) // per_half) == core_ref[0]
        g = jnp.where(mine, gm_ref[...], gt_ref[...])
        d, m2, v2 = _adam_step(w_ref[...], g, m_ref[...], v_ref[...])
        g_out[...] = g
        d_out[...] = d
        m_out[...] = m2
        v_out[...] = v2

    full = pl.BlockSpec((tm, cols), lambda i, core_ref: (i, 0))

    def half(wanted):
        def index(i, core_ref):
            in_use = ((i // per_half) == core_ref[0]) == wanted
            return (jnp.where(in_use, i % per_half, 0), 0)
        return pl.BlockSpec((tm, cols), index)

    return pl.pallas_call(
        body,
        grid_spec=pltpu.PrefetchScalarGridSpec(
            num_scalar_prefetch=1, grid=(rows // tm,),
            in_specs=[full, half(True), half(False), full, full] + [HBM_OPERAND] * len(after),
            out_specs=[full, full, full, full]),
        out_shape=[_sds((rows, cols))] * 4, compiler_params=_cp(("parallel",)), name=name)(
            core, w, g_mine, g_theirs, m, v, *after)


HELD_TRANSPOSED = ("w_ff_gate", "w_ff_up")


def _as_rows(name, arr):
    return arr[0].T if name in HELD_TRANSPOSED else arr[0]


def _from_rows(name, arr2d):
    return (arr2d.T if name in HELD_TRANSPOSED else arr2d)[None]


STORED_SWAPPED = ("ssm_b_re", "ssm_b_im")


def _as_stored(name, arr):
    return jnp.swapaxes(arr, -1, -2) if name in STORED_SWAPPED else arr


def _pack_rows(arrs):
    flat = jnp.concatenate([a.reshape(-1).astype(F32) for a in arrs])
    rows = -(-flat.shape[0] // 1024) * 8
    return jnp.pad(flat, (0, rows * 128 - flat.shape[0])).reshape(rows, 128)


def _unpack_rows(vec, shapes):
    flat = vec.reshape(-1)
    out, off = [], 0
    for shp in shapes:
        size = math.prod(shp)
        out.append(flat[off:off + size].reshape(shp))
        off += size
    return out


SMALL = ("b_gate", "ssm_a_re", "ssm_a_im", "ssm_log_dt", "ssm_b_re", "ssm_b_im", "ssm_c_re", "ssm_c_im", "ssm_d",
         "ln1_g", "ln1_b", "ln2_g", "ln2_b")
GATHER_GROUPS = (("w_in", ("w_in",)), ("mixer", ("w_attn_br", "w_ssm_br", "w_glu", "w_out")),
                 ("ffn_up", ("w_ff_gate", "w_ff_up")), ("ffn_down", ("w_ff_down",)))
REDUCE_GROUPS = (("ffn", ("w_ff_down", "w_ff_gate", "w_ff_up")),
                 ("mixer", ("w_out", "w_ssm_br", "w_glu", "w_attn_br")), ("w_in", ("w_in",)))
WEIGHTS = ("w_in", "b_gate", "w_attn_br", "w_ssm_br", "w_out", "ssm_a_re", "ssm_a_im", "ssm_log_dt", "ssm_b_re",
           "ssm_b_im", "ssm_c_re", "ssm_c_im", "ssm_d", "w_glu", "ln1_g", "ln1_b", "w_ff_gate", "w_ff_up", "w_ff_down",
           "ln2_g", "ln2_b")


def kernel(x, w_in, b_gate, w_attn_br, w_ssm_br, w_out, ssm_a_re, ssm_a_im, ssm_log_dt, ssm_b_re, ssm_b_im, ssm_c_re, ssm_c_im, ssm_d, w_glu, ln1_g, ln1_b, w_ff_gate, w_ff_up, w_ff_down, ln2_g, ln2_b, loss_target, m_w_in, m_b_gate, m_w_attn_br, m_w_ssm_br, m_w_out, m_ssm_a_re, m_ssm_a_im, m_ssm_log_dt, m_ssm_b_re, m_ssm_b_im, m_ssm_c_re, m_ssm_c_im, m_ssm_d, m_w_glu, m_ln1_g, m_ln1_b, m_w_ff_gate, m_w_ff_up, m_w_ff_down, m_ln2_g, m_ln2_b, v_w_in, v_b_gate, v_w_attn_br, v_w_ssm_br, v_w_out, v_ssm_a_re, v_ssm_a_im, v_ssm_log_dt, v_ssm_b_re, v_ssm_b_im, v_ssm_c_re, v_ssm_c_im, v_ssm_d, v_w_glu, v_ln1_g, v_ln1_b, v_w_ff_gate, v_w_ff_up, v_w_ff_down, v_ln2_g, v_ln2_b):
    given = dict(locals())
    px, py, pc = _place()
    chip = 2 * px + py
    core_s = jnp.reshape(pc, (1,)).astype(jnp.int32)
    chip_s = jnp.reshape(chip, (1,)).astype(jnp.int32)

    wts = {}
    for tag, names in GATHER_GROUPS:
        shards = [_as_rows(n, given[n]).astype(BF16) for n in names]
        first = tag == GATHER_GROUPS[0][0]
        slots = _gather_weights(shards, name="gather_" + tag, own_slot=not first)
        if first:
            wts["w_in_parts"] = (chip_s, shards[0], slots[0])
        else:
            wts.update(zip(names, slots))
    ncol = D_MODEL // N_CHIPS
    bg_mine = jnp.where(pc == 0, b_gate[0], jnp.zeros_like(b_gate[0]))
    bg_full = lax.dynamic_update_slice(jnp.zeros((2, D_MODEL), F32), bg_mine, (0, chip * ncol))
    bg_slots = _exchange_rows(bg_full.reshape(16, 128), name="exchange_gate_bias")
    bg_full = _sum_slots(bg_slots, name="sum_gate_bias").reshape(2, D_MODEL)
    small = {n: given[n][0] for n in SMALL if n.startswith("ssm")}
    small.update({n: given[n] for n in ("ln1_g", "ln1_b", "ln2_g", "ln2_b")})
    small["b_gate"] = bg_full

    groups = dict(REDUCE_GROUPS)
    parts, recvd, reduced, sent = {}, {}, {}, {}
    grads, delta, new_m, new_v, done = {}, {}, {}, {}, {}

    def start(tag, big_g, add_after):
        parts[tag], recvd[tag] = _reduce_scatter_start([big_g[n] for n in groups[tag]], core_s, tag=tag,
                                                       add_after=add_after)
        return parts[tag]

    def reduce_sum(tag, after):
        reduced[tag] = _reduce_scatter_finish(parts[tag], recvd[tag], chip_s, tag=tag, sum_after=after)
        return reduced[tag][0]

    def adam(tag, after):
        for n, g_mine, g_theirs in zip(groups[tag], *reduced[tag]):
            res = _adamw_halves(core_s, _as_rows(n, given[n]), g_mine, g_theirs, _as_rows(n, given["m_" + n]),
                                _as_rows(n, given["v_" + n]), name="adamw_" + n, after=after)
            done[n] = res[1]
            grads[n], delta[n], new_m[n], new_v[n] = [_from_rows(n, r) for r in res]

    def ffn_grads(big_g, norm_bwd):
        return start("ffn", big_g, (norm_bwd,))

    def mixer_grads(big_g, scan_bwd):
        return start("mixer", big_g, (scan_bwd, *reduce_sum("ffn", (scan_bwd,))))

    def small_grads(small_g, loss_mine):
        sent["stored"] = [_as_stored(n, small_g[n]) for n in SMALL] + [loss_mine.reshape(1)]
        packed = _pack_rows(sent["stored"])
        sent["slots"] = _exchange_rows(packed, name="exchange_small")
        return (packed,)

    grad_x_after, g_w_in, attention_bwd = _local_step(x[0], loss_target[0], wts, small,
                                                      ffn_grads, mixer_grads, small_grads)

    reduce_sum("mixer", (attention_bwd,))
    summed = _sum_slots(sent["slots"], name="sum_small", after=(g_w_in,))
    adam("mixer", (g_w_in,))
    start("w_in", {"w_in": g_w_in}, (summed, *[done[n] for n in groups["mixer"]]))
    in_flight = (parts["w_in"][0],)
    grad_x = grad_x_after(in_flight)
    adam("ffn", in_flight)
    summed = _unpack_rows(summed, [a.shape for a in sent["stored"]])
    loss = summed.pop()[0]
    at = SMALL.index("b_gate")
    summed[at] = lax.dynamic_slice(summed[at], (0, chip * ncol), (2, ncol))
    summed = [g.reshape(1, -1) if g.ndim == 1 else g for g in summed]
    held = [[_as_stored(n, given[prefix + n]).reshape(g.shape) for n, g in zip(SMALL, summed)]
            for prefix in ("", "m_", "v_")]
    small_out = _adamw_each(held[0], summed, held[1], held[2], name="adamw_small", after=in_flight)
    for out, arrs in zip((grads, delta, new_m, new_v), (summed, *small_out)):
        out.update((n, _as_stored(n, a).reshape(given[n].shape)) for n, a in zip(SMALL, arrs))
    reduce_sum("w_in", (*[done[n] for n in groups["ffn"]], small_out[0][0], grad_x))
    adam("w_in", ())

    return (loss, grad_x.reshape(x.shape), *[grads[n] for n in WEIGHTS], *[delta[n] for n in WEIGHTS],
            *[new_m[n] for n in WEIGHTS], *[new_v[n] for n in WEIGHTS])
```

```python
import math

import jax
import jax.numpy as jnp
from jax import lax
from jax.experimental import pallas as pl
from jax.experimental.pallas import tpu as pltpu
from jax.experimental.pallas import tpu_sc as plsc

F32 = jnp.float32
BF16 = jnp.bfloat16
MESH = pl.DeviceIdType.MESH

D_MODEL = 1024
SEQ = 2048
HEAD_DIM = 64
ATTN_HEADS = 8
DILATIONS = (1, 4, 16)
ATTN_WIDTH = ATTN_HEADS * HEAD_DIM
QKV_WIDTH = 3 * ATTN_WIDTH
BLOCK = 128
ROPE_THETA = 10000.0
NEG_INF = -1e30
SSM_GROUP = 16
SSM_GROUPS = 32
SSM_WIDTH = 512
SSM_STATE = 64
SSM_LANES = SSM_GROUPS * SSM_STATE
SCAN_CHUNKS = 8
SCAN_STEPS = SEQ // SCAN_CHUNKS
IN_WIDTH = 3 * QKV_WIDTH + SSM_WIDTH + 2 * D_MODEL
D_FF = 2816
N_CHIPS = 4
N_DEV = 8
DN_ALPHA = 2.0 ** 0.25
LN_EPS = 1e-5
ADAM_LR = 0.001
ADAM_B1 = 0.9
ADAM_B2 = 0.999
ADAM_EPS = 1e-08
ADAM_WD = 0.01
ADAM_STEP = 10
GELU_C = math.sqrt(2.0 / math.pi)
GELU_K = 0.044715

VMEM_LIMIT_BYTES = 56 * 1024 * 1024


def _sds(shape, dtype=F32):
    return jax.ShapeDtypeStruct(tuple(shape), dtype)


def _cp(semantics=None):
    return pltpu.CompilerParams(dimension_semantics=semantics, vmem_limit_bytes=VMEM_LIMIT_BYTES)


HBM_OPERAND = pl.BlockSpec(memory_space=pl.ANY)


def _matmul(a, b, *, grid, a_spec, b_spec, o_spec, out_shape, dims, k_axis=None, name, after=()):
    nk = grid[k_axis] if k_axis is not None else 1
    o_block = tuple(d for d in o_spec.block_shape if d is not None)
    n_after = len(after)

    def body(a_ref, b_ref, *rest):
        o_ref, acc = rest[n_after], rest[n_after + 1:]
        part = lax.dot_general(a_ref[...].astype(BF16), b_ref[...].astype(BF16),
                               (((dims[0],), (dims[1],)), ((), ())), preferred_element_type=F32)
        if k_axis is None:
            o_ref[...] = part.astype(o_ref.dtype)
        else:
            k = pl.program_id(k_axis)

            @pl.when(k == 0)
            def _():
                acc[0][...] = part

            @pl.when(k > 0)
            def _():
                acc[0][...] += part

            @pl.when(k == nk - 1)
            def _():
                o_ref[...] = acc[0][...].astype(o_ref.dtype)

    sem = tuple("arbitrary" if ax == k_axis else "parallel" for ax in range(len(grid)))
    return pl.pallas_call(
        body, grid=grid, in_specs=[a_spec, b_spec] + [HBM_OPERAND] * n_after, out_specs=o_spec, out_shape=out_shape,
        scratch_shapes=[pltpu.VMEM(o_block, F32)] if k_axis is not None else [],
        compiler_params=_cp(sem), name=name)(a, b, *after)


def _grad_layer_input(chip, dy, wg, w_own, dz, *, tm, name, after=()):
    k, ns = wg.shape[1], wg.shape[2]
    m = dy.shape[0]

    def body(chip_ref, dy_ref, w_ref, own_ref, dz_ref, *rest):
        o_ref, acc = rest[len(after)], rest[len(after) + 1]
        s = pl.program_id(1)
        mine = s == chip_ref[0]

        @pl.when(s == 0)
        def _():
            acc[...] = DN_ALPHA * dz_ref[...]

        def add(w_block):
            acc[...] += lax.dot_general(dy_ref[...], w_block[...], (((1,), (1,)), ((), ())),
                                        preferred_element_type=F32)

        pl.when(mine)(lambda: add(own_ref))
        pl.when(jnp.logical_not(mine))(lambda: add(w_ref))

        @pl.when(s == N_CHIPS - 1)
        def _():
            o_ref[...] = acc[...]

    def gathered(i, s, chip_ref):
        neighbour = jnp.where(s == N_CHIPS - 1, s - 1, s + 1)
        return (jnp.where(s == chip_ref[0], neighbour, s), 0, 0)

    rows =pl.BlockSpec((tm, k), lambda i, s, chip_ref: (i, 0))
    return pl.pallas_call(
        body,
        grid_spec=pltpu.PrefetchScalarGridSpec(
            num_scalar_prefetch=1, grid=(m // tm, N_CHIPS),
            in_specs=[pl.BlockSpec((tm, ns), lambda i, s, chip_ref: (i, s)), pl.BlockSpec((None, k, ns), gathered),
                      pl.BlockSpec((k, ns), lambda i, s, chip_ref: (0, 0)), rows] + [HBM_OPERAND] * len(after),
            out_specs=rows, scratch_shapes=[pltpu.VMEM((tm, k), F32)]),
        out_shape=_sds((m, k)), compiler_params=_cp(("parallel", "arbitrary")), name=name)(
            chip, dy, wg, w_own, dz, *after)


def _mm_cols_tn(a, dy, *, ns, name, after=()):
    m, k = a.shape
    return _matmul(a, dy, grid=(N_CHIPS,), a_spec=pl.BlockSpec((m, k), lambda s: (0, 0)),
                   b_spec=pl.BlockSpec((m, ns), lambda s: (0, s)),
                   o_spec=pl.BlockSpec((None, k, ns), lambda s: (s, 0, 0)),
                   out_shape=_sds((N_CHIPS, k, ns), BF16), dims=(0, 0), name=name, after=after)


def _mm_rows_tn(a, dy, *, name):
    m, k = a.shape
    rows, n = k // N_CHIPS, dy.shape[1]
    return _matmul(a, dy, grid=(N_CHIPS,), a_spec=pl.BlockSpec((m, rows), lambda s: (0, s)),
                   b_spec=pl.BlockSpec((m, n), lambda s: (0, 0)),
                   o_spec=pl.BlockSpec((None, rows, n), lambda s: (s, 0, 0)),
                   out_shape=_sds((N_CHIPS, rows, n), BF16), dims=(0, 0), name=name)


def _rowwise(fn, tiled, full, outs, accs=(), *, tm, name, after=()):
    args, in_specs = [], []
    for t in tiled:
        if isinstance(t, tuple):
            arr, w, cb = t
            in_specs.append(pl.BlockSpec((tm, w), lambda i, cb=cb: (i, cb)))
        else:
            arr = t
            in_specs.append(pl.BlockSpec((tm, arr.shape[1]), lambda i: (i, 0)))
        args.append(arr)
    rows = args[0].shape[0]
    for f in full:
        in_specs.append(pl.BlockSpec(f.shape, lambda i, nd=f.ndim: (0,) * nd))
        args.append(f)
    out_specs = [pl.BlockSpec((tm, o.shape[1]), lambda i: (i, 0)) for o in outs]
    out_specs += [pl.BlockSpec(a.shape, lambda i, nd=len(a.shape): (0,) * nd) for a in accs]
    n_in, n_out = len(args), len(outs)
    in_specs += [HBM_OPERAND] * len(after)
    first_out = n_in + len(after)

    def body(*refs):
        res = fn(*[r[...] for r in refs[:n_in]])
        res = res if isinstance(res, (tuple, list)) else (res,)
        for r, v in zip(refs[first_out:first_out + n_out], res[:n_out]):
            r[...] = v.astype(r.dtype)
        i = pl.program_id(0)
        for r, v in zip(refs[first_out + n_out:], res[n_out:]):
            @pl.when(i == 0)
            def _(r=r, v=v):
                r[...] = v

            @pl.when(i > 0)
            def _(r=r, v=v):
                r[...] += v

    res = pl.pallas_call(
        body, grid=(rows // tm,), in_specs=in_specs, out_specs=out_specs, out_shape=list(outs) + list(accs),
        compiler_params=_cp(("arbitrary",) if accs else ("parallel",)), name=name)(*args, *after)
    return res


def _colsum(v):
    return jnp.sum(v, axis=0, keepdims=True)


def _ln_stats(z):
    mu = jnp.mean(z, axis=-1, keepdims=True)
    zc = z - mu
    var = jnp.mean(zc * zc, axis=-1, keepdims=True)
    rstd = lax.rsqrt(var + LN_EPS)
    return zc * rstd, rstd


def _ln_bwd(dy, xhat, rstd, g):
    dxh = dy * g
    m1 = jnp.mean(dxh, axis=-1, keepdims=True)
    m2 = jnp.mean(dxh * xhat, axis=-1, keepdims=True)
    return rstd * (dxh - m1 - xhat * m2)


def _swap_halves(t):
    w = t.shape[-1]
    lane = lax.broadcasted_iota(jnp.int32, t.shape, t.ndim - 1)
    return jnp.where((lane % HEAD_DIM) < HEAD_DIM // 2, pltpu.roll(t, w - HEAD_DIM // 2, t.ndim - 1),
                     pltpu.roll(t, HEAD_DIM // 2, t.ndim - 1))


PHASES = max(DILATIONS)
PAIR = 2 * HEAD_DIM
UNITS = SEQ // BLOCK
UNIT_BATCH = 16
ROPE_ROWS = 256
TAIL_COLS = 256


def _to_phase_rows(t):
    return t.reshape(SEQ // PHASES, PHASES, t.shape[1]).transpose(1, 0, 2).reshape(t.shape)


def _reorder_rows(arr, *, to_phase, name):
    def body(a_ref, o_ref):
        for rho in range(PHASES):
            phase = pl.ds(rho * BLOCK, BLOCK)
            strided = pl.ds(rho, BLOCK, stride=PHASES)
            src, dst = (strided, phase) if to_phase else (phase, strided)
            o_ref[dst, :] = a_ref[src, :]

    spec = pl.BlockSpec((SEQ, BLOCK), lambda j: (0, j))
    return pl.pallas_call(body, grid=(arr.shape[1] // BLOCK,), in_specs=[spec], out_specs=spec,
                          out_shape=_sds(arr.shape), compiler_params=_cp(("parallel",)), name=name)(arr)


def _rope(t, cf, ss):
    return t * cf + _swap_halves(t) * ss


def _rope_transposed(d, cf, ss):
    return d * cf + _swap_halves(d * ss)


def _unit_pieces(u, dil):
    pieces, length = PHASES // dil, 8 * dil
    if dil == 1:
        rho, i = 0, u
    elif dil == PHASES:
        rho, i = u, 0
    else:
        rho, i = jnp.bitwise_and(u, dil - 1), jnp.right_shift(u, dil.bit_length() - 1)
    before = jnp.maximum(i - 1, 0)
    cur = [pl.multiple_of((rho + dil * k) * BLOCK + length * i, 8) for k in range(pieces)]
    prev = [pl.multiple_of((rho + dil * k) * BLOCK + length * before, 8) for k in range(pieces)]
    return i, cur, prev


def _load_tile(ref, starts, dil):
    return jnp.concatenate([ref[pl.ds(st, 8 * dil), :] for st in starts], axis=0)


def _store_tile(ref, starts, dil, val, head=None, accumulate=False):
    length = 8 * dil
    lanes = slice(None) if head is None else pl.ds(head * HEAD_DIM, HEAD_DIM)
    cols = slice(None) if head is None else slice(head * HEAD_DIM, (head + 1) * HEAD_DIM)
    for k, st in enumerate(starts):
        piece = val[k * length:(k + 1) * length, cols]
        if accumulate:
            ref[pl.ds(st, length), lanes] += piece
        else:
            ref[pl.ds(st, length), lanes] = piece


def _tile_position(idx, dil):
    pieces, length = PHASES // dil, 8 * dil
    return pieces * jnp.bitwise_and(idx, length - 1) + jnp.right_shift(idx, length.bit_length() - 1)


def _band_mask(i, dil):
    row = lax.broadcasted_iota(jnp.int32, (BLOCK, 2 * BLOCK), 0)
    col = lax.broadcasted_iota(jnp.int32, (BLOCK, 2 * BLOCK), 1)
    key_pos = _tile_position(jnp.bitwise_and(col, BLOCK - 1), dil) + jnp.where(col >= BLOCK, 0, -BLOCK)
    dist = _tile_position(row, dil) - key_pos
    return (dist >= 0) & (dist <= BLOCK) & ((col >= BLOCK) | (i > 0))


def _causal_mask():
    row = lax.broadcasted_iota(jnp.int32, (BLOCK, BLOCK), 0)
    col = lax.broadcasted_iota(jnp.int32, (BLOCK, BLOCK), 1)
    return row >= col


def _pair_views(col0):
    return [pl.BlockSpec((SEQ, PAIR), lambda hp, g=g: (0, col0 // PAIR + g * (ATTN_WIDTH // PAIR) + hp))
            for g in range(len(DILATIONS))]


def _project_shard(shard, x_ref, w_ref, cf_ref, ss_ref, o_ref):
    ns = w_ref.shape[1]
    tiles = ns // PAIR
    xb = x_ref[...].astype(BF16)
    cf, ss = cf_ref[...], ss_ref[...]

    def write(rotated, scaled):
        for t0 in range(0, tiles, 2):
            strip = jnp.dot(xb, w_ref[:, t0 * PAIR:(t0 + 2) * PAIR], preferred_element_type=F32)
            for t in (t0, t0 + 1):
                val = strip[:, (t - t0) * PAIR:(t - t0 + 1) * PAIR]
                if t < rotated:
                    val = _rope(val, cf, ss)
                    if t < scaled:
                        val = val * (1.0 / math.sqrt(HEAD_DIM))
                o_ref[:, t * PAIR:(t + 1) * PAIR] = val

    for s in range(N_CHIPS):
        rotated = min(max(2 * QKV_WIDTH - s * ns, 0), ns) // PAIR
        scaled = min(max(QKV_WIDTH - s * ns, 0), ns) // PAIR
        @pl.when(shard == s)
        def _(rotated=rotated, scaled=scaled):
            write(rotated, scaled)


def _project_in_own(chip, x, w_own, cos_f, sin_s):
    ns = w_own.shape[1]

    def body(chip_ref, x_ref, w_ref, cf_ref, ss_ref, o_ref):
        _project_shard(chip_ref[0], x_ref, w_ref, cf_ref, ss_ref, o_ref)

    table = pl.BlockSpec((FF_ROWS, PAIR), lambda i, chip_ref: (i, 0))
    return pl.pallas_call(
        body,
        grid_spec=pltpu.PrefetchScalarGridSpec(
            num_scalar_prefetch=1, grid=(SEQ // FF_ROWS,),
            in_specs=[pl.BlockSpec((FF_ROWS, D_MODEL), lambda i, chip_ref: (i, 0)),
                      pl.BlockSpec((D_MODEL, ns), lambda i, chip_ref: (0, 0)), table, table],
            out_specs=pl.BlockSpec((FF_ROWS, ns), lambda i, chip_ref: (i, chip_ref[0]))),
        out_shape=_sds((SEQ, N_CHIPS * ns)), compiler_params=_cp(("parallel",)), name="project_in_own")(
            chip, x, w_own, cos_f, sin_s)


def _project_in(chip, x, wg, cos_f, sin_s, started):
    ns = wg.shape[2]

    def other(j, chip_ref):
        return (chip_ref[0] + 1 + j) % N_CHIPS

    def body(chip_ref, x_ref, w_ref, cf_ref, ss_ref, started_ref, o_ref):
        _project_shard(other(pl.program_id(1), chip_ref), x_ref, w_ref, cf_ref, ss_ref, o_ref)

    table = pl.BlockSpec((FF_ROWS, PAIR), lambda i, j, chip_ref: (i, 0))
    return pl.pallas_call(
        body,
        grid_spec=pltpu.PrefetchScalarGridSpec(
            num_scalar_prefetch=1, grid=(SEQ // FF_ROWS, N_CHIPS - 1),
            in_specs=[pl.BlockSpec((FF_ROWS, D_MODEL), lambda i, j, chip_ref: (i, 0)),
                      pl.BlockSpec((None, D_MODEL, ns), lambda i, j, chip_ref: (other(j, chip_ref), 0, 0)),
                      table, table, HBM_OPERAND],
            out_specs=pl.BlockSpec((FF_ROWS, ns), lambda i, j, chip_ref: (i, other(j, chip_ref)))),
        out_shape=_sds((SEQ, N_CHIPS * ns)), input_output_aliases={5: 0},
        compiler_params=_cp(("parallel", "parallel")), name="project_in")(chip, x, wg, cos_f, sin_s, started)


def _attention_fwd(proj):
    ng = len(DILATIONS)

    def body(*refs):
        q_refs, k_refs, v_refs = refs[:ng], refs[ng:2 * ng], refs[2 * ng:3 * ng]
        attn_ref, lse_ref = refs[3 * ng:]
        qr_refs, kr_refs = q_refs, k_refs
        first = lax.broadcasted_iota(jnp.int32, (BLOCK, PAIR), 1) < HEAD_DIM
        for g, dil in enumerate(DILATIONS):
            two_blocks = SEQ // dil > BLOCK

            def units(t, carry, g=g, dil=dil, two_blocks=two_blocks):
                picked = [_unit_pieces(t * UNIT_BATCH + j, dil) for j in range(UNIT_BATCH)]

                def tiles(ref, with_prev=False):
                    if with_prev and two_blocks:
                        return jnp.stack([jnp.concatenate([_load_tile(ref, prev, dil), _load_tile(ref, rows, dil)],
                                                          axis=0) for _, rows, prev in picked])
                    return jnp.stack([_load_tile(ref, rows, dil) for _, rows, _ in picked])

                qq = tiles(qr_refs[g]).astype(BF16)
                kk = tiles(kr_refs[g], True).astype(BF16)
                vv = tiles(v_refs[g], True).astype(BF16)
                if two_blocks:
                    valid = jnp.stack([_band_mask(i, dil) for i, _, _ in picked])
                else:
                    valid = _causal_mask()[None]
                mine = first[None]
                zero = jnp.zeros_like(qq)
                outs, lses = [], []
                for qh in (jnp.where(mine, qq, zero), jnp.where(mine, zero, qq)):
                    s = jnp.einsum("pqd,pkd->pqk", qh, kk, preferred_element_type=F32)
                    s = jnp.where(valid, s, NEG_INF)
                    m = jnp.max(s, axis=-1, keepdims=True)
                    p = jnp.exp(s - m)
                    l = jnp.sum(p, axis=-1, keepdims=True)
                    outs.append(jnp.einsum("pqk,pkd->pqd", p.astype(BF16), vv, preferred_element_type=F32) * (1.0 / l))
                    lses.append(m + jnp.log(l))
                o = jnp.where(mine, outs[0], outs[1])
                lse = jnp.where(mine, lses[0], lses[1])
                if g > 0:
                    lse_old = tiles(lse_ref)
                    m = jnp.maximum(lse_old, lse)
                    lse_new = m + jnp.log(jnp.exp(lse_old - m) + jnp.exp(lse - m))
                    o = tiles(attn_ref) * jnp.exp(lse_old - lse_new) + o * jnp.exp(lse - lse_new)
                    lse = lse_new
                for j, (_, rows, _) in enumerate(picked):
                    _store_tile(attn_ref, rows, dil, o[j])
                    _store_tile(lse_ref, rows, dil, lse[j])
                return carry

            lax.fori_loop(0, UNITS // UNIT_BATCH, units, 0)

    out = pl.BlockSpec((SEQ, PAIR), lambda hp: (0, hp))
    return pl.pallas_call(
        body, grid=(ATTN_WIDTH // PAIR,),
        in_specs=_pair_views(0) + _pair_views(QKV_WIDTH) + _pair_views(2 * QKV_WIDTH),
        out_specs=[out, out], out_shape=[_sds((SEQ, ATTN_WIDTH)), _sds((SEQ, ATTN_WIDTH))],
        compiler_params=_cp(("parallel",)), name="attention_fwd")(*([proj] * (3 * ng)))


def _attention_bwd(proj, cos_f, sin_s, d_attn, attn, lse, d_u, d_gl):
    pairs = ATTN_WIDTH // PAIR
    last = len(DILATIONS) * pairs - 1

    def accumulate(dil, qr_ref, kr_ref, v_ref, do_ref, o_ref, lse_ref, dq_acc, dk_acc, dv_acc):
        two_blocks = SEQ // dil > BLOCK
        dk_acc[...] = jnp.zeros_like(dk_acc)
        dv_acc[...] = jnp.zeros_like(dv_acc)
        nk = 2 * BLOCK if two_blocks else BLOCK
        first = lax.broadcasted_iota(jnp.int32, (BLOCK, PAIR), 1) < HEAD_DIM
        first_k = lax.broadcasted_iota(jnp.int32, (nk, PAIR), 1) < HEAD_DIM

        def units(t, carry):
            picked = [_unit_pieces(t * UNIT_BATCH + j, dil) for j in range(UNIT_BATCH)]

            def tiles(ref, with_prev=False):
                if with_prev and two_blocks:
                    return jnp.stack([jnp.concatenate([_load_tile(ref, prev, dil), _load_tile(ref, rows, dil)], axis=0)
                                      for _, rows, prev in picked])
                return jnp.stack([_load_tile(ref, rows, dil) for _, rows, _ in picked])

            qq = tiles(qr_ref).astype(BF16)
            kk = tiles(kr_ref, True).astype(BF16)
            vv = tiles(v_ref, True).astype(BF16)
            dof = tiles(do_ref)
            dd = dof * tiles(o_ref)
            lse3 = tiles(lse_ref)
            dob = dof.astype(BF16)
            if two_blocks:
                valid = jnp.stack([_band_mask(i, dil) for i, _, _ in picked])
            else:
                valid = _causal_mask()[None]
            zq, zf = jnp.zeros_like(qq), jnp.zeros_like(dd)
            dqs, dks, dvs = [], [], []
            for head in range(2):
                mine = first[None] if head == 0 else jnp.logical_not(first)[None]
                delta = jnp.sum(jnp.where(mine, dd, zf), axis=-1, keepdims=True)
                lse_h = lse3[:, :, head * HEAD_DIM:head * HEAD_DIM + 1]
                s = jnp.einsum("pqd,pkd->pqk", jnp.where(mine, qq, zq), kk, preferred_element_type=F32)
                p = jnp.where(valid, jnp.exp(s - lse_h), 0.0)
                dp = jnp.einsum("pqd,pkd->pqk", jnp.where(mine, dob, zq), vv, preferred_element_type=F32)
                ds = (p * (dp - delta)).astype(BF16)
                dqs.append(jnp.einsum("pqk,pkd->pqd", ds, kk, preferred_element_type=F32))
                dks.append(jnp.einsum("pqk,pqd->pkd", ds, qq, preferred_element_type=F32))
                dvs.append(jnp.einsum("pqk,pqd->pkd", p.astype(BF16), dob, preferred_element_type=F32))
            dq = jnp.where(first[None], dqs[0], dqs[1])
            dk = jnp.where(first_k[None], dks[0], dks[1])
            dv = jnp.where(first_k[None], dvs[0], dvs[1])
            for j, (_, rows, prev) in enumerate(picked):
                _store_tile(dq_acc, rows, dil, dq[j])
                _store_tile(dk_acc, rows, dil, dk[j, nk - BLOCK:], accumulate=True)
                _store_tile(dv_acc, rows, dil, dv[j, nk - BLOCK:], accumulate=True)
                if two_blocks:
                    _store_tile(dk_acc, prev, dil, dk[j, :BLOCK], accumulate=True)
                    _store_tile(dv_acc, prev, dil, dv[j, :BLOCK], accumulate=True)
            return carry

        lax.fori_loop(0, UNITS // UNIT_BATCH, units, 0)

    def body(qr_ref, kr_ref, v_ref, cf_ref, ss_ref, do_ref, o_ref, lse_ref, du_ref, dgl_ref, out_ref,
             dq_acc, dk_acc, dv_acc, dq_buf, dk_buf, dv_buf, sems):
        step = pl.program_id(0) * pairs + pl.program_id(1)

        def columns(at):
            return [pltpu.make_async_copy(
                buf, out_ref.at[:, pl.ds(pl.multiple_of(j * QKV_WIDTH + at * PAIR, PAIR), PAIR)], sems.at[j])
                for j, buf in enumerate((dq_buf, dk_buf, dv_buf))]

        def tail(ref, col0, at):
            cols = pl.ds(pl.multiple_of(col0 + at * TAIL_COLS, TAIL_COLS), TAIL_COLS)
            return pltpu.make_async_copy(ref, out_ref.at[:, cols], sems.at[3])

        gl_steps, u_steps = 2 * D_MODEL // TAIL_COLS, SSM_WIDTH // TAIL_COLS
        from_gl = step < gl_steps
        from_u = jnp.logical_and(step >= gl_steps, step < gl_steps + u_steps)
        tail_gl = tail(dgl_ref, 3 * QKV_WIDTH + SSM_WIDTH, step)
        tail_u = tail(du_ref, 3 * QKV_WIDTH, step - gl_steps)
        pl.when(from_gl)(tail_gl.start)
        pl.when(from_u)(tail_u.start)

        for g, dil in enumerate(DILATIONS):
            @pl.when(pl.program_id(0) == g)
            def _(dil=dil):
                accumulate(dil, qr_ref, kr_ref, v_ref, do_ref, o_ref, lse_ref, dq_acc, dk_acc, dv_acc)

        @pl.when(step > 0)
        def _():
            for cp in columns(step - 1):
                cp.wait()

        def finish(t, carry):
            rows = pl.ds(pl.multiple_of(t * ROPE_ROWS, ROPE_ROWS), ROPE_ROWS)
            cf, ss = cf_ref[rows, :], ss_ref[rows, :]
            dq = dq_acc[rows, :] * (1.0 / math.sqrt(HEAD_DIM))
            dq_buf[rows, :] = _rope_transposed(dq, cf, ss).astype(BF16)
            dk_buf[rows, :] = _rope_transposed(dk_acc[rows, :], cf, ss).astype(BF16)
            dv_buf[rows, :] = dv_acc[rows, :].astype(BF16)
            return carry

        lax.fori_loop(0, SEQ // ROPE_ROWS, finish, 0)
        for cp in columns(step):
            cp.start()
        pl.when(from_gl)(tail_gl.wait)
        pl.when(from_u)(tail_u.wait)

        @pl.when(step == last)
        def _():
            for cp in columns(step):
                cp.wait()

    whole = pl.BlockSpec((SEQ, PAIR), lambda g, hp: (0, 0))
    pair = pl.BlockSpec((SEQ, PAIR), lambda g, hp: (0, hp))
    views = [pl.BlockSpec((SEQ, PAIR), lambda g, hp, c0=col0 // PAIR: (0, c0 + g * pairs + hp))
             for col0 in (0, QKV_WIDTH, 2 * QKV_WIDTH)]
    gl_blocks, u_blocks = 2 * D_MODEL // TAIL_COLS, SSM_WIDTH // TAIL_COLS
    assert gl_blocks + u_blocks <= last + 1
    gl_spec = pl.BlockSpec((SEQ, TAIL_COLS), lambda g, hp: (0, jnp.minimum(g * pairs + hp, gl_blocks - 1)))
    u_spec = pl.BlockSpec((SEQ, TAIL_COLS),
                          lambda g, hp: (0, jnp.clip(g * pairs + hp - gl_blocks, 0, u_blocks - 1)))
    return pl.pallas_call(
        body, grid=(len(DILATIONS), pairs),
        in_specs=views + [whole, whole, pair, pair, pair, u_spec, gl_spec],
        out_specs=HBM_OPERAND, out_shape=_sds((SEQ, IN_WIDTH), BF16),
        scratch_shapes=[pltpu.VMEM((SEQ, PAIR), F32)] * 3 + [pltpu.VMEM((SEQ, PAIR), BF16)] * 3
        + [pltpu.SemaphoreType.DMA((4,))],
        compiler_params=_cp(("arbitrary", "arbitrary")), name="attention_bwd")(
            proj, proj, proj, cos_f, sin_s, d_attn, attn, lse, d_u, d_gl)


def _cmul(ar, ai, br, bi):
    return ar * br - ai * bi, ar * bi + ai * br


def _pow256(ar, ai):
    for _ in range(8):
        ar, ai = _cmul(ar, ai, ar, ai)
    return ar, ai


def _chunk_carries(first_r, first_i, pr, pi, reverse):
    rows = lax.broadcasted_iota(jnp.int32, first_r.shape, 0)
    out_r = jnp.zeros_like(first_r)
    out_i = jnp.zeros_like(first_i)
    hr = jnp.zeros_like(first_r[0:1])
    hi = jnp.zeros_like(hr)
    order = range(SCAN_CHUNKS - 1, -1, -1) if reverse else range(SCAN_CHUNKS)
    for c in order:
        out_r = jnp.where(rows == c, hr, out_r)
        out_i = jnp.where(rows == c, hi, out_i)
        tr, ti = _cmul(pr[0:1], pi[0:1], hr, hi)
        hr = first_r[c:c + 1] + tr
        hi = first_i[c:c + 1] + ti
    return out_r, out_i


def _tile(j):
    return pl.ds(pl.multiple_of(j * SCAN_CHUNKS, SCAN_CHUNKS), SCAN_CHUNKS)


def _to_scan_rows(t):
    per = SCAN_STEPS // PHASES
    return t.reshape(PHASES, SCAN_CHUNKS, per, t.shape[1]).transpose(2, 0, 1, 3).reshape(t.shape)


def _from_scan_rows(t):
    per = SCAN_STEPS // PHASES
    return t.reshape(per, PHASES, SCAN_CHUNKS, t.shape[1]).transpose(1, 2, 0, 3).reshape(t.shape)


def _scan_in_place(hr_ref, hi_ref, a_r, a_i):
    def local(j, carry):
        tr, ti = _cmul(a_r, a_i, carry[0], carry[1])
        nr = tr + hr_ref[_tile(j), :]
        ni = ti + hi_ref[_tile(j), :]
        hr_ref[_tile(j), :] = nr
        hi_ref[_tile(j), :] = ni
        return nr, ni

    zero = jnp.zeros_like(a_r)
    last_r, last_i = lax.fori_loop(0, SCAN_STEPS, local, (zero, zero), unroll=4)
    pr, pi = _pow256(a_r, a_i)
    er, ei = _chunk_carries(last_r, last_i, pr, pi, reverse=False)

    def fix(j, carry):
        tr, ti = _cmul(carry[0], carry[1], er, ei)
        hr_ref[_tile(j), :] += tr
        hi_ref[_tile(j), :] += ti
        return _cmul(carry[0], carry[1], a_r, a_i)

    lax.fori_loop(0, SCAN_STEPS, fix, (a_r, a_i), unroll=4)
    return er, ei


def _reverse_scan_in_place(lr_ref, li_ref, hr_ref, hi_ref, er, ei, a_r, a_i):
    def local(t, carry):
        j = SCAN_STEPS - 1 - t
        tr, ti = _cmul(a_r, a_i, carry[0], carry[1])
        nr = tr + lr_ref[_tile(j), :]
        ni = ti + li_ref[_tile(j), :]
        lr_ref[_tile(j), :] = nr
        li_ref[_tile(j), :] = ni
        return nr, ni

    zero = jnp.zeros_like(a_r)
    first_r, first_i = lax.fori_loop(0, SCAN_STEPS, local, (zero, zero), unroll=4)
    pr, pi = _pow256(a_r, a_i)
    nxt_r, nxt_i = _chunk_carries(first_r, first_i, pr, pi, reverse=True)

    def accumulate(lam_r, lam_i, hp_r, hp_i, acc):
        return (acc[0] + lam_r * hp_r + lam_i * hp_i, acc[1] + lam_i * hp_r - lam_r * hp_i)

    def fix(t, carry):
        qr, qi, acc_r, acc_i = carry
        j = SCAN_STEPS - 1 - t
        tr, ti = _cmul(qr, qi, nxt_r, nxt_i)
        lam_r = lr_ref[_tile(j), :] + tr
        lam_i = li_ref[_tile(j), :] + ti
        lr_ref[_tile(j), :] = lam_r
        li_ref[_tile(j), :] = lam_i
        acc_r, acc_i = accumulate(lam_r, lam_i, hr_ref[_tile(j - 1), :], hi_ref[_tile(j - 1), :], (acc_r, acc_i))
        qr, qi = _cmul(qr, qi, a_r, a_i)
        return qr, qi, acc_r, acc_i

    qr, qi, acc_r, acc_i = lax.fori_loop(0, SCAN_STEPS - 1, fix, (a_r, a_i, zero, zero), unroll=4)
    tr, ti = _cmul(qr, qi, nxt_r, nxt_i)
    lam_r = lr_ref[_tile(0), :] + tr
    lam_i = li_ref[_tile(0), :] + ti
    lr_ref[_tile(0), :] = lam_r
    li_ref[_tile(0), :] = lam_i
    acc_r, acc_i = accumulate(lam_r, lam_i, er, ei, (acc_r, acc_i))
    return jnp.sum(acc_r, axis=0, keepdims=True), jnp.sum(acc_i, axis=0, keepdims=True)


def _rope_tables():
    half = HEAD_DIM // 2
    inv_freq = ROPE_THETA ** (-jnp.arange(half, dtype=F32) / half)
    ang = jnp.arange(SEQ, dtype=F32)[:, None] * inv_freq[None, :]
    cos, sin = jnp.cos(ang), jnp.sin(ang)
    cos_f = jnp.concatenate([cos, cos, cos, cos], axis=1)
    sin_s = jnp.concatenate([-sin, sin, -sin, sin], axis=1)
    return cos_f, sin_s


def _ssm_discretise(a_re, a_im, log_dt, b_re, b_im):
    lam = lax.complex(a_re, a_im)
    dt = jnp.exp(log_dt)[:, None]
    a_bar = jnp.exp(lam * dt)
    b_bar = ((a_bar - 1.0) / lam)[..., None] * lax.complex(b_re, b_im)
    return a_bar.real, a_bar.imag, b_bar.real, b_bar.imag


SSM_SLABS = 4
SLAB_GROUPS = SSM_GROUPS // SSM_SLABS
SLAB_IN = SSM_WIDTH // SSM_SLABS
SLAB_STATE = SSM_LANES // SSM_SLABS


def _slab_block_diag(blocks):
    _, r, c = blocks.shape
    eye = jnp.eye(SLAB_GROUPS, dtype=blocks.dtype)
    b5 = blocks.reshape(SSM_SLABS, SLAB_GROUPS, r, 1, c) * eye[None, :, None, :, None]
    return b5.reshape(SSM_SLABS, SLAB_GROUPS * r, SLAB_GROUPS * c)


def _diag_blocks(a, b):
    ra, cb = a.shape[1], b.shape[1]
    wa, wb = ra // SLAB_GROUPS, cb // SLAB_GROUPS
    d = lax.dot_general(a, b, (((0,), (0,)), ((), ())), preferred_element_type=F32)
    row_g = jnp.right_shift(lax.broadcasted_iota(jnp.int32, (ra, cb), 0), wa.bit_length() - 1)
    col_g = jnp.right_shift(lax.broadcasted_iota(jnp.int32, (ra, cb), 1), wb.bit_length() - 1)
    d = jnp.where(row_g == col_g, d, 0.0)
    fold = (jnp.bitwise_and(lax.broadcasted_iota(jnp.int32, (cb, wb), 0), wb - 1)
            == lax.broadcasted_iota(jnp.int32, (cb, wb), 1)).astype(F32)
    return jnp.dot(d, fold, preferred_element_type=F32, precision=lax.Precision.HIGHEST)


def _slab_specs():
    tok = pl.BlockSpec((SEQ, SLAB_IN), lambda j: (0, j))
    state = pl.BlockSpec((SEQ, SLAB_STATE), lambda j: (0, j))
    b_in = pl.BlockSpec((None, SLAB_IN, SLAB_STATE), lambda j: (j, 0, 0))
    c_out = pl.BlockSpec((None, SLAB_STATE, SLAB_IN), lambda j: (j, 0, 0))
    vec = pl.BlockSpec((1, SLAB_STATE), lambda j: (0, j))
    ent = pl.BlockSpec((SCAN_CHUNKS, SLAB_STATE), lambda j: (0, j))
    return tok, state, b_in, c_out, vec, ent


def _ssm_forward(u, b_in_r, b_in_i, c_out_r, c_out_ni, a_r, a_i):
    def body(u_ref, br_ref, bi_ref, cr_ref, ci_ref, ar_ref, ai_ref, y_ref, hr_ref, hi_ref, er_ref, ei_ref):
        uu = u_ref[...]
        hr_ref[...] = jnp.dot(uu, br_ref[...], preferred_element_type=F32)
        hi_ref[...] = jnp.dot(uu, bi_ref[...], preferred_element_type=F32)
        a_re = jnp.broadcast_to(ar_ref[...], (SCAN_CHUNKS, SLAB_STATE))
        a_im = jnp.broadcast_to(ai_ref[...], (SCAN_CHUNKS, SLAB_STATE))
        er_ref[...], ei_ref[...] = _scan_in_place(hr_ref, hi_ref, a_re, a_im)
        y_ref[...] = (jnp.dot(hr_ref[...].astype(BF16), cr_ref[...], preferred_element_type=F32)
                      + jnp.dot(hi_ref[...].astype(BF16), ci_ref[...], preferred_element_type=F32))

    tok, state, b_in, c_out, vec, ent = _slab_specs()
    return pl.pallas_call(
        body, grid=(SSM_SLABS,), in_specs=[tok, b_in, b_in, c_out, c_out, vec, vec],
        out_specs=[tok, state, state, ent, ent],
        out_shape=[_sds((SEQ, SSM_WIDTH)), _sds((SEQ, SSM_LANES)), _sds((SEQ, SSM_LANES)),
                   _sds((SCAN_CHUNKS, SSM_LANES)), _sds((SCAN_CHUNKS, SSM_LANES))],
        compiler_params=_cp(("parallel",)), name="ssm_forward")(u, b_in_r, b_in_i, c_out_r, c_out_ni, a_r, a_i)


def _ssm_backward(d_y, d_u_skip, u, h_r, h_i, e_r, e_i, b_in_r, b_in_i, c_out_r, c_out_ni, a_r, a_i):
    def body(dy_ref, skip_ref, u_ref, hr_ref, hi_ref, er_ref, ei_ref, br_ref, bi_ref, cr_ref, ci_ref, ar_ref, ai_ref,
             du_ref, dar_ref, dai_ref, dcr_ref, dci_ref, dbr_ref, dbi_ref, lr_ref, li_ref):
        dy = dy_ref[...]
        lr_ref[...] = _dot_nt(dy, cr_ref[...])
        li_ref[...] = _dot_nt(dy, ci_ref[...])
        a_re = jnp.broadcast_to(ar_ref[...], (SCAN_CHUNKS, SLAB_STATE))
        a_im = -jnp.broadcast_to(ai_ref[...], (SCAN_CHUNKS, SLAB_STATE))
        dar_ref[...], dai_ref[...] = _reverse_scan_in_place(lr_ref, li_ref, hr_ref, hi_ref, er_ref[...], ei_ref[...],
                                                            a_re, a_im)
        dcr_ref[...] = _diag_blocks(dy, hr_ref[...].astype(BF16))
        dci_ref[...] = _diag_blocks(dy, hi_ref[...].astype(BF16))
        lam_r, lam_i = lr_ref[...].astype(BF16), li_ref[...].astype(BF16)
        uu = u_ref[...]
        dbr_ref[...] = _diag_blocks(uu, lam_r)
        dbi_ref[...] = _diag_blocks(uu, lam_i)
        du = skip_ref[...] + _dot_nt(lam_r, br_ref[...]) + _dot_nt(lam_i, bi_ref[...])
        du_ref[...] = du.astype(BF16)

    tok, state, b_in, c_out, vec, ent = _slab_specs()
    db = pl.BlockSpec((SLAB_IN, SSM_STATE), lambda j: (j, 0))
    return pl.pallas_call(
        body, grid=(SSM_SLABS,), in_specs=[tok, tok, tok, state, state, ent, ent, b_in, b_in, c_out, c_out, vec, vec],
        out_specs=[tok, vec, vec, db, db, db, db],
        out_shape=[_sds((SEQ, SSM_WIDTH), BF16), _sds((1, SSM_LANES)), _sds((1, SSM_LANES))]
        + [_sds((SSM_WIDTH, SSM_STATE))] * 4,
        scratch_shapes=[pltpu.VMEM((SEQ, SLAB_STATE), F32)] * 2,
        compiler_params=_cp(("parallel",)), name="ssm_backward")(
            d_y, d_u_skip, u, h_r, h_i, e_r, e_i, b_in_r, b_in_i, c_out_r, c_out_ni, a_r, a_i)


FF_ROWS = 1024
FF_SHARD = D_FF // N_CHIPS


def _dot_nt(a, b):
    return lax.dot_general(a, b, (((1,), (1,)), ((), ())), preferred_element_type=F32)


def _ffn_up(h, w_gate_t, w_up_t):
    def body(h_ref, wg_ref, wu_ref, a_ref, b_ref, act_ref):
        hb = h_ref[...].astype(BF16)
        a = _dot_nt(hb, wg_ref[...])
        b = _dot_nt(hb, wu_ref[...])
        a_ref[...] = a
        b_ref[...] = b
        act_ref[...] = (a * jax.nn.sigmoid(a) * b).astype(BF16)

    w_spec = pl.BlockSpec((None, FF_SHARD, D_MODEL), lambda i, k: (k, 0, 0))
    o_spec = pl.BlockSpec((None, FF_ROWS, FF_SHARD), lambda i, k: (k, i, 0))
    shape = (N_CHIPS, SEQ, FF_SHARD)
    return pl.pallas_call(
        body, grid=(SEQ // FF_ROWS, N_CHIPS),
        in_specs=[pl.BlockSpec((FF_ROWS, D_MODEL), lambda i, k: (i, 0)), w_spec, w_spec],
        out_specs=[o_spec, o_spec, o_spec], out_shape=[_sds(shape), _sds(shape), _sds(shape, BF16)],
        compiler_params=_cp(("parallel", "parallel")), name="ffn_up")(h, w_gate_t, w_up_t)


def _ffn_down_ln2_loss(act, w_down, h, tgt, ln_g, ln_b):
    def body(act_ref, w_ref, h_ref, tgt_ref, g_ref, b_ref, dz_ref, loss_ref, dg_ref, db_ref, acc):
        i, k = pl.program_id(0), pl.program_id(1)
        part = jnp.dot(act_ref[...], w_ref[...], preferred_element_type=F32)

        @pl.when(k == 0)
        def _():
            acc[...] = part

        @pl.when(k > 0)
        def _():
            acc[...] += part

        @pl.when(k == N_CHIPS - 1)
        def _():
            g = g_ref[...]
            xhat, rstd = _ln_stats(DN_ALPHA * h_ref[...] + acc[...])
            err = xhat * g + b_ref[...] - tgt_ref[...]
            d_out = err * (1.0 / D_MODEL)
            dz_ref[...] = _ln_bwd(d_out, xhat, rstd, g)
            loss_rows = jnp.sum(err * err, axis=-1, keepdims=True) * (0.5 / D_MODEL)
            sums = (jnp.broadcast_to(jnp.sum(loss_rows, axis=0, keepdims=True), loss_ref.shape),
                    _colsum(d_out * xhat), _colsum(d_out))
            for ref, val in zip((loss_ref, dg_ref, db_ref), sums):
                @pl.when(i == 0)
                def _(ref=ref, val=val):
                    ref[...] = val

                @pl.when(i > 0)
                def _(ref=ref, val=val):
                    ref[...] += val

    row = pl.BlockSpec((FF_ROWS, D_MODEL), lambda i, k: (i, 0))
    vec = pl.BlockSpec((1, D_MODEL), lambda i, k: (0, 0))
    return pl.pallas_call(
        body, grid=(SEQ // FF_ROWS, N_CHIPS),
        in_specs=[pl.BlockSpec((None, FF_ROWS, FF_SHARD), lambda i, k: (k, i, 0)),
                  pl.BlockSpec((None, FF_SHARD, D_MODEL), lambda i, k: (k, 0, 0)), row, row, vec, vec],
        out_specs=[row, pl.BlockSpec((1, BLOCK), lambda i, k: (0, 0)), vec, vec],
        out_shape=[_sds((SEQ, D_MODEL)), _sds((1, BLOCK)), _sds((1, D_MODEL)), _sds((1, D_MODEL))],
        scratch_shapes=[pltpu.VMEM((FF_ROWS, D_MODEL), F32)],
        compiler_params=_cp(("arbitrary", "arbitrary")), name="ffn_down_ln2_loss")(act, w_down, h, tgt, ln_g, ln_b)


def _ffn_down_bwd(dz, w_down, a, b):
    def body(dz_ref, wd_ref, a_ref, b_ref, da_ref, db_ref):
        d_act = _dot_nt(dz_ref[...].astype(BF16), wd_ref[...])
        av = a_ref[...]
        sg = jax.nn.sigmoid(av)
        da_ref[...] = (d_act * b_ref[...] * sg * (1.0 + av * (1.0 - sg))).astype(BF16)
        db_ref[...] = (d_act * av * sg).astype(BF16)

    t_spec = pl.BlockSpec((None, FF_ROWS, FF_SHARD), lambda i, k: (k, i, 0))
    shape = (N_CHIPS, SEQ, FF_SHARD)
    return pl.pallas_call(
        body, grid=(SEQ // FF_ROWS, N_CHIPS),
        in_specs=[pl.BlockSpec((FF_ROWS, D_MODEL), lambda i, k: (i, 0)),
                  pl.BlockSpec((None, FF_SHARD, D_MODEL), lambda i, k: (k, 0, 0)), t_spec, t_spec],
        out_specs=[t_spec, t_spec], out_shape=[_sds(shape, BF16), _sds(shape, BF16)],
        compiler_params=_cp(("parallel", "parallel")), name="ffn_down_bwd")(dz, w_down, a, b)


def _ffn_dh(d_a, d_b, w_gate_t, w_up_t):
    def body(da_ref, db_ref, wg_ref, wu_ref, o_ref, acc):
        k = pl.program_id(1)
        part = (jnp.dot(da_ref[...], wg_ref[...], preferred_element_type=F32)
                + jnp.dot(db_ref[...], wu_ref[...], preferred_element_type=F32))

        @pl.when(k == 0)
        def _():
            acc[...] = part

        @pl.when(k > 0)
        def _():
            acc[...] += part

        @pl.when(k == N_CHIPS - 1)
        def _():
            o_ref[...] = acc[...]

    t_spec = pl.BlockSpec((None, FF_ROWS, FF_SHARD), lambda i, k: (k, i, 0))
    w_spec = pl.BlockSpec((None, FF_SHARD, D_MODEL), lambda i, k: (k, 0, 0))
    return pl.pallas_call(
        body, grid=(SEQ // FF_ROWS, N_CHIPS), in_specs=[t_spec, t_spec, w_spec, w_spec],
        out_specs=pl.BlockSpec((FF_ROWS, D_MODEL), lambda i, k: (i, 0)), out_shape=_sds((SEQ, D_MODEL)),
        scratch_shapes=[pltpu.VMEM((FF_ROWS, D_MODEL), F32)],
        compiler_params=_cp(("parallel", "arbitrary")), name="ffn_dh")(d_a, d_b, w_gate_t, w_up_t)


def _local_step(x, tgt, wts, small, ffn_grads, mixer_grads, small_grads):
    s = SEQ
    cos_f, sin_s = [_to_phase_rows(t) for t in _rope_tables()]
    x = _reorder_rows(x, to_phase=True, name="phase_rows_x")
    tgt = _reorder_rows(tgt, to_phase=True, name="phase_rows_target")

    chip, w_own, w_others = wts["w_in_parts"]
    proj_own = _project_in_own(chip, x, w_own, cos_f, sin_s)
    proj = _project_in(chip, x, w_others, cos_f, sin_s, proj_own)

    attn, lse = _attention_fwd(proj)

    (abar_r, abar_i, bbar_r, bbar_i), ssm_vjp = jax.vjp(
        _ssm_discretise, small["ssm_a_re"], small["ssm_a_im"], small["ssm_log_dt"], small["ssm_b_re"], small["ssm_b_im"])
    b_in_r, b_in_i = [_slab_block_diag(b.transpose(0, 2, 1)).astype(BF16) for b in (bbar_r, bbar_i)]
    c_out_r = _slab_block_diag(small["ssm_c_re"].transpose(0, 2, 1)).astype(BF16)
    c_out_ni = _slab_block_diag(-small["ssm_c_im"].transpose(0, 2, 1)).astype(BF16)
    a_r, a_i = abar_r.reshape(1, SSM_LANES), abar_i.reshape(1, SSM_LANES)
    d_skip = small["ssm_d"].reshape(1, SSM_WIDTH)

    u_f = _to_scan_rows(proj[:, 3 * QKV_WIDTH:3 * QKV_WIDTH + SSM_WIDTH])
    u_p = u_f.astype(BF16)
    y_c, h_r, h_i, e_r, e_i = _ssm_forward(u_p, b_in_r, b_in_i, c_out_r, c_out_ni, a_r, a_i)

    def branch(t, wg):
        return jnp.concatenate([jnp.dot(t, wg[k], preferred_element_type=F32) for k in range(N_CHIPS)], axis=1)

    def branch_t(t, wg):
        ns = wg.shape[2]
        return sum(_dot_nt(t[:, k * ns:(k + 1) * ns], wg[k]) for k in range(N_CHIPS))

    def gelu_glu(yc, u, dsk, wg):
        y = yc + dsk * u
        gel = (0.5 * y * (1.0 + jnp.tanh(GELU_C * (y + GELU_K * y * y * y)))).astype(BF16)
        glu = branch(gel, wg)
        return y, gel, glu, glu[:, :SSM_WIDTH] * jax.nn.sigmoid(glu[:, SSM_WIDTH:])

    y_s5, gel, glu, y_glu = _rowwise(
        gelu_glu, [y_c, u_f], [d_skip, wts["w_glu"]],
        [_sds((s, SSM_WIDTH)), _sds((s, SSM_WIDTH), BF16), _sds((s, 2 * SSM_WIDTH)), _sds((s, SSM_WIDTH), BF16)],
        tm=512, name="ssm_gelu_glu")
    y_glu = _from_scan_rows(y_glu)

    gl0 = (proj, D_MODEL, (3 * QKV_WIDTH + SSM_WIDTH) // D_MODEL)
    gl1 = (proj, D_MODEL, (3 * QKV_WIDTH + SSM_WIDTH) // D_MODEL + 1)
    b_gate = small["b_gate"]
    w_out = wts["w_out"].reshape(D_MODEL, D_MODEL)

    def mix_ln1(l0, l1, at, yg, xv, bg, wa, ws, wo, g, b):
        ya = branch(at.astype(BF16), wa)
        ys = branch(yg, ws)
        mixed = (jax.nn.sigmoid(l0 + bg[0:1]) * ya + jax.nn.sigmoid(l1 + bg[1:2]) * ys).astype(BF16)
        z = DN_ALPHA * xv + jnp.dot(mixed, wo, preferred_element_type=F32)
        xhat, _ = _ln_stats(z)
        return ya, ys, mixed, z, xhat * g + b

    y_attn, y_ssm, mixed, z1, h = _rowwise(
        mix_ln1, [gl0, gl1, attn, y_glu, x],
        [b_gate, wts["w_attn_br"], wts["w_ssm_br"], w_out, small["ln1_g"], small["ln1_b"]],
        [_sds((s, D_MODEL)), _sds((s, D_MODEL)), _sds((s, D_MODEL), BF16), _sds((s, D_MODEL)), _sds((s, D_MODEL))],
        tm=256, name="mix_ln1")

    nf = D_FF // N_CHIPS
    w_gate_t, w_up_t, w_down = wts["w_ff_gate"], wts["w_ff_up"], wts["w_ff_down"]
    ff_a, ff_b, act = _ffn_up(h, w_gate_t, w_up_t)
    dz2, loss_v, d_ln2_g, d_ln2_b = _ffn_down_ln2_loss(act, w_down, h, tgt, small["ln2_g"], small["ln2_b"])

    d_a, d_b = _ffn_down_bwd(dz2, w_down, ff_a, ff_b)

    def grad_rows(lhs, rhs, name):
        return _matmul(lhs, rhs, grid=(N_CHIPS,), a_spec=pl.BlockSpec((None, s, nf), lambda k: (k, 0, 0)),
                       b_spec=pl.BlockSpec((s, D_MODEL), lambda k: (0, 0)),
                       o_spec=pl.BlockSpec((None, nf, D_MODEL), lambda k: (k, 0, 0)),
                       out_shape=_sds((N_CHIPS, nf, D_MODEL), BF16), dims=(0, 0), name=name)

    g_w_ff_down = grad_rows(act, dz2, "g_w_ff_down")
    g_w_ff_gate = grad_rows(d_a, h, "g_w_ff_gate")
    g_w_ff_up = grad_rows(d_b, h, "g_w_ff_up")
    dh_ff = _ffn_dh(d_a, d_b, w_gate_t, w_up_t)

    def ln1_gate_bwd(dz, dff, z, l0, l1, ya, ys, g, bg, wo, wa, ws):
        xhat, rstd = _ln_stats(z)
        dh = DN_ALPHA * dz + dff
        dz_in = _ln_bwd(dh, xhat, rstd, g)
        dm = _dot_nt(dz_in.astype(BF16), wo)
        g0 = jax.nn.sigmoid(l0 + bg[0:1])
        g1 = jax.nn.sigmoid(l1 + bg[1:2])
        dl0 = dm * ya * g0 * (1.0 - g0)
        dl1 = dm * ys * g1 * (1.0 - g1)
        dya, dys = (dm * g0).astype(BF16), (dm * g1).astype(BF16)
        return (dz_in, dya, dys, jnp.concatenate([dl0, dl1], axis=1), branch_t(dya, wa), branch_t(dys, ws),
                _colsum(dh * xhat), _colsum(dh), _colsum(dl0), _colsum(dl1))

    dz1, d_y_attn, d_y_ssm, d_gl, d_attn, d_y_glu, d_ln1_g, d_ln1_b, d_bg0, d_bg1 = _rowwise(
        ln1_gate_bwd, [dz2, dh_ff, z1, gl0, gl1, y_attn, y_ssm],
        [small["ln1_g"], b_gate, w_out, wts["w_attn_br"], wts["w_ssm_br"]],
        [_sds((s, D_MODEL)), _sds((s, D_MODEL), BF16), _sds((s, D_MODEL), BF16), _sds((s, 2 * D_MODEL), BF16),
         _sds((s, ATTN_WIDTH)), _sds((s, SSM_WIDTH))],
        [_sds((1, D_MODEL))] * 4, tm=256, name="ln1_gate_bwd", after=(g_w_ff_down, g_w_ff_gate, g_w_ff_up))
    ffn_sent = ffn_grads({"w_ff_down": g_w_ff_down, "w_ff_gate": g_w_ff_gate, "w_ff_up": g_w_ff_up}, dz1)
    g_w_out = _mm_rows_tn(mixed, dz1, name="g_w_out")

    g_w_ssm_br = _mm_cols_tn(y_glu, d_y_ssm, ns=D_MODEL // N_CHIPS, name="g_w_ssm_br")
    d_y_glu = _to_scan_rows(d_y_glu)

    def glu_gelu_bwd(dyg, gl, y, u, dsk, wg):
        ga, gb = gl[:, :SSM_WIDTH], gl[:, SSM_WIDTH:]
        sg = jax.nn.sigmoid(gb)
        d_gl = jnp.concatenate([dyg * sg, dyg * ga * sg * (1.0 - sg)], axis=1).astype(BF16)
        dg = branch_t(d_gl, wg)
        th = jnp.tanh(GELU_C * (y + GELU_K * y * y * y))
        dy = dg * (0.5 * (1.0 + th) + 0.5 * y * (1.0 - th * th) * GELU_C * (1.0 + 3.0 * GELU_K * y * y))
        return d_gl, dy, dy * dsk, _colsum(dy * u)

    d_glu, d_y, d_u_skip, d_ssm_d = _rowwise(
        glu_gelu_bwd, [d_y_glu, glu, y_s5, u_f], [d_skip, wts["w_glu"]],
        [_sds((s, 2 * SSM_WIDTH), BF16), _sds((s, SSM_WIDTH), BF16), _sds((s, SSM_WIDTH))], [_sds((1, SSM_WIDTH))],
        tm=512, name="glu_gelu_bwd", after=tuple(ffn_sent))
    g_w_glu = _mm_cols_tn(gel, d_glu, ns=2 * SSM_WIDTH // N_CHIPS, name="g_w_glu")
    d_u, d_abar_r, d_abar_i, d_c_r, d_c_ni, d_bin_r, d_bin_i = _ssm_backward(
        d_y, d_u_skip, u_p, h_r, h_i, e_r, e_i, b_in_r, b_in_i, c_out_r, c_out_ni, a_r, a_i)
    d_u = _from_scan_rows(d_u)
    d_bbar_r = d_bin_r.reshape(SSM_GROUPS, SSM_GROUP, SSM_STATE).transpose(0, 2, 1)
    d_bbar_i = d_bin_i.reshape(SSM_GROUPS, SSM_GROUP, SSM_STATE).transpose(0, 2, 1)
    d_a_re, d_a_im, d_log_dt, d_b_re, d_b_im = ssm_vjp(
        (d_abar_r.reshape(SSM_GROUPS, SSM_STATE), d_abar_i.reshape(SSM_GROUPS, SSM_STATE), d_bbar_r, d_bbar_i))
    d_c_re = d_c_r.reshape(SSM_GROUPS, SSM_GROUP, SSM_STATE)
    d_c_im = -d_c_ni.reshape(SSM_GROUPS, SSM_GROUP, SSM_STATE)

    g_w_attn_br = _mm_cols_tn(attn, d_y_attn, ns=D_MODEL // N_CHIPS, name="g_w_attn_br")
    mixer_grads({"w_out": g_w_out, "w_ssm_br": g_w_ssm_br, "w_glu": g_w_glu, "w_attn_br": g_w_attn_br}, d_abar_r)
    small_g = {"b_gate": jnp.concatenate([d_bg0, d_bg1], axis=0), "ssm_a_re": d_a_re, "ssm_a_im": d_a_im,
               "ssm_log_dt": d_log_dt, "ssm_b_re": d_b_re, "ssm_b_im": d_b_im, "ssm_c_re": d_c_re, "ssm_c_im": d_c_im,
               "ssm_d": d_ssm_d.reshape(SSM_WIDTH), "ln1_g": d_ln1_g, "ln1_b": d_ln1_b, "ln2_g": d_ln2_g,
               "ln2_b": d_ln2_b}
    shared = small_grads(small_g, loss_v[0, 0])
    d_proj = _attention_bwd(proj, cos_f, sin_s, d_attn, attn, lse, d_u, d_gl)

    g_w_in = _mm_cols_tn(x, d_proj, ns=IN_WIDTH // N_CHIPS, name="g_w_in", after=tuple(shared))

    def grad_x_after(after):
        dx = _grad_layer_input(chip, d_proj, w_others, w_own, dz1, tm=FF_ROWS, name="dx_proj", after=after)
        return _reorder_rows(dx, to_phase=False, name="grad_x")

    return grad_x_after, g_w_in, d_proj


GATHER_ID, SWAP_ID, SCATTER_ID, JOIN_ID, EXCHANGE_ID = 1, 2, 3, 4, 5


def _place():
    return lax.axis_index("x"), lax.axis_index("y"), lax.axis_index("c")


def _other_chips(x, y):
    return [(1 - x, y), (x, 1 - y), (1 - x, 1 - y)]


def _handshake(peers):
    barrier = pltpu.get_barrier_semaphore()
    for peer in peers:
        pl.semaphore_signal(barrier, inc=1, device_id=peer, device_id_type=MESH)
    pl.semaphore_wait(barrier, len(peers))


def _sequencer(body, arrays, out_type, sems, collective_id, name):
    return pl.kernel(body, name=name, out_type=out_type,
                     mesh=plsc.ScalarSubcoreMesh(axis_name="sequencer", num_cores=1), scratch_types=sems,
                     compiler_params=pltpu.CompilerParams(collective_id=collective_id))(*arrays)


def _gather_weights(shards, *, name, own_slot=True):
    nw = len(shards)

    def body(*refs):
        ins, outs = refs[:nw], refs[nw:2 * nw]
        send_sems, recv_sems, relay_send, relay_recv, pass_send, pass_recv, local_sems = refs[2 * nw:]
        x, y, c = _place()
        chip = 2 * x + y
        sibling = (x, y, 1 - c)
        nbr = [(1 - x, y, c), (x, 1 - y, c)]
        s_nbr = [2 * (1 - x) + y, 2 * x + (1 - y)]
        s_diag = 2 * (1 - x) + (1 - y)
        _handshake([sibling] + nbr)

        def copy(src, dst, send_sem, recv_sem, to):
            return pltpu.make_async_remote_copy(src_ref=src, dst_ref=dst, send_sem=send_sem, recv_sem=recv_sem,
                                                device_id=to, device_id_type=MESH)

        def rows(w, core, quarter=None):
            hw = shards[w].shape[0] // 2
            if quarter is None:
                return pl.ds(core * hw, hw)
            return pl.ds(core * hw + quarter * (hw // 2), hw // 2)

        own, sent = [], []
        for w in range(nw):
            if own_slot:
                cp = pltpu.make_async_copy(ins[w], outs[w].at[chip], local_sems.at[w])
                cp.start()
                own.append(cp)
            for d in range(2):
                cp = copy(ins[w].at[rows(w, c)], outs[w].at[chip, rows(w, c)], send_sems.at[w, d], recv_sems.at[w, d],
                          nbr[d])
                cp.start()
                sent.append(cp)
        for w in range(nw):
            for d in range(2):
                landed = outs[w].at[s_nbr[d], rows(w, c)]
                copy(landed, landed, send_sems.at[w, d], recv_sems.at[w, d], nbr[d]).wait_recv()
                quarter = outs[w].at[s_nbr[d], rows(w, c, d)]
                for cp in (copy(quarter, quarter, relay_send.at[w, d], relay_recv.at[w, d], nbr[1 - d]),
                           copy(landed, landed, pass_send.at[w, d], pass_recv.at[w, d], sibling)):
                    cp.start()
                    sent.append(cp)
        for w in range(nw):
            for d in range(2):
                quarter = outs[w].at[s_diag, rows(w, c, d)]
                copy(quarter, quarter, relay_send.at[w, d], relay_recv.at[w, d], nbr[1 - d]).wait_recv()
                cp = copy(quarter, quarter, pass_send.at[w, 2 + d], pass_recv.at[w, 2 + d], sibling)
                cp.start()
                sent.append(cp)
        for w in range(nw):
            for d in range(2):
                theirs = outs[w].at[s_nbr[d], rows(w, 1 - c)]
                copy(theirs, theirs, pass_send.at[w, d], pass_recv.at[w, d], sibling).wait_recv()
                theirs = outs[w].at[s_diag, rows(w, 1 - c, d)]
                copy(theirs, theirs, pass_send.at[w, 2 + d], pass_recv.at[w, 2 + d], sibling).wait_recv()
        for cp in sent:
            cp.wait_send()
        for cp in own:
            cp.wait()

    sem = pltpu.SemaphoreType.DMA
    return _sequencer(body, shards, [_sds((N_CHIPS,) + a.shape, a.dtype) for a in shards],
                      [sem((nw, 2)), sem((nw, 2)), sem((nw, 2)), sem((nw, 2)), sem((nw, 4)), sem((nw, 4)), sem((nw,))],
                      GATHER_ID, name)


def _swap_other_halves(grads, *, name):
    nw = len(grads)

    def body(*refs):
        ins, outs = refs[:nw], refs[nw:2 * nw]
        send_sems, recv_sems = refs[2 * nw:]
        x, y, c = _place()
        _handshake([(x, y, 1 - c)])
        cps = []
        for w in range(nw):
            hw = grads[w].shape[1] // 2
            cp = pltpu.make_async_remote_copy(
                src_ref=ins[w].at[:, pl.ds((1 - c) * hw, hw)], dst_ref=outs[w], send_sem=send_sems.at[w],
                recv_sem=recv_sems.at[w], device_id=(x, y, 1 - c), device_id_type=MESH)
            cp.start()
            cps.append(cp)
        for cp in cps:
            cp.wait()

    sem = pltpu.SemaphoreType.DMA
    return _sequencer(body, grads, [_sds((N_CHIPS, g.shape[1] // 2, g.shape[2]), g.dtype) for g in grads],
                      [sem((nw,)), sem((nw,))], SWAP_ID, name)


def _add_my_halves(core, grads, others, *, name, after=()):
    nw = len(grads)
    halves = [g.shape[1] // 2 for g in grads]

    def body(core_ref, *refs):
        outs = refs[2 * nw + len(after):]
        for g_ref, o_ref, out_ref in zip(refs[:nw], refs[nw:2 * nw], outs):
            out_ref[...] = (g_ref[...].astype(F32) + o_ref[...].astype(F32)).astype(out_ref.dtype)

    in_specs = [pl.BlockSpec((None, None, hw, g.shape[2]), lambda s, core_ref: (s, core_ref[0], 0, 0))
                for g, hw in zip(grads, halves)]
    in_specs += [pl.BlockSpec((None, hw, g.shape[2]), lambda s, core_ref: (s, 0, 0)) for g, hw in zip(grads, halves)]
    return pl.pallas_call(
        body,
        grid_spec=pltpu.PrefetchScalarGridSpec(
            num_scalar_prefetch=1, grid=(N_CHIPS,), in_specs=in_specs + [HBM_OPERAND] * len(after),
            out_specs=[pl.BlockSpec((None, hw, g.shape[2]), lambda s, core_ref: (s, 0, 0))
                       for g, hw in zip(grads, halves)]),
        out_shape=[_sds((N_CHIPS, hw, g.shape[2]), BF16) for g, hw in zip(grads, halves)],
        compiler_params=_cp(("parallel",)), name=name)(
            core, *[g.reshape(N_CHIPS, 2, hw, g.shape[2]) for g, hw in zip(grads, halves)], *others, *after)


def _scatter_partials(parts, *, name):
    nw = len(parts)

    def body(*refs):
        ins, outs = refs[:nw], refs[nw:2 * nw]
        send_sems, recv_sems = refs[2 * nw:]
        x, y, c = _place()
        _handshake([(cx, cy, c) for cx, cy in _other_chips(x, y)])
        cps = []
        for w in range(nw):
            for j, (cx, cy) in enumerate(_other_chips(x, y)):
                cp = pltpu.make_async_remote_copy(
                    src_ref=ins[w].at[2 * cx + cy], dst_ref=outs[w].at[j], send_sem=send_sems.at[w, j],
                    recv_sem=recv_sems.at[w, j], device_id=(cx, cy, c), device_id_type=MESH)
                cp.start()
                cps.append(cp)
        for cp in cps:
            cp.wait()

    sem = pltpu.SemaphoreType.DMA
    return _sequencer(body, parts, [_sds((3,) + p.shape[1:], p.dtype) for p in parts],
                      [sem((nw, 3)), sem((nw, 3))], SCATTER_ID, name)


SUM_STEPS = 2


def _sum_partials(chip, parts, recvd, *, name, after=()):
    nw = len(parts)
    rows = [p.shape[1] // SUM_STEPS for p in parts]

    def body(chip_ref, *refs):
        outs = refs[2 * nw + len(after):]
        for p_ref, r_ref, out_ref in zip(refs[:nw], refs[nw:2 * nw], outs):
            acc = p_ref[...].astype(F32)
            for j in range(3):
                acc = acc + r_ref[j].astype(F32)
            out_ref[...] = acc

    in_specs = [pl.BlockSpec((None, th, p.shape[2]), lambda i, chip_ref: (chip_ref[0], i, 0))
                for p, th in zip(parts, rows)]
    in_specs += [pl.BlockSpec((3, th, p.shape[2]), lambda i, chip_ref: (0, i, 0)) for p, th in zip(parts, rows)]
    return pl.pallas_call(
        body,
        grid_spec=pltpu.PrefetchScalarGridSpec(
            num_scalar_prefetch=1, grid=(SUM_STEPS,), in_specs=in_specs + [HBM_OPERAND] * len(after),
            out_specs=[pl.BlockSpec((th, p.shape[2]), lambda i, chip_ref: (i, 0)) for p, th in zip(parts, rows)]),
        out_shape=[_sds(p.shape[1:]) for p in parts], compiler_params=_cp(("parallel",)), name=name)(
            chip, *parts, *recvd, *after)


def _swap_reduced_halves(halves, *, name):
    nw = len(halves)

    def body(*refs):
        ins, outs = refs[:nw], refs[nw:2 * nw]
        send_sems, recv_sems = refs[2 * nw:]
        x, y, c = _place()
        _handshake([(x, y, 1 - c)])
        cps = []
        for w in range(nw):
            cp = pltpu.make_async_remote_copy(
                src_ref=ins[w], dst_ref=outs[w], send_sem=send_sems.at[w], recv_sem=recv_sems.at[w],
                device_id=(x, y, 1 - c), device_id_type=MESH)
            cp.start()
            cps.append(cp)
        for cp in cps:
            cp.wait()

    sem = pltpu.SemaphoreType.DMA
    return _sequencer(body, halves, [_sds(h.shape, h.dtype) for h in halves], [sem((nw,)), sem((nw,))], JOIN_ID, name)


def _exchange_rows(vec, *, name):
    def body(v_ref, slots, send_sems, recv_sems, local_sem):
        x, y, c = _place()
        me = 4 * x + 2 * y + c
        peers = []
        for mask in range(1, N_DEV):
            peers.append((1 - x if mask & 4 else x, 1 - y if mask & 2 else y, 1 - c if mask & 1 else c))
        _handshake(peers)
        own = pltpu.make_async_copy(v_ref, slots.at[me], local_sem)
        own.start()
        cps = []
        for k, peer in enumerate(peers):
            cp = pltpu.make_async_remote_copy(
                src_ref=v_ref, dst_ref=slots.at[me], send_sem=send_sems.at[k], recv_sem=recv_sems.at[k],
                device_id=peer, device_id_type=MESH)
            cp.start()
            cps.append(cp)
        for k, (px, py, pc) in enumerate(peers):
            pltpu.make_async_remote_copy(
                src_ref=v_ref, dst_ref=slots.at[4 * px + 2 * py + pc], send_sem=send_sems.at[k],
                recv_sem=recv_sems.at[k], device_id=(px, py, pc), device_id_type=MESH).wait_recv()
        for cp in cps:
            cp.wait_send()
        own.wait()

    sem = pltpu.SemaphoreType.DMA
    return _sequencer(body, [vec], [_sds((N_DEV,) + vec.shape)], [sem((N_DEV - 1,)), sem((N_DEV - 1,)), sem(())],
                      EXCHANGE_ID, name)[0]


def _sum_slots(slots, *, name, after=()):
    def body(s_ref, *rest):
        out_ref = rest[len(after)]
        acc = s_ref[0]
        for d in range(1, N_DEV):
            acc = acc + s_ref[d]
        out_ref[...] = acc

    vmem = pl.BlockSpec(memory_space=pltpu.VMEM)
    return pl.pallas_call(
        body, in_specs=[vmem] + [HBM_OPERAND] * len(after), out_specs=vmem, out_shape=_sds(slots.shape[1:]),
        compiler_params=pltpu.CompilerParams(vmem_limit_bytes=VMEM_LIMIT_BYTES), name=name)(slots, *after)


def _reduce_scatter_start(grads, core, *, tag, add_after=()):
    others = _swap_other_halves(grads, name="swap_other_halves_" + tag)
    parts = _add_my_halves(core, grads, others, name="add_my_halves_" + tag, after=add_after)
    return parts, _scatter_partials(parts, name="scatter_partials_" + tag)


def _reduce_scatter_finish(parts, recvd, chip, *, tag, sum_after=()):
    mine = _sum_partials(chip, parts, recvd, name="sum_partials_" + tag, after=sum_after)
    return mine, _swap_reduced_halves(mine, name="swap_reduced_halves_" + tag)


ADAM_BLOCK_ELEMS = 256 * 1024


def _adam_rows(rows, cols):
    tm = rows
    while tm * cols > ADAM_BLOCK_ELEMS and tm % 16 == 0:
        tm //= 2
    return tm


def _adam_step(wv, gv, mv, vv):
    m2 = ADAM_B1 * mv + (1.0 - ADAM_B1) * gv
    v2 = ADAM_B2 * vv + (1.0 - ADAM_B2) * (gv * gv)
    m_hat = m2 / (1.0 - ADAM_B1 ** ADAM_STEP)
    v_hat = v2 / (1.0 - ADAM_B2 ** ADAM_STEP)
    return -ADAM_LR * (m_hat / (jnp.sqrt(v_hat) + ADAM_EPS) + ADAM_WD * wv), m2, v2


def _adamw_each(ws, gs, ms, vs, *, name, after=()):
    n = len(ws)
    whole = pl.BlockSpec(memory_space=pltpu.VMEM)

    def body(*refs):
        ins, outs = refs[:4 * n], refs[4 * n + len(after):]
        for i in range(n):
            res = _adam_step(*(ins[k * n + i][...] for k in range(4)))
            for k in range(3):
                outs[k * n + i][...] = res[k]

    out = pl.pallas_call(body, in_specs=[whole] * (4 * n) + [HBM_OPERAND] * len(after),
                         out_shape=[_sds(w.shape) for w in ws] * 3, name=name)(*ws, *gs, *ms, *vs, *after)
    return out[:n], out[n:2 * n], out[2 * n:]


def _adamw_halves(core, w, g_mine, g_theirs, m, v, *, name, after=()):
    rows, cols = w.shape
    hw = rows // 2
    tm = _adam_rows(hw, cols)
    per_half = hw // tm

    def body(core_ref, w_ref, gm_ref, gt_ref, m_ref, v_ref, *rest):
        g_out, d_out, m_out, v_out = rest[len(after):]
        mine = (pl.program_id(0) // per_half) == core_ref[0]
        g = jnp.where(mine, gm_ref[...], gt_ref[...])
        d, m2, v2 = _adam_step(w_ref[...], g, m_ref[...], v_ref[...])
        g_out[...] = g
        d_out[...] = d
        m_out[...] = m2
        v_out[...] = v2

    full = pl.BlockSpec((tm, cols), lambda i, core_ref: (i, 0))

    def half(wanted):
        def index(i, core_ref):
            in_use = ((i // per_half) == core_ref[0]) == wanted
            return (jnp.where(in_use, i % per_half, 0), 0)
        return pl.BlockSpec((tm, cols), index)

    return pl.pallas_call(
        body,
        grid_spec=pltpu.PrefetchScalarGridSpec(
            num_scalar_prefetch=1, grid=(rows // tm,),
            in_specs=[full, half(True), half(False), full, full] + [HBM_OPERAND] * len(after),
            out_specs=[full, full, full, full]),
        out_shape=[_sds((rows, cols))] * 4, compiler_params=_cp(("parallel",)), name=name)(
            core, w, g_mine, g_theirs, m, v, *after)


HELD_TRANSPOSED = ("w_ff_gate", "w_ff_up")


def _as_rows(name, arr):
    return arr[0].T if name in HELD_TRANSPOSED else arr[0]


def _from_rows(name, arr2d):
    return (arr2d.T if name in HELD_TRANSPOSED else arr2d)[None]


STORED_SWAPPED = ("ssm_b_re", "ssm_b_im")


def _as_stored(name, arr):
    return jnp.swapaxes(arr, -1, -2) if name in STORED_SWAPPED else arr


def _pack_rows(arrs):
    flat = jnp.concatenate([a.reshape(-1).astype(F32) for a in arrs])
    rows = -(-flat.shape[0] // 1024) * 8
    return jnp.pad(flat, (0, rows * 128 - flat.shape[0])).reshape(rows, 128)


def _unpack_rows(vec, shapes):
    flat = vec.reshape(-1)
    out, off = [], 0
    for shp in shapes:
        size = math.prod(shp)
        out.append(flat[off:off + size].reshape(shp))
        off += size
    return out


SMALL = ("b_gate", "ssm_a_re", "ssm_a_im", "ssm_log_dt", "ssm_b_re", "ssm_b_im", "ssm_c_re", "ssm_c_im", "ssm_d",
         "ln1_g", "ln1_b", "ln2_g", "ln2_b")
GATHER_GROUPS = (("w_in", ("w_in",)), ("mixer", ("w_attn_br", "w_ssm_br", "w_glu", "w_out")),
                 ("ffn_up", ("w_ff_gate", "w_ff_up")), ("ffn_down", ("w_ff_down",)))
REDUCE_GROUPS = (("ffn", ("w_ff_down", "w_ff_gate", "w_ff_up")),
                 ("mixer", ("w_out", "w_ssm_br", "w_glu", "w_attn_br")), ("w_in", ("w_in",)))
WEIGHTS = ("w_in", "b_gate", "w_attn_br", "w_ssm_br", "w_out", "ssm_a_re", "ssm_a_im", "ssm_log_dt", "ssm_b_re",
           "ssm_b_im", "ssm_c_re", "ssm_c_im", "ssm_d", "w_glu", "ln1_g", "ln1_b", "w_ff_gate", "w_ff_up", "w_ff_down",
           "ln2_g", "ln2_b")


def kernel(x, w_in, b_gate, w_attn_br, w_ssm_br, w_out, ssm_a_re, ssm_a_im, ssm_log_dt, ssm_b_re, ssm_b_im, ssm_c_re, ssm_c_im, ssm_d, w_glu, ln1_g, ln1_b, w_ff_gate, w_ff_up, w_ff_down, ln2_g, ln2_b, loss_target, m_w_in, m_b_gate, m_w_attn_br, m_w_ssm_br, m_w_out, m_ssm_a_re, m_ssm_a_im, m_ssm_log_dt, m_ssm_b_re, m_ssm_b_im, m_ssm_c_re, m_ssm_c_im, m_ssm_d, m_w_glu, m_ln1_g, m_ln1_b, m_w_ff_gate, m_w_ff_up, m_w_ff_down, m_ln2_g, m_ln2_b, v_w_in, v_b_gate, v_w_attn_br, v_w_ssm_br, v_w_out, v_ssm_a_re, v_ssm_a_im, v_ssm_log_dt, v_ssm_b_re, v_ssm_b_im, v_ssm_c_re, v_ssm_c_im, v_ssm_d, v_w_glu, v_ln1_g, v_ln1_b, v_w_ff_gate, v_w_ff_up, v_w_ff_down, v_ln2_g, v_ln2_b):
    given = dict(locals())
    px, py, pc = _place()
    chip = 2 * px + py
    core_s = jnp.reshape(pc, (1,)).astype(jnp.int32)
    chip_s = jnp.reshape(chip, (1,)).astype(jnp.int32)

    wts = {}
    for tag, names in GATHER_GROUPS:
        shards = [_as_rows(n, given[n]).astype(BF16) for n in names]
        first = tag == GATHER_GROUPS[0][0]
        slots = _gather_weights(shards, name="gather_" + tag, own_slot=not first)
        if first:
            wts["w_in_parts"] = (chip_s, shards[0], slots[0])
        else:
            wts.update(zip(names, slots))
    ncol = D_MODEL // N_CHIPS
    bg_mine = jnp.where(pc == 0, b_gate[0], jnp.zeros_like(b_gate[0]))
    bg_full = lax.dynamic_update_slice(jnp.zeros((2, D_MODEL), F32), bg_mine, (0, chip * ncol))
    bg_slots = _exchange_rows(bg_full.reshape(16, 128), name="exchange_gate_bias")
    bg_full = _sum_slots(bg_slots, name="sum_gate_bias").reshape(2, D_MODEL)
    small = {n: given[n][0] for n in SMALL if n.startswith("ssm")}
    small.update({n: given[n] for n in ("ln1_g", "ln1_b", "ln2_g", "ln2_b")})
    small["b_gate"] = bg_full

    groups = dict(REDUCE_GROUPS)
    parts, recvd, reduced, sent = {}, {}, {}, {}
    grads, delta, new_m, new_v, done = {}, {}, {}, {}, {}

    def start(tag, big_g, add_after):
        parts[tag], recvd[tag] = _reduce_scatter_start([big_g[n] for n in groups[tag]], core_s, tag=tag,
                                                       add_after=add_after)
        return parts[tag]

    def reduce_sum(tag, after):
        reduced[tag] = _reduce_scatter_finish(parts[tag], recvd[tag], chip_s, tag=tag, sum_after=after)
        return reduced[tag][0]

    def adam(tag, after):
        for n, g_mine, g_theirs in zip(groups[tag], *reduced[tag]):
            res = _adamw_halves(core_s, _as_rows(n, given[n]), g_mine, g_theirs, _as_rows(n, given["m_" + n]),
                                _as_rows(n, given["v_" + n]), name="adamw_" + n, after=after)
            done[n] = res[1]
            grads[n], delta[n], new_m[n], new_v[n] = [_from_rows(n, r) for r in res]

    def ffn_grads(big_g, norm_bwd):
        return start("ffn", big_g, (norm_bwd,))

    def mixer_grads(big_g, scan_bwd):
        return start("mixer", big_g, (scan_bwd, *reduce_sum("ffn", (scan_bwd,))))

    def small_grads(small_g, loss_mine):
        sent["stored"] = [_as_stored(n, small_g[n]) for n in SMALL] + [loss_mine.reshape(1)]
        packed = _pack_rows(sent["stored"])
        sent["slots"] = _exchange_rows(packed, name="exchange_small")
        return (packed,)

    grad_x_after, g_w_in, attention_bwd = _local_step(x[0], loss_target[0], wts, small,
                                                      ffn_grads, mixer_grads, small_grads)

    reduce_sum("mixer", (attention_bwd,))
    summed = _sum_slots(sent["slots"], name="sum_small", after=(g_w_in,))
    adam("mixer", (g_w_in,))
    start("w_in", {"w_in": g_w_in}, (summed, *[done[n] for n in groups["mixer"]]))
    in_flight = (parts["w_in"][0],)
    grad_x = grad_x_after(in_flight)
    adam("ffn", in_flight)
    summed = _unpack_rows(summed, [a.shape for a in sent["stored"]])
    loss = summed.pop()[0]
    at = SMALL.index("b_gate")
    summed[at] = lax.dynamic_slice(summed[at], (0, chip * ncol), (2, ncol))
    summed = [g.reshape(1, -1) if g.ndim == 1 else g for g in summed]
    held = [[_as_stored(n, given[prefix + n]).reshape(g.shape) for n, g in zip(SMALL, summed)]
            for prefix in ("", "m_", "v_")]
    small_out = _adamw_each(held[0], summed, held[1], held[2], name="adamw_small", after=in_flight)
    for out, arrs in zip((grads, delta, new_m, new_v), (summed, *small_out)):
        out.update((n, _as_stored(n, a).reshape(given[n].shape)) for n, a in zip(SMALL, arrs))
    reduce_sum("w_in", (*[done[n] for n in groups["ffn"]], small_out[0][0], grad_x))
    adam("w_in", ())

    return (loss, grad_x.reshape(x.shape), *[grads[n] for n in WEIGHTS], *[delta[n] for n in WEIGHTS],
            *[new_m[n] for n in WEIGHTS], *[new_v[n] for n in WEIGHTS])
```

```python
import math

import jax
import jax.numpy as jnp
from jax import lax
from jax.experimental import pallas as pl
from jax.experimental.pallas import tpu as pltpu
from jax.experimental.pallas import tpu_sc as plsc

F32 = jnp.float32
BF16 = jnp.bfloat16
MESH = pl.DeviceIdType.MESH

D_MODEL = 1024
SEQ = 2048
HEAD_DIM = 64
ATTN_HEADS = 8
DILATIONS = (1, 4, 16)
ATTN_WIDTH = ATTN_HEADS * HEAD_DIM
QKV_WIDTH = 3 * ATTN_WIDTH
BLOCK = 128
ROPE_THETA = 10000.0
NEG_INF = -1e30
SSM_GROUP = 16
SSM_GROUPS = 32
SSM_WIDTH = 512
SSM_STATE = 64
SSM_LANES = SSM_GROUPS * SSM_STATE
SCAN_CHUNKS = 8
SCAN_STEPS = SEQ // SCAN_CHUNKS
IN_WIDTH = 3 * QKV_WIDTH + SSM_WIDTH + 2 * D_MODEL
D_FF = 2816
N_CHIPS = 4
N_DEV = 8
DN_ALPHA = 2.0 ** 0.25
LN_EPS = 1e-5
ADAM_LR = 0.001
ADAM_B1 = 0.9
ADAM_B2 = 0.999
ADAM_EPS = 1e-08
ADAM_WD = 0.01
ADAM_STEP = 10
GELU_C = math.sqrt(2.0 / math.pi)
GELU_K = 0.044715

VMEM_LIMIT_BYTES = 56 * 1024 * 1024


def _sds(shape, dtype=F32):
    return jax.ShapeDtypeStruct(tuple(shape), dtype)


def _cp(semantics=None):
    return pltpu.CompilerParams(dimension_semantics=semantics, vmem_limit_bytes=VMEM_LIMIT_BYTES)


HBM_OPERAND = pl.BlockSpec(memory_space=pl.ANY)


def _matmul(a, b, *, grid, a_spec, b_spec, o_spec, out_shape, dims, k_axis=None, name, after=()):
    nk = grid[k_axis] if k_axis is not None else 1
    o_block = tuple(d for d in o_spec.block_shape if d is not None)
    n_after = len(after)

    def body(a_ref, b_ref, *rest):
        o_ref, acc = rest[n_after], rest[n_after + 1:]
        part = lax.dot_general(a_ref[...].astype(BF16), b_ref[...].astype(BF16),
                               (((dims[0],), (dims[1],)), ((), ())), preferred_element_type=F32)
        if k_axis is None:
            o_ref[...] = part.astype(o_ref.dtype)
        else:
            k = pl.program_id(k_axis)

            @pl.when(k == 0)
            def _():
                acc[0][...] = part

            @pl.when(k > 0)
            def _():
                acc[0][...] += part

            @pl.when(k == nk - 1)
            def _():
                o_ref[...] = acc[0][...].astype(o_ref.dtype)

    sem = tuple("arbitrary" if ax == k_axis else "parallel" for ax in range(len(grid)))
    return pl.pallas_call(
        body, grid=grid, in_specs=[a_spec, b_spec] + [HBM_OPERAND] * n_after, out_specs=o_spec, out_shape=out_shape,
        scratch_shapes=[pltpu.VMEM(o_block, F32)] if k_axis is not None else [],
        compiler_params=_cp(sem), name=name)(a, b, *after)


def _grad_layer_input(chip, dy, wg, w_own, dz, *, tm, name, after=()):
    k, ns = wg.shape[1], wg.shape[2]
    m = dy.shape[0]

    def body(chip_ref, dy_ref, w_ref, own_ref, dz_ref, *rest):
        o_ref, acc = rest[len(after)], rest[len(after) + 1]
        s = pl.program_id(1)
        mine = s == chip_ref[0]

        @pl.when(s == 0)
        def _():
            acc[...] = DN_ALPHA * dz_ref[...]

        def add(w_block):
            acc[...] += lax.dot_general(dy_ref[...], w_block[...], (((1,), (1,)), ((), ())),
                                        preferred_element_type=F32)

        pl.when(mine)(lambda: add(own_ref))
        pl.when(jnp.logical_not(mine))(lambda: add(w_ref))

        @pl.when(s == N_CHIPS - 1)
        def _():
            o_ref[...] = acc[...]

    def gathered(i, s, chip_ref):
        neighbour = jnp.where(s == N_CHIPS - 1, s - 1, s + 1)
        return (jnp.where(s == chip_ref[0], neighbour, s), 0, 0)

    rows =pl.BlockSpec((tm, k), lambda i, s, chip_ref: (i, 0))
    return pl.pallas_call(
        body,
        grid_spec=pltpu.PrefetchScalarGridSpec(
            num_scalar_prefetch=1, grid=(m // tm, N_CHIPS),
            in_specs=[pl.BlockSpec((tm, ns), lambda i, s, chip_ref: (i, s)), pl.BlockSpec((None, k, ns), gathered),
                      pl.BlockSpec((k, ns), lambda i, s, chip_ref: (0, 0)), rows] + [HBM_OPERAND] * len(after),
            out_specs=rows, scratch_shapes=[pltpu.VMEM((tm, k), F32)]),
        out_shape=_sds((m, k)), compiler_params=_cp(("parallel", "arbitrary")), name=name)(
            chip, dy, wg, w_own, dz, *after)


def _mm_cols_tn(a, dy, *, ns, name, after=()):
    m, k = a.shape
    return _matmul(a, dy, grid=(N_CHIPS,), a_spec=pl.BlockSpec((m, k), lambda s: (0, 0)),
                   b_spec=pl.BlockSpec((m, ns), lambda s: (0, s)),
                   o_spec=pl.BlockSpec((None, k, ns), lambda s: (s, 0, 0)),
                   out_shape=_sds((N_CHIPS, k, ns), BF16), dims=(0, 0), name=name, after=after)


def _mm_rows_tn(a, dy, *, name):
    m, k = a.shape
    rows, n = k // N_CHIPS, dy.shape[1]
    return _matmul(a, dy, grid=(N_CHIPS,), a_spec=pl.BlockSpec((m, rows), lambda s: (0, s)),
                   b_spec=pl.BlockSpec((m, n), lambda s: (0, 0)),
                   o_spec=pl.BlockSpec((None, rows, n), lambda s: (s, 0, 0)),
                   out_shape=_sds((N_CHIPS, rows, n), BF16), dims=(0, 0), name=name)


def _rowwise(fn, tiled, full, outs, accs=(), *, tm, name, after=()):
    args, in_specs = [], []
    for t in tiled:
        if isinstance(t, tuple):
            arr, w, cb = t
            in_specs.append(pl.BlockSpec((tm, w), lambda i, cb=cb: (i, cb)))
        else:
            arr = t
            in_specs.append(pl.BlockSpec((tm, arr.shape[1]), lambda i: (i, 0)))
        args.append(arr)
    rows = args[0].shape[0]
    for f in full:
        in_specs.append(pl.BlockSpec(f.shape, lambda i, nd=f.ndim: (0,) * nd))
        args.append(f)
    out_specs = [pl.BlockSpec((tm, o.shape[1]), lambda i: (i, 0)) for o in outs]
    out_specs += [pl.BlockSpec(a.shape, lambda i, nd=len(a.shape): (0,) * nd) for a in accs]
    n_in, n_out = len(args), len(outs)
    in_specs += [HBM_OPERAND] * len(after)
    first_out = n_in + len(after)

    def body(*refs):
        res = fn(*[r[...] for r in refs[:n_in]])
        res = res if isinstance(res, (tuple, list)) else (res,)
        for r, v in zip(refs[first_out:first_out + n_out], res[:n_out]):
            r[...] = v.astype(r.dtype)
        i = pl.program_id(0)
        for r, v in zip(refs[first_out + n_out:], res[n_out:]):
            @pl.when(i == 0)
            def _(r=r, v=v):
                r[...] = v

            @pl.when(i > 0)
            def _(r=r, v=v):
                r[...] += v

    res = pl.pallas_call(
        body, grid=(rows // tm,), in_specs=in_specs, out_specs=out_specs, out_shape=list(outs) + list(accs),
        compiler_params=_cp(("arbitrary",) if accs else ("parallel",)), name=name)(*args, *after)
    return res


def _colsum(v):
    return jnp.sum(v, axis=0, keepdims=True)


def _ln_stats(z):
    mu = jnp.mean(z, axis=-1, keepdims=True)
    zc = z - mu
    var = jnp.mean(zc * zc, axis=-1, keepdims=True)
    rstd = lax.rsqrt(var + LN_EPS)
    return zc * rstd, rstd


def _ln_bwd(dy, xhat, rstd, g):
    dxh = dy * g
    m1 = jnp.mean(dxh, axis=-1, keepdims=True)
    m2 = jnp.mean(dxh * xhat, axis=-1, keepdims=True)
    return rstd * (dxh - m1 - xhat * m2)


def _swap_halves(t):
    w = t.shape[-1]
    lane = lax.broadcasted_iota(jnp.int32, t.shape, t.ndim - 1)
    return jnp.where((lane % HEAD_DIM) < HEAD_DIM // 2, pltpu.roll(t, w - HEAD_DIM // 2, t.ndim - 1),
                     pltpu.roll(t, HEAD_DIM // 2, t.ndim - 1))


PHASES = max(DILATIONS)
PAIR = 2 * HEAD_DIM
UNITS = SEQ // BLOCK
UNIT_BATCH = 16
ROPE_ROWS = 256
TAIL_COLS = 256


def _to_phase_rows(t):
    return t.reshape(SEQ // PHASES, PHASES, t.shape[1]).transpose(1, 0, 2).reshape(t.shape)


def _reorder_rows(arr, *, to_phase, name):
    def body(a_ref, o_ref):
        for rho in range(PHASES):
            phase = pl.ds(rho * BLOCK, BLOCK)
            strided = pl.ds(rho, BLOCK, stride=PHASES)
            src, dst = (strided, phase) if to_phase else (phase, strided)
            o_ref[dst, :] = a_ref[src, :]

    spec = pl.BlockSpec((SEQ, BLOCK), lambda j: (0, j))
    return pl.pallas_call(body, grid=(arr.shape[1] // BLOCK,), in_specs=[spec], out_specs=spec,
                          out_shape=_sds(arr.shape), compiler_params=_cp(("parallel",)), name=name)(arr)


def _rope(t, cf, ss):
    return t * cf + _swap_halves(t) * ss


def _rope_transposed(d, cf, ss):
    return d * cf + _swap_halves(d * ss)


def _unit_pieces(u, dil):
    pieces, length = PHASES // dil, 8 * dil
    if dil == 1:
        rho, i = 0, u
    elif dil == PHASES:
        rho, i = u, 0
    else:
        rho, i = jnp.bitwise_and(u, dil - 1), jnp.right_shift(u, dil.bit_length() - 1)
    before = jnp.maximum(i - 1, 0)
    cur = [pl.multiple_of((rho + dil * k) * BLOCK + length * i, 8) for k in range(pieces)]
    prev = [pl.multiple_of((rho + dil * k) * BLOCK + length * before, 8) for k in range(pieces)]
    return i, cur, prev


def _load_tile(ref, starts, dil):
    return jnp.concatenate([ref[pl.ds(st, 8 * dil), :] for st in starts], axis=0)


def _store_tile(ref, starts, dil, val, head=None, accumulate=False):
    length = 8 * dil
    lanes = slice(None) if head is None else pl.ds(head * HEAD_DIM, HEAD_DIM)
    cols = slice(None) if head is None else slice(head * HEAD_DIM, (head + 1) * HEAD_DIM)
    for k, st in enumerate(starts):
        piece = val[k * length:(k + 1) * length, cols]
        if accumulate:
            ref[pl.ds(st, length), lanes] += piece
        else:
            ref[pl.ds(st, length), lanes] = piece


def _tile_position(idx, dil):
    pieces, length = PHASES // dil, 8 * dil
    return pieces * jnp.bitwise_and(idx, length - 1) + jnp.right_shift(idx, length.bit_length() - 1)


def _band_mask(i, dil):
    row = lax.broadcasted_iota(jnp.int32, (BLOCK, 2 * BLOCK), 0)
    col = lax.broadcasted_iota(jnp.int32, (BLOCK, 2 * BLOCK), 1)
    key_pos = _tile_position(jnp.bitwise_and(col, BLOCK - 1), dil) + jnp.where(col >= BLOCK, 0, -BLOCK)
    dist = _tile_position(row, dil) - key_pos
    return (dist >= 0) & (dist <= BLOCK) & ((col >= BLOCK) | (i > 0))


def _causal_mask():
    row = lax.broadcasted_iota(jnp.int32, (BLOCK, BLOCK), 0)
    col = lax.broadcasted_iota(jnp.int32, (BLOCK, BLOCK), 1)
    return row >= col


def _pair_views(col0):
    return [pl.BlockSpec((SEQ, PAIR), lambda hp, g=g: (0, col0 // PAIR + g * (ATTN_WIDTH // PAIR) + hp))
            for g in range(len(DILATIONS))]


def _project_shard(shard, x_ref, w_ref, cf_ref, ss_ref, o_ref):
    ns = w_ref.shape[1]
    tiles = ns // PAIR
    xb = x_ref[...].astype(BF16)
    cf, ss = cf_ref[...], ss_ref[...]

    def write(rotated, scaled):
        for t0 in range(0, tiles, 2):
            strip = jnp.dot(xb, w_ref[:, t0 * PAIR:(t0 + 2) * PAIR], preferred_element_type=F32)
            for t in (t0, t0 + 1):
                val = strip[:, (t - t0) * PAIR:(t - t0 + 1) * PAIR]
                if t < rotated:
                    val = _rope(val, cf, ss)
                    if t < scaled:
                        val = val * (1.0 / math.sqrt(HEAD_DIM))
                o_ref[:, t * PAIR:(t + 1) * PAIR] = val

    for s in range(N_CHIPS):
        rotated = min(max(2 * QKV_WIDTH - s * ns, 0), ns) // PAIR
        scaled = min(max(QKV_WIDTH - s * ns, 0), ns) // PAIR
        @pl.when(shard == s)
        def _(rotated=rotated, scaled=scaled):
            write(rotated, scaled)


def _project_in_own(chip, x, w_own, cos_f, sin_s):
    ns = w_own.shape[1]

    def body(chip_ref, x_ref, w_ref, cf_ref, ss_ref, o_ref):
        _project_shard(chip_ref[0], x_ref, w_ref, cf_ref, ss_ref, o_ref)

    table = pl.BlockSpec((FF_ROWS, PAIR), lambda i, chip_ref: (i, 0))
    return pl.pallas_call(
        body,
        grid_spec=pltpu.PrefetchScalarGridSpec(
            num_scalar_prefetch=1, grid=(SEQ // FF_ROWS,),
            in_specs=[pl.BlockSpec((FF_ROWS, D_MODEL), lambda i, chip_ref: (i, 0)),
                      pl.BlockSpec((D_MODEL, ns), lambda i, chip_ref: (0, 0)), table, table],
            out_specs=pl.BlockSpec((FF_ROWS, ns), lambda i, chip_ref: (i, chip_ref[0]))),
        out_shape=_sds((SEQ, N_CHIPS * ns)), compiler_params=_cp(("parallel",)), name="project_in_own")(
            chip, x, w_own, cos_f, sin_s)


def _project_in(chip, x, wg, cos_f, sin_s, started):
    ns = wg.shape[2]

    def other(j, chip_ref):
        return (chip_ref[0] + 1 + j) % N_CHIPS

    def body(chip_ref, x_ref, w_ref, cf_ref, ss_ref, started_ref, o_ref):
        _project_shard(other(pl.program_id(1), chip_ref), x_ref, w_ref, cf_ref, ss_ref, o_ref)

    table = pl.BlockSpec((FF_ROWS, PAIR), lambda i, j, chip_ref: (i, 0))
    return pl.pallas_call(
        body,
        grid_spec=pltpu.PrefetchScalarGridSpec(
            num_scalar_prefetch=1, grid=(SEQ // FF_ROWS, N_CHIPS - 1),
            in_specs=[pl.BlockSpec((FF_ROWS, D_MODEL), lambda i, j, chip_ref: (i, 0)),
                      pl.BlockSpec((None, D_MODEL, ns), lambda i, j, chip_ref: (other(j, chip_ref), 0, 0)),
                      table, table, HBM_OPERAND],
            out_specs=pl.BlockSpec((FF_ROWS, ns), lambda i, j, chip_ref: (i, other(j, chip_ref)))),
        out_shape=_sds((SEQ, N_CHIPS * ns)), input_output_aliases={5: 0},
        compiler_params=_cp(("parallel", "parallel")), name="project_in")(chip, x, wg, cos_f, sin_s, started)


def _attention_fwd(proj):
    ng = len(DILATIONS)

    def body(*refs):
        q_refs, k_refs, v_refs = refs[:ng], refs[ng:2 * ng], refs[2 * ng:3 * ng]
        attn_ref, lse_ref = refs[3 * ng:]
        qr_refs, kr_refs = q_refs, k_refs
        first = lax.broadcasted_iota(jnp.int32, (BLOCK, PAIR), 1) < HEAD_DIM
        for g, dil in enumerate(DILATIONS):
            two_blocks = SEQ // dil > BLOCK

            def units(t, carry, g=g, dil=dil, two_blocks=two_blocks):
                picked = [_unit_pieces(t * UNIT_BATCH + j, dil) for j in range(UNIT_BATCH)]

                def tiles(ref, with_prev=False):
                    if with_prev and two_blocks:
                        return jnp.stack([jnp.concatenate([_load_tile(ref, prev, dil), _load_tile(ref, rows, dil)],
                                                          axis=0) for _, rows, prev in picked])
                    return jnp.stack([_load_tile(ref, rows, dil) for _, rows, _ in picked])

                qq = tiles(qr_refs[g]).astype(BF16)
                kk = tiles(kr_refs[g], True).astype(BF16)
                vv = tiles(v_refs[g], True).astype(BF16)
                if two_blocks:
                    valid = jnp.stack([_band_mask(i, dil) for i, _, _ in picked])
                else:
                    valid = _causal_mask()[None]
                mine = first[None]
                zero = jnp.zeros_like(qq)
                outs, lses = [], []
                for qh in (jnp.where(mine, qq, zero), jnp.where(mine, zero, qq)):
                    s = jnp.einsum("pqd,pkd->pqk", qh, kk, preferred_element_type=F32)
                    s = jnp.where(valid, s, NEG_INF)
                    m = jnp.max(s, axis=-1, keepdims=True)
                    p = jnp.exp(s - m)
                    l = jnp.sum(p, axis=-1, keepdims=True)
                    outs.append(jnp.einsum("pqk,pkd->pqd", p.astype(BF16), vv, preferred_element_type=F32) * (1.0 / l))
                    lses.append(m + jnp.log(l))
                o = jnp.where(mine, outs[0], outs[1])
                lse = jnp.where(mine, lses[0], lses[1])
                if g > 0:
                    lse_old = tiles(lse_ref)
                    m = jnp.maximum(lse_old, lse)
                    lse_new = m + jnp.log(jnp.exp(lse_old - m) + jnp.exp(lse - m))
                    o = tiles(attn_ref) * jnp.exp(lse_old - lse_new) + o * jnp.exp(lse - lse_new)
                    lse = lse_new
                for j, (_, rows, _) in enumerate(picked):
                    _store_tile(attn_ref, rows, dil, o[j])
                    _store_tile(lse_ref, rows, dil, lse[j])
                return carry

            lax.fori_loop(0, UNITS // UNIT_BATCH, units, 0)

    out = pl.BlockSpec((SEQ, PAIR), lambda hp: (0, hp))
    return pl.pallas_call(
        body, grid=(ATTN_WIDTH // PAIR,),
        in_specs=_pair_views(0) + _pair_views(QKV_WIDTH) + _pair_views(2 * QKV_WIDTH),
        out_specs=[out, out], out_shape=[_sds((SEQ, ATTN_WIDTH)), _sds((SEQ, ATTN_WIDTH))],
        compiler_params=_cp(("parallel",)), name="attention_fwd")(*([proj] * (3 * ng)))


def _attention_bwd(proj, cos_f, sin_s, d_attn, attn, lse, d_u, d_gl):
    pairs = ATTN_WIDTH // PAIR
    last = len(DILATIONS) * pairs - 1

    def accumulate(dil, qr_ref, kr_ref, v_ref, do_ref, o_ref, lse_ref, dq_acc, dk_acc, dv_acc):
        two_blocks = SEQ // dil > BLOCK
        dk_acc[...] = jnp.zeros_like(dk_acc)
        dv_acc[...] = jnp.zeros_like(dv_acc)
        nk = 2 * BLOCK if two_blocks else BLOCK
        first = lax.broadcasted_iota(jnp.int32, (BLOCK, PAIR), 1) < HEAD_DIM
        first_k = lax.broadcasted_iota(jnp.int32, (nk, PAIR), 1) < HEAD_DIM

        def units(t, carry):
            picked = [_unit_pieces(t * UNIT_BATCH + j, dil) for j in range(UNIT_BATCH)]

            def tiles(ref, with_prev=False):
                if with_prev and two_blocks:
                    return jnp.stack([jnp.concatenate([_load_tile(ref, prev, dil), _load_tile(ref, rows, dil)], axis=0)
                                      for _, rows, prev in picked])
                return jnp.stack([_load_tile(ref, rows, dil) for _, rows, _ in picked])

            qq = tiles(qr_ref).astype(BF16)
            kk = tiles(kr_ref, True).astype(BF16)
            vv = tiles(v_ref, True).astype(BF16)
            dof = tiles(do_ref)
            dd = dof * tiles(o_ref)
            lse3 = tiles(lse_ref)
            dob = dof.astype(BF16)
            if two_blocks:
                valid = jnp.stack([_band_mask(i, dil) for i, _, _ in picked])
            else:
                valid = _causal_mask()[None]
            zq, zf = jnp.zeros_like(qq), jnp.zeros_like(dd)
            dqs, dks, dvs = [], [], []
            for head in range(2):
                mine = first[None] if head == 0 else jnp.logical_not(first)[None]
                delta = jnp.sum(jnp.where(mine, dd, zf), axis=-1, keepdims=True)
                lse_h = lse3[:, :, head * HEAD_DIM:head * HEAD_DIM + 1]
                s = jnp.einsum("pqd,pkd->pqk", jnp.where(mine, qq, zq), kk, preferred_element_type=F32)
                p = jnp.where(valid, jnp.exp(s - lse_h), 0.0)
                dp = jnp.einsum("pqd,pkd->pqk", jnp.where(mine, dob, zq), vv, preferred_element_type=F32)
                ds = (p * (dp - delta)).astype(BF16)
                dqs.append(jnp.einsum("pqk,pkd->pqd", ds, kk, preferred_element_type=F32))
                dks.append(jnp.einsum("pqk,pqd->pkd", ds, qq, preferred_element_type=F32))
                dvs.append(jnp.einsum("pqk,pqd->pkd", p.astype(BF16), dob, preferred_element_type=F32))
            dq = jnp.where(first[None], dqs[0], dqs[1])
            dk = jnp.where(first_k[None], dks[0], dks[1])
            dv = jnp.where(first_k[None], dvs[0], dvs[1])
            for j, (_, rows, prev) in enumerate(picked):
                _store_tile(dq_acc, rows, dil, dq[j])
                _store_tile(dk_acc, rows, dil, dk[j, nk - BLOCK:], accumulate=True)
                _store_tile(dv_acc, rows, dil, dv[j, nk - BLOCK:], accumulate=True)
                if two_blocks:
                    _store_tile(dk_acc, prev, dil, dk[j, :BLOCK], accumulate=True)
                    _store_tile(dv_acc, prev, dil, dv[j, :BLOCK], accumulate=True)
            return carry

        lax.fori_loop(0, UNITS // UNIT_BATCH, units, 0)

    def body(qr_ref, kr_ref, v_ref, cf_ref, ss_ref, do_ref, o_ref, lse_ref, du_ref, dgl_ref, out_ref,
             dq_acc, dk_acc, dv_acc, dq_buf, dk_buf, dv_buf, sems):
        step = pl.program_id(0) * pairs + pl.program_id(1)

        def columns(at):
            return [pltpu.make_async_copy(
                buf, out_ref.at[:, pl.ds(pl.multiple_of(j * QKV_WIDTH + at * PAIR, PAIR), PAIR)], sems.at[j])
                for j, buf in enumerate((dq_buf, dk_buf, dv_buf))]

        def tail(ref, col0, at):
            cols = pl.ds(pl.multiple_of(col0 + at * TAIL_COLS, TAIL_COLS), TAIL_COLS)
            return pltpu.make_async_copy(ref, out_ref.at[:, cols], sems.at[3])

        gl_steps, u_steps = 2 * D_MODEL // TAIL_COLS, SSM_WIDTH // TAIL_COLS
        from_gl = step < gl_steps
        from_u = jnp.logical_and(step >= gl_steps, step < gl_steps + u_steps)
        tail_gl = tail(dgl_ref, 3 * QKV_WIDTH + SSM_WIDTH, step)
        tail_u = tail(du_ref, 3 * QKV_WIDTH, step - gl_steps)
        pl.when(from_gl)(tail_gl.start)
        pl.when(from_u)(tail_u.start)

        for g, dil in enumerate(DILATIONS):
            @pl.when(pl.program_id(0) == g)
            def _(dil=dil):
                accumulate(dil, qr_ref, kr_ref, v_ref, do_ref, o_ref, lse_ref, dq_acc, dk_acc, dv_acc)

        @pl.when(step > 0)
        def _():
            for cp in columns(step - 1):
                cp.wait()

        def finish(t, carry):
            rows = pl.ds(pl.multiple_of(t * ROPE_ROWS, ROPE_ROWS), ROPE_ROWS)
            cf, ss = cf_ref[rows, :], ss_ref[rows, :]
            dq = dq_acc[rows, :] * (1.0 / math.sqrt(HEAD_DIM))
            dq_buf[rows, :] = _rope_transposed(dq, cf, ss).astype(BF16)
            dk_buf[rows, :] = _rope_transposed(dk_acc[rows, :], cf, ss).astype(BF16)
            dv_buf[rows, :] = dv_acc[rows, :].astype(BF16)
            return carry

        lax.fori_loop(0, SEQ // ROPE_ROWS, finish, 0)
        for cp in columns(step):
            cp.start()
        pl.when(from_gl)(tail_gl.wait)
        pl.when(from_u)(tail_u.wait)

        @pl.when(step == last)
        def _():
            for cp in columns(step):
                cp.wait()

    whole = pl.BlockSpec((SEQ, PAIR), lambda g, hp: (0, 0))
    pair = pl.BlockSpec((SEQ, PAIR), lambda g, hp: (0, hp))
    views = [pl.BlockSpec((SEQ, PAIR), lambda g, hp, c0=col0 // PAIR: (0, c0 + g * pairs + hp))
             for col0 in (0, QKV_WIDTH, 2 * QKV_WIDTH)]
    gl_blocks, u_blocks = 2 * D_MODEL // TAIL_COLS, SSM_WIDTH // TAIL_COLS
    assert gl_blocks + u_blocks <= last + 1
    gl_spec = pl.BlockSpec((SEQ, TAIL_COLS), lambda g, hp: (0, jnp.minimum(g * pairs + hp, gl_blocks - 1)))
    u_spec = pl.BlockSpec((SEQ, TAIL_COLS),
                          lambda g, hp: (0, jnp.clip(g * pairs + hp - gl_blocks, 0, u_blocks - 1)))
    return pl.pallas_call(
        body, grid=(len(DILATIONS), pairs),
        in_specs=views + [whole, whole, pair, pair, pair, u_spec, gl_spec],
        out_specs=HBM_OPERAND, out_shape=_sds((SEQ, IN_WIDTH), BF16),
        scratch_shapes=[pltpu.VMEM((SEQ, PAIR), F32)] * 3 + [pltpu.VMEM((SEQ, PAIR), BF16)] * 3
        + [pltpu.SemaphoreType.DMA((4,))],
        compiler_params=_cp(("arbitrary", "arbitrary")), name="attention_bwd")(
            proj, proj, proj, cos_f, sin_s, d_attn, attn, lse, d_u, d_gl)


def _cmul(ar, ai, br, bi):
    return ar * br - ai * bi, ar * bi + ai * br


def _pow256(ar, ai):
    for _ in range(8):
        ar, ai = _cmul(ar, ai, ar, ai)
    return ar, ai


def _chunk_carries(first_r, first_i, pr, pi, reverse):
    rows = lax.broadcasted_iota(jnp.int32, first_r.shape, 0)
    out_r = jnp.zeros_like(first_r)
    out_i = jnp.zeros_like(first_i)
    hr = jnp.zeros_like(first_r[0:1])
    hi = jnp.zeros_like(hr)
    order = range(SCAN_CHUNKS - 1, -1, -1) if reverse else range(SCAN_CHUNKS)
    for c in order:
        out_r = jnp.where(rows == c, hr, out_r)
        out_i = jnp.where(rows == c, hi, out_i)
        tr, ti = _cmul(pr[0:1], pi[0:1], hr, hi)
        hr = first_r[c:c + 1] + tr
        hi = first_i[c:c + 1] + ti
    return out_r, out_i


def _tile(j):
    return pl.ds(pl.multiple_of(j * SCAN_CHUNKS, SCAN_CHUNKS), SCAN_CHUNKS)


def _to_scan_rows(t):
    per = SCAN_STEPS // PHASES
    return t.reshape(PHASES, SCAN_CHUNKS, per, t.shape[1]).transpose(2, 0, 1, 3).reshape(t.shape)


def _from_scan_rows(t):
    per = SCAN_STEPS // PHASES
    return t.reshape(per, PHASES, SCAN_CHUNKS, t.shape[1]).transpose(1, 2, 0, 3).reshape(t.shape)


def _scan_in_place(hr_ref, hi_ref, a_r, a_i):
    def local(j, carry):
        tr, ti = _cmul(a_r, a_i, carry[0], carry[1])
        nr = tr + hr_ref[_tile(j), :]
        ni = ti + hi_ref[_tile(j), :]
        hr_ref[_tile(j), :] = nr
        hi_ref[_tile(j), :] = ni
        return nr, ni

    zero = jnp.zeros_like(a_r)
    last_r, last_i = lax.fori_loop(0, SCAN_STEPS, local, (zero, zero), unroll=4)
    pr, pi = _pow256(a_r, a_i)
    er, ei = _chunk_carries(last_r, last_i, pr, pi, reverse=False)

    def fix(j, carry):
        tr, ti = _cmul(carry[0], carry[1], er, ei)
        hr_ref[_tile(j), :] += tr
        hi_ref[_tile(j), :] += ti
        return _cmul(carry[0], carry[1], a_r, a_i)

    lax.fori_loop(0, SCAN_STEPS, fix, (a_r, a_i), unroll=4)
    return er, ei


def _reverse_scan_in_place(lr_ref, li_ref, hr_ref, hi_ref, er, ei, a_r, a_i):
    def local(t, carry):
        j = SCAN_STEPS - 1 - t
        tr, ti = _cmul(a_r, a_i, carry[0], carry[1])
        nr = tr + lr_ref[_tile(j), :]
        ni = ti + li_ref[_tile(j), :]
        lr_ref[_tile(j), :] = nr
        li_ref[_tile(j), :] = ni
        return nr, ni

    zero = jnp.zeros_like(a_r)
    first_r, first_i = lax.fori_loop(0, SCAN_STEPS, local, (zero, zero), unroll=4)
    pr, pi = _pow256(a_r, a_i)
    nxt_r, nxt_i = _chunk_carries(first_r, first_i, pr, pi, reverse=True)

    def accumulate(lam_r, lam_i, hp_r, hp_i, acc):
        return (acc[0] + lam_r * hp_r + lam_i * hp_i, acc[1] + lam_i * hp_r - lam_r * hp_i)

    def fix(t, carry):
        qr, qi, acc_r, acc_i = carry
        j = SCAN_STEPS - 1 - t
        tr, ti = _cmul(qr, qi, nxt_r, nxt_i)
        lam_r = lr_ref[_tile(j), :] + tr
        lam_i = li_ref[_tile(j), :] + ti
        lr_ref[_tile(j), :] = lam_r
        li_ref[_tile(j), :] = lam_i
        acc_r, acc_i = accumulate(lam_r, lam_i, hr_ref[_tile(j - 1), :], hi_ref[_tile(j - 1), :], (acc_r, acc_i))
        qr, qi = _cmul(qr, qi, a_r, a_i)
        return qr, qi, acc_r, acc_i

    qr, qi, acc_r, acc_i = lax.fori_loop(0, SCAN_STEPS - 1, fix, (a_r, a_i, zero, zero), unroll=4)
    tr, ti = _cmul(qr, qi, nxt_r, nxt_i)
    lam_r = lr_ref[_tile(0), :] + tr
    lam_i = li_ref[_tile(0), :] + ti
    lr_ref[_tile(0), :] = lam_r
    li_ref[_tile(0), :] = lam_i
    acc_r, acc_i = accumulate(lam_r, lam_i, er, ei, (acc_r, acc_i))
    return jnp.sum(acc_r, axis=0, keepdims=True), jnp.sum(acc_i, axis=0, keepdims=True)


def _rope_tables():
    half = HEAD_DIM // 2
    inv_freq = ROPE_THETA ** (-jnp.arange(half, dtype=F32) / half)
    ang = jnp.arange(SEQ, dtype=F32)[:, None] * inv_freq[None, :]
    cos, sin = jnp.cos(ang), jnp.sin(ang)
    cos_f = jnp.concatenate([cos, cos, cos, cos], axis=1)
    sin_s = jnp.concatenate([-sin, sin, -sin, sin], axis=1)
    return cos_f, sin_s


def _ssm_discretise(a_re, a_im, log_dt, b_re, b_im):
    lam = lax.complex(a_re, a_im)
    dt = jnp.exp(log_dt)[:, None]
    a_bar = jnp.exp(lam * dt)
    b_bar = ((a_bar - 1.0) / lam)[..., None] * lax.complex(b_re, b_im)
    return a_bar.real, a_bar.imag, b_bar.real, b_bar.imag


SSM_SLABS = 4
SLAB_GROUPS = SSM_GROUPS // SSM_SLABS
SLAB_IN = SSM_WIDTH // SSM_SLABS
SLAB_STATE = SSM_LANES // SSM_SLABS


def _slab_block_diag(blocks):
    _, r, c = blocks.shape
    eye = jnp.eye(SLAB_GROUPS, dtype=blocks.dtype)
    b5 = blocks.reshape(SSM_SLABS, SLAB_GROUPS, r, 1, c) * eye[None, :, None, :, None]
    return b5.reshape(SSM_SLABS, SLAB_GROUPS * r, SLAB_GROUPS * c)


def _diag_blocks(a, b):
    ra, cb = a.shape[1], b.shape[1]
    wa, wb = ra // SLAB_GROUPS, cb // SLAB_GROUPS
    d = lax.dot_general(a, b, (((0,), (0,)), ((), ())), preferred_element_type=F32)
    row_g = jnp.right_shift(lax.broadcasted_iota(jnp.int32, (ra, cb), 0), wa.bit_length() - 1)
    col_g = jnp.right_shift(lax.broadcasted_iota(jnp.int32, (ra, cb), 1), wb.bit_length() - 1)
    d = jnp.where(row_g == col_g, d, 0.0)
    fold = (jnp.bitwise_and(lax.broadcasted_iota(jnp.int32, (cb, wb), 0), wb - 1)
            == lax.broadcasted_iota(jnp.int32, (cb, wb), 1)).astype(F32)
    return jnp.dot(d, fold, preferred_element_type=F32, precision=lax.Precision.HIGHEST)


def _slab_specs():
    tok = pl.BlockSpec((SEQ, SLAB_IN), lambda j: (0, j))
    state = pl.BlockSpec((SEQ, SLAB_STATE), lambda j: (0, j))
    b_in = pl.BlockSpec((None, SLAB_IN, SLAB_STATE), lambda j: (j, 0, 0))
    c_out = pl.BlockSpec((None, SLAB_STATE, SLAB_IN), lambda j: (j, 0, 0))
    vec = pl.BlockSpec((1, SLAB_STATE), lambda j: (0, j))
    ent = pl.BlockSpec((SCAN_CHUNKS, SLAB_STATE), lambda j: (0, j))
    return tok, state, b_in, c_out, vec, ent


def _ssm_forward(u, b_in_r, b_in_i, c_out_r, c_out_ni, a_r, a_i):
    def body(u_ref, br_ref, bi_ref, cr_ref, ci_ref, ar_ref, ai_ref, y_ref, hr_ref, hi_ref, er_ref, ei_ref):
        uu = u_ref[...]
        hr_ref[...] = jnp.dot(uu, br_ref[...], preferred_element_type=F32)
        hi_ref[...] = jnp.dot(uu, bi_ref[...], preferred_element_type=F32)
        a_re = jnp.broadcast_to(ar_ref[...], (SCAN_CHUNKS, SLAB_STATE))
        a_im = jnp.broadcast_to(ai_ref[...], (SCAN_CHUNKS, SLAB_STATE))
        er_ref[...], ei_ref[...] = _scan_in_place(hr_ref, hi_ref, a_re, a_im)
        y_ref[...] = (jnp.dot(hr_ref[...].astype(BF16), cr_ref[...], preferred_element_type=F32)
                      + jnp.dot(hi_ref[...].astype(BF16), ci_ref[...], preferred_element_type=F32))

    tok, state, b_in, c_out, vec, ent = _slab_specs()
    return pl.pallas_call(
        body, grid=(SSM_SLABS,), in_specs=[tok, b_in, b_in, c_out, c_out, vec, vec],
        out_specs=[tok, state, state, ent, ent],
        out_shape=[_sds((SEQ, SSM_WIDTH)), _sds((SEQ, SSM_LANES)), _sds((SEQ, SSM_LANES)),
                   _sds((SCAN_CHUNKS, SSM_LANES)), _sds((SCAN_CHUNKS, SSM_LANES))],
        compiler_params=_cp(("parallel",)), name="ssm_forward")(u, b_in_r, b_in_i, c_out_r, c_out_ni, a_r, a_i)


def _ssm_backward(d_y, d_u_skip, u, h_r, h_i, e_r, e_i, b_in_r, b_in_i, c_out_r, c_out_ni, a_r, a_i):
    def body(dy_ref, skip_ref, u_ref, hr_ref, hi_ref, er_ref, ei_ref, br_ref, bi_ref, cr_ref, ci_ref, ar_ref, ai_ref,
             du_ref, dar_ref, dai_ref, dcr_ref, dci_ref, dbr_ref, dbi_ref, lr_ref, li_ref):
        dy = dy_ref[...]
        lr_ref[...] = _dot_nt(dy, cr_ref[...])
        li_ref[...] = _dot_nt(dy, ci_ref[...])
        a_re = jnp.broadcast_to(ar_ref[...], (SCAN_CHUNKS, SLAB_STATE))
        a_im = -jnp.broadcast_to(ai_ref[...], (SCAN_CHUNKS, SLAB_STATE))
        dar_ref[...], dai_ref[...] = _reverse_scan_in_place(lr_ref, li_ref, hr_ref, hi_ref, er_ref[...], ei_ref[...],
                                                            a_re, a_im)
        dcr_ref[...] = _diag_blocks(dy, hr_ref[...].astype(BF16))
        dci_ref[...] = _diag_blocks(dy, hi_ref[...].astype(BF16))
        lam_r, lam_i = lr_ref[...].astype(BF16), li_ref[...].astype(BF16)
        uu = u_ref[...]
        dbr_ref[...] = _diag_blocks(uu, lam_r)
        dbi_ref[...] = _diag_blocks(uu, lam_i)
        du = skip_ref[...] + _dot_nt(lam_r, br_ref[...]) + _dot_nt(lam_i, bi_ref[...])
        du_ref[...] = du.astype(BF16)

    tok, state, b_in, c_out, vec, ent = _slab_specs()
    db = pl.BlockSpec((SLAB_IN, SSM_STATE), lambda j: (j, 0))
    return pl.pallas_call(
        body, grid=(SSM_SLABS,), in_specs=[tok, tok, tok, state, state, ent, ent, b_in, b_in, c_out, c_out, vec, vec],
        out_specs=[tok, vec, vec, db, db, db, db],
        out_shape=[_sds((SEQ, SSM_WIDTH), BF16), _sds((1, SSM_LANES)), _sds((1, SSM_LANES))]
        + [_sds((SSM_WIDTH, SSM_STATE))] * 4,
        scratch_shapes=[pltpu.VMEM((SEQ, SLAB_STATE), F32)] * 2,
        compiler_params=_cp(("parallel",)), name="ssm_backward")(
            d_y, d_u_skip, u, h_r, h_i, e_r, e_i, b_in_r, b_in_i, c_out_r, c_out_ni, a_r, a_i)


FF_ROWS = 1024
FF_SHARD = D_FF // N_CHIPS
NORM_TILE = 512


def _dot_nt(a, b):
    return lax.dot_general(a, b, (((1,), (1,)), ((), ())), preferred_element_type=F32)


def _ffn_up(h, w_gate_t, w_up_t):
    def body(h_ref, wg_ref, wu_ref, a_ref, b_ref, act_ref):
        hb = h_ref[...].astype(BF16)
        a = _dot_nt(hb, wg_ref[...])
        b = _dot_nt(hb, wu_ref[...])
        a_ref[...] = a
        b_ref[...] = b
        act_ref[...] = (a * jax.nn.sigmoid(a) * b).astype(BF16)

    w_spec = pl.BlockSpec((None, FF_SHARD, D_MODEL), lambda i, k: (k, 0, 0))
    o_spec = pl.BlockSpec((None, FF_ROWS, FF_SHARD), lambda i, k: (k, i, 0))
    shape = (N_CHIPS, SEQ, FF_SHARD)
    return pl.pallas_call(
        body, grid=(SEQ // FF_ROWS, N_CHIPS),
        in_specs=[pl.BlockSpec((FF_ROWS, D_MODEL), lambda i, k: (i, 0)), w_spec, w_spec],
        out_specs=[o_spec, o_spec, o_spec], out_shape=[_sds(shape), _sds(shape), _sds(shape, BF16)],
        compiler_params=_cp(("parallel", "parallel")), name="ffn_up")(h, w_gate_t, w_up_t)


def _ffn_down_ln2_loss(act, w_down, h, tgt, ln_g, ln_b):
    tiles, quarter = SEQ // NORM_TILE, NORM_TILE // N_CHIPS

    def body(act_ref, w_ref, h_ref, tgt_ref, g_ref, b_ref, dz_ref, loss_ref, dg_ref, db_ref, acc, done):
        i, k = pl.program_id(0), pl.program_id(1)

        @pl.when(jnp.logical_and(i == 0, k == 0))
        def _():
            acc[...] = jnp.zeros_like(acc)
            done[...] = jnp.zeros_like(done)
            loss_ref[...] = jnp.zeros_like(loss_ref)
            dg_ref[...] = jnp.zeros_like(dg_ref)
            db_ref[...] = jnp.zeros_like(db_ref)

        part = jnp.dot(act_ref[...], w_ref[...], preferred_element_type=F32)
        acc[...] = jnp.where(k == 0, part, acc[...] + part)

        rows = pl.ds(pl.multiple_of(k * quarter, quarter), quarter)
        counts = jnp.where(i > 0, 1.0, 0.0)
        g = g_ref[...]
        xhat, rstd = _ln_stats(DN_ALPHA * h_ref[rows, :] + done[rows, :])
        err = xhat * g + b_ref[...] - tgt_ref[rows, :]
        d_out = err * (counts / D_MODEL)
        dz_ref[rows, :] = _ln_bwd(d_out, xhat, rstd, g)
        loss_rows = jnp.sum(err * err, axis=-1, keepdims=True) * (counts * 0.5 / D_MODEL)
        loss_ref[...] += jnp.broadcast_to(jnp.sum(loss_rows, axis=0, keepdims=True), loss_ref.shape)
        dg_ref[...] += _colsum(d_out * xhat)
        db_ref[...] += _colsum(d_out)

        @pl.when(k == N_CHIPS - 1)
        def _():
            done[...] = acc[...]

    def matmul_step(i, k):
        drain = i == tiles
        return jnp.where(drain, N_CHIPS - 1, k), jnp.where(drain, tiles - 1, i)

    row = pl.BlockSpec((NORM_TILE, D_MODEL), lambda i, k: (jnp.maximum(i - 1, 0), 0))
    vec = pl.BlockSpec((1, D_MODEL), lambda i, k: (0, 0))
    return pl.pallas_call(
        body, grid=(tiles + 1, N_CHIPS),
        in_specs=[pl.BlockSpec((None, NORM_TILE, FF_SHARD), lambda i, k: (*matmul_step(i, k), 0)),
                  pl.BlockSpec((None, FF_SHARD, D_MODEL), lambda i, k: (matmul_step(i, k)[0], 0, 0)),
                  row, row, vec, vec],
        out_specs=[row, pl.BlockSpec((1, BLOCK), lambda i, k: (0, 0)), vec, vec],
        out_shape=[_sds((SEQ, D_MODEL)), _sds((1, BLOCK)), _sds((1, D_MODEL)), _sds((1, D_MODEL))],
        scratch_shapes=[pltpu.VMEM((NORM_TILE, D_MODEL), F32)] * 2,
        compiler_params=_cp(("arbitrary", "arbitrary")), name="ffn_down_ln2_loss")(act, w_down, h, tgt, ln_g, ln_b)


def _ffn_down_bwd(dz, w_down, a, b):
    def body(dz_ref, wd_ref, a_ref, b_ref, da_ref, db_ref):
        d_act = _dot_nt(dz_ref[...].astype(BF16), wd_ref[...])
        av = a_ref[...]
        sg = jax.nn.sigmoid(av)
        da_ref[...] = (d_act * b_ref[...] * sg * (1.0 + av * (1.0 - sg))).astype(BF16)
        db_ref[...] = (d_act * av * sg).astype(BF16)

    t_spec = pl.BlockSpec((None, FF_ROWS, FF_SHARD), lambda i, k: (k, i, 0))
    shape = (N_CHIPS, SEQ, FF_SHARD)
    return pl.pallas_call(
        body, grid=(SEQ // FF_ROWS, N_CHIPS),
        in_specs=[pl.BlockSpec((FF_ROWS, D_MODEL), lambda i, k: (i, 0)),
                  pl.BlockSpec((None, FF_SHARD, D_MODEL), lambda i, k: (k, 0, 0)), t_spec, t_spec],
        out_specs=[t_spec, t_spec], out_shape=[_sds(shape, BF16), _sds(shape, BF16)],
        compiler_params=_cp(("parallel", "parallel")), name="ffn_down_bwd")(dz, w_down, a, b)


def _ffn_dh(d_a, d_b, w_gate_t, w_up_t):
    def body(da_ref, db_ref, wg_ref, wu_ref, o_ref, acc):
        k = pl.program_id(1)
        part = (jnp.dot(da_ref[...], wg_ref[...], preferred_element_type=F32)
                + jnp.dot(db_ref[...], wu_ref[...], preferred_element_type=F32))

        @pl.when(k == 0)
        def _():
            acc[...] = part

        @pl.when(k > 0)
        def _():
            acc[...] += part

        @pl.when(k == N_CHIPS - 1)
        def _():
            o_ref[...] = acc[...]

    t_spec = pl.BlockSpec((None, FF_ROWS, FF_SHARD), lambda i, k: (k, i, 0))
    w_spec = pl.BlockSpec((None, FF_SHARD, D_MODEL), lambda i, k: (k, 0, 0))
    return pl.pallas_call(
        body, grid=(SEQ // FF_ROWS, N_CHIPS), in_specs=[t_spec, t_spec, w_spec, w_spec],
        out_specs=pl.BlockSpec((FF_ROWS, D_MODEL), lambda i, k: (i, 0)), out_shape=_sds((SEQ, D_MODEL)),
        scratch_shapes=[pltpu.VMEM((FF_ROWS, D_MODEL), F32)],
        compiler_params=_cp(("parallel", "arbitrary")), name="ffn_dh")(d_a, d_b, w_gate_t, w_up_t)


def _local_step(x, tgt, wts, small, ffn_grads, mixer_grads, small_grads):
    s = SEQ
    cos_f, sin_s = [_to_phase_rows(t) for t in _rope_tables()]
    x = _reorder_rows(x, to_phase=True, name="phase_rows_x")
    tgt = _reorder_rows(tgt, to_phase=True, name="phase_rows_target")

    chip, w_own, w_others = wts["w_in_parts"]
    proj_own = _project_in_own(chip, x, w_own, cos_f, sin_s)
    proj = _project_in(chip, x, w_others, cos_f, sin_s, proj_own)

    attn, lse = _attention_fwd(proj)

    (abar_r, abar_i, bbar_r, bbar_i), ssm_vjp = jax.vjp(
        _ssm_discretise, small["ssm_a_re"], small["ssm_a_im"], small["ssm_log_dt"], small["ssm_b_re"], small["ssm_b_im"])
    b_in_r, b_in_i = [_slab_block_diag(b.transpose(0, 2, 1)).astype(BF16) for b in (bbar_r, bbar_i)]
    c_out_r = _slab_block_diag(small["ssm_c_re"].transpose(0, 2, 1)).astype(BF16)
    c_out_ni = _slab_block_diag(-small["ssm_c_im"].transpose(0, 2, 1)).astype(BF16)
    a_r, a_i = abar_r.reshape(1, SSM_LANES), abar_i.reshape(1, SSM_LANES)
    d_skip = small["ssm_d"].reshape(1, SSM_WIDTH)

    u_f = _to_scan_rows(proj[:, 3 * QKV_WIDTH:3 * QKV_WIDTH + SSM_WIDTH])
    u_p = u_f.astype(BF16)
    y_c, h_r, h_i, e_r, e_i = _ssm_forward(u_p, b_in_r, b_in_i, c_out_r, c_out_ni, a_r, a_i)

    def branch(t, wg):
        return jnp.concatenate([jnp.dot(t, wg[k], preferred_element_type=F32) for k in range(N_CHIPS)], axis=1)

    def branch_t(t, wg):
        ns = wg.shape[2]
        return sum(_dot_nt(t[:, k * ns:(k + 1) * ns], wg[k]) for k in range(N_CHIPS))

    def gelu_glu(yc, u, dsk, wg):
        y = yc + dsk * u
        gel = (0.5 * y * (1.0 + jnp.tanh(GELU_C * (y + GELU_K * y * y * y)))).astype(BF16)
        glu = branch(gel, wg)
        return y, gel, glu, glu[:, :SSM_WIDTH] * jax.nn.sigmoid(glu[:, SSM_WIDTH:])

    y_s5, gel, glu, y_glu = _rowwise(
        gelu_glu, [y_c, u_f], [d_skip, wts["w_glu"]],
        [_sds((s, SSM_WIDTH)), _sds((s, SSM_WIDTH), BF16), _sds((s, 2 * SSM_WIDTH)), _sds((s, SSM_WIDTH), BF16)],
        tm=512, name="ssm_gelu_glu")
    y_glu = _from_scan_rows(y_glu)

    gl0 = (proj, D_MODEL, (3 * QKV_WIDTH + SSM_WIDTH) // D_MODEL)
    gl1 = (proj, D_MODEL, (3 * QKV_WIDTH + SSM_WIDTH) // D_MODEL + 1)
    b_gate = small["b_gate"]
    w_out = wts["w_out"].reshape(D_MODEL, D_MODEL)

    def mix_ln1(l0, l1, at, yg, xv, bg, wa, ws, wo, g, b):
        ya = branch(at.astype(BF16), wa)
        ys = branch(yg, ws)
        mixed = (jax.nn.sigmoid(l0 + bg[0:1]) * ya + jax.nn.sigmoid(l1 + bg[1:2]) * ys).astype(BF16)
        z = DN_ALPHA * xv + jnp.dot(mixed, wo, preferred_element_type=F32)
        xhat, _ = _ln_stats(z)
        return ya, ys, mixed, z, xhat * g + b

    y_attn, y_ssm, mixed, z1, h = _rowwise(
        mix_ln1, [gl0, gl1, attn, y_glu, x],
        [b_gate, wts["w_attn_br"], wts["w_ssm_br"], w_out, small["ln1_g"], small["ln1_b"]],
        [_sds((s, D_MODEL)), _sds((s, D_MODEL)), _sds((s, D_MODEL), BF16), _sds((s, D_MODEL)), _sds((s, D_MODEL))],
        tm=256, name="mix_ln1")

    nf = D_FF // N_CHIPS
    w_gate_t, w_up_t, w_down = wts["w_ff_gate"], wts["w_ff_up"], wts["w_ff_down"]
    ff_a, ff_b, act = _ffn_up(h, w_gate_t, w_up_t)
    dz2, loss_v, d_ln2_g, d_ln2_b = _ffn_down_ln2_loss(act, w_down, h, tgt, small["ln2_g"], small["ln2_b"])

    d_a, d_b = _ffn_down_bwd(dz2, w_down, ff_a, ff_b)

    def grad_rows(lhs, rhs, name):
        return _matmul(lhs, rhs, grid=(N_CHIPS,), a_spec=pl.BlockSpec((None, s, nf), lambda k: (k, 0, 0)),
                       b_spec=pl.BlockSpec((s, D_MODEL), lambda k: (0, 0)),
                       o_spec=pl.BlockSpec((None, nf, D_MODEL), lambda k: (k, 0, 0)),
                       out_shape=_sds((N_CHIPS, nf, D_MODEL), BF16), dims=(0, 0), name=name)

    g_w_ff_down = grad_rows(act, dz2, "g_w_ff_down")
    g_w_ff_gate = grad_rows(d_a, h, "g_w_ff_gate")
    g_w_ff_up = grad_rows(d_b, h, "g_w_ff_up")
    dh_ff = _ffn_dh(d_a, d_b, w_gate_t, w_up_t)

    def ln1_gate_bwd(dz, dff, z, l0, l1, ya, ys, g, bg, wo, wa, ws):
        xhat, rstd = _ln_stats(z)
        dh = DN_ALPHA * dz + dff
        dz_in = _ln_bwd(dh, xhat, rstd, g)
        dm = _dot_nt(dz_in.astype(BF16), wo)
        g0 = jax.nn.sigmoid(l0 + bg[0:1])
        g1 = jax.nn.sigmoid(l1 + bg[1:2])
        dl0 = dm * ya * g0 * (1.0 - g0)
        dl1 = dm * ys * g1 * (1.0 - g1)
        dya, dys = (dm * g0).astype(BF16), (dm * g1).astype(BF16)
        return (dz_in, dya, dys, jnp.concatenate([dl0, dl1], axis=1), branch_t(dya, wa), branch_t(dys, ws),
                _colsum(dh * xhat), _colsum(dh), _colsum(dl0), _colsum(dl1))

    dz1, d_y_attn, d_y_ssm, d_gl, d_attn, d_y_glu, d_ln1_g, d_ln1_b, d_bg0, d_bg1 = _rowwise(
        ln1_gate_bwd, [dz2, dh_ff, z1, gl0, gl1, y_attn, y_ssm],
        [small["ln1_g"], b_gate, w_out, wts["w_attn_br"], wts["w_ssm_br"]],
        [_sds((s, D_MODEL)), _sds((s, D_MODEL), BF16), _sds((s, D_MODEL), BF16), _sds((s, 2 * D_MODEL), BF16),
         _sds((s, ATTN_WIDTH)), _sds((s, SSM_WIDTH))],
        [_sds((1, D_MODEL))] * 4, tm=256, name="ln1_gate_bwd", after=(g_w_ff_down, g_w_ff_gate, g_w_ff_up))
    ffn_sent = ffn_grads({"w_ff_down": g_w_ff_down, "w_ff_gate": g_w_ff_gate, "w_ff_up": g_w_ff_up}, dz1)
    g_w_out = _mm_rows_tn(mixed, dz1, name="g_w_out")

    g_w_ssm_br = _mm_cols_tn(y_glu, d_y_ssm, ns=D_MODEL // N_CHIPS, name="g_w_ssm_br")
    d_y_glu = _to_scan_rows(d_y_glu)

    def glu_gelu_bwd(dyg, gl, y, u, dsk, wg):
        ga, gb = gl[:, :SSM_WIDTH], gl[:, SSM_WIDTH:]
        sg = jax.nn.sigmoid(gb)
        d_gl = jnp.concatenate([dyg * sg, dyg * ga * sg * (1.0 - sg)], axis=1).astype(BF16)
        dg = branch_t(d_gl, wg)
        th = jnp.tanh(GELU_C * (y + GELU_K * y * y * y))
        dy = dg * (0.5 * (1.0 + th) + 0.5 * y * (1.0 - th * th) * GELU_C * (1.0 + 3.0 * GELU_K * y * y))
        return d_gl, dy, dy * dsk, _colsum(dy * u)

    d_glu, d_y, d_u_skip, d_ssm_d = _rowwise(
        glu_gelu_bwd, [d_y_glu, glu, y_s5, u_f], [d_skip, wts["w_glu"]],
        [_sds((s, 2 * SSM_WIDTH), BF16), _sds((s, SSM_WIDTH), BF16), _sds((s, SSM_WIDTH))], [_sds((1, SSM_WIDTH))],
        tm=512, name="glu_gelu_bwd", after=tuple(ffn_sent))
    g_w_glu = _mm_cols_tn(gel, d_glu, ns=2 * SSM_WIDTH // N_CHIPS, name="g_w_glu")
    d_u, d_abar_r, d_abar_i, d_c_r, d_c_ni, d_bin_r, d_bin_i = _ssm_backward(
        d_y, d_u_skip, u_p, h_r, h_i, e_r, e_i, b_in_r, b_in_i, c_out_r, c_out_ni, a_r, a_i)
    d_u = _from_scan_rows(d_u)
    d_bbar_r = d_bin_r.reshape(SSM_GROUPS, SSM_GROUP, SSM_STATE).transpose(0, 2, 1)
    d_bbar_i = d_bin_i.reshape(SSM_GROUPS, SSM_GROUP, SSM_STATE).transpose(0, 2, 1)
    d_a_re, d_a_im, d_log_dt, d_b_re, d_b_im = ssm_vjp(
        (d_abar_r.reshape(SSM_GROUPS, SSM_STATE), d_abar_i.reshape(SSM_GROUPS, SSM_STATE), d_bbar_r, d_bbar_i))
    d_c_re = d_c_r.reshape(SSM_GROUPS, SSM_GROUP, SSM_STATE)
    d_c_im = -d_c_ni.reshape(SSM_GROUPS, SSM_GROUP, SSM_STATE)

    g_w_attn_br = _mm_cols_tn(attn, d_y_attn, ns=D_MODEL // N_CHIPS, name="g_w_attn_br")
    mixer_grads({"w_out": g_w_out, "w_ssm_br": g_w_ssm_br, "w_glu": g_w_glu, "w_attn_br": g_w_attn_br}, d_abar_r)
    small_g = {"b_gate": jnp.concatenate([d_bg0, d_bg1], axis=0), "ssm_a_re": d_a_re, "ssm_a_im": d_a_im,
               "ssm_log_dt": d_log_dt, "ssm_b_re": d_b_re, "ssm_b_im": d_b_im, "ssm_c_re": d_c_re, "ssm_c_im": d_c_im,
               "ssm_d": d_ssm_d.reshape(SSM_WIDTH), "ln1_g": d_ln1_g, "ln1_b": d_ln1_b, "ln2_g": d_ln2_g,
               "ln2_b": d_ln2_b}
    shared = small_grads(small_g, loss_v[0, 0])
    d_proj = _attention_bwd(proj, cos_f, sin_s, d_attn, attn, lse, d_u, d_gl)

    g_w_in = _mm_cols_tn(x, d_proj, ns=IN_WIDTH // N_CHIPS, name="g_w_in", after=tuple(shared))

    def grad_x_after(after):
        dx = _grad_layer_input(chip, d_proj, w_others, w_own, dz1, tm=FF_ROWS, name="dx_proj", after=after)
        return _reorder_rows(dx, to_phase=False, name="grad_x")

    return grad_x_after, g_w_in, d_proj


GATHER_ID, SWAP_ID, SCATTER_ID, JOIN_ID, EXCHANGE_ID = 1, 2, 3, 4, 5


def _place():
    return lax.axis_index("x"), lax.axis_index("y"), lax.axis_index("c")


def _other_chips(x, y):
    return [(1 - x, y), (x, 1 - y), (1 - x, 1 - y)]


def _handshake(peers):
    barrier = pltpu.get_barrier_semaphore()
    for peer in peers:
        pl.semaphore_signal(barrier, inc=1, device_id=peer, device_id_type=MESH)
    pl.semaphore_wait(barrier, len(peers))


def _sequencer(body, arrays, out_type, sems, collective_id, name):
    return pl.kernel(body, name=name, out_type=out_type,
                     mesh=plsc.ScalarSubcoreMesh(axis_name="sequencer", num_cores=1), scratch_types=sems,
                     compiler_params=pltpu.CompilerParams(collective_id=collective_id))(*arrays)


def _gather_weights(shards, *, name, own_slot=True):
    nw = len(shards)

    def body(*refs):
        ins, outs = refs[:nw], refs[nw:2 * nw]
        send_sems, recv_sems, relay_send, relay_recv, pass_send, pass_recv, local_sems = refs[2 * nw:]
        x, y, c = _place()
        chip = 2 * x + y
        sibling = (x, y, 1 - c)
        nbr = [(1 - x, y, c), (x, 1 - y, c)]
        s_nbr = [2 * (1 - x) + y, 2 * x + (1 - y)]
        s_diag = 2 * (1 - x) + (1 - y)
        _handshake([sibling] + nbr)

        def copy(src, dst, send_sem, recv_sem, to):
            return pltpu.make_async_remote_copy(src_ref=src, dst_ref=dst, send_sem=send_sem, recv_sem=recv_sem,
                                                device_id=to, device_id_type=MESH)

        def rows(w, core, quarter=None):
            hw = shards[w].shape[0] // 2
            if quarter is None:
                return pl.ds(core * hw, hw)
            return pl.ds(core * hw + quarter * (hw // 2), hw // 2)

        own, sent = [], []
        for w in range(nw):
            if own_slot:
                cp = pltpu.make_async_copy(ins[w], outs[w].at[chip], local_sems.at[w])
                cp.start()
                own.append(cp)
            for d in range(2):
                cp = copy(ins[w].at[rows(w, c)], outs[w].at[chip, rows(w, c)], send_sems.at[w, d], recv_sems.at[w, d],
                          nbr[d])
                cp.start()
                sent.append(cp)
        for w in range(nw):
            for d in range(2):
                landed = outs[w].at[s_nbr[d], rows(w, c)]
                copy(landed, landed, send_sems.at[w, d], recv_sems.at[w, d], nbr[d]).wait_recv()
                quarter = outs[w].at[s_nbr[d], rows(w, c, d)]
                for cp in (copy(quarter, quarter, relay_send.at[w, d], relay_recv.at[w, d], nbr[1 - d]),
                           copy(landed, landed, pass_send.at[w, d], pass_recv.at[w, d], sibling)):
                    cp.start()
                    sent.append(cp)
        for w in range(nw):
            for d in range(2):
                quarter = outs[w].at[s_diag, rows(w, c, d)]
                copy(quarter, quarter, relay_send.at[w, d], relay_recv.at[w, d], nbr[1 - d]).wait_recv()
                cp = copy(quarter, quarter, pass_send.at[w, 2 + d], pass_recv.at[w, 2 + d], sibling)
                cp.start()
                sent.append(cp)
        for w in range(nw):
            for d in range(2):
                theirs = outs[w].at[s_nbr[d], rows(w, 1 - c)]
                copy(theirs, theirs, pass_send.at[w, d], pass_recv.at[w, d], sibling).wait_recv()
                theirs = outs[w].at[s_diag, rows(w, 1 - c, d)]
                copy(theirs, theirs, pass_send.at[w, 2 + d], pass_recv.at[w, 2 + d], sibling).wait_recv()
        for cp in sent:
            cp.wait_send()
        for cp in own:
            cp.wait()

    sem = pltpu.SemaphoreType.DMA
    return _sequencer(body, shards, [_sds((N_CHIPS,) + a.shape, a.dtype) for a in shards],
                      [sem((nw, 2)), sem((nw, 2)), sem((nw, 2)), sem((nw, 2)), sem((nw, 4)), sem((nw, 4)), sem((nw,))],
                      GATHER_ID, name)


def _swap_other_halves(grads, *, name):
    nw = len(grads)

    def body(*refs):
        ins, outs = refs[:nw], refs[nw:2 * nw]
        send_sems, recv_sems = refs[2 * nw:]
        x, y, c = _place()
        _handshake([(x, y, 1 - c)])
        cps = []
        for w in range(nw):
            hw = grads[w].shape[1] // 2
            cp = pltpu.make_async_remote_copy(
                src_ref=ins[w].at[:, pl.ds((1 - c) * hw, hw)], dst_ref=outs[w], send_sem=send_sems.at[w],
                recv_sem=recv_sems.at[w], device_id=(x, y, 1 - c), device_id_type=MESH)
            cp.start()
            cps.append(cp)
        for cp in cps:
            cp.wait()

    sem = pltpu.SemaphoreType.DMA
    return _sequencer(body, grads, [_sds((N_CHIPS, g.shape[1] // 2, g.shape[2]), g.dtype) for g in grads],
                      [sem((nw,)), sem((nw,))], SWAP_ID, name)


def _add_my_halves(core, grads, others, *, name, after=()):
    nw = len(grads)
    halves = [g.shape[1] // 2 for g in grads]

    def body(core_ref, *refs):
        outs = refs[2 * nw + len(after):]
        for g_ref, o_ref, out_ref in zip(refs[:nw], refs[nw:2 * nw], outs):
            out_ref[...] = (g_ref[...].astype(F32) + o_ref[...].astype(F32)).astype(out_ref.dtype)

    in_specs = [pl.BlockSpec((None, None, hw, g.shape[2]), lambda s, core_ref: (s, core_ref[0], 0, 0))
                for g, hw in zip(grads, halves)]
    in_specs += [pl.BlockSpec((None, hw, g.shape[2]), lambda s, core_ref: (s, 0, 0)) for g, hw in zip(grads, halves)]
    return pl.pallas_call(
        body,
        grid_spec=pltpu.PrefetchScalarGridSpec(
            num_scalar_prefetch=1, grid=(N_CHIPS,), in_specs=in_specs + [HBM_OPERAND] * len(after),
            out_specs=[pl.BlockSpec((None, hw, g.shape[2]), lambda s, core_ref: (s, 0, 0))
                       for g, hw in zip(grads, halves)]),
        out_shape=[_sds((N_CHIPS, hw, g.shape[2]), BF16) for g, hw in zip(grads, halves)],
        compiler_params=_cp(("parallel",)), name=name)(
            core, *[g.reshape(N_CHIPS, 2, hw, g.shape[2]) for g, hw in zip(grads, halves)], *others, *after)


def _scatter_partials(parts, *, name):
    nw = len(parts)

    def body(*refs):
        ins, outs = refs[:nw], refs[nw:2 * nw]
        send_sems, recv_sems = refs[2 * nw:]
        x, y, c = _place()
        _handshake([(cx, cy, c) for cx, cy in _other_chips(x, y)])
        cps = []
        for w in range(nw):
            for j, (cx, cy) in enumerate(_other_chips(x, y)):
                cp = pltpu.make_async_remote_copy(
                    src_ref=ins[w].at[2 * cx + cy], dst_ref=outs[w].at[j], send_sem=send_sems.at[w, j],
                    recv_sem=recv_sems.at[w, j], device_id=(cx, cy, c), device_id_type=MESH)
                cp.start()
                cps.append(cp)
        for cp in cps:
            cp.wait()

    sem = pltpu.SemaphoreType.DMA
    return _sequencer(body, parts, [_sds((3,) + p.shape[1:], p.dtype) for p in parts],
                      [sem((nw, 3)), sem((nw, 3))], SCATTER_ID, name)


SUM_STEPS = 2


def _sum_partials(chip, parts, recvd, *, name, after=()):
    nw = len(parts)
    rows = [p.shape[1] // SUM_STEPS for p in parts]

    def body(chip_ref, *refs):
        outs = refs[2 * nw + len(after):]
        for p_ref, r_ref, out_ref in zip(refs[:nw], refs[nw:2 * nw], outs):
            acc = p_ref[...].astype(F32)
            for j in range(3):
                acc = acc + r_ref[j].astype(F32)
            out_ref[...] = acc

    in_specs = [pl.BlockSpec((None, th, p.shape[2]), lambda i, chip_ref: (chip_ref[0], i, 0))
                for p, th in zip(parts, rows)]
    in_specs += [pl.BlockSpec((3, th, p.shape[2]), lambda i, chip_ref: (0, i, 0)) for p, th in zip(parts, rows)]
    return pl.pallas_call(
        body,
        grid_spec=pltpu.PrefetchScalarGridSpec(
            num_scalar_prefetch=1, grid=(SUM_STEPS,), in_specs=in_specs + [HBM_OPERAND] * len(after),
            out_specs=[pl.BlockSpec((th, p.shape[2]), lambda i, chip_ref: (i, 0)) for p, th in zip(parts, rows)]),
        out_shape=[_sds(p.shape[1:]) for p in parts], compiler_params=_cp(("parallel",)), name=name)(
            chip, *parts, *recvd, *after)


def _swap_reduced_halves(halves, *, name):
    nw = len(halves)

    def body(*refs):
        ins, outs = refs[:nw], refs[nw:2 * nw]
        send_sems, recv_sems = refs[2 * nw:]
        x, y, c = _place()
        _handshake([(x, y, 1 - c)])
        cps = []
        for w in range(nw):
            cp = pltpu.make_async_remote_copy(
                src_ref=ins[w], dst_ref=outs[w], send_sem=send_sems.at[w], recv_sem=recv_sems.at[w],
                device_id=(x, y, 1 - c), device_id_type=MESH)
            cp.start()
            cps.append(cp)
        for cp in cps:
            cp.wait()

    sem = pltpu.SemaphoreType.DMA
    return _sequencer(body, halves, [_sds(h.shape, h.dtype) for h in halves], [sem((nw,)), sem((nw,))], JOIN_ID, name)


def _exchange_rows(vec, *, name):
    def body(v_ref, slots, send_sems, recv_sems, local_sem):
        x, y, c = _place()
        me = 4 * x + 2 * y + c
        peers = []
        for mask in range(1, N_DEV):
            peers.append((1 - x if mask & 4 else x, 1 - y if mask & 2 else y, 1 - c if mask & 1 else c))
        _handshake(peers)
        own = pltpu.make_async_copy(v_ref, slots.at[me], local_sem)
        own.start()
        cps = []
        for k, peer in enumerate(peers):
            cp = pltpu.make_async_remote_copy(
                src_ref=v_ref, dst_ref=slots.at[me], send_sem=send_sems.at[k], recv_sem=recv_sems.at[k],
                device_id=peer, device_id_type=MESH)
            cp.start()
            cps.append(cp)
        for k, (px, py, pc) in enumerate(peers):
            pltpu.make_async_remote_copy(
                src_ref=v_ref, dst_ref=slots.at[4 * px + 2 * py + pc], send_sem=send_sems.at[k],
                recv_sem=recv_sems.at[k], device_id=(px, py, pc), device_id_type=MESH).wait_recv()
        for cp in cps:
            cp.wait_send()
        own.wait()

    sem = pltpu.SemaphoreType.DMA
    return _sequencer(body, [vec], [_sds((N_DEV,) + vec.shape)], [sem((N_DEV - 1,)), sem((N_DEV - 1,)), sem(())],
                      EXCHANGE_ID, name)[0]


def _sum_slots(slots, *, name, after=()):
    def body(s_ref, *rest):
        out_ref = rest[len(after)]
        acc = s_ref[0]
        for d in range(1, N_DEV):
            acc = acc + s_ref[d]
        out_ref[...] = acc

    vmem = pl.BlockSpec(memory_space=pltpu.VMEM)
    return pl.pallas_call(
        body, in_specs=[vmem] + [HBM_OPERAND] * len(after), out_specs=vmem, out_shape=_sds(slots.shape[1:]),
        compiler_params=pltpu.CompilerParams(vmem_limit_bytes=VMEM_LIMIT_BYTES), name=name)(slots, *after)


def _reduce_scatter_start(grads, core, *, tag, add_after=()):
    others = _swap_other_halves(grads, name="swap_other_halves_" + tag)
    parts = _add_my_halves(core, grads, others, name="add_my_halves_" + tag, after=add_after)
    return parts, _scatter_partials(parts, name="scatter_partials_" + tag)


def _reduce_scatter_finish(parts, recvd, chip, *, tag, sum_after=()):
    mine = _sum_partials(chip, parts, recvd, name="sum_partials_" + tag, after=sum_after)
    return mine, _swap_reduced_halves(mine, name="swap_reduced_halves_" + tag)


ADAM_BLOCK_ELEMS = 256 * 1024


def _adam_rows(rows, cols):
    tm = rows
    while tm * cols > ADAM_BLOCK_ELEMS and tm % 16 == 0:
        tm //= 2
    return tm


def _adam_step(wv, gv, mv, vv):
    m2 = ADAM_B1 * mv + (1.0 - ADAM_B1) * gv
    v2 = ADAM_B2 * vv + (1.0 - ADAM_B2) * (gv * gv)
    m_hat = m2 / (1.0 - ADAM_B1 ** ADAM_STEP)
    v_hat = v2 / (1.0 - ADAM_B2 ** ADAM_STEP)
    return -ADAM_LR * (m_hat / (jnp.sqrt(v_hat) + ADAM_EPS) + ADAM_WD * wv), m2, v2


def _adamw_each(ws, gs, ms, vs, *, name, after=()):
    n = len(ws)
    whole = pl.BlockSpec(memory_space=pltpu.VMEM)

    def body(*refs):
        ins, outs = refs[:4 * n], refs[4 * n + len(after):]
        for i in range(n):
            res = _adam_step(*(ins[k * n + i][...] for k in range(4)))
            for k in range(3):
                outs[k * n + i][...] = res[k]

    out = pl.pallas_call(body, in_specs=[whole] * (4 * n) + [HBM_OPERAND] * len(after),
                         out_shape=[_sds(w.shape) for w in ws] * 3, name=name)(*ws, *gs, *ms, *vs, *after)
    return out[:n], out[n:2 * n], out[2 * n:]


def _adamw_halves(core, w, g_mine, g_theirs, m, v, *, name, after=()):
    rows, cols = w.shape
    hw = rows // 2
    tm = _adam_rows(hw, cols)
    per_half = hw // tm

    def body(core_ref, w_ref, gm_ref, gt_ref, m_ref, v_ref, *rest):
        g_out, d_out, m_out, v_out = rest[len(after):]
        mine = (pl.program_id(0) // per_half) == core_ref[0]
        g = jnp.where(mine, gm_ref[...], gt_ref[...])
        d, m2, v2 = _adam_step(w_ref[...], g, m_ref[...], v_ref[...])
        g_out[...] = g
        d_out[...] = d
        m_out[...] = m2
        v_out[...] = v2

    full = pl.BlockSpec((tm, cols), lambda i, core_ref: (i, 0))

    def half(wanted):
        def index(i, core_ref):
            in_use = ((i // per_half) == core_ref[0]) == wanted
            return (jnp.where(in_use, i % per_half, 0), 0)
        return pl.BlockSpec((tm, cols), index)

    return pl.pallas_call(
        body,
        grid_spec=pltpu.PrefetchScalarGridSpec(
            num_scalar_prefetch=1, grid=(rows // tm,),
            in_specs=[full, half(True), half(False), full, full] + [HBM_OPERAND] * len(after),
            out_specs=[full, full, full, full]),
        out_shape=[_sds((rows, cols))] * 4, compiler_params=_cp(("parallel",)), name=name)(
            core, w, g_mine, g_theirs, m, v, *after)


HELD_TRANSPOSED = ("w_ff_gate", "w_ff_up")


def _as_rows(name, arr):
    return arr[0].T if name in HELD_TRANSPOSED else arr[0]


def _from_rows(name, arr2d):
    return (arr2d.T if name in HELD_TRANSPOSED else arr2d)[None]


STORED_SWAPPED = ("ssm_b_re", "ssm_b_im")


def _as_stored(name, arr):
    return jnp.swapaxes(arr, -1, -2) if name in STORED_SWAPPED else arr


def _pack_rows(arrs):
    flat = jnp.concatenate([a.reshape(-1).astype(F32) for a in arrs])
    rows = -(-flat.shape[0] // 1024) * 8
    return jnp.pad(flat, (0, rows * 128 - flat.shape[0])).reshape(rows, 128)


def _unpack_rows(vec, shapes):
    flat = vec.reshape(-1)
    out, off = [], 0
    for shp in shapes:
        size = math.prod(shp)
        out.append(flat[off:off + size].reshape(shp))
        off += size
    return out


SMALL = ("b_gate", "ssm_a_re", "ssm_a_im", "ssm_log_dt", "ssm_b_re", "ssm_b_im", "ssm_c_re", "ssm_c_im", "ssm_d",
         "ln1_g", "ln1_b", "ln2_g", "ln2_b")
GATHER_GROUPS = (("w_in", ("w_in",)), ("mixer", ("w_attn_br", "w_ssm_br", "w_glu", "w_out")),
                 ("ffn_up", ("w_ff_gate", "w_ff_up")), ("ffn_down", ("w_ff_down",)))
REDUCE_GROUPS = (("ffn", ("w_ff_down", "w_ff_gate", "w_ff_up")),
                 ("mixer", ("w_out", "w_ssm_br", "w_glu", "w_attn_br")), ("w_in", ("w_in",)))
WEIGHTS = ("w_in", "b_gate", "w_attn_br", "w_ssm_br", "w_out", "ssm_a_re", "ssm_a_im", "ssm_log_dt", "ssm_b_re",
           "ssm_b_im", "ssm_c_re", "ssm_c_im", "ssm_d", "w_glu", "ln1_g", "ln1_b", "w_ff_gate", "w_ff_up", "w_ff_down",
           "ln2_g", "ln2_b")


def kernel(x, w_in, b_gate, w_attn_br, w_ssm_br, w_out, ssm_a_re, ssm_a_im, ssm_log_dt, ssm_b_re, ssm_b_im, ssm_c_re, ssm_c_im, ssm_d, w_glu, ln1_g, ln1_b, w_ff_gate, w_ff_up, w_ff_down, ln2_g, ln2_b, loss_target, m_w_in, m_b_gate, m_w_attn_br, m_w_ssm_br, m_w_out, m_ssm_a_re, m_ssm_a_im, m_ssm_log_dt, m_ssm_b_re, m_ssm_b_im, m_ssm_c_re, m_ssm_c_im, m_ssm_d, m_w_glu, m_ln1_g, m_ln1_b, m_w_ff_gate, m_w_ff_up, m_w_ff_down, m_ln2_g, m_ln2_b, v_w_in, v_b_gate, v_w_attn_br, v_w_ssm_br, v_w_out, v_ssm_a_re, v_ssm_a_im, v_ssm_log_dt, v_ssm_b_re, v_ssm_b_im, v_ssm_c_re, v_ssm_c_im, v_ssm_d, v_w_glu, v_ln1_g, v_ln1_b, v_w_ff_gate, v_w_ff_up, v_w_ff_down, v_ln2_g, v_ln2_b):
    given = dict(locals())
    px, py, pc = _place()
    chip = 2 * px + py
    core_s = jnp.reshape(pc, (1,)).astype(jnp.int32)
    chip_s = jnp.reshape(chip, (1,)).astype(jnp.int32)

    wts = {}
    for tag, names in GATHER_GROUPS:
        shards = [_as_rows(n, given[n]).astype(BF16) for n in names]
        first = tag == GATHER_GROUPS[0][0]
        slots = _gather_weights(shards, name="gather_" + tag, own_slot=not first)
        if first:
            wts["w_in_parts"] = (chip_s, shards[0], slots[0])
        else:
            wts.update(zip(names, slots))
    ncol = D_MODEL // N_CHIPS
    bg_mine = jnp.where(pc == 0, b_gate[0], jnp.zeros_like(b_gate[0]))
    bg_full = lax.dynamic_update_slice(jnp.zeros((2, D_MODEL), F32), bg_mine, (0, chip * ncol))
    bg_slots = _exchange_rows(bg_full.reshape(16, 128), name="exchange_gate_bias")
    bg_full = _sum_slots(bg_slots, name="sum_gate_bias").reshape(2, D_MODEL)
    small = {n: given[n][0] for n in SMALL if n.startswith("ssm")}
    small.update({n: given[n] for n in ("ln1_g", "ln1_b", "ln2_g", "ln2_b")})
    small["b_gate"] = bg_full

    groups = dict(REDUCE_GROUPS)
    parts, recvd, reduced, sent = {}, {}, {}, {}
    grads, delta, new_m, new_v, done = {}, {}, {}, {}, {}

    def start(tag, big_g, add_after):
        parts[tag], recvd[tag] = _reduce_scatter_start([big_g[n] for n in groups[tag]], core_s, tag=tag,
                                                       add_after=add_after)
        return parts[tag]

    def reduce_sum(tag, after):
        reduced[tag] = _reduce_scatter_finish(parts[tag], recvd[tag], chip_s, tag=tag, sum_after=after)
        return reduced[tag][0]

    def adam(tag, after):
        for n, g_mine, g_theirs in zip(groups[tag], *reduced[tag]):
            res = _adamw_halves(core_s, _as_rows(n, given[n]), g_mine, g_theirs, _as_rows(n, given["m_" + n]),
                                _as_rows(n, given["v_" + n]), name="adamw_" + n, after=after)
            done[n] = res[1]
            grads[n], delta[n], new_m[n], new_v[n] = [_from_rows(n, r) for r in res]

    def ffn_grads(big_g, norm_bwd):
        return start("ffn", big_g, (norm_bwd,))

    def mixer_grads(big_g, scan_bwd):
        return start("mixer", big_g, (scan_bwd, *reduce_sum("ffn", (scan_bwd,))))

    def small_grads(small_g, loss_mine):
        sent["stored"] = [_as_stored(n, small_g[n]) for n in SMALL] + [loss_mine.reshape(1)]
        packed = _pack_rows(sent["stored"])
        sent["slots"] = _exchange_rows(packed, name="exchange_small")
        return (packed,)

    grad_x_after, g_w_in, attention_bwd = _local_step(x[0], loss_target[0], wts, small,
                                                      ffn_grads, mixer_grads, small_grads)

    reduce_sum("mixer", (attention_bwd,))
    summed = _sum_slots(sent["slots"], name="sum_small", after=(g_w_in,))
    adam("mixer", (g_w_in,))
    start("w_in", {"w_in": g_w_in}, (summed, *[done[n] for n in groups["mixer"]]))
    in_flight = (parts["w_in"][0],)
    grad_x = grad_x_after(in_flight)
    adam("ffn", in_flight)
    summed = _unpack_rows(summed, [a.shape for a in sent["stored"]])
    loss = summed.pop()[0]
    at = SMALL.index("b_gate")
    summed[at] = lax.dynamic_slice(summed[at], (0, chip * ncol), (2, ncol))
    summed = [g.reshape(1, -1) if g.ndim == 1 else g for g in summed]
    held = [[_as_stored(n, given[prefix + n]).reshape(g.shape) for n, g in zip(SMALL, summed)]
            for prefix in ("", "m_", "v_")]
    small_out = _adamw_each(held[0], summed, held[1], held[2], name="adamw_small", after=in_flight)
    for out, arrs in zip((grads, delta, new_m, new_v), (summed, *small_out)):
        out.update((n, _as_stored(n, a).reshape(given[n].shape)) for n, a in zip(SMALL, arrs))
    reduce_sum("w_in", (*[done[n] for n in groups["ffn"]], small_out[0][0], grad_x))
    adam("w_in", ())

    return (loss, grad_x.reshape(x.shape), *[grads[n] for n in WEIGHTS], *[delta[n] for n in WEIGHTS],
            *[new_m[n] for n in WEIGHTS], *[new_v[n] for n in WEIGHTS])
```
